```python
import jax, jax.numpy as jnp
from jax import lax
import numpy as np

D_MODEL = 1024
BATCH = 8
SEQ = 4096
DEPTH = 2

EPS = 1e-6
ROPE_THETA = 10000.0
Q_BLOCK = 128

LRU_WIDTH = 1024
LRU_BLOCKS = 8
LRU_BLOCK_W = LRU_WIDTH // LRU_BLOCKS
CONV_WIDTH = 4
LRU_C = 8.0

MLA_HEADS = 8
MLA_NOPE = 64
MLA_ROPE = 32
MLA_QK = MLA_NOPE + MLA_ROPE
MLA_V = 64
Q_LORA = 256
KV_LORA = 128
MLA_WIDTH = MLA_HEADS * MLA_V

DIL_GROUPS = ((128, 1), (512, 4), (2048, 16))
DIL_HEADS = 8
DIL_HD = 64
DIL_QKV = len(DIL_GROUPS) * DIL_HEADS * DIL_HD
DIL_WIDTH = DIL_HEADS * DIL_HD

N_BRANCH = 3
SPLITS = (LRU_WIDTH, LRU_WIDTH, Q_LORA, KV_LORA, MLA_ROPE, MLA_WIDTH,
          DIL_QKV, DIL_QKV, DIL_QKV, DIL_WIDTH, N_BRANCH * D_MODEL)
IN_WIDTH = (2 * LRU_WIDTH + Q_LORA + KV_LORA + MLA_ROPE + MLA_WIDTH
            + 3 * DIL_QKV + DIL_WIDTH + N_BRANCH * D_MODEL)

kernel_name = 'hybrid_rglru_mla_dilated_swa'


def rms_norm(x, g):
    xf = x.astype(jnp.float32)
    y = xf * lax.rsqrt(jnp.mean(xf * xf, axis=-1, keepdims=True) + EPS)
    return (y * g.astype(jnp.float32)).astype(x.dtype)


def rotary(x, pos):
    d = x.shape[-1]
    inv = ROPE_THETA ** (-jnp.arange(0, d, 2, dtype=jnp.float32) / d)
    ang = pos.astype(jnp.float32)[..., None] * inv
    cos = jnp.cos(ang)[:, :, None, :]
    sin = jnp.sin(ang)[:, :, None, :]
    xf = x.astype(jnp.float32)
    x1, x2 = xf[..., : d // 2], xf[..., d // 2:]
    return jnp.concatenate([x1 * cos - x2 * sin, x2 * cos + x1 * sin], axis=-1).astype(x.dtype)


def rg_lru_branch(xb, conv_w, conv_b, w_gx, b_gx, w_ga, b_ga, lam):
    B, S, _ = xb.shape
    xp = jnp.pad(xb, ((0, 0), (CONV_WIDTH - 1, 0), (0, 0)))
    xc = conv_b
    for k in range(CONV_WIDTH):
        xc = xc + xp[:, k:k + S] * conv_w[k]
    xblk = xc.reshape(B, S, LRU_BLOCKS, LRU_BLOCK_W)
    gx = jax.nn.sigmoid(jnp.einsum('bsnc,ncd->bsnd', xblk, w_gx) + b_gx).reshape(B, S, LRU_WIDTH)
    ga = jax.nn.sigmoid(jnp.einsum('bsnc,ncd->bsnd', xblk, w_ga) + b_ga).reshape(B, S, LRU_WIDTH)
    log_a = -LRU_C * ga.astype(jnp.float32) * jax.nn.softplus(-lam.astype(jnp.float32))
    a = jnp.exp(log_a)
    mult = jnp.sqrt(-jnp.expm1(2.0 * log_a))
    b = mult * (gx * xc).astype(jnp.float32)

    def combine(left, right):
        a1, b1 = left
        a2, b2 = right
        return a1 * a2, a2 * b1 + b2

    _, h = lax.associative_scan(combine, (a, b), axis=1)
    return h.astype(xb.dtype)


def causal_attention(q, k, v, scale):
    B, S, H, dk = q.shape
    nb = S // Q_BLOCK
    qb = q.reshape(B, nb, Q_BLOCK, H, dk).transpose(1, 0, 3, 2, 4)
    kt = k.transpose(0, 2, 1, 3)
    vt = v.transpose(0, 2, 1, 3)
    kpos = jnp.arange(S)

    def one_block(args):
        qi, i = args
        s = jnp.einsum('bhqd,bhkd->bhqk', qi, kt).astype(jnp.float32) * scale
        qpos = i * Q_BLOCK + jnp.arange(Q_BLOCK)
        s = jnp.where(kpos[None, :] <= qpos[:, None], s, -jnp.inf)
        p = jax.nn.softmax(s, axis=-1)
        return jnp.einsum('bhqk,bhkd->bhqd', p.astype(vt.dtype), vt)

    o = lax.map(one_block, (qb, jnp.arange(nb)))
    return o.transpose(1, 0, 3, 2, 4).reshape(B, S, H, -1)


def mla_branch(cq, ckv, kr, pos, g_cq, g_ckv, w_uq, w_ukv, g_qn, g_kn):
    B, S, _ = cq.shape
    q = (rms_norm(cq, g_cq) @ w_uq).reshape(B, S, MLA_HEADS, MLA_QK)
    kv = (rms_norm(ckv, g_ckv) @ w_ukv).reshape(B, S, MLA_HEADS, MLA_NOPE + MLA_V)
    k_nope, v = kv[..., :MLA_NOPE], kv[..., MLA_NOPE:]
    k_rope = jnp.broadcast_to(kr[:, :, None, :], (B, S, MLA_HEADS, MLA_ROPE))
    k = jnp.concatenate([k_nope, k_rope], axis=-1)
    q = rms_norm(q, g_qn)
    k = rms_norm(k, g_kn)
    q = jnp.concatenate([q[..., :MLA_NOPE], rotary(q[..., MLA_NOPE:], pos)], axis=-1)
    k = jnp.concatenate([k[..., :MLA_NOPE], rotary(k[..., MLA_NOPE:], pos)], axis=-1)
    o = causal_attention(q, k, v, MLA_QK ** -0.5)
    return o.reshape(B, S, MLA_WIDTH)


def dilated_group(q, k, v, window, dilation):
    B, S, H, d = q.shape
    nk = window // dilation
    span = dilation * nk
    s_pad = -(-S // span) * span
    M = s_pad // dilation
    nb = M // nk

    def to_strided(t):
        t = jnp.pad(t, ((0, 0), (0, s_pad - S), (0, 0), (0, 0))).reshape(B, M, dilation, H, d)
        return t.transpose(0, 2, 3, 1, 4).reshape(B, dilation, H, nb, nk, d)

    qb, kb, vb = to_strided(q), to_strided(k), to_strided(v)

    def prev(t):
        return jnp.pad(t, ((0, 0), (0, 0), (0, 0), (1, 0), (0, 0), (0, 0)))[:, :, :, :-1]

    kw = jnp.concatenate([prev(kb), kb], axis=4)
    vw = jnp.concatenate([prev(vb), vb], axis=4)
    s = jnp.einsum('brhnqd,brhnkd->brhnqk', qb, kw).astype(jnp.float32) * (DIL_HD ** -0.5)
    qi = jnp.arange(nk)[:, None]
    ki = jnp.arange(2 * nk)[None, :]
    band = (ki >= qi) & (ki <= qi + nk)
    not_first = jnp.arange(nb)[:, None, None] > 0
    mask = band[None] & (not_first | (ki >= nk)[None])
    s = jnp.where(mask, s, -jnp.inf)
    m = jnp.max(s, axis=-1, keepdims=True)
    e = jnp.exp(s - m)
    den = jnp.sum(e, axis=-1, keepdims=True)
    o = jnp.einsum('brhnqk,brhnkd->brhnqd', (e / den).astype(vw.dtype), vw)
    lse = (m + jnp.log(den))[..., 0]
    o = o.reshape(B, dilation, H, M, d).transpose(0, 3, 1, 2, 4).reshape(B, s_pad, H, d)[:, :S]
    lse = lse.reshape(B, dilation, H, M).transpose(0, 3, 1, 2).reshape(B, s_pad, H)[:, :S]
    return o, lse


def dilated_branch(q, k, v, pos, g_qn, g_kn):
    B, S, _ = q.shape
    nh = len(DIL_GROUPS) * DIL_HEADS
    q = rotary(rms_norm(q.reshape(B, S, nh, DIL_HD), g_qn), pos)
    k = rotary(rms_norm(k.reshape(B, S, nh, DIL_HD), g_kn), pos)
    v = v.reshape(B, S, nh, DIL_HD)
    outs, lses = [], []
    for gi, (window, dilation) in enumerate(DIL_GROUPS):
        sl = slice(gi * DIL_HEADS, (gi + 1) * DIL_HEADS)
        o, l = dilated_group(q[:, :, sl], k[:, :, sl], v[:, :, sl], window, dilation)
        outs.append(o)
        lses.append(l)
    wts = jax.nn.softmax(jnp.stack(lses, axis=0), axis=0)
    o = jnp.sum(wts[..., None].astype(v.dtype) * jnp.stack(outs, axis=0), axis=0)
    return o.reshape(B, S, DIL_WIDTH)


def hybrid_layer(x, pos, norm_g, w_in, conv_w, conv_b, w_gx, b_gx, w_ga, b_ga, lam, w_lru_o,
                 g_cq, g_ckv, w_uq, w_ukv, g_mqn, g_mkn, w_mla_o, g_dqn, g_dkn, w_dil_o,
                 b_merge, w_out):
    B, S, _ = x.shape
    h = rms_norm(x, norm_g)
    z = h @ w_in
    idx = np.cumsum(SPLITS)[:-1].tolist()
    (lru_x, lru_g, cq, ckv, kr, mla_g, dq, dk, dv, dil_g, merge) = jnp.split(z, idx, axis=-1)
    y_lru = rg_lru_branch(lru_x, conv_w, conv_b, w_gx, b_gx, w_ga, b_ga, lam) * jax.nn.silu(lru_g)
    y_mla = mla_branch(cq, ckv, kr, pos, g_cq, g_ckv, w_uq, w_ukv, g_mqn, g_mkn) * jax.nn.silu(mla_g)
    y_dil = dilated_branch(dq, dk, dv, pos, g_dqn, g_dkn) * jax.nn.silu(dil_g)
    gates = jax.nn.sigmoid(merge + b_merge).reshape(B, S, N_BRANCH, D_MODEL)
    merged = (gates[:, :, 0] * (y_lru @ w_lru_o)
              + gates[:, :, 1] * (y_mla @ w_mla_o)
              + gates[:, :, 2] * (y_dil @ w_dil_o))
    return x + merged @ w_out


def _fwd_setup_inputs(seed: int = 0) -> dict:
    key = jax.random.key(seed)
    ks = jax.random.split(key, 24)

    def nrm(k, shape, scale):
        return jax.random.normal(k, shape, jnp.float32) * scale

    def gain(k, shape):
        return 1.0 + 0.05 * jax.random.normal(k, shape, jnp.float32)

    x = nrm(ks[0], (BATCH, SEQ, D_MODEL), 1.0)
    offsets = jax.random.randint(ks[1], (BATCH, 1), 0, 1024, dtype=jnp.int32)
    positions = offsets + jnp.arange(SEQ, dtype=jnp.int32)[None, :]
    a0 = jax.random.uniform(ks[8], (DEPTH, LRU_WIDTH), jnp.float32, 0.9, 0.999)
    return {
        'x': x,
        'positions': positions,
        'norm_g': gain(ks[2], (DEPTH, D_MODEL)),
        'w_in': nrm(ks[3], (DEPTH, D_MODEL, IN_WIDTH), D_MODEL ** -0.5),
        'conv_w': nrm(ks[4], (DEPTH, CONV_WIDTH, LRU_WIDTH), CONV_WIDTH ** -0.5),
        'conv_b': nrm(ks[5], (DEPTH, LRU_WIDTH), 0.02),
        'w_gate_x': nrm(ks[6], (DEPTH, LRU_BLOCKS, LRU_BLOCK_W, LRU_BLOCK_W), LRU_BLOCK_W ** -0.5),
        'b_gate_x': nrm(ks[7], (DEPTH, LRU_BLOCKS, LRU_BLOCK_W), 0.1),
        'w_gate_a': nrm(ks[9], (DEPTH, LRU_BLOCKS, LRU_BLOCK_W, LRU_BLOCK_W), LRU_BLOCK_W ** -0.5),
        'b_gate_a': nrm(ks[10], (DEPTH, LRU_BLOCKS, LRU_BLOCK_W), 0.1),
        'lru_lambda': jnp.log(a0) - jnp.log1p(-a0),
        'w_lru_o': nrm(ks[11], (DEPTH, LRU_WIDTH, D_MODEL), LRU_WIDTH ** -0.5),
        'cq_norm_g': gain(ks[12], (DEPTH, Q_LORA)),
        'ckv_norm_g': gain(ks[13], (DEPTH, KV_LORA)),
        'w_uq': nrm(ks[14], (DEPTH, Q_LORA, MLA_HEADS * MLA_QK), Q_LORA ** -0.5),
        'w_ukv': nrm(ks[15], (DEPTH, KV_LORA, MLA_HEADS * (MLA_NOPE + MLA_V)), KV_LORA ** -0.5),
        'mla_q_norm_g': gain(ks[16], (DEPTH, MLA_QK)),
        'mla_k_norm_g': gain(ks[17], (DEPTH, MLA_QK)),
        'w_mla_o': nrm(ks[18], (DEPTH, MLA_WIDTH, D_MODEL), MLA_WIDTH ** -0.5),
        'dil_q_norm_g': gain(ks[19], (DEPTH, DIL_HD)),
        'dil_k_norm_g': gain(ks[20], (DEPTH, DIL_HD)),
        'w_dil_o': nrm(ks[21], (DEPTH, DIL_WIDTH, D_MODEL), DIL_WIDTH ** -0.5),
        'b_merge': nrm(ks[22], (DEPTH, N_BRANCH * D_MODEL), 0.1),
        'w_out': nrm(ks[23], (DEPTH, D_MODEL, D_MODEL), D_MODEL ** -0.5),
    }


def _fwd_reference(x, positions, norm_g, w_in, conv_w, conv_b, w_gate_x, b_gate_x, w_gate_a, b_gate_a,
              lru_lambda, w_lru_o, cq_norm_g, ckv_norm_g, w_uq, w_ukv, mla_q_norm_g, mla_k_norm_g,
              w_mla_o, dil_q_norm_g, dil_k_norm_g, w_dil_o, b_merge, w_out):
    for l in range(DEPTH):
        x = hybrid_layer(x, positions, norm_g[l], w_in[l], conv_w[l], conv_b[l],
                         w_gate_x[l], b_gate_x[l], w_gate_a[l], b_gate_a[l], lru_lambda[l],
                         w_lru_o[l], cq_norm_g[l], ckv_norm_g[l], w_uq[l], w_ukv[l],
                         mla_q_norm_g[l], mla_k_norm_g[l], w_mla_o[l], dil_q_norm_g[l],
                         dil_k_norm_g[l], w_dil_o[l], b_merge[l], w_out[l])
    return x


import jax as _jax
import jax.numpy as _jnp

TWIN_FORMAT = 'train_step'
FWD_PARAMS = ['x', 'positions', 'norm_g', 'w_in', 'conv_w', 'conv_b', 'w_gate_x', 'b_gate_x', 'w_gate_a', 'b_gate_a', 'lru_lambda', 'w_lru_o', 'cq_norm_g', 'ckv_norm_g', 'w_uq', 'w_ukv', 'mla_q_norm_g', 'mla_k_norm_g', 'w_mla_o', 'dil_q_norm_g', 'dil_k_norm_g', 'w_dil_o', 'b_merge', 'w_out']
TWIN_WEIGHTS = ['norm_g', 'w_in', 'conv_w', 'conv_b', 'w_gate_x', 'b_gate_x', 'w_gate_a', 'b_gate_a', 'lru_lambda', 'w_lru_o', 'cq_norm_g', 'ckv_norm_g', 'w_uq', 'w_ukv', 'mla_q_norm_g', 'mla_k_norm_g', 'w_mla_o', 'dil_q_norm_g', 'dil_k_norm_g', 'w_dil_o', 'b_merge', 'w_out']
TWIN_DIFF_INPUT = 'x'
TWIN_INPUTS = ['x', 'positions', 'norm_g', 'w_in', 'conv_w', 'conv_b', 'w_gate_x', 'b_gate_x', 'w_gate_a', 'b_gate_a', 'lru_lambda', 'w_lru_o', 'cq_norm_g', 'ckv_norm_g', 'w_uq', 'w_ukv', 'mla_q_norm_g', 'mla_k_norm_g', 'w_mla_o', 'dil_q_norm_g', 'dil_k_norm_g', 'w_dil_o', 'b_merge', 'w_out', 'loss_target', 'm_norm_g', 'm_w_in', 'm_conv_w', 'm_conv_b', 'm_w_gate_x', 'm_b_gate_x', 'm_w_gate_a', 'm_b_gate_a', 'm_lru_lambda', 'm_w_lru_o', 'm_cq_norm_g', 'm_ckv_norm_g', 'm_w_uq', 'm_w_ukv', 'm_mla_q_norm_g', 'm_mla_k_norm_g', 'm_w_mla_o', 'm_dil_q_norm_g', 'm_dil_k_norm_g', 'm_w_dil_o', 'm_b_merge', 'm_w_out', 'v_norm_g', 'v_w_in', 'v_conv_w', 'v_conv_b', 'v_w_gate_x', 'v_b_gate_x', 'v_w_gate_a', 'v_b_gate_a', 'v_lru_lambda', 'v_w_lru_o', 'v_cq_norm_g', 'v_ckv_norm_g', 'v_w_uq', 'v_w_ukv', 'v_mla_q_norm_g', 'v_mla_k_norm_g', 'v_w_mla_o', 'v_dil_q_norm_g', 'v_dil_k_norm_g', 'v_w_dil_o', 'v_b_merge', 'v_w_out']
TWIN_OUTPUTS = ['loss', 'grad_x', 'grad_norm_g', 'grad_w_in', 'grad_conv_w', 'grad_conv_b', 'grad_w_gate_x', 'grad_b_gate_x', 'grad_w_gate_a', 'grad_b_gate_a', 'grad_lru_lambda', 'grad_w_lru_o', 'grad_cq_norm_g', 'grad_ckv_norm_g', 'grad_w_uq', 'grad_w_ukv', 'grad_mla_q_norm_g', 'grad_mla_k_norm_g', 'grad_w_mla_o', 'grad_dil_q_norm_g', 'grad_dil_k_norm_g', 'grad_w_dil_o', 'grad_b_merge', 'grad_w_out', 'delta_norm_g', 'delta_w_in', 'delta_conv_w', 'delta_conv_b', 'delta_w_gate_x', 'delta_b_gate_x', 'delta_w_gate_a', 'delta_b_gate_a', 'delta_lru_lambda', 'delta_w_lru_o', 'delta_cq_norm_g', 'delta_ckv_norm_g', 'delta_w_uq', 'delta_w_ukv', 'delta_mla_q_norm_g', 'delta_mla_k_norm_g', 'delta_w_mla_o', 'delta_dil_q_norm_g', 'delta_dil_k_norm_g', 'delta_w_dil_o', 'delta_b_merge', 'delta_w_out', 'new_m_norm_g', 'new_m_w_in', 'new_m_conv_w', 'new_m_conv_b', 'new_m_w_gate_x', 'new_m_b_gate_x', 'new_m_w_gate_a', 'new_m_b_gate_a', 'new_m_lru_lambda', 'new_m_w_lru_o', 'new_m_cq_norm_g', 'new_m_ckv_norm_g', 'new_m_w_uq', 'new_m_w_ukv', 'new_m_mla_q_norm_g', 'new_m_mla_k_norm_g', 'new_m_w_mla_o', 'new_m_dil_q_norm_g', 'new_m_dil_k_norm_g', 'new_m_w_dil_o', 'new_m_b_merge', 'new_m_w_out', 'new_v_norm_g', 'new_v_w_in', 'new_v_conv_w', 'new_v_conv_b', 'new_v_w_gate_x', 'new_v_b_gate_x', 'new_v_w_gate_a', 'new_v_b_gate_a', 'new_v_lru_lambda', 'new_v_w_lru_o', 'new_v_cq_norm_g', 'new_v_ckv_norm_g', 'new_v_w_uq', 'new_v_w_ukv', 'new_v_mla_q_norm_g', 'new_v_mla_k_norm_g', 'new_v_w_mla_o', 'new_v_dil_q_norm_g', 'new_v_dil_k_norm_g', 'new_v_w_dil_o', 'new_v_b_merge', 'new_v_w_out']
TWIN_LEAF_KINDS = {'loss': 'loss', 'grad_x': 'grad_x', 'grad_norm_g': 'grad_w', 'grad_w_in': 'grad_w', 'grad_conv_w': 'grad_w', 'grad_conv_b': 'grad_w', 'grad_w_gate_x': 'grad_w', 'grad_b_gate_x': 'grad_w', 'grad_w_gate_a': 'grad_w', 'grad_b_gate_a': 'grad_w', 'grad_lru_lambda': 'grad_w', 'grad_w_lru_o': 'grad_w', 'grad_cq_norm_g': 'grad_w', 'grad_ckv_norm_g': 'grad_w', 'grad_w_uq': 'grad_w', 'grad_w_ukv': 'grad_w', 'grad_mla_q_norm_g': 'grad_w', 'grad_mla_k_norm_g': 'grad_w', 'grad_w_mla_o': 'grad_w', 'grad_dil_q_norm_g': 'grad_w', 'grad_dil_k_norm_g': 'grad_w', 'grad_w_dil_o': 'grad_w', 'grad_b_merge': 'grad_w', 'grad_w_out': 'grad_w', 'delta_norm_g': 'delta_w', 'delta_w_in': 'delta_w', 'delta_conv_w': 'delta_w', 'delta_conv_b': 'delta_w', 'delta_w_gate_x': 'delta_w', 'delta_b_gate_x': 'delta_w', 'delta_w_gate_a': 'delta_w', 'delta_b_gate_a': 'delta_w', 'delta_lru_lambda': 'delta_w', 'delta_w_lru_o': 'delta_w', 'delta_cq_norm_g': 'delta_w', 'delta_ckv_norm_g': 'delta_w', 'delta_w_uq': 'delta_w', 'delta_w_ukv': 'delta_w', 'delta_mla_q_norm_g': 'delta_w', 'delta_mla_k_norm_g': 'delta_w', 'delta_w_mla_o': 'delta_w', 'delta_dil_q_norm_g': 'delta_w', 'delta_dil_k_norm_g': 'delta_w', 'delta_w_dil_o': 'delta_w', 'delta_b_merge': 'delta_w', 'delta_w_out': 'delta_w', 'new_m_norm_g': 'new_m', 'new_m_w_in': 'new_m', 'new_m_conv_w': 'new_m', 'new_m_conv_b': 'new_m', 'new_m_w_gate_x': 'new_m', 'new_m_b_gate_x': 'new_m', 'new_m_w_gate_a': 'new_m', 'new_m_b_gate_a': 'new_m', 'new_m_lru_lambda': 'new_m', 'new_m_w_lru_o': 'new_m', 'new_m_cq_norm_g': 'new_m', 'new_m_ckv_norm_g': 'new_m', 'new_m_w_uq': 'new_m', 'new_m_w_ukv': 'new_m', 'new_m_mla_q_norm_g': 'new_m', 'new_m_mla_k_norm_g': 'new_m', 'new_m_w_mla_o': 'new_m', 'new_m_dil_q_norm_g': 'new_m', 'new_m_dil_k_norm_g': 'new_m', 'new_m_w_dil_o': 'new_m', 'new_m_b_merge': 'new_m', 'new_m_w_out': 'new_m', 'new_v_norm_g': 'new_v', 'new_v_w_in': 'new_v', 'new_v_conv_w': 'new_v', 'new_v_conv_b': 'new_v', 'new_v_w_gate_x': 'new_v', 'new_v_b_gate_x': 'new_v', 'new_v_w_gate_a': 'new_v', 'new_v_b_gate_a': 'new_v', 'new_v_lru_lambda': 'new_v', 'new_v_w_lru_o': 'new_v', 'new_v_cq_norm_g': 'new_v', 'new_v_ckv_norm_g': 'new_v', 'new_v_w_uq': 'new_v', 'new_v_w_ukv': 'new_v', 'new_v_mla_q_norm_g': 'new_v', 'new_v_mla_k_norm_g': 'new_v', 'new_v_w_mla_o': 'new_v', 'new_v_dil_q_norm_g': 'new_v', 'new_v_dil_k_norm_g': 'new_v', 'new_v_w_dil_o': 'new_v', 'new_v_b_merge': 'new_v', 'new_v_w_out': 'new_v'}


def _forward(args):
    return _fwd_reference(*[args[k] for k in FWD_PARAMS])


def _output_shape():
    out = _jax.eval_shape(lambda: _forward(_fwd_setup_inputs(0)))
    return out.shape, out.dtype

N_MICROBATCH = 1
ADAM_LR = 0.001
ADAM_B1 = 0.9
ADAM_B2 = 0.999
ADAM_EPS = 1e-08
ADAM_WD = 0.01
ADAM_STEP = 10
PER_EXAMPLE_BATCH_AXIS = {'x': 0, 'positions': 0, 'loss_target': 0}
SHARED_INPUTS = []
_WEIGHT_DTYPES = {'norm_g': _jnp.float32, 'w_in': _jnp.float32, 'conv_w': _jnp.float32, 'conv_b': _jnp.float32, 'w_gate_x': _jnp.float32, 'b_gate_x': _jnp.float32, 'w_gate_a': _jnp.float32, 'b_gate_a': _jnp.float32, 'lru_lambda': _jnp.float32, 'w_lru_o': _jnp.float32, 'cq_norm_g': _jnp.float32, 'ckv_norm_g': _jnp.float32, 'w_uq': _jnp.float32, 'w_ukv': _jnp.float32, 'mla_q_norm_g': _jnp.float32, 'mla_k_norm_g': _jnp.float32, 'w_mla_o': _jnp.float32, 'dil_q_norm_g': _jnp.float32, 'dil_k_norm_g': _jnp.float32, 'w_dil_o': _jnp.float32, 'b_merge': _jnp.float32, 'w_out': _jnp.float32}
MOMENT_SCALE = {'norm_g': 2.440179e+00, 'w_in': 3.193535e-02, 'conv_w': 8.294229e-01, 'conv_b': 2.126310e+00, 'w_gate_x': 1.515123e-01, 'b_gate_x': 5.263750e-01, 'w_gate_a': 7.713037e-02, 'b_gate_a': 1.033601e-01, 'lru_lambda': 1.972960e-01, 'w_lru_o': 6.831211e-02, 'cq_norm_g': 3.619170e-02, 'ckv_norm_g': 2.740137e-01, 'w_uq': 2.045230e-02, 'w_ukv': 2.796125e-02, 'mla_q_norm_g': 1.996576e-01, 'mla_k_norm_g': 2.011859e-01, 'w_mla_o': 2.116827e-02, 'dil_q_norm_g': 2.265674e-01, 'dil_k_norm_g': 2.264977e-01, 'w_dil_o': 1.725720e-02, 'b_merge': 2.106510e-01, 'w_out': 5.995729e-02}


def _to_microbatches(a, axis):
    t = _jnp.moveaxis(a, axis, 0)
    t = t.reshape((N_MICROBATCH, t.shape[0] // N_MICROBATCH) + t.shape[1:])
    return _jnp.moveaxis(t, 1, axis + 1)


def setup_inputs(seed: int = 0) -> dict:
    inp = _fwd_setup_inputs(seed)
    key = _jax.random.fold_in(_jax.random.key(seed), 7919)
    shape, _ = _output_shape()
    out = dict(inp)
    out["loss_target"] = _jax.random.normal(_jax.random.fold_in(key, 0), shape, _jnp.float32)
    for i, name in enumerate(TWIN_WEIGHTS):
        w = inp[name].astype(_jnp.float32)
        if MOMENT_SCALE is None:
            s = _jnp.sqrt(_jnp.mean(_jnp.square(w)) + 1e-30)
        else:
            s = MOMENT_SCALE[name]
        km, kv = _jax.random.split(_jax.random.fold_in(key, i + 1))
        out[name] = w
        out["m_" + name] = s * _jax.random.normal(km, w.shape, _jnp.float32)
        out["v_" + name] = (s * s) * _jax.random.uniform(kv, w.shape, _jnp.float32, 0.5, 1.5)
    if N_MICROBATCH > 1:
        for name, axis in PER_EXAMPLE_BATCH_AXIS.items():
            out[name] = _to_microbatches(out[name], axis)
    return {'x': out['x'], 'positions': out['positions'], 'norm_g': out['norm_g'], 'w_in': out['w_in'], 'conv_w': out['conv_w'], 'conv_b': out['conv_b'], 'w_gate_x': out['w_gate_x'], 'b_gate_x': out['b_gate_x'], 'w_gate_a': out['w_gate_a'], 'b_gate_a': out['b_gate_a'], 'lru_lambda': out['lru_lambda'], 'w_lru_o': out['w_lru_o'], 'cq_norm_g': out['cq_norm_g'], 'ckv_norm_g': out['ckv_norm_g'], 'w_uq': out['w_uq'], 'w_ukv': out['w_ukv'], 'mla_q_norm_g': out['mla_q_norm_g'], 'mla_k_norm_g': out['mla_k_norm_g'], 'w_mla_o': out['w_mla_o'], 'dil_q_norm_g': out['dil_q_norm_g'], 'dil_k_norm_g': out['dil_k_norm_g'], 'w_dil_o': out['w_dil_o'], 'b_merge': out['b_merge'], 'w_out': out['w_out'], 'loss_target': out['loss_target'], 'm_norm_g': out['m_norm_g'], 'm_w_in': out['m_w_in'], 'm_conv_w': out['m_conv_w'], 'm_conv_b': out['m_conv_b'], 'm_w_gate_x': out['m_w_gate_x'], 'm_b_gate_x': out['m_b_gate_x'], 'm_w_gate_a': out['m_w_gate_a'], 'm_b_gate_a': out['m_b_gate_a'], 'm_lru_lambda': out['m_lru_lambda'], 'm_w_lru_o': out['m_w_lru_o'], 'm_cq_norm_g': out['m_cq_norm_g'], 'm_ckv_norm_g': out['m_ckv_norm_g'], 'm_w_uq': out['m_w_uq'], 'm_w_ukv': out['m_w_ukv'], 'm_mla_q_norm_g': out['m_mla_q_norm_g'], 'm_mla_k_norm_g': out['m_mla_k_norm_g'], 'm_w_mla_o': out['m_w_mla_o'], 'm_dil_q_norm_g': out['m_dil_q_norm_g'], 'm_dil_k_norm_g': out['m_dil_k_norm_g'], 'm_w_dil_o': out['m_w_dil_o'], 'm_b_merge': out['m_b_merge'], 'm_w_out': out['m_w_out'], 'v_norm_g': out['v_norm_g'], 'v_w_in': out['v_w_in'], 'v_conv_w': out['v_conv_w'], 'v_conv_b': out['v_conv_b'], 'v_w_gate_x': out['v_w_gate_x'], 'v_b_gate_x': out['v_b_gate_x'], 'v_w_gate_a': out['v_w_gate_a'], 'v_b_gate_a': out['v_b_gate_a'], 'v_lru_lambda': out['v_lru_lambda'], 'v_w_lru_o': out['v_w_lru_o'], 'v_cq_norm_g': out['v_cq_norm_g'], 'v_ckv_norm_g': out['v_ckv_norm_g'], 'v_w_uq': out['v_w_uq'], 'v_w_ukv': out['v_w_ukv'], 'v_mla_q_norm_g': out['v_mla_q_norm_g'], 'v_mla_k_norm_g': out['v_mla_k_norm_g'], 'v_w_mla_o': out['v_w_mla_o'], 'v_dil_q_norm_g': out['v_dil_q_norm_g'], 'v_dil_k_norm_g': out['v_dil_k_norm_g'], 'v_w_dil_o': out['v_w_dil_o'], 'v_b_merge': out['v_b_merge'], 'v_w_out': out['v_w_out']}


def _loss(weights, diff, rest, loss_target):
    with _jax.named_scope("forward"):
        args = {**rest, TWIN_DIFF_INPUT: diff, **{k: w.astype(_WEIGHT_DTYPES[k]) for k, w in weights.items()}}
        y = _forward(args)
    with _jax.named_scope("loss_head"):
        err = _jnp.square(y.astype(_jnp.float32) - loss_target)
        return 0.5 * _jnp.sum(_jnp.mean(err, axis=-1)) if err.ndim else 0.5 * err


def _adamw(w, g, m, v):
    m = ADAM_B1 * m + (1.0 - ADAM_B1) * g
    v = ADAM_B2 * v + (1.0 - ADAM_B2) * _jnp.square(g)
    m_hat = m / (1.0 - ADAM_B1 ** ADAM_STEP)
    v_hat = v / (1.0 - ADAM_B2 ** ADAM_STEP)
    delta = -ADAM_LR * (m_hat / (_jnp.sqrt(v_hat) + ADAM_EPS) + ADAM_WD * w)
    return delta, m, v


def reference(x, positions, norm_g, w_in, conv_w, conv_b, w_gate_x, b_gate_x, w_gate_a, b_gate_a, lru_lambda, w_lru_o, cq_norm_g, ckv_norm_g, w_uq, w_ukv, mla_q_norm_g, mla_k_norm_g, w_mla_o, dil_q_norm_g, dil_k_norm_g, w_dil_o, b_merge, w_out, loss_target, m_norm_g, m_w_in, m_conv_w, m_conv_b, m_w_gate_x, m_b_gate_x, m_w_gate_a, m_b_gate_a, m_lru_lambda, m_w_lru_o, m_cq_norm_g, m_ckv_norm_g, m_w_uq, m_w_ukv, m_mla_q_norm_g, m_mla_k_norm_g, m_w_mla_o, m_dil_q_norm_g, m_dil_k_norm_g, m_w_dil_o, m_b_merge, m_w_out, v_norm_g, v_w_in, v_conv_w, v_conv_b, v_w_gate_x, v_b_gate_x, v_w_gate_a, v_b_gate_a, v_lru_lambda, v_w_lru_o, v_cq_norm_g, v_ckv_norm_g, v_w_uq, v_w_ukv, v_mla_q_norm_g, v_mla_k_norm_g, v_w_mla_o, v_dil_q_norm_g, v_dil_k_norm_g, v_w_dil_o, v_b_merge, v_w_out):
    given = dict(x=x, positions=positions, norm_g=norm_g, w_in=w_in, conv_w=conv_w, conv_b=conv_b, w_gate_x=w_gate_x, b_gate_x=b_gate_x, w_gate_a=w_gate_a, b_gate_a=b_gate_a, lru_lambda=lru_lambda, w_lru_o=w_lru_o, cq_norm_g=cq_norm_g, ckv_norm_g=ckv_norm_g, w_uq=w_uq, w_ukv=w_ukv, mla_q_norm_g=mla_q_norm_g, mla_k_norm_g=mla_k_norm_g, w_mla_o=w_mla_o, dil_q_norm_g=dil_q_norm_g, dil_k_norm_g=dil_k_norm_g, w_dil_o=w_dil_o, b_merge=b_merge, w_out=w_out, loss_target=loss_target, m_norm_g=m_norm_g, m_w_in=m_w_in, m_conv_w=m_conv_w, m_conv_b=m_conv_b, m_w_gate_x=m_w_gate_x, m_b_gate_x=m_b_gate_x, m_w_gate_a=m_w_gate_a, m_b_gate_a=m_b_gate_a, m_lru_lambda=m_lru_lambda, m_w_lru_o=m_w_lru_o, m_cq_norm_g=m_cq_norm_g, m_ckv_norm_g=m_ckv_norm_g, m_w_uq=m_w_uq, m_w_ukv=m_w_ukv, m_mla_q_norm_g=m_mla_q_norm_g, m_mla_k_norm_g=m_mla_k_norm_g, m_w_mla_o=m_w_mla_o, m_dil_q_norm_g=m_dil_q_norm_g, m_dil_k_norm_g=m_dil_k_norm_g, m_w_dil_o=m_w_dil_o, m_b_merge=m_b_merge, m_w_out=m_w_out, v_norm_g=v_norm_g, v_w_in=v_w_in, v_conv_w=v_conv_w, v_conv_b=v_conv_b, v_w_gate_x=v_w_gate_x, v_b_gate_x=v_b_gate_x, v_w_gate_a=v_w_gate_a, v_b_gate_a=v_b_gate_a, v_lru_lambda=v_lru_lambda, v_w_lru_o=v_w_lru_o, v_cq_norm_g=v_cq_norm_g, v_ckv_norm_g=v_ckv_norm_g, v_w_uq=v_w_uq, v_w_ukv=v_w_ukv, v_mla_q_norm_g=v_mla_q_norm_g, v_mla_k_norm_g=v_mla_k_norm_g, v_w_mla_o=v_w_mla_o, v_dil_q_norm_g=v_dil_q_norm_g, v_dil_k_norm_g=v_dil_k_norm_g, v_w_dil_o=v_w_dil_o, v_b_merge=v_b_merge, v_w_out=v_w_out)
    weights = {n: given[n] for n in TWIN_WEIGHTS}
    shared = {n: given[n] for n in SHARED_INPUTS}
    per_example = {n: given[n] for n in ['x', 'positions']}
    grad_fn = _jax.value_and_grad(_loss, argnums=(0, 1))

    def one_microbatch(ex, loss_target):
        ex = dict(ex)
        diff = ex.pop(TWIN_DIFF_INPUT)
        return grad_fn(weights, diff, {**shared, **ex}, loss_target)

    if N_MICROBATCH == 1:
        loss, (grad_w, grad_x) = one_microbatch(per_example, given["loss_target"])
    else:
        def body(carry, xs):
            loss_sum, grad_sum = carry
            l_k, (gw_k, gx_k) = one_microbatch(xs[0], xs[1])
            with _jax.named_scope("update"):
                return (loss_sum + l_k, _jax.tree.map(_jnp.add, grad_sum, gw_k)), gx_k

        init = (_jnp.zeros((), _jnp.float32), _jax.tree.map(_jnp.zeros_like, weights))
        (loss, grad_w), grad_x = _jax.lax.scan(body, init, (per_example, given["loss_target"]))
    with _jax.named_scope("update"):
        delta_w, new_m, new_v = {}, {}, {}
        for n in TWIN_WEIGHTS:
            delta_w[n], new_m[n], new_v[n] = _adamw(weights[n], grad_w[n], given["m_" + n], given["v_" + n])
    return (loss, grad_x, *[grad_w[n] for n in TWIN_WEIGHTS], *[delta_w[n] for n in TWIN_WEIGHTS],
            *[new_m[n] for n in TWIN_WEIGHTS], *[new_v[n] for n in TWIN_WEIGHTS])
```

```python
import functools

import numpy as np
import jax
import jax.numpy as jnp
from jax import lax
from jax.experimental import pallas as pl
from jax.experimental.pallas import tpu as pltpu

F32 = jnp.float32
BF16 = jnp.bfloat16

N_DEV = 8
D = 1024
DEPTH = 2
EPS = 1e-6
ROPE_THETA = 10000.0
LRU_C = 8.0
LANE = 128
SUB = 8
IN_WIDTH = 11168
SHARD_IN = IN_WIDTH // N_DEV

C_LRUX, C_LRUG, C_CQ, C_CKV, C_KR, C_MLAG, C_DQ, C_DK, C_DV, C_DILG, C_MERGE = 0, 8, 16, 18, 19, 20, 24, 36, 48, 60, 64
ZW = 88 * LANE
KR_LANE = 64

MLA_QK = 96
MLA_SCALE = MLA_QK ** -0.5
DIL_HD = 64
DIL_SCALE = DIL_HD ** -0.5
DIL_DILATIONS = (1, 4, 16)
NK = 128

ADAM_LR, ADAM_B1, ADAM_B2, ADAM_EPS, ADAM_WD, ADAM_STEP = 0.001, 0.9, 0.999, 1e-08, 0.01, 10

NEG = -1e30
VMEM_LIMIT = 48 * 1024 * 1024


def _cp(**kw):
    return pltpu.CompilerParams(vmem_limit_bytes=VMEM_LIMIT, **kw)


def _sig(x):
    return 1.0 / (1.0 + jnp.exp(-x))


def _silu(x):
    return x * _sig(x)


def _dsilu(x):
    s = _sig(x)
    return s * (1.0 + x * (1.0 - s))


def _dot(a, b, dims):
    return lax.dot_general(a, b, (dims, ((), ())), preferred_element_type=F32)


def _nn(a, b):
    return _dot(a, b, ((1,), (0,)))


def _nt(a, b):
    return _dot(a, b, ((1,), (1,)))


def _tn(a, b):
    return _dot(a, b, ((0,), (0,)))


def _rsum(x):
    return jnp.sum(x, axis=-1, keepdims=True)


def _csum(x):
    return jnp.sum(x, axis=0, keepdims=True)


def _mm(a, b, *, mode, name, out_dtype=F32, add=None, tm=1024, tn=1024, tk=1024):
    if mode == "nn":
        (M, K), (K2, N) = a.shape, b.shape
    elif mode == "nt":
        (M, K), (N, K2) = a.shape, b.shape
    else:
        (K, M), (K2, N) = a.shape, b.shape
    assert K == K2
    tm, tn, tk = min(tm, M), min(tn, N), min(tk, K)
    assert M % tm == 0 and N % tn == 0 and K % tk == 0
    nk = K // tk
    fn = {"nn": _nn, "nt": _nt, "tn": _tn}[mode]
    has_add = add is not None

    def body(*refs):
        a_ref, b_ref = refs[0], refs[1]
        add_ref = refs[2] if has_add else None
        o_ref = refs[3] if has_add else refs[2]
        part = fn(a_ref[...].astype(BF16), b_ref[...].astype(BF16))

        def fin(acc):
            if has_add:
                acc = acc + add_ref[...]
            o_ref[...] = acc.astype(out_dtype)

        if nk == 1:
            fin(part)
        else:
            acc_ref = refs[-1]
            k = pl.program_id(2)

            @pl.when(k == 0)
            def _():
                acc_ref[...] = part

            @pl.when(k > 0)
            def _():
                acc_ref[...] += part

            @pl.when(k == nk - 1)
            def _():
                fin(acc_ref[...])

    a_spec = pl.BlockSpec((tk, tm), lambda i, j, k: (k, i)) if mode == "tn" else pl.BlockSpec((tm, tk), lambda i, j, k: (i, k))
    b_spec = pl.BlockSpec((tn, tk), lambda i, j, k: (j, k)) if mode == "nt" else pl.BlockSpec((tk, tn), lambda i, j, k: (k, j))
    o_spec = pl.BlockSpec((tm, tn), lambda i, j, k: (i, j))
    in_specs, args = [a_spec, b_spec], [a, b]
    if has_add:
        in_specs.append(o_spec)
        args.append(add)
    return pl.pallas_call(
        body, name=name, grid=(M // tm, N // tn, nk), in_specs=in_specs, out_specs=o_spec,
        out_shape=jax.ShapeDtypeStruct((M, N), out_dtype),
        scratch_shapes=[pltpu.VMEM((tm, tn), F32)] if nk > 1 else [],
        compiler_params=_cp(dimension_semantics=("parallel", "parallel", "arbitrary")),
    )(*args)


T_ROW = 512


def _rms_in_fwd(x, g):
    S = x.shape[0]
    T = T_ROW

    def body(x_ref, g_ref, h_ref):
        xv = x_ref[...]
        r = lax.rsqrt(jnp.mean(xv * xv, axis=-1, keepdims=True) + EPS)
        h_ref[...] = (xv * r * g_ref[...]).astype(BF16)

    return pl.pallas_call(
        body, name="rms_in_fwd", grid=(S // T,),
        in_specs=[pl.BlockSpec((T, D), lambda i: (i, 0)), pl.BlockSpec((1, D), lambda i: (0, 0))],
        out_specs=pl.BlockSpec((T, D), lambda i: (i, 0)),
        out_shape=jax.ShapeDtypeStruct((S, D), BF16), compiler_params=_cp(),
    )(x, g)


def _rms_in_bwd(x, g, dh, dres):
    S = x.shape[0]
    T = T_ROW

    def body(x_ref, g_ref, dh_ref, dr_ref, dx_ref, dg_ref):
        i = pl.program_id(0)
        xv = x_ref[...]
        r = lax.rsqrt(jnp.mean(xv * xv, axis=-1, keepdims=True) + EPS)
        xn = xv * r
        dy = dh_ref[...]
        part = _csum(dy * xn)

        @pl.when(i == 0)
        def _():
            dg_ref[...] = part

        @pl.when(i > 0)
        def _():
            dg_ref[...] += part

        dxh = dy * g_ref[...]
        dx_ref[...] = dr_ref[...] + r * (dxh - xn * jnp.mean(dxh * xn, axis=-1, keepdims=True))

    row = pl.BlockSpec((T, D), lambda i: (i, 0))
    vec = pl.BlockSpec((1, D), lambda i: (0, 0))
    return pl.pallas_call(
        body, name="rms_in_bwd", grid=(S // T,), in_specs=[row, vec, row, row], out_specs=[row, vec],
        out_shape=[jax.ShapeDtypeStruct((S, D), F32), jax.ShapeDtypeStruct((1, D), F32)], compiler_params=_cp(),
    )(x, g, dh, dres)


T_LRU = 512


def _neg_expm1(y):
    ser = -y * (1.0 + y * 0.5 * (1.0 + y * (1.0 / 3.0) * (1.0 + y * 0.25 * (1.0 + y * 0.2))))
    return jnp.where(y > -0.03, ser, 1.0 - jnp.exp(y))


def _softplus_neg(lam):
    e = jnp.exp(-jnp.abs(lam))
    l1p = jnp.where(e < 0.01, e * (1.0 - e * (0.5 - e * (1.0 / 3.0 - e * 0.25))), jnp.log(1.0 + e))
    return jnp.maximum(-lam, 0.0) + l1p


def _scan_fwd(a, b, T):
    row = lax.broadcasted_iota(jnp.int32, a.shape, 0)
    d = 1
    while d < T:
        m = row >= d
        b = jnp.where(m, a * pltpu.roll(b, d, 0) + b, b)
        a = jnp.where(m, a * pltpu.roll(a, d, 0), a)
        d *= 2
    return a, b


def _scan_bwd(a, b, T):
    row = lax.broadcasted_iota(jnp.int32, a.shape, 0)
    d = 1
    while d < T:
        m = row < T - d
        b = jnp.where(m, a * pltpu.roll(b, T - d, 0) + b, b)
        a = jnp.where(m, a * pltpu.roll(a, T - d, 0), a)
        d *= 2
    return b


def _lru_common(x, prev, first, cw_ref, cb_ref, wgx_ref, bgx_ref, wga_ref, bga_ref, lam_ref, T):
    row = lax.broadcasted_iota(jnp.int32, x.shape, 0)
    prev = jnp.where(first, 0.0, prev)
    xs = []
    for j in (3, 2, 1):
        pv = jnp.tile(pltpu.roll(prev, j, 0), (T // SUB, 1))
        xs.append(jnp.where(row < j, pv, pltpu.roll(x, j, 0)))
    xs.append(x)
    xc = cb_ref[...] + cw_ref[0:1, :] * xs[0] + cw_ref[1:2, :] * xs[1] + cw_ref[2:3, :] * xs[2] + cw_ref[3:4, :] * xs[3]
    xcb = xc.astype(BF16)
    gx = _sig(_nn(xcb, wgx_ref[0]) + bgx_ref[0])
    ga = _sig(_nn(xcb, wga_ref[0]) + bga_ref[0])
    sp = _softplus_neg(lam_ref[...])
    log_a = -LRU_C * ga * sp
    a = jnp.exp(log_a)
    mult = jnp.sqrt(_neg_expm1(2.0 * log_a))
    return xs, xc, xcb, gx, ga, sp, a, mult


def _lru_specs(T, tmap):
    def at(col0):
        return pl.BlockSpec((T, LANE), lambda n, i: (tmap(i), col0 + n))

    def prev(col0):
        return pl.BlockSpec((SUB, LANE), lambda n, i: (jnp.maximum(tmap(i) * (T // SUB) - 1, 0), col0 + n))

    small = [
        pl.BlockSpec((4, LANE), lambda n, i: (0, n)),
        pl.BlockSpec((1, LANE), lambda n, i: (0, n)),
        pl.BlockSpec((1, LANE, LANE), lambda n, i: (n, 0, 0)),
        pl.BlockSpec((1, 1, LANE), lambda n, i: (n, 0, 0)),
        pl.BlockSpec((1, LANE, LANE), lambda n, i: (n, 0, 0)),
        pl.BlockSpec((1, 1, LANE), lambda n, i: (n, 0, 0)),
        pl.BlockSpec((1, LANE), lambda n, i: (0, n)),
    ]
    return at, prev, small


def _lru_fwd(zp, w):
    S = zp.shape[0]
    T = T_LRU
    at, prev, small = _lru_specs(T, lambda i: i)

    def body(x_ref, xp_ref, g_ref, cw_ref, cb_ref, wgx_ref, bgx_ref, wga_ref, bga_ref, lam_ref, hs_ref, y_ref, carry_ref):
        i = pl.program_id(1)

        @pl.when(i == 0)
        def _():
            carry_ref[...] = jnp.zeros_like(carry_ref)

        x = x_ref[...]
        _, xc, _, gx, _, _, a, mult = _lru_common(x, xp_ref[...], i == 0, cw_ref, cb_ref, wgx_ref, bgx_ref, wga_ref, bga_ref, lam_ref, T)
        A, B = _scan_fwd(a, mult * gx * xc, T)
        h = B + A * carry_ref[SUB - 1:SUB, :]
        hs_ref[...] = h
        carry_ref[...] = hs_ref[T - SUB:T, :]
        y_ref[...] = (h * _silu(g_ref[...])).astype(BF16)

    out = pl.BlockSpec((T, LANE), lambda n, i: (i, n))
    return pl.pallas_call(
        body, name="lru_fwd", grid=(8, S // T),
        in_specs=[at(C_LRUX), prev(C_LRUX), at(C_LRUG)] + small, out_specs=[out, out],
        out_shape=[jax.ShapeDtypeStruct((S, D), F32), jax.ShapeDtypeStruct((S, D), BF16)],
        scratch_shapes=[pltpu.VMEM((SUB, LANE), F32)],
        compiler_params=_cp(dimension_semantics=("parallel", "arbitrary")),
    )(zp, zp, zp, w["conv_w"], w["conv_b"], w["w_gx"], w["b_gx"], w["w_ga"], w["b_ga"], w["lam"])


def _lru_bwd(zp, hs, dy, w, dz):
    S = zp.shape[0]
    T = T_LRU
    nT = S // T
    at, prev, small = _lru_specs(T, lambda i: nT - 1 - i)

    def body(x_ref, xp_ref, g_ref, h_ref, hp_ref, dy_ref, cw_ref, cb_ref, wgx_ref, bgx_ref, wga_ref, bga_ref, lam_ref, dz_in,
             dzx_ref, dcw_ref, dcb_ref, dwgx_ref, dbgx_ref, dwga_ref, dbga_ref, dlam_ref, carry_ref, head_ref):
        del dz_in
        j = pl.program_id(1)
        it = nT - 1 - j

        @pl.when(j == 0)
        def _():
            for r in (carry_ref, head_ref, dcw_ref, dcb_ref, dwgx_ref, dbgx_ref, dwga_ref, dbga_ref, dlam_ref):
                r[...] = jnp.zeros_like(r)

        first = it == 0
        x = x_ref[...]
        xs, xc, xcb, gx, ga, sp, a, mult = _lru_common(x, xp_ref[...], first, cw_ref, cb_ref, wgx_ref, bgx_ref, wga_ref, bga_ref, lam_ref, T)
        row = lax.broadcasted_iota(jnp.int32, x.shape, 0)
        u = gx * xc
        h = h_ref[...]
        hp = jnp.where(first, 0.0, hp_ref[...])
        hm1 = jnp.where(row < 1, jnp.tile(pltpu.roll(hp, 1, 0), (T // SUB, 1)), pltpu.roll(h, 1, 0))
        dho = dy_ref[...] * _silu(g_ref[...])
        gin = jnp.where(row == T - 1, dho + carry_ref[0:1, :], dho)
        abar = jnp.where(row == T - 1, 0.0, pltpu.roll(a, T - 1, 0))
        dh = _scan_bwd(abar, gin, T)
        carry_ref[...] = (a * dh)[0:SUB, :]
        da = dh * hm1
        dmult = dh * u
        du = dh * mult
        dgx = du * xc
        dxc = du * gx
        dlog_a = da * a - dmult * a * a / mult
        dga = dlog_a * (-LRU_C * sp)
        lam = lam_ref[...]
        dlam_ref[...] += _csum(dlog_a * (-LRU_C * ga)) * (-1.0 / (1.0 + jnp.exp(lam)))
        dpa = dga * ga * (1.0 - ga)
        dpx = dgx * gx * (1.0 - gx)
        dpab, dpxb = dpa.astype(BF16), dpx.astype(BF16)
        dxc = dxc + _nt(dpxb, wgx_ref[0]) + _nt(dpab, wga_ref[0])
        dwgx_ref[0] += _tn(xcb, dpxb)
        dwga_ref[0] += _tn(xcb, dpab)
        dbgx_ref[0] += _csum(dpx)
        dbga_ref[0] += _csum(dpa)
        dcb_ref[...] += _csum(dxc)
        for k in range(4):
            dcw_ref[k:k + 1, :] += _csum(dxc * xs[k])
        head = head_ref[...]
        dx = cw_ref[3:4, :] * dxc
        for jj in (1, 2, 3):
            hv = jnp.tile(pltpu.roll(head, SUB - jj, 0), (T // SUB, 1))
            dx = dx + cw_ref[3 - jj:4 - jj, :] * jnp.where(row >= T - jj, hv, pltpu.roll(dxc, T - jj, 0))
        head_ref[...] = dxc[0:SUB, :]
        dzx_ref[...] = dx.astype(BF16)

    def acc(shape, imap):
        return pl.BlockSpec(shape, imap)

    out_specs = [
        pl.BlockSpec((T, LANE), lambda n, i: (nT - 1 - i, C_LRUX + n)),
        acc((4, LANE), lambda n, i: (0, n)), acc((1, LANE), lambda n, i: (0, n)),
        acc((1, LANE, LANE), lambda n, i: (n, 0, 0)), acc((1, 1, LANE), lambda n, i: (n, 0, 0)),
        acc((1, LANE, LANE), lambda n, i: (n, 0, 0)), acc((1, 1, LANE), lambda n, i: (n, 0, 0)),
        acc((1, LANE), lambda n, i: (0, n)),
    ]
    out_shape = [
        jax.ShapeDtypeStruct(dz.shape, BF16),
        jax.ShapeDtypeStruct((4, D), F32), jax.ShapeDtypeStruct((1, D), F32),
        jax.ShapeDtypeStruct((8, LANE, LANE), F32), jax.ShapeDtypeStruct((8, 1, LANE), F32),
        jax.ShapeDtypeStruct((8, LANE, LANE), F32), jax.ShapeDtypeStruct((8, 1, LANE), F32),
        jax.ShapeDtypeStruct((1, D), F32),
    ]
    dyspec = pl.BlockSpec((T, LANE), lambda n, i: (nT - 1 - i, n))
    hprev = pl.BlockSpec((SUB, LANE), lambda n, i: (jnp.maximum((nT - 1 - i) * (T // SUB) - 1, 0), n))
    return pl.pallas_call(
        body, name="lru_bwd", grid=(8, nT),
        in_specs=[at(C_LRUX), prev(C_LRUX), at(C_LRUG), dyspec, hprev, dyspec] + small + [pl.BlockSpec(memory_space=pl.ANY)],
        out_specs=out_specs, out_shape=out_shape,
        scratch_shapes=[pltpu.VMEM((SUB, LANE), F32), pltpu.VMEM((SUB, LANE), F32)],
        input_output_aliases={13: 0},
        compiler_params=_cp(dimension_semantics=("parallel", "arbitrary")),
    )(zp, zp, zp, hs, hs, dy, w["conv_w"], w["conv_b"], w["w_gx"], w["b_gx"], w["w_ga"], w["b_ga"], w["lam"], dz)


def _lru_gate_bwd(zp, hs, dy, dz):
    S = zp.shape[0]
    T = T_ROW

    def body(g_ref, h_ref, dy_ref, dz_in, o_ref):
        del dz_in
        o_ref[...] = (dy_ref[...] * h_ref[...] * _dsilu(g_ref[...])).astype(BF16)

    row = pl.BlockSpec((T, D), lambda i: (i, 0))
    zc = pl.BlockSpec((T, D), lambda i: (i, C_LRUG // 8))
    return pl.pallas_call(
        body, name="lru_gate_bwd", grid=(S // T,), in_specs=[zc, row, row, pl.BlockSpec(memory_space=pl.ANY)], out_specs=zc,
        out_shape=jax.ShapeDtypeStruct(dz.shape, BF16), input_output_aliases={3: 0}, compiler_params=_cp(),
    )(zp, hs, dy, dz)


def _rope_tables(pos):
    pf = pos.astype(F32)[:, None]

    def cs(d):
        inv = ROPE_THETA ** (-jnp.arange(0, d, 2, dtype=F32) / d)
        ang = pf * inv
        return jnp.cos(ang), jnp.sin(ang)

    S = pos.shape[0]
    c, s = cs(32)
    one, zero = jnp.ones((S, 64), F32), jnp.zeros((S, 16), F32)
    z32, z64 = jnp.zeros((S, 32), F32), jnp.zeros((S, 64), F32)
    mla = (jnp.concatenate([one, c, c, jnp.ones((S, 32), F32)], 1),
           jnp.concatenate([z64, zero, s, z32], 1),
           jnp.concatenate([z64, -s, zero, z32], 1))
    c, s = cs(64)
    dil = (jnp.concatenate([c, c, c, c], 1),
           jnp.concatenate([z32, s, z32, s], 1),
           jnp.concatenate([-s, z32, -s, z32], 1))
    return mla, dil


def _rope(x, C, S1, S2, sh):
    return x * C + pltpu.roll(x, sh, 1) * S1 + pltpu.roll(x, LANE - sh, 1) * S2


def _rope_t(dy, C, S1, S2, sh):
    return dy * C + pltpu.roll(dy * S1, LANE - sh, 1) + pltpu.roll(dy * S2, sh, 1)


def _lane(shape):
    return lax.broadcasted_iota(jnp.int32, shape, 1)


T_MLA = 256
TA = 512


def _zcol(T, width, col_lanes):
    assert (col_lanes * LANE) % width == 0
    return pl.BlockSpec((T, width), lambda i: (i, col_lanes * LANE // width))


def _full(shape):
    return pl.BlockSpec(shape, lambda *_: (0,) * len(shape))


def _mla_pre_fwd(zp, w, tab):
    S = zp.shape[0]
    T = T_MLA

    def body(cq_ref, ckv_ref, kr_ref, gcq_ref, gckv_ref, wuq_ref, wuk_ref, wuv_ref, gq_ref, gk_ref, C_ref, S1_ref, S2_ref,
             q_ref, k_ref, v_ref):
        cq = cq_ref[...]
        cqn = (cq * lax.rsqrt(jnp.mean(cq * cq, axis=-1, keepdims=True) + EPS) * gcq_ref[...]).astype(BF16)
        ckv = ckv_ref[...]
        ckvn = (ckv * lax.rsqrt(jnp.mean(ckv * ckv, axis=-1, keepdims=True) + EPS) * gckv_ref[...]).astype(BF16)
        q0 = _nn(cqn, wuq_ref[...])
        k0 = _nn(ckvn, wuk_ref[...])
        krb = kr_ref[...]
        C, S1, S2 = C_ref[...], S1_ref[...], S2_ref[...]
        for h in range(8):
            sl = slice(h * LANE, (h + 1) * LANE)
            xq = q0[:, sl]
            xq = xq * lax.rsqrt(_rsum(xq * xq) * (1.0 / MLA_QK) + EPS) * gq_ref[...]
            q_ref[:, sl] = _rope(xq, C, S1, S2, 16).astype(BF16)
            xk = k0[:, sl] + krb
            xk = xk * lax.rsqrt(_rsum(xk * xk) * (1.0 / MLA_QK) + EPS) * gk_ref[...]
            k_ref[:, sl] = _rope(xk, C, S1, S2, 16).astype(BF16)
        v_ref[...] = _nn(ckvn, wuv_ref[...]).astype(BF16)

    tabspec = pl.BlockSpec((T, LANE), lambda i: (i, 0))
    in_specs = [_zcol(T, 256, C_CQ), _zcol(T, LANE, C_CKV), _zcol(T, LANE, C_KR), _full((1, 256)), _full((1, LANE)),
                _full((256, 1024)), _full((LANE, 1024)), _full((LANE, 512)), _full((1, LANE)), _full((1, LANE)),
                tabspec, tabspec, tabspec]
    return pl.pallas_call(
        body, name="mla_pre_fwd", grid=(S // T,), in_specs=in_specs,
        out_specs=[pl.BlockSpec((T, 1024), lambda i: (i, 0)), pl.BlockSpec((T, 1024), lambda i: (i, 0)), pl.BlockSpec((T, 512), lambda i: (i, 0))],
        out_shape=[jax.ShapeDtypeStruct((S, 1024), BF16), jax.ShapeDtypeStruct((S, 1024), BF16), jax.ShapeDtypeStruct((S, 512), BF16)],
        compiler_params=_cp(),
    )(zp, zp, zp, w["g_cq"], w["g_ckv"], w["w_uq"], w["w_uk"], w["w_uv"], w["g_mq"], w["g_mk"], *tab)


def _mla_attn_fwd(q, k, v, zp):
    S = q.shape[0]
    nq = S // TA

    def body(q_ref, k_ref, v_ref, g_ref, o_ref, lse_ref, y_ref):
        qi = pl.program_id(1)
        lane = _lane((TA, LANE))
        rowi = lax.broadcasted_iota(jnp.int32, (TA, TA), 0)
        coli = lax.broadcasted_iota(jnp.int32, (TA, TA), 1)
        o_tot = jnp.zeros((TA, LANE), F32)
        for hh in range(2):
            cs = slice(hh * LANE, (hh + 1) * LANE)
            hm = (lane < 64) if hh == 0 else (lane >= 64)
            qh = q_ref[:, cs]

            def step(kb, carry, masked, cs=cs, hm=hm, qh=qh):
                m, l, acc = carry
                off = pl.multiple_of(kb * TA, TA)
                kh = k_ref[pl.ds(off, TA), cs]
                vv = v_ref[pl.ds(off, TA), :]
                vh = jnp.where(hm, vv, jnp.zeros_like(vv))
                s = _nt(qh, kh) * MLA_SCALE
                if masked:
                    s = jnp.where(rowi >= coli, s, NEG)
                m_new = jnp.maximum(m, jnp.max(s, axis=-1, keepdims=True))
                alpha = jnp.exp(m - m_new)
                p = jnp.exp(s - m_new)
                l = alpha * l + _rsum(p)
                acc = alpha * acc + _nn(p.astype(BF16), vh)
                return m_new, l, acc

            init = (jnp.full((TA, 1), NEG, F32), jnp.zeros((TA, 1), F32), jnp.zeros((TA, LANE), F32))
            carry = lax.fori_loop(0, qi, lambda kb, c: step(kb, c, False), init)
            m, l, acc = step(qi, carry, True)
            o_tot = o_tot + acc / l
            lse_ref[:, cs] = jnp.broadcast_to(m + jnp.log(l), (TA, LANE))
        o_ref[...] = o_tot
        y_ref[...] = (o_tot * _silu(g_ref[...])).astype(BF16)

    blk = pl.BlockSpec((TA, LANE), lambda p, i: (i, p))
    return pl.pallas_call(
        body, name="mla_attn_fwd", grid=(4, nq),
        in_specs=[pl.BlockSpec((TA, 256), lambda p, i: (i, p)), pl.BlockSpec((S, 256), lambda p, i: (0, p)),
                  pl.BlockSpec((S, LANE), lambda p, i: (0, p)), pl.BlockSpec((TA, LANE), lambda p, i: (i, C_MLAG + p))],
        out_specs=[blk, pl.BlockSpec((TA, 256), lambda p, i: (i, p)), blk],
        out_shape=[jax.ShapeDtypeStruct((S, 512), F32), jax.ShapeDtypeStruct((S, 1024), F32), jax.ShapeDtypeStruct((S, 512), BF16)],
        compiler_params=_cp(dimension_semantics=("parallel", "arbitrary")),
    )(q, k, v, zp)


def _mla_post_bwd(zp, o, dy, dz):
    S = zp.shape[0]
    T = T_ROW

    def body(g_ref, o_ref, dy_ref, dz_in, dz_ref, do_ref, D_ref):
        del dz_in
        g, o_, dy_ = g_ref[...], o_ref[...], dy_ref[...]
        do = dy_ * _silu(g)
        do_ref[...] = do.astype(BF16)
        dz_ref[...] = (dy_ * o_ * _dsilu(g)).astype(BF16)
        prod = do * o_
        lane = _lane((T, LANE))
        for p in range(4):
            pr = prod[:, p * LANE:(p + 1) * LANE]
            da = _rsum(jnp.where(lane < 64, pr, 0.0))
            db = _rsum(jnp.where(lane >= 64, pr, 0.0))
            D_ref[:, 2 * p * LANE:(2 * p + 1) * LANE] = jnp.broadcast_to(da, (T, LANE))
            D_ref[:, (2 * p + 1) * LANE:(2 * p + 2) * LANE] = jnp.broadcast_to(db, (T, LANE))

    row = pl.BlockSpec((T, 512), lambda i: (i, 0))
    zc = _zcol(T, 512, C_MLAG)
    return pl.pallas_call(
        body, name="mla_post_bwd", grid=(S // T,), in_specs=[zc, row, row, pl.BlockSpec(memory_space=pl.ANY)],
        out_specs=[zc, row, pl.BlockSpec((T, 1024), lambda i: (i, 0))],
        out_shape=[jax.ShapeDtypeStruct(dz.shape, BF16), jax.ShapeDtypeStruct((S, 512), BF16), jax.ShapeDtypeStruct((S, 1024), F32)],
        input_output_aliases={3: 0}, compiler_params=_cp(),
    )(zp, o, dy, dz)


def _mla_attn_bwd(q, k, v, do, lse, Dr):
    S = q.shape[0]
    nq = S // TA

    def body(q_ref, do_ref, lse_ref, D_ref, k_ref, v_ref, dq_ref, dk_ref, dv_ref):
        ki = pl.program_id(1)

        @pl.when(ki == 0)
        def _():
            dq_ref[...] = jnp.zeros_like(dq_ref)

        lane = _lane((TA, LANE))
        rowi = lax.broadcasted_iota(jnp.int32, (TA, TA), 0)
        coli = lax.broadcasted_iota(jnp.int32, (TA, TA), 1)
        dv_tot = jnp.zeros((TA, LANE), F32)
        for hh in range(2):
            cs = slice(hh * LANE, (hh + 1) * LANE)
            hm = (lane < 64) if hh == 0 else (lane >= 64)
            kh = k_ref[:, cs]
            vv = v_ref[...]
            vm = jnp.where(hm, vv, jnp.zeros_like(vv))

            def step(qb, carry, masked, cs=cs, kh=kh, vm=vm):
                dk_acc, dv_acc = carry
                off = pl.multiple_of(qb * TA, TA)
                qh = q_ref[pl.ds(off, TA), cs]
                doh = do_ref[pl.ds(off, TA), :]
                ls = jnp.tile(lse_ref[pl.ds(off, TA), cs], (1, TA // LANE))
                dd = jnp.tile(D_ref[pl.ds(off, TA), cs], (1, TA // LANE))
                s = _nt(qh, kh) * MLA_SCALE
                if masked:
                    s = jnp.where(rowi >= coli, s, NEG)
                p = jnp.exp(s - ls)
                dp = _nt(doh, vm)
                ds = (p * (dp - dd) * MLA_SCALE).astype(BF16)
                dv_acc = dv_acc + _tn(p.astype(BF16), doh)
                dk_acc = dk_acc + _tn(ds, qh)
                dq_ref[pl.ds(off, TA), cs] += _nn(ds, kh)
                return dk_acc, dv_acc

            z = jnp.zeros((TA, LANE), F32)
            carry = step(ki, (z, z), True)
            dk_acc, dv_acc = lax.fori_loop(ki + 1, nq, lambda qb, c: step(qb, c, False), carry)
            dk_ref[:, cs] = dk_acc
            dv_tot = dv_tot + jnp.where(hm, dv_acc, 0.0)
        dv_ref[...] = dv_tot

    pair = pl.BlockSpec((S, 256), lambda p, i: (0, p))
    return pl.pallas_call(
        body, name="mla_attn_bwd", grid=(4, nq),
        in_specs=[pair, pl.BlockSpec((S, LANE), lambda p, i: (0, p)), pair, pair,
                  pl.BlockSpec((TA, 256), lambda p, i: (i, p)), pl.BlockSpec((TA, LANE), lambda p, i: (i, p))],
        out_specs=[pair, pl.BlockSpec((TA, 256), lambda p, i: (i, p)), pl.BlockSpec((TA, LANE), lambda p, i: (i, p))],
        out_shape=[jax.ShapeDtypeStruct((S, 1024), F32), jax.ShapeDtypeStruct((S, 1024), F32), jax.ShapeDtypeStruct((S, 512), F32)],
        compiler_params=_cp(dimension_semantics=("parallel", "arbitrary")),
    )(q, do, lse, Dr, k, v)


def _mla_pre_bwd(zp, dq, dk, dv, w, tab, dz):
    S = zp.shape[0]
    T = T_MLA

    def body(cq_ref, ckv_ref, kr_ref, dq_ref, dk_ref, dv_ref, gcq_ref, gckv_ref, wuq_ref, wuk_ref, wuv_ref, gq_ref, gk_ref,
             C_ref, S1_ref, S2_ref, dz_in, dz_ref, dwuq_ref, dwuk_ref, dwuv_ref, dgcq_ref, dgckv_ref, dgq_ref, dgk_ref):
        del dz_in
        i = pl.program_id(0)

        @pl.when(i == 0)
        def _():
            for r in (dwuq_ref, dwuk_ref, dwuv_ref, dgcq_ref, dgckv_ref, dgq_ref, dgk_ref):
                r[...] = jnp.zeros_like(r)

        cq = cq_ref[...]
        rq = lax.rsqrt(jnp.mean(cq * cq, axis=-1, keepdims=True) + EPS)
        cqh = cq * rq
        cqn = (cqh * gcq_ref[...]).astype(BF16)
        ckv = ckv_ref[...]
        rkv = lax.rsqrt(jnp.mean(ckv * ckv, axis=-1, keepdims=True) + EPS)
        ckvh = ckv * rkv
        ckvn = (ckvh * gckv_ref[...]).astype(BF16)
        q0 = _nn(cqn, wuq_ref[...])
        k0 = _nn(ckvn, wuk_ref[...])
        krb = kr_ref[...]
        C, S1, S2 = C_ref[...], S1_ref[...], S2_ref[...]
        gq, gk = gq_ref[...], gk_ref[...]

        def head_bwd(x, dy, g):
            r = lax.rsqrt(_rsum(x * x) * (1.0 / MLA_QK) + EPS)
            xn = x * r
            dyn = _rope_t(dy, C, S1, S2, 16)
            dxh = dyn * g
            return r * (dxh - xn * _rsum(dxh * xn) * (1.0 / MLA_QK)), _csum(dyn * xn)

        dq0, dk0 = [], []
        dgq_acc = jnp.zeros((1, LANE), F32)
        dgk_acc = jnp.zeros((1, LANE), F32)
        dkr = jnp.zeros((T, LANE), F32)
        for h in range(8):
            sl = slice(h * LANE, (h + 1) * LANE)
            dxq, gq_p = head_bwd(q0[:, sl], dq_ref[:, sl], gq)
            dxk, gk_p = head_bwd(k0[:, sl] + krb, dk_ref[:, sl], gk)
            dq0.append(dxq.astype(BF16))
            dk0.append(dxk.astype(BF16))
            dkr = dkr + dxk
            dgq_acc = dgq_acc + gq_p
            dgk_acc = dgk_acc + gk_p
        dgq_ref[...] += dgq_acc
        dgk_ref[...] += dgk_acc
        dq0 = jnp.concatenate(dq0, axis=1)
        dk0 = jnp.concatenate(dk0, axis=1)
        dvb = dv_ref[...].astype(BF16)
        dwuq_ref[...] += _tn(cqn, dq0)
        dwuk_ref[...] += _tn(ckvn, dk0)
        dwuv_ref[...] += _tn(ckvn, dvb)
        dcqn = _nt(dq0, wuq_ref[...])
        dckvn = _nt(dk0, wuk_ref[...]) + _nt(dvb, wuv_ref[...])
        dgcq_ref[...] += _csum(dcqn * cqh)
        dgckv_ref[...] += _csum(dckvn * ckvh)
        dxh = dcqn * gcq_ref[...]
        dz_ref[:, 0:256] = (rq * (dxh - cqh * jnp.mean(dxh * cqh, axis=-1, keepdims=True))).astype(BF16)
        dxh = dckvn * gckv_ref[...]
        dz_ref[:, 256:384] = (rkv * (dxh - ckvh * jnp.mean(dxh * ckvh, axis=-1, keepdims=True))).astype(BF16)
        lane = _lane((T, LANE))
        dz_ref[:, 384:512] = jnp.where((lane >= KR_LANE) & (lane < KR_LANE + 32), dkr, 0.0).astype(BF16)

    tabspec = pl.BlockSpec((T, LANE), lambda i: (i, 0))
    in_specs = [_zcol(T, 256, C_CQ), _zcol(T, LANE, C_CKV), _zcol(T, LANE, C_KR),
                pl.BlockSpec((T, 1024), lambda i: (i, 0)), pl.BlockSpec((T, 1024), lambda i: (i, 0)), pl.BlockSpec((T, 512), lambda i: (i, 0)),
                _full((1, 256)), _full((1, LANE)), _full((256, 1024)), _full((LANE, 1024)), _full((LANE, 512)), _full((1, LANE)), _full((1, LANE)),
                tabspec, tabspec, tabspec, pl.BlockSpec(memory_space=pl.ANY)]
    out_specs = [_zcol(T, 512, C_CQ), _full((256, 1024)), _full((LANE, 1024)), _full((LANE, 512)), _full((1, 256)), _full((1, LANE)),
                 _full((1, LANE)), _full((1, LANE))]
    out_shape = [jax.ShapeDtypeStruct(dz.shape, BF16), jax.ShapeDtypeStruct((256, 1024), F32), jax.ShapeDtypeStruct((LANE, 1024), F32),
                 jax.ShapeDtypeStruct((LANE, 512), F32), jax.ShapeDtypeStruct((1, 256), F32), jax.ShapeDtypeStruct((1, LANE), F32),
                 jax.ShapeDtypeStruct((1, LANE), F32), jax.ShapeDtypeStruct((1, LANE), F32)]
    return pl.pallas_call(
        body, name="mla_pre_bwd", grid=(S // T,), in_specs=in_specs, out_specs=out_specs, out_shape=out_shape,
        input_output_aliases={16: 0}, compiler_params=_cp(),
    )(zp, zp, zp, dq, dk, dv, w["g_cq"], w["g_ckv"], w["w_uq"], w["w_uk"], w["w_uv"], w["g_mq"], w["g_mk"], *tab, dz)


T_DIL = 256


def _head_stats(x, lane):
    sq = x * x
    sa = _rsum(jnp.where(lane < 64, sq, 0.0))
    sb = _rsum(jnp.where(lane >= 64, sq, 0.0))
    return lax.rsqrt(jnp.where(lane < 64, sa, sb) * (1.0 / DIL_HD) + EPS)


def _head_sum(x, lane):
    sa = _rsum(jnp.where(lane < 64, x, 0.0))
    sb = _rsum(jnp.where(lane >= 64, x, 0.0))
    return jnp.where(lane < 64, sa, sb)


def _dil_pre_fwd(zp, w, tab):
    S = zp.shape[0]
    T = T_DIL

    def body(q_ref, k_ref, v_ref, gq_ref, gk_ref, C_ref, S1_ref, S2_ref, qo_ref, ko_ref, vo_ref):
        C, S1, S2 = C_ref[...], S1_ref[...], S2_ref[...]
        lane = _lane((T, LANE))
        for b in range(12):
            sl = slice(b * LANE, (b + 1) * LANE)
            x = q_ref[:, sl]
            qo_ref[:, sl] = _rope(x * _head_stats(x, lane) * gq_ref[...], C, S1, S2, 32).astype(BF16)
            x = k_ref[:, sl]
            ko_ref[:, sl] = _rope(x * _head_stats(x, lane) * gk_ref[...], C, S1, S2, 32).astype(BF16)
        vo_ref[...] = v_ref[...].astype(BF16)

    tabspec = pl.BlockSpec((T, LANE), lambda i: (i, 0))
    out = pl.BlockSpec((T, 1536), lambda i: (i, 0))
    return pl.pallas_call(
        body, name="dil_pre_fwd", grid=(S // T,),
        in_specs=[_zcol(T, 1536, C_DQ), _zcol(T, 1536, C_DK), _zcol(T, 1536, C_DV), _full((1, LANE)), _full((1, LANE)), tabspec, tabspec, tabspec],
        out_specs=[out, out, out], out_shape=[jax.ShapeDtypeStruct((S, 1536), BF16)] * 3, compiler_params=_cp(),
    )(zp, zp, zp, w["g_dq"], w["g_dk"], *tab)


def _dil_masks(n):
    row = lax.broadcasted_iota(jnp.int32, (NK, NK), 0)
    col = lax.broadcasted_iota(jnp.int32, (NK, NK), 1)
    return col <= row, (col >= row) & (n > 0)


def _dil_attn_fwd(q, k, v, col0, ncol, name):
    M = q.shape[0]
    nb = M // NK

    def body(q_ref, kc_ref, kp_ref, vc_ref, vp_ref, o_ref, lse_ref):
        n = pl.program_id(0)
        mc, mp = _dil_masks(n)
        lane = _lane((NK, LANE))
        q_, kc, kp, vc, vp = q_ref[...], kc_ref[...], kp_ref[...], vc_ref[...], vp_ref[...]
        zb = jnp.zeros_like(q_)
        o_tot = jnp.zeros((NK, LANE), F32)
        lse_tot = jnp.zeros((NK, LANE), F32)
        for hh in range(2):
            hm = (lane < 64) if hh == 0 else (lane >= 64)
            qm = jnp.where(hm, q_, zb)
            sc = jnp.where(mc, _nt(qm, kc) * DIL_SCALE, NEG)
            sp = jnp.where(mp, _nt(qm, kp) * DIL_SCALE, NEG)
            m = jnp.maximum(jnp.max(sc, axis=-1, keepdims=True), jnp.max(sp, axis=-1, keepdims=True))
            ec = jnp.exp(sc - m)
            ep = jnp.exp(sp - m)
            den = _rsum(ec) + _rsum(ep)
            o = (_nn(ec.astype(BF16), jnp.where(hm, vc, zb)) + _nn(ep.astype(BF16), jnp.where(hm, vp, zb))) / den
            o_tot = o_tot + o
            lse_tot = jnp.where(hm, m + jnp.log(den), lse_tot)
        o_ref[...] = o_tot
        lse_ref[...] = lse_tot

    cur = pl.BlockSpec((NK, LANE), lambda n, c: (n, col0 + c))
    prv = pl.BlockSpec((NK, LANE), lambda n, c: (jnp.maximum(n - 1, 0), col0 + c))
    out = pl.BlockSpec((NK, LANE), lambda n, c: (n, c))
    return pl.pallas_call(
        body, name=name, grid=(nb, ncol), in_specs=[cur, cur, prv, cur, prv], out_specs=[out, out],
        out_shape=[jax.ShapeDtypeStruct((M, ncol * LANE), F32)] * 2, compiler_params=_cp(),
    )(q, k, k, v, v)


def _dil_combine(os_, ls_, zp):
    S = zp.shape[0]
    T = T_ROW

    def body(o0, o1, o2, l0, l1, l2, g_ref, oc_ref, L_ref, y_ref):
        a, b, c = l0[...], l1[...], l2[...]
        mx = jnp.maximum(jnp.maximum(a, b), c)
        ea, eb, ec = jnp.exp(a - mx), jnp.exp(b - mx), jnp.exp(c - mx)
        den = ea + eb + ec
        oc = (ea * o0[...] + eb * o1[...] + ec * o2[...]) / den
        oc_ref[...] = oc
        L_ref[...] = mx + jnp.log(den)
        y_ref[...] = (oc * _silu(g_ref[...])).astype(BF16)

    row = pl.BlockSpec((T, 512), lambda i: (i, 0))
    return pl.pallas_call(
        body, name="dil_combine", grid=(S // T,), in_specs=[row] * 6 + [_zcol(T, 512, C_DILG)], out_specs=[row, row, row],
        out_shape=[jax.ShapeDtypeStruct((S, 512), F32), jax.ShapeDtypeStruct((S, 512), F32), jax.ShapeDtypeStruct((S, 512), BF16)],
        compiler_params=_cp(),
    )(*os_, *ls_, zp)


def _dil_comb_bwd(zp, oc, dy, dz):
    S = zp.shape[0]
    T = T_ROW

    def body(g_ref, o_ref, dy_ref, dz_in, dz_ref, do_ref, D_ref):
        del dz_in
        g, o_, dy_ = g_ref[...], o_ref[...], dy_ref[...]
        do = dy_ * _silu(g)
        do_ref[...] = do.astype(BF16)
        dz_ref[...] = (dy_ * o_ * _dsilu(g)).astype(BF16)
        lane = _lane((T, LANE))
        for p in range(4):
            sl = slice(p * LANE, (p + 1) * LANE)
            D_ref[:, sl] = _head_sum(do[:, sl] * o_[:, sl], lane)

    row = pl.BlockSpec((T, 512), lambda i: (i, 0))
    zc = _zcol(T, 512, C_DILG)
    return pl.pallas_call(
        body, name="dil_comb_bwd", grid=(S // T,), in_specs=[zc, row, row, pl.BlockSpec(memory_space=pl.ANY)], out_specs=[zc, row, row],
        out_shape=[jax.ShapeDtypeStruct(dz.shape, BF16), jax.ShapeDtypeStruct((S, 512), BF16), jax.ShapeDtypeStruct((S, 512), F32)],
        input_output_aliases={3: 0}, compiler_params=_cp(),
    )(zp, oc, dy, dz)


def _dil_attn_bwd(q, k, v, do, L, Dr, col0, ncol, name):
    M = q.shape[0]
    nb = M // NK

    def body(qc_ref, qn_ref, kc_ref, kp_ref, vc_ref, vp_ref, doc_ref, don_ref, Lc_ref, Ln_ref, Dc_ref, Dn_ref, dq_ref, dk_ref, dv_ref):
        n = pl.program_id(0)
        mc, mp = _dil_masks(n)
        mnext = (lax.broadcasted_iota(jnp.int32, (NK, NK), 1) >= lax.broadcasted_iota(jnp.int32, (NK, NK), 0)) & (n < nb - 1)
        lane = _lane((NK, LANE))
        qc, qn, kc, kp, vc, vp = qc_ref[...], qn_ref[...], kc_ref[...], kp_ref[...], vc_ref[...], vp_ref[...]
        doc, don = doc_ref[...], don_ref[...]
        zb = jnp.zeros_like(qc)
        dq_tot = jnp.zeros((NK, LANE), F32)
        dk_tot = jnp.zeros((NK, LANE), F32)
        dv_tot = jnp.zeros((NK, LANE), F32)
        for hh in range(2):
            hm = (lane < 64) if hh == 0 else (lane >= 64)

            def bcast(x, hm=hm):
                return jnp.where(hm, x, pltpu.roll(x, 64, 1))

            Lc, Ln, Dc, Dn = bcast(Lc_ref[...]), bcast(Ln_ref[...]), bcast(Dc_ref[...]), bcast(Dn_ref[...])
            qm = jnp.where(hm, qc, zb)
            qnm = jnp.where(hm, qn, zb)
            vcm = jnp.where(hm, vc, zb)
            vpm = jnp.where(hm, vp, zb)
            pc = jnp.exp(jnp.where(mc, _nt(qm, kc) * DIL_SCALE, NEG) - Lc)
            pp = jnp.exp(jnp.where(mp, _nt(qm, kp) * DIL_SCALE, NEG) - Lc)
            dsc = (pc * (_nt(doc, vcm) - Dc) * DIL_SCALE).astype(BF16)
            dsp = (pp * (_nt(doc, vpm) - Dc) * DIL_SCALE).astype(BF16)
            dq_tot = dq_tot + jnp.where(hm, _nn(dsc, kc) + _nn(dsp, kp), 0.0)
            p2 = jnp.exp(jnp.where(mnext, _nt(qnm, kc) * DIL_SCALE, NEG) - Ln)
            ds2 = (p2 * (_nt(don, vcm) - Dn) * DIL_SCALE).astype(BF16)
            dk_tot = dk_tot + _tn(dsc, qm) + _tn(ds2, qnm)
            dv_tot = dv_tot + jnp.where(hm, _tn(pc.astype(BF16), doc) + _tn(p2.astype(BF16), don), 0.0)
        dq_ref[...] = dq_tot
        dk_ref[...] = dk_tot
        dv_ref[...] = dv_tot

    def spec(shift, c0):
        return pl.BlockSpec((NK, LANE), lambda n, c: (jnp.clip(n + shift, 0, nb - 1), c0 + c))

    out = spec(0, 0)
    return pl.pallas_call(
        body, name=name, grid=(nb, ncol),
        in_specs=[spec(0, col0), spec(1, col0), spec(0, col0), spec(-1, col0), spec(0, col0), spec(-1, col0),
                  spec(0, 0), spec(1, 0), spec(0, 0), spec(1, 0), spec(0, 0), spec(1, 0)],
        out_specs=[out, out, out], out_shape=[jax.ShapeDtypeStruct((M, ncol * LANE), F32)] * 3, compiler_params=_cp(),
    )(q, q, k, k, v, v, do, do, L, L, Dr, Dr)


def _dil_pre_bwd(zp, dy, g, tab, dz, col, name):
    S = zp.shape[0]
    T = T_DIL

    def body(x_ref, dy_ref, g_ref, C_ref, S1_ref, S2_ref, dz_in, dz_ref, dg_ref):
        del dz_in
        i = pl.program_id(0)
        C, S1, S2 = C_ref[...], S1_ref[...], S2_ref[...]
        lane = _lane((T, LANE))
        gv = g_ref[...]
        acc = jnp.zeros((1, LANE), F32)
        for b in range(12):
            sl = slice(b * LANE, (b + 1) * LANE)
            x = x_ref[:, sl]
            r = _head_stats(x, lane)
            xn = x * r
            dyn = _rope_t(dy_ref[:, sl], C, S1, S2, 32)
            acc = acc + _csum(dyn * xn)
            dxh = dyn * gv
            dz_ref[:, sl] = (r * (dxh - xn * _head_sum(dxh * xn, lane) * (1.0 / DIL_HD))).astype(BF16)

        @pl.when(i == 0)
        def _():
            dg_ref[...] = acc

        @pl.when(i > 0)
        def _():
            dg_ref[...] += acc

    tabspec = pl.BlockSpec((T, LANE), lambda i: (i, 0))
    zc = _zcol(T, 1536, col)
    return pl.pallas_call(
        body, name=name, grid=(S // T,),
        in_specs=[zc, pl.BlockSpec((T, 1536), lambda i: (i, 0)), _full((1, LANE)), tabspec, tabspec, tabspec, pl.BlockSpec(memory_space=pl.ANY)],
        out_specs=[zc, _full((1, LANE))], out_shape=[jax.ShapeDtypeStruct(dz.shape, BF16), jax.ShapeDtypeStruct((1, LANE), F32)],
        input_output_aliases={6: 0}, compiler_params=_cp(),
    )(zp, dy, g, *tab, dz)


def _cast_into(src, dz, col, width, name):
    S = src.shape[0]
    T = T_ROW

    def body(s_ref, dz_in, o_ref):
        del dz_in
        o_ref[...] = s_ref[...].astype(BF16)

    return pl.pallas_call(
        body, name=name, grid=(S // T,), in_specs=[pl.BlockSpec((T, width), lambda i: (i, 0)), pl.BlockSpec(memory_space=pl.ANY)],
        out_specs=_zcol(T, width, col), out_shape=jax.ShapeDtypeStruct(dz.shape, BF16), input_output_aliases={1: 0}, compiler_params=_cp(),
    )(src, dz)


T_MRG = 256


def _merge_fwd(P, zp, b_merge):
    S = zp.shape[0]
    T = T_MRG

    def body(p0, p1, p2, m0, m1, m2, b_ref, o_ref):
        acc = jnp.zeros((T, D), F32)
        for j, (p, m) in enumerate(((p0, m0), (p1, m1), (p2, m2))):
            acc = acc + _sig(m[...] + b_ref[:, j * D:(j + 1) * D]) * p[...]
        o_ref[...] = acc.astype(BF16)

    row = pl.BlockSpec((T, D), lambda i: (i, 0))
    return pl.pallas_call(
        body, name="merge_fwd", grid=(S // T,),
        in_specs=[row, row, row] + [_zcol(T, D, C_MERGE + 8 * j) for j in range(3)] + [_full((1, 3 * D))], out_specs=row,
        out_shape=jax.ShapeDtypeStruct((S, D), BF16), compiler_params=_cp(),
    )(*P, zp, zp, zp, b_merge)


def _merge_bwd(dm, Pj, zp, bj, dz, j):
    S = zp.shape[0]
    T = T_MRG

    def body(dm_ref, p_ref, m_ref, b_ref, dz_in, dz_ref, dp_ref, db_ref):
        del dz_in
        i = pl.program_id(0)
        g = _sig(m_ref[...] + b_ref[...])
        dmv = dm_ref[...]
        dp_ref[...] = (dmv * g).astype(BF16)
        dg = dmv * p_ref[...] * g * (1.0 - g)
        dz_ref[...] = dg.astype(BF16)
        part = _csum(dg)

        @pl.when(i == 0)
        def _():
            db_ref[...] = part

        @pl.when(i > 0)
        def _():
            db_ref[...] += part

    row = pl.BlockSpec((T, D), lambda i: (i, 0))
    zc = _zcol(T, D, C_MERGE + 8 * j)
    return pl.pallas_call(
        body, name=f"merge_bwd{j}", grid=(S // T,), in_specs=[row, row, zc, _full((1, D)), pl.BlockSpec(memory_space=pl.ANY)],
        out_specs=[zc, row, _full((1, D))],
        out_shape=[jax.ShapeDtypeStruct(dz.shape, BF16), jax.ShapeDtypeStruct((S, D), BF16), jax.ShapeDtypeStruct((1, D), F32)],
        input_output_aliases={4: 0}, compiler_params=_cp(),
    )(dm, Pj, zp, bj, dz)


def _loss_fwd_bwd(y, target):
    S = y.shape[0]
    T = T_ROW

    def body(y_ref, t_ref, loss_ref, dy_ref):
        i = pl.program_id(0)
        err = y_ref[...] - t_ref[...]
        dy_ref[...] = err * (1.0 / D)
        part = jnp.sum(err * err, keepdims=True).reshape(1, 1) * (0.5 / D)

        @pl.when(i == 0)
        def _():
            loss_ref[...] = part

        @pl.when(i > 0)
        def _():
            loss_ref[...] += part

    row = pl.BlockSpec((T, D), lambda i: (i, 0))
    return pl.pallas_call(
        body, name="loss", grid=(S // T,), in_specs=[row, row], out_specs=[_full((1, 1)), row],
        out_shape=[jax.ShapeDtypeStruct((1, 1), F32), jax.ShapeDtypeStruct((S, D), F32)], compiler_params=_cp(),
    )(y, target)


def _strided(a, g):
    d = DIL_DILATIONS[g]
    S = a.shape[0]
    return a[:, g * 512:(g + 1) * 512].reshape(S // d, d * 512)


def _layer_fwd(x, w, tabs):
    mla_tab, dil_tab = tabs
    S = x.shape[0]
    h = _rms_in_fwd(x, w["norm_g"])
    zp = _mm(h, w["w_in"], mode="nn", name="in_proj")
    hs, y_lru = _lru_fwd(zp, w)
    q, k, v = _mla_pre_fwd(zp, w, mla_tab)
    o_mla, lse, y_mla = _mla_attn_fwd(q, k, v, zp)
    qd, kd, vd = _dil_pre_fwd(zp, w, dil_tab)
    og, lg, strided = [], [], []
    for g, d in enumerate(DIL_DILATIONS):
        if d == 1:
            qs, ks, vs, col0 = qd, kd, vd, 0
        else:
            qs, ks, vs, col0 = _strided(qd, g), _strided(kd, g), _strided(vd, g), 0
        o, l = _dil_attn_fwd(qs, ks, vs, col0, 4 * d, f"dil_attn_fwd{g}")
        strided.append((qs, ks, vs))
        og.append(o.reshape(S, 512))
        lg.append(l.reshape(S, 512))
    oc, L, y_dil = _dil_combine(og, lg, zp)
    P = [_mm(y_lru, w["w_lru_o"], mode="nn", name="lru_out"), _mm(y_mla, w["w_mla_o"], mode="nn", name="mla_out"),
         _mm(y_dil, w["w_dil_o"], mode="nn", name="dil_out")]
    merged = _merge_fwd(P, zp, w["b_merge"])
    x_out = _mm(merged, w["w_out"], mode="nn", name="out_proj", add=x)
    saved = dict(x=x, h=h, zp=zp, hs=hs, y=(y_lru, y_mla, y_dil), q=q, k=k, v=v, o_mla=o_mla, lse=lse, strided=strided, oc=oc, L=L, P=P,
                 merged=merged)
    return x_out, saved


def _layer_bwd(dout, w, tabs, sv):
    mla_tab, dil_tab = tabs
    zp = sv["zp"]
    S = zp.shape[0]
    g = {}
    dm = _mm(dout, w["w_out"], mode="nt", name="d_merged")
    g["w_out"] = _mm(sv["merged"], dout, mode="tn", name="dw_out")
    dz = jnp.zeros((S, ZW), BF16)
    dP, db = [], []
    for j in range(3):
        dz, dpj, dbj = _merge_bwd(dm, sv["P"][j], zp, w["b_merge"][:, j * D:(j + 1) * D], dz, j)
        dP.append(dpj)
        db.append(dbj)
    g["b_merge"] = jnp.concatenate(db, axis=1)
    names = ("w_lru_o", "w_mla_o", "w_dil_o")
    dy = []
    for j in range(3):
        dy.append(_mm(dP[j], w[names[j]], mode="nt", name="dy_" + names[j]))
        g[names[j]] = _mm(sv["y"][j], dP[j], mode="tn", name="d" + names[j])
    dz = _lru_gate_bwd(zp, sv["hs"], dy[0], dz)
    dz, g["conv_w"], g["conv_b"], g["w_gx"], g["b_gx"], g["w_ga"], g["b_ga"], g["lam"] = _lru_bwd(zp, sv["hs"], dy[0], w, dz)
    dz, do, Dr = _mla_post_bwd(zp, sv["o_mla"], dy[1], dz)
    dq, dk, dv = _mla_attn_bwd(sv["q"], sv["k"], sv["v"], do, sv["lse"], Dr)
    dz, g["w_uq"], g["w_uk"], g["w_uv"], g["g_cq"], g["g_ckv"], g["g_mq"], g["g_mk"] = _mla_pre_bwd(zp, dq, dk, dv, w, mla_tab, dz)
    dz, dod, Dd = _dil_comb_bwd(zp, sv["oc"], dy[2], dz)
    dqs, dks, dvs = [], [], []
    for gi, d in enumerate(DIL_DILATIONS):
        qs, ks, vs = sv["strided"][gi]
        col0 = 0
        shp = (S // d, d * 512)
        a, b, c = _dil_attn_bwd(qs, ks, vs, dod.reshape(shp), sv["L"].reshape(shp), Dd.reshape(shp), col0, 4 * d, f"dil_attn_bwd{gi}")
        dqs.append(a.reshape(S, 512))
        dks.append(b.reshape(S, 512))
        dvs.append(c.reshape(S, 512))
    dz, g["g_dq"] = _dil_pre_bwd(zp, jnp.concatenate(dqs, axis=1), w["g_dq"], dil_tab, dz, C_DQ, "dil_pre_bwd_q")
    dz, g["g_dk"] = _dil_pre_bwd(zp, jnp.concatenate(dks, axis=1), w["g_dk"], dil_tab, dz, C_DK, "dil_pre_bwd_k")
    dz = _cast_into(jnp.concatenate(dvs, axis=1), dz, C_DV, 1536, "dil_dv")
    dh = _mm(dz, w["w_in"], mode="nt", name="d_h")
    g["w_in"] = _mm(sv["h"], dz, mode="tn", name="dw_in")
    dx, g["norm_g"] = _rms_in_bwd(sv["x"], w["norm_g"], dh, dout)
    return dx, g


def _peers():
    mx, my, mc = lax.axis_index("x"), lax.axis_index("y"), lax.axis_index("c")
    me = 4 * mx + 2 * my + mc
    out = []
    for k in range(1, N_DEV):
        px = 1 - mx if k & 4 else mx
        py = 1 - my if k & 2 else my
        pc = 1 - mc if k & 1 else mc
        out.append(((px, py, pc), 4 * px + 2 * py + pc))
    return me, out


def _exchange(buf, name, gather):
    R = buf.shape[-2]

    def body(x_ref, o_ref, send_sems, recv_sems, local_sem):
        me, peers = _peers()
        mine = pltpu.make_async_copy(x_ref if gather else x_ref.at[me], o_ref.at[me], local_sem)
        mine.start()
        copies = []
        for k, (peer, pidx) in enumerate(peers):
            cp = pltpu.make_async_remote_copy(
                src_ref=x_ref if gather else x_ref.at[pidx], dst_ref=o_ref.at[me],
                send_sem=send_sems.at[k], recv_sem=recv_sems.at[k], device_id=peer, device_id_type=pl.DeviceIdType.MESH)
            cp.start()
            copies.append(cp)
        for cp in copies:
            cp.wait()
        mine.wait()

    return pl.pallas_call(
        body, name=name, out_shape=jax.ShapeDtypeStruct((N_DEV, R, LANE), buf.dtype),
        in_specs=[pl.BlockSpec(memory_space=pl.ANY)], out_specs=pl.BlockSpec(memory_space=pl.ANY),
        scratch_shapes=[pltpu.SemaphoreType.DMA((N_DEV - 1,)), pltpu.SemaphoreType.DMA((N_DEV - 1,)), pltpu.SemaphoreType.DMA(())],
        compiler_params=pltpu.CompilerParams(has_side_effects=True),
    )(buf)


PACK_ROWS = 512


def _sum8(buf, name):
    R = buf.shape[1]
    tr = PACK_ROWS

    def body(b_ref, o_ref):
        acc = b_ref[0]
        for s in range(1, N_DEV):
            acc = acc + b_ref[s]
        o_ref[...] = acc

    return pl.pallas_call(
        body, name=name, grid=(R // tr,), in_specs=[pl.BlockSpec((N_DEV, tr, LANE), lambda i: (0, i, 0))],
        out_specs=pl.BlockSpec((tr, LANE), lambda i: (i, 0)), out_shape=jax.ShapeDtypeStruct((R, LANE), F32), compiler_params=_cp(),
    )(buf)


def _pack(arrs, dtype, lead):
    flat = [a.astype(dtype).reshape(a.shape[:lead] + (-1,)) for a in arrs]
    cat = jnp.concatenate(flat, axis=-1)
    n = cat.shape[-1]
    unit = PACK_ROWS * LANE
    pad = (-n) % unit
    if pad:
        cat = jnp.pad(cat, [(0, 0)] * lead + [(0, pad)])
    return cat.reshape(cat.shape[:lead] + ((n + pad) // LANE, LANE))


def _unpack(buf, shapes, lead):
    flat = buf.reshape(buf.shape[:lead] + (-1,))
    out, off = [], 0
    for shp in shapes:
        n = int(np.prod(shp))
        out.append(flat[..., off:off + n].reshape(buf.shape[:lead] + tuple(shp)))
        off += n
    return out


def _to_dest(a, axis):
    shp = a.shape
    a = a.reshape(shp[:axis] + (N_DEV, shp[axis] // N_DEV) + shp[axis + 1:])
    return jnp.moveaxis(a, axis, 0)


def _from_src(a, axis):
    a = jnp.moveaxis(a, 0, axis)
    shp = a.shape
    return a.reshape(shp[:axis] + (shp[axis] * shp[axis + 1],) + shp[axis + 2:])


def _adamw(w, g, m, v, name):
    rows, cols = w.shape
    tr = rows
    while tr * cols * 4 > (3 << 19) and tr % 16 == 0:
        tr //= 2
    c1 = 1.0 - ADAM_B1 ** ADAM_STEP
    c2 = 1.0 - ADAM_B2 ** ADAM_STEP

    def body(w_ref, g_ref, m_ref, v_ref, d_ref, mo_ref, vo_ref):
        gv = g_ref[...]
        mn = ADAM_B1 * m_ref[...] + (1.0 - ADAM_B1) * gv
        vn = ADAM_B2 * v_ref[...] + (1.0 - ADAM_B2) * (gv * gv)
        mo_ref[...] = mn
        vo_ref[...] = vn
        d_ref[...] = -ADAM_LR * ((mn / c1) / (jnp.sqrt(vn / c2) + ADAM_EPS) + ADAM_WD * w_ref[...])

    spec = pl.BlockSpec((tr, cols), lambda i: (i, 0))
    return pl.pallas_call(
        body, name=name, grid=(rows // tr,), in_specs=[spec] * 4, out_specs=[spec] * 3,
        out_shape=[jax.ShapeDtypeStruct((rows, cols), F32)] * 3, compiler_params=_cp(),
    )(w, g, m, v)


IN_NAMES = ['x', 'positions', 'norm_g', 'w_in', 'conv_w', 'conv_b', 'w_gate_x', 'b_gate_x', 'w_gate_a', 'b_gate_a', 'lru_lambda', 'w_lru_o',
            'cq_norm_g', 'ckv_norm_g', 'w_uq', 'w_ukv', 'mla_q_norm_g', 'mla_k_norm_g', 'w_mla_o', 'dil_q_norm_g', 'dil_k_norm_g', 'w_dil_o',
            'b_merge', 'w_out']
WEIGHTS = IN_NAMES[2:]
SHARDED = {'w_in': 2, 'conv_w': 2, 'w_lru_o': 1, 'w_uq': 2, 'w_ukv': 2, 'w_mla_o': 2, 'w_dil_o': 2, 'w_out': 1}
MATMUL_SHARDED = [n for n in WEIGHTS if n in SHARDED and n != 'conv_w']
REPLICATED = [n for n in WEIGHTS if n not in SHARDED]
FULL_SHAPES = {'w_in': (DEPTH, D, IN_WIDTH), 'conv_w': (DEPTH, 4, D), 'w_lru_o': (DEPTH, D, D), 'w_uq': (DEPTH, 256, 768),
               'w_ukv': (DEPTH, 128, 1024), 'w_mla_o': (DEPTH, 512, D), 'w_dil_o': (DEPTH, 512, D), 'w_out': (DEPTH, D, D)}

_KR0 = C_KR * LANE


def _layer_weights(full, rep, l):
    w_in = full['w_in'][l]
    zc = lambda n: jnp.zeros((D, n), BF16)
    w_uq = jnp.pad(full['w_uq'][l].reshape(256, 8, MLA_QK), ((0, 0), (0, 0), (0, LANE - MLA_QK))).reshape(256, 1024)
    ukv = full['w_ukv'][l].reshape(128, 8, 128)
    g96 = lambda a: jnp.pad(a[l].reshape(1, MLA_QK), ((0, 0), (0, LANE - MLA_QK)))
    g64 = lambda a: jnp.tile(a[l].reshape(1, DIL_HD), (1, 2))
    return dict(
        norm_g=rep['norm_g'][l].reshape(1, D),
        w_in=jnp.concatenate([w_in[:, :_KR0], zc(KR_LANE), w_in[:, _KR0:_KR0 + 32], zc(32), w_in[:, _KR0 + 32:]], axis=1),
        conv_w=full['conv_w'][l], conv_b=rep['conv_b'][l].reshape(1, D),
        w_gx=rep['w_gate_x'][l].astype(BF16), b_gx=rep['b_gate_x'][l].reshape(8, 1, LANE),
        w_ga=rep['w_gate_a'][l].astype(BF16), b_ga=rep['b_gate_a'][l].reshape(8, 1, LANE),
        lam=rep['lru_lambda'][l].reshape(1, D),
        w_lru_o=full['w_lru_o'][l], w_mla_o=full['w_mla_o'][l], w_dil_o=full['w_dil_o'][l], w_out=full['w_out'][l],
        g_cq=rep['cq_norm_g'][l].reshape(1, 256), g_ckv=rep['ckv_norm_g'][l].reshape(1, 128),
        w_uq=w_uq, w_uk=jnp.pad(ukv[:, :, :64], ((0, 0), (0, 0), (0, 64))).reshape(128, 1024), w_uv=ukv[:, :, 64:].reshape(128, 512),
        g_mq=g96(rep['mla_q_norm_g']), g_mk=g96(rep['mla_k_norm_g']), g_dq=g64(rep['dil_q_norm_g']), g_dk=g64(rep['dil_k_norm_g']),
        b_merge=rep['b_merge'][l].reshape(1, 3 * D),
    )


def _natural_grads(g):
    gi = g['w_in']
    uk = g['w_uk'].reshape(128, 8, 128)[:, :, :64]
    uv = g['w_uv'].reshape(128, 8, 64)
    return {
        'norm_g': g['norm_g'].reshape(D),
        'w_in': jnp.concatenate([gi[:, :_KR0], gi[:, _KR0 + KR_LANE:_KR0 + KR_LANE + 32], gi[:, _KR0 + LANE:]], axis=1),
        'conv_w': g['conv_w'], 'conv_b': g['conv_b'].reshape(D),
        'w_gate_x': g['w_gx'], 'b_gate_x': g['b_gx'].reshape(8, LANE), 'w_gate_a': g['w_ga'], 'b_gate_a': g['b_ga'].reshape(8, LANE),
        'lru_lambda': g['lam'].reshape(D), 'w_lru_o': g['w_lru_o'],
        'cq_norm_g': g['g_cq'].reshape(256), 'ckv_norm_g': g['g_ckv'].reshape(128),
        'w_uq': g['w_uq'].reshape(256, 8, LANE)[:, :, :MLA_QK].reshape(256, 768),
        'w_ukv': jnp.concatenate([uk, uv], axis=-1).reshape(128, 1024),
        'mla_q_norm_g': g['g_mq'][0, :MLA_QK], 'mla_k_norm_g': g['g_mk'][0, :MLA_QK], 'w_mla_o': g['w_mla_o'],
        'dil_q_norm_g': g['g_dq'][0, :DIL_HD] + g['g_dq'][0, DIL_HD:], 'dil_k_norm_g': g['g_dk'][0, :DIL_HD] + g['g_dk'][0, DIL_HD:],
        'w_dil_o': g['w_dil_o'], 'b_merge': g['b_merge'].reshape(3 * D), 'w_out': g['w_out'],
    }


def _device_step(x, pos, target, full, rep):
    tabs = _rope_tables(pos)
    ws = [_layer_weights(full, rep, l) for l in range(DEPTH)]
    saved = []
    for l in range(DEPTH):
        x, sv = _layer_fwd(x, ws[l], tabs)
        saved.append(sv)
    loss, dx = _loss_fwd_bwd(x, target)
    grads = [None] * DEPTH
    for l in reversed(range(DEPTH)):
        dx, g = _layer_bwd(dx, ws[l], tabs, saved[l])
        grads[l] = _natural_grads(g)
    return loss[0, 0], dx, {n: jnp.stack([grads[l][n] for l in range(DEPTH)]) for n in WEIGHTS}


def kernel(x, positions, norm_g, w_in, conv_w, conv_b, w_gate_x, b_gate_x, w_gate_a, b_gate_a, lru_lambda, w_lru_o, cq_norm_g, ckv_norm_g, w_uq, w_ukv, mla_q_norm_g, mla_k_norm_g, w_mla_o, dil_q_norm_g, dil_k_norm_g, w_dil_o, b_merge, w_out, loss_target, m_norm_g, m_w_in, m_conv_w, m_conv_b, m_w_gate_x, m_b_gate_x, m_w_gate_a, m_b_gate_a, m_lru_lambda, m_w_lru_o, m_cq_norm_g, m_ckv_norm_g, m_w_uq, m_w_ukv, m_mla_q_norm_g, m_mla_k_norm_g, m_w_mla_o, m_dil_q_norm_g, m_dil_k_norm_g, m_w_dil_o, m_b_merge, m_w_out, v_norm_g, v_w_in, v_conv_w, v_conv_b, v_w_gate_x, v_b_gate_x, v_w_gate_a, v_b_gate_a, v_lru_lambda, v_w_lru_o, v_cq_norm_g, v_ckv_norm_g, v_w_uq, v_w_ukv, v_mla_q_norm_g, v_mla_k_norm_g, v_w_mla_o, v_dil_q_norm_g, v_dil_k_norm_g, v_w_dil_o, v_b_merge, v_w_out):
    args = (x, positions, norm_g, w_in, conv_w, conv_b, w_gate_x, b_gate_x, w_gate_a, b_gate_a, lru_lambda, w_lru_o, cq_norm_g, ckv_norm_g, w_uq, w_ukv, mla_q_norm_g, mla_k_norm_g, w_mla_o, dil_q_norm_g, dil_k_norm_g, w_dil_o, b_merge, w_out)
    moments_m = (m_norm_g, m_w_in, m_conv_w, m_conv_b, m_w_gate_x, m_b_gate_x, m_w_gate_a, m_b_gate_a, m_lru_lambda, m_w_lru_o, m_cq_norm_g, m_ckv_norm_g, m_w_uq, m_w_ukv, m_mla_q_norm_g, m_mla_k_norm_g, m_w_mla_o, m_dil_q_norm_g, m_dil_k_norm_g, m_w_dil_o, m_b_merge, m_w_out)
    moments_v = (v_norm_g, v_w_in, v_conv_w, v_conv_b, v_w_gate_x, v_b_gate_x, v_w_gate_a, v_b_gate_a, v_lru_lambda, v_w_lru_o, v_cq_norm_g, v_ckv_norm_g, v_w_uq, v_w_ukv, v_mla_q_norm_g, v_mla_k_norm_g, v_w_mla_o, v_dil_q_norm_g, v_dil_k_norm_g, v_w_dil_o, v_b_merge, v_w_out)
    a = dict(zip(IN_NAMES, args))
    wd = {n: a[n] for n in WEIGHTS}
    md = dict(zip(WEIGHTS, moments_m))
    vd = dict(zip(WEIGHTS, moments_v))

    shard_shapes = [wd[n].shape for n in MATMUL_SHARDED]
    gathered = _exchange(_pack([wd[n] for n in MATMUL_SHARDED], BF16, 0), "gather_weights", True)
    full = {n: _from_src(p, SHARDED[n]) for n, p in zip(MATMUL_SHARDED, _unpack(gathered, shard_shapes, 1))}
    cw = _exchange(_pack([wd['conv_w']], F32, 0), "gather_conv_w", True)
    full['conv_w'] = _from_src(_unpack(cw, [wd['conv_w'].shape], 1)[0], SHARDED['conv_w'])

    loss, grad_x, grads = _device_step(x[0], positions[0], loss_target[0], full, wd)

    sharded = [n for n in WEIGHTS if n in SHARDED]
    sent = _pack([_to_dest(grads[n], SHARDED[n]) for n in sharded], F32, 1)
    got = _sum8(_exchange(sent, "scatter_grads", False), "sum_sharded")
    gsh = dict(zip(sharded, _unpack(got, [wd[n].shape for n in sharded], 0)))
    rshapes = [wd[n].shape for n in REPLICATED]
    grep = _sum8(_exchange(_pack([grads[n] for n in REPLICATED], F32, 0), "gather_grads", True), "sum_replicated")

    out_g, out_d, out_m, out_v = {}, {}, {}, {}
    d_, m_, v_ = _adamw(_pack([wd[n] for n in REPLICATED], F32, 0), grep, _pack([md[n] for n in REPLICATED], F32, 0),
                        _pack([vd[n] for n in REPLICATED], F32, 0), "adamw_replicated")
    for dst, buf in ((out_g, grep), (out_d, d_), (out_m, m_), (out_v, v_)):
        dst.update(zip(REPLICATED, _unpack(buf, rshapes, 0)))
    for n in sharded:
        shp = wd[n].shape
        two = (shp[0] * shp[1], shp[2])
        d_, m_, v_ = _adamw(wd[n].reshape(two), gsh[n].reshape(two), md[n].reshape(two), vd[n].reshape(two), "adamw_" + n)
        out_g[n], out_d[n], out_m[n], out_v[n] = gsh[n], d_.reshape(shp), m_.reshape(shp), v_.reshape(shp)

    loss = lax.psum(loss, ("x", "y", "c"))
    return (loss, grad_x[None], *[out_g[n] for n in WEIGHTS], *[out_d[n] for n in WEIGHTS], *[out_m[n] for n in WEIGHTS],
            *[out_v[n] for n in WEIGHTS])
```

```python
import functools

import numpy as np
import jax
import jax.numpy as jnp
from jax import lax
from jax.experimental import pallas as pl
from jax.experimental.pallas import tpu as pltpu

F32 = jnp.float32
BF16 = jnp.bfloat16

N_DEV = 8
D = 1024
DEPTH = 2
EPS = 1e-6
ROPE_THETA = 10000.0
LRU_C = 8.0
LANE = 128
SUB = 8
IN_WIDTH = 11168
SHARD_IN = IN_WIDTH // N_DEV

C_LRUX, C_LRUG, C_CQ, C_CKV, C_KR, C_MLAG, C_DQ, C_DK, C_DV, C_DILG, C_MERGE = 0, 8, 16, 18, 19, 20, 24, 36, 48, 60, 64
ZW = 88 * LANE
KR_LANE = 64

MLA_QK = 96
MLA_SCALE = MLA_QK ** -0.5
DIL_HD = 64
DIL_SCALE = DIL_HD ** -0.5
DIL_DILATIONS = (1, 4, 16)
NK = 128

ADAM_LR, ADAM_B1, ADAM_B2, ADAM_EPS, ADAM_WD, ADAM_STEP = 0.001, 0.9, 0.999, 1e-08, 0.01, 10

NEG = -1e30
VMEM_LIMIT = 48 * 1024 * 1024


def _cp(**kw):
    return pltpu.CompilerParams(vmem_limit_bytes=VMEM_LIMIT, **kw)


def _sig(x):
    return 1.0 / (1.0 + jnp.exp(-x))


def _silu(x):
    return x * _sig(x)


def _dsilu(x):
    s = _sig(x)
    return s * (1.0 + x * (1.0 - s))


def _dot(a, b, dims):
    return lax.dot_general(a, b, (dims, ((), ())), preferred_element_type=F32)


def _nn(a, b):
    return _dot(a, b, ((1,), (0,)))


def _nt(a, b):
    return _dot(a, b, ((1,), (1,)))


def _tn(a, b):
    return _dot(a, b, ((0,), (0,)))


def _rsum(x):
    return jnp.sum(x, axis=-1, keepdims=True)


def _csum(x):
    return jnp.sum(x, axis=0, keepdims=True)


def _mm(a, b, *, mode, name, out_dtype=F32, add=None, tm=1024, tn=1024, tk=1024):
    if mode == "nn":
        (M, K), (K2, N) = a.shape, b.shape
    elif mode == "nt":
        (M, K), (N, K2) = a.shape, b.shape
    else:
        (K, M), (K2, N) = a.shape, b.shape
    assert K == K2
    tm, tn, tk = min(tm, M), min(tn, N), min(tk, K)
    assert M % tm == 0 and N % tn == 0 and K % tk == 0
    nk = K // tk
    fn = {"nn": _nn, "nt": _nt, "tn": _tn}[mode]
    has_add = add is not None

    def body(*refs):
        a_ref, b_ref = refs[0], refs[1]
        add_ref = refs[2] if has_add else None
        o_ref = refs[3] if has_add else refs[2]
        part = fn(a_ref[...].astype(BF16), b_ref[...].astype(BF16))

        def fin(acc):
            if has_add:
                acc = acc + add_ref[...]
            o_ref[...] = acc.astype(out_dtype)

        if nk == 1:
            fin(part)
        else:
            acc_ref = refs[-1]
            k = pl.program_id(2)

            @pl.when(k == 0)
            def _():
                acc_ref[...] = part

            @pl.when(k > 0)
            def _():
                acc_ref[...] += part

            @pl.when(k == nk - 1)
            def _():
                fin(acc_ref[...])

    a_spec = pl.BlockSpec((tk, tm), lambda i, j, k: (k, i)) if mode == "tn" else pl.BlockSpec((tm, tk), lambda i, j, k: (i, k))
    b_spec = pl.BlockSpec((tn, tk), lambda i, j, k: (j, k)) if mode == "nt" else pl.BlockSpec((tk, tn), lambda i, j, k: (k, j))
    o_spec = pl.BlockSpec((tm, tn), lambda i, j, k: (i, j))
    in_specs, args = [a_spec, b_spec], [a, b]
    if has_add:
        in_specs.append(o_spec)
        args.append(add)
    return pl.pallas_call(
        body, name=name, grid=(M // tm, N // tn, nk), in_specs=in_specs, out_specs=o_spec,
        out_shape=jax.ShapeDtypeStruct((M, N), out_dtype),
        scratch_shapes=[pltpu.VMEM((tm, tn), F32)] if nk > 1 else [],
        compiler_params=_cp(dimension_semantics=("parallel", "parallel", "arbitrary")),
    )(*args)


T_ROW = 512


def _rms_in_fwd(x, g):
    S = x.shape[0]
    T = T_ROW

    def body(x_ref, g_ref, h_ref):
        xv = x_ref[...]
        r = lax.rsqrt(jnp.mean(xv * xv, axis=-1, keepdims=True) + EPS)
        h_ref[...] = (xv * r * g_ref[...]).astype(BF16)

    return pl.pallas_call(
        body, name="rms_in_fwd", grid=(S // T,),
        in_specs=[pl.BlockSpec((T, D), lambda i: (i, 0)), pl.BlockSpec((1, D), lambda i: (0, 0))],
        out_specs=pl.BlockSpec((T, D), lambda i: (i, 0)),
        out_shape=jax.ShapeDtypeStruct((S, D), BF16), compiler_params=_cp(),
    )(x, g)


def _rms_in_bwd(x, g, dh, dres):
    S = x.shape[0]
    T = T_ROW

    def body(x_ref, g_ref, dh_ref, dr_ref, dx_ref, dg_ref):
        i = pl.program_id(0)
        xv = x_ref[...]
        r = lax.rsqrt(jnp.mean(xv * xv, axis=-1, keepdims=True) + EPS)
        xn = xv * r
        dy = dh_ref[...]
        part = _csum(dy * xn)

        @pl.when(i == 0)
        def _():
            dg_ref[...] = part

        @pl.when(i > 0)
        def _():
            dg_ref[...] += part

        dxh = dy * g_ref[...]
        dx_ref[...] = dr_ref[...] + r * (dxh - xn * jnp.mean(dxh * xn, axis=-1, keepdims=True))

    row = pl.BlockSpec((T, D), lambda i: (i, 0))
    vec = pl.BlockSpec((1, D), lambda i: (0, 0))
    return pl.pallas_call(
        body, name="rms_in_bwd", grid=(S // T,), in_specs=[row, vec, row, row], out_specs=[row, vec],
        out_shape=[jax.ShapeDtypeStruct((S, D), F32), jax.ShapeDtypeStruct((1, D), F32)], compiler_params=_cp(),
    )(x, g, dh, dres)


T_LRU = 512


def _neg_expm1(y):
    ser = -y * (1.0 + y * 0.5 * (1.0 + y * (1.0 / 3.0) * (1.0 + y * 0.25 * (1.0 + y * 0.2))))
    return jnp.where(y > -0.03, ser, 1.0 - jnp.exp(y))


def _softplus_neg(lam):
    e = jnp.exp(-jnp.abs(lam))
    l1p = jnp.where(e < 0.01, e * (1.0 - e * (0.5 - e * (1.0 / 3.0 - e * 0.25))), jnp.log(1.0 + e))
    return jnp.maximum(-lam, 0.0) + l1p


def _scan_fwd(a, b, T):
    row = lax.broadcasted_iota(jnp.int32, a.shape, 0)
    d = 1
    while d < T:
        m = row >= d
        b = jnp.where(m, a * pltpu.roll(b, d, 0) + b, b)
        a = jnp.where(m, a * pltpu.roll(a, d, 0), a)
        d *= 2
    return a, b


def _scan_bwd(a, b, T):
    row = lax.broadcasted_iota(jnp.int32, a.shape, 0)
    d = 1
    while d < T:
        m = row < T - d
        b = jnp.where(m, a * pltpu.roll(b, T - d, 0) + b, b)
        a = jnp.where(m, a * pltpu.roll(a, T - d, 0), a)
        d *= 2
    return b


def _lru_common(x, prev, first, cw_ref, cb_ref, wgx_ref, bgx_ref, wga_ref, bga_ref, lam_ref, T):
    row = lax.broadcasted_iota(jnp.int32, x.shape, 0)
    prev = jnp.where(first, 0.0, prev)
    xs = []
    for j in (3, 2, 1):
        pv = jnp.tile(pltpu.roll(prev, j, 0), (T // SUB, 1))
        xs.append(jnp.where(row < j, pv, pltpu.roll(x, j, 0)))
    xs.append(x)
    xc = cb_ref[...] + cw_ref[0:1, :] * xs[0] + cw_ref[1:2, :] * xs[1] + cw_ref[2:3, :] * xs[2] + cw_ref[3:4, :] * xs[3]
    xcb = xc.astype(BF16)
    gx = _sig(_nn(xcb, wgx_ref[0]) + bgx_ref[0])
    ga = _sig(_nn(xcb, wga_ref[0]) + bga_ref[0])
    sp = _softplus_neg(lam_ref[...])
    log_a = -LRU_C * ga * sp
    a = jnp.exp(log_a)
    mult = jnp.sqrt(_neg_expm1(2.0 * log_a))
    return xs, xc, xcb, gx, ga, sp, a, mult


def _lru_specs(T, tmap):
    def at(col0):
        return pl.BlockSpec((T, LANE), lambda n, i: (tmap(i), col0 + n))

    def prev(col0):
        return pl.BlockSpec((SUB, LANE), lambda n, i: (jnp.maximum(tmap(i) * (T // SUB) - 1, 0), col0 + n))

    small = [
        pl.BlockSpec((4, LANE), lambda n, i: (0, n)),
        pl.BlockSpec((1, LANE), lambda n, i: (0, n)),
        pl.BlockSpec((1, LANE, LANE), lambda n, i: (n, 0, 0)),
        pl.BlockSpec((1, 1, LANE), lambda n, i: (n, 0, 0)),
        pl.BlockSpec((1, LANE, LANE), lambda n, i: (n, 0, 0)),
        pl.BlockSpec((1, 1, LANE), lambda n, i: (n, 0, 0)),
        pl.BlockSpec((1, LANE), lambda n, i: (0, n)),
    ]
    return at, prev, small


def _lru_fwd(zp, w):
    S = zp.shape[0]
    T = T_LRU
    at, prev, small = _lru_specs(T, lambda i: i)

    def body(x_ref, xp_ref, g_ref, cw_ref, cb_ref, wgx_ref, bgx_ref, wga_ref, bga_ref, lam_ref, hs_ref, y_ref, carry_ref):
        i = pl.program_id(1)

        @pl.when(i == 0)
        def _():
            carry_ref[...] = jnp.zeros_like(carry_ref)

        x = x_ref[...]
        _, xc, _, gx, _, _, a, mult = _lru_common(x, xp_ref[...], i == 0, cw_ref, cb_ref, wgx_ref, bgx_ref, wga_ref, bga_ref, lam_ref, T)
        A, B = _scan_fwd(a, mult * gx * xc, T)
        h = B + A * carry_ref[SUB - 1:SUB, :]
        hs_ref[...] = h
        carry_ref[...] = hs_ref[T - SUB:T, :]
        y_ref[...] = (h * _silu(g_ref[...])).astype(BF16)

    out = pl.BlockSpec((T, LANE), lambda n, i: (i, n))
    return pl.pallas_call(
        body, name="lru_fwd", grid=(8, S // T),
        in_specs=[at(C_LRUX), prev(C_LRUX), at(C_LRUG)] + small, out_specs=[out, out],
        out_shape=[jax.ShapeDtypeStruct((S, D), F32), jax.ShapeDtypeStruct((S, D), BF16)],
        scratch_shapes=[pltpu.VMEM((SUB, LANE), F32)],
        compiler_params=_cp(dimension_semantics=("parallel", "arbitrary")),
    )(zp, zp, zp, w["conv_w"], w["conv_b"], w["w_gx"], w["b_gx"], w["w_ga"], w["b_ga"], w["lam"])


def _lru_bwd(zp, hs, dy, w, dz):
    S = zp.shape[0]
    T = T_LRU
    nT = S // T
    at, prev, small = _lru_specs(T, lambda i: nT - 1 - i)

    def body(x_ref, xp_ref, g_ref, h_ref, hp_ref, dy_ref, cw_ref, cb_ref, wgx_ref, bgx_ref, wga_ref, bga_ref, lam_ref, dz_in,
             dzx_ref, dcw_ref, dcb_ref, dwgx_ref, dbgx_ref, dwga_ref, dbga_ref, dlam_ref, carry_ref, head_ref):
        del dz_in
        j = pl.program_id(1)
        it = nT - 1 - j

        @pl.when(j == 0)
        def _():
            for r in (carry_ref, head_ref, dcw_ref, dcb_ref, dwgx_ref, dbgx_ref, dwga_ref, dbga_ref, dlam_ref):
                r[...] = jnp.zeros_like(r)

        first = it == 0
        x = x_ref[...]
        xs, xc, xcb, gx, ga, sp, a, mult = _lru_common(x, xp_ref[...], first, cw_ref, cb_ref, wgx_ref, bgx_ref, wga_ref, bga_ref, lam_ref, T)
        row = lax.broadcasted_iota(jnp.int32, x.shape, 0)
        u = gx * xc
        h = h_ref[...]
        hp = jnp.where(first, 0.0, hp_ref[...])
        hm1 = jnp.where(row < 1, jnp.tile(pltpu.roll(hp, 1, 0), (T // SUB, 1)), pltpu.roll(h, 1, 0))
        dho = dy_ref[...] * _silu(g_ref[...])
        gin = jnp.where(row == T - 1, dho + carry_ref[0:1, :], dho)
        abar = jnp.where(row == T - 1, 0.0, pltpu.roll(a, T - 1, 0))
        dh = _scan_bwd(abar, gin, T)
        carry_ref[...] = (a * dh)[0:SUB, :]
        da = dh * hm1
        dmult = dh * u
        du = dh * mult
        dgx = du * xc
        dxc = du * gx
        dlog_a = da * a - dmult * a * a / mult
        dga = dlog_a * (-LRU_C * sp)
        lam = lam_ref[...]
        dlam_ref[...] += _csum(dlog_a * (-LRU_C * ga)) * (-1.0 / (1.0 + jnp.exp(lam)))
        dpa = dga * ga * (1.0 - ga)
        dpx = dgx * gx * (1.0 - gx)
        dpab, dpxb = dpa.astype(BF16), dpx.astype(BF16)
        dxc = dxc + _nt(dpxb, wgx_ref[0]) + _nt(dpab, wga_ref[0])
        dwgx_ref[0] += _tn(xcb, dpxb)
        dwga_ref[0] += _tn(xcb, dpab)
        dbgx_ref[0] += _csum(dpx)
        dbga_ref[0] += _csum(dpa)
        dcb_ref[...] += _csum(dxc)
        for k in range(4):
            dcw_ref[k:k + 1, :] += _csum(dxc * xs[k])
        head = head_ref[...]
        dx = cw_ref[3:4, :] * dxc
        for jj in (1, 2, 3):
            hv = jnp.tile(pltpu.roll(head, SUB - jj, 0), (T // SUB, 1))
            dx = dx + cw_ref[3 - jj:4 - jj, :] * jnp.where(row >= T - jj, hv, pltpu.roll(dxc, T - jj, 0))
        head_ref[...] = dxc[0:SUB, :]
        dzx_ref[...] = dx.astype(BF16)

    def acc(shape, imap):
        return pl.BlockSpec(shape, imap)

    out_specs = [
        pl.BlockSpec((T, LANE), lambda n, i: (nT - 1 - i, C_LRUX + n)),
        acc((4, LANE), lambda n, i: (0, n)), acc((1, LANE), lambda n, i: (0, n)),
        acc((1, LANE, LANE), lambda n, i: (n, 0, 0)), acc((1, 1, LANE), lambda n, i: (n, 0, 0)),
        acc((1, LANE, LANE), lambda n, i: (n, 0, 0)), acc((1, 1, LANE), lambda n, i: (n, 0, 0)),
        acc((1, LANE), lambda n, i: (0, n)),
    ]
    out_shape = [
        jax.ShapeDtypeStruct(dz.shape, BF16),
        jax.ShapeDtypeStruct((4, D), F32), jax.ShapeDtypeStruct((1, D), F32),
        jax.ShapeDtypeStruct((8, LANE, LANE), F32), jax.ShapeDtypeStruct((8, 1, LANE), F32),
        jax.ShapeDtypeStruct((8, LANE, LANE), F32), jax.ShapeDtypeStruct((8, 1, LANE), F32),
        jax.ShapeDtypeStruct((1, D), F32),
    ]
    dyspec = pl.BlockSpec((T, LANE), lambda n, i: (nT - 1 - i, n))
    hprev = pl.BlockSpec((SUB, LANE), lambda n, i: (jnp.maximum((nT - 1 - i) * (T // SUB) - 1, 0), n))
    return pl.pallas_call(
        body, name="lru_bwd", grid=(8, nT),
        in_specs=[at(C_LRUX), prev(C_LRUX), at(C_LRUG), dyspec, hprev, dyspec] + small + [pl.BlockSpec(memory_space=pl.ANY)],
        out_specs=out_specs, out_shape=out_shape,
        scratch_shapes=[pltpu.VMEM((SUB, LANE), F32), pltpu.VMEM((SUB, LANE), F32)],
        input_output_aliases={13: 0},
        compiler_params=_cp(dimension_semantics=("parallel", "arbitrary")),
    )(zp, zp, zp, hs, hs, dy, w["conv_w"], w["conv_b"], w["w_gx"], w["b_gx"], w["w_ga"], w["b_ga"], w["lam"], dz)


def _lru_gate_bwd(zp, hs, dy, dz):
    S = zp.shape[0]
    T = T_ROW

    def body(g_ref, h_ref, dy_ref, dz_in, o_ref):
        del dz_in
        o_ref[...] = (dy_ref[...] * h_ref[...] * _dsilu(g_ref[...])).astype(BF16)

    row = pl.BlockSpec((T, D), lambda i: (i, 0))
    zc = pl.BlockSpec((T, D), lambda i: (i, C_LRUG // 8))
    return pl.pallas_call(
        body, name="lru_gate_bwd", grid=(S // T,), in_specs=[zc, row, row, pl.BlockSpec(memory_space=pl.ANY)], out_specs=zc,
        out_shape=jax.ShapeDtypeStruct(dz.shape, BF16), input_output_aliases={3: 0}, compiler_params=_cp(),
    )(zp, hs, dy, dz)


def _rope_tables(pos):
    pf = pos.astype(F32)[:, None]

    def cs(d):
        inv = ROPE_THETA ** (-jnp.arange(0, d, 2, dtype=F32) / d)
        ang = pf * inv
        return jnp.cos(ang), jnp.sin(ang)

    S = pos.shape[0]
    c, s = cs(32)
    one, zero = jnp.ones((S, 64), F32), jnp.zeros((S, 16), F32)
    z32, z64 = jnp.zeros((S, 32), F32), jnp.zeros((S, 64), F32)
    mla = (jnp.concatenate([one, c, c, jnp.ones((S, 32), F32)], 1),
           jnp.concatenate([z64, zero, s, z32], 1),
           jnp.concatenate([z64, -s, zero, z32], 1))
    c, s = cs(64)
    dil = (jnp.concatenate([c, c, c, c], 1),
           jnp.concatenate([z32, s, z32, s], 1),
           jnp.concatenate([-s, z32, -s, z32], 1))
    return mla, dil


def _rope(x, C, S1, S2, sh):
    return x * C + pltpu.roll(x, sh, 1) * S1 + pltpu.roll(x, LANE - sh, 1) * S2


def _rope_t(dy, C, S1, S2, sh):
    return dy * C + pltpu.roll(dy * S1, LANE - sh, 1) + pltpu.roll(dy * S2, sh, 1)


def _lane(shape):
    return lax.broadcasted_iota(jnp.int32, shape, 1)


T_MLA = 256
TA = 512


def _zcol(T, width, col_lanes):
    assert (col_lanes * LANE) % width == 0
    return pl.BlockSpec((T, width), lambda i: (i, col_lanes * LANE // width))


def _full(shape):
    return pl.BlockSpec(shape, lambda *_: (0,) * len(shape))


def _mla_pre_fwd(zp, w, tab):
    S = zp.shape[0]
    T = T_MLA

    def body(cq_ref, ckv_ref, kr_ref, gcq_ref, gckv_ref, wuq_ref, wuk_ref, wuv_ref, gq_ref, gk_ref, C_ref, S1_ref, S2_ref,
             q_ref, k_ref, v_ref):
        cq = cq_ref[...]
        cqn = (cq * lax.rsqrt(jnp.mean(cq * cq, axis=-1, keepdims=True) + EPS) * gcq_ref[...]).astype(BF16)
        ckv = ckv_ref[...]
        ckvn = (ckv * lax.rsqrt(jnp.mean(ckv * ckv, axis=-1, keepdims=True) + EPS) * gckv_ref[...]).astype(BF16)
        q0 = _nn(cqn, wuq_ref[...])
        k0 = _nn(ckvn, wuk_ref[...])
        krb = kr_ref[...]
        C, S1, S2 = C_ref[...], S1_ref[...], S2_ref[...]
        for h in range(8):
            sl = slice(h * LANE, (h + 1) * LANE)
            xq = q0[:, sl]
            xq = xq * lax.rsqrt(_rsum(xq * xq) * (1.0 / MLA_QK) + EPS) * gq_ref[...]
            q_ref[:, sl] = _rope(xq, C, S1, S2, 16).astype(BF16)
            xk = k0[:, sl] + krb
            xk = xk * lax.rsqrt(_rsum(xk * xk) * (1.0 / MLA_QK) + EPS) * gk_ref[...]
            k_ref[:, sl] = _rope(xk, C, S1, S2, 16).astype(BF16)
        v_ref[...] = _nn(ckvn, wuv_ref[...]).astype(BF16)

    tabspec = pl.BlockSpec((T, LANE), lambda i: (i, 0))
    in_specs = [_zcol(T, 256, C_CQ), _zcol(T, LANE, C_CKV), _zcol(T, LANE, C_KR), _full((1, 256)), _full((1, LANE)),
                _full((256, 1024)), _full((LANE, 1024)), _full((LANE, 512)), _full((1, LANE)), _full((1, LANE)),
                tabspec, tabspec, tabspec]
    return pl.pallas_call(
        body, name="mla_pre_fwd", grid=(S // T,), in_specs=in_specs,
        out_specs=[pl.BlockSpec((T, 1024), lambda i: (i, 0)), pl.BlockSpec((T, 1024), lambda i: (i, 0)), pl.BlockSpec((T, 512), lambda i: (i, 0))],
        out_shape=[jax.ShapeDtypeStruct((S, 1024), BF16), jax.ShapeDtypeStruct((S, 1024), BF16), jax.ShapeDtypeStruct((S, 512), BF16)],
        compiler_params=_cp(),
    )(zp, zp, zp, w["g_cq"], w["g_ckv"], w["w_uq"], w["w_uk"], w["w_uv"], w["g_mq"], w["g_mk"], *tab)


def _mla_attn_fwd(q, k, v, zp):
    S = q.shape[0]
    nq = S // TA

    def body(q_ref, k_ref, v_ref, g_ref, o_ref, lse_ref, y_ref):
        qi = pl.program_id(1)
        lane = _lane((TA, LANE))
        rowi = lax.broadcasted_iota(jnp.int32, (TA, TA), 0)
        coli = lax.broadcasted_iota(jnp.int32, (TA, TA), 1)
        o_tot = jnp.zeros((TA, LANE), F32)
        for hh in range(2):
            cs = slice(hh * LANE, (hh + 1) * LANE)
            hm = (lane < 64) if hh == 0 else (lane >= 64)
            qh = q_ref[:, cs]

            def step(kb, carry, masked, cs=cs, hm=hm, qh=qh):
                m, l, acc = carry
                off = pl.multiple_of(kb * TA, TA)
                kh = k_ref[pl.ds(off, TA), cs]
                vv = v_ref[pl.ds(off, TA), :]
                vh = jnp.where(hm, vv, jnp.zeros_like(vv))
                s = _nt(qh, kh) * MLA_SCALE
                if masked:
                    s = jnp.where(rowi >= coli, s, NEG)
                m_new = jnp.maximum(m, jnp.max(s, axis=-1, keepdims=True))
                alpha = jnp.exp(m - m_new)
                p = jnp.exp(s - m_new)
                l = alpha * l + _rsum(p)
                acc = alpha * acc + _nn(p.astype(BF16), vh)
                return m_new, l, acc

            init = (jnp.full((TA, 1), NEG, F32), jnp.zeros((TA, 1), F32), jnp.zeros((TA, LANE), F32))
            carry = lax.fori_loop(0, qi, lambda kb, c: step(kb, c, False), init)
            m, l, acc = step(qi, carry, True)
            o_tot = o_tot + acc / l
            lse_ref[:, cs] = jnp.broadcast_to(m + jnp.log(l), (TA, LANE))
        o_ref[...] = o_tot
        y_ref[...] = (o_tot * _silu(g_ref[...])).astype(BF16)

    blk = pl.BlockSpec((TA, LANE), lambda p, i: (i, p))
    return pl.pallas_call(
        body, name="mla_attn_fwd", grid=(4, nq),
        in_specs=[pl.BlockSpec((TA, 256), lambda p, i: (i, p)), pl.BlockSpec((S, 256), lambda p, i: (0, p)),
                  pl.BlockSpec((S, LANE), lambda p, i: (0, p)), pl.BlockSpec((TA, LANE), lambda p, i: (i, C_MLAG + p))],
        out_specs=[blk, pl.BlockSpec((TA, 256), lambda p, i: (i, p)), blk],
        out_shape=[jax.ShapeDtypeStruct((S, 512), F32), jax.ShapeDtypeStruct((S, 1024), F32), jax.ShapeDtypeStruct((S, 512), BF16)],
        compiler_params=_cp(dimension_semantics=("parallel", "arbitrary")),
    )(q, k, v, zp)


def _mla_post_bwd(zp, o, dy, dz):
    S = zp.shape[0]
    T = T_ROW

    def body(g_ref, o_ref, dy_ref, dz_in, dz_ref, do_ref, D_ref):
        del dz_in
        g, o_, dy_ = g_ref[...], o_ref[...], dy_ref[...]
        do = dy_ * _silu(g)
        do_ref[...] = do.astype(BF16)
        dz_ref[...] = (dy_ * o_ * _dsilu(g)).astype(BF16)
        prod = do * o_
        lane = _lane((T, LANE))
        for p in range(4):
            pr = prod[:, p * LANE:(p + 1) * LANE]
            da = _rsum(jnp.where(lane < 64, pr, 0.0))
            db = _rsum(jnp.where(lane >= 64, pr, 0.0))
            D_ref[:, 2 * p * LANE:(2 * p + 1) * LANE] = jnp.broadcast_to(da, (T, LANE))
            D_ref[:, (2 * p + 1) * LANE:(2 * p + 2) * LANE] = jnp.broadcast_to(db, (T, LANE))

    row = pl.BlockSpec((T, 512), lambda i: (i, 0))
    zc = _zcol(T, 512, C_MLAG)
    return pl.pallas_call(
        body, name="mla_post_bwd", grid=(S // T,), in_specs=[zc, row, row, pl.BlockSpec(memory_space=pl.ANY)],
        out_specs=[zc, row, pl.BlockSpec((T, 1024), lambda i: (i, 0))],
        out_shape=[jax.ShapeDtypeStruct(dz.shape, BF16), jax.ShapeDtypeStruct((S, 512), BF16), jax.ShapeDtypeStruct((S, 1024), F32)],
        input_output_aliases={3: 0}, compiler_params=_cp(),
    )(zp, o, dy, dz)


def _mla_attn_bwd(q, k, v, do, lse, Dr):
    S = q.shape[0]
    nq = S // TA

    def body(q_ref, do_ref, lse_ref, D_ref, k_ref, v_ref, dq_ref, dk_ref, dv_ref):
        ki = pl.program_id(1)

        @pl.when(ki == 0)
        def _():
            dq_ref[...] = jnp.zeros_like(dq_ref)

        lane = _lane((TA, LANE))
        rowi = lax.broadcasted_iota(jnp.int32, (TA, TA), 0)
        coli = lax.broadcasted_iota(jnp.int32, (TA, TA), 1)
        dv_tot = jnp.zeros((TA, LANE), F32)
        for hh in range(2):
            cs = slice(hh * LANE, (hh + 1) * LANE)
            hm = (lane < 64) if hh == 0 else (lane >= 64)
            kh = k_ref[:, cs]
            vv = v_ref[...]
            vm = jnp.where(hm, vv, jnp.zeros_like(vv))

            def step(qb, carry, masked, cs=cs, kh=kh, vm=vm):
                dk_acc, dv_acc = carry
                off = pl.multiple_of(qb * TA, TA)
                qh = q_ref[pl.ds(off, TA), cs]
                doh = do_ref[pl.ds(off, TA), :]
                ls = jnp.tile(lse_ref[pl.ds(off, TA), cs], (1, TA // LANE))
                dd = jnp.tile(D_ref[pl.ds(off, TA), cs], (1, TA // LANE))
                s = _nt(qh, kh) * MLA_SCALE
                if masked:
                    s = jnp.where(rowi >= coli, s, NEG)
                p = jnp.exp(s - ls)
                dp = _nt(doh, vm)
                ds = (p * (dp - dd) * MLA_SCALE).astype(BF16)
                dv_acc = dv_acc + _tn(p.astype(BF16), doh)
                dk_acc = dk_acc + _tn(ds, qh)
                dq_ref[pl.ds(off, TA), cs] += _nn(ds, kh)
                return dk_acc, dv_acc

            z = jnp.zeros((TA, LANE), F32)
            carry = step(ki, (z, z), True)
            dk_acc, dv_acc = lax.fori_loop(ki + 1, nq, lambda qb, c: step(qb, c, False), carry)
            dk_ref[:, cs] = dk_acc
            dv_tot = dv_tot + jnp.where(hm, dv_acc, 0.0)
        dv_ref[...] = dv_tot

    pair = pl.BlockSpec((S, 256), lambda p, i: (0, p))
    return pl.pallas_call(
        body, name="mla_attn_bwd", grid=(4, nq),
        in_specs=[pair, pl.BlockSpec((S, LANE), lambda p, i: (0, p)), pair, pair,
                  pl.BlockSpec((TA, 256), lambda p, i: (i, p)), pl.BlockSpec((TA, LANE), lambda p, i: (i, p))],
        out_specs=[pair, pl.BlockSpec((TA, 256), lambda p, i: (i, p)), pl.BlockSpec((TA, LANE), lambda p, i: (i, p))],
        out_shape=[jax.ShapeDtypeStruct((S, 1024), F32), jax.ShapeDtypeStruct((S, 1024), F32), jax.ShapeDtypeStruct((S, 512), F32)],
        compiler_params=_cp(dimension_semantics=("parallel", "arbitrary")),
    )(q, do, lse, Dr, k, v)


def _mla_pre_bwd(zp, dq, dk, dv, w, tab, dz):
    S = zp.shape[0]
    T = T_MLA

    def body(cq_ref, ckv_ref, kr_ref, dq_ref, dk_ref, dv_ref, gcq_ref, gckv_ref, wuq_ref, wuk_ref, wuv_ref, gq_ref, gk_ref,
             C_ref, S1_ref, S2_ref, dz_in, dz_ref, dwuq_ref, dwuk_ref, dwuv_ref, dgcq_ref, dgckv_ref, dgq_ref, dgk_ref):
        del dz_in
        i = pl.program_id(0)

        @pl.when(i == 0)
        def _():
            for r in (dwuq_ref, dwuk_ref, dwuv_ref, dgcq_ref, dgckv_ref, dgq_ref, dgk_ref):
                r[...] = jnp.zeros_like(r)

        cq = cq_ref[...]
        rq = lax.rsqrt(jnp.mean(cq * cq, axis=-1, keepdims=True) + EPS)
        cqh = cq * rq
        cqn = (cqh * gcq_ref[...]).astype(BF16)
        ckv = ckv_ref[...]
        rkv = lax.rsqrt(jnp.mean(ckv * ckv, axis=-1, keepdims=True) + EPS)
        ckvh = ckv * rkv
        ckvn = (ckvh * gckv_ref[...]).astype(BF16)
        q0 = _nn(cqn, wuq_ref[...])
        k0 = _nn(ckvn, wuk_ref[...])
        krb = kr_ref[...]
        C, S1, S2 = C_ref[...], S1_ref[...], S2_ref[...]
        gq, gk = gq_ref[...], gk_ref[...]

        def head_bwd(x, dy, g):
            r = lax.rsqrt(_rsum(x * x) * (1.0 / MLA_QK) + EPS)
            xn = x * r
            dyn = _rope_t(dy, C, S1, S2, 16)
            dxh = dyn * g
            return r * (dxh - xn * _rsum(dxh * xn) * (1.0 / MLA_QK)), _csum(dyn * xn)

        dq0, dk0 = [], []
        dgq_acc = jnp.zeros((1, LANE), F32)
        dgk_acc = jnp.zeros((1, LANE), F32)
        dkr = jnp.zeros((T, LANE), F32)
        for h in range(8):
            sl = slice(h * LANE, (h + 1) * LANE)
            dxq, gq_p = head_bwd(q0[:, sl], dq_ref[:, sl], gq)
            dxk, gk_p = head_bwd(k0[:, sl] + krb, dk_ref[:, sl], gk)
            dq0.append(dxq.astype(BF16))
            dk0.append(dxk.astype(BF16))
            dkr = dkr + dxk
            dgq_acc = dgq_acc + gq_p
            dgk_acc = dgk_acc + gk_p
        dgq_ref[...] += dgq_acc
        dgk_ref[...] += dgk_acc
        dq0 = jnp.concatenate(dq0, axis=1)
        dk0 = jnp.concatenate(dk0, axis=1)
        dvb = dv_ref[...].astype(BF16)
        dwuq_ref[...] += _tn(cqn, dq0)
        dwuk_ref[...] += _tn(ckvn, dk0)
        dwuv_ref[...] += _tn(ckvn, dvb)
        dcqn = _nt(dq0, wuq_ref[...])
        dckvn = _nt(dk0, wuk_ref[...]) + _nt(dvb, wuv_ref[...])
        dgcq_ref[...] += _csum(dcqn * cqh)
        dgckv_ref[...] += _csum(dckvn * ckvh)
        dxh = dcqn * gcq_ref[...]
        dz_ref[:, 0:256] = (rq * (dxh - cqh * jnp.mean(dxh * cqh, axis=-1, keepdims=True))).astype(BF16)
        dxh = dckvn * gckv_ref[...]
        dz_ref[:, 256:384] = (rkv * (dxh - ckvh * jnp.mean(dxh * ckvh, axis=-1, keepdims=True))).astype(BF16)
        lane = _lane((T, LANE))
        dz_ref[:, 384:512] = jnp.where((lane >= KR_LANE) & (lane < KR_LANE + 32), dkr, 0.0).astype(BF16)

    tabspec = pl.BlockSpec((T, LANE), lambda i: (i, 0))
    in_specs = [_zcol(T, 256, C_CQ), _zcol(T, LANE, C_CKV), _zcol(T, LANE, C_KR),
                pl.BlockSpec((T, 1024), lambda i: (i, 0)), pl.BlockSpec((T, 1024), lambda i: (i, 0)), pl.BlockSpec((T, 512), lambda i: (i, 0)),
                _full((1, 256)), _full((1, LANE)), _full((256, 1024)), _full((LANE, 1024)), _full((LANE, 512)), _full((1, LANE)), _full((1, LANE)),
                tabspec, tabspec, tabspec, pl.BlockSpec(memory_space=pl.ANY)]
    out_specs = [_zcol(T, 512, C_CQ), _full((256, 1024)), _full((LANE, 1024)), _full((LANE, 512)), _full((1, 256)), _full((1, LANE)),
                 _full((1, LANE)), _full((1, LANE))]
    out_shape = [jax.ShapeDtypeStruct(dz.shape, BF16), jax.ShapeDtypeStruct((256, 1024), F32), jax.ShapeDtypeStruct((LANE, 1024), F32),
                 jax.ShapeDtypeStruct((LANE, 512), F32), jax.ShapeDtypeStruct((1, 256), F32), jax.ShapeDtypeStruct((1, LANE), F32),
                 jax.ShapeDtypeStruct((1, LANE), F32), jax.ShapeDtypeStruct((1, LANE), F32)]
    return pl.pallas_call(
        body, name="mla_pre_bwd", grid=(S // T,), in_specs=in_specs, out_specs=out_specs, out_shape=out_shape,
        input_output_aliases={16: 0}, compiler_params=_cp(),
    )(zp, zp, zp, dq, dk, dv, w["g_cq"], w["g_ckv"], w["w_uq"], w["w_uk"], w["w_uv"], w["g_mq"], w["g_mk"], *tab, dz)


T_DIL = 256


def _head_stats(x, lane):
    sq = x * x
    sa = _rsum(jnp.where(lane < 64, sq, 0.0))
    sb = _rsum(jnp.where(lane >= 64, sq, 0.0))
    return lax.rsqrt(jnp.where(lane < 64, sa, sb) * (1.0 / DIL_HD) + EPS)


def _head_sum(x, lane):
    sa = _rsum(jnp.where(lane < 64, x, 0.0))
    sb = _rsum(jnp.where(lane >= 64, x, 0.0))
    return jnp.where(lane < 64, sa, sb)


def _dil_pre_fwd(zp, w, tab):
    S = zp.shape[0]
    T = T_DIL

    def body(q_ref, k_ref, v_ref, gq_ref, gk_ref, C_ref, S1_ref, S2_ref, qo_ref, ko_ref, vo_ref):
        C, S1, S2 = C_ref[...], S1_ref[...], S2_ref[...]
        lane = _lane((T, LANE))
        for b in range(12):
            sl = slice(b * LANE, (b + 1) * LANE)
            x = q_ref[:, sl]
            qo_ref[:, sl] = _rope(x * _head_stats(x, lane) * gq_ref[...], C, S1, S2, 32).astype(BF16)
            x = k_ref[:, sl]
            ko_ref[:, sl] = _rope(x * _head_stats(x, lane) * gk_ref[...], C, S1, S2, 32).astype(BF16)
        vo_ref[...] = v_ref[...].astype(BF16)

    tabspec = pl.BlockSpec((T, LANE), lambda i: (i, 0))
    out = pl.BlockSpec((T, 1536), lambda i: (i, 0))
    return pl.pallas_call(
        body, name="dil_pre_fwd", grid=(S // T,),
        in_specs=[_zcol(T, 1536, C_DQ), _zcol(T, 1536, C_DK), _zcol(T, 1536, C_DV), _full((1, LANE)), _full((1, LANE)), tabspec, tabspec, tabspec],
        out_specs=[out, out, out], out_shape=[jax.ShapeDtypeStruct((S, 1536), BF16)] * 3, compiler_params=_cp(),
    )(zp, zp, zp, w["g_dq"], w["g_dk"], *tab)


def _dil_masks(n):
    row = lax.broadcasted_iota(jnp.int32, (NK, NK), 0)
    col = lax.broadcasted_iota(jnp.int32, (NK, NK), 1)
    return col <= row, (col >= row) & (n > 0)


def _dil_attn_fwd(q, k, v, col0, ncol, name):
    M = q.shape[0]
    nb = M // NK

    def body(q_ref, kc_ref, kp_ref, vc_ref, vp_ref, o_ref, lse_ref):
        n = pl.program_id(0)
        mc, mp = _dil_masks(n)
        lane = _lane((NK, LANE))
        q_, kc, kp, vc, vp = q_ref[...], kc_ref[...], kp_ref[...], vc_ref[...], vp_ref[...]
        zb = jnp.zeros_like(q_)
        o_tot = jnp.zeros((NK, LANE), F32)
        lse_tot = jnp.zeros((NK, LANE), F32)
        for hh in range(2):
            hm = (lane < 64) if hh == 0 else (lane >= 64)
            qm = jnp.where(hm, q_, zb)
            sc = jnp.where(mc, _nt(qm, kc) * DIL_SCALE, NEG)
            sp = jnp.where(mp, _nt(qm, kp) * DIL_SCALE, NEG)
            m = jnp.maximum(jnp.max(sc, axis=-1, keepdims=True), jnp.max(sp, axis=-1, keepdims=True))
            ec = jnp.exp(sc - m)
            ep = jnp.exp(sp - m)
            den = _rsum(ec) + _rsum(ep)
            o = (_nn(ec.astype(BF16), jnp.where(hm, vc, zb)) + _nn(ep.astype(BF16), jnp.where(hm, vp, zb))) / den
            o_tot = o_tot + o
            lse_tot = jnp.where(hm, m + jnp.log(den), lse_tot)
        o_ref[...] = o_tot
        lse_ref[...] = lse_tot

    cur = pl.BlockSpec((NK, LANE), lambda n, c: (n, col0 + c))
    prv = pl.BlockSpec((NK, LANE), lambda n, c: (jnp.maximum(n - 1, 0), col0 + c))
    out = pl.BlockSpec((NK, LANE), lambda n, c: (n, c))
    return pl.pallas_call(
        body, name=name, grid=(nb, ncol), in_specs=[cur, cur, prv, cur, prv], out_specs=[out, out],
        out_shape=[jax.ShapeDtypeStruct((M, ncol * LANE), F32)] * 2, compiler_params=_cp(),
    )(q, k, k, v, v)


def _dil_combine(os_, ls_, zp):
    S = zp.shape[0]
    T = T_ROW

    def body(o0, o1, o2, l0, l1, l2, g_ref, oc_ref, L_ref, y_ref):
        a, b, c = l0[...], l1[...], l2[...]
        mx = jnp.maximum(jnp.maximum(a, b), c)
        ea, eb, ec = jnp.exp(a - mx), jnp.exp(b - mx), jnp.exp(c - mx)
        den = ea + eb + ec
        oc = (ea * o0[...] + eb * o1[...] + ec * o2[...]) / den
        oc_ref[...] = oc
        L_ref[...] = mx + jnp.log(den)
        y_ref[...] = (oc * _silu(g_ref[...])).astype(BF16)

    row = pl.BlockSpec((T, 512), lambda i: (i, 0))
    return pl.pallas_call(
        body, name="dil_combine", grid=(S // T,), in_specs=[row] * 6 + [_zcol(T, 512, C_DILG)], out_specs=[row, row, row],
        out_shape=[jax.ShapeDtypeStruct((S, 512), F32), jax.ShapeDtypeStruct((S, 512), F32), jax.ShapeDtypeStruct((S, 512), BF16)],
        compiler_params=_cp(),
    )(*os_, *ls_, zp)


def _dil_comb_bwd(zp, oc, dy, dz):
    S = zp.shape[0]
    T = T_ROW

    def body(g_ref, o_ref, dy_ref, dz_in, dz_ref, do_ref, D_ref):
        del dz_in
        g, o_, dy_ = g_ref[...], o_ref[...], dy_ref[...]
        do = dy_ * _silu(g)
        do_ref[...] = do.astype(BF16)
        dz_ref[...] = (dy_ * o_ * _dsilu(g)).astype(BF16)
        lane = _lane((T, LANE))
        for p in range(4):
            sl = slice(p * LANE, (p + 1) * LANE)
            D_ref[:, sl] = _head_sum(do[:, sl] * o_[:, sl], lane)

    row = pl.BlockSpec((T, 512), lambda i: (i, 0))
    zc = _zcol(T, 512, C_DILG)
    return pl.pallas_call(
        body, name="dil_comb_bwd", grid=(S // T,), in_specs=[zc, row, row, pl.BlockSpec(memory_space=pl.ANY)], out_specs=[zc, row, row],
        out_shape=[jax.ShapeDtypeStruct(dz.shape, BF16), jax.ShapeDtypeStruct((S, 512), BF16), jax.ShapeDtypeStruct((S, 512), F32)],
        input_output_aliases={3: 0}, compiler_params=_cp(),
    )(zp, oc, dy, dz)


def _dil_attn_bwd(q, k, v, do, L, Dr, col0, ncol, name):
    M = q.shape[0]
    nb = M // NK

    def body(qc_ref, qn_ref, kc_ref, kp_ref, vc_ref, vp_ref, doc_ref, don_ref, Lc_ref, Ln_ref, Dc_ref, Dn_ref, dq_ref, dk_ref, dv_ref):
        n = pl.program_id(0)
        mc, mp = _dil_masks(n)
        mnext = (lax.broadcasted_iota(jnp.int32, (NK, NK), 1) >= lax.broadcasted_iota(jnp.int32, (NK, NK), 0)) & (n < nb - 1)
        lane = _lane((NK, LANE))
        qc, qn, kc, kp, vc, vp = qc_ref[...], qn_ref[...], kc_ref[...], kp_ref[...], vc_ref[...], vp_ref[...]
        doc, don = doc_ref[...], don_ref[...]
        zb = jnp.zeros_like(qc)
        dq_tot = jnp.zeros((NK, LANE), F32)
        dk_tot = jnp.zeros((NK, LANE), F32)
        dv_tot = jnp.zeros((NK, LANE), F32)
        for hh in range(2):
            hm = (lane < 64) if hh == 0 else (lane >= 64)

            def bcast(x, hm=hm):
                return jnp.where(hm, x, pltpu.roll(x, 64, 1))

            Lc, Ln, Dc, Dn = bcast(Lc_ref[...]), bcast(Ln_ref[...]), bcast(Dc_ref[...]), bcast(Dn_ref[...])
            qm = jnp.where(hm, qc, zb)
            qnm = jnp.where(hm, qn, zb)
            vcm = jnp.where(hm, vc, zb)
            vpm = jnp.where(hm, vp, zb)
            pc = jnp.exp(jnp.where(mc, _nt(qm, kc) * DIL_SCALE, NEG) - Lc)
            pp = jnp.exp(jnp.where(mp, _nt(qm, kp) * DIL_SCALE, NEG) - Lc)
            dsc = (pc * (_nt(doc, vcm) - Dc) * DIL_SCALE).astype(BF16)
            dsp = (pp * (_nt(doc, vpm) - Dc) * DIL_SCALE).astype(BF16)
            dq_tot = dq_tot + jnp.where(hm, _nn(dsc, kc) + _nn(dsp, kp), 0.0)
            p2 = jnp.exp(jnp.where(mnext, _nt(qnm, kc) * DIL_SCALE, NEG) - Ln)
            ds2 = (p2 * (_nt(don, vcm) - Dn) * DIL_SCALE).astype(BF16)
            dk_tot = dk_tot + _tn(dsc, qm) + _tn(ds2, qnm)
            dv_tot = dv_tot + jnp.where(hm, _tn(pc.astype(BF16), doc) + _tn(p2.astype(BF16), don), 0.0)
        dq_ref[...] = dq_tot
        dk_ref[...] = dk_tot
        dv_ref[...] = dv_tot

    def spec(shift, c0):
        return pl.BlockSpec((NK, LANE), lambda n, c: (jnp.clip(n + shift, 0, nb - 1), c0 + c))

    out = spec(0, 0)
    return pl.pallas_call(
        body, name=name, grid=(nb, ncol),
        in_specs=[spec(0, col0), spec(1, col0), spec(0, col0), spec(-1, col0), spec(0, col0), spec(-1, col0),
                  spec(0, 0), spec(1, 0), spec(0, 0), spec(1, 0), spec(0, 0), spec(1, 0)],
        out_specs=[out, out, out], out_shape=[jax.ShapeDtypeStruct((M, ncol * LANE), F32)] * 3, compiler_params=_cp(),
    )(q, q, k, k, v, v, do, do, L, L, Dr, Dr)


def _dil_pre_bwd(zp, dy, g, tab, dz, col, name):
    S = zp.shape[0]
    T = T_DIL

    def body(x_ref, dy_ref, g_ref, C_ref, S1_ref, S2_ref, dz_in, dz_ref, dg_ref):
        del dz_in
        i = pl.program_id(0)
        C, S1, S2 = C_ref[...], S1_ref[...], S2_ref[...]
        lane = _lane((T, LANE))
        gv = g_ref[...]
        acc = jnp.zeros((1, LANE), F32)
        for b in range(12):
            sl = slice(b * LANE, (b + 1) * LANE)
            x = x_ref[:, sl]
            r = _head_stats(x, lane)
            xn = x * r
            dyn = _rope_t(dy_ref[:, sl], C, S1, S2, 32)
            acc = acc + _csum(dyn * xn)
            dxh = dyn * gv
            dz_ref[:, sl] = (r * (dxh - xn * _head_sum(dxh * xn, lane) * (1.0 / DIL_HD))).astype(BF16)

        @pl.when(i == 0)
        def _():
            dg_ref[...] = acc

        @pl.when(i > 0)
        def _():
            dg_ref[...] += acc

    tabspec = pl.BlockSpec((T, LANE), lambda i: (i, 0))
    zc = _zcol(T, 1536, col)
    return pl.pallas_call(
        body, name=name, grid=(S // T,),
        in_specs=[zc, pl.BlockSpec((T, 1536), lambda i: (i, 0)), _full((1, LANE)), tabspec, tabspec, tabspec, pl.BlockSpec(memory_space=pl.ANY)],
        out_specs=[zc, _full((1, LANE))], out_shape=[jax.ShapeDtypeStruct(dz.shape, BF16), jax.ShapeDtypeStruct((1, LANE), F32)],
        input_output_aliases={6: 0}, compiler_params=_cp(),
    )(zp, dy, g, *tab, dz)


def _cast_into(src, dz, col, width, name):
    S = src.shape[0]
    T = T_ROW

    def body(s_ref, dz_in, o_ref):
        del dz_in
        o_ref[...] = s_ref[...].astype(BF16)

    return pl.pallas_call(
        body, name=name, grid=(S // T,), in_specs=[pl.BlockSpec((T, width), lambda i: (i, 0)), pl.BlockSpec(memory_space=pl.ANY)],
        out_specs=_zcol(T, width, col), out_shape=jax.ShapeDtypeStruct(dz.shape, BF16), input_output_aliases={1: 0}, compiler_params=_cp(),
    )(src, dz)


T_MRG = 256


def _merge_fwd(P, zp, b_merge):
    S = zp.shape[0]
    T = T_MRG

    def body(p0, p1, p2, m0, m1, m2, b_ref, o_ref):
        acc = jnp.zeros((T, D), F32)
        for j, (p, m) in enumerate(((p0, m0), (p1, m1), (p2, m2))):
            acc = acc + _sig(m[...] + b_ref[:, j * D:(j + 1) * D]) * p[...]
        o_ref[...] = acc.astype(BF16)

    row = pl.BlockSpec((T, D), lambda i: (i, 0))
    return pl.pallas_call(
        body, name="merge_fwd", grid=(S // T,),
        in_specs=[row, row, row] + [_zcol(T, D, C_MERGE + 8 * j) for j in range(3)] + [_full((1, 3 * D))], out_specs=row,
        out_shape=jax.ShapeDtypeStruct((S, D), BF16), compiler_params=_cp(),
    )(*P, zp, zp, zp, b_merge)


def _merge_bwd(dm, Pj, zp, bj, dz, j):
    S = zp.shape[0]
    T = T_MRG

    def body(dm_ref, p_ref, m_ref, b_ref, dz_in, dz_ref, dp_ref, db_ref):
        del dz_in
        i = pl.program_id(0)
        g = _sig(m_ref[...] + b_ref[...])
        dmv = dm_ref[...]
        dp_ref[...] = (dmv * g).astype(BF16)
        dg = dmv * p_ref[...] * g * (1.0 - g)
        dz_ref[...] = dg.astype(BF16)
        part = _csum(dg)

        @pl.when(i == 0)
        def _():
            db_ref[...] = part

        @pl.when(i > 0)
        def _():
            db_ref[...] += part

    row = pl.BlockSpec((T, D), lambda i: (i, 0))
    zc = _zcol(T, D, C_MERGE + 8 * j)
    return pl.pallas_call(
        body, name=f"merge_bwd{j}", grid=(S // T,), in_specs=[row, row, zc, _full((1, D)), pl.BlockSpec(memory_space=pl.ANY)],
        out_specs=[zc, row, _full((1, D))],
        out_shape=[jax.ShapeDtypeStruct(dz.shape, BF16), jax.ShapeDtypeStruct((S, D), BF16), jax.ShapeDtypeStruct((1, D), F32)],
        input_output_aliases={4: 0}, compiler_params=_cp(),
    )(dm, Pj, zp, bj, dz)


def _loss_fwd_bwd(y, target):
    S = y.shape[0]
    T = T_ROW

    def body(y_ref, t_ref, loss_ref, dy_ref):
        i = pl.program_id(0)
        err = y_ref[...] - t_ref[...]
        dy_ref[...] = err * (1.0 / D)
        part = jnp.sum(err * err, keepdims=True).reshape(1, 1) * (0.5 / D)

        @pl.when(i == 0)
        def _():
            loss_ref[...] = part

        @pl.when(i > 0)
        def _():
            loss_ref[...] += part

    row = pl.BlockSpec((T, D), lambda i: (i, 0))
    return pl.pallas_call(
        body, name="loss", grid=(S // T,), in_specs=[row, row], out_specs=[_full((1, 1)), row],
        out_shape=[jax.ShapeDtypeStruct((1, 1), F32), jax.ShapeDtypeStruct((S, D), F32)], compiler_params=_cp(),
    )(y, target)


def _strided(a, g):
    d = DIL_DILATIONS[g]
    S = a.shape[0]
    return a[:, g * 512:(g + 1) * 512].reshape(S // d, d * 512)


def _layer_fwd(x, w, tabs):
    mla_tab, dil_tab = tabs
    S = x.shape[0]
    h = _rms_in_fwd(x, w["norm_g"])
    zp = _mm(h, w["w_in"], mode="nn", name="in_proj")
    hs, y_lru = _lru_fwd(zp, w)
    q, k, v = _mla_pre_fwd(zp, w, mla_tab)
    o_mla, lse, y_mla = _mla_attn_fwd(q, k, v, zp)
    qd, kd, vd = _dil_pre_fwd(zp, w, dil_tab)
    og, lg, strided = [], [], []
    for g, d in enumerate(DIL_DILATIONS):
        if d == 1:
            qs, ks, vs, col0 = qd, kd, vd, 0
        else:
            qs, ks, vs, col0 = _strided(qd, g), _strided(kd, g), _strided(vd, g), 0
        o, l = _dil_attn_fwd(qs, ks, vs, col0, 4 * d, f"dil_attn_fwd{g}")
        strided.append((qs, ks, vs))
        og.append(o.reshape(S, 512))
        lg.append(l.reshape(S, 512))
    oc, L, y_dil = _dil_combine(og, lg, zp)
    P = [_mm(y_lru, w["w_lru_o"], mode="nn", name="lru_out"), _mm(y_mla, w["w_mla_o"], mode="nn", name="mla_out"),
         _mm(y_dil, w["w_dil_o"], mode="nn", name="dil_out")]
    merged = _merge_fwd(P, zp, w["b_merge"])
    x_out = _mm(merged, w["w_out"], mode="nn", name="out_proj", add=x)
    saved = dict(x=x, h=h, zp=zp, hs=hs, y=(y_lru, y_mla, y_dil), q=q, k=k, v=v, o_mla=o_mla, lse=lse, strided=strided, oc=oc, L=L, P=P,
                 merged=merged)
    return x_out, saved


def _layer_bwd(dout, w, tabs, sv):
    mla_tab, dil_tab = tabs
    zp = sv["zp"]
    S = zp.shape[0]
    g = {}
    dm = _mm(dout, w["w_out"], mode="nt", name="d_merged")
    g["w_out"] = _mm(sv["merged"], dout, mode="tn", name="dw_out")
    dz = lax.empty((S, ZW), BF16)
    dP, db = [], []
    for j in range(3):
        dz, dpj, dbj = _merge_bwd(dm, sv["P"][j], zp, w["b_merge"][:, j * D:(j + 1) * D], dz, j)
        dP.append(dpj)
        db.append(dbj)
    g["b_merge"] = jnp.concatenate(db, axis=1)
    names = ("w_lru_o", "w_mla_o", "w_dil_o")
    dy = []
    for j in range(3):
        dy.append(_mm(dP[j], w[names[j]], mode="nt", name="dy_" + names[j]))
        g[names[j]] = _mm(sv["y"][j], dP[j], mode="tn", name="d" + names[j])
    dz = _lru_gate_bwd(zp, sv["hs"], dy[0], dz)
    dz, g["conv_w"], g["conv_b"], g["w_gx"], g["b_gx"], g["w_ga"], g["b_ga"], g["lam"] = _lru_bwd(zp, sv["hs"], dy[0], w, dz)
    dz, do, Dr = _mla_post_bwd(zp, sv["o_mla"], dy[1], dz)
    dq, dk, dv = _mla_attn_bwd(sv["q"], sv["k"], sv["v"], do, sv["lse"], Dr)
    dz, g["w_uq"], g["w_uk"], g["w_uv"], g["g_cq"], g["g_ckv"], g["g_mq"], g["g_mk"] = _mla_pre_bwd(zp, dq, dk, dv, w, mla_tab, dz)
    dz, dod, Dd = _dil_comb_bwd(zp, sv["oc"], dy[2], dz)
    dqs, dks, dvs = [], [], []
    for gi, d in enumerate(DIL_DILATIONS):
        qs, ks, vs = sv["strided"][gi]
        col0 = 0
        shp = (S // d, d * 512)
        a, b, c = _dil_attn_bwd(qs, ks, vs, dod.reshape(shp), sv["L"].reshape(shp), Dd.reshape(shp), col0, 4 * d, f"dil_attn_bwd{gi}")
        dqs.append(a.reshape(S, 512))
        dks.append(b.reshape(S, 512))
        dvs.append(c.reshape(S, 512))
    dz, g["g_dq"] = _dil_pre_bwd(zp, jnp.concatenate(dqs, axis=1), w["g_dq"], dil_tab, dz, C_DQ, "dil_pre_bwd_q")
    dz, g["g_dk"] = _dil_pre_bwd(zp, jnp.concatenate(dks, axis=1), w["g_dk"], dil_tab, dz, C_DK, "dil_pre_bwd_k")
    dz = _cast_into(jnp.concatenate(dvs, axis=1), dz, C_DV, 1536, "dil_dv")
    dh = _mm(dz, w["w_in"], mode="nt", name="d_h")
    g["w_in"] = _mm(sv["h"], dz, mode="tn", name="dw_in")
    dx, g["norm_g"] = _rms_in_bwd(sv["x"], w["norm_g"], dh, dout)
    return dx, g


def _peers():
    mx, my, mc = lax.axis_index("x"), lax.axis_index("y"), lax.axis_index("c")
    me = 4 * mx + 2 * my + mc
    out = []
    for k in range(1, N_DEV):
        px = 1 - mx if k & 4 else mx
        py = 1 - my if k & 2 else my
        pc = 1 - mc if k & 1 else mc
        out.append(((px, py, pc), 4 * px + 2 * py + pc))
    return me, out


def _gather_all(arrs, name):
    n = len(arrs)

    def body(*refs):
        ins, outs = refs[:n], refs[n:2 * n]
        send_sems, recv_sems, local_sems = refs[2 * n:]
        me, peers = _peers()
        mine = [pltpu.make_async_copy(ins[a], outs[a].at[me], local_sems.at[a]) for a in range(n)]
        for cp in mine:
            cp.start()
        copies = []
        for k, (peer, _) in enumerate(peers):
            for a in range(n):
                cp = pltpu.make_async_remote_copy(
                    src_ref=ins[a], dst_ref=outs[a].at[me], send_sem=send_sems.at[k * n + a], recv_sem=recv_sems.at[k * n + a],
                    device_id=peer, device_id_type=pl.DeviceIdType.MESH)
                cp.start()
                copies.append(cp)
        for cp in copies + mine:
            cp.wait()

    nsem = (N_DEV - 1) * n
    return pl.pallas_call(
        body, name=name, out_shape=[jax.ShapeDtypeStruct((N_DEV,) + a.shape, a.dtype) for a in arrs],
        in_specs=[pl.BlockSpec(memory_space=pl.ANY)] * n, out_specs=[pl.BlockSpec(memory_space=pl.ANY)] * n,
        scratch_shapes=[pltpu.SemaphoreType.DMA((nsem,)), pltpu.SemaphoreType.DMA((nsem,)), pltpu.SemaphoreType.DMA((n,))],
        compiler_params=pltpu.CompilerParams(has_side_effects=True),
    )(*arrs)


WIN = 13 * LANE


def _win_base(s):
    n = s * SHARD_IN
    a0 = n + jnp.where(n >= _KR0, KR_LANE, 0) + jnp.where(n >= _KR0 + 32, 32, 0)
    return jnp.minimum(a0 // LANE, (ZW - WIN) // LANE)


def _win_offsets(s):
    n = s * SHARD_IN + jnp.arange(SHARD_IN)
    o = s * SHARD_IN - _win_base(s) * LANE
    return n, (o, o + KR_LANE, o + LANE - 32)


def _to_window(shard, s):
    n, offs = _win_offsets(s)
    masks = (n < _KR0, (n >= _KR0) & (n < _KR0 + 32), n >= _KR0 + 32)
    zero = jnp.zeros(shard.shape[:2] + (WIN,), shard.dtype)
    out = zero
    for m, o in zip(masks, offs):
        out = out + lax.dynamic_update_slice(zero, jnp.where(m[None, None, :], shard, jnp.zeros_like(shard)), (0, 0, o))
    return out


def _from_window(win, s):
    n, offs = _win_offsets(s)
    a, b, c = [lax.dynamic_slice(win, (0, 0, o), win.shape[:2] + (SHARD_IN,)) for o in offs]
    return jnp.where((n < _KR0)[None, None, :], a, jnp.where((n < _KR0 + 32)[None, None, :], b, c))


def _win_base_static(s):
    n = s * SHARD_IN
    a0 = n + (KR_LANE if n >= _KR0 else 0) + (32 if n >= _KR0 + 32 else 0)
    return min(a0 // LANE, (ZW - WIN) // LANE)


def _assemble_w_in(gw):
    tr = 128
    bases = [_win_base_static(s) for s in range(N_DEV)]

    def body(g_ref, o_ref):
        for j in range(ZW // LANE):
            acc = None
            for s in range(N_DEV):
                if bases[s] <= j < bases[s] + WIN // LANE:
                    piece = g_ref[s, 0, :, (j - bases[s]) * LANE:(j - bases[s] + 1) * LANE]
                    acc = piece if acc is None else acc + piece
            o_ref[0, :, j * LANE:(j + 1) * LANE] = acc

    return pl.pallas_call(
        body, name="assemble_w_in", grid=(DEPTH, D // tr),
        in_specs=[pl.BlockSpec((N_DEV, 1, tr, WIN), lambda l, i: (0, l, i, 0))],
        out_specs=pl.BlockSpec((1, tr, ZW), lambda l, i: (l, i, 0)),
        out_shape=jax.ShapeDtypeStruct((DEPTH, D, ZW), gw.dtype), compiler_params=_cp(),
    )(gw)


def _cols(width):
    return lambda ref, p: ref.at[:, pl.ds(pl.multiple_of(p * width, width), width)]


def _rows(height):
    return lambda ref, p: ref.at[pl.ds(pl.multiple_of(p * height, height), height), :]


SCATTER = {
    'w_in': (lambda ref, p: ref.at[:, pl.ds(pl.multiple_of(_win_base(p) * LANE, LANE), WIN)], (D, WIN)),
    'conv_w': (_cols(LANE), (4, LANE)),
    'w_lru_o': (_rows(LANE), (LANE, D)),
    'w_uq': (_cols(LANE), (256, LANE)),
    'w_ukv': (_cols(LANE), (128, LANE)),
    'w_mla_o': (_cols(LANE), (512, LANE)),
    'w_dil_o': (_cols(LANE), (512, LANE)),
    'w_out': (_rows(LANE), (LANE, D)),
}


def _scatter_grads(glayers):
    names = list(SCATTER)
    nw = len(names)
    arrs = [glayers[l][nm] for l in range(DEPTH) for nm in names]

    def body(*refs):
        ins, outs = refs[:DEPTH * nw], refs[DEPTH * nw:DEPTH * nw + nw]
        send_sems, recv_sems, local_sems = refs[DEPTH * nw + nw:]
        me, peers = _peers()
        mine, copies = [], []
        for l in range(DEPTH):
            for a, nm in enumerate(names):
                cp = pltpu.make_async_copy(SCATTER[nm][0](ins[l * nw + a], me), outs[a].at[me, l], local_sems.at[l * nw + a])
                cp.start()
                mine.append(cp)
        for k, (peer, pidx) in enumerate(peers):
            for l in range(DEPTH):
                for a, nm in enumerate(names):
                    i = (k * DEPTH + l) * nw + a
                    cp = pltpu.make_async_remote_copy(
                        src_ref=SCATTER[nm][0](ins[l * nw + a], pidx), dst_ref=outs[a].at[me, l], send_sem=send_sems.at[i],
                        recv_sem=recv_sems.at[i], device_id=peer, device_id_type=pl.DeviceIdType.MESH)
                    cp.start()
                    copies.append(cp)
        for cp in copies + mine:
            cp.wait()

    nsem = (N_DEV - 1) * DEPTH * nw
    outs = pl.pallas_call(
        body, name="scatter_grads", out_shape=[jax.ShapeDtypeStruct((N_DEV, DEPTH) + SCATTER[nm][1], F32) for nm in names],
        in_specs=[pl.BlockSpec(memory_space=pl.ANY)] * (DEPTH * nw), out_specs=[pl.BlockSpec(memory_space=pl.ANY)] * nw,
        scratch_shapes=[pltpu.SemaphoreType.DMA((nsem,)), pltpu.SemaphoreType.DMA((nsem,)), pltpu.SemaphoreType.DMA((DEPTH * nw,))],
        compiler_params=pltpu.CompilerParams(has_side_effects=True),
    )(*arrs)
    return dict(zip(names, outs))


PACK_ROWS = 512


def _sum8(buf, name):
    _, R, C = buf.shape
    tr = R
    while tr * C * 4 * N_DEV > (1 << 22) and tr % 16 == 0:
        tr //= 2

    def body(b_ref, o_ref):
        acc = b_ref[0]
        for s in range(1, N_DEV):
            acc = acc + b_ref[s]
        o_ref[...] = acc

    return pl.pallas_call(
        body, name=name, grid=(R // tr,), in_specs=[pl.BlockSpec((N_DEV, tr, C), lambda i: (0, i, 0))],
        out_specs=pl.BlockSpec((tr, C), lambda i: (i, 0)), out_shape=jax.ShapeDtypeStruct((R, C), F32), compiler_params=_cp(),
    )(buf)


def _pack(arrs, dtype, lead):
    flat = [a.astype(dtype).reshape(a.shape[:lead] + (-1,)) for a in arrs]
    cat = jnp.concatenate(flat, axis=-1)
    n = cat.shape[-1]
    unit = PACK_ROWS * LANE
    pad = (-n) % unit
    if pad:
        cat = jnp.pad(cat, [(0, 0)] * lead + [(0, pad)])
    return cat.reshape(cat.shape[:lead] + ((n + pad) // LANE, LANE))


def _unpack(buf, shapes, lead):
    flat = buf.reshape(buf.shape[:lead] + (-1,))
    out, off = [], 0
    for shp in shapes:
        n = int(np.prod(shp))
        out.append(flat[..., off:off + n].reshape(buf.shape[:lead] + tuple(shp)))
        off += n
    return out


def _adamw(w, g, m, v, name):
    rows, cols = w.shape
    tr = rows
    while tr * cols * 4 > (3 << 19) and tr % 16 == 0:
        tr //= 2
    c1 = 1.0 - ADAM_B1 ** ADAM_STEP
    c2 = 1.0 - ADAM_B2 ** ADAM_STEP

    def body(w_ref, g_ref, m_ref, v_ref, d_ref, mo_ref, vo_ref):
        gv = g_ref[...]
        mn = ADAM_B1 * m_ref[...] + (1.0 - ADAM_B1) * gv
        vn = ADAM_B2 * v_ref[...] + (1.0 - ADAM_B2) * (gv * gv)
        mo_ref[...] = mn
        vo_ref[...] = vn
        d_ref[...] = -ADAM_LR * ((mn / c1) / (jnp.sqrt(vn / c2) + ADAM_EPS) + ADAM_WD * w_ref[...])

    spec = pl.BlockSpec((tr, cols), lambda i: (i, 0))
    return pl.pallas_call(
        body, name=name, grid=(rows // tr,), in_specs=[spec] * 4, out_specs=[spec] * 3,
        out_shape=[jax.ShapeDtypeStruct((rows, cols), F32)] * 3, compiler_params=_cp(),
    )(w, g, m, v)


IN_NAMES = ['x', 'positions', 'norm_g', 'w_in', 'conv_w', 'conv_b', 'w_gate_x', 'b_gate_x', 'w_gate_a', 'b_gate_a', 'lru_lambda', 'w_lru_o',
            'cq_norm_g', 'ckv_norm_g', 'w_uq', 'w_ukv', 'mla_q_norm_g', 'mla_k_norm_g', 'w_mla_o', 'dil_q_norm_g', 'dil_k_norm_g', 'w_dil_o',
            'b_merge', 'w_out']
WEIGHTS = IN_NAMES[2:]
REPLICATED = [n for n in WEIGHTS if n not in SCATTER]

_KR0 = C_KR * LANE


def _layer_weights(gw, rep, l):
    by_rows = lambda a: a[:, l].reshape(-1, a.shape[-1])
    by_cols = lambda a: jnp.swapaxes(a[:, l], 0, 1).reshape(a.shape[2], -1)
    ukv = jnp.swapaxes(gw['w_ukv'][:, l], 0, 1)
    g96 = lambda a: jnp.pad(a[l].reshape(1, MLA_QK), ((0, 0), (0, LANE - MLA_QK)))
    g64 = lambda a: jnp.tile(a[l].reshape(1, DIL_HD), (1, 2))
    return dict(
        norm_g=rep['norm_g'][l].reshape(1, D), w_in=gw['w_in'][l],
        conv_w=by_cols(gw['conv_w']), conv_b=rep['conv_b'][l].reshape(1, D),
        w_gx=rep['w_gate_x'][l].astype(BF16), b_gx=rep['b_gate_x'][l].reshape(8, 1, LANE),
        w_ga=rep['w_gate_a'][l].astype(BF16), b_ga=rep['b_gate_a'][l].reshape(8, 1, LANE),
        lam=rep['lru_lambda'][l].reshape(1, D),
        w_lru_o=by_rows(gw['w_lru_o']), w_mla_o=by_cols(gw['w_mla_o']), w_dil_o=by_cols(gw['w_dil_o']), w_out=by_rows(gw['w_out']),
        g_cq=rep['cq_norm_g'][l].reshape(1, 256), g_ckv=rep['ckv_norm_g'][l].reshape(1, 128),
        w_uq=by_cols(gw['w_uq']), w_uk=jnp.pad(ukv[:, :, :64], ((0, 0), (0, 0), (0, 64))).reshape(128, 1024),
        w_uv=ukv[:, :, 64:].reshape(128, 512),
        g_mq=g96(rep['mla_q_norm_g']), g_mk=g96(rep['mla_k_norm_g']), g_dq=g64(rep['dil_q_norm_g']), g_dk=g64(rep['dil_k_norm_g']),
        b_merge=rep['b_merge'][l].reshape(1, 3 * D),
    )


def _sharded_grads(g):
    uk = g['w_uk'].reshape(128, 8, 128)[:, :, :64]
    uv = g['w_uv'].reshape(128, 8, 64)
    return {'w_in': g['w_in'], 'conv_w': g['conv_w'], 'w_lru_o': g['w_lru_o'], 'w_uq': g['w_uq'],
            'w_ukv': jnp.concatenate([uk, uv], axis=-1).reshape(128, 1024), 'w_mla_o': g['w_mla_o'], 'w_dil_o': g['w_dil_o'],
            'w_out': g['w_out']}


def _replicated_grads(g):
    return {
        'norm_g': g['norm_g'].reshape(D), 'conv_b': g['conv_b'].reshape(D),
        'w_gate_x': g['w_gx'], 'b_gate_x': g['b_gx'].reshape(8, LANE), 'w_gate_a': g['w_ga'], 'b_gate_a': g['b_ga'].reshape(8, LANE),
        'lru_lambda': g['lam'].reshape(D), 'cq_norm_g': g['g_cq'].reshape(256), 'ckv_norm_g': g['g_ckv'].reshape(128),
        'mla_q_norm_g': g['g_mq'][0, :MLA_QK], 'mla_k_norm_g': g['g_mk'][0, :MLA_QK],
        'dil_q_norm_g': g['g_dq'][0, :DIL_HD] + g['g_dq'][0, DIL_HD:], 'dil_k_norm_g': g['g_dk'][0, :DIL_HD] + g['g_dk'][0, DIL_HD:],
        'b_merge': g['b_merge'].reshape(3 * D),
    }


def _device_step(x, pos, target, ws):
    tabs = _rope_tables(pos)
    saved = []
    for l in range(DEPTH):
        x, sv = _layer_fwd(x, ws[l], tabs)
        saved.append(sv)
    loss, dx = _loss_fwd_bwd(x, target)
    grads = [None] * DEPTH
    for l in reversed(range(DEPTH)):
        dx, grads[l] = _layer_bwd(dx, ws[l], tabs, saved[l])
    return loss[0, 0], dx, grads


def kernel(x, positions, norm_g, w_in, conv_w, conv_b, w_gate_x, b_gate_x, w_gate_a, b_gate_a, lru_lambda, w_lru_o, cq_norm_g, ckv_norm_g, w_uq, w_ukv, mla_q_norm_g, mla_k_norm_g, w_mla_o, dil_q_norm_g, dil_k_norm_g, w_dil_o, b_merge, w_out, loss_target, m_norm_g, m_w_in, m_conv_w, m_conv_b, m_w_gate_x, m_b_gate_x, m_w_gate_a, m_b_gate_a, m_lru_lambda, m_w_lru_o, m_cq_norm_g, m_ckv_norm_g, m_w_uq, m_w_ukv, m_mla_q_norm_g, m_mla_k_norm_g, m_w_mla_o, m_dil_q_norm_g, m_dil_k_norm_g, m_w_dil_o, m_b_merge, m_w_out, v_norm_g, v_w_in, v_conv_w, v_conv_b, v_w_gate_x, v_b_gate_x, v_w_gate_a, v_b_gate_a, v_lru_lambda, v_w_lru_o, v_cq_norm_g, v_ckv_norm_g, v_w_uq, v_w_ukv, v_mla_q_norm_g, v_mla_k_norm_g, v_w_mla_o, v_dil_q_norm_g, v_dil_k_norm_g, v_w_dil_o, v_b_merge, v_w_out):
    args = (x, positions, norm_g, w_in, conv_w, conv_b, w_gate_x, b_gate_x, w_gate_a, b_gate_a, lru_lambda, w_lru_o, cq_norm_g, ckv_norm_g, w_uq, w_ukv, mla_q_norm_g, mla_k_norm_g, w_mla_o, dil_q_norm_g, dil_k_norm_g, w_dil_o, b_merge, w_out)
    moments_m = (m_norm_g, m_w_in, m_conv_w, m_conv_b, m_w_gate_x, m_b_gate_x, m_w_gate_a, m_b_gate_a, m_lru_lambda, m_w_lru_o, m_cq_norm_g, m_ckv_norm_g, m_w_uq, m_w_ukv, m_mla_q_norm_g, m_mla_k_norm_g, m_w_mla_o, m_dil_q_norm_g, m_dil_k_norm_g, m_w_dil_o, m_b_merge, m_w_out)
    moments_v = (v_norm_g, v_w_in, v_conv_w, v_conv_b, v_w_gate_x, v_b_gate_x, v_w_gate_a, v_b_gate_a, v_lru_lambda, v_w_lru_o, v_cq_norm_g, v_ckv_norm_g, v_w_uq, v_w_ukv, v_mla_q_norm_g, v_mla_k_norm_g, v_w_mla_o, v_dil_q_norm_g, v_dil_k_norm_g, v_w_dil_o, v_b_merge, v_w_out)
    a = dict(zip(IN_NAMES, args))
    wd = {n: a[n] for n in WEIGHTS}
    md = dict(zip(WEIGHTS, moments_m))
    vd = dict(zip(WEIGHTS, moments_v))

    me = 4 * lax.axis_index("x") + 2 * lax.axis_index("y") + lax.axis_index("c")

    order = ['w_in', 'w_lru_o', 'w_uq', 'w_ukv', 'w_mla_o', 'w_dil_o', 'w_out', 'conv_w']
    local = {n: wd[n].astype(BF16) for n in order[:-1]}
    local['w_in'] = _to_window(local['w_in'], me)
    local['w_uq'] = jnp.pad(local['w_uq'], ((0, 0), (0, 0), (0, LANE - MLA_QK)))
    local['conv_w'] = wd['conv_w']
    gw = dict(zip(order, _gather_all([local[n] for n in order], "gather_weights")))
    gw['w_in'] = _assemble_w_in(gw['w_in'])
    ws = [_layer_weights(gw, wd, l) for l in range(DEPTH)]

    loss, grad_x, glayers = _device_step(x[0], positions[0], loss_target[0], ws)

    sharded = list(SCATTER)
    gsh = {}
    for n, buf in _scatter_grads([_sharded_grads(g) for g in glayers]).items():
        r, c = buf.shape[2:]
        gsh[n] = _sum8(buf.reshape(N_DEV, DEPTH * r, c), "sum_" + n).reshape(DEPTH, r, c)
    gsh['w_in'] = _from_window(gsh['w_in'], me)
    gsh['w_uq'] = gsh['w_uq'][:, :, :MLA_QK]
    rshapes = [wd[n].shape for n in REPLICATED]
    rgrads = [_replicated_grads(g) for g in glayers]
    rpacked = _pack([jnp.stack([rgrads[l][n] for l in range(DEPTH)]) for n in REPLICATED], F32, 0)
    grep = _sum8(_gather_all([rpacked], "gather_grads")[0], "sum_replicated")

    out_g, out_d, out_m, out_v = {}, {}, {}, {}
    d_, m_, v_ = _adamw(_pack([wd[n] for n in REPLICATED], F32, 0), grep, _pack([md[n] for n in REPLICATED], F32, 0),
                        _pack([vd[n] for n in REPLICATED], F32, 0), "adamw_replicated")
    for dst, buf in ((out_g, grep), (out_d, d_), (out_m, m_), (out_v, v_)):
        dst.update(zip(REPLICATED, _unpack(buf, rshapes, 0)))
    for n in sharded:
        shp = wd[n].shape
        two = (shp[0] * shp[1], shp[2])
        d_, m_, v_ = _adamw(wd[n].reshape(two), gsh[n].reshape(two), md[n].reshape(two), vd[n].reshape(two), "adamw_" + n)
        out_g[n], out_d[n], out_m[n], out_v[n] = gsh[n], d_.reshape(shp), m_.reshape(shp), v_.reshape(shp)

    loss = lax.psum(loss, ("x", "y", "c"))
    return (loss, grad_x[None], *[out_g[n] for n in WEIGHTS], *[out_d[n] for n in WEIGHTS], *[out_m[n] for n in WEIGHTS],
            *[out_v[n] for n in WEIGHTS])
```

```python
import functools

import numpy as np
import jax
import jax.numpy as jnp
from jax import lax
from jax.experimental import pallas as pl
from jax.experimental.pallas import tpu as pltpu

F32 = jnp.float32
BF16 = jnp.bfloat16

N_DEV = 8
D = 1024
DEPTH = 2
EPS = 1e-6
ROPE_THETA = 10000.0
LRU_C = 8.0
LANE = 128
SUB = 8
IN_WIDTH = 11168
SHARD_IN = IN_WIDTH // N_DEV

C_LRUX, C_LRUG, C_CQ, C_CKV, C_KR, C_MLAG, C_DQ, C_DK, C_DV, C_DILG, C_MERGE = 0, 8, 16, 18, 19, 20, 24, 36, 48, 60, 64
ZW = 88 * LANE
KR_LANE = 64

MLA_QK = 96
MLA_SCALE = MLA_QK ** -0.5
DIL_HD = 64
DIL_SCALE = DIL_HD ** -0.5
DIL_DILATIONS = (1, 4, 16)
NK = 128

ADAM_LR, ADAM_B1, ADAM_B2, ADAM_EPS, ADAM_WD, ADAM_STEP = 0.001, 0.9, 0.999, 1e-08, 0.01, 10

NEG = -1e30
VMEM_LIMIT = 48 * 1024 * 1024


def _cp(**kw):
    return pltpu.CompilerParams(vmem_limit_bytes=VMEM_LIMIT, **kw)


def _sig(x):
    return 1.0 / (1.0 + jnp.exp(-x))


def _silu(x):
    return x * _sig(x)


def _dsilu(x):
    s = _sig(x)
    return s * (1.0 + x * (1.0 - s))


def _dot(a, b, dims):
    return lax.dot_general(a, b, (dims, ((), ())), preferred_element_type=F32)


def _nn(a, b):
    return _dot(a, b, ((1,), (0,)))


def _nt(a, b):
    return _dot(a, b, ((1,), (1,)))


def _tn(a, b):
    return _dot(a, b, ((0,), (0,)))


def _rsum(x):
    return jnp.sum(x, axis=-1, keepdims=True)


def _csum(x):
    return jnp.sum(x, axis=0, keepdims=True)


def _mm(a, b, *, mode, name, out_dtype=F32, add=None, tm=1024, tn=1024, tk=1024):
    if mode == "nn":
        (M, K), (K2, N) = a.shape, b.shape
    elif mode == "nt":
        (M, K), (N, K2) = a.shape, b.shape
    else:
        (K, M), (K2, N) = a.shape, b.shape
    assert K == K2
    tm, tn, tk = min(tm, M), min(tn, N), min(tk, K)
    assert M % tm == 0 and N % tn == 0 and K % tk == 0
    nk = K // tk
    fn = {"nn": _nn, "nt": _nt, "tn": _tn}[mode]
    has_add = add is not None

    def body(*refs):
        a_ref, b_ref = refs[0], refs[1]
        add_ref = refs[2] if has_add else None
        o_ref = refs[3] if has_add else refs[2]
        part = fn(a_ref[...].astype(BF16), b_ref[...].astype(BF16))

        def fin(acc):
            if has_add:
                acc = acc + add_ref[...]
            o_ref[...] = acc.astype(out_dtype)

        if nk == 1:
            fin(part)
        else:
            acc_ref = refs[-1]
            k = pl.program_id(2)

            @pl.when(k == 0)
            def _():
                acc_ref[...] = part

            @pl.when(k > 0)
            def _():
                acc_ref[...] += part

            @pl.when(k == nk - 1)
            def _():
                fin(acc_ref[...])

    a_spec = pl.BlockSpec((tk, tm), lambda i, j, k: (k, i)) if mode == "tn" else pl.BlockSpec((tm, tk), lambda i, j, k: (i, k))
    b_spec = pl.BlockSpec((tn, tk), lambda i, j, k: (j, k)) if mode == "nt" else pl.BlockSpec((tk, tn), lambda i, j, k: (k, j))
    o_spec = pl.BlockSpec((tm, tn), lambda i, j, k: (i, j))
    in_specs, args = [a_spec, b_spec], [a, b]
    if has_add:
        in_specs.append(o_spec)
        args.append(add)
    return pl.pallas_call(
        body, name=name, grid=(M // tm, N // tn, nk), in_specs=in_specs, out_specs=o_spec,
        out_shape=jax.ShapeDtypeStruct((M, N), out_dtype),
        scratch_shapes=[pltpu.VMEM((tm, tn), F32)] if nk > 1 else [],
        compiler_params=_cp(dimension_semantics=("parallel", "parallel", "arbitrary")),
    )(*args)


T_ROW = 512


def _rms_in_fwd(x, g):
    S = x.shape[0]
    T = T_ROW

    def body(x_ref, g_ref, h_ref):
        xv = x_ref[...]
        r = lax.rsqrt(jnp.mean(xv * xv, axis=-1, keepdims=True) + EPS)
        h_ref[...] = (xv * r * g_ref[...]).astype(BF16)

    return pl.pallas_call(
        body, name="rms_in_fwd", grid=(S // T,),
        in_specs=[pl.BlockSpec((T, D), lambda i: (i, 0)), pl.BlockSpec((1, D), lambda i: (0, 0))],
        out_specs=pl.BlockSpec((T, D), lambda i: (i, 0)),
        out_shape=jax.ShapeDtypeStruct((S, D), BF16), compiler_params=_cp(),
    )(x, g)


def _rms_in_bwd(x, g, dh, dres):
    S = x.shape[0]
    T = T_ROW

    def body(x_ref, g_ref, dh_ref, dr_ref, dx_ref, dg_ref):
        i = pl.program_id(0)
        xv = x_ref[...]
        r = lax.rsqrt(jnp.mean(xv * xv, axis=-1, keepdims=True) + EPS)
        xn = xv * r
        dy = dh_ref[...]
        part = _csum(dy * xn)

        @pl.when(i == 0)
        def _():
            dg_ref[...] = part

        @pl.when(i > 0)
        def _():
            dg_ref[...] += part

        dxh = dy * g_ref[...]
        dx_ref[...] = dr_ref[...] + r * (dxh - xn * jnp.mean(dxh * xn, axis=-1, keepdims=True))

    row = pl.BlockSpec((T, D), lambda i: (i, 0))
    vec = pl.BlockSpec((1, D), lambda i: (0, 0))
    return pl.pallas_call(
        body, name="rms_in_bwd", grid=(S // T,), in_specs=[row, vec, row, row], out_specs=[row, vec],
        out_shape=[jax.ShapeDtypeStruct((S, D), F32), jax.ShapeDtypeStruct((1, D), F32)], compiler_params=_cp(),
    )(x, g, dh, dres)


T_LRU = 512


def _neg_expm1(y):
    ser = -y * (1.0 + y * 0.5 * (1.0 + y * (1.0 / 3.0) * (1.0 + y * 0.25 * (1.0 + y * 0.2))))
    return jnp.where(y > -0.03, ser, 1.0 - jnp.exp(y))


def _softplus_neg(lam):
    e = jnp.exp(-jnp.abs(lam))
    l1p = jnp.where(e < 0.01, e * (1.0 - e * (0.5 - e * (1.0 / 3.0 - e * 0.25))), jnp.log(1.0 + e))
    return jnp.maximum(-lam, 0.0) + l1p


def _scan_fwd(a, b, T):
    row = lax.broadcasted_iota(jnp.int32, a.shape, 0)
    d = 1
    while d < T:
        m = row >= d
        b = jnp.where(m, a * pltpu.roll(b, d, 0) + b, b)
        a = jnp.where(m, a * pltpu.roll(a, d, 0), a)
        d *= 2
    return a, b


def _scan_bwd(a, b, T):
    row = lax.broadcasted_iota(jnp.int32, a.shape, 0)
    d = 1
    while d < T:
        m = row < T - d
        b = jnp.where(m, a * pltpu.roll(b, T - d, 0) + b, b)
        a = jnp.where(m, a * pltpu.roll(a, T - d, 0), a)
        d *= 2
    return b


def _lru_common(x, prev, first, cw_ref, cb_ref, wgx_ref, bgx_ref, wga_ref, bga_ref, lam_ref, T):
    row = lax.broadcasted_iota(jnp.int32, x.shape, 0)
    prev = jnp.where(first, 0.0, prev)
    xs = []
    for j in (3, 2, 1):
        pv = jnp.tile(pltpu.roll(prev, j, 0), (T // SUB, 1))
        xs.append(jnp.where(row < j, pv, pltpu.roll(x, j, 0)))
    xs.append(x)
    xc = cb_ref[...] + cw_ref[0:1, :] * xs[0] + cw_ref[1:2, :] * xs[1] + cw_ref[2:3, :] * xs[2] + cw_ref[3:4, :] * xs[3]
    xcb = xc.astype(BF16)
    gx = _sig(_nn(xcb, wgx_ref[0]) + bgx_ref[0])
    ga = _sig(_nn(xcb, wga_ref[0]) + bga_ref[0])
    sp = _softplus_neg(lam_ref[...])
    log_a = -LRU_C * ga * sp
    a = jnp.exp(log_a)
    mult = jnp.sqrt(_neg_expm1(2.0 * log_a))
    return xs, xc, xcb, gx, ga, sp, a, mult


def _lru_specs(T, tmap):
    def at(col0):
        return pl.BlockSpec((T, LANE), lambda n, i: (tmap(i), col0 + n))

    def prev(col0):
        return pl.BlockSpec((SUB, LANE), lambda n, i: (jnp.maximum(tmap(i) * (T // SUB) - 1, 0), col0 + n))

    small = [
        pl.BlockSpec((4, LANE), lambda n, i: (0, n)),
        pl.BlockSpec((1, LANE), lambda n, i: (0, n)),
        pl.BlockSpec((1, LANE, LANE), lambda n, i: (n, 0, 0)),
        pl.BlockSpec((1, 1, LANE), lambda n, i: (n, 0, 0)),
        pl.BlockSpec((1, LANE, LANE), lambda n, i: (n, 0, 0)),
        pl.BlockSpec((1, 1, LANE), lambda n, i: (n, 0, 0)),
        pl.BlockSpec((1, LANE), lambda n, i: (0, n)),
    ]
    return at, prev, small


def _lru_fwd(zp, w):
    S = zp.shape[0]
    T = T_LRU
    at, prev, small = _lru_specs(T, lambda i: i)

    def body(x_ref, xp_ref, g_ref, cw_ref, cb_ref, wgx_ref, bgx_ref, wga_ref, bga_ref, lam_ref, hs_ref, y_ref, carry_ref):
        i = pl.program_id(1)

        @pl.when(i == 0)
        def _():
            carry_ref[...] = jnp.zeros_like(carry_ref)

        x = x_ref[...]
        _, xc, _, gx, _, _, a, mult = _lru_common(x, xp_ref[...], i == 0, cw_ref, cb_ref, wgx_ref, bgx_ref, wga_ref, bga_ref, lam_ref, T)
        A, B = _scan_fwd(a, mult * gx * xc, T)
        h = B + A * carry_ref[SUB - 1:SUB, :]
        hs_ref[...] = h
        carry_ref[...] = hs_ref[T - SUB:T, :]
        y_ref[...] = (h * _silu(g_ref[...])).astype(BF16)

    out = pl.BlockSpec((T, LANE), lambda n, i: (i, n))
    return pl.pallas_call(
        body, name="lru_fwd", grid=(8, S // T),
        in_specs=[at(C_LRUX), prev(C_LRUX), at(C_LRUG)] + small, out_specs=[out, out],
        out_shape=[jax.ShapeDtypeStruct((S, D), F32), jax.ShapeDtypeStruct((S, D), BF16)],
        scratch_shapes=[pltpu.VMEM((SUB, LANE), F32)],
        compiler_params=_cp(dimension_semantics=("parallel", "arbitrary")),
    )(zp, zp, zp, w["conv_w"], w["conv_b"], w["w_gx"], w["b_gx"], w["w_ga"], w["b_ga"], w["lam"])


def _lru_bwd(zp, hs, dy, w, dz):
    S = zp.shape[0]
    T = T_LRU
    nT = S // T
    at, prev, small = _lru_specs(T, lambda i: nT - 1 - i)

    def body(x_ref, xp_ref, g_ref, h_ref, hp_ref, dy_ref, cw_ref, cb_ref, wgx_ref, bgx_ref, wga_ref, bga_ref, lam_ref, dz_in,
             dzx_ref, dcw_ref, dcb_ref, dwgx_ref, dbgx_ref, dwga_ref, dbga_ref, dlam_ref, carry_ref, head_ref):
        del dz_in
        j = pl.program_id(1)
        it = nT - 1 - j

        @pl.when(j == 0)
        def _():
            for r in (carry_ref, head_ref, dcw_ref, dcb_ref, dwgx_ref, dbgx_ref, dwga_ref, dbga_ref, dlam_ref):
                r[...] = jnp.zeros_like(r)

        first = it == 0
        x = x_ref[...]
        xs, xc, xcb, gx, ga, sp, a, mult = _lru_common(x, xp_ref[...], first, cw_ref, cb_ref, wgx_ref, bgx_ref, wga_ref, bga_ref, lam_ref, T)
        row = lax.broadcasted_iota(jnp.int32, x.shape, 0)
        u = gx * xc
        h = h_ref[...]
        hp = jnp.where(first, 0.0, hp_ref[...])
        hm1 = jnp.where(row < 1, jnp.tile(pltpu.roll(hp, 1, 0), (T // SUB, 1)), pltpu.roll(h, 1, 0))
        dho = dy_ref[...] * _silu(g_ref[...])
        gin = jnp.where(row == T - 1, dho + carry_ref[0:1, :], dho)
        abar = jnp.where(row == T - 1, 0.0, pltpu.roll(a, T - 1, 0))
        dh = _scan_bwd(abar, gin, T)
        carry_ref[...] = (a * dh)[0:SUB, :]
        da = dh * hm1
        dmult = dh * u
        du = dh * mult
        dgx = du * xc
        dxc = du * gx
        dlog_a = da * a - dmult * a * a / mult
        dga = dlog_a * (-LRU_C * sp)
        lam = lam_ref[...]
        dlam_ref[...] += _csum(dlog_a * (-LRU_C * ga)) * (-1.0 / (1.0 + jnp.exp(lam)))
        dpa = dga * ga * (1.0 - ga)
        dpx = dgx * gx * (1.0 - gx)
        dpab, dpxb = dpa.astype(BF16), dpx.astype(BF16)
        dxc = dxc + _nt(dpxb, wgx_ref[0]) + _nt(dpab, wga_ref[0])
        dwgx_ref[0] += _tn(xcb, dpxb)
        dwga_ref[0] += _tn(xcb, dpab)
        dbgx_ref[0] += _csum(dpx)
        dbga_ref[0] += _csum(dpa)
        dcb_ref[...] += _csum(dxc)
        for k in range(4):
            dcw_ref[k:k + 1, :] += _csum(dxc * xs[k])
        head = head_ref[...]
        dx = cw_ref[3:4, :] * dxc
        for jj in (1, 2, 3):
            hv = jnp.tile(pltpu.roll(head, SUB - jj, 0), (T // SUB, 1))
            dx = dx + cw_ref[3 - jj:4 - jj, :] * jnp.where(row >= T - jj, hv, pltpu.roll(dxc, T - jj, 0))
        head_ref[...] = dxc[0:SUB, :]
        dzx_ref[...] = dx.astype(BF16)

    def acc(shape, imap):
        return pl.BlockSpec(shape, imap)

    out_specs = [
        pl.BlockSpec((T, LANE), lambda n, i: (nT - 1 - i, C_LRUX + n)),
        acc((4, LANE), lambda n, i: (0, n)), acc((1, LANE), lambda n, i: (0, n)),
        acc((1, LANE, LANE), lambda n, i: (n, 0, 0)), acc((1, 1, LANE), lambda n, i: (n, 0, 0)),
        acc((1, LANE, LANE), lambda n, i: (n, 0, 0)), acc((1, 1, LANE), lambda n, i: (n, 0, 0)),
        acc((1, LANE), lambda n, i: (0, n)),
    ]
    out_shape = [
        jax.ShapeDtypeStruct(dz.shape, BF16),
        jax.ShapeDtypeStruct((4, D), F32), jax.ShapeDtypeStruct((1, D), F32),
        jax.ShapeDtypeStruct((8, LANE, LANE), F32), jax.ShapeDtypeStruct((8, 1, LANE), F32),
        jax.ShapeDtypeStruct((8, LANE, LANE), F32), jax.ShapeDtypeStruct((8, 1, LANE), F32),
        jax.ShapeDtypeStruct((1, D), F32),
    ]
    dyspec = pl.BlockSpec((T, LANE), lambda n, i: (nT - 1 - i, n))
    hprev = pl.BlockSpec((SUB, LANE), lambda n, i: (jnp.maximum((nT - 1 - i) * (T // SUB) - 1, 0), n))
    return pl.pallas_call(
        body, name="lru_bwd", grid=(8, nT),
        in_specs=[at(C_LRUX), prev(C_LRUX), at(C_LRUG), dyspec, hprev, dyspec] + small + [pl.BlockSpec(memory_space=pl.ANY)],
        out_specs=out_specs, out_shape=out_shape,
        scratch_shapes=[pltpu.VMEM((SUB, LANE), F32), pltpu.VMEM((SUB, LANE), F32)],
        input_output_aliases={13: 0},
        compiler_params=_cp(dimension_semantics=("parallel", "arbitrary")),
    )(zp, zp, zp, hs, hs, dy, w["conv_w"], w["conv_b"], w["w_gx"], w["b_gx"], w["w_ga"], w["b_ga"], w["lam"], dz)


def _lru_gate_bwd(zp, hs, dy, dz):
    S = zp.shape[0]
    T = T_ROW

    def body(g_ref, h_ref, dy_ref, dz_in, o_ref):
        del dz_in
        o_ref[...] = (dy_ref[...] * h_ref[...] * _dsilu(g_ref[...])).astype(BF16)

    row = pl.BlockSpec((T, D), lambda i: (i, 0))
    zc = pl.BlockSpec((T, D), lambda i: (i, C_LRUG // 8))
    return pl.pallas_call(
        body, name="lru_gate_bwd", grid=(S // T,), in_specs=[zc, row, row, pl.BlockSpec(memory_space=pl.ANY)], out_specs=zc,
        out_shape=jax.ShapeDtypeStruct(dz.shape, BF16), input_output_aliases={3: 0}, compiler_params=_cp(),
    )(zp, hs, dy, dz)


def _rope_tables(pos):
    pf = pos.astype(F32)[:, None]

    def cs(d):
        inv = ROPE_THETA ** (-jnp.arange(0, d, 2, dtype=F32) / d)
        ang = pf * inv
        return jnp.cos(ang), jnp.sin(ang)

    S = pos.shape[0]
    c, s = cs(32)
    one, zero = jnp.ones((S, 64), F32), jnp.zeros((S, 16), F32)
    z32, z64 = jnp.zeros((S, 32), F32), jnp.zeros((S, 64), F32)
    mla = (jnp.concatenate([one, c, c, jnp.ones((S, 32), F32)], 1),
           jnp.concatenate([z64, zero, s, z32], 1),
           jnp.concatenate([z64, -s, zero, z32], 1))
    c, s = cs(64)
    dil = (jnp.concatenate([c, c, c, c], 1),
           jnp.concatenate([z32, s, z32, s], 1),
           jnp.concatenate([-s, z32, -s, z32], 1))
    return mla, dil


def _rope(x, C, S1, S2, sh):
    return x * C + pltpu.roll(x, sh, 1) * S1 + pltpu.roll(x, LANE - sh, 1) * S2


def _rope_t(dy, C, S1, S2, sh):
    return dy * C + pltpu.roll(dy * S1, LANE - sh, 1) + pltpu.roll(dy * S2, sh, 1)


def _lane(shape):
    return lax.broadcasted_iota(jnp.int32, shape, 1)


T_MLA = 256
TA = 512


def _zcol(T, width, col_lanes):
    assert (col_lanes * LANE) % width == 0
    return pl.BlockSpec((T, width), lambda i: (i, col_lanes * LANE // width))


def _full(shape):
    return pl.BlockSpec(shape, lambda *_: (0,) * len(shape))


def _mla_pre_fwd(zp, w, tab):
    S = zp.shape[0]
    T = T_MLA

    def body(cq_ref, ckv_ref, kr_ref, gcq_ref, gckv_ref, wuq_ref, wuk_ref, wuv_ref, gq_ref, gk_ref, C_ref, S1_ref, S2_ref,
             q_ref, k_ref, v_ref):
        cq = cq_ref[...]
        cqn = (cq * lax.rsqrt(jnp.mean(cq * cq, axis=-1, keepdims=True) + EPS) * gcq_ref[...]).astype(BF16)
        ckv = ckv_ref[...]
        ckvn = (ckv * lax.rsqrt(jnp.mean(ckv * ckv, axis=-1, keepdims=True) + EPS) * gckv_ref[...]).astype(BF16)
        q0 = _nn(cqn, wuq_ref[...])
        k0 = _nn(ckvn, wuk_ref[...])
        krb = kr_ref[...]
        C, S1, S2 = C_ref[...], S1_ref[...], S2_ref[...]
        for h in range(8):
            sl = slice(h * LANE, (h + 1) * LANE)
            xq = q0[:, sl]
            xq = xq * lax.rsqrt(_rsum(xq * xq) * (1.0 / MLA_QK) + EPS) * gq_ref[...]
            q_ref[:, sl] = _rope(xq, C, S1, S2, 16).astype(BF16)
            xk = k0[:, sl] + krb
            xk = xk * lax.rsqrt(_rsum(xk * xk) * (1.0 / MLA_QK) + EPS) * gk_ref[...]
            k_ref[:, sl] = _rope(xk, C, S1, S2, 16).astype(BF16)
        v_ref[...] = _nn(ckvn, wuv_ref[...]).astype(BF16)

    tabspec = pl.BlockSpec((T, LANE), lambda i: (i, 0))
    in_specs = [_zcol(T, 256, C_CQ), _zcol(T, LANE, C_CKV), _zcol(T, LANE, C_KR), _full((1, 256)), _full((1, LANE)),
                _full((256, 1024)), _full((LANE, 1024)), _full((LANE, 512)), _full((1, LANE)), _full((1, LANE)),
                tabspec, tabspec, tabspec]
    return pl.pallas_call(
        body, name="mla_pre_fwd", grid=(S // T,), in_specs=in_specs,
        out_specs=[pl.BlockSpec((T, 1024), lambda i: (i, 0)), pl.BlockSpec((T, 1024), lambda i: (i, 0)), pl.BlockSpec((T, 512), lambda i: (i, 0))],
        out_shape=[jax.ShapeDtypeStruct((S, 1024), BF16), jax.ShapeDtypeStruct((S, 1024), BF16), jax.ShapeDtypeStruct((S, 512), BF16)],
        compiler_params=_cp(),
    )(zp, zp, zp, w["g_cq"], w["g_ckv"], w["w_uq"], w["w_uk"], w["w_uv"], w["g_mq"], w["g_mk"], *tab)


def _mla_attn_fwd(q, k, v, zp):
    S = q.shape[0]
    nq = S // TA

    def body(q_ref, k_ref, v_ref, g_ref, o_ref, lse_ref, y_ref):
        qi = pl.program_id(1)
        lane = _lane((TA, LANE))
        rowi = lax.broadcasted_iota(jnp.int32, (TA, TA), 0)
        coli = lax.broadcasted_iota(jnp.int32, (TA, TA), 1)
        o_tot = jnp.zeros((TA, LANE), F32)
        for hh in range(2):
            cs = slice(hh * LANE, (hh + 1) * LANE)
            hm = (lane < 64) if hh == 0 else (lane >= 64)
            qh = q_ref[:, cs]

            def step(kb, carry, masked, cs=cs, hm=hm, qh=qh):
                m, l, acc = carry
                off = pl.multiple_of(kb * TA, TA)
                kh = k_ref[pl.ds(off, TA), cs]
                vv = v_ref[pl.ds(off, TA), :]
                vh = jnp.where(hm, vv, jnp.zeros_like(vv))
                s = _nt(qh, kh) * MLA_SCALE
                if masked:
                    s = jnp.where(rowi >= coli, s, NEG)
                m_new = jnp.maximum(m, jnp.max(s, axis=-1, keepdims=True))
                alpha = jnp.exp(m - m_new)
                p = jnp.exp(s - m_new)
                l = alpha * l + _rsum(p)
                acc = alpha * acc + _nn(p.astype(BF16), vh)
                return m_new, l, acc

            init = (jnp.full((TA, 1), NEG, F32), jnp.zeros((TA, 1), F32), jnp.zeros((TA, LANE), F32))
            carry = lax.fori_loop(0, qi, lambda kb, c: step(kb, c, False), init)
            m, l, acc = step(qi, carry, True)
            o_tot = o_tot + acc / l
            lse_ref[:, cs] = jnp.broadcast_to(m + jnp.log(l), (TA, LANE))
        o_ref[...] = o_tot
        y_ref[...] = (o_tot * _silu(g_ref[...])).astype(BF16)

    blk = pl.BlockSpec((TA, LANE), lambda p, i: (i, p))
    return pl.pallas_call(
        body, name="mla_attn_fwd", grid=(4, nq),
        in_specs=[pl.BlockSpec((TA, 256), lambda p, i: (i, p)), pl.BlockSpec((S, 256), lambda p, i: (0, p)),
                  pl.BlockSpec((S, LANE), lambda p, i: (0, p)), pl.BlockSpec((TA, LANE), lambda p, i: (i, C_MLAG + p))],
        out_specs=[blk, pl.BlockSpec((TA, 256), lambda p, i: (i, p)), blk],
        out_shape=[jax.ShapeDtypeStruct((S, 512), F32), jax.ShapeDtypeStruct((S, 1024), F32), jax.ShapeDtypeStruct((S, 512), BF16)],
        compiler_params=_cp(dimension_semantics=("parallel", "arbitrary")),
    )(q, k, v, zp)


def _mla_post_bwd(zp, o, dy, dz):
    S = zp.shape[0]
    T = T_ROW

    def body(g_ref, o_ref, dy_ref, dz_in, dz_ref, do_ref, D_ref):
        del dz_in
        g, o_, dy_ = g_ref[...], o_ref[...], dy_ref[...]
        do = dy_ * _silu(g)
        do_ref[...] = do.astype(BF16)
        dz_ref[...] = (dy_ * o_ * _dsilu(g)).astype(BF16)
        prod = do * o_
        lane = _lane((T, LANE))
        for p in range(4):
            pr = prod[:, p * LANE:(p + 1) * LANE]
            da = _rsum(jnp.where(lane < 64, pr, 0.0))
            db = _rsum(jnp.where(lane >= 64, pr, 0.0))
            D_ref[:, 2 * p * LANE:(2 * p + 1) * LANE] = jnp.broadcast_to(da, (T, LANE))
            D_ref[:, (2 * p + 1) * LANE:(2 * p + 2) * LANE] = jnp.broadcast_to(db, (T, LANE))

    row = pl.BlockSpec((T, 512), lambda i: (i, 0))
    zc = _zcol(T, 512, C_MLAG)
    return pl.pallas_call(
        body, name="mla_post_bwd", grid=(S // T,), in_specs=[zc, row, row, pl.BlockSpec(memory_space=pl.ANY)],
        out_specs=[zc, row, pl.BlockSpec((T, 1024), lambda i: (i, 0))],
        out_shape=[jax.ShapeDtypeStruct(dz.shape, BF16), jax.ShapeDtypeStruct((S, 512), BF16), jax.ShapeDtypeStruct((S, 1024), F32)],
        input_output_aliases={3: 0}, compiler_params=_cp(),
    )(zp, o, dy, dz)


def _mla_attn_bwd(q, k, v, do, lse, Dr):
    S = q.shape[0]
    nq = S // TA

    def body(q_ref, do_ref, lse_ref, D_ref, k_ref, v_ref, dq_ref, dk_ref, dv_ref):
        ki = pl.program_id(1)

        @pl.when(ki == 0)
        def _():
            dq_ref[...] = jnp.zeros_like(dq_ref)

        lane = _lane((TA, LANE))
        rowi = lax.broadcasted_iota(jnp.int32, (TA, TA), 0)
        coli = lax.broadcasted_iota(jnp.int32, (TA, TA), 1)
        dv_tot = jnp.zeros((TA, LANE), F32)
        for hh in range(2):
            cs = slice(hh * LANE, (hh + 1) * LANE)
            hm = (lane < 64) if hh == 0 else (lane >= 64)
            kh = k_ref[:, cs]
            vv = v_ref[...]
            vm = jnp.where(hm, vv, jnp.zeros_like(vv))

            def step(qb, carry, masked, cs=cs, kh=kh, vm=vm):
                dk_acc, dv_acc = carry
                off = pl.multiple_of(qb * TA, TA)
                qh = q_ref[pl.ds(off, TA), cs]
                doh = do_ref[pl.ds(off, TA), :]
                ls = jnp.tile(lse_ref[pl.ds(off, TA), cs], (1, TA // LANE))
                dd = jnp.tile(D_ref[pl.ds(off, TA), cs], (1, TA // LANE))
                s = _nt(qh, kh) * MLA_SCALE
                if masked:
                    s = jnp.where(rowi >= coli, s, NEG)
                p = jnp.exp(s - ls)
                dp = _nt(doh, vm)
                ds = (p * (dp - dd) * MLA_SCALE).astype(BF16)
                dv_acc = dv_acc + _tn(p.astype(BF16), doh)
                dk_acc = dk_acc + _tn(ds, qh)
                dq_ref[pl.ds(off, TA), cs] += _nn(ds, kh)
                return dk_acc, dv_acc

            z = jnp.zeros((TA, LANE), F32)
            carry = step(ki, (z, z), True)
            dk_acc, dv_acc = lax.fori_loop(ki + 1, nq, lambda qb, c: step(qb, c, False), carry)
            dk_ref[:, cs] = dk_acc
            dv_tot = dv_tot + jnp.where(hm, dv_acc, 0.0)
        dv_ref[...] = dv_tot

    pair = pl.BlockSpec((S, 256), lambda p, i: (0, p))
    return pl.pallas_call(
        body, name="mla_attn_bwd", grid=(4, nq),
        in_specs=[pair, pl.BlockSpec((S, LANE), lambda p, i: (0, p)), pair, pair,
                  pl.BlockSpec((TA, 256), lambda p, i: (i, p)), pl.BlockSpec((TA, LANE), lambda p, i: (i, p))],
        out_specs=[pair, pl.BlockSpec((TA, 256), lambda p, i: (i, p)), pl.BlockSpec((TA, LANE), lambda p, i: (i, p))],
        out_shape=[jax.ShapeDtypeStruct((S, 1024), F32), jax.ShapeDtypeStruct((S, 1024), F32), jax.ShapeDtypeStruct((S, 512), F32)],
        compiler_params=_cp(dimension_semantics=("parallel", "arbitrary")),
    )(q, do, lse, Dr, k, v)


def _mla_pre_bwd(zp, dq, dk, dv, w, tab, dz):
    S = zp.shape[0]
    T = T_MLA

    def body(cq_ref, ckv_ref, kr_ref, dq_ref, dk_ref, dv_ref, gcq_ref, gckv_ref, wuq_ref, wuk_ref, wuv_ref, gq_ref, gk_ref,
             C_ref, S1_ref, S2_ref, dz_in, dz_ref, dwuq_ref, dwuk_ref, dwuv_ref, dgcq_ref, dgckv_ref, dgq_ref, dgk_ref):
        del dz_in
        i = pl.program_id(0)

        @pl.when(i == 0)
        def _():
            for r in (dwuq_ref, dwuk_ref, dwuv_ref, dgcq_ref, dgckv_ref, dgq_ref, dgk_ref):
                r[...] = jnp.zeros_like(r)

        cq = cq_ref[...]
        rq = lax.rsqrt(jnp.mean(cq * cq, axis=-1, keepdims=True) + EPS)
        cqh = cq * rq
        cqn = (cqh * gcq_ref[...]).astype(BF16)
        ckv = ckv_ref[...]
        rkv = lax.rsqrt(jnp.mean(ckv * ckv, axis=-1, keepdims=True) + EPS)
        ckvh = ckv * rkv
        ckvn = (ckvh * gckv_ref[...]).astype(BF16)
        q0 = _nn(cqn, wuq_ref[...])
        k0 = _nn(ckvn, wuk_ref[...])
        krb = kr_ref[...]
        C, S1, S2 = C_ref[...], S1_ref[...], S2_ref[...]
        gq, gk = gq_ref[...], gk_ref[...]

        def head_bwd(x, dy, g):
            r = lax.rsqrt(_rsum(x * x) * (1.0 / MLA_QK) + EPS)
            xn = x * r
            dyn = _rope_t(dy, C, S1, S2, 16)
            dxh = dyn * g
            return r * (dxh - xn * _rsum(dxh * xn) * (1.0 / MLA_QK)), _csum(dyn * xn)

        dq0, dk0 = [], []
        dgq_acc = jnp.zeros((1, LANE), F32)
        dgk_acc = jnp.zeros((1, LANE), F32)
        dkr = jnp.zeros((T, LANE), F32)
        for h in range(8):
            sl = slice(h * LANE, (h + 1) * LANE)
            dxq, gq_p = head_bwd(q0[:, sl], dq_ref[:, sl], gq)
            dxk, gk_p = head_bwd(k0[:, sl] + krb, dk_ref[:, sl], gk)
            dq0.append(dxq.astype(BF16))
            dk0.append(dxk.astype(BF16))
            dkr = dkr + dxk
            dgq_acc = dgq_acc + gq_p
            dgk_acc = dgk_acc + gk_p
        dgq_ref[...] += dgq_acc
        dgk_ref[...] += dgk_acc
        dq0 = jnp.concatenate(dq0, axis=1)
        dk0 = jnp.concatenate(dk0, axis=1)
        dvb = dv_ref[...].astype(BF16)
        dwuq_ref[...] += _tn(cqn, dq0)
        dwuk_ref[...] += _tn(ckvn, dk0)
        dwuv_ref[...] += _tn(ckvn, dvb)
        dcqn = _nt(dq0, wuq_ref[...])
        dckvn = _nt(dk0, wuk_ref[...]) + _nt(dvb, wuv_ref[...])
        dgcq_ref[...] += _csum(dcqn * cqh)
        dgckv_ref[...] += _csum(dckvn * ckvh)
        dxh = dcqn * gcq_ref[...]
        dz_ref[:, 0:256] = (rq * (dxh - cqh * jnp.mean(dxh * cqh, axis=-1, keepdims=True))).astype(BF16)
        dxh = dckvn * gckv_ref[...]
        dz_ref[:, 256:384] = (rkv * (dxh - ckvh * jnp.mean(dxh * ckvh, axis=-1, keepdims=True))).astype(BF16)
        lane = _lane((T, LANE))
        dz_ref[:, 384:512] = jnp.where((lane >= KR_LANE) & (lane < KR_LANE + 32), dkr, 0.0).astype(BF16)

    tabspec = pl.BlockSpec((T, LANE), lambda i: (i, 0))
    in_specs = [_zcol(T, 256, C_CQ), _zcol(T, LANE, C_CKV), _zcol(T, LANE, C_KR),
                pl.BlockSpec((T, 1024), lambda i: (i, 0)), pl.BlockSpec((T, 1024), lambda i: (i, 0)), pl.BlockSpec((T, 512), lambda i: (i, 0)),
                _full((1, 256)), _full((1, LANE)), _full((256, 1024)), _full((LANE, 1024)), _full((LANE, 512)), _full((1, LANE)), _full((1, LANE)),
                tabspec, tabspec, tabspec, pl.BlockSpec(memory_space=pl.ANY)]
    out_specs = [_zcol(T, 512, C_CQ), _full((256, 1024)), _full((LANE, 1024)), _full((LANE, 512)), _full((1, 256)), _full((1, LANE)),
                 _full((1, LANE)), _full((1, LANE))]
    out_shape = [jax.ShapeDtypeStruct(dz.shape, BF16), jax.ShapeDtypeStruct((256, 1024), F32), jax.ShapeDtypeStruct((LANE, 1024), F32),
                 jax.ShapeDtypeStruct((LANE, 512), F32), jax.ShapeDtypeStruct((1, 256), F32), jax.ShapeDtypeStruct((1, LANE), F32),
                 jax.ShapeDtypeStruct((1, LANE), F32), jax.ShapeDtypeStruct((1, LANE), F32)]
    return pl.pallas_call(
        body, name="mla_pre_bwd", grid=(S // T,), in_specs=in_specs, out_specs=out_specs, out_shape=out_shape,
        input_output_aliases={16: 0}, compiler_params=_cp(),
    )(zp, zp, zp, dq, dk, dv, w["g_cq"], w["g_ckv"], w["w_uq"], w["w_uk"], w["w_uv"], w["g_mq"], w["g_mk"], *tab, dz)


T_DIL = 256


def _head_stats(x, lane):
    sq = x * x
    sa = _rsum(jnp.where(lane < 64, sq, 0.0))
    sb = _rsum(jnp.where(lane >= 64, sq, 0.0))
    return lax.rsqrt(jnp.where(lane < 64, sa, sb) * (1.0 / DIL_HD) + EPS)


def _head_sum(x, lane):
    sa = _rsum(jnp.where(lane < 64, x, 0.0))
    sb = _rsum(jnp.where(lane >= 64, x, 0.0))
    return jnp.where(lane < 64, sa, sb)


def _dil_pre_fwd(zp, w, tab):
    S = zp.shape[0]
    T = T_DIL

    def body(q_ref, k_ref, gq_ref, gk_ref, C_ref, S1_ref, S2_ref, qo_ref, ko_ref):
        C, S1, S2 = C_ref[...], S1_ref[...], S2_ref[...]
        lane = _lane((T, LANE))
        for b in range(12):
            sl = slice(b * LANE, (b + 1) * LANE)
            x = q_ref[:, sl]
            qo_ref[:, sl] = _rope(x * _head_stats(x, lane) * gq_ref[...], C, S1, S2, 32)
            x = k_ref[:, sl]
            ko_ref[:, sl] = _rope(x * _head_stats(x, lane) * gk_ref[...], C, S1, S2, 32)

    tabspec = pl.BlockSpec((T, LANE), lambda i: (i, 0))
    out = pl.BlockSpec((T, 1536), lambda i: (i, 0))
    return pl.pallas_call(
        body, name="dil_pre_fwd", grid=(S // T,),
        in_specs=[_zcol(T, 1536, C_DQ), _zcol(T, 1536, C_DK), _full((1, LANE)), _full((1, LANE)), tabspec, tabspec, tabspec],
        out_specs=[out, out], out_shape=[jax.ShapeDtypeStruct((S, 1536), F32)] * 2, compiler_params=_cp(),
    )(zp, zp, w["g_dq"], w["g_dk"], *tab)


DIL_ROWS = 2048


def _dil_geometry(g, S):
    d = DIL_DILATIONS[g]
    P = NK * d
    return d, P, DIL_ROWS // P, S // P


def _dil_rows(start, d):
    return pl.ds(pl.multiple_of(start, NK), NK) if d == 1 else pl.ds(start, NK, stride=d)


def _dil_specs(g, S, col0):
    _, P, m, nb = _dil_geometry(g, S)
    cur = pl.BlockSpec((DIL_ROWS, LANE), lambda sb, c: (sb, col0 + c))
    prv = pl.BlockSpec((P, LANE), lambda sb, c: (jnp.maximum(sb * m - 1, 0), col0 + c))
    nxt = pl.BlockSpec((P, LANE), lambda sb, c: (jnp.minimum((sb + 1) * m, nb - 1), col0 + c))
    return cur, prv, nxt


def _dil_masks(n, nb):
    row = lax.broadcasted_iota(jnp.int32, (NK, NK), 0)
    col = lax.broadcasted_iota(jnp.int32, (NK, NK), 1)
    return col <= row, (col >= row) & (n > 0), (col >= row) & (n < nb - 1)


def _dil_attn_fwd(q, k, zp, g):
    S = q.shape[0]
    d, P, m, nb = _dil_geometry(g, S)
    R = DIL_ROWS

    def body(q_ref, kc_ref, kp_ref, vc_ref, vp_ref, o_ref, lse_ref, *scr):
        sb = pl.program_id(0)
        if m > 1:
            ks_ref, vs_ref = scr
            ks_ref[0:P, :] = kp_ref[...]
            ks_ref[P:P + R, :] = kc_ref[...]
            vs_ref[0:P, :] = vp_ref[...]
            vs_ref[P:P + R, :] = vc_ref[...]
        lane = _lane((NK, LANE))

        def unit(u, carry):
            j = u // d
            rows = _dil_rows(j * P + (u - j * d), d)
            if m > 1:
                rows_c = _dil_rows(j * P + (u - j * d) + P, d)
                kp, kc, vp, vc = ks_ref[rows, :], ks_ref[rows_c, :], vs_ref[rows, :], vs_ref[rows_c, :]
            else:
                kp, kc, vp, vc = kp_ref[rows, :], kc_ref[rows, :], vp_ref[rows, :], vc_ref[rows, :]
            kp, kc, vp, vc = kp.astype(BF16), kc.astype(BF16), vp.astype(BF16), vc.astype(BF16)
            q_ = q_ref[rows, :].astype(BF16)
            mc, mp, _ = _dil_masks(sb * m + j, nb)
            zb = jnp.zeros_like(q_)
            o_tot = jnp.zeros((NK, LANE), F32)
            lse_tot = jnp.zeros((NK, LANE), F32)
            for hh in range(2):
                hm = (lane < 64) if hh == 0 else (lane >= 64)
                qm = jnp.where(hm, q_, zb)
                sc = jnp.where(mc, _nt(qm, kc) * DIL_SCALE, NEG)
                sp = jnp.where(mp, _nt(qm, kp) * DIL_SCALE, NEG)
                mx = jnp.maximum(jnp.max(sc, axis=-1, keepdims=True), jnp.max(sp, axis=-1, keepdims=True))
                ec = jnp.exp(sc - mx)
                ep = jnp.exp(sp - mx)
                den = _rsum(ec) + _rsum(ep)
                o = (_nn(ec.astype(BF16), jnp.where(hm, vc, zb)) + _nn(ep.astype(BF16), jnp.where(hm, vp, zb))) / den
                o_tot = o_tot + o
                lse_tot = jnp.where(hm, mx + jnp.log(den), lse_tot)
            o_ref[rows, :] = o_tot
            lse_ref[rows, :] = lse_tot
            return carry

        lax.fori_loop(0, R // NK, unit, 0)

    qcur, qprv, _ = _dil_specs(g, S, 4 * g)
    vcur, vprv, _ = _dil_specs(g, S, C_DV + 4 * g)
    out = pl.BlockSpec((R, LANE), lambda sb, c: (sb, c))
    return pl.pallas_call(
        body, name=f"dil_attn_fwd{g}", grid=(S // R, 4), in_specs=[qcur, qcur, qprv, vcur, vprv], out_specs=[out, out],
        out_shape=[jax.ShapeDtypeStruct((S, 512), F32)] * 2,
        scratch_shapes=[pltpu.VMEM((P + R, LANE), F32)] * 2 if m > 1 else [], compiler_params=_cp(),
    )(q, k, k, zp, zp)


def _dil_combine(os_, ls_, zp):
    S = zp.shape[0]
    T = T_ROW

    def body(o0, o1, o2, l0, l1, l2, g_ref, oc_ref, L_ref, y_ref):
        a, b, c = l0[...], l1[...], l2[...]
        mx = jnp.maximum(jnp.maximum(a, b), c)
        ea, eb, ec = jnp.exp(a - mx), jnp.exp(b - mx), jnp.exp(c - mx)
        den = ea + eb + ec
        oc = (ea * o0[...] + eb * o1[...] + ec * o2[...]) / den
        oc_ref[...] = oc
        L_ref[...] = mx + jnp.log(den)
        y_ref[...] = (oc * _silu(g_ref[...])).astype(BF16)

    row = pl.BlockSpec((T, 512), lambda i: (i, 0))
    return pl.pallas_call(
        body, name="dil_combine", grid=(S // T,), in_specs=[row] * 6 + [_zcol(T, 512, C_DILG)], out_specs=[row, row, row],
        out_shape=[jax.ShapeDtypeStruct((S, 512), F32), jax.ShapeDtypeStruct((S, 512), F32), jax.ShapeDtypeStruct((S, 512), BF16)],
        compiler_params=_cp(),
    )(*os_, *ls_, zp)


def _dil_comb_bwd(zp, oc, dy, dz):
    S = zp.shape[0]
    T = T_ROW

    def body(g_ref, o_ref, dy_ref, dz_in, dz_ref, do_ref, D_ref):
        del dz_in
        g, o_, dy_ = g_ref[...], o_ref[...], dy_ref[...]
        do = dy_ * _silu(g)
        do_ref[...] = do
        dz_ref[...] = (dy_ * o_ * _dsilu(g)).astype(BF16)
        lane = _lane((T, LANE))
        for p in range(4):
            sl = slice(p * LANE, (p + 1) * LANE)
            D_ref[:, sl] = _head_sum(do[:, sl] * o_[:, sl], lane)

    row = pl.BlockSpec((T, 512), lambda i: (i, 0))
    zc = _zcol(T, 512, C_DILG)
    return pl.pallas_call(
        body, name="dil_comb_bwd", grid=(S // T,), in_specs=[zc, row, row, pl.BlockSpec(memory_space=pl.ANY)], out_specs=[zc, row, row],
        out_shape=[jax.ShapeDtypeStruct(dz.shape, BF16), jax.ShapeDtypeStruct((S, 512), F32), jax.ShapeDtypeStruct((S, 512), F32)],
        input_output_aliases={3: 0}, compiler_params=_cp(),
    )(zp, oc, dy, dz)


def _dil_attn_bwd(q, k, zp, do, L, Dr, g):
    S = q.shape[0]
    d, P, m, nb = _dil_geometry(g, S)
    R = DIL_ROWS
    n_q, n_k = 4, 2

    def body(*refs):
        q_side = refs[0:2 * n_q]
        k_side = refs[2 * n_q:2 * n_q + 2 * n_k]
        dq_ref, dk_ref, dv_ref = refs[2 * n_q + 2 * n_k:2 * n_q + 2 * n_k + 3]
        scr = refs[2 * n_q + 2 * n_k + 3:]
        sb = pl.program_id(0)
        if m > 1:
            for a in range(n_q):
                scr[a][0:R, :] = q_side[2 * a][...]
                scr[a][R:R + P, :] = q_side[2 * a + 1][...]
            for a in range(n_k):
                scr[n_q + a][0:P, :] = k_side[2 * a + 1][...]
                scr[n_q + a][P:P + R, :] = k_side[2 * a][...]
        lane = _lane((NK, LANE))

        def unit(u, carry):
            j = u // d
            start = j * P + (u - j * d)
            rows = _dil_rows(start, d)
            if m > 1:
                rows_b = _dil_rows(start + P, d)
                qc, doc, Lc_, Dc_ = [scr[a][rows, :] for a in range(n_q)]
                qn, don, Ln_, Dn_ = [scr[a][rows_b, :] for a in range(n_q)]
                kp, vp = [scr[n_q + a][rows, :] for a in range(n_k)]
                kc, vc = [scr[n_q + a][rows_b, :] for a in range(n_k)]
            else:
                qc, doc, Lc_, Dc_ = [q_side[2 * a][rows, :] for a in range(n_q)]
                qn, don, Ln_, Dn_ = [q_side[2 * a + 1][rows, :] for a in range(n_q)]
                kc, vc = [k_side[2 * a][rows, :] for a in range(n_k)]
                kp, vp = [k_side[2 * a + 1][rows, :] for a in range(n_k)]
            qc, qn, doc, don = qc.astype(BF16), qn.astype(BF16), doc.astype(BF16), don.astype(BF16)
            kc, kp, vc, vp = kc.astype(BF16), kp.astype(BF16), vc.astype(BF16), vp.astype(BF16)
            mc, mp, mnext = _dil_masks(sb * m + j, nb)
            zb = jnp.zeros_like(qc)
            dq_tot = jnp.zeros((NK, LANE), F32)
            dk_tot = jnp.zeros((NK, LANE), F32)
            dv_tot = jnp.zeros((NK, LANE), F32)
            for hh in range(2):
                hm = (lane < 64) if hh == 0 else (lane >= 64)

                def bcast(x, hm=hm):
                    return jnp.where(hm, x, pltpu.roll(x, 64, 1))

                Lc, Ln, Dc, Dn = bcast(Lc_), bcast(Ln_), bcast(Dc_), bcast(Dn_)
                qm = jnp.where(hm, qc, zb)
                qnm = jnp.where(hm, qn, zb)
                vcm = jnp.where(hm, vc, zb)
                vpm = jnp.where(hm, vp, zb)
                pc = jnp.exp(jnp.where(mc, _nt(qm, kc) * DIL_SCALE, NEG) - Lc)
                pp = jnp.exp(jnp.where(mp, _nt(qm, kp) * DIL_SCALE, NEG) - Lc)
                dsc = (pc * (_nt(doc, vcm) - Dc) * DIL_SCALE).astype(BF16)
                dsp = (pp * (_nt(doc, vpm) - Dc) * DIL_SCALE).astype(BF16)
                dq_tot = dq_tot + jnp.where(hm, _nn(dsc, kc) + _nn(dsp, kp), 0.0)
                p2 = jnp.exp(jnp.where(mnext, _nt(qnm, kc) * DIL_SCALE, NEG) - Ln)
                ds2 = (p2 * (_nt(don, vcm) - Dn) * DIL_SCALE).astype(BF16)
                dk_tot = dk_tot + _tn(dsc, qm) + _tn(ds2, qnm)
                dv_tot = dv_tot + jnp.where(hm, _tn(pc.astype(BF16), doc) + _tn(p2.astype(BF16), don), 0.0)
            dq_ref[rows, :] = dq_tot
            dk_ref[rows, :] = dk_tot
            dv_ref[rows, :] = dv_tot
            return carry

        lax.fori_loop(0, R // NK, unit, 0)

    qcur, qprv, qnxt = _dil_specs(g, S, 4 * g)
    vcur, vprv, _ = _dil_specs(g, S, C_DV + 4 * g)
    ocur, _, onxt = _dil_specs(g, S, 0)
    out = pl.BlockSpec((R, LANE), lambda sb, c: (sb, c))
    scratch = [pltpu.VMEM((P + R, LANE), F32)] * (n_q + n_k) if m > 1 else []
    return pl.pallas_call(
        body, name=f"dil_attn_bwd{g}", grid=(S // R, 4),
        in_specs=[qcur, qnxt, ocur, onxt, ocur, onxt, ocur, onxt, qcur, qprv, vcur, vprv],
        out_specs=[out, out, out], out_shape=[jax.ShapeDtypeStruct((S, 512), F32)] * 3, scratch_shapes=scratch, compiler_params=_cp(),
    )(q, q, do, do, L, L, Dr, Dr, k, k, zp, zp)


def _dil_pre_bwd(zp, dys, g, tab, dz, col, name):
    S = zp.shape[0]
    T = T_DIL

    def body(x_ref, dy0_ref, dy1_ref, dy2_ref, g_ref, C_ref, S1_ref, S2_ref, dz_in, dz_ref, dg_ref):
        del dz_in
        i = pl.program_id(0)
        C, S1, S2 = C_ref[...], S1_ref[...], S2_ref[...]
        lane = _lane((T, LANE))
        gv = g_ref[...]
        acc = jnp.zeros((1, LANE), F32)
        for b in range(12):
            sl = slice(b * LANE, (b + 1) * LANE)
            x = x_ref[:, sl]
            r = _head_stats(x, lane)
            xn = x * r
            dy_ref = (dy0_ref, dy1_ref, dy2_ref)[b // 4]
            dyn = _rope_t(dy_ref[:, (b % 4) * LANE:(b % 4 + 1) * LANE], C, S1, S2, 32)
            acc = acc + _csum(dyn * xn)
            dxh = dyn * gv
            dz_ref[:, sl] = (r * (dxh - xn * _head_sum(dxh * xn, lane) * (1.0 / DIL_HD))).astype(BF16)

        @pl.when(i == 0)
        def _():
            dg_ref[...] = acc

        @pl.when(i > 0)
        def _():
            dg_ref[...] += acc

    tabspec = pl.BlockSpec((T, LANE), lambda i: (i, 0))
    zc = _zcol(T, 1536, col)
    grp = pl.BlockSpec((T, 512), lambda i: (i, 0))
    return pl.pallas_call(
        body, name=name, grid=(S // T,),
        in_specs=[zc, grp, grp, grp, _full((1, LANE)), tabspec, tabspec, tabspec, pl.BlockSpec(memory_space=pl.ANY)],
        out_specs=[zc, _full((1, LANE))], out_shape=[jax.ShapeDtypeStruct(dz.shape, BF16), jax.ShapeDtypeStruct((1, LANE), F32)],
        input_output_aliases={8: 0}, compiler_params=_cp(),
    )(zp, *dys, g, *tab, dz)


def _dil_dv_into(dvs, dz):
    S = dz.shape[0]
    T = T_ROW

    def body(s0, s1, s2, dz_in, o_ref):
        del dz_in
        for gi, s in enumerate((s0, s1, s2)):
            o_ref[:, gi * 512:(gi + 1) * 512] = s[...].astype(BF16)

    grp = pl.BlockSpec((T, 512), lambda i: (i, 0))
    return pl.pallas_call(
        body, name="dil_dv", grid=(S // T,), in_specs=[grp, grp, grp, pl.BlockSpec(memory_space=pl.ANY)],
        out_specs=_zcol(T, 1536, C_DV), out_shape=jax.ShapeDtypeStruct(dz.shape, BF16), input_output_aliases={3: 0}, compiler_params=_cp(),
    )(*dvs, dz)


T_MRG = 256


def _merge_fwd(P, zp, b_merge):
    S = zp.shape[0]
    T = T_MRG

    def body(p0, p1, p2, m0, m1, m2, b_ref, o_ref):
        acc = jnp.zeros((T, D), F32)
        for j, (p, m) in enumerate(((p0, m0), (p1, m1), (p2, m2))):
            acc = acc + _sig(m[...] + b_ref[:, j * D:(j + 1) * D]) * p[...]
        o_ref[...] = acc.astype(BF16)

    row = pl.BlockSpec((T, D), lambda i: (i, 0))
    return pl.pallas_call(
        body, name="merge_fwd", grid=(S // T,),
        in_specs=[row, row, row] + [_zcol(T, D, C_MERGE + 8 * j) for j in range(3)] + [_full((1, 3 * D))], out_specs=row,
        out_shape=jax.ShapeDtypeStruct((S, D), BF16), compiler_params=_cp(),
    )(*P, zp, zp, zp, b_merge)


def _merge_bwd(dm, Pj, zp, bj, dz, j):
    S = zp.shape[0]
    T = T_MRG

    def body(dm_ref, p_ref, m_ref, b_ref, dz_in, dz_ref, dp_ref, db_ref):
        del dz_in
        i = pl.program_id(0)
        g = _sig(m_ref[...] + b_ref[...])
        dmv = dm_ref[...]
        dp_ref[...] = (dmv * g).astype(BF16)
        dg = dmv * p_ref[...] * g * (1.0 - g)
        dz_ref[...] = dg.astype(BF16)
        part = _csum(dg)

        @pl.when(i == 0)
        def _():
            db_ref[...] = part

        @pl.when(i > 0)
        def _():
            db_ref[...] += part

    row = pl.BlockSpec((T, D), lambda i: (i, 0))
    zc = _zcol(T, D, C_MERGE + 8 * j)
    return pl.pallas_call(
        body, name=f"merge_bwd{j}", grid=(S // T,), in_specs=[row, row, zc, _full((1, D)), pl.BlockSpec(memory_space=pl.ANY)],
        out_specs=[zc, row, _full((1, D))],
        out_shape=[jax.ShapeDtypeStruct(dz.shape, BF16), jax.ShapeDtypeStruct((S, D), BF16), jax.ShapeDtypeStruct((1, D), F32)],
        input_output_aliases={4: 0}, compiler_params=_cp(),
    )(dm, Pj, zp, bj, dz)


def _loss_fwd_bwd(y, target):
    S = y.shape[0]
    T = T_ROW

    def body(y_ref, t_ref, loss_ref, dy_ref):
        i = pl.program_id(0)
        err = y_ref[...] - t_ref[...]
        dy_ref[...] = err * (1.0 / D)
        part = jnp.sum(err * err, keepdims=True).reshape(1, 1) * (0.5 / D)

        @pl.when(i == 0)
        def _():
            loss_ref[...] = part

        @pl.when(i > 0)
        def _():
            loss_ref[...] += part

    row = pl.BlockSpec((T, D), lambda i: (i, 0))
    return pl.pallas_call(
        body, name="loss", grid=(S // T,), in_specs=[row, row], out_specs=[_full((1, 1)), row],
        out_shape=[jax.ShapeDtypeStruct((1, 1), F32), jax.ShapeDtypeStruct((S, D), F32)], compiler_params=_cp(),
    )(y, target)


def _layer_fwd(x, w, tabs):
    mla_tab, dil_tab = tabs
    S = x.shape[0]
    h = _rms_in_fwd(x, w["norm_g"])
    zp = _mm(h, w["w_in"], mode="nn", name="in_proj")
    hs, y_lru = _lru_fwd(zp, w)
    q, k, v = _mla_pre_fwd(zp, w, mla_tab)
    o_mla, lse, y_mla = _mla_attn_fwd(q, k, v, zp)
    qd, kd = _dil_pre_fwd(zp, w, dil_tab)
    og, lg = zip(*[_dil_attn_fwd(qd, kd, zp, g) for g in range(len(DIL_DILATIONS))])
    oc, L, y_dil = _dil_combine(og, lg, zp)
    P = [_mm(y_lru, w["w_lru_o"], mode="nn", name="lru_out"), _mm(y_mla, w["w_mla_o"], mode="nn", name="mla_out"),
         _mm(y_dil, w["w_dil_o"], mode="nn", name="dil_out")]
    merged = _merge_fwd(P, zp, w["b_merge"])
    x_out = _mm(merged, w["w_out"], mode="nn", name="out_proj", add=x)
    saved = dict(x=x, h=h, zp=zp, hs=hs, y=(y_lru, y_mla, y_dil), q=q, k=k, v=v, o_mla=o_mla, lse=lse, qd=qd, kd=kd, oc=oc, L=L, P=P,
                 merged=merged)
    return x_out, saved


def _layer_bwd(dout, w, tabs, sv):
    mla_tab, dil_tab = tabs
    zp = sv["zp"]
    S = zp.shape[0]
    g = {}
    dm = _mm(dout, w["w_out"], mode="nt", name="d_merged")
    g["w_out"] = _mm(sv["merged"], dout, mode="tn", name="dw_out", out_dtype=BF16)
    dz = lax.empty((S, ZW), BF16)
    dP, db = [], []
    for j in range(3):
        dz, dpj, dbj = _merge_bwd(dm, sv["P"][j], zp, w["b_merge"][:, j * D:(j + 1) * D], dz, j)
        dP.append(dpj)
        db.append(dbj)
    g["b_merge"] = jnp.concatenate(db, axis=1)
    names = ("w_lru_o", "w_mla_o", "w_dil_o")
    dy = []
    for j in range(3):
        dy.append(_mm(dP[j], w[names[j]], mode="nt", name="dy_" + names[j]))
        g[names[j]] = _mm(sv["y"][j], dP[j], mode="tn", name="d" + names[j], out_dtype=BF16)
    dz = _lru_gate_bwd(zp, sv["hs"], dy[0], dz)
    dz, g["conv_w"], g["conv_b"], g["w_gx"], g["b_gx"], g["w_ga"], g["b_ga"], g["lam"] = _lru_bwd(zp, sv["hs"], dy[0], w, dz)
    dz, do, Dr = _mla_post_bwd(zp, sv["o_mla"], dy[1], dz)
    dq, dk, dv = _mla_attn_bwd(sv["q"], sv["k"], sv["v"], do, sv["lse"], Dr)
    dz, g["w_uq"], g["w_uk"], g["w_uv"], g["g_cq"], g["g_ckv"], g["g_mq"], g["g_mk"] = _mla_pre_bwd(zp, dq, dk, dv, w, mla_tab, dz)
    dz, dod, Dd = _dil_comb_bwd(zp, sv["oc"], dy[2], dz)
    dqs, dks, dvs = zip(*[_dil_attn_bwd(sv["qd"], sv["kd"], zp, dod, sv["L"], Dd, gi) for gi in range(len(DIL_DILATIONS))])
    dz, g["g_dq"] = _dil_pre_bwd(zp, dqs, w["g_dq"], dil_tab, dz, C_DQ, "dil_pre_bwd_q")
    dz, g["g_dk"] = _dil_pre_bwd(zp, dks, w["g_dk"], dil_tab, dz, C_DK, "dil_pre_bwd_k")
    dz = _dil_dv_into(dvs, dz)
    dh = _mm(dz, w["w_in"], mode="nt", name="d_h")
    g["w_in"] = _mm(sv["h"], dz, mode="tn", name="dw_in", out_dtype=BF16)
    dx, g["norm_g"] = _rms_in_bwd(sv["x"], w["norm_g"], dh, dout)
    return dx, g


def _peers():
    mx, my, mc = lax.axis_index("x"), lax.axis_index("y"), lax.axis_index("c")
    me = 4 * mx + 2 * my + mc
    out = []
    for k in range(1, N_DEV):
        px = 1 - mx if k & 4 else mx
        py = 1 - my if k & 2 else my
        pc = 1 - mc if k & 1 else mc
        out.append(((px, py, pc), 4 * px + 2 * py + pc))
    return me, out


def _gather_all(arrs, name):
    n = len(arrs)

    def body(*refs):
        ins, outs = refs[:n], refs[n:2 * n]
        send_sems, recv_sems, local_sems = refs[2 * n:]
        me, peers = _peers()
        mine = [pltpu.make_async_copy(ins[a], outs[a].at[me], local_sems.at[a]) for a in range(n)]
        for cp in mine:
            cp.start()
        copies = []
        for k, (peer, _) in enumerate(peers):
            for a in range(n):
                cp = pltpu.make_async_remote_copy(
                    src_ref=ins[a], dst_ref=outs[a].at[me], send_sem=send_sems.at[k * n + a], recv_sem=recv_sems.at[k * n + a],
                    device_id=peer, device_id_type=pl.DeviceIdType.MESH)
                cp.start()
                copies.append(cp)
        for cp in copies + mine:
            cp.wait()

    nsem = (N_DEV - 1) * n
    return pl.pallas_call(
        body, name=name, out_shape=[jax.ShapeDtypeStruct((N_DEV,) + a.shape, a.dtype) for a in arrs],
        in_specs=[pl.BlockSpec(memory_space=pl.ANY)] * n, out_specs=[pl.BlockSpec(memory_space=pl.ANY)] * n,
        scratch_shapes=[pltpu.SemaphoreType.DMA((nsem,)), pltpu.SemaphoreType.DMA((nsem,)), pltpu.SemaphoreType.DMA((n,))],
        compiler_params=pltpu.CompilerParams(has_side_effects=True),
    )(*arrs)


WIN = 13 * LANE


def _win_base(s):
    n = s * SHARD_IN
    a0 = n + jnp.where(n >= _KR0, KR_LANE, 0) + jnp.where(n >= _KR0 + 32, 32, 0)
    return jnp.minimum(a0 // LANE, (ZW - WIN) // LANE)


def _win_offsets(s):
    n = s * SHARD_IN + jnp.arange(SHARD_IN)
    o = s * SHARD_IN - _win_base(s) * LANE
    return n, (o, o + KR_LANE, o + LANE - 32)


def _to_window(shard, s):
    n, offs = _win_offsets(s)
    masks = (n < _KR0, (n >= _KR0) & (n < _KR0 + 32), n >= _KR0 + 32)
    zero = jnp.zeros(shard.shape[:2] + (WIN,), shard.dtype)
    out = zero
    for m, o in zip(masks, offs):
        out = out + lax.dynamic_update_slice(zero, jnp.where(m[None, None, :], shard, jnp.zeros_like(shard)), (0, 0, o))
    return out


def _from_window(win, s):
    n, offs = _win_offsets(s)
    a, b, c = [lax.dynamic_slice(win, (0, 0, o), win.shape[:2] + (SHARD_IN,)) for o in offs]
    return jnp.where((n < _KR0)[None, None, :], a, jnp.where((n < _KR0 + 32)[None, None, :], b, c))


def _win_base_static(s):
    n = s * SHARD_IN
    a0 = n + (KR_LANE if n >= _KR0 else 0) + (32 if n >= _KR0 + 32 else 0)
    return min(a0 // LANE, (ZW - WIN) // LANE)


def _assemble_w_in(gw):
    tr = 128
    bases = [_win_base_static(s) for s in range(N_DEV)]

    def body(g_ref, o_ref):
        for j in range(ZW // LANE):
            acc = None
            for s in range(N_DEV):
                if bases[s] <= j < bases[s] + WIN // LANE:
                    piece = g_ref[s, 0, :, (j - bases[s]) * LANE:(j - bases[s] + 1) * LANE]
                    acc = piece if acc is None else acc + piece
            o_ref[0, :, j * LANE:(j + 1) * LANE] = acc

    return pl.pallas_call(
        body, name="assemble_w_in", grid=(DEPTH, D // tr),
        in_specs=[pl.BlockSpec((N_DEV, 1, tr, WIN), lambda l, i: (0, l, i, 0))],
        out_specs=pl.BlockSpec((1, tr, ZW), lambda l, i: (l, i, 0)),
        out_shape=jax.ShapeDtypeStruct((DEPTH, D, ZW), gw.dtype), compiler_params=_cp(),
    )(gw)


def _cols(width):
    return lambda ref, p: ref.at[:, pl.ds(pl.multiple_of(p * width, width), width)]


def _rows(height):
    return lambda ref, p: ref.at[pl.ds(pl.multiple_of(p * height, height), height), :]


SCATTER = {
    'w_in': (lambda ref, p: ref.at[:, pl.ds(pl.multiple_of(_win_base(p) * LANE, LANE), WIN)], (D, WIN), BF16),
    'conv_w': (_cols(LANE), (4, LANE), F32),
    'w_lru_o': (_rows(LANE), (LANE, D), BF16),
    'w_uq': (_cols(LANE), (256, LANE), F32),
    'w_ukv': (_cols(LANE), (128, LANE), F32),
    'w_mla_o': (_cols(LANE), (512, LANE), BF16),
    'w_dil_o': (_cols(LANE), (512, LANE), BF16),
    'w_out': (_rows(LANE), (LANE, D), BF16),
}


def _scatter_grads(glayers):
    names = list(SCATTER)
    nw = len(names)
    arrs = [glayers[l][nm] for l in range(DEPTH) for nm in names]

    def body(*refs):
        ins, outs = refs[:DEPTH * nw], refs[DEPTH * nw:DEPTH * nw + nw]
        send_sems, recv_sems, local_sems = refs[DEPTH * nw + nw:]
        me, peers = _peers()
        mine, copies = [], []
        for l in range(DEPTH):
            for a, nm in enumerate(names):
                cp = pltpu.make_async_copy(SCATTER[nm][0](ins[l * nw + a], me), outs[a].at[me, l], local_sems.at[l * nw + a])
                cp.start()
                mine.append(cp)
        for k, (peer, pidx) in enumerate(peers):
            for l in range(DEPTH):
                for a, nm in enumerate(names):
                    i = (k * DEPTH + l) * nw + a
                    cp = pltpu.make_async_remote_copy(
                        src_ref=SCATTER[nm][0](ins[l * nw + a], pidx), dst_ref=outs[a].at[me, l], send_sem=send_sems.at[i],
                        recv_sem=recv_sems.at[i], device_id=peer, device_id_type=pl.DeviceIdType.MESH)
                    cp.start()
                    copies.append(cp)
        for cp in copies + mine:
            cp.wait()

    nsem = (N_DEV - 1) * DEPTH * nw
    outs = pl.pallas_call(
        body, name="scatter_grads", out_shape=[jax.ShapeDtypeStruct((N_DEV, DEPTH) + SCATTER[nm][1], SCATTER[nm][2]) for nm in names],
        in_specs=[pl.BlockSpec(memory_space=pl.ANY)] * (DEPTH * nw), out_specs=[pl.BlockSpec(memory_space=pl.ANY)] * nw,
        scratch_shapes=[pltpu.SemaphoreType.DMA((nsem,)), pltpu.SemaphoreType.DMA((nsem,)), pltpu.SemaphoreType.DMA((DEPTH * nw,))],
        compiler_params=pltpu.CompilerParams(has_side_effects=True),
    )(*arrs)
    return dict(zip(names, outs))


PACK_ROWS = 512


def _sum8(buf, name):
    _, R, C = buf.shape
    tr = R
    while tr * C * 4 * N_DEV > (1 << 22) and tr % 16 == 0:
        tr //= 2

    def body(b_ref, o_ref):
        acc = b_ref[0].astype(F32)
        for s in range(1, N_DEV):
            acc = acc + b_ref[s].astype(F32)
        o_ref[...] = acc

    return pl.pallas_call(
        body, name=name, grid=(R // tr,), in_specs=[pl.BlockSpec((N_DEV, tr, C), lambda i: (0, i, 0))],
        out_specs=pl.BlockSpec((tr, C), lambda i: (i, 0)), out_shape=jax.ShapeDtypeStruct((R, C), F32), compiler_params=_cp(),
    )(buf)


def _pack(arrs, dtype, lead):
    flat = [a.astype(dtype).reshape(a.shape[:lead] + (-1,)) for a in arrs]
    cat = jnp.concatenate(flat, axis=-1)
    n = cat.shape[-1]
    unit = PACK_ROWS * LANE
    pad = (-n) % unit
    if pad:
        cat = jnp.pad(cat, [(0, 0)] * lead + [(0, pad)])
    return cat.reshape(cat.shape[:lead] + ((n + pad) // LANE, LANE))


def _unpack(buf, shapes, lead):
    flat = buf.reshape(buf.shape[:lead] + (-1,))
    out, off = [], 0
    for shp in shapes:
        n = int(np.prod(shp))
        out.append(flat[..., off:off + n].reshape(buf.shape[:lead] + tuple(shp)))
        off += n
    return out


def _adamw(w, g, m, v, name):
    rows, cols = w.shape
    tr = rows
    while tr * cols * 4 > (3 << 19) and tr % 16 == 0:
        tr //= 2
    c1 = 1.0 - ADAM_B1 ** ADAM_STEP
    c2 = 1.0 - ADAM_B2 ** ADAM_STEP

    def body(w_ref, g_ref, m_ref, v_ref, d_ref, mo_ref, vo_ref):
        gv = g_ref[...]
        mn = ADAM_B1 * m_ref[...] + (1.0 - ADAM_B1) * gv
        vn = ADAM_B2 * v_ref[...] + (1.0 - ADAM_B2) * (gv * gv)
        mo_ref[...] = mn
        vo_ref[...] = vn
        d_ref[...] = -ADAM_LR * ((mn / c1) / (jnp.sqrt(vn / c2) + ADAM_EPS) + ADAM_WD * w_ref[...])

    spec = pl.BlockSpec((tr, cols), lambda i: (i, 0))
    return pl.pallas_call(
        body, name=name, grid=(rows // tr,), in_specs=[spec] * 4, out_specs=[spec] * 3,
        out_shape=[jax.ShapeDtypeStruct((rows, cols), F32)] * 3, compiler_params=_cp(),
    )(w, g, m, v)


IN_NAMES = ['x', 'positions', 'norm_g', 'w_in', 'conv_w', 'conv_b', 'w_gate_x', 'b_gate_x', 'w_gate_a', 'b_gate_a', 'lru_lambda', 'w_lru_o',
            'cq_norm_g', 'ckv_norm_g', 'w_uq', 'w_ukv', 'mla_q_norm_g', 'mla_k_norm_g', 'w_mla_o', 'dil_q_norm_g', 'dil_k_norm_g', 'w_dil_o',
            'b_merge', 'w_out']
WEIGHTS = IN_NAMES[2:]
REPLICATED = [n for n in WEIGHTS if n not in SCATTER]

_KR0 = C_KR * LANE


def _layer_weights(gw, rep, l):
    by_rows = lambda a: a[:, l].reshape(-1, a.shape[-1])
    by_cols = lambda a: jnp.swapaxes(a[:, l], 0, 1).reshape(a.shape[2], -1)
    ukv = jnp.swapaxes(gw['w_ukv'][:, l], 0, 1)
    g96 = lambda a: jnp.pad(a[l].reshape(1, MLA_QK), ((0, 0), (0, LANE - MLA_QK)))
    g64 = lambda a: jnp.tile(a[l].reshape(1, DIL_HD), (1, 2))
    return dict(
        norm_g=rep['norm_g'][l].reshape(1, D), w_in=gw['w_in'][l],
        conv_w=by_cols(gw['conv_w']), conv_b=rep['conv_b'][l].reshape(1, D),
        w_gx=rep['w_gate_x'][l].astype(BF16), b_gx=rep['b_gate_x'][l].reshape(8, 1, LANE),
        w_ga=rep['w_gate_a'][l].astype(BF16), b_ga=rep['b_gate_a'][l].reshape(8, 1, LANE),
        lam=rep['lru_lambda'][l].reshape(1, D),
        w_lru_o=by_rows(gw['w_lru_o']), w_mla_o=by_cols(gw['w_mla_o']), w_dil_o=by_cols(gw['w_dil_o']), w_out=by_rows(gw['w_out']),
        g_cq=rep['cq_norm_g'][l].reshape(1, 256), g_ckv=rep['ckv_norm_g'][l].reshape(1, 128),
        w_uq=by_cols(gw['w_uq']), w_uk=jnp.pad(ukv[:, :, :64], ((0, 0), (0, 0), (0, 64))).reshape(128, 1024),
        w_uv=ukv[:, :, 64:].reshape(128, 512),
        g_mq=g96(rep['mla_q_norm_g']), g_mk=g96(rep['mla_k_norm_g']), g_dq=g64(rep['dil_q_norm_g']), g_dk=g64(rep['dil_k_norm_g']),
        b_merge=rep['b_merge'][l].reshape(1, 3 * D),
    )


def _sharded_grads(g):
    uk = g['w_uk'].reshape(128, 8, 128)[:, :, :64]
    uv = g['w_uv'].reshape(128, 8, 64)
    return {'w_in': g['w_in'], 'conv_w': g['conv_w'], 'w_lru_o': g['w_lru_o'], 'w_uq': g['w_uq'],
            'w_ukv': jnp.concatenate([uk, uv], axis=-1).reshape(128, 1024), 'w_mla_o': g['w_mla_o'], 'w_dil_o': g['w_dil_o'],
            'w_out': g['w_out']}


def _replicated_grads(g):
    return {
        'norm_g': g['norm_g'].reshape(D), 'conv_b': g['conv_b'].reshape(D),
        'w_gate_x': g['w_gx'], 'b_gate_x': g['b_gx'].reshape(8, LANE), 'w_gate_a': g['w_ga'], 'b_gate_a': g['b_ga'].reshape(8, LANE),
        'lru_lambda': g['lam'].reshape(D), 'cq_norm_g': g['g_cq'].reshape(256), 'ckv_norm_g': g['g_ckv'].reshape(128),
        'mla_q_norm_g': g['g_mq'][0, :MLA_QK], 'mla_k_norm_g': g['g_mk'][0, :MLA_QK],
        'dil_q_norm_g': g['g_dq'][0, :DIL_HD] + g['g_dq'][0, DIL_HD:], 'dil_k_norm_g': g['g_dk'][0, :DIL_HD] + g['g_dk'][0, DIL_HD:],
        'b_merge': g['b_merge'].reshape(3 * D),
    }


def _device_step(x, pos, target, ws):
    tabs = _rope_tables(pos)
    saved = []
    for l in range(DEPTH):
        x, sv = _layer_fwd(x, ws[l], tabs)
        saved.append(sv)
    loss, dx = _loss_fwd_bwd(x, target)
    grads = [None] * DEPTH
    for l in reversed(range(DEPTH)):
        dx, grads[l] = _layer_bwd(dx, ws[l], tabs, saved[l])
    return loss[0, 0], dx, grads


def kernel(x, positions, norm_g, w_in, conv_w, conv_b, w_gate_x, b_gate_x, w_gate_a, b_gate_a, lru_lambda, w_lru_o, cq_norm_g, ckv_norm_g, w_uq, w_ukv, mla_q_norm_g, mla_k_norm_g, w_mla_o, dil_q_norm_g, dil_k_norm_g, w_dil_o, b_merge, w_out, loss_target, m_norm_g, m_w_in, m_conv_w, m_conv_b, m_w_gate_x, m_b_gate_x, m_w_gate_a, m_b_gate_a, m_lru_lambda, m_w_lru_o, m_cq_norm_g, m_ckv_norm_g, m_w_uq, m_w_ukv, m_mla_q_norm_g, m_mla_k_norm_g, m_w_mla_o, m_dil_q_norm_g, m_dil_k_norm_g, m_w_dil_o, m_b_merge, m_w_out, v_norm_g, v_w_in, v_conv_w, v_conv_b, v_w_gate_x, v_b_gate_x, v_w_gate_a, v_b_gate_a, v_lru_lambda, v_w_lru_o, v_cq_norm_g, v_ckv_norm_g, v_w_uq, v_w_ukv, v_mla_q_norm_g, v_mla_k_norm_g, v_w_mla_o, v_dil_q_norm_g, v_dil_k_norm_g, v_w_dil_o, v_b_merge, v_w_out):
    args = (x, positions, norm_g, w_in, conv_w, conv_b, w_gate_x, b_gate_x, w_gate_a, b_gate_a, lru_lambda, w_lru_o, cq_norm_g, ckv_norm_g, w_uq, w_ukv, mla_q_norm_g, mla_k_norm_g, w_mla_o, dil_q_norm_g, dil_k_norm_g, w_dil_o, b_merge, w_out)
    moments_m = (m_norm_g, m_w_in, m_conv_w, m_conv_b, m_w_gate_x, m_b_gate_x, m_w_gate_a, m_b_gate_a, m_lru_lambda, m_w_lru_o, m_cq_norm_g, m_ckv_norm_g, m_w_uq, m_w_ukv, m_mla_q_norm_g, m_mla_k_norm_g, m_w_mla_o, m_dil_q_norm_g, m_dil_k_norm_g, m_w_dil_o, m_b_merge, m_w_out)
    moments_v = (v_norm_g, v_w_in, v_conv_w, v_conv_b, v_w_gate_x, v_b_gate_x, v_w_gate_a, v_b_gate_a, v_lru_lambda, v_w_lru_o, v_cq_norm_g, v_ckv_norm_g, v_w_uq, v_w_ukv, v_mla_q_norm_g, v_mla_k_norm_g, v_w_mla_o, v_dil_q_norm_g, v_dil_k_norm_g, v_w_dil_o, v_b_merge, v_w_out)
    a = dict(zip(IN_NAMES, args))
    wd = {n: a[n] for n in WEIGHTS}
    md = dict(zip(WEIGHTS, moments_m))
    vd = dict(zip(WEIGHTS, moments_v))

    me = 4 * lax.axis_index("x") + 2 * lax.axis_index("y") + lax.axis_index("c")

    order = ['w_in', 'w_lru_o', 'w_uq', 'w_ukv', 'w_mla_o', 'w_dil_o', 'w_out', 'conv_w']
    local = {n: wd[n].astype(BF16) for n in order[:-1]}
    local['w_in'] = _to_window(local['w_in'], me)
    local['w_uq'] = jnp.pad(local['w_uq'], ((0, 0), (0, 0), (0, LANE - MLA_QK)))
    local['conv_w'] = wd['conv_w']
    gw = dict(zip(order, _gather_all([local[n] for n in order], "gather_weights")))
    gw['w_in'] = _assemble_w_in(gw['w_in'])
    ws = [_layer_weights(gw, wd, l) for l in range(DEPTH)]

    loss, grad_x, glayers = _device_step(x[0], positions[0], loss_target[0], ws)

    sharded = list(SCATTER)
    gsh = {}
    for n, buf in _scatter_grads([_sharded_grads(g) for g in glayers]).items():
        r, c = buf.shape[2:]
        gsh[n] = _sum8(buf.reshape(N_DEV, DEPTH * r, c), "sum_" + n).reshape(DEPTH, r, c)
    gsh['w_in'] = _from_window(gsh['w_in'], me)
    gsh['w_uq'] = gsh['w_uq'][:, :, :MLA_QK]
    rshapes = [wd[n].shape for n in REPLICATED]
    rgrads = [_replicated_grads(g) for g in glayers]
    rpacked = _pack([jnp.stack([rgrads[l][n] for l in range(DEPTH)]) for n in REPLICATED], F32, 0)
    grep = _sum8(_gather_all([rpacked], "gather_grads")[0], "sum_replicated")

    out_g, out_d, out_m, out_v = {}, {}, {}, {}
    d_, m_, v_ = _adamw(_pack([wd[n] for n in REPLICATED], F32, 0), grep, _pack([md[n] for n in REPLICATED], F32, 0),
                        _pack([vd[n] for n in REPLICATED], F32, 0), "adamw_replicated")
    for dst, buf in ((out_g, grep), (out_d, d_), (out_m, m_), (out_v, v_)):
        dst.update(zip(REPLICATED, _unpack(buf, rshapes, 0)))
    for n in sharded:
        shp = wd[n].shape
        two = (shp[0] * shp[1], shp[2])
        d_, m_, v_ = _adamw(wd[n].reshape(two), gsh[n].reshape(two), md[n].reshape(two), vd[n].reshape(two), "adamw_" + n)
        out_g[n], out_d[n], out_m[n], out_v[n] = gsh[n], d_.reshape(shp), m_.reshape(shp), v_.reshape(shp)

    loss = lax.psum(loss, ("x", "y", "c"))
    return (loss, grad_x[None], *[out_g[n] for n in WEIGHTS], *[out_d[n] for n in WEIGHTS], *[out_m[n] for n in WEIGHTS],
            *[out_v[n] for n in WEIGHTS])
```

```python
import functools

import numpy as np
import jax
import jax.numpy as jnp
from jax import lax
from jax.experimental import pallas as pl
from jax.experimental.pallas import tpu as pltpu

F32 = jnp.float32
BF16 = jnp.bfloat16

N_DEV = 8
D = 1024
DEPTH = 2
EPS = 1e-6
ROPE_THETA = 10000.0
LRU_C = 8.0
LANE = 128
SUB = 8
IN_WIDTH = 11168
SHARD_IN = IN_WIDTH // N_DEV

C_LRUX, C_LRUG, C_CQ, C_CKV, C_KR, C_MLAG, C_DQ, C_DK, C_DV, C_DILG, C_MERGE = 0, 8, 16, 18, 19, 20, 24, 36, 48, 60, 64
ZW = 88 * LANE
KR_LANE = 64

MLA_QK = 96
MLA_SCALE = MLA_QK ** -0.5
DIL_HD = 64
DIL_SCALE = DIL_HD ** -0.5
DIL_DILATIONS = (1, 4, 16)
NK = 128

ADAM_LR, ADAM_B1, ADAM_B2, ADAM_EPS, ADAM_WD, ADAM_STEP = 0.001, 0.9, 0.999, 1e-08, 0.01, 10

NEG = -1e30
VMEM_LIMIT = 48 * 1024 * 1024


def _cp(**kw):
    return pltpu.CompilerParams(vmem_limit_bytes=VMEM_LIMIT, **kw)


def _sig(x):
    return 1.0 / (1.0 + jnp.exp(-x))


def _silu(x):
    return x * _sig(x)


def _dsilu(x):
    s = _sig(x)
    return s * (1.0 + x * (1.0 - s))


def _dot(a, b, dims):
    return lax.dot_general(a, b, (dims, ((), ())), preferred_element_type=F32)


def _nn(a, b):
    return _dot(a, b, ((1,), (0,)))


def _nt(a, b):
    return _dot(a, b, ((1,), (1,)))


def _tn(a, b):
    return _dot(a, b, ((0,), (0,)))


def _rsum(x):
    return jnp.sum(x, axis=-1, keepdims=True)


def _csum(x):
    return jnp.sum(x, axis=0, keepdims=True)


def _mm(a, b, *, mode, name, out_dtype=F32, add=None, tm=1024, tn=1024, tk=1024):
    if mode == "nn":
        (M, K), (K2, N) = a.shape, b.shape
    elif mode == "nt":
        (M, K), (N, K2) = a.shape, b.shape
    else:
        (K, M), (K2, N) = a.shape, b.shape
    assert K == K2
    tm, tn, tk = min(tm, M), min(tn, N), min(tk, K)
    assert M % tm == 0 and N % tn == 0 and K % tk == 0
    nk = K // tk
    fn = {"nn": _nn, "nt": _nt, "tn": _tn}[mode]
    has_add = add is not None

    def body(*refs):
        a_ref, b_ref = refs[0], refs[1]
        add_ref = refs[2] if has_add else None
        o_ref = refs[3] if has_add else refs[2]
        part = fn(a_ref[...].astype(BF16), b_ref[...].astype(BF16))

        def fin(acc):
            if has_add:
                acc = acc + add_ref[...]
            o_ref[...] = acc.astype(out_dtype)

        if nk == 1:
            fin(part)
        else:
            acc_ref = refs[-1]
            k = pl.program_id(2)

            @pl.when(k == 0)
            def _():
                acc_ref[...] = part

            @pl.when(k > 0)
            def _():
                acc_ref[...] += part

            @pl.when(k == nk - 1)
            def _():
                fin(acc_ref[...])

    a_spec = pl.BlockSpec((tk, tm), lambda i, j, k: (k, i)) if mode == "tn" else pl.BlockSpec((tm, tk), lambda i, j, k: (i, k))
    b_spec = pl.BlockSpec((tn, tk), lambda i, j, k: (j, k)) if mode == "nt" else pl.BlockSpec((tk, tn), lambda i, j, k: (k, j))
    o_spec = pl.BlockSpec((tm, tn), lambda i, j, k: (i, j))
    in_specs, args = [a_spec, b_spec], [a, b]
    if has_add:
        in_specs.append(o_spec)
        args.append(add)
    return pl.pallas_call(
        body, name=name, grid=(M // tm, N // tn, nk), in_specs=in_specs, out_specs=o_spec,
        out_shape=jax.ShapeDtypeStruct((M, N), out_dtype),
        scratch_shapes=[pltpu.VMEM((tm, tn), F32)] if nk > 1 else [],
        compiler_params=_cp(dimension_semantics=("parallel", "parallel", "arbitrary")),
    )(*args)


T_ROW = 512


def _rms_in_fwd(x, g):
    S = x.shape[0]
    T = T_ROW

    def body(x_ref, g_ref, h_ref):
        xv = x_ref[...]
        r = lax.rsqrt(jnp.mean(xv * xv, axis=-1, keepdims=True) + EPS)
        h_ref[...] = (xv * r * g_ref[...]).astype(BF16)

    return pl.pallas_call(
        body, name="rms_in_fwd", grid=(S // T,),
        in_specs=[pl.BlockSpec((T, D), lambda i: (i, 0)), pl.BlockSpec((1, D), lambda i: (0, 0))],
        out_specs=pl.BlockSpec((T, D), lambda i: (i, 0)),
        out_shape=jax.ShapeDtypeStruct((S, D), BF16), compiler_params=_cp(),
    )(x, g)


def _rms_in_bwd(x, g, dh, dres):
    S = x.shape[0]
    T = T_ROW

    def body(x_ref, g_ref, dh_ref, dr_ref, dx_ref, dg_ref):
        i = pl.program_id(0)
        xv = x_ref[...]
        r = lax.rsqrt(jnp.mean(xv * xv, axis=-1, keepdims=True) + EPS)
        xn = xv * r
        dy = dh_ref[...]
        part = _csum(dy * xn)

        @pl.when(i == 0)
        def _():
            dg_ref[...] = part

        @pl.when(i > 0)
        def _():
            dg_ref[...] += part

        dxh = dy * g_ref[...]
        dx_ref[...] = dr_ref[...] + r * (dxh - xn * jnp.mean(dxh * xn, axis=-1, keepdims=True))

    row = pl.BlockSpec((T, D), lambda i: (i, 0))
    vec = pl.BlockSpec((1, D), lambda i: (0, 0))
    return pl.pallas_call(
        body, name="rms_in_bwd", grid=(S // T,), in_specs=[row, vec, row, row], out_specs=[row, vec],
        out_shape=[jax.ShapeDtypeStruct((S, D), F32), jax.ShapeDtypeStruct((1, D), F32)], compiler_params=_cp(),
    )(x, g, dh, dres)


T_LRU = 512


def _neg_expm1(y):
    ser = -y * (1.0 + y * 0.5 * (1.0 + y * (1.0 / 3.0) * (1.0 + y * 0.25 * (1.0 + y * 0.2))))
    return jnp.where(y > -0.03, ser, 1.0 - jnp.exp(y))


def _softplus_neg(lam):
    e = jnp.exp(-jnp.abs(lam))
    l1p = jnp.where(e < 0.01, e * (1.0 - e * (0.5 - e * (1.0 / 3.0 - e * 0.25))), jnp.log(1.0 + e))
    return jnp.maximum(-lam, 0.0) + l1p


def _scan_fwd(a, b, T):
    row = lax.broadcasted_iota(jnp.int32, a.shape, 0)
    d = 1
    while d < T:
        m = row >= d
        b = jnp.where(m, a * pltpu.roll(b, d, 0) + b, b)
        a = jnp.where(m, a * pltpu.roll(a, d, 0), a)
        d *= 2
    return a, b


def _scan_bwd(a, b, T):
    row = lax.broadcasted_iota(jnp.int32, a.shape, 0)
    d = 1
    while d < T:
        m = row < T - d
        b = jnp.where(m, a * pltpu.roll(b, T - d, 0) + b, b)
        a = jnp.where(m, a * pltpu.roll(a, T - d, 0), a)
        d *= 2
    return b


def _lru_common(x, prev, first, cw_ref, cb_ref, wgx_ref, bgx_ref, wga_ref, bga_ref, lam_ref, T):
    row = lax.broadcasted_iota(jnp.int32, x.shape, 0)
    prev = jnp.where(first, 0.0, prev)
    xs = []
    for j in (3, 2, 1):
        pv = jnp.tile(pltpu.roll(prev, j, 0), (T // SUB, 1))
        xs.append(jnp.where(row < j, pv, pltpu.roll(x, j, 0)))
    xs.append(x)
    xc = cb_ref[...] + cw_ref[0:1, :] * xs[0] + cw_ref[1:2, :] * xs[1] + cw_ref[2:3, :] * xs[2] + cw_ref[3:4, :] * xs[3]
    xcb = xc.astype(BF16)
    gx = _sig(_nn(xcb, wgx_ref[0]) + bgx_ref[0])
    ga = _sig(_nn(xcb, wga_ref[0]) + bga_ref[0])
    sp = _softplus_neg(lam_ref[...])
    log_a = -LRU_C * ga * sp
    a = jnp.exp(log_a)
    mult = jnp.sqrt(_neg_expm1(2.0 * log_a))
    return xs, xc, xcb, gx, ga, sp, a, mult


def _lru_specs(T, tmap):
    def at(col0):
        return pl.BlockSpec((T, LANE), lambda n, i: (tmap(i), col0 + n))

    def prev(col0):
        return pl.BlockSpec((SUB, LANE), lambda n, i: (jnp.maximum(tmap(i) * (T // SUB) - 1, 0), col0 + n))

    small = [
        pl.BlockSpec((4, LANE), lambda n, i: (0, n)),
        pl.BlockSpec((1, LANE), lambda n, i: (0, n)),
        pl.BlockSpec((1, LANE, LANE), lambda n, i: (n, 0, 0)),
        pl.BlockSpec((1, 1, LANE), lambda n, i: (n, 0, 0)),
        pl.BlockSpec((1, LANE, LANE), lambda n, i: (n, 0, 0)),
        pl.BlockSpec((1, 1, LANE), lambda n, i: (n, 0, 0)),
        pl.BlockSpec((1, LANE), lambda n, i: (0, n)),
    ]
    return at, prev, small


def _lru_fwd(zp, w):
    S = zp.shape[0]
    T = T_LRU
    at, prev, small = _lru_specs(T, lambda i: i)

    def body(x_ref, xp_ref, g_ref, cw_ref, cb_ref, wgx_ref, bgx_ref, wga_ref, bga_ref, lam_ref, hs_ref, y_ref, carry_ref):
        i = pl.program_id(1)

        @pl.when(i == 0)
        def _():
            carry_ref[...] = jnp.zeros_like(carry_ref)

        x = x_ref[...]
        _, xc, _, gx, _, _, a, mult = _lru_common(x, xp_ref[...], i == 0, cw_ref, cb_ref, wgx_ref, bgx_ref, wga_ref, bga_ref, lam_ref, T)
        A, B = _scan_fwd(a, mult * gx * xc, T)
        h = B + A * carry_ref[SUB - 1:SUB, :]
        hs_ref[...] = h
        carry_ref[...] = hs_ref[T - SUB:T, :]
        y_ref[...] = (h * _silu(g_ref[...])).astype(BF16)

    out = pl.BlockSpec((T, LANE), lambda n, i: (i, n))
    return pl.pallas_call(
        body, name="lru_fwd", grid=(8, S // T),
        in_specs=[at(C_LRUX), prev(C_LRUX), at(C_LRUG)] + small, out_specs=[out, out],
        out_shape=[jax.ShapeDtypeStruct((S, D), F32), jax.ShapeDtypeStruct((S, D), BF16)],
        scratch_shapes=[pltpu.VMEM((SUB, LANE), F32)],
        compiler_params=_cp(dimension_semantics=("parallel", "arbitrary")),
    )(zp, zp, zp, w["conv_w"], w["conv_b"], w["w_gx"], w["b_gx"], w["w_ga"], w["b_ga"], w["lam"])


def _lru_bwd(zp, hs, dy, w, dz):
    S = zp.shape[0]
    T = T_LRU
    nT = S // T
    at, prev, small = _lru_specs(T, lambda i: nT - 1 - i)

    def body(x_ref, xp_ref, g_ref, h_ref, hp_ref, dy_ref, cw_ref, cb_ref, wgx_ref, bgx_ref, wga_ref, bga_ref, lam_ref, dz_in,
             dzx_ref, dcw_ref, dcb_ref, dwgx_ref, dbgx_ref, dwga_ref, dbga_ref, dlam_ref, carry_ref, head_ref):
        del dz_in
        j = pl.program_id(1)
        it = nT - 1 - j

        @pl.when(j == 0)
        def _():
            for r in (carry_ref, head_ref, dcw_ref, dcb_ref, dwgx_ref, dbgx_ref, dwga_ref, dbga_ref, dlam_ref):
                r[...] = jnp.zeros_like(r)

        first = it == 0
        x = x_ref[...]
        xs, xc, xcb, gx, ga, sp, a, mult = _lru_common(x, xp_ref[...], first, cw_ref, cb_ref, wgx_ref, bgx_ref, wga_ref, bga_ref, lam_ref, T)
        row = lax.broadcasted_iota(jnp.int32, x.shape, 0)
        u = gx * xc
        h = h_ref[...]
        hp = jnp.where(first, 0.0, hp_ref[...])
        hm1 = jnp.where(row < 1, jnp.tile(pltpu.roll(hp, 1, 0), (T // SUB, 1)), pltpu.roll(h, 1, 0))
        dho = dy_ref[...] * _silu(g_ref[...])
        gin = jnp.where(row == T - 1, dho + carry_ref[0:1, :], dho)
        abar = jnp.where(row == T - 1, 0.0, pltpu.roll(a, T - 1, 0))
        dh = _scan_bwd(abar, gin, T)
        carry_ref[...] = (a * dh)[0:SUB, :]
        da = dh * hm1
        dmult = dh * u
        du = dh * mult
        dgx = du * xc
        dxc = du * gx
        dlog_a = da * a - dmult * a * a / mult
        dga = dlog_a * (-LRU_C * sp)
        lam = lam_ref[...]
        dlam_ref[...] += _csum(dlog_a * (-LRU_C * ga)) * (-1.0 / (1.0 + jnp.exp(lam)))
        dpa = dga * ga * (1.0 - ga)
        dpx = dgx * gx * (1.0 - gx)
        dpab, dpxb = dpa.astype(BF16), dpx.astype(BF16)
        dxc = dxc + _nt(dpxb, wgx_ref[0]) + _nt(dpab, wga_ref[0])
        dwgx_ref[0] += _tn(xcb, dpxb)
        dwga_ref[0] += _tn(xcb, dpab)
        dbgx_ref[0] += _csum(dpx)
        dbga_ref[0] += _csum(dpa)
        dcb_ref[...] += _csum(dxc)
        for k in range(4):
            dcw_ref[k:k + 1, :] += _csum(dxc * xs[k])
        head = head_ref[...]
        dx = cw_ref[3:4, :] * dxc
        for jj in (1, 2, 3):
            hv = jnp.tile(pltpu.roll(head, SUB - jj, 0), (T // SUB, 1))
            dx = dx + cw_ref[3 - jj:4 - jj, :] * jnp.where(row >= T - jj, hv, pltpu.roll(dxc, T - jj, 0))
        head_ref[...] = dxc[0:SUB, :]
        dzx_ref[...] = dx.astype(BF16)

    def acc(shape, imap):
        return pl.BlockSpec(shape, imap)

    out_specs = [
        pl.BlockSpec((T, LANE), lambda n, i: (nT - 1 - i, C_LRUX + n)),
        acc((4, LANE), lambda n, i: (0, n)), acc((1, LANE), lambda n, i: (0, n)),
        acc((1, LANE, LANE), lambda n, i: (n, 0, 0)), acc((1, 1, LANE), lambda n, i: (n, 0, 0)),
        acc((1, LANE, LANE), lambda n, i: (n, 0, 0)), acc((1, 1, LANE), lambda n, i: (n, 0, 0)),
        acc((1, LANE), lambda n, i: (0, n)),
    ]
    out_shape = [
        jax.ShapeDtypeStruct(dz.shape, BF16),
        jax.ShapeDtypeStruct((4, D), F32), jax.ShapeDtypeStruct((1, D), F32),
        jax.ShapeDtypeStruct((8, LANE, LANE), F32), jax.ShapeDtypeStruct((8, 1, LANE), F32),
        jax.ShapeDtypeStruct((8, LANE, LANE), F32), jax.ShapeDtypeStruct((8, 1, LANE), F32),
        jax.ShapeDtypeStruct((1, D), F32),
    ]
    dyspec = pl.BlockSpec((T, LANE), lambda n, i: (nT - 1 - i, n))
    hprev = pl.BlockSpec((SUB, LANE), lambda n, i: (jnp.maximum((nT - 1 - i) * (T // SUB) - 1, 0), n))
    return pl.pallas_call(
        body, name="lru_bwd", grid=(8, nT),
        in_specs=[at(C_LRUX), prev(C_LRUX), at(C_LRUG), dyspec, hprev, dyspec] + small + [pl.BlockSpec(memory_space=pl.ANY)],
        out_specs=out_specs, out_shape=out_shape,
        scratch_shapes=[pltpu.VMEM((SUB, LANE), F32), pltpu.VMEM((SUB, LANE), F32)],
        input_output_aliases={13: 0},
        compiler_params=_cp(dimension_semantics=("parallel", "arbitrary")),
    )(zp, zp, zp, hs, hs, dy, w["conv_w"], w["conv_b"], w["w_gx"], w["b_gx"], w["w_ga"], w["b_ga"], w["lam"], dz)


def _lru_gate_bwd(zp, hs, dy, dz):
    S = zp.shape[0]
    T = T_ROW

    def body(g_ref, h_ref, dy_ref, dz_in, o_ref):
        del dz_in
        o_ref[...] = (dy_ref[...] * h_ref[...] * _dsilu(g_ref[...])).astype(BF16)

    row = pl.BlockSpec((T, D), lambda i: (i, 0))
    zc = pl.BlockSpec((T, D), lambda i: (i, C_LRUG // 8))
    return pl.pallas_call(
        body, name="lru_gate_bwd", grid=(S // T,), in_specs=[zc, row, row, pl.BlockSpec(memory_space=pl.ANY)], out_specs=zc,
        out_shape=jax.ShapeDtypeStruct(dz.shape, BF16), input_output_aliases={3: 0}, compiler_params=_cp(),
    )(zp, hs, dy, dz)


def _rope_tables(pos):
    pf = pos.astype(F32)[:, None]

    def cs(d):
        inv = ROPE_THETA ** (-jnp.arange(0, d, 2, dtype=F32) / d)
        ang = pf * inv
        return jnp.cos(ang), jnp.sin(ang)

    S = pos.shape[0]
    c, s = cs(32)
    one, zero = jnp.ones((S, 64), F32), jnp.zeros((S, 16), F32)
    z32, z64 = jnp.zeros((S, 32), F32), jnp.zeros((S, 64), F32)
    mla = (jnp.concatenate([one, c, c, jnp.ones((S, 32), F32)], 1),
           jnp.concatenate([z64, zero, s, z32], 1),
           jnp.concatenate([z64, -s, zero, z32], 1))
    c, s = cs(64)
    dil = (jnp.concatenate([c, c, c, c], 1),
           jnp.concatenate([z32, s, z32, s], 1),
           jnp.concatenate([-s, z32, -s, z32], 1))
    return mla, dil


def _rope(x, C, S1, S2, sh):
    return x * C + pltpu.roll(x, sh, 1) * S1 + pltpu.roll(x, LANE - sh, 1) * S2


def _rope_t(dy, C, S1, S2, sh):
    return dy * C + pltpu.roll(dy * S1, LANE - sh, 1) + pltpu.roll(dy * S2, sh, 1)


def _lane(shape):
    return lax.broadcasted_iota(jnp.int32, shape, 1)


T_MLA = 256
TA = 512


def _zcol(T, width, col_lanes):
    assert (col_lanes * LANE) % width == 0
    return pl.BlockSpec((T, width), lambda i: (i, col_lanes * LANE // width))


def _full(shape):
    return pl.BlockSpec(shape, lambda *_: (0,) * len(shape))


def _mla_pre_fwd(zp, w, tab):
    S = zp.shape[0]
    T = T_MLA

    def body(cq_ref, ckv_ref, kr_ref, gcq_ref, gckv_ref, wuq_ref, wuk_ref, wuv_ref, gq_ref, gk_ref, C_ref, S1_ref, S2_ref,
             q_ref, k_ref, v_ref):
        cq = cq_ref[...]
        cqn = (cq * lax.rsqrt(jnp.mean(cq * cq, axis=-1, keepdims=True) + EPS) * gcq_ref[...]).astype(BF16)
        ckv = ckv_ref[...]
        ckvn = (ckv * lax.rsqrt(jnp.mean(ckv * ckv, axis=-1, keepdims=True) + EPS) * gckv_ref[...]).astype(BF16)
        q0 = _nn(cqn, wuq_ref[...])
        k0 = _nn(ckvn, wuk_ref[...])
        krb = kr_ref[...]
        C, S1, S2 = C_ref[...], S1_ref[...], S2_ref[...]
        for h in range(8):
            sl = slice(h * LANE, (h + 1) * LANE)
            xq = q0[:, sl]
            xq = xq * lax.rsqrt(_rsum(xq * xq) * (1.0 / MLA_QK) + EPS) * gq_ref[...]
            q_ref[:, sl] = _rope(xq, C, S1, S2, 16).astype(BF16)
            xk = k0[:, sl] + krb
            xk = xk * lax.rsqrt(_rsum(xk * xk) * (1.0 / MLA_QK) + EPS) * gk_ref[...]
            k_ref[:, sl] = _rope(xk, C, S1, S2, 16).astype(BF16)
        v_ref[...] = _nn(ckvn, wuv_ref[...]).astype(BF16)

    tabspec = pl.BlockSpec((T, LANE), lambda i: (i, 0))
    in_specs = [_zcol(T, 256, C_CQ), _zcol(T, LANE, C_CKV), _zcol(T, LANE, C_KR), _full((1, 256)), _full((1, LANE)),
                _full((256, 1024)), _full((LANE, 1024)), _full((LANE, 512)), _full((1, LANE)), _full((1, LANE)),
                tabspec, tabspec, tabspec]
    return pl.pallas_call(
        body, name="mla_pre_fwd", grid=(S // T,), in_specs=in_specs,
        out_specs=[pl.BlockSpec((T, 1024), lambda i: (i, 0)), pl.BlockSpec((T, 1024), lambda i: (i, 0)), pl.BlockSpec((T, 512), lambda i: (i, 0))],
        out_shape=[jax.ShapeDtypeStruct((S, 1024), BF16), jax.ShapeDtypeStruct((S, 1024), BF16), jax.ShapeDtypeStruct((S, 512), BF16)],
        compiler_params=_cp(),
    )(zp, zp, zp, w["g_cq"], w["g_ckv"], w["w_uq"], w["w_uk"], w["w_uv"], w["g_mq"], w["g_mk"], *tab)


def _mla_attn_fwd(q, k, v, zp):
    S = q.shape[0]
    nq = S // TA

    def body(q_ref, k_ref, v_ref, g_ref, o_ref, lse_ref, y_ref):
        qi = pl.program_id(1)
        lane = _lane((TA, LANE))
        rowi = lax.broadcasted_iota(jnp.int32, (TA, TA), 0)
        coli = lax.broadcasted_iota(jnp.int32, (TA, TA), 1)
        o_tot = jnp.zeros((TA, LANE), F32)
        for hh in range(2):
            cs = slice(hh * LANE, (hh + 1) * LANE)
            hm = (lane < 64) if hh == 0 else (lane >= 64)
            qh = q_ref[:, cs]

            def step(kb, carry, masked, cs=cs, hm=hm, qh=qh):
                m, l, acc = carry
                off = pl.multiple_of(kb * TA, TA)
                kh = k_ref[pl.ds(off, TA), cs]
                vv = v_ref[pl.ds(off, TA), :]
                vh = jnp.where(hm, vv, jnp.zeros_like(vv))
                s = _nt(qh, kh) * MLA_SCALE
                if masked:
                    s = jnp.where(rowi >= coli, s, NEG)
                m_new = jnp.maximum(m, jnp.max(s, axis=-1, keepdims=True))
                alpha = jnp.exp(m - m_new)
                p = jnp.exp(s - m_new)
                l = alpha * l + _rsum(p)
                acc = alpha * acc + _nn(p.astype(BF16), vh)
                return m_new, l, acc

            init = (jnp.full((TA, 1), NEG, F32), jnp.zeros((TA, 1), F32), jnp.zeros((TA, LANE), F32))
            carry = lax.fori_loop(0, qi, lambda kb, c: step(kb, c, False), init)
            m, l, acc = step(qi, carry, True)
            o_tot = o_tot + acc / l
            lse_ref[:, cs] = jnp.broadcast_to(m + jnp.log(l), (TA, LANE))
        o_ref[...] = o_tot
        y_ref[...] = (o_tot * _silu(g_ref[...])).astype(BF16)

    blk = pl.BlockSpec((TA, LANE), lambda p, i: (i, p))
    return pl.pallas_call(
        body, name="mla_attn_fwd", grid=(4, nq),
        in_specs=[pl.BlockSpec((TA, 256), lambda p, i: (i, p)), pl.BlockSpec((S, 256), lambda p, i: (0, p)),
                  pl.BlockSpec((S, LANE), lambda p, i: (0, p)), pl.BlockSpec((TA, LANE), lambda p, i: (i, C_MLAG + p))],
        out_specs=[blk, pl.BlockSpec((TA, 256), lambda p, i: (i, p)), blk],
        out_shape=[jax.ShapeDtypeStruct((S, 512), F32), jax.ShapeDtypeStruct((S, 1024), F32), jax.ShapeDtypeStruct((S, 512), BF16)],
        compiler_params=_cp(dimension_semantics=("parallel", "arbitrary")),
    )(q, k, v, zp)


def _mla_post_bwd(zp, o, dy, dz):
    S = zp.shape[0]
    T = T_ROW

    def body(g_ref, o_ref, dy_ref, dz_in, dz_ref, do_ref, D_ref):
        del dz_in
        g, o_, dy_ = g_ref[...], o_ref[...], dy_ref[...]
        do = dy_ * _silu(g)
        do_ref[...] = do.astype(BF16)
        dz_ref[...] = (dy_ * o_ * _dsilu(g)).astype(BF16)
        prod = do * o_
        lane = _lane((T, LANE))
        for p in range(4):
            pr = prod[:, p * LANE:(p + 1) * LANE]
            da = _rsum(jnp.where(lane < 64, pr, 0.0))
            db = _rsum(jnp.where(lane >= 64, pr, 0.0))
            D_ref[:, 2 * p * LANE:(2 * p + 1) * LANE] = jnp.broadcast_to(da, (T, LANE))
            D_ref[:, (2 * p + 1) * LANE:(2 * p + 2) * LANE] = jnp.broadcast_to(db, (T, LANE))

    row = pl.BlockSpec((T, 512), lambda i: (i, 0))
    zc = _zcol(T, 512, C_MLAG)
    return pl.pallas_call(
        body, name="mla_post_bwd", grid=(S // T,), in_specs=[zc, row, row, pl.BlockSpec(memory_space=pl.ANY)],
        out_specs=[zc, row, pl.BlockSpec((T, 1024), lambda i: (i, 0))],
        out_shape=[jax.ShapeDtypeStruct(dz.shape, BF16), jax.ShapeDtypeStruct((S, 512), BF16), jax.ShapeDtypeStruct((S, 1024), F32)],
        input_output_aliases={3: 0}, compiler_params=_cp(),
    )(zp, o, dy, dz)


def _mla_attn_bwd(q, k, v, do, lse, Dr):
    S = q.shape[0]
    nq = S // TA

    def body(q_ref, do_ref, lse_ref, D_ref, k_ref, v_ref, dq_ref, dk_ref, dv_ref):
        ki = pl.program_id(1)

        @pl.when(ki == 0)
        def _():
            dq_ref[...] = jnp.zeros_like(dq_ref)

        lane = _lane((TA, LANE))
        rowi = lax.broadcasted_iota(jnp.int32, (TA, TA), 0)
        coli = lax.broadcasted_iota(jnp.int32, (TA, TA), 1)
        dv_tot = jnp.zeros((TA, LANE), F32)
        for hh in range(2):
            cs = slice(hh * LANE, (hh + 1) * LANE)
            hm = (lane < 64) if hh == 0 else (lane >= 64)
            kh = k_ref[:, cs]
            vv = v_ref[...]
            vm = jnp.where(hm, vv, jnp.zeros_like(vv))

            def step(qb, carry, masked, cs=cs, kh=kh, vm=vm):
                dk_acc, dv_acc = carry
                off = pl.multiple_of(qb * TA, TA)
                qh = q_ref[pl.ds(off, TA), cs]
                doh = do_ref[pl.ds(off, TA), :]
                ls = jnp.tile(lse_ref[pl.ds(off, TA), cs], (1, TA // LANE))
                dd = jnp.tile(D_ref[pl.ds(off, TA), cs], (1, TA // LANE))
                s = _nt(qh, kh) * MLA_SCALE
                if masked:
                    s = jnp.where(rowi >= coli, s, NEG)
                p = jnp.exp(s - ls)
                dp = _nt(doh, vm)
                ds = (p * (dp - dd) * MLA_SCALE).astype(BF16)
                dv_acc = dv_acc + _tn(p.astype(BF16), doh)
                dk_acc = dk_acc + _tn(ds, qh)
                dq_ref[pl.ds(off, TA), cs] += _nn(ds, kh)
                return dk_acc, dv_acc

            z = jnp.zeros((TA, LANE), F32)
            carry = step(ki, (z, z), True)
            dk_acc, dv_acc = lax.fori_loop(ki + 1, nq, lambda qb, c: step(qb, c, False), carry)
            dk_ref[:, cs] = dk_acc
            dv_tot = dv_tot + jnp.where(hm, dv_acc, 0.0)
        dv_ref[...] = dv_tot

    pair = pl.BlockSpec((S, 256), lambda p, i: (0, p))
    return pl.pallas_call(
        body, name="mla_attn_bwd", grid=(4, nq),
        in_specs=[pair, pl.BlockSpec((S, LANE), lambda p, i: (0, p)), pair, pair,
                  pl.BlockSpec((TA, 256), lambda p, i: (i, p)), pl.BlockSpec((TA, LANE), lambda p, i: (i, p))],
        out_specs=[pair, pl.BlockSpec((TA, 256), lambda p, i: (i, p)), pl.BlockSpec((TA, LANE), lambda p, i: (i, p))],
        out_shape=[jax.ShapeDtypeStruct((S, 1024), F32), jax.ShapeDtypeStruct((S, 1024), F32), jax.ShapeDtypeStruct((S, 512), F32)],
        compiler_params=_cp(dimension_semantics=("parallel", "arbitrary")),
    )(q, do, lse, Dr, k, v)


def _mla_pre_bwd(zp, dq, dk, dv, w, tab, dz):
    S = zp.shape[0]
    T = T_MLA

    def body(cq_ref, ckv_ref, kr_ref, dq_ref, dk_ref, dv_ref, gcq_ref, gckv_ref, wuq_ref, wuk_ref, wuv_ref, gq_ref, gk_ref,
             C_ref, S1_ref, S2_ref, dz_in, dz_ref, dwuq_ref, dwuk_ref, dwuv_ref, dgcq_ref, dgckv_ref, dgq_ref, dgk_ref):
        del dz_in
        i = pl.program_id(0)

        @pl.when(i == 0)
        def _():
            for r in (dwuq_ref, dwuk_ref, dwuv_ref, dgcq_ref, dgckv_ref, dgq_ref, dgk_ref):
                r[...] = jnp.zeros_like(r)

        cq = cq_ref[...]
        rq = lax.rsqrt(jnp.mean(cq * cq, axis=-1, keepdims=True) + EPS)
        cqh = cq * rq
        cqn = (cqh * gcq_ref[...]).astype(BF16)
        ckv = ckv_ref[...]
        rkv = lax.rsqrt(jnp.mean(ckv * ckv, axis=-1, keepdims=True) + EPS)
        ckvh = ckv * rkv
        ckvn = (ckvh * gckv_ref[...]).astype(BF16)
        q0 = _nn(cqn, wuq_ref[...])
        k0 = _nn(ckvn, wuk_ref[...])
        krb = kr_ref[...]
        C, S1, S2 = C_ref[...], S1_ref[...], S2_ref[...]
        gq, gk = gq_ref[...], gk_ref[...]

        def head_bwd(x, dy, g):
            r = lax.rsqrt(_rsum(x * x) * (1.0 / MLA_QK) + EPS)
            xn = x * r
            dyn = _rope_t(dy, C, S1, S2, 16)
            dxh = dyn * g
            return r * (dxh - xn * _rsum(dxh * xn) * (1.0 / MLA_QK)), _csum(dyn * xn)

        dq0, dk0 = [], []
        dgq_acc = jnp.zeros((1, LANE), F32)
        dgk_acc = jnp.zeros((1, LANE), F32)
        dkr = jnp.zeros((T, LANE), F32)
        for h in range(8):
            sl = slice(h * LANE, (h + 1) * LANE)
            dxq, gq_p = head_bwd(q0[:, sl], dq_ref[:, sl], gq)
            dxk, gk_p = head_bwd(k0[:, sl] + krb, dk_ref[:, sl], gk)
            dq0.append(dxq.astype(BF16))
            dk0.append(dxk.astype(BF16))
            dkr = dkr + dxk
            dgq_acc = dgq_acc + gq_p
            dgk_acc = dgk_acc + gk_p
        dgq_ref[...] += dgq_acc
        dgk_ref[...] += dgk_acc
        dq0 = jnp.concatenate(dq0, axis=1)
        dk0 = jnp.concatenate(dk0, axis=1)
        dvb = dv_ref[...].astype(BF16)
        dwuq_ref[...] += _tn(cqn, dq0)
        dwuk_ref[...] += _tn(ckvn, dk0)
        dwuv_ref[...] += _tn(ckvn, dvb)
        dcqn = _nt(dq0, wuq_ref[...])
        dckvn = _nt(dk0, wuk_ref[...]) + _nt(dvb, wuv_ref[...])
        dgcq_ref[...] += _csum(dcqn * cqh)
        dgckv_ref[...] += _csum(dckvn * ckvh)
        dxh = dcqn * gcq_ref[...]
        dz_ref[:, 0:256] = (rq * (dxh - cqh * jnp.mean(dxh * cqh, axis=-1, keepdims=True))).astype(BF16)
        dxh = dckvn * gckv_ref[...]
        dz_ref[:, 256:384] = (rkv * (dxh - ckvh * jnp.mean(dxh * ckvh, axis=-1, keepdims=True))).astype(BF16)
        lane = _lane((T, LANE))
        dz_ref[:, 384:512] = jnp.where((lane >= KR_LANE) & (lane < KR_LANE + 32), dkr, 0.0).astype(BF16)

    tabspec = pl.BlockSpec((T, LANE), lambda i: (i, 0))
    in_specs = [_zcol(T, 256, C_CQ), _zcol(T, LANE, C_CKV), _zcol(T, LANE, C_KR),
                pl.BlockSpec((T, 1024), lambda i: (i, 0)), pl.BlockSpec((T, 1024), lambda i: (i, 0)), pl.BlockSpec((T, 512), lambda i: (i, 0)),
                _full((1, 256)), _full((1, LANE)), _full((256, 1024)), _full((LANE, 1024)), _full((LANE, 512)), _full((1, LANE)), _full((1, LANE)),
                tabspec, tabspec, tabspec, pl.BlockSpec(memory_space=pl.ANY)]
    out_specs = [_zcol(T, 512, C_CQ), _full((256, 1024)), _full((LANE, 1024)), _full((LANE, 512)), _full((1, 256)), _full((1, LANE)),
                 _full((1, LANE)), _full((1, LANE))]
    out_shape = [jax.ShapeDtypeStruct(dz.shape, BF16), jax.ShapeDtypeStruct((256, 1024), F32), jax.ShapeDtypeStruct((LANE, 1024), F32),
                 jax.ShapeDtypeStruct((LANE, 512), F32), jax.ShapeDtypeStruct((1, 256), F32), jax.ShapeDtypeStruct((1, LANE), F32),
                 jax.ShapeDtypeStruct((1, LANE), F32), jax.ShapeDtypeStruct((1, LANE), F32)]
    return pl.pallas_call(
        body, name="mla_pre_bwd", grid=(S // T,), in_specs=in_specs, out_specs=out_specs, out_shape=out_shape,
        input_output_aliases={16: 0}, compiler_params=_cp(),
    )(zp, zp, zp, dq, dk, dv, w["g_cq"], w["g_ckv"], w["w_uq"], w["w_uk"], w["w_uv"], w["g_mq"], w["g_mk"], *tab, dz)


T_DIL = 256


def _head_stats(x, lane):
    sq = x * x
    sa = _rsum(jnp.where(lane < 64, sq, 0.0))
    sb = _rsum(jnp.where(lane >= 64, sq, 0.0))
    return lax.rsqrt(jnp.where(lane < 64, sa, sb) * (1.0 / DIL_HD) + EPS)


def _head_sum(x, lane):
    sa = _rsum(jnp.where(lane < 64, x, 0.0))
    sb = _rsum(jnp.where(lane >= 64, x, 0.0))
    return jnp.where(lane < 64, sa, sb)


def _dil_pre_fwd(zp, w, tab):
    S = zp.shape[0]
    T = T_DIL

    def body(q_ref, k_ref, gq_ref, gk_ref, C_ref, S1_ref, S2_ref, qo_ref, ko_ref):
        C, S1, S2 = C_ref[...], S1_ref[...], S2_ref[...]
        lane = _lane((T, LANE))
        for b in range(12):
            sl = slice(b * LANE, (b + 1) * LANE)
            x = q_ref[:, sl]
            qo_ref[:, sl] = _rope(x * _head_stats(x, lane) * gq_ref[...], C, S1, S2, 32)
            x = k_ref[:, sl]
            ko_ref[:, sl] = _rope(x * _head_stats(x, lane) * gk_ref[...], C, S1, S2, 32)

    tabspec = pl.BlockSpec((T, LANE), lambda i: (i, 0))
    out = pl.BlockSpec((T, 1536), lambda i: (i, 0))
    return pl.pallas_call(
        body, name="dil_pre_fwd", grid=(S // T,),
        in_specs=[_zcol(T, 1536, C_DQ), _zcol(T, 1536, C_DK), _full((1, LANE)), _full((1, LANE)), tabspec, tabspec, tabspec],
        out_specs=[out, out], out_shape=[jax.ShapeDtypeStruct((S, 1536), F32)] * 2, compiler_params=_cp(),
    )(zp, zp, w["g_dq"], w["g_dk"], *tab)


DIL_ROWS = 2048


def _dil_geometry(g, S):
    d = DIL_DILATIONS[g]
    P = NK * d
    return d, P, DIL_ROWS // P, S // P


def _dil_rows(start, d):
    return pl.ds(pl.multiple_of(start, NK), NK) if d == 1 else pl.ds(start, NK, stride=d)


def _dil_specs(g, S, col0):
    _, P, m, nb = _dil_geometry(g, S)
    cur = pl.BlockSpec((DIL_ROWS, LANE), lambda sb, c: (sb, col0 + c))
    prv = pl.BlockSpec((P, LANE), lambda sb, c: (jnp.maximum(sb * m - 1, 0), col0 + c))
    nxt = pl.BlockSpec((P, LANE), lambda sb, c: (jnp.minimum((sb + 1) * m, nb - 1), col0 + c))
    return cur, prv, nxt


def _dil_masks(n, nb):
    row = lax.broadcasted_iota(jnp.int32, (NK, NK), 0)
    col = lax.broadcasted_iota(jnp.int32, (NK, NK), 1)
    return col <= row, (col >= row) & (n > 0), (col >= row) & (n < nb - 1)


def _dil_attn_fwd(q, k, zp, g):
    S = q.shape[0]
    d, P, m, nb = _dil_geometry(g, S)
    R = DIL_ROWS

    def body(q_ref, kc_ref, kp_ref, vc_ref, vp_ref, o_ref, lse_ref, *scr):
        sb = pl.program_id(0)
        if m > 1:
            ks_ref, vs_ref = scr
            ks_ref[0:P, :] = kp_ref[...]
            ks_ref[P:P + R, :] = kc_ref[...]
            vs_ref[0:P, :] = vp_ref[...]
            vs_ref[P:P + R, :] = vc_ref[...]
        lane = _lane((NK, LANE))

        def unit(u, carry):
            j = u // d
            rows = _dil_rows(j * P + (u - j * d), d)
            if m > 1:
                rows_c = _dil_rows(j * P + (u - j * d) + P, d)
                kp, kc, vp, vc = ks_ref[rows, :], ks_ref[rows_c, :], vs_ref[rows, :], vs_ref[rows_c, :]
            else:
                kp, kc, vp, vc = kp_ref[rows, :], kc_ref[rows, :], vp_ref[rows, :], vc_ref[rows, :]
            kp, kc, vp, vc = kp.astype(BF16), kc.astype(BF16), vp.astype(BF16), vc.astype(BF16)
            q_ = q_ref[rows, :].astype(BF16)
            mc, mp, _ = _dil_masks(sb * m + j, nb)
            zb = jnp.zeros_like(q_)
            o_tot = jnp.zeros((NK, LANE), F32)
            lse_tot = jnp.zeros((NK, LANE), F32)
            for hh in range(2):
                hm = (lane < 64) if hh == 0 else (lane >= 64)
                qm = jnp.where(hm, q_, zb)
                sc = jnp.where(mc, _nt(qm, kc) * DIL_SCALE, NEG)
                sp = jnp.where(mp, _nt(qm, kp) * DIL_SCALE, NEG)
                mx = jnp.maximum(jnp.max(sc, axis=-1, keepdims=True), jnp.max(sp, axis=-1, keepdims=True))
                ec = jnp.exp(sc - mx)
                ep = jnp.exp(sp - mx)
                den = _rsum(ec) + _rsum(ep)
                o = (_nn(ec.astype(BF16), jnp.where(hm, vc, zb)) + _nn(ep.astype(BF16), jnp.where(hm, vp, zb))) / den
                o_tot = o_tot + o
                lse_tot = jnp.where(hm, mx + jnp.log(den), lse_tot)
            o_ref[rows, :] = o_tot
            lse_ref[rows, :] = lse_tot
            return carry

        lax.fori_loop(0, R // NK, unit, 0, unroll=2)

    qcur, qprv, _ = _dil_specs(g, S, 4 * g)
    vcur, vprv, _ = _dil_specs(g, S, C_DV + 4 * g)
    out = pl.BlockSpec((R, LANE), lambda sb, c: (sb, c))
    return pl.pallas_call(
        body, name=f"dil_attn_fwd{g}", grid=(S // R, 4), in_specs=[qcur, qcur, qprv, vcur, vprv], out_specs=[out, out],
        out_shape=[jax.ShapeDtypeStruct((S, 512), F32)] * 2,
        scratch_shapes=[pltpu.VMEM((P + R, LANE), F32)] * 2 if m > 1 else [], compiler_params=_cp(),
    )(q, k, k, zp, zp)


def _dil_combine(os_, ls_, zp):
    S = zp.shape[0]
    T = T_ROW

    def body(o0, o1, o2, l0, l1, l2, g_ref, oc_ref, L_ref, y_ref):
        a, b, c = l0[...], l1[...], l2[...]
        mx = jnp.maximum(jnp.maximum(a, b), c)
        ea, eb, ec = jnp.exp(a - mx), jnp.exp(b - mx), jnp.exp(c - mx)
        den = ea + eb + ec
        oc = (ea * o0[...] + eb * o1[...] + ec * o2[...]) / den
        oc_ref[...] = oc
        L_ref[...] = mx + jnp.log(den)
        y_ref[...] = (oc * _silu(g_ref[...])).astype(BF16)

    row = pl.BlockSpec((T, 512), lambda i: (i, 0))
    return pl.pallas_call(
        body, name="dil_combine", grid=(S // T,), in_specs=[row] * 6 + [_zcol(T, 512, C_DILG)], out_specs=[row, row, row],
        out_shape=[jax.ShapeDtypeStruct((S, 512), F32), jax.ShapeDtypeStruct((S, 512), F32), jax.ShapeDtypeStruct((S, 512), BF16)],
        compiler_params=_cp(),
    )(*os_, *ls_, zp)


def _dil_comb_bwd(zp, oc, dy, dz):
    S = zp.shape[0]
    T = T_ROW

    def body(g_ref, o_ref, dy_ref, dz_in, dz_ref, do_ref, D_ref):
        del dz_in
        g, o_, dy_ = g_ref[...], o_ref[...], dy_ref[...]
        do = dy_ * _silu(g)
        do_ref[...] = do
        dz_ref[...] = (dy_ * o_ * _dsilu(g)).astype(BF16)
        lane = _lane((T, LANE))
        for p in range(4):
            sl = slice(p * LANE, (p + 1) * LANE)
            D_ref[:, sl] = _head_sum(do[:, sl] * o_[:, sl], lane)

    row = pl.BlockSpec((T, 512), lambda i: (i, 0))
    zc = _zcol(T, 512, C_DILG)
    return pl.pallas_call(
        body, name="dil_comb_bwd", grid=(S // T,), in_specs=[zc, row, row, pl.BlockSpec(memory_space=pl.ANY)], out_specs=[zc, row, row],
        out_shape=[jax.ShapeDtypeStruct(dz.shape, BF16), jax.ShapeDtypeStruct((S, 512), F32), jax.ShapeDtypeStruct((S, 512), F32)],
        input_output_aliases={3: 0}, compiler_params=_cp(),
    )(zp, oc, dy, dz)


def _dil_attn_bwd(q, k, zp, do, L, Dr, g):
    S = q.shape[0]
    d, P, m, nb = _dil_geometry(g, S)
    R = DIL_ROWS
    n_q, n_k = 4, 2

    def body(*refs):
        q_side = refs[0:2 * n_q]
        k_side = refs[2 * n_q:2 * n_q + 2 * n_k]
        dq_ref, dk_ref, dv_ref = refs[2 * n_q + 2 * n_k:2 * n_q + 2 * n_k + 3]
        scr = refs[2 * n_q + 2 * n_k + 3:]
        sb = pl.program_id(0)
        if m > 1:
            for a in range(n_q):
                scr[a][0:R, :] = q_side[2 * a][...]
                scr[a][R:R + P, :] = q_side[2 * a + 1][...]
            for a in range(n_k):
                scr[n_q + a][0:P, :] = k_side[2 * a + 1][...]
                scr[n_q + a][P:P + R, :] = k_side[2 * a][...]
        lane = _lane((NK, LANE))

        def unit(u, carry):
            j = u // d
            start = j * P + (u - j * d)
            rows = _dil_rows(start, d)
            if m > 1:
                rows_b = _dil_rows(start + P, d)
                qc, doc, Lc_, Dc_ = [scr[a][rows, :] for a in range(n_q)]
                qn, don, Ln_, Dn_ = [scr[a][rows_b, :] for a in range(n_q)]
                kp, vp = [scr[n_q + a][rows, :] for a in range(n_k)]
                kc, vc = [scr[n_q + a][rows_b, :] for a in range(n_k)]
            else:
                qc, doc, Lc_, Dc_ = [q_side[2 * a][rows, :] for a in range(n_q)]
                qn, don, Ln_, Dn_ = [q_side[2 * a + 1][rows, :] for a in range(n_q)]
                kc, vc = [k_side[2 * a][rows, :] for a in range(n_k)]
                kp, vp = [k_side[2 * a + 1][rows, :] for a in range(n_k)]
            qc, qn, doc, don = qc.astype(BF16), qn.astype(BF16), doc.astype(BF16), don.astype(BF16)
            kc, kp, vc, vp = kc.astype(BF16), kp.astype(BF16), vc.astype(BF16), vp.astype(BF16)
            mc, mp, mnext = _dil_masks(sb * m + j, nb)
            zb = jnp.zeros_like(qc)
            dq_tot = jnp.zeros((NK, LANE), F32)
            dk_tot = jnp.zeros((NK, LANE), F32)
            dv_tot = jnp.zeros((NK, LANE), F32)
            for hh in range(2):
                hm = (lane < 64) if hh == 0 else (lane >= 64)

                def bcast(x, hm=hm):
                    return jnp.where(hm, x, pltpu.roll(x, 64, 1))

                Lc, Ln, Dc, Dn = bcast(Lc_), bcast(Ln_), bcast(Dc_), bcast(Dn_)
                qm = jnp.where(hm, qc, zb)
                qnm = jnp.where(hm, qn, zb)
                vcm = jnp.where(hm, vc, zb)
                vpm = jnp.where(hm, vp, zb)
                pc = jnp.exp(jnp.where(mc, _nt(qm, kc) * DIL_SCALE, NEG) - Lc)
                pp = jnp.exp(jnp.where(mp, _nt(qm, kp) * DIL_SCALE, NEG) - Lc)
                dsc = (pc * (_nt(doc, vcm) - Dc) * DIL_SCALE).astype(BF16)
                dsp = (pp * (_nt(doc, vpm) - Dc) * DIL_SCALE).astype(BF16)
                dq_tot = dq_tot + jnp.where(hm, _nn(dsc, kc) + _nn(dsp, kp), 0.0)
                p2 = jnp.exp(jnp.where(mnext, _nt(qnm, kc) * DIL_SCALE, NEG) - Ln)
                ds2 = (p2 * (_nt(don, vcm) - Dn) * DIL_SCALE).astype(BF16)
                dk_tot = dk_tot + _tn(dsc, qm) + _tn(ds2, qnm)
                dv_tot = dv_tot + jnp.where(hm, _tn(pc.astype(BF16), doc) + _tn(p2.astype(BF16), don), 0.0)
            dq_ref[rows, :] = dq_tot
            dk_ref[rows, :] = dk_tot
            dv_ref[rows, :] = dv_tot
            return carry

        lax.fori_loop(0, R // NK, unit, 0, unroll=2)

    qcur, qprv, qnxt = _dil_specs(g, S, 4 * g)
    vcur, vprv, _ = _dil_specs(g, S, C_DV + 4 * g)
    ocur, _, onxt = _dil_specs(g, S, 0)
    out = pl.BlockSpec((R, LANE), lambda sb, c: (sb, c))
    scratch = [pltpu.VMEM((P + R, LANE), F32)] * (n_q + n_k) if m > 1 else []
    return pl.pallas_call(
        body, name=f"dil_attn_bwd{g}", grid=(S // R, 4),
        in_specs=[qcur, qnxt, ocur, onxt, ocur, onxt, ocur, onxt, qcur, qprv, vcur, vprv],
        out_specs=[out, out, out], out_shape=[jax.ShapeDtypeStruct((S, 512), F32)] * 3, scratch_shapes=scratch, compiler_params=_cp(),
    )(q, q, do, do, L, L, Dr, Dr, k, k, zp, zp)


def _dil_pre_bwd(zp, dys, g, tab, dz, col, name):
    S = zp.shape[0]
    T = T_DIL

    def body(x_ref, dy0_ref, dy1_ref, dy2_ref, g_ref, C_ref, S1_ref, S2_ref, dz_in, dz_ref, dg_ref):
        del dz_in
        i = pl.program_id(0)
        C, S1, S2 = C_ref[...], S1_ref[...], S2_ref[...]
        lane = _lane((T, LANE))
        gv = g_ref[...]
        acc = jnp.zeros((1, LANE), F32)
        for b in range(12):
            sl = slice(b * LANE, (b + 1) * LANE)
            x = x_ref[:, sl]
            r = _head_stats(x, lane)
            xn = x * r
            dy_ref = (dy0_ref, dy1_ref, dy2_ref)[b // 4]
            dyn = _rope_t(dy_ref[:, (b % 4) * LANE:(b % 4 + 1) * LANE], C, S1, S2, 32)
            acc = acc + _csum(dyn * xn)
            dxh = dyn * gv
            dz_ref[:, sl] = (r * (dxh - xn * _head_sum(dxh * xn, lane) * (1.0 / DIL_HD))).astype(BF16)

        @pl.when(i == 0)
        def _():
            dg_ref[...] = acc

        @pl.when(i > 0)
        def _():
            dg_ref[...] += acc

    tabspec = pl.BlockSpec((T, LANE), lambda i: (i, 0))
    zc = _zcol(T, 1536, col)
    grp = pl.BlockSpec((T, 512), lambda i: (i, 0))
    return pl.pallas_call(
        body, name=name, grid=(S // T,),
        in_specs=[zc, grp, grp, grp, _full((1, LANE)), tabspec, tabspec, tabspec, pl.BlockSpec(memory_space=pl.ANY)],
        out_specs=[zc, _full((1, LANE))], out_shape=[jax.ShapeDtypeStruct(dz.shape, BF16), jax.ShapeDtypeStruct((1, LANE), F32)],
        input_output_aliases={8: 0}, compiler_params=_cp(),
    )(zp, *dys, g, *tab, dz)


def _dil_dv_into(dvs, dz):
    S = dz.shape[0]
    T = T_ROW

    def body(s0, s1, s2, dz_in, o_ref):
        del dz_in
        for gi, s in enumerate((s0, s1, s2)):
            o_ref[:, gi * 512:(gi + 1) * 512] = s[...].astype(BF16)

    grp = pl.BlockSpec((T, 512), lambda i: (i, 0))
    return pl.pallas_call(
        body, name="dil_dv", grid=(S // T,), in_specs=[grp, grp, grp, pl.BlockSpec(memory_space=pl.ANY)],
        out_specs=_zcol(T, 1536, C_DV), out_shape=jax.ShapeDtypeStruct(dz.shape, BF16), input_output_aliases={3: 0}, compiler_params=_cp(),
    )(*dvs, dz)


T_MRG = 256


def _merge_fwd(P, zp, b_merge):
    S = zp.shape[0]
    T = T_MRG

    def body(p0, p1, p2, m0, m1, m2, b_ref, o_ref):
        acc = jnp.zeros((T, D), F32)
        for j, (p, m) in enumerate(((p0, m0), (p1, m1), (p2, m2))):
            acc = acc + _sig(m[...] + b_ref[:, j * D:(j + 1) * D]) * p[...]
        o_ref[...] = acc.astype(BF16)

    row = pl.BlockSpec((T, D), lambda i: (i, 0))
    return pl.pallas_call(
        body, name="merge_fwd", grid=(S // T,),
        in_specs=[row, row, row] + [_zcol(T, D, C_MERGE + 8 * j) for j in range(3)] + [_full((1, 3 * D))], out_specs=row,
        out_shape=jax.ShapeDtypeStruct((S, D), BF16), compiler_params=_cp(),
    )(*P, zp, zp, zp, b_merge)


def _merge_bwd(dm, Pj, zp, bj, dz, j):
    S = zp.shape[0]
    T = T_MRG

    def body(dm_ref, p_ref, m_ref, b_ref, dz_in, dz_ref, dp_ref, db_ref):
        del dz_in
        i = pl.program_id(0)
        g = _sig(m_ref[...] + b_ref[...])
        dmv = dm_ref[...]
        dp_ref[...] = (dmv * g).astype(BF16)
        dg = dmv * p_ref[...] * g * (1.0 - g)
        dz_ref[...] = dg.astype(BF16)
        part = _csum(dg)

        @pl.when(i == 0)
        def _():
            db_ref[...] = part

        @pl.when(i > 0)
        def _():
            db_ref[...] += part

    row = pl.BlockSpec((T, D), lambda i: (i, 0))
    zc = _zcol(T, D, C_MERGE + 8 * j)
    return pl.pallas_call(
        body, name=f"merge_bwd{j}", grid=(S // T,), in_specs=[row, row, zc, _full((1, D)), pl.BlockSpec(memory_space=pl.ANY)],
        out_specs=[zc, row, _full((1, D))],
        out_shape=[jax.ShapeDtypeStruct(dz.shape, BF16), jax.ShapeDtypeStruct((S, D), BF16), jax.ShapeDtypeStruct((1, D), F32)],
        input_output_aliases={4: 0}, compiler_params=_cp(),
    )(dm, Pj, zp, bj, dz)


def _loss_fwd_bwd(y, target):
    S = y.shape[0]
    T = T_ROW

    def body(y_ref, t_ref, loss_ref, dy_ref):
        i = pl.program_id(0)
        err = y_ref[...] - t_ref[...]
        dy_ref[...] = err * (1.0 / D)
        part = jnp.sum(err * err, keepdims=True).reshape(1, 1) * (0.5 / D)

        @pl.when(i == 0)
        def _():
            loss_ref[...] = part

        @pl.when(i > 0)
        def _():
            loss_ref[...] += part

    row = pl.BlockSpec((T, D), lambda i: (i, 0))
    return pl.pallas_call(
        body, name="loss", grid=(S // T,), in_specs=[row, row], out_specs=[_full((1, 1)), row],
        out_shape=[jax.ShapeDtypeStruct((1, 1), F32), jax.ShapeDtypeStruct((S, D), F32)], compiler_params=_cp(),
    )(y, target)


def _layer_fwd(x, w, tabs):
    mla_tab, dil_tab = tabs
    S = x.shape[0]
    h = _rms_in_fwd(x, w["norm_g"])
    zp = _mm(h, w["w_in"], mode="nn", name="in_proj")
    hs, y_lru = _lru_fwd(zp, w)
    q, k, v = _mla_pre_fwd(zp, w, mla_tab)
    o_mla, lse, y_mla = _mla_attn_fwd(q, k, v, zp)
    qd, kd = _dil_pre_fwd(zp, w, dil_tab)
    og, lg = zip(*[_dil_attn_fwd(qd, kd, zp, g) for g in range(len(DIL_DILATIONS))])
    oc, L, y_dil = _dil_combine(og, lg, zp)
    P = [_mm(y_lru, w["w_lru_o"], mode="nn", name="lru_out"), _mm(y_mla, w["w_mla_o"], mode="nn", name="mla_out"),
         _mm(y_dil, w["w_dil_o"], mode="nn", name="dil_out")]
    merged = _merge_fwd(P, zp, w["b_merge"])
    x_out = _mm(merged, w["w_out"], mode="nn", name="out_proj", add=x)
    saved = dict(x=x, h=h, zp=zp, hs=hs, y=(y_lru, y_mla, y_dil), q=q, k=k, v=v, o_mla=o_mla, lse=lse, qd=qd, kd=kd, oc=oc, L=L, P=P,
                 merged=merged)
    return x_out, saved


def _layer_bwd(dout, w, tabs, sv):
    mla_tab, dil_tab = tabs
    zp = sv["zp"]
    S = zp.shape[0]
    g = {}
    dm = _mm(dout, w["w_out"], mode="nt", name="d_merged")
    g["w_out"] = _mm(sv["merged"], dout, mode="tn", name="dw_out", out_dtype=BF16)
    dz = lax.empty((S, ZW), BF16)
    dP, db = [], []
    for j in range(3):
        dz, dpj, dbj = _merge_bwd(dm, sv["P"][j], zp, w["b_merge"][:, j * D:(j + 1) * D], dz, j)
        dP.append(dpj)
        db.append(dbj)
    g["b_merge"] = jnp.concatenate(db, axis=1)
    names = ("w_lru_o", "w_mla_o", "w_dil_o")
    dy = []
    for j in range(3):
        dy.append(_mm(dP[j], w[names[j]], mode="nt", name="dy_" + names[j]))
        g[names[j]] = _mm(sv["y"][j], dP[j], mode="tn", name="d" + names[j], out_dtype=BF16)
    dz = _lru_gate_bwd(zp, sv["hs"], dy[0], dz)
    dz, g["conv_w"], g["conv_b"], g["w_gx"], g["b_gx"], g["w_ga"], g["b_ga"], g["lam"] = _lru_bwd(zp, sv["hs"], dy[0], w, dz)
    dz, do, Dr = _mla_post_bwd(zp, sv["o_mla"], dy[1], dz)
    dq, dk, dv = _mla_attn_bwd(sv["q"], sv["k"], sv["v"], do, sv["lse"], Dr)
    dz, g["w_uq"], g["w_uk"], g["w_uv"], g["g_cq"], g["g_ckv"], g["g_mq"], g["g_mk"] = _mla_pre_bwd(zp, dq, dk, dv, w, mla_tab, dz)
    dz, dod, Dd = _dil_comb_bwd(zp, sv["oc"], dy[2], dz)
    dqs, dks, dvs = zip(*[_dil_attn_bwd(sv["qd"], sv["kd"], zp, dod, sv["L"], Dd, gi) for gi in range(len(DIL_DILATIONS))])
    dz, g["g_dq"] = _dil_pre_bwd(zp, dqs, w["g_dq"], dil_tab, dz, C_DQ, "dil_pre_bwd_q")
    dz, g["g_dk"] = _dil_pre_bwd(zp, dks, w["g_dk"], dil_tab, dz, C_DK, "dil_pre_bwd_k")
    dz = _dil_dv_into(dvs, dz)
    dh = _mm(dz, w["w_in"], mode="nt", name="d_h")
    g["w_in"] = _mm(sv["h"], dz, mode="tn", name="dw_in", out_dtype=BF16)
    dx, g["norm_g"] = _rms_in_bwd(sv["x"], w["norm_g"], dh, dout)
    return dx, g


def _peers():
    mx, my, mc = lax.axis_index("x"), lax.axis_index("y"), lax.axis_index("c")
    me = 4 * mx + 2 * my + mc
    out = []
    for k in range(1, N_DEV):
        px = 1 - mx if k & 4 else mx
        py = 1 - my if k & 2 else my
        pc = 1 - mc if k & 1 else mc
        out.append(((px, py, pc), 4 * px + 2 * py + pc))
    return me, out


def _whole(ref, p):
    del p
    return ref


def _exchange(srcs, slicers, slices, name):
    n = len(srcs)

    def body(*refs):
        ins, outs = refs[:n], refs[n:2 * n]
        send_sems, recv_sems, local_sems = refs[2 * n:]
        me, peers = _peers()
        mine = [pltpu.make_async_copy(slicers[a](ins[a], me), outs[a].at[me], local_sems.at[a]) for a in range(n)]
        for cp in mine:
            cp.start()
        copies = []
        for k, (peer, pidx) in enumerate(peers):
            for a in range(n):
                cp = pltpu.make_async_remote_copy(
                    src_ref=slicers[a](ins[a], pidx), dst_ref=outs[a].at[me], send_sem=send_sems.at[k * n + a],
                    recv_sem=recv_sems.at[k * n + a], device_id=peer, device_id_type=pl.DeviceIdType.MESH)
                cp.start()
                copies.append(cp)
        for cp in copies + mine:
            cp.wait()

    nsem = (N_DEV - 1) * n
    return pl.pallas_call(
        body, name=name, out_shape=[jax.ShapeDtypeStruct((N_DEV,) + shp, dt) for shp, dt in slices],
        in_specs=[pl.BlockSpec(memory_space=pl.ANY)] * n, out_specs=[pl.BlockSpec(memory_space=pl.ANY)] * n,
        scratch_shapes=[pltpu.SemaphoreType.DMA((nsem,)), pltpu.SemaphoreType.DMA((nsem,)), pltpu.SemaphoreType.DMA((n,))],
        compiler_params=pltpu.CompilerParams(has_side_effects=True),
    )(*srcs)


_HBM = pl.BlockSpec(memory_space=pltpu.HBM)
_SEM = pl.BlockSpec(memory_space=pltpu.SEMAPHORE)
_DATAFLOW = pltpu.SideEffectType.DATAFLOW_SIDE_EFFECTING


def _exchange_start(srcs, slicers, slices, after, name):
    n = len(srcs)
    nsem = (N_DEV - 1) * n
    lands = [lax.empty((N_DEV,) + shp, dt) for shp, dt in slices]

    def body(*refs):
        ins, lands_in = refs[:n], refs[n:2 * n]
        send_sems, recv_sems = refs[2 * n + 1], refs[2 * n + 2]
        token = refs[-1]
        me, peers = _peers()
        for k, (peer, pidx) in enumerate(peers):
            for a in range(n):
                pltpu.make_async_remote_copy(
                    src_ref=slicers[a](ins[a], pidx), dst_ref=lands_in[a].at[me], send_sem=send_sems.at[k * n + a],
                    recv_sem=recv_sems.at[k * n + a], device_id=peer, device_id_type=pl.DeviceIdType.MESH).start()
        token[...] = jnp.zeros_like(token)

    hbm = lambda a: pltpu.with_memory_space_constraint(a, pltpu.HBM)
    return pl.pallas_call(
        body, name=name,
        out_shape=(pltpu.SemaphoreType.DMA((nsem,)), pltpu.SemaphoreType.DMA((nsem,)), *[pltpu.HBM(a.shape, a.dtype) for a in srcs],
                   *[pltpu.HBM(a.shape, a.dtype) for a in lands], jax.ShapeDtypeStruct((SUB, LANE), F32)),
        in_specs=[_HBM] * (2 * n) + [pl.BlockSpec(memory_space=pl.ANY)],
        out_specs=(_SEM, _SEM, *[_HBM] * (2 * n), pl.BlockSpec(memory_space=pltpu.VMEM)),
        input_output_aliases={i: 2 + i for i in range(2 * n)},
        compiler_params=pltpu.CompilerParams(has_side_effects=_DATAFLOW),
    )(*[hbm(a) for a in srcs], *[hbm(a) for a in lands], after)


def _exchange_wait(started, slicers, after, name):
    n = (len(started) - 3) // 2
    send_sems, recv_sems, thru = started[0], started[1], started[2:2 + 2 * n]

    def body(*refs):
        srcs, lands = refs[:n], refs[n:2 * n]
        send_sems, recv_sems = refs[2 * n], refs[2 * n + 1]
        me, peers = _peers()
        for k, (peer, pidx) in enumerate(peers):
            for a in range(n):
                cp = pltpu.make_async_remote_copy(
                    src_ref=slicers[a](srcs[a], pidx), dst_ref=lands[a].at[me], send_sem=send_sems.at[k * n + a],
                    recv_sem=recv_sems.at[k * n + a], device_id=peer, device_id_type=pl.DeviceIdType.MESH)
                cp.wait_send()
                cp.wait_recv()

    outs = pl.pallas_call(
        body, name=name, out_shape=[pltpu.HBM(a.shape, a.dtype) for a in thru],
        in_specs=[_HBM] * (2 * n) + [_SEM, _SEM, pl.BlockSpec(memory_space=pl.ANY)], out_specs=[_HBM] * (2 * n),
        input_output_aliases={i: i for i in range(2 * n)}, compiler_params=pltpu.CompilerParams(has_side_effects=_DATAFLOW),
    )(*thru, send_sems, recv_sems, after)
    return outs[:n], outs[n:]


def _put_own(srcs, lands, slicers, name):
    n = len(srcs)

    def body(*refs):
        ins, outs, sems = refs[:n], refs[2 * n:3 * n], refs[3 * n]
        me, _ = _peers()
        mine = [pltpu.make_async_copy(slicers[a](ins[a], me), outs[a].at[me], sems.at[a]) for a in range(n)]
        for cp in mine:
            cp.start()
        for cp in mine:
            cp.wait()

    return pl.pallas_call(
        body, name=name, out_shape=[jax.ShapeDtypeStruct(a.shape, a.dtype) for a in lands],
        in_specs=[pl.BlockSpec(memory_space=pl.ANY)] * (2 * n), out_specs=[pl.BlockSpec(memory_space=pl.ANY)] * n,
        scratch_shapes=[pltpu.SemaphoreType.DMA((n,))], input_output_aliases={n + a: a for a in range(n)},
        compiler_params=pltpu.CompilerParams(has_side_effects=True),
    )(*srcs, *lands)


WIN = 13 * LANE


def _win_base(s):
    n = s * SHARD_IN
    a0 = n + jnp.where(n >= _KR0, KR_LANE, 0) + jnp.where(n >= _KR0 + 32, 32, 0)
    return jnp.minimum(a0 // LANE, (ZW - WIN) // LANE)


def _win_offsets(s):
    n = s * SHARD_IN + jnp.arange(SHARD_IN)
    o = s * SHARD_IN - _win_base(s) * LANE
    return n, (o, o + KR_LANE, o + LANE - 32)


def _to_window(shard, s):
    n, offs = _win_offsets(s)
    masks = (n < _KR0, (n >= _KR0) & (n < _KR0 + 32), n >= _KR0 + 32)
    zero = jnp.zeros(shard.shape[:2] + (WIN,), shard.dtype)
    out = zero
    for m, o in zip(masks, offs):
        out = out + lax.dynamic_update_slice(zero, jnp.where(m[None, None, :], shard, jnp.zeros_like(shard)), (0, 0, o))
    return out


def _from_window(win, s):
    n, offs = _win_offsets(s)
    a, b, c = [lax.dynamic_slice(win, (0, 0, o), win.shape[:2] + (SHARD_IN,)) for o in offs]
    return jnp.where((n < _KR0)[None, None, :], a, jnp.where((n < _KR0 + 32)[None, None, :], b, c))


def _win_base_static(s):
    n = s * SHARD_IN
    a0 = n + (KR_LANE if n >= _KR0 else 0) + (32 if n >= _KR0 + 32 else 0)
    return min(a0 // LANE, (ZW - WIN) // LANE)


def _assemble_w_in(gw):
    tr = 128
    bases = [_win_base_static(s) for s in range(N_DEV)]

    def body(g_ref, o_ref):
        for j in range(ZW // LANE):
            acc = None
            for s in range(N_DEV):
                if bases[s] <= j < bases[s] + WIN // LANE:
                    piece = g_ref[s, :, (j - bases[s]) * LANE:(j - bases[s] + 1) * LANE]
                    acc = piece if acc is None else acc + piece
            o_ref[:, j * LANE:(j + 1) * LANE] = acc

    return pl.pallas_call(
        body, name="assemble_w_in", grid=(D // tr,), in_specs=[pl.BlockSpec((N_DEV, tr, WIN), lambda i: (0, i, 0))],
        out_specs=pl.BlockSpec((tr, ZW), lambda i: (i, 0)), out_shape=jax.ShapeDtypeStruct((D, ZW), gw.dtype), compiler_params=_cp(),
    )(gw)


def _cols(width):
    return lambda ref, p: ref.at[:, pl.ds(pl.multiple_of(p * width, width), width)]


def _rows(height):
    return lambda ref, p: ref.at[pl.ds(pl.multiple_of(p * height, height), height), :]


SCATTER = {
    'w_in': (lambda ref, p: ref.at[:, pl.ds(pl.multiple_of(_win_base(p) * LANE, LANE), WIN)], (D, WIN), BF16),
    'conv_w': (_cols(LANE), (4, LANE), F32),
    'w_lru_o': (_rows(LANE), (LANE, D), BF16),
    'w_uq': (_cols(LANE), (256, LANE), F32),
    'w_ukv': (_cols(LANE), (128, LANE), F32),
    'w_mla_o': (_cols(LANE), (512, LANE), BF16),
    'w_dil_o': (_cols(LANE), (512, LANE), BF16),
    'w_out': (_rows(LANE), (LANE, D), BF16),
}


PACK_ROWS = 512


def _sum8(buf, name):
    _, R, C = buf.shape
    tr = R
    while tr * C * 4 * N_DEV > (1 << 22) and tr % 16 == 0:
        tr //= 2

    def body(b_ref, o_ref):
        acc = b_ref[0].astype(F32)
        for s in range(1, N_DEV):
            acc = acc + b_ref[s].astype(F32)
        o_ref[...] = acc

    return pl.pallas_call(
        body, name=name, grid=(R // tr,), in_specs=[pl.BlockSpec((N_DEV, tr, C), lambda i: (0, i, 0))],
        out_specs=pl.BlockSpec((tr, C), lambda i: (i, 0)), out_shape=jax.ShapeDtypeStruct((R, C), F32), compiler_params=_cp(),
    )(buf)


def _pack(arrs, dtype, lead):
    flat = [a.astype(dtype).reshape(a.shape[:lead] + (-1,)) for a in arrs]
    cat = jnp.concatenate(flat, axis=-1)
    n = cat.shape[-1]
    unit = PACK_ROWS * LANE
    pad = (-n) % unit
    if pad:
        cat = jnp.pad(cat, [(0, 0)] * lead + [(0, pad)])
    return cat.reshape(cat.shape[:lead] + ((n + pad) // LANE, LANE))


def _unpack(buf, shapes, lead):
    flat = buf.reshape(buf.shape[:lead] + (-1,))
    out, off = [], 0
    for shp in shapes:
        n = int(np.prod(shp))
        out.append(flat[..., off:off + n].reshape(buf.shape[:lead] + tuple(shp)))
        off += n
    return out


def _adamw(w, g, m, v, name):
    rows, cols = w.shape
    tr = rows
    while tr * cols * 4 > (3 << 19) and tr % 16 == 0:
        tr //= 2
    c1 = 1.0 - ADAM_B1 ** ADAM_STEP
    c2 = 1.0 - ADAM_B2 ** ADAM_STEP

    def body(w_ref, g_ref, m_ref, v_ref, d_ref, mo_ref, vo_ref):
        gv = g_ref[...]
        mn = ADAM_B1 * m_ref[...] + (1.0 - ADAM_B1) * gv
        vn = ADAM_B2 * v_ref[...] + (1.0 - ADAM_B2) * (gv * gv)
        mo_ref[...] = mn
        vo_ref[...] = vn
        d_ref[...] = -ADAM_LR * ((mn / c1) / (jnp.sqrt(vn / c2) + ADAM_EPS) + ADAM_WD * w_ref[...])

    spec = pl.BlockSpec((tr, cols), lambda i: (i, 0))
    return pl.pallas_call(
        body, name=name, grid=(rows // tr,), in_specs=[spec] * 4, out_specs=[spec] * 3,
        out_shape=[jax.ShapeDtypeStruct((rows, cols), F32)] * 3, compiler_params=_cp(),
    )(w, g, m, v)


IN_NAMES = ['x', 'positions', 'norm_g', 'w_in', 'conv_w', 'conv_b', 'w_gate_x', 'b_gate_x', 'w_gate_a', 'b_gate_a', 'lru_lambda', 'w_lru_o',
            'cq_norm_g', 'ckv_norm_g', 'w_uq', 'w_ukv', 'mla_q_norm_g', 'mla_k_norm_g', 'w_mla_o', 'dil_q_norm_g', 'dil_k_norm_g', 'w_dil_o',
            'b_merge', 'w_out']
WEIGHTS = IN_NAMES[2:]
REPLICATED = [n for n in WEIGHTS if n not in SCATTER]

_KR0 = C_KR * LANE


GATHERED = ['w_in', 'w_lru_o', 'w_uq', 'w_ukv', 'w_mla_o', 'w_dil_o', 'w_out', 'conv_w']


def _local_weights(wd, me):
    loc = {n: wd[n].astype(BF16) for n in GATHERED[:-1]}
    loc['w_in'] = _to_window(loc['w_in'], me)
    loc['w_uq'] = jnp.pad(loc['w_uq'], ((0, 0), (0, 0), (0, LANE - MLA_QK)))
    loc['conv_w'] = wd['conv_w']
    return [[loc[n][l] for n in GATHERED] for l in range(DEPTH)]


def _layer_weights(gathered, rep, l):
    gw = dict(zip(GATHERED, gathered))
    by_rows = lambda a: a.reshape(-1, a.shape[-1])
    by_cols = lambda a: jnp.swapaxes(a, 0, 1).reshape(a.shape[1], -1)
    ukv = jnp.swapaxes(gw['w_ukv'], 0, 1)
    g96 = lambda a: jnp.pad(a[l].reshape(1, MLA_QK), ((0, 0), (0, LANE - MLA_QK)))
    g64 = lambda a: jnp.tile(a[l].reshape(1, DIL_HD), (1, 2))
    return dict(
        norm_g=rep['norm_g'][l].reshape(1, D), w_in=_assemble_w_in(gw['w_in']),
        conv_w=by_cols(gw['conv_w']), conv_b=rep['conv_b'][l].reshape(1, D),
        w_gx=rep['w_gate_x'][l].astype(BF16), b_gx=rep['b_gate_x'][l].reshape(8, 1, LANE),
        w_ga=rep['w_gate_a'][l].astype(BF16), b_ga=rep['b_gate_a'][l].reshape(8, 1, LANE),
        lam=rep['lru_lambda'][l].reshape(1, D),
        w_lru_o=by_rows(gw['w_lru_o']), w_mla_o=by_cols(gw['w_mla_o']), w_dil_o=by_cols(gw['w_dil_o']), w_out=by_rows(gw['w_out']),
        g_cq=rep['cq_norm_g'][l].reshape(1, 256), g_ckv=rep['ckv_norm_g'][l].reshape(1, 128),
        w_uq=by_cols(gw['w_uq']), w_uk=jnp.pad(ukv[:, :, :64], ((0, 0), (0, 0), (0, 64))).reshape(128, 1024),
        w_uv=ukv[:, :, 64:].reshape(128, 512),
        g_mq=g96(rep['mla_q_norm_g']), g_mk=g96(rep['mla_k_norm_g']), g_dq=g64(rep['dil_q_norm_g']), g_dk=g64(rep['dil_k_norm_g']),
        b_merge=rep['b_merge'][l].reshape(1, 3 * D),
    )


def _sharded_grads(g):
    uk = g['w_uk'].reshape(128, 8, 128)[:, :, :64]
    uv = g['w_uv'].reshape(128, 8, 64)
    d = {'w_in': g['w_in'], 'conv_w': g['conv_w'], 'w_lru_o': g['w_lru_o'], 'w_uq': g['w_uq'],
         'w_ukv': jnp.concatenate([uk, uv], axis=-1).reshape(128, 1024), 'w_mla_o': g['w_mla_o'], 'w_dil_o': g['w_dil_o'],
         'w_out': g['w_out']}
    return [d[n] for n in SCATTER]


def _replicated_grads(g):
    return {
        'norm_g': g['norm_g'].reshape(D), 'conv_b': g['conv_b'].reshape(D),
        'w_gate_x': g['w_gx'], 'b_gate_x': g['b_gx'].reshape(8, LANE), 'w_gate_a': g['w_ga'], 'b_gate_a': g['b_ga'].reshape(8, LANE),
        'lru_lambda': g['lam'].reshape(D), 'cq_norm_g': g['g_cq'].reshape(256), 'ckv_norm_g': g['g_ckv'].reshape(128),
        'mla_q_norm_g': g['g_mq'][0, :MLA_QK], 'mla_k_norm_g': g['g_mk'][0, :MLA_QK],
        'dil_q_norm_g': g['g_dq'][0, :DIL_HD] + g['g_dq'][0, DIL_HD:], 'dil_k_norm_g': g['g_dk'][0, :DIL_HD] + g['g_dk'][0, DIL_HD:],
        'b_merge': g['b_merge'].reshape(3 * D),
    }


def kernel(x, positions, norm_g, w_in, conv_w, conv_b, w_gate_x, b_gate_x, w_gate_a, b_gate_a, lru_lambda, w_lru_o, cq_norm_g, ckv_norm_g, w_uq, w_ukv, mla_q_norm_g, mla_k_norm_g, w_mla_o, dil_q_norm_g, dil_k_norm_g, w_dil_o, b_merge, w_out, loss_target, m_norm_g, m_w_in, m_conv_w, m_conv_b, m_w_gate_x, m_b_gate_x, m_w_gate_a, m_b_gate_a, m_lru_lambda, m_w_lru_o, m_cq_norm_g, m_ckv_norm_g, m_w_uq, m_w_ukv, m_mla_q_norm_g, m_mla_k_norm_g, m_w_mla_o, m_dil_q_norm_g, m_dil_k_norm_g, m_w_dil_o, m_b_merge, m_w_out, v_norm_g, v_w_in, v_conv_w, v_conv_b, v_w_gate_x, v_b_gate_x, v_w_gate_a, v_b_gate_a, v_lru_lambda, v_w_lru_o, v_cq_norm_g, v_ckv_norm_g, v_w_uq, v_w_ukv, v_mla_q_norm_g, v_mla_k_norm_g, v_w_mla_o, v_dil_q_norm_g, v_dil_k_norm_g, v_w_dil_o, v_b_merge, v_w_out):
    args = (x, positions, norm_g, w_in, conv_w, conv_b, w_gate_x, b_gate_x, w_gate_a, b_gate_a, lru_lambda, w_lru_o, cq_norm_g, ckv_norm_g, w_uq, w_ukv, mla_q_norm_g, mla_k_norm_g, w_mla_o, dil_q_norm_g, dil_k_norm_g, w_dil_o, b_merge, w_out)
    moments_m = (m_norm_g, m_w_in, m_conv_w, m_conv_b, m_w_gate_x, m_b_gate_x, m_w_gate_a, m_b_gate_a, m_lru_lambda, m_w_lru_o, m_cq_norm_g, m_ckv_norm_g, m_w_uq, m_w_ukv, m_mla_q_norm_g, m_mla_k_norm_g, m_w_mla_o, m_dil_q_norm_g, m_dil_k_norm_g, m_w_dil_o, m_b_merge, m_w_out)
    moments_v = (v_norm_g, v_w_in, v_conv_w, v_conv_b, v_w_gate_x, v_b_gate_x, v_w_gate_a, v_b_gate_a, v_lru_lambda, v_w_lru_o, v_cq_norm_g, v_ckv_norm_g, v_w_uq, v_w_ukv, v_mla_q_norm_g, v_mla_k_norm_g, v_w_mla_o, v_dil_q_norm_g, v_dil_k_norm_g, v_w_dil_o, v_b_merge, v_w_out)
    a = dict(zip(IN_NAMES, args))
    wd = {n: a[n] for n in WEIGHTS}
    md = dict(zip(WEIGHTS, moments_m))
    vd = dict(zip(WEIGHTS, moments_v))

    me = 4 * lax.axis_index("x") + 2 * lax.axis_index("y") + lax.axis_index("c")

    assert DEPTH == 2
    xs, tabs = x[0], _rope_tables(positions[0])
    whole = [_whole] * len(GATHERED)
    slicers = [SCATTER[n][0] for n in SCATTER]
    grad_slices = [SCATTER[n][1:] for n in SCATTER]

    local = _local_weights(wd, me)
    w_slices = [(a.shape, a.dtype) for a in local[0]]
    landed0 = _exchange(local[0], whole, w_slices, "gather_w0")
    flying = _exchange_start(local[1], whole, w_slices, landed0[0], "gather_w1_start")
    rep0 = dict(wd, norm_g=wd['norm_g'] + flying[-1][0, 0])
    w0 = _layer_weights(landed0, rep0, 0)
    x1, saved0 = _layer_fwd(xs, w0, tabs)
    sent, landed1 = _exchange_wait(flying, whole, x1, "gather_w1_wait")
    w1 = _layer_weights(_put_own(sent, landed1, whole, "gather_w1_own"), wd, 1)
    x2, saved1 = _layer_fwd(x1, w1, tabs)
    loss, dx2 = _loss_fwd_bwd(x2, loss_target[0])
    loss = loss[0, 0]

    dx1, g1 = _layer_bwd(dx2, w1, tabs, saved1)
    flying = _exchange_start(_sharded_grads(g1), slicers, grad_slices, dx1, "scatter_g1_start")
    w0 = dict(w0, b_merge=w0['b_merge'] + flying[-1][0, 0])
    grad_x, g0 = _layer_bwd(dx1, w0, tabs, saved0)
    sent, got1 = _exchange_wait(flying, slicers, grad_x, "scatter_g1_wait")
    got1 = _put_own(sent, got1, slicers, "scatter_g1_own")
    got0 = _exchange(_sharded_grads(g0), slicers, grad_slices, "scatter_g0")
    glayers = [g0, g1]
    sharded = list(SCATTER)
    gsh = {}
    for i, n in enumerate(sharded):
        gsh[n] = jnp.stack([_sum8(got[i], f"sum_{n}_{l}") for l, got in enumerate((got0, got1))])
    gsh['w_in'] = _from_window(gsh['w_in'], me)
    gsh['w_uq'] = gsh['w_uq'][:, :, :MLA_QK]
    rshapes = [wd[n].shape for n in REPLICATED]
    rgrads = [_replicated_grads(g) for g in glayers]
    rpacked = _pack([jnp.stack([rgrads[l][n] for l in range(DEPTH)]) for n in REPLICATED], F32, 0)
    grep = _sum8(_exchange([rpacked], [_whole], [(rpacked.shape, F32)], "gather_grads")[0], "sum_replicated")

    out_g, out_d, out_m, out_v = {}, {}, {}, {}
    d_, m_, v_ = _adamw(_pack([wd[n] for n in REPLICATED], F32, 0), grep, _pack([md[n] for n in REPLICATED], F32, 0),
                        _pack([vd[n] for n in REPLICATED], F32, 0), "adamw_replicated")
    for dst, buf in ((out_g, grep), (out_d, d_), (out_m, m_), (out_v, v_)):
        dst.update(zip(REPLICATED, _unpack(buf, rshapes, 0)))
    for n in sharded:
        shp = wd[n].shape
        two = (shp[0] * shp[1], shp[2])
        d_, m_, v_ = _adamw(wd[n].reshape(two), gsh[n].reshape(two), md[n].reshape(two), vd[n].reshape(two), "adamw_" + n)
        out_g[n], out_d[n], out_m[n], out_v[n] = gsh[n], d_.reshape(shp), m_.reshape(shp), v_.reshape(shp)

    loss = lax.psum(loss, ("x", "y", "c"))
    return (loss, grad_x[None], *[out_g[n] for n in WEIGHTS], *[out_d[n] for n in WEIGHTS], *[out_m[n] for n in WEIGHTS],
            *[out_v[n] for n in WEIGHTS])
```

```python
import functools

import numpy as np
import jax
import jax.numpy as jnp
from jax import lax
from jax.experimental import pallas as pl
from jax.experimental.pallas import tpu as pltpu

F32 = jnp.float32
BF16 = jnp.bfloat16

N_DEV = 8
D = 1024
DEPTH = 2
EPS = 1e-6
ROPE_THETA = 10000.0
LRU_C = 8.0
LANE = 128
SUB = 8
IN_WIDTH = 11168
SHARD_IN = IN_WIDTH // N_DEV

C_LRUX, C_LRUG, C_CQ, C_CKV, C_KR, C_MLAG, C_DQ, C_DK, C_DV, C_DILG, C_MERGE = 0, 8, 16, 18, 19, 20, 24, 36, 48, 60, 64
ZW = 88 * LANE
KR_LANE = 64

MLA_QK = 96
MLA_SCALE = MLA_QK ** -0.5
DIL_HD = 64
DIL_SCALE = DIL_HD ** -0.5
DIL_DILATIONS = (1, 4, 16)
NK = 128

ADAM_LR, ADAM_B1, ADAM_B2, ADAM_EPS, ADAM_WD, ADAM_STEP = 0.001, 0.9, 0.999, 1e-08, 0.01, 10

NEG = -1e30
VMEM_LIMIT = 48 * 1024 * 1024


def _cp(**kw):
    return pltpu.CompilerParams(vmem_limit_bytes=VMEM_LIMIT, **kw)


def _sig(x):
    return 1.0 / (1.0 + jnp.exp(-x))


def _silu(x):
    return x * _sig(x)


def _dsilu(x):
    s = _sig(x)
    return s * (1.0 + x * (1.0 - s))


def _dot(a, b, dims):
    return lax.dot_general(a, b, (dims, ((), ())), preferred_element_type=F32)


def _nn(a, b):
    return _dot(a, b, ((1,), (0,)))


def _nt(a, b):
    return _dot(a, b, ((1,), (1,)))


def _tn(a, b):
    return _dot(a, b, ((0,), (0,)))


def _rsum(x):
    return jnp.sum(x, axis=-1, keepdims=True)


def _csum(x):
    return jnp.sum(x, axis=0, keepdims=True)


def _mm(a, b, *, mode, name, out_dtype=F32, add=None, tm=1024, tn=1024, tk=1024):
    if mode == "nn":
        (M, K), (K2, N) = a.shape, b.shape
    elif mode == "nt":
        (M, K), (N, K2) = a.shape, b.shape
    else:
        (K, M), (K2, N) = a.shape, b.shape
    assert K == K2
    tm, tn, tk = min(tm, M), min(tn, N), min(tk, K)
    assert M % tm == 0 and N % tn == 0 and K % tk == 0
    nk = K // tk
    fn = {"nn": _nn, "nt": _nt, "tn": _tn}[mode]
    has_add = add is not None

    def body(*refs):
        a_ref, b_ref = refs[0], refs[1]
        add_ref = refs[2] if has_add else None
        o_ref = refs[3] if has_add else refs[2]
        part = fn(a_ref[...].astype(BF16), b_ref[...].astype(BF16))

        def fin(acc):
            if has_add:
                acc = acc + add_ref[...]
            o_ref[...] = acc.astype(out_dtype)

        if nk == 1:
            fin(part)
        else:
            acc_ref = refs[-1]
            k = pl.program_id(2)

            @pl.when(k == 0)
            def _():
                acc_ref[...] = part

            @pl.when(k > 0)
            def _():
                acc_ref[...] += part

            @pl.when(k == nk - 1)
            def _():
                fin(acc_ref[...])

    a_spec = pl.BlockSpec((tk, tm), lambda i, j, k: (k, i)) if mode == "tn" else pl.BlockSpec((tm, tk), lambda i, j, k: (i, k))
    b_spec = pl.BlockSpec((tn, tk), lambda i, j, k: (j, k)) if mode == "nt" else pl.BlockSpec((tk, tn), lambda i, j, k: (k, j))
    o_spec = pl.BlockSpec((tm, tn), lambda i, j, k: (i, j))
    in_specs, args = [a_spec, b_spec], [a, b]
    if has_add:
        in_specs.append(o_spec)
        args.append(add)
    return pl.pallas_call(
        body, name=name, grid=(M // tm, N // tn, nk), in_specs=in_specs, out_specs=o_spec,
        out_shape=jax.ShapeDtypeStruct((M, N), out_dtype),
        scratch_shapes=[pltpu.VMEM((tm, tn), F32)] if nk > 1 else [],
        compiler_params=_cp(dimension_semantics=("parallel", "parallel", "arbitrary")),
    )(*args)


T_ROW = 512


def _rms_in_fwd(x, g):
    S = x.shape[0]
    T = T_ROW

    def body(x_ref, g_ref, h_ref):
        xv = x_ref[...]
        r = lax.rsqrt(jnp.mean(xv * xv, axis=-1, keepdims=True) + EPS)
        h_ref[...] = (xv * r * g_ref[...]).astype(BF16)

    return pl.pallas_call(
        body, name="rms_in_fwd", grid=(S // T,),
        in_specs=[pl.BlockSpec((T, D), lambda i: (i, 0)), pl.BlockSpec((1, D), lambda i: (0, 0))],
        out_specs=pl.BlockSpec((T, D), lambda i: (i, 0)),
        out_shape=jax.ShapeDtypeStruct((S, D), BF16), compiler_params=_cp(),
    )(x, g)


def _rms_in_bwd(x, g, dh, dres):
    S = x.shape[0]
    T = T_ROW

    def body(x_ref, g_ref, dh_ref, dr_ref, dx_ref, dg_ref):
        i = pl.program_id(0)
        xv = x_ref[...]
        r = lax.rsqrt(jnp.mean(xv * xv, axis=-1, keepdims=True) + EPS)
        xn = xv * r
        dy = dh_ref[...]
        part = _csum(dy * xn)

        @pl.when(i == 0)
        def _():
            dg_ref[...] = part

        @pl.when(i > 0)
        def _():
            dg_ref[...] += part

        dxh = dy * g_ref[...]
        dx_ref[...] = dr_ref[...] + r * (dxh - xn * jnp.mean(dxh * xn, axis=-1, keepdims=True))

    row = pl.BlockSpec((T, D), lambda i: (i, 0))
    vec = pl.BlockSpec((1, D), lambda i: (0, 0))
    return pl.pallas_call(
        body, name="rms_in_bwd", grid=(S // T,), in_specs=[row, vec, row, row], out_specs=[row, vec],
        out_shape=[jax.ShapeDtypeStruct((S, D), F32), jax.ShapeDtypeStruct((1, D), F32)], compiler_params=_cp(),
    )(x, g, dh, dres)


T_LRU = 512


def _neg_expm1(y):
    ser = -y * (1.0 + y * 0.5 * (1.0 + y * (1.0 / 3.0) * (1.0 + y * 0.25 * (1.0 + y * 0.2))))
    return jnp.where(y > -0.03, ser, 1.0 - jnp.exp(y))


def _softplus_neg(lam):
    e = jnp.exp(-jnp.abs(lam))
    l1p = jnp.where(e < 0.01, e * (1.0 - e * (0.5 - e * (1.0 / 3.0 - e * 0.25))), jnp.log(1.0 + e))
    return jnp.maximum(-lam, 0.0) + l1p


def _scan_fwd(a, b, T):
    row = lax.broadcasted_iota(jnp.int32, a.shape, 0)
    d = 1
    while d < T:
        m = row >= d
        b = jnp.where(m, a * pltpu.roll(b, d, 0) + b, b)
        a = jnp.where(m, a * pltpu.roll(a, d, 0), a)
        d *= 2
    return a, b


def _scan_bwd(a, b, T):
    row = lax.broadcasted_iota(jnp.int32, a.shape, 0)
    d = 1
    while d < T:
        m = row < T - d
        b = jnp.where(m, a * pltpu.roll(b, T - d, 0) + b, b)
        a = jnp.where(m, a * pltpu.roll(a, T - d, 0), a)
        d *= 2
    return b


def _lru_common(x, prev, first, cw_ref, cb_ref, wgx_ref, bgx_ref, wga_ref, bga_ref, lam_ref, T):
    row = lax.broadcasted_iota(jnp.int32, x.shape, 0)
    prev = jnp.where(first, 0.0, prev)
    xs = []
    for j in (3, 2, 1):
        pv = jnp.tile(pltpu.roll(prev, j, 0), (T // SUB, 1))
        xs.append(jnp.where(row < j, pv, pltpu.roll(x, j, 0)))
    xs.append(x)
    xc = cb_ref[...] + cw_ref[0:1, :] * xs[0] + cw_ref[1:2, :] * xs[1] + cw_ref[2:3, :] * xs[2] + cw_ref[3:4, :] * xs[3]
    xcb = xc.astype(BF16)
    gx = _sig(_nn(xcb, wgx_ref[0]) + bgx_ref[0])
    ga = _sig(_nn(xcb, wga_ref[0]) + bga_ref[0])
    sp = _softplus_neg(lam_ref[...])
    log_a = -LRU_C * ga * sp
    a = jnp.exp(log_a)
    mult = jnp.sqrt(_neg_expm1(2.0 * log_a))
    return xs, xc, xcb, gx, ga, sp, a, mult


def _lru_specs(T, tmap):
    def at(col0):
        return pl.BlockSpec((T, LANE), lambda n, i: (tmap(i), col0 + n))

    def prev(col0):
        return pl.BlockSpec((SUB, LANE), lambda n, i: (jnp.maximum(tmap(i) * (T // SUB) - 1, 0), col0 + n))

    small = [
        pl.BlockSpec((4, LANE), lambda n, i: (0, n)),
        pl.BlockSpec((1, LANE), lambda n, i: (0, n)),
        pl.BlockSpec((1, LANE, LANE), lambda n, i: (n, 0, 0)),
        pl.BlockSpec((1, 1, LANE), lambda n, i: (n, 0, 0)),
        pl.BlockSpec((1, LANE, LANE), lambda n, i: (n, 0, 0)),
        pl.BlockSpec((1, 1, LANE), lambda n, i: (n, 0, 0)),
        pl.BlockSpec((1, LANE), lambda n, i: (0, n)),
    ]
    return at, prev, small


def _lru_fwd(zp, w):
    S = zp.shape[0]
    T = T_LRU
    at, prev, small = _lru_specs(T, lambda i: i)

    def body(x_ref, xp_ref, g_ref, cw_ref, cb_ref, wgx_ref, bgx_ref, wga_ref, bga_ref, lam_ref, hs_ref, y_ref, carry_ref):
        i = pl.program_id(1)

        @pl.when(i == 0)
        def _():
            carry_ref[...] = jnp.zeros_like(carry_ref)

        x = x_ref[...]
        _, xc, _, gx, _, _, a, mult = _lru_common(x, xp_ref[...], i == 0, cw_ref, cb_ref, wgx_ref, bgx_ref, wga_ref, bga_ref, lam_ref, T)
        A, B = _scan_fwd(a, mult * gx * xc, T)
        h = B + A * carry_ref[SUB - 1:SUB, :]
        hs_ref[...] = h
        carry_ref[...] = hs_ref[T - SUB:T, :]
        y_ref[...] = (h * _silu(g_ref[...])).astype(BF16)

    out = pl.BlockSpec((T, LANE), lambda n, i: (i, n))
    return pl.pallas_call(
        body, name="lru_fwd", grid=(8, S // T),
        in_specs=[at(C_LRUX), prev(C_LRUX), at(C_LRUG)] + small, out_specs=[out, out],
        out_shape=[jax.ShapeDtypeStruct((S, D), F32), jax.ShapeDtypeStruct((S, D), BF16)],
        scratch_shapes=[pltpu.VMEM((SUB, LANE), F32)],
        compiler_params=_cp(dimension_semantics=("parallel", "arbitrary")),
    )(zp, zp, zp, w["conv_w"], w["conv_b"], w["w_gx"], w["b_gx"], w["w_ga"], w["b_ga"], w["lam"])


def _lru_bwd(zp, hs, dy, w, dz):
    S = zp.shape[0]
    T = T_LRU
    nT = S // T
    at, prev, small = _lru_specs(T, lambda i: nT - 1 - i)

    def body(x_ref, xp_ref, g_ref, h_ref, hp_ref, dy_ref, cw_ref, cb_ref, wgx_ref, bgx_ref, wga_ref, bga_ref, lam_ref, dz_in,
             dzx_ref, dcw_ref, dcb_ref, dwgx_ref, dbgx_ref, dwga_ref, dbga_ref, dlam_ref, carry_ref, head_ref):
        del dz_in
        j = pl.program_id(1)
        it = nT - 1 - j

        @pl.when(j == 0)
        def _():
            for r in (carry_ref, head_ref, dcw_ref, dcb_ref, dwgx_ref, dbgx_ref, dwga_ref, dbga_ref, dlam_ref):
                r[...] = jnp.zeros_like(r)

        first = it == 0
        x = x_ref[...]
        xs, xc, xcb, gx, ga, sp, a, mult = _lru_common(x, xp_ref[...], first, cw_ref, cb_ref, wgx_ref, bgx_ref, wga_ref, bga_ref, lam_ref, T)
        row = lax.broadcasted_iota(jnp.int32, x.shape, 0)
        u = gx * xc
        h = h_ref[...]
        hp = jnp.where(first, 0.0, hp_ref[...])
        hm1 = jnp.where(row < 1, jnp.tile(pltpu.roll(hp, 1, 0), (T // SUB, 1)), pltpu.roll(h, 1, 0))
        dho = dy_ref[...] * _silu(g_ref[...])
        gin = jnp.where(row == T - 1, dho + carry_ref[0:1, :], dho)
        abar = jnp.where(row == T - 1, 0.0, pltpu.roll(a, T - 1, 0))
        dh = _scan_bwd(abar, gin, T)
        carry_ref[...] = (a * dh)[0:SUB, :]
        da = dh * hm1
        dmult = dh * u
        du = dh * mult
        dgx = du * xc
        dxc = du * gx
        dlog_a = da * a - dmult * a * a / mult
        dga = dlog_a * (-LRU_C * sp)
        lam = lam_ref[...]
        dlam_ref[...] += _csum(dlog_a * (-LRU_C * ga)) * (-1.0 / (1.0 + jnp.exp(lam)))
        dpa = dga * ga * (1.0 - ga)
        dpx = dgx * gx * (1.0 - gx)
        dpab, dpxb = dpa.astype(BF16), dpx.astype(BF16)
        dxc = dxc + _nt(dpxb, wgx_ref[0]) + _nt(dpab, wga_ref[0])
        dwgx_ref[0] += _tn(xcb, dpxb)
        dwga_ref[0] += _tn(xcb, dpab)
        dbgx_ref[0] += _csum(dpx)
        dbga_ref[0] += _csum(dpa)
        dcb_ref[...] += _csum(dxc)
        for k in range(4):
            dcw_ref[k:k + 1, :] += _csum(dxc * xs[k])
        head = head_ref[...]
        dx = cw_ref[3:4, :] * dxc
        for jj in (1, 2, 3):
            hv = jnp.tile(pltpu.roll(head, SUB - jj, 0), (T // SUB, 1))
            dx = dx + cw_ref[3 - jj:4 - jj, :] * jnp.where(row >= T - jj, hv, pltpu.roll(dxc, T - jj, 0))
        head_ref[...] = dxc[0:SUB, :]
        dzx_ref[...] = dx.astype(BF16)

    def acc(shape, imap):
        return pl.BlockSpec(shape, imap)

    out_specs = [
        pl.BlockSpec((T, LANE), lambda n, i: (nT - 1 - i, C_LRUX + n)),
        acc((4, LANE), lambda n, i: (0, n)), acc((1, LANE), lambda n, i: (0, n)),
        acc((1, LANE, LANE), lambda n, i: (n, 0, 0)), acc((1, 1, LANE), lambda n, i: (n, 0, 0)),
        acc((1, LANE, LANE), lambda n, i: (n, 0, 0)), acc((1, 1, LANE), lambda n, i: (n, 0, 0)),
        acc((1, LANE), lambda n, i: (0, n)),
    ]
    out_shape = [
        jax.ShapeDtypeStruct(dz.shape, BF16),
        jax.ShapeDtypeStruct((4, D), F32), jax.ShapeDtypeStruct((1, D), F32),
        jax.ShapeDtypeStruct((8, LANE, LANE), F32), jax.ShapeDtypeStruct((8, 1, LANE), F32),
        jax.ShapeDtypeStruct((8, LANE, LANE), F32), jax.ShapeDtypeStruct((8, 1, LANE), F32),
        jax.ShapeDtypeStruct((1, D), F32),
    ]
    dyspec = pl.BlockSpec((T, LANE), lambda n, i: (nT - 1 - i, n))
    hprev = pl.BlockSpec((SUB, LANE), lambda n, i: (jnp.maximum((nT - 1 - i) * (T // SUB) - 1, 0), n))
    return pl.pallas_call(
        body, name="lru_bwd", grid=(8, nT),
        in_specs=[at(C_LRUX), prev(C_LRUX), at(C_LRUG), dyspec, hprev, dyspec] + small + [pl.BlockSpec(memory_space=pl.ANY)],
        out_specs=out_specs, out_shape=out_shape,
        scratch_shapes=[pltpu.VMEM((SUB, LANE), F32), pltpu.VMEM((SUB, LANE), F32)],
        input_output_aliases={13: 0},
        compiler_params=_cp(dimension_semantics=("parallel", "arbitrary")),
    )(zp, zp, zp, hs, hs, dy, w["conv_w"], w["conv_b"], w["w_gx"], w["b_gx"], w["w_ga"], w["b_ga"], w["lam"], dz)


def _lru_gate_bwd(zp, hs, dy, dz):
    S = zp.shape[0]
    T = T_ROW

    def body(g_ref, h_ref, dy_ref, dz_in, o_ref):
        del dz_in
        o_ref[...] = (dy_ref[...] * h_ref[...] * _dsilu(g_ref[...])).astype(BF16)

    row = pl.BlockSpec((T, D), lambda i: (i, 0))
    zc = pl.BlockSpec((T, D), lambda i: (i, C_LRUG // 8))
    return pl.pallas_call(
        body, name="lru_gate_bwd", grid=(S // T,), in_specs=[zc, row, row, pl.BlockSpec(memory_space=pl.ANY)], out_specs=zc,
        out_shape=jax.ShapeDtypeStruct(dz.shape, BF16), input_output_aliases={3: 0}, compiler_params=_cp(),
    )(zp, hs, dy, dz)


def _rope_tables(pos):
    pf = pos.astype(F32)[:, None]

    def cs(d):
        inv = ROPE_THETA ** (-jnp.arange(0, d, 2, dtype=F32) / d)
        ang = pf * inv
        return jnp.cos(ang), jnp.sin(ang)

    S = pos.shape[0]
    c, s = cs(32)
    one, zero = jnp.ones((S, 64), F32), jnp.zeros((S, 16), F32)
    z32, z64 = jnp.zeros((S, 32), F32), jnp.zeros((S, 64), F32)
    mla = (jnp.concatenate([one, c, c, jnp.ones((S, 32), F32)], 1),
           jnp.concatenate([z64, zero, s, z32], 1),
           jnp.concatenate([z64, -s, zero, z32], 1))
    c, s = cs(64)
    dil = (jnp.concatenate([c, c, c, c], 1),
           jnp.concatenate([z32, s, z32, s], 1),
           jnp.concatenate([-s, z32, -s, z32], 1))
    return mla, dil


def _rope(x, C, S1, S2, sh):
    return x * C + pltpu.roll(x, sh, 1) * S1 + pltpu.roll(x, LANE - sh, 1) * S2


def _rope_t(dy, C, S1, S2, sh):
    return dy * C + pltpu.roll(dy * S1, LANE - sh, 1) + pltpu.roll(dy * S2, sh, 1)


def _lane(shape):
    return lax.broadcasted_iota(jnp.int32, shape, 1)


T_MLA = 256
TA = 512


def _zcol(T, width, col_lanes):
    assert (col_lanes * LANE) % width == 0
    return pl.BlockSpec((T, width), lambda i: (i, col_lanes * LANE // width))


def _full(shape):
    return pl.BlockSpec(shape, lambda *_: (0,) * len(shape))


def _mla_pre_fwd(zp, w, tab):
    S = zp.shape[0]
    T = T_MLA

    def body(cq_ref, ckv_ref, kr_ref, gcq_ref, gckv_ref, wuq_ref, wuk_ref, wuv_ref, gq_ref, gk_ref, C_ref, S1_ref, S2_ref,
             q_ref, k_ref, v_ref):
        cq = cq_ref[...]
        cqn = (cq * lax.rsqrt(jnp.mean(cq * cq, axis=-1, keepdims=True) + EPS) * gcq_ref[...]).astype(BF16)
        ckv = ckv_ref[...]
        ckvn = (ckv * lax.rsqrt(jnp.mean(ckv * ckv, axis=-1, keepdims=True) + EPS) * gckv_ref[...]).astype(BF16)
        q0 = _nn(cqn, wuq_ref[...])
        k0 = _nn(ckvn, wuk_ref[...])
        krb = kr_ref[...]
        C, S1, S2 = C_ref[...], S1_ref[...], S2_ref[...]
        for h in range(8):
            sl = slice(h * LANE, (h + 1) * LANE)
            xq = q0[:, sl]
            xq = xq * lax.rsqrt(_rsum(xq * xq) * (1.0 / MLA_QK) + EPS) * gq_ref[...]
            q_ref[:, sl] = _rope(xq, C, S1, S2, 16).astype(BF16)
            xk = k0[:, sl] + krb
            xk = xk * lax.rsqrt(_rsum(xk * xk) * (1.0 / MLA_QK) + EPS) * gk_ref[...]
            k_ref[:, sl] = _rope(xk, C, S1, S2, 16).astype(BF16)
        v_ref[...] = _nn(ckvn, wuv_ref[...]).astype(BF16)

    tabspec = pl.BlockSpec((T, LANE), lambda i: (i, 0))
    in_specs = [_zcol(T, 256, C_CQ), _zcol(T, LANE, C_CKV), _zcol(T, LANE, C_KR), _full((1, 256)), _full((1, LANE)),
                _full((256, 1024)), _full((LANE, 1024)), _full((LANE, 512)), _full((1, LANE)), _full((1, LANE)),
                tabspec, tabspec, tabspec]
    return pl.pallas_call(
        body, name="mla_pre_fwd", grid=(S // T,), in_specs=in_specs,
        out_specs=[pl.BlockSpec((T, 1024), lambda i: (i, 0)), pl.BlockSpec((T, 1024), lambda i: (i, 0)), pl.BlockSpec((T, 512), lambda i: (i, 0))],
        out_shape=[jax.ShapeDtypeStruct((S, 1024), BF16), jax.ShapeDtypeStruct((S, 1024), BF16), jax.ShapeDtypeStruct((S, 512), BF16)],
        compiler_params=_cp(),
    )(zp, zp, zp, w["g_cq"], w["g_ckv"], w["w_uq"], w["w_uk"], w["w_uv"], w["g_mq"], w["g_mk"], *tab)


def _mla_attn_fwd(q, k, v, zp):
    S = q.shape[0]
    nq = S // TA

    def body(q_ref, k_ref, v_ref, g_ref, o_ref, lse_ref, y_ref):
        qi = pl.program_id(1)
        lane = _lane((TA, LANE))
        rowi = lax.broadcasted_iota(jnp.int32, (TA, TA), 0)
        coli = lax.broadcasted_iota(jnp.int32, (TA, TA), 1)
        o_tot = jnp.zeros((TA, LANE), F32)
        for hh in range(2):
            cs = slice(hh * LANE, (hh + 1) * LANE)
            hm = (lane < 64) if hh == 0 else (lane >= 64)
            qh = q_ref[:, cs]

            def step(kb, carry, masked, cs=cs, hm=hm, qh=qh):
                m, l, acc = carry
                off = pl.multiple_of(kb * TA, TA)
                kh = k_ref[pl.ds(off, TA), cs]
                vv = v_ref[pl.ds(off, TA), :]
                vh = jnp.where(hm, vv, jnp.zeros_like(vv))
                s = _nt(qh, kh) * MLA_SCALE
                if masked:
                    s = jnp.where(rowi >= coli, s, NEG)
                m_new = jnp.maximum(m, jnp.max(s, axis=-1, keepdims=True))
                alpha = jnp.exp(m - m_new)
                p = jnp.exp(s - m_new)
                l = alpha * l + _rsum(p)
                acc = alpha * acc + _nn(p.astype(BF16), vh)
                return m_new, l, acc

            init = (jnp.full((TA, 1), NEG, F32), jnp.zeros((TA, 1), F32), jnp.zeros((TA, LANE), F32))
            carry = lax.fori_loop(0, qi, lambda kb, c: step(kb, c, False), init)
            m, l, acc = step(qi, carry, True)
            o_tot = o_tot + acc / l
            lse_ref[:, cs] = jnp.broadcast_to(m + jnp.log(l), (TA, LANE))
        o_ref[...] = o_tot
        y_ref[...] = (o_tot * _silu(g_ref[...])).astype(BF16)

    blk = pl.BlockSpec((TA, LANE), lambda p, i: (i, p))
    return pl.pallas_call(
        body, name="mla_attn_fwd", grid=(4, nq),
        in_specs=[pl.BlockSpec((TA, 256), lambda p, i: (i, p)), pl.BlockSpec((S, 256), lambda p, i: (0, p)),
                  pl.BlockSpec((S, LANE), lambda p, i: (0, p)), pl.BlockSpec((TA, LANE), lambda p, i: (i, C_MLAG + p))],
        out_specs=[blk, pl.BlockSpec((TA, 256), lambda p, i: (i, p)), blk],
        out_shape=[jax.ShapeDtypeStruct((S, 512), F32), jax.ShapeDtypeStruct((S, 1024), F32), jax.ShapeDtypeStruct((S, 512), BF16)],
        compiler_params=_cp(dimension_semantics=("parallel", "arbitrary")),
    )(q, k, v, zp)


def _mla_post_bwd(zp, o, dy, dz):
    S = zp.shape[0]
    T = T_ROW

    def body(g_ref, o_ref, dy_ref, dz_in, dz_ref, do_ref, D_ref):
        del dz_in
        g, o_, dy_ = g_ref[...], o_ref[...], dy_ref[...]
        do = dy_ * _silu(g)
        do_ref[...] = do.astype(BF16)
        dz_ref[...] = (dy_ * o_ * _dsilu(g)).astype(BF16)
        prod = do * o_
        lane = _lane((T, LANE))
        for p in range(4):
            pr = prod[:, p * LANE:(p + 1) * LANE]
            da = _rsum(jnp.where(lane < 64, pr, 0.0))
            db = _rsum(jnp.where(lane >= 64, pr, 0.0))
            D_ref[:, 2 * p * LANE:(2 * p + 1) * LANE] = jnp.broadcast_to(da, (T, LANE))
            D_ref[:, (2 * p + 1) * LANE:(2 * p + 2) * LANE] = jnp.broadcast_to(db, (T, LANE))

    row = pl.BlockSpec((T, 512), lambda i: (i, 0))
    zc = _zcol(T, 512, C_MLAG)
    return pl.pallas_call(
        body, name="mla_post_bwd", grid=(S // T,), in_specs=[zc, row, row, pl.BlockSpec(memory_space=pl.ANY)],
        out_specs=[zc, row, pl.BlockSpec((T, 1024), lambda i: (i, 0))],
        out_shape=[jax.ShapeDtypeStruct(dz.shape, BF16), jax.ShapeDtypeStruct((S, 512), BF16), jax.ShapeDtypeStruct((S, 1024), F32)],
        input_output_aliases={3: 0}, compiler_params=_cp(),
    )(zp, o, dy, dz)


def _mla_attn_bwd(q, k, v, do, lse, Dr):
    S = q.shape[0]
    nq = S // TA

    def body(q_ref, do_ref, lse_ref, D_ref, k_ref, v_ref, dq_ref, dk_ref, dv_ref):
        ki = pl.program_id(1)

        @pl.when(ki == 0)
        def _():
            dq_ref[...] = jnp.zeros_like(dq_ref)

        lane = _lane((TA, LANE))
        rowi = lax.broadcasted_iota(jnp.int32, (TA, TA), 0)
        coli = lax.broadcasted_iota(jnp.int32, (TA, TA), 1)
        dv_tot = jnp.zeros((TA, LANE), F32)
        for hh in range(2):
            cs = slice(hh * LANE, (hh + 1) * LANE)
            hm = (lane < 64) if hh == 0 else (lane >= 64)
            kh = k_ref[:, cs]
            vv = v_ref[...]
            vm = jnp.where(hm, vv, jnp.zeros_like(vv))

            def step(qb, carry, masked, cs=cs, kh=kh, vm=vm):
                dk_acc, dv_acc = carry
                off = pl.multiple_of(qb * TA, TA)
                qh = q_ref[pl.ds(off, TA), cs]
                doh = do_ref[pl.ds(off, TA), :]
                ls = jnp.tile(lse_ref[pl.ds(off, TA), cs], (1, TA // LANE))
                dd = jnp.tile(D_ref[pl.ds(off, TA), cs], (1, TA // LANE))
                s = _nt(qh, kh) * MLA_SCALE
                if masked:
                    s = jnp.where(rowi >= coli, s, NEG)
                p = jnp.exp(s - ls)
                dp = _nt(doh, vm)
                ds = (p * (dp - dd) * MLA_SCALE).astype(BF16)
                dv_acc = dv_acc + _tn(p.astype(BF16), doh)
                dk_acc = dk_acc + _tn(ds, qh)
                dq_ref[pl.ds(off, TA), cs] += _nn(ds, kh)
                return dk_acc, dv_acc

            z = jnp.zeros((TA, LANE), F32)
            carry = step(ki, (z, z), True)
            dk_acc, dv_acc = lax.fori_loop(ki + 1, nq, lambda qb, c: step(qb, c, False), carry)
            dk_ref[:, cs] = dk_acc
            dv_tot = dv_tot + jnp.where(hm, dv_acc, 0.0)
        dv_ref[...] = dv_tot

    pair = pl.BlockSpec((S, 256), lambda p, i: (0, p))
    return pl.pallas_call(
        body, name="mla_attn_bwd", grid=(4, nq),
        in_specs=[pair, pl.BlockSpec((S, LANE), lambda p, i: (0, p)), pair, pair,
                  pl.BlockSpec((TA, 256), lambda p, i: (i, p)), pl.BlockSpec((TA, LANE), lambda p, i: (i, p))],
        out_specs=[pair, pl.BlockSpec((TA, 256), lambda p, i: (i, p)), pl.BlockSpec((TA, LANE), lambda p, i: (i, p))],
        out_shape=[jax.ShapeDtypeStruct((S, 1024), F32), jax.ShapeDtypeStruct((S, 1024), F32), jax.ShapeDtypeStruct((S, 512), F32)],
        compiler_params=_cp(dimension_semantics=("parallel", "arbitrary")),
    )(q, do, lse, Dr, k, v)


def _mla_pre_bwd(zp, dq, dk, dv, w, tab, dz):
    S = zp.shape[0]
    T = T_MLA

    def body(cq_ref, ckv_ref, kr_ref, dq_ref, dk_ref, dv_ref, gcq_ref, gckv_ref, wuq_ref, wuk_ref, wuv_ref, gq_ref, gk_ref,
             C_ref, S1_ref, S2_ref, dz_in, dz_ref, dwuq_ref, dwuk_ref, dwuv_ref, dgcq_ref, dgckv_ref, dgq_ref, dgk_ref):
        del dz_in
        i = pl.program_id(0)

        @pl.when(i == 0)
        def _():
            for r in (dwuq_ref, dwuk_ref, dwuv_ref, dgcq_ref, dgckv_ref, dgq_ref, dgk_ref):
                r[...] = jnp.zeros_like(r)

        cq = cq_ref[...]
        rq = lax.rsqrt(jnp.mean(cq * cq, axis=-1, keepdims=True) + EPS)
        cqh = cq * rq
        cqn = (cqh * gcq_ref[...]).astype(BF16)
        ckv = ckv_ref[...]
        rkv = lax.rsqrt(jnp.mean(ckv * ckv, axis=-1, keepdims=True) + EPS)
        ckvh = ckv * rkv
        ckvn = (ckvh * gckv_ref[...]).astype(BF16)
        q0 = _nn(cqn, wuq_ref[...])
        k0 = _nn(ckvn, wuk_ref[...])
        krb = kr_ref[...]
        C, S1, S2 = C_ref[...], S1_ref[...], S2_ref[...]
        gq, gk = gq_ref[...], gk_ref[...]

        def head_bwd(x, dy, g):
            r = lax.rsqrt(_rsum(x * x) * (1.0 / MLA_QK) + EPS)
            xn = x * r
            dyn = _rope_t(dy, C, S1, S2, 16)
            dxh = dyn * g
            return r * (dxh - xn * _rsum(dxh * xn) * (1.0 / MLA_QK)), _csum(dyn * xn)

        dq0, dk0 = [], []
        dgq_acc = jnp.zeros((1, LANE), F32)
        dgk_acc = jnp.zeros((1, LANE), F32)
        dkr = jnp.zeros((T, LANE), F32)
        for h in range(8):
            sl = slice(h * LANE, (h + 1) * LANE)
            dxq, gq_p = head_bwd(q0[:, sl], dq_ref[:, sl], gq)
            dxk, gk_p = head_bwd(k0[:, sl] + krb, dk_ref[:, sl], gk)
            dq0.append(dxq.astype(BF16))
            dk0.append(dxk.astype(BF16))
            dkr = dkr + dxk
            dgq_acc = dgq_acc + gq_p
            dgk_acc = dgk_acc + gk_p
        dgq_ref[...] += dgq_acc
        dgk_ref[...] += dgk_acc
        dq0 = jnp.concatenate(dq0, axis=1)
        dk0 = jnp.concatenate(dk0, axis=1)
        dvb = dv_ref[...].astype(BF16)
        dwuq_ref[...] += _tn(cqn, dq0)
        dwuk_ref[...] += _tn(ckvn, dk0)
        dwuv_ref[...] += _tn(ckvn, dvb)
        dcqn = _nt(dq0, wuq_ref[...])
        dckvn = _nt(dk0, wuk_ref[...]) + _nt(dvb, wuv_ref[...])
        dgcq_ref[...] += _csum(dcqn * cqh)
        dgckv_ref[...] += _csum(dckvn * ckvh)
        dxh = dcqn * gcq_ref[...]
        dz_ref[:, 0:256] = (rq * (dxh - cqh * jnp.mean(dxh * cqh, axis=-1, keepdims=True))).astype(BF16)
        dxh = dckvn * gckv_ref[...]
        dz_ref[:, 256:384] = (rkv * (dxh - ckvh * jnp.mean(dxh * ckvh, axis=-1, keepdims=True))).astype(BF16)
        lane = _lane((T, LANE))
        dz_ref[:, 384:512] = jnp.where((lane >= KR_LANE) & (lane < KR_LANE + 32), dkr, 0.0).astype(BF16)

    tabspec = pl.BlockSpec((T, LANE), lambda i: (i, 0))
    in_specs = [_zcol(T, 256, C_CQ), _zcol(T, LANE, C_CKV), _zcol(T, LANE, C_KR),
                pl.BlockSpec((T, 1024), lambda i: (i, 0)), pl.BlockSpec((T, 1024), lambda i: (i, 0)), pl.BlockSpec((T, 512), lambda i: (i, 0)),
                _full((1, 256)), _full((1, LANE)), _full((256, 1024)), _full((LANE, 1024)), _full((LANE, 512)), _full((1, LANE)), _full((1, LANE)),
                tabspec, tabspec, tabspec, pl.BlockSpec(memory_space=pl.ANY)]
    out_specs = [_zcol(T, 512, C_CQ), _full((256, 1024)), _full((LANE, 1024)), _full((LANE, 512)), _full((1, 256)), _full((1, LANE)),
                 _full((1, LANE)), _full((1, LANE))]
    out_shape = [jax.ShapeDtypeStruct(dz.shape, BF16), jax.ShapeDtypeStruct((256, 1024), F32), jax.ShapeDtypeStruct((LANE, 1024), F32),
                 jax.ShapeDtypeStruct((LANE, 512), F32), jax.ShapeDtypeStruct((1, 256), F32), jax.ShapeDtypeStruct((1, LANE), F32),
                 jax.ShapeDtypeStruct((1, LANE), F32), jax.ShapeDtypeStruct((1, LANE), F32)]
    return pl.pallas_call(
        body, name="mla_pre_bwd", grid=(S // T,), in_specs=in_specs, out_specs=out_specs, out_shape=out_shape,
        input_output_aliases={16: 0}, compiler_params=_cp(),
    )(zp, zp, zp, dq, dk, dv, w["g_cq"], w["g_ckv"], w["w_uq"], w["w_uk"], w["w_uv"], w["g_mq"], w["g_mk"], *tab, dz)


T_DIL = 256


def _head_stats(x, lane):
    sq = x * x
    sa = _rsum(jnp.where(lane < 64, sq, 0.0))
    sb = _rsum(jnp.where(lane >= 64, sq, 0.0))
    return lax.rsqrt(jnp.where(lane < 64, sa, sb) * (1.0 / DIL_HD) + EPS)


def _head_sum(x, lane):
    sa = _rsum(jnp.where(lane < 64, x, 0.0))
    sb = _rsum(jnp.where(lane >= 64, x, 0.0))
    return jnp.where(lane < 64, sa, sb)


def _dil_pre_fwd(zp, w, tab):
    S = zp.shape[0]
    T = T_DIL

    def body(q_ref, k_ref, gq_ref, gk_ref, C_ref, S1_ref, S2_ref, qo_ref, ko_ref):
        C, S1, S2 = C_ref[...], S1_ref[...], S2_ref[...]
        lane = _lane((T, LANE))
        for b in range(12):
            sl = slice(b * LANE, (b + 1) * LANE)
            x = q_ref[:, sl]
            qo_ref[:, sl] = _rope(x * _head_stats(x, lane) * gq_ref[...], C, S1, S2, 32)
            x = k_ref[:, sl]
            ko_ref[:, sl] = _rope(x * _head_stats(x, lane) * gk_ref[...], C, S1, S2, 32)

    tabspec = pl.BlockSpec((T, LANE), lambda i: (i, 0))
    out = pl.BlockSpec((T, 1536), lambda i: (i, 0))
    return pl.pallas_call(
        body, name="dil_pre_fwd", grid=(S // T,),
        in_specs=[_zcol(T, 1536, C_DQ), _zcol(T, 1536, C_DK), _full((1, LANE)), _full((1, LANE)), tabspec, tabspec, tabspec],
        out_specs=[out, out], out_shape=[jax.ShapeDtypeStruct((S, 1536), F32)] * 2, compiler_params=_cp(),
    )(zp, zp, w["g_dq"], w["g_dk"], *tab)


DIL_ROWS = 2048


def _dil_geometry(g, S):
    d = DIL_DILATIONS[g]
    P = NK * d
    return d, P, DIL_ROWS // P, S // P


def _dil_rows(start, d):
    return pl.ds(pl.multiple_of(start, NK), NK) if d == 1 else pl.ds(start, NK, stride=d)


def _dil_specs(g, S, col0):
    _, P, m, nb = _dil_geometry(g, S)
    cur = pl.BlockSpec((DIL_ROWS, LANE), lambda sb, c: (sb, col0 + c))
    prv = pl.BlockSpec((P, LANE), lambda sb, c: (jnp.maximum(sb * m - 1, 0), col0 + c))
    nxt = pl.BlockSpec((P, LANE), lambda sb, c: (jnp.minimum((sb + 1) * m, nb - 1), col0 + c))
    return cur, prv, nxt


def _dil_masks(n, nb):
    row = lax.broadcasted_iota(jnp.int32, (NK, NK), 0)
    col = lax.broadcasted_iota(jnp.int32, (NK, NK), 1)
    return col <= row, (col >= row) & (n > 0), (col >= row) & (n < nb - 1)


def _dil_attn_fwd(q, k, zp, g):
    S = q.shape[0]
    d, P, m, nb = _dil_geometry(g, S)
    R = DIL_ROWS

    def body(q_ref, kc_ref, kp_ref, vc_ref, vp_ref, o_ref, lse_ref, *scr):
        sb = pl.program_id(0)
        if m > 1:
            ks_ref, vs_ref = scr
            ks_ref[0:P, :] = kp_ref[...]
            ks_ref[P:P + R, :] = kc_ref[...]
            vs_ref[0:P, :] = vp_ref[...]
            vs_ref[P:P + R, :] = vc_ref[...]
        lane = _lane((NK, LANE))

        def unit(u, carry):
            j = u // d
            rows = _dil_rows(j * P + (u - j * d), d)
            if m > 1:
                rows_c = _dil_rows(j * P + (u - j * d) + P, d)
                kp, kc, vp, vc = ks_ref[rows, :], ks_ref[rows_c, :], vs_ref[rows, :], vs_ref[rows_c, :]
            else:
                kp, kc, vp, vc = kp_ref[rows, :], kc_ref[rows, :], vp_ref[rows, :], vc_ref[rows, :]
            kp, kc, vp, vc = kp.astype(BF16), kc.astype(BF16), vp.astype(BF16), vc.astype(BF16)
            q_ = q_ref[rows, :].astype(BF16)
            mc, mp, _ = _dil_masks(sb * m + j, nb)
            zb = jnp.zeros_like(q_)
            o_tot = jnp.zeros((NK, LANE), F32)
            lse_tot = jnp.zeros((NK, LANE), F32)
            for hh in range(2):
                hm = (lane < 64) if hh == 0 else (lane >= 64)
                qm = jnp.where(hm, q_, zb)
                sc = jnp.where(mc, _nt(qm, kc) * DIL_SCALE, NEG)
                sp = jnp.where(mp, _nt(qm, kp) * DIL_SCALE, NEG)
                mx = jnp.maximum(jnp.max(sc, axis=-1, keepdims=True), jnp.max(sp, axis=-1, keepdims=True))
                ec = jnp.exp(sc - mx)
                ep = jnp.exp(sp - mx)
                den = _rsum(ec) + _rsum(ep)
                o = (_nn(ec.astype(BF16), jnp.where(hm, vc, zb)) + _nn(ep.astype(BF16), jnp.where(hm, vp, zb))) / den
                o_tot = o_tot + o
                lse_tot = jnp.where(hm, mx + jnp.log(den), lse_tot)
            o_ref[rows, :] = o_tot
            lse_ref[rows, :] = lse_tot
            return carry

        lax.fori_loop(0, R // NK, unit, 0, unroll=2)

    qcur, qprv, _ = _dil_specs(g, S, 4 * g)
    vcur, vprv, _ = _dil_specs(g, S, C_DV + 4 * g)
    out = pl.BlockSpec((R, LANE), lambda sb, c: (sb, c))
    return pl.pallas_call(
        body, name=f"dil_attn_fwd{g}", grid=(S // R, 4), in_specs=[qcur, qcur, qprv, vcur, vprv], out_specs=[out, out],
        out_shape=[jax.ShapeDtypeStruct((S, 512), F32)] * 2,
        scratch_shapes=[pltpu.VMEM((P + R, LANE), F32)] * 2 if m > 1 else [], compiler_params=_cp(),
    )(q, k, k, zp, zp)


def _dil_combine(os_, ls_, zp):
    S = zp.shape[0]
    T = T_ROW

    def body(o0, o1, o2, l0, l1, l2, g_ref, oc_ref, L_ref, y_ref):
        a, b, c = l0[...], l1[...], l2[...]
        mx = jnp.maximum(jnp.maximum(a, b), c)
        ea, eb, ec = jnp.exp(a - mx), jnp.exp(b - mx), jnp.exp(c - mx)
        den = ea + eb + ec
        oc = (ea * o0[...] + eb * o1[...] + ec * o2[...]) / den
        oc_ref[...] = oc
        L_ref[...] = mx + jnp.log(den)
        y_ref[...] = (oc * _silu(g_ref[...])).astype(BF16)

    row = pl.BlockSpec((T, 512), lambda i: (i, 0))
    return pl.pallas_call(
        body, name="dil_combine", grid=(S // T,), in_specs=[row] * 6 + [_zcol(T, 512, C_DILG)], out_specs=[row, row, row],
        out_shape=[jax.ShapeDtypeStruct((S, 512), F32), jax.ShapeDtypeStruct((S, 512), F32), jax.ShapeDtypeStruct((S, 512), BF16)],
        compiler_params=_cp(),
    )(*os_, *ls_, zp)


def _dil_comb_bwd(zp, oc, dy, dz):
    S = zp.shape[0]
    T = T_ROW

    def body(g_ref, o_ref, dy_ref, dz_in, dz_ref, do_ref, D_ref):
        del dz_in
        g, o_, dy_ = g_ref[...], o_ref[...], dy_ref[...]
        do = dy_ * _silu(g)
        do_ref[...] = do
        dz_ref[...] = (dy_ * o_ * _dsilu(g)).astype(BF16)
        lane = _lane((T, LANE))
        for p in range(4):
            sl = slice(p * LANE, (p + 1) * LANE)
            D_ref[:, sl] = _head_sum(do[:, sl] * o_[:, sl], lane)

    row = pl.BlockSpec((T, 512), lambda i: (i, 0))
    zc = _zcol(T, 512, C_DILG)
    return pl.pallas_call(
        body, name="dil_comb_bwd", grid=(S // T,), in_specs=[zc, row, row, pl.BlockSpec(memory_space=pl.ANY)], out_specs=[zc, row, row],
        out_shape=[jax.ShapeDtypeStruct(dz.shape, BF16), jax.ShapeDtypeStruct((S, 512), F32), jax.ShapeDtypeStruct((S, 512), F32)],
        input_output_aliases={3: 0}, compiler_params=_cp(),
    )(zp, oc, dy, dz)


def _dil_attn_bwd(q, k, zp, do, L, Dr, g):
    S = q.shape[0]
    d, P, m, nb = _dil_geometry(g, S)
    R = DIL_ROWS
    n_q, n_k = 4, 2

    def body(*refs):
        q_side = refs[0:2 * n_q]
        k_side = refs[2 * n_q:2 * n_q + 2 * n_k]
        dq_ref, dk_ref, dv_ref = refs[2 * n_q + 2 * n_k:2 * n_q + 2 * n_k + 3]
        scr = refs[2 * n_q + 2 * n_k + 3:]
        sb = pl.program_id(0)
        if m > 1:
            for a in range(n_q):
                scr[a][0:R, :] = q_side[2 * a][...]
                scr[a][R:R + P, :] = q_side[2 * a + 1][...]
            for a in range(n_k):
                scr[n_q + a][0:P, :] = k_side[2 * a + 1][...]
                scr[n_q + a][P:P + R, :] = k_side[2 * a][...]
        lane = _lane((NK, LANE))

        def unit(u, carry):
            j = u // d
            start = j * P + (u - j * d)
            rows = _dil_rows(start, d)
            if m > 1:
                rows_b = _dil_rows(start + P, d)
                qc, doc, Lc_, Dc_ = [scr[a][rows, :] for a in range(n_q)]
                qn, don, Ln_, Dn_ = [scr[a][rows_b, :] for a in range(n_q)]
                kp, vp = [scr[n_q + a][rows, :] for a in range(n_k)]
                kc, vc = [scr[n_q + a][rows_b, :] for a in range(n_k)]
            else:
                qc, doc, Lc_, Dc_ = [q_side[2 * a][rows, :] for a in range(n_q)]
                qn, don, Ln_, Dn_ = [q_side[2 * a + 1][rows, :] for a in range(n_q)]
                kc, vc = [k_side[2 * a][rows, :] for a in range(n_k)]
                kp, vp = [k_side[2 * a + 1][rows, :] for a in range(n_k)]
            qc, qn, doc, don = qc.astype(BF16), qn.astype(BF16), doc.astype(BF16), don.astype(BF16)
            kc, kp, vc, vp = kc.astype(BF16), kp.astype(BF16), vc.astype(BF16), vp.astype(BF16)
            mc, mp, mnext = _dil_masks(sb * m + j, nb)
            zb = jnp.zeros_like(qc)
            dq_tot = jnp.zeros((NK, LANE), F32)
            dk_tot = jnp.zeros((NK, LANE), F32)
            dv_tot = jnp.zeros((NK, LANE), F32)
            for hh in range(2):
                hm = (lane < 64) if hh == 0 else (lane >= 64)

                def bcast(x, hm=hm):
                    return jnp.where(hm, x, pltpu.roll(x, 64, 1))

                Lc, Ln, Dc, Dn = bcast(Lc_), bcast(Ln_), bcast(Dc_), bcast(Dn_)
                qm = jnp.where(hm, qc, zb)
                qnm = jnp.where(hm, qn, zb)
                vcm = jnp.where(hm, vc, zb)
                vpm = jnp.where(hm, vp, zb)
                pc = jnp.exp(jnp.where(mc, _nt(qm, kc) * DIL_SCALE, NEG) - Lc)
                pp = jnp.exp(jnp.where(mp, _nt(qm, kp) * DIL_SCALE, NEG) - Lc)
                dsc = (pc * (_nt(doc, vcm) - Dc) * DIL_SCALE).astype(BF16)
                dsp = (pp * (_nt(doc, vpm) - Dc) * DIL_SCALE).astype(BF16)
                dq_tot = dq_tot + jnp.where(hm, _nn(dsc, kc) + _nn(dsp, kp), 0.0)
                p2 = jnp.exp(jnp.where(mnext, _nt(qnm, kc) * DIL_SCALE, NEG) - Ln)
                ds2 = (p2 * (_nt(don, vcm) - Dn) * DIL_SCALE).astype(BF16)
                dk_tot = dk_tot + _tn(dsc, qm) + _tn(ds2, qnm)
                dv_tot = dv_tot + jnp.where(hm, _tn(pc.astype(BF16), doc) + _tn(p2.astype(BF16), don), 0.0)
            dq_ref[rows, :] = dq_tot
            dk_ref[rows, :] = dk_tot
            dv_ref[rows, :] = dv_tot
            return carry

        lax.fori_loop(0, R // NK, unit, 0, unroll=2)

    qcur, qprv, qnxt = _dil_specs(g, S, 4 * g)
    vcur, vprv, _ = _dil_specs(g, S, C_DV + 4 * g)
    ocur, _, onxt = _dil_specs(g, S, 0)
    out = pl.BlockSpec((R, LANE), lambda sb, c: (sb, c))
    scratch = [pltpu.VMEM((P + R, LANE), F32)] * (n_q + n_k) if m > 1 else []
    return pl.pallas_call(
        body, name=f"dil_attn_bwd{g}", grid=(S // R, 4),
        in_specs=[qcur, qnxt, ocur, onxt, ocur, onxt, ocur, onxt, qcur, qprv, vcur, vprv],
        out_specs=[out, out, out], out_shape=[jax.ShapeDtypeStruct((S, 512), F32)] * 3, scratch_shapes=scratch, compiler_params=_cp(),
    )(q, q, do, do, L, L, Dr, Dr, k, k, zp, zp)


def _dil_pre_bwd(zp, dys, g, tab, dz, col, name):
    S = zp.shape[0]
    T = T_DIL

    def body(x_ref, dy0_ref, dy1_ref, dy2_ref, g_ref, C_ref, S1_ref, S2_ref, dz_in, dz_ref, dg_ref):
        del dz_in
        i = pl.program_id(0)
        C, S1, S2 = C_ref[...], S1_ref[...], S2_ref[...]
        lane = _lane((T, LANE))
        gv = g_ref[...]
        acc = jnp.zeros((1, LANE), F32)
        for b in range(12):
            sl = slice(b * LANE, (b + 1) * LANE)
            x = x_ref[:, sl]
            r = _head_stats(x, lane)
            xn = x * r
            dy_ref = (dy0_ref, dy1_ref, dy2_ref)[b // 4]
            dyn = _rope_t(dy_ref[:, (b % 4) * LANE:(b % 4 + 1) * LANE], C, S1, S2, 32)
            acc = acc + _csum(dyn * xn)
            dxh = dyn * gv
            dz_ref[:, sl] = (r * (dxh - xn * _head_sum(dxh * xn, lane) * (1.0 / DIL_HD))).astype(BF16)

        @pl.when(i == 0)
        def _():
            dg_ref[...] = acc

        @pl.when(i > 0)
        def _():
            dg_ref[...] += acc

    tabspec = pl.BlockSpec((T, LANE), lambda i: (i, 0))
    zc = _zcol(T, 1536, col)
    grp = pl.BlockSpec((T, 512), lambda i: (i, 0))
    return pl.pallas_call(
        body, name=name, grid=(S // T,),
        in_specs=[zc, grp, grp, grp, _full((1, LANE)), tabspec, tabspec, tabspec, pl.BlockSpec(memory_space=pl.ANY)],
        out_specs=[zc, _full((1, LANE))], out_shape=[jax.ShapeDtypeStruct(dz.shape, BF16), jax.ShapeDtypeStruct((1, LANE), F32)],
        input_output_aliases={8: 0}, compiler_params=_cp(),
    )(zp, *dys, g, *tab, dz)


def _dil_dv_into(dvs, dz):
    S = dz.shape[0]
    T = T_ROW

    def body(s0, s1, s2, dz_in, o_ref):
        del dz_in
        for gi, s in enumerate((s0, s1, s2)):
            o_ref[:, gi * 512:(gi + 1) * 512] = s[...].astype(BF16)

    grp = pl.BlockSpec((T, 512), lambda i: (i, 0))
    return pl.pallas_call(
        body, name="dil_dv", grid=(S // T,), in_specs=[grp, grp, grp, pl.BlockSpec(memory_space=pl.ANY)],
        out_specs=_zcol(T, 1536, C_DV), out_shape=jax.ShapeDtypeStruct(dz.shape, BF16), input_output_aliases={3: 0}, compiler_params=_cp(),
    )(*dvs, dz)


T_MRG = 256


def _merge_fwd(P, zp, b_merge):
    S = zp.shape[0]
    T = T_MRG

    def body(p0, p1, p2, m0, m1, m2, b_ref, o_ref):
        acc = jnp.zeros((T, D), F32)
        for j, (p, m) in enumerate(((p0, m0), (p1, m1), (p2, m2))):
            acc = acc + _sig(m[...] + b_ref[:, j * D:(j + 1) * D]) * p[...]
        o_ref[...] = acc.astype(BF16)

    row = pl.BlockSpec((T, D), lambda i: (i, 0))
    return pl.pallas_call(
        body, name="merge_fwd", grid=(S // T,),
        in_specs=[row, row, row] + [_zcol(T, D, C_MERGE + 8 * j) for j in range(3)] + [_full((1, 3 * D))], out_specs=row,
        out_shape=jax.ShapeDtypeStruct((S, D), BF16), compiler_params=_cp(),
    )(*P, zp, zp, zp, b_merge)


def _merge_bwd(dm, Pj, zp, bj, dz, j):
    S = zp.shape[0]
    T = T_MRG

    def body(dm_ref, p_ref, m_ref, b_ref, dz_in, dz_ref, dp_ref, db_ref):
        del dz_in
        i = pl.program_id(0)
        g = _sig(m_ref[...] + b_ref[...])
        dmv = dm_ref[...]
        dp_ref[...] = (dmv * g).astype(BF16)
        dg = dmv * p_ref[...] * g * (1.0 - g)
        dz_ref[...] = dg.astype(BF16)
        part = _csum(dg)

        @pl.when(i == 0)
        def _():
            db_ref[...] = part

        @pl.when(i > 0)
        def _():
            db_ref[...] += part

    row = pl.BlockSpec((T, D), lambda i: (i, 0))
    zc = _zcol(T, D, C_MERGE + 8 * j)
    return pl.pallas_call(
        body, name=f"merge_bwd{j}", grid=(S // T,), in_specs=[row, row, zc, _full((1, D)), pl.BlockSpec(memory_space=pl.ANY)],
        out_specs=[zc, row, _full((1, D))],
        out_shape=[jax.ShapeDtypeStruct(dz.shape, BF16), jax.ShapeDtypeStruct((S, D), BF16), jax.ShapeDtypeStruct((1, D), F32)],
        input_output_aliases={4: 0}, compiler_params=_cp(),
    )(dm, Pj, zp, bj, dz)


def _loss_fwd_bwd(y, target):
    S = y.shape[0]
    T = T_ROW

    def body(y_ref, t_ref, loss_ref, dy_ref):
        i = pl.program_id(0)
        err = y_ref[...] - t_ref[...]
        dy_ref[...] = err * (1.0 / D)
        part = jnp.sum(err * err, keepdims=True).reshape(1, 1) * (0.5 / D)

        @pl.when(i == 0)
        def _():
            loss_ref[...] = part

        @pl.when(i > 0)
        def _():
            loss_ref[...] += part

    row = pl.BlockSpec((T, D), lambda i: (i, 0))
    return pl.pallas_call(
        body, name="loss", grid=(S // T,), in_specs=[row, row], out_specs=[_full((1, 1)), row],
        out_shape=[jax.ShapeDtypeStruct((1, 1), F32), jax.ShapeDtypeStruct((S, D), F32)], compiler_params=_cp(),
    )(y, target)


def _layer_fwd(x, w, tabs):
    mla_tab, dil_tab = tabs
    S = x.shape[0]
    h = _rms_in_fwd(x, w["norm_g"])
    zp = _mm(h, w["w_in"], mode="nn", name="in_proj")
    hs, y_lru = _lru_fwd(zp, w)
    q, k, v = _mla_pre_fwd(zp, w, mla_tab)
    o_mla, lse, y_mla = _mla_attn_fwd(q, k, v, zp)
    qd, kd = _dil_pre_fwd(zp, w, dil_tab)
    og, lg = zip(*[_dil_attn_fwd(qd, kd, zp, g) for g in range(len(DIL_DILATIONS))])
    oc, L, y_dil = _dil_combine(og, lg, zp)
    P = [_mm(y_lru, w["w_lru_o"], mode="nn", name="lru_out"), _mm(y_mla, w["w_mla_o"], mode="nn", name="mla_out"),
         _mm(y_dil, w["w_dil_o"], mode="nn", name="dil_out")]
    merged = _merge_fwd(P, zp, w["b_merge"])
    x_out = _mm(merged, w["w_out"], mode="nn", name="out_proj", add=x)
    saved = dict(x=x, h=h, zp=zp, hs=hs, y=(y_lru, y_mla, y_dil), q=q, k=k, v=v, o_mla=o_mla, lse=lse, qd=qd, kd=kd, oc=oc, L=L, P=P,
                 merged=merged)
    return x_out, saved


def _layer_bwd(dout, w, tabs, sv):
    mla_tab, dil_tab = tabs
    zp = sv["zp"]
    S = zp.shape[0]
    g = {}
    dm = _mm(dout, w["w_out"], mode="nt", name="d_merged")
    g["w_out"] = _mm(sv["merged"], dout, mode="tn", name="dw_out", out_dtype=BF16)
    dz = lax.empty((S, ZW), BF16)
    dP, db = [], []
    for j in range(3):
        dz, dpj, dbj = _merge_bwd(dm, sv["P"][j], zp, w["b_merge"][:, j * D:(j + 1) * D], dz, j)
        dP.append(dpj)
        db.append(dbj)
    g["b_merge"] = jnp.concatenate(db, axis=1)
    names = ("w_lru_o", "w_mla_o", "w_dil_o")
    dy = []
    for j in range(3):
        dy.append(_mm(dP[j], w[names[j]], mode="nt", name="dy_" + names[j]))
        g[names[j]] = _mm(sv["y"][j], dP[j], mode="tn", name="d" + names[j], out_dtype=BF16)
    dz = _lru_gate_bwd(zp, sv["hs"], dy[0], dz)
    dz, g["conv_w"], g["conv_b"], g["w_gx"], g["b_gx"], g["w_ga"], g["b_ga"], g["lam"] = _lru_bwd(zp, sv["hs"], dy[0], w, dz)
    dz, do, Dr = _mla_post_bwd(zp, sv["o_mla"], dy[1], dz)
    dq, dk, dv = _mla_attn_bwd(sv["q"], sv["k"], sv["v"], do, sv["lse"], Dr)
    dz, g["w_uq"], g["w_uk"], g["w_uv"], g["g_cq"], g["g_ckv"], g["g_mq"], g["g_mk"] = _mla_pre_bwd(zp, dq, dk, dv, w, mla_tab, dz)
    dz, dod, Dd = _dil_comb_bwd(zp, sv["oc"], dy[2], dz)
    dqs, dks, dvs = zip(*[_dil_attn_bwd(sv["qd"], sv["kd"], zp, dod, sv["L"], Dd, gi) for gi in range(len(DIL_DILATIONS))])
    dz, g["g_dq"] = _dil_pre_bwd(zp, dqs, w["g_dq"], dil_tab, dz, C_DQ, "dil_pre_bwd_q")
    dz, g["g_dk"] = _dil_pre_bwd(zp, dks, w["g_dk"], dil_tab, dz, C_DK, "dil_pre_bwd_k")
    dz = _dil_dv_into(dvs, dz)
    dh = _mm(dz, w["w_in"], mode="nt", name="d_h")
    g["w_in"] = _mm(sv["h"], dz, mode="tn", name="dw_in", out_dtype=BF16)
    dx, g["norm_g"] = _rms_in_bwd(sv["x"], w["norm_g"], dh, dout)
    return dx, g


def _peers():
    mx, my, mc = lax.axis_index("x"), lax.axis_index("y"), lax.axis_index("c")
    me = 4 * mx + 2 * my + mc
    out = []
    for k in range(1, N_DEV):
        px = 1 - mx if k & 4 else mx
        py = 1 - my if k & 2 else my
        pc = 1 - mc if k & 1 else mc
        out.append(((px, py, pc), 4 * px + 2 * py + pc))
    return me, out


def _whole(ref, p):
    del p
    return ref


def _exchange(srcs, slicers, slices, name):
    n = len(srcs)

    def body(*refs):
        ins, outs = refs[:n], refs[n:2 * n]
        send_sems, recv_sems, local_sems = refs[2 * n:]
        me, peers = _peers()
        mine = [pltpu.make_async_copy(slicers[a](ins[a], me), outs[a].at[me], local_sems.at[a]) for a in range(n)]
        for cp in mine:
            cp.start()
        copies = []
        for k, (peer, pidx) in enumerate(peers):
            for a in range(n):
                cp = pltpu.make_async_remote_copy(
                    src_ref=slicers[a](ins[a], pidx), dst_ref=outs[a].at[me], send_sem=send_sems.at[k * n + a],
                    recv_sem=recv_sems.at[k * n + a], device_id=peer, device_id_type=pl.DeviceIdType.MESH)
                cp.start()
                copies.append(cp)
        for cp in copies + mine:
            cp.wait()

    nsem = (N_DEV - 1) * n
    return pl.pallas_call(
        body, name=name, out_shape=[jax.ShapeDtypeStruct((N_DEV,) + shp, dt) for shp, dt in slices],
        in_specs=[pl.BlockSpec(memory_space=pl.ANY)] * n, out_specs=[pl.BlockSpec(memory_space=pl.ANY)] * n,
        scratch_shapes=[pltpu.SemaphoreType.DMA((nsem,)), pltpu.SemaphoreType.DMA((nsem,)), pltpu.SemaphoreType.DMA((n,))],
        compiler_params=pltpu.CompilerParams(has_side_effects=True),
    )(*srcs)


def _gather_two_level(srcs, name):
    n = len(srcs)

    def body(*refs):
        ins, outs = refs[:n], refs[n:2 * n]
        send_sems, recv_sems, local_sems = refs[2 * n:]
        mx, my, mc = lax.axis_index("x"), lax.axis_index("y"), lax.axis_index("c")
        me, sibling = (mx, my, mc), (mx, my, 1 - mc)
        chips = [(1 - mx, my), (mx, 1 - my), (1 - mx, 1 - my)]
        slot = lambda d: 4 * d[0] + 2 * d[1] + d[2]

        def copy(j, a, block, to, own=False):
            return pltpu.make_async_remote_copy(
                src_ref=ins[a] if own else outs[a].at[slot(block)], dst_ref=outs[a].at[slot(block)],
                send_sem=send_sems.at[j * n + a], recv_sem=recv_sems.at[j * n + a], device_id=to, device_id_type=pl.DeviceIdType.MESH)

        mine = [pltpu.make_async_copy(ins[a], outs[a].at[slot(me)], local_sems.at[a]) for a in range(n)]
        first = [copy(1 + j, a, me, (*chip, mc), own=True) for j, chip in enumerate(chips) for a in range(n)]
        first += [copy(0, a, me, sibling, own=True) for a in range(n)]
        for cp in mine + first:
            cp.start()
        passed = []
        for j, chip in enumerate(chips):
            for a in range(n):
                copy(1 + j, a, (*chip, mc), me).wait_recv()
                cp = copy(4 + j, a, (*chip, mc), sibling)
                cp.start()
                passed.append(cp)
        for a in range(n):
            copy(0, a, sibling, me).wait_recv()
        for j, chip in enumerate(chips):
            for a in range(n):
                copy(4 + j, a, (*chip, 1 - mc), me).wait_recv()
        for cp in first + passed:
            cp.wait_send()
        for cp in mine:
            cp.wait()

    nsem = (N_DEV - 1) * n
    return pl.pallas_call(
        body, name=name, out_shape=[jax.ShapeDtypeStruct((N_DEV,) + a.shape, a.dtype) for a in srcs],
        in_specs=[pl.BlockSpec(memory_space=pl.ANY)] * n, out_specs=[pl.BlockSpec(memory_space=pl.ANY)] * n,
        scratch_shapes=[pltpu.SemaphoreType.DMA((nsem,)), pltpu.SemaphoreType.DMA((nsem,)), pltpu.SemaphoreType.DMA((n,))],
        compiler_params=pltpu.CompilerParams(has_side_effects=True),
    )(*srcs)


_HBM = pl.BlockSpec(memory_space=pltpu.HBM)
_SEM = pl.BlockSpec(memory_space=pltpu.SEMAPHORE)
_DATAFLOW = pltpu.SideEffectType.DATAFLOW_SIDE_EFFECTING


def _exchange_start(srcs, slicers, slices, after, name):
    n = len(srcs)
    nsem = (N_DEV - 1) * n
    lands = [lax.empty((N_DEV,) + shp, dt) for shp, dt in slices]

    def body(*refs):
        ins, lands_in = refs[:n], refs[n:2 * n]
        send_sems, recv_sems, local_sems = refs[2 * n + 1], refs[2 * n + 2], refs[2 * n + 3]
        token = refs[-1]
        me, peers = _peers()
        for a in range(n):
            pltpu.make_async_copy(slicers[a](ins[a], me), lands_in[a].at[me], local_sems.at[a]).start()
        for k, (peer, pidx) in enumerate(peers):
            for a in range(n):
                pltpu.make_async_remote_copy(
                    src_ref=slicers[a](ins[a], pidx), dst_ref=lands_in[a].at[me], send_sem=send_sems.at[k * n + a],
                    recv_sem=recv_sems.at[k * n + a], device_id=peer, device_id_type=pl.DeviceIdType.MESH).start()
        token[...] = jnp.zeros_like(token)

    hbm = lambda a: pltpu.with_memory_space_constraint(a, pltpu.HBM)
    return pl.pallas_call(
        body, name=name,
        out_shape=(pltpu.SemaphoreType.DMA((nsem,)), pltpu.SemaphoreType.DMA((nsem,)), pltpu.SemaphoreType.DMA((n,)),
                   *[pltpu.HBM(a.shape, a.dtype) for a in srcs], *[pltpu.HBM(a.shape, a.dtype) for a in lands],
                   jax.ShapeDtypeStruct((SUB, LANE), F32)),
        in_specs=[_HBM] * (2 * n) + [pl.BlockSpec(memory_space=pl.ANY)],
        out_specs=(_SEM, _SEM, _SEM, *[_HBM] * (2 * n), pl.BlockSpec(memory_space=pltpu.VMEM)),
        input_output_aliases={i: 3 + i for i in range(2 * n)},
        compiler_params=pltpu.CompilerParams(has_side_effects=_DATAFLOW),
    )(*[hbm(a) for a in srcs], *[hbm(a) for a in lands], after)


def _exchange_wait(started, slicers, after, name):
    n = (len(started) - 4) // 2
    sems, thru = started[0:3], started[3:3 + 2 * n]

    def body(*refs):
        srcs, lands = refs[:n], refs[n:2 * n]
        send_sems, recv_sems, local_sems = refs[2 * n], refs[2 * n + 1], refs[2 * n + 2]
        me, peers = _peers()
        for k, (peer, pidx) in enumerate(peers):
            for a in range(n):
                cp = pltpu.make_async_remote_copy(
                    src_ref=slicers[a](srcs[a], pidx), dst_ref=lands[a].at[me], send_sem=send_sems.at[k * n + a],
                    recv_sem=recv_sems.at[k * n + a], device_id=peer, device_id_type=pl.DeviceIdType.MESH)
                cp.wait_send()
                cp.wait_recv()
        for a in range(n):
            pltpu.make_async_copy(slicers[a](srcs[a], me), lands[a].at[me], local_sems.at[a]).wait()

    outs = pl.pallas_call(
        body, name=name, out_shape=[pltpu.HBM(a.shape, a.dtype) for a in thru],
        in_specs=[_HBM] * (2 * n) + [_SEM, _SEM, _SEM, pl.BlockSpec(memory_space=pl.ANY)], out_specs=[_HBM] * (2 * n),
        input_output_aliases={i: i for i in range(2 * n)}, compiler_params=pltpu.CompilerParams(has_side_effects=_DATAFLOW),
    )(*thru, *sems, after)
    return outs[n:]


WIN = 13 * LANE


def _win_base(s):
    n = s * SHARD_IN
    a0 = n + jnp.where(n >= _KR0, KR_LANE, 0) + jnp.where(n >= _KR0 + 32, 32, 0)
    return jnp.minimum(a0 // LANE, (ZW - WIN) // LANE)


def _win_offsets(s):
    n = s * SHARD_IN + jnp.arange(SHARD_IN)
    o = s * SHARD_IN - _win_base(s) * LANE
    return n, (o, o + KR_LANE, o + LANE - 32)


def _to_window(shard, s):
    n, offs = _win_offsets(s)
    masks = (n < _KR0, (n >= _KR0) & (n < _KR0 + 32), n >= _KR0 + 32)
    zero = jnp.zeros(shard.shape[:2] + (WIN,), shard.dtype)
    out = zero
    for m, o in zip(masks, offs):
        out = out + lax.dynamic_update_slice(zero, jnp.where(m[None, None, :], shard, jnp.zeros_like(shard)), (0, 0, o))
    return out


def _from_window(win, s):
    n, offs = _win_offsets(s)
    a, b, c = [lax.dynamic_slice(win, (0, 0, o), win.shape[:2] + (SHARD_IN,)) for o in offs]
    return jnp.where((n < _KR0)[None, None, :], a, jnp.where((n < _KR0 + 32)[None, None, :], b, c))


def _win_base_static(s):
    n = s * SHARD_IN
    a0 = n + (KR_LANE if n >= _KR0 else 0) + (32 if n >= _KR0 + 32 else 0)
    return min(a0 // LANE, (ZW - WIN) // LANE)


def _assemble_w_in(gw):
    tr = 128
    bases = [_win_base_static(s) for s in range(N_DEV)]

    def body(g_ref, o_ref):
        for j in range(ZW // LANE):
            acc = None
            for s in range(N_DEV):
                if bases[s] <= j < bases[s] + WIN // LANE:
                    piece = g_ref[s, :, (j - bases[s]) * LANE:(j - bases[s] + 1) * LANE]
                    acc = piece if acc is None else acc + piece
            o_ref[:, j * LANE:(j + 1) * LANE] = acc

    return pl.pallas_call(
        body, name="assemble_w_in", grid=(D // tr,), in_specs=[pl.BlockSpec((N_DEV, tr, WIN), lambda i: (0, i, 0))],
        out_specs=pl.BlockSpec((tr, ZW), lambda i: (i, 0)), out_shape=jax.ShapeDtypeStruct((D, ZW), gw.dtype), compiler_params=_cp(),
    )(gw)


def _cols(width):
    return lambda ref, p: ref.at[:, pl.ds(pl.multiple_of(p * width, width), width)]


def _rows(height):
    return lambda ref, p: ref.at[pl.ds(pl.multiple_of(p * height, height), height), :]


SCATTER = {
    'w_in': (lambda ref, p: ref.at[:, pl.ds(pl.multiple_of(_win_base(p) * LANE, LANE), WIN)], (D, WIN), BF16),
    'conv_w': (_cols(LANE), (4, LANE), F32),
    'w_lru_o': (_rows(LANE), (LANE, D), BF16),
    'w_uq': (_cols(LANE), (256, LANE), F32),
    'w_ukv': (_cols(LANE), (128, LANE), F32),
    'w_mla_o': (_cols(LANE), (512, LANE), BF16),
    'w_dil_o': (_cols(LANE), (512, LANE), BF16),
    'w_out': (_rows(LANE), (LANE, D), BF16),
}


PACK_ROWS = 512


def _sum8(buf, name):
    _, R, C = buf.shape
    tr = R
    while tr * C * 4 * N_DEV > (1 << 22) and tr % 16 == 0:
        tr //= 2

    def body(b_ref, o_ref):
        acc = b_ref[0].astype(F32)
        for s in range(1, N_DEV):
            acc = acc + b_ref[s].astype(F32)
        o_ref[...] = acc

    return pl.pallas_call(
        body, name=name, grid=(R // tr,), in_specs=[pl.BlockSpec((N_DEV, tr, C), lambda i: (0, i, 0))],
        out_specs=pl.BlockSpec((tr, C), lambda i: (i, 0)), out_shape=jax.ShapeDtypeStruct((R, C), F32), compiler_params=_cp(),
    )(buf)


def _pack(arrs, dtype, lead):
    flat = [a.astype(dtype).reshape(a.shape[:lead] + (-1,)) for a in arrs]
    cat = jnp.concatenate(flat, axis=-1)
    n = cat.shape[-1]
    unit = PACK_ROWS * LANE
    pad = (-n) % unit
    if pad:
        cat = jnp.pad(cat, [(0, 0)] * lead + [(0, pad)])
    return cat.reshape(cat.shape[:lead] + ((n + pad) // LANE, LANE))


def _unpack(buf, shapes, lead):
    flat = buf.reshape(buf.shape[:lead] + (-1,))
    out, off = [], 0
    for shp in shapes:
        n = int(np.prod(shp))
        out.append(flat[..., off:off + n].reshape(buf.shape[:lead] + tuple(shp)))
        off += n
    return out


def _adamw(w, g, m, v, name):
    rows, cols = w.shape
    tr = rows
    while tr * cols * 4 > (3 << 19) and tr % 16 == 0:
        tr //= 2
    c1 = 1.0 - ADAM_B1 ** ADAM_STEP
    c2 = 1.0 - ADAM_B2 ** ADAM_STEP

    def body(w_ref, g_ref, m_ref, v_ref, d_ref, mo_ref, vo_ref):
        gv = g_ref[...]
        mn = ADAM_B1 * m_ref[...] + (1.0 - ADAM_B1) * gv
        vn = ADAM_B2 * v_ref[...] + (1.0 - ADAM_B2) * (gv * gv)
        mo_ref[...] = mn
        vo_ref[...] = vn
        d_ref[...] = -ADAM_LR * ((mn / c1) / (jnp.sqrt(vn / c2) + ADAM_EPS) + ADAM_WD * w_ref[...])

    spec = pl.BlockSpec((tr, cols), lambda i: (i, 0))
    return pl.pallas_call(
        body, name=name, grid=(rows // tr,), in_specs=[spec] * 4, out_specs=[spec] * 3,
        out_shape=[jax.ShapeDtypeStruct((rows, cols), F32)] * 3, compiler_params=_cp(),
    )(w, g, m, v)


IN_NAMES = ['x', 'positions', 'norm_g', 'w_in', 'conv_w', 'conv_b', 'w_gate_x', 'b_gate_x', 'w_gate_a', 'b_gate_a', 'lru_lambda', 'w_lru_o',
            'cq_norm_g', 'ckv_norm_g', 'w_uq', 'w_ukv', 'mla_q_norm_g', 'mla_k_norm_g', 'w_mla_o', 'dil_q_norm_g', 'dil_k_norm_g', 'w_dil_o',
            'b_merge', 'w_out']
WEIGHTS = IN_NAMES[2:]
REPLICATED = [n for n in WEIGHTS if n not in SCATTER]

_KR0 = C_KR * LANE


GATHERED = ['w_in', 'w_lru_o', 'w_uq', 'w_ukv', 'w_mla_o', 'w_dil_o', 'w_out', 'conv_w']


def _local_weights(wd, me):
    loc = {n: wd[n].astype(BF16) for n in GATHERED[:-1]}
    loc['w_in'] = _to_window(loc['w_in'], me)
    loc['w_uq'] = jnp.pad(loc['w_uq'], ((0, 0), (0, 0), (0, LANE - MLA_QK)))
    loc['conv_w'] = wd['conv_w']
    return [[loc[n][l] for n in GATHERED] for l in range(DEPTH)]


def _layer_weights(gathered, rep, l):
    gw = dict(zip(GATHERED, gathered))
    by_rows = lambda a: a.reshape(-1, a.shape[-1])
    by_cols = lambda a: jnp.swapaxes(a, 0, 1).reshape(a.shape[1], -1)
    ukv = jnp.swapaxes(gw['w_ukv'], 0, 1)
    g96 = lambda a: jnp.pad(a[l].reshape(1, MLA_QK), ((0, 0), (0, LANE - MLA_QK)))
    g64 = lambda a: jnp.tile(a[l].reshape(1, DIL_HD), (1, 2))
    return dict(
        norm_g=rep['norm_g'][l].reshape(1, D), w_in=_assemble_w_in(gw['w_in']),
        conv_w=by_cols(gw['conv_w']), conv_b=rep['conv_b'][l].reshape(1, D),
        w_gx=rep['w_gate_x'][l].astype(BF16), b_gx=rep['b_gate_x'][l].reshape(8, 1, LANE),
        w_ga=rep['w_gate_a'][l].astype(BF16), b_ga=rep['b_gate_a'][l].reshape(8, 1, LANE),
        lam=rep['lru_lambda'][l].reshape(1, D),
        w_lru_o=by_rows(gw['w_lru_o']), w_mla_o=by_cols(gw['w_mla_o']), w_dil_o=by_cols(gw['w_dil_o']), w_out=by_rows(gw['w_out']),
        g_cq=rep['cq_norm_g'][l].reshape(1, 256), g_ckv=rep['ckv_norm_g'][l].reshape(1, 128),
        w_uq=by_cols(gw['w_uq']), w_uk=jnp.pad(ukv[:, :, :64], ((0, 0), (0, 0), (0, 64))).reshape(128, 1024),
        w_uv=ukv[:, :, 64:].reshape(128, 512),
        g_mq=g96(rep['mla_q_norm_g']), g_mk=g96(rep['mla_k_norm_g']), g_dq=g64(rep['dil_q_norm_g']), g_dk=g64(rep['dil_k_norm_g']),
        b_merge=rep['b_merge'][l].reshape(1, 3 * D),
    )


def _sharded_grads(g):
    uk = g['w_uk'].reshape(128, 8, 128)[:, :, :64]
    uv = g['w_uv'].reshape(128, 8, 64)
    d = {'w_in': g['w_in'], 'conv_w': g['conv_w'], 'w_lru_o': g['w_lru_o'], 'w_uq': g['w_uq'],
         'w_ukv': jnp.concatenate([uk, uv], axis=-1).reshape(128, 1024), 'w_mla_o': g['w_mla_o'], 'w_dil_o': g['w_dil_o'],
         'w_out': g['w_out']}
    return [d[n] for n in SCATTER]


def _replicated_grads(g):
    return {
        'norm_g': g['norm_g'].reshape(D), 'conv_b': g['conv_b'].reshape(D),
        'w_gate_x': g['w_gx'], 'b_gate_x': g['b_gx'].reshape(8, LANE), 'w_gate_a': g['w_ga'], 'b_gate_a': g['b_ga'].reshape(8, LANE),
        'lru_lambda': g['lam'].reshape(D), 'cq_norm_g': g['g_cq'].reshape(256), 'ckv_norm_g': g['g_ckv'].reshape(128),
        'mla_q_norm_g': g['g_mq'][0, :MLA_QK], 'mla_k_norm_g': g['g_mk'][0, :MLA_QK],
        'dil_q_norm_g': g['g_dq'][0, :DIL_HD] + g['g_dq'][0, DIL_HD:], 'dil_k_norm_g': g['g_dk'][0, :DIL_HD] + g['g_dk'][0, DIL_HD:],
        'b_merge': g['b_merge'].reshape(3 * D),
    }


def kernel(x, positions, norm_g, w_in, conv_w, conv_b, w_gate_x, b_gate_x, w_gate_a, b_gate_a, lru_lambda, w_lru_o, cq_norm_g, ckv_norm_g, w_uq, w_ukv, mla_q_norm_g, mla_k_norm_g, w_mla_o, dil_q_norm_g, dil_k_norm_g, w_dil_o, b_merge, w_out, loss_target, m_norm_g, m_w_in, m_conv_w, m_conv_b, m_w_gate_x, m_b_gate_x, m_w_gate_a, m_b_gate_a, m_lru_lambda, m_w_lru_o, m_cq_norm_g, m_ckv_norm_g, m_w_uq, m_w_ukv, m_mla_q_norm_g, m_mla_k_norm_g, m_w_mla_o, m_dil_q_norm_g, m_dil_k_norm_g, m_w_dil_o, m_b_merge, m_w_out, v_norm_g, v_w_in, v_conv_w, v_conv_b, v_w_gate_x, v_b_gate_x, v_w_gate_a, v_b_gate_a, v_lru_lambda, v_w_lru_o, v_cq_norm_g, v_ckv_norm_g, v_w_uq, v_w_ukv, v_mla_q_norm_g, v_mla_k_norm_g, v_w_mla_o, v_dil_q_norm_g, v_dil_k_norm_g, v_w_dil_o, v_b_merge, v_w_out):
    args = (x, positions, norm_g, w_in, conv_w, conv_b, w_gate_x, b_gate_x, w_gate_a, b_gate_a, lru_lambda, w_lru_o, cq_norm_g, ckv_norm_g, w_uq, w_ukv, mla_q_norm_g, mla_k_norm_g, w_mla_o, dil_q_norm_g, dil_k_norm_g, w_dil_o, b_merge, w_out)
    moments_m = (m_norm_g, m_w_in, m_conv_w, m_conv_b, m_w_gate_x, m_b_gate_x, m_w_gate_a, m_b_gate_a, m_lru_lambda, m_w_lru_o, m_cq_norm_g, m_ckv_norm_g, m_w_uq, m_w_ukv, m_mla_q_norm_g, m_mla_k_norm_g, m_w_mla_o, m_dil_q_norm_g, m_dil_k_norm_g, m_w_dil_o, m_b_merge, m_w_out)
    moments_v = (v_norm_g, v_w_in, v_conv_w, v_conv_b, v_w_gate_x, v_b_gate_x, v_w_gate_a, v_b_gate_a, v_lru_lambda, v_w_lru_o, v_cq_norm_g, v_ckv_norm_g, v_w_uq, v_w_ukv, v_mla_q_norm_g, v_mla_k_norm_g, v_w_mla_o, v_dil_q_norm_g, v_dil_k_norm_g, v_w_dil_o, v_b_merge, v_w_out)
    a = dict(zip(IN_NAMES, args))
    wd = {n: a[n] for n in WEIGHTS}
    md = dict(zip(WEIGHTS, moments_m))
    vd = dict(zip(WEIGHTS, moments_v))

    me = 4 * lax.axis_index("x") + 2 * lax.axis_index("y") + lax.axis_index("c")

    assert DEPTH == 2
    xs, tabs = x[0], _rope_tables(positions[0])
    whole = [_whole] * len(GATHERED)
    slicers = [SCATTER[n][0] for n in SCATTER]
    grad_slices = [SCATTER[n][1:] for n in SCATTER]

    local = _local_weights(wd, me)
    w_slices = [(a.shape, a.dtype) for a in local[0]]
    landed0 = _gather_two_level(local[0], "gather_w0")
    flying = _exchange_start(local[1], whole, w_slices, landed0[0], "gather_w1_start")
    rep0 = dict(wd, norm_g=wd['norm_g'] + flying[-1][0, 0])
    w0 = _layer_weights(landed0, rep0, 0)
    x1, saved0 = _layer_fwd(xs, w0, tabs)
    w1 = _layer_weights(_exchange_wait(flying, whole, x1, "gather_w1_wait"), wd, 1)
    x2, saved1 = _layer_fwd(x1, w1, tabs)
    loss, dx2 = _loss_fwd_bwd(x2, loss_target[0])
    loss = loss[0, 0]

    dx1, g1 = _layer_bwd(dx2, w1, tabs, saved1)
    flying = _exchange_start(_sharded_grads(g1), slicers, grad_slices, dx1, "scatter_g1_start")
    w0 = dict(w0, b_merge=w0['b_merge'] + flying[-1][0, 0])
    grad_x, g0 = _layer_bwd(dx1, w0, tabs, saved0)
    got1 = _exchange_wait(flying, slicers, grad_x, "scatter_g1_wait")
    got0 = _exchange(_sharded_grads(g0), slicers, grad_slices, "scatter_g0")
    glayers = [g0, g1]
    sharded = list(SCATTER)
    gsh = {}
    for i, n in enumerate(sharded):
        gsh[n] = jnp.stack([_sum8(got[i], f"sum_{n}_{l}") for l, got in enumerate((got0, got1))])
    gsh['w_in'] = _from_window(gsh['w_in'], me)
    gsh['w_uq'] = gsh['w_uq'][:, :, :MLA_QK]
    rshapes = [wd[n].shape for n in REPLICATED]
    rgrads = [_replicated_grads(g) for g in glayers]
    rpacked = _pack([jnp.stack([rgrads[l][n] for l in range(DEPTH)]) for n in REPLICATED], F32, 0)
    grep = _sum8(_exchange([rpacked], [_whole], [(rpacked.shape, F32)], "gather_grads")[0], "sum_replicated")

    out_g, out_d, out_m, out_v = {}, {}, {}, {}
    d_, m_, v_ = _adamw(_pack([wd[n] for n in REPLICATED], F32, 0), grep, _pack([md[n] for n in REPLICATED], F32, 0),
                        _pack([vd[n] for n in REPLICATED], F32, 0), "adamw_replicated")
    for dst, buf in ((out_g, grep), (out_d, d_), (out_m, m_), (out_v, v_)):
        dst.update(zip(REPLICATED, _unpack(buf, rshapes, 0)))
    for n in sharded:
        shp = wd[n].shape
        two = (shp[0] * shp[1], shp[2])
        d_, m_, v_ = _adamw(wd[n].reshape(two), gsh[n].reshape(two), md[n].reshape(two), vd[n].reshape(two), "adamw_" + n)
        out_g[n], out_d[n], out_m[n], out_v[n] = gsh[n], d_.reshape(shp), m_.reshape(shp), v_.reshape(shp)

    loss = lax.psum(loss, ("x", "y", "c"))
    return (loss, grad_x[None], *[out_g[n] for n in WEIGHTS], *[out_d[n] for n in WEIGHTS], *[out_m[n] for n in WEIGHTS],
            *[out_v[n] for n in WEIGHTS])
```

```python
import functools

import numpy as np
import jax
import jax.numpy as jnp
from jax import lax
from jax.experimental import pallas as pl
from jax.experimental.pallas import tpu as pltpu

F32 = jnp.float32
BF16 = jnp.bfloat16

N_DEV = 8
D = 1024
DEPTH = 2
EPS = 1e-6
ROPE_THETA = 10000.0
LRU_C = 8.0
LANE = 128
SUB = 8
IN_WIDTH = 11168
SHARD_IN = IN_WIDTH // N_DEV

C_LRUX, C_LRUG, C_CQ, C_CKV, C_KR, C_MLAG, C_DQ, C_DK, C_DV, C_DILG, C_MERGE = 0, 8, 16, 18, 19, 20, 24, 36, 48, 60, 64
ZW = 88 * LANE
KR_LANE = 64

MLA_QK = 96
MLA_SCALE = MLA_QK ** -0.5
DIL_HD = 64
DIL_SCALE = DIL_HD ** -0.5
DIL_DILATIONS = (1, 4, 16)
NK = 128

ADAM_LR, ADAM_B1, ADAM_B2, ADAM_EPS, ADAM_WD, ADAM_STEP = 0.001, 0.9, 0.999, 1e-08, 0.01, 10

NEG = -1e30
VMEM_LIMIT = 48 * 1024 * 1024


def _cp(**kw):
    return pltpu.CompilerParams(vmem_limit_bytes=VMEM_LIMIT, **kw)


def _sig(x):
    return 1.0 / (1.0 + jnp.exp(-x))


def _silu(x):
    return x * _sig(x)


def _dsilu(x):
    s = _sig(x)
    return s * (1.0 + x * (1.0 - s))


def _dot(a, b, dims):
    return lax.dot_general(a, b, (dims, ((), ())), preferred_element_type=F32)


def _nn(a, b):
    return _dot(a, b, ((1,), (0,)))


def _nt(a, b):
    return _dot(a, b, ((1,), (1,)))


def _tn(a, b):
    return _dot(a, b, ((0,), (0,)))


def _rsum(x):
    return jnp.sum(x, axis=-1, keepdims=True)


def _csum(x):
    return jnp.sum(x, axis=0, keepdims=True)


def _mm(a, b, *, mode, name, out_dtype=F32, add=None, after=None, tm=1024, tn=1024, tk=1024):
    if mode == "nn":
        (M, K), (K2, N) = a.shape, b.shape
    elif mode == "nt":
        (M, K), (N, K2) = a.shape, b.shape
    else:
        (K, M), (K2, N) = a.shape, b.shape
    assert K == K2
    tm, tn, tk = min(tm, M), min(tn, N), min(tk, K)
    assert M % tm == 0 and N % tn == 0 and K % tk == 0
    nk = K // tk
    fn = {"nn": _nn, "nt": _nt, "tn": _tn}[mode]
    has_add = add is not None

    def body(*refs):
        a_ref, b_ref = refs[0], refs[1]
        add_ref = refs[2] if has_add else None
        o_ref = refs[2 + has_add + (after is not None)]
        part = fn(a_ref[...].astype(BF16), b_ref[...].astype(BF16))

        def fin(acc):
            if has_add:
                acc = acc + add_ref[...]
            o_ref[...] = acc.astype(out_dtype)

        if nk == 1:
            fin(part)
        else:
            acc_ref = refs[-1]
            k = pl.program_id(2)

            @pl.when(k == 0)
            def _():
                acc_ref[...] = part

            @pl.when(k > 0)
            def _():
                acc_ref[...] += part

            @pl.when(k == nk - 1)
            def _():
                fin(acc_ref[...])

    a_spec = pl.BlockSpec((tk, tm), lambda i, j, k: (k, i)) if mode == "tn" else pl.BlockSpec((tm, tk), lambda i, j, k: (i, k))
    b_spec = pl.BlockSpec((tn, tk), lambda i, j, k: (j, k)) if mode == "nt" else pl.BlockSpec((tk, tn), lambda i, j, k: (k, j))
    o_spec = pl.BlockSpec((tm, tn), lambda i, j, k: (i, j))
    in_specs, args = [a_spec, b_spec], [a, b]
    if has_add:
        in_specs.append(o_spec)
        args.append(add)
    if after is not None:
        in_specs.append(pl.BlockSpec(memory_space=pl.ANY))
        args.append(after)
    return pl.pallas_call(
        body, name=name, grid=(M // tm, N // tn, nk), in_specs=in_specs, out_specs=o_spec,
        out_shape=jax.ShapeDtypeStruct((M, N), out_dtype),
        scratch_shapes=[pltpu.VMEM((tm, tn), F32)] if nk > 1 else [],
        compiler_params=_cp(dimension_semantics=("parallel", "parallel", "arbitrary")),
    )(*args)


T_ROW = 512


def _rms_in_fwd(x, g):
    S = x.shape[0]
    T = T_ROW

    def body(x_ref, g_ref, h_ref):
        xv = x_ref[...]
        r = lax.rsqrt(jnp.mean(xv * xv, axis=-1, keepdims=True) + EPS)
        h_ref[...] = (xv * r * g_ref[...]).astype(BF16)

    return pl.pallas_call(
        body, name="rms_in_fwd", grid=(S // T,),
        in_specs=[pl.BlockSpec((T, D), lambda i: (i, 0)), pl.BlockSpec((1, D), lambda i: (0, 0))],
        out_specs=pl.BlockSpec((T, D), lambda i: (i, 0)),
        out_shape=jax.ShapeDtypeStruct((S, D), BF16), compiler_params=_cp(),
    )(x, g)


def _rms_in_bwd(x, g, dh, dres):
    S = x.shape[0]
    T = T_ROW

    def body(x_ref, g_ref, dh_ref, dr_ref, dx_ref, dg_ref):
        i = pl.program_id(0)
        xv = x_ref[...]
        r = lax.rsqrt(jnp.mean(xv * xv, axis=-1, keepdims=True) + EPS)
        xn = xv * r
        dy = dh_ref[...]
        part = _csum(dy * xn)

        @pl.when(i == 0)
        def _():
            dg_ref[...] = part

        @pl.when(i > 0)
        def _():
            dg_ref[...] += part

        dxh = dy * g_ref[...]
        dx_ref[...] = dr_ref[...] + r * (dxh - xn * jnp.mean(dxh * xn, axis=-1, keepdims=True))

    row = pl.BlockSpec((T, D), lambda i: (i, 0))
    vec = pl.BlockSpec((1, D), lambda i: (0, 0))
    return pl.pallas_call(
        body, name="rms_in_bwd", grid=(S // T,), in_specs=[row, vec, row, row], out_specs=[row, vec],
        out_shape=[jax.ShapeDtypeStruct((S, D), F32), jax.ShapeDtypeStruct((1, D), F32)], compiler_params=_cp(),
    )(x, g, dh, dres)


T_LRU = 512


def _neg_expm1(y):
    ser = -y * (1.0 + y * 0.5 * (1.0 + y * (1.0 / 3.0) * (1.0 + y * 0.25 * (1.0 + y * 0.2))))
    return jnp.where(y > -0.03, ser, 1.0 - jnp.exp(y))


def _softplus_neg(lam):
    e = jnp.exp(-jnp.abs(lam))
    l1p = jnp.where(e < 0.01, e * (1.0 - e * (0.5 - e * (1.0 / 3.0 - e * 0.25))), jnp.log(1.0 + e))
    return jnp.maximum(-lam, 0.0) + l1p


def _scan_fwd(a, b, T):
    row = lax.broadcasted_iota(jnp.int32, a.shape, 0)
    d = 1
    while d < T:
        m = row >= d
        b = jnp.where(m, a * pltpu.roll(b, d, 0) + b, b)
        a = jnp.where(m, a * pltpu.roll(a, d, 0), a)
        d *= 2
    return a, b


def _scan_bwd(a, b, T):
    row = lax.broadcasted_iota(jnp.int32, a.shape, 0)
    d = 1
    while d < T:
        m = row < T - d
        b = jnp.where(m, a * pltpu.roll(b, T - d, 0) + b, b)
        a = jnp.where(m, a * pltpu.roll(a, T - d, 0), a)
        d *= 2
    return b


def _lru_common(x, prev, first, cw_ref, cb_ref, wgx_ref, bgx_ref, wga_ref, bga_ref, lam_ref, T):
    row = lax.broadcasted_iota(jnp.int32, x.shape, 0)
    prev = jnp.where(first, 0.0, prev)
    xs = []
    for j in (3, 2, 1):
        pv = jnp.tile(pltpu.roll(prev, j, 0), (T // SUB, 1))
        xs.append(jnp.where(row < j, pv, pltpu.roll(x, j, 0)))
    xs.append(x)
    xc = cb_ref[...] + cw_ref[0:1, :] * xs[0] + cw_ref[1:2, :] * xs[1] + cw_ref[2:3, :] * xs[2] + cw_ref[3:4, :] * xs[3]
    xcb = xc.astype(BF16)
    gx = _sig(_nn(xcb, wgx_ref[0]) + bgx_ref[0])
    ga = _sig(_nn(xcb, wga_ref[0]) + bga_ref[0])
    sp = _softplus_neg(lam_ref[...])
    log_a = -LRU_C * ga * sp
    a = jnp.exp(log_a)
    mult = jnp.sqrt(_neg_expm1(2.0 * log_a))
    return xs, xc, xcb, gx, ga, sp, a, mult


def _lru_specs(T, tmap):
    def at(col0):
        return pl.BlockSpec((T, LANE), lambda n, i: (tmap(i), col0 + n))

    def prev(col0):
        return pl.BlockSpec((SUB, LANE), lambda n, i: (jnp.maximum(tmap(i) * (T // SUB) - 1, 0), col0 + n))

    small = [
        pl.BlockSpec((4, LANE), lambda n, i: (0, n)),
        pl.BlockSpec((1, LANE), lambda n, i: (0, n)),
        pl.BlockSpec((1, LANE, LANE), lambda n, i: (n, 0, 0)),
        pl.BlockSpec((1, 1, LANE), lambda n, i: (n, 0, 0)),
        pl.BlockSpec((1, LANE, LANE), lambda n, i: (n, 0, 0)),
        pl.BlockSpec((1, 1, LANE), lambda n, i: (n, 0, 0)),
        pl.BlockSpec((1, LANE), lambda n, i: (0, n)),
    ]
    return at, prev, small


def _lru_fwd(zp, w):
    S = zp.shape[0]
    T = T_LRU
    at, prev, small = _lru_specs(T, lambda i: i)

    def body(x_ref, xp_ref, g_ref, cw_ref, cb_ref, wgx_ref, bgx_ref, wga_ref, bga_ref, lam_ref, hs_ref, y_ref, carry_ref):
        i = pl.program_id(1)

        @pl.when(i == 0)
        def _():
            carry_ref[...] = jnp.zeros_like(carry_ref)

        x = x_ref[...]
        _, xc, _, gx, _, _, a, mult = _lru_common(x, xp_ref[...], i == 0, cw_ref, cb_ref, wgx_ref, bgx_ref, wga_ref, bga_ref, lam_ref, T)
        A, B = _scan_fwd(a, mult * gx * xc, T)
        h = B + A * carry_ref[SUB - 1:SUB, :]
        hs_ref[...] = h
        carry_ref[...] = hs_ref[T - SUB:T, :]
        y_ref[...] = (h * _silu(g_ref[...])).astype(BF16)

    out = pl.BlockSpec((T, LANE), lambda n, i: (i, n))
    return pl.pallas_call(
        body, name="lru_fwd", grid=(8, S // T),
        in_specs=[at(C_LRUX), prev(C_LRUX), at(C_LRUG)] + small, out_specs=[out, out],
        out_shape=[jax.ShapeDtypeStruct((S, D), F32), jax.ShapeDtypeStruct((S, D), BF16)],
        scratch_shapes=[pltpu.VMEM((SUB, LANE), F32)],
        compiler_params=_cp(dimension_semantics=("parallel", "arbitrary")),
    )(zp, zp, zp, w["conv_w"], w["conv_b"], w["w_gx"], w["b_gx"], w["w_ga"], w["b_ga"], w["lam"])


def _lru_bwd(zp, hs, dy, w, dz):
    S = zp.shape[0]
    T = T_LRU
    nT = S // T
    at, prev, small = _lru_specs(T, lambda i: nT - 1 - i)

    def body(x_ref, xp_ref, g_ref, h_ref, hp_ref, dy_ref, cw_ref, cb_ref, wgx_ref, bgx_ref, wga_ref, bga_ref, lam_ref, dz_in,
             dzx_ref, dcw_ref, dcb_ref, dwgx_ref, dbgx_ref, dwga_ref, dbga_ref, dlam_ref, carry_ref, head_ref):
        del dz_in
        j = pl.program_id(1)
        it = nT - 1 - j

        @pl.when(j == 0)
        def _():
            for r in (carry_ref, head_ref, dcw_ref, dcb_ref, dwgx_ref, dbgx_ref, dwga_ref, dbga_ref, dlam_ref):
                r[...] = jnp.zeros_like(r)

        first = it == 0
        x = x_ref[...]
        xs, xc, xcb, gx, ga, sp, a, mult = _lru_common(x, xp_ref[...], first, cw_ref, cb_ref, wgx_ref, bgx_ref, wga_ref, bga_ref, lam_ref, T)
        row = lax.broadcasted_iota(jnp.int32, x.shape, 0)
        u = gx * xc
        h = h_ref[...]
        hp = jnp.where(first, 0.0, hp_ref[...])
        hm1 = jnp.where(row < 1, jnp.tile(pltpu.roll(hp, 1, 0), (T // SUB, 1)), pltpu.roll(h, 1, 0))
        dho = dy_ref[...] * _silu(g_ref[...])
        gin = jnp.where(row == T - 1, dho + carry_ref[0:1, :], dho)
        abar = jnp.where(row == T - 1, 0.0, pltpu.roll(a, T - 1, 0))
        dh = _scan_bwd(abar, gin, T)
        carry_ref[...] = (a * dh)[0:SUB, :]
        da = dh * hm1
        dmult = dh * u
        du = dh * mult
        dgx = du * xc
        dxc = du * gx
        dlog_a = da * a - dmult * a * a / mult
        dga = dlog_a * (-LRU_C * sp)
        lam = lam_ref[...]
        dlam_ref[...] += _csum(dlog_a * (-LRU_C * ga)) * (-1.0 / (1.0 + jnp.exp(lam)))
        dpa = dga * ga * (1.0 - ga)
        dpx = dgx * gx * (1.0 - gx)
        dpab, dpxb = dpa.astype(BF16), dpx.astype(BF16)
        dxc = dxc + _nt(dpxb, wgx_ref[0]) + _nt(dpab, wga_ref[0])
        dwgx_ref[0] += _tn(xcb, dpxb)
        dwga_ref[0] += _tn(xcb, dpab)
        dbgx_ref[0] += _csum(dpx)
        dbga_ref[0] += _csum(dpa)
        dcb_ref[...] += _csum(dxc)
        for k in range(4):
            dcw_ref[k:k + 1, :] += _csum(dxc * xs[k])
        head = head_ref[...]
        dx = cw_ref[3:4, :] * dxc
        for jj in (1, 2, 3):
            hv = jnp.tile(pltpu.roll(head, SUB - jj, 0), (T // SUB, 1))
            dx = dx + cw_ref[3 - jj:4 - jj, :] * jnp.where(row >= T - jj, hv, pltpu.roll(dxc, T - jj, 0))
        head_ref[...] = dxc[0:SUB, :]
        dzx_ref[...] = dx.astype(BF16)

    def acc(shape, imap):
        return pl.BlockSpec(shape, imap)

    out_specs = [
        pl.BlockSpec((T, LANE), lambda n, i: (nT - 1 - i, C_LRUX + n)),
        acc((4, LANE), lambda n, i: (0, n)), acc((1, LANE), lambda n, i: (0, n)),
        acc((1, LANE, LANE), lambda n, i: (n, 0, 0)), acc((1, 1, LANE), lambda n, i: (n, 0, 0)),
        acc((1, LANE, LANE), lambda n, i: (n, 0, 0)), acc((1, 1, LANE), lambda n, i: (n, 0, 0)),
        acc((1, LANE), lambda n, i: (0, n)),
    ]
    out_shape = [
        jax.ShapeDtypeStruct(dz.shape, BF16),
        jax.ShapeDtypeStruct((4, D), F32), jax.ShapeDtypeStruct((1, D), F32),
        jax.ShapeDtypeStruct((8, LANE, LANE), F32), jax.ShapeDtypeStruct((8, 1, LANE), F32),
        jax.ShapeDtypeStruct((8, LANE, LANE), F32), jax.ShapeDtypeStruct((8, 1, LANE), F32),
        jax.ShapeDtypeStruct((1, D), F32),
    ]
    dyspec = pl.BlockSpec((T, LANE), lambda n, i: (nT - 1 - i, n))
    hprev = pl.BlockSpec((SUB, LANE), lambda n, i: (jnp.maximum((nT - 1 - i) * (T // SUB) - 1, 0), n))
    return pl.pallas_call(
        body, name="lru_bwd", grid=(8, nT),
        in_specs=[at(C_LRUX), prev(C_LRUX), at(C_LRUG), dyspec, hprev, dyspec] + small + [pl.BlockSpec(memory_space=pl.ANY)],
        out_specs=out_specs, out_shape=out_shape,
        scratch_shapes=[pltpu.VMEM((SUB, LANE), F32), pltpu.VMEM((SUB, LANE), F32)],
        input_output_aliases={13: 0},
        compiler_params=_cp(dimension_semantics=("parallel", "arbitrary")),
    )(zp, zp, zp, hs, hs, dy, w["conv_w"], w["conv_b"], w["w_gx"], w["b_gx"], w["w_ga"], w["b_ga"], w["lam"], dz)


def _lru_gate_bwd(zp, hs, dy, dz):
    S = zp.shape[0]
    T = T_ROW

    def body(g_ref, h_ref, dy_ref, dz_in, o_ref):
        del dz_in
        o_ref[...] = (dy_ref[...] * h_ref[...] * _dsilu(g_ref[...])).astype(BF16)

    row = pl.BlockSpec((T, D), lambda i: (i, 0))
    zc = pl.BlockSpec((T, D), lambda i: (i, C_LRUG // 8))
    return pl.pallas_call(
        body, name="lru_gate_bwd", grid=(S // T,), in_specs=[zc, row, row, pl.BlockSpec(memory_space=pl.ANY)], out_specs=zc,
        out_shape=jax.ShapeDtypeStruct(dz.shape, BF16), input_output_aliases={3: 0}, compiler_params=_cp(),
    )(zp, hs, dy, dz)


def _rope_tables(pos):
    pf = pos.astype(F32)[:, None]

    def cs(d):
        inv = ROPE_THETA ** (-jnp.arange(0, d, 2, dtype=F32) / d)
        ang = pf * inv
        return jnp.cos(ang), jnp.sin(ang)

    S = pos.shape[0]
    c, s = cs(32)
    one, zero = jnp.ones((S, 64), F32), jnp.zeros((S, 16), F32)
    z32, z64 = jnp.zeros((S, 32), F32), jnp.zeros((S, 64), F32)
    mla = (jnp.concatenate([one, c, c, jnp.ones((S, 32), F32)], 1),
           jnp.concatenate([z64, zero, s, z32], 1),
           jnp.concatenate([z64, -s, zero, z32], 1))
    c, s = cs(64)
    dil = (jnp.concatenate([c, c, c, c], 1),
           jnp.concatenate([z32, s, z32, s], 1),
           jnp.concatenate([-s, z32, -s, z32], 1))
    return mla, dil


def _rope(x, C, S1, S2, sh):
    return x * C + pltpu.roll(x, sh, 1) * S1 + pltpu.roll(x, LANE - sh, 1) * S2


def _rope_t(dy, C, S1, S2, sh):
    return dy * C + pltpu.roll(dy * S1, LANE - sh, 1) + pltpu.roll(dy * S2, sh, 1)


def _lane(shape):
    return lax.broadcasted_iota(jnp.int32, shape, 1)


T_MLA = 256
TA = 512


def _zcol(T, width, col_lanes):
    assert (col_lanes * LANE) % width == 0
    return pl.BlockSpec((T, width), lambda i: (i, col_lanes * LANE // width))


def _full(shape):
    return pl.BlockSpec(shape, lambda *_: (0,) * len(shape))


def _mla_pre_fwd(zp, w, tab):
    S = zp.shape[0]
    T = T_MLA

    def body(cq_ref, ckv_ref, kr_ref, gcq_ref, gckv_ref, wuq_ref, wuk_ref, wuv_ref, gq_ref, gk_ref, C_ref, S1_ref, S2_ref,
             q_ref, k_ref, v_ref):
        cq = cq_ref[...]
        cqn = (cq * lax.rsqrt(jnp.mean(cq * cq, axis=-1, keepdims=True) + EPS) * gcq_ref[...]).astype(BF16)
        ckv = ckv_ref[...]
        ckvn = (ckv * lax.rsqrt(jnp.mean(ckv * ckv, axis=-1, keepdims=True) + EPS) * gckv_ref[...]).astype(BF16)
        q0 = _nn(cqn, wuq_ref[...])
        k0 = _nn(ckvn, wuk_ref[...])
        krb = kr_ref[...]
        C, S1, S2 = C_ref[...], S1_ref[...], S2_ref[...]
        for h in range(8):
            sl = slice(h * LANE, (h + 1) * LANE)
            xq = q0[:, sl]
            xq = xq * lax.rsqrt(_rsum(xq * xq) * (1.0 / MLA_QK) + EPS) * gq_ref[...]
            q_ref[:, sl] = _rope(xq, C, S1, S2, 16).astype(BF16)
            xk = k0[:, sl] + krb
            xk = xk * lax.rsqrt(_rsum(xk * xk) * (1.0 / MLA_QK) + EPS) * gk_ref[...]
            k_ref[:, sl] = _rope(xk, C, S1, S2, 16).astype(BF16)
        v_ref[...] = _nn(ckvn, wuv_ref[...]).astype(BF16)

    tabspec = pl.BlockSpec((T, LANE), lambda i: (i, 0))
    in_specs = [_zcol(T, 256, C_CQ), _zcol(T, LANE, C_CKV), _zcol(T, LANE, C_KR), _full((1, 256)), _full((1, LANE)),
                _full((256, 1024)), _full((LANE, 1024)), _full((LANE, 512)), _full((1, LANE)), _full((1, LANE)),
                tabspec, tabspec, tabspec]
    return pl.pallas_call(
        body, name="mla_pre_fwd", grid=(S // T,), in_specs=in_specs,
        out_specs=[pl.BlockSpec((T, 1024), lambda i: (i, 0)), pl.BlockSpec((T, 1024), lambda i: (i, 0)), pl.BlockSpec((T, 512), lambda i: (i, 0))],
        out_shape=[jax.ShapeDtypeStruct((S, 1024), BF16), jax.ShapeDtypeStruct((S, 1024), BF16), jax.ShapeDtypeStruct((S, 512), BF16)],
        compiler_params=_cp(),
    )(zp, zp, zp, w["g_cq"], w["g_ckv"], w["w_uq"], w["w_uk"], w["w_uv"], w["g_mq"], w["g_mk"], *tab)


def _mla_attn_fwd(q, k, v, zp):
    S = q.shape[0]
    nq = S // TA

    def body(q_ref, k_ref, v_ref, g_ref, o_ref, lse_ref, y_ref):
        qi = pl.program_id(1)
        lane = _lane((TA, LANE))
        rowi = lax.broadcasted_iota(jnp.int32, (TA, TA), 0)
        coli = lax.broadcasted_iota(jnp.int32, (TA, TA), 1)
        o_tot = jnp.zeros((TA, LANE), F32)
        for hh in range(2):
            cs = slice(hh * LANE, (hh + 1) * LANE)
            hm = (lane < 64) if hh == 0 else (lane >= 64)
            qh = q_ref[:, cs]

            def step(kb, carry, masked, cs=cs, hm=hm, qh=qh):
                m, l, acc = carry
                off = pl.multiple_of(kb * TA, TA)
                kh = k_ref[pl.ds(off, TA), cs]
                vv = v_ref[pl.ds(off, TA), :]
                vh = jnp.where(hm, vv, jnp.zeros_like(vv))
                s = _nt(qh, kh) * MLA_SCALE
                if masked:
                    s = jnp.where(rowi >= coli, s, NEG)
                m_new = jnp.maximum(m, jnp.max(s, axis=-1, keepdims=True))
                alpha = jnp.exp(m - m_new)
                p = jnp.exp(s - m_new)
                l = alpha * l + _rsum(p)
                acc = alpha * acc + _nn(p.astype(BF16), vh)
                return m_new, l, acc

            init = (jnp.full((TA, 1), NEG, F32), jnp.zeros((TA, 1), F32), jnp.zeros((TA, LANE), F32))
            carry = lax.fori_loop(0, qi, lambda kb, c: step(kb, c, False), init)
            m, l, acc = step(qi, carry, True)
            o_tot = o_tot + acc / l
            lse_ref[:, cs] = jnp.broadcast_to(m + jnp.log(l), (TA, LANE))
        o_ref[...] = o_tot
        y_ref[...] = (o_tot * _silu(g_ref[...])).astype(BF16)

    blk = pl.BlockSpec((TA, LANE), lambda p, i: (i, p))
    return pl.pallas_call(
        body, name="mla_attn_fwd", grid=(4, nq),
        in_specs=[pl.BlockSpec((TA, 256), lambda p, i: (i, p)), pl.BlockSpec((S, 256), lambda p, i: (0, p)),
                  pl.BlockSpec((S, LANE), lambda p, i: (0, p)), pl.BlockSpec((TA, LANE), lambda p, i: (i, C_MLAG + p))],
        out_specs=[blk, pl.BlockSpec((TA, 256), lambda p, i: (i, p)), blk],
        out_shape=[jax.ShapeDtypeStruct((S, 512), F32), jax.ShapeDtypeStruct((S, 1024), F32), jax.ShapeDtypeStruct((S, 512), BF16)],
        compiler_params=_cp(dimension_semantics=("parallel", "arbitrary")),
    )(q, k, v, zp)


def _mla_post_bwd(zp, o, dy, dz):
    S = zp.shape[0]
    T = T_ROW

    def body(g_ref, o_ref, dy_ref, dz_in, dz_ref, do_ref, D_ref):
        del dz_in
        g, o_, dy_ = g_ref[...], o_ref[...], dy_ref[...]
        do = dy_ * _silu(g)
        do_ref[...] = do.astype(BF16)
        dz_ref[...] = (dy_ * o_ * _dsilu(g)).astype(BF16)
        prod = do * o_
        lane = _lane((T, LANE))
        for p in range(4):
            pr = prod[:, p * LANE:(p + 1) * LANE]
            da = _rsum(jnp.where(lane < 64, pr, 0.0))
            db = _rsum(jnp.where(lane >= 64, pr, 0.0))
            D_ref[:, 2 * p * LANE:(2 * p + 1) * LANE] = jnp.broadcast_to(da, (T, LANE))
            D_ref[:, (2 * p + 1) * LANE:(2 * p + 2) * LANE] = jnp.broadcast_to(db, (T, LANE))

    row = pl.BlockSpec((T, 512), lambda i: (i, 0))
    zc = _zcol(T, 512, C_MLAG)
    return pl.pallas_call(
        body, name="mla_post_bwd", grid=(S // T,), in_specs=[zc, row, row, pl.BlockSpec(memory_space=pl.ANY)],
        out_specs=[zc, row, pl.BlockSpec((T, 1024), lambda i: (i, 0))],
        out_shape=[jax.ShapeDtypeStruct(dz.shape, BF16), jax.ShapeDtypeStruct((S, 512), BF16), jax.ShapeDtypeStruct((S, 1024), F32)],
        input_output_aliases={3: 0}, compiler_params=_cp(),
    )(zp, o, dy, dz)


def _mla_attn_bwd(q, k, v, do, lse, Dr):
    S = q.shape[0]
    nq = S // TA

    def body(q_ref, do_ref, lse_ref, D_ref, k_ref, v_ref, dq_ref, dk_ref, dv_ref):
        ki = pl.program_id(1)

        @pl.when(ki == 0)
        def _():
            dq_ref[...] = jnp.zeros_like(dq_ref)

        lane = _lane((TA, LANE))
        rowi = lax.broadcasted_iota(jnp.int32, (TA, TA), 0)
        coli = lax.broadcasted_iota(jnp.int32, (TA, TA), 1)
        dv_tot = jnp.zeros((TA, LANE), F32)
        for hh in range(2):
            cs = slice(hh * LANE, (hh + 1) * LANE)
            hm = (lane < 64) if hh == 0 else (lane >= 64)
            kh = k_ref[:, cs]
            vv = v_ref[...]
            vm = jnp.where(hm, vv, jnp.zeros_like(vv))

            def step(qb, carry, masked, cs=cs, kh=kh, vm=vm):
                dk_acc, dv_acc = carry
                off = pl.multiple_of(qb * TA, TA)
                qh = q_ref[pl.ds(off, TA), cs]
                doh = do_ref[pl.ds(off, TA), :]
                ls = jnp.tile(lse_ref[pl.ds(off, TA), cs], (1, TA // LANE))
                dd = jnp.tile(D_ref[pl.ds(off, TA), cs], (1, TA // LANE))
                s = _nt(qh, kh) * MLA_SCALE
                if masked:
                    s = jnp.where(rowi >= coli, s, NEG)
                p = jnp.exp(s - ls)
                dp = _nt(doh, vm)
                ds = (p * (dp - dd) * MLA_SCALE).astype(BF16)
                dv_acc = dv_acc + _tn(p.astype(BF16), doh)
                dk_acc = dk_acc + _tn(ds, qh)
                dq_ref[pl.ds(off, TA), cs] += _nn(ds, kh)
                return dk_acc, dv_acc

            z = jnp.zeros((TA, LANE), F32)
            carry = step(ki, (z, z), True)
            dk_acc, dv_acc = lax.fori_loop(ki + 1, nq, lambda qb, c: step(qb, c, False), carry)
            dk_ref[:, cs] = dk_acc
            dv_tot = dv_tot + jnp.where(hm, dv_acc, 0.0)
        dv_ref[...] = dv_tot

    pair = pl.BlockSpec((S, 256), lambda p, i: (0, p))
    return pl.pallas_call(
        body, name="mla_attn_bwd", grid=(4, nq),
        in_specs=[pair, pl.BlockSpec((S, LANE), lambda p, i: (0, p)), pair, pair,
                  pl.BlockSpec((TA, 256), lambda p, i: (i, p)), pl.BlockSpec((TA, LANE), lambda p, i: (i, p))],
        out_specs=[pair, pl.BlockSpec((TA, 256), lambda p, i: (i, p)), pl.BlockSpec((TA, LANE), lambda p, i: (i, p))],
        out_shape=[jax.ShapeDtypeStruct((S, 1024), F32), jax.ShapeDtypeStruct((S, 1024), F32), jax.ShapeDtypeStruct((S, 512), F32)],
        compiler_params=_cp(dimension_semantics=("parallel", "arbitrary")),
    )(q, do, lse, Dr, k, v)


def _mla_pre_bwd(zp, dq, dk, dv, w, tab, dz):
    S = zp.shape[0]
    T = T_MLA

    def body(cq_ref, ckv_ref, kr_ref, dq_ref, dk_ref, dv_ref, gcq_ref, gckv_ref, wuq_ref, wuk_ref, wuv_ref, gq_ref, gk_ref,
             C_ref, S1_ref, S2_ref, dz_in, dz_ref, dwuq_ref, dwuk_ref, dwuv_ref, dgcq_ref, dgckv_ref, dgq_ref, dgk_ref):
        del dz_in
        i = pl.program_id(0)

        @pl.when(i == 0)
        def _():
            for r in (dwuq_ref, dwuk_ref, dwuv_ref, dgcq_ref, dgckv_ref, dgq_ref, dgk_ref):
                r[...] = jnp.zeros_like(r)

        cq = cq_ref[...]
        rq = lax.rsqrt(jnp.mean(cq * cq, axis=-1, keepdims=True) + EPS)
        cqh = cq * rq
        cqn = (cqh * gcq_ref[...]).astype(BF16)
        ckv = ckv_ref[...]
        rkv = lax.rsqrt(jnp.mean(ckv * ckv, axis=-1, keepdims=True) + EPS)
        ckvh = ckv * rkv
        ckvn = (ckvh * gckv_ref[...]).astype(BF16)
        q0 = _nn(cqn, wuq_ref[...])
        k0 = _nn(ckvn, wuk_ref[...])
        krb = kr_ref[...]
        C, S1, S2 = C_ref[...], S1_ref[...], S2_ref[...]
        gq, gk = gq_ref[...], gk_ref[...]

        def head_bwd(x, dy, g):
            r = lax.rsqrt(_rsum(x * x) * (1.0 / MLA_QK) + EPS)
            xn = x * r
            dyn = _rope_t(dy, C, S1, S2, 16)
            dxh = dyn * g
            return r * (dxh - xn * _rsum(dxh * xn) * (1.0 / MLA_QK)), _csum(dyn * xn)

        dq0, dk0 = [], []
        dgq_acc = jnp.zeros((1, LANE), F32)
        dgk_acc = jnp.zeros((1, LANE), F32)
        dkr = jnp.zeros((T, LANE), F32)
        for h in range(8):
            sl = slice(h * LANE, (h + 1) * LANE)
            dxq, gq_p = head_bwd(q0[:, sl], dq_ref[:, sl], gq)
            dxk, gk_p = head_bwd(k0[:, sl] + krb, dk_ref[:, sl], gk)
            dq0.append(dxq.astype(BF16))
            dk0.append(dxk.astype(BF16))
            dkr = dkr + dxk
            dgq_acc = dgq_acc + gq_p
            dgk_acc = dgk_acc + gk_p
        dgq_ref[...] += dgq_acc
        dgk_ref[...] += dgk_acc
        dq0 = jnp.concatenate(dq0, axis=1)
        dk0 = jnp.concatenate(dk0, axis=1)
        dvb = dv_ref[...].astype(BF16)
        dwuq_ref[...] += _tn(cqn, dq0)
        dwuk_ref[...] += _tn(ckvn, dk0)
        dwuv_ref[...] += _tn(ckvn, dvb)
        dcqn = _nt(dq0, wuq_ref[...])
        dckvn = _nt(dk0, wuk_ref[...]) + _nt(dvb, wuv_ref[...])
        dgcq_ref[...] += _csum(dcqn * cqh)
        dgckv_ref[...] += _csum(dckvn * ckvh)
        dxh = dcqn * gcq_ref[...]
        dz_ref[:, 0:256] = (rq * (dxh - cqh * jnp.mean(dxh * cqh, axis=-1, keepdims=True))).astype(BF16)
        dxh = dckvn * gckv_ref[...]
        dz_ref[:, 256:384] = (rkv * (dxh - ckvh * jnp.mean(dxh * ckvh, axis=-1, keepdims=True))).astype(BF16)
        lane = _lane((T, LANE))
        dz_ref[:, 384:512] = jnp.where((lane >= KR_LANE) & (lane < KR_LANE + 32), dkr, 0.0).astype(BF16)

    tabspec = pl.BlockSpec((T, LANE), lambda i: (i, 0))
    in_specs = [_zcol(T, 256, C_CQ), _zcol(T, LANE, C_CKV), _zcol(T, LANE, C_KR),
                pl.BlockSpec((T, 1024), lambda i: (i, 0)), pl.BlockSpec((T, 1024), lambda i: (i, 0)), pl.BlockSpec((T, 512), lambda i: (i, 0)),
                _full((1, 256)), _full((1, LANE)), _full((256, 1024)), _full((LANE, 1024)), _full((LANE, 512)), _full((1, LANE)), _full((1, LANE)),
                tabspec, tabspec, tabspec, pl.BlockSpec(memory_space=pl.ANY)]
    out_specs = [_zcol(T, 512, C_CQ), _full((256, 1024)), _full((LANE, 1024)), _full((LANE, 512)), _full((1, 256)), _full((1, LANE)),
                 _full((1, LANE)), _full((1, LANE))]
    out_shape = [jax.ShapeDtypeStruct(dz.shape, BF16), jax.ShapeDtypeStruct((256, 1024), F32), jax.ShapeDtypeStruct((LANE, 1024), F32),
                 jax.ShapeDtypeStruct((LANE, 512), F32), jax.ShapeDtypeStruct((1, 256), F32), jax.ShapeDtypeStruct((1, LANE), F32),
                 jax.ShapeDtypeStruct((1, LANE), F32), jax.ShapeDtypeStruct((1, LANE), F32)]
    return pl.pallas_call(
        body, name="mla_pre_bwd", grid=(S // T,), in_specs=in_specs, out_specs=out_specs, out_shape=out_shape,
        input_output_aliases={16: 0}, compiler_params=_cp(),
    )(zp, zp, zp, dq, dk, dv, w["g_cq"], w["g_ckv"], w["w_uq"], w["w_uk"], w["w_uv"], w["g_mq"], w["g_mk"], *tab, dz)


T_DIL = 256


def _head_stats(x, lane):
    sq = x * x
    sa = _rsum(jnp.where(lane < 64, sq, 0.0))
    sb = _rsum(jnp.where(lane >= 64, sq, 0.0))
    return lax.rsqrt(jnp.where(lane < 64, sa, sb) * (1.0 / DIL_HD) + EPS)


def _head_sum(x, lane):
    sa = _rsum(jnp.where(lane < 64, x, 0.0))
    sb = _rsum(jnp.where(lane >= 64, x, 0.0))
    return jnp.where(lane < 64, sa, sb)


def _dil_pre_fwd(zp, w, tab):
    S = zp.shape[0]
    T = T_DIL

    def body(q_ref, k_ref, gq_ref, gk_ref, C_ref, S1_ref, S2_ref, qo_ref, ko_ref):
        C, S1, S2 = C_ref[...], S1_ref[...], S2_ref[...]
        lane = _lane((T, LANE))
        for b in range(12):
            sl = slice(b * LANE, (b + 1) * LANE)
            x = q_ref[:, sl]
            qo_ref[:, sl] = _rope(x * _head_stats(x, lane) * gq_ref[...], C, S1, S2, 32)
            x = k_ref[:, sl]
            ko_ref[:, sl] = _rope(x * _head_stats(x, lane) * gk_ref[...], C, S1, S2, 32)

    tabspec = pl.BlockSpec((T, LANE), lambda i: (i, 0))
    out = pl.BlockSpec((T, 1536), lambda i: (i, 0))
    return pl.pallas_call(
        body, name="dil_pre_fwd", grid=(S // T,),
        in_specs=[_zcol(T, 1536, C_DQ), _zcol(T, 1536, C_DK), _full((1, LANE)), _full((1, LANE)), tabspec, tabspec, tabspec],
        out_specs=[out, out], out_shape=[jax.ShapeDtypeStruct((S, 1536), F32)] * 2, compiler_params=_cp(),
    )(zp, zp, w["g_dq"], w["g_dk"], *tab)


DIL_ROWS = 2048


def _dil_geometry(g, S):
    d = DIL_DILATIONS[g]
    P = NK * d
    return d, P, DIL_ROWS // P, S // P


def _dil_rows(start, d):
    return pl.ds(pl.multiple_of(start, NK), NK) if d == 1 else pl.ds(start, NK, stride=d)


def _dil_specs(g, S, col0):
    _, P, m, nb = _dil_geometry(g, S)
    cur = pl.BlockSpec((DIL_ROWS, LANE), lambda sb, c: (sb, col0 + c))
    prv = pl.BlockSpec((P, LANE), lambda sb, c: (jnp.maximum(sb * m - 1, 0), col0 + c))
    nxt = pl.BlockSpec((P, LANE), lambda sb, c: (jnp.minimum((sb + 1) * m, nb - 1), col0 + c))
    return cur, prv, nxt


def _dil_masks(n, nb):
    row = lax.broadcasted_iota(jnp.int32, (NK, NK), 0)
    col = lax.broadcasted_iota(jnp.int32, (NK, NK), 1)
    return col <= row, (col >= row) & (n > 0), (col >= row) & (n < nb - 1)


def _dil_attn_fwd(q, k, zp, g):
    S = q.shape[0]
    d, P, m, nb = _dil_geometry(g, S)
    R = DIL_ROWS

    def body(q_ref, kc_ref, kp_ref, vc_ref, vp_ref, o_ref, lse_ref, *scr):
        sb = pl.program_id(0)
        if m > 1:
            ks_ref, vs_ref = scr
            ks_ref[0:P, :] = kp_ref[...]
            ks_ref[P:P + R, :] = kc_ref[...]
            vs_ref[0:P, :] = vp_ref[...]
            vs_ref[P:P + R, :] = vc_ref[...]
        lane = _lane((NK, LANE))

        def unit(u, carry):
            j = u // d
            rows = _dil_rows(j * P + (u - j * d), d)
            if m > 1:
                rows_c = _dil_rows(j * P + (u - j * d) + P, d)
                kp, kc, vp, vc = ks_ref[rows, :], ks_ref[rows_c, :], vs_ref[rows, :], vs_ref[rows_c, :]
            else:
                kp, kc, vp, vc = kp_ref[rows, :], kc_ref[rows, :], vp_ref[rows, :], vc_ref[rows, :]
            kp, kc, vp, vc = kp.astype(BF16), kc.astype(BF16), vp.astype(BF16), vc.astype(BF16)
            q_ = q_ref[rows, :].astype(BF16)
            mc, mp, _ = _dil_masks(sb * m + j, nb)
            zb = jnp.zeros_like(q_)
            o_tot = jnp.zeros((NK, LANE), F32)
            lse_tot = jnp.zeros((NK, LANE), F32)
            for hh in range(2):
                hm = (lane < 64) if hh == 0 else (lane >= 64)
                qm = jnp.where(hm, q_, zb)
                sc = jnp.where(mc, _nt(qm, kc) * DIL_SCALE, NEG)
                sp = jnp.where(mp, _nt(qm, kp) * DIL_SCALE, NEG)
                mx = jnp.maximum(jnp.max(sc, axis=-1, keepdims=True), jnp.max(sp, axis=-1, keepdims=True))
                ec = jnp.exp(sc - mx)
                ep = jnp.exp(sp - mx)
                den = _rsum(ec) + _rsum(ep)
                o = (_nn(ec.astype(BF16), jnp.where(hm, vc, zb)) + _nn(ep.astype(BF16), jnp.where(hm, vp, zb))) / den
                o_tot = o_tot + o
                lse_tot = jnp.where(hm, mx + jnp.log(den), lse_tot)
            o_ref[rows, :] = o_tot
            lse_ref[rows, :] = lse_tot
            return carry

        lax.fori_loop(0, R // NK, unit, 0, unroll=2)

    qcur, qprv, _ = _dil_specs(g, S, 4 * g)
    vcur, vprv, _ = _dil_specs(g, S, C_DV + 4 * g)
    out = pl.BlockSpec((R, LANE), lambda sb, c: (sb, c))
    return pl.pallas_call(
        body, name=f"dil_attn_fwd{g}", grid=(S // R, 4), in_specs=[qcur, qcur, qprv, vcur, vprv], out_specs=[out, out],
        out_shape=[jax.ShapeDtypeStruct((S, 512), F32)] * 2,
        scratch_shapes=[pltpu.VMEM((P + R, LANE), F32)] * 2 if m > 1 else [], compiler_params=_cp(),
    )(q, k, k, zp, zp)


def _dil_combine(os_, ls_, zp):
    S = zp.shape[0]
    T = T_ROW

    def body(o0, o1, o2, l0, l1, l2, g_ref, oc_ref, L_ref, y_ref):
        a, b, c = l0[...], l1[...], l2[...]
        mx = jnp.maximum(jnp.maximum(a, b), c)
        ea, eb, ec = jnp.exp(a - mx), jnp.exp(b - mx), jnp.exp(c - mx)
        den = ea + eb + ec
        oc = (ea * o0[...] + eb * o1[...] + ec * o2[...]) / den
        oc_ref[...] = oc
        L_ref[...] = mx + jnp.log(den)
        y_ref[...] = (oc * _silu(g_ref[...])).astype(BF16)

    row = pl.BlockSpec((T, 512), lambda i: (i, 0))
    return pl.pallas_call(
        body, name="dil_combine", grid=(S // T,), in_specs=[row] * 6 + [_zcol(T, 512, C_DILG)], out_specs=[row, row, row],
        out_shape=[jax.ShapeDtypeStruct((S, 512), F32), jax.ShapeDtypeStruct((S, 512), F32), jax.ShapeDtypeStruct((S, 512), BF16)],
        compiler_params=_cp(),
    )(*os_, *ls_, zp)


def _dil_comb_bwd(zp, oc, dy, dz):
    S = zp.shape[0]
    T = T_ROW

    def body(g_ref, o_ref, dy_ref, dz_in, dz_ref, do_ref, D_ref):
        del dz_in
        g, o_, dy_ = g_ref[...], o_ref[...], dy_ref[...]
        do = dy_ * _silu(g)
        do_ref[...] = do
        dz_ref[...] = (dy_ * o_ * _dsilu(g)).astype(BF16)
        lane = _lane((T, LANE))
        for p in range(4):
            sl = slice(p * LANE, (p + 1) * LANE)
            D_ref[:, sl] = _head_sum(do[:, sl] * o_[:, sl], lane)

    row = pl.BlockSpec((T, 512), lambda i: (i, 0))
    zc = _zcol(T, 512, C_DILG)
    return pl.pallas_call(
        body, name="dil_comb_bwd", grid=(S // T,), in_specs=[zc, row, row, pl.BlockSpec(memory_space=pl.ANY)], out_specs=[zc, row, row],
        out_shape=[jax.ShapeDtypeStruct(dz.shape, BF16), jax.ShapeDtypeStruct((S, 512), F32), jax.ShapeDtypeStruct((S, 512), F32)],
        input_output_aliases={3: 0}, compiler_params=_cp(),
    )(zp, oc, dy, dz)


def _dil_attn_bwd(q, k, zp, do, L, Dr, g):
    S = q.shape[0]
    d, P, m, nb = _dil_geometry(g, S)
    R = DIL_ROWS
    n_q, n_k = 4, 2

    def body(*refs):
        q_side = refs[0:2 * n_q]
        k_side = refs[2 * n_q:2 * n_q + 2 * n_k]
        dq_ref, dk_ref, dv_ref = refs[2 * n_q + 2 * n_k:2 * n_q + 2 * n_k + 3]
        scr = refs[2 * n_q + 2 * n_k + 3:]
        sb = pl.program_id(0)
        if m > 1:
            for a in range(n_q):
                scr[a][0:R, :] = q_side[2 * a][...]
                scr[a][R:R + P, :] = q_side[2 * a + 1][...]
            for a in range(n_k):
                scr[n_q + a][0:P, :] = k_side[2 * a + 1][...]
                scr[n_q + a][P:P + R, :] = k_side[2 * a][...]
        lane = _lane((NK, LANE))

        def unit(u, carry):
            j = u // d
            start = j * P + (u - j * d)
            rows = _dil_rows(start, d)
            if m > 1:
                rows_b = _dil_rows(start + P, d)
                qc, doc, Lc_, Dc_ = [scr[a][rows, :] for a in range(n_q)]
                qn, don, Ln_, Dn_ = [scr[a][rows_b, :] for a in range(n_q)]
                kp, vp = [scr[n_q + a][rows, :] for a in range(n_k)]
                kc, vc = [scr[n_q + a][rows_b, :] for a in range(n_k)]
            else:
                qc, doc, Lc_, Dc_ = [q_side[2 * a][rows, :] for a in range(n_q)]
                qn, don, Ln_, Dn_ = [q_side[2 * a + 1][rows, :] for a in range(n_q)]
                kc, vc = [k_side[2 * a][rows, :] for a in range(n_k)]
                kp, vp = [k_side[2 * a + 1][rows, :] for a in range(n_k)]
            qc, qn, doc, don = qc.astype(BF16), qn.astype(BF16), doc.astype(BF16), don.astype(BF16)
            kc, kp, vc, vp = kc.astype(BF16), kp.astype(BF16), vc.astype(BF16), vp.astype(BF16)
            mc, mp, mnext = _dil_masks(sb * m + j, nb)
            zb = jnp.zeros_like(qc)
            dq_tot = jnp.zeros((NK, LANE), F32)
            dk_tot = jnp.zeros((NK, LANE), F32)
            dv_tot = jnp.zeros((NK, LANE), F32)
            for hh in range(2):
                hm = (lane < 64) if hh == 0 else (lane >= 64)

                def bcast(x, hm=hm):
                    return jnp.where(hm, x, pltpu.roll(x, 64, 1))

                Lc, Ln, Dc, Dn = bcast(Lc_), bcast(Ln_), bcast(Dc_), bcast(Dn_)
                qm = jnp.where(hm, qc, zb)
                qnm = jnp.where(hm, qn, zb)
                vcm = jnp.where(hm, vc, zb)
                vpm = jnp.where(hm, vp, zb)
                pc = jnp.exp(jnp.where(mc, _nt(qm, kc) * DIL_SCALE, NEG) - Lc)
                pp = jnp.exp(jnp.where(mp, _nt(qm, kp) * DIL_SCALE, NEG) - Lc)
                dsc = (pc * (_nt(doc, vcm) - Dc) * DIL_SCALE).astype(BF16)
                dsp = (pp * (_nt(doc, vpm) - Dc) * DIL_SCALE).astype(BF16)
                dq_tot = dq_tot + jnp.where(hm, _nn(dsc, kc) + _nn(dsp, kp), 0.0)
                p2 = jnp.exp(jnp.where(mnext, _nt(qnm, kc) * DIL_SCALE, NEG) - Ln)
                ds2 = (p2 * (_nt(don, vcm) - Dn) * DIL_SCALE).astype(BF16)
                dk_tot = dk_tot + _tn(dsc, qm) + _tn(ds2, qnm)
                dv_tot = dv_tot + jnp.where(hm, _tn(pc.astype(BF16), doc) + _tn(p2.astype(BF16), don), 0.0)
            dq_ref[rows, :] = dq_tot
            dk_ref[rows, :] = dk_tot
            dv_ref[rows, :] = dv_tot
            return carry

        lax.fori_loop(0, R // NK, unit, 0, unroll=2)

    qcur, qprv, qnxt = _dil_specs(g, S, 4 * g)
    vcur, vprv, _ = _dil_specs(g, S, C_DV + 4 * g)
    ocur, _, onxt = _dil_specs(g, S, 0)
    out = pl.BlockSpec((R, LANE), lambda sb, c: (sb, c))
    scratch = [pltpu.VMEM((P + R, LANE), F32)] * (n_q + n_k) if m > 1 else []
    return pl.pallas_call(
        body, name=f"dil_attn_bwd{g}", grid=(S // R, 4),
        in_specs=[qcur, qnxt, ocur, onxt, ocur, onxt, ocur, onxt, qcur, qprv, vcur, vprv],
        out_specs=[out, out, out], out_shape=[jax.ShapeDtypeStruct((S, 512), F32)] * 3, scratch_shapes=scratch, compiler_params=_cp(),
    )(q, q, do, do, L, L, Dr, Dr, k, k, zp, zp)


def _dil_pre_bwd(zp, dys, g, tab, dz, col, name):
    S = zp.shape[0]
    T = T_DIL

    def body(x_ref, dy0_ref, dy1_ref, dy2_ref, g_ref, C_ref, S1_ref, S2_ref, dz_in, dz_ref, dg_ref):
        del dz_in
        i = pl.program_id(0)
        C, S1, S2 = C_ref[...], S1_ref[...], S2_ref[...]
        lane = _lane((T, LANE))
        gv = g_ref[...]
        acc = jnp.zeros((1, LANE), F32)
        for b in range(12):
            sl = slice(b * LANE, (b + 1) * LANE)
            x = x_ref[:, sl]
            r = _head_stats(x, lane)
            xn = x * r
            dy_ref = (dy0_ref, dy1_ref, dy2_ref)[b // 4]
            dyn = _rope_t(dy_ref[:, (b % 4) * LANE:(b % 4 + 1) * LANE], C, S1, S2, 32)
            acc = acc + _csum(dyn * xn)
            dxh = dyn * gv
            dz_ref[:, sl] = (r * (dxh - xn * _head_sum(dxh * xn, lane) * (1.0 / DIL_HD))).astype(BF16)

        @pl.when(i == 0)
        def _():
            dg_ref[...] = acc

        @pl.when(i > 0)
        def _():
            dg_ref[...] += acc

    tabspec = pl.BlockSpec((T, LANE), lambda i: (i, 0))
    zc = _zcol(T, 1536, col)
    grp = pl.BlockSpec((T, 512), lambda i: (i, 0))
    return pl.pallas_call(
        body, name=name, grid=(S // T,),
        in_specs=[zc, grp, grp, grp, _full((1, LANE)), tabspec, tabspec, tabspec, pl.BlockSpec(memory_space=pl.ANY)],
        out_specs=[zc, _full((1, LANE))], out_shape=[jax.ShapeDtypeStruct(dz.shape, BF16), jax.ShapeDtypeStruct((1, LANE), F32)],
        input_output_aliases={8: 0}, compiler_params=_cp(),
    )(zp, *dys, g, *tab, dz)


def _dil_dv_into(dvs, dz):
    S = dz.shape[0]
    T = T_ROW

    def body(s0, s1, s2, dz_in, o_ref):
        del dz_in
        for gi, s in enumerate((s0, s1, s2)):
            o_ref[:, gi * 512:(gi + 1) * 512] = s[...].astype(BF16)

    grp = pl.BlockSpec((T, 512), lambda i: (i, 0))
    return pl.pallas_call(
        body, name="dil_dv", grid=(S // T,), in_specs=[grp, grp, grp, pl.BlockSpec(memory_space=pl.ANY)],
        out_specs=_zcol(T, 1536, C_DV), out_shape=jax.ShapeDtypeStruct(dz.shape, BF16), input_output_aliases={3: 0}, compiler_params=_cp(),
    )(*dvs, dz)


T_MRG = 256


def _merge_fwd(P, zp, b_merge):
    S = zp.shape[0]
    T = T_MRG

    def body(p0, p1, p2, m0, m1, m2, b_ref, o_ref):
        acc = jnp.zeros((T, D), F32)
        for j, (p, m) in enumerate(((p0, m0), (p1, m1), (p2, m2))):
            acc = acc + _sig(m[...] + b_ref[:, j * D:(j + 1) * D]) * p[...]
        o_ref[...] = acc.astype(BF16)

    row = pl.BlockSpec((T, D), lambda i: (i, 0))
    return pl.pallas_call(
        body, name="merge_fwd", grid=(S // T,),
        in_specs=[row, row, row] + [_zcol(T, D, C_MERGE + 8 * j) for j in range(3)] + [_full((1, 3 * D))], out_specs=row,
        out_shape=jax.ShapeDtypeStruct((S, D), BF16), compiler_params=_cp(),
    )(*P, zp, zp, zp, b_merge)


def _merge_bwd(dm, Pj, zp, bj, dz, j):
    S = zp.shape[0]
    T = T_MRG

    def body(dm_ref, p_ref, m_ref, b_ref, dz_in, dz_ref, dp_ref, db_ref):
        del dz_in
        i = pl.program_id(0)
        g = _sig(m_ref[...] + b_ref[...])
        dmv = dm_ref[...]
        dp_ref[...] = (dmv * g).astype(BF16)
        dg = dmv * p_ref[...] * g * (1.0 - g)
        dz_ref[...] = dg.astype(BF16)
        part = _csum(dg)

        @pl.when(i == 0)
        def _():
            db_ref[...] = part

        @pl.when(i > 0)
        def _():
            db_ref[...] += part

    row = pl.BlockSpec((T, D), lambda i: (i, 0))
    zc = _zcol(T, D, C_MERGE + 8 * j)
    return pl.pallas_call(
        body, name=f"merge_bwd{j}", grid=(S // T,), in_specs=[row, row, zc, _full((1, D)), pl.BlockSpec(memory_space=pl.ANY)],
        out_specs=[zc, row, _full((1, D))],
        out_shape=[jax.ShapeDtypeStruct(dz.shape, BF16), jax.ShapeDtypeStruct((S, D), BF16), jax.ShapeDtypeStruct((1, D), F32)],
        input_output_aliases={4: 0}, compiler_params=_cp(),
    )(dm, Pj, zp, bj, dz)


def _loss_fwd_bwd(y, target):
    S = y.shape[0]
    T = T_ROW

    def body(y_ref, t_ref, loss_ref, dy_ref):
        i = pl.program_id(0)
        err = y_ref[...] - t_ref[...]
        dy_ref[...] = err * (1.0 / D)
        part = jnp.sum(err * err, keepdims=True).reshape(1, 1) * (0.5 / D)

        @pl.when(i == 0)
        def _():
            loss_ref[...] = part

        @pl.when(i > 0)
        def _():
            loss_ref[...] += part

    row = pl.BlockSpec((T, D), lambda i: (i, 0))
    return pl.pallas_call(
        body, name="loss", grid=(S // T,), in_specs=[row, row], out_specs=[_full((1, 1)), row],
        out_shape=[jax.ShapeDtypeStruct((1, 1), F32), jax.ShapeDtypeStruct((S, D), F32)], compiler_params=_cp(),
    )(y, target)


def _layer_fwd(x, w, tabs):
    mla_tab, dil_tab = tabs
    S = x.shape[0]
    h = _rms_in_fwd(x, w["norm_g"])
    zp = _mm(h, w["w_in"], mode="nn", name="in_proj")
    hs, y_lru = _lru_fwd(zp, w)
    q, k, v = _mla_pre_fwd(zp, w, mla_tab)
    o_mla, lse, y_mla = _mla_attn_fwd(q, k, v, zp)
    qd, kd = _dil_pre_fwd(zp, w, dil_tab)
    og, lg = zip(*[_dil_attn_fwd(qd, kd, zp, g) for g in range(len(DIL_DILATIONS))])
    oc, L, y_dil = _dil_combine(og, lg, zp)
    P = [_mm(y_lru, w["w_lru_o"], mode="nn", name="lru_out"), _mm(y_mla, w["w_mla_o"], mode="nn", name="mla_out"),
         _mm(y_dil, w["w_dil_o"], mode="nn", name="dil_out")]
    merged = _merge_fwd(P, zp, w["b_merge"])
    x_out = _mm(merged, w["w_out"], mode="nn", name="out_proj", add=x)
    saved = dict(x=x, h=h, zp=zp, hs=hs, y=(y_lru, y_mla, y_dil), q=q, k=k, v=v, o_mla=o_mla, lse=lse, qd=qd, kd=kd, oc=oc, L=L, P=P,
                 merged=merged)
    return x_out, saved


def _layer_bwd(dout, w, tabs, sv, hook=None, after=None):
    mla_tab, dil_tab = tabs
    zp = sv["zp"]
    S = zp.shape[0]
    g = {}
    dm = _mm(dout, w["w_out"], mode="nt", name="d_merged", after=after)
    g["w_out"] = _mm(sv["merged"], dout, mode="tn", name="dw_out", out_dtype=BF16)
    dz = lax.empty((S, ZW), BF16)
    dP, db = [], []
    for j in range(3):
        dz, dpj, dbj = _merge_bwd(dm, sv["P"][j], zp, w["b_merge"][:, j * D:(j + 1) * D], dz, j)
        dP.append(dpj)
        db.append(dbj)
    g["b_merge"] = jnp.concatenate(db, axis=1)
    names = ("w_lru_o", "w_mla_o", "w_dil_o")
    dy = []
    for j in range(3):
        dy.append(_mm(dP[j], w[names[j]], mode="nt", name="dy_" + names[j]))
        g[names[j]] = _mm(sv["y"][j], dP[j], mode="tn", name="d" + names[j], out_dtype=BF16)
    dz = _lru_gate_bwd(zp, sv["hs"], dy[0], dz)
    dz, g["conv_w"], g["conv_b"], g["w_gx"], g["b_gx"], g["w_ga"], g["b_ga"], g["lam"] = _lru_bwd(zp, sv["hs"], dy[0], w, dz)
    dz, do, Dr = _mla_post_bwd(zp, sv["o_mla"], dy[1], dz)
    dq, dk, dv = _mla_attn_bwd(sv["q"], sv["k"], sv["v"], do, sv["lse"], Dr)
    dz, g["w_uq"], g["w_uk"], g["w_uv"], g["g_cq"], g["g_ckv"], g["g_mq"], g["g_mk"] = _mla_pre_bwd(zp, dq, dk, dv, w, mla_tab, dz)
    dz, dod, Dd = _dil_comb_bwd(zp, sv["oc"], dy[2], dz)
    dqs, dks, dvs = zip(*[_dil_attn_bwd(sv["qd"], sv["kd"], zp, dod, sv["L"], Dd, gi) for gi in range(len(DIL_DILATIONS))])
    dz, g["g_dq"] = _dil_pre_bwd(zp, dqs, w["g_dq"], dil_tab, dz, C_DQ, "dil_pre_bwd_q")
    dz, g["g_dk"] = _dil_pre_bwd(zp, dks, w["g_dk"], dil_tab, dz, C_DK, "dil_pre_bwd_k")
    dz = _dil_dv_into(dvs, dz)
    g["w_in"] = _mm(sv["h"], dz, mode="tn", name="dw_in", out_dtype=BF16)
    token = hook(g) if hook is not None else None
    dh = _mm(dz, w["w_in"], mode="nt", name="d_h", after=token)
    dx, g["norm_g"] = _rms_in_bwd(sv["x"], w["norm_g"], dh, dout)
    return dx, g


def _peers():
    mx, my, mc = lax.axis_index("x"), lax.axis_index("y"), lax.axis_index("c")
    me = 4 * mx + 2 * my + mc
    out = []
    for k in range(1, N_DEV):
        px = 1 - mx if k & 4 else mx
        py = 1 - my if k & 2 else my
        pc = 1 - mc if k & 1 else mc
        out.append(((px, py, pc), 4 * px + 2 * py + pc))
    return me, out


def _whole(ref, p):
    del p
    return ref


def _exchange(srcs, slicers, slices, name):
    n = len(srcs)

    def body(*refs):
        ins, outs = refs[:n], refs[n:2 * n]
        send_sems, recv_sems, local_sems = refs[2 * n:]
        me, peers = _peers()
        mine = [pltpu.make_async_copy(slicers[a](ins[a], me), outs[a].at[me], local_sems.at[a]) for a in range(n)]
        for cp in mine:
            cp.start()
        copies = []
        for k, (peer, pidx) in enumerate(peers):
            for a in range(n):
                cp = pltpu.make_async_remote_copy(
                    src_ref=slicers[a](ins[a], pidx), dst_ref=outs[a].at[me], send_sem=send_sems.at[k * n + a],
                    recv_sem=recv_sems.at[k * n + a], device_id=peer, device_id_type=pl.DeviceIdType.MESH)
                cp.start()
                copies.append(cp)
        for cp in copies + mine:
            cp.wait()

    nsem = (N_DEV - 1) * n
    return pl.pallas_call(
        body, name=name, out_shape=[jax.ShapeDtypeStruct((N_DEV,) + shp, dt) for shp, dt in slices],
        in_specs=[pl.BlockSpec(memory_space=pl.ANY)] * n, out_specs=[pl.BlockSpec(memory_space=pl.ANY)] * n,
        scratch_shapes=[pltpu.SemaphoreType.DMA((nsem,)), pltpu.SemaphoreType.DMA((nsem,)), pltpu.SemaphoreType.DMA((n,))],
        compiler_params=pltpu.CompilerParams(has_side_effects=True),
    )(*srcs)


def _gather_two_level(srcs, name):
    n = len(srcs)

    def body(*refs):
        ins, outs = refs[:n], refs[n:2 * n]
        send_sems, recv_sems, local_sems = refs[2 * n:]
        mx, my, mc = lax.axis_index("x"), lax.axis_index("y"), lax.axis_index("c")
        me, sibling = (mx, my, mc), (mx, my, 1 - mc)
        chips = [(1 - mx, my), (mx, 1 - my), (1 - mx, 1 - my)]
        slot = lambda d: 4 * d[0] + 2 * d[1] + d[2]

        def copy(j, a, block, to, own=False):
            return pltpu.make_async_remote_copy(
                src_ref=ins[a] if own else outs[a].at[slot(block)], dst_ref=outs[a].at[slot(block)],
                send_sem=send_sems.at[j * n + a], recv_sem=recv_sems.at[j * n + a], device_id=to, device_id_type=pl.DeviceIdType.MESH)

        mine = [pltpu.make_async_copy(ins[a], outs[a].at[slot(me)], local_sems.at[a]) for a in range(n)]
        first = [copy(1 + j, a, me, (*chip, mc), own=True) for j, chip in enumerate(chips) for a in range(n)]
        first += [copy(0, a, me, sibling, own=True) for a in range(n)]
        for cp in mine + first:
            cp.start()
        passed = []
        for j, chip in enumerate(chips):
            for a in range(n):
                copy(1 + j, a, (*chip, mc), me).wait_recv()
                cp = copy(4 + j, a, (*chip, mc), sibling)
                cp.start()
                passed.append(cp)
        for a in range(n):
            copy(0, a, sibling, me).wait_recv()
        for j, chip in enumerate(chips):
            for a in range(n):
                copy(4 + j, a, (*chip, 1 - mc), me).wait_recv()
        for cp in first + passed:
            cp.wait_send()
        for cp in mine:
            cp.wait()

    nsem = (N_DEV - 1) * n
    return pl.pallas_call(
        body, name=name, out_shape=[jax.ShapeDtypeStruct((N_DEV,) + a.shape, a.dtype) for a in srcs],
        in_specs=[pl.BlockSpec(memory_space=pl.ANY)] * n, out_specs=[pl.BlockSpec(memory_space=pl.ANY)] * n,
        scratch_shapes=[pltpu.SemaphoreType.DMA((nsem,)), pltpu.SemaphoreType.DMA((nsem,)), pltpu.SemaphoreType.DMA((n,))],
        compiler_params=pltpu.CompilerParams(has_side_effects=True),
    )(*srcs)


_HBM = pl.BlockSpec(memory_space=pltpu.HBM)
_SEM = pl.BlockSpec(memory_space=pltpu.SEMAPHORE)
_DATAFLOW = pltpu.SideEffectType.DATAFLOW_SIDE_EFFECTING


def _exchange_start(srcs, slicers, slices, after, name):
    n = len(srcs)
    nsem = (N_DEV - 1) * n
    lands = [lax.empty((N_DEV,) + shp, dt) for shp, dt in slices]

    def body(*refs):
        ins, lands_in = refs[:n], refs[n:2 * n]
        send_sems, recv_sems, local_sems = refs[2 * n + 1], refs[2 * n + 2], refs[2 * n + 3]
        token = refs[-1]
        me, peers = _peers()
        for a in range(n):
            pltpu.make_async_copy(slicers[a](ins[a], me), lands_in[a].at[me], local_sems.at[a]).start()
        for k, (peer, pidx) in enumerate(peers):
            for a in range(n):
                pltpu.make_async_remote_copy(
                    src_ref=slicers[a](ins[a], pidx), dst_ref=lands_in[a].at[me], send_sem=send_sems.at[k * n + a],
                    recv_sem=recv_sems.at[k * n + a], device_id=peer, device_id_type=pl.DeviceIdType.MESH).start()
        token[...] = jnp.zeros_like(token)

    hbm = lambda a: pltpu.with_memory_space_constraint(a, pltpu.HBM)
    return pl.pallas_call(
        body, name=name,
        out_shape=(pltpu.SemaphoreType.DMA((nsem,)), pltpu.SemaphoreType.DMA((nsem,)), pltpu.SemaphoreType.DMA((n,)),
                   *[pltpu.HBM(a.shape, a.dtype) for a in srcs], *[pltpu.HBM(a.shape, a.dtype) for a in lands],
                   jax.ShapeDtypeStruct((SUB, LANE), F32)),
        in_specs=[_HBM] * (2 * n) + [pl.BlockSpec(memory_space=pl.ANY)],
        out_specs=(_SEM, _SEM, _SEM, *[_HBM] * (2 * n), pl.BlockSpec(memory_space=pltpu.VMEM)),
        input_output_aliases={i: 3 + i for i in range(2 * n)},
        compiler_params=pltpu.CompilerParams(has_side_effects=_DATAFLOW),
    )(*[hbm(a) for a in srcs], *[hbm(a) for a in lands], after)


def _exchange_wait(started, slicers, after, name):
    n = (len(started) - 4) // 2
    sems, thru = started[0:3], started[3:3 + 2 * n]

    def body(*refs):
        srcs, lands = refs[:n], refs[n:2 * n]
        send_sems, recv_sems, local_sems = refs[2 * n], refs[2 * n + 1], refs[2 * n + 2]
        me, peers = _peers()
        for k, (peer, pidx) in enumerate(peers):
            for a in range(n):
                cp = pltpu.make_async_remote_copy(
                    src_ref=slicers[a](srcs[a], pidx), dst_ref=lands[a].at[me], send_sem=send_sems.at[k * n + a],
                    recv_sem=recv_sems.at[k * n + a], device_id=peer, device_id_type=pl.DeviceIdType.MESH)
                cp.wait_send()
                cp.wait_recv()
        for a in range(n):
            pltpu.make_async_copy(slicers[a](srcs[a], me), lands[a].at[me], local_sems.at[a]).wait()

    outs = pl.pallas_call(
        body, name=name, out_shape=[pltpu.HBM(a.shape, a.dtype) for a in thru],
        in_specs=[_HBM] * (2 * n) + [_SEM, _SEM, _SEM, pl.BlockSpec(memory_space=pl.ANY)], out_specs=[_HBM] * (2 * n),
        input_output_aliases={i: i for i in range(2 * n)}, compiler_params=pltpu.CompilerParams(has_side_effects=_DATAFLOW),
    )(*thru, *sems, after)
    return outs[n:]


WIN = 13 * LANE


def _win_base(s):
    n = s * SHARD_IN
    a0 = n + jnp.where(n >= _KR0, KR_LANE, 0) + jnp.where(n >= _KR0 + 32, 32, 0)
    return jnp.minimum(a0 // LANE, (ZW - WIN) // LANE)


def _win_offsets(s):
    n = s * SHARD_IN + jnp.arange(SHARD_IN)
    o = s * SHARD_IN - _win_base(s) * LANE
    return n, (o, o + KR_LANE, o + LANE - 32)


def _to_window(shard, s):
    _, offs = _win_offsets(s)
    padded = jnp.pad(shard, ((0, 0), (0, 0), (WIN, WIN)))
    a, b, c = [lax.dynamic_slice(padded, (0, 0, WIN - o), shard.shape[:2] + (WIN,)) for o in offs]
    col = (_win_base(s) * LANE + jnp.arange(WIN))[None, None, :]
    zero = jnp.zeros_like(a)
    return jnp.where(col < _KR0, a, jnp.where((col >= _KR0 + KR_LANE) & (col < _KR0 + KR_LANE + 32), b, jnp.where(col >= _KR0 + LANE, c, zero)))


def _from_window(win, s):
    n, offs = _win_offsets(s)
    a, b, c = [lax.dynamic_slice(win, (0, 0, o), win.shape[:2] + (SHARD_IN,)) for o in offs]
    return jnp.where((n < _KR0)[None, None, :], a, jnp.where((n < _KR0 + 32)[None, None, :], b, c))


def _win_base_static(s):
    n = s * SHARD_IN
    a0 = n + (KR_LANE if n >= _KR0 else 0) + (32 if n >= _KR0 + 32 else 0)
    return min(a0 // LANE, (ZW - WIN) // LANE)


def _assemble_w_in(gw):
    tr = 128
    bases = [_win_base_static(s) for s in range(N_DEV)]

    def body(g_ref, o_ref):
        for j in range(ZW // LANE):
            acc = None
            for s in range(N_DEV):
                if bases[s] <= j < bases[s] + WIN // LANE:
                    piece = g_ref[s, :, (j - bases[s]) * LANE:(j - bases[s] + 1) * LANE]
                    acc = piece if acc is None else acc + piece
            o_ref[:, j * LANE:(j + 1) * LANE] = acc

    return pl.pallas_call(
        body, name="assemble_w_in", grid=(D // tr,), in_specs=[pl.BlockSpec((N_DEV, tr, WIN), lambda i: (0, i, 0))],
        out_specs=pl.BlockSpec((tr, ZW), lambda i: (i, 0)), out_shape=jax.ShapeDtypeStruct((D, ZW), gw.dtype), compiler_params=_cp(),
    )(gw)


def _cols(width):
    return lambda ref, p: ref.at[:, pl.ds(pl.multiple_of(p * width, width), width)]


def _rows(height):
    return lambda ref, p: ref.at[pl.ds(pl.multiple_of(p * height, height), height), :]


SCATTER = {
    'w_in': (lambda ref, p: ref.at[:, pl.ds(pl.multiple_of(_win_base(p) * LANE, LANE), WIN)], (D, WIN), BF16),
    'conv_w': (_cols(LANE), (4, LANE), F32),
    'w_lru_o': (_rows(LANE), (LANE, D), BF16),
    'w_uq': (_cols(LANE), (256, LANE), F32),
    'w_ukv': (_cols(LANE), (128, LANE), F32),
    'w_mla_o': (_cols(LANE), (512, LANE), BF16),
    'w_dil_o': (_cols(LANE), (512, LANE), BF16),
    'w_out': (_rows(LANE), (LANE, D), BF16),
}


PACK_ROWS = 64


def _packed_rows(shapes):
    n = sum(int(np.prod(s)) for s in shapes)
    return -(-n // (PACK_ROWS * LANE)) * PACK_ROWS


def _sum8(buf, name):
    _, R, C = buf.shape
    tr = R
    while tr * C * 4 * N_DEV > (1 << 22) and tr % 16 == 0:
        tr //= 2

    def body(b_ref, o_ref):
        acc = b_ref[0].astype(F32)
        for s in range(1, N_DEV):
            acc = acc + b_ref[s].astype(F32)
        o_ref[...] = acc

    return pl.pallas_call(
        body, name=name, grid=(R // tr,), in_specs=[pl.BlockSpec((N_DEV, tr, C), lambda i: (0, i, 0))],
        out_specs=pl.BlockSpec((tr, C), lambda i: (i, 0)), out_shape=jax.ShapeDtypeStruct((R, C), F32), compiler_params=_cp(),
    )(buf)


def _pack(arrs, dtype, lead):
    flat = [a.astype(dtype).reshape(a.shape[:lead] + (-1,)) for a in arrs]
    cat = jnp.concatenate(flat, axis=-1)
    n = cat.shape[-1]
    unit = PACK_ROWS * LANE
    pad = (-n) % unit
    if pad:
        cat = jnp.pad(cat, [(0, 0)] * lead + [(0, pad)])
    return cat.reshape(cat.shape[:lead] + ((n + pad) // LANE, LANE))


def _unpack(buf, shapes, lead):
    flat = buf.reshape(buf.shape[:lead] + (-1,))
    out, off = [], 0
    for shp in shapes:
        n = int(np.prod(shp))
        out.append(flat[..., off:off + n].reshape(buf.shape[:lead] + tuple(shp)))
        off += n
    return out


def _adamw(w, g, m, v, name):
    rows, cols = w.shape
    tr = rows
    while tr * cols * 4 > (3 << 19) and tr % 16 == 0:
        tr //= 2
    c1 = 1.0 - ADAM_B1 ** ADAM_STEP
    c2 = 1.0 - ADAM_B2 ** ADAM_STEP

    def body(w_ref, g_ref, m_ref, v_ref, d_ref, mo_ref, vo_ref):
        gv = g_ref[...]
        mn = ADAM_B1 * m_ref[...] + (1.0 - ADAM_B1) * gv
        vn = ADAM_B2 * v_ref[...] + (1.0 - ADAM_B2) * (gv * gv)
        mo_ref[...] = mn
        vo_ref[...] = vn
        d_ref[...] = -ADAM_LR * ((mn / c1) / (jnp.sqrt(vn / c2) + ADAM_EPS) + ADAM_WD * w_ref[...])

    spec = pl.BlockSpec((tr, cols), lambda i: (i, 0))
    return pl.pallas_call(
        body, name=name, grid=(rows // tr,), in_specs=[spec] * 4, out_specs=[spec] * 3,
        out_shape=[jax.ShapeDtypeStruct((rows, cols), F32)] * 3, compiler_params=_cp(),
    )(w, g, m, v)


IN_NAMES = ['x', 'positions', 'norm_g', 'w_in', 'conv_w', 'conv_b', 'w_gate_x', 'b_gate_x', 'w_gate_a', 'b_gate_a', 'lru_lambda', 'w_lru_o',
            'cq_norm_g', 'ckv_norm_g', 'w_uq', 'w_ukv', 'mla_q_norm_g', 'mla_k_norm_g', 'w_mla_o', 'dil_q_norm_g', 'dil_k_norm_g', 'w_dil_o',
            'b_merge', 'w_out']
WEIGHTS = IN_NAMES[2:]
REPLICATED = [n for n in WEIGHTS if n not in SCATTER]
GATE_WEIGHTS = ('w_gate_x', 'w_gate_a')

_KR0 = C_KR * LANE


GATHERED = ['w_in', 'w_lru_o', 'w_uq', 'w_ukv', 'w_mla_o', 'w_dil_o', 'w_out', 'conv_w']


def _local_weights(wd, me):
    loc = {n: wd[n].astype(BF16) for n in GATHERED[:-1]}
    loc['w_in'] = _to_window(loc['w_in'], me)
    loc['w_uq'] = jnp.pad(loc['w_uq'], ((0, 0), (0, 0), (0, LANE - MLA_QK)))
    loc['conv_w'] = wd['conv_w']
    return [[loc[n][l] for n in GATHERED] for l in range(DEPTH)]


def _layer_weights(gathered, rep, l):
    gw = dict(zip(GATHERED, gathered))
    by_rows = lambda a: a.reshape(-1, a.shape[-1])
    by_cols = lambda a: jnp.swapaxes(a, 0, 1).reshape(a.shape[1], -1)
    ukv = jnp.swapaxes(gw['w_ukv'], 0, 1)
    g96 = lambda a: jnp.pad(a[l].reshape(1, MLA_QK), ((0, 0), (0, LANE - MLA_QK)))
    g64 = lambda a: jnp.tile(a[l].reshape(1, DIL_HD), (1, 2))
    return dict(
        norm_g=rep['norm_g'][l].reshape(1, D), w_in=_assemble_w_in(gw['w_in']),
        conv_w=by_cols(gw['conv_w']), conv_b=rep['conv_b'][l].reshape(1, D),
        w_gx=rep['w_gate_x'][l].astype(BF16), b_gx=rep['b_gate_x'][l].reshape(8, 1, LANE),
        w_ga=rep['w_gate_a'][l].astype(BF16), b_ga=rep['b_gate_a'][l].reshape(8, 1, LANE),
        lam=rep['lru_lambda'][l].reshape(1, D),
        w_lru_o=by_rows(gw['w_lru_o']), w_mla_o=by_cols(gw['w_mla_o']), w_dil_o=by_cols(gw['w_dil_o']), w_out=by_rows(gw['w_out']),
        g_cq=rep['cq_norm_g'][l].reshape(1, 256), g_ckv=rep['ckv_norm_g'][l].reshape(1, 128),
        w_uq=by_cols(gw['w_uq']), w_uk=jnp.pad(ukv[:, :, :64], ((0, 0), (0, 0), (0, 64))).reshape(128, 1024),
        w_uv=ukv[:, :, 64:].reshape(128, 512),
        g_mq=g96(rep['mla_q_norm_g']), g_mk=g96(rep['mla_k_norm_g']), g_dq=g64(rep['dil_q_norm_g']), g_dk=g64(rep['dil_k_norm_g']),
        b_merge=rep['b_merge'][l].reshape(1, 3 * D),
    )


def _sharded_grads(g):
    uk = g['w_uk'].reshape(128, 8, 128)[:, :, :64]
    uv = g['w_uv'].reshape(128, 8, 64)
    d = {'w_in': g['w_in'], 'conv_w': g['conv_w'], 'w_lru_o': g['w_lru_o'], 'w_uq': g['w_uq'],
         'w_ukv': jnp.concatenate([uk, uv], axis=-1).reshape(128, 1024), 'w_mla_o': g['w_mla_o'], 'w_dil_o': g['w_dil_o'],
         'w_out': g['w_out']}
    return [d[n] for n in SCATTER]


def _replicated_grads(g):
    return {
        'conv_b': g['conv_b'].reshape(D),
        'w_gate_x': g['w_gx'], 'b_gate_x': g['b_gx'].reshape(8, LANE), 'w_gate_a': g['w_ga'], 'b_gate_a': g['b_ga'].reshape(8, LANE),
        'lru_lambda': g['lam'].reshape(D), 'cq_norm_g': g['g_cq'].reshape(256), 'ckv_norm_g': g['g_ckv'].reshape(128),
        'mla_q_norm_g': g['g_mq'][0, :MLA_QK], 'mla_k_norm_g': g['g_mk'][0, :MLA_QK],
        'dil_q_norm_g': g['g_dq'][0, :DIL_HD] + g['g_dq'][0, DIL_HD:], 'dil_k_norm_g': g['g_dk'][0, :DIL_HD] + g['g_dk'][0, DIL_HD:],
        'b_merge': g['b_merge'].reshape(3 * D),
    }


def kernel(x, positions, norm_g, w_in, conv_w, conv_b, w_gate_x, b_gate_x, w_gate_a, b_gate_a, lru_lambda, w_lru_o, cq_norm_g, ckv_norm_g, w_uq, w_ukv, mla_q_norm_g, mla_k_norm_g, w_mla_o, dil_q_norm_g, dil_k_norm_g, w_dil_o, b_merge, w_out, loss_target, m_norm_g, m_w_in, m_conv_w, m_conv_b, m_w_gate_x, m_b_gate_x, m_w_gate_a, m_b_gate_a, m_lru_lambda, m_w_lru_o, m_cq_norm_g, m_ckv_norm_g, m_w_uq, m_w_ukv, m_mla_q_norm_g, m_mla_k_norm_g, m_w_mla_o, m_dil_q_norm_g, m_dil_k_norm_g, m_w_dil_o, m_b_merge, m_w_out, v_norm_g, v_w_in, v_conv_w, v_conv_b, v_w_gate_x, v_b_gate_x, v_w_gate_a, v_b_gate_a, v_lru_lambda, v_w_lru_o, v_cq_norm_g, v_ckv_norm_g, v_w_uq, v_w_ukv, v_mla_q_norm_g, v_mla_k_norm_g, v_w_mla_o, v_dil_q_norm_g, v_dil_k_norm_g, v_w_dil_o, v_b_merge, v_w_out):
    args = (x, positions, norm_g, w_in, conv_w, conv_b, w_gate_x, b_gate_x, w_gate_a, b_gate_a, lru_lambda, w_lru_o, cq_norm_g, ckv_norm_g, w_uq, w_ukv, mla_q_norm_g, mla_k_norm_g, w_mla_o, dil_q_norm_g, dil_k_norm_g, w_dil_o, b_merge, w_out)
    moments_m = (m_norm_g, m_w_in, m_conv_w, m_conv_b, m_w_gate_x, m_b_gate_x, m_w_gate_a, m_b_gate_a, m_lru_lambda, m_w_lru_o, m_cq_norm_g, m_ckv_norm_g, m_w_uq, m_w_ukv, m_mla_q_norm_g, m_mla_k_norm_g, m_w_mla_o, m_dil_q_norm_g, m_dil_k_norm_g, m_w_dil_o, m_b_merge, m_w_out)
    moments_v = (v_norm_g, v_w_in, v_conv_w, v_conv_b, v_w_gate_x, v_b_gate_x, v_w_gate_a, v_b_gate_a, v_lru_lambda, v_w_lru_o, v_cq_norm_g, v_ckv_norm_g, v_w_uq, v_w_ukv, v_mla_q_norm_g, v_mla_k_norm_g, v_w_mla_o, v_dil_q_norm_g, v_dil_k_norm_g, v_w_dil_o, v_b_merge, v_w_out)
    a = dict(zip(IN_NAMES, args))
    wd = {n: a[n] for n in WEIGHTS}
    md = dict(zip(WEIGHTS, moments_m))
    vd = dict(zip(WEIGHTS, moments_v))

    me = 4 * lax.axis_index("x") + 2 * lax.axis_index("y") + lax.axis_index("c")

    assert DEPTH == 2
    xs, tabs = x[0], _rope_tables(positions[0])
    whole = [_whole] * len(GATHERED)
    slicers = [SCATTER[n][0] for n in SCATTER]
    grad_slices = [SCATTER[n][1:] for n in SCATTER]

    local = _local_weights(wd, me)
    w_slices = [(a.shape, a.dtype) for a in local[0]]
    landed0 = _gather_two_level(local[0], "gather_w0")
    flying = _exchange_start(local[1], whole, w_slices, landed0[0], "gather_w1_start")
    rep0 = dict(wd, norm_g=wd['norm_g'] + flying[-1][0, 0])
    w0 = _layer_weights(landed0, rep0, 0)
    x1, saved0 = _layer_fwd(xs, w0, tabs)
    w1 = _layer_weights(_exchange_wait(flying, whole, x1, "gather_w1_wait"), wd, 1)
    x2, saved1 = _layer_fwd(x1, w1, tabs)
    loss, dx2 = _loss_fwd_bwd(x2, loss_target[0])
    loss = loss[0, 0]

    sharded = list(SCATTER)
    nsh = len(sharded)
    small = [n for n in REPLICATED if n not in GATE_WEIGHTS and n != 'norm_g']

    def outgoing(g):
        r = _replicated_grads(g)
        return (_sharded_grads(g) + [_pack([r[n] for n in small], F32, 0)]
                + [r[n].astype(BF16).reshape(8 * LANE, LANE) for n in GATE_WEIGHTS])

    out_slicers = slicers + [_whole] * 3
    out_slices = grad_slices + [((_packed_rows([wd[n].shape[1:] for n in small]), LANE), F32)] + [((8 * LANE, LANE), BF16)] * 2
    dx1, g1 = _layer_bwd(dx2, w1, tabs, saved1)
    flying1 = _exchange_start(outgoing(g1), out_slicers, out_slices, dx1, "scatter_g1_start")
    later = {}

    def send_layer0(g):
        later['got1'] = _exchange_wait(flying1, out_slicers, g['w_in'], "scatter_g1_wait")
        later['flying0'] = _exchange_start(outgoing(g), out_slicers, out_slices, later['got1'][0], "scatter_g0_start")
        return later['flying0'][-1]

    grad_x, g0 = _layer_bwd(dx1, w0, tabs, saved0, hook=send_layer0, after=flying1[-1])
    names = sharded + ['small'] + list(GATE_WEIGHTS)
    sum1 = [_sum8(b, f"sum_{n}_1") for n, b in zip(names, later['got1'])]
    got0 = _exchange_wait(later['flying0'], out_slicers, sum1[0], "scatter_g0_wait")
    sum0 = [_sum8(b, f"sum_{n}_0") for n, b in zip(names, got0)]
    norm_part = _pack([jnp.stack([g['norm_g'].reshape(D) for g in (g0, g1)])], F32, 0)
    norm_sum = _sum8(_exchange([norm_part], [_whole], [(norm_part.shape, F32)], "gather_norm_g")[0], "sum_norm_g")

    gsh = {n: jnp.stack([sum0[i], sum1[i]]) for i, n in enumerate(sharded)}
    gsh['w_in'] = _from_window(gsh['w_in'], me)
    gsh['w_uq'] = gsh['w_uq'][:, :, :MLA_QK]
    grep = {'norm_g': _unpack(norm_sum, [wd['norm_g'].shape], 0)[0]}
    per_layer = [_unpack(s[nsh], [wd[n].shape[1:] for n in small], 0) for s in (sum0, sum1)]
    grep.update({n: jnp.stack([per_layer[l][i] for l in range(DEPTH)]) for i, n in enumerate(small)})
    for i, n in enumerate(GATE_WEIGHTS):
        grep[n] = jnp.stack([sum0[nsh + 1 + i], sum1[nsh + 1 + i]]).reshape(wd[n].shape)

    out_g, out_d, out_m, out_v = {}, {}, {}, {}
    vecs = ['norm_g'] + small
    vshapes = [wd[n].shape for n in vecs]
    packed_g = _pack([grep[n] for n in vecs], F32, 0)
    d_, m_, v_ = _adamw(_pack([wd[n] for n in vecs], F32, 0), packed_g, _pack([md[n] for n in vecs], F32, 0),
                        _pack([vd[n] for n in vecs], F32, 0), "adamw_vectors")
    for dst, buf in ((out_d, d_), (out_m, m_), (out_v, v_)):
        dst.update(zip(vecs, _unpack(buf, vshapes, 0)))
    out_g.update({n: grep[n] for n in vecs})
    gsh.update({n: grep[n] for n in GATE_WEIGHTS})
    for n in sharded + list(GATE_WEIGHTS):
        if n in GATE_WEIGHTS:
            shp = wd[n].shape
            two = (shp[0] * shp[1] * shp[2], shp[3])
            d_, m_, v_ = _adamw(wd[n].reshape(two), gsh[n].reshape(two), md[n].reshape(two), vd[n].reshape(two), "adamw_" + n)
            out_g[n], out_d[n], out_m[n], out_v[n] = gsh[n], d_.reshape(shp), m_.reshape(shp), v_.reshape(shp)
            continue
        shp = wd[n].shape
        two = (shp[0] * shp[1], shp[2])
        d_, m_, v_ = _adamw(wd[n].reshape(two), gsh[n].reshape(two), md[n].reshape(two), vd[n].reshape(two), "adamw_" + n)
        out_g[n], out_d[n], out_m[n], out_v[n] = gsh[n], d_.reshape(shp), m_.reshape(shp), v_.reshape(shp)

    loss = lax.psum(loss, ("x", "y", "c"))
    return (loss, grad_x[None], *[out_g[n] for n in WEIGHTS], *[out_d[n] for n in WEIGHTS], *[out_m[n] for n in WEIGHTS],
            *[out_v[n] for n in WEIGHTS])
```

```python
import functools

import numpy as np
import jax
import jax.numpy as jnp
from jax import lax
from jax.experimental import pallas as pl
from jax.experimental.pallas import tpu as pltpu

F32 = jnp.float32
BF16 = jnp.bfloat16

N_DEV = 8
D = 1024
DEPTH = 2
EPS = 1e-6
ROPE_THETA = 10000.0
LRU_C = 8.0
LANE = 128
SUB = 8
IN_WIDTH = 11168
SHARD_IN = IN_WIDTH // N_DEV

C_LRUX, C_LRUG, C_CQ, C_CKV, C_KR, C_MLAG, C_DQ, C_DK, C_DV, C_DILG, C_MERGE = 0, 8, 16, 18, 19, 20, 24, 36, 48, 60, 64
ZW = 88 * LANE
KR_LANE = 64

MLA_QK = 96
MLA_SCALE = MLA_QK ** -0.5
DIL_HD = 64
DIL_SCALE = DIL_HD ** -0.5
DIL_DILATIONS = (1, 4, 16)
NK = 128

ADAM_LR, ADAM_B1, ADAM_B2, ADAM_EPS, ADAM_WD, ADAM_STEP = 0.001, 0.9, 0.999, 1e-08, 0.01, 10

NEG = -1e30
VMEM_LIMIT = 48 * 1024 * 1024


def _cp(**kw):
    return pltpu.CompilerParams(vmem_limit_bytes=VMEM_LIMIT, **kw)


def _sig(x):
    return 1.0 / (1.0 + jnp.exp(-x))


def _silu(x):
    return x * _sig(x)


def _dsilu(x):
    s = _sig(x)
    return s * (1.0 + x * (1.0 - s))


def _dot(a, b, dims):
    return lax.dot_general(a, b, (dims, ((), ())), preferred_element_type=F32)


def _nn(a, b):
    return _dot(a, b, ((1,), (0,)))


def _nt(a, b):
    return _dot(a, b, ((1,), (1,)))


def _tn(a, b):
    return _dot(a, b, ((0,), (0,)))


def _rsum(x):
    return jnp.sum(x, axis=-1, keepdims=True)


def _csum(x):
    return jnp.sum(x, axis=0, keepdims=True)


def _mm(a, b, *, mode, name, out_dtype=F32, add=None, after=None, tm=1024, tn=1024, tk=1024):
    if mode == "nn":
        (M, K), (K2, N) = a.shape, b.shape
    elif mode == "nt":
        (M, K), (N, K2) = a.shape, b.shape
    else:
        (K, M), (K2, N) = a.shape, b.shape
    assert K == K2
    tm, tn, tk = min(tm, M), min(tn, N), min(tk, K)
    assert M % tm == 0 and N % tn == 0 and K % tk == 0
    nk = K // tk
    fn = {"nn": _nn, "nt": _nt, "tn": _tn}[mode]
    has_add = add is not None

    def body(*refs):
        a_ref, b_ref = refs[0], refs[1]
        add_ref = refs[2] if has_add else None
        o_ref = refs[2 + has_add + (after is not None)]
        part = fn(a_ref[...].astype(BF16), b_ref[...].astype(BF16))

        def fin(acc):
            if has_add:
                acc = acc + add_ref[...]
            o_ref[...] = acc.astype(out_dtype)

        if nk == 1:
            fin(part)
        else:
            acc_ref = refs[-1]
            k = pl.program_id(2)

            @pl.when(k == 0)
            def _():
                acc_ref[...] = part

            @pl.when(k > 0)
            def _():
                acc_ref[...] += part

            @pl.when(k == nk - 1)
            def _():
                fin(acc_ref[...])

    a_spec = pl.BlockSpec((tk, tm), lambda i, j, k: (k, i)) if mode == "tn" else pl.BlockSpec((tm, tk), lambda i, j, k: (i, k))
    b_spec = pl.BlockSpec((tn, tk), lambda i, j, k: (j, k)) if mode == "nt" else pl.BlockSpec((tk, tn), lambda i, j, k: (k, j))
    o_spec = pl.BlockSpec((tm, tn), lambda i, j, k: (i, j))
    in_specs, args = [a_spec, b_spec], [a, b]
    if has_add:
        in_specs.append(o_spec)
        args.append(add)
    if after is not None:
        in_specs.append(pl.BlockSpec(memory_space=pl.ANY))
        args.append(after)
    return pl.pallas_call(
        body, name=name, grid=(M // tm, N // tn, nk), in_specs=in_specs, out_specs=o_spec,
        out_shape=jax.ShapeDtypeStruct((M, N), out_dtype),
        scratch_shapes=[pltpu.VMEM((tm, tn), F32)] if nk > 1 else [],
        compiler_params=_cp(dimension_semantics=("parallel", "parallel", "arbitrary")),
    )(*args)


T_ROW = 512


def _rms_in_fwd(x, g):
    S = x.shape[0]
    T = T_ROW

    def body(x_ref, g_ref, h_ref):
        xv = x_ref[...]
        r = lax.rsqrt(jnp.mean(xv * xv, axis=-1, keepdims=True) + EPS)
        h_ref[...] = (xv * r * g_ref[...]).astype(BF16)

    return pl.pallas_call(
        body, name="rms_in_fwd", grid=(S // T,),
        in_specs=[pl.BlockSpec((T, D), lambda i: (i, 0)), pl.BlockSpec((1, D), lambda i: (0, 0))],
        out_specs=pl.BlockSpec((T, D), lambda i: (i, 0)),
        out_shape=jax.ShapeDtypeStruct((S, D), BF16), compiler_params=_cp(),
    )(x, g)


def _rms_in_bwd(x, g, dh, dres):
    S = x.shape[0]
    T = T_ROW

    def body(x_ref, g_ref, dh_ref, dr_ref, dx_ref, dg_ref):
        i = pl.program_id(0)
        xv = x_ref[...]
        r = lax.rsqrt(jnp.mean(xv * xv, axis=-1, keepdims=True) + EPS)
        xn = xv * r
        dy = dh_ref[...]
        part = _csum(dy * xn)

        @pl.when(i == 0)
        def _():
            dg_ref[...] = part

        @pl.when(i > 0)
        def _():
            dg_ref[...] += part

        dxh = dy * g_ref[...]
        dx_ref[...] = dr_ref[...] + r * (dxh - xn * jnp.mean(dxh * xn, axis=-1, keepdims=True))

    row = pl.BlockSpec((T, D), lambda i: (i, 0))
    vec = pl.BlockSpec((1, D), lambda i: (0, 0))
    return pl.pallas_call(
        body, name="rms_in_bwd", grid=(S // T,), in_specs=[row, vec, row, row], out_specs=[row, vec],
        out_shape=[jax.ShapeDtypeStruct((S, D), F32), jax.ShapeDtypeStruct((1, D), F32)], compiler_params=_cp(),
    )(x, g, dh, dres)


T_LRU = 512


def _neg_expm1(y):
    ser = -y * (1.0 + y * 0.5 * (1.0 + y * (1.0 / 3.0) * (1.0 + y * 0.25 * (1.0 + y * 0.2))))
    return jnp.where(y > -0.03, ser, 1.0 - jnp.exp(y))


def _softplus_neg(lam):
    e = jnp.exp(-jnp.abs(lam))
    l1p = jnp.where(e < 0.01, e * (1.0 - e * (0.5 - e * (1.0 / 3.0 - e * 0.25))), jnp.log(1.0 + e))
    return jnp.maximum(-lam, 0.0) + l1p


def _scan_fwd(a, b, T):
    row = lax.broadcasted_iota(jnp.int32, a.shape, 0)
    d = 1
    while d < T:
        m = row >= d
        b = jnp.where(m, a * pltpu.roll(b, d, 0) + b, b)
        a = jnp.where(m, a * pltpu.roll(a, d, 0), a)
        d *= 2
    return a, b


def _scan_bwd(a, b, T):
    row = lax.broadcasted_iota(jnp.int32, a.shape, 0)
    d = 1
    while d < T:
        m = row < T - d
        b = jnp.where(m, a * pltpu.roll(b, T - d, 0) + b, b)
        a = jnp.where(m, a * pltpu.roll(a, T - d, 0), a)
        d *= 2
    return b


def _lru_common(x, prev, first, cw_ref, cb_ref, wgx_ref, bgx_ref, wga_ref, bga_ref, lam_ref, T):
    row = lax.broadcasted_iota(jnp.int32, x.shape, 0)
    prev = jnp.where(first, 0.0, prev)
    xs = []
    for j in (3, 2, 1):
        pv = jnp.tile(pltpu.roll(prev, j, 0), (T // SUB, 1))
        xs.append(jnp.where(row < j, pv, pltpu.roll(x, j, 0)))
    xs.append(x)
    xc = cb_ref[...] + cw_ref[0:1, :] * xs[0] + cw_ref[1:2, :] * xs[1] + cw_ref[2:3, :] * xs[2] + cw_ref[3:4, :] * xs[3]
    xcb = xc.astype(BF16)
    gx = _sig(_nn(xcb, wgx_ref[0]) + bgx_ref[0])
    ga = _sig(_nn(xcb, wga_ref[0]) + bga_ref[0])
    sp = _softplus_neg(lam_ref[...])
    log_a = -LRU_C * ga * sp
    a = jnp.exp(log_a)
    mult = jnp.sqrt(_neg_expm1(2.0 * log_a))
    return xs, xc, xcb, gx, ga, sp, a, mult


def _lru_specs(T, tmap):
    def at(col0):
        return pl.BlockSpec((T, LANE), lambda n, i: (tmap(i), col0 + n))

    def prev(col0):
        return pl.BlockSpec((SUB, LANE), lambda n, i: (jnp.maximum(tmap(i) * (T // SUB) - 1, 0), col0 + n))

    small = [
        pl.BlockSpec((4, LANE), lambda n, i: (0, n)),
        pl.BlockSpec((1, LANE), lambda n, i: (0, n)),
        pl.BlockSpec((1, LANE, LANE), lambda n, i: (n, 0, 0)),
        pl.BlockSpec((1, 1, LANE), lambda n, i: (n, 0, 0)),
        pl.BlockSpec((1, LANE, LANE), lambda n, i: (n, 0, 0)),
        pl.BlockSpec((1, 1, LANE), lambda n, i: (n, 0, 0)),
        pl.BlockSpec((1, LANE), lambda n, i: (0, n)),
    ]
    return at, prev, small


def _lru_fwd(zp, w):
    S = zp.shape[0]
    T = T_LRU
    at, prev, small = _lru_specs(T, lambda i: i)

    def body(x_ref, xp_ref, g_ref, cw_ref, cb_ref, wgx_ref, bgx_ref, wga_ref, bga_ref, lam_ref, hs_ref, y_ref, carry_ref):
        i = pl.program_id(1)

        @pl.when(i == 0)
        def _():
            carry_ref[...] = jnp.zeros_like(carry_ref)

        x = x_ref[...]
        _, xc, _, gx, _, _, a, mult = _lru_common(x, xp_ref[...], i == 0, cw_ref, cb_ref, wgx_ref, bgx_ref, wga_ref, bga_ref, lam_ref, T)
        A, B = _scan_fwd(a, mult * gx * xc, T)
        h = B + A * carry_ref[SUB - 1:SUB, :]
        hs_ref[...] = h
        carry_ref[...] = hs_ref[T - SUB:T, :]
        y_ref[...] = (h * _silu(g_ref[...])).astype(BF16)

    out = pl.BlockSpec((T, LANE), lambda n, i: (i, n))
    return pl.pallas_call(
        body, name="lru_fwd", grid=(8, S // T),
        in_specs=[at(C_LRUX), prev(C_LRUX), at(C_LRUG)] + small, out_specs=[out, out],
        out_shape=[jax.ShapeDtypeStruct((S, D), F32), jax.ShapeDtypeStruct((S, D), BF16)],
        scratch_shapes=[pltpu.VMEM((SUB, LANE), F32)],
        compiler_params=_cp(dimension_semantics=("parallel", "arbitrary")),
    )(zp, zp, zp, w["conv_w"], w["conv_b"], w["w_gx"], w["b_gx"], w["w_ga"], w["b_ga"], w["lam"])


def _lru_bwd(zp, hs, dy, w, dz):
    S = zp.shape[0]
    T = T_LRU
    nT = S // T
    at, prev, small = _lru_specs(T, lambda i: nT - 1 - i)

    def body(x_ref, xp_ref, g_ref, h_ref, hp_ref, dy_ref, cw_ref, cb_ref, wgx_ref, bgx_ref, wga_ref, bga_ref, lam_ref, dz_in,
             dzx_ref, dcw_ref, dcb_ref, dwgx_ref, dbgx_ref, dwga_ref, dbga_ref, dlam_ref, carry_ref, head_ref):
        del dz_in
        j = pl.program_id(1)
        it = nT - 1 - j

        @pl.when(j == 0)
        def _():
            for r in (carry_ref, head_ref, dcw_ref, dcb_ref, dwgx_ref, dbgx_ref, dwga_ref, dbga_ref, dlam_ref):
                r[...] = jnp.zeros_like(r)

        first = it == 0
        x = x_ref[...]
        xs, xc, xcb, gx, ga, sp, a, mult = _lru_common(x, xp_ref[...], first, cw_ref, cb_ref, wgx_ref, bgx_ref, wga_ref, bga_ref, lam_ref, T)
        row = lax.broadcasted_iota(jnp.int32, x.shape, 0)
        u = gx * xc
        h = h_ref[...]
        hp = jnp.where(first, 0.0, hp_ref[...])
        hm1 = jnp.where(row < 1, jnp.tile(pltpu.roll(hp, 1, 0), (T // SUB, 1)), pltpu.roll(h, 1, 0))
        dho = dy_ref[...] * _silu(g_ref[...])
        gin = jnp.where(row == T - 1, dho + carry_ref[0:1, :], dho)
        abar = jnp.where(row == T - 1, 0.0, pltpu.roll(a, T - 1, 0))
        dh = _scan_bwd(abar, gin, T)
        carry_ref[...] = (a * dh)[0:SUB, :]
        da = dh * hm1
        dmult = dh * u
        du = dh * mult
        dgx = du * xc
        dxc = du * gx
        dlog_a = da * a - dmult * a * a / mult
        dga = dlog_a * (-LRU_C * sp)
        lam = lam_ref[...]
        dlam_ref[...] += _csum(dlog_a * (-LRU_C * ga)) * (-1.0 / (1.0 + jnp.exp(lam)))
        dpa = dga * ga * (1.0 - ga)
        dpx = dgx * gx * (1.0 - gx)
        dpab, dpxb = dpa.astype(BF16), dpx.astype(BF16)
        dxc = dxc + _nt(dpxb, wgx_ref[0]) + _nt(dpab, wga_ref[0])
        dwgx_ref[0] += _tn(xcb, dpxb)
        dwga_ref[0] += _tn(xcb, dpab)
        dbgx_ref[0] += _csum(dpx)
        dbga_ref[0] += _csum(dpa)
        dcb_ref[...] += _csum(dxc)
        for k in range(4):
            dcw_ref[k:k + 1, :] += _csum(dxc * xs[k])
        head = head_ref[...]
        dx = cw_ref[3:4, :] * dxc
        for jj in (1, 2, 3):
            hv = jnp.tile(pltpu.roll(head, SUB - jj, 0), (T // SUB, 1))
            dx = dx + cw_ref[3 - jj:4 - jj, :] * jnp.where(row >= T - jj, hv, pltpu.roll(dxc, T - jj, 0))
        head_ref[...] = dxc[0:SUB, :]
        dzx_ref[...] = dx.astype(BF16)

    def acc(shape, imap):
        return pl.BlockSpec(shape, imap)

    out_specs = [
        pl.BlockSpec((T, LANE), lambda n, i: (nT - 1 - i, C_LRUX + n)),
        acc((4, LANE), lambda n, i: (0, n)), acc((1, LANE), lambda n, i: (0, n)),
        acc((1, LANE, LANE), lambda n, i: (n, 0, 0)), acc((1, 1, LANE), lambda n, i: (n, 0, 0)),
        acc((1, LANE, LANE), lambda n, i: (n, 0, 0)), acc((1, 1, LANE), lambda n, i: (n, 0, 0)),
        acc((1, LANE), lambda n, i: (0, n)),
    ]
    out_shape = [
        jax.ShapeDtypeStruct(dz.shape, BF16),
        jax.ShapeDtypeStruct((4, D), F32), jax.ShapeDtypeStruct((1, D), F32),
        jax.ShapeDtypeStruct((8, LANE, LANE), F32), jax.ShapeDtypeStruct((8, 1, LANE), F32),
        jax.ShapeDtypeStruct((8, LANE, LANE), F32), jax.ShapeDtypeStruct((8, 1, LANE), F32),
        jax.ShapeDtypeStruct((1, D), F32),
    ]
    dyspec = pl.BlockSpec((T, LANE), lambda n, i: (nT - 1 - i, n))
    hprev = pl.BlockSpec((SUB, LANE), lambda n, i: (jnp.maximum((nT - 1 - i) * (T // SUB) - 1, 0), n))
    return pl.pallas_call(
        body, name="lru_bwd", grid=(8, nT),
        in_specs=[at(C_LRUX), prev(C_LRUX), at(C_LRUG), dyspec, hprev, dyspec] + small + [pl.BlockSpec(memory_space=pl.ANY)],
        out_specs=out_specs, out_shape=out_shape,
        scratch_shapes=[pltpu.VMEM((SUB, LANE), F32), pltpu.VMEM((SUB, LANE), F32)],
        input_output_aliases={13: 0},
        compiler_params=_cp(dimension_semantics=("parallel", "arbitrary")),
    )(zp, zp, zp, hs, hs, dy, w["conv_w"], w["conv_b"], w["w_gx"], w["b_gx"], w["w_ga"], w["b_ga"], w["lam"], dz)


def _lru_gate_bwd(zp, hs, dy, dz):
    S = zp.shape[0]
    T = T_ROW

    def body(g_ref, h_ref, dy_ref, dz_in, o_ref):
        del dz_in
        o_ref[...] = (dy_ref[...] * h_ref[...] * _dsilu(g_ref[...])).astype(BF16)

    row = pl.BlockSpec((T, D), lambda i: (i, 0))
    zc = pl.BlockSpec((T, D), lambda i: (i, C_LRUG // 8))
    return pl.pallas_call(
        body, name="lru_gate_bwd", grid=(S // T,), in_specs=[zc, row, row, pl.BlockSpec(memory_space=pl.ANY)], out_specs=zc,
        out_shape=jax.ShapeDtypeStruct(dz.shape, BF16), input_output_aliases={3: 0}, compiler_params=_cp(),
    )(zp, hs, dy, dz)


def _rope_tables(pos):
    pf = pos.astype(F32)[:, None]

    def cs(d):
        inv = ROPE_THETA ** (-jnp.arange(0, d, 2, dtype=F32) / d)
        ang = pf * inv
        return jnp.cos(ang), jnp.sin(ang)

    S = pos.shape[0]
    c, s = cs(32)
    one, zero = jnp.ones((S, 64), F32), jnp.zeros((S, 16), F32)
    z32, z64 = jnp.zeros((S, 32), F32), jnp.zeros((S, 64), F32)
    mla = (jnp.concatenate([one, c, c, jnp.ones((S, 32), F32)], 1),
           jnp.concatenate([z64, zero, s, z32], 1),
           jnp.concatenate([z64, -s, zero, z32], 1))
    c, s = cs(64)
    dil = (jnp.concatenate([c, c, c, c], 1),
           jnp.concatenate([z32, s, z32, s], 1),
           jnp.concatenate([-s, z32, -s, z32], 1))
    return mla, dil


def _rope(x, C, S1, S2, sh):
    return x * C + pltpu.roll(x, sh, 1) * S1 + pltpu.roll(x, LANE - sh, 1) * S2


def _rope_t(dy, C, S1, S2, sh):
    return dy * C + pltpu.roll(dy * S1, LANE - sh, 1) + pltpu.roll(dy * S2, sh, 1)


def _lane(shape):
    return lax.broadcasted_iota(jnp.int32, shape, 1)


T_MLA = 256
TA = 512


def _zcol(T, width, col_lanes):
    assert (col_lanes * LANE) % width == 0
    return pl.BlockSpec((T, width), lambda i: (i, col_lanes * LANE // width))


def _full(shape):
    return pl.BlockSpec(shape, lambda *_: (0,) * len(shape))


def _mla_pre_fwd(zp, w, tab):
    S = zp.shape[0]
    T = T_MLA

    def body(cq_ref, ckv_ref, kr_ref, gcq_ref, gckv_ref, wuq_ref, wuk_ref, wuv_ref, gq_ref, gk_ref, C_ref, S1_ref, S2_ref,
             q_ref, k_ref, v_ref):
        cq = cq_ref[...]
        cqn = (cq * lax.rsqrt(jnp.mean(cq * cq, axis=-1, keepdims=True) + EPS) * gcq_ref[...]).astype(BF16)
        ckv = ckv_ref[...]
        ckvn = (ckv * lax.rsqrt(jnp.mean(ckv * ckv, axis=-1, keepdims=True) + EPS) * gckv_ref[...]).astype(BF16)
        q0 = _nn(cqn, wuq_ref[...])
        k0 = _nn(ckvn, wuk_ref[...])
        krb = kr_ref[...]
        C, S1, S2 = C_ref[...], S1_ref[...], S2_ref[...]
        for h in range(8):
            sl = slice(h * LANE, (h + 1) * LANE)
            xq = q0[:, sl]
            xq = xq * lax.rsqrt(_rsum(xq * xq) * (1.0 / MLA_QK) + EPS) * gq_ref[...]
            q_ref[:, sl] = _rope(xq, C, S1, S2, 16).astype(BF16)
            xk = k0[:, sl] + krb
            xk = xk * lax.rsqrt(_rsum(xk * xk) * (1.0 / MLA_QK) + EPS) * gk_ref[...]
            k_ref[:, sl] = _rope(xk, C, S1, S2, 16).astype(BF16)
        v_ref[...] = _nn(ckvn, wuv_ref[...]).astype(BF16)

    tabspec = pl.BlockSpec((T, LANE), lambda i: (i, 0))
    in_specs = [_zcol(T, 256, C_CQ), _zcol(T, LANE, C_CKV), _zcol(T, LANE, C_KR), _full((1, 256)), _full((1, LANE)),
                _full((256, 1024)), _full((LANE, 1024)), _full((LANE, 512)), _full((1, LANE)), _full((1, LANE)),
                tabspec, tabspec, tabspec]
    return pl.pallas_call(
        body, name="mla_pre_fwd", grid=(S // T,), in_specs=in_specs,
        out_specs=[pl.BlockSpec((T, 1024), lambda i: (i, 0)), pl.BlockSpec((T, 1024), lambda i: (i, 0)), pl.BlockSpec((T, 512), lambda i: (i, 0))],
        out_shape=[jax.ShapeDtypeStruct((S, 1024), BF16), jax.ShapeDtypeStruct((S, 1024), BF16), jax.ShapeDtypeStruct((S, 512), BF16)],
        compiler_params=_cp(),
    )(zp, zp, zp, w["g_cq"], w["g_ckv"], w["w_uq"], w["w_uk"], w["w_uv"], w["g_mq"], w["g_mk"], *tab)


def _mla_attn_fwd(q, k, v, zp):
    S = q.shape[0]
    nq = S // TA

    def body(q_ref, k_ref, v_ref, g_ref, o_ref, lse_ref, y_ref):
        qi = pl.program_id(1)
        lane = _lane((TA, LANE))
        rowi = lax.broadcasted_iota(jnp.int32, (TA, TA), 0)
        coli = lax.broadcasted_iota(jnp.int32, (TA, TA), 1)
        o_tot = jnp.zeros((TA, LANE), F32)
        for hh in range(2):
            cs = slice(hh * LANE, (hh + 1) * LANE)
            hm = (lane < 64) if hh == 0 else (lane >= 64)
            qh = q_ref[:, cs]

            def step(kb, carry, masked, cs=cs, hm=hm, qh=qh):
                m, l, acc = carry
                off = pl.multiple_of(kb * TA, TA)
                kh = k_ref[pl.ds(off, TA), cs]
                vv = v_ref[pl.ds(off, TA), :]
                vh = jnp.where(hm, vv, jnp.zeros_like(vv))
                s = _nt(qh, kh) * MLA_SCALE
                if masked:
                    s = jnp.where(rowi >= coli, s, NEG)
                m_new = jnp.maximum(m, jnp.max(s, axis=-1, keepdims=True))
                alpha = jnp.exp(m - m_new)
                p = jnp.exp(s - m_new)
                l = alpha * l + _rsum(p)
                acc = alpha * acc + _nn(p.astype(BF16), vh)
                return m_new, l, acc

            init = (jnp.full((TA, 1), NEG, F32), jnp.zeros((TA, 1), F32), jnp.zeros((TA, LANE), F32))
            carry = lax.fori_loop(0, qi, lambda kb, c: step(kb, c, False), init)
            m, l, acc = step(qi, carry, True)
            o_tot = o_tot + acc / l
            lse_ref[:, cs] = jnp.broadcast_to(m + jnp.log(l), (TA, LANE))
        o_ref[...] = o_tot
        y_ref[...] = (o_tot * _silu(g_ref[...])).astype(BF16)

    blk = pl.BlockSpec((TA, LANE), lambda p, i: (i, p))
    return pl.pallas_call(
        body, name="mla_attn_fwd", grid=(4, nq),
        in_specs=[pl.BlockSpec((TA, 256), lambda p, i: (i, p)), pl.BlockSpec((S, 256), lambda p, i: (0, p)),
                  pl.BlockSpec((S, LANE), lambda p, i: (0, p)), pl.BlockSpec((TA, LANE), lambda p, i: (i, C_MLAG + p))],
        out_specs=[blk, pl.BlockSpec((TA, 256), lambda p, i: (i, p)), blk],
        out_shape=[jax.ShapeDtypeStruct((S, 512), F32), jax.ShapeDtypeStruct((S, 1024), F32), jax.ShapeDtypeStruct((S, 512), BF16)],
        compiler_params=_cp(dimension_semantics=("parallel", "arbitrary")),
    )(q, k, v, zp)


def _mla_post_bwd(zp, o, dy, dz):
    S = zp.shape[0]
    T = T_ROW

    def body(g_ref, o_ref, dy_ref, dz_in, dz_ref, do_ref, D_ref):
        del dz_in
        g, o_, dy_ = g_ref[...], o_ref[...], dy_ref[...]
        do = dy_ * _silu(g)
        do_ref[...] = do.astype(BF16)
        dz_ref[...] = (dy_ * o_ * _dsilu(g)).astype(BF16)
        prod = do * o_
        lane = _lane((T, LANE))
        for p in range(4):
            pr = prod[:, p * LANE:(p + 1) * LANE]
            da = _rsum(jnp.where(lane < 64, pr, 0.0))
            db = _rsum(jnp.where(lane >= 64, pr, 0.0))
            D_ref[:, 2 * p * LANE:(2 * p + 1) * LANE] = jnp.broadcast_to(da, (T, LANE))
            D_ref[:, (2 * p + 1) * LANE:(2 * p + 2) * LANE] = jnp.broadcast_to(db, (T, LANE))

    row = pl.BlockSpec((T, 512), lambda i: (i, 0))
    zc = _zcol(T, 512, C_MLAG)
    return pl.pallas_call(
        body, name="mla_post_bwd", grid=(S // T,), in_specs=[zc, row, row, pl.BlockSpec(memory_space=pl.ANY)],
        out_specs=[zc, row, pl.BlockSpec((T, 1024), lambda i: (i, 0))],
        out_shape=[jax.ShapeDtypeStruct(dz.shape, BF16), jax.ShapeDtypeStruct((S, 512), BF16), jax.ShapeDtypeStruct((S, 1024), F32)],
        input_output_aliases={3: 0}, compiler_params=_cp(),
    )(zp, o, dy, dz)


def _mla_attn_bwd(q, k, v, do, lse, Dr):
    S = q.shape[0]
    nq = S // TA

    def body(q_ref, do_ref, lse_ref, D_ref, k_ref, v_ref, dq_ref, dk_ref, dv_ref):
        ki = pl.program_id(1)

        @pl.when(ki == 0)
        def _():
            dq_ref[...] = jnp.zeros_like(dq_ref)

        lane = _lane((TA, LANE))
        rowi = lax.broadcasted_iota(jnp.int32, (TA, TA), 0)
        coli = lax.broadcasted_iota(jnp.int32, (TA, TA), 1)
        dv_tot = jnp.zeros((TA, LANE), F32)
        for hh in range(2):
            cs = slice(hh * LANE, (hh + 1) * LANE)
            hm = (lane < 64) if hh == 0 else (lane >= 64)
            kh = k_ref[:, cs]
            vv = v_ref[...]
            vm = jnp.where(hm, vv, jnp.zeros_like(vv))

            def step(qb, carry, masked, cs=cs, kh=kh, vm=vm):
                dk_acc, dv_acc = carry
                off = pl.multiple_of(qb * TA, TA)
                qh = q_ref[pl.ds(off, TA), cs]
                doh = do_ref[pl.ds(off, TA), :]
                ls = jnp.tile(lse_ref[pl.ds(off, TA), cs], (1, TA // LANE))
                dd = jnp.tile(D_ref[pl.ds(off, TA), cs], (1, TA // LANE))
                s = _nt(qh, kh) * MLA_SCALE
                if masked:
                    s = jnp.where(rowi >= coli, s, NEG)
                p = jnp.exp(s - ls)
                dp = _nt(doh, vm)
                ds = (p * (dp - dd) * MLA_SCALE).astype(BF16)
                dv_acc = dv_acc + _tn(p.astype(BF16), doh)
                dk_acc = dk_acc + _tn(ds, qh)
                dq_ref[pl.ds(off, TA), cs] += _nn(ds, kh)
                return dk_acc, dv_acc

            z = jnp.zeros((TA, LANE), F32)
            carry = step(ki, (z, z), True)
            dk_acc, dv_acc = lax.fori_loop(ki + 1, nq, lambda qb, c: step(qb, c, False), carry)
            dk_ref[:, cs] = dk_acc
            dv_tot = dv_tot + jnp.where(hm, dv_acc, 0.0)
        dv_ref[...] = dv_tot

    pair = pl.BlockSpec((S, 256), lambda p, i: (0, p))
    return pl.pallas_call(
        body, name="mla_attn_bwd", grid=(4, nq),
        in_specs=[pair, pl.BlockSpec((S, LANE), lambda p, i: (0, p)), pair, pair,
                  pl.BlockSpec((TA, 256), lambda p, i: (i, p)), pl.BlockSpec((TA, LANE), lambda p, i: (i, p))],
        out_specs=[pair, pl.BlockSpec((TA, 256), lambda p, i: (i, p)), pl.BlockSpec((TA, LANE), lambda p, i: (i, p))],
        out_shape=[jax.ShapeDtypeStruct((S, 1024), F32), jax.ShapeDtypeStruct((S, 1024), F32), jax.ShapeDtypeStruct((S, 512), F32)],
        compiler_params=_cp(dimension_semantics=("parallel", "arbitrary")),
    )(q, do, lse, Dr, k, v)


def _mla_pre_bwd(zp, dq, dk, dv, w, tab, dz):
    S = zp.shape[0]
    T = T_MLA

    def body(cq_ref, ckv_ref, kr_ref, dq_ref, dk_ref, dv_ref, gcq_ref, gckv_ref, wuq_ref, wuk_ref, wuv_ref, gq_ref, gk_ref,
             C_ref, S1_ref, S2_ref, dz_in, dz_ref, dwuq_ref, dwuk_ref, dwuv_ref, dgcq_ref, dgckv_ref, dgq_ref, dgk_ref):
        del dz_in
        i = pl.program_id(0)

        @pl.when(i == 0)
        def _():
            for r in (dwuq_ref, dwuk_ref, dwuv_ref, dgcq_ref, dgckv_ref, dgq_ref, dgk_ref):
                r[...] = jnp.zeros_like(r)

        cq = cq_ref[...]
        rq = lax.rsqrt(jnp.mean(cq * cq, axis=-1, keepdims=True) + EPS)
        cqh = cq * rq
        cqn = (cqh * gcq_ref[...]).astype(BF16)
        ckv = ckv_ref[...]
        rkv = lax.rsqrt(jnp.mean(ckv * ckv, axis=-1, keepdims=True) + EPS)
        ckvh = ckv * rkv
        ckvn = (ckvh * gckv_ref[...]).astype(BF16)
        q0 = _nn(cqn, wuq_ref[...])
        k0 = _nn(ckvn, wuk_ref[...])
        krb = kr_ref[...]
        C, S1, S2 = C_ref[...], S1_ref[...], S2_ref[...]
        gq, gk = gq_ref[...], gk_ref[...]

        def head_bwd(x, dy, g):
            r = lax.rsqrt(_rsum(x * x) * (1.0 / MLA_QK) + EPS)
            xn = x * r
            dyn = _rope_t(dy, C, S1, S2, 16)
            dxh = dyn * g
            return r * (dxh - xn * _rsum(dxh * xn) * (1.0 / MLA_QK)), _csum(dyn * xn)

        dq0, dk0 = [], []
        dgq_acc = jnp.zeros((1, LANE), F32)
        dgk_acc = jnp.zeros((1, LANE), F32)
        dkr = jnp.zeros((T, LANE), F32)
        for h in range(8):
            sl = slice(h * LANE, (h + 1) * LANE)
            dxq, gq_p = head_bwd(q0[:, sl], dq_ref[:, sl], gq)
            dxk, gk_p = head_bwd(k0[:, sl] + krb, dk_ref[:, sl], gk)
            dq0.append(dxq.astype(BF16))
            dk0.append(dxk.astype(BF16))
            dkr = dkr + dxk
            dgq_acc = dgq_acc + gq_p
            dgk_acc = dgk_acc + gk_p
        dgq_ref[...] += dgq_acc
        dgk_ref[...] += dgk_acc
        dq0 = jnp.concatenate(dq0, axis=1)
        dk0 = jnp.concatenate(dk0, axis=1)
        dvb = dv_ref[...].astype(BF16)
        dwuq_ref[...] += _tn(cqn, dq0)
        dwuk_ref[...] += _tn(ckvn, dk0)
        dwuv_ref[...] += _tn(ckvn, dvb)
        dcqn = _nt(dq0, wuq_ref[...])
        dckvn = _nt(dk0, wuk_ref[...]) + _nt(dvb, wuv_ref[...])
        dgcq_ref[...] += _csum(dcqn * cqh)
        dgckv_ref[...] += _csum(dckvn * ckvh)
        dxh = dcqn * gcq_ref[...]
        dz_ref[:, 0:256] = (rq * (dxh - cqh * jnp.mean(dxh * cqh, axis=-1, keepdims=True))).astype(BF16)
        dxh = dckvn * gckv_ref[...]
        dz_ref[:, 256:384] = (rkv * (dxh - ckvh * jnp.mean(dxh * ckvh, axis=-1, keepdims=True))).astype(BF16)
        lane = _lane((T, LANE))
        dz_ref[:, 384:512] = jnp.where((lane >= KR_LANE) & (lane < KR_LANE + 32), dkr, 0.0).astype(BF16)

    tabspec = pl.BlockSpec((T, LANE), lambda i: (i, 0))
    in_specs = [_zcol(T, 256, C_CQ), _zcol(T, LANE, C_CKV), _zcol(T, LANE, C_KR),
                pl.BlockSpec((T, 1024), lambda i: (i, 0)), pl.BlockSpec((T, 1024), lambda i: (i, 0)), pl.BlockSpec((T, 512), lambda i: (i, 0)),
                _full((1, 256)), _full((1, LANE)), _full((256, 1024)), _full((LANE, 1024)), _full((LANE, 512)), _full((1, LANE)), _full((1, LANE)),
                tabspec, tabspec, tabspec, pl.BlockSpec(memory_space=pl.ANY)]
    out_specs = [_zcol(T, 512, C_CQ), _full((256, 1024)), _full((LANE, 1024)), _full((LANE, 512)), _full((1, 256)), _full((1, LANE)),
                 _full((1, LANE)), _full((1, LANE))]
    out_shape = [jax.ShapeDtypeStruct(dz.shape, BF16), jax.ShapeDtypeStruct((256, 1024), F32), jax.ShapeDtypeStruct((LANE, 1024), F32),
                 jax.ShapeDtypeStruct((LANE, 512), F32), jax.ShapeDtypeStruct((1, 256), F32), jax.ShapeDtypeStruct((1, LANE), F32),
                 jax.ShapeDtypeStruct((1, LANE), F32), jax.ShapeDtypeStruct((1, LANE), F32)]
    return pl.pallas_call(
        body, name="mla_pre_bwd", grid=(S // T,), in_specs=in_specs, out_specs=out_specs, out_shape=out_shape,
        input_output_aliases={16: 0}, compiler_params=_cp(),
    )(zp, zp, zp, dq, dk, dv, w["g_cq"], w["g_ckv"], w["w_uq"], w["w_uk"], w["w_uv"], w["g_mq"], w["g_mk"], *tab, dz)


T_DIL = 256


def _head_stats(x, lane):
    sq = x * x
    sa = _rsum(jnp.where(lane < 64, sq, 0.0))
    sb = _rsum(jnp.where(lane >= 64, sq, 0.0))
    return lax.rsqrt(jnp.where(lane < 64, sa, sb) * (1.0 / DIL_HD) + EPS)


def _head_sum(x, lane):
    sa = _rsum(jnp.where(lane < 64, x, 0.0))
    sb = _rsum(jnp.where(lane >= 64, x, 0.0))
    return jnp.where(lane < 64, sa, sb)


def _dil_pre_fwd(zp, w, tab):
    S = zp.shape[0]
    T = T_DIL

    def body(q_ref, k_ref, gq_ref, gk_ref, C_ref, S1_ref, S2_ref, qo_ref, ko_ref):
        C, S1, S2 = C_ref[...], S1_ref[...], S2_ref[...]
        lane = _lane((T, LANE))
        for b in range(12):
            sl = slice(b * LANE, (b + 1) * LANE)
            x = q_ref[:, sl]
            qo_ref[:, sl] = _rope(x * _head_stats(x, lane) * gq_ref[...], C, S1, S2, 32)
            x = k_ref[:, sl]
            ko_ref[:, sl] = _rope(x * _head_stats(x, lane) * gk_ref[...], C, S1, S2, 32)

    tabspec = pl.BlockSpec((T, LANE), lambda i: (i, 0))
    out = pl.BlockSpec((T, 1536), lambda i: (i, 0))
    return pl.pallas_call(
        body, name="dil_pre_fwd", grid=(S // T,),
        in_specs=[_zcol(T, 1536, C_DQ), _zcol(T, 1536, C_DK), _full((1, LANE)), _full((1, LANE)), tabspec, tabspec, tabspec],
        out_specs=[out, out], out_shape=[jax.ShapeDtypeStruct((S, 1536), F32)] * 2, compiler_params=_cp(),
    )(zp, zp, w["g_dq"], w["g_dk"], *tab)


DIL_ROWS = 2048


def _dil_geometry(g, S):
    d = DIL_DILATIONS[g]
    P = NK * d
    return d, P, DIL_ROWS // P, S // P


def _dil_rows(start, d, blocks=1):
    return pl.ds(pl.multiple_of(start, NK), blocks * NK) if d == 1 else pl.ds(start, blocks * NK, stride=d)


def _dil_specs(g, S, col0):
    _, P, m, nb = _dil_geometry(g, S)
    cur = pl.BlockSpec((DIL_ROWS, LANE), lambda sb, c: (sb, col0 + c))
    prv = pl.BlockSpec((P, LANE), lambda sb, c: (jnp.maximum(sb * m - 1, 0), col0 + c))
    nxt = pl.BlockSpec((P, LANE), lambda sb, c: (jnp.minimum((sb + 1) * m, nb - 1), col0 + c))
    return cur, prv, nxt


def _dil_attn_fwd(q, k, zp, g):
    S = q.shape[0]
    d, P, m, nb = _dil_geometry(g, S)
    R = DIL_ROWS

    def body(q_ref, kc_ref, kp_ref, vc_ref, vp_ref, o_ref, lse_ref, *scr):
        sb = pl.program_id(0)
        if m > 1:
            ks_ref, vs_ref = scr
            ks_ref[0:P, :] = kp_ref[...]
            ks_ref[P:P + R, :] = kc_ref[...]
            vs_ref[0:P, :] = vp_ref[...]
            vs_ref[P:P + R, :] = vc_ref[...]
        lane = _lane((NK, LANE))

        def unit(u, carry):
            j = u // d
            start = j * P + (u - j * d)
            rows = _dil_rows(start, d)
            if m > 1:
                k2, v2 = ks_ref[_dil_rows(start, d, 2), :], vs_ref[_dil_rows(start, d, 2), :]
            else:
                k2 = jnp.concatenate([kp_ref[rows, :], kc_ref[rows, :]], axis=0)
                v2 = jnp.concatenate([vp_ref[rows, :], vc_ref[rows, :]], axis=0)
            k2, v2 = k2.astype(BF16), v2.astype(BF16)
            q_ = q_ref[rows, :].astype(BF16)
            row = lax.broadcasted_iota(jnp.int32, (NK, 2 * NK), 0)
            col = lax.broadcasted_iota(jnp.int32, (NK, 2 * NK), 1)
            band = (col >= row) & (col <= row + NK) & ((col >= NK) | (sb * m + j > 0))
            lane2 = _lane((2 * NK, LANE))
            zb, zv = jnp.zeros_like(q_), jnp.zeros_like(v2)
            o_tot = jnp.zeros((NK, LANE), F32)
            lse_tot = jnp.zeros((NK, LANE), F32)
            for hh in range(2):
                hm = (lane < 64) if hh == 0 else (lane >= 64)
                hm2 = (lane2 < 64) if hh == 0 else (lane2 >= 64)
                s_ = jnp.where(band, _nt(jnp.where(hm, q_, zb), k2) * DIL_SCALE, NEG)
                mx = jnp.max(s_, axis=-1, keepdims=True)
                e = jnp.exp(s_ - mx)
                den = _rsum(e)
                o_tot = o_tot + _nn(e.astype(BF16), jnp.where(hm2, v2, zv)) / den
                lse_tot = jnp.where(hm, mx + jnp.log(den), lse_tot)
            o_ref[rows, :] = o_tot
            lse_ref[rows, :] = lse_tot
            return carry

        lax.fori_loop(0, R // NK, unit, 0, unroll=8)

    qcur, qprv, _ = _dil_specs(g, S, 4 * g)
    vcur, vprv, _ = _dil_specs(g, S, C_DV + 4 * g)
    out = pl.BlockSpec((R, LANE), lambda sb, c: (sb, c))
    return pl.pallas_call(
        body, name=f"dil_attn_fwd{g}", grid=(S // R, 4), in_specs=[qcur, qcur, qprv, vcur, vprv], out_specs=[out, out],
        out_shape=[jax.ShapeDtypeStruct((S, 512), F32)] * 2,
        scratch_shapes=[pltpu.VMEM((P + R, LANE), F32)] * 2 if m > 1 else [], compiler_params=_cp(),
    )(q, k, k, zp, zp)


def _dil_combine(os_, ls_, zp):
    S = zp.shape[0]
    T = T_ROW

    def body(o0, o1, o2, l0, l1, l2, g_ref, oc_ref, L_ref, y_ref):
        a, b, c = l0[...], l1[...], l2[...]
        mx = jnp.maximum(jnp.maximum(a, b), c)
        ea, eb, ec = jnp.exp(a - mx), jnp.exp(b - mx), jnp.exp(c - mx)
        den = ea + eb + ec
        oc = (ea * o0[...] + eb * o1[...] + ec * o2[...]) / den
        oc_ref[...] = oc
        L_ref[...] = mx + jnp.log(den)
        y_ref[...] = (oc * _silu(g_ref[...])).astype(BF16)

    row = pl.BlockSpec((T, 512), lambda i: (i, 0))
    return pl.pallas_call(
        body, name="dil_combine", grid=(S // T,), in_specs=[row] * 6 + [_zcol(T, 512, C_DILG)], out_specs=[row, row, row],
        out_shape=[jax.ShapeDtypeStruct((S, 512), F32), jax.ShapeDtypeStruct((S, 512), F32), jax.ShapeDtypeStruct((S, 512), BF16)],
        compiler_params=_cp(),
    )(*os_, *ls_, zp)


def _dil_comb_bwd(zp, oc, dy, dz):
    S = zp.shape[0]
    T = T_ROW

    def body(g_ref, o_ref, dy_ref, dz_in, dz_ref, do_ref, D_ref):
        del dz_in
        g, o_, dy_ = g_ref[...], o_ref[...], dy_ref[...]
        do = dy_ * _silu(g)
        do_ref[...] = do
        dz_ref[...] = (dy_ * o_ * _dsilu(g)).astype(BF16)
        lane = _lane((T, LANE))
        for p in range(4):
            sl = slice(p * LANE, (p + 1) * LANE)
            D_ref[:, sl] = _head_sum(do[:, sl] * o_[:, sl], lane)

    row = pl.BlockSpec((T, 512), lambda i: (i, 0))
    zc = _zcol(T, 512, C_DILG)
    return pl.pallas_call(
        body, name="dil_comb_bwd", grid=(S // T,), in_specs=[zc, row, row, pl.BlockSpec(memory_space=pl.ANY)], out_specs=[zc, row, row],
        out_shape=[jax.ShapeDtypeStruct(dz.shape, BF16), jax.ShapeDtypeStruct((S, 512), F32), jax.ShapeDtypeStruct((S, 512), F32)],
        input_output_aliases={3: 0}, compiler_params=_cp(),
    )(zp, oc, dy, dz)


def _dil_attn_bwd(q, k, zp, do, L, Dr, g):
    S = q.shape[0]
    d, P, m, nb = _dil_geometry(g, S)
    R = DIL_ROWS
    n_q, n_k = 4, 2

    def body(*refs):
        q_side = refs[0:2 * n_q]
        k_side = refs[2 * n_q:2 * n_q + 2 * n_k]
        dq_ref, dk_ref, dv_ref = refs[2 * n_q + 2 * n_k:2 * n_q + 2 * n_k + 3]
        scr = refs[2 * n_q + 2 * n_k + 3:]
        sb = pl.program_id(0)
        if m > 1:
            for a in range(n_q):
                scr[a][0:R, :] = q_side[2 * a][...]
                scr[a][R:R + P, :] = q_side[2 * a + 1][...]
            for a in range(n_k):
                scr[n_q + a][0:P, :] = k_side[2 * a + 1][...]
                scr[n_q + a][P:P + R, :] = k_side[2 * a][...]
        lane = _lane((NK, LANE))

        def unit(u, carry):
            j = u // d
            start = j * P + (u - j * d)
            rows = _dil_rows(start, d)
            if m > 1:
                rows_b = _dil_rows(start + P, d)
                q2, do2, L2, D2 = [scr[a][_dil_rows(start, d, 2), :] for a in range(n_q)]
                kp, vp = [scr[n_q + a][rows, :] for a in range(n_k)]
                kc, vc = [scr[n_q + a][rows_b, :] for a in range(n_k)]
            else:
                q2, do2, L2, D2 = [jnp.concatenate([q_side[2 * a][rows, :], q_side[2 * a + 1][rows, :]], axis=0) for a in range(n_q)]
                kc, vc = [k_side[2 * a][rows, :] for a in range(n_k)]
                kp, vp = [k_side[2 * a + 1][rows, :] for a in range(n_k)]
            q2, do2 = q2.astype(BF16), do2.astype(BF16)
            kc, kp, vc, vp = kc.astype(BF16), kp.astype(BF16), vc.astype(BF16), vp.astype(BF16)
            n = sb * m + j
            row2 = lax.broadcasted_iota(jnp.int32, (2 * NK, NK), 0)
            col2 = lax.broadcasted_iota(jnp.int32, (2 * NK, NK), 1)
            m2 = ((row2 < NK) & (col2 <= row2)) | ((row2 >= NK) & (col2 >= row2 - NK) & (n < nb - 1))
            row = lax.broadcasted_iota(jnp.int32, (NK, NK), 0)
            col = lax.broadcasted_iota(jnp.int32, (NK, NK), 1)
            mp = (col >= row) & (n > 0)
            lane2 = _lane((2 * NK, LANE))
            zq, zb = jnp.zeros_like(q2), jnp.zeros_like(kc)
            dq_tot = jnp.zeros((NK, LANE), F32)
            dk_tot = jnp.zeros((NK, LANE), F32)
            dv_tot = jnp.zeros((NK, LANE), F32)
            for hh in range(2):
                hm = (lane < 64) if hh == 0 else (lane >= 64)
                hm2 = (lane2 < 64) if hh == 0 else (lane2 >= 64)
                Lb = jnp.where(hm2, L2, pltpu.roll(L2, 64, 1))
                Db = jnp.where(hm2, D2, pltpu.roll(D2, 64, 1))
                qm2 = jnp.where(hm2, q2, zq)
                vcm = jnp.where(hm, vc, zb)
                vpm = jnp.where(hm, vp, zb)
                p2 = jnp.exp(jnp.where(m2, _nt(qm2, kc) * DIL_SCALE, NEG) - Lb)
                ds2 = (p2 * (_nt(do2, vcm) - Db) * DIL_SCALE).astype(BF16)
                dk_tot = dk_tot + _tn(ds2, qm2)
                dv_tot = dv_tot + jnp.where(hm, _tn(p2.astype(BF16), do2), 0.0)
                pp = jnp.exp(jnp.where(mp, _nt(qm2[0:NK], kp) * DIL_SCALE, NEG) - Lb[0:NK])
                dsp = (pp * (_nt(do2[0:NK], vpm) - Db[0:NK]) * DIL_SCALE).astype(BF16)
                dq_tot = dq_tot + jnp.where(hm, _nn(ds2[0:NK], kc) + _nn(dsp, kp), 0.0)
            dq_ref[rows, :] = dq_tot
            dk_ref[rows, :] = dk_tot
            dv_ref[rows, :] = dv_tot
            return carry

        lax.fori_loop(0, R // NK, unit, 0, unroll=8)

    qcur, qprv, qnxt = _dil_specs(g, S, 4 * g)
    vcur, vprv, _ = _dil_specs(g, S, C_DV + 4 * g)
    ocur, _, onxt = _dil_specs(g, S, 0)
    out = pl.BlockSpec((R, LANE), lambda sb, c: (sb, c))
    scratch = [pltpu.VMEM((P + R, LANE), F32)] * (n_q + n_k) if m > 1 else []
    return pl.pallas_call(
        body, name=f"dil_attn_bwd{g}", grid=(S // R, 4),
        in_specs=[qcur, qnxt, ocur, onxt, ocur, onxt, ocur, onxt, qcur, qprv, vcur, vprv],
        out_specs=[out, out, out], out_shape=[jax.ShapeDtypeStruct((S, 512), F32)] * 3, scratch_shapes=scratch, compiler_params=_cp(),
    )(q, q, do, do, L, L, Dr, Dr, k, k, zp, zp)


def _dil_pre_bwd(zp, dys, g, tab, dz, col, name):
    S = zp.shape[0]
    T = T_DIL

    def body(x_ref, dy0_ref, dy1_ref, dy2_ref, g_ref, C_ref, S1_ref, S2_ref, dz_in, dz_ref, dg_ref):
        del dz_in
        i = pl.program_id(0)
        C, S1, S2 = C_ref[...], S1_ref[...], S2_ref[...]
        lane = _lane((T, LANE))
        gv = g_ref[...]
        acc = jnp.zeros((1, LANE), F32)
        for b in range(12):
            sl = slice(b * LANE, (b + 1) * LANE)
            x = x_ref[:, sl]
            r = _head_stats(x, lane)
            xn = x * r
            dy_ref = (dy0_ref, dy1_ref, dy2_ref)[b // 4]
            dyn = _rope_t(dy_ref[:, (b % 4) * LANE:(b % 4 + 1) * LANE], C, S1, S2, 32)
            acc = acc + _csum(dyn * xn)
            dxh = dyn * gv
            dz_ref[:, sl] = (r * (dxh - xn * _head_sum(dxh * xn, lane) * (1.0 / DIL_HD))).astype(BF16)

        @pl.when(i == 0)
        def _():
            dg_ref[...] = acc

        @pl.when(i > 0)
        def _():
            dg_ref[...] += acc

    tabspec = pl.BlockSpec((T, LANE), lambda i: (i, 0))
    zc = _zcol(T, 1536, col)
    grp = pl.BlockSpec((T, 512), lambda i: (i, 0))
    return pl.pallas_call(
        body, name=name, grid=(S // T,),
        in_specs=[zc, grp, grp, grp, _full((1, LANE)), tabspec, tabspec, tabspec, pl.BlockSpec(memory_space=pl.ANY)],
        out_specs=[zc, _full((1, LANE))], out_shape=[jax.ShapeDtypeStruct(dz.shape, BF16), jax.ShapeDtypeStruct((1, LANE), F32)],
        input_output_aliases={8: 0}, compiler_params=_cp(),
    )(zp, *dys, g, *tab, dz)


def _dil_dv_into(dvs, dz):
    S = dz.shape[0]
    T = T_ROW

    def body(s0, s1, s2, dz_in, o_ref):
        del dz_in
        for gi, s in enumerate((s0, s1, s2)):
            o_ref[:, gi * 512:(gi + 1) * 512] = s[...].astype(BF16)

    grp = pl.BlockSpec((T, 512), lambda i: (i, 0))
    return pl.pallas_call(
        body, name="dil_dv", grid=(S // T,), in_specs=[grp, grp, grp, pl.BlockSpec(memory_space=pl.ANY)],
        out_specs=_zcol(T, 1536, C_DV), out_shape=jax.ShapeDtypeStruct(dz.shape, BF16), input_output_aliases={3: 0}, compiler_params=_cp(),
    )(*dvs, dz)


T_MRG = 256


def _merge_fwd(P, zp, b_merge):
    S = zp.shape[0]
    T = T_MRG

    def body(p0, p1, p2, m0, m1, m2, b_ref, o_ref):
        acc = jnp.zeros((T, D), F32)
        for j, (p, m) in enumerate(((p0, m0), (p1, m1), (p2, m2))):
            acc = acc + _sig(m[...] + b_ref[:, j * D:(j + 1) * D]) * p[...]
        o_ref[...] = acc.astype(BF16)

    row = pl.BlockSpec((T, D), lambda i: (i, 0))
    return pl.pallas_call(
        body, name="merge_fwd", grid=(S // T,),
        in_specs=[row, row, row] + [_zcol(T, D, C_MERGE + 8 * j) for j in range(3)] + [_full((1, 3 * D))], out_specs=row,
        out_shape=jax.ShapeDtypeStruct((S, D), BF16), compiler_params=_cp(),
    )(*P, zp, zp, zp, b_merge)


def _merge_bwd(dm, Pj, zp, bj, dz, j):
    S = zp.shape[0]
    T = T_MRG

    def body(dm_ref, p_ref, m_ref, b_ref, dz_in, dz_ref, dp_ref, db_ref):
        del dz_in
        i = pl.program_id(0)
        g = _sig(m_ref[...] + b_ref[...])
        dmv = dm_ref[...]
        dp_ref[...] = (dmv * g).astype(BF16)
        dg = dmv * p_ref[...] * g * (1.0 - g)
        dz_ref[...] = dg.astype(BF16)
        part = _csum(dg)

        @pl.when(i == 0)
        def _():
            db_ref[...] = part

        @pl.when(i > 0)
        def _():
            db_ref[...] += part

    row = pl.BlockSpec((T, D), lambda i: (i, 0))
    zc = _zcol(T, D, C_MERGE + 8 * j)
    return pl.pallas_call(
        body, name=f"merge_bwd{j}", grid=(S // T,), in_specs=[row, row, zc, _full((1, D)), pl.BlockSpec(memory_space=pl.ANY)],
        out_specs=[zc, row, _full((1, D))],
        out_shape=[jax.ShapeDtypeStruct(dz.shape, BF16), jax.ShapeDtypeStruct((S, D), BF16), jax.ShapeDtypeStruct((1, D), F32)],
        input_output_aliases={4: 0}, compiler_params=_cp(),
    )(dm, Pj, zp, bj, dz)


def _loss_fwd_bwd(y, target):
    S = y.shape[0]
    T = T_ROW

    def body(y_ref, t_ref, loss_ref, dy_ref):
        i = pl.program_id(0)
        err = y_ref[...] - t_ref[...]
        dy_ref[...] = err * (1.0 / D)
        part = jnp.sum(err * err, keepdims=True).reshape(1, 1) * (0.5 / D)

        @pl.when(i == 0)
        def _():
            loss_ref[...] = part

        @pl.when(i > 0)
        def _():
            loss_ref[...] += part

    row = pl.BlockSpec((T, D), lambda i: (i, 0))
    return pl.pallas_call(
        body, name="loss", grid=(S // T,), in_specs=[row, row], out_specs=[_full((1, 1)), row],
        out_shape=[jax.ShapeDtypeStruct((1, 1), F32), jax.ShapeDtypeStruct((S, D), F32)], compiler_params=_cp(),
    )(y, target)


def _layer_fwd(x, w, tabs):
    mla_tab, dil_tab = tabs
    S = x.shape[0]
    h = _rms_in_fwd(x, w["norm_g"])
    zp = _mm(h, w["w_in"], mode="nn", name="in_proj")
    hs, y_lru = _lru_fwd(zp, w)
    q, k, v = _mla_pre_fwd(zp, w, mla_tab)
    o_mla, lse, y_mla = _mla_attn_fwd(q, k, v, zp)
    qd, kd = _dil_pre_fwd(zp, w, dil_tab)
    og, lg = zip(*[_dil_attn_fwd(qd, kd, zp, g) for g in range(len(DIL_DILATIONS))])
    oc, L, y_dil = _dil_combine(og, lg, zp)
    P = [_mm(y_lru, w["w_lru_o"], mode="nn", name="lru_out"), _mm(y_mla, w["w_mla_o"], mode="nn", name="mla_out"),
         _mm(y_dil, w["w_dil_o"], mode="nn", name="dil_out")]
    merged = _merge_fwd(P, zp, w["b_merge"])
    x_out = _mm(merged, w["w_out"], mode="nn", name="out_proj", add=x)
    saved = dict(x=x, h=h, zp=zp, hs=hs, y=(y_lru, y_mla, y_dil), q=q, k=k, v=v, o_mla=o_mla, lse=lse, qd=qd, kd=kd, oc=oc, L=L, P=P,
                 merged=merged)
    return x_out, saved


def _layer_bwd(dout, w, tabs, sv, hook=None, after=None):
    mla_tab, dil_tab = tabs
    zp = sv["zp"]
    S = zp.shape[0]
    g = {}
    dm = _mm(dout, w["w_out"], mode="nt", name="d_merged", after=after)
    g["w_out"] = _mm(sv["merged"], dout, mode="tn", name="dw_out", out_dtype=BF16)
    dz = lax.empty((S, ZW), BF16)
    dP, db = [], []
    for j in range(3):
        dz, dpj, dbj = _merge_bwd(dm, sv["P"][j], zp, w["b_merge"][:, j * D:(j + 1) * D], dz, j)
        dP.append(dpj)
        db.append(dbj)
    g["b_merge"] = jnp.concatenate(db, axis=1)
    names = ("w_lru_o", "w_mla_o", "w_dil_o")
    dy = []
    for j in range(3):
        dy.append(_mm(dP[j], w[names[j]], mode="nt", name="dy_" + names[j]))
        g[names[j]] = _mm(sv["y"][j], dP[j], mode="tn", name="d" + names[j], out_dtype=BF16)
    dz = _lru_gate_bwd(zp, sv["hs"], dy[0], dz)
    dz, g["conv_w"], g["conv_b"], g["w_gx"], g["b_gx"], g["w_ga"], g["b_ga"], g["lam"] = _lru_bwd(zp, sv["hs"], dy[0], w, dz)
    dz, do, Dr = _mla_post_bwd(zp, sv["o_mla"], dy[1], dz)
    dq, dk, dv = _mla_attn_bwd(sv["q"], sv["k"], sv["v"], do, sv["lse"], Dr)
    dz, g["w_uq"], g["w_uk"], g["w_uv"], g["g_cq"], g["g_ckv"], g["g_mq"], g["g_mk"] = _mla_pre_bwd(zp, dq, dk, dv, w, mla_tab, dz)
    dz, dod, Dd = _dil_comb_bwd(zp, sv["oc"], dy[2], dz)
    dqs, dks, dvs = zip(*[_dil_attn_bwd(sv["qd"], sv["kd"], zp, dod, sv["L"], Dd, gi) for gi in range(len(DIL_DILATIONS))])
    dz, g["g_dq"] = _dil_pre_bwd(zp, dqs, w["g_dq"], dil_tab, dz, C_DQ, "dil_pre_bwd_q")
    dz, g["g_dk"] = _dil_pre_bwd(zp, dks, w["g_dk"], dil_tab, dz, C_DK, "dil_pre_bwd_k")
    dz = _dil_dv_into(dvs, dz)
    g["w_in"] = _mm(sv["h"], dz, mode="tn", name="dw_in", out_dtype=BF16)
    token = hook(g) if hook is not None else None
    dh = _mm(dz, w["w_in"], mode="nt", name="d_h", after=token)
    dx, g["norm_g"] = _rms_in_bwd(sv["x"], w["norm_g"], dh, dout)
    return dx, g


def _peers():
    mx, my, mc = lax.axis_index("x"), lax.axis_index("y"), lax.axis_index("c")
    me = 4 * mx + 2 * my + mc
    out = []
    for k in range(1, N_DEV):
        px = 1 - mx if k & 4 else mx
        py = 1 - my if k & 2 else my
        pc = 1 - mc if k & 1 else mc
        out.append(((px, py, pc), 4 * px + 2 * py + pc))
    return me, out


def _whole(ref, p):
    del p
    return ref


def _exchange(srcs, slicers, slices, name):
    n = len(srcs)

    def body(*refs):
        ins, outs = refs[:n], refs[n:2 * n]
        send_sems, recv_sems, local_sems = refs[2 * n:]
        me, peers = _peers()
        mine = [pltpu.make_async_copy(slicers[a](ins[a], me), outs[a].at[me], local_sems.at[a]) for a in range(n)]
        for cp in mine:
            cp.start()
        copies = []
        for k, (peer, pidx) in enumerate(peers):
            for a in range(n):
                cp = pltpu.make_async_remote_copy(
                    src_ref=slicers[a](ins[a], pidx), dst_ref=outs[a].at[me], send_sem=send_sems.at[k * n + a],
                    recv_sem=recv_sems.at[k * n + a], device_id=peer, device_id_type=pl.DeviceIdType.MESH)
                cp.start()
                copies.append(cp)
        for cp in copies + mine:
            cp.wait()

    nsem = (N_DEV - 1) * n
    return pl.pallas_call(
        body, name=name, out_shape=[jax.ShapeDtypeStruct((N_DEV,) + shp, dt) for shp, dt in slices],
        in_specs=[pl.BlockSpec(memory_space=pl.ANY)] * n, out_specs=[pl.BlockSpec(memory_space=pl.ANY)] * n,
        scratch_shapes=[pltpu.SemaphoreType.DMA((nsem,)), pltpu.SemaphoreType.DMA((nsem,)), pltpu.SemaphoreType.DMA((n,))],
        compiler_params=pltpu.CompilerParams(has_side_effects=True),
    )(*srcs)


def _gather_two_level(srcs, name):
    n = len(srcs)

    def body(*refs):
        ins, outs = refs[:n], refs[n:2 * n]
        send_sems, recv_sems, local_sems = refs[2 * n:]
        mx, my, mc = lax.axis_index("x"), lax.axis_index("y"), lax.axis_index("c")
        me, sibling = (mx, my, mc), (mx, my, 1 - mc)
        chips = [(1 - mx, my), (mx, 1 - my), (1 - mx, 1 - my)]
        slot = lambda d: 4 * d[0] + 2 * d[1] + d[2]

        def copy(j, a, block, to, own=False):
            return pltpu.make_async_remote_copy(
                src_ref=ins[a] if own else outs[a].at[slot(block)], dst_ref=outs[a].at[slot(block)],
                send_sem=send_sems.at[j * n + a], recv_sem=recv_sems.at[j * n + a], device_id=to, device_id_type=pl.DeviceIdType.MESH)

        mine = [pltpu.make_async_copy(ins[a], outs[a].at[slot(me)], local_sems.at[a]) for a in range(n)]
        first = [copy(1 + j, a, me, (*chip, mc), own=True) for j, chip in enumerate(chips) for a in range(n)]
        first += [copy(0, a, me, sibling, own=True) for a in range(n)]
        for cp in mine + first:
            cp.start()
        passed = []
        for j, chip in enumerate(chips):
            for a in range(n):
                copy(1 + j, a, (*chip, mc), me).wait_recv()
                cp = copy(4 + j, a, (*chip, mc), sibling)
                cp.start()
                passed.append(cp)
        for a in range(n):
            copy(0, a, sibling, me).wait_recv()
        for j, chip in enumerate(chips):
            for a in range(n):
                copy(4 + j, a, (*chip, 1 - mc), me).wait_recv()
        for cp in first + passed:
            cp.wait_send()
        for cp in mine:
            cp.wait()

    nsem = (N_DEV - 1) * n
    return pl.pallas_call(
        body, name=name, out_shape=[jax.ShapeDtypeStruct((N_DEV,) + a.shape, a.dtype) for a in srcs],
        in_specs=[pl.BlockSpec(memory_space=pl.ANY)] * n, out_specs=[pl.BlockSpec(memory_space=pl.ANY)] * n,
        scratch_shapes=[pltpu.SemaphoreType.DMA((nsem,)), pltpu.SemaphoreType.DMA((nsem,)), pltpu.SemaphoreType.DMA((n,))],
        compiler_params=pltpu.CompilerParams(has_side_effects=True),
    )(*srcs)


_HBM = pl.BlockSpec(memory_space=pltpu.HBM)
_SEM = pl.BlockSpec(memory_space=pltpu.SEMAPHORE)
_DATAFLOW = pltpu.SideEffectType.DATAFLOW_SIDE_EFFECTING


def _exchange_start(srcs, slicers, slices, after, name):
    n = len(srcs)
    nsem = (N_DEV - 1) * n
    lands = [lax.empty((N_DEV,) + shp, dt) for shp, dt in slices]

    def body(*refs):
        ins, lands_in = refs[:n], refs[n:2 * n]
        send_sems, recv_sems, local_sems = refs[2 * n + 1], refs[2 * n + 2], refs[2 * n + 3]
        token = refs[-1]
        me, peers = _peers()
        for a in range(n):
            pltpu.make_async_copy(slicers[a](ins[a], me), lands_in[a].at[me], local_sems.at[a]).start()
        for k, (peer, pidx) in enumerate(peers):
            for a in range(n):
                pltpu.make_async_remote_copy(
                    src_ref=slicers[a](ins[a], pidx), dst_ref=lands_in[a].at[me], send_sem=send_sems.at[k * n + a],
                    recv_sem=recv_sems.at[k * n + a], device_id=peer, device_id_type=pl.DeviceIdType.MESH).start()
        token[...] = jnp.zeros_like(token)

    hbm = lambda a: pltpu.with_memory_space_constraint(a, pltpu.HBM)
    return pl.pallas_call(
        body, name=name,
        out_shape=(pltpu.SemaphoreType.DMA((nsem,)), pltpu.SemaphoreType.DMA((nsem,)), pltpu.SemaphoreType.DMA((n,)),
                   *[pltpu.HBM(a.shape, a.dtype) for a in srcs], *[pltpu.HBM(a.shape, a.dtype) for a in lands],
                   jax.ShapeDtypeStruct((SUB, LANE), F32)),
        in_specs=[_HBM] * (2 * n) + [pl.BlockSpec(memory_space=pl.ANY)],
        out_specs=(_SEM, _SEM, _SEM, *[_HBM] * (2 * n), pl.BlockSpec(memory_space=pltpu.VMEM)),
        input_output_aliases={i: 3 + i for i in range(2 * n)},
        compiler_params=pltpu.CompilerParams(has_side_effects=_DATAFLOW),
    )(*[hbm(a) for a in srcs], *[hbm(a) for a in lands], after)


def _exchange_wait(started, slicers, after, name):
    n = (len(started) - 4) // 2
    sems, thru = started[0:3], started[3:3 + 2 * n]

    def body(*refs):
        srcs, lands = refs[:n], refs[n:2 * n]
        send_sems, recv_sems, local_sems = refs[2 * n], refs[2 * n + 1], refs[2 * n + 2]
        me, peers = _peers()
        for k, (peer, pidx) in enumerate(peers):
            for a in range(n):
                cp = pltpu.make_async_remote_copy(
                    src_ref=slicers[a](srcs[a], pidx), dst_ref=lands[a].at[me], send_sem=send_sems.at[k * n + a],
                    recv_sem=recv_sems.at[k * n + a], device_id=peer, device_id_type=pl.DeviceIdType.MESH)
                cp.wait_send()
                cp.wait_recv()
        for a in range(n):
            pltpu.make_async_copy(slicers[a](srcs[a], me), lands[a].at[me], local_sems.at[a]).wait()

    outs = pl.pallas_call(
        body, name=name, out_shape=[pltpu.HBM(a.shape, a.dtype) for a in thru],
        in_specs=[_HBM] * (2 * n) + [_SEM, _SEM, _SEM, pl.BlockSpec(memory_space=pl.ANY)], out_specs=[_HBM] * (2 * n),
        input_output_aliases={i: i for i in range(2 * n)}, compiler_params=pltpu.CompilerParams(has_side_effects=_DATAFLOW),
    )(*thru, *sems, after)
    return outs[n:]


WIN = 13 * LANE


def _win_base(s):
    n = s * SHARD_IN
    a0 = n + jnp.where(n >= _KR0, KR_LANE, 0) + jnp.where(n >= _KR0 + 32, 32, 0)
    return jnp.minimum(a0 // LANE, (ZW - WIN) // LANE)


def _win_offsets(s):
    n = s * SHARD_IN + jnp.arange(SHARD_IN)
    o = s * SHARD_IN - _win_base(s) * LANE
    return n, (o, o + KR_LANE, o + LANE - 32)


def _to_window(shard, s):
    _, offs = _win_offsets(s)
    padded = jnp.pad(shard, ((0, 0), (0, 0), (WIN, WIN)))
    a, b, c = [lax.dynamic_slice(padded, (0, 0, WIN - o), shard.shape[:2] + (WIN,)) for o in offs]
    col = (_win_base(s) * LANE + jnp.arange(WIN))[None, None, :]
    zero = jnp.zeros_like(a)
    return jnp.where(col < _KR0, a, jnp.where((col >= _KR0 + KR_LANE) & (col < _KR0 + KR_LANE + 32), b, jnp.where(col >= _KR0 + LANE, c, zero)))


def _from_window(win, s):
    n, offs = _win_offsets(s)
    a, b, c = [lax.dynamic_slice(win, (0, 0, o), win.shape[:2] + (SHARD_IN,)) for o in offs]
    return jnp.where((n < _KR0)[None, None, :], a, jnp.where((n < _KR0 + 32)[None, None, :], b, c))


def _win_base_static(s):
    n = s * SHARD_IN
    a0 = n + (KR_LANE if n >= _KR0 else 0) + (32 if n >= _KR0 + 32 else 0)
    return min(a0 // LANE, (ZW - WIN) // LANE)


def _assemble_w_in(gw):
    tr = 128
    bases = [_win_base_static(s) for s in range(N_DEV)]

    def body(g_ref, o_ref):
        for j in range(ZW // LANE):
            acc = None
            for s in range(N_DEV):
                if bases[s] <= j < bases[s] + WIN // LANE:
                    piece = g_ref[s, :, (j - bases[s]) * LANE:(j - bases[s] + 1) * LANE]
                    acc = piece if acc is None else acc + piece
            o_ref[:, j * LANE:(j + 1) * LANE] = acc

    return pl.pallas_call(
        body, name="assemble_w_in", grid=(D // tr,), in_specs=[pl.BlockSpec((N_DEV, tr, WIN), lambda i: (0, i, 0))],
        out_specs=pl.BlockSpec((tr, ZW), lambda i: (i, 0)), out_shape=jax.ShapeDtypeStruct((D, ZW), gw.dtype), compiler_params=_cp(),
    )(gw)


def _cols(width):
    return lambda ref, p: ref.at[:, pl.ds(pl.multiple_of(p * width, width), width)]


def _rows(height):
    return lambda ref, p: ref.at[pl.ds(pl.multiple_of(p * height, height), height), :]


SCATTER = {
    'w_in': (lambda ref, p: ref.at[:, pl.ds(pl.multiple_of(_win_base(p) * LANE, LANE), WIN)], (D, WIN), BF16),
    'conv_w': (_cols(LANE), (4, LANE), F32),
    'w_lru_o': (_rows(LANE), (LANE, D), BF16),
    'w_uq': (_cols(LANE), (256, LANE), F32),
    'w_ukv': (_cols(LANE), (128, LANE), F32),
    'w_mla_o': (_cols(LANE), (512, LANE), BF16),
    'w_dil_o': (_cols(LANE), (512, LANE), BF16),
    'w_out': (_rows(LANE), (LANE, D), BF16),
}


PACK_ROWS = 64


def _packed_rows(shapes):
    n = sum(int(np.prod(s)) for s in shapes)
    return -(-n // (PACK_ROWS * LANE)) * PACK_ROWS


def _sum8(buf, name):
    _, R, C = buf.shape
    tr = R
    while tr * C * 4 * N_DEV > (1 << 22) and tr % 16 == 0:
        tr //= 2

    def body(b_ref, o_ref):
        acc = b_ref[0].astype(F32)
        for s in range(1, N_DEV):
            acc = acc + b_ref[s].astype(F32)
        o_ref[...] = acc

    return pl.pallas_call(
        body, name=name, grid=(R // tr,), in_specs=[pl.BlockSpec((N_DEV, tr, C), lambda i: (0, i, 0))],
        out_specs=pl.BlockSpec((tr, C), lambda i: (i, 0)), out_shape=jax.ShapeDtypeStruct((R, C), F32), compiler_params=_cp(),
    )(buf)


def _pack(arrs, dtype, lead):
    flat = [a.astype(dtype).reshape(a.shape[:lead] + (-1,)) for a in arrs]
    cat = jnp.concatenate(flat, axis=-1)
    n = cat.shape[-1]
    unit = PACK_ROWS * LANE
    pad = (-n) % unit
    if pad:
        cat = jnp.pad(cat, [(0, 0)] * lead + [(0, pad)])
    return cat.reshape(cat.shape[:lead] + ((n + pad) // LANE, LANE))


def _unpack(buf, shapes, lead):
    flat = buf.reshape(buf.shape[:lead] + (-1,))
    out, off = [], 0
    for shp in shapes:
        n = int(np.prod(shp))
        out.append(flat[..., off:off + n].reshape(buf.shape[:lead] + tuple(shp)))
        off += n
    return out


def _adamw(w, g, m, v, name):
    rows, cols = w.shape
    tr = rows
    while tr * cols * 4 > (3 << 19) and tr % 16 == 0:
        tr //= 2
    c1 = 1.0 - ADAM_B1 ** ADAM_STEP
    c2 = 1.0 - ADAM_B2 ** ADAM_STEP

    def body(w_ref, g_ref, m_ref, v_ref, d_ref, mo_ref, vo_ref):
        gv = g_ref[...]
        mn = ADAM_B1 * m_ref[...] + (1.0 - ADAM_B1) * gv
        vn = ADAM_B2 * v_ref[...] + (1.0 - ADAM_B2) * (gv * gv)
        mo_ref[...] = mn
        vo_ref[...] = vn
        d_ref[...] = -ADAM_LR * ((mn / c1) / (jnp.sqrt(vn / c2) + ADAM_EPS) + ADAM_WD * w_ref[...])

    spec = pl.BlockSpec((tr, cols), lambda i: (i, 0))
    return pl.pallas_call(
        body, name=name, grid=(rows // tr,), in_specs=[spec] * 4, out_specs=[spec] * 3,
        out_shape=[jax.ShapeDtypeStruct((rows, cols), F32)] * 3, compiler_params=_cp(),
    )(w, g, m, v)


IN_NAMES = ['x', 'positions', 'norm_g', 'w_in', 'conv_w', 'conv_b', 'w_gate_x', 'b_gate_x', 'w_gate_a', 'b_gate_a', 'lru_lambda', 'w_lru_o',
            'cq_norm_g', 'ckv_norm_g', 'w_uq', 'w_ukv', 'mla_q_norm_g', 'mla_k_norm_g', 'w_mla_o', 'dil_q_norm_g', 'dil_k_norm_g', 'w_dil_o',
            'b_merge', 'w_out']
WEIGHTS = IN_NAMES[2:]
REPLICATED = [n for n in WEIGHTS if n not in SCATTER]
GATE_WEIGHTS = ('w_gate_x', 'w_gate_a')

_KR0 = C_KR * LANE


GATHERED = ['w_in', 'w_lru_o', 'w_uq', 'w_ukv', 'w_mla_o', 'w_dil_o', 'w_out', 'conv_w']


def _local_weights(wd, me):
    loc = {n: wd[n].astype(BF16) for n in GATHERED[:-1]}
    loc['w_in'] = _to_window(loc['w_in'], me)
    loc['w_uq'] = jnp.pad(loc['w_uq'], ((0, 0), (0, 0), (0, LANE - MLA_QK)))
    loc['conv_w'] = wd['conv_w']
    return [[loc[n][l] for n in GATHERED] for l in range(DEPTH)]


def _layer_weights(gathered, rep, l):
    gw = dict(zip(GATHERED, gathered))
    by_rows = lambda a: a.reshape(-1, a.shape[-1])
    by_cols = lambda a: jnp.swapaxes(a, 0, 1).reshape(a.shape[1], -1)
    ukv = jnp.swapaxes(gw['w_ukv'], 0, 1)
    g96 = lambda a: jnp.pad(a[l].reshape(1, MLA_QK), ((0, 0), (0, LANE - MLA_QK)))
    g64 = lambda a: jnp.tile(a[l].reshape(1, DIL_HD), (1, 2))
    return dict(
        norm_g=rep['norm_g'][l].reshape(1, D), w_in=_assemble_w_in(gw['w_in']),
        conv_w=by_cols(gw['conv_w']), conv_b=rep['conv_b'][l].reshape(1, D),
        w_gx=rep['w_gate_x'][l].astype(BF16), b_gx=rep['b_gate_x'][l].reshape(8, 1, LANE),
        w_ga=rep['w_gate_a'][l].astype(BF16), b_ga=rep['b_gate_a'][l].reshape(8, 1, LANE),
        lam=rep['lru_lambda'][l].reshape(1, D),
        w_lru_o=by_rows(gw['w_lru_o']), w_mla_o=by_cols(gw['w_mla_o']), w_dil_o=by_cols(gw['w_dil_o']), w_out=by_rows(gw['w_out']),
        g_cq=rep['cq_norm_g'][l].reshape(1, 256), g_ckv=rep['ckv_norm_g'][l].reshape(1, 128),
        w_uq=by_cols(gw['w_uq']), w_uk=jnp.pad(ukv[:, :, :64], ((0, 0), (0, 0), (0, 64))).reshape(128, 1024),
        w_uv=ukv[:, :, 64:].reshape(128, 512),
        g_mq=g96(rep['mla_q_norm_g']), g_mk=g96(rep['mla_k_norm_g']), g_dq=g64(rep['dil_q_norm_g']), g_dk=g64(rep['dil_k_norm_g']),
        b_merge=rep['b_merge'][l].reshape(1, 3 * D),
    )


def _sharded_grads(g):
    uk = g['w_uk'].reshape(128, 8, 128)[:, :, :64]
    uv = g['w_uv'].reshape(128, 8, 64)
    d = {'w_in': g['w_in'], 'conv_w': g['conv_w'], 'w_lru_o': g['w_lru_o'], 'w_uq': g['w_uq'],
         'w_ukv': jnp.concatenate([uk, uv], axis=-1).reshape(128, 1024), 'w_mla_o': g['w_mla_o'], 'w_dil_o': g['w_dil_o'],
         'w_out': g['w_out']}
    return [d[n] for n in SCATTER]


def _replicated_grads(g):
    return {
        'conv_b': g['conv_b'].reshape(D),
        'w_gate_x': g['w_gx'], 'b_gate_x': g['b_gx'].reshape(8, LANE), 'w_gate_a': g['w_ga'], 'b_gate_a': g['b_ga'].reshape(8, LANE),
        'lru_lambda': g['lam'].reshape(D), 'cq_norm_g': g['g_cq'].reshape(256), 'ckv_norm_g': g['g_ckv'].reshape(128),
        'mla_q_norm_g': g['g_mq'][0, :MLA_QK], 'mla_k_norm_g': g['g_mk'][0, :MLA_QK],
        'dil_q_norm_g': g['g_dq'][0, :DIL_HD] + g['g_dq'][0, DIL_HD:], 'dil_k_norm_g': g['g_dk'][0, :DIL_HD] + g['g_dk'][0, DIL_HD:],
        'b_merge': g['b_merge'].reshape(3 * D),
    }


def kernel(x, positions, norm_g, w_in, conv_w, conv_b, w_gate_x, b_gate_x, w_gate_a, b_gate_a, lru_lambda, w_lru_o, cq_norm_g, ckv_norm_g, w_uq, w_ukv, mla_q_norm_g, mla_k_norm_g, w_mla_o, dil_q_norm_g, dil_k_norm_g, w_dil_o, b_merge, w_out, loss_target, m_norm_g, m_w_in, m_conv_w, m_conv_b, m_w_gate_x, m_b_gate_x, m_w_gate_a, m_b_gate_a, m_lru_lambda, m_w_lru_o, m_cq_norm_g, m_ckv_norm_g, m_w_uq, m_w_ukv, m_mla_q_norm_g, m_mla_k_norm_g, m_w_mla_o, m_dil_q_norm_g, m_dil_k_norm_g, m_w_dil_o, m_b_merge, m_w_out, v_norm_g, v_w_in, v_conv_w, v_conv_b, v_w_gate_x, v_b_gate_x, v_w_gate_a, v_b_gate_a, v_lru_lambda, v_w_lru_o, v_cq_norm_g, v_ckv_norm_g, v_w_uq, v_w_ukv, v_mla_q_norm_g, v_mla_k_norm_g, v_w_mla_o, v_dil_q_norm_g, v_dil_k_norm_g, v_w_dil_o, v_b_merge, v_w_out):
    args = (x, positions, norm_g, w_in, conv_w, conv_b, w_gate_x, b_gate_x, w_gate_a, b_gate_a, lru_lambda, w_lru_o, cq_norm_g, ckv_norm_g, w_uq, w_ukv, mla_q_norm_g, mla_k_norm_g, w_mla_o, dil_q_norm_g, dil_k_norm_g, w_dil_o, b_merge, w_out)
    moments_m = (m_norm_g, m_w_in, m_conv_w, m_conv_b, m_w_gate_x, m_b_gate_x, m_w_gate_a, m_b_gate_a, m_lru_lambda, m_w_lru_o, m_cq_norm_g, m_ckv_norm_g, m_w_uq, m_w_ukv, m_mla_q_norm_g, m_mla_k_norm_g, m_w_mla_o, m_dil_q_norm_g, m_dil_k_norm_g, m_w_dil_o, m_b_merge, m_w_out)
    moments_v = (v_norm_g, v_w_in, v_conv_w, v_conv_b, v_w_gate_x, v_b_gate_x, v_w_gate_a, v_b_gate_a, v_lru_lambda, v_w_lru_o, v_cq_norm_g, v_ckv_norm_g, v_w_uq, v_w_ukv, v_mla_q_norm_g, v_mla_k_norm_g, v_w_mla_o, v_dil_q_norm_g, v_dil_k_norm_g, v_w_dil_o, v_b_merge, v_w_out)
    a = dict(zip(IN_NAMES, args))
    wd = {n: a[n] for n in WEIGHTS}
    md = dict(zip(WEIGHTS, moments_m))
    vd = dict(zip(WEIGHTS, moments_v))

    me = 4 * lax.axis_index("x") + 2 * lax.axis_index("y") + lax.axis_index("c")

    assert DEPTH == 2
    xs, tabs = x[0], _rope_tables(positions[0])
    whole = [_whole] * len(GATHERED)
    slicers = [SCATTER[n][0] for n in SCATTER]
    grad_slices = [SCATTER[n][1:] for n in SCATTER]

    local = _local_weights(wd, me)
    w_slices = [(a.shape, a.dtype) for a in local[0]]
    landed0 = _gather_two_level(local[0], "gather_w0")
    flying = _exchange_start(local[1], whole, w_slices, landed0[0], "gather_w1_start")
    rep0 = dict(wd, norm_g=wd['norm_g'] + flying[-1][0, 0])
    w0 = _layer_weights(landed0, rep0, 0)
    x1, saved0 = _layer_fwd(xs, w0, tabs)
    w1 = _layer_weights(_exchange_wait(flying, whole, x1, "gather_w1_wait"), wd, 1)
    x2, saved1 = _layer_fwd(x1, w1, tabs)
    loss, dx2 = _loss_fwd_bwd(x2, loss_target[0])
    loss = loss[0, 0]

    sharded = list(SCATTER)
    nsh = len(sharded)
    small = [n for n in REPLICATED if n not in GATE_WEIGHTS and n != 'norm_g']

    def outgoing(g):
        r = _replicated_grads(g)
        return (_sharded_grads(g) + [_pack([r[n] for n in small], F32, 0)]
                + [r[n].astype(BF16).reshape(8 * LANE, LANE) for n in GATE_WEIGHTS])

    out_slicers = slicers + [_whole] * 3
    out_slices = grad_slices + [((_packed_rows([wd[n].shape[1:] for n in small]), LANE), F32)] + [((8 * LANE, LANE), BF16)] * 2
    dx1, g1 = _layer_bwd(dx2, w1, tabs, saved1)
    flying1 = _exchange_start(outgoing(g1), out_slicers, out_slices, dx1, "scatter_g1_start")
    later = {}

    def send_layer0(g):
        later['got1'] = _exchange_wait(flying1, out_slicers, g['w_in'], "scatter_g1_wait")
        later['flying0'] = _exchange_start(outgoing(g), out_slicers, out_slices, later['got1'][0], "scatter_g0_start")
        return later['flying0'][-1]

    grad_x, g0 = _layer_bwd(dx1, w0, tabs, saved0, hook=send_layer0, after=flying1[-1])
    names = sharded + ['small'] + list(GATE_WEIGHTS)
    sum1 = [_sum8(b, f"sum_{n}_1") for n, b in zip(names, later['got1'])]
    got0 = _exchange_wait(later['flying0'], out_slicers, sum1[0], "scatter_g0_wait")
    sum0 = [_sum8(b, f"sum_{n}_0") for n, b in zip(names, got0)]
    norm_part = _pack([jnp.stack([g['norm_g'].reshape(D) for g in (g0, g1)])], F32, 0)
    norm_sum = _sum8(_exchange([norm_part], [_whole], [(norm_part.shape, F32)], "gather_norm_g")[0], "sum_norm_g")

    gsh = {n: jnp.stack([sum0[i], sum1[i]]) for i, n in enumerate(sharded)}
    gsh['w_in'] = _from_window(gsh['w_in'], me)
    gsh['w_uq'] = gsh['w_uq'][:, :, :MLA_QK]
    grep = {'norm_g': _unpack(norm_sum, [wd['norm_g'].shape], 0)[0]}
    per_layer = [_unpack(s[nsh], [wd[n].shape[1:] for n in small], 0) for s in (sum0, sum1)]
    grep.update({n: jnp.stack([per_layer[l][i] for l in range(DEPTH)]) for i, n in enumerate(small)})
    for i, n in enumerate(GATE_WEIGHTS):
        grep[n] = jnp.stack([sum0[nsh + 1 + i], sum1[nsh + 1 + i]]).reshape(wd[n].shape)

    out_g, out_d, out_m, out_v = {}, {}, {}, {}
    vecs = ['norm_g'] + small
    vshapes = [wd[n].shape for n in vecs]
    packed_g = _pack([grep[n] for n in vecs], F32, 0)
    d_, m_, v_ = _adamw(_pack([wd[n] for n in vecs], F32, 0), packed_g, _pack([md[n] for n in vecs], F32, 0),
                        _pack([vd[n] for n in vecs], F32, 0), "adamw_vectors")
    for dst, buf in ((out_d, d_), (out_m, m_), (out_v, v_)):
        dst.update(zip(vecs, _unpack(buf, vshapes, 0)))
    out_g.update({n: grep[n] for n in vecs})
    gsh.update({n: grep[n] for n in GATE_WEIGHTS})
    for n in sharded + list(GATE_WEIGHTS):
        if n in GATE_WEIGHTS:
            shp = wd[n].shape
            two = (shp[0] * shp[1] * shp[2], shp[3])
            d_, m_, v_ = _adamw(wd[n].reshape(two), gsh[n].reshape(two), md[n].reshape(two), vd[n].reshape(two), "adamw_" + n)
            out_g[n], out_d[n], out_m[n], out_v[n] = gsh[n], d_.reshape(shp), m_.reshape(shp), v_.reshape(shp)
            continue
        shp = wd[n].shape
        two = (shp[0] * shp[1], shp[2])
        d_, m_, v_ = _adamw(wd[n].reshape(two), gsh[n].reshape(two), md[n].reshape(two), vd[n].reshape(two), "adamw_" + n)
        out_g[n], out_d[n], out_m[n], out_v[n] = gsh[n], d_.reshape(shp), m_.reshape(shp), v_.reshape(shp)

    loss = lax.psum(loss, ("x", "y", "c"))
    return (loss, grad_x[None], *[out_g[n] for n in WEIGHTS], *[out_d[n] for n in WEIGHTS], *[out_m[n] for n in WEIGHTS],
            *[out_v[n] for n in WEIGHTS])
```

```python
import functools

import numpy as np
import jax
import jax.numpy as jnp
from jax import lax
from jax.experimental import pallas as pl
from jax.experimental.pallas import tpu as pltpu

F32 = jnp.float32
BF16 = jnp.bfloat16

N_DEV = 8
D = 1024
DEPTH = 2
EPS = 1e-6
ROPE_THETA = 10000.0
LRU_C = 8.0
LANE = 128
SUB = 8
IN_WIDTH = 11168
SHARD_IN = IN_WIDTH // N_DEV

C_LRUX, C_LRUG, C_CQ, C_CKV, C_KR, C_MLAG, C_DQ, C_DK, C_DV, C_DILG, C_MERGE = 0, 8, 16, 18, 19, 20, 24, 36, 48, 60, 64
ZW = 88 * LANE
KR_LANE = 64

MLA_QK = 96
MLA_SCALE = MLA_QK ** -0.5
DIL_HD = 64
DIL_SCALE = DIL_HD ** -0.5
DIL_DILATIONS = (1, 4, 16)
NK = 128

ADAM_LR, ADAM_B1, ADAM_B2, ADAM_EPS, ADAM_WD, ADAM_STEP = 0.001, 0.9, 0.999, 1e-08, 0.01, 10

NEG = -1e30
VMEM_LIMIT = 48 * 1024 * 1024


def _cp(**kw):
    return pltpu.CompilerParams(vmem_limit_bytes=VMEM_LIMIT, **kw)


def _sig(x):
    return 1.0 / (1.0 + jnp.exp(-x))


def _silu(x):
    return x * _sig(x)


def _dsilu(x):
    s = _sig(x)
    return s * (1.0 + x * (1.0 - s))


def _dot(a, b, dims):
    return lax.dot_general(a, b, (dims, ((), ())), preferred_element_type=F32)


def _nn(a, b):
    return _dot(a, b, ((1,), (0,)))


def _nt(a, b):
    return _dot(a, b, ((1,), (1,)))


def _tn(a, b):
    return _dot(a, b, ((0,), (0,)))


def _rsum(x):
    return jnp.sum(x, axis=-1, keepdims=True)


def _csum(x):
    return jnp.sum(x, axis=0, keepdims=True)


def _mm(a, b, *, mode, name, out_dtype=F32, add=None, after=None, tm=1024, tn=1024, tk=1024):
    if mode == "nn":
        (M, K), (K2, N) = a.shape, b.shape
    elif mode == "nt":
        (M, K), (N, K2) = a.shape, b.shape
    else:
        (K, M), (K2, N) = a.shape, b.shape
    assert K == K2
    tm, tn, tk = min(tm, M), min(tn, N), min(tk, K)
    assert M % tm == 0 and N % tn == 0 and K % tk == 0
    nk = K // tk
    fn = {"nn": _nn, "nt": _nt, "tn": _tn}[mode]
    has_add = add is not None

    def body(*refs):
        a_ref, b_ref = refs[0], refs[1]
        add_ref = refs[2] if has_add else None
        o_ref = refs[2 + has_add + (after is not None)]
        part = fn(a_ref[...].astype(BF16), b_ref[...].astype(BF16))

        def fin(acc):
            if has_add:
                acc = acc + add_ref[...]
            o_ref[...] = acc.astype(out_dtype)

        if nk == 1:
            fin(part)
        else:
            acc_ref = refs[-1]
            k = pl.program_id(2)

            @pl.when(k == 0)
            def _():
                acc_ref[...] = part

            @pl.when(k > 0)
            def _():
                acc_ref[...] += part

            @pl.when(k == nk - 1)
            def _():
                fin(acc_ref[...])

    a_spec = pl.BlockSpec((tk, tm), lambda i, j, k: (k, i)) if mode == "tn" else pl.BlockSpec((tm, tk), lambda i, j, k: (i, k))
    b_spec = pl.BlockSpec((tn, tk), lambda i, j, k: (j, k)) if mode == "nt" else pl.BlockSpec((tk, tn), lambda i, j, k: (k, j))
    o_spec = pl.BlockSpec((tm, tn), lambda i, j, k: (i, j))
    in_specs, args = [a_spec, b_spec], [a, b]
    if has_add:
        in_specs.append(o_spec)
        args.append(add)
    if after is not None:
        in_specs.append(pl.BlockSpec(memory_space=pl.ANY))
        args.append(after)
    return pl.pallas_call(
        body, name=name, grid=(M // tm, N // tn, nk), in_specs=in_specs, out_specs=o_spec,
        out_shape=jax.ShapeDtypeStruct((M, N), out_dtype),
        scratch_shapes=[pltpu.VMEM((tm, tn), F32)] if nk > 1 else [],
        compiler_params=_cp(dimension_semantics=("parallel", "parallel", "arbitrary")),
    )(*args)


T_ROW = 512


def _rms_in_fwd(x, g):
    S = x.shape[0]
    T = T_ROW

    def body(x_ref, g_ref, h_ref):
        xv = x_ref[...]
        r = lax.rsqrt(jnp.mean(xv * xv, axis=-1, keepdims=True) + EPS)
        h_ref[...] = (xv * r * g_ref[...]).astype(BF16)

    return pl.pallas_call(
        body, name="rms_in_fwd", grid=(S // T,),
        in_specs=[pl.BlockSpec((T, D), lambda i: (i, 0)), pl.BlockSpec((1, D), lambda i: (0, 0))],
        out_specs=pl.BlockSpec((T, D), lambda i: (i, 0)),
        out_shape=jax.ShapeDtypeStruct((S, D), BF16), compiler_params=_cp(),
    )(x, g)


def _rms_in_bwd(x, g, dh, dres):
    S = x.shape[0]
    T = T_ROW

    def body(x_ref, g_ref, dh_ref, dr_ref, dx_ref, dg_ref):
        i = pl.program_id(0)
        xv = x_ref[...]
        r = lax.rsqrt(jnp.mean(xv * xv, axis=-1, keepdims=True) + EPS)
        xn = xv * r
        dy = dh_ref[...]
        part = _csum(dy * xn)

        @pl.when(i == 0)
        def _():
            dg_ref[...] = part

        @pl.when(i > 0)
        def _():
            dg_ref[...] += part

        dxh = dy * g_ref[...]
        dx_ref[...] = dr_ref[...] + r * (dxh - xn * jnp.mean(dxh * xn, axis=-1, keepdims=True))

    row = pl.BlockSpec((T, D), lambda i: (i, 0))
    vec = pl.BlockSpec((1, D), lambda i: (0, 0))
    return pl.pallas_call(
        body, name="rms_in_bwd", grid=(S // T,), in_specs=[row, vec, row, row], out_specs=[row, vec],
        out_shape=[jax.ShapeDtypeStruct((S, D), F32), jax.ShapeDtypeStruct((1, D), F32)], compiler_params=_cp(),
    )(x, g, dh, dres)


T_LRU = 512


def _neg_expm1(y):
    ser = -y * (1.0 + y * 0.5 * (1.0 + y * (1.0 / 3.0) * (1.0 + y * 0.25 * (1.0 + y * 0.2))))
    return jnp.where(y > -0.03, ser, 1.0 - jnp.exp(y))


def _softplus_neg(lam):
    e = jnp.exp(-jnp.abs(lam))
    l1p = jnp.where(e < 0.01, e * (1.0 - e * (0.5 - e * (1.0 / 3.0 - e * 0.25))), jnp.log(1.0 + e))
    return jnp.maximum(-lam, 0.0) + l1p


def _scan_fwd(a, b, T):
    row = lax.broadcasted_iota(jnp.int32, a.shape, 0)
    d = 1
    while d < T:
        m = row >= d
        b = jnp.where(m, a * pltpu.roll(b, d, 0) + b, b)
        a = jnp.where(m, a * pltpu.roll(a, d, 0), a)
        d *= 2
    return a, b


def _scan_bwd(a, b, T):
    row = lax.broadcasted_iota(jnp.int32, a.shape, 0)
    d = 1
    while d < T:
        m = row < T - d
        b = jnp.where(m, a * pltpu.roll(b, T - d, 0) + b, b)
        a = jnp.where(m, a * pltpu.roll(a, T - d, 0), a)
        d *= 2
    return b


def _lru_common(x, prev, first, cw_ref, cb_ref, wgx_ref, bgx_ref, wga_ref, bga_ref, lam_ref, T):
    row = lax.broadcasted_iota(jnp.int32, x.shape, 0)
    prev = jnp.where(first, 0.0, prev)
    xs = []
    for j in (3, 2, 1):
        pv = jnp.tile(pltpu.roll(prev, j, 0), (T // SUB, 1))
        xs.append(jnp.where(row < j, pv, pltpu.roll(x, j, 0)))
    xs.append(x)
    xc = cb_ref[...] + cw_ref[0:1, :] * xs[0] + cw_ref[1:2, :] * xs[1] + cw_ref[2:3, :] * xs[2] + cw_ref[3:4, :] * xs[3]
    xcb = xc.astype(BF16)
    gx = _sig(_nn(xcb, wgx_ref[0]) + bgx_ref[0])
    ga = _sig(_nn(xcb, wga_ref[0]) + bga_ref[0])
    sp = _softplus_neg(lam_ref[...])
    log_a = -LRU_C * ga * sp
    a = jnp.exp(log_a)
    mult = jnp.sqrt(_neg_expm1(2.0 * log_a))
    return xs, xc, xcb, gx, ga, sp, a, mult


def _lru_specs(T, tmap):
    def at(col0):
        return pl.BlockSpec((T, LANE), lambda n, i: (tmap(i), col0 + n))

    def prev(col0):
        return pl.BlockSpec((SUB, LANE), lambda n, i: (jnp.maximum(tmap(i) * (T // SUB) - 1, 0), col0 + n))

    small = [
        pl.BlockSpec((4, LANE), lambda n, i: (0, n)),
        pl.BlockSpec((1, LANE), lambda n, i: (0, n)),
        pl.BlockSpec((1, LANE, LANE), lambda n, i: (n, 0, 0)),
        pl.BlockSpec((1, 1, LANE), lambda n, i: (n, 0, 0)),
        pl.BlockSpec((1, LANE, LANE), lambda n, i: (n, 0, 0)),
        pl.BlockSpec((1, 1, LANE), lambda n, i: (n, 0, 0)),
        pl.BlockSpec((1, LANE), lambda n, i: (0, n)),
    ]
    return at, prev, small


def _lru_fwd(zp, w):
    S = zp.shape[0]
    T = T_LRU
    at, prev, small = _lru_specs(T, lambda i: i)

    def body(x_ref, xp_ref, g_ref, cw_ref, cb_ref, wgx_ref, bgx_ref, wga_ref, bga_ref, lam_ref, hs_ref, y_ref, carry_ref):
        i = pl.program_id(1)

        @pl.when(i == 0)
        def _():
            carry_ref[...] = jnp.zeros_like(carry_ref)

        x = x_ref[...]
        _, xc, _, gx, _, _, a, mult = _lru_common(x, xp_ref[...], i == 0, cw_ref, cb_ref, wgx_ref, bgx_ref, wga_ref, bga_ref, lam_ref, T)
        A, B = _scan_fwd(a, mult * gx * xc, T)
        h = B + A * carry_ref[SUB - 1:SUB, :]
        hs_ref[...] = h
        carry_ref[...] = hs_ref[T - SUB:T, :]
        y_ref[...] = (h * _silu(g_ref[...])).astype(BF16)

    out = pl.BlockSpec((T, LANE), lambda n, i: (i, n))
    return pl.pallas_call(
        body, name="lru_fwd", grid=(8, S // T),
        in_specs=[at(C_LRUX), prev(C_LRUX), at(C_LRUG)] + small, out_specs=[out, out],
        out_shape=[jax.ShapeDtypeStruct((S, D), F32), jax.ShapeDtypeStruct((S, D), BF16)],
        scratch_shapes=[pltpu.VMEM((SUB, LANE), F32)],
        compiler_params=_cp(dimension_semantics=("parallel", "arbitrary")),
    )(zp, zp, zp, w["conv_w"], w["conv_b"], w["w_gx"], w["b_gx"], w["w_ga"], w["b_ga"], w["lam"])


def _lru_bwd(zp, hs, dy, w, dz):
    S = zp.shape[0]
    T = T_LRU
    nT = S // T
    at, prev, small = _lru_specs(T, lambda i: nT - 1 - i)

    def body(x_ref, xp_ref, g_ref, h_ref, hp_ref, dy_ref, cw_ref, cb_ref, wgx_ref, bgx_ref, wga_ref, bga_ref, lam_ref, dz_in,
             dzx_ref, dcw_ref, dcb_ref, dwgx_ref, dbgx_ref, dwga_ref, dbga_ref, dlam_ref, carry_ref, head_ref):
        del dz_in
        j = pl.program_id(1)
        it = nT - 1 - j

        @pl.when(j == 0)
        def _():
            for r in (carry_ref, head_ref, dcw_ref, dcb_ref, dwgx_ref, dbgx_ref, dwga_ref, dbga_ref, dlam_ref):
                r[...] = jnp.zeros_like(r)

        first = it == 0
        x = x_ref[...]
        xs, xc, xcb, gx, ga, sp, a, mult = _lru_common(x, xp_ref[...], first, cw_ref, cb_ref, wgx_ref, bgx_ref, wga_ref, bga_ref, lam_ref, T)
        row = lax.broadcasted_iota(jnp.int32, x.shape, 0)
        u = gx * xc
        h = h_ref[...]
        hp = jnp.where(first, 0.0, hp_ref[...])
        hm1 = jnp.where(row < 1, jnp.tile(pltpu.roll(hp, 1, 0), (T // SUB, 1)), pltpu.roll(h, 1, 0))
        dho = dy_ref[...] * _silu(g_ref[...])
        gin = jnp.where(row == T - 1, dho + carry_ref[0:1, :], dho)
        abar = jnp.where(row == T - 1, 0.0, pltpu.roll(a, T - 1, 0))
        dh = _scan_bwd(abar, gin, T)
        carry_ref[...] = (a * dh)[0:SUB, :]
        da = dh * hm1
        dmult = dh * u
        du = dh * mult
        dgx = du * xc
        dxc = du * gx
        dlog_a = da * a - dmult * a * a / mult
        dga = dlog_a * (-LRU_C * sp)
        lam = lam_ref[...]
        dlam_ref[...] += _csum(dlog_a * (-LRU_C * ga)) * (-1.0 / (1.0 + jnp.exp(lam)))
        dpa = dga * ga * (1.0 - ga)
        dpx = dgx * gx * (1.0 - gx)
        dpab, dpxb = dpa.astype(BF16), dpx.astype(BF16)
        dxc = dxc + _nt(dpxb, wgx_ref[0]) + _nt(dpab, wga_ref[0])
        dwgx_ref[0] += _tn(xcb, dpxb)
        dwga_ref[0] += _tn(xcb, dpab)
        dbgx_ref[0] += _csum(dpx)
        dbga_ref[0] += _csum(dpa)
        dcb_ref[...] += _csum(dxc)
        for k in range(4):
            dcw_ref[k:k + 1, :] += _csum(dxc * xs[k])
        head = head_ref[...]
        dx = cw_ref[3:4, :] * dxc
        for jj in (1, 2, 3):
            hv = jnp.tile(pltpu.roll(head, SUB - jj, 0), (T // SUB, 1))
            dx = dx + cw_ref[3 - jj:4 - jj, :] * jnp.where(row >= T - jj, hv, pltpu.roll(dxc, T - jj, 0))
        head_ref[...] = dxc[0:SUB, :]
        dzx_ref[...] = dx.astype(BF16)

    def acc(shape, imap):
        return pl.BlockSpec(shape, imap)

    out_specs = [
        pl.BlockSpec((T, LANE), lambda n, i: (nT - 1 - i, C_LRUX + n)),
        acc((4, LANE), lambda n, i: (0, n)), acc((1, LANE), lambda n, i: (0, n)),
        acc((1, LANE, LANE), lambda n, i: (n, 0, 0)), acc((1, 1, LANE), lambda n, i: (n, 0, 0)),
        acc((1, LANE, LANE), lambda n, i: (n, 0, 0)), acc((1, 1, LANE), lambda n, i: (n, 0, 0)),
        acc((1, LANE), lambda n, i: (0, n)),
    ]
    out_shape = [
        jax.ShapeDtypeStruct(dz.shape, BF16),
        jax.ShapeDtypeStruct((4, D), F32), jax.ShapeDtypeStruct((1, D), F32),
        jax.ShapeDtypeStruct((8, LANE, LANE), F32), jax.ShapeDtypeStruct((8, 1, LANE), F32),
        jax.ShapeDtypeStruct((8, LANE, LANE), F32), jax.ShapeDtypeStruct((8, 1, LANE), F32),
        jax.ShapeDtypeStruct((1, D), F32),
    ]
    dyspec = pl.BlockSpec((T, LANE), lambda n, i: (nT - 1 - i, n))
    hprev = pl.BlockSpec((SUB, LANE), lambda n, i: (jnp.maximum((nT - 1 - i) * (T // SUB) - 1, 0), n))
    return pl.pallas_call(
        body, name="lru_bwd", grid=(8, nT),
        in_specs=[at(C_LRUX), prev(C_LRUX), at(C_LRUG), dyspec, hprev, dyspec] + small + [pl.BlockSpec(memory_space=pl.ANY)],
        out_specs=out_specs, out_shape=out_shape,
        scratch_shapes=[pltpu.VMEM((SUB, LANE), F32), pltpu.VMEM((SUB, LANE), F32)],
        input_output_aliases={13: 0},
        compiler_params=_cp(dimension_semantics=("parallel", "arbitrary")),
    )(zp, zp, zp, hs, hs, dy, w["conv_w"], w["conv_b"], w["w_gx"], w["b_gx"], w["w_ga"], w["b_ga"], w["lam"], dz)


def _lru_gate_bwd(zp, hs, dy, dz):
    S = zp.shape[0]
    T = T_ROW

    def body(g_ref, h_ref, dy_ref, dz_in, o_ref):
        del dz_in
        o_ref[...] = (dy_ref[...] * h_ref[...] * _dsilu(g_ref[...])).astype(BF16)

    row = pl.BlockSpec((T, D), lambda i: (i, 0))
    zc = pl.BlockSpec((T, D), lambda i: (i, C_LRUG // 8))
    return pl.pallas_call(
        body, name="lru_gate_bwd", grid=(S // T,), in_specs=[zc, row, row, pl.BlockSpec(memory_space=pl.ANY)], out_specs=zc,
        out_shape=jax.ShapeDtypeStruct(dz.shape, BF16), input_output_aliases={3: 0}, compiler_params=_cp(),
    )(zp, hs, dy, dz)


def _rope_tables(pos):
    pf = pos.astype(F32)[:, None]

    def cs(d):
        inv = ROPE_THETA ** (-jnp.arange(0, d, 2, dtype=F32) / d)
        ang = pf * inv
        return jnp.cos(ang), jnp.sin(ang)

    S = pos.shape[0]
    c, s = cs(32)
    one, zero = jnp.ones((S, 64), F32), jnp.zeros((S, 16), F32)
    z32, z64 = jnp.zeros((S, 32), F32), jnp.zeros((S, 64), F32)
    mla = (jnp.concatenate([one, c, c, jnp.ones((S, 32), F32)], 1),
           jnp.concatenate([z64, zero, s, z32], 1),
           jnp.concatenate([z64, -s, zero, z32], 1))
    c, s = cs(64)
    dil = (jnp.concatenate([c, c, c, c], 1),
           jnp.concatenate([z32, s, z32, s], 1),
           jnp.concatenate([-s, z32, -s, z32], 1))
    return mla, dil


def _rope(x, C, S1, S2, sh):
    return x * C + pltpu.roll(x, sh, 1) * S1 + pltpu.roll(x, LANE - sh, 1) * S2


def _rope_t(dy, C, S1, S2, sh):
    return dy * C + pltpu.roll(dy * S1, LANE - sh, 1) + pltpu.roll(dy * S2, sh, 1)


def _lane(shape):
    return lax.broadcasted_iota(jnp.int32, shape, 1)


T_MLA = 256
TA = 512


def _zcol(T, width, col_lanes):
    assert (col_lanes * LANE) % width == 0
    return pl.BlockSpec((T, width), lambda i: (i, col_lanes * LANE // width))


def _full(shape):
    return pl.BlockSpec(shape, lambda *_: (0,) * len(shape))


def _mla_pre_fwd(zp, w, tab):
    S = zp.shape[0]
    T = T_MLA

    def body(cq_ref, ckv_ref, kr_ref, gcq_ref, gckv_ref, wuq_ref, wuk_ref, wuv_ref, gq_ref, gk_ref, C_ref, S1_ref, S2_ref,
             q_ref, k_ref, v_ref):
        cq = cq_ref[...]
        cqn = (cq * lax.rsqrt(jnp.mean(cq * cq, axis=-1, keepdims=True) + EPS) * gcq_ref[...]).astype(BF16)
        ckv = ckv_ref[...]
        ckvn = (ckv * lax.rsqrt(jnp.mean(ckv * ckv, axis=-1, keepdims=True) + EPS) * gckv_ref[...]).astype(BF16)
        q0 = _nn(cqn, wuq_ref[...])
        k0 = _nn(ckvn, wuk_ref[...])
        krb = kr_ref[...]
        C, S1, S2 = C_ref[...], S1_ref[...], S2_ref[...]
        for h in range(8):
            sl = slice(h * LANE, (h + 1) * LANE)
            xq = q0[:, sl]
            xq = xq * lax.rsqrt(_rsum(xq * xq) * (1.0 / MLA_QK) + EPS) * gq_ref[...]
            q_ref[:, sl] = _rope(xq, C, S1, S2, 16).astype(BF16)
            xk = k0[:, sl] + krb
            xk = xk * lax.rsqrt(_rsum(xk * xk) * (1.0 / MLA_QK) + EPS) * gk_ref[...]
            k_ref[:, sl] = _rope(xk, C, S1, S2, 16).astype(BF16)
        v_ref[...] = _nn(ckvn, wuv_ref[...]).astype(BF16)

    tabspec = pl.BlockSpec((T, LANE), lambda i: (i, 0))
    in_specs = [_zcol(T, 256, C_CQ), _zcol(T, LANE, C_CKV), _zcol(T, LANE, C_KR), _full((1, 256)), _full((1, LANE)),
                _full((256, 1024)), _full((LANE, 1024)), _full((LANE, 512)), _full((1, LANE)), _full((1, LANE)),
                tabspec, tabspec, tabspec]
    return pl.pallas_call(
        body, name="mla_pre_fwd", grid=(S // T,), in_specs=in_specs,
        out_specs=[pl.BlockSpec((T, 1024), lambda i: (i, 0)), pl.BlockSpec((T, 1024), lambda i: (i, 0)), pl.BlockSpec((T, 512), lambda i: (i, 0))],
        out_shape=[jax.ShapeDtypeStruct((S, 1024), BF16), jax.ShapeDtypeStruct((S, 1024), BF16), jax.ShapeDtypeStruct((S, 512), BF16)],
        compiler_params=_cp(),
    )(zp, zp, zp, w["g_cq"], w["g_ckv"], w["w_uq"], w["w_uk"], w["w_uv"], w["g_mq"], w["g_mk"], *tab)


def _mla_attn_fwd(q, k, v, zp):
    S = q.shape[0]
    nq = S // TA

    def body(q_ref, k_ref, v_ref, g_ref, o_ref, lse_ref, y_ref):
        qi = pl.program_id(1)
        lane = _lane((TA, LANE))
        rowi = lax.broadcasted_iota(jnp.int32, (TA, TA), 0)
        coli = lax.broadcasted_iota(jnp.int32, (TA, TA), 1)
        o_tot = jnp.zeros((TA, LANE), F32)
        for hh in range(2):
            cs = slice(hh * LANE, (hh + 1) * LANE)
            hm = (lane < 64) if hh == 0 else (lane >= 64)
            qh = q_ref[:, cs]

            def step(kb, carry, masked, cs=cs, hm=hm, qh=qh):
                m, l, acc = carry
                off = pl.multiple_of(kb * TA, TA)
                kh = k_ref[pl.ds(off, TA), cs]
                vv = v_ref[pl.ds(off, TA), :]
                vh = jnp.where(hm, vv, jnp.zeros_like(vv))
                s = _nt(qh, kh) * MLA_SCALE
                if masked:
                    s = jnp.where(rowi >= coli, s, NEG)
                m_new = jnp.maximum(m, jnp.max(s, axis=-1, keepdims=True))
                alpha = jnp.exp(m - m_new)
                p = jnp.exp(s - m_new)
                l = alpha * l + _rsum(p)
                acc = alpha * acc + _nn(p.astype(BF16), vh)
                return m_new, l, acc

            init = (jnp.full((TA, 1), NEG, F32), jnp.zeros((TA, 1), F32), jnp.zeros((TA, LANE), F32))
            carry = lax.fori_loop(0, qi, lambda kb, c: step(kb, c, False), init)
            m, l, acc = step(qi, carry, True)
            o_tot = o_tot + acc / l
            lse_ref[:, cs] = jnp.broadcast_to(m + jnp.log(l), (TA, LANE))
        o_ref[...] = o_tot
        y_ref[...] = (o_tot * _silu(g_ref[...])).astype(BF16)

    blk = pl.BlockSpec((TA, LANE), lambda p, i: (i, p))
    return pl.pallas_call(
        body, name="mla_attn_fwd", grid=(4, nq),
        in_specs=[pl.BlockSpec((TA, 256), lambda p, i: (i, p)), pl.BlockSpec((S, 256), lambda p, i: (0, p)),
                  pl.BlockSpec((S, LANE), lambda p, i: (0, p)), pl.BlockSpec((TA, LANE), lambda p, i: (i, C_MLAG + p))],
        out_specs=[blk, pl.BlockSpec((TA, 256), lambda p, i: (i, p)), blk],
        out_shape=[jax.ShapeDtypeStruct((S, 512), F32), jax.ShapeDtypeStruct((S, 1024), F32), jax.ShapeDtypeStruct((S, 512), BF16)],
        compiler_params=_cp(dimension_semantics=("parallel", "arbitrary")),
    )(q, k, v, zp)


def _mla_post_bwd(zp, o, dy, dz):
    S = zp.shape[0]
    T = T_ROW

    def body(g_ref, o_ref, dy_ref, dz_in, dz_ref, do_ref, D_ref):
        del dz_in
        g, o_, dy_ = g_ref[...], o_ref[...], dy_ref[...]
        do = dy_ * _silu(g)
        do_ref[...] = do.astype(BF16)
        dz_ref[...] = (dy_ * o_ * _dsilu(g)).astype(BF16)
        prod = do * o_
        lane = _lane((T, LANE))
        for p in range(4):
            pr = prod[:, p * LANE:(p + 1) * LANE]
            da = _rsum(jnp.where(lane < 64, pr, 0.0))
            db = _rsum(jnp.where(lane >= 64, pr, 0.0))
            D_ref[:, 2 * p * LANE:(2 * p + 1) * LANE] = jnp.broadcast_to(da, (T, LANE))
            D_ref[:, (2 * p + 1) * LANE:(2 * p + 2) * LANE] = jnp.broadcast_to(db, (T, LANE))

    row = pl.BlockSpec((T, 512), lambda i: (i, 0))
    zc = _zcol(T, 512, C_MLAG)
    return pl.pallas_call(
        body, name="mla_post_bwd", grid=(S // T,), in_specs=[zc, row, row, pl.BlockSpec(memory_space=pl.ANY)],
        out_specs=[zc, row, pl.BlockSpec((T, 1024), lambda i: (i, 0))],
        out_shape=[jax.ShapeDtypeStruct(dz.shape, BF16), jax.ShapeDtypeStruct((S, 512), BF16), jax.ShapeDtypeStruct((S, 1024), F32)],
        input_output_aliases={3: 0}, compiler_params=_cp(),
    )(zp, o, dy, dz)


def _mla_attn_bwd(q, k, v, do, lse, Dr):
    S = q.shape[0]
    nq = S // TA

    def body(q_ref, do_ref, lse_ref, D_ref, k_ref, v_ref, dq_ref, dk_ref, dv_ref):
        ki = pl.program_id(1)

        @pl.when(ki == 0)
        def _():
            dq_ref[...] = jnp.zeros_like(dq_ref)

        lane = _lane((TA, LANE))
        rowi = lax.broadcasted_iota(jnp.int32, (TA, TA), 0)
        coli = lax.broadcasted_iota(jnp.int32, (TA, TA), 1)
        dv_tot = jnp.zeros((TA, LANE), F32)
        for hh in range(2):
            cs = slice(hh * LANE, (hh + 1) * LANE)
            hm = (lane < 64) if hh == 0 else (lane >= 64)
            kh = k_ref[:, cs]
            vv = v_ref[...]
            vm = jnp.where(hm, vv, jnp.zeros_like(vv))

            def step(qb, carry, masked, cs=cs, kh=kh, vm=vm):
                dk_acc, dv_acc = carry
                off = pl.multiple_of(qb * TA, TA)
                qh = q_ref[pl.ds(off, TA), cs]
                doh = do_ref[pl.ds(off, TA), :]
                ls = jnp.tile(lse_ref[pl.ds(off, TA), cs], (1, TA // LANE))
                dd = jnp.tile(D_ref[pl.ds(off, TA), cs], (1, TA // LANE))
                s = _nt(qh, kh) * MLA_SCALE
                if masked:
                    s = jnp.where(rowi >= coli, s, NEG)
                p = jnp.exp(s - ls)
                dp = _nt(doh, vm)
                ds = (p * (dp - dd) * MLA_SCALE).astype(BF16)
                dv_acc = dv_acc + _tn(p.astype(BF16), doh)
                dk_acc = dk_acc + _tn(ds, qh)
                dq_ref[pl.ds(off, TA), cs] += _nn(ds, kh)
                return dk_acc, dv_acc

            z = jnp.zeros((TA, LANE), F32)
            carry = step(ki, (z, z), True)
            dk_acc, dv_acc = lax.fori_loop(ki + 1, nq, lambda qb, c: step(qb, c, False), carry)
            dk_ref[:, cs] = dk_acc
            dv_tot = dv_tot + jnp.where(hm, dv_acc, 0.0)
        dv_ref[...] = dv_tot

    pair = pl.BlockSpec((S, 256), lambda p, i: (0, p))
    return pl.pallas_call(
        body, name="mla_attn_bwd", grid=(4, nq),
        in_specs=[pair, pl.BlockSpec((S, LANE), lambda p, i: (0, p)), pair, pair,
                  pl.BlockSpec((TA, 256), lambda p, i: (i, p)), pl.BlockSpec((TA, LANE), lambda p, i: (i, p))],
        out_specs=[pair, pl.BlockSpec((TA, 256), lambda p, i: (i, p)), pl.BlockSpec((TA, LANE), lambda p, i: (i, p))],
        out_shape=[jax.ShapeDtypeStruct((S, 1024), F32), jax.ShapeDtypeStruct((S, 1024), F32), jax.ShapeDtypeStruct((S, 512), F32)],
        compiler_params=_cp(dimension_semantics=("parallel", "arbitrary")),
    )(q, do, lse, Dr, k, v)


def _mla_pre_bwd(zp, dq, dk, dv, w, tab, dz):
    S = zp.shape[0]
    T = T_MLA

    def body(cq_ref, ckv_ref, kr_ref, dq_ref, dk_ref, dv_ref, gcq_ref, gckv_ref, wuq_ref, wuk_ref, wuv_ref, gq_ref, gk_ref,
             C_ref, S1_ref, S2_ref, dz_in, dz_ref, dwuq_ref, dwuk_ref, dwuv_ref, dgcq_ref, dgckv_ref, dgq_ref, dgk_ref):
        del dz_in
        i = pl.program_id(0)

        @pl.when(i == 0)
        def _():
            for r in (dwuq_ref, dwuk_ref, dwuv_ref, dgcq_ref, dgckv_ref, dgq_ref, dgk_ref):
                r[...] = jnp.zeros_like(r)

        cq = cq_ref[...]
        rq = lax.rsqrt(jnp.mean(cq * cq, axis=-1, keepdims=True) + EPS)
        cqh = cq * rq
        cqn = (cqh * gcq_ref[...]).astype(BF16)
        ckv = ckv_ref[...]
        rkv = lax.rsqrt(jnp.mean(ckv * ckv, axis=-1, keepdims=True) + EPS)
        ckvh = ckv * rkv
        ckvn = (ckvh * gckv_ref[...]).astype(BF16)
        q0 = _nn(cqn, wuq_ref[...])
        k0 = _nn(ckvn, wuk_ref[...])
        krb = kr_ref[...]
        C, S1, S2 = C_ref[...], S1_ref[...], S2_ref[...]
        gq, gk = gq_ref[...], gk_ref[...]

        def head_bwd(x, dy, g):
            r = lax.rsqrt(_rsum(x * x) * (1.0 / MLA_QK) + EPS)
            xn = x * r
            dyn = _rope_t(dy, C, S1, S2, 16)
            dxh = dyn * g
            return r * (dxh - xn * _rsum(dxh * xn) * (1.0 / MLA_QK)), _csum(dyn * xn)

        dq0, dk0 = [], []
        dgq_acc = jnp.zeros((1, LANE), F32)
        dgk_acc = jnp.zeros((1, LANE), F32)
        dkr = jnp.zeros((T, LANE), F32)
        for h in range(8):
            sl = slice(h * LANE, (h + 1) * LANE)
            dxq, gq_p = head_bwd(q0[:, sl], dq_ref[:, sl], gq)
            dxk, gk_p = head_bwd(k0[:, sl] + krb, dk_ref[:, sl], gk)
            dq0.append(dxq.astype(BF16))
            dk0.append(dxk.astype(BF16))
            dkr = dkr + dxk
            dgq_acc = dgq_acc + gq_p
            dgk_acc = dgk_acc + gk_p
        dgq_ref[...] += dgq_acc
        dgk_ref[...] += dgk_acc
        dq0 = jnp.concatenate(dq0, axis=1)
        dk0 = jnp.concatenate(dk0, axis=1)
        dvb = dv_ref[...].astype(BF16)
        dwuq_ref[...] += _tn(cqn, dq0)
        dwuk_ref[...] += _tn(ckvn, dk0)
        dwuv_ref[...] += _tn(ckvn, dvb)
        dcqn = _nt(dq0, wuq_ref[...])
        dckvn = _nt(dk0, wuk_ref[...]) + _nt(dvb, wuv_ref[...])
        dgcq_ref[...] += _csum(dcqn * cqh)
        dgckv_ref[...] += _csum(dckvn * ckvh)
        dxh = dcqn * gcq_ref[...]
        dz_ref[:, 0:256] = (rq * (dxh - cqh * jnp.mean(dxh * cqh, axis=-1, keepdims=True))).astype(BF16)
        dxh = dckvn * gckv_ref[...]
        dz_ref[:, 256:384] = (rkv * (dxh - ckvh * jnp.mean(dxh * ckvh, axis=-1, keepdims=True))).astype(BF16)
        lane = _lane((T, LANE))
        dz_ref[:, 384:512] = jnp.where((lane >= KR_LANE) & (lane < KR_LANE + 32), dkr, 0.0).astype(BF16)

    tabspec = pl.BlockSpec((T, LANE), lambda i: (i, 0))
    in_specs = [_zcol(T, 256, C_CQ), _zcol(T, LANE, C_CKV), _zcol(T, LANE, C_KR),
                pl.BlockSpec((T, 1024), lambda i: (i, 0)), pl.BlockSpec((T, 1024), lambda i: (i, 0)), pl.BlockSpec((T, 512), lambda i: (i, 0)),
                _full((1, 256)), _full((1, LANE)), _full((256, 1024)), _full((LANE, 1024)), _full((LANE, 512)), _full((1, LANE)), _full((1, LANE)),
                tabspec, tabspec, tabspec, pl.BlockSpec(memory_space=pl.ANY)]
    out_specs = [_zcol(T, 512, C_CQ), _full((256, 1024)), _full((LANE, 1024)), _full((LANE, 512)), _full((1, 256)), _full((1, LANE)),
                 _full((1, LANE)), _full((1, LANE))]
    out_shape = [jax.ShapeDtypeStruct(dz.shape, BF16), jax.ShapeDtypeStruct((256, 1024), F32), jax.ShapeDtypeStruct((LANE, 1024), F32),
                 jax.ShapeDtypeStruct((LANE, 512), F32), jax.ShapeDtypeStruct((1, 256), F32), jax.ShapeDtypeStruct((1, LANE), F32),
                 jax.ShapeDtypeStruct((1, LANE), F32), jax.ShapeDtypeStruct((1, LANE), F32)]
    return pl.pallas_call(
        body, name="mla_pre_bwd", grid=(S // T,), in_specs=in_specs, out_specs=out_specs, out_shape=out_shape,
        input_output_aliases={16: 0}, compiler_params=_cp(),
    )(zp, zp, zp, dq, dk, dv, w["g_cq"], w["g_ckv"], w["w_uq"], w["w_uk"], w["w_uv"], w["g_mq"], w["g_mk"], *tab, dz)


T_DIL = 256


def _head_stats(x, lane):
    sq = x * x
    sa = _rsum(jnp.where(lane < 64, sq, 0.0))
    sb = _rsum(jnp.where(lane >= 64, sq, 0.0))
    return lax.rsqrt(jnp.where(lane < 64, sa, sb) * (1.0 / DIL_HD) + EPS)


def _head_sum(x, lane):
    sa = _rsum(jnp.where(lane < 64, x, 0.0))
    sb = _rsum(jnp.where(lane >= 64, x, 0.0))
    return jnp.where(lane < 64, sa, sb)


def _dil_pre_fwd(zp, w, tab):
    S = zp.shape[0]
    T = T_DIL

    def body(q_ref, k_ref, gq_ref, gk_ref, C_ref, S1_ref, S2_ref, qo_ref, ko_ref):
        C, S1, S2 = C_ref[...], S1_ref[...], S2_ref[...]
        lane = _lane((T, LANE))
        for b in range(12):
            sl = slice(b * LANE, (b + 1) * LANE)
            x = q_ref[:, sl]
            qo_ref[:, sl] = _rope(x * _head_stats(x, lane) * gq_ref[...], C, S1, S2, 32)
            x = k_ref[:, sl]
            ko_ref[:, sl] = _rope(x * _head_stats(x, lane) * gk_ref[...], C, S1, S2, 32)

    tabspec = pl.BlockSpec((T, LANE), lambda i: (i, 0))
    out = pl.BlockSpec((T, 1536), lambda i: (i, 0))
    return pl.pallas_call(
        body, name="dil_pre_fwd", grid=(S // T,),
        in_specs=[_zcol(T, 1536, C_DQ), _zcol(T, 1536, C_DK), _full((1, LANE)), _full((1, LANE)), tabspec, tabspec, tabspec],
        out_specs=[out, out], out_shape=[jax.ShapeDtypeStruct((S, 1536), F32)] * 2, compiler_params=_cp(),
    )(zp, zp, w["g_dq"], w["g_dk"], *tab)


DIL_ROWS = 2048


def _dil_geometry(g, S):
    d = DIL_DILATIONS[g]
    P = NK * d
    return d, P, DIL_ROWS // P, S // P


def _dil_rows(start, d, blocks=1):
    return pl.ds(pl.multiple_of(start, NK), blocks * NK) if d == 1 else pl.ds(start, blocks * NK, stride=d)


def _dil_specs(g, S, col0):
    _, P, m, nb = _dil_geometry(g, S)
    cur = pl.BlockSpec((DIL_ROWS, LANE), lambda sb, c: (sb, col0 + c))
    prv = pl.BlockSpec((P, LANE), lambda sb, c: (jnp.maximum(sb * m - 1, 0), col0 + c))
    nxt = pl.BlockSpec((P, LANE), lambda sb, c: (jnp.minimum((sb + 1) * m, nb - 1), col0 + c))
    return cur, prv, nxt


def _dil_attn_fwd(q, k, zp, g):
    S = q.shape[0]
    d, P, m, nb = _dil_geometry(g, S)
    R = DIL_ROWS

    def body(q_ref, kc_ref, kp_ref, vc_ref, vp_ref, o_ref, lse_ref, *scr):
        sb = pl.program_id(0)
        if m > 1:
            ks_ref, vs_ref = scr
            ks_ref[0:P, :] = kp_ref[...]
            ks_ref[P:P + R, :] = kc_ref[...]
            vs_ref[0:P, :] = vp_ref[...]
            vs_ref[P:P + R, :] = vc_ref[...]
        lane = _lane((NK, LANE))

        def unit(u, carry):
            j = u // d
            start = j * P + (u - j * d)
            rows = _dil_rows(start, d)
            if m > 1:
                k2, v2 = ks_ref[_dil_rows(start, d, 2), :], vs_ref[_dil_rows(start, d, 2), :]
            else:
                k2 = jnp.concatenate([kp_ref[rows, :], kc_ref[rows, :]], axis=0)
                v2 = jnp.concatenate([vp_ref[rows, :], vc_ref[rows, :]], axis=0)
            k2, v2 = k2.astype(BF16), v2.astype(BF16)
            q_ = q_ref[rows, :].astype(BF16)
            row = lax.broadcasted_iota(jnp.int32, (NK, 2 * NK), 0)
            col = lax.broadcasted_iota(jnp.int32, (NK, 2 * NK), 1)
            band = (col >= row) & (col <= row + NK) & ((col >= NK) | (sb * m + j > 0))
            lane2 = _lane((2 * NK, LANE))
            zb, zv = jnp.zeros_like(q_), jnp.zeros_like(v2)
            o_tot = jnp.zeros((NK, LANE), F32)
            lse_tot = jnp.zeros((NK, LANE), F32)
            for hh in range(2):
                hm = (lane < 64) if hh == 0 else (lane >= 64)
                hm2 = (lane2 < 64) if hh == 0 else (lane2 >= 64)
                s_ = jnp.where(band, _nt(jnp.where(hm, q_, zb), k2) * DIL_SCALE, NEG)
                mx = jnp.max(s_, axis=-1, keepdims=True)
                e = jnp.exp(s_ - mx)
                den = _rsum(e)
                o_tot = o_tot + _nn(e.astype(BF16), jnp.where(hm2, v2, zv)) / den
                lse_tot = jnp.where(hm, mx + jnp.log(den), lse_tot)
            o_ref[rows, :] = o_tot
            lse_ref[rows, :] = lse_tot
            return carry

        lax.fori_loop(0, R // NK, unit, 0, unroll=8)

    qcur, qprv, _ = _dil_specs(g, S, 4 * g)
    vcur, vprv, _ = _dil_specs(g, S, C_DV + 4 * g)
    out = pl.BlockSpec((R, LANE), lambda sb, c: (sb, c))
    return pl.pallas_call(
        body, name=f"dil_attn_fwd{g}", grid=(S // R, 4), in_specs=[qcur, qcur, qprv, vcur, vprv], out_specs=[out, out],
        out_shape=[jax.ShapeDtypeStruct((S, 512), F32)] * 2,
        scratch_shapes=[pltpu.VMEM((P + R, LANE), F32)] * 2 if m > 1 else [], compiler_params=_cp(),
    )(q, k, k, zp, zp)


def _dil_combine(os_, ls_, zp):
    S = zp.shape[0]
    T = T_ROW

    def body(o0, o1, o2, l0, l1, l2, g_ref, oc_ref, L_ref, y_ref):
        a, b, c = l0[...], l1[...], l2[...]
        mx = jnp.maximum(jnp.maximum(a, b), c)
        ea, eb, ec = jnp.exp(a - mx), jnp.exp(b - mx), jnp.exp(c - mx)
        den = ea + eb + ec
        oc = (ea * o0[...] + eb * o1[...] + ec * o2[...]) / den
        oc_ref[...] = oc
        L_ref[...] = mx + jnp.log(den)
        y_ref[...] = (oc * _silu(g_ref[...])).astype(BF16)

    row = pl.BlockSpec((T, 512), lambda i: (i, 0))
    return pl.pallas_call(
        body, name="dil_combine", grid=(S // T,), in_specs=[row] * 6 + [_zcol(T, 512, C_DILG)], out_specs=[row, row, row],
        out_shape=[jax.ShapeDtypeStruct((S, 512), F32), jax.ShapeDtypeStruct((S, 512), F32), jax.ShapeDtypeStruct((S, 512), BF16)],
        compiler_params=_cp(),
    )(*os_, *ls_, zp)


def _dil_comb_bwd(zp, oc, dy, dz):
    S = zp.shape[0]
    T = T_ROW

    def body(g_ref, o_ref, dy_ref, dz_in, dz_ref, do_ref, D_ref):
        del dz_in
        g, o_, dy_ = g_ref[...], o_ref[...], dy_ref[...]
        do = dy_ * _silu(g)
        do_ref[...] = do
        dz_ref[...] = (dy_ * o_ * _dsilu(g)).astype(BF16)
        lane = _lane((T, LANE))
        for p in range(4):
            sl = slice(p * LANE, (p + 1) * LANE)
            D_ref[:, sl] = _head_sum(do[:, sl] * o_[:, sl], lane)

    row = pl.BlockSpec((T, 512), lambda i: (i, 0))
    zc = _zcol(T, 512, C_DILG)
    return pl.pallas_call(
        body, name="dil_comb_bwd", grid=(S // T,), in_specs=[zc, row, row, pl.BlockSpec(memory_space=pl.ANY)], out_specs=[zc, row, row],
        out_shape=[jax.ShapeDtypeStruct(dz.shape, BF16), jax.ShapeDtypeStruct((S, 512), F32), jax.ShapeDtypeStruct((S, 512), F32)],
        input_output_aliases={3: 0}, compiler_params=_cp(),
    )(zp, oc, dy, dz)


def _dil_attn_bwd(q, k, zp, do, L, Dr, g):
    S = q.shape[0]
    d, P, m, nb = _dil_geometry(g, S)
    R = DIL_ROWS
    n_q, n_k = 4, 2

    def body(*refs):
        q_side = refs[0:2 * n_q]
        k_side = refs[2 * n_q:2 * n_q + 2 * n_k]
        dq_ref, dk_ref, dv_ref = refs[2 * n_q + 2 * n_k:2 * n_q + 2 * n_k + 3]
        scr = refs[2 * n_q + 2 * n_k + 3:]
        sb = pl.program_id(0)
        if m > 1:
            for a in range(n_q):
                scr[a][0:R, :] = q_side[2 * a][...]
                scr[a][R:R + P, :] = q_side[2 * a + 1][...]
            for a in range(n_k):
                scr[n_q + a][0:P, :] = k_side[2 * a + 1][...]
                scr[n_q + a][P:P + R, :] = k_side[2 * a][...]
        lane = _lane((NK, LANE))

        def unit(u, carry):
            j = u // d
            start = j * P + (u - j * d)
            rows = _dil_rows(start, d)
            if m > 1:
                rows_b = _dil_rows(start + P, d)
                q2, do2, L2, D2 = [scr[a][_dil_rows(start, d, 2), :] for a in range(n_q)]
                kp, vp = [scr[n_q + a][rows, :] for a in range(n_k)]
                kc, vc = [scr[n_q + a][rows_b, :] for a in range(n_k)]
            else:
                q2, do2, L2, D2 = [jnp.concatenate([q_side[2 * a][rows, :], q_side[2 * a + 1][rows, :]], axis=0) for a in range(n_q)]
                kc, vc = [k_side[2 * a][rows, :] for a in range(n_k)]
                kp, vp = [k_side[2 * a + 1][rows, :] for a in range(n_k)]
            q2, do2 = q2.astype(BF16), do2.astype(BF16)
            kc, kp, vc, vp = kc.astype(BF16), kp.astype(BF16), vc.astype(BF16), vp.astype(BF16)
            n = sb * m + j
            row2 = lax.broadcasted_iota(jnp.int32, (2 * NK, NK), 0)
            col2 = lax.broadcasted_iota(jnp.int32, (2 * NK, NK), 1)
            m2 = ((row2 < NK) & (col2 <= row2)) | ((row2 >= NK) & (col2 >= row2 - NK) & (n < nb - 1))
            row = lax.broadcasted_iota(jnp.int32, (NK, NK), 0)
            col = lax.broadcasted_iota(jnp.int32, (NK, NK), 1)
            mp = (col >= row) & (n > 0)
            lane2 = _lane((2 * NK, LANE))
            zq, zb = jnp.zeros_like(q2), jnp.zeros_like(kc)
            dq_tot = jnp.zeros((NK, LANE), F32)
            dk_tot = jnp.zeros((NK, LANE), F32)
            dv_tot = jnp.zeros((NK, LANE), F32)
            for hh in range(2):
                hm = (lane < 64) if hh == 0 else (lane >= 64)
                hm2 = (lane2 < 64) if hh == 0 else (lane2 >= 64)
                Lb = jnp.where(hm2, L2, pltpu.roll(L2, 64, 1))
                Db = jnp.where(hm2, D2, pltpu.roll(D2, 64, 1))
                qm2 = jnp.where(hm2, q2, zq)
                vcm = jnp.where(hm, vc, zb)
                vpm = jnp.where(hm, vp, zb)
                p2 = jnp.exp(jnp.where(m2, _nt(qm2, kc) * DIL_SCALE, NEG) - Lb)
                ds2 = (p2 * (_nt(do2, vcm) - Db) * DIL_SCALE).astype(BF16)
                dk_tot = dk_tot + _tn(ds2, qm2)
                dv_tot = dv_tot + jnp.where(hm, _tn(p2.astype(BF16), do2), 0.0)
                pp = jnp.exp(jnp.where(mp, _nt(qm2[0:NK], kp) * DIL_SCALE, NEG) - Lb[0:NK])
                dsp = (pp * (_nt(do2[0:NK], vpm) - Db[0:NK]) * DIL_SCALE).astype(BF16)
                dq_tot = dq_tot + jnp.where(hm, _nn(ds2[0:NK], kc) + _nn(dsp, kp), 0.0)
            dq_ref[rows, :] = dq_tot
            dk_ref[rows, :] = dk_tot
            dv_ref[rows, :] = dv_tot
            return carry

        lax.fori_loop(0, R // NK, unit, 0, unroll=8)

    qcur, qprv, qnxt = _dil_specs(g, S, 4 * g)
    vcur, vprv, _ = _dil_specs(g, S, C_DV + 4 * g)
    ocur, _, onxt = _dil_specs(g, S, 0)
    out = pl.BlockSpec((R, LANE), lambda sb, c: (sb, c))
    scratch = [pltpu.VMEM((P + R, LANE), F32)] * (n_q + n_k) if m > 1 else []
    return pl.pallas_call(
        body, name=f"dil_attn_bwd{g}", grid=(S // R, 4),
        in_specs=[qcur, qnxt, ocur, onxt, ocur, onxt, ocur, onxt, qcur, qprv, vcur, vprv],
        out_specs=[out, out, out], out_shape=[jax.ShapeDtypeStruct((S, 512), F32)] * 3, scratch_shapes=scratch, compiler_params=_cp(),
    )(q, q, do, do, L, L, Dr, Dr, k, k, zp, zp)


def _dil_pre_bwd(zp, dys, g, tab, dz, col, name):
    S = zp.shape[0]
    T = T_DIL

    def body(x_ref, dy0_ref, dy1_ref, dy2_ref, g_ref, C_ref, S1_ref, S2_ref, dz_in, dz_ref, dg_ref):
        del dz_in
        i = pl.program_id(0)
        C, S1, S2 = C_ref[...], S1_ref[...], S2_ref[...]
        lane = _lane((T, LANE))
        gv = g_ref[...]
        acc = jnp.zeros((1, LANE), F32)
        for b in range(12):
            sl = slice(b * LANE, (b + 1) * LANE)
            x = x_ref[:, sl]
            r = _head_stats(x, lane)
            xn = x * r
            dy_ref = (dy0_ref, dy1_ref, dy2_ref)[b // 4]
            dyn = _rope_t(dy_ref[:, (b % 4) * LANE:(b % 4 + 1) * LANE], C, S1, S2, 32)
            acc = acc + _csum(dyn * xn)
            dxh = dyn * gv
            dz_ref[:, sl] = (r * (dxh - xn * _head_sum(dxh * xn, lane) * (1.0 / DIL_HD))).astype(BF16)

        @pl.when(i == 0)
        def _():
            dg_ref[...] = acc

        @pl.when(i > 0)
        def _():
            dg_ref[...] += acc

    tabspec = pl.BlockSpec((T, LANE), lambda i: (i, 0))
    zc = _zcol(T, 1536, col)
    grp = pl.BlockSpec((T, 512), lambda i: (i, 0))
    return pl.pallas_call(
        body, name=name, grid=(S // T,),
        in_specs=[zc, grp, grp, grp, _full((1, LANE)), tabspec, tabspec, tabspec, pl.BlockSpec(memory_space=pl.ANY)],
        out_specs=[zc, _full((1, LANE))], out_shape=[jax.ShapeDtypeStruct(dz.shape, BF16), jax.ShapeDtypeStruct((1, LANE), F32)],
        input_output_aliases={8: 0}, compiler_params=_cp(),
    )(zp, *dys, g, *tab, dz)


def _dil_dv_into(dvs, dz):
    S = dz.shape[0]
    T = T_ROW

    def body(s0, s1, s2, dz_in, o_ref):
        del dz_in
        for gi, s in enumerate((s0, s1, s2)):
            o_ref[:, gi * 512:(gi + 1) * 512] = s[...].astype(BF16)

    grp = pl.BlockSpec((T, 512), lambda i: (i, 0))
    return pl.pallas_call(
        body, name="dil_dv", grid=(S // T,), in_specs=[grp, grp, grp, pl.BlockSpec(memory_space=pl.ANY)],
        out_specs=_zcol(T, 1536, C_DV), out_shape=jax.ShapeDtypeStruct(dz.shape, BF16), input_output_aliases={3: 0}, compiler_params=_cp(),
    )(*dvs, dz)


T_MRG = 256


def _merge_fwd(P, zp, b_merge):
    S = zp.shape[0]
    T = T_MRG

    def body(p0, p1, p2, m0, m1, m2, b_ref, o_ref):
        acc = jnp.zeros((T, D), F32)
        for j, (p, m) in enumerate(((p0, m0), (p1, m1), (p2, m2))):
            acc = acc + _sig(m[...] + b_ref[:, j * D:(j + 1) * D]) * p[...]
        o_ref[...] = acc.astype(BF16)

    row = pl.BlockSpec((T, D), lambda i: (i, 0))
    return pl.pallas_call(
        body, name="merge_fwd", grid=(S // T,),
        in_specs=[row, row, row] + [_zcol(T, D, C_MERGE + 8 * j) for j in range(3)] + [_full((1, 3 * D))], out_specs=row,
        out_shape=jax.ShapeDtypeStruct((S, D), BF16), compiler_params=_cp(),
    )(*P, zp, zp, zp, b_merge)


def _merge_bwd(dm, Pj, zp, bj, dz, j):
    S = zp.shape[0]
    T = T_MRG

    def body(dm_ref, p_ref, m_ref, b_ref, dz_in, dz_ref, dp_ref, db_ref):
        del dz_in
        i = pl.program_id(0)
        g = _sig(m_ref[...] + b_ref[...])
        dmv = dm_ref[...]
        dp_ref[...] = (dmv * g).astype(BF16)
        dg = dmv * p_ref[...] * g * (1.0 - g)
        dz_ref[...] = dg.astype(BF16)
        part = _csum(dg)

        @pl.when(i == 0)
        def _():
            db_ref[...] = part

        @pl.when(i > 0)
        def _():
            db_ref[...] += part

    row = pl.BlockSpec((T, D), lambda i: (i, 0))
    zc = _zcol(T, D, C_MERGE + 8 * j)
    return pl.pallas_call(
        body, name=f"merge_bwd{j}", grid=(S // T,), in_specs=[row, row, zc, _full((1, D)), pl.BlockSpec(memory_space=pl.ANY)],
        out_specs=[zc, row, _full((1, D))],
        out_shape=[jax.ShapeDtypeStruct(dz.shape, BF16), jax.ShapeDtypeStruct((S, D), BF16), jax.ShapeDtypeStruct((1, D), F32)],
        input_output_aliases={4: 0}, compiler_params=_cp(),
    )(dm, Pj, zp, bj, dz)


def _loss_fwd_bwd(y, target):
    S = y.shape[0]
    T = T_ROW

    def body(y_ref, t_ref, loss_ref, dy_ref):
        i = pl.program_id(0)
        err = y_ref[...] - t_ref[...]
        dy_ref[...] = err * (1.0 / D)
        part = jnp.sum(err * err, keepdims=True).reshape(1, 1) * (0.5 / D)

        @pl.when(i == 0)
        def _():
            loss_ref[...] = part

        @pl.when(i > 0)
        def _():
            loss_ref[...] += part

    row = pl.BlockSpec((T, D), lambda i: (i, 0))
    return pl.pallas_call(
        body, name="loss", grid=(S // T,), in_specs=[row, row], out_specs=[_full((1, 1)), row],
        out_shape=[jax.ShapeDtypeStruct((1, 1), F32), jax.ShapeDtypeStruct((S, D), F32)], compiler_params=_cp(),
    )(y, target)


def _layer_fwd(x, w, tabs):
    mla_tab, dil_tab = tabs
    S = x.shape[0]
    h = _rms_in_fwd(x, w["norm_g"])
    zp = _mm(h, w["w_in"], mode="nn", name="in_proj")
    hs, y_lru = _lru_fwd(zp, w)
    q, k, v = _mla_pre_fwd(zp, w, mla_tab)
    o_mla, lse, y_mla = _mla_attn_fwd(q, k, v, zp)
    qd, kd = _dil_pre_fwd(zp, w, dil_tab)
    og, lg = zip(*[_dil_attn_fwd(qd, kd, zp, g) for g in range(len(DIL_DILATIONS))])
    oc, L, y_dil = _dil_combine(og, lg, zp)
    P = [_mm(y_lru, w["w_lru_o"], mode="nn", name="lru_out"), _mm(y_mla, w["w_mla_o"], mode="nn", name="mla_out"),
         _mm(y_dil, w["w_dil_o"], mode="nn", name="dil_out")]
    merged = _merge_fwd(P, zp, w["b_merge"])
    x_out = _mm(merged, w["w_out"], mode="nn", name="out_proj", add=x)
    saved = dict(x=x, h=h, zp=zp, hs=hs, y=(y_lru, y_mla, y_dil), q=q, k=k, v=v, o_mla=o_mla, lse=lse, qd=qd, kd=kd, oc=oc, L=L, P=P,
                 merged=merged)
    return x_out, saved


def _layer_bwd(dout, w, tabs, sv, hook=None, after=None):
    mla_tab, dil_tab = tabs
    zp = sv["zp"]
    S = zp.shape[0]
    g = {}
    dm = _mm(dout, w["w_out"], mode="nt", name="d_merged", after=after)
    g["w_out"] = _mm(sv["merged"], dout, mode="tn", name="dw_out", out_dtype=BF16)
    dz = lax.empty((S, ZW), BF16)
    dP, db = [], []
    for j in range(3):
        dz, dpj, dbj = _merge_bwd(dm, sv["P"][j], zp, w["b_merge"][:, j * D:(j + 1) * D], dz, j)
        dP.append(dpj)
        db.append(dbj)
    g["b_merge"] = jnp.concatenate(db, axis=1)
    names = ("w_lru_o", "w_mla_o", "w_dil_o")
    dy = []
    for j in range(3):
        dy.append(_mm(dP[j], w[names[j]], mode="nt", name="dy_" + names[j]))
        g[names[j]] = _mm(sv["y"][j], dP[j], mode="tn", name="d" + names[j], out_dtype=BF16)
    dz = _lru_gate_bwd(zp, sv["hs"], dy[0], dz)
    dz, g["conv_w"], g["conv_b"], g["w_gx"], g["b_gx"], g["w_ga"], g["b_ga"], g["lam"] = _lru_bwd(zp, sv["hs"], dy[0], w, dz)
    dz, do, Dr = _mla_post_bwd(zp, sv["o_mla"], dy[1], dz)
    dq, dk, dv = _mla_attn_bwd(sv["q"], sv["k"], sv["v"], do, sv["lse"], Dr)
    dz, g["w_uq"], g["w_uk"], g["w_uv"], g["g_cq"], g["g_ckv"], g["g_mq"], g["g_mk"] = _mla_pre_bwd(zp, dq, dk, dv, w, mla_tab, dz)
    dz, dod, Dd = _dil_comb_bwd(zp, sv["oc"], dy[2], dz)
    dqs, dks, dvs = zip(*[_dil_attn_bwd(sv["qd"], sv["kd"], zp, dod, sv["L"], Dd, gi) for gi in range(len(DIL_DILATIONS))])
    dz, g["g_dq"] = _dil_pre_bwd(zp, dqs, w["g_dq"], dil_tab, dz, C_DQ, "dil_pre_bwd_q")
    dz, g["g_dk"] = _dil_pre_bwd(zp, dks, w["g_dk"], dil_tab, dz, C_DK, "dil_pre_bwd_k")
    dz = _dil_dv_into(dvs, dz)
    g["w_in"] = _mm(sv["h"], dz, mode="tn", name="dw_in", out_dtype=BF16)
    token = hook(g) if hook is not None else None
    dh = _mm(dz, w["w_in"], mode="nt", name="d_h", after=token)
    dx, g["norm_g"] = _rms_in_bwd(sv["x"], w["norm_g"], dh, dout)
    return dx, g


def _peers():
    mx, my, mc = lax.axis_index("x"), lax.axis_index("y"), lax.axis_index("c")
    me = 4 * mx + 2 * my + mc
    out = []
    for k in range(1, N_DEV):
        px = 1 - mx if k & 4 else mx
        py = 1 - my if k & 2 else my
        pc = 1 - mc if k & 1 else mc
        out.append(((px, py, pc), 4 * px + 2 * py + pc))
    return me, out


def _whole(ref, p):
    del p
    return ref


def _exchange(srcs, slicers, slices, name):
    n = len(srcs)

    def body(*refs):
        ins, outs = refs[:n], refs[n:2 * n]
        send_sems, recv_sems, local_sems = refs[2 * n:]
        me, peers = _peers()
        mine = [pltpu.make_async_copy(slicers[a](ins[a], me), outs[a].at[me], local_sems.at[a]) for a in range(n)]
        for cp in mine:
            cp.start()
        copies = []
        for k, (peer, pidx) in enumerate(peers):
            for a in range(n):
                cp = pltpu.make_async_remote_copy(
                    src_ref=slicers[a](ins[a], pidx), dst_ref=outs[a].at[me], send_sem=send_sems.at[k * n + a],
                    recv_sem=recv_sems.at[k * n + a], device_id=peer, device_id_type=pl.DeviceIdType.MESH)
                cp.start()
                copies.append(cp)
        for cp in copies + mine:
            cp.wait()

    nsem = (N_DEV - 1) * n
    return pl.pallas_call(
        body, name=name, out_shape=[jax.ShapeDtypeStruct((N_DEV,) + shp, dt) for shp, dt in slices],
        in_specs=[pl.BlockSpec(memory_space=pl.ANY)] * n, out_specs=[pl.BlockSpec(memory_space=pl.ANY)] * n,
        scratch_shapes=[pltpu.SemaphoreType.DMA((nsem,)), pltpu.SemaphoreType.DMA((nsem,)), pltpu.SemaphoreType.DMA((n,))],
        compiler_params=pltpu.CompilerParams(has_side_effects=True),
    )(*srcs)


def _gather_two_level(srcs, name):
    n = len(srcs)

    def body(*refs):
        ins, outs = refs[:n], refs[n:2 * n]
        send_sems, recv_sems, local_sems = refs[2 * n:]
        mx, my, mc = lax.axis_index("x"), lax.axis_index("y"), lax.axis_index("c")
        me, sibling = (mx, my, mc), (mx, my, 1 - mc)
        chips = [(1 - mx, my), (mx, 1 - my), (1 - mx, 1 - my)]
        slot = lambda d: 4 * d[0] + 2 * d[1] + d[2]

        def copy(j, a, block, to, own=False):
            return pltpu.make_async_remote_copy(
                src_ref=ins[a] if own else outs[a].at[slot(block)], dst_ref=outs[a].at[slot(block)],
                send_sem=send_sems.at[j * n + a], recv_sem=recv_sems.at[j * n + a], device_id=to, device_id_type=pl.DeviceIdType.MESH)

        mine = [pltpu.make_async_copy(ins[a], outs[a].at[slot(me)], local_sems.at[a]) for a in range(n)]
        first = [copy(1 + j, a, me, (*chip, mc), own=True) for j, chip in enumerate(chips) for a in range(n)]
        first += [copy(0, a, me, sibling, own=True) for a in range(n)]
        for cp in mine + first:
            cp.start()
        passed = []
        for j, chip in enumerate(chips):
            for a in range(n):
                copy(1 + j, a, (*chip, mc), me).wait_recv()
                cp = copy(4 + j, a, (*chip, mc), sibling)
                cp.start()
                passed.append(cp)
        for a in range(n):
            copy(0, a, sibling, me).wait_recv()
        for j, chip in enumerate(chips):
            for a in range(n):
                copy(4 + j, a, (*chip, 1 - mc), me).wait_recv()
        for cp in first + passed:
            cp.wait_send()
        for cp in mine:
            cp.wait()

    nsem = (N_DEV - 1) * n
    return pl.pallas_call(
        body, name=name, out_shape=[jax.ShapeDtypeStruct((N_DEV,) + a.shape, a.dtype) for a in srcs],
        in_specs=[pl.BlockSpec(memory_space=pl.ANY)] * n, out_specs=[pl.BlockSpec(memory_space=pl.ANY)] * n,
        scratch_shapes=[pltpu.SemaphoreType.DMA((nsem,)), pltpu.SemaphoreType.DMA((nsem,)), pltpu.SemaphoreType.DMA((n,))],
        compiler_params=pltpu.CompilerParams(has_side_effects=True),
    )(*srcs)


_HBM = pl.BlockSpec(memory_space=pltpu.HBM)
_SEM = pl.BlockSpec(memory_space=pltpu.SEMAPHORE)
_DATAFLOW = pltpu.SideEffectType.DATAFLOW_SIDE_EFFECTING


def _plan_chips():
    mx, my, mc = lax.axis_index("x"), lax.axis_index("y"), lax.axis_index("c")
    return 2 * mx + my, [((cx, cy, mc), 2 * cx + cy) for cx, cy in ((1 - mx, my), (mx, 1 - my), (1 - mx, 1 - my))]


def _pair_exchange(srcs, slicers, slices, sliced, name):
    n = len(srcs)
    pieces = [4 if s else 1 for s in sliced]

    def body(*refs):
        ins = refs[:n]
        outs = list(refs[n:-3])
        send_sems, recv_sems, local_sems = refs[-3:]
        mx, my, mc = lax.axis_index("x"), lax.axis_index("y"), lax.axis_index("c")
        copies, i, li = [], 0, 0
        for a in range(n):
            recv = outs.pop(0)
            own = outs.pop(0) if sliced[a] else None
            for q in range(pieces[a]):
                src = slicers[a](ins[a], 2 * q + 1 - mc) if sliced[a] else ins[a]
                copies.append(pltpu.make_async_remote_copy(
                    src_ref=src, dst_ref=recv.at[q], send_sem=send_sems.at[i], recv_sem=recv_sems.at[i],
                    device_id=(mx, my, 1 - mc), device_id_type=pl.DeviceIdType.MESH))
                i += 1
                if sliced[a]:
                    copies.append(pltpu.make_async_copy(slicers[a](ins[a], 2 * q + mc), own.at[q], local_sems.at[li]))
                    li += 1
        for cp in copies:
            cp.start()
        for cp in copies:
            cp.wait()

    out_shape = []
    for (shp, dt), s, p in zip(slices, sliced, pieces):
        out_shape += [jax.ShapeDtypeStruct((p,) + shp, dt)] * (2 if s else 1)
    outs = list(pl.pallas_call(
        body, name=name, out_shape=out_shape, in_specs=[pl.BlockSpec(memory_space=pl.ANY)] * n,
        out_specs=[pl.BlockSpec(memory_space=pl.ANY)] * len(out_shape),
        scratch_shapes=[pltpu.SemaphoreType.DMA((sum(pieces),)), pltpu.SemaphoreType.DMA((sum(pieces),)),
                        pltpu.SemaphoreType.DMA((4 * sum(sliced),))],
        compiler_params=pltpu.CompilerParams(has_side_effects=True),
    )(*srcs))
    recv, own = [], []
    for s in sliced:
        recv.append(outs.pop(0))
        own.append(outs.pop(0) if s else None)
    return recv, own


def _add2(x, y, name):
    shp = x.shape
    x, y = x.reshape(-1, shp[-1]), y.reshape(-1, shp[-1])
    R, C = x.shape
    tr = R
    while tr * C * 4 > (1 << 21) and tr % 32 == 0:
        tr //= 2

    def body(x_ref, y_ref, o_ref):
        o_ref[...] = (x_ref[...].astype(F32) + y_ref[...].astype(F32)).astype(o_ref.dtype)

    spec = pl.BlockSpec((tr, C), lambda i: (i, 0))
    return pl.pallas_call(body, name=name, grid=(R // tr,), in_specs=[spec, spec], out_specs=spec,
                          out_shape=jax.ShapeDtypeStruct((R, C), x.dtype), compiler_params=_cp())(x, y).reshape(shp)


def _exchange_start(srcs, slicers, slices, after, name, plan=_peers, nslots=N_DEV):
    n = len(srcs)
    nsem = (nslots - 1) * n
    lands = [lax.empty((nslots,) + shp, dt) for shp, dt in slices]

    def body(*refs):
        ins, lands_in = refs[:n], refs[n:2 * n]
        send_sems, recv_sems, local_sems = refs[2 * n + 1], refs[2 * n + 2], refs[2 * n + 3]
        token = refs[-1]
        me, peers = plan()
        for a in range(n):
            pltpu.make_async_copy(slicers[a](ins[a], me), lands_in[a].at[me], local_sems.at[a]).start()
        for k, (peer, pidx) in enumerate(peers):
            for a in range(n):
                pltpu.make_async_remote_copy(
                    src_ref=slicers[a](ins[a], pidx), dst_ref=lands_in[a].at[me], send_sem=send_sems.at[k * n + a],
                    recv_sem=recv_sems.at[k * n + a], device_id=peer, device_id_type=pl.DeviceIdType.MESH).start()
        token[...] = jnp.zeros_like(token)

    hbm = lambda a: pltpu.with_memory_space_constraint(a, pltpu.HBM)
    return pl.pallas_call(
        body, name=name,
        out_shape=(pltpu.SemaphoreType.DMA((nsem,)), pltpu.SemaphoreType.DMA((nsem,)), pltpu.SemaphoreType.DMA((n,)),
                   *[pltpu.HBM(a.shape, a.dtype) for a in srcs], *[pltpu.HBM(a.shape, a.dtype) for a in lands],
                   jax.ShapeDtypeStruct((SUB, LANE), F32)),
        in_specs=[_HBM] * (2 * n) + [pl.BlockSpec(memory_space=pl.ANY)],
        out_specs=(_SEM, _SEM, _SEM, *[_HBM] * (2 * n), pl.BlockSpec(memory_space=pltpu.VMEM)),
        input_output_aliases={i: 3 + i for i in range(2 * n)},
        compiler_params=pltpu.CompilerParams(has_side_effects=_DATAFLOW),
    )(*[hbm(a) for a in srcs], *[hbm(a) for a in lands], after)


def _exchange_wait(started, slicers, after, name, plan=_peers):
    n = (len(started) - 4) // 2
    sems, thru = started[0:3], started[3:3 + 2 * n]

    def body(*refs):
        srcs, lands = refs[:n], refs[n:2 * n]
        send_sems, recv_sems, local_sems = refs[2 * n], refs[2 * n + 1], refs[2 * n + 2]
        me, peers = plan()
        for k, (peer, pidx) in enumerate(peers):
            for a in range(n):
                cp = pltpu.make_async_remote_copy(
                    src_ref=slicers[a](srcs[a], pidx), dst_ref=lands[a].at[me], send_sem=send_sems.at[k * n + a],
                    recv_sem=recv_sems.at[k * n + a], device_id=peer, device_id_type=pl.DeviceIdType.MESH)
                cp.wait_send()
                cp.wait_recv()
        for a in range(n):
            pltpu.make_async_copy(slicers[a](srcs[a], me), lands[a].at[me], local_sems.at[a]).wait()

    outs = pl.pallas_call(
        body, name=name, out_shape=[pltpu.HBM(a.shape, a.dtype) for a in thru],
        in_specs=[_HBM] * (2 * n) + [_SEM, _SEM, _SEM, pl.BlockSpec(memory_space=pl.ANY)], out_specs=[_HBM] * (2 * n),
        input_output_aliases={i: i for i in range(2 * n)}, compiler_params=pltpu.CompilerParams(has_side_effects=_DATAFLOW),
    )(*thru, *sems, after)
    return outs[n:]


WIN = 13 * LANE


def _win_base(s):
    n = s * SHARD_IN
    a0 = n + jnp.where(n >= _KR0, KR_LANE, 0) + jnp.where(n >= _KR0 + 32, 32, 0)
    return jnp.minimum(a0 // LANE, (ZW - WIN) // LANE)


def _win_offsets(s):
    n = s * SHARD_IN + jnp.arange(SHARD_IN)
    o = s * SHARD_IN - _win_base(s) * LANE
    return n, (o, o + KR_LANE, o + LANE - 32)


def _to_window(shard, s):
    _, offs = _win_offsets(s)
    padded = jnp.pad(shard, ((0, 0), (0, 0), (WIN, WIN)))
    a, b, c = [lax.dynamic_slice(padded, (0, 0, WIN - o), shard.shape[:2] + (WIN,)) for o in offs]
    col = (_win_base(s) * LANE + jnp.arange(WIN))[None, None, :]
    zero = jnp.zeros_like(a)
    return jnp.where(col < _KR0, a, jnp.where((col >= _KR0 + KR_LANE) & (col < _KR0 + KR_LANE + 32), b, jnp.where(col >= _KR0 + LANE, c, zero)))


def _from_window(win, s):
    n, offs = _win_offsets(s)
    a, b, c = [lax.dynamic_slice(win, (0, 0, o), win.shape[:2] + (SHARD_IN,)) for o in offs]
    return jnp.where((n < _KR0)[None, None, :], a, jnp.where((n < _KR0 + 32)[None, None, :], b, c))


def _win_base_static(s):
    n = s * SHARD_IN
    a0 = n + (KR_LANE if n >= _KR0 else 0) + (32 if n >= _KR0 + 32 else 0)
    return min(a0 // LANE, (ZW - WIN) // LANE)


def _assemble_w_in(gw):
    tr = 128
    bases = [_win_base_static(s) for s in range(N_DEV)]

    def body(g_ref, o_ref):
        for j in range(ZW // LANE):
            acc = None
            for s in range(N_DEV):
                if bases[s] <= j < bases[s] + WIN // LANE:
                    piece = g_ref[s, :, (j - bases[s]) * LANE:(j - bases[s] + 1) * LANE]
                    acc = piece if acc is None else acc + piece
            o_ref[:, j * LANE:(j + 1) * LANE] = acc

    return pl.pallas_call(
        body, name="assemble_w_in", grid=(D // tr,), in_specs=[pl.BlockSpec((N_DEV, tr, WIN), lambda i: (0, i, 0))],
        out_specs=pl.BlockSpec((tr, ZW), lambda i: (i, 0)), out_shape=jax.ShapeDtypeStruct((D, ZW), gw.dtype), compiler_params=_cp(),
    )(gw)


def _cols(width):
    return lambda ref, p: ref.at[:, pl.ds(pl.multiple_of(p * width, width), width)]


def _rows(height):
    return lambda ref, p: ref.at[pl.ds(pl.multiple_of(p * height, height), height), :]


SCATTER = {
    'w_in': (lambda ref, p: ref.at[:, pl.ds(pl.multiple_of(_win_base(p) * LANE, LANE), WIN)], (D, WIN), BF16),
    'conv_w': (_cols(LANE), (4, LANE), F32),
    'w_lru_o': (_rows(LANE), (LANE, D), BF16),
    'w_uq': (_cols(LANE), (256, LANE), F32),
    'w_ukv': (_cols(LANE), (128, LANE), F32),
    'w_mla_o': (_cols(LANE), (512, LANE), BF16),
    'w_dil_o': (_cols(LANE), (512, LANE), BF16),
    'w_out': (_rows(LANE), (LANE, D), BF16),
}


PACK_ROWS = 64


def _packed_rows(shapes):
    n = sum(int(np.prod(s)) for s in shapes)
    return -(-n // (PACK_ROWS * LANE)) * PACK_ROWS


def _sum8(buf, name):
    ns, R, C = buf.shape
    tr = R
    while tr * C * 4 * ns > (1 << 22) and tr % 32 == 0:
        tr //= 2

    def body(b_ref, o_ref):
        acc = b_ref[0].astype(F32)
        for s in range(1, ns):
            acc = acc + b_ref[s].astype(F32)
        o_ref[...] = acc

    return pl.pallas_call(
        body, name=name, grid=(R // tr,), in_specs=[pl.BlockSpec((ns, tr, C), lambda i: (0, i, 0))],
        out_specs=pl.BlockSpec((tr, C), lambda i: (i, 0)), out_shape=jax.ShapeDtypeStruct((R, C), F32), compiler_params=_cp(),
    )(buf)


def _pack(arrs, dtype, lead):
    flat = [a.astype(dtype).reshape(a.shape[:lead] + (-1,)) for a in arrs]
    cat = jnp.concatenate(flat, axis=-1)
    n = cat.shape[-1]
    unit = PACK_ROWS * LANE
    pad = (-n) % unit
    if pad:
        cat = jnp.pad(cat, [(0, 0)] * lead + [(0, pad)])
    return cat.reshape(cat.shape[:lead] + ((n + pad) // LANE, LANE))


def _unpack(buf, shapes, lead):
    flat = buf.reshape(buf.shape[:lead] + (-1,))
    out, off = [], 0
    for shp in shapes:
        n = int(np.prod(shp))
        out.append(flat[..., off:off + n].reshape(buf.shape[:lead] + tuple(shp)))
        off += n
    return out


def _adamw(w, g, m, v, name):
    rows, cols = w.shape
    tr = rows
    while tr * cols * 4 > (3 << 19) and tr % 16 == 0:
        tr //= 2
    c1 = 1.0 - ADAM_B1 ** ADAM_STEP
    c2 = 1.0 - ADAM_B2 ** ADAM_STEP

    def body(w_ref, g_ref, m_ref, v_ref, d_ref, mo_ref, vo_ref):
        gv = g_ref[...]
        mn = ADAM_B1 * m_ref[...] + (1.0 - ADAM_B1) * gv
        vn = ADAM_B2 * v_ref[...] + (1.0 - ADAM_B2) * (gv * gv)
        mo_ref[...] = mn
        vo_ref[...] = vn
        d_ref[...] = -ADAM_LR * ((mn / c1) / (jnp.sqrt(vn / c2) + ADAM_EPS) + ADAM_WD * w_ref[...])

    spec = pl.BlockSpec((tr, cols), lambda i: (i, 0))
    return pl.pallas_call(
        body, name=name, grid=(rows // tr,), in_specs=[spec] * 4, out_specs=[spec] * 3,
        out_shape=[jax.ShapeDtypeStruct((rows, cols), F32)] * 3, compiler_params=_cp(),
    )(w, g, m, v)


IN_NAMES = ['x', 'positions', 'norm_g', 'w_in', 'conv_w', 'conv_b', 'w_gate_x', 'b_gate_x', 'w_gate_a', 'b_gate_a', 'lru_lambda', 'w_lru_o',
            'cq_norm_g', 'ckv_norm_g', 'w_uq', 'w_ukv', 'mla_q_norm_g', 'mla_k_norm_g', 'w_mla_o', 'dil_q_norm_g', 'dil_k_norm_g', 'w_dil_o',
            'b_merge', 'w_out']
WEIGHTS = IN_NAMES[2:]
REPLICATED = [n for n in WEIGHTS if n not in SCATTER]
GATE_WEIGHTS = ('w_gate_x', 'w_gate_a')

_KR0 = C_KR * LANE


GATHERED = ['w_in', 'w_lru_o', 'w_uq', 'w_ukv', 'w_mla_o', 'w_dil_o', 'w_out', 'conv_w']


def _local_weights(wd, me):
    loc = {n: wd[n].astype(BF16) for n in GATHERED[:-1]}
    loc['w_in'] = _to_window(loc['w_in'], me)
    loc['w_uq'] = jnp.pad(loc['w_uq'], ((0, 0), (0, 0), (0, LANE - MLA_QK)))
    loc['conv_w'] = wd['conv_w']
    return [[loc[n][l] for n in GATHERED] for l in range(DEPTH)]


def _layer_weights(gathered, rep, l):
    gw = dict(zip(GATHERED, gathered))
    by_rows = lambda a: a.reshape(-1, a.shape[-1])
    by_cols = lambda a: jnp.swapaxes(a, 0, 1).reshape(a.shape[1], -1)
    ukv = jnp.swapaxes(gw['w_ukv'], 0, 1)
    g96 = lambda a: jnp.pad(a[l].reshape(1, MLA_QK), ((0, 0), (0, LANE - MLA_QK)))
    g64 = lambda a: jnp.tile(a[l].reshape(1, DIL_HD), (1, 2))
    return dict(
        norm_g=rep['norm_g'][l].reshape(1, D), w_in=_assemble_w_in(gw['w_in']),
        conv_w=by_cols(gw['conv_w']), conv_b=rep['conv_b'][l].reshape(1, D),
        w_gx=rep['w_gate_x'][l].astype(BF16), b_gx=rep['b_gate_x'][l].reshape(8, 1, LANE),
        w_ga=rep['w_gate_a'][l].astype(BF16), b_ga=rep['b_gate_a'][l].reshape(8, 1, LANE),
        lam=rep['lru_lambda'][l].reshape(1, D),
        w_lru_o=by_rows(gw['w_lru_o']), w_mla_o=by_cols(gw['w_mla_o']), w_dil_o=by_cols(gw['w_dil_o']), w_out=by_rows(gw['w_out']),
        g_cq=rep['cq_norm_g'][l].reshape(1, 256), g_ckv=rep['ckv_norm_g'][l].reshape(1, 128),
        w_uq=by_cols(gw['w_uq']), w_uk=jnp.pad(ukv[:, :, :64], ((0, 0), (0, 0), (0, 64))).reshape(128, 1024),
        w_uv=ukv[:, :, 64:].reshape(128, 512),
        g_mq=g96(rep['mla_q_norm_g']), g_mk=g96(rep['mla_k_norm_g']), g_dq=g64(rep['dil_q_norm_g']), g_dk=g64(rep['dil_k_norm_g']),
        b_merge=rep['b_merge'][l].reshape(1, 3 * D),
    )


def _sharded_grads(g):
    uk = g['w_uk'].reshape(128, 8, 128)[:, :, :64]
    uv = g['w_uv'].reshape(128, 8, 64)
    d = {'w_in': g['w_in'], 'conv_w': g['conv_w'], 'w_lru_o': g['w_lru_o'], 'w_uq': g['w_uq'],
         'w_ukv': jnp.concatenate([uk, uv], axis=-1).reshape(128, 1024), 'w_mla_o': g['w_mla_o'], 'w_dil_o': g['w_dil_o'],
         'w_out': g['w_out']}
    return [d[n] for n in SCATTER]


def _replicated_grads(g):
    return {
        'conv_b': g['conv_b'].reshape(D),
        'w_gate_x': g['w_gx'], 'b_gate_x': g['b_gx'].reshape(8, LANE), 'w_gate_a': g['w_ga'], 'b_gate_a': g['b_ga'].reshape(8, LANE),
        'lru_lambda': g['lam'].reshape(D), 'cq_norm_g': g['g_cq'].reshape(256), 'ckv_norm_g': g['g_ckv'].reshape(128),
        'mla_q_norm_g': g['g_mq'][0, :MLA_QK], 'mla_k_norm_g': g['g_mk'][0, :MLA_QK],
        'dil_q_norm_g': g['g_dq'][0, :DIL_HD] + g['g_dq'][0, DIL_HD:], 'dil_k_norm_g': g['g_dk'][0, :DIL_HD] + g['g_dk'][0, DIL_HD:],
        'b_merge': g['b_merge'].reshape(3 * D),
    }


def kernel(x, positions, norm_g, w_in, conv_w, conv_b, w_gate_x, b_gate_x, w_gate_a, b_gate_a, lru_lambda, w_lru_o, cq_norm_g, ckv_norm_g, w_uq, w_ukv, mla_q_norm_g, mla_k_norm_g, w_mla_o, dil_q_norm_g, dil_k_norm_g, w_dil_o, b_merge, w_out, loss_target, m_norm_g, m_w_in, m_conv_w, m_conv_b, m_w_gate_x, m_b_gate_x, m_w_gate_a, m_b_gate_a, m_lru_lambda, m_w_lru_o, m_cq_norm_g, m_ckv_norm_g, m_w_uq, m_w_ukv, m_mla_q_norm_g, m_mla_k_norm_g, m_w_mla_o, m_dil_q_norm_g, m_dil_k_norm_g, m_w_dil_o, m_b_merge, m_w_out, v_norm_g, v_w_in, v_conv_w, v_conv_b, v_w_gate_x, v_b_gate_x, v_w_gate_a, v_b_gate_a, v_lru_lambda, v_w_lru_o, v_cq_norm_g, v_ckv_norm_g, v_w_uq, v_w_ukv, v_mla_q_norm_g, v_mla_k_norm_g, v_w_mla_o, v_dil_q_norm_g, v_dil_k_norm_g, v_w_dil_o, v_b_merge, v_w_out):
    args = (x, positions, norm_g, w_in, conv_w, conv_b, w_gate_x, b_gate_x, w_gate_a, b_gate_a, lru_lambda, w_lru_o, cq_norm_g, ckv_norm_g, w_uq, w_ukv, mla_q_norm_g, mla_k_norm_g, w_mla_o, dil_q_norm_g, dil_k_norm_g, w_dil_o, b_merge, w_out)
    moments_m = (m_norm_g, m_w_in, m_conv_w, m_conv_b, m_w_gate_x, m_b_gate_x, m_w_gate_a, m_b_gate_a, m_lru_lambda, m_w_lru_o, m_cq_norm_g, m_ckv_norm_g, m_w_uq, m_w_ukv, m_mla_q_norm_g, m_mla_k_norm_g, m_w_mla_o, m_dil_q_norm_g, m_dil_k_norm_g, m_w_dil_o, m_b_merge, m_w_out)
    moments_v = (v_norm_g, v_w_in, v_conv_w, v_conv_b, v_w_gate_x, v_b_gate_x, v_w_gate_a, v_b_gate_a, v_lru_lambda, v_w_lru_o, v_cq_norm_g, v_ckv_norm_g, v_w_uq, v_w_ukv, v_mla_q_norm_g, v_mla_k_norm_g, v_w_mla_o, v_dil_q_norm_g, v_dil_k_norm_g, v_w_dil_o, v_b_merge, v_w_out)
    a = dict(zip(IN_NAMES, args))
    wd = {n: a[n] for n in WEIGHTS}
    md = dict(zip(WEIGHTS, moments_m))
    vd = dict(zip(WEIGHTS, moments_v))

    me = 4 * lax.axis_index("x") + 2 * lax.axis_index("y") + lax.axis_index("c")

    assert DEPTH == 2
    xs, tabs = x[0], _rope_tables(positions[0])
    whole = [_whole] * len(GATHERED)
    slicers = [SCATTER[n][0] for n in SCATTER]
    grad_slices = [SCATTER[n][1:] for n in SCATTER]

    local = _local_weights(wd, me)
    w_slices = [(a.shape, a.dtype) for a in local[0]]
    landed0 = _gather_two_level(local[0], "gather_w0")
    flying = _exchange_start(local[1], whole, w_slices, landed0[0], "gather_w1_start")
    rep0 = dict(wd, norm_g=wd['norm_g'] + flying[-1][0, 0])
    w0 = _layer_weights(landed0, rep0, 0)
    x1, saved0 = _layer_fwd(xs, w0, tabs)
    w1 = _layer_weights(_exchange_wait(flying, whole, x1, "gather_w1_wait"), wd, 1)
    x2, saved1 = _layer_fwd(x1, w1, tabs)
    loss, dx2 = _loss_fwd_bwd(x2, loss_target[0])
    loss = loss[0, 0]

    sharded = list(SCATTER)
    nsh = len(sharded)
    small = [n for n in REPLICATED if n not in GATE_WEIGHTS and n != 'norm_g']

    def outgoing(g):
        r = _replicated_grads(g)
        return (_sharded_grads(g) + [_pack([r[n] for n in small], F32, 0)]
                + [r[n].astype(BF16).reshape(8 * LANE, LANE) for n in GATE_WEIGHTS])

    out_slicers = slicers + [_whole] * 3
    out_slices = grad_slices + [((_packed_rows([wd[n].shape[1:] for n in small]), LANE), F32)] + [((8 * LANE, LANE), BF16)] * 2
    dx1, g1 = _layer_bwd(dx2, w1, tabs, saved1)
    flying1 = _exchange_start(outgoing(g1), out_slicers, out_slices, dx1, "scatter_g1_start")
    later = {}

    names = sharded + ['small'] + list(GATE_WEIGHTS)
    sliced = [True] * nsh + [False] * 3
    by_chip = [(lambda ref, q: ref.at[q])] * nsh + [_whole] * 3

    def send_layer0(g):
        later['got1'] = _exchange_wait(flying1, out_slicers, g['w_in'], "scatter_g1_wait")
        mine = outgoing(g)
        came, own = _pair_exchange(mine, out_slicers, out_slices, sliced, "pair_g0")
        halves = [_add2(o if s else a, c if s else c[0], f"pair_sum_{n}") for n, a, c, o, s in zip(names, mine, came, own, sliced)]
        later['flying0'] = _exchange_start(halves, by_chip, out_slices, later['got1'][0], "scatter_g0_start", plan=_plan_chips, nslots=4)
        return later['flying0'][-1]

    grad_x, g0 = _layer_bwd(dx1, w0, tabs, saved0, hook=send_layer0, after=flying1[-1])
    sum1 = [_sum8(b, f"sum_{n}_1") for n, b in zip(names, later['got1'])]
    got0 = _exchange_wait(later['flying0'], by_chip, sum1[0], "scatter_g0_wait", plan=_plan_chips)
    sum0 = [_sum8(b, f"sum_{n}_0") for n, b in zip(names, got0)]
    norm_part = _pack([jnp.stack([g['norm_g'].reshape(D) for g in (g0, g1)])], F32, 0)
    norm_sum = _sum8(_exchange([norm_part], [_whole], [(norm_part.shape, F32)], "gather_norm_g")[0], "sum_norm_g")

    gsh = {n: jnp.stack([sum0[i], sum1[i]]) for i, n in enumerate(sharded)}
    gsh['w_in'] = _from_window(gsh['w_in'], me)
    gsh['w_uq'] = gsh['w_uq'][:, :, :MLA_QK]
    grep = {'norm_g': _unpack(norm_sum, [wd['norm_g'].shape], 0)[0]}
    per_layer = [_unpack(s[nsh], [wd[n].shape[1:] for n in small], 0) for s in (sum0, sum1)]
    grep.update({n: jnp.stack([per_layer[l][i] for l in range(DEPTH)]) for i, n in enumerate(small)})
    for i, n in enumerate(GATE_WEIGHTS):
        grep[n] = jnp.stack([sum0[nsh + 1 + i], sum1[nsh + 1 + i]]).reshape(wd[n].shape)

    out_g, out_d, out_m, out_v = {}, {}, {}, {}
    vecs = ['norm_g'] + small
    vshapes = [wd[n].shape for n in vecs]
    packed_g = _pack([grep[n] for n in vecs], F32, 0)
    d_, m_, v_ = _adamw(_pack([wd[n] for n in vecs], F32, 0), packed_g, _pack([md[n] for n in vecs], F32, 0),
                        _pack([vd[n] for n in vecs], F32, 0), "adamw_vectors")
    for dst, buf in ((out_d, d_), (out_m, m_), (out_v, v_)):
        dst.update(zip(vecs, _unpack(buf, vshapes, 0)))
    out_g.update({n: grep[n] for n in vecs})
    gsh.update({n: grep[n] for n in GATE_WEIGHTS})
    for n in sharded + list(GATE_WEIGHTS):
        if n in GATE_WEIGHTS:
            shp = wd[n].shape
            two = (shp[0] * shp[1] * shp[2], shp[3])
            d_, m_, v_ = _adamw(wd[n].reshape(two), gsh[n].reshape(two), md[n].reshape(two), vd[n].reshape(two), "adamw_" + n)
            out_g[n], out_d[n], out_m[n], out_v[n] = gsh[n], d_.reshape(shp), m_.reshape(shp), v_.reshape(shp)
            continue
        shp = wd[n].shape
        two = (shp[0] * shp[1], shp[2])
        d_, m_, v_ = _adamw(wd[n].reshape(two), gsh[n].reshape(two), md[n].reshape(two), vd[n].reshape(two), "adamw_" + n)
        out_g[n], out_d[n], out_m[n], out_v[n] = gsh[n], d_.reshape(shp), m_.reshape(shp), v_.reshape(shp)

    loss = lax.psum(loss, ("x", "y", "c"))
    return (loss, grad_x[None], *[out_g[n] for n in WEIGHTS], *[out_d[n] for n in WEIGHTS], *[out_m[n] for n in WEIGHTS],
            *[out_v[n] for n in WEIGHTS])
```

```python
import functools

import numpy as np
import jax
import jax.numpy as jnp
from jax import lax
from jax.experimental import pallas as pl
from jax.experimental.pallas import tpu as pltpu

F32 = jnp.float32
BF16 = jnp.bfloat16

N_DEV = 8
D = 1024
DEPTH = 2
EPS = 1e-6
ROPE_THETA = 10000.0
LRU_C = 8.0
LANE = 128
SUB = 8
IN_WIDTH = 11168
SHARD_IN = IN_WIDTH // N_DEV

C_LRUX, C_LRUG, C_CQ, C_CKV, C_KR, C_MLAG, C_DQ, C_DK, C_DV, C_DILG, C_MERGE = 0, 8, 16, 18, 19, 20, 24, 36, 48, 60, 64
ZW = 88 * LANE
KR_LANE = 64

MLA_QK = 96
MLA_SCALE = MLA_QK ** -0.5
DIL_HD = 64
DIL_SCALE = DIL_HD ** -0.5
DIL_DILATIONS = (1, 4, 16)
NK = 128

ADAM_LR, ADAM_B1, ADAM_B2, ADAM_EPS, ADAM_WD, ADAM_STEP = 0.001, 0.9, 0.999, 1e-08, 0.01, 10

NEG = -1e30
VMEM_LIMIT = 48 * 1024 * 1024


def _cp(**kw):
    return pltpu.CompilerParams(vmem_limit_bytes=VMEM_LIMIT, **kw)


def _sig(x):
    return 1.0 / (1.0 + jnp.exp(-x))


def _silu(x):
    return x * _sig(x)


def _dsilu(x):
    s = _sig(x)
    return s * (1.0 + x * (1.0 - s))


def _dot(a, b, dims):
    return lax.dot_general(a, b, (dims, ((), ())), preferred_element_type=F32)


def _nn(a, b):
    return _dot(a, b, ((1,), (0,)))


def _nt(a, b):
    return _dot(a, b, ((1,), (1,)))


def _tn(a, b):
    return _dot(a, b, ((0,), (0,)))


def _rsum(x):
    return jnp.sum(x, axis=-1, keepdims=True)


def _csum(x):
    return jnp.sum(x, axis=0, keepdims=True)


def _mm(a, b, *, mode, name, out_dtype=F32, add=None, after=None, tm=1024, tn=1024, tk=1024):
    if mode == "nn":
        (M, K), (K2, N) = a.shape, b.shape
    elif mode == "nt":
        (M, K), (N, K2) = a.shape, b.shape
    else:
        (K, M), (K2, N) = a.shape, b.shape
    assert K == K2
    tm, tn, tk = min(tm, M), min(tn, N), min(tk, K)
    assert M % tm == 0 and N % tn == 0 and K % tk == 0
    nk = K // tk
    fn = {"nn": _nn, "nt": _nt, "tn": _tn}[mode]
    has_add = add is not None

    def body(*refs):
        a_ref, b_ref = refs[0], refs[1]
        add_ref = refs[2] if has_add else None
        o_ref = refs[2 + has_add + (after is not None)]
        part = fn(a_ref[...].astype(BF16), b_ref[...].astype(BF16))

        def fin(acc):
            if has_add:
                acc = acc + add_ref[...]
            o_ref[...] = acc.astype(out_dtype)

        if nk == 1:
            fin(part)
        else:
            acc_ref = refs[-1]
            k = pl.program_id(2)

            @pl.when(k == 0)
            def _():
                acc_ref[...] = part

            @pl.when(k > 0)
            def _():
                acc_ref[...] += part

            @pl.when(k == nk - 1)
            def _():
                fin(acc_ref[...])

    a_spec = pl.BlockSpec((tk, tm), lambda i, j, k: (k, i)) if mode == "tn" else pl.BlockSpec((tm, tk), lambda i, j, k: (i, k))
    b_spec = pl.BlockSpec((tn, tk), lambda i, j, k: (j, k)) if mode == "nt" else pl.BlockSpec((tk, tn), lambda i, j, k: (k, j))
    o_spec = pl.BlockSpec((tm, tn), lambda i, j, k: (i, j))
    in_specs, args = [a_spec, b_spec], [a, b]
    if has_add:
        in_specs.append(o_spec)
        args.append(add)
    if after is not None:
        in_specs.append(pl.BlockSpec(memory_space=pl.ANY))
        args.append(after)
    return pl.pallas_call(
        body, name=name, grid=(M // tm, N // tn, nk), in_specs=in_specs, out_specs=o_spec,
        out_shape=jax.ShapeDtypeStruct((M, N), out_dtype),
        scratch_shapes=[pltpu.VMEM((tm, tn), F32)] if nk > 1 else [],
        compiler_params=_cp(dimension_semantics=("parallel", "parallel", "arbitrary")),
    )(*args)


T_ROW = 512


def _rms_in_fwd(x, g):
    S = x.shape[0]
    T = T_ROW

    def body(x_ref, g_ref, h_ref):
        xv = x_ref[...]
        r = lax.rsqrt(jnp.mean(xv * xv, axis=-1, keepdims=True) + EPS)
        h_ref[...] = (xv * r * g_ref[...]).astype(BF16)

    return pl.pallas_call(
        body, name="rms_in_fwd", grid=(S // T,),
        in_specs=[pl.BlockSpec((T, D), lambda i: (i, 0)), pl.BlockSpec((1, D), lambda i: (0, 0))],
        out_specs=pl.BlockSpec((T, D), lambda i: (i, 0)),
        out_shape=jax.ShapeDtypeStruct((S, D), BF16), compiler_params=_cp(),
    )(x, g)


def _rms_in_bwd(x, g, dh, dres):
    S = x.shape[0]
    T = T_ROW

    def body(x_ref, g_ref, dh_ref, dr_ref, dx_ref, dg_ref):
        i = pl.program_id(0)
        xv = x_ref[...]
        r = lax.rsqrt(jnp.mean(xv * xv, axis=-1, keepdims=True) + EPS)
        xn = xv * r
        dy = dh_ref[...]
        part = _csum(dy * xn)

        @pl.when(i == 0)
        def _():
            dg_ref[...] = part

        @pl.when(i > 0)
        def _():
            dg_ref[...] += part

        dxh = dy * g_ref[...]
        dx_ref[...] = dr_ref[...] + r * (dxh - xn * jnp.mean(dxh * xn, axis=-1, keepdims=True))

    row = pl.BlockSpec((T, D), lambda i: (i, 0))
    vec = pl.BlockSpec((1, D), lambda i: (0, 0))
    return pl.pallas_call(
        body, name="rms_in_bwd", grid=(S // T,), in_specs=[row, vec, row, row], out_specs=[row, vec],
        out_shape=[jax.ShapeDtypeStruct((S, D), F32), jax.ShapeDtypeStruct((1, D), F32)], compiler_params=_cp(),
    )(x, g, dh, dres)


T_LRU = 512


def _neg_expm1(y):
    ser = -y * (1.0 + y * 0.5 * (1.0 + y * (1.0 / 3.0) * (1.0 + y * 0.25 * (1.0 + y * 0.2))))
    return jnp.where(y > -0.03, ser, 1.0 - jnp.exp(y))


def _softplus_neg(lam):
    e = jnp.exp(-jnp.abs(lam))
    l1p = jnp.where(e < 0.01, e * (1.0 - e * (0.5 - e * (1.0 / 3.0 - e * 0.25))), jnp.log(1.0 + e))
    return jnp.maximum(-lam, 0.0) + l1p


def _scan_fwd(a, b, T):
    row = lax.broadcasted_iota(jnp.int32, a.shape, 0)
    d = 1
    while d < T:
        m = row >= d
        b = jnp.where(m, a * pltpu.roll(b, d, 0) + b, b)
        a = jnp.where(m, a * pltpu.roll(a, d, 0), a)
        d *= 2
    return a, b


def _scan_bwd(a, b, T):
    row = lax.broadcasted_iota(jnp.int32, a.shape, 0)
    d = 1
    while d < T:
        m = row < T - d
        b = jnp.where(m, a * pltpu.roll(b, T - d, 0) + b, b)
        a = jnp.where(m, a * pltpu.roll(a, T - d, 0), a)
        d *= 2
    return b


def _lru_common(x, prev, first, cw_ref, cb_ref, wgx_ref, bgx_ref, wga_ref, bga_ref, lam_ref, T):
    row = lax.broadcasted_iota(jnp.int32, x.shape, 0)
    prev = jnp.where(first, 0.0, prev)
    xs = []
    for j in (3, 2, 1):
        pv = jnp.tile(pltpu.roll(prev, j, 0), (T // SUB, 1))
        xs.append(jnp.where(row < j, pv, pltpu.roll(x, j, 0)))
    xs.append(x)
    xc = cb_ref[...] + cw_ref[0:1, :] * xs[0] + cw_ref[1:2, :] * xs[1] + cw_ref[2:3, :] * xs[2] + cw_ref[3:4, :] * xs[3]
    xcb = xc.astype(BF16)
    gx = _sig(_nn(xcb, wgx_ref[0]) + bgx_ref[0])
    ga = _sig(_nn(xcb, wga_ref[0]) + bga_ref[0])
    sp = _softplus_neg(lam_ref[...])
    log_a = -LRU_C * ga * sp
    a = jnp.exp(log_a)
    mult = jnp.sqrt(_neg_expm1(2.0 * log_a))
    return xs, xc, xcb, gx, ga, sp, a, mult


def _lru_specs(T, tmap):
    def at(col0):
        return pl.BlockSpec((T, LANE), lambda n, i: (tmap(i), col0 + n))

    def prev(col0):
        return pl.BlockSpec((SUB, LANE), lambda n, i: (jnp.maximum(tmap(i) * (T // SUB) - 1, 0), col0 + n))

    small = [
        pl.BlockSpec((4, LANE), lambda n, i: (0, n)),
        pl.BlockSpec((1, LANE), lambda n, i: (0, n)),
        pl.BlockSpec((1, LANE, LANE), lambda n, i: (n, 0, 0)),
        pl.BlockSpec((1, 1, LANE), lambda n, i: (n, 0, 0)),
        pl.BlockSpec((1, LANE, LANE), lambda n, i: (n, 0, 0)),
        pl.BlockSpec((1, 1, LANE), lambda n, i: (n, 0, 0)),
        pl.BlockSpec((1, LANE), lambda n, i: (0, n)),
    ]
    return at, prev, small


def _lru_fwd(zp, w):
    S = zp.shape[0]
    T = T_LRU
    at, prev, small = _lru_specs(T, lambda i: i)

    def body(x_ref, xp_ref, g_ref, cw_ref, cb_ref, wgx_ref, bgx_ref, wga_ref, bga_ref, lam_ref, hs_ref, y_ref, carry_ref):
        i = pl.program_id(1)

        @pl.when(i == 0)
        def _():
            carry_ref[...] = jnp.zeros_like(carry_ref)

        x = x_ref[...]
        _, xc, _, gx, _, _, a, mult = _lru_common(x, xp_ref[...], i == 0, cw_ref, cb_ref, wgx_ref, bgx_ref, wga_ref, bga_ref, lam_ref, T)
        A, B = _scan_fwd(a, mult * gx * xc, T)
        h = B + A * carry_ref[SUB - 1:SUB, :]
        hs_ref[...] = h
        carry_ref[...] = hs_ref[T - SUB:T, :]
        y_ref[...] = (h * _silu(g_ref[...])).astype(BF16)

    out = pl.BlockSpec((T, LANE), lambda n, i: (i, n))
    return pl.pallas_call(
        body, name="lru_fwd", grid=(8, S // T),
        in_specs=[at(C_LRUX), prev(C_LRUX), at(C_LRUG)] + small, out_specs=[out, out],
        out_shape=[jax.ShapeDtypeStruct((S, D), F32), jax.ShapeDtypeStruct((S, D), BF16)],
        scratch_shapes=[pltpu.VMEM((SUB, LANE), F32)],
        compiler_params=_cp(dimension_semantics=("parallel", "arbitrary")),
    )(zp, zp, zp, w["conv_w"], w["conv_b"], w["w_gx"], w["b_gx"], w["w_ga"], w["b_ga"], w["lam"])


def _lru_bwd(zp, hs, dy, w, dz):
    S = zp.shape[0]
    T = T_LRU
    nT = S // T
    at, prev, small = _lru_specs(T, lambda i: nT - 1 - i)

    def body(x_ref, xp_ref, g_ref, h_ref, hp_ref, dy_ref, cw_ref, cb_ref, wgx_ref, bgx_ref, wga_ref, bga_ref, lam_ref, dz_in,
             dzx_ref, dcw_ref, dcb_ref, dwgx_ref, dbgx_ref, dwga_ref, dbga_ref, dlam_ref, carry_ref, head_ref):
        del dz_in
        j = pl.program_id(1)
        it = nT - 1 - j

        @pl.when(j == 0)
        def _():
            for r in (carry_ref, head_ref, dcw_ref, dcb_ref, dwgx_ref, dbgx_ref, dwga_ref, dbga_ref, dlam_ref):
                r[...] = jnp.zeros_like(r)

        first = it == 0
        x = x_ref[...]
        xs, xc, xcb, gx, ga, sp, a, mult = _lru_common(x, xp_ref[...], first, cw_ref, cb_ref, wgx_ref, bgx_ref, wga_ref, bga_ref, lam_ref, T)
        row = lax.broadcasted_iota(jnp.int32, x.shape, 0)
        u = gx * xc
        h = h_ref[...]
        hp = jnp.where(first, 0.0, hp_ref[...])
        hm1 = jnp.where(row < 1, jnp.tile(pltpu.roll(hp, 1, 0), (T // SUB, 1)), pltpu.roll(h, 1, 0))
        dho = dy_ref[...] * _silu(g_ref[...])
        gin = jnp.where(row == T - 1, dho + carry_ref[0:1, :], dho)
        abar = jnp.where(row == T - 1, 0.0, pltpu.roll(a, T - 1, 0))
        dh = _scan_bwd(abar, gin, T)
        carry_ref[...] = (a * dh)[0:SUB, :]
        da = dh * hm1
        dmult = dh * u
        du = dh * mult
        dgx = du * xc
        dxc = du * gx
        dlog_a = da * a - dmult * a * a / mult
        dga = dlog_a * (-LRU_C * sp)
        lam = lam_ref[...]
        dlam_ref[...] += _csum(dlog_a * (-LRU_C * ga)) * (-1.0 / (1.0 + jnp.exp(lam)))
        dpa = dga * ga * (1.0 - ga)
        dpx = dgx * gx * (1.0 - gx)
        dpab, dpxb = dpa.astype(BF16), dpx.astype(BF16)
        dxc = dxc + _nt(dpxb, wgx_ref[0]) + _nt(dpab, wga_ref[0])
        dwgx_ref[0] += _tn(xcb, dpxb)
        dwga_ref[0] += _tn(xcb, dpab)
        dbgx_ref[0] += _csum(dpx)
        dbga_ref[0] += _csum(dpa)
        dcb_ref[...] += _csum(dxc)
        for k in range(4):
            dcw_ref[k:k + 1, :] += _csum(dxc * xs[k])
        head = head_ref[...]
        dx = cw_ref[3:4, :] * dxc
        for jj in (1, 2, 3):
            hv = jnp.tile(pltpu.roll(head, SUB - jj, 0), (T // SUB, 1))
            dx = dx + cw_ref[3 - jj:4 - jj, :] * jnp.where(row >= T - jj, hv, pltpu.roll(dxc, T - jj, 0))
        head_ref[...] = dxc[0:SUB, :]
        dzx_ref[...] = dx.astype(BF16)

    def acc(shape, imap):
        return pl.BlockSpec(shape, imap)

    out_specs = [
        pl.BlockSpec((T, LANE), lambda n, i: (nT - 1 - i, C_LRUX + n)),
        acc((4, LANE), lambda n, i: (0, n)), acc((1, LANE), lambda n, i: (0, n)),
        acc((1, LANE, LANE), lambda n, i: (n, 0, 0)), acc((1, 1, LANE), lambda n, i: (n, 0, 0)),
        acc((1, LANE, LANE), lambda n, i: (n, 0, 0)), acc((1, 1, LANE), lambda n, i: (n, 0, 0)),
        acc((1, LANE), lambda n, i: (0, n)),
    ]
    out_shape = [
        jax.ShapeDtypeStruct(dz.shape, BF16),
        jax.ShapeDtypeStruct((4, D), F32), jax.ShapeDtypeStruct((1, D), F32),
        jax.ShapeDtypeStruct((8, LANE, LANE), F32), jax.ShapeDtypeStruct((8, 1, LANE), F32),
        jax.ShapeDtypeStruct((8, LANE, LANE), F32), jax.ShapeDtypeStruct((8, 1, LANE), F32),
        jax.ShapeDtypeStruct((1, D), F32),
    ]
    dyspec = pl.BlockSpec((T, LANE), lambda n, i: (nT - 1 - i, n))
    hprev = pl.BlockSpec((SUB, LANE), lambda n, i: (jnp.maximum((nT - 1 - i) * (T // SUB) - 1, 0), n))
    return pl.pallas_call(
        body, name="lru_bwd", grid=(8, nT),
        in_specs=[at(C_LRUX), prev(C_LRUX), at(C_LRUG), dyspec, hprev, dyspec] + small + [pl.BlockSpec(memory_space=pl.ANY)],
        out_specs=out_specs, out_shape=out_shape,
        scratch_shapes=[pltpu.VMEM((SUB, LANE), F32), pltpu.VMEM((SUB, LANE), F32)],
        input_output_aliases={13: 0},
        compiler_params=_cp(dimension_semantics=("parallel", "arbitrary")),
    )(zp, zp, zp, hs, hs, dy, w["conv_w"], w["conv_b"], w["w_gx"], w["b_gx"], w["w_ga"], w["b_ga"], w["lam"], dz)


def _lru_gate_bwd(zp, hs, dy, dz):
    S = zp.shape[0]
    T = T_ROW

    def body(g_ref, h_ref, dy_ref, dz_in, o_ref):
        del dz_in
        o_ref[...] = (dy_ref[...] * h_ref[...] * _dsilu(g_ref[...])).astype(BF16)

    row = pl.BlockSpec((T, D), lambda i: (i, 0))
    zc = pl.BlockSpec((T, D), lambda i: (i, C_LRUG // 8))
    return pl.pallas_call(
        body, name="lru_gate_bwd", grid=(S // T,), in_specs=[zc, row, row, pl.BlockSpec(memory_space=pl.ANY)], out_specs=zc,
        out_shape=jax.ShapeDtypeStruct(dz.shape, BF16), input_output_aliases={3: 0}, compiler_params=_cp(),
    )(zp, hs, dy, dz)


def _rope_tables(pos):
    pf = pos.astype(F32)[:, None]

    def cs(d):
        inv = ROPE_THETA ** (-jnp.arange(0, d, 2, dtype=F32) / d)
        ang = pf * inv
        return jnp.cos(ang), jnp.sin(ang)

    S = pos.shape[0]
    c, s = cs(32)
    one, zero = jnp.ones((S, 64), F32), jnp.zeros((S, 16), F32)
    z32, z64 = jnp.zeros((S, 32), F32), jnp.zeros((S, 64), F32)
    mla = (jnp.concatenate([one, c, c, jnp.ones((S, 32), F32)], 1),
           jnp.concatenate([z64, zero, s, z32], 1),
           jnp.concatenate([z64, -s, zero, z32], 1))
    c, s = cs(64)
    dil = (jnp.concatenate([c, c, c, c], 1),
           jnp.concatenate([z32, s, z32, s], 1),
           jnp.concatenate([-s, z32, -s, z32], 1))
    return mla, dil


def _rope(x, C, S1, S2, sh):
    return x * C + pltpu.roll(x, sh, 1) * S1 + pltpu.roll(x, LANE - sh, 1) * S2


def _rope_t(dy, C, S1, S2, sh):
    return dy * C + pltpu.roll(dy * S1, LANE - sh, 1) + pltpu.roll(dy * S2, sh, 1)


def _lane(shape):
    return lax.broadcasted_iota(jnp.int32, shape, 1)


T_MLA = 256
TA = 512


def _zcol(T, width, col_lanes):
    assert (col_lanes * LANE) % width == 0
    return pl.BlockSpec((T, width), lambda i: (i, col_lanes * LANE // width))


def _full(shape):
    return pl.BlockSpec(shape, lambda *_: (0,) * len(shape))


def _mla_pre_fwd(zp, w, tab):
    S = zp.shape[0]
    T = T_MLA

    def body(cq_ref, ckv_ref, kr_ref, gcq_ref, gckv_ref, wuq_ref, wuk_ref, wuv_ref, gq_ref, gk_ref, C_ref, S1_ref, S2_ref,
             q_ref, k_ref, v_ref):
        cq = cq_ref[...]
        cqn = (cq * lax.rsqrt(jnp.mean(cq * cq, axis=-1, keepdims=True) + EPS) * gcq_ref[...]).astype(BF16)
        ckv = ckv_ref[...]
        ckvn = (ckv * lax.rsqrt(jnp.mean(ckv * ckv, axis=-1, keepdims=True) + EPS) * gckv_ref[...]).astype(BF16)
        q0 = _nn(cqn, wuq_ref[...])
        k0 = _nn(ckvn, wuk_ref[...])
        krb = kr_ref[...]
        C, S1, S2 = C_ref[...], S1_ref[...], S2_ref[...]
        for h in range(8):
            sl = slice(h * LANE, (h + 1) * LANE)
            xq = q0[:, sl]
            xq = xq * lax.rsqrt(_rsum(xq * xq) * (1.0 / MLA_QK) + EPS) * gq_ref[...]
            q_ref[:, sl] = _rope(xq, C, S1, S2, 16).astype(BF16)
            xk = k0[:, sl] + krb
            xk = xk * lax.rsqrt(_rsum(xk * xk) * (1.0 / MLA_QK) + EPS) * gk_ref[...]
            k_ref[:, sl] = _rope(xk, C, S1, S2, 16).astype(BF16)
        v_ref[...] = _nn(ckvn, wuv_ref[...]).astype(BF16)

    tabspec = pl.BlockSpec((T, LANE), lambda i: (i, 0))
    in_specs = [_zcol(T, 256, C_CQ), _zcol(T, LANE, C_CKV), _zcol(T, LANE, C_KR), _full((1, 256)), _full((1, LANE)),
                _full((256, 1024)), _full((LANE, 1024)), _full((LANE, 512)), _full((1, LANE)), _full((1, LANE)),
                tabspec, tabspec, tabspec]
    return pl.pallas_call(
        body, name="mla_pre_fwd", grid=(S // T,), in_specs=in_specs,
        out_specs=[pl.BlockSpec((T, 1024), lambda i: (i, 0)), pl.BlockSpec((T, 1024), lambda i: (i, 0)), pl.BlockSpec((T, 512), lambda i: (i, 0))],
        out_shape=[jax.ShapeDtypeStruct((S, 1024), BF16), jax.ShapeDtypeStruct((S, 1024), BF16), jax.ShapeDtypeStruct((S, 512), BF16)],
        compiler_params=_cp(),
    )(zp, zp, zp, w["g_cq"], w["g_ckv"], w["w_uq"], w["w_uk"], w["w_uv"], w["g_mq"], w["g_mk"], *tab)


def _mla_attn_fwd(q, k, v, zp):
    S = q.shape[0]
    nq = S // TA

    def body(q_ref, k_ref, v_ref, g_ref, o_ref, lse_ref, y_ref):
        qi = pl.program_id(1)
        lane = _lane((TA, LANE))
        rowi = lax.broadcasted_iota(jnp.int32, (TA, TA), 0)
        coli = lax.broadcasted_iota(jnp.int32, (TA, TA), 1)
        o_tot = jnp.zeros((TA, LANE), F32)
        for hh in range(2):
            cs = slice(hh * LANE, (hh + 1) * LANE)
            hm = (lane < 64) if hh == 0 else (lane >= 64)
            qh = q_ref[:, cs]

            def step(kb, carry, masked, cs=cs, hm=hm, qh=qh):
                m, l, acc = carry
                off = pl.multiple_of(kb * TA, TA)
                kh = k_ref[pl.ds(off, TA), cs]
                vv = v_ref[pl.ds(off, TA), :]
                vh = jnp.where(hm, vv, jnp.zeros_like(vv))
                s = _nt(qh, kh) * MLA_SCALE
                if masked:
                    s = jnp.where(rowi >= coli, s, NEG)
                m_new = jnp.maximum(m, jnp.max(s, axis=-1, keepdims=True))
                alpha = jnp.exp(m - m_new)
                p = jnp.exp(s - m_new)
                l = alpha * l + _rsum(p)
                acc = alpha * acc + _nn(p.astype(BF16), vh)
                return m_new, l, acc

            init = (jnp.full((TA, 1), NEG, F32), jnp.zeros((TA, 1), F32), jnp.zeros((TA, LANE), F32))
            carry = lax.fori_loop(0, qi, lambda kb, c: step(kb, c, False), init)
            m, l, acc = step(qi, carry, True)
            o_tot = o_tot + acc / l
            lse_ref[:, cs] = jnp.broadcast_to(m + jnp.log(l), (TA, LANE))
        o_ref[...] = o_tot
        y_ref[...] = (o_tot * _silu(g_ref[...])).astype(BF16)

    blk = pl.BlockSpec((TA, LANE), lambda p, i: (i, p))
    return pl.pallas_call(
        body, name="mla_attn_fwd", grid=(4, nq),
        in_specs=[pl.BlockSpec((TA, 256), lambda p, i: (i, p)), pl.BlockSpec((S, 256), lambda p, i: (0, p)),
                  pl.BlockSpec((S, LANE), lambda p, i: (0, p)), pl.BlockSpec((TA, LANE), lambda p, i: (i, C_MLAG + p))],
        out_specs=[blk, pl.BlockSpec((TA, 256), lambda p, i: (i, p)), blk],
        out_shape=[jax.ShapeDtypeStruct((S, 512), F32), jax.ShapeDtypeStruct((S, 1024), F32), jax.ShapeDtypeStruct((S, 512), BF16)],
        compiler_params=_cp(dimension_semantics=("parallel", "arbitrary")),
    )(q, k, v, zp)


def _mla_post_bwd(zp, o, dy, dz):
    S = zp.shape[0]
    T = T_ROW

    def body(g_ref, o_ref, dy_ref, dz_in, dz_ref, do_ref, D_ref):
        del dz_in
        g, o_, dy_ = g_ref[...], o_ref[...], dy_ref[...]
        do = dy_ * _silu(g)
        do_ref[...] = do.astype(BF16)
        dz_ref[...] = (dy_ * o_ * _dsilu(g)).astype(BF16)
        prod = do * o_
        lane = _lane((T, LANE))
        for p in range(4):
            pr = prod[:, p * LANE:(p + 1) * LANE]
            da = _rsum(jnp.where(lane < 64, pr, 0.0))
            db = _rsum(jnp.where(lane >= 64, pr, 0.0))
            D_ref[:, 2 * p * LANE:(2 * p + 1) * LANE] = jnp.broadcast_to(da, (T, LANE))
            D_ref[:, (2 * p + 1) * LANE:(2 * p + 2) * LANE] = jnp.broadcast_to(db, (T, LANE))

    row = pl.BlockSpec((T, 512), lambda i: (i, 0))
    zc = _zcol(T, 512, C_MLAG)
    return pl.pallas_call(
        body, name="mla_post_bwd", grid=(S // T,), in_specs=[zc, row, row, pl.BlockSpec(memory_space=pl.ANY)],
        out_specs=[zc, row, pl.BlockSpec((T, 1024), lambda i: (i, 0))],
        out_shape=[jax.ShapeDtypeStruct(dz.shape, BF16), jax.ShapeDtypeStruct((S, 512), BF16), jax.ShapeDtypeStruct((S, 1024), F32)],
        input_output_aliases={3: 0}, compiler_params=_cp(),
    )(zp, o, dy, dz)


def _mla_attn_bwd(q, k, v, do, lse, Dr):
    S = q.shape[0]
    nq = S // TA

    def body(q_ref, do_ref, lse_ref, D_ref, k_ref, v_ref, dq_ref, dk_ref, dv_ref):
        ki = pl.program_id(1)

        @pl.when(ki == 0)
        def _():
            dq_ref[...] = jnp.zeros_like(dq_ref)

        lane = _lane((TA, LANE))
        rowi = lax.broadcasted_iota(jnp.int32, (TA, TA), 0)
        coli = lax.broadcasted_iota(jnp.int32, (TA, TA), 1)
        dv_tot = jnp.zeros((TA, LANE), F32)
        for hh in range(2):
            cs = slice(hh * LANE, (hh + 1) * LANE)
            hm = (lane < 64) if hh == 0 else (lane >= 64)
            kh = k_ref[:, cs]
            vv = v_ref[...]
            vm = jnp.where(hm, vv, jnp.zeros_like(vv))

            def step(qb, carry, masked, cs=cs, kh=kh, vm=vm):
                dk_acc, dv_acc = carry
                off = pl.multiple_of(qb * TA, TA)
                qh = q_ref[pl.ds(off, TA), cs]
                doh = do_ref[pl.ds(off, TA), :]
                ls = jnp.tile(lse_ref[pl.ds(off, TA), cs], (1, TA // LANE))
                dd = jnp.tile(D_ref[pl.ds(off, TA), cs], (1, TA // LANE))
                s = _nt(qh, kh) * MLA_SCALE
                if masked:
                    s = jnp.where(rowi >= coli, s, NEG)
                p = jnp.exp(s - ls)
                dp = _nt(doh, vm)
                ds = (p * (dp - dd) * MLA_SCALE).astype(BF16)
                dv_acc = dv_acc + _tn(p.astype(BF16), doh)
                dk_acc = dk_acc + _tn(ds, qh)
                dq_ref[pl.ds(off, TA), cs] += _nn(ds, kh)
                return dk_acc, dv_acc

            z = jnp.zeros((TA, LANE), F32)
            carry = step(ki, (z, z), True)
            dk_acc, dv_acc = lax.fori_loop(ki + 1, nq, lambda qb, c: step(qb, c, False), carry)
            dk_ref[:, cs] = dk_acc
            dv_tot = dv_tot + jnp.where(hm, dv_acc, 0.0)
        dv_ref[...] = dv_tot

    pair = pl.BlockSpec((S, 256), lambda p, i: (0, p))
    return pl.pallas_call(
        body, name="mla_attn_bwd", grid=(4, nq),
        in_specs=[pair, pl.BlockSpec((S, LANE), lambda p, i: (0, p)), pair, pair,
                  pl.BlockSpec((TA, 256), lambda p, i: (i, p)), pl.BlockSpec((TA, LANE), lambda p, i: (i, p))],
        out_specs=[pair, pl.BlockSpec((TA, 256), lambda p, i: (i, p)), pl.BlockSpec((TA, LANE), lambda p, i: (i, p))],
        out_shape=[jax.ShapeDtypeStruct((S, 1024), F32), jax.ShapeDtypeStruct((S, 1024), F32), jax.ShapeDtypeStruct((S, 512), F32)],
        compiler_params=_cp(dimension_semantics=("parallel", "arbitrary")),
    )(q, do, lse, Dr, k, v)


def _mla_pre_bwd(zp, dq, dk, dv, w, tab, dz):
    S = zp.shape[0]
    T = T_MLA

    def body(cq_ref, ckv_ref, kr_ref, dq_ref, dk_ref, dv_ref, gcq_ref, gckv_ref, wuq_ref, wuk_ref, wuv_ref, gq_ref, gk_ref,
             C_ref, S1_ref, S2_ref, dz_in, dz_ref, dwuq_ref, dwuk_ref, dwuv_ref, dgcq_ref, dgckv_ref, dgq_ref, dgk_ref):
        del dz_in
        i = pl.program_id(0)

        @pl.when(i == 0)
        def _():
            for r in (dwuq_ref, dwuk_ref, dwuv_ref, dgcq_ref, dgckv_ref, dgq_ref, dgk_ref):
                r[...] = jnp.zeros_like(r)

        cq = cq_ref[...]
        rq = lax.rsqrt(jnp.mean(cq * cq, axis=-1, keepdims=True) + EPS)
        cqh = cq * rq
        cqn = (cqh * gcq_ref[...]).astype(BF16)
        ckv = ckv_ref[...]
        rkv = lax.rsqrt(jnp.mean(ckv * ckv, axis=-1, keepdims=True) + EPS)
        ckvh = ckv * rkv
        ckvn = (ckvh * gckv_ref[...]).astype(BF16)
        q0 = _nn(cqn, wuq_ref[...])
        k0 = _nn(ckvn, wuk_ref[...])
        krb = kr_ref[...]
        C, S1, S2 = C_ref[...], S1_ref[...], S2_ref[...]
        gq, gk = gq_ref[...], gk_ref[...]

        def head_bwd(x, dy, g):
            r = lax.rsqrt(_rsum(x * x) * (1.0 / MLA_QK) + EPS)
            xn = x * r
            dyn = _rope_t(dy, C, S1, S2, 16)
            dxh = dyn * g
            return r * (dxh - xn * _rsum(dxh * xn) * (1.0 / MLA_QK)), _csum(dyn * xn)

        dq0, dk0 = [], []
        dgq_acc = jnp.zeros((1, LANE), F32)
        dgk_acc = jnp.zeros((1, LANE), F32)
        dkr = jnp.zeros((T, LANE), F32)
        for h in range(8):
            sl = slice(h * LANE, (h + 1) * LANE)
            dxq, gq_p = head_bwd(q0[:, sl], dq_ref[:, sl], gq)
            dxk, gk_p = head_bwd(k0[:, sl] + krb, dk_ref[:, sl], gk)
            dq0.append(dxq.astype(BF16))
            dk0.append(dxk.astype(BF16))
            dkr = dkr + dxk
            dgq_acc = dgq_acc + gq_p
            dgk_acc = dgk_acc + gk_p
        dgq_ref[...] += dgq_acc
        dgk_ref[...] += dgk_acc
        dq0 = jnp.concatenate(dq0, axis=1)
        dk0 = jnp.concatenate(dk0, axis=1)
        dvb = dv_ref[...].astype(BF16)
        dwuq_ref[...] += _tn(cqn, dq0)
        dwuk_ref[...] += _tn(ckvn, dk0)
        dwuv_ref[...] += _tn(ckvn, dvb)
        dcqn = _nt(dq0, wuq_ref[...])
        dckvn = _nt(dk0, wuk_ref[...]) + _nt(dvb, wuv_ref[...])
        dgcq_ref[...] += _csum(dcqn * cqh)
        dgckv_ref[...] += _csum(dckvn * ckvh)
        dxh = dcqn * gcq_ref[...]
        dz_ref[:, 0:256] = (rq * (dxh - cqh * jnp.mean(dxh * cqh, axis=-1, keepdims=True))).astype(BF16)
        dxh = dckvn * gckv_ref[...]
        dz_ref[:, 256:384] = (rkv * (dxh - ckvh * jnp.mean(dxh * ckvh, axis=-1, keepdims=True))).astype(BF16)
        lane = _lane((T, LANE))
        dz_ref[:, 384:512] = jnp.where((lane >= KR_LANE) & (lane < KR_LANE + 32), dkr, 0.0).astype(BF16)

    tabspec = pl.BlockSpec((T, LANE), lambda i: (i, 0))
    in_specs = [_zcol(T, 256, C_CQ), _zcol(T, LANE, C_CKV), _zcol(T, LANE, C_KR),
                pl.BlockSpec((T, 1024), lambda i: (i, 0)), pl.BlockSpec((T, 1024), lambda i: (i, 0)), pl.BlockSpec((T, 512), lambda i: (i, 0)),
                _full((1, 256)), _full((1, LANE)), _full((256, 1024)), _full((LANE, 1024)), _full((LANE, 512)), _full((1, LANE)), _full((1, LANE)),
                tabspec, tabspec, tabspec, pl.BlockSpec(memory_space=pl.ANY)]
    out_specs = [_zcol(T, 512, C_CQ), _full((256, 1024)), _full((LANE, 1024)), _full((LANE, 512)), _full((1, 256)), _full((1, LANE)),
                 _full((1, LANE)), _full((1, LANE))]
    out_shape = [jax.ShapeDtypeStruct(dz.shape, BF16), jax.ShapeDtypeStruct((256, 1024), F32), jax.ShapeDtypeStruct((LANE, 1024), F32),
                 jax.ShapeDtypeStruct((LANE, 512), F32), jax.ShapeDtypeStruct((1, 256), F32), jax.ShapeDtypeStruct((1, LANE), F32),
                 jax.ShapeDtypeStruct((1, LANE), F32), jax.ShapeDtypeStruct((1, LANE), F32)]
    return pl.pallas_call(
        body, name="mla_pre_bwd", grid=(S // T,), in_specs=in_specs, out_specs=out_specs, out_shape=out_shape,
        input_output_aliases={16: 0}, compiler_params=_cp(),
    )(zp, zp, zp, dq, dk, dv, w["g_cq"], w["g_ckv"], w["w_uq"], w["w_uk"], w["w_uv"], w["g_mq"], w["g_mk"], *tab, dz)


T_DIL = 256


def _head_stats(x, lane):
    sq = x * x
    sa = _rsum(jnp.where(lane < 64, sq, 0.0))
    sb = _rsum(jnp.where(lane >= 64, sq, 0.0))
    return lax.rsqrt(jnp.where(lane < 64, sa, sb) * (1.0 / DIL_HD) + EPS)


def _head_sum(x, lane):
    sa = _rsum(jnp.where(lane < 64, x, 0.0))
    sb = _rsum(jnp.where(lane >= 64, x, 0.0))
    return jnp.where(lane < 64, sa, sb)


def _dil_pre_fwd(zp, w, tab):
    S = zp.shape[0]
    T = T_DIL

    def body(q_ref, k_ref, gq_ref, gk_ref, C_ref, S1_ref, S2_ref, qo_ref, ko_ref):
        C, S1, S2 = C_ref[...], S1_ref[...], S2_ref[...]
        lane = _lane((T, LANE))
        for b in range(12):
            sl = slice(b * LANE, (b + 1) * LANE)
            x = q_ref[:, sl]
            qo_ref[:, sl] = _rope(x * _head_stats(x, lane) * gq_ref[...], C, S1, S2, 32)
            x = k_ref[:, sl]
            ko_ref[:, sl] = _rope(x * _head_stats(x, lane) * gk_ref[...], C, S1, S2, 32)

    tabspec = pl.BlockSpec((T, LANE), lambda i: (i, 0))
    out = pl.BlockSpec((T, 1536), lambda i: (i, 0))
    return pl.pallas_call(
        body, name="dil_pre_fwd", grid=(S // T,),
        in_specs=[_zcol(T, 1536, C_DQ), _zcol(T, 1536, C_DK), _full((1, LANE)), _full((1, LANE)), tabspec, tabspec, tabspec],
        out_specs=[out, out], out_shape=[jax.ShapeDtypeStruct((S, 1536), F32)] * 2, compiler_params=_cp(),
    )(zp, zp, w["g_dq"], w["g_dk"], *tab)


DIL_ROWS = 2048


def _dil_geometry(g, S):
    d = DIL_DILATIONS[g]
    P = NK * d
    return d, P, DIL_ROWS // P, S // P


def _dil_rows(start, d, blocks=1):
    return pl.ds(pl.multiple_of(start, NK), blocks * NK) if d == 1 else pl.ds(start, blocks * NK, stride=d)


def _dil_specs(g, S, col0):
    _, P, m, nb = _dil_geometry(g, S)
    cur = pl.BlockSpec((DIL_ROWS, LANE), lambda sb, c: (sb, col0 + c))
    prv = pl.BlockSpec((P, LANE), lambda sb, c: (jnp.maximum(sb * m - 1, 0), col0 + c))
    nxt = pl.BlockSpec((P, LANE), lambda sb, c: (jnp.minimum((sb + 1) * m, nb - 1), col0 + c))
    return cur, prv, nxt


def _dil_attn_fwd(q, k, zp, g):
    S = q.shape[0]
    d, P, m, nb = _dil_geometry(g, S)
    R = DIL_ROWS

    def body(q_ref, kc_ref, kp_ref, vc_ref, vp_ref, o_ref, lse_ref, *scr):
        sb = pl.program_id(0)
        if m > 1:
            ks_ref, vs_ref = scr
            ks_ref[0:P, :] = kp_ref[...]
            ks_ref[P:P + R, :] = kc_ref[...]
            vs_ref[0:P, :] = vp_ref[...]
            vs_ref[P:P + R, :] = vc_ref[...]
        lane = _lane((NK, LANE))

        def unit(u, carry):
            j = u // d
            start = j * P + (u - j * d)
            rows = _dil_rows(start, d)
            if m > 1:
                k2, v2 = ks_ref[_dil_rows(start, d, 2), :], vs_ref[_dil_rows(start, d, 2), :]
            else:
                k2 = jnp.concatenate([kp_ref[rows, :], kc_ref[rows, :]], axis=0)
                v2 = jnp.concatenate([vp_ref[rows, :], vc_ref[rows, :]], axis=0)
            k2, v2 = k2.astype(BF16), v2.astype(BF16)
            q_ = q_ref[rows, :].astype(BF16)
            row = lax.broadcasted_iota(jnp.int32, (NK, 2 * NK), 0)
            col = lax.broadcasted_iota(jnp.int32, (NK, 2 * NK), 1)
            band = (col >= row) & (col <= row + NK) & ((col >= NK) | (sb * m + j > 0))
            lane2 = _lane((2 * NK, LANE))
            zb, zv = jnp.zeros_like(q_), jnp.zeros_like(v2)
            o_tot = jnp.zeros((NK, LANE), F32)
            lse_tot = jnp.zeros((NK, LANE), F32)
            for hh in range(2):
                hm = (lane < 64) if hh == 0 else (lane >= 64)
                hm2 = (lane2 < 64) if hh == 0 else (lane2 >= 64)
                s_ = jnp.where(band, _nt(jnp.where(hm, q_, zb), k2) * DIL_SCALE, NEG)
                mx = jnp.max(s_, axis=-1, keepdims=True)
                e = jnp.exp(s_ - mx)
                den = _rsum(e)
                o_tot = o_tot + _nn(e.astype(BF16), jnp.where(hm2, v2, zv)) / den
                lse_tot = jnp.where(hm, mx + jnp.log(den), lse_tot)
            o_ref[rows, :] = o_tot
            lse_ref[rows, :] = lse_tot
            return carry

        lax.fori_loop(0, R // NK, unit, 0, unroll=8)

    qcur, qprv, _ = _dil_specs(g, S, 4 * g)
    vcur, vprv, _ = _dil_specs(g, S, C_DV + 4 * g)
    out = pl.BlockSpec((R, LANE), lambda sb, c: (sb, c))
    return pl.pallas_call(
        body, name=f"dil_attn_fwd{g}", grid=(S // R, 4), in_specs=[qcur, qcur, qprv, vcur, vprv], out_specs=[out, out],
        out_shape=[jax.ShapeDtypeStruct((S, 512), F32)] * 2,
        scratch_shapes=[pltpu.VMEM((P + R, LANE), F32)] * 2 if m > 1 else [], compiler_params=_cp(),
    )(q, k, k, zp, zp)


def _dil_combine(os_, ls_, zp):
    S = zp.shape[0]
    T = T_ROW

    def body(o0, o1, o2, l0, l1, l2, g_ref, oc_ref, L_ref, y_ref):
        a, b, c = l0[...], l1[...], l2[...]
        mx = jnp.maximum(jnp.maximum(a, b), c)
        ea, eb, ec = jnp.exp(a - mx), jnp.exp(b - mx), jnp.exp(c - mx)
        den = ea + eb + ec
        oc = (ea * o0[...] + eb * o1[...] + ec * o2[...]) / den
        oc_ref[...] = oc
        L_ref[...] = mx + jnp.log(den)
        y_ref[...] = (oc * _silu(g_ref[...])).astype(BF16)

    row = pl.BlockSpec((T, 512), lambda i: (i, 0))
    return pl.pallas_call(
        body, name="dil_combine", grid=(S // T,), in_specs=[row] * 6 + [_zcol(T, 512, C_DILG)], out_specs=[row, row, row],
        out_shape=[jax.ShapeDtypeStruct((S, 512), F32), jax.ShapeDtypeStruct((S, 512), F32), jax.ShapeDtypeStruct((S, 512), BF16)],
        compiler_params=_cp(),
    )(*os_, *ls_, zp)


def _dil_comb_bwd(zp, oc, dy, dz):
    S = zp.shape[0]
    T = T_ROW

    def body(g_ref, o_ref, dy_ref, dz_in, dz_ref, do_ref, D_ref):
        del dz_in
        g, o_, dy_ = g_ref[...], o_ref[...], dy_ref[...]
        do = dy_ * _silu(g)
        do_ref[...] = do
        dz_ref[...] = (dy_ * o_ * _dsilu(g)).astype(BF16)
        lane = _lane((T, LANE))
        for p in range(4):
            sl = slice(p * LANE, (p + 1) * LANE)
            D_ref[:, sl] = _head_sum(do[:, sl] * o_[:, sl], lane)

    row = pl.BlockSpec((T, 512), lambda i: (i, 0))
    zc = _zcol(T, 512, C_DILG)
    return pl.pallas_call(
        body, name="dil_comb_bwd", grid=(S // T,), in_specs=[zc, row, row, pl.BlockSpec(memory_space=pl.ANY)], out_specs=[zc, row, row],
        out_shape=[jax.ShapeDtypeStruct(dz.shape, BF16), jax.ShapeDtypeStruct((S, 512), F32), jax.ShapeDtypeStruct((S, 512), F32)],
        input_output_aliases={3: 0}, compiler_params=_cp(),
    )(zp, oc, dy, dz)


def _dil_attn_bwd(q, k, zp, do, L, Dr, g):
    S = q.shape[0]
    d, P, m, nb = _dil_geometry(g, S)
    R = DIL_ROWS
    n_q, n_k = 4, 2

    def body(*refs):
        q_side = refs[0:2 * n_q]
        k_side = refs[2 * n_q:2 * n_q + 2 * n_k]
        dq_ref, dk_ref, dv_ref = refs[2 * n_q + 2 * n_k:2 * n_q + 2 * n_k + 3]
        scr = refs[2 * n_q + 2 * n_k + 3:]
        sb = pl.program_id(0)
        if m > 1:
            for a in range(n_q):
                scr[a][0:R, :] = q_side[2 * a][...]
                scr[a][R:R + P, :] = q_side[2 * a + 1][...]
            for a in range(n_k):
                scr[n_q + a][0:P, :] = k_side[2 * a + 1][...]
                scr[n_q + a][P:P + R, :] = k_side[2 * a][...]
        lane = _lane((NK, LANE))

        def unit(u, carry):
            j = u // d
            start = j * P + (u - j * d)
            rows = _dil_rows(start, d)
            if m > 1:
                rows_b = _dil_rows(start + P, d)
                q2, do2, L2, D2 = [scr[a][_dil_rows(start, d, 2), :] for a in range(n_q)]
                kp, vp = [scr[n_q + a][rows, :] for a in range(n_k)]
                kc, vc = [scr[n_q + a][rows_b, :] for a in range(n_k)]
            else:
                q2, do2, L2, D2 = [jnp.concatenate([q_side[2 * a][rows, :], q_side[2 * a + 1][rows, :]], axis=0) for a in range(n_q)]
                kc, vc = [k_side[2 * a][rows, :] for a in range(n_k)]
                kp, vp = [k_side[2 * a + 1][rows, :] for a in range(n_k)]
            q2, do2 = q2.astype(BF16), do2.astype(BF16)
            kc, kp, vc, vp = kc.astype(BF16), kp.astype(BF16), vc.astype(BF16), vp.astype(BF16)
            n = sb * m + j
            row2 = lax.broadcasted_iota(jnp.int32, (2 * NK, NK), 0)
            col2 = lax.broadcasted_iota(jnp.int32, (2 * NK, NK), 1)
            m2 = ((row2 < NK) & (col2 <= row2)) | ((row2 >= NK) & (col2 >= row2 - NK) & (n < nb - 1))
            row = lax.broadcasted_iota(jnp.int32, (NK, NK), 0)
            col = lax.broadcasted_iota(jnp.int32, (NK, NK), 1)
            mp = (col >= row) & (n > 0)
            lane2 = _lane((2 * NK, LANE))
            zq, zb = jnp.zeros_like(q2), jnp.zeros_like(kc)
            dq_tot = jnp.zeros((NK, LANE), F32)
            dk_tot = jnp.zeros((NK, LANE), F32)
            dv_tot = jnp.zeros((NK, LANE), F32)
            for hh in range(2):
                hm = (lane < 64) if hh == 0 else (lane >= 64)
                hm2 = (lane2 < 64) if hh == 0 else (lane2 >= 64)
                Lb = jnp.where(hm2, L2, pltpu.roll(L2, 64, 1))
                Db = jnp.where(hm2, D2, pltpu.roll(D2, 64, 1))
                qm2 = jnp.where(hm2, q2, zq)
                vcm = jnp.where(hm, vc, zb)
                vpm = jnp.where(hm, vp, zb)
                p2 = jnp.exp(jnp.where(m2, _nt(qm2, kc) * DIL_SCALE, NEG) - Lb)
                ds2 = (p2 * (_nt(do2, vcm) - Db) * DIL_SCALE).astype(BF16)
                dk_tot = dk_tot + _tn(ds2, qm2)
                dv_tot = dv_tot + jnp.where(hm, _tn(p2.astype(BF16), do2), 0.0)
                pp = jnp.exp(jnp.where(mp, _nt(qm2[0:NK], kp) * DIL_SCALE, NEG) - Lb[0:NK])
                dsp = (pp * (_nt(do2[0:NK], vpm) - Db[0:NK]) * DIL_SCALE).astype(BF16)
                dq_tot = dq_tot + jnp.where(hm, _nn(ds2[0:NK], kc) + _nn(dsp, kp), 0.0)
            dq_ref[rows, :] = dq_tot
            dk_ref[rows, :] = dk_tot
            dv_ref[rows, :] = dv_tot
            return carry

        lax.fori_loop(0, R // NK, unit, 0, unroll=8)

    qcur, qprv, qnxt = _dil_specs(g, S, 4 * g)
    vcur, vprv, _ = _dil_specs(g, S, C_DV + 4 * g)
    ocur, _, onxt = _dil_specs(g, S, 0)
    out = pl.BlockSpec((R, LANE), lambda sb, c: (sb, c))
    scratch = [pltpu.VMEM((P + R, LANE), F32)] * (n_q + n_k) if m > 1 else []
    return pl.pallas_call(
        body, name=f"dil_attn_bwd{g}", grid=(S // R, 4),
        in_specs=[qcur, qnxt, ocur, onxt, ocur, onxt, ocur, onxt, qcur, qprv, vcur, vprv],
        out_specs=[out, out, out], out_shape=[jax.ShapeDtypeStruct((S, 512), F32)] * 3, scratch_shapes=scratch, compiler_params=_cp(),
    )(q, q, do, do, L, L, Dr, Dr, k, k, zp, zp)


def _dil_pre_bwd(zp, dys, g, tab, dz, col, name):
    S = zp.shape[0]
    T = T_DIL

    def body(x_ref, dy0_ref, dy1_ref, dy2_ref, g_ref, C_ref, S1_ref, S2_ref, dz_in, dz_ref, dg_ref):
        del dz_in
        i = pl.program_id(0)
        C, S1, S2 = C_ref[...], S1_ref[...], S2_ref[...]
        lane = _lane((T, LANE))
        gv = g_ref[...]
        acc = jnp.zeros((1, LANE), F32)
        for b in range(12):
            sl = slice(b * LANE, (b + 1) * LANE)
            x = x_ref[:, sl]
            r = _head_stats(x, lane)
            xn = x * r
            dy_ref = (dy0_ref, dy1_ref, dy2_ref)[b // 4]
            dyn = _rope_t(dy_ref[:, (b % 4) * LANE:(b % 4 + 1) * LANE], C, S1, S2, 32)
            acc = acc + _csum(dyn * xn)
            dxh = dyn * gv
            dz_ref[:, sl] = (r * (dxh - xn * _head_sum(dxh * xn, lane) * (1.0 / DIL_HD))).astype(BF16)

        @pl.when(i == 0)
        def _():
            dg_ref[...] = acc

        @pl.when(i > 0)
        def _():
            dg_ref[...] += acc

    tabspec = pl.BlockSpec((T, LANE), lambda i: (i, 0))
    zc = _zcol(T, 1536, col)
    grp = pl.BlockSpec((T, 512), lambda i: (i, 0))
    return pl.pallas_call(
        body, name=name, grid=(S // T,),
        in_specs=[zc, grp, grp, grp, _full((1, LANE)), tabspec, tabspec, tabspec, pl.BlockSpec(memory_space=pl.ANY)],
        out_specs=[zc, _full((1, LANE))], out_shape=[jax.ShapeDtypeStruct(dz.shape, BF16), jax.ShapeDtypeStruct((1, LANE), F32)],
        input_output_aliases={8: 0}, compiler_params=_cp(),
    )(zp, *dys, g, *tab, dz)


def _dil_dv_into(dvs, dz):
    S = dz.shape[0]
    T = T_ROW

    def body(s0, s1, s2, dz_in, o_ref):
        del dz_in
        for gi, s in enumerate((s0, s1, s2)):
            o_ref[:, gi * 512:(gi + 1) * 512] = s[...].astype(BF16)

    grp = pl.BlockSpec((T, 512), lambda i: (i, 0))
    return pl.pallas_call(
        body, name="dil_dv", grid=(S // T,), in_specs=[grp, grp, grp, pl.BlockSpec(memory_space=pl.ANY)],
        out_specs=_zcol(T, 1536, C_DV), out_shape=jax.ShapeDtypeStruct(dz.shape, BF16), input_output_aliases={3: 0}, compiler_params=_cp(),
    )(*dvs, dz)


T_MRG = 256


def _merge_fwd(P, zp, b_merge):
    S = zp.shape[0]
    T = T_MRG

    def body(p0, p1, p2, m0, m1, m2, b_ref, o_ref):
        acc = jnp.zeros((T, D), F32)
        for j, (p, m) in enumerate(((p0, m0), (p1, m1), (p2, m2))):
            acc = acc + _sig(m[...] + b_ref[:, j * D:(j + 1) * D]) * p[...]
        o_ref[...] = acc.astype(BF16)

    row = pl.BlockSpec((T, D), lambda i: (i, 0))
    return pl.pallas_call(
        body, name="merge_fwd", grid=(S // T,),
        in_specs=[row, row, row] + [_zcol(T, D, C_MERGE + 8 * j) for j in range(3)] + [_full((1, 3 * D))], out_specs=row,
        out_shape=jax.ShapeDtypeStruct((S, D), BF16), compiler_params=_cp(),
    )(*P, zp, zp, zp, b_merge)


def _merge_bwd(dm, Pj, zp, bj, dz, j):
    S = zp.shape[0]
    T = T_MRG

    def body(dm_ref, p_ref, m_ref, b_ref, dz_in, dz_ref, dp_ref, db_ref):
        del dz_in
        i = pl.program_id(0)
        g = _sig(m_ref[...] + b_ref[...])
        dmv = dm_ref[...]
        dp_ref[...] = (dmv * g).astype(BF16)
        dg = dmv * p_ref[...] * g * (1.0 - g)
        dz_ref[...] = dg.astype(BF16)
        part = _csum(dg)

        @pl.when(i == 0)
        def _():
            db_ref[...] = part

        @pl.when(i > 0)
        def _():
            db_ref[...] += part

    row = pl.BlockSpec((T, D), lambda i: (i, 0))
    zc = _zcol(T, D, C_MERGE + 8 * j)
    return pl.pallas_call(
        body, name=f"merge_bwd{j}", grid=(S // T,), in_specs=[row, row, zc, _full((1, D)), pl.BlockSpec(memory_space=pl.ANY)],
        out_specs=[zc, row, _full((1, D))],
        out_shape=[jax.ShapeDtypeStruct(dz.shape, BF16), jax.ShapeDtypeStruct((S, D), BF16), jax.ShapeDtypeStruct((1, D), F32)],
        input_output_aliases={4: 0}, compiler_params=_cp(),
    )(dm, Pj, zp, bj, dz)


def _loss_fwd_bwd(y, target):
    S = y.shape[0]
    T = T_ROW

    def body(y_ref, t_ref, loss_ref, dy_ref):
        i = pl.program_id(0)
        err = y_ref[...] - t_ref[...]
        dy_ref[...] = err * (1.0 / D)
        part = jnp.sum(err * err, keepdims=True).reshape(1, 1) * (0.5 / D)

        @pl.when(i == 0)
        def _():
            loss_ref[...] = part

        @pl.when(i > 0)
        def _():
            loss_ref[...] += part

    row = pl.BlockSpec((T, D), lambda i: (i, 0))
    return pl.pallas_call(
        body, name="loss", grid=(S // T,), in_specs=[row, row], out_specs=[_full((1, 1)), row],
        out_shape=[jax.ShapeDtypeStruct((1, 1), F32), jax.ShapeDtypeStruct((S, D), F32)], compiler_params=_cp(),
    )(y, target)


def _layer_fwd(x, w, tabs):
    mla_tab, dil_tab = tabs
    S = x.shape[0]
    h = _rms_in_fwd(x, w["norm_g"])
    zp = _mm(h, w["w_in"], mode="nn", name="in_proj")
    hs, y_lru = _lru_fwd(zp, w)
    q, k, v = _mla_pre_fwd(zp, w, mla_tab)
    o_mla, lse, y_mla = _mla_attn_fwd(q, k, v, zp)
    qd, kd = _dil_pre_fwd(zp, w, dil_tab)
    og, lg = zip(*[_dil_attn_fwd(qd, kd, zp, g) for g in range(len(DIL_DILATIONS))])
    oc, L, y_dil = _dil_combine(og, lg, zp)
    P = [_mm(y_lru, w["w_lru_o"], mode="nn", name="lru_out"), _mm(y_mla, w["w_mla_o"], mode="nn", name="mla_out"),
         _mm(y_dil, w["w_dil_o"], mode="nn", name="dil_out")]
    merged = _merge_fwd(P, zp, w["b_merge"])
    x_out = _mm(merged, w["w_out"], mode="nn", name="out_proj", add=x)
    saved = dict(x=x, h=h, zp=zp, hs=hs, y=(y_lru, y_mla, y_dil), q=q, k=k, v=v, o_mla=o_mla, lse=lse, qd=qd, kd=kd, oc=oc, L=L, P=P,
                 merged=merged)
    return x_out, saved


def _layer_bwd(dout, w, tabs, sv, hook=None, after=None):
    mla_tab, dil_tab = tabs
    zp = sv["zp"]
    S = zp.shape[0]
    g = {}
    dm = _mm(dout, w["w_out"], mode="nt", name="d_merged", after=after)
    g["w_out"] = _mm(sv["merged"], dout, mode="tn", name="dw_out", out_dtype=BF16)
    dz = lax.empty((S, ZW), BF16)
    dP, db = [], []
    for j in range(3):
        dz, dpj, dbj = _merge_bwd(dm, sv["P"][j], zp, w["b_merge"][:, j * D:(j + 1) * D], dz, j)
        dP.append(dpj)
        db.append(dbj)
    g["b_merge"] = jnp.concatenate(db, axis=1)
    names = ("w_lru_o", "w_mla_o", "w_dil_o")
    dy = []
    for j in range(3):
        dy.append(_mm(dP[j], w[names[j]], mode="nt", name="dy_" + names[j]))
        g[names[j]] = _mm(sv["y"][j], dP[j], mode="tn", name="d" + names[j], out_dtype=BF16)
    dz = _lru_gate_bwd(zp, sv["hs"], dy[0], dz)
    dz, g["conv_w"], g["conv_b"], g["w_gx"], g["b_gx"], g["w_ga"], g["b_ga"], g["lam"] = _lru_bwd(zp, sv["hs"], dy[0], w, dz)
    dz, do, Dr = _mla_post_bwd(zp, sv["o_mla"], dy[1], dz)
    dq, dk, dv = _mla_attn_bwd(sv["q"], sv["k"], sv["v"], do, sv["lse"], Dr)
    dz, g["w_uq"], g["w_uk"], g["w_uv"], g["g_cq"], g["g_ckv"], g["g_mq"], g["g_mk"] = _mla_pre_bwd(zp, dq, dk, dv, w, mla_tab, dz)
    dz, dod, Dd = _dil_comb_bwd(zp, sv["oc"], dy[2], dz)
    dqs, dks, dvs = zip(*[_dil_attn_bwd(sv["qd"], sv["kd"], zp, dod, sv["L"], Dd, gi) for gi in range(len(DIL_DILATIONS))])
    dz, g["g_dq"] = _dil_pre_bwd(zp, dqs, w["g_dq"], dil_tab, dz, C_DQ, "dil_pre_bwd_q")
    dz, g["g_dk"] = _dil_pre_bwd(zp, dks, w["g_dk"], dil_tab, dz, C_DK, "dil_pre_bwd_k")
    dz = _dil_dv_into(dvs, dz)
    g["w_in"] = _mm(sv["h"], dz, mode="tn", name="dw_in", out_dtype=BF16)
    token = hook(g) if hook is not None else None
    dh = _mm(dz, w["w_in"], mode="nt", name="d_h", after=token)
    dx, g["norm_g"] = _rms_in_bwd(sv["x"], w["norm_g"], dh, dout)
    return dx, g


def _peers():
    mx, my, mc = lax.axis_index("x"), lax.axis_index("y"), lax.axis_index("c")
    me = 4 * mx + 2 * my + mc
    out = []
    for k in range(1, N_DEV):
        px = 1 - mx if k & 4 else mx
        py = 1 - my if k & 2 else my
        pc = 1 - mc if k & 1 else mc
        out.append(((px, py, pc), 4 * px + 2 * py + pc))
    return me, out


def _whole(ref, p):
    del p
    return ref


def _exchange(srcs, slicers, slices, name):
    n = len(srcs)

    def body(*refs):
        ins, outs = refs[:n], refs[n:2 * n]
        send_sems, recv_sems, local_sems = refs[2 * n:]
        me, peers = _peers()
        mine = [pltpu.make_async_copy(slicers[a](ins[a], me), outs[a].at[me], local_sems.at[a]) for a in range(n)]
        for cp in mine:
            cp.start()
        copies = []
        for k, (peer, pidx) in enumerate(peers):
            for a in range(n):
                cp = pltpu.make_async_remote_copy(
                    src_ref=slicers[a](ins[a], pidx), dst_ref=outs[a].at[me], send_sem=send_sems.at[k * n + a],
                    recv_sem=recv_sems.at[k * n + a], device_id=peer, device_id_type=pl.DeviceIdType.MESH)
                cp.start()
                copies.append(cp)
        for cp in copies + mine:
            cp.wait()

    nsem = (N_DEV - 1) * n
    return pl.pallas_call(
        body, name=name, out_shape=[jax.ShapeDtypeStruct((N_DEV,) + shp, dt) for shp, dt in slices],
        in_specs=[pl.BlockSpec(memory_space=pl.ANY)] * n, out_specs=[pl.BlockSpec(memory_space=pl.ANY)] * n,
        scratch_shapes=[pltpu.SemaphoreType.DMA((nsem,)), pltpu.SemaphoreType.DMA((nsem,)), pltpu.SemaphoreType.DMA((n,))],
        compiler_params=pltpu.CompilerParams(has_side_effects=True),
    )(*srcs)


def _gather_two_level(srcs, name):
    n = len(srcs)

    def body(*refs):
        ins, outs = refs[:n], refs[n:2 * n]
        send_sems, recv_sems, local_sems = refs[2 * n:]
        mx, my, mc = lax.axis_index("x"), lax.axis_index("y"), lax.axis_index("c")
        me, sibling = (mx, my, mc), (mx, my, 1 - mc)
        chips = [(1 - mx, my), (mx, 1 - my), (1 - mx, 1 - my)]
        slot = lambda d: 4 * d[0] + 2 * d[1] + d[2]

        def copy(j, a, block, to, own=False):
            return pltpu.make_async_remote_copy(
                src_ref=ins[a] if own else outs[a].at[slot(block)], dst_ref=outs[a].at[slot(block)],
                send_sem=send_sems.at[j * n + a], recv_sem=recv_sems.at[j * n + a], device_id=to, device_id_type=pl.DeviceIdType.MESH)

        mine = [pltpu.make_async_copy(ins[a], outs[a].at[slot(me)], local_sems.at[a]) for a in range(n)]
        first = [copy(1 + j, a, me, (*chip, mc), own=True) for j, chip in enumerate(chips) for a in range(n)]
        first += [copy(0, a, me, sibling, own=True) for a in range(n)]
        for cp in mine + first:
            cp.start()
        passed = []
        for j, chip in enumerate(chips):
            for a in range(n):
                copy(1 + j, a, (*chip, mc), me).wait_recv()
                cp = copy(4 + j, a, (*chip, mc), sibling)
                cp.start()
                passed.append(cp)
        for a in range(n):
            copy(0, a, sibling, me).wait_recv()
        for j, chip in enumerate(chips):
            for a in range(n):
                copy(4 + j, a, (*chip, 1 - mc), me).wait_recv()
        for cp in first + passed:
            cp.wait_send()
        for cp in mine:
            cp.wait()

    nsem = (N_DEV - 1) * n
    return pl.pallas_call(
        body, name=name, out_shape=[jax.ShapeDtypeStruct((N_DEV,) + a.shape, a.dtype) for a in srcs],
        in_specs=[pl.BlockSpec(memory_space=pl.ANY)] * n, out_specs=[pl.BlockSpec(memory_space=pl.ANY)] * n,
        scratch_shapes=[pltpu.SemaphoreType.DMA((nsem,)), pltpu.SemaphoreType.DMA((nsem,)), pltpu.SemaphoreType.DMA((n,))],
        compiler_params=pltpu.CompilerParams(has_side_effects=True),
    )(*srcs)


_HBM = pl.BlockSpec(memory_space=pltpu.HBM)
_SEM = pl.BlockSpec(memory_space=pltpu.SEMAPHORE)
_DATAFLOW = pltpu.SideEffectType.DATAFLOW_SIDE_EFFECTING


def _plan_chips():
    mx, my, mc = lax.axis_index("x"), lax.axis_index("y"), lax.axis_index("c")
    return 2 * mx + my, [((cx, cy, mc), 2 * cx + cy) for cx, cy in ((1 - mx, my), (mx, 1 - my), (1 - mx, 1 - my))]


def _pair_exchange(srcs, slicers, slices, sliced, name):
    n = len(srcs)
    pieces = [4 if s else 1 for s in sliced]

    def body(*refs):
        ins, outs = refs[:n], refs[n:2 * n]
        send_sems, recv_sems = refs[2 * n:]
        mx, my, mc = lax.axis_index("x"), lax.axis_index("y"), lax.axis_index("c")
        copies = []
        for a in range(n):
            for q in range(pieces[a]):
                i = len(copies)
                copies.append(pltpu.make_async_remote_copy(
                    src_ref=slicers[a](ins[a], 2 * q + 1 - mc) if sliced[a] else ins[a], dst_ref=outs[a].at[q],
                    send_sem=send_sems.at[i], recv_sem=recv_sems.at[i], device_id=(mx, my, 1 - mc), device_id_type=pl.DeviceIdType.MESH))
        for cp in copies:
            cp.start()
        for cp in copies:
            cp.wait()

    return pl.pallas_call(
        body, name=name, out_shape=[jax.ShapeDtypeStruct((p,) + shp, dt) for (shp, dt), p in zip(slices, pieces)],
        in_specs=[pl.BlockSpec(memory_space=pl.ANY)] * n, out_specs=[pl.BlockSpec(memory_space=pl.ANY)] * n,
        scratch_shapes=[pltpu.SemaphoreType.DMA((sum(pieces),)), pltpu.SemaphoreType.DMA((sum(pieces),))],
        compiler_params=pltpu.CompilerParams(has_side_effects=True),
    )(*srcs)


def _pair_add(src, came, first_blk, axis, name):
    _, r, c = came.shape
    nblk = (c if axis == 1 else r) // LANE
    if axis == 1:
        s_spec = pl.BlockSpec((r, LANE), lambda q, j, fb: (0, fb[q] + j))
        o_spec = pl.BlockSpec((1, r, LANE), lambda q, j, fb: (q, 0, j))
    else:
        s_spec = pl.BlockSpec((LANE, c), lambda q, j, fb: (fb[q] + j, 0))
        o_spec = pl.BlockSpec((1, LANE, c), lambda q, j, fb: (q, j, 0))

    def body(fb_ref, x_ref, y_ref, o_ref):
        del fb_ref
        o_ref[0] = (x_ref[...].astype(F32) + y_ref[0].astype(F32)).astype(o_ref.dtype)

    return pl.pallas_call(
        body, name=name, out_shape=jax.ShapeDtypeStruct(came.shape, came.dtype),
        grid_spec=pltpu.PrefetchScalarGridSpec(num_scalar_prefetch=1, grid=(4, nblk), in_specs=[s_spec, o_spec], out_specs=o_spec),
        compiler_params=_cp(),
    )(first_blk, src, came)


def _add2(x, y, name):
    shp = x.shape
    x, y = x.reshape(-1, shp[-1]), y.reshape(-1, shp[-1])
    R, C = x.shape
    tr = R
    while tr * C * 4 > (1 << 21) and tr % 32 == 0:
        tr //= 2

    def body(x_ref, y_ref, o_ref):
        o_ref[...] = (x_ref[...].astype(F32) + y_ref[...].astype(F32)).astype(o_ref.dtype)

    spec = pl.BlockSpec((tr, C), lambda i: (i, 0))
    return pl.pallas_call(body, name=name, grid=(R // tr,), in_specs=[spec, spec], out_specs=spec,
                          out_shape=jax.ShapeDtypeStruct((R, C), x.dtype), compiler_params=_cp())(x, y).reshape(shp)


def _exchange_start(srcs, slicers, slices, after, name, plan=_peers, nslots=N_DEV):
    n = len(srcs)
    nsem = (nslots - 1) * n
    lands = [lax.empty((nslots,) + shp, dt) for shp, dt in slices]

    def body(*refs):
        ins, lands_in = refs[:n], refs[n:2 * n]
        send_sems, recv_sems, local_sems = refs[2 * n + 1], refs[2 * n + 2], refs[2 * n + 3]
        token = refs[-1]
        me, peers = plan()
        for a in range(n):
            pltpu.make_async_copy(slicers[a](ins[a], me), lands_in[a].at[me], local_sems.at[a]).start()
        for k, (peer, pidx) in enumerate(peers):
            for a in range(n):
                pltpu.make_async_remote_copy(
                    src_ref=slicers[a](ins[a], pidx), dst_ref=lands_in[a].at[me], send_sem=send_sems.at[k * n + a],
                    recv_sem=recv_sems.at[k * n + a], device_id=peer, device_id_type=pl.DeviceIdType.MESH).start()
        token[...] = jnp.zeros_like(token)

    hbm = lambda a: pltpu.with_memory_space_constraint(a, pltpu.HBM)
    return pl.pallas_call(
        body, name=name,
        out_shape=(pltpu.SemaphoreType.DMA((nsem,)), pltpu.SemaphoreType.DMA((nsem,)), pltpu.SemaphoreType.DMA((n,)),
                   *[pltpu.HBM(a.shape, a.dtype) for a in srcs], *[pltpu.HBM(a.shape, a.dtype) for a in lands],
                   jax.ShapeDtypeStruct((SUB, LANE), F32)),
        in_specs=[_HBM] * (2 * n) + [pl.BlockSpec(memory_space=pl.ANY)],
        out_specs=(_SEM, _SEM, _SEM, *[_HBM] * (2 * n), pl.BlockSpec(memory_space=pltpu.VMEM)),
        input_output_aliases={i: 3 + i for i in range(2 * n)},
        compiler_params=pltpu.CompilerParams(has_side_effects=_DATAFLOW),
    )(*[hbm(a) for a in srcs], *[hbm(a) for a in lands], after)


def _exchange_wait(started, slicers, after, name, plan=_peers):
    n = (len(started) - 4) // 2
    sems, thru = started[0:3], started[3:3 + 2 * n]

    def body(*refs):
        srcs, lands = refs[:n], refs[n:2 * n]
        send_sems, recv_sems, local_sems = refs[2 * n], refs[2 * n + 1], refs[2 * n + 2]
        me, peers = plan()
        for k, (peer, pidx) in enumerate(peers):
            for a in range(n):
                cp = pltpu.make_async_remote_copy(
                    src_ref=slicers[a](srcs[a], pidx), dst_ref=lands[a].at[me], send_sem=send_sems.at[k * n + a],
                    recv_sem=recv_sems.at[k * n + a], device_id=peer, device_id_type=pl.DeviceIdType.MESH)
                cp.wait_send()
                cp.wait_recv()
        for a in range(n):
            pltpu.make_async_copy(slicers[a](srcs[a], me), lands[a].at[me], local_sems.at[a]).wait()

    outs = pl.pallas_call(
        body, name=name, out_shape=[pltpu.HBM(a.shape, a.dtype) for a in thru],
        in_specs=[_HBM] * (2 * n) + [_SEM, _SEM, _SEM, pl.BlockSpec(memory_space=pl.ANY)], out_specs=[_HBM] * (2 * n),
        input_output_aliases={i: i for i in range(2 * n)}, compiler_params=pltpu.CompilerParams(has_side_effects=_DATAFLOW),
    )(*thru, *sems, after)
    return outs[n:]


WIN = 13 * LANE


def _win_base(s):
    n = s * SHARD_IN
    a0 = n + jnp.where(n >= _KR0, KR_LANE, 0) + jnp.where(n >= _KR0 + 32, 32, 0)
    return jnp.minimum(a0 // LANE, (ZW - WIN) // LANE)


def _win_offsets(s):
    n = s * SHARD_IN + jnp.arange(SHARD_IN)
    o = s * SHARD_IN - _win_base(s) * LANE
    return n, (o, o + KR_LANE, o + LANE - 32)


def _to_window(shard, s):
    _, offs = _win_offsets(s)
    padded = jnp.pad(shard, ((0, 0), (0, 0), (WIN, WIN)))
    a, b, c = [lax.dynamic_slice(padded, (0, 0, WIN - o), shard.shape[:2] + (WIN,)) for o in offs]
    col = (_win_base(s) * LANE + jnp.arange(WIN))[None, None, :]
    zero = jnp.zeros_like(a)
    return jnp.where(col < _KR0, a, jnp.where((col >= _KR0 + KR_LANE) & (col < _KR0 + KR_LANE + 32), b, jnp.where(col >= _KR0 + LANE, c, zero)))


def _from_window(win, s):
    n, offs = _win_offsets(s)
    a, b, c = [lax.dynamic_slice(win, (0, 0, o), win.shape[:2] + (SHARD_IN,)) for o in offs]
    return jnp.where((n < _KR0)[None, None, :], a, jnp.where((n < _KR0 + 32)[None, None, :], b, c))


def _win_base_static(s):
    n = s * SHARD_IN
    a0 = n + (KR_LANE if n >= _KR0 else 0) + (32 if n >= _KR0 + 32 else 0)
    return min(a0 // LANE, (ZW - WIN) // LANE)


def _assemble_w_in(gw):
    tr = 128
    bases = [_win_base_static(s) for s in range(N_DEV)]

    def body(g_ref, o_ref):
        for j in range(ZW // LANE):
            acc = None
            for s in range(N_DEV):
                if bases[s] <= j < bases[s] + WIN // LANE:
                    piece = g_ref[s, :, (j - bases[s]) * LANE:(j - bases[s] + 1) * LANE]
                    acc = piece if acc is None else acc + piece
            o_ref[:, j * LANE:(j + 1) * LANE] = acc

    return pl.pallas_call(
        body, name="assemble_w_in", grid=(D // tr,), in_specs=[pl.BlockSpec((N_DEV, tr, WIN), lambda i: (0, i, 0))],
        out_specs=pl.BlockSpec((tr, ZW), lambda i: (i, 0)), out_shape=jax.ShapeDtypeStruct((D, ZW), gw.dtype), compiler_params=_cp(),
    )(gw)


def _cols(width):
    return lambda ref, p: ref.at[:, pl.ds(pl.multiple_of(p * width, width), width)]


def _rows(height):
    return lambda ref, p: ref.at[pl.ds(pl.multiple_of(p * height, height), height), :]


SCATTER = {
    'w_in': (lambda ref, p: ref.at[:, pl.ds(pl.multiple_of(_win_base(p) * LANE, LANE), WIN)], (D, WIN), BF16),
    'conv_w': (_cols(LANE), (4, LANE), F32),
    'w_lru_o': (_rows(LANE), (LANE, D), BF16),
    'w_uq': (_cols(LANE), (256, LANE), F32),
    'w_ukv': (_cols(LANE), (128, LANE), F32),
    'w_mla_o': (_cols(LANE), (512, LANE), BF16),
    'w_dil_o': (_cols(LANE), (512, LANE), BF16),
    'w_out': (_rows(LANE), (LANE, D), BF16),
}
SLICED_AXIS = {'w_in': 1, 'conv_w': 1, 'w_lru_o': 0, 'w_uq': 1, 'w_ukv': 1, 'w_mla_o': 1, 'w_dil_o': 1, 'w_out': 0}


PACK_ROWS = 64


def _packed_rows(shapes):
    n = sum(int(np.prod(s)) for s in shapes)
    return -(-n // (PACK_ROWS * LANE)) * PACK_ROWS


def _sum8(buf, name):
    ns, R, C = buf.shape
    tr = R
    while tr * C * 4 * ns > (1 << 22) and tr % 32 == 0:
        tr //= 2

    def body(b_ref, o_ref):
        acc = b_ref[0].astype(F32)
        for s in range(1, ns):
            acc = acc + b_ref[s].astype(F32)
        o_ref[...] = acc

    return pl.pallas_call(
        body, name=name, grid=(R // tr,), in_specs=[pl.BlockSpec((ns, tr, C), lambda i: (0, i, 0))],
        out_specs=pl.BlockSpec((tr, C), lambda i: (i, 0)), out_shape=jax.ShapeDtypeStruct((R, C), F32), compiler_params=_cp(),
    )(buf)


def _pack(arrs, dtype, lead):
    flat = [a.astype(dtype).reshape(a.shape[:lead] + (-1,)) for a in arrs]
    cat = jnp.concatenate(flat, axis=-1)
    n = cat.shape[-1]
    unit = PACK_ROWS * LANE
    pad = (-n) % unit
    if pad:
        cat = jnp.pad(cat, [(0, 0)] * lead + [(0, pad)])
    return cat.reshape(cat.shape[:lead] + ((n + pad) // LANE, LANE))


def _unpack(buf, shapes, lead):
    flat = buf.reshape(buf.shape[:lead] + (-1,))
    out, off = [], 0
    for shp in shapes:
        n = int(np.prod(shp))
        out.append(flat[..., off:off + n].reshape(buf.shape[:lead] + tuple(shp)))
        off += n
    return out


def _adamw(w, g, m, v, name):
    rows, cols = w.shape
    tr = rows
    while tr * cols * 4 > (3 << 19) and tr % 16 == 0:
        tr //= 2
    c1 = 1.0 - ADAM_B1 ** ADAM_STEP
    c2 = 1.0 - ADAM_B2 ** ADAM_STEP

    def body(w_ref, g_ref, m_ref, v_ref, d_ref, mo_ref, vo_ref):
        gv = g_ref[...]
        mn = ADAM_B1 * m_ref[...] + (1.0 - ADAM_B1) * gv
        vn = ADAM_B2 * v_ref[...] + (1.0 - ADAM_B2) * (gv * gv)
        mo_ref[...] = mn
        vo_ref[...] = vn
        d_ref[...] = -ADAM_LR * ((mn / c1) / (jnp.sqrt(vn / c2) + ADAM_EPS) + ADAM_WD * w_ref[...])

    spec = pl.BlockSpec((tr, cols), lambda i: (i, 0))
    return pl.pallas_call(
        body, name=name, grid=(rows // tr,), in_specs=[spec] * 4, out_specs=[spec] * 3,
        out_shape=[jax.ShapeDtypeStruct((rows, cols), F32)] * 3, compiler_params=_cp(),
    )(w, g, m, v)


IN_NAMES = ['x', 'positions', 'norm_g', 'w_in', 'conv_w', 'conv_b', 'w_gate_x', 'b_gate_x', 'w_gate_a', 'b_gate_a', 'lru_lambda', 'w_lru_o',
            'cq_norm_g', 'ckv_norm_g', 'w_uq', 'w_ukv', 'mla_q_norm_g', 'mla_k_norm_g', 'w_mla_o', 'dil_q_norm_g', 'dil_k_norm_g', 'w_dil_o',
            'b_merge', 'w_out']
WEIGHTS = IN_NAMES[2:]
REPLICATED = [n for n in WEIGHTS if n not in SCATTER]
GATE_WEIGHTS = ('w_gate_x', 'w_gate_a')

_KR0 = C_KR * LANE


GATHERED = ['w_in', 'w_lru_o', 'w_uq', 'w_ukv', 'w_mla_o', 'w_dil_o', 'w_out', 'conv_w']


def _local_weights(wd, me):
    loc = {n: wd[n].astype(BF16) for n in GATHERED[:-1]}
    loc['w_in'] = _to_window(loc['w_in'], me)
    loc['w_uq'] = jnp.pad(loc['w_uq'], ((0, 0), (0, 0), (0, LANE - MLA_QK)))
    loc['conv_w'] = wd['conv_w']
    return [[loc[n][l] for n in GATHERED] for l in range(DEPTH)]


def _layer_weights(gathered, rep, l):
    gw = dict(zip(GATHERED, gathered))
    by_rows = lambda a: a.reshape(-1, a.shape[-1])
    by_cols = lambda a: jnp.swapaxes(a, 0, 1).reshape(a.shape[1], -1)
    ukv = jnp.swapaxes(gw['w_ukv'], 0, 1)
    g96 = lambda a: jnp.pad(a[l].reshape(1, MLA_QK), ((0, 0), (0, LANE - MLA_QK)))
    g64 = lambda a: jnp.tile(a[l].reshape(1, DIL_HD), (1, 2))
    return dict(
        norm_g=rep['norm_g'][l].reshape(1, D), w_in=_assemble_w_in(gw['w_in']),
        conv_w=by_cols(gw['conv_w']), conv_b=rep['conv_b'][l].reshape(1, D),
        w_gx=rep['w_gate_x'][l].astype(BF16), b_gx=rep['b_gate_x'][l].reshape(8, 1, LANE),
        w_ga=rep['w_gate_a'][l].astype(BF16), b_ga=rep['b_gate_a'][l].reshape(8, 1, LANE),
        lam=rep['lru_lambda'][l].reshape(1, D),
        w_lru_o=by_rows(gw['w_lru_o']), w_mla_o=by_cols(gw['w_mla_o']), w_dil_o=by_cols(gw['w_dil_o']), w_out=by_rows(gw['w_out']),
        g_cq=rep['cq_norm_g'][l].reshape(1, 256), g_ckv=rep['ckv_norm_g'][l].reshape(1, 128),
        w_uq=by_cols(gw['w_uq']), w_uk=jnp.pad(ukv[:, :, :64], ((0, 0), (0, 0), (0, 64))).reshape(128, 1024),
        w_uv=ukv[:, :, 64:].reshape(128, 512),
        g_mq=g96(rep['mla_q_norm_g']), g_mk=g96(rep['mla_k_norm_g']), g_dq=g64(rep['dil_q_norm_g']), g_dk=g64(rep['dil_k_norm_g']),
        b_merge=rep['b_merge'][l].reshape(1, 3 * D),
    )


def _sharded_grads(g):
    uk = g['w_uk'].reshape(128, 8, 128)[:, :, :64]
    uv = g['w_uv'].reshape(128, 8, 64)
    d = {'w_in': g['w_in'], 'conv_w': g['conv_w'], 'w_lru_o': g['w_lru_o'], 'w_uq': g['w_uq'],
         'w_ukv': jnp.concatenate([uk, uv], axis=-1).reshape(128, 1024), 'w_mla_o': g['w_mla_o'], 'w_dil_o': g['w_dil_o'],
         'w_out': g['w_out']}
    return [d[n] for n in SCATTER]


def _replicated_grads(g):
    return {
        'conv_b': g['conv_b'].reshape(D),
        'w_gate_x': g['w_gx'], 'b_gate_x': g['b_gx'].reshape(8, LANE), 'w_gate_a': g['w_ga'], 'b_gate_a': g['b_ga'].reshape(8, LANE),
        'lru_lambda': g['lam'].reshape(D), 'cq_norm_g': g['g_cq'].reshape(256), 'ckv_norm_g': g['g_ckv'].reshape(128),
        'mla_q_norm_g': g['g_mq'][0, :MLA_QK], 'mla_k_norm_g': g['g_mk'][0, :MLA_QK],
        'dil_q_norm_g': g['g_dq'][0, :DIL_HD] + g['g_dq'][0, DIL_HD:], 'dil_k_norm_g': g['g_dk'][0, :DIL_HD] + g['g_dk'][0, DIL_HD:],
        'b_merge': g['b_merge'].reshape(3 * D),
    }


def kernel(x, positions, norm_g, w_in, conv_w, conv_b, w_gate_x, b_gate_x, w_gate_a, b_gate_a, lru_lambda, w_lru_o, cq_norm_g, ckv_norm_g, w_uq, w_ukv, mla_q_norm_g, mla_k_norm_g, w_mla_o, dil_q_norm_g, dil_k_norm_g, w_dil_o, b_merge, w_out, loss_target, m_norm_g, m_w_in, m_conv_w, m_conv_b, m_w_gate_x, m_b_gate_x, m_w_gate_a, m_b_gate_a, m_lru_lambda, m_w_lru_o, m_cq_norm_g, m_ckv_norm_g, m_w_uq, m_w_ukv, m_mla_q_norm_g, m_mla_k_norm_g, m_w_mla_o, m_dil_q_norm_g, m_dil_k_norm_g, m_w_dil_o, m_b_merge, m_w_out, v_norm_g, v_w_in, v_conv_w, v_conv_b, v_w_gate_x, v_b_gate_x, v_w_gate_a, v_b_gate_a, v_lru_lambda, v_w_lru_o, v_cq_norm_g, v_ckv_norm_g, v_w_uq, v_w_ukv, v_mla_q_norm_g, v_mla_k_norm_g, v_w_mla_o, v_dil_q_norm_g, v_dil_k_norm_g, v_w_dil_o, v_b_merge, v_w_out):
    args = (x, positions, norm_g, w_in, conv_w, conv_b, w_gate_x, b_gate_x, w_gate_a, b_gate_a, lru_lambda, w_lru_o, cq_norm_g, ckv_norm_g, w_uq, w_ukv, mla_q_norm_g, mla_k_norm_g, w_mla_o, dil_q_norm_g, dil_k_norm_g, w_dil_o, b_merge, w_out)
    moments_m = (m_norm_g, m_w_in, m_conv_w, m_conv_b, m_w_gate_x, m_b_gate_x, m_w_gate_a, m_b_gate_a, m_lru_lambda, m_w_lru_o, m_cq_norm_g, m_ckv_norm_g, m_w_uq, m_w_ukv, m_mla_q_norm_g, m_mla_k_norm_g, m_w_mla_o, m_dil_q_norm_g, m_dil_k_norm_g, m_w_dil_o, m_b_merge, m_w_out)
    moments_v = (v_norm_g, v_w_in, v_conv_w, v_conv_b, v_w_gate_x, v_b_gate_x, v_w_gate_a, v_b_gate_a, v_lru_lambda, v_w_lru_o, v_cq_norm_g, v_ckv_norm_g, v_w_uq, v_w_ukv, v_mla_q_norm_g, v_mla_k_norm_g, v_w_mla_o, v_dil_q_norm_g, v_dil_k_norm_g, v_w_dil_o, v_b_merge, v_w_out)
    a = dict(zip(IN_NAMES, args))
    wd = {n: a[n] for n in WEIGHTS}
    md = dict(zip(WEIGHTS, moments_m))
    vd = dict(zip(WEIGHTS, moments_v))

    me = 4 * lax.axis_index("x") + 2 * lax.axis_index("y") + lax.axis_index("c")

    assert DEPTH == 2
    xs, tabs = x[0], _rope_tables(positions[0])
    whole = [_whole] * len(GATHERED)
    slicers = [SCATTER[n][0] for n in SCATTER]
    grad_slices = [SCATTER[n][1:3] for n in SCATTER]

    local = _local_weights(wd, me)
    w_slices = [(a.shape, a.dtype) for a in local[0]]
    landed0 = _gather_two_level(local[0], "gather_w0")
    flying = _exchange_start(local[1], whole, w_slices, landed0[0], "gather_w1_start")
    rep0 = dict(wd, norm_g=wd['norm_g'] + flying[-1][0, 0])
    w0 = _layer_weights(landed0, rep0, 0)
    x1, saved0 = _layer_fwd(xs, w0, tabs)
    w1 = _layer_weights(_exchange_wait(flying, whole, x1, "gather_w1_wait"), wd, 1)
    x2, saved1 = _layer_fwd(x1, w1, tabs)
    loss, dx2 = _loss_fwd_bwd(x2, loss_target[0])
    loss = loss[0, 0]

    sharded = list(SCATTER)
    nsh = len(sharded)
    small = [n for n in REPLICATED if n not in GATE_WEIGHTS and n != 'norm_g']

    def outgoing(g):
        r = _replicated_grads(g)
        return (_sharded_grads(g) + [_pack([r[n] for n in small], F32, 0)]
                + [r[n].astype(BF16).reshape(8 * LANE, LANE) for n in GATE_WEIGHTS])

    out_slicers = slicers + [_whole] * 3
    out_slices = grad_slices + [((_packed_rows([wd[n].shape[1:] for n in small]), LANE), F32)] + [((8 * LANE, LANE), BF16)] * 2
    dx1, g1 = _layer_bwd(dx2, w1, tabs, saved1)
    flying1 = _exchange_start(outgoing(g1), out_slicers, out_slices, dx1, "scatter_g1_start")
    later = {}

    names = sharded + ['small'] + list(GATE_WEIGHTS)
    sliced = [True] * nsh + [False] * 3
    by_chip = [(lambda ref, q: ref.at[q])] * nsh + [_whole] * 3

    def send_layer0(g):
        later['got1'] = _exchange_wait(flying1, out_slicers, g['w_in'], "scatter_g1_wait")
        mine = outgoing(g)
        came = _pair_exchange(mine, out_slicers, out_slices, sliced, "pair_g0")
        my_side = 2 * jnp.arange(4, dtype=jnp.int32) + lax.axis_index("c")
        halves = []
        for n, a, c in zip(names, mine, came):
            if n in SCATTER:
                first = _win_base(my_side) if n == 'w_in' else my_side
                halves.append(_pair_add(a, c, first.astype(jnp.int32), SLICED_AXIS[n], f"pair_sum_{n}"))
            else:
                halves.append(_add2(a, c[0], f"pair_sum_{n}"))
        later['flying0'] = _exchange_start(halves, by_chip, out_slices, later['got1'][0], "scatter_g0_start", plan=_plan_chips, nslots=4)
        return later['flying0'][-1]

    grad_x, g0 = _layer_bwd(dx1, w0, tabs, saved0, hook=send_layer0, after=flying1[-1])
    sum1 = [_sum8(b, f"sum_{n}_1") for n, b in zip(names, later['got1'])]
    got0 = _exchange_wait(later['flying0'], by_chip, grad_x, "scatter_g0_wait", plan=_plan_chips)
    sum0 = [_sum8(b, f"sum_{n}_0") for n, b in zip(names, got0)]
    norm_part = _pack([jnp.stack([g['norm_g'].reshape(D) for g in (g0, g1)])], F32, 0)
    norm_sum = _sum8(_exchange([norm_part], [_whole], [(norm_part.shape, F32)], "gather_norm_g")[0], "sum_norm_g")

    gsh = {n: jnp.stack([sum0[i], sum1[i]]) for i, n in enumerate(sharded)}
    gsh['w_in'] = _from_window(gsh['w_in'], me)
    gsh['w_uq'] = gsh['w_uq'][:, :, :MLA_QK]
    grep = {'norm_g': _unpack(norm_sum, [wd['norm_g'].shape], 0)[0]}
    per_layer = [_unpack(s[nsh], [wd[n].shape[1:] for n in small], 0) for s in (sum0, sum1)]
    grep.update({n: jnp.stack([per_layer[l][i] for l in range(DEPTH)]) for i, n in enumerate(small)})
    for i, n in enumerate(GATE_WEIGHTS):
        grep[n] = jnp.stack([sum0[nsh + 1 + i], sum1[nsh + 1 + i]]).reshape(wd[n].shape)

    out_g, out_d, out_m, out_v = {}, {}, {}, {}
    vecs = ['norm_g'] + small
    vshapes = [wd[n].shape for n in vecs]
    packed_g = _pack([grep[n] for n in vecs], F32, 0)
    d_, m_, v_ = _adamw(_pack([wd[n] for n in vecs], F32, 0), packed_g, _pack([md[n] for n in vecs], F32, 0),
                        _pack([vd[n] for n in vecs], F32, 0), "adamw_vectors")
    for dst, buf in ((out_d, d_), (out_m, m_), (out_v, v_)):
        dst.update(zip(vecs, _unpack(buf, vshapes, 0)))
    out_g.update({n: grep[n] for n in vecs})
    gsh.update({n: grep[n] for n in GATE_WEIGHTS})
    for n in sharded + list(GATE_WEIGHTS):
        if n in GATE_WEIGHTS:
            shp = wd[n].shape
            two = (shp[0] * shp[1] * shp[2], shp[3])
            d_, m_, v_ = _adamw(wd[n].reshape(two), gsh[n].reshape(two), md[n].reshape(two), vd[n].reshape(two), "adamw_" + n)
            out_g[n], out_d[n], out_m[n], out_v[n] = gsh[n], d_.reshape(shp), m_.reshape(shp), v_.reshape(shp)
            continue
        shp = wd[n].shape
        two = (shp[0] * shp[1], shp[2])
        d_, m_, v_ = _adamw(wd[n].reshape(two), gsh[n].reshape(two), md[n].reshape(two), vd[n].reshape(two), "adamw_" + n)
        out_g[n], out_d[n], out_m[n], out_v[n] = gsh[n], d_.reshape(shp), m_.reshape(shp), v_.reshape(shp)

    loss = lax.psum(loss, ("x", "y", "c"))
    return (loss, grad_x[None], *[out_g[n] for n in WEIGHTS], *[out_d[n] for n in WEIGHTS], *[out_m[n] for n in WEIGHTS],
            *[out_v[n] for n in WEIGHTS])
```

```python
import functools

import numpy as np
import jax
import jax.numpy as jnp
from jax import lax
from jax.experimental import pallas as pl
from jax.experimental.pallas import tpu as pltpu

F32 = jnp.float32
BF16 = jnp.bfloat16

N_DEV = 8
D = 1024
DEPTH = 2
EPS = 1e-6
ROPE_THETA = 10000.0
LRU_C = 8.0
LANE = 128
SUB = 8
IN_WIDTH = 11168
SHARD_IN = IN_WIDTH // N_DEV

C_LRUX, C_LRUG, C_CQ, C_CKV, C_KR, C_MLAG, C_DQ, C_DK, C_DV, C_DILG, C_MERGE = 0, 8, 16, 18, 19, 20, 24, 36, 48, 60, 64
ZW = 88 * LANE
KR_LANE = 64

MLA_QK = 96
MLA_SCALE = MLA_QK ** -0.5
DIL_HD = 64
DIL_SCALE = DIL_HD ** -0.5
DIL_DILATIONS = (1, 4, 16)
NK = 128

ADAM_LR, ADAM_B1, ADAM_B2, ADAM_EPS, ADAM_WD, ADAM_STEP = 0.001, 0.9, 0.999, 1e-08, 0.01, 10

NEG = -1e30
VMEM_LIMIT = 48 * 1024 * 1024


def _cp(**kw):
    return pltpu.CompilerParams(vmem_limit_bytes=VMEM_LIMIT, **kw)


def _sig(x):
    return 1.0 / (1.0 + jnp.exp(-x))


def _silu(x):
    return x * _sig(x)


def _dsilu(x):
    s = _sig(x)
    return s * (1.0 + x * (1.0 - s))


def _dot(a, b, dims):
    return lax.dot_general(a, b, (dims, ((), ())), preferred_element_type=F32)


def _nn(a, b):
    return _dot(a, b, ((1,), (0,)))


def _nt(a, b):
    return _dot(a, b, ((1,), (1,)))


def _tn(a, b):
    return _dot(a, b, ((0,), (0,)))


def _rsum(x):
    return jnp.sum(x, axis=-1, keepdims=True)


def _csum(x):
    return jnp.sum(x, axis=0, keepdims=True)


def _mm(a, b, *, mode, name, out_dtype=F32, add=None, after=None, tm=1024, tn=1024, tk=1024):
    if mode == "nn":
        (M, K), (K2, N) = a.shape, b.shape
    elif mode == "nt":
        (M, K), (N, K2) = a.shape, b.shape
    else:
        (K, M), (K2, N) = a.shape, b.shape
    assert K == K2
    tm, tn, tk = min(tm, M), min(tn, N), min(tk, K)
    assert M % tm == 0 and N % tn == 0 and K % tk == 0
    nk = K // tk
    fn = {"nn": _nn, "nt": _nt, "tn": _tn}[mode]
    has_add = add is not None

    def body(*refs):
        a_ref, b_ref = refs[0], refs[1]
        add_ref = refs[2] if has_add else None
        o_ref = refs[2 + has_add + (after is not None)]
        part = fn(a_ref[...].astype(BF16), b_ref[...].astype(BF16))

        def fin(acc):
            if has_add:
                acc = acc + add_ref[...]
            o_ref[...] = acc.astype(out_dtype)

        if nk == 1:
            fin(part)
        else:
            acc_ref = refs[-1]
            k = pl.program_id(2)

            @pl.when(k == 0)
            def _():
                acc_ref[...] = part

            @pl.when(k > 0)
            def _():
                acc_ref[...] += part

            @pl.when(k == nk - 1)
            def _():
                fin(acc_ref[...])

    a_spec = pl.BlockSpec((tk, tm), lambda i, j, k: (k, i)) if mode == "tn" else pl.BlockSpec((tm, tk), lambda i, j, k: (i, k))
    b_spec = pl.BlockSpec((tn, tk), lambda i, j, k: (j, k)) if mode == "nt" else pl.BlockSpec((tk, tn), lambda i, j, k: (k, j))
    o_spec = pl.BlockSpec((tm, tn), lambda i, j, k: (i, j))
    in_specs, args = [a_spec, b_spec], [a, b]
    if has_add:
        in_specs.append(o_spec)
        args.append(add)
    if after is not None:
        in_specs.append(pl.BlockSpec(memory_space=pl.ANY))
        args.append(after)
    return pl.pallas_call(
        body, name=name, grid=(M // tm, N // tn, nk), in_specs=in_specs, out_specs=o_spec,
        out_shape=jax.ShapeDtypeStruct((M, N), out_dtype),
        scratch_shapes=[pltpu.VMEM((tm, tn), F32)] if nk > 1 else [],
        compiler_params=_cp(dimension_semantics=("parallel", "parallel", "arbitrary")),
    )(*args)


T_ROW = 512


def _rms_in_fwd(x, g):
    S = x.shape[0]
    T = T_ROW

    def body(x_ref, g_ref, h_ref):
        xv = x_ref[...]
        r = lax.rsqrt(jnp.mean(xv * xv, axis=-1, keepdims=True) + EPS)
        h_ref[...] = (xv * r * g_ref[...]).astype(BF16)

    return pl.pallas_call(
        body, name="rms_in_fwd", grid=(S // T,),
        in_specs=[pl.BlockSpec((T, D), lambda i: (i, 0)), pl.BlockSpec((1, D), lambda i: (0, 0))],
        out_specs=pl.BlockSpec((T, D), lambda i: (i, 0)),
        out_shape=jax.ShapeDtypeStruct((S, D), BF16), compiler_params=_cp(),
    )(x, g)


def _rms_in_bwd(x, g, dh, dres):
    S = x.shape[0]
    T = T_ROW

    def body(x_ref, g_ref, dh_ref, dr_ref, dx_ref, dg_ref):
        i = pl.program_id(0)
        xv = x_ref[...]
        r = lax.rsqrt(jnp.mean(xv * xv, axis=-1, keepdims=True) + EPS)
        xn = xv * r
        dy = dh_ref[...]
        part = _csum(dy * xn)

        @pl.when(i == 0)
        def _():
            dg_ref[...] = part

        @pl.when(i > 0)
        def _():
            dg_ref[...] += part

        dxh = dy * g_ref[...]
        dx_ref[...] = dr_ref[...] + r * (dxh - xn * jnp.mean(dxh * xn, axis=-1, keepdims=True))

    row = pl.BlockSpec((T, D), lambda i: (i, 0))
    vec = pl.BlockSpec((1, D), lambda i: (0, 0))
    return pl.pallas_call(
        body, name="rms_in_bwd", grid=(S // T,), in_specs=[row, vec, row, row], out_specs=[row, vec],
        out_shape=[jax.ShapeDtypeStruct((S, D), F32), jax.ShapeDtypeStruct((1, D), F32)], compiler_params=_cp(),
    )(x, g, dh, dres)


T_LRU = 512


def _neg_expm1(y):
    ser = -y * (1.0 + y * 0.5 * (1.0 + y * (1.0 / 3.0) * (1.0 + y * 0.25 * (1.0 + y * 0.2))))
    return jnp.where(y > -0.03, ser, 1.0 - jnp.exp(y))


def _softplus_neg(lam):
    e = jnp.exp(-jnp.abs(lam))
    l1p = jnp.where(e < 0.01, e * (1.0 - e * (0.5 - e * (1.0 / 3.0 - e * 0.25))), jnp.log(1.0 + e))
    return jnp.maximum(-lam, 0.0) + l1p


def _scan_fwd(a, b, T):
    row = lax.broadcasted_iota(jnp.int32, a.shape, 0)
    d = 1
    while d < T:
        m = row >= d
        b = jnp.where(m, a * pltpu.roll(b, d, 0) + b, b)
        a = jnp.where(m, a * pltpu.roll(a, d, 0), a)
        d *= 2
    return a, b


def _scan_bwd(a, b, T):
    row = lax.broadcasted_iota(jnp.int32, a.shape, 0)
    d = 1
    while d < T:
        m = row < T - d
        b = jnp.where(m, a * pltpu.roll(b, T - d, 0) + b, b)
        a = jnp.where(m, a * pltpu.roll(a, T - d, 0), a)
        d *= 2
    return b


def _lru_common(x, prev, first, cw_ref, cb_ref, wgx_ref, bgx_ref, wga_ref, bga_ref, lam_ref, T):
    row = lax.broadcasted_iota(jnp.int32, x.shape, 0)
    prev = jnp.where(first, 0.0, prev)
    xs = []
    for j in (3, 2, 1):
        pv = jnp.tile(pltpu.roll(prev, j, 0), (T // SUB, 1))
        xs.append(jnp.where(row < j, pv, pltpu.roll(x, j, 0)))
    xs.append(x)
    xc = cb_ref[...] + cw_ref[0:1, :] * xs[0] + cw_ref[1:2, :] * xs[1] + cw_ref[2:3, :] * xs[2] + cw_ref[3:4, :] * xs[3]
    xcb = xc.astype(BF16)
    gx = _sig(_nn(xcb, wgx_ref[0]) + bgx_ref[0])
    ga = _sig(_nn(xcb, wga_ref[0]) + bga_ref[0])
    sp = _softplus_neg(lam_ref[...])
    log_a = -LRU_C * ga * sp
    a = jnp.exp(log_a)
    mult = jnp.sqrt(_neg_expm1(2.0 * log_a))
    return xs, xc, xcb, gx, ga, sp, a, mult


def _lru_specs(T, tmap):
    def at(col0):
        return pl.BlockSpec((T, LANE), lambda n, i: (tmap(i), col0 + n))

    def prev(col0):
        return pl.BlockSpec((SUB, LANE), lambda n, i: (jnp.maximum(tmap(i) * (T // SUB) - 1, 0), col0 + n))

    small = [
        pl.BlockSpec((4, LANE), lambda n, i: (0, n)),
        pl.BlockSpec((1, LANE), lambda n, i: (0, n)),
        pl.BlockSpec((1, LANE, LANE), lambda n, i: (n, 0, 0)),
        pl.BlockSpec((1, 1, LANE), lambda n, i: (n, 0, 0)),
        pl.BlockSpec((1, LANE, LANE), lambda n, i: (n, 0, 0)),
        pl.BlockSpec((1, 1, LANE), lambda n, i: (n, 0, 0)),
        pl.BlockSpec((1, LANE), lambda n, i: (0, n)),
    ]
    return at, prev, small


def _lru_fwd(zp, w):
    S = zp.shape[0]
    T = T_LRU
    at, prev, small = _lru_specs(T, lambda i: i)

    def body(x_ref, xp_ref, g_ref, cw_ref, cb_ref, wgx_ref, bgx_ref, wga_ref, bga_ref, lam_ref, hs_ref, y_ref, carry_ref):
        i = pl.program_id(1)

        @pl.when(i == 0)
        def _():
            carry_ref[...] = jnp.zeros_like(carry_ref)

        x = x_ref[...]
        _, xc, _, gx, _, _, a, mult = _lru_common(x, xp_ref[...], i == 0, cw_ref, cb_ref, wgx_ref, bgx_ref, wga_ref, bga_ref, lam_ref, T)
        A, B = _scan_fwd(a, mult * gx * xc, T)
        h = B + A * carry_ref[SUB - 1:SUB, :]
        hs_ref[...] = h
        carry_ref[...] = hs_ref[T - SUB:T, :]
        y_ref[...] = (h * _silu(g_ref[...])).astype(BF16)

    out = pl.BlockSpec((T, LANE), lambda n, i: (i, n))
    return pl.pallas_call(
        body, name="lru_fwd", grid=(8, S // T),
        in_specs=[at(C_LRUX), prev(C_LRUX), at(C_LRUG)] + small, out_specs=[out, out],
        out_shape=[jax.ShapeDtypeStruct((S, D), F32), jax.ShapeDtypeStruct((S, D), BF16)],
        scratch_shapes=[pltpu.VMEM((SUB, LANE), F32)],
        compiler_params=_cp(dimension_semantics=("parallel", "arbitrary")),
    )(zp, zp, zp, w["conv_w"], w["conv_b"], w["w_gx"], w["b_gx"], w["w_ga"], w["b_ga"], w["lam"])


def _lru_bwd(zp, hs, dy, w, dz):
    S = zp.shape[0]
    T = T_LRU
    nT = S // T
    at, prev, small = _lru_specs(T, lambda i: nT - 1 - i)

    def body(x_ref, xp_ref, g_ref, h_ref, hp_ref, dy_ref, cw_ref, cb_ref, wgx_ref, bgx_ref, wga_ref, bga_ref, lam_ref, dz_in,
             dzx_ref, dcw_ref, dcb_ref, dwgx_ref, dbgx_ref, dwga_ref, dbga_ref, dlam_ref, carry_ref, head_ref):
        del dz_in
        j = pl.program_id(1)
        it = nT - 1 - j

        @pl.when(j == 0)
        def _():
            for r in (carry_ref, head_ref, dcw_ref, dcb_ref, dwgx_ref, dbgx_ref, dwga_ref, dbga_ref, dlam_ref):
                r[...] = jnp.zeros_like(r)

        first = it == 0
        x = x_ref[...]
        xs, xc, xcb, gx, ga, sp, a, mult = _lru_common(x, xp_ref[...], first, cw_ref, cb_ref, wgx_ref, bgx_ref, wga_ref, bga_ref, lam_ref, T)
        row = lax.broadcasted_iota(jnp.int32, x.shape, 0)
        u = gx * xc
        h = h_ref[...]
        hp = jnp.where(first, 0.0, hp_ref[...])
        hm1 = jnp.where(row < 1, jnp.tile(pltpu.roll(hp, 1, 0), (T // SUB, 1)), pltpu.roll(h, 1, 0))
        dho = dy_ref[...] * _silu(g_ref[...])
        gin = jnp.where(row == T - 1, dho + carry_ref[0:1, :], dho)
        abar = jnp.where(row == T - 1, 0.0, pltpu.roll(a, T - 1, 0))
        dh = _scan_bwd(abar, gin, T)
        carry_ref[...] = (a * dh)[0:SUB, :]
        da = dh * hm1
        dmult = dh * u
        du = dh * mult
        dgx = du * xc
        dxc = du * gx
        dlog_a = da * a - dmult * a * a / mult
        dga = dlog_a * (-LRU_C * sp)
        lam = lam_ref[...]
        dlam_ref[...] += _csum(dlog_a * (-LRU_C * ga)) * (-1.0 / (1.0 + jnp.exp(lam)))
        dpa = dga * ga * (1.0 - ga)
        dpx = dgx * gx * (1.0 - gx)
        dpab, dpxb = dpa.astype(BF16), dpx.astype(BF16)
        dxc = dxc + _nt(dpxb, wgx_ref[0]) + _nt(dpab, wga_ref[0])
        dwgx_ref[0] += _tn(xcb, dpxb)
        dwga_ref[0] += _tn(xcb, dpab)
        dbgx_ref[0] += _csum(dpx)
        dbga_ref[0] += _csum(dpa)
        dcb_ref[...] += _csum(dxc)
        for k in range(4):
            dcw_ref[k:k + 1, :] += _csum(dxc * xs[k])
        head = head_ref[...]
        dx = cw_ref[3:4, :] * dxc
        for jj in (1, 2, 3):
            hv = jnp.tile(pltpu.roll(head, SUB - jj, 0), (T // SUB, 1))
            dx = dx + cw_ref[3 - jj:4 - jj, :] * jnp.where(row >= T - jj, hv, pltpu.roll(dxc, T - jj, 0))
        head_ref[...] = dxc[0:SUB, :]
        dzx_ref[...] = dx.astype(BF16)

    def acc(shape, imap):
        return pl.BlockSpec(shape, imap)

    out_specs = [
        pl.BlockSpec((T, LANE), lambda n, i: (nT - 1 - i, C_LRUX + n)),
        acc((4, LANE), lambda n, i: (0, n)), acc((1, LANE), lambda n, i: (0, n)),
        acc((1, LANE, LANE), lambda n, i: (n, 0, 0)), acc((1, 1, LANE), lambda n, i: (n, 0, 0)),
        acc((1, LANE, LANE), lambda n, i: (n, 0, 0)), acc((1, 1, LANE), lambda n, i: (n, 0, 0)),
        acc((1, LANE), lambda n, i: (0, n)),
    ]
    out_shape = [
        jax.ShapeDtypeStruct(dz.shape, BF16),
        jax.ShapeDtypeStruct((4, D), F32), jax.ShapeDtypeStruct((1, D), F32),
        jax.ShapeDtypeStruct((8, LANE, LANE), F32), jax.ShapeDtypeStruct((8, 1, LANE), F32),
        jax.ShapeDtypeStruct((8, LANE, LANE), F32), jax.ShapeDtypeStruct((8, 1, LANE), F32),
        jax.ShapeDtypeStruct((1, D), F32),
    ]
    dyspec = pl.BlockSpec((T, LANE), lambda n, i: (nT - 1 - i, n))
    hprev = pl.BlockSpec((SUB, LANE), lambda n, i: (jnp.maximum((nT - 1 - i) * (T // SUB) - 1, 0), n))
    return pl.pallas_call(
        body, name="lru_bwd", grid=(8, nT),
        in_specs=[at(C_LRUX), prev(C_LRUX), at(C_LRUG), dyspec, hprev, dyspec] + small + [pl.BlockSpec(memory_space=pl.ANY)],
        out_specs=out_specs, out_shape=out_shape,
        scratch_shapes=[pltpu.VMEM((SUB, LANE), F32), pltpu.VMEM((SUB, LANE), F32)],
        input_output_aliases={13: 0},
        compiler_params=_cp(dimension_semantics=("parallel", "arbitrary")),
    )(zp, zp, zp, hs, hs, dy, w["conv_w"], w["conv_b"], w["w_gx"], w["b_gx"], w["w_ga"], w["b_ga"], w["lam"], dz)


def _lru_gate_bwd(zp, hs, dy, dz):
    S = zp.shape[0]
    T = T_ROW

    def body(g_ref, h_ref, dy_ref, dz_in, o_ref):
        del dz_in
        o_ref[...] = (dy_ref[...] * h_ref[...] * _dsilu(g_ref[...])).astype(BF16)

    row = pl.BlockSpec((T, D), lambda i: (i, 0))
    zc = pl.BlockSpec((T, D), lambda i: (i, C_LRUG // 8))
    return pl.pallas_call(
        body, name="lru_gate_bwd", grid=(S // T,), in_specs=[zc, row, row, pl.BlockSpec(memory_space=pl.ANY)], out_specs=zc,
        out_shape=jax.ShapeDtypeStruct(dz.shape, BF16), input_output_aliases={3: 0}, compiler_params=_cp(),
    )(zp, hs, dy, dz)


def _rope_tables(pos):
    pf = pos.astype(F32)[:, None]

    def cs(d):
        inv = ROPE_THETA ** (-jnp.arange(0, d, 2, dtype=F32) / d)
        ang = pf * inv
        return jnp.cos(ang), jnp.sin(ang)

    S = pos.shape[0]
    c, s = cs(32)
    one, zero = jnp.ones((S, 64), F32), jnp.zeros((S, 16), F32)
    z32, z64 = jnp.zeros((S, 32), F32), jnp.zeros((S, 64), F32)
    mla = (jnp.concatenate([one, c, c, jnp.ones((S, 32), F32)], 1),
           jnp.concatenate([z64, zero, s, z32], 1),
           jnp.concatenate([z64, -s, zero, z32], 1))
    c, s = cs(64)
    dil = (jnp.concatenate([c, c, c, c], 1),
           jnp.concatenate([z32, s, z32, s], 1),
           jnp.concatenate([-s, z32, -s, z32], 1))
    return mla, dil


def _rope(x, C, S1, S2, sh):
    return x * C + pltpu.roll(x, sh, 1) * S1 + pltpu.roll(x, LANE - sh, 1) * S2


def _rope_t(dy, C, S1, S2, sh):
    return dy * C + pltpu.roll(dy * S1, LANE - sh, 1) + pltpu.roll(dy * S2, sh, 1)


def _lane(shape):
    return lax.broadcasted_iota(jnp.int32, shape, 1)


T_MLA = 256
TA = 512


def _zcol(T, width, col_lanes):
    assert (col_lanes * LANE) % width == 0
    return pl.BlockSpec((T, width), lambda i: (i, col_lanes * LANE // width))


def _full(shape):
    return pl.BlockSpec(shape, lambda *_: (0,) * len(shape))


def _mla_pre_fwd(zp, w, tab):
    S = zp.shape[0]
    T = T_MLA

    def body(cq_ref, ckv_ref, kr_ref, gcq_ref, gckv_ref, wuq_ref, wuk_ref, wuv_ref, gq_ref, gk_ref, C_ref, S1_ref, S2_ref,
             q_ref, k_ref, v_ref):
        cq = cq_ref[...]
        cqn = (cq * lax.rsqrt(jnp.mean(cq * cq, axis=-1, keepdims=True) + EPS) * gcq_ref[...]).astype(BF16)
        ckv = ckv_ref[...]
        ckvn = (ckv * lax.rsqrt(jnp.mean(ckv * ckv, axis=-1, keepdims=True) + EPS) * gckv_ref[...]).astype(BF16)
        q0 = _nn(cqn, wuq_ref[...])
        k0 = _nn(ckvn, wuk_ref[...])
        krb = kr_ref[...]
        C, S1, S2 = C_ref[...], S1_ref[...], S2_ref[...]
        for h in range(8):
            sl = slice(h * LANE, (h + 1) * LANE)
            xq = q0[:, sl]
            xq = xq * lax.rsqrt(_rsum(xq * xq) * (1.0 / MLA_QK) + EPS) * gq_ref[...]
            q_ref[:, sl] = _rope(xq, C, S1, S2, 16).astype(BF16)
            xk = k0[:, sl] + krb
            xk = xk * lax.rsqrt(_rsum(xk * xk) * (1.0 / MLA_QK) + EPS) * gk_ref[...]
            k_ref[:, sl] = _rope(xk, C, S1, S2, 16).astype(BF16)
        v_ref[...] = _nn(ckvn, wuv_ref[...]).astype(BF16)

    tabspec = pl.BlockSpec((T, LANE), lambda i: (i, 0))
    in_specs = [_zcol(T, 256, C_CQ), _zcol(T, LANE, C_CKV), _zcol(T, LANE, C_KR), _full((1, 256)), _full((1, LANE)),
                _full((256, 1024)), _full((LANE, 1024)), _full((LANE, 512)), _full((1, LANE)), _full((1, LANE)),
                tabspec, tabspec, tabspec]
    return pl.pallas_call(
        body, name="mla_pre_fwd", grid=(S // T,), in_specs=in_specs,
        out_specs=[pl.BlockSpec((T, 1024), lambda i: (i, 0)), pl.BlockSpec((T, 1024), lambda i: (i, 0)), pl.BlockSpec((T, 512), lambda i: (i, 0))],
        out_shape=[jax.ShapeDtypeStruct((S, 1024), BF16), jax.ShapeDtypeStruct((S, 1024), BF16), jax.ShapeDtypeStruct((S, 512), BF16)],
        compiler_params=_cp(),
    )(zp, zp, zp, w["g_cq"], w["g_ckv"], w["w_uq"], w["w_uk"], w["w_uv"], w["g_mq"], w["g_mk"], *tab)


def _mla_attn_fwd(q, k, v, zp):
    S = q.shape[0]
    nq = S // TA

    def body(q_ref, k_ref, v_ref, g_ref, o_ref, lse_ref, y_ref):
        qi = pl.program_id(1)
        lane = _lane((TA, LANE))
        rowi = lax.broadcasted_iota(jnp.int32, (TA, TA), 0)
        coli = lax.broadcasted_iota(jnp.int32, (TA, TA), 1)
        o_tot = jnp.zeros((TA, LANE), F32)
        for hh in range(2):
            cs = slice(hh * LANE, (hh + 1) * LANE)
            hm = (lane < 64) if hh == 0 else (lane >= 64)
            qh = q_ref[:, cs]

            def step(kb, carry, masked, cs=cs, hm=hm, qh=qh):
                m, l, acc = carry
                off = pl.multiple_of(kb * TA, TA)
                kh = k_ref[pl.ds(off, TA), cs]
                vv = v_ref[pl.ds(off, TA), :]
                vh = jnp.where(hm, vv, jnp.zeros_like(vv))
                s = _nt(qh, kh) * MLA_SCALE
                if masked:
                    s = jnp.where(rowi >= coli, s, NEG)
                m_new = jnp.maximum(m, jnp.max(s, axis=-1, keepdims=True))
                alpha = jnp.exp(m - m_new)
                p = jnp.exp(s - m_new)
                l = alpha * l + _rsum(p)
                acc = alpha * acc + _nn(p.astype(BF16), vh)
                return m_new, l, acc

            init = (jnp.full((TA, 1), NEG, F32), jnp.zeros((TA, 1), F32), jnp.zeros((TA, LANE), F32))
            carry = lax.fori_loop(0, qi, lambda kb, c: step(kb, c, False), init)
            m, l, acc = step(qi, carry, True)
            o_tot = o_tot + acc / l
            lse_ref[:, cs] = jnp.broadcast_to(m + jnp.log(l), (TA, LANE))
        o_ref[...] = o_tot
        y_ref[...] = (o_tot * _silu(g_ref[...])).astype(BF16)

    blk = pl.BlockSpec((TA, LANE), lambda p, i: (i, p))
    return pl.pallas_call(
        body, name="mla_attn_fwd", grid=(4, nq),
        in_specs=[pl.BlockSpec((TA, 256), lambda p, i: (i, p)), pl.BlockSpec((S, 256), lambda p, i: (0, p)),
                  pl.BlockSpec((S, LANE), lambda p, i: (0, p)), pl.BlockSpec((TA, LANE), lambda p, i: (i, C_MLAG + p))],
        out_specs=[blk, pl.BlockSpec((TA, 256), lambda p, i: (i, p)), blk],
        out_shape=[jax.ShapeDtypeStruct((S, 512), F32), jax.ShapeDtypeStruct((S, 1024), F32), jax.ShapeDtypeStruct((S, 512), BF16)],
        compiler_params=_cp(dimension_semantics=("parallel", "arbitrary")),
    )(q, k, v, zp)


def _mla_post_bwd(zp, o, dy, dz):
    S = zp.shape[0]
    T = T_ROW

    def body(g_ref, o_ref, dy_ref, dz_in, dz_ref, do_ref, D_ref):
        del dz_in
        g, o_, dy_ = g_ref[...], o_ref[...], dy_ref[...]
        do = dy_ * _silu(g)
        do_ref[...] = do.astype(BF16)
        dz_ref[...] = (dy_ * o_ * _dsilu(g)).astype(BF16)
        prod = do * o_
        lane = _lane((T, LANE))
        for p in range(4):
            pr = prod[:, p * LANE:(p + 1) * LANE]
            da = _rsum(jnp.where(lane < 64, pr, 0.0))
            db = _rsum(jnp.where(lane >= 64, pr, 0.0))
            D_ref[:, 2 * p * LANE:(2 * p + 1) * LANE] = jnp.broadcast_to(da, (T, LANE))
            D_ref[:, (2 * p + 1) * LANE:(2 * p + 2) * LANE] = jnp.broadcast_to(db, (T, LANE))

    row = pl.BlockSpec((T, 512), lambda i: (i, 0))
    zc = _zcol(T, 512, C_MLAG)
    return pl.pallas_call(
        body, name="mla_post_bwd", grid=(S // T,), in_specs=[zc, row, row, pl.BlockSpec(memory_space=pl.ANY)],
        out_specs=[zc, row, pl.BlockSpec((T, 1024), lambda i: (i, 0))],
        out_shape=[jax.ShapeDtypeStruct(dz.shape, BF16), jax.ShapeDtypeStruct((S, 512), BF16), jax.ShapeDtypeStruct((S, 1024), F32)],
        input_output_aliases={3: 0}, compiler_params=_cp(),
    )(zp, o, dy, dz)


def _mla_attn_bwd(q, k, v, do, lse, Dr):
    S = q.shape[0]
    nq = S // TA

    def body(q_ref, do_ref, lse_ref, D_ref, k_ref, v_ref, dq_ref, dk_ref, dv_ref):
        ki = pl.program_id(1)

        @pl.when(ki == 0)
        def _():
            dq_ref[...] = jnp.zeros_like(dq_ref)

        lane = _lane((TA, LANE))
        rowi = lax.broadcasted_iota(jnp.int32, (TA, TA), 0)
        coli = lax.broadcasted_iota(jnp.int32, (TA, TA), 1)
        dv_tot = jnp.zeros((TA, LANE), F32)
        for hh in range(2):
            cs = slice(hh * LANE, (hh + 1) * LANE)
            hm = (lane < 64) if hh == 0 else (lane >= 64)
            kh = k_ref[:, cs]
            vv = v_ref[...]
            vm = jnp.where(hm, vv, jnp.zeros_like(vv))

            def step(qb, carry, masked, cs=cs, kh=kh, vm=vm):
                dk_acc, dv_acc = carry
                off = pl.multiple_of(qb * TA, TA)
                qh = q_ref[pl.ds(off, TA), cs]
                doh = do_ref[pl.ds(off, TA), :]
                ls = jnp.tile(lse_ref[pl.ds(off, TA), cs], (1, TA // LANE))
                dd = jnp.tile(D_ref[pl.ds(off, TA), cs], (1, TA // LANE))
                s = _nt(qh, kh) * MLA_SCALE
                if masked:
                    s = jnp.where(rowi >= coli, s, NEG)
                p = jnp.exp(s - ls)
                dp = _nt(doh, vm)
                ds = (p * (dp - dd) * MLA_SCALE).astype(BF16)
                dv_acc = dv_acc + _tn(p.astype(BF16), doh)
                dk_acc = dk_acc + _tn(ds, qh)
                dq_ref[pl.ds(off, TA), cs] += _nn(ds, kh)
                return dk_acc, dv_acc

            z = jnp.zeros((TA, LANE), F32)
            carry = step(ki, (z, z), True)
            dk_acc, dv_acc = lax.fori_loop(ki + 1, nq, lambda qb, c: step(qb, c, False), carry)
            dk_ref[:, cs] = dk_acc
            dv_tot = dv_tot + jnp.where(hm, dv_acc, 0.0)
        dv_ref[...] = dv_tot

    pair = pl.BlockSpec((S, 256), lambda p, i: (0, p))
    return pl.pallas_call(
        body, name="mla_attn_bwd", grid=(4, nq),
        in_specs=[pair, pl.BlockSpec((S, LANE), lambda p, i: (0, p)), pair, pair,
                  pl.BlockSpec((TA, 256), lambda p, i: (i, p)), pl.BlockSpec((TA, LANE), lambda p, i: (i, p))],
        out_specs=[pair, pl.BlockSpec((TA, 256), lambda p, i: (i, p)), pl.BlockSpec((TA, LANE), lambda p, i: (i, p))],
        out_shape=[jax.ShapeDtypeStruct((S, 1024), F32), jax.ShapeDtypeStruct((S, 1024), F32), jax.ShapeDtypeStruct((S, 512), F32)],
        compiler_params=_cp(dimension_semantics=("parallel", "arbitrary")),
    )(q, do, lse, Dr, k, v)


def _mla_pre_bwd(zp, dq, dk, dv, w, tab, dz):
    S = zp.shape[0]
    T = T_MLA

    def body(cq_ref, ckv_ref, kr_ref, dq_ref, dk_ref, dv_ref, gcq_ref, gckv_ref, wuq_ref, wuk_ref, wuv_ref, gq_ref, gk_ref,
             C_ref, S1_ref, S2_ref, dz_in, dz_ref, dwuq_ref, dwuk_ref, dwuv_ref, dgcq_ref, dgckv_ref, dgq_ref, dgk_ref):
        del dz_in
        i = pl.program_id(0)

        @pl.when(i == 0)
        def _():
            for r in (dwuq_ref, dwuk_ref, dwuv_ref, dgcq_ref, dgckv_ref, dgq_ref, dgk_ref):
                r[...] = jnp.zeros_like(r)

        cq = cq_ref[...]
        rq = lax.rsqrt(jnp.mean(cq * cq, axis=-1, keepdims=True) + EPS)
        cqh = cq * rq
        cqn = (cqh * gcq_ref[...]).astype(BF16)
        ckv = ckv_ref[...]
        rkv = lax.rsqrt(jnp.mean(ckv * ckv, axis=-1, keepdims=True) + EPS)
        ckvh = ckv * rkv
        ckvn = (ckvh * gckv_ref[...]).astype(BF16)
        q0 = _nn(cqn, wuq_ref[...])
        k0 = _nn(ckvn, wuk_ref[...])
        krb = kr_ref[...]
        C, S1, S2 = C_ref[...], S1_ref[...], S2_ref[...]
        gq, gk = gq_ref[...], gk_ref[...]

        def head_bwd(x, dy, g):
            r = lax.rsqrt(_rsum(x * x) * (1.0 / MLA_QK) + EPS)
            xn = x * r
            dyn = _rope_t(dy, C, S1, S2, 16)
            dxh = dyn * g
            return r * (dxh - xn * _rsum(dxh * xn) * (1.0 / MLA_QK)), _csum(dyn * xn)

        dq0, dk0 = [], []
        dgq_acc = jnp.zeros((1, LANE), F32)
        dgk_acc = jnp.zeros((1, LANE), F32)
        dkr = jnp.zeros((T, LANE), F32)
        for h in range(8):
            sl = slice(h * LANE, (h + 1) * LANE)
            dxq, gq_p = head_bwd(q0[:, sl], dq_ref[:, sl], gq)
            dxk, gk_p = head_bwd(k0[:, sl] + krb, dk_ref[:, sl], gk)
            dq0.append(dxq.astype(BF16))
            dk0.append(dxk.astype(BF16))
            dkr = dkr + dxk
            dgq_acc = dgq_acc + gq_p
            dgk_acc = dgk_acc + gk_p
        dgq_ref[...] += dgq_acc
        dgk_ref[...] += dgk_acc
        dq0 = jnp.concatenate(dq0, axis=1)
        dk0 = jnp.concatenate(dk0, axis=1)
        dvb = dv_ref[...].astype(BF16)
        dwuq_ref[...] += _tn(cqn, dq0)
        dwuk_ref[...] += _tn(ckvn, dk0)
        dwuv_ref[...] += _tn(ckvn, dvb)
        dcqn = _nt(dq0, wuq_ref[...])
        dckvn = _nt(dk0, wuk_ref[...]) + _nt(dvb, wuv_ref[...])
        dgcq_ref[...] += _csum(dcqn * cqh)
        dgckv_ref[...] += _csum(dckvn * ckvh)
        dxh = dcqn * gcq_ref[...]
        dz_ref[:, 0:256] = (rq * (dxh - cqh * jnp.mean(dxh * cqh, axis=-1, keepdims=True))).astype(BF16)
        dxh = dckvn * gckv_ref[...]
        dz_ref[:, 256:384] = (rkv * (dxh - ckvh * jnp.mean(dxh * ckvh, axis=-1, keepdims=True))).astype(BF16)
        lane = _lane((T, LANE))
        dz_ref[:, 384:512] = jnp.where((lane >= KR_LANE) & (lane < KR_LANE + 32), dkr, 0.0).astype(BF16)

    tabspec = pl.BlockSpec((T, LANE), lambda i: (i, 0))
    in_specs = [_zcol(T, 256, C_CQ), _zcol(T, LANE, C_CKV), _zcol(T, LANE, C_KR),
                pl.BlockSpec((T, 1024), lambda i: (i, 0)), pl.BlockSpec((T, 1024), lambda i: (i, 0)), pl.BlockSpec((T, 512), lambda i: (i, 0)),
                _full((1, 256)), _full((1, LANE)), _full((256, 1024)), _full((LANE, 1024)), _full((LANE, 512)), _full((1, LANE)), _full((1, LANE)),
                tabspec, tabspec, tabspec, pl.BlockSpec(memory_space=pl.ANY)]
    out_specs = [_zcol(T, 512, C_CQ), _full((256, 1024)), _full((LANE, 1024)), _full((LANE, 512)), _full((1, 256)), _full((1, LANE)),
                 _full((1, LANE)), _full((1, LANE))]
    out_shape = [jax.ShapeDtypeStruct(dz.shape, BF16), jax.ShapeDtypeStruct((256, 1024), F32), jax.ShapeDtypeStruct((LANE, 1024), F32),
                 jax.ShapeDtypeStruct((LANE, 512), F32), jax.ShapeDtypeStruct((1, 256), F32), jax.ShapeDtypeStruct((1, LANE), F32),
                 jax.ShapeDtypeStruct((1, LANE), F32), jax.ShapeDtypeStruct((1, LANE), F32)]
    return pl.pallas_call(
        body, name="mla_pre_bwd", grid=(S // T,), in_specs=in_specs, out_specs=out_specs, out_shape=out_shape,
        input_output_aliases={16: 0}, compiler_params=_cp(),
    )(zp, zp, zp, dq, dk, dv, w["g_cq"], w["g_ckv"], w["w_uq"], w["w_uk"], w["w_uv"], w["g_mq"], w["g_mk"], *tab, dz)


T_DIL = 256


def _head_stats(x, lane):
    sq = x * x
    sa = _rsum(jnp.where(lane < 64, sq, 0.0))
    sb = _rsum(jnp.where(lane >= 64, sq, 0.0))
    return lax.rsqrt(jnp.where(lane < 64, sa, sb) * (1.0 / DIL_HD) + EPS)


def _head_sum(x, lane):
    sa = _rsum(jnp.where(lane < 64, x, 0.0))
    sb = _rsum(jnp.where(lane >= 64, x, 0.0))
    return jnp.where(lane < 64, sa, sb)


def _head_stats_mxu(x):
    r = lax.broadcasted_iota(jnp.int32, (LANE, LANE), 0)
    c = lax.broadcasted_iota(jnp.int32, (LANE, LANE), 1)
    ones = jnp.where((r < 64) == (c < 64), 1.0, 0.0).astype(F32)
    ss = lax.dot_general(x * x, ones, (((1,), (0,)), ((), ())), precision=lax.Precision.HIGHEST, preferred_element_type=F32)
    return lax.rsqrt(ss * (1.0 / DIL_HD) + EPS)


def _dil_pre_fwd(zp, w, tab):
    S = zp.shape[0]
    T = T_DIL

    def body(q_ref, k_ref, gq_ref, gk_ref, C_ref, S1_ref, S2_ref, qo_ref, ko_ref):
        C, S1, S2 = C_ref[...], S1_ref[...], S2_ref[...]
        for b in range(12):
            sl = slice(b * LANE, (b + 1) * LANE)
            x = q_ref[:, sl]
            qo_ref[:, sl] = _rope(x * _head_stats_mxu(x) * gq_ref[...], C, S1, S2, 32)
            x = k_ref[:, sl]
            ko_ref[:, sl] = _rope(x * _head_stats_mxu(x) * gk_ref[...], C, S1, S2, 32)

    tabspec = pl.BlockSpec((T, LANE), lambda i: (i, 0))
    out = pl.BlockSpec((T, 1536), lambda i: (i, 0))
    return pl.pallas_call(
        body, name="dil_pre_fwd", grid=(S // T,),
        in_specs=[_zcol(T, 1536, C_DQ), _zcol(T, 1536, C_DK), _full((1, LANE)), _full((1, LANE)), tabspec, tabspec, tabspec],
        out_specs=[out, out], out_shape=[jax.ShapeDtypeStruct((S, 1536), F32)] * 2, compiler_params=_cp(),
    )(zp, zp, w["g_dq"], w["g_dk"], *tab)


DIL_ROWS = 2048


def _dil_geometry(g, S):
    d = DIL_DILATIONS[g]
    P = NK * d
    return d, P, DIL_ROWS // P, S // P


def _dil_rows(start, d, blocks=1):
    return pl.ds(pl.multiple_of(start, NK), blocks * NK) if d == 1 else pl.ds(start, blocks * NK, stride=d)


def _dil_specs(g, S, col0):
    _, P, m, nb = _dil_geometry(g, S)
    cur = pl.BlockSpec((DIL_ROWS, LANE), lambda sb, c: (sb, col0 + c))
    prv = pl.BlockSpec((P, LANE), lambda sb, c: (jnp.maximum(sb * m - 1, 0), col0 + c))
    nxt = pl.BlockSpec((P, LANE), lambda sb, c: (jnp.minimum((sb + 1) * m, nb - 1), col0 + c))
    return cur, prv, nxt


def _dil_attn_fwd(q, k, zp, g):
    S = q.shape[0]
    d, P, m, nb = _dil_geometry(g, S)
    R = DIL_ROWS

    def body(q_ref, kc_ref, kp_ref, vc_ref, vp_ref, o_ref, lse_ref, *scr):
        sb = pl.program_id(0)
        if m > 1:
            ks_ref, vs_ref = scr
            ks_ref[0:P, :] = kp_ref[...]
            ks_ref[P:P + R, :] = kc_ref[...]
            vs_ref[0:P, :] = vp_ref[...]
            vs_ref[P:P + R, :] = vc_ref[...]
        lane = _lane((NK, LANE))

        def unit(u, carry):
            j = u // d
            start = j * P + (u - j * d)
            rows = _dil_rows(start, d)
            if m > 1:
                k2, v2 = ks_ref[_dil_rows(start, d, 2), :], vs_ref[_dil_rows(start, d, 2), :]
            else:
                k2 = jnp.concatenate([kp_ref[rows, :], kc_ref[rows, :]], axis=0)
                v2 = jnp.concatenate([vp_ref[rows, :], vc_ref[rows, :]], axis=0)
            k2, v2 = k2.astype(BF16), v2.astype(BF16)
            q_ = q_ref[rows, :].astype(BF16)
            row = lax.broadcasted_iota(jnp.int32, (NK, 2 * NK), 0)
            col = lax.broadcasted_iota(jnp.int32, (NK, 2 * NK), 1)
            band = (col >= row) & (col <= row + NK) & ((col >= NK) | (sb * m + j > 0))
            lane2 = _lane((2 * NK, LANE))
            zb, zv = jnp.zeros_like(q_), jnp.zeros_like(v2)
            o_tot = jnp.zeros((NK, LANE), F32)
            lse_tot = jnp.zeros((NK, LANE), F32)
            for hh in range(2):
                hm = (lane < 64) if hh == 0 else (lane >= 64)
                hm2 = (lane2 < 64) if hh == 0 else (lane2 >= 64)
                s_ = jnp.where(band, _nt(jnp.where(hm, q_, zb), k2) * DIL_SCALE, NEG)
                mx = jnp.max(s_, axis=-1, keepdims=True)
                e = jnp.exp(s_ - mx)
                den = _rsum(e)
                o_tot = o_tot + _nn(e.astype(BF16), jnp.where(hm2, v2, zv)) / den
                lse_tot = jnp.where(hm, mx + jnp.log(den), lse_tot)
            o_ref[rows, :] = o_tot
            lse_ref[rows, :] = lse_tot
            return carry

        lax.fori_loop(0, R // NK, unit, 0, unroll=8)

    qcur, qprv, _ = _dil_specs(g, S, 4 * g)
    vcur, vprv, _ = _dil_specs(g, S, C_DV + 4 * g)
    out = pl.BlockSpec((R, LANE), lambda sb, c: (sb, c))
    return pl.pallas_call(
        body, name=f"dil_attn_fwd{g}", grid=(S // R, 4), in_specs=[qcur, qcur, qprv, vcur, vprv], out_specs=[out, out],
        out_shape=[jax.ShapeDtypeStruct((S, 512), F32)] * 2,
        scratch_shapes=[pltpu.VMEM((P + R, LANE), F32)] * 2 if m > 1 else [], compiler_params=_cp(),
    )(q, k, k, zp, zp)


def _dil_combine(os_, ls_, zp):
    S = zp.shape[0]
    T = T_ROW

    def body(o0, o1, o2, l0, l1, l2, g_ref, oc_ref, L_ref, y_ref):
        a, b, c = l0[...], l1[...], l2[...]
        mx = jnp.maximum(jnp.maximum(a, b), c)
        ea, eb, ec = jnp.exp(a - mx), jnp.exp(b - mx), jnp.exp(c - mx)
        den = ea + eb + ec
        oc = (ea * o0[...] + eb * o1[...] + ec * o2[...]) / den
        oc_ref[...] = oc
        L_ref[...] = mx + jnp.log(den)
        y_ref[...] = (oc * _silu(g_ref[...])).astype(BF16)

    row = pl.BlockSpec((T, 512), lambda i: (i, 0))
    return pl.pallas_call(
        body, name="dil_combine", grid=(S // T,), in_specs=[row] * 6 + [_zcol(T, 512, C_DILG)], out_specs=[row, row, row],
        out_shape=[jax.ShapeDtypeStruct((S, 512), F32), jax.ShapeDtypeStruct((S, 512), F32), jax.ShapeDtypeStruct((S, 512), BF16)],
        compiler_params=_cp(),
    )(*os_, *ls_, zp)


def _dil_comb_bwd(zp, oc, dy, dz):
    S = zp.shape[0]
    T = T_ROW

    def body(g_ref, o_ref, dy_ref, dz_in, dz_ref, do_ref, D_ref):
        del dz_in
        g, o_, dy_ = g_ref[...], o_ref[...], dy_ref[...]
        do = dy_ * _silu(g)
        do_ref[...] = do
        dz_ref[...] = (dy_ * o_ * _dsilu(g)).astype(BF16)
        lane = _lane((T, LANE))
        for p in range(4):
            sl = slice(p * LANE, (p + 1) * LANE)
            D_ref[:, sl] = _head_sum(do[:, sl] * o_[:, sl], lane)

    row = pl.BlockSpec((T, 512), lambda i: (i, 0))
    zc = _zcol(T, 512, C_DILG)
    return pl.pallas_call(
        body, name="dil_comb_bwd", grid=(S // T,), in_specs=[zc, row, row, pl.BlockSpec(memory_space=pl.ANY)], out_specs=[zc, row, row],
        out_shape=[jax.ShapeDtypeStruct(dz.shape, BF16), jax.ShapeDtypeStruct((S, 512), F32), jax.ShapeDtypeStruct((S, 512), F32)],
        input_output_aliases={3: 0}, compiler_params=_cp(),
    )(zp, oc, dy, dz)


def _dil_attn_bwd(q, k, zp, do, L, Dr, g):
    S = q.shape[0]
    d, P, m, nb = _dil_geometry(g, S)
    R = DIL_ROWS
    n_q, n_k = 4, 2

    def body(*refs):
        q_side = refs[0:2 * n_q]
        k_side = refs[2 * n_q:2 * n_q + 2 * n_k]
        dq_ref, dk_ref, dv_ref = refs[2 * n_q + 2 * n_k:2 * n_q + 2 * n_k + 3]
        scr = refs[2 * n_q + 2 * n_k + 3:]
        sb = pl.program_id(0)
        if m > 1:
            for a in range(n_q):
                scr[a][0:R, :] = q_side[2 * a][...]
                scr[a][R:R + P, :] = q_side[2 * a + 1][...]
            for a in range(n_k):
                scr[n_q + a][0:P, :] = k_side[2 * a + 1][...]
                scr[n_q + a][P:P + R, :] = k_side[2 * a][...]
        lane = _lane((NK, LANE))

        def unit(u, carry):
            j = u // d
            start = j * P + (u - j * d)
            rows = _dil_rows(start, d)
            if m > 1:
                rows_b = _dil_rows(start + P, d)
                q2, do2, L2, D2 = [scr[a][_dil_rows(start, d, 2), :] for a in range(n_q)]
                kp, vp = [scr[n_q + a][rows, :] for a in range(n_k)]
                kc, vc = [scr[n_q + a][rows_b, :] for a in range(n_k)]
            else:
                q2, do2, L2, D2 = [jnp.concatenate([q_side[2 * a][rows, :], q_side[2 * a + 1][rows, :]], axis=0) for a in range(n_q)]
                kc, vc = [k_side[2 * a][rows, :] for a in range(n_k)]
                kp, vp = [k_side[2 * a + 1][rows, :] for a in range(n_k)]
            q2, do2 = q2.astype(BF16), do2.astype(BF16)
            kc, kp, vc, vp = kc.astype(BF16), kp.astype(BF16), vc.astype(BF16), vp.astype(BF16)
            n = sb * m + j
            row2 = lax.broadcasted_iota(jnp.int32, (2 * NK, NK), 0)
            col2 = lax.broadcasted_iota(jnp.int32, (2 * NK, NK), 1)
            m2 = ((row2 < NK) & (col2 <= row2)) | ((row2 >= NK) & (col2 >= row2 - NK) & (n < nb - 1))
            row = lax.broadcasted_iota(jnp.int32, (NK, NK), 0)
            col = lax.broadcasted_iota(jnp.int32, (NK, NK), 1)
            mp = (col >= row) & (n > 0)
            lane2 = _lane((2 * NK, LANE))
            zq, zb = jnp.zeros_like(q2), jnp.zeros_like(kc)
            dq_tot = jnp.zeros((NK, LANE), F32)
            dk_tot = jnp.zeros((NK, LANE), F32)
            dv_tot = jnp.zeros((NK, LANE), F32)
            for hh in range(2):
                hm = (lane < 64) if hh == 0 else (lane >= 64)
                hm2 = (lane2 < 64) if hh == 0 else (lane2 >= 64)
                Lb = jnp.where(hm2, L2, pltpu.roll(L2, 64, 1))
                Db = jnp.where(hm2, D2, pltpu.roll(D2, 64, 1))
                qm2 = jnp.where(hm2, q2, zq)
                vcm = jnp.where(hm, vc, zb)
                vpm = jnp.where(hm, vp, zb)
                p2 = jnp.exp(jnp.where(m2, _nt(qm2, kc) * DIL_SCALE, NEG) - Lb)
                ds2 = (p2 * (_nt(do2, vcm) - Db) * DIL_SCALE).astype(BF16)
                dk_tot = dk_tot + _tn(ds2, qm2)
                dv_tot = dv_tot + jnp.where(hm, _tn(p2.astype(BF16), do2), 0.0)
                pp = jnp.exp(jnp.where(mp, _nt(qm2[0:NK], kp) * DIL_SCALE, NEG) - Lb[0:NK])
                dsp = (pp * (_nt(do2[0:NK], vpm) - Db[0:NK]) * DIL_SCALE).astype(BF16)
                dq_tot = dq_tot + jnp.where(hm, _nn(ds2[0:NK], kc) + _nn(dsp, kp), 0.0)
            dq_ref[rows, :] = dq_tot
            dk_ref[rows, :] = dk_tot
            dv_ref[rows, :] = dv_tot
            return carry

        lax.fori_loop(0, R // NK, unit, 0, unroll=8)

    qcur, qprv, qnxt = _dil_specs(g, S, 4 * g)
    vcur, vprv, _ = _dil_specs(g, S, C_DV + 4 * g)
    ocur, _, onxt = _dil_specs(g, S, 0)
    out = pl.BlockSpec((R, LANE), lambda sb, c: (sb, c))
    scratch = [pltpu.VMEM((P + R, LANE), F32)] * (n_q + n_k) if m > 1 else []
    return pl.pallas_call(
        body, name=f"dil_attn_bwd{g}", grid=(S // R, 4),
        in_specs=[qcur, qnxt, ocur, onxt, ocur, onxt, ocur, onxt, qcur, qprv, vcur, vprv],
        out_specs=[out, out, out], out_shape=[jax.ShapeDtypeStruct((S, 512), F32)] * 3, scratch_shapes=scratch, compiler_params=_cp(),
    )(q, q, do, do, L, L, Dr, Dr, k, k, zp, zp)


def _dil_pre_bwd(zp, dys, g, tab, dz, col, name):
    S = zp.shape[0]
    T = T_DIL

    def body(x_ref, dy0_ref, dy1_ref, dy2_ref, g_ref, C_ref, S1_ref, S2_ref, dz_in, dz_ref, dg_ref):
        del dz_in
        i = pl.program_id(0)
        C, S1, S2 = C_ref[...], S1_ref[...], S2_ref[...]
        lane = _lane((T, LANE))
        gv = g_ref[...]
        acc = jnp.zeros((1, LANE), F32)
        for b in range(12):
            sl = slice(b * LANE, (b + 1) * LANE)
            x = x_ref[:, sl]
            r = _head_stats(x, lane)
            xn = x * r
            dy_ref = (dy0_ref, dy1_ref, dy2_ref)[b // 4]
            dyn = _rope_t(dy_ref[:, (b % 4) * LANE:(b % 4 + 1) * LANE], C, S1, S2, 32)
            acc = acc + _csum(dyn * xn)
            dxh = dyn * gv
            dz_ref[:, sl] = (r * (dxh - xn * _head_sum(dxh * xn, lane) * (1.0 / DIL_HD))).astype(BF16)

        @pl.when(i == 0)
        def _():
            dg_ref[...] = acc

        @pl.when(i > 0)
        def _():
            dg_ref[...] += acc

    tabspec = pl.BlockSpec((T, LANE), lambda i: (i, 0))
    zc = _zcol(T, 1536, col)
    grp = pl.BlockSpec((T, 512), lambda i: (i, 0))
    return pl.pallas_call(
        body, name=name, grid=(S // T,),
        in_specs=[zc, grp, grp, grp, _full((1, LANE)), tabspec, tabspec, tabspec, pl.BlockSpec(memory_space=pl.ANY)],
        out_specs=[zc, _full((1, LANE))], out_shape=[jax.ShapeDtypeStruct(dz.shape, BF16), jax.ShapeDtypeStruct((1, LANE), F32)],
        input_output_aliases={8: 0}, compiler_params=_cp(),
    )(zp, *dys, g, *tab, dz)


def _dil_dv_into(dvs, dz):
    S = dz.shape[0]
    T = T_ROW

    def body(s0, s1, s2, dz_in, o_ref):
        del dz_in
        for gi, s in enumerate((s0, s1, s2)):
            o_ref[:, gi * 512:(gi + 1) * 512] = s[...].astype(BF16)

    grp = pl.BlockSpec((T, 512), lambda i: (i, 0))
    return pl.pallas_call(
        body, name="dil_dv", grid=(S // T,), in_specs=[grp, grp, grp, pl.BlockSpec(memory_space=pl.ANY)],
        out_specs=_zcol(T, 1536, C_DV), out_shape=jax.ShapeDtypeStruct(dz.shape, BF16), input_output_aliases={3: 0}, compiler_params=_cp(),
    )(*dvs, dz)


T_MRG = 256


def _merge_fwd(P, zp, b_merge):
    S = zp.shape[0]
    T = T_MRG

    def body(p0, p1, p2, m0, m1, m2, b_ref, o_ref):
        acc = jnp.zeros((T, D), F32)
        for j, (p, m) in enumerate(((p0, m0), (p1, m1), (p2, m2))):
            acc = acc + _sig(m[...] + b_ref[:, j * D:(j + 1) * D]) * p[...]
        o_ref[...] = acc.astype(BF16)

    row = pl.BlockSpec((T, D), lambda i: (i, 0))
    return pl.pallas_call(
        body, name="merge_fwd", grid=(S // T,),
        in_specs=[row, row, row] + [_zcol(T, D, C_MERGE + 8 * j) for j in range(3)] + [_full((1, 3 * D))], out_specs=row,
        out_shape=jax.ShapeDtypeStruct((S, D), BF16), compiler_params=_cp(),
    )(*P, zp, zp, zp, b_merge)


def _merge_bwd(dm, Pj, zp, bj, dz, j):
    S = zp.shape[0]
    T = T_MRG

    def body(dm_ref, p_ref, m_ref, b_ref, dz_in, dz_ref, dp_ref, db_ref):
        del dz_in
        i = pl.program_id(0)
        g = _sig(m_ref[...] + b_ref[...])
        dmv = dm_ref[...]
        dp_ref[...] = (dmv * g).astype(BF16)
        dg = dmv * p_ref[...] * g * (1.0 - g)
        dz_ref[...] = dg.astype(BF16)
        part = _csum(dg)

        @pl.when(i == 0)
        def _():
            db_ref[...] = part

        @pl.when(i > 0)
        def _():
            db_ref[...] += part

    row = pl.BlockSpec((T, D), lambda i: (i, 0))
    zc = _zcol(T, D, C_MERGE + 8 * j)
    return pl.pallas_call(
        body, name=f"merge_bwd{j}", grid=(S // T,), in_specs=[row, row, zc, _full((1, D)), pl.BlockSpec(memory_space=pl.ANY)],
        out_specs=[zc, row, _full((1, D))],
        out_shape=[jax.ShapeDtypeStruct(dz.shape, BF16), jax.ShapeDtypeStruct((S, D), BF16), jax.ShapeDtypeStruct((1, D), F32)],
        input_output_aliases={4: 0}, compiler_params=_cp(),
    )(dm, Pj, zp, bj, dz)


def _loss_fwd_bwd(y, target):
    S = y.shape[0]
    T = T_ROW

    def body(y_ref, t_ref, loss_ref, dy_ref):
        i = pl.program_id(0)
        err = y_ref[...] - t_ref[...]
        dy_ref[...] = err * (1.0 / D)
        part = jnp.sum(err * err, keepdims=True).reshape(1, 1) * (0.5 / D)

        @pl.when(i == 0)
        def _():
            loss_ref[...] = part

        @pl.when(i > 0)
        def _():
            loss_ref[...] += part

    row = pl.BlockSpec((T, D), lambda i: (i, 0))
    return pl.pallas_call(
        body, name="loss", grid=(S // T,), in_specs=[row, row], out_specs=[_full((1, 1)), row],
        out_shape=[jax.ShapeDtypeStruct((1, 1), F32), jax.ShapeDtypeStruct((S, D), F32)], compiler_params=_cp(),
    )(y, target)


def _layer_fwd(x, w, tabs):
    mla_tab, dil_tab = tabs
    S = x.shape[0]
    h = _rms_in_fwd(x, w["norm_g"])
    zp = _mm(h, w["w_in"], mode="nn", name="in_proj")
    hs, y_lru = _lru_fwd(zp, w)
    q, k, v = _mla_pre_fwd(zp, w, mla_tab)
    o_mla, lse, y_mla = _mla_attn_fwd(q, k, v, zp)
    qd, kd = _dil_pre_fwd(zp, w, dil_tab)
    og, lg = zip(*[_dil_attn_fwd(qd, kd, zp, g) for g in range(len(DIL_DILATIONS))])
    oc, L, y_dil = _dil_combine(og, lg, zp)
    P = [_mm(y_lru, w["w_lru_o"], mode="nn", name="lru_out"), _mm(y_mla, w["w_mla_o"], mode="nn", name="mla_out"),
         _mm(y_dil, w["w_dil_o"], mode="nn", name="dil_out")]
    merged = _merge_fwd(P, zp, w["b_merge"])
    x_out = _mm(merged, w["w_out"], mode="nn", name="out_proj", add=x)
    saved = dict(x=x, h=h, zp=zp, hs=hs, y=(y_lru, y_mla, y_dil), q=q, k=k, v=v, o_mla=o_mla, lse=lse, qd=qd, kd=kd, oc=oc, L=L, P=P,
                 merged=merged)
    return x_out, saved


def _layer_bwd(dout, w, tabs, sv, hook=None, after=None):
    mla_tab, dil_tab = tabs
    zp = sv["zp"]
    S = zp.shape[0]
    g = {}
    dm = _mm(dout, w["w_out"], mode="nt", name="d_merged", after=after)
    g["w_out"] = _mm(sv["merged"], dout, mode="tn", name="dw_out", out_dtype=BF16)
    dz = lax.empty((S, ZW), BF16)
    dP, db = [], []
    for j in range(3):
        dz, dpj, dbj = _merge_bwd(dm, sv["P"][j], zp, w["b_merge"][:, j * D:(j + 1) * D], dz, j)
        dP.append(dpj)
        db.append(dbj)
    g["b_merge"] = jnp.concatenate(db, axis=1)
    names = ("w_lru_o", "w_mla_o", "w_dil_o")
    dy = []
    for j in range(3):
        dy.append(_mm(dP[j], w[names[j]], mode="nt", name="dy_" + names[j]))
        g[names[j]] = _mm(sv["y"][j], dP[j], mode="tn", name="d" + names[j], out_dtype=BF16)
    dz = _lru_gate_bwd(zp, sv["hs"], dy[0], dz)
    dz, g["conv_w"], g["conv_b"], g["w_gx"], g["b_gx"], g["w_ga"], g["b_ga"], g["lam"] = _lru_bwd(zp, sv["hs"], dy[0], w, dz)
    dz, do, Dr = _mla_post_bwd(zp, sv["o_mla"], dy[1], dz)
    dq, dk, dv = _mla_attn_bwd(sv["q"], sv["k"], sv["v"], do, sv["lse"], Dr)
    dz, g["w_uq"], g["w_uk"], g["w_uv"], g["g_cq"], g["g_ckv"], g["g_mq"], g["g_mk"] = _mla_pre_bwd(zp, dq, dk, dv, w, mla_tab, dz)
    dz, dod, Dd = _dil_comb_bwd(zp, sv["oc"], dy[2], dz)
    dqs, dks, dvs = zip(*[_dil_attn_bwd(sv["qd"], sv["kd"], zp, dod, sv["L"], Dd, gi) for gi in range(len(DIL_DILATIONS))])
    dz, g["g_dq"] = _dil_pre_bwd(zp, dqs, w["g_dq"], dil_tab, dz, C_DQ, "dil_pre_bwd_q")
    dz, g["g_dk"] = _dil_pre_bwd(zp, dks, w["g_dk"], dil_tab, dz, C_DK, "dil_pre_bwd_k")
    dz = _dil_dv_into(dvs, dz)
    g["w_in"] = _mm(sv["h"], dz, mode="tn", name="dw_in", out_dtype=BF16)
    token = hook(g) if hook is not None else None
    dh = _mm(dz, w["w_in"], mode="nt", name="d_h", after=token, tk=ZW // 4)
    dx, g["norm_g"] = _rms_in_bwd(sv["x"], w["norm_g"], dh, dout)
    return dx, g


def _peers():
    mx, my, mc = lax.axis_index("x"), lax.axis_index("y"), lax.axis_index("c")
    me = 4 * mx + 2 * my + mc
    out = []
    for k in range(1, N_DEV):
        px = 1 - mx if k & 4 else mx
        py = 1 - my if k & 2 else my
        pc = 1 - mc if k & 1 else mc
        out.append(((px, py, pc), 4 * px + 2 * py + pc))
    return me, out


def _whole(ref, p):
    del p
    return ref


def _exchange(srcs, slicers, slices, name):
    n = len(srcs)

    def body(*refs):
        ins, outs = refs[:n], refs[n:2 * n]
        send_sems, recv_sems, local_sems = refs[2 * n:]
        me, peers = _peers()
        mine = [pltpu.make_async_copy(slicers[a](ins[a], me), outs[a].at[me], local_sems.at[a]) for a in range(n)]
        for cp in mine:
            cp.start()
        copies = []
        for k, (peer, pidx) in enumerate(peers):
            for a in range(n):
                cp = pltpu.make_async_remote_copy(
                    src_ref=slicers[a](ins[a], pidx), dst_ref=outs[a].at[me], send_sem=send_sems.at[k * n + a],
                    recv_sem=recv_sems.at[k * n + a], device_id=peer, device_id_type=pl.DeviceIdType.MESH)
                cp.start()
                copies.append(cp)
        for cp in copies + mine:
            cp.wait()

    nsem = (N_DEV - 1) * n
    return pl.pallas_call(
        body, name=name, out_shape=[jax.ShapeDtypeStruct((N_DEV,) + shp, dt) for shp, dt in slices],
        in_specs=[pl.BlockSpec(memory_space=pl.ANY)] * n, out_specs=[pl.BlockSpec(memory_space=pl.ANY)] * n,
        scratch_shapes=[pltpu.SemaphoreType.DMA((nsem,)), pltpu.SemaphoreType.DMA((nsem,)), pltpu.SemaphoreType.DMA((n,))],
        compiler_params=pltpu.CompilerParams(has_side_effects=True),
    )(*srcs)


def _gather_two_level(srcs, name):
    n = len(srcs)

    def body(*refs):
        ins, outs = refs[:n], refs[n:2 * n]
        send_sems, recv_sems, local_sems = refs[2 * n:]
        mx, my, mc = lax.axis_index("x"), lax.axis_index("y"), lax.axis_index("c")
        me, sibling = (mx, my, mc), (mx, my, 1 - mc)
        chips = [(1 - mx, my), (mx, 1 - my), (1 - mx, 1 - my)]
        slot = lambda d: 4 * d[0] + 2 * d[1] + d[2]

        def copy(j, a, block, to, own=False):
            return pltpu.make_async_remote_copy(
                src_ref=ins[a] if own else outs[a].at[slot(block)], dst_ref=outs[a].at[slot(block)],
                send_sem=send_sems.at[j * n + a], recv_sem=recv_sems.at[j * n + a], device_id=to, device_id_type=pl.DeviceIdType.MESH)

        mine = [pltpu.make_async_copy(ins[a], outs[a].at[slot(me)], local_sems.at[a]) for a in range(n)]
        first = [copy(1 + j, a, me, (*chip, mc), own=True) for j, chip in enumerate(chips) for a in range(n)]
        first += [copy(0, a, me, sibling, own=True) for a in range(n)]
        for cp in mine + first:
            cp.start()
        passed = []
        for j, chip in enumerate(chips):
            for a in range(n):
                copy(1 + j, a, (*chip, mc), me).wait_recv()
                cp = copy(4 + j, a, (*chip, mc), sibling)
                cp.start()
                passed.append(cp)
        for a in range(n):
            copy(0, a, sibling, me).wait_recv()
        for j, chip in enumerate(chips):
            for a in range(n):
                copy(4 + j, a, (*chip, 1 - mc), me).wait_recv()
        for cp in first + passed:
            cp.wait_send()
        for cp in mine:
            cp.wait()

    nsem = (N_DEV - 1) * n
    return pl.pallas_call(
        body, name=name, out_shape=[jax.ShapeDtypeStruct((N_DEV,) + a.shape, a.dtype) for a in srcs],
        in_specs=[pl.BlockSpec(memory_space=pl.ANY)] * n, out_specs=[pl.BlockSpec(memory_space=pl.ANY)] * n,
        scratch_shapes=[pltpu.SemaphoreType.DMA((nsem,)), pltpu.SemaphoreType.DMA((nsem,)), pltpu.SemaphoreType.DMA((n,))],
        compiler_params=pltpu.CompilerParams(has_side_effects=True),
    )(*srcs)


_HBM = pl.BlockSpec(memory_space=pltpu.HBM)
_SEM = pl.BlockSpec(memory_space=pltpu.SEMAPHORE)
_DATAFLOW = pltpu.SideEffectType.DATAFLOW_SIDE_EFFECTING


def _plan_chips():
    mx, my, mc = lax.axis_index("x"), lax.axis_index("y"), lax.axis_index("c")
    return 2 * mx + my, [((cx, cy, mc), 2 * cx + cy) for cx, cy in ((1 - mx, my), (mx, 1 - my), (1 - mx, 1 - my))]


def _pair_exchange(srcs, slicers, slices, sliced, name):
    n = len(srcs)
    pieces = [4 if s else 1 for s in sliced]

    def body(*refs):
        ins, outs = refs[:n], refs[n:2 * n]
        send_sems, recv_sems = refs[2 * n:]
        mx, my, mc = lax.axis_index("x"), lax.axis_index("y"), lax.axis_index("c")
        copies = []
        for a in range(n):
            for q in range(pieces[a]):
                i = len(copies)
                copies.append(pltpu.make_async_remote_copy(
                    src_ref=slicers[a](ins[a], 2 * q + 1 - mc) if sliced[a] else ins[a], dst_ref=outs[a].at[q],
                    send_sem=send_sems.at[i], recv_sem=recv_sems.at[i], device_id=(mx, my, 1 - mc), device_id_type=pl.DeviceIdType.MESH))
        for cp in copies:
            cp.start()
        for cp in copies:
            cp.wait()

    return pl.pallas_call(
        body, name=name, out_shape=[jax.ShapeDtypeStruct((p,) + shp, dt) for (shp, dt), p in zip(slices, pieces)],
        in_specs=[pl.BlockSpec(memory_space=pl.ANY)] * n, out_specs=[pl.BlockSpec(memory_space=pl.ANY)] * n,
        scratch_shapes=[pltpu.SemaphoreType.DMA((sum(pieces),)), pltpu.SemaphoreType.DMA((sum(pieces),))],
        compiler_params=pltpu.CompilerParams(has_side_effects=True),
    )(*srcs)


def _pair_add(src, came, first_blk, axis, name):
    _, r, c = came.shape
    nblk = (c if axis == 1 else r) // LANE
    if axis == 1:
        s_spec = pl.BlockSpec((r, LANE), lambda q, j, fb: (0, fb[q] + j))
        o_spec = pl.BlockSpec((1, r, LANE), lambda q, j, fb: (q, 0, j))
    else:
        s_spec = pl.BlockSpec((LANE, c), lambda q, j, fb: (fb[q] + j, 0))
        o_spec = pl.BlockSpec((1, LANE, c), lambda q, j, fb: (q, j, 0))

    def body(fb_ref, x_ref, y_ref, o_ref):
        del fb_ref
        o_ref[0] = (x_ref[...].astype(F32) + y_ref[0].astype(F32)).astype(o_ref.dtype)

    return pl.pallas_call(
        body, name=name, out_shape=jax.ShapeDtypeStruct(came.shape, came.dtype),
        grid_spec=pltpu.PrefetchScalarGridSpec(num_scalar_prefetch=1, grid=(4, nblk), in_specs=[s_spec, o_spec], out_specs=o_spec),
        compiler_params=_cp(),
    )(first_blk, src, came)


def _add2(x, y, name):
    shp = x.shape
    x, y = x.reshape(-1, shp[-1]), y.reshape(-1, shp[-1])
    R, C = x.shape
    tr = R
    while tr * C * 4 > (1 << 21) and tr % 32 == 0:
        tr //= 2

    def body(x_ref, y_ref, o_ref):
        o_ref[...] = (x_ref[...].astype(F32) + y_ref[...].astype(F32)).astype(o_ref.dtype)

    spec = pl.BlockSpec((tr, C), lambda i: (i, 0))
    return pl.pallas_call(body, name=name, grid=(R // tr,), in_specs=[spec, spec], out_specs=spec,
                          out_shape=jax.ShapeDtypeStruct((R, C), x.dtype), compiler_params=_cp())(x, y).reshape(shp)


def _exchange_start(srcs, slicers, slices, after, name, plan=_peers, nslots=N_DEV):
    n = len(srcs)
    nsem = (nslots - 1) * n
    lands = [lax.empty((nslots,) + shp, dt) for shp, dt in slices]

    def body(*refs):
        ins, lands_in = refs[:n], refs[n:2 * n]
        send_sems, recv_sems, local_sems = refs[2 * n + 1], refs[2 * n + 2], refs[2 * n + 3]
        token = refs[-1]
        me, peers = plan()
        for a in range(n):
            pltpu.make_async_copy(slicers[a](ins[a], me), lands_in[a].at[me], local_sems.at[a]).start()
        for k, (peer, pidx) in enumerate(peers):
            for a in range(n):
                pltpu.make_async_remote_copy(
                    src_ref=slicers[a](ins[a], pidx), dst_ref=lands_in[a].at[me], send_sem=send_sems.at[k * n + a],
                    recv_sem=recv_sems.at[k * n + a], device_id=peer, device_id_type=pl.DeviceIdType.MESH).start()
        token[...] = jnp.zeros_like(token)

    hbm = lambda a: pltpu.with_memory_space_constraint(a, pltpu.HBM)
    return pl.pallas_call(
        body, name=name,
        out_shape=(pltpu.SemaphoreType.DMA((nsem,)), pltpu.SemaphoreType.DMA((nsem,)), pltpu.SemaphoreType.DMA((n,)),
                   *[pltpu.HBM(a.shape, a.dtype) for a in srcs], *[pltpu.HBM(a.shape, a.dtype) for a in lands],
                   jax.ShapeDtypeStruct((SUB, LANE), F32)),
        in_specs=[_HBM] * (2 * n) + [pl.BlockSpec(memory_space=pl.ANY)],
        out_specs=(_SEM, _SEM, _SEM, *[_HBM] * (2 * n), pl.BlockSpec(memory_space=pltpu.VMEM)),
        input_output_aliases={i: 3 + i for i in range(2 * n)},
        compiler_params=pltpu.CompilerParams(has_side_effects=_DATAFLOW),
    )(*[hbm(a) for a in srcs], *[hbm(a) for a in lands], after)


def _exchange_wait(started, slicers, after, name, plan=_peers):
    n = (len(started) - 4) // 2
    sems, thru = started[0:3], started[3:3 + 2 * n]

    def body(*refs):
        srcs, lands = refs[:n], refs[n:2 * n]
        send_sems, recv_sems, local_sems = refs[2 * n], refs[2 * n + 1], refs[2 * n + 2]
        me, peers = plan()
        for k, (peer, pidx) in enumerate(peers):
            for a in range(n):
                cp = pltpu.make_async_remote_copy(
                    src_ref=slicers[a](srcs[a], pidx), dst_ref=lands[a].at[me], send_sem=send_sems.at[k * n + a],
                    recv_sem=recv_sems.at[k * n + a], device_id=peer, device_id_type=pl.DeviceIdType.MESH)
                cp.wait_send()
                cp.wait_recv()
        for a in range(n):
            pltpu.make_async_copy(slicers[a](srcs[a], me), lands[a].at[me], local_sems.at[a]).wait()

    outs = pl.pallas_call(
        body, name=name, out_shape=[pltpu.HBM(a.shape, a.dtype) for a in thru],
        in_specs=[_HBM] * (2 * n) + [_SEM, _SEM, _SEM, pl.BlockSpec(memory_space=pl.ANY)], out_specs=[_HBM] * (2 * n),
        input_output_aliases={i: i for i in range(2 * n)}, compiler_params=pltpu.CompilerParams(has_side_effects=_DATAFLOW),
    )(*thru, *sems, after)
    return outs[n:]


WIN = 13 * LANE


def _win_base(s):
    n = s * SHARD_IN
    a0 = n + jnp.where(n >= _KR0, KR_LANE, 0) + jnp.where(n >= _KR0 + 32, 32, 0)
    return jnp.minimum(a0 // LANE, (ZW - WIN) // LANE)


def _win_offsets(s):
    n = s * SHARD_IN + jnp.arange(SHARD_IN)
    o = s * SHARD_IN - _win_base(s) * LANE
    return n, (o, o + KR_LANE, o + LANE - 32)


def _to_window(shard, s):
    _, offs = _win_offsets(s)
    padded = jnp.pad(shard, ((0, 0), (0, 0), (WIN, WIN)))
    a, b, c = [lax.dynamic_slice(padded, (0, 0, WIN - o), shard.shape[:2] + (WIN,)) for o in offs]
    col = (_win_base(s) * LANE + jnp.arange(WIN))[None, None, :]
    zero = jnp.zeros_like(a)
    return jnp.where(col < _KR0, a, jnp.where((col >= _KR0 + KR_LANE) & (col < _KR0 + KR_LANE + 32), b, jnp.where(col >= _KR0 + LANE, c, zero)))


def _from_window(win, s):
    n, offs = _win_offsets(s)
    a, b, c = [lax.dynamic_slice(win, (0, 0, o), win.shape[:2] + (SHARD_IN,)) for o in offs]
    return jnp.where((n < _KR0)[None, None, :], a, jnp.where((n < _KR0 + 32)[None, None, :], b, c))


def _win_base_static(s):
    n = s * SHARD_IN
    a0 = n + (KR_LANE if n >= _KR0 else 0) + (32 if n >= _KR0 + 32 else 0)
    return min(a0 // LANE, (ZW - WIN) // LANE)


def _assemble_w_in(gw):
    tr = 128
    bases = [_win_base_static(s) for s in range(N_DEV)]

    def body(g_ref, o_ref):
        for j in range(ZW // LANE):
            acc = None
            for s in range(N_DEV):
                if bases[s] <= j < bases[s] + WIN // LANE:
                    piece = g_ref[s, :, (j - bases[s]) * LANE:(j - bases[s] + 1) * LANE]
                    acc = piece if acc is None else acc + piece
            o_ref[:, j * LANE:(j + 1) * LANE] = acc

    return pl.pallas_call(
        body, name="assemble_w_in", grid=(D // tr,), in_specs=[pl.BlockSpec((N_DEV, tr, WIN), lambda i: (0, i, 0))],
        out_specs=pl.BlockSpec((tr, ZW), lambda i: (i, 0)), out_shape=jax.ShapeDtypeStruct((D, ZW), gw.dtype), compiler_params=_cp(),
    )(gw)


def _cols(width):
    return lambda ref, p: ref.at[:, pl.ds(pl.multiple_of(p * width, width), width)]


def _rows(height):
    return lambda ref, p: ref.at[pl.ds(pl.multiple_of(p * height, height), height), :]


SCATTER = {
    'w_in': (lambda ref, p: ref.at[:, pl.ds(pl.multiple_of(_win_base(p) * LANE, LANE), WIN)], (D, WIN), BF16),
    'conv_w': (_cols(LANE), (4, LANE), F32),
    'w_lru_o': (_rows(LANE), (LANE, D), BF16),
    'w_uq': (_cols(LANE), (256, LANE), F32),
    'w_ukv': (_cols(LANE), (128, LANE), F32),
    'w_mla_o': (_cols(LANE), (512, LANE), BF16),
    'w_dil_o': (_cols(LANE), (512, LANE), BF16),
    'w_out': (_rows(LANE), (LANE, D), BF16),
}
SLICED_AXIS = {'w_in': 1, 'conv_w': 1, 'w_lru_o': 0, 'w_uq': 1, 'w_ukv': 1, 'w_mla_o': 1, 'w_dil_o': 1, 'w_out': 0}


PACK_ROWS = 64


def _packed_rows(shapes):
    n = sum(int(np.prod(s)) for s in shapes)
    return -(-n // (PACK_ROWS * LANE)) * PACK_ROWS


def _sum8(buf, name):
    ns, R, C = buf.shape
    tr = R
    while tr * C * 4 * ns > (1 << 22) and tr % 32 == 0:
        tr //= 2

    def body(b_ref, o_ref):
        acc = b_ref[0].astype(F32)
        for s in range(1, ns):
            acc = acc + b_ref[s].astype(F32)
        o_ref[...] = acc

    return pl.pallas_call(
        body, name=name, grid=(R // tr,), in_specs=[pl.BlockSpec((ns, tr, C), lambda i: (0, i, 0))],
        out_specs=pl.BlockSpec((tr, C), lambda i: (i, 0)), out_shape=jax.ShapeDtypeStruct((R, C), F32), compiler_params=_cp(),
    )(buf)


def _pack(arrs, dtype, lead):
    flat = [a.astype(dtype).reshape(a.shape[:lead] + (-1,)) for a in arrs]
    cat = jnp.concatenate(flat, axis=-1)
    n = cat.shape[-1]
    unit = PACK_ROWS * LANE
    pad = (-n) % unit
    if pad:
        cat = jnp.pad(cat, [(0, 0)] * lead + [(0, pad)])
    return cat.reshape(cat.shape[:lead] + ((n + pad) // LANE, LANE))


def _unpack(buf, shapes, lead):
    flat = buf.reshape(buf.shape[:lead] + (-1,))
    out, off = [], 0
    for shp in shapes:
        n = int(np.prod(shp))
        out.append(flat[..., off:off + n].reshape(buf.shape[:lead] + tuple(shp)))
        off += n
    return out


def _adamw(w, g, m, v, name):
    layers, rows, cols = w.shape
    tr = rows
    while tr * cols * 4 > (3 << 19) and tr % 16 == 0:
        tr //= 2
    c1 = 1.0 - ADAM_B1 ** ADAM_STEP
    c2 = 1.0 - ADAM_B2 ** ADAM_STEP

    def body(w_ref, g_ref, m_ref, v_ref, d_ref, mo_ref, vo_ref):
        gv = g_ref[...]
        mn = ADAM_B1 * m_ref[...] + (1.0 - ADAM_B1) * gv
        vn = ADAM_B2 * v_ref[...] + (1.0 - ADAM_B2) * (gv * gv)
        mo_ref[...] = mn
        vo_ref[...] = vn
        d_ref[...] = -ADAM_LR * ((mn / c1) / (jnp.sqrt(vn / c2) + ADAM_EPS) + ADAM_WD * w_ref[...])

    spec = pl.BlockSpec((1, tr, cols), lambda l, i: (l, i, 0))
    return pl.pallas_call(
        body, name=name, grid=(layers, rows // tr), in_specs=[spec] * 4, out_specs=[spec] * 3,
        out_shape=[jax.ShapeDtypeStruct((layers, rows, cols), F32)] * 3, compiler_params=_cp(),
    )(w, g, m, v)


IN_NAMES = ['x', 'positions', 'norm_g', 'w_in', 'conv_w', 'conv_b', 'w_gate_x', 'b_gate_x', 'w_gate_a', 'b_gate_a', 'lru_lambda', 'w_lru_o',
            'cq_norm_g', 'ckv_norm_g', 'w_uq', 'w_ukv', 'mla_q_norm_g', 'mla_k_norm_g', 'w_mla_o', 'dil_q_norm_g', 'dil_k_norm_g', 'w_dil_o',
            'b_merge', 'w_out']
WEIGHTS = IN_NAMES[2:]
REPLICATED = [n for n in WEIGHTS if n not in SCATTER]
GATE_WEIGHTS = ('w_gate_x', 'w_gate_a')

_KR0 = C_KR * LANE


GATHERED = ['w_in', 'w_lru_o', 'w_uq', 'w_ukv', 'w_mla_o', 'w_dil_o', 'w_out', 'conv_w']


def _local_weights(wd, me):
    loc = {n: wd[n].astype(BF16) for n in GATHERED[:-1]}
    loc['w_in'] = _to_window(loc['w_in'], me)
    loc['w_uq'] = jnp.pad(loc['w_uq'], ((0, 0), (0, 0), (0, LANE - MLA_QK)))
    loc['conv_w'] = wd['conv_w']
    return [[loc[n][l] for n in GATHERED] for l in range(DEPTH)]


def _layer_weights(gathered, rep, l):
    gw = dict(zip(GATHERED, gathered))
    by_rows = lambda a: a.reshape(-1, a.shape[-1])
    by_cols = lambda a: jnp.swapaxes(a, 0, 1).reshape(a.shape[1], -1)
    ukv = jnp.swapaxes(gw['w_ukv'], 0, 1)
    g96 = lambda a: jnp.pad(a[l].reshape(1, MLA_QK), ((0, 0), (0, LANE - MLA_QK)))
    g64 = lambda a: jnp.tile(a[l].reshape(1, DIL_HD), (1, 2))
    return dict(
        norm_g=rep['norm_g'][l].reshape(1, D), w_in=_assemble_w_in(gw['w_in']),
        conv_w=by_cols(gw['conv_w']), conv_b=rep['conv_b'][l].reshape(1, D),
        w_gx=rep['w_gate_x'][l].astype(BF16), b_gx=rep['b_gate_x'][l].reshape(8, 1, LANE),
        w_ga=rep['w_gate_a'][l].astype(BF16), b_ga=rep['b_gate_a'][l].reshape(8, 1, LANE),
        lam=rep['lru_lambda'][l].reshape(1, D),
        w_lru_o=by_rows(gw['w_lru_o']), w_mla_o=by_cols(gw['w_mla_o']), w_dil_o=by_cols(gw['w_dil_o']), w_out=by_rows(gw['w_out']),
        g_cq=rep['cq_norm_g'][l].reshape(1, 256), g_ckv=rep['ckv_norm_g'][l].reshape(1, 128),
        w_uq=by_cols(gw['w_uq']), w_uk=jnp.pad(ukv[:, :, :64], ((0, 0), (0, 0), (0, 64))).reshape(128, 1024),
        w_uv=ukv[:, :, 64:].reshape(128, 512),
        g_mq=g96(rep['mla_q_norm_g']), g_mk=g96(rep['mla_k_norm_g']), g_dq=g64(rep['dil_q_norm_g']), g_dk=g64(rep['dil_k_norm_g']),
        b_merge=rep['b_merge'][l].reshape(1, 3 * D),
    )


def _sharded_grads(g):
    uk = g['w_uk'].reshape(128, 8, 128)[:, :, :64]
    uv = g['w_uv'].reshape(128, 8, 64)
    d = {'w_in': g['w_in'], 'conv_w': g['conv_w'], 'w_lru_o': g['w_lru_o'], 'w_uq': g['w_uq'],
         'w_ukv': jnp.concatenate([uk, uv], axis=-1).reshape(128, 1024), 'w_mla_o': g['w_mla_o'], 'w_dil_o': g['w_dil_o'],
         'w_out': g['w_out']}
    return [d[n] for n in SCATTER]


def _replicated_grads(g):
    return {
        'conv_b': g['conv_b'].reshape(D),
        'w_gate_x': g['w_gx'], 'b_gate_x': g['b_gx'].reshape(8, LANE), 'w_gate_a': g['w_ga'], 'b_gate_a': g['b_ga'].reshape(8, LANE),
        'lru_lambda': g['lam'].reshape(D), 'cq_norm_g': g['g_cq'].reshape(256), 'ckv_norm_g': g['g_ckv'].reshape(128),
        'mla_q_norm_g': g['g_mq'][0, :MLA_QK], 'mla_k_norm_g': g['g_mk'][0, :MLA_QK],
        'dil_q_norm_g': g['g_dq'][0, :DIL_HD] + g['g_dq'][0, DIL_HD:], 'dil_k_norm_g': g['g_dk'][0, :DIL_HD] + g['g_dk'][0, DIL_HD:],
        'b_merge': g['b_merge'].reshape(3 * D),
    }


def kernel(x, positions, norm_g, w_in, conv_w, conv_b, w_gate_x, b_gate_x, w_gate_a, b_gate_a, lru_lambda, w_lru_o, cq_norm_g, ckv_norm_g, w_uq, w_ukv, mla_q_norm_g, mla_k_norm_g, w_mla_o, dil_q_norm_g, dil_k_norm_g, w_dil_o, b_merge, w_out, loss_target, m_norm_g, m_w_in, m_conv_w, m_conv_b, m_w_gate_x, m_b_gate_x, m_w_gate_a, m_b_gate_a, m_lru_lambda, m_w_lru_o, m_cq_norm_g, m_ckv_norm_g, m_w_uq, m_w_ukv, m_mla_q_norm_g, m_mla_k_norm_g, m_w_mla_o, m_dil_q_norm_g, m_dil_k_norm_g, m_w_dil_o, m_b_merge, m_w_out, v_norm_g, v_w_in, v_conv_w, v_conv_b, v_w_gate_x, v_b_gate_x, v_w_gate_a, v_b_gate_a, v_lru_lambda, v_w_lru_o, v_cq_norm_g, v_ckv_norm_g, v_w_uq, v_w_ukv, v_mla_q_norm_g, v_mla_k_norm_g, v_w_mla_o, v_dil_q_norm_g, v_dil_k_norm_g, v_w_dil_o, v_b_merge, v_w_out):
    args = (x, positions, norm_g, w_in, conv_w, conv_b, w_gate_x, b_gate_x, w_gate_a, b_gate_a, lru_lambda, w_lru_o, cq_norm_g, ckv_norm_g, w_uq, w_ukv, mla_q_norm_g, mla_k_norm_g, w_mla_o, dil_q_norm_g, dil_k_norm_g, w_dil_o, b_merge, w_out)
    moments_m = (m_norm_g, m_w_in, m_conv_w, m_conv_b, m_w_gate_x, m_b_gate_x, m_w_gate_a, m_b_gate_a, m_lru_lambda, m_w_lru_o, m_cq_norm_g, m_ckv_norm_g, m_w_uq, m_w_ukv, m_mla_q_norm_g, m_mla_k_norm_g, m_w_mla_o, m_dil_q_norm_g, m_dil_k_norm_g, m_w_dil_o, m_b_merge, m_w_out)
    moments_v = (v_norm_g, v_w_in, v_conv_w, v_conv_b, v_w_gate_x, v_b_gate_x, v_w_gate_a, v_b_gate_a, v_lru_lambda, v_w_lru_o, v_cq_norm_g, v_ckv_norm_g, v_w_uq, v_w_ukv, v_mla_q_norm_g, v_mla_k_norm_g, v_w_mla_o, v_dil_q_norm_g, v_dil_k_norm_g, v_w_dil_o, v_b_merge, v_w_out)
    a = dict(zip(IN_NAMES, args))
    wd = {n: a[n] for n in WEIGHTS}
    md = dict(zip(WEIGHTS, moments_m))
    vd = dict(zip(WEIGHTS, moments_v))

    me = 4 * lax.axis_index("x") + 2 * lax.axis_index("y") + lax.axis_index("c")

    assert DEPTH == 2
    xs, tabs = x[0], _rope_tables(positions[0])
    whole = [_whole] * len(GATHERED)
    slicers = [SCATTER[n][0] for n in SCATTER]
    grad_slices = [SCATTER[n][1:3] for n in SCATTER]

    local = _local_weights(wd, me)
    w_slices = [(a.shape, a.dtype) for a in local[0]]
    landed0 = _gather_two_level(local[0], "gather_w0")
    flying = _exchange_start(local[1], whole, w_slices, landed0[0], "gather_w1_start")
    rep0 = dict(wd, norm_g=wd['norm_g'] + flying[-1][0, 0])
    w0 = _layer_weights(landed0, rep0, 0)
    x1, saved0 = _layer_fwd(xs, w0, tabs)
    w1 = _layer_weights(_exchange_wait(flying, whole, x1, "gather_w1_wait"), wd, 1)
    x2, saved1 = _layer_fwd(x1, w1, tabs)
    loss, dx2 = _loss_fwd_bwd(x2, loss_target[0])
    loss = loss[0, 0]

    sharded = list(SCATTER)
    nsh = len(sharded)
    small = [n for n in REPLICATED if n not in GATE_WEIGHTS and n != 'norm_g']

    def outgoing(g):
        r = _replicated_grads(g)
        return (_sharded_grads(g) + [_pack([r[n] for n in small], F32, 0)]
                + [r[n].astype(BF16).reshape(8 * LANE, LANE) for n in GATE_WEIGHTS])

    out_slicers = slicers + [_whole] * 3
    out_slices = grad_slices + [((_packed_rows([wd[n].shape[1:] for n in small]), LANE), F32)] + [((8 * LANE, LANE), BF16)] * 2
    dx1, g1 = _layer_bwd(dx2, w1, tabs, saved1)
    flying1 = _exchange_start(outgoing(g1), out_slicers, out_slices, dx1, "scatter_g1_start")
    later = {}

    names = sharded + ['small'] + list(GATE_WEIGHTS)
    sliced = [True] * nsh + [False] * 3
    by_chip = [(lambda ref, q: ref.at[q])] * nsh + [_whole] * 3

    def send_layer0(g):
        later['got1'] = _exchange_wait(flying1, out_slicers, g['w_in'], "scatter_g1_wait")
        mine = outgoing(g)
        came = _pair_exchange(mine, out_slicers, out_slices, sliced, "pair_g0")
        my_side = 2 * jnp.arange(4, dtype=jnp.int32) + lax.axis_index("c")
        halves = []
        for n, a, c in zip(names, mine, came):
            if n in SCATTER:
                first = _win_base(my_side) if n == 'w_in' else my_side
                halves.append(_pair_add(a, c, first.astype(jnp.int32), SLICED_AXIS[n], f"pair_sum_{n}"))
            else:
                halves.append(_add2(a, c[0], f"pair_sum_{n}"))
        later['flying0'] = _exchange_start(halves, by_chip, out_slices, later['got1'][0], "scatter_g0_start", plan=_plan_chips, nslots=4)
        return later['flying0'][-1]

    grad_x, g0 = _layer_bwd(dx1, w0, tabs, saved0, hook=send_layer0, after=flying1[-1])
    sum1 = [_sum8(b, f"sum_{n}_1") for n, b in zip(names, later['got1'])]
    got0 = _exchange_wait(later['flying0'], by_chip, grad_x, "scatter_g0_wait", plan=_plan_chips)
    sum0 = [_sum8(b, f"sum_{n}_0") for n, b in zip(names, got0)]
    norm_part = _pack([jnp.stack([g['norm_g'].reshape(D) for g in (g0, g1)])], F32, 0)
    norm_sum = _sum8(_exchange([norm_part], [_whole], [(norm_part.shape, F32)], "gather_norm_g")[0], "sum_norm_g")

    gsh = {n: jnp.stack([sum0[i], sum1[i]]) for i, n in enumerate(sharded)}
    gsh['w_in'] = _from_window(gsh['w_in'], me)
    gsh['w_uq'] = gsh['w_uq'][:, :, :MLA_QK]
    grep = {'norm_g': _unpack(norm_sum, [wd['norm_g'].shape], 0)[0]}
    per_layer = [_unpack(s[nsh], [wd[n].shape[1:] for n in small], 0) for s in (sum0, sum1)]
    grep.update({n: jnp.stack([per_layer[l][i] for l in range(DEPTH)]) for i, n in enumerate(small)})
    for i, n in enumerate(GATE_WEIGHTS):
        grep[n] = jnp.stack([sum0[nsh + 1 + i], sum1[nsh + 1 + i]]).reshape(wd[n].shape)

    out_g, out_d, out_m, out_v = {}, {}, {}, {}
    vecs = ['norm_g'] + small
    vshapes = [wd[n].shape for n in vecs]
    packed_g = _pack([grep[n] for n in vecs], F32, 0)
    d_, m_, v_ = _adamw(_pack([wd[n] for n in vecs], F32, 0)[None], packed_g[None], _pack([md[n] for n in vecs], F32, 0)[None],
                        _pack([vd[n] for n in vecs], F32, 0)[None], "adamw_vectors")
    for dst, buf in ((out_d, d_), (out_m, m_), (out_v, v_)):
        dst.update(zip(vecs, _unpack(buf[0], vshapes, 0)))
    out_g.update({n: grep[n] for n in vecs})
    gsh.update({n: grep[n] for n in GATE_WEIGHTS})
    for n in sharded + list(GATE_WEIGHTS):
        shp = wd[n].shape
        three = (shp[0], -1, shp[-1])
        d_, m_, v_ = _adamw(wd[n].reshape(three), gsh[n].reshape(three), md[n].reshape(three), vd[n].reshape(three), "adamw_" + n)
        out_g[n], out_d[n], out_m[n], out_v[n] = gsh[n], d_.reshape(shp), m_.reshape(shp), v_.reshape(shp)

    loss = lax.psum(loss, ("x", "y", "c"))
    return (loss, grad_x[None], *[out_g[n] for n in WEIGHTS], *[out_d[n] for n in WEIGHTS], *[out_m[n] for n in WEIGHTS],
            *[out_v[n] for n in WEIGHTS])
```

```python
import functools

import numpy as np
import jax
import jax.numpy as jnp
from jax import lax
from jax.experimental import pallas as pl
from jax.experimental.pallas import tpu as pltpu

F32 = jnp.float32
BF16 = jnp.bfloat16

N_DEV = 8
D = 1024
DEPTH = 2
EPS = 1e-6
ROPE_THETA = 10000.0
LRU_C = 8.0
LANE = 128
SUB = 8
IN_WIDTH = 11168
SHARD_IN = IN_WIDTH // N_DEV

C_LRUX, C_LRUG, C_CQ, C_CKV, C_KR, C_MLAG, C_DQ, C_DK, C_DV, C_DILG, C_MERGE = 0, 8, 16, 18, 19, 20, 24, 36, 48, 60, 64
ZW = 88 * LANE
KR_LANE = 64

MLA_QK = 96
MLA_SCALE = MLA_QK ** -0.5
DIL_HD = 64
DIL_SCALE = DIL_HD ** -0.5
DIL_DILATIONS = (1, 4, 16)
NK = 128

ADAM_LR, ADAM_B1, ADAM_B2, ADAM_EPS, ADAM_WD, ADAM_STEP = 0.001, 0.9, 0.999, 1e-08, 0.01, 10

NEG = -1e30
LOG2E = 1.4426950408889634
VMEM_LIMIT = 48 * 1024 * 1024


def _cp(**kw):
    return pltpu.CompilerParams(vmem_limit_bytes=VMEM_LIMIT, **kw)


def _sig(x):
    return 1.0 / (1.0 + jnp.exp(-x))


def _silu(x):
    return x * _sig(x)


def _dsilu(x):
    s = _sig(x)
    return s * (1.0 + x * (1.0 - s))


def _dot(a, b, dims):
    return lax.dot_general(a, b, (dims, ((), ())), preferred_element_type=F32)


def _nn(a, b):
    return _dot(a, b, ((1,), (0,)))


def _nt(a, b):
    return _dot(a, b, ((1,), (1,)))


def _tn(a, b):
    return _dot(a, b, ((0,), (0,)))


def _rsum(x):
    return jnp.sum(x, axis=-1, keepdims=True)


def _csum(x):
    return jnp.sum(x, axis=0, keepdims=True)


def _mm(a, b, *, mode, name, out_dtype=F32, add=None, after=None, tm=1024, tn=1024, tk=1024):
    if mode == "nn":
        (M, K), (K2, N) = a.shape, b.shape
    elif mode == "nt":
        (M, K), (N, K2) = a.shape, b.shape
    else:
        (K, M), (K2, N) = a.shape, b.shape
    assert K == K2
    tm, tn, tk = min(tm, M), min(tn, N), min(tk, K)
    assert M % tm == 0 and N % tn == 0 and K % tk == 0
    nk = K // tk
    fn = {"nn": _nn, "nt": _nt, "tn": _tn}[mode]
    has_add = add is not None

    def body(*refs):
        a_ref, b_ref = refs[0], refs[1]
        add_ref = refs[2] if has_add else None
        o_ref = refs[2 + has_add + (after is not None)]
        part = fn(a_ref[...].astype(BF16), b_ref[...].astype(BF16))

        def fin(acc):
            if has_add:
                acc = acc + add_ref[...]
            o_ref[...] = acc.astype(out_dtype)

        if nk == 1:
            fin(part)
        else:
            acc_ref = refs[-1]
            k = pl.program_id(2)

            @pl.when(k == 0)
            def _():
                acc_ref[...] = part

            @pl.when(k > 0)
            def _():
                acc_ref[...] += part

            @pl.when(k == nk - 1)
            def _():
                fin(acc_ref[...])

    a_spec = pl.BlockSpec((tk, tm), lambda i, j, k: (k, i)) if mode == "tn" else pl.BlockSpec((tm, tk), lambda i, j, k: (i, k))
    b_spec = pl.BlockSpec((tn, tk), lambda i, j, k: (j, k)) if mode == "nt" else pl.BlockSpec((tk, tn), lambda i, j, k: (k, j))
    o_spec = pl.BlockSpec((tm, tn), lambda i, j, k: (i, j))
    in_specs, args = [a_spec, b_spec], [a, b]
    if has_add:
        in_specs.append(o_spec)
        args.append(add)
    if after is not None:
        in_specs.append(pl.BlockSpec(memory_space=pl.ANY))
        args.append(after)
    return pl.pallas_call(
        body, name=name, grid=(M // tm, N // tn, nk), in_specs=in_specs, out_specs=o_spec,
        out_shape=jax.ShapeDtypeStruct((M, N), out_dtype),
        scratch_shapes=[pltpu.VMEM((tm, tn), F32)] if nk > 1 else [],
        compiler_params=_cp(dimension_semantics=("parallel", "parallel", "arbitrary")),
    )(*args)


T_ROW = 512


def _rms_in_fwd(x, g):
    S = x.shape[0]
    T = T_ROW

    def body(x_ref, g_ref, h_ref):
        xv = x_ref[...]
        r = lax.rsqrt(jnp.mean(xv * xv, axis=-1, keepdims=True) + EPS)
        h_ref[...] = (xv * r * g_ref[...]).astype(BF16)

    return pl.pallas_call(
        body, name="rms_in_fwd", grid=(S // T,),
        in_specs=[pl.BlockSpec((T, D), lambda i: (i, 0)), pl.BlockSpec((1, D), lambda i: (0, 0))],
        out_specs=pl.BlockSpec((T, D), lambda i: (i, 0)),
        out_shape=jax.ShapeDtypeStruct((S, D), BF16), compiler_params=_cp(),
    )(x, g)


def _rms_in_bwd(x, g, dh, dres):
    S = x.shape[0]
    T = T_ROW

    def body(x_ref, g_ref, dh_ref, dr_ref, dx_ref, dg_ref):
        i = pl.program_id(0)
        xv = x_ref[...]
        r = lax.rsqrt(jnp.mean(xv * xv, axis=-1, keepdims=True) + EPS)
        xn = xv * r
        dy = dh_ref[...]
        part = _csum(dy * xn)

        @pl.when(i == 0)
        def _():
            dg_ref[...] = part

        @pl.when(i > 0)
        def _():
            dg_ref[...] += part

        dxh = dy * g_ref[...]
        dx_ref[...] = dr_ref[...] + r * (dxh - xn * jnp.mean(dxh * xn, axis=-1, keepdims=True))

    row = pl.BlockSpec((T, D), lambda i: (i, 0))
    vec = pl.BlockSpec((1, D), lambda i: (0, 0))
    return pl.pallas_call(
        body, name="rms_in_bwd", grid=(S // T,), in_specs=[row, vec, row, row], out_specs=[row, vec],
        out_shape=[jax.ShapeDtypeStruct((S, D), F32), jax.ShapeDtypeStruct((1, D), F32)], compiler_params=_cp(),
    )(x, g, dh, dres)


T_LRU = 512


def _neg_expm1(y):
    ser = -y * (1.0 + y * 0.5 * (1.0 + y * (1.0 / 3.0) * (1.0 + y * 0.25 * (1.0 + y * 0.2))))
    return jnp.where(y > -0.03, ser, 1.0 - jnp.exp(y))


def _softplus_neg(lam):
    e = jnp.exp(-jnp.abs(lam))
    l1p = jnp.where(e < 0.01, e * (1.0 - e * (0.5 - e * (1.0 / 3.0 - e * 0.25))), jnp.log(1.0 + e))
    return jnp.maximum(-lam, 0.0) + l1p


def _scan_fwd(a, b, T):
    row = lax.broadcasted_iota(jnp.int32, a.shape, 0)
    d = 1
    while d < T:
        m = row >= d
        b = jnp.where(m, a * pltpu.roll(b, d, 0) + b, b)
        a = jnp.where(m, a * pltpu.roll(a, d, 0), a)
        d *= 2
    return a, b


def _scan_bwd(a, b, T):
    row = lax.broadcasted_iota(jnp.int32, a.shape, 0)
    d = 1
    while d < T:
        m = row < T - d
        b = jnp.where(m, a * pltpu.roll(b, T - d, 0) + b, b)
        a = jnp.where(m, a * pltpu.roll(a, T - d, 0), a)
        d *= 2
    return b


def _lru_common(x, prev, first, cw_ref, cb_ref, wgx_ref, bgx_ref, wga_ref, bga_ref, lam_ref, T):
    row = lax.broadcasted_iota(jnp.int32, x.shape, 0)
    prev = jnp.where(first, 0.0, prev)
    xs = []
    for j in (3, 2, 1):
        pv = jnp.tile(pltpu.roll(prev, j, 0), (T // SUB, 1))
        xs.append(jnp.where(row < j, pv, pltpu.roll(x, j, 0)))
    xs.append(x)
    xc = cb_ref[...] + cw_ref[0:1, :] * xs[0] + cw_ref[1:2, :] * xs[1] + cw_ref[2:3, :] * xs[2] + cw_ref[3:4, :] * xs[3]
    xcb = xc.astype(BF16)
    gx = _sig(_nn(xcb, wgx_ref[0]) + bgx_ref[0])
    ga = _sig(_nn(xcb, wga_ref[0]) + bga_ref[0])
    sp = _softplus_neg(lam_ref[...])
    log_a = -LRU_C * ga * sp
    a = jnp.exp(log_a)
    mult = jnp.sqrt(_neg_expm1(2.0 * log_a))
    return xs, xc, xcb, gx, ga, sp, a, mult


def _lru_specs(T, tmap):
    def at(col0):
        return pl.BlockSpec((T, LANE), lambda n, i: (tmap(i), col0 + n))

    def prev(col0):
        return pl.BlockSpec((SUB, LANE), lambda n, i: (jnp.maximum(tmap(i) * (T // SUB) - 1, 0), col0 + n))

    small = [
        pl.BlockSpec((4, LANE), lambda n, i: (0, n)),
        pl.BlockSpec((1, LANE), lambda n, i: (0, n)),
        pl.BlockSpec((1, LANE, LANE), lambda n, i: (n, 0, 0)),
        pl.BlockSpec((1, 1, LANE), lambda n, i: (n, 0, 0)),
        pl.BlockSpec((1, LANE, LANE), lambda n, i: (n, 0, 0)),
        pl.BlockSpec((1, 1, LANE), lambda n, i: (n, 0, 0)),
        pl.BlockSpec((1, LANE), lambda n, i: (0, n)),
    ]
    return at, prev, small


def _lru_fwd(zp, w):
    S = zp.shape[0]
    T = T_LRU
    at, prev, small = _lru_specs(T, lambda i: i)

    def body(x_ref, xp_ref, g_ref, cw_ref, cb_ref, wgx_ref, bgx_ref, wga_ref, bga_ref, lam_ref, hs_ref, y_ref, carry_ref):
        i = pl.program_id(1)

        @pl.when(i == 0)
        def _():
            carry_ref[...] = jnp.zeros_like(carry_ref)

        x = x_ref[...]
        _, xc, _, gx, _, _, a, mult = _lru_common(x, xp_ref[...], i == 0, cw_ref, cb_ref, wgx_ref, bgx_ref, wga_ref, bga_ref, lam_ref, T)
        A, B = _scan_fwd(a, mult * gx * xc, T)
        h = B + A * carry_ref[SUB - 1:SUB, :]
        hs_ref[...] = h
        carry_ref[...] = hs_ref[T - SUB:T, :]
        y_ref[...] = (h * _silu(g_ref[...])).astype(BF16)

    out = pl.BlockSpec((T, LANE), lambda n, i: (i, n))
    return pl.pallas_call(
        body, name="lru_fwd", grid=(8, S // T),
        in_specs=[at(C_LRUX), prev(C_LRUX), at(C_LRUG)] + small, out_specs=[out, out],
        out_shape=[jax.ShapeDtypeStruct((S, D), F32), jax.ShapeDtypeStruct((S, D), BF16)],
        scratch_shapes=[pltpu.VMEM((SUB, LANE), F32)],
        compiler_params=_cp(dimension_semantics=("parallel", "arbitrary")),
    )(zp, zp, zp, w["conv_w"], w["conv_b"], w["w_gx"], w["b_gx"], w["w_ga"], w["b_ga"], w["lam"])


def _lru_bwd(zp, hs, dy, w, dz):
    S = zp.shape[0]
    T = T_LRU
    nT = S // T
    at, prev, small = _lru_specs(T, lambda i: nT - 1 - i)

    def body(x_ref, xp_ref, g_ref, h_ref, hp_ref, dy_ref, cw_ref, cb_ref, wgx_ref, bgx_ref, wga_ref, bga_ref, lam_ref, dz_in,
             dzx_ref, dcw_ref, dcb_ref, dwgx_ref, dbgx_ref, dwga_ref, dbga_ref, dlam_ref, carry_ref, head_ref):
        del dz_in
        j = pl.program_id(1)
        it = nT - 1 - j

        @pl.when(j == 0)
        def _():
            for r in (carry_ref, head_ref, dcw_ref, dcb_ref, dwgx_ref, dbgx_ref, dwga_ref, dbga_ref, dlam_ref):
                r[...] = jnp.zeros_like(r)

        first = it == 0
        x = x_ref[...]
        xs, xc, xcb, gx, ga, sp, a, mult = _lru_common(x, xp_ref[...], first, cw_ref, cb_ref, wgx_ref, bgx_ref, wga_ref, bga_ref, lam_ref, T)
        row = lax.broadcasted_iota(jnp.int32, x.shape, 0)
        u = gx * xc
        h = h_ref[...]
        hp = jnp.where(first, 0.0, hp_ref[...])
        hm1 = jnp.where(row < 1, jnp.tile(pltpu.roll(hp, 1, 0), (T // SUB, 1)), pltpu.roll(h, 1, 0))
        dho = dy_ref[...] * _silu(g_ref[...])
        gin = jnp.where(row == T - 1, dho + carry_ref[0:1, :], dho)
        abar = jnp.where(row == T - 1, 0.0, pltpu.roll(a, T - 1, 0))
        dh = _scan_bwd(abar, gin, T)
        carry_ref[...] = (a * dh)[0:SUB, :]
        da = dh * hm1
        dmult = dh * u
        du = dh * mult
        dgx = du * xc
        dxc = du * gx
        dlog_a = da * a - dmult * a * a / mult
        dga = dlog_a * (-LRU_C * sp)
        lam = lam_ref[...]
        dlam_ref[...] += _csum(dlog_a * (-LRU_C * ga)) * (-1.0 / (1.0 + jnp.exp(lam)))
        dpa = dga * ga * (1.0 - ga)
        dpx = dgx * gx * (1.0 - gx)
        dpab, dpxb = dpa.astype(BF16), dpx.astype(BF16)
        dxc = dxc + _nt(dpxb, wgx_ref[0]) + _nt(dpab, wga_ref[0])
        dwgx_ref[0] += _tn(xcb, dpxb)
        dwga_ref[0] += _tn(xcb, dpab)
        dbgx_ref[0] += _csum(dpx)
        dbga_ref[0] += _csum(dpa)
        dcb_ref[...] += _csum(dxc)
        for k in range(4):
            dcw_ref[k:k + 1, :] += _csum(dxc * xs[k])
        head = head_ref[...]
        dx = cw_ref[3:4, :] * dxc
        for jj in (1, 2, 3):
            hv = jnp.tile(pltpu.roll(head, SUB - jj, 0), (T // SUB, 1))
            dx = dx + cw_ref[3 - jj:4 - jj, :] * jnp.where(row >= T - jj, hv, pltpu.roll(dxc, T - jj, 0))
        head_ref[...] = dxc[0:SUB, :]
        dzx_ref[...] = dx.astype(BF16)

    def acc(shape, imap):
        return pl.BlockSpec(shape, imap)

    out_specs = [
        pl.BlockSpec((T, LANE), lambda n, i: (nT - 1 - i, C_LRUX + n)),
        acc((4, LANE), lambda n, i: (0, n)), acc((1, LANE), lambda n, i: (0, n)),
        acc((1, LANE, LANE), lambda n, i: (n, 0, 0)), acc((1, 1, LANE), lambda n, i: (n, 0, 0)),
        acc((1, LANE, LANE), lambda n, i: (n, 0, 0)), acc((1, 1, LANE), lambda n, i: (n, 0, 0)),
        acc((1, LANE), lambda n, i: (0, n)),
    ]
    out_shape = [
        jax.ShapeDtypeStruct(dz.shape, BF16),
        jax.ShapeDtypeStruct((4, D), F32), jax.ShapeDtypeStruct((1, D), F32),
        jax.ShapeDtypeStruct((8, LANE, LANE), F32), jax.ShapeDtypeStruct((8, 1, LANE), F32),
        jax.ShapeDtypeStruct((8, LANE, LANE), F32), jax.ShapeDtypeStruct((8, 1, LANE), F32),
        jax.ShapeDtypeStruct((1, D), F32),
    ]
    dyspec = pl.BlockSpec((T, LANE), lambda n, i: (nT - 1 - i, n))
    hprev = pl.BlockSpec((SUB, LANE), lambda n, i: (jnp.maximum((nT - 1 - i) * (T // SUB) - 1, 0), n))
    return pl.pallas_call(
        body, name="lru_bwd", grid=(8, nT),
        in_specs=[at(C_LRUX), prev(C_LRUX), at(C_LRUG), dyspec, hprev, dyspec] + small + [pl.BlockSpec(memory_space=pl.ANY)],
        out_specs=out_specs, out_shape=out_shape,
        scratch_shapes=[pltpu.VMEM((SUB, LANE), F32), pltpu.VMEM((SUB, LANE), F32)],
        input_output_aliases={13: 0},
        compiler_params=_cp(dimension_semantics=("parallel", "arbitrary")),
    )(zp, zp, zp, hs, hs, dy, w["conv_w"], w["conv_b"], w["w_gx"], w["b_gx"], w["w_ga"], w["b_ga"], w["lam"], dz)


def _lru_gate_bwd(zp, hs, dy, dz):
    S = zp.shape[0]
    T = T_ROW

    def body(g_ref, h_ref, dy_ref, dz_in, o_ref):
        del dz_in
        o_ref[...] = (dy_ref[...] * h_ref[...] * _dsilu(g_ref[...])).astype(BF16)

    row = pl.BlockSpec((T, D), lambda i: (i, 0))
    zc = pl.BlockSpec((T, D), lambda i: (i, C_LRUG // 8))
    return pl.pallas_call(
        body, name="lru_gate_bwd", grid=(S // T,), in_specs=[zc, row, row, pl.BlockSpec(memory_space=pl.ANY)], out_specs=zc,
        out_shape=jax.ShapeDtypeStruct(dz.shape, BF16), input_output_aliases={3: 0}, compiler_params=_cp(),
    )(zp, hs, dy, dz)


def _rope_tables(pos):
    pf = pos.astype(F32)[:, None]

    def cs(d):
        inv = ROPE_THETA ** (-jnp.arange(0, d, 2, dtype=F32) / d)
        ang = pf * inv
        return jnp.cos(ang), jnp.sin(ang)

    S = pos.shape[0]
    c, s = cs(32)
    one, zero = jnp.ones((S, 64), F32), jnp.zeros((S, 16), F32)
    z32, z64 = jnp.zeros((S, 32), F32), jnp.zeros((S, 64), F32)
    mla = (jnp.concatenate([one, c, c, jnp.ones((S, 32), F32)], 1),
           jnp.concatenate([z64, zero, s, z32], 1),
           jnp.concatenate([z64, -s, zero, z32], 1))
    c, s = cs(64)
    dil = (jnp.concatenate([c, c, c, c], 1),
           jnp.concatenate([z32, s, z32, s], 1),
           jnp.concatenate([-s, z32, -s, z32], 1))
    return mla, dil


def _rope(x, C, S1, S2, sh):
    return x * C + pltpu.roll(x, sh, 1) * S1 + pltpu.roll(x, LANE - sh, 1) * S2


def _rope_t(dy, C, S1, S2, sh):
    return dy * C + pltpu.roll(dy * S1, LANE - sh, 1) + pltpu.roll(dy * S2, sh, 1)


def _lane(shape):
    return lax.broadcasted_iota(jnp.int32, shape, 1)


T_MLA = 256
TA = 512


def _zcol(T, width, col_lanes):
    assert (col_lanes * LANE) % width == 0
    return pl.BlockSpec((T, width), lambda i: (i, col_lanes * LANE // width))


def _full(shape):
    return pl.BlockSpec(shape, lambda *_: (0,) * len(shape))


def _mla_pre_fwd(zp, w, tab):
    S = zp.shape[0]
    T = T_MLA

    def body(cq_ref, ckv_ref, kr_ref, gcq_ref, gckv_ref, wuq_ref, wuk_ref, wuv_ref, gq_ref, gk_ref, C_ref, S1_ref, S2_ref,
             q_ref, k_ref, v_ref):
        cq = cq_ref[...]
        cqn = (cq * lax.rsqrt(jnp.mean(cq * cq, axis=-1, keepdims=True) + EPS) * gcq_ref[...]).astype(BF16)
        ckv = ckv_ref[...]
        ckvn = (ckv * lax.rsqrt(jnp.mean(ckv * ckv, axis=-1, keepdims=True) + EPS) * gckv_ref[...]).astype(BF16)
        q0 = _nn(cqn, wuq_ref[...])
        k0 = _nn(ckvn, wuk_ref[...])
        krb = kr_ref[...]
        C, S1, S2 = C_ref[...], S1_ref[...], S2_ref[...]
        for h in range(8):
            sl = slice(h * LANE, (h + 1) * LANE)
            xq = q0[:, sl]
            xq = xq * lax.rsqrt(_rsum(xq * xq) * (1.0 / MLA_QK) + EPS) * gq_ref[...]
            q_ref[:, sl] = _rope(xq, C, S1, S2, 16).astype(BF16)
            xk = k0[:, sl] + krb
            xk = xk * lax.rsqrt(_rsum(xk * xk) * (1.0 / MLA_QK) + EPS) * gk_ref[...]
            k_ref[:, sl] = _rope(xk, C, S1, S2, 16).astype(BF16)
        v_ref[...] = _nn(ckvn, wuv_ref[...]).astype(BF16)

    tabspec = pl.BlockSpec((T, LANE), lambda i: (i, 0))
    in_specs = [_zcol(T, 256, C_CQ), _zcol(T, LANE, C_CKV), _zcol(T, LANE, C_KR), _full((1, 256)), _full((1, LANE)),
                _full((256, 1024)), _full((LANE, 1024)), _full((LANE, 512)), _full((1, LANE)), _full((1, LANE)),
                tabspec, tabspec, tabspec]
    return pl.pallas_call(
        body, name="mla_pre_fwd", grid=(S // T,), in_specs=in_specs,
        out_specs=[pl.BlockSpec((T, 1024), lambda i: (i, 0)), pl.BlockSpec((T, 1024), lambda i: (i, 0)), pl.BlockSpec((T, 512), lambda i: (i, 0))],
        out_shape=[jax.ShapeDtypeStruct((S, 1024), BF16), jax.ShapeDtypeStruct((S, 1024), BF16), jax.ShapeDtypeStruct((S, 512), BF16)],
        compiler_params=_cp(),
    )(zp, zp, zp, w["g_cq"], w["g_ckv"], w["w_uq"], w["w_uk"], w["w_uv"], w["g_mq"], w["g_mk"], *tab)


def _mla_attn_fwd(q, k, v, zp):
    S = q.shape[0]
    nq = S // TA

    def body(q_ref, k_ref, v_ref, g_ref, o_ref, lse_ref, y_ref):
        qi = pl.program_id(1)
        lane = _lane((TA, LANE))
        rowi = lax.broadcasted_iota(jnp.int32, (TA, TA), 0)
        coli = lax.broadcasted_iota(jnp.int32, (TA, TA), 1)
        o_tot = jnp.zeros((TA, LANE), F32)
        for hh in range(2):
            cs = slice(hh * LANE, (hh + 1) * LANE)
            hm = (lane < 64) if hh == 0 else (lane >= 64)
            qh = q_ref[:, cs]
            ones_lane = 64 if hh == 0 else 0

            def step(kb, carry, masked, cs=cs, hm=hm, qh=qh, ones_lane=ones_lane):
                m, acc = carry
                off = pl.multiple_of(kb * TA, TA)
                kh = k_ref[pl.ds(off, TA), cs]
                vv = v_ref[pl.ds(off, TA), :]
                vh = jnp.where(hm, vv, jnp.where(lane == ones_lane, jnp.ones_like(vv), jnp.zeros_like(vv)))
                s = _nt(qh, kh) * (MLA_SCALE * LOG2E)
                if masked:
                    s = jnp.where(rowi >= coli, s, NEG)
                m_new = jnp.maximum(m, jnp.max(s, axis=-1, keepdims=True))
                acc = jnp.exp2(m - m_new) * acc + _nn(jnp.exp2(s - m_new).astype(BF16), vh)
                return m_new, acc

            init = (jnp.full((TA, 1), NEG, F32), jnp.zeros((TA, LANE), F32))
            carry = lax.fori_loop(0, qi, lambda kb, c: step(kb, c, False), init)
            m, acc = step(qi, carry, True)
            l = _rsum(jnp.where(lane == ones_lane, acc, 0.0))
            o_tot = o_tot + jnp.where(hm, acc, 0.0) / l
            lse_ref[:, cs] = jnp.broadcast_to(m * (1.0 / LOG2E) + jnp.log(l), (TA, LANE))
        o_ref[...] = o_tot
        y_ref[...] = (o_tot * _silu(g_ref[...])).astype(BF16)

    blk = pl.BlockSpec((TA, LANE), lambda p, i: (i, p))
    return pl.pallas_call(
        body, name="mla_attn_fwd", grid=(4, nq),
        in_specs=[pl.BlockSpec((TA, 256), lambda p, i: (i, p)), pl.BlockSpec((S, 256), lambda p, i: (0, p)),
                  pl.BlockSpec((S, LANE), lambda p, i: (0, p)), pl.BlockSpec((TA, LANE), lambda p, i: (i, C_MLAG + p))],
        out_specs=[blk, pl.BlockSpec((TA, 256), lambda p, i: (i, p)), blk],
        out_shape=[jax.ShapeDtypeStruct((S, 512), F32), jax.ShapeDtypeStruct((S, 1024), F32), jax.ShapeDtypeStruct((S, 512), BF16)],
        compiler_params=_cp(dimension_semantics=("parallel", "arbitrary")),
    )(q, k, v, zp)


def _mla_post_bwd(zp, o, dy, dz):
    S = zp.shape[0]
    T = T_ROW

    def body(g_ref, o_ref, dy_ref, dz_in, dz_ref, do_ref, D_ref):
        del dz_in
        g, o_, dy_ = g_ref[...], o_ref[...], dy_ref[...]
        do = dy_ * _silu(g)
        do_ref[...] = do.astype(BF16)
        dz_ref[...] = (dy_ * o_ * _dsilu(g)).astype(BF16)
        prod = do * o_
        lane = _lane((T, LANE))
        for p in range(4):
            pr = prod[:, p * LANE:(p + 1) * LANE]
            da = _rsum(jnp.where(lane < 64, pr, 0.0))
            db = _rsum(jnp.where(lane >= 64, pr, 0.0))
            D_ref[:, 2 * p * LANE:(2 * p + 1) * LANE] = jnp.broadcast_to(da, (T, LANE))
            D_ref[:, (2 * p + 1) * LANE:(2 * p + 2) * LANE] = jnp.broadcast_to(db, (T, LANE))

    row = pl.BlockSpec((T, 512), lambda i: (i, 0))
    zc = _zcol(T, 512, C_MLAG)
    return pl.pallas_call(
        body, name="mla_post_bwd", grid=(S // T,), in_specs=[zc, row, row, pl.BlockSpec(memory_space=pl.ANY)],
        out_specs=[zc, row, pl.BlockSpec((T, 1024), lambda i: (i, 0))],
        out_shape=[jax.ShapeDtypeStruct(dz.shape, BF16), jax.ShapeDtypeStruct((S, 512), BF16), jax.ShapeDtypeStruct((S, 1024), F32)],
        input_output_aliases={3: 0}, compiler_params=_cp(),
    )(zp, o, dy, dz)


def _mla_attn_bwd(q, k, v, do, lse, Dr):
    S = q.shape[0]
    nq = S // TA

    def body(q_ref, do_ref, lse_ref, D_ref, k_ref, v_ref, dq_ref, dk_ref, dv_ref):
        ki = pl.program_id(1)

        @pl.when(ki == 0)
        def _():
            dq_ref[...] = jnp.zeros_like(dq_ref)

        lane = _lane((TA, LANE))
        rowi = lax.broadcasted_iota(jnp.int32, (TA, TA), 0)
        coli = lax.broadcasted_iota(jnp.int32, (TA, TA), 1)
        dv_tot = jnp.zeros((TA, LANE), F32)
        for hh in range(2):
            cs = slice(hh * LANE, (hh + 1) * LANE)
            hm = (lane < 64) if hh == 0 else (lane >= 64)
            kh = k_ref[:, cs]
            vv = v_ref[...]
            vm = jnp.where(hm, vv, jnp.zeros_like(vv))

            def step(qb, carry, masked, cs=cs, kh=kh, vm=vm):
                dk_acc, dv_acc = carry
                off = pl.multiple_of(qb * TA, TA)
                qh = q_ref[pl.ds(off, TA), cs]
                doh = do_ref[pl.ds(off, TA), :]
                ls = jnp.tile(lse_ref[pl.ds(off, TA), cs], (1, TA // LANE))
                dd = jnp.tile(D_ref[pl.ds(off, TA), cs], (1, TA // LANE))
                s = _nt(qh, kh) * MLA_SCALE
                if masked:
                    s = jnp.where(rowi >= coli, s, NEG)
                p = jnp.exp(s - ls)
                dp = _nt(doh, vm)
                ds = (p * (dp - dd) * MLA_SCALE).astype(BF16)
                dv_acc = dv_acc + _tn(p.astype(BF16), doh)
                dk_acc = dk_acc + _tn(ds, qh)
                dq_ref[pl.ds(off, TA), cs] += _nn(ds, kh)
                return dk_acc, dv_acc

            z = jnp.zeros((TA, LANE), F32)
            carry = step(ki, (z, z), True)
            dk_acc, dv_acc = lax.fori_loop(ki + 1, nq, lambda qb, c: step(qb, c, False), carry)
            dk_ref[:, cs] = dk_acc
            dv_tot = dv_tot + jnp.where(hm, dv_acc, 0.0)
        dv_ref[...] = dv_tot

    pair = pl.BlockSpec((S, 256), lambda p, i: (0, p))
    return pl.pallas_call(
        body, name="mla_attn_bwd", grid=(4, nq),
        in_specs=[pair, pl.BlockSpec((S, LANE), lambda p, i: (0, p)), pair, pair,
                  pl.BlockSpec((TA, 256), lambda p, i: (i, p)), pl.BlockSpec((TA, LANE), lambda p, i: (i, p))],
        out_specs=[pair, pl.BlockSpec((TA, 256), lambda p, i: (i, p)), pl.BlockSpec((TA, LANE), lambda p, i: (i, p))],
        out_shape=[jax.ShapeDtypeStruct((S, 1024), F32), jax.ShapeDtypeStruct((S, 1024), F32), jax.ShapeDtypeStruct((S, 512), F32)],
        compiler_params=_cp(dimension_semantics=("parallel", "arbitrary")),
    )(q, do, lse, Dr, k, v)


def _mla_pre_bwd(zp, dq, dk, dv, w, tab, dz):
    S = zp.shape[0]
    T = T_MLA

    def body(cq_ref, ckv_ref, kr_ref, dq_ref, dk_ref, dv_ref, gcq_ref, gckv_ref, wuq_ref, wuk_ref, wuv_ref, gq_ref, gk_ref,
             C_ref, S1_ref, S2_ref, dz_in, dz_ref, dwuq_ref, dwuk_ref, dwuv_ref, dgcq_ref, dgckv_ref, dgq_ref, dgk_ref):
        del dz_in
        i = pl.program_id(0)

        @pl.when(i == 0)
        def _():
            for r in (dwuq_ref, dwuk_ref, dwuv_ref, dgcq_ref, dgckv_ref, dgq_ref, dgk_ref):
                r[...] = jnp.zeros_like(r)

        cq = cq_ref[...]
        rq = lax.rsqrt(jnp.mean(cq * cq, axis=-1, keepdims=True) + EPS)
        cqh = cq * rq
        cqn = (cqh * gcq_ref[...]).astype(BF16)
        ckv = ckv_ref[...]
        rkv = lax.rsqrt(jnp.mean(ckv * ckv, axis=-1, keepdims=True) + EPS)
        ckvh = ckv * rkv
        ckvn = (ckvh * gckv_ref[...]).astype(BF16)
        q0 = _nn(cqn, wuq_ref[...])
        k0 = _nn(ckvn, wuk_ref[...])
        krb = kr_ref[...]
        C, S1, S2 = C_ref[...], S1_ref[...], S2_ref[...]
        gq, gk = gq_ref[...], gk_ref[...]

        def head_bwd(x, dy, g):
            r = lax.rsqrt(_rsum(x * x) * (1.0 / MLA_QK) + EPS)
            xn = x * r
            dyn = _rope_t(dy, C, S1, S2, 16)
            dxh = dyn * g
            return r * (dxh - xn * _rsum(dxh * xn) * (1.0 / MLA_QK)), _csum(dyn * xn)

        dq0, dk0 = [], []
        dgq_acc = jnp.zeros((1, LANE), F32)
        dgk_acc = jnp.zeros((1, LANE), F32)
        dkr = jnp.zeros((T, LANE), F32)
        for h in range(8):
            sl = slice(h * LANE, (h + 1) * LANE)
            dxq, gq_p = head_bwd(q0[:, sl], dq_ref[:, sl], gq)
            dxk, gk_p = head_bwd(k0[:, sl] + krb, dk_ref[:, sl], gk)
            dq0.append(dxq.astype(BF16))
            dk0.append(dxk.astype(BF16))
            dkr = dkr + dxk
            dgq_acc = dgq_acc + gq_p
            dgk_acc = dgk_acc + gk_p
        dgq_ref[...] += dgq_acc
        dgk_ref[...] += dgk_acc
        dq0 = jnp.concatenate(dq0, axis=1)
        dk0 = jnp.concatenate(dk0, axis=1)
        dvb = dv_ref[...].astype(BF16)
        dwuq_ref[...] += _tn(cqn, dq0)
        dwuk_ref[...] += _tn(ckvn, dk0)
        dwuv_ref[...] += _tn(ckvn, dvb)
        dcqn = _nt(dq0, wuq_ref[...])
        dckvn = _nt(dk0, wuk_ref[...]) + _nt(dvb, wuv_ref[...])
        dgcq_ref[...] += _csum(dcqn * cqh)
        dgckv_ref[...] += _csum(dckvn * ckvh)
        dxh = dcqn * gcq_ref[...]
        dz_ref[:, 0:256] = (rq * (dxh - cqh * jnp.mean(dxh * cqh, axis=-1, keepdims=True))).astype(BF16)
        dxh = dckvn * gckv_ref[...]
        dz_ref[:, 256:384] = (rkv * (dxh - ckvh * jnp.mean(dxh * ckvh, axis=-1, keepdims=True))).astype(BF16)
        lane = _lane((T, LANE))
        dz_ref[:, 384:512] = jnp.where((lane >= KR_LANE) & (lane < KR_LANE + 32), dkr, 0.0).astype(BF16)

    tabspec = pl.BlockSpec((T, LANE), lambda i: (i, 0))
    in_specs = [_zcol(T, 256, C_CQ), _zcol(T, LANE, C_CKV), _zcol(T, LANE, C_KR),
                pl.BlockSpec((T, 1024), lambda i: (i, 0)), pl.BlockSpec((T, 1024), lambda i: (i, 0)), pl.BlockSpec((T, 512), lambda i: (i, 0)),
                _full((1, 256)), _full((1, LANE)), _full((256, 1024)), _full((LANE, 1024)), _full((LANE, 512)), _full((1, LANE)), _full((1, LANE)),
                tabspec, tabspec, tabspec, pl.BlockSpec(memory_space=pl.ANY)]
    out_specs = [_zcol(T, 512, C_CQ), _full((256, 1024)), _full((LANE, 1024)), _full((LANE, 512)), _full((1, 256)), _full((1, LANE)),
                 _full((1, LANE)), _full((1, LANE))]
    out_shape = [jax.ShapeDtypeStruct(dz.shape, BF16), jax.ShapeDtypeStruct((256, 1024), F32), jax.ShapeDtypeStruct((LANE, 1024), F32),
                 jax.ShapeDtypeStruct((LANE, 512), F32), jax.ShapeDtypeStruct((1, 256), F32), jax.ShapeDtypeStruct((1, LANE), F32),
                 jax.ShapeDtypeStruct((1, LANE), F32), jax.ShapeDtypeStruct((1, LANE), F32)]
    return pl.pallas_call(
        body, name="mla_pre_bwd", grid=(S // T,), in_specs=in_specs, out_specs=out_specs, out_shape=out_shape,
        input_output_aliases={16: 0}, compiler_params=_cp(),
    )(zp, zp, zp, dq, dk, dv, w["g_cq"], w["g_ckv"], w["w_uq"], w["w_uk"], w["w_uv"], w["g_mq"], w["g_mk"], *tab, dz)


T_DIL = 256


def _head_stats(x, lane):
    sq = x * x
    sa = _rsum(jnp.where(lane < 64, sq, 0.0))
    sb = _rsum(jnp.where(lane >= 64, sq, 0.0))
    return lax.rsqrt(jnp.where(lane < 64, sa, sb) * (1.0 / DIL_HD) + EPS)


def _head_sum(x, lane):
    sa = _rsum(jnp.where(lane < 64, x, 0.0))
    sb = _rsum(jnp.where(lane >= 64, x, 0.0))
    return jnp.where(lane < 64, sa, sb)


def _head_stats_mxu(x):
    r = lax.broadcasted_iota(jnp.int32, (LANE, LANE), 0)
    c = lax.broadcasted_iota(jnp.int32, (LANE, LANE), 1)
    ones = jnp.where((r < 64) == (c < 64), 1.0, 0.0).astype(F32)
    ss = lax.dot_general(x * x, ones, (((1,), (0,)), ((), ())), precision=lax.Precision.HIGHEST, preferred_element_type=F32)
    return lax.rsqrt(ss * (1.0 / DIL_HD) + EPS)


def _dil_pre_fwd(zp, w, tab):
    S = zp.shape[0]
    T = T_DIL

    def body(q_ref, k_ref, gq_ref, gk_ref, C_ref, S1_ref, S2_ref, qo_ref, ko_ref):
        C, S1, S2 = C_ref[...], S1_ref[...], S2_ref[...]
        for b in range(12):
            sl = slice(b * LANE, (b + 1) * LANE)
            x = q_ref[:, sl]
            qo_ref[:, sl] = _rope(x * _head_stats_mxu(x) * gq_ref[...], C, S1, S2, 32)
            x = k_ref[:, sl]
            ko_ref[:, sl] = _rope(x * _head_stats_mxu(x) * gk_ref[...], C, S1, S2, 32)

    tabspec = pl.BlockSpec((T, LANE), lambda i: (i, 0))
    out = pl.BlockSpec((T, 1536), lambda i: (i, 0))
    return pl.pallas_call(
        body, name="dil_pre_fwd", grid=(S // T,),
        in_specs=[_zcol(T, 1536, C_DQ), _zcol(T, 1536, C_DK), _full((1, LANE)), _full((1, LANE)), tabspec, tabspec, tabspec],
        out_specs=[out, out], out_shape=[jax.ShapeDtypeStruct((S, 1536), F32)] * 2, compiler_params=_cp(),
    )(zp, zp, w["g_dq"], w["g_dk"], *tab)


DIL_ROWS = 2048


def _dil_geometry(g, S):
    d = DIL_DILATIONS[g]
    P = NK * d
    return d, P, DIL_ROWS // P, S // P


def _dil_rows(start, d, blocks=1):
    return pl.ds(pl.multiple_of(start, NK), blocks * NK) if d == 1 else pl.ds(start, blocks * NK, stride=d)


def _dil_specs(g, S, col0):
    _, P, m, nb = _dil_geometry(g, S)
    cur = pl.BlockSpec((DIL_ROWS, LANE), lambda sb, c: (sb, col0 + c))
    prv = pl.BlockSpec((P, LANE), lambda sb, c: (jnp.maximum(sb * m - 1, 0), col0 + c))
    nxt = pl.BlockSpec((P, LANE), lambda sb, c: (jnp.minimum((sb + 1) * m, nb - 1), col0 + c))
    return cur, prv, nxt


def _dil_attn_fwd(q, k, zp, g):
    S = q.shape[0]
    d, P, m, nb = _dil_geometry(g, S)
    R = DIL_ROWS

    def body(q_ref, kc_ref, kp_ref, vc_ref, vp_ref, o_ref, lse_ref, *scr):
        sb = pl.program_id(0)
        if m > 1:
            ks_ref, vs_ref = scr
            ks_ref[0:P, :] = kp_ref[...]
            ks_ref[P:P + R, :] = kc_ref[...]
            vs_ref[0:P, :] = vp_ref[...]
            vs_ref[P:P + R, :] = vc_ref[...]
        lane = _lane((NK, LANE))

        def unit(u, carry):
            j = u // d
            start = j * P + (u - j * d)
            rows = _dil_rows(start, d)
            if m > 1:
                k2, v2 = ks_ref[_dil_rows(start, d, 2), :], vs_ref[_dil_rows(start, d, 2), :]
            else:
                k2 = jnp.concatenate([kp_ref[rows, :], kc_ref[rows, :]], axis=0)
                v2 = jnp.concatenate([vp_ref[rows, :], vc_ref[rows, :]], axis=0)
            k2, v2 = k2.astype(BF16), v2.astype(BF16)
            q_ = q_ref[rows, :].astype(BF16)
            row = lax.broadcasted_iota(jnp.int32, (NK, 2 * NK), 0)
            col = lax.broadcasted_iota(jnp.int32, (NK, 2 * NK), 1)
            band = (col >= row) & (col <= row + NK) & ((col >= NK) | (sb * m + j > 0))
            lane2 = _lane((2 * NK, LANE))
            zb, zv = jnp.zeros_like(q_), jnp.zeros_like(v2)
            o_tot = jnp.zeros((NK, LANE), F32)
            lse_tot = jnp.zeros((NK, LANE), F32)
            for hh in range(2):
                hm = (lane < 64) if hh == 0 else (lane >= 64)
                hm2 = (lane2 < 64) if hh == 0 else (lane2 >= 64)
                s_ = jnp.where(band, _nt(jnp.where(hm, q_, zb), k2) * DIL_SCALE, NEG)
                mx = jnp.max(s_, axis=-1, keepdims=True)
                e = jnp.exp(s_ - mx)
                den = _rsum(e)
                o_tot = o_tot + _nn(e.astype(BF16), jnp.where(hm2, v2, zv)) / den
                lse_tot = jnp.where(hm, mx + jnp.log(den), lse_tot)
            o_ref[rows, :] = o_tot
            lse_ref[rows, :] = lse_tot
            return carry

        lax.fori_loop(0, R // NK, unit, 0, unroll=8)

    qcur, qprv, _ = _dil_specs(g, S, 4 * g)
    vcur, vprv, _ = _dil_specs(g, S, C_DV + 4 * g)
    out = pl.BlockSpec((R, LANE), lambda sb, c: (sb, c))
    return pl.pallas_call(
        body, name=f"dil_attn_fwd{g}", grid=(S // R, 4), in_specs=[qcur, qcur, qprv, vcur, vprv], out_specs=[out, out],
        out_shape=[jax.ShapeDtypeStruct((S, 512), F32)] * 2,
        scratch_shapes=[pltpu.VMEM((P + R, LANE), F32)] * 2 if m > 1 else [], compiler_params=_cp(),
    )(q, k, k, zp, zp)


def _dil_combine(os_, ls_, zp):
    S = zp.shape[0]
    T = T_ROW

    def body(o0, o1, o2, l0, l1, l2, g_ref, oc_ref, L_ref, y_ref):
        a, b, c = l0[...], l1[...], l2[...]
        mx = jnp.maximum(jnp.maximum(a, b), c)
        ea, eb, ec = jnp.exp(a - mx), jnp.exp(b - mx), jnp.exp(c - mx)
        den = ea + eb + ec
        oc = (ea * o0[...] + eb * o1[...] + ec * o2[...]) / den
        oc_ref[...] = oc
        L_ref[...] = mx + jnp.log(den)
        y_ref[...] = (oc * _silu(g_ref[...])).astype(BF16)

    row = pl.BlockSpec((T, 512), lambda i: (i, 0))
    return pl.pallas_call(
        body, name="dil_combine", grid=(S // T,), in_specs=[row] * 6 + [_zcol(T, 512, C_DILG)], out_specs=[row, row, row],
        out_shape=[jax.ShapeDtypeStruct((S, 512), F32), jax.ShapeDtypeStruct((S, 512), F32), jax.ShapeDtypeStruct((S, 512), BF16)],
        compiler_params=_cp(),
    )(*os_, *ls_, zp)


def _dil_comb_bwd(zp, oc, dy, dz):
    S = zp.shape[0]
    T = T_ROW

    def body(g_ref, o_ref, dy_ref, dz_in, dz_ref, do_ref, D_ref):
        del dz_in
        g, o_, dy_ = g_ref[...], o_ref[...], dy_ref[...]
        do = dy_ * _silu(g)
        do_ref[...] = do
        dz_ref[...] = (dy_ * o_ * _dsilu(g)).astype(BF16)
        lane = _lane((T, LANE))
        for p in range(4):
            sl = slice(p * LANE, (p + 1) * LANE)
            D_ref[:, sl] = _head_sum(do[:, sl] * o_[:, sl], lane)

    row = pl.BlockSpec((T, 512), lambda i: (i, 0))
    zc = _zcol(T, 512, C_DILG)
    return pl.pallas_call(
        body, name="dil_comb_bwd", grid=(S // T,), in_specs=[zc, row, row, pl.BlockSpec(memory_space=pl.ANY)], out_specs=[zc, row, row],
        out_shape=[jax.ShapeDtypeStruct(dz.shape, BF16), jax.ShapeDtypeStruct((S, 512), F32), jax.ShapeDtypeStruct((S, 512), F32)],
        input_output_aliases={3: 0}, compiler_params=_cp(),
    )(zp, oc, dy, dz)


def _dil_attn_bwd(q, k, zp, do, L, Dr, g):
    S = q.shape[0]
    d, P, m, nb = _dil_geometry(g, S)
    R = DIL_ROWS
    n_q, n_k = 4, 2

    def body(*refs):
        q_side = refs[0:2 * n_q]
        k_side = refs[2 * n_q:2 * n_q + 2 * n_k]
        dq_ref, dk_ref, dv_ref = refs[2 * n_q + 2 * n_k:2 * n_q + 2 * n_k + 3]
        scr = refs[2 * n_q + 2 * n_k + 3:]
        sb = pl.program_id(0)
        if m > 1:
            for a in range(n_q):
                scr[a][0:R, :] = q_side[2 * a][...]
                scr[a][R:R + P, :] = q_side[2 * a + 1][...]
            for a in range(n_k):
                scr[n_q + a][0:P, :] = k_side[2 * a + 1][...]
                scr[n_q + a][P:P + R, :] = k_side[2 * a][...]
        lane = _lane((NK, LANE))

        def unit(u, carry):
            j = u // d
            start = j * P + (u - j * d)
            rows = _dil_rows(start, d)
            if m > 1:
                rows_b = _dil_rows(start + P, d)
                q2, do2, L2, D2 = [scr[a][_dil_rows(start, d, 2), :] for a in range(n_q)]
                kp, vp = [scr[n_q + a][rows, :] for a in range(n_k)]
                kc, vc = [scr[n_q + a][rows_b, :] for a in range(n_k)]
            else:
                q2, do2, L2, D2 = [jnp.concatenate([q_side[2 * a][rows, :], q_side[2 * a + 1][rows, :]], axis=0) for a in range(n_q)]
                kc, vc = [k_side[2 * a][rows, :] for a in range(n_k)]
                kp, vp = [k_side[2 * a + 1][rows, :] for a in range(n_k)]
            q2, do2 = q2.astype(BF16), do2.astype(BF16)
            kc, kp, vc, vp = kc.astype(BF16), kp.astype(BF16), vc.astype(BF16), vp.astype(BF16)
            n = sb * m + j
            row2 = lax.broadcasted_iota(jnp.int32, (2 * NK, NK), 0)
            col2 = lax.broadcasted_iota(jnp.int32, (2 * NK, NK), 1)
            m2 = ((row2 < NK) & (col2 <= row2)) | ((row2 >= NK) & (col2 >= row2 - NK) & (n < nb - 1))
            row = lax.broadcasted_iota(jnp.int32, (NK, NK), 0)
            col = lax.broadcasted_iota(jnp.int32, (NK, NK), 1)
            mp = (col >= row) & (n > 0)
            lane2 = _lane((2 * NK, LANE))
            zq, zb = jnp.zeros_like(q2), jnp.zeros_like(kc)
            dq_tot = jnp.zeros((NK, LANE), F32)
            dk_tot = jnp.zeros((NK, LANE), F32)
            dv_tot = jnp.zeros((NK, LANE), F32)
            for hh in range(2):
                hm = (lane < 64) if hh == 0 else (lane >= 64)
                hm2 = (lane2 < 64) if hh == 0 else (lane2 >= 64)
                Lb = jnp.where(hm2, L2, pltpu.roll(L2, 64, 1))
                Db = jnp.where(hm2, D2, pltpu.roll(D2, 64, 1))
                qm2 = jnp.where(hm2, q2, zq)
                vcm = jnp.where(hm, vc, zb)
                vpm = jnp.where(hm, vp, zb)
                p2 = jnp.exp(jnp.where(m2, _nt(qm2, kc) * DIL_SCALE, NEG) - Lb)
                ds2 = (p2 * (_nt(do2, vcm) - Db) * DIL_SCALE).astype(BF16)
                dk_tot = dk_tot + _tn(ds2, qm2)
                dv_tot = dv_tot + jnp.where(hm, _tn(p2.astype(BF16), do2), 0.0)
                pp = jnp.exp(jnp.where(mp, _nt(qm2[0:NK], kp) * DIL_SCALE, NEG) - Lb[0:NK])
                dsp = (pp * (_nt(do2[0:NK], vpm) - Db[0:NK]) * DIL_SCALE).astype(BF16)
                dq_tot = dq_tot + jnp.where(hm, _nn(ds2[0:NK], kc) + _nn(dsp, kp), 0.0)
            dq_ref[rows, :] = dq_tot
            dk_ref[rows, :] = dk_tot
            dv_ref[rows, :] = dv_tot
            return carry

        lax.fori_loop(0, R // NK, unit, 0, unroll=8)

    qcur, qprv, qnxt = _dil_specs(g, S, 4 * g)
    vcur, vprv, _ = _dil_specs(g, S, C_DV + 4 * g)
    ocur, _, onxt = _dil_specs(g, S, 0)
    out = pl.BlockSpec((R, LANE), lambda sb, c: (sb, c))
    scratch = [pltpu.VMEM((P + R, LANE), F32)] * (n_q + n_k) if m > 1 else []
    return pl.pallas_call(
        body, name=f"dil_attn_bwd{g}", grid=(S // R, 4),
        in_specs=[qcur, qnxt, ocur, onxt, ocur, onxt, ocur, onxt, qcur, qprv, vcur, vprv],
        out_specs=[out, out, out], out_shape=[jax.ShapeDtypeStruct((S, 512), F32)] * 3, scratch_shapes=scratch, compiler_params=_cp(),
    )(q, q, do, do, L, L, Dr, Dr, k, k, zp, zp)


def _dil_pre_bwd(zp, dys, g, tab, dz, col, name):
    S = zp.shape[0]
    T = T_DIL

    def body(x_ref, dy0_ref, dy1_ref, dy2_ref, g_ref, C_ref, S1_ref, S2_ref, dz_in, dz_ref, dg_ref):
        del dz_in
        i = pl.program_id(0)
        C, S1, S2 = C_ref[...], S1_ref[...], S2_ref[...]
        lane = _lane((T, LANE))
        gv = g_ref[...]
        acc = jnp.zeros((1, LANE), F32)
        for b in range(12):
            sl = slice(b * LANE, (b + 1) * LANE)
            x = x_ref[:, sl]
            r = _head_stats(x, lane)
            xn = x * r
            dy_ref = (dy0_ref, dy1_ref, dy2_ref)[b // 4]
            dyn = _rope_t(dy_ref[:, (b % 4) * LANE:(b % 4 + 1) * LANE], C, S1, S2, 32)
            acc = acc + _csum(dyn * xn)
            dxh = dyn * gv
            dz_ref[:, sl] = (r * (dxh - xn * _head_sum(dxh * xn, lane) * (1.0 / DIL_HD))).astype(BF16)

        @pl.when(i == 0)
        def _():
            dg_ref[...] = acc

        @pl.when(i > 0)
        def _():
            dg_ref[...] += acc

    tabspec = pl.BlockSpec((T, LANE), lambda i: (i, 0))
    zc = _zcol(T, 1536, col)
    grp = pl.BlockSpec((T, 512), lambda i: (i, 0))
    return pl.pallas_call(
        body, name=name, grid=(S // T,),
        in_specs=[zc, grp, grp, grp, _full((1, LANE)), tabspec, tabspec, tabspec, pl.BlockSpec(memory_space=pl.ANY)],
        out_specs=[zc, _full((1, LANE))], out_shape=[jax.ShapeDtypeStruct(dz.shape, BF16), jax.ShapeDtypeStruct((1, LANE), F32)],
        input_output_aliases={8: 0}, compiler_params=_cp(),
    )(zp, *dys, g, *tab, dz)


def _dil_dv_into(dvs, dz):
    S = dz.shape[0]
    T = T_ROW

    def body(s0, s1, s2, dz_in, o_ref):
        del dz_in
        for gi, s in enumerate((s0, s1, s2)):
            o_ref[:, gi * 512:(gi + 1) * 512] = s[...].astype(BF16)

    grp = pl.BlockSpec((T, 512), lambda i: (i, 0))
    return pl.pallas_call(
        body, name="dil_dv", grid=(S // T,), in_specs=[grp, grp, grp, pl.BlockSpec(memory_space=pl.ANY)],
        out_specs=_zcol(T, 1536, C_DV), out_shape=jax.ShapeDtypeStruct(dz.shape, BF16), input_output_aliases={3: 0}, compiler_params=_cp(),
    )(*dvs, dz)


T_MRG = 256


def _merge_fwd(P, zp, b_merge):
    S = zp.shape[0]
    T = T_MRG

    def body(p0, p1, p2, m0, m1, m2, b_ref, o_ref):
        acc = jnp.zeros((T, D), F32)
        for j, (p, m) in enumerate(((p0, m0), (p1, m1), (p2, m2))):
            acc = acc + _sig(m[...] + b_ref[:, j * D:(j + 1) * D]) * p[...]
        o_ref[...] = acc.astype(BF16)

    row = pl.BlockSpec((T, D), lambda i: (i, 0))
    return pl.pallas_call(
        body, name="merge_fwd", grid=(S // T,),
        in_specs=[row, row, row] + [_zcol(T, D, C_MERGE + 8 * j) for j in range(3)] + [_full((1, 3 * D))], out_specs=row,
        out_shape=jax.ShapeDtypeStruct((S, D), BF16), compiler_params=_cp(),
    )(*P, zp, zp, zp, b_merge)


def _merge_bwd(dm, Pj, zp, bj, dz, j):
    S = zp.shape[0]
    T = T_MRG

    def body(dm_ref, p_ref, m_ref, b_ref, dz_in, dz_ref, dp_ref, db_ref):
        del dz_in
        i = pl.program_id(0)
        g = _sig(m_ref[...] + b_ref[...])
        dmv = dm_ref[...]
        dp_ref[...] = (dmv * g).astype(BF16)
        dg = dmv * p_ref[...] * g * (1.0 - g)
        dz_ref[...] = dg.astype(BF16)
        part = _csum(dg)

        @pl.when(i == 0)
        def _():
            db_ref[...] = part

        @pl.when(i > 0)
        def _():
            db_ref[...] += part

    row = pl.BlockSpec((T, D), lambda i: (i, 0))
    zc = _zcol(T, D, C_MERGE + 8 * j)
    return pl.pallas_call(
        body, name=f"merge_bwd{j}", grid=(S // T,), in_specs=[row, row, zc, _full((1, D)), pl.BlockSpec(memory_space=pl.ANY)],
        out_specs=[zc, row, _full((1, D))],
        out_shape=[jax.ShapeDtypeStruct(dz.shape, BF16), jax.ShapeDtypeStruct((S, D), BF16), jax.ShapeDtypeStruct((1, D), F32)],
        input_output_aliases={4: 0}, compiler_params=_cp(),
    )(dm, Pj, zp, bj, dz)


def _loss_fwd_bwd(y, target):
    S = y.shape[0]
    T = T_ROW

    def body(y_ref, t_ref, loss_ref, dy_ref):
        i = pl.program_id(0)
        err = y_ref[...] - t_ref[...]
        dy_ref[...] = err * (1.0 / D)
        part = jnp.sum(err * err, keepdims=True).reshape(1, 1) * (0.5 / D)

        @pl.when(i == 0)
        def _():
            loss_ref[...] = part

        @pl.when(i > 0)
        def _():
            loss_ref[...] += part

    row = pl.BlockSpec((T, D), lambda i: (i, 0))
    return pl.pallas_call(
        body, name="loss", grid=(S // T,), in_specs=[row, row], out_specs=[_full((1, 1)), row],
        out_shape=[jax.ShapeDtypeStruct((1, 1), F32), jax.ShapeDtypeStruct((S, D), F32)], compiler_params=_cp(),
    )(y, target)


def _layer_fwd(x, w, tabs):
    mla_tab, dil_tab = tabs
    S = x.shape[0]
    h = _rms_in_fwd(x, w["norm_g"])
    zp = _mm(h, w["w_in"], mode="nn", name="in_proj")
    hs, y_lru = _lru_fwd(zp, w)
    q, k, v = _mla_pre_fwd(zp, w, mla_tab)
    o_mla, lse, y_mla = _mla_attn_fwd(q, k, v, zp)
    qd, kd = _dil_pre_fwd(zp, w, dil_tab)
    og, lg = zip(*[_dil_attn_fwd(qd, kd, zp, g) for g in range(len(DIL_DILATIONS))])
    oc, L, y_dil = _dil_combine(og, lg, zp)
    P = [_mm(y_lru, w["w_lru_o"], mode="nn", name="lru_out"), _mm(y_mla, w["w_mla_o"], mode="nn", name="mla_out"),
         _mm(y_dil, w["w_dil_o"], mode="nn", name="dil_out")]
    merged = _merge_fwd(P, zp, w["b_merge"])
    x_out = _mm(merged, w["w_out"], mode="nn", name="out_proj", add=x)
    saved = dict(x=x, h=h, zp=zp, hs=hs, y=(y_lru, y_mla, y_dil), q=q, k=k, v=v, o_mla=o_mla, lse=lse, qd=qd, kd=kd, oc=oc, L=L, P=P,
                 merged=merged)
    return x_out, saved


def _layer_bwd(dout, w, tabs, sv, hook=None, after=None):
    mla_tab, dil_tab = tabs
    zp = sv["zp"]
    S = zp.shape[0]
    g = {}
    dm = _mm(dout, w["w_out"], mode="nt", name="d_merged", after=after)
    g["w_out"] = _mm(sv["merged"], dout, mode="tn", name="dw_out", out_dtype=BF16)
    dz = lax.empty((S, ZW), BF16)
    dP, db = [], []
    for j in range(3):
        dz, dpj, dbj = _merge_bwd(dm, sv["P"][j], zp, w["b_merge"][:, j * D:(j + 1) * D], dz, j)
        dP.append(dpj)
        db.append(dbj)
    g["b_merge"] = jnp.concatenate(db, axis=1)
    names = ("w_lru_o", "w_mla_o", "w_dil_o")
    dy = []
    for j in range(3):
        dy.append(_mm(dP[j], w[names[j]], mode="nt", name="dy_" + names[j]))
        g[names[j]] = _mm(sv["y"][j], dP[j], mode="tn", name="d" + names[j], out_dtype=BF16)
    dz = _lru_gate_bwd(zp, sv["hs"], dy[0], dz)
    dz, g["conv_w"], g["conv_b"], g["w_gx"], g["b_gx"], g["w_ga"], g["b_ga"], g["lam"] = _lru_bwd(zp, sv["hs"], dy[0], w, dz)
    dz, do, Dr = _mla_post_bwd(zp, sv["o_mla"], dy[1], dz)
    dq, dk, dv = _mla_attn_bwd(sv["q"], sv["k"], sv["v"], do, sv["lse"], Dr)
    dz, g["w_uq"], g["w_uk"], g["w_uv"], g["g_cq"], g["g_ckv"], g["g_mq"], g["g_mk"] = _mla_pre_bwd(zp, dq, dk, dv, w, mla_tab, dz)
    dz, dod, Dd = _dil_comb_bwd(zp, sv["oc"], dy[2], dz)
    dqs, dks, dvs = zip(*[_dil_attn_bwd(sv["qd"], sv["kd"], zp, dod, sv["L"], Dd, gi) for gi in range(len(DIL_DILATIONS))])
    dz, g["g_dq"] = _dil_pre_bwd(zp, dqs, w["g_dq"], dil_tab, dz, C_DQ, "dil_pre_bwd_q")
    dz, g["g_dk"] = _dil_pre_bwd(zp, dks, w["g_dk"], dil_tab, dz, C_DK, "dil_pre_bwd_k")
    dz = _dil_dv_into(dvs, dz)
    g["w_in"] = _mm(sv["h"], dz, mode="tn", name="dw_in", out_dtype=BF16)
    token = hook(g) if hook is not None else None
    dh = _mm(dz, w["w_in"], mode="nt", name="d_h", after=token, tk=ZW // 4)
    dx, g["norm_g"] = _rms_in_bwd(sv["x"], w["norm_g"], dh, dout)
    return dx, g


def _peers():
    mx, my, mc = lax.axis_index("x"), lax.axis_index("y"), lax.axis_index("c")
    me = 4 * mx + 2 * my + mc
    out = []
    for k in range(1, N_DEV):
        px = 1 - mx if k & 4 else mx
        py = 1 - my if k & 2 else my
        pc = 1 - mc if k & 1 else mc
        out.append(((px, py, pc), 4 * px + 2 * py + pc))
    return me, out


def _whole(ref, p):
    del p
    return ref


def _exchange(srcs, slicers, slices, name):
    n = len(srcs)

    def body(*refs):
        ins, outs = refs[:n], refs[n:2 * n]
        send_sems, recv_sems, local_sems = refs[2 * n:]
        me, peers = _peers()
        mine = [pltpu.make_async_copy(slicers[a](ins[a], me), outs[a].at[me], local_sems.at[a]) for a in range(n)]
        for cp in mine:
            cp.start()
        copies = []
        for k, (peer, pidx) in enumerate(peers):
            for a in range(n):
                cp = pltpu.make_async_remote_copy(
                    src_ref=slicers[a](ins[a], pidx), dst_ref=outs[a].at[me], send_sem=send_sems.at[k * n + a],
                    recv_sem=recv_sems.at[k * n + a], device_id=peer, device_id_type=pl.DeviceIdType.MESH)
                cp.start()
                copies.append(cp)
        for cp in copies + mine:
            cp.wait()

    nsem = (N_DEV - 1) * n
    return pl.pallas_call(
        body, name=name, out_shape=[jax.ShapeDtypeStruct((N_DEV,) + shp, dt) for shp, dt in slices],
        in_specs=[pl.BlockSpec(memory_space=pl.ANY)] * n, out_specs=[pl.BlockSpec(memory_space=pl.ANY)] * n,
        scratch_shapes=[pltpu.SemaphoreType.DMA((nsem,)), pltpu.SemaphoreType.DMA((nsem,)), pltpu.SemaphoreType.DMA((n,))],
        compiler_params=pltpu.CompilerParams(has_side_effects=True),
    )(*srcs)


def _gather_two_level(srcs, name):
    n = len(srcs)

    def body(*refs):
        ins, outs = refs[:n], refs[n:2 * n]
        send_sems, recv_sems, local_sems = refs[2 * n:]
        mx, my, mc = lax.axis_index("x"), lax.axis_index("y"), lax.axis_index("c")
        me, sibling = (mx, my, mc), (mx, my, 1 - mc)
        chips = [(1 - mx, my), (mx, 1 - my), (1 - mx, 1 - my)]
        slot = lambda d: 4 * d[0] + 2 * d[1] + d[2]

        def copy(j, a, block, to, own=False):
            return pltpu.make_async_remote_copy(
                src_ref=ins[a] if own else outs[a].at[slot(block)], dst_ref=outs[a].at[slot(block)],
                send_sem=send_sems.at[j * n + a], recv_sem=recv_sems.at[j * n + a], device_id=to, device_id_type=pl.DeviceIdType.MESH)

        mine = [pltpu.make_async_copy(ins[a], outs[a].at[slot(me)], local_sems.at[a]) for a in range(n)]
        first = [copy(1 + j, a, me, (*chip, mc), own=True) for j, chip in enumerate(chips) for a in range(n)]
        first += [copy(0, a, me, sibling, own=True) for a in range(n)]
        for cp in mine + first:
            cp.start()
        passed = []
        for j, chip in enumerate(chips):
            for a in range(n):
                copy(1 + j, a, (*chip, mc), me).wait_recv()
                cp = copy(4 + j, a, (*chip, mc), sibling)
                cp.start()
                passed.append(cp)
        for a in range(n):
            copy(0, a, sibling, me).wait_recv()
        for j, chip in enumerate(chips):
            for a in range(n):
                copy(4 + j, a, (*chip, 1 - mc), me).wait_recv()
        for cp in first + passed:
            cp.wait_send()
        for cp in mine:
            cp.wait()

    nsem = (N_DEV - 1) * n
    return pl.pallas_call(
        body, name=name, out_shape=[jax.ShapeDtypeStruct((N_DEV,) + a.shape, a.dtype) for a in srcs],
        in_specs=[pl.BlockSpec(memory_space=pl.ANY)] * n, out_specs=[pl.BlockSpec(memory_space=pl.ANY)] * n,
        scratch_shapes=[pltpu.SemaphoreType.DMA((nsem,)), pltpu.SemaphoreType.DMA((nsem,)), pltpu.SemaphoreType.DMA((n,))],
        compiler_params=pltpu.CompilerParams(has_side_effects=True),
    )(*srcs)


_HBM = pl.BlockSpec(memory_space=pltpu.HBM)
_SEM = pl.BlockSpec(memory_space=pltpu.SEMAPHORE)
_DATAFLOW = pltpu.SideEffectType.DATAFLOW_SIDE_EFFECTING


def _plan_chips():
    mx, my, mc = lax.axis_index("x"), lax.axis_index("y"), lax.axis_index("c")
    return 2 * mx + my, [((cx, cy, mc), 2 * cx + cy) for cx, cy in ((1 - mx, my), (mx, 1 - my), (1 - mx, 1 - my))]


def _pair_exchange(srcs, slicers, slices, sliced, name):
    n = len(srcs)
    pieces = [4 if s else 1 for s in sliced]

    def body(*refs):
        ins, outs = refs[:n], refs[n:2 * n]
        send_sems, recv_sems = refs[2 * n:]
        mx, my, mc = lax.axis_index("x"), lax.axis_index("y"), lax.axis_index("c")
        copies = []
        for a in range(n):
            for q in range(pieces[a]):
                i = len(copies)
                copies.append(pltpu.make_async_remote_copy(
                    src_ref=slicers[a](ins[a], 2 * q + 1 - mc) if sliced[a] else ins[a], dst_ref=outs[a].at[q],
                    send_sem=send_sems.at[i], recv_sem=recv_sems.at[i], device_id=(mx, my, 1 - mc), device_id_type=pl.DeviceIdType.MESH))
        for cp in copies:
            cp.start()
        for cp in copies:
            cp.wait()

    return pl.pallas_call(
        body, name=name, out_shape=[jax.ShapeDtypeStruct((p,) + shp, dt) for (shp, dt), p in zip(slices, pieces)],
        in_specs=[pl.BlockSpec(memory_space=pl.ANY)] * n, out_specs=[pl.BlockSpec(memory_space=pl.ANY)] * n,
        scratch_shapes=[pltpu.SemaphoreType.DMA((sum(pieces),)), pltpu.SemaphoreType.DMA((sum(pieces),))],
        compiler_params=pltpu.CompilerParams(has_side_effects=True),
    )(*srcs)


def _pair_add(src, came, first_blk, axis, name):
    _, r, c = came.shape
    nblk = (c if axis == 1 else r) // LANE
    if axis == 1:
        s_spec = pl.BlockSpec((r, LANE), lambda q, j, fb: (0, fb[q] + j))
        o_spec = pl.BlockSpec((1, r, LANE), lambda q, j, fb: (q, 0, j))
    else:
        s_spec = pl.BlockSpec((LANE, c), lambda q, j, fb: (fb[q] + j, 0))
        o_spec = pl.BlockSpec((1, LANE, c), lambda q, j, fb: (q, j, 0))

    def body(fb_ref, x_ref, y_ref, o_ref):
        del fb_ref
        o_ref[0] = (x_ref[...].astype(F32) + y_ref[0].astype(F32)).astype(o_ref.dtype)

    return pl.pallas_call(
        body, name=name, out_shape=jax.ShapeDtypeStruct(came.shape, came.dtype),
        grid_spec=pltpu.PrefetchScalarGridSpec(num_scalar_prefetch=1, grid=(4, nblk), in_specs=[s_spec, o_spec], out_specs=o_spec),
        compiler_params=_cp(),
    )(first_blk, src, came)


def _add2(x, y, name):
    shp = x.shape
    x, y = x.reshape(-1, shp[-1]), y.reshape(-1, shp[-1])
    R, C = x.shape
    tr = R
    while tr * C * 4 > (1 << 21) and tr % 32 == 0:
        tr //= 2

    def body(x_ref, y_ref, o_ref):
        o_ref[...] = (x_ref[...].astype(F32) + y_ref[...].astype(F32)).astype(o_ref.dtype)

    spec = pl.BlockSpec((tr, C), lambda i: (i, 0))
    return pl.pallas_call(body, name=name, grid=(R // tr,), in_specs=[spec, spec], out_specs=spec,
                          out_shape=jax.ShapeDtypeStruct((R, C), x.dtype), compiler_params=_cp())(x, y).reshape(shp)


def _exchange_start(srcs, slicers, slices, after, name, plan=_peers, nslots=N_DEV):
    n = len(srcs)
    nsem = (nslots - 1) * n
    lands = [lax.empty((nslots,) + shp, dt) for shp, dt in slices]

    def body(*refs):
        ins, lands_in = refs[:n], refs[n:2 * n]
        send_sems, recv_sems, local_sems = refs[2 * n + 1], refs[2 * n + 2], refs[2 * n + 3]
        token = refs[-1]
        me, peers = plan()
        for a in range(n):
            pltpu.make_async_copy(slicers[a](ins[a], me), lands_in[a].at[me], local_sems.at[a]).start()
        for k, (peer, pidx) in enumerate(peers):
            for a in range(n):
                pltpu.make_async_remote_copy(
                    src_ref=slicers[a](ins[a], pidx), dst_ref=lands_in[a].at[me], send_sem=send_sems.at[k * n + a],
                    recv_sem=recv_sems.at[k * n + a], device_id=peer, device_id_type=pl.DeviceIdType.MESH).start()
        token[...] = jnp.zeros_like(token)

    hbm = lambda a: pltpu.with_memory_space_constraint(a, pltpu.HBM)
    return pl.pallas_call(
        body, name=name,
        out_shape=(pltpu.SemaphoreType.DMA((nsem,)), pltpu.SemaphoreType.DMA((nsem,)), pltpu.SemaphoreType.DMA((n,)),
                   *[pltpu.HBM(a.shape, a.dtype) for a in srcs], *[pltpu.HBM(a.shape, a.dtype) for a in lands],
                   jax.ShapeDtypeStruct((SUB, LANE), F32)),
        in_specs=[_HBM] * (2 * n) + [pl.BlockSpec(memory_space=pl.ANY)],
        out_specs=(_SEM, _SEM, _SEM, *[_HBM] * (2 * n), pl.BlockSpec(memory_space=pltpu.VMEM)),
        input_output_aliases={i: 3 + i for i in range(2 * n)},
        compiler_params=pltpu.CompilerParams(has_side_effects=_DATAFLOW),
    )(*[hbm(a) for a in srcs], *[hbm(a) for a in lands], after)


def _exchange_wait(started, slicers, after, name, plan=_peers):
    n = (len(started) - 4) // 2
    sems, thru = started[0:3], started[3:3 + 2 * n]

    def body(*refs):
        srcs, lands = refs[:n], refs[n:2 * n]
        send_sems, recv_sems, local_sems = refs[2 * n], refs[2 * n + 1], refs[2 * n + 2]
        me, peers = plan()
        for k, (peer, pidx) in enumerate(peers):
            for a in range(n):
                cp = pltpu.make_async_remote_copy(
                    src_ref=slicers[a](srcs[a], pidx), dst_ref=lands[a].at[me], send_sem=send_sems.at[k * n + a],
                    recv_sem=recv_sems.at[k * n + a], device_id=peer, device_id_type=pl.DeviceIdType.MESH)
                cp.wait_send()
                cp.wait_recv()
        for a in range(n):
            pltpu.make_async_copy(slicers[a](srcs[a], me), lands[a].at[me], local_sems.at[a]).wait()

    outs = pl.pallas_call(
        body, name=name, out_shape=[pltpu.HBM(a.shape, a.dtype) for a in thru],
        in_specs=[_HBM] * (2 * n) + [_SEM, _SEM, _SEM, pl.BlockSpec(memory_space=pl.ANY)], out_specs=[_HBM] * (2 * n),
        input_output_aliases={i: i for i in range(2 * n)}, compiler_params=pltpu.CompilerParams(has_side_effects=_DATAFLOW),
    )(*thru, *sems, after)
    return outs[n:]


WIN = 13 * LANE


def _win_base(s):
    n = s * SHARD_IN
    a0 = n + jnp.where(n >= _KR0, KR_LANE, 0) + jnp.where(n >= _KR0 + 32, 32, 0)
    return jnp.minimum(a0 // LANE, (ZW - WIN) // LANE)


def _win_offsets(s):
    n = s * SHARD_IN + jnp.arange(SHARD_IN)
    o = s * SHARD_IN - _win_base(s) * LANE
    return n, (o, o + KR_LANE, o + LANE - 32)


def _to_window(shard, s):
    _, offs = _win_offsets(s)
    padded = jnp.pad(shard, ((0, 0), (0, 0), (WIN, WIN)))
    a, b, c = [lax.dynamic_slice(padded, (0, 0, WIN - o), shard.shape[:2] + (WIN,)) for o in offs]
    col = (_win_base(s) * LANE + jnp.arange(WIN))[None, None, :]
    zero = jnp.zeros_like(a)
    return jnp.where(col < _KR0, a, jnp.where((col >= _KR0 + KR_LANE) & (col < _KR0 + KR_LANE + 32), b, jnp.where(col >= _KR0 + LANE, c, zero)))


def _from_window(win, s):
    n, offs = _win_offsets(s)
    a, b, c = [lax.dynamic_slice(win, (0, 0, o), win.shape[:2] + (SHARD_IN,)) for o in offs]
    return jnp.where((n < _KR0)[None, None, :], a, jnp.where((n < _KR0 + 32)[None, None, :], b, c))


def _win_base_static(s):
    n = s * SHARD_IN
    a0 = n + (KR_LANE if n >= _KR0 else 0) + (32 if n >= _KR0 + 32 else 0)
    return min(a0 // LANE, (ZW - WIN) // LANE)


def _assemble_w_in(gw):
    tr = 128
    bases = [_win_base_static(s) for s in range(N_DEV)]

    def body(g_ref, o_ref):
        for j in range(ZW // LANE):
            acc = None
            for s in range(N_DEV):
                if bases[s] <= j < bases[s] + WIN // LANE:
                    piece = g_ref[s, :, (j - bases[s]) * LANE:(j - bases[s] + 1) * LANE]
                    acc = piece if acc is None else acc + piece
            o_ref[:, j * LANE:(j + 1) * LANE] = acc

    return pl.pallas_call(
        body, name="assemble_w_in", grid=(D // tr,), in_specs=[pl.BlockSpec((N_DEV, tr, WIN), lambda i: (0, i, 0))],
        out_specs=pl.BlockSpec((tr, ZW), lambda i: (i, 0)), out_shape=jax.ShapeDtypeStruct((D, ZW), gw.dtype), compiler_params=_cp(),
    )(gw)


def _cols(width):
    return lambda ref, p: ref.at[:, pl.ds(pl.multiple_of(p * width, width), width)]


def _rows(height):
    return lambda ref, p: ref.at[pl.ds(pl.multiple_of(p * height, height), height), :]


SCATTER = {
    'w_in': (lambda ref, p: ref.at[:, pl.ds(pl.multiple_of(_win_base(p) * LANE, LANE), WIN)], (D, WIN), BF16),
    'conv_w': (_cols(LANE), (4, LANE), F32),
    'w_lru_o': (_rows(LANE), (LANE, D), BF16),
    'w_uq': (_cols(LANE), (256, LANE), F32),
    'w_ukv': (_cols(LANE), (128, LANE), F32),
    'w_mla_o': (_cols(LANE), (512, LANE), BF16),
    'w_dil_o': (_cols(LANE), (512, LANE), BF16),
    'w_out': (_rows(LANE), (LANE, D), BF16),
}
SLICED_AXIS = {'w_in': 1, 'conv_w': 1, 'w_lru_o': 0, 'w_uq': 1, 'w_ukv': 1, 'w_mla_o': 1, 'w_dil_o': 1, 'w_out': 0}


PACK_ROWS = 64


def _packed_rows(shapes):
    n = sum(int(np.prod(s)) for s in shapes)
    return -(-n // (PACK_ROWS * LANE)) * PACK_ROWS


def _sum8(buf, name):
    ns, R, C = buf.shape
    tr = R
    while tr * C * 4 * ns > (1 << 22) and tr % 32 == 0:
        tr //= 2

    def body(b_ref, o_ref):
        acc = b_ref[0].astype(F32)
        for s in range(1, ns):
            acc = acc + b_ref[s].astype(F32)
        o_ref[...] = acc

    return pl.pallas_call(
        body, name=name, grid=(R // tr,), in_specs=[pl.BlockSpec((ns, tr, C), lambda i: (0, i, 0))],
        out_specs=pl.BlockSpec((tr, C), lambda i: (i, 0)), out_shape=jax.ShapeDtypeStruct((R, C), F32), compiler_params=_cp(),
    )(buf)


def _pack(arrs, dtype, lead):
    flat = [a.astype(dtype).reshape(a.shape[:lead] + (-1,)) for a in arrs]
    cat = jnp.concatenate(flat, axis=-1)
    n = cat.shape[-1]
    unit = PACK_ROWS * LANE
    pad = (-n) % unit
    if pad:
        cat = jnp.pad(cat, [(0, 0)] * lead + [(0, pad)])
    return cat.reshape(cat.shape[:lead] + ((n + pad) // LANE, LANE))


def _unpack(buf, shapes, lead):
    flat = buf.reshape(buf.shape[:lead] + (-1,))
    out, off = [], 0
    for shp in shapes:
        n = int(np.prod(shp))
        out.append(flat[..., off:off + n].reshape(buf.shape[:lead] + tuple(shp)))
        off += n
    return out


def _adamw(w, g, m, v, name):
    layers, rows, cols = w.shape
    tr = rows
    while tr * cols * 4 > (3 << 19) and tr % 16 == 0:
        tr //= 2
    c1 = 1.0 - ADAM_B1 ** ADAM_STEP
    c2 = 1.0 - ADAM_B2 ** ADAM_STEP

    def body(w_ref, g_ref, m_ref, v_ref, d_ref, mo_ref, vo_ref):
        gv = g_ref[...]
        mn = ADAM_B1 * m_ref[...] + (1.0 - ADAM_B1) * gv
        vn = ADAM_B2 * v_ref[...] + (1.0 - ADAM_B2) * (gv * gv)
        mo_ref[...] = mn
        vo_ref[...] = vn
        d_ref[...] = -ADAM_LR * ((mn / c1) / (jnp.sqrt(vn / c2) + ADAM_EPS) + ADAM_WD * w_ref[...])

    spec = pl.BlockSpec((1, tr, cols), lambda l, i: (l, i, 0))
    return pl.pallas_call(
        body, name=name, grid=(layers, rows // tr), in_specs=[spec] * 4, out_specs=[spec] * 3,
        out_shape=[jax.ShapeDtypeStruct((layers, rows, cols), F32)] * 3, compiler_params=_cp(),
    )(w, g, m, v)


IN_NAMES = ['x', 'positions', 'norm_g', 'w_in', 'conv_w', 'conv_b', 'w_gate_x', 'b_gate_x', 'w_gate_a', 'b_gate_a', 'lru_lambda', 'w_lru_o',
            'cq_norm_g', 'ckv_norm_g', 'w_uq', 'w_ukv', 'mla_q_norm_g', 'mla_k_norm_g', 'w_mla_o', 'dil_q_norm_g', 'dil_k_norm_g', 'w_dil_o',
            'b_merge', 'w_out']
WEIGHTS = IN_NAMES[2:]
REPLICATED = [n for n in WEIGHTS if n not in SCATTER]
GATE_WEIGHTS = ('w_gate_x', 'w_gate_a')

_KR0 = C_KR * LANE


GATHERED = ['w_in', 'w_lru_o', 'w_uq', 'w_ukv', 'w_mla_o', 'w_dil_o', 'w_out', 'conv_w']


def _local_weights(wd, me):
    loc = {n: wd[n].astype(BF16) for n in GATHERED[:-1]}
    loc['w_in'] = _to_window(loc['w_in'], me)
    loc['w_uq'] = jnp.pad(loc['w_uq'], ((0, 0), (0, 0), (0, LANE - MLA_QK)))
    loc['conv_w'] = wd['conv_w']
    return [[loc[n][l] for n in GATHERED] for l in range(DEPTH)]


def _layer_weights(gathered, rep, l):
    gw = dict(zip(GATHERED, gathered))
    by_rows = lambda a: a.reshape(-1, a.shape[-1])
    by_cols = lambda a: jnp.swapaxes(a, 0, 1).reshape(a.shape[1], -1)
    ukv = jnp.swapaxes(gw['w_ukv'], 0, 1)
    g96 = lambda a: jnp.pad(a[l].reshape(1, MLA_QK), ((0, 0), (0, LANE - MLA_QK)))
    g64 = lambda a: jnp.tile(a[l].reshape(1, DIL_HD), (1, 2))
    return dict(
        norm_g=rep['norm_g'][l].reshape(1, D), w_in=_assemble_w_in(gw['w_in']),
        conv_w=by_cols(gw['conv_w']), conv_b=rep['conv_b'][l].reshape(1, D),
        w_gx=rep['w_gate_x'][l].astype(BF16), b_gx=rep['b_gate_x'][l].reshape(8, 1, LANE),
        w_ga=rep['w_gate_a'][l].astype(BF16), b_ga=rep['b_gate_a'][l].reshape(8, 1, LANE),
        lam=rep['lru_lambda'][l].reshape(1, D),
        w_lru_o=by_rows(gw['w_lru_o']), w_mla_o=by_cols(gw['w_mla_o']), w_dil_o=by_cols(gw['w_dil_o']), w_out=by_rows(gw['w_out']),
        g_cq=rep['cq_norm_g'][l].reshape(1, 256), g_ckv=rep['ckv_norm_g'][l].reshape(1, 128),
        w_uq=by_cols(gw['w_uq']), w_uk=jnp.pad(ukv[:, :, :64], ((0, 0), (0, 0), (0, 64))).reshape(128, 1024),
        w_uv=ukv[:, :, 64:].reshape(128, 512),
        g_mq=g96(rep['mla_q_norm_g']), g_mk=g96(rep['mla_k_norm_g']), g_dq=g64(rep['dil_q_norm_g']), g_dk=g64(rep['dil_k_norm_g']),
        b_merge=rep['b_merge'][l].reshape(1, 3 * D),
    )


def _sharded_grads(g):
    uk = g['w_uk'].reshape(128, 8, 128)[:, :, :64]
    uv = g['w_uv'].reshape(128, 8, 64)
    d = {'w_in': g['w_in'], 'conv_w': g['conv_w'], 'w_lru_o': g['w_lru_o'], 'w_uq': g['w_uq'],
         'w_ukv': jnp.concatenate([uk, uv], axis=-1).reshape(128, 1024), 'w_mla_o': g['w_mla_o'], 'w_dil_o': g['w_dil_o'],
         'w_out': g['w_out']}
    return [d[n] for n in SCATTER]


def _replicated_grads(g):
    return {
        'conv_b': g['conv_b'].reshape(D),
        'w_gate_x': g['w_gx'], 'b_gate_x': g['b_gx'].reshape(8, LANE), 'w_gate_a': g['w_ga'], 'b_gate_a': g['b_ga'].reshape(8, LANE),
        'lru_lambda': g['lam'].reshape(D), 'cq_norm_g': g['g_cq'].reshape(256), 'ckv_norm_g': g['g_ckv'].reshape(128),
        'mla_q_norm_g': g['g_mq'][0, :MLA_QK], 'mla_k_norm_g': g['g_mk'][0, :MLA_QK],
        'dil_q_norm_g': g['g_dq'][0, :DIL_HD] + g['g_dq'][0, DIL_HD:], 'dil_k_norm_g': g['g_dk'][0, :DIL_HD] + g['g_dk'][0, DIL_HD:],
        'b_merge': g['b_merge'].reshape(3 * D),
    }


def kernel(x, positions, norm_g, w_in, conv_w, conv_b, w_gate_x, b_gate_x, w_gate_a, b_gate_a, lru_lambda, w_lru_o, cq_norm_g, ckv_norm_g, w_uq, w_ukv, mla_q_norm_g, mla_k_norm_g, w_mla_o, dil_q_norm_g, dil_k_norm_g, w_dil_o, b_merge, w_out, loss_target, m_norm_g, m_w_in, m_conv_w, m_conv_b, m_w_gate_x, m_b_gate_x, m_w_gate_a, m_b_gate_a, m_lru_lambda, m_w_lru_o, m_cq_norm_g, m_ckv_norm_g, m_w_uq, m_w_ukv, m_mla_q_norm_g, m_mla_k_norm_g, m_w_mla_o, m_dil_q_norm_g, m_dil_k_norm_g, m_w_dil_o, m_b_merge, m_w_out, v_norm_g, v_w_in, v_conv_w, v_conv_b, v_w_gate_x, v_b_gate_x, v_w_gate_a, v_b_gate_a, v_lru_lambda, v_w_lru_o, v_cq_norm_g, v_ckv_norm_g, v_w_uq, v_w_ukv, v_mla_q_norm_g, v_mla_k_norm_g, v_w_mla_o, v_dil_q_norm_g, v_dil_k_norm_g, v_w_dil_o, v_b_merge, v_w_out):
    args = (x, positions, norm_g, w_in, conv_w, conv_b, w_gate_x, b_gate_x, w_gate_a, b_gate_a, lru_lambda, w_lru_o, cq_norm_g, ckv_norm_g, w_uq, w_ukv, mla_q_norm_g, mla_k_norm_g, w_mla_o, dil_q_norm_g, dil_k_norm_g, w_dil_o, b_merge, w_out)
    moments_m = (m_norm_g, m_w_in, m_conv_w, m_conv_b, m_w_gate_x, m_b_gate_x, m_w_gate_a, m_b_gate_a, m_lru_lambda, m_w_lru_o, m_cq_norm_g, m_ckv_norm_g, m_w_uq, m_w_ukv, m_mla_q_norm_g, m_mla_k_norm_g, m_w_mla_o, m_dil_q_norm_g, m_dil_k_norm_g, m_w_dil_o, m_b_merge, m_w_out)
    moments_v = (v_norm_g, v_w_in, v_conv_w, v_conv_b, v_w_gate_x, v_b_gate_x, v_w_gate_a, v_b_gate_a, v_lru_lambda, v_w_lru_o, v_cq_norm_g, v_ckv_norm_g, v_w_uq, v_w_ukv, v_mla_q_norm_g, v_mla_k_norm_g, v_w_mla_o, v_dil_q_norm_g, v_dil_k_norm_g, v_w_dil_o, v_b_merge, v_w_out)
    a = dict(zip(IN_NAMES, args))
    wd = {n: a[n] for n in WEIGHTS}
    md = dict(zip(WEIGHTS, moments_m))
    vd = dict(zip(WEIGHTS, moments_v))

    me = 4 * lax.axis_index("x") + 2 * lax.axis_index("y") + lax.axis_index("c")

    assert DEPTH == 2
    xs, tabs = x[0], _rope_tables(positions[0])
    whole = [_whole] * len(GATHERED)
    slicers = [SCATTER[n][0] for n in SCATTER]
    grad_slices = [SCATTER[n][1:3] for n in SCATTER]

    local = _local_weights(wd, me)
    w_slices = [(a.shape, a.dtype) for a in local[0]]
    landed0 = _gather_two_level(local[0], "gather_w0")
    flying = _exchange_start(local[1], whole, w_slices, landed0[0], "gather_w1_start")
    rep0 = dict(wd, norm_g=wd['norm_g'] + flying[-1][0, 0])
    w0 = _layer_weights(landed0, rep0, 0)
    x1, saved0 = _layer_fwd(xs, w0, tabs)
    w1 = _layer_weights(_exchange_wait(flying, whole, x1, "gather_w1_wait"), wd, 1)
    x2, saved1 = _layer_fwd(x1, w1, tabs)
    loss, dx2 = _loss_fwd_bwd(x2, loss_target[0])
    loss = loss[0, 0]

    sharded = list(SCATTER)
    nsh = len(sharded)
    small = [n for n in REPLICATED if n not in GATE_WEIGHTS and n != 'norm_g']

    def outgoing(g):
        r = _replicated_grads(g)
        return (_sharded_grads(g) + [_pack([r[n] for n in small], F32, 0)]
                + [r[n].astype(BF16).reshape(8 * LANE, LANE) for n in GATE_WEIGHTS])

    out_slicers = slicers + [_whole] * 3
    out_slices = grad_slices + [((_packed_rows([wd[n].shape[1:] for n in small]), LANE), F32)] + [((8 * LANE, LANE), BF16)] * 2
    dx1, g1 = _layer_bwd(dx2, w1, tabs, saved1)
    flying1 = _exchange_start(outgoing(g1), out_slicers, out_slices, dx1, "scatter_g1_start")
    later = {}

    names = sharded + ['small'] + list(GATE_WEIGHTS)
    sliced = [True] * nsh + [False] * 3
    by_chip = [(lambda ref, q: ref.at[q])] * nsh + [_whole] * 3

    def send_layer0(g):
        later['got1'] = _exchange_wait(flying1, out_slicers, g['w_in'], "scatter_g1_wait")
        mine = outgoing(g)
        came = _pair_exchange(mine, out_slicers, out_slices, sliced, "pair_g0")
        my_side = 2 * jnp.arange(4, dtype=jnp.int32) + lax.axis_index("c")
        halves = []
        for n, a, c in zip(names, mine, came):
            if n in SCATTER:
                first = _win_base(my_side) if n == 'w_in' else my_side
                halves.append(_pair_add(a, c, first.astype(jnp.int32), SLICED_AXIS[n], f"pair_sum_{n}"))
            else:
                halves.append(_add2(a, c[0], f"pair_sum_{n}"))
        later['flying0'] = _exchange_start(halves, by_chip, out_slices, later['got1'][0], "scatter_g0_start", plan=_plan_chips, nslots=4)
        return later['flying0'][-1]

    grad_x, g0 = _layer_bwd(dx1, w0, tabs, saved0, hook=send_layer0, after=flying1[-1])
    sum1 = [_sum8(b, f"sum_{n}_1") for n, b in zip(names, later['got1'])]
    behind = grad_x[:1, :1] + sum1[0][:1, :1]
    got0 = _exchange_wait(later['flying0'], by_chip, behind, "scatter_g0_wait", plan=_plan_chips)
    sum0 = [_sum8(b, f"sum_{n}_0") for n, b in zip(names, got0)]
    norm_part = _pack([jnp.stack([g['norm_g'].reshape(D) for g in (g0, g1)])], F32, 0)
    norm_sum = _sum8(_exchange([norm_part], [_whole], [(norm_part.shape, F32)], "gather_norm_g")[0], "sum_norm_g")

    gsh = {n: jnp.stack([sum0[i], sum1[i]]) for i, n in enumerate(sharded)}
    gsh['w_in'] = _from_window(gsh['w_in'], me)
    gsh['w_uq'] = gsh['w_uq'][:, :, :MLA_QK]
    grep = {'norm_g': _unpack(norm_sum, [wd['norm_g'].shape], 0)[0]}
    per_layer = [_unpack(s[nsh], [wd[n].shape[1:] for n in small], 0) for s in (sum0, sum1)]
    grep.update({n: jnp.stack([per_layer[l][i] for l in range(DEPTH)]) for i, n in enumerate(small)})
    for i, n in enumerate(GATE_WEIGHTS):
        grep[n] = jnp.stack([sum0[nsh + 1 + i], sum1[nsh + 1 + i]]).reshape(wd[n].shape)

    out_g, out_d, out_m, out_v = {}, {}, {}, {}
    vecs = ['norm_g'] + small
    vshapes = [wd[n].shape for n in vecs]
    packed_g = _pack([grep[n] for n in vecs], F32, 0)
    d_, m_, v_ = _adamw(_pack([wd[n] for n in vecs], F32, 0)[None], packed_g[None], _pack([md[n] for n in vecs], F32, 0)[None],
                        _pack([vd[n] for n in vecs], F32, 0)[None], "adamw_vectors")
    for dst, buf in ((out_d, d_), (out_m, m_), (out_v, v_)):
        dst.update(zip(vecs, _unpack(buf[0], vshapes, 0)))
    out_g.update({n: grep[n] for n in vecs})
    gsh.update({n: grep[n] for n in GATE_WEIGHTS})
    for n in sharded + list(GATE_WEIGHTS):
        shp = wd[n].shape
        three = (1, -1, shp[-1])
        d_, m_, v_ = _adamw(wd[n].reshape(three), gsh[n].reshape(three), md[n].reshape(three), vd[n].reshape(three), "adamw_" + n)
        out_g[n], out_d[n], out_m[n], out_v[n] = gsh[n], d_.reshape(shp), m_.reshape(shp), v_.reshape(shp)

    loss = lax.psum(loss, ("x", "y", "c"))
    return (loss, grad_x[None], *[out_g[n] for n in WEIGHTS], *[out_d[n] for n in WEIGHTS], *[out_m[n] for n in WEIGHTS],
            *[out_v[n] for n in WEIGHTS])
```

```python
import functools

import numpy as np
import jax
import jax.numpy as jnp
from jax import lax
from jax.experimental import pallas as pl
from jax.experimental.pallas import tpu as pltpu

F32 = jnp.float32
BF16 = jnp.bfloat16

N_DEV = 8
D = 1024
DEPTH = 2
EPS = 1e-6
ROPE_THETA = 10000.0
LRU_C = 8.0
LANE = 128
SUB = 8
IN_WIDTH = 11168
SHARD_IN = IN_WIDTH // N_DEV

C_LRUX, C_LRUG, C_CQ, C_CKV, C_KR, C_MLAG, C_DQ, C_DK, C_DV, C_DILG, C_MERGE = 0, 8, 16, 18, 19, 20, 24, 36, 48, 60, 64
ZW = 88 * LANE
KR_LANE = 64

MLA_QK = 96
MLA_SCALE = MLA_QK ** -0.5
DIL_HD = 64
DIL_SCALE = DIL_HD ** -0.5
DIL_DILATIONS = (1, 4, 16)
NK = 128

ADAM_LR, ADAM_B1, ADAM_B2, ADAM_EPS, ADAM_WD, ADAM_STEP = 0.001, 0.9, 0.999, 1e-08, 0.01, 10

NEG = -1e30
LOG2E = 1.4426950408889634
VMEM_LIMIT = 48 * 1024 * 1024


def _cp(**kw):
    return pltpu.CompilerParams(vmem_limit_bytes=VMEM_LIMIT, **kw)


def _sig(x):
    return 1.0 / (1.0 + jnp.exp(-x))


def _silu(x):
    return x * _sig(x)


def _dsilu(x):
    s = _sig(x)
    return s * (1.0 + x * (1.0 - s))


def _dot(a, b, dims):
    return lax.dot_general(a, b, (dims, ((), ())), preferred_element_type=F32)


def _nn(a, b):
    return _dot(a, b, ((1,), (0,)))


def _nt(a, b):
    return _dot(a, b, ((1,), (1,)))


def _tn(a, b):
    return _dot(a, b, ((0,), (0,)))


def _rsum(x):
    return jnp.sum(x, axis=-1, keepdims=True)


def _rsum_mxu(x):
    ones = jnp.ones((x.shape[-1], LANE), F32)
    return lax.dot_general(x, ones, (((1,), (0,)), ((), ())), precision=lax.Precision.HIGHEST, preferred_element_type=F32)


def _csum(x):
    return jnp.sum(x, axis=0, keepdims=True)


def _mm(a, b, *, mode, name, out_dtype=F32, add=None, after=None, tm=1024, tn=1024, tk=1024):
    if mode == "nn":
        (M, K), (K2, N) = a.shape, b.shape
    elif mode == "nt":
        (M, K), (N, K2) = a.shape, b.shape
    else:
        (K, M), (K2, N) = a.shape, b.shape
    assert K == K2
    tm, tn, tk = min(tm, M), min(tn, N), min(tk, K)
    assert M % tm == 0 and N % tn == 0 and K % tk == 0
    nk = K // tk
    fn = {"nn": _nn, "nt": _nt, "tn": _tn}[mode]
    has_add = add is not None

    def body(*refs):
        a_ref, b_ref = refs[0], refs[1]
        add_ref = refs[2] if has_add else None
        o_ref = refs[2 + has_add + (after is not None)]
        part = fn(a_ref[...].astype(BF16), b_ref[...].astype(BF16))

        def fin(acc):
            if has_add:
                acc = acc + add_ref[...]
            o_ref[...] = acc.astype(out_dtype)

        if nk == 1:
            fin(part)
        else:
            acc_ref = refs[-1]
            k = pl.program_id(2)

            @pl.when(k == 0)
            def _():
                acc_ref[...] = part

            @pl.when(k > 0)
            def _():
                acc_ref[...] += part

            @pl.when(k == nk - 1)
            def _():
                fin(acc_ref[...])

    a_spec = pl.BlockSpec((tk, tm), lambda i, j, k: (k, i)) if mode == "tn" else pl.BlockSpec((tm, tk), lambda i, j, k: (i, k))
    b_spec = pl.BlockSpec((tn, tk), lambda i, j, k: (j, k)) if mode == "nt" else pl.BlockSpec((tk, tn), lambda i, j, k: (k, j))
    o_spec = pl.BlockSpec((tm, tn), lambda i, j, k: (i, j))
    in_specs, args = [a_spec, b_spec], [a, b]
    if has_add:
        in_specs.append(o_spec)
        args.append(add)
    if after is not None:
        in_specs.append(pl.BlockSpec(memory_space=pl.ANY))
        args.append(after)
    return pl.pallas_call(
        body, name=name, grid=(M // tm, N // tn, nk), in_specs=in_specs, out_specs=o_spec,
        out_shape=jax.ShapeDtypeStruct((M, N), out_dtype),
        scratch_shapes=[pltpu.VMEM((tm, tn), F32)] if nk > 1 else [],
        compiler_params=_cp(dimension_semantics=("parallel", "parallel", "arbitrary")),
    )(*args)


T_ROW = 512


def _rms_in_fwd(x, g):
    S = x.shape[0]
    T = T_ROW

    def body(x_ref, g_ref, h_ref):
        xv = x_ref[...]
        r = lax.rsqrt(jnp.mean(xv * xv, axis=-1, keepdims=True) + EPS)
        h_ref[...] = (xv * r * g_ref[...]).astype(BF16)

    return pl.pallas_call(
        body, name="rms_in_fwd", grid=(S // T,),
        in_specs=[pl.BlockSpec((T, D), lambda i: (i, 0)), pl.BlockSpec((1, D), lambda i: (0, 0))],
        out_specs=pl.BlockSpec((T, D), lambda i: (i, 0)),
        out_shape=jax.ShapeDtypeStruct((S, D), BF16), compiler_params=_cp(),
    )(x, g)


def _rms_in_bwd(x, g, dh, dres):
    S = x.shape[0]
    T = T_ROW

    def body(x_ref, g_ref, dh_ref, dr_ref, dx_ref, dg_ref):
        i = pl.program_id(0)
        xv = x_ref[...]
        r = lax.rsqrt(jnp.mean(xv * xv, axis=-1, keepdims=True) + EPS)
        xn = xv * r
        dy = dh_ref[...]
        part = _csum(dy * xn)

        @pl.when(i == 0)
        def _():
            dg_ref[...] = part

        @pl.when(i > 0)
        def _():
            dg_ref[...] += part

        dxh = dy * g_ref[...]
        dx_ref[...] = dr_ref[...] + r * (dxh - xn * jnp.mean(dxh * xn, axis=-1, keepdims=True))

    row = pl.BlockSpec((T, D), lambda i: (i, 0))
    vec = pl.BlockSpec((1, D), lambda i: (0, 0))
    return pl.pallas_call(
        body, name="rms_in_bwd", grid=(S // T,), in_specs=[row, vec, row, row], out_specs=[row, vec],
        out_shape=[jax.ShapeDtypeStruct((S, D), F32), jax.ShapeDtypeStruct((1, D), F32)], compiler_params=_cp(),
    )(x, g, dh, dres)


T_LRU = 512


def _neg_expm1(y):
    ser = -y * (1.0 + y * 0.5 * (1.0 + y * (1.0 / 3.0) * (1.0 + y * 0.25 * (1.0 + y * 0.2))))
    return jnp.where(y > -0.03, ser, 1.0 - jnp.exp(y))


def _softplus_neg(lam):
    e = jnp.exp(-jnp.abs(lam))
    l1p = jnp.where(e < 0.01, e * (1.0 - e * (0.5 - e * (1.0 / 3.0 - e * 0.25))), jnp.log(1.0 + e))
    return jnp.maximum(-lam, 0.0) + l1p


def _scan_fwd(a, b, T):
    row = lax.broadcasted_iota(jnp.int32, a.shape, 0)
    d = 1
    while d < T:
        m = row >= d
        b = jnp.where(m, a * pltpu.roll(b, d, 0) + b, b)
        a = jnp.where(m, a * pltpu.roll(a, d, 0), a)
        d *= 2
    return a, b


def _scan_bwd(a, b, T):
    row = lax.broadcasted_iota(jnp.int32, a.shape, 0)
    d = 1
    while d < T:
        m = row < T - d
        b = jnp.where(m, a * pltpu.roll(b, T - d, 0) + b, b)
        a = jnp.where(m, a * pltpu.roll(a, T - d, 0), a)
        d *= 2
    return b


def _lru_common(x, prev, first, cw_ref, cb_ref, wgx_ref, bgx_ref, wga_ref, bga_ref, lam_ref, T):
    row = lax.broadcasted_iota(jnp.int32, x.shape, 0)
    prev = jnp.where(first, 0.0, prev)
    xs = []
    for j in (3, 2, 1):
        pv = jnp.tile(pltpu.roll(prev, j, 0), (T // SUB, 1))
        xs.append(jnp.where(row < j, pv, pltpu.roll(x, j, 0)))
    xs.append(x)
    xc = cb_ref[...] + cw_ref[0:1, :] * xs[0] + cw_ref[1:2, :] * xs[1] + cw_ref[2:3, :] * xs[2] + cw_ref[3:4, :] * xs[3]
    xcb = xc.astype(BF16)
    gx = _sig(_nn(xcb, wgx_ref[0]) + bgx_ref[0])
    ga = _sig(_nn(xcb, wga_ref[0]) + bga_ref[0])
    sp = _softplus_neg(lam_ref[...])
    log_a = -LRU_C * ga * sp
    a = jnp.exp(log_a)
    mult = jnp.sqrt(_neg_expm1(2.0 * log_a))
    return xs, xc, xcb, gx, ga, sp, a, mult


def _lru_specs(T, tmap):
    def at(col0):
        return pl.BlockSpec((T, LANE), lambda n, i: (tmap(i), col0 + n))

    def prev(col0):
        return pl.BlockSpec((SUB, LANE), lambda n, i: (jnp.maximum(tmap(i) * (T // SUB) - 1, 0), col0 + n))

    small = [
        pl.BlockSpec((4, LANE), lambda n, i: (0, n)),
        pl.BlockSpec((1, LANE), lambda n, i: (0, n)),
        pl.BlockSpec((1, LANE, LANE), lambda n, i: (n, 0, 0)),
        pl.BlockSpec((1, 1, LANE), lambda n, i: (n, 0, 0)),
        pl.BlockSpec((1, LANE, LANE), lambda n, i: (n, 0, 0)),
        pl.BlockSpec((1, 1, LANE), lambda n, i: (n, 0, 0)),
        pl.BlockSpec((1, LANE), lambda n, i: (0, n)),
    ]
    return at, prev, small


def _lru_fwd(zp, w):
    S = zp.shape[0]
    T = T_LRU
    at, prev, small = _lru_specs(T, lambda i: i)

    def body(x_ref, xp_ref, g_ref, cw_ref, cb_ref, wgx_ref, bgx_ref, wga_ref, bga_ref, lam_ref, hs_ref, y_ref, carry_ref):
        i = pl.program_id(1)

        @pl.when(i == 0)
        def _():
            carry_ref[...] = jnp.zeros_like(carry_ref)

        x = x_ref[...]
        _, xc, _, gx, _, _, a, mult = _lru_common(x, xp_ref[...], i == 0, cw_ref, cb_ref, wgx_ref, bgx_ref, wga_ref, bga_ref, lam_ref, T)
        A, B = _scan_fwd(a, mult * gx * xc, T)
        h = B + A * carry_ref[SUB - 1:SUB, :]
        hs_ref[...] = h
        carry_ref[...] = hs_ref[T - SUB:T, :]
        y_ref[...] = (h * _silu(g_ref[...])).astype(BF16)

    out = pl.BlockSpec((T, LANE), lambda n, i: (i, n))
    return pl.pallas_call(
        body, name="lru_fwd", grid=(8, S // T),
        in_specs=[at(C_LRUX), prev(C_LRUX), at(C_LRUG)] + small, out_specs=[out, out],
        out_shape=[jax.ShapeDtypeStruct((S, D), F32), jax.ShapeDtypeStruct((S, D), BF16)],
        scratch_shapes=[pltpu.VMEM((SUB, LANE), F32)],
        compiler_params=_cp(dimension_semantics=("parallel", "arbitrary")),
    )(zp, zp, zp, w["conv_w"], w["conv_b"], w["w_gx"], w["b_gx"], w["w_ga"], w["b_ga"], w["lam"])


def _lru_bwd(zp, hs, dy, w, dz):
    S = zp.shape[0]
    T = T_LRU
    nT = S // T
    at, prev, small = _lru_specs(T, lambda i: nT - 1 - i)

    def body(x_ref, xp_ref, g_ref, h_ref, hp_ref, dy_ref, cw_ref, cb_ref, wgx_ref, bgx_ref, wga_ref, bga_ref, lam_ref, dz_in,
             dzx_ref, dcw_ref, dcb_ref, dwgx_ref, dbgx_ref, dwga_ref, dbga_ref, dlam_ref, carry_ref, head_ref):
        del dz_in
        j = pl.program_id(1)
        it = nT - 1 - j

        @pl.when(j == 0)
        def _():
            for r in (carry_ref, head_ref, dcw_ref, dcb_ref, dwgx_ref, dbgx_ref, dwga_ref, dbga_ref, dlam_ref):
                r[...] = jnp.zeros_like(r)

        first = it == 0
        x = x_ref[...]
        xs, xc, xcb, gx, ga, sp, a, mult = _lru_common(x, xp_ref[...], first, cw_ref, cb_ref, wgx_ref, bgx_ref, wga_ref, bga_ref, lam_ref, T)
        row = lax.broadcasted_iota(jnp.int32, x.shape, 0)
        u = gx * xc
        h = h_ref[...]
        hp = jnp.where(first, 0.0, hp_ref[...])
        hm1 = jnp.where(row < 1, jnp.tile(pltpu.roll(hp, 1, 0), (T // SUB, 1)), pltpu.roll(h, 1, 0))
        dho = dy_ref[...] * _silu(g_ref[...])
        gin = jnp.where(row == T - 1, dho + carry_ref[0:1, :], dho)
        abar = jnp.where(row == T - 1, 0.0, pltpu.roll(a, T - 1, 0))
        dh = _scan_bwd(abar, gin, T)
        carry_ref[...] = (a * dh)[0:SUB, :]
        da = dh * hm1
        dmult = dh * u
        du = dh * mult
        dgx = du * xc
        dxc = du * gx
        dlog_a = da * a - dmult * a * a / mult
        dga = dlog_a * (-LRU_C * sp)
        lam = lam_ref[...]
        dlam_ref[...] += _csum(dlog_a * (-LRU_C * ga)) * (-1.0 / (1.0 + jnp.exp(lam)))
        dpa = dga * ga * (1.0 - ga)
        dpx = dgx * gx * (1.0 - gx)
        dpab, dpxb = dpa.astype(BF16), dpx.astype(BF16)
        dxc = dxc + _nt(dpxb, wgx_ref[0]) + _nt(dpab, wga_ref[0])
        dwgx_ref[0] += _tn(xcb, dpxb)
        dwga_ref[0] += _tn(xcb, dpab)
        dbgx_ref[0] += _csum(dpx)
        dbga_ref[0] += _csum(dpa)
        dcb_ref[...] += _csum(dxc)
        for k in range(4):
            dcw_ref[k:k + 1, :] += _csum(dxc * xs[k])
        head = head_ref[...]
        dx = cw_ref[3:4, :] * dxc
        for jj in (1, 2, 3):
            hv = jnp.tile(pltpu.roll(head, SUB - jj, 0), (T // SUB, 1))
            dx = dx + cw_ref[3 - jj:4 - jj, :] * jnp.where(row >= T - jj, hv, pltpu.roll(dxc, T - jj, 0))
        head_ref[...] = dxc[0:SUB, :]
        dzx_ref[...] = dx.astype(BF16)

    def acc(shape, imap):
        return pl.BlockSpec(shape, imap)

    out_specs = [
        pl.BlockSpec((T, LANE), lambda n, i: (nT - 1 - i, C_LRUX + n)),
        acc((4, LANE), lambda n, i: (0, n)), acc((1, LANE), lambda n, i: (0, n)),
        acc((1, LANE, LANE), lambda n, i: (n, 0, 0)), acc((1, 1, LANE), lambda n, i: (n, 0, 0)),
        acc((1, LANE, LANE), lambda n, i: (n, 0, 0)), acc((1, 1, LANE), lambda n, i: (n, 0, 0)),
        acc((1, LANE), lambda n, i: (0, n)),
    ]
    out_shape = [
        jax.ShapeDtypeStruct(dz.shape, BF16),
        jax.ShapeDtypeStruct((4, D), F32), jax.ShapeDtypeStruct((1, D), F32),
        jax.ShapeDtypeStruct((8, LANE, LANE), F32), jax.ShapeDtypeStruct((8, 1, LANE), F32),
        jax.ShapeDtypeStruct((8, LANE, LANE), F32), jax.ShapeDtypeStruct((8, 1, LANE), F32),
        jax.ShapeDtypeStruct((1, D), F32),
    ]
    dyspec = pl.BlockSpec((T, LANE), lambda n, i: (nT - 1 - i, n))
    hprev = pl.BlockSpec((SUB, LANE), lambda n, i: (jnp.maximum((nT - 1 - i) * (T // SUB) - 1, 0), n))
    return pl.pallas_call(
        body, name="lru_bwd", grid=(8, nT),
        in_specs=[at(C_LRUX), prev(C_LRUX), at(C_LRUG), dyspec, hprev, dyspec] + small + [pl.BlockSpec(memory_space=pl.ANY)],
        out_specs=out_specs, out_shape=out_shape,
        scratch_shapes=[pltpu.VMEM((SUB, LANE), F32), pltpu.VMEM((SUB, LANE), F32)],
        input_output_aliases={13: 0},
        compiler_params=_cp(dimension_semantics=("parallel", "arbitrary")),
    )(zp, zp, zp, hs, hs, dy, w["conv_w"], w["conv_b"], w["w_gx"], w["b_gx"], w["w_ga"], w["b_ga"], w["lam"], dz)


def _lru_gate_bwd(zp, hs, dy, dz):
    S = zp.shape[0]
    T = T_ROW

    def body(g_ref, h_ref, dy_ref, dz_in, o_ref):
        del dz_in
        o_ref[...] = (dy_ref[...] * h_ref[...] * _dsilu(g_ref[...])).astype(BF16)

    row = pl.BlockSpec((T, D), lambda i: (i, 0))
    zc = pl.BlockSpec((T, D), lambda i: (i, C_LRUG // 8))
    return pl.pallas_call(
        body, name="lru_gate_bwd", grid=(S // T,), in_specs=[zc, row, row, pl.BlockSpec(memory_space=pl.ANY)], out_specs=zc,
        out_shape=jax.ShapeDtypeStruct(dz.shape, BF16), input_output_aliases={3: 0}, compiler_params=_cp(),
    )(zp, hs, dy, dz)


def _rope_tables(pos):
    pf = pos.astype(F32)[:, None]

    def cs(d):
        inv = ROPE_THETA ** (-jnp.arange(0, d, 2, dtype=F32) / d)
        ang = pf * inv
        return jnp.cos(ang), jnp.sin(ang)

    S = pos.shape[0]
    c, s = cs(32)
    one, zero = jnp.ones((S, 64), F32), jnp.zeros((S, 16), F32)
    z32, z64 = jnp.zeros((S, 32), F32), jnp.zeros((S, 64), F32)
    mla = (jnp.concatenate([one, c, c, jnp.ones((S, 32), F32)], 1),
           jnp.concatenate([z64, zero, s, z32], 1),
           jnp.concatenate([z64, -s, zero, z32], 1))
    c, s = cs(64)
    dil = (jnp.concatenate([c, c, c, c], 1),
           jnp.concatenate([z32, s, z32, s], 1),
           jnp.concatenate([-s, z32, -s, z32], 1))
    return mla, dil


def _rope(x, C, S1, S2, sh):
    return x * C + pltpu.roll(x, sh, 1) * S1 + pltpu.roll(x, LANE - sh, 1) * S2


def _rope_t(dy, C, S1, S2, sh):
    return dy * C + pltpu.roll(dy * S1, LANE - sh, 1) + pltpu.roll(dy * S2, sh, 1)


def _lane(shape):
    return lax.broadcasted_iota(jnp.int32, shape, 1)


T_MLA = 256
TA = 512


def _zcol(T, width, col_lanes):
    assert (col_lanes * LANE) % width == 0
    return pl.BlockSpec((T, width), lambda i: (i, col_lanes * LANE // width))


def _full(shape):
    return pl.BlockSpec(shape, lambda *_: (0,) * len(shape))


def _mla_pre_fwd(zp, w, tab):
    S = zp.shape[0]
    T = T_MLA

    def body(cq_ref, ckv_ref, kr_ref, gcq_ref, gckv_ref, wuq_ref, wuk_ref, wuv_ref, gq_ref, gk_ref, C_ref, S1_ref, S2_ref,
             q_ref, k_ref, v_ref):
        cq = cq_ref[...]
        cqn = (cq * lax.rsqrt(jnp.mean(cq * cq, axis=-1, keepdims=True) + EPS) * gcq_ref[...]).astype(BF16)
        ckv = ckv_ref[...]
        ckvn = (ckv * lax.rsqrt(jnp.mean(ckv * ckv, axis=-1, keepdims=True) + EPS) * gckv_ref[...]).astype(BF16)
        q0 = _nn(cqn, wuq_ref[...])
        k0 = _nn(ckvn, wuk_ref[...])
        krb = kr_ref[...]
        C, S1, S2 = C_ref[...], S1_ref[...], S2_ref[...]
        for h in range(8):
            sl = slice(h * LANE, (h + 1) * LANE)
            xq = q0[:, sl]
            xq = xq * lax.rsqrt(_rsum_mxu(xq * xq) * (1.0 / MLA_QK) + EPS) * gq_ref[...]
            q_ref[:, sl] = _rope(xq, C, S1, S2, 16).astype(BF16)
            xk = k0[:, sl] + krb
            xk = xk * lax.rsqrt(_rsum_mxu(xk * xk) * (1.0 / MLA_QK) + EPS) * gk_ref[...]
            k_ref[:, sl] = _rope(xk, C, S1, S2, 16).astype(BF16)
        v_ref[...] = _nn(ckvn, wuv_ref[...]).astype(BF16)

    tabspec = pl.BlockSpec((T, LANE), lambda i: (i, 0))
    in_specs = [_zcol(T, 256, C_CQ), _zcol(T, LANE, C_CKV), _zcol(T, LANE, C_KR), _full((1, 256)), _full((1, LANE)),
                _full((256, 1024)), _full((LANE, 1024)), _full((LANE, 512)), _full((1, LANE)), _full((1, LANE)),
                tabspec, tabspec, tabspec]
    return pl.pallas_call(
        body, name="mla_pre_fwd", grid=(S // T,), in_specs=in_specs,
        out_specs=[pl.BlockSpec((T, 1024), lambda i: (i, 0)), pl.BlockSpec((T, 1024), lambda i: (i, 0)), pl.BlockSpec((T, 512), lambda i: (i, 0))],
        out_shape=[jax.ShapeDtypeStruct((S, 1024), BF16), jax.ShapeDtypeStruct((S, 1024), BF16), jax.ShapeDtypeStruct((S, 512), BF16)],
        compiler_params=_cp(),
    )(zp, zp, zp, w["g_cq"], w["g_ckv"], w["w_uq"], w["w_uk"], w["w_uv"], w["g_mq"], w["g_mk"], *tab)


def _mla_attn_fwd(q, k, v, zp):
    S = q.shape[0]
    nq = S // TA

    def body(q_ref, k_ref, v_ref, g_ref, o_ref, lse_ref, y_ref):
        qi = pl.program_id(1)
        lane = _lane((TA, LANE))
        rowi = lax.broadcasted_iota(jnp.int32, (TA, TA), 0)
        coli = lax.broadcasted_iota(jnp.int32, (TA, TA), 1)
        o_tot = jnp.zeros((TA, LANE), F32)
        for hh in range(2):
            cs = slice(hh * LANE, (hh + 1) * LANE)
            hm = (lane < 64) if hh == 0 else (lane >= 64)
            qh = q_ref[:, cs]
            ones_lane = 64 if hh == 0 else 0

            def step(kb, carry, masked, cs=cs, hm=hm, qh=qh, ones_lane=ones_lane):
                m, acc = carry
                off = pl.multiple_of(kb * TA, TA)
                kh = k_ref[pl.ds(off, TA), cs]
                vv = v_ref[pl.ds(off, TA), :]
                vh = jnp.where(hm, vv, jnp.where(lane == ones_lane, jnp.ones_like(vv), jnp.zeros_like(vv)))
                s = _nt(qh, kh) * (MLA_SCALE * LOG2E)
                if masked:
                    s = jnp.where(rowi >= coli, s, NEG)
                m_new = jnp.maximum(m, jnp.max(s, axis=-1, keepdims=True))
                acc = jnp.exp2(m - m_new) * acc + _nn(jnp.exp2(s - m_new).astype(BF16), vh)
                return m_new, acc

            init = (jnp.full((TA, 1), NEG, F32), jnp.zeros((TA, LANE), F32))
            carry = lax.fori_loop(0, qi, lambda kb, c: step(kb, c, False), init)
            m, acc = step(qi, carry, True)
            l = _rsum(jnp.where(lane == ones_lane, acc, 0.0))
            o_tot = o_tot + jnp.where(hm, acc, 0.0) / l
            lse_ref[:, cs] = jnp.broadcast_to(m * (1.0 / LOG2E) + jnp.log(l), (TA, LANE))
        o_ref[...] = o_tot
        y_ref[...] = (o_tot * _silu(g_ref[...])).astype(BF16)

    blk = pl.BlockSpec((TA, LANE), lambda p, i: (i, p))
    return pl.pallas_call(
        body, name="mla_attn_fwd", grid=(4, nq),
        in_specs=[pl.BlockSpec((TA, 256), lambda p, i: (i, p)), pl.BlockSpec((S, 256), lambda p, i: (0, p)),
                  pl.BlockSpec((S, LANE), lambda p, i: (0, p)), pl.BlockSpec((TA, LANE), lambda p, i: (i, C_MLAG + p))],
        out_specs=[blk, pl.BlockSpec((TA, 256), lambda p, i: (i, p)), blk],
        out_shape=[jax.ShapeDtypeStruct((S, 512), F32), jax.ShapeDtypeStruct((S, 1024), F32), jax.ShapeDtypeStruct((S, 512), BF16)],
        compiler_params=_cp(dimension_semantics=("parallel", "arbitrary")),
    )(q, k, v, zp)


def _mla_post_bwd(zp, o, dy, dz):
    S = zp.shape[0]
    T = T_ROW

    def body(g_ref, o_ref, dy_ref, dz_in, dz_ref, do_ref, D_ref):
        del dz_in
        g, o_, dy_ = g_ref[...], o_ref[...], dy_ref[...]
        do = dy_ * _silu(g)
        do_ref[...] = do.astype(BF16)
        dz_ref[...] = (dy_ * o_ * _dsilu(g)).astype(BF16)
        prod = do * o_
        lane = _lane((T, LANE))
        for p in range(4):
            pr = prod[:, p * LANE:(p + 1) * LANE]
            da = _rsum(jnp.where(lane < 64, pr, 0.0))
            db = _rsum(jnp.where(lane >= 64, pr, 0.0))
            D_ref[:, 2 * p * LANE:(2 * p + 1) * LANE] = jnp.broadcast_to(da, (T, LANE))
            D_ref[:, (2 * p + 1) * LANE:(2 * p + 2) * LANE] = jnp.broadcast_to(db, (T, LANE))

    row = pl.BlockSpec((T, 512), lambda i: (i, 0))
    zc = _zcol(T, 512, C_MLAG)
    return pl.pallas_call(
        body, name="mla_post_bwd", grid=(S // T,), in_specs=[zc, row, row, pl.BlockSpec(memory_space=pl.ANY)],
        out_specs=[zc, row, pl.BlockSpec((T, 1024), lambda i: (i, 0))],
        out_shape=[jax.ShapeDtypeStruct(dz.shape, BF16), jax.ShapeDtypeStruct((S, 512), BF16), jax.ShapeDtypeStruct((S, 1024), F32)],
        input_output_aliases={3: 0}, compiler_params=_cp(),
    )(zp, o, dy, dz)


def _mla_attn_bwd(q, k, v, do, lse, Dr):
    S = q.shape[0]
    nq = S // TA

    def body(q_ref, do_ref, lse_ref, D_ref, k_ref, v_ref, dq_ref, dk_ref, dv_ref):
        ki = pl.program_id(1)

        @pl.when(ki == 0)
        def _():
            dq_ref[...] = jnp.zeros_like(dq_ref)

        lane = _lane((TA, LANE))
        rowi = lax.broadcasted_iota(jnp.int32, (TA, TA), 0)
        coli = lax.broadcasted_iota(jnp.int32, (TA, TA), 1)
        dv_tot = jnp.zeros((TA, LANE), F32)
        for hh in range(2):
            cs = slice(hh * LANE, (hh + 1) * LANE)
            hm = (lane < 64) if hh == 0 else (lane >= 64)
            kh = k_ref[:, cs]
            vv = v_ref[...]
            vm = jnp.where(hm, vv, jnp.zeros_like(vv))

            def step(qb, carry, masked, cs=cs, kh=kh, vm=vm):
                dk_acc, dv_acc = carry
                off = pl.multiple_of(qb * TA, TA)
                qh = q_ref[pl.ds(off, TA), cs]
                doh = do_ref[pl.ds(off, TA), :]
                ls = jnp.tile(lse_ref[pl.ds(off, TA), cs], (1, TA // LANE))
                dd = jnp.tile(D_ref[pl.ds(off, TA), cs], (1, TA // LANE))
                s = _nt(qh, kh) * MLA_SCALE
                if masked:
                    s = jnp.where(rowi >= coli, s, NEG)
                p = jnp.exp(s - ls)
                dp = _nt(doh, vm)
                ds = (p * (dp - dd) * MLA_SCALE).astype(BF16)
                dv_acc = dv_acc + _tn(p.astype(BF16), doh)
                dk_acc = dk_acc + _tn(ds, qh)
                dq_ref[pl.ds(off, TA), cs] += _nn(ds, kh)
                return dk_acc, dv_acc

            z = jnp.zeros((TA, LANE), F32)
            carry = step(ki, (z, z), True)
            dk_acc, dv_acc = lax.fori_loop(ki + 1, nq, lambda qb, c: step(qb, c, False), carry)
            dk_ref[:, cs] = dk_acc
            dv_tot = dv_tot + jnp.where(hm, dv_acc, 0.0)
        dv_ref[...] = dv_tot

    pair = pl.BlockSpec((S, 256), lambda p, i: (0, p))
    return pl.pallas_call(
        body, name="mla_attn_bwd", grid=(4, nq),
        in_specs=[pair, pl.BlockSpec((S, LANE), lambda p, i: (0, p)), pair, pair,
                  pl.BlockSpec((TA, 256), lambda p, i: (i, p)), pl.BlockSpec((TA, LANE), lambda p, i: (i, p))],
        out_specs=[pair, pl.BlockSpec((TA, 256), lambda p, i: (i, p)), pl.BlockSpec((TA, LANE), lambda p, i: (i, p))],
        out_shape=[jax.ShapeDtypeStruct((S, 1024), F32), jax.ShapeDtypeStruct((S, 1024), F32), jax.ShapeDtypeStruct((S, 512), F32)],
        compiler_params=_cp(dimension_semantics=("parallel", "arbitrary")),
    )(q, do, lse, Dr, k, v)


def _mla_pre_bwd(zp, dq, dk, dv, w, tab, dz):
    S = zp.shape[0]
    T = T_MLA

    def body(cq_ref, ckv_ref, kr_ref, dq_ref, dk_ref, dv_ref, gcq_ref, gckv_ref, wuq_ref, wuk_ref, wuv_ref, gq_ref, gk_ref,
             C_ref, S1_ref, S2_ref, dz_in, dz_ref, dwuq_ref, dwuk_ref, dwuv_ref, dgcq_ref, dgckv_ref, dgq_ref, dgk_ref):
        del dz_in
        i = pl.program_id(0)

        @pl.when(i == 0)
        def _():
            for r in (dwuq_ref, dwuk_ref, dwuv_ref, dgcq_ref, dgckv_ref, dgq_ref, dgk_ref):
                r[...] = jnp.zeros_like(r)

        cq = cq_ref[...]
        rq = lax.rsqrt(jnp.mean(cq * cq, axis=-1, keepdims=True) + EPS)
        cqh = cq * rq
        cqn = (cqh * gcq_ref[...]).astype(BF16)
        ckv = ckv_ref[...]
        rkv = lax.rsqrt(jnp.mean(ckv * ckv, axis=-1, keepdims=True) + EPS)
        ckvh = ckv * rkv
        ckvn = (ckvh * gckv_ref[...]).astype(BF16)
        q0 = _nn(cqn, wuq_ref[...])
        k0 = _nn(ckvn, wuk_ref[...])
        krb = kr_ref[...]
        C, S1, S2 = C_ref[...], S1_ref[...], S2_ref[...]
        gq, gk = gq_ref[...], gk_ref[...]

        def head_bwd(x, dy, g):
            r = lax.rsqrt(_rsum_mxu(x * x) * (1.0 / MLA_QK) + EPS)
            xn = x * r
            dyn = _rope_t(dy, C, S1, S2, 16)
            dxh = dyn * g
            return r * (dxh - xn * _rsum_mxu(dxh * xn) * (1.0 / MLA_QK)), _csum(dyn * xn)

        dq0, dk0 = [], []
        dgq_acc = jnp.zeros((1, LANE), F32)
        dgk_acc = jnp.zeros((1, LANE), F32)
        dkr = jnp.zeros((T, LANE), F32)
        for h in range(8):
            sl = slice(h * LANE, (h + 1) * LANE)
            dxq, gq_p = head_bwd(q0[:, sl], dq_ref[:, sl], gq)
            dxk, gk_p = head_bwd(k0[:, sl] + krb, dk_ref[:, sl], gk)
            dq0.append(dxq.astype(BF16))
            dk0.append(dxk.astype(BF16))
            dkr = dkr + dxk
            dgq_acc = dgq_acc + gq_p
            dgk_acc = dgk_acc + gk_p
        dgq_ref[...] += dgq_acc
        dgk_ref[...] += dgk_acc
        dq0 = jnp.concatenate(dq0, axis=1)
        dk0 = jnp.concatenate(dk0, axis=1)
        dvb = dv_ref[...].astype(BF16)
        dwuq_ref[...] += _tn(cqn, dq0)
        dwuk_ref[...] += _tn(ckvn, dk0)
        dwuv_ref[...] += _tn(ckvn, dvb)
        dcqn = _nt(dq0, wuq_ref[...])
        dckvn = _nt(dk0, wuk_ref[...]) + _nt(dvb, wuv_ref[...])
        dgcq_ref[...] += _csum(dcqn * cqh)
        dgckv_ref[...] += _csum(dckvn * ckvh)
        dxh = dcqn * gcq_ref[...]
        dz_ref[:, 0:256] = (rq * (dxh - cqh * jnp.mean(dxh * cqh, axis=-1, keepdims=True))).astype(BF16)
        dxh = dckvn * gckv_ref[...]
        dz_ref[:, 256:384] = (rkv * (dxh - ckvh * jnp.mean(dxh * ckvh, axis=-1, keepdims=True))).astype(BF16)
        lane = _lane((T, LANE))
        dz_ref[:, 384:512] = jnp.where((lane >= KR_LANE) & (lane < KR_LANE + 32), dkr, 0.0).astype(BF16)

    tabspec = pl.BlockSpec((T, LANE), lambda i: (i, 0))
    in_specs = [_zcol(T, 256, C_CQ), _zcol(T, LANE, C_CKV), _zcol(T, LANE, C_KR),
                pl.BlockSpec((T, 1024), lambda i: (i, 0)), pl.BlockSpec((T, 1024), lambda i: (i, 0)), pl.BlockSpec((T, 512), lambda i: (i, 0)),
                _full((1, 256)), _full((1, LANE)), _full((256, 1024)), _full((LANE, 1024)), _full((LANE, 512)), _full((1, LANE)), _full((1, LANE)),
                tabspec, tabspec, tabspec, pl.BlockSpec(memory_space=pl.ANY)]
    out_specs = [_zcol(T, 512, C_CQ), _full((256, 1024)), _full((LANE, 1024)), _full((LANE, 512)), _full((1, 256)), _full((1, LANE)),
                 _full((1, LANE)), _full((1, LANE))]
    out_shape = [jax.ShapeDtypeStruct(dz.shape, BF16), jax.ShapeDtypeStruct((256, 1024), F32), jax.ShapeDtypeStruct((LANE, 1024), F32),
                 jax.ShapeDtypeStruct((LANE, 512), F32), jax.ShapeDtypeStruct((1, 256), F32), jax.ShapeDtypeStruct((1, LANE), F32),
                 jax.ShapeDtypeStruct((1, LANE), F32), jax.ShapeDtypeStruct((1, LANE), F32)]
    return pl.pallas_call(
        body, name="mla_pre_bwd", grid=(S // T,), in_specs=in_specs, out_specs=out_specs, out_shape=out_shape,
        input_output_aliases={16: 0}, compiler_params=_cp(),
    )(zp, zp, zp, dq, dk, dv, w["g_cq"], w["g_ckv"], w["w_uq"], w["w_uk"], w["w_uv"], w["g_mq"], w["g_mk"], *tab, dz)


T_DIL = 256


def _head_stats(x, lane):
    sq = x * x
    sa = _rsum(jnp.where(lane < 64, sq, 0.0))
    sb = _rsum(jnp.where(lane >= 64, sq, 0.0))
    return lax.rsqrt(jnp.where(lane < 64, sa, sb) * (1.0 / DIL_HD) + EPS)


def _head_sum(x, lane):
    sa = _rsum(jnp.where(lane < 64, x, 0.0))
    sb = _rsum(jnp.where(lane >= 64, x, 0.0))
    return jnp.where(lane < 64, sa, sb)


def _head_stats_mxu(x):
    r = lax.broadcasted_iota(jnp.int32, (LANE, LANE), 0)
    c = lax.broadcasted_iota(jnp.int32, (LANE, LANE), 1)
    ones = jnp.where((r < 64) == (c < 64), 1.0, 0.0).astype(F32)
    ss = lax.dot_general(x * x, ones, (((1,), (0,)), ((), ())), precision=lax.Precision.HIGHEST, preferred_element_type=F32)
    return lax.rsqrt(ss * (1.0 / DIL_HD) + EPS)


def _dil_pre_fwd(zp, w, tab):
    S = zp.shape[0]
    T = T_DIL

    def body(q_ref, k_ref, gq_ref, gk_ref, C_ref, S1_ref, S2_ref, qo_ref, ko_ref):
        C, S1, S2 = C_ref[...], S1_ref[...], S2_ref[...]
        for b in range(12):
            sl = slice(b * LANE, (b + 1) * LANE)
            x = q_ref[:, sl]
            qo_ref[:, sl] = _rope(x * _head_stats_mxu(x) * gq_ref[...], C, S1, S2, 32)
            x = k_ref[:, sl]
            ko_ref[:, sl] = _rope(x * _head_stats_mxu(x) * gk_ref[...], C, S1, S2, 32)

    tabspec = pl.BlockSpec((T, LANE), lambda i: (i, 0))
    out = pl.BlockSpec((T, 1536), lambda i: (i, 0))
    return pl.pallas_call(
        body, name="dil_pre_fwd", grid=(S // T,),
        in_specs=[_zcol(T, 1536, C_DQ), _zcol(T, 1536, C_DK), _full((1, LANE)), _full((1, LANE)), tabspec, tabspec, tabspec],
        out_specs=[out, out], out_shape=[jax.ShapeDtypeStruct((S, 1536), F32)] * 2, compiler_params=_cp(),
    )(zp, zp, w["g_dq"], w["g_dk"], *tab)


DIL_ROWS = 2048


def _dil_geometry(g, S):
    d = DIL_DILATIONS[g]
    P = NK * d
    return d, P, DIL_ROWS // P, S // P


def _dil_rows(start, d, blocks=1):
    return pl.ds(pl.multiple_of(start, NK), blocks * NK) if d == 1 else pl.ds(start, blocks * NK, stride=d)


def _dil_specs(g, S, col0):
    _, P, m, nb = _dil_geometry(g, S)
    cur = pl.BlockSpec((DIL_ROWS, LANE), lambda sb, c: (sb, col0 + c))
    prv = pl.BlockSpec((P, LANE), lambda sb, c: (jnp.maximum(sb * m - 1, 0), col0 + c))
    nxt = pl.BlockSpec((P, LANE), lambda sb, c: (jnp.minimum((sb + 1) * m, nb - 1), col0 + c))
    return cur, prv, nxt


def _dil_attn_fwd(q, k, zp, g):
    S = q.shape[0]
    d, P, m, nb = _dil_geometry(g, S)
    R = DIL_ROWS

    def body(q_ref, kc_ref, kp_ref, vc_ref, vp_ref, o_ref, lse_ref, *scr):
        sb = pl.program_id(0)
        if m > 1:
            ks_ref, vs_ref = scr
            ks_ref[0:P, :] = kp_ref[...]
            ks_ref[P:P + R, :] = kc_ref[...]
            vs_ref[0:P, :] = vp_ref[...]
            vs_ref[P:P + R, :] = vc_ref[...]
        lane = _lane((NK, LANE))

        def unit(u, carry):
            j = u // d
            start = j * P + (u - j * d)
            rows = _dil_rows(start, d)
            if m > 1:
                k2, v2 = ks_ref[_dil_rows(start, d, 2), :], vs_ref[_dil_rows(start, d, 2), :]
            else:
                k2 = jnp.concatenate([kp_ref[rows, :], kc_ref[rows, :]], axis=0)
                v2 = jnp.concatenate([vp_ref[rows, :], vc_ref[rows, :]], axis=0)
            k2, v2 = k2.astype(BF16), v2.astype(BF16)
            q_ = q_ref[rows, :].astype(BF16)
            row = lax.broadcasted_iota(jnp.int32, (NK, 2 * NK), 0)
            col = lax.broadcasted_iota(jnp.int32, (NK, 2 * NK), 1)
            band = (col >= row) & (col <= row + NK) & ((col >= NK) | (sb * m + j > 0))
            lane2 = _lane((2 * NK, LANE))
            zb, zv = jnp.zeros_like(q_), jnp.zeros_like(v2)
            o_tot = jnp.zeros((NK, LANE), F32)
            lse_tot = jnp.zeros((NK, LANE), F32)
            for hh in range(2):
                hm = (lane < 64) if hh == 0 else (lane >= 64)
                hm2 = (lane2 < 64) if hh == 0 else (lane2 >= 64)
                s_ = jnp.where(band, _nt(jnp.where(hm, q_, zb), k2) * DIL_SCALE, NEG)
                mx = jnp.max(s_, axis=-1, keepdims=True)
                e = jnp.exp(s_ - mx)
                den = _rsum(e)
                o_tot = o_tot + _nn(e.astype(BF16), jnp.where(hm2, v2, zv)) / den
                lse_tot = jnp.where(hm, mx + jnp.log(den), lse_tot)
            o_ref[rows, :] = o_tot
            lse_ref[rows, :] = lse_tot
            return carry

        lax.fori_loop(0, R // NK, unit, 0, unroll=8)

    qcur, qprv, _ = _dil_specs(g, S, 4 * g)
    vcur, vprv, _ = _dil_specs(g, S, C_DV + 4 * g)
    out = pl.BlockSpec((R, LANE), lambda sb, c: (sb, c))
    return pl.pallas_call(
        body, name=f"dil_attn_fwd{g}", grid=(S // R, 4), in_specs=[qcur, qcur, qprv, vcur, vprv], out_specs=[out, out],
        out_shape=[jax.ShapeDtypeStruct((S, 512), F32)] * 2,
        scratch_shapes=[pltpu.VMEM((P + R, LANE), F32)] * 2 if m > 1 else [], compiler_params=_cp(),
    )(q, k, k, zp, zp)


def _dil_combine(os_, ls_, zp):
    S = zp.shape[0]
    T = T_ROW

    def body(o0, o1, o2, l0, l1, l2, g_ref, oc_ref, L_ref, y_ref):
        a, b, c = l0[...], l1[...], l2[...]
        mx = jnp.maximum(jnp.maximum(a, b), c)
        ea, eb, ec = jnp.exp(a - mx), jnp.exp(b - mx), jnp.exp(c - mx)
        den = ea + eb + ec
        oc = (ea * o0[...] + eb * o1[...] + ec * o2[...]) / den
        oc_ref[...] = oc
        L_ref[...] = mx + jnp.log(den)
        y_ref[...] = (oc * _silu(g_ref[...])).astype(BF16)

    row = pl.BlockSpec((T, 512), lambda i: (i, 0))
    return pl.pallas_call(
        body, name="dil_combine", grid=(S // T,), in_specs=[row] * 6 + [_zcol(T, 512, C_DILG)], out_specs=[row, row, row],
        out_shape=[jax.ShapeDtypeStruct((S, 512), F32), jax.ShapeDtypeStruct((S, 512), F32), jax.ShapeDtypeStruct((S, 512), BF16)],
        compiler_params=_cp(),
    )(*os_, *ls_, zp)


def _dil_comb_bwd(zp, oc, dy, dz):
    S = zp.shape[0]
    T = T_ROW

    def body(g_ref, o_ref, dy_ref, dz_in, dz_ref, do_ref, D_ref):
        del dz_in
        g, o_, dy_ = g_ref[...], o_ref[...], dy_ref[...]
        do = dy_ * _silu(g)
        do_ref[...] = do
        dz_ref[...] = (dy_ * o_ * _dsilu(g)).astype(BF16)
        lane = _lane((T, LANE))
        for p in range(4):
            sl = slice(p * LANE, (p + 1) * LANE)
            D_ref[:, sl] = _head_sum(do[:, sl] * o_[:, sl], lane)

    row = pl.BlockSpec((T, 512), lambda i: (i, 0))
    zc = _zcol(T, 512, C_DILG)
    return pl.pallas_call(
        body, name="dil_comb_bwd", grid=(S // T,), in_specs=[zc, row, row, pl.BlockSpec(memory_space=pl.ANY)], out_specs=[zc, row, row],
        out_shape=[jax.ShapeDtypeStruct(dz.shape, BF16), jax.ShapeDtypeStruct((S, 512), F32), jax.ShapeDtypeStruct((S, 512), F32)],
        input_output_aliases={3: 0}, compiler_params=_cp(),
    )(zp, oc, dy, dz)


def _dil_attn_bwd(q, k, zp, do, L, Dr, g):
    S = q.shape[0]
    d, P, m, nb = _dil_geometry(g, S)
    R = DIL_ROWS
    n_q, n_k = 4, 2

    def body(*refs):
        q_side = refs[0:2 * n_q]
        k_side = refs[2 * n_q:2 * n_q + 2 * n_k]
        dq_ref, dk_ref, dv_ref = refs[2 * n_q + 2 * n_k:2 * n_q + 2 * n_k + 3]
        scr = refs[2 * n_q + 2 * n_k + 3:]
        sb = pl.program_id(0)
        if m > 1:
            for a in range(n_q):
                scr[a][0:R, :] = q_side[2 * a][...]
                scr[a][R:R + P, :] = q_side[2 * a + 1][...]
            for a in range(n_k):
                scr[n_q + a][0:P, :] = k_side[2 * a + 1][...]
                scr[n_q + a][P:P + R, :] = k_side[2 * a][...]
        lane = _lane((NK, LANE))

        def unit(u, carry):
            j = u // d
            start = j * P + (u - j * d)
            rows = _dil_rows(start, d)
            if m > 1:
                rows_b = _dil_rows(start + P, d)
                q2, do2, L2, D2 = [scr[a][_dil_rows(start, d, 2), :] for a in range(n_q)]
                kp, vp = [scr[n_q + a][rows, :] for a in range(n_k)]
                kc, vc = [scr[n_q + a][rows_b, :] for a in range(n_k)]
            else:
                q2, do2, L2, D2 = [jnp.concatenate([q_side[2 * a][rows, :], q_side[2 * a + 1][rows, :]], axis=0) for a in range(n_q)]
                kc, vc = [k_side[2 * a][rows, :] for a in range(n_k)]
                kp, vp = [k_side[2 * a + 1][rows, :] for a in range(n_k)]
            q2, do2 = q2.astype(BF16), do2.astype(BF16)
            kc, kp, vc, vp = kc.astype(BF16), kp.astype(BF16), vc.astype(BF16), vp.astype(BF16)
            n = sb * m + j
            hA = _lane((2 * NK, LANE)) < 64
            zq = jnp.zeros_like(q2)
            L2r, D2r = pltpu.roll(L2, 64, 1), pltpu.roll(D2, 64, 1)
            Q4 = jnp.concatenate([jnp.where(hA, q2, zq), jnp.where(hA, zq, q2)], axis=0)
            O4 = jnp.concatenate([jnp.where(hA, do2, zq), jnp.where(hA, zq, do2)], axis=0)
            L4 = jnp.concatenate([jnp.where(hA, L2, L2r), jnp.where(hA, L2r, L2)], axis=0)
            D4 = jnp.concatenate([jnp.where(hA, D2, D2r), jnp.where(hA, D2r, D2)], axis=0)
            row4 = lax.broadcasted_iota(jnp.int32, (4 * NK, NK), 0) & (2 * NK - 1)
            col4 = lax.broadcasted_iota(jnp.int32, (4 * NK, NK), 1)
            m4 = ((row4 < NK) & (col4 <= row4)) | ((row4 >= NK) & (col4 >= row4 - NK) & (n < nb - 1))
            p4 = jnp.exp(jnp.where(m4, _nt(Q4, kc) * DIL_SCALE, NEG) - L4)
            ds4 = (p4 * (_nt(O4, vc) - D4) * DIL_SCALE).astype(BF16)
            dk_tot = _tn(ds4, Q4)
            dv_tot = _tn(p4.astype(BF16), O4)
            pick = lambda x: jnp.concatenate([x[0:NK], x[2 * NK:3 * NK]], axis=0)
            Qn, On, Ln, Dn = pick(Q4), pick(O4), pick(L4), pick(D4)
            rowp = lax.broadcasted_iota(jnp.int32, (2 * NK, NK), 0) & (NK - 1)
            colp = lax.broadcasted_iota(jnp.int32, (2 * NK, NK), 1)
            pp = jnp.exp(jnp.where((colp >= rowp) & (n > 0), _nt(Qn, kp) * DIL_SCALE, NEG) - Ln)
            dsp = (pp * (_nt(On, vp) - Dn) * DIL_SCALE).astype(BF16)
            dq2 = _nn(pick(ds4), kc) + _nn(dsp, kp)
            dq_tot = jnp.where(lane < 64, dq2[0:NK], dq2[NK:2 * NK])
            dq_ref[rows, :] = dq_tot
            dk_ref[rows, :] = dk_tot
            dv_ref[rows, :] = dv_tot
            return carry

        lax.fori_loop(0, R // NK, unit, 0, unroll=8)

    qcur, qprv, qnxt = _dil_specs(g, S, 4 * g)
    vcur, vprv, _ = _dil_specs(g, S, C_DV + 4 * g)
    ocur, _, onxt = _dil_specs(g, S, 0)
    out = pl.BlockSpec((R, LANE), lambda sb, c: (sb, c))
    scratch = [pltpu.VMEM((P + R, LANE), F32)] * (n_q + n_k) if m > 1 else []
    return pl.pallas_call(
        body, name=f"dil_attn_bwd{g}", grid=(S // R, 4),
        in_specs=[qcur, qnxt, ocur, onxt, ocur, onxt, ocur, onxt, qcur, qprv, vcur, vprv],
        out_specs=[out, out, out], out_shape=[jax.ShapeDtypeStruct((S, 512), F32)] * 3, scratch_shapes=scratch, compiler_params=_cp(),
    )(q, q, do, do, L, L, Dr, Dr, k, k, zp, zp)


def _dil_pre_bwd(zp, dys, g, tab, dz, col, name):
    S = zp.shape[0]
    T = T_DIL

    def body(x_ref, dy0_ref, dy1_ref, dy2_ref, g_ref, C_ref, S1_ref, S2_ref, dz_in, dz_ref, dg_ref):
        del dz_in
        i = pl.program_id(0)
        C, S1, S2 = C_ref[...], S1_ref[...], S2_ref[...]
        lane = _lane((T, LANE))
        gv = g_ref[...]
        acc = jnp.zeros((1, LANE), F32)
        for b in range(12):
            sl = slice(b * LANE, (b + 1) * LANE)
            x = x_ref[:, sl]
            r = _head_stats(x, lane)
            xn = x * r
            dy_ref = (dy0_ref, dy1_ref, dy2_ref)[b // 4]
            dyn = _rope_t(dy_ref[:, (b % 4) * LANE:(b % 4 + 1) * LANE], C, S1, S2, 32)
            acc = acc + _csum(dyn * xn)
            dxh = dyn * gv
            dz_ref[:, sl] = (r * (dxh - xn * _head_sum(dxh * xn, lane) * (1.0 / DIL_HD))).astype(BF16)

        @pl.when(i == 0)
        def _():
            dg_ref[...] = acc

        @pl.when(i > 0)
        def _():
            dg_ref[...] += acc

    tabspec = pl.BlockSpec((T, LANE), lambda i: (i, 0))
    zc = _zcol(T, 1536, col)
    grp = pl.BlockSpec((T, 512), lambda i: (i, 0))
    return pl.pallas_call(
        body, name=name, grid=(S // T,),
        in_specs=[zc, grp, grp, grp, _full((1, LANE)), tabspec, tabspec, tabspec, pl.BlockSpec(memory_space=pl.ANY)],
        out_specs=[zc, _full((1, LANE))], out_shape=[jax.ShapeDtypeStruct(dz.shape, BF16), jax.ShapeDtypeStruct((1, LANE), F32)],
        input_output_aliases={8: 0}, compiler_params=_cp(),
    )(zp, *dys, g, *tab, dz)


def _dil_dv_into(dvs, dz):
    S = dz.shape[0]
    T = T_ROW

    def body(s0, s1, s2, dz_in, o_ref):
        del dz_in
        for gi, s in enumerate((s0, s1, s2)):
            o_ref[:, gi * 512:(gi + 1) * 512] = s[...].astype(BF16)

    grp = pl.BlockSpec((T, 512), lambda i: (i, 0))
    return pl.pallas_call(
        body, name="dil_dv", grid=(S // T,), in_specs=[grp, grp, grp, pl.BlockSpec(memory_space=pl.ANY)],
        out_specs=_zcol(T, 1536, C_DV), out_shape=jax.ShapeDtypeStruct(dz.shape, BF16), input_output_aliases={3: 0}, compiler_params=_cp(),
    )(*dvs, dz)


T_MRG = 256


def _merge_fwd(P, zp, b_merge):
    S = zp.shape[0]
    T = T_MRG

    def body(p0, p1, p2, m0, m1, m2, b_ref, o_ref):
        acc = jnp.zeros((T, D), F32)
        for j, (p, m) in enumerate(((p0, m0), (p1, m1), (p2, m2))):
            acc = acc + _sig(m[...] + b_ref[:, j * D:(j + 1) * D]) * p[...]
        o_ref[...] = acc.astype(BF16)

    row = pl.BlockSpec((T, D), lambda i: (i, 0))
    return pl.pallas_call(
        body, name="merge_fwd", grid=(S // T,),
        in_specs=[row, row, row] + [_zcol(T, D, C_MERGE + 8 * j) for j in range(3)] + [_full((1, 3 * D))], out_specs=row,
        out_shape=jax.ShapeDtypeStruct((S, D), BF16), compiler_params=_cp(),
    )(*P, zp, zp, zp, b_merge)


def _merge_bwd(dm, Pj, zp, bj, dz, j):
    S = zp.shape[0]
    T = T_MRG

    def body(dm_ref, p_ref, m_ref, b_ref, dz_in, dz_ref, dp_ref, db_ref):
        del dz_in
        i = pl.program_id(0)
        g = _sig(m_ref[...] + b_ref[...])
        dmv = dm_ref[...]
        dp_ref[...] = (dmv * g).astype(BF16)
        dg = dmv * p_ref[...] * g * (1.0 - g)
        dz_ref[...] = dg.astype(BF16)
        part = _csum(dg)

        @pl.when(i == 0)
        def _():
            db_ref[...] = part

        @pl.when(i > 0)
        def _():
            db_ref[...] += part

    row = pl.BlockSpec((T, D), lambda i: (i, 0))
    zc = _zcol(T, D, C_MERGE + 8 * j)
    return pl.pallas_call(
        body, name=f"merge_bwd{j}", grid=(S // T,), in_specs=[row, row, zc, _full((1, D)), pl.BlockSpec(memory_space=pl.ANY)],
        out_specs=[zc, row, _full((1, D))],
        out_shape=[jax.ShapeDtypeStruct(dz.shape, BF16), jax.ShapeDtypeStruct((S, D), BF16), jax.ShapeDtypeStruct((1, D), F32)],
        input_output_aliases={4: 0}, compiler_params=_cp(),
    )(dm, Pj, zp, bj, dz)


def _loss_fwd_bwd(y, target):
    S = y.shape[0]
    T = T_ROW

    def body(y_ref, t_ref, loss_ref, dy_ref):
        i = pl.program_id(0)
        err = y_ref[...] - t_ref[...]
        dy_ref[...] = err * (1.0 / D)
        part = jnp.sum(err * err, keepdims=True).reshape(1, 1) * (0.5 / D)

        @pl.when(i == 0)
        def _():
            loss_ref[...] = part

        @pl.when(i > 0)
        def _():
            loss_ref[...] += part

    row = pl.BlockSpec((T, D), lambda i: (i, 0))
    return pl.pallas_call(
        body, name="loss", grid=(S // T,), in_specs=[row, row], out_specs=[_full((1, 1)), row],
        out_shape=[jax.ShapeDtypeStruct((1, 1), F32), jax.ShapeDtypeStruct((S, D), F32)], compiler_params=_cp(),
    )(y, target)


def _layer_fwd(x, w, tabs):
    mla_tab, dil_tab = tabs
    S = x.shape[0]
    h = _rms_in_fwd(x, w["norm_g"])
    zp = _mm(h, w["w_in"], mode="nn", name="in_proj")
    hs, y_lru = _lru_fwd(zp, w)
    q, k, v = _mla_pre_fwd(zp, w, mla_tab)
    o_mla, lse, y_mla = _mla_attn_fwd(q, k, v, zp)
    qd, kd = _dil_pre_fwd(zp, w, dil_tab)
    og, lg = zip(*[_dil_attn_fwd(qd, kd, zp, g) for g in range(len(DIL_DILATIONS))])
    oc, L, y_dil = _dil_combine(og, lg, zp)
    P = [_mm(y_lru, w["w_lru_o"], mode="nn", name="lru_out"), _mm(y_mla, w["w_mla_o"], mode="nn", name="mla_out"),
         _mm(y_dil, w["w_dil_o"], mode="nn", name="dil_out")]
    merged = _merge_fwd(P, zp, w["b_merge"])
    x_out = _mm(merged, w["w_out"], mode="nn", name="out_proj", add=x)
    saved = dict(x=x, h=h, zp=zp, hs=hs, y=(y_lru, y_mla, y_dil), q=q, k=k, v=v, o_mla=o_mla, lse=lse, qd=qd, kd=kd, oc=oc, L=L, P=P,
                 merged=merged)
    return x_out, saved


def _layer_bwd(dout, w, tabs, sv, hook=None, after=None):
    mla_tab, dil_tab = tabs
    zp = sv["zp"]
    S = zp.shape[0]
    g = {}
    dm = _mm(dout, w["w_out"], mode="nt", name="d_merged", after=after)
    g["w_out"] = _mm(sv["merged"], dout, mode="tn", name="dw_out", out_dtype=BF16)
    dz = lax.empty((S, ZW), BF16)
    dP, db = [], []
    for j in range(3):
        dz, dpj, dbj = _merge_bwd(dm, sv["P"][j], zp, w["b_merge"][:, j * D:(j + 1) * D], dz, j)
        dP.append(dpj)
        db.append(dbj)
    g["b_merge"] = jnp.concatenate(db, axis=1)
    names = ("w_lru_o", "w_mla_o", "w_dil_o")
    dy = []
    for j in range(3):
        dy.append(_mm(dP[j], w[names[j]], mode="nt", name="dy_" + names[j]))
        g[names[j]] = _mm(sv["y"][j], dP[j], mode="tn", name="d" + names[j], out_dtype=BF16)
    dz = _lru_gate_bwd(zp, sv["hs"], dy[0], dz)
    dz, g["conv_w"], g["conv_b"], g["w_gx"], g["b_gx"], g["w_ga"], g["b_ga"], g["lam"] = _lru_bwd(zp, sv["hs"], dy[0], w, dz)
    dz, do, Dr = _mla_post_bwd(zp, sv["o_mla"], dy[1], dz)
    dq, dk, dv = _mla_attn_bwd(sv["q"], sv["k"], sv["v"], do, sv["lse"], Dr)
    dz, g["w_uq"], g["w_uk"], g["w_uv"], g["g_cq"], g["g_ckv"], g["g_mq"], g["g_mk"] = _mla_pre_bwd(zp, dq, dk, dv, w, mla_tab, dz)
    dz, dod, Dd = _dil_comb_bwd(zp, sv["oc"], dy[2], dz)
    dqs, dks, dvs = zip(*[_dil_attn_bwd(sv["qd"], sv["kd"], zp, dod, sv["L"], Dd, gi) for gi in range(len(DIL_DILATIONS))])
    dz, g["g_dq"] = _dil_pre_bwd(zp, dqs, w["g_dq"], dil_tab, dz, C_DQ, "dil_pre_bwd_q")
    dz, g["g_dk"] = _dil_pre_bwd(zp, dks, w["g_dk"], dil_tab, dz, C_DK, "dil_pre_bwd_k")
    dz = _dil_dv_into(dvs, dz)
    g["w_in"] = _mm(sv["h"], dz, mode="tn", name="dw_in", out_dtype=BF16)
    token = hook(g) if hook is not None else None
    dh = _mm(dz, w["w_in"], mode="nt", name="d_h", after=token, tk=ZW // 4)
    dx, g["norm_g"] = _rms_in_bwd(sv["x"], w["norm_g"], dh, dout)
    return dx, g


def _peers():
    mx, my, mc = lax.axis_index("x"), lax.axis_index("y"), lax.axis_index("c")
    me = 4 * mx + 2 * my + mc
    out = []
    for k in range(1, N_DEV):
        px = 1 - mx if k & 4 else mx
        py = 1 - my if k & 2 else my
        pc = 1 - mc if k & 1 else mc
        out.append(((px, py, pc), 4 * px + 2 * py + pc))
    return me, out


def _whole(ref, p):
    del p
    return ref


def _exchange(srcs, slicers, slices, name):
    n = len(srcs)

    def body(*refs):
        ins, outs = refs[:n], refs[n:2 * n]
        send_sems, recv_sems, local_sems = refs[2 * n:]
        me, peers = _peers()
        mine = [pltpu.make_async_copy(slicers[a](ins[a], me), outs[a].at[me], local_sems.at[a]) for a in range(n)]
        for cp in mine:
            cp.start()
        copies = []
        for k, (peer, pidx) in enumerate(peers):
            for a in range(n):
                cp = pltpu.make_async_remote_copy(
                    src_ref=slicers[a](ins[a], pidx), dst_ref=outs[a].at[me], send_sem=send_sems.at[k * n + a],
                    recv_sem=recv_sems.at[k * n + a], device_id=peer, device_id_type=pl.DeviceIdType.MESH)
                cp.start()
                copies.append(cp)
        for cp in copies + mine:
            cp.wait()

    nsem = (N_DEV - 1) * n
    return pl.pallas_call(
        body, name=name, out_shape=[jax.ShapeDtypeStruct((N_DEV,) + shp, dt) for shp, dt in slices],
        in_specs=[pl.BlockSpec(memory_space=pl.ANY)] * n, out_specs=[pl.BlockSpec(memory_space=pl.ANY)] * n,
        scratch_shapes=[pltpu.SemaphoreType.DMA((nsem,)), pltpu.SemaphoreType.DMA((nsem,)), pltpu.SemaphoreType.DMA((n,))],
        compiler_params=pltpu.CompilerParams(has_side_effects=True),
    )(*srcs)


def _gather_two_level(srcs, name):
    n = len(srcs)

    def body(*refs):
        ins, outs = refs[:n], refs[n:2 * n]
        send_sems, recv_sems, local_sems = refs[2 * n:]
        mx, my, mc = lax.axis_index("x"), lax.axis_index("y"), lax.axis_index("c")
        me, sibling = (mx, my, mc), (mx, my, 1 - mc)
        chips = [(1 - mx, my), (mx, 1 - my), (1 - mx, 1 - my)]
        slot = lambda d: 4 * d[0] + 2 * d[1] + d[2]

        def copy(j, a, block, to, own=False):
            return pltpu.make_async_remote_copy(
                src_ref=ins[a] if own else outs[a].at[slot(block)], dst_ref=outs[a].at[slot(block)],
                send_sem=send_sems.at[j * n + a], recv_sem=recv_sems.at[j * n + a], device_id=to, device_id_type=pl.DeviceIdType.MESH)

        mine = [pltpu.make_async_copy(ins[a], outs[a].at[slot(me)], local_sems.at[a]) for a in range(n)]
        first = [copy(1 + j, a, me, (*chip, mc), own=True) for j, chip in enumerate(chips) for a in range(n)]
        first += [copy(0, a, me, sibling, own=True) for a in range(n)]
        for cp in mine + first:
            cp.start()
        passed = []
        for j, chip in enumerate(chips):
            for a in range(n):
                copy(1 + j, a, (*chip, mc), me).wait_recv()
                cp = copy(4 + j, a, (*chip, mc), sibling)
                cp.start()
                passed.append(cp)
        for a in range(n):
            copy(0, a, sibling, me).wait_recv()
        for j, chip in enumerate(chips):
            for a in range(n):
                copy(4 + j, a, (*chip, 1 - mc), me).wait_recv()
        for cp in first + passed:
            cp.wait_send()
        for cp in mine:
            cp.wait()

    nsem = (N_DEV - 1) * n
    return pl.pallas_call(
        body, name=name, out_shape=[jax.ShapeDtypeStruct((N_DEV,) + a.shape, a.dtype) for a in srcs],
        in_specs=[pl.BlockSpec(memory_space=pl.ANY)] * n, out_specs=[pl.BlockSpec(memory_space=pl.ANY)] * n,
        scratch_shapes=[pltpu.SemaphoreType.DMA((nsem,)), pltpu.SemaphoreType.DMA((nsem,)), pltpu.SemaphoreType.DMA((n,))],
        compiler_params=pltpu.CompilerParams(has_side_effects=True),
    )(*srcs)


_HBM = pl.BlockSpec(memory_space=pltpu.HBM)
_SEM = pl.BlockSpec(memory_space=pltpu.SEMAPHORE)
_DATAFLOW = pltpu.SideEffectType.DATAFLOW_SIDE_EFFECTING


def _plan_chips():
    mx, my, mc = lax.axis_index("x"), lax.axis_index("y"), lax.axis_index("c")
    return 2 * mx + my, [((cx, cy, mc), 2 * cx + cy) for cx, cy in ((1 - mx, my), (mx, 1 - my), (1 - mx, 1 - my))]


def _pair_exchange(srcs, slicers, slices, sliced, name):
    n = len(srcs)
    pieces = [4 if s else 1 for s in sliced]

    def body(*refs):
        ins, outs = refs[:n], refs[n:2 * n]
        send_sems, recv_sems = refs[2 * n:]
        mx, my, mc = lax.axis_index("x"), lax.axis_index("y"), lax.axis_index("c")
        copies = []
        for a in range(n):
            for q in range(pieces[a]):
                i = len(copies)
                copies.append(pltpu.make_async_remote_copy(
                    src_ref=slicers[a](ins[a], 2 * q + 1 - mc) if sliced[a] else ins[a], dst_ref=outs[a].at[q],
                    send_sem=send_sems.at[i], recv_sem=recv_sems.at[i], device_id=(mx, my, 1 - mc), device_id_type=pl.DeviceIdType.MESH))
        for cp in copies:
            cp.start()
        for cp in copies:
            cp.wait()

    return pl.pallas_call(
        body, name=name, out_shape=[jax.ShapeDtypeStruct((p,) + shp, dt) for (shp, dt), p in zip(slices, pieces)],
        in_specs=[pl.BlockSpec(memory_space=pl.ANY)] * n, out_specs=[pl.BlockSpec(memory_space=pl.ANY)] * n,
        scratch_shapes=[pltpu.SemaphoreType.DMA((sum(pieces),)), pltpu.SemaphoreType.DMA((sum(pieces),))],
        compiler_params=pltpu.CompilerParams(has_side_effects=True),
    )(*srcs)


def _pair_add(src, came, first_blk, axis, name):
    _, r, c = came.shape
    nblk = (c if axis == 1 else r) // LANE
    if axis == 1:
        s_spec = pl.BlockSpec((r, LANE), lambda q, j, fb: (0, fb[q] + j))
        o_spec = pl.BlockSpec((1, r, LANE), lambda q, j, fb: (q, 0, j))
    else:
        s_spec = pl.BlockSpec((LANE, c), lambda q, j, fb: (fb[q] + j, 0))
        o_spec = pl.BlockSpec((1, LANE, c), lambda q, j, fb: (q, j, 0))

    def body(fb_ref, x_ref, y_ref, o_ref):
        del fb_ref
        o_ref[0] = (x_ref[...].astype(F32) + y_ref[0].astype(F32)).astype(o_ref.dtype)

    return pl.pallas_call(
        body, name=name, out_shape=jax.ShapeDtypeStruct(came.shape, came.dtype),
        grid_spec=pltpu.PrefetchScalarGridSpec(num_scalar_prefetch=1, grid=(4, nblk), in_specs=[s_spec, o_spec], out_specs=o_spec),
        compiler_params=_cp(),
    )(first_blk, src, came)


def _add2(x, y, name):
    shp = x.shape
    x, y = x.reshape(-1, shp[-1]), y.reshape(-1, shp[-1])
    R, C = x.shape
    tr = R
    while tr * C * 4 > (1 << 21) and tr % 32 == 0:
        tr //= 2

    def body(x_ref, y_ref, o_ref):
        o_ref[...] = (x_ref[...].astype(F32) + y_ref[...].astype(F32)).astype(o_ref.dtype)

    spec = pl.BlockSpec((tr, C), lambda i: (i, 0))
    return pl.pallas_call(body, name=name, grid=(R // tr,), in_specs=[spec, spec], out_specs=spec,
                          out_shape=jax.ShapeDtypeStruct((R, C), x.dtype), compiler_params=_cp())(x, y).reshape(shp)


def _exchange_start(srcs, slicers, slices, after, name, plan=_peers, nslots=N_DEV):
    n = len(srcs)
    nsem = (nslots - 1) * n
    lands = [lax.empty((nslots,) + shp, dt) for shp, dt in slices]

    def body(*refs):
        ins, lands_in = refs[:n], refs[n:2 * n]
        send_sems, recv_sems, local_sems = refs[2 * n + 1], refs[2 * n + 2], refs[2 * n + 3]
        token = refs[-1]
        me, peers = plan()
        for a in range(n):
            pltpu.make_async_copy(slicers[a](ins[a], me), lands_in[a].at[me], local_sems.at[a]).start()
        for k, (peer, pidx) in enumerate(peers):
            for a in range(n):
                pltpu.make_async_remote_copy(
                    src_ref=slicers[a](ins[a], pidx), dst_ref=lands_in[a].at[me], send_sem=send_sems.at[k * n + a],
                    recv_sem=recv_sems.at[k * n + a], device_id=peer, device_id_type=pl.DeviceIdType.MESH).start()
        token[...] = jnp.zeros_like(token)

    hbm = lambda a: pltpu.with_memory_space_constraint(a, pltpu.HBM)
    return pl.pallas_call(
        body, name=name,
        out_shape=(pltpu.SemaphoreType.DMA((nsem,)), pltpu.SemaphoreType.DMA((nsem,)), pltpu.SemaphoreType.DMA((n,)),
                   *[pltpu.HBM(a.shape, a.dtype) for a in srcs], *[pltpu.HBM(a.shape, a.dtype) for a in lands],
                   jax.ShapeDtypeStruct((SUB, LANE), F32)),
        in_specs=[_HBM] * (2 * n) + [pl.BlockSpec(memory_space=pl.ANY)],
        out_specs=(_SEM, _SEM, _SEM, *[_HBM] * (2 * n), pl.BlockSpec(memory_space=pltpu.VMEM)),
        input_output_aliases={i: 3 + i for i in range(2 * n)},
        compiler_params=pltpu.CompilerParams(has_side_effects=_DATAFLOW),
    )(*[hbm(a) for a in srcs], *[hbm(a) for a in lands], after)


def _exchange_wait(started, slicers, after, name, plan=_peers):
    n = (len(started) - 4) // 2
    sems, thru = started[0:3], started[3:3 + 2 * n]

    def body(*refs):
        srcs, lands = refs[:n], refs[n:2 * n]
        send_sems, recv_sems, local_sems = refs[2 * n], refs[2 * n + 1], refs[2 * n + 2]
        me, peers = plan()
        for k, (peer, pidx) in enumerate(peers):
            for a in range(n):
                cp = pltpu.make_async_remote_copy(
                    src_ref=slicers[a](srcs[a], pidx), dst_ref=lands[a].at[me], send_sem=send_sems.at[k * n + a],
                    recv_sem=recv_sems.at[k * n + a], device_id=peer, device_id_type=pl.DeviceIdType.MESH)
                cp.wait_send()
                cp.wait_recv()
        for a in range(n):
            pltpu.make_async_copy(slicers[a](srcs[a], me), lands[a].at[me], local_sems.at[a]).wait()

    outs = pl.pallas_call(
        body, name=name, out_shape=[pltpu.HBM(a.shape, a.dtype) for a in thru],
        in_specs=[_HBM] * (2 * n) + [_SEM, _SEM, _SEM, pl.BlockSpec(memory_space=pl.ANY)], out_specs=[_HBM] * (2 * n),
        input_output_aliases={i: i for i in range(2 * n)}, compiler_params=pltpu.CompilerParams(has_side_effects=_DATAFLOW),
    )(*thru, *sems, after)
    return outs[n:]


WIN = 13 * LANE


def _win_base(s):
    n = s * SHARD_IN
    a0 = n + jnp.where(n >= _KR0, KR_LANE, 0) + jnp.where(n >= _KR0 + 32, 32, 0)
    return jnp.minimum(a0 // LANE, (ZW - WIN) // LANE)


def _win_offsets(s):
    n = s * SHARD_IN + jnp.arange(SHARD_IN)
    o = s * SHARD_IN - _win_base(s) * LANE
    return n, (o, o + KR_LANE, o + LANE - 32)


def _to_window(shard, s):
    _, offs = _win_offsets(s)
    padded = jnp.pad(shard, ((0, 0), (0, 0), (WIN, WIN)))
    a, b, c = [lax.dynamic_slice(padded, (0, 0, WIN - o), shard.shape[:2] + (WIN,)) for o in offs]
    col = (_win_base(s) * LANE + jnp.arange(WIN))[None, None, :]
    zero = jnp.zeros_like(a)
    return jnp.where(col < _KR0, a, jnp.where((col >= _KR0 + KR_LANE) & (col < _KR0 + KR_LANE + 32), b, jnp.where(col >= _KR0 + LANE, c, zero)))


def _from_window(win, s):
    n, offs = _win_offsets(s)
    a, b, c = [lax.dynamic_slice(win, (0, 0, o), win.shape[:2] + (SHARD_IN,)) for o in offs]
    return jnp.where((n < _KR0)[None, None, :], a, jnp.where((n < _KR0 + 32)[None, None, :], b, c))


def _win_base_static(s):
    n = s * SHARD_IN
    a0 = n + (KR_LANE if n >= _KR0 else 0) + (32 if n >= _KR0 + 32 else 0)
    return min(a0 // LANE, (ZW - WIN) // LANE)


def _assemble_w_in(gw):
    tr = 128
    bases = [_win_base_static(s) for s in range(N_DEV)]

    def body(g_ref, o_ref):
        for j in range(ZW // LANE):
            acc = None
            for s in range(N_DEV):
                if bases[s] <= j < bases[s] + WIN // LANE:
                    piece = g_ref[s, :, (j - bases[s]) * LANE:(j - bases[s] + 1) * LANE]
                    acc = piece if acc is None else acc + piece
            o_ref[:, j * LANE:(j + 1) * LANE] = acc

    return pl.pallas_call(
        body, name="assemble_w_in", grid=(D // tr,), in_specs=[pl.BlockSpec((N_DEV, tr, WIN), lambda i: (0, i, 0))],
        out_specs=pl.BlockSpec((tr, ZW), lambda i: (i, 0)), out_shape=jax.ShapeDtypeStruct((D, ZW), gw.dtype), compiler_params=_cp(),
    )(gw)


def _cols(width):
    return lambda ref, p: ref.at[:, pl.ds(pl.multiple_of(p * width, width), width)]


def _rows(height):
    return lambda ref, p: ref.at[pl.ds(pl.multiple_of(p * height, height), height), :]


SCATTER = {
    'w_in': (lambda ref, p: ref.at[:, pl.ds(pl.multiple_of(_win_base(p) * LANE, LANE), WIN)], (D, WIN), BF16),
    'conv_w': (_cols(LANE), (4, LANE), F32),
    'w_lru_o': (_rows(LANE), (LANE, D), BF16),
    'w_uq': (_cols(LANE), (256, LANE), F32),
    'w_ukv': (_cols(LANE), (128, LANE), F32),
    'w_mla_o': (_cols(LANE), (512, LANE), BF16),
    'w_dil_o': (_cols(LANE), (512, LANE), BF16),
    'w_out': (_rows(LANE), (LANE, D), BF16),
}
SLICED_AXIS = {'w_in': 1, 'conv_w': 1, 'w_lru_o': 0, 'w_uq': 1, 'w_ukv': 1, 'w_mla_o': 1, 'w_dil_o': 1, 'w_out': 0}


PACK_ROWS = 64


def _packed_rows(shapes):
    n = sum(int(np.prod(s)) for s in shapes)
    return -(-n // (PACK_ROWS * LANE)) * PACK_ROWS


def _sum8(buf, name):
    ns, R, C = buf.shape
    tr = R
    while tr * C * 4 * ns > (1 << 22) and tr % 32 == 0:
        tr //= 2

    def body(b_ref, o_ref):
        acc = b_ref[0].astype(F32)
        for s in range(1, ns):
            acc = acc + b_ref[s].astype(F32)
        o_ref[...] = acc

    return pl.pallas_call(
        body, name=name, grid=(R // tr,), in_specs=[pl.BlockSpec((ns, tr, C), lambda i: (0, i, 0))],
        out_specs=pl.BlockSpec((tr, C), lambda i: (i, 0)), out_shape=jax.ShapeDtypeStruct((R, C), F32), compiler_params=_cp(),
    )(buf)


def _pack(arrs, dtype, lead):
    flat = [a.astype(dtype).reshape(a.shape[:lead] + (-1,)) for a in arrs]
    cat = jnp.concatenate(flat, axis=-1)
    n = cat.shape[-1]
    unit = PACK_ROWS * LANE
    pad = (-n) % unit
    if pad:
        cat = jnp.pad(cat, [(0, 0)] * lead + [(0, pad)])
    return cat.reshape(cat.shape[:lead] + ((n + pad) // LANE, LANE))


def _unpack(buf, shapes, lead):
    flat = buf.reshape(buf.shape[:lead] + (-1,))
    out, off = [], 0
    for shp in shapes:
        n = int(np.prod(shp))
        out.append(flat[..., off:off + n].reshape(buf.shape[:lead] + tuple(shp)))
        off += n
    return out


def _adamw(w, g, m, v, name):
    layers, rows, cols = w.shape
    tr = rows
    while tr * cols * 4 > (3 << 19) and tr % 16 == 0:
        tr //= 2
    c1 = 1.0 - ADAM_B1 ** ADAM_STEP
    c2 = 1.0 - ADAM_B2 ** ADAM_STEP

    def body(w_ref, g_ref, m_ref, v_ref, d_ref, mo_ref, vo_ref):
        gv = g_ref[...]
        mn = ADAM_B1 * m_ref[...] + (1.0 - ADAM_B1) * gv
        vn = ADAM_B2 * v_ref[...] + (1.0 - ADAM_B2) * (gv * gv)
        mo_ref[...] = mn
        vo_ref[...] = vn
        d_ref[...] = -ADAM_LR * ((mn / c1) / (jnp.sqrt(vn / c2) + ADAM_EPS) + ADAM_WD * w_ref[...])

    spec = pl.BlockSpec((1, tr, cols), lambda l, i: (l, i, 0))
    return pl.pallas_call(
        body, name=name, grid=(layers, rows // tr), in_specs=[spec] * 4, out_specs=[spec] * 3,
        out_shape=[jax.ShapeDtypeStruct((layers, rows, cols), F32)] * 3, compiler_params=_cp(),
    )(w, g, m, v)


IN_NAMES = ['x', 'positions', 'norm_g', 'w_in', 'conv_w', 'conv_b', 'w_gate_x', 'b_gate_x', 'w_gate_a', 'b_gate_a', 'lru_lambda', 'w_lru_o',
            'cq_norm_g', 'ckv_norm_g', 'w_uq', 'w_ukv', 'mla_q_norm_g', 'mla_k_norm_g', 'w_mla_o', 'dil_q_norm_g', 'dil_k_norm_g', 'w_dil_o',
            'b_merge', 'w_out']
WEIGHTS = IN_NAMES[2:]
REPLICATED = [n for n in WEIGHTS if n not in SCATTER]
GATE_WEIGHTS = ('w_gate_x', 'w_gate_a')

_KR0 = C_KR * LANE


GATHERED = ['w_in', 'w_lru_o', 'w_uq', 'w_ukv', 'w_mla_o', 'w_dil_o', 'w_out', 'conv_w']


def _local_weights(wd, me):
    loc = {n: wd[n].astype(BF16) for n in GATHERED[:-1]}
    loc['w_in'] = _to_window(loc['w_in'], me)
    loc['w_uq'] = jnp.pad(loc['w_uq'], ((0, 0), (0, 0), (0, LANE - MLA_QK)))
    loc['conv_w'] = wd['conv_w']
    return [[loc[n][l] for n in GATHERED] for l in range(DEPTH)]


def _layer_weights(gathered, rep, l):
    gw = dict(zip(GATHERED, gathered))
    by_rows = lambda a: a.reshape(-1, a.shape[-1])
    by_cols = lambda a: jnp.swapaxes(a, 0, 1).reshape(a.shape[1], -1)
    ukv = jnp.swapaxes(gw['w_ukv'], 0, 1)
    g96 = lambda a: jnp.pad(a[l].reshape(1, MLA_QK), ((0, 0), (0, LANE - MLA_QK)))
    g64 = lambda a: jnp.tile(a[l].reshape(1, DIL_HD), (1, 2))
    return dict(
        norm_g=rep['norm_g'][l].reshape(1, D), w_in=_assemble_w_in(gw['w_in']),
        conv_w=by_cols(gw['conv_w']), conv_b=rep['conv_b'][l].reshape(1, D),
        w_gx=rep['w_gate_x'][l].astype(BF16), b_gx=rep['b_gate_x'][l].reshape(8, 1, LANE),
        w_ga=rep['w_gate_a'][l].astype(BF16), b_ga=rep['b_gate_a'][l].reshape(8, 1, LANE),
        lam=rep['lru_lambda'][l].reshape(1, D),
        w_lru_o=by_rows(gw['w_lru_o']), w_mla_o=by_cols(gw['w_mla_o']), w_dil_o=by_cols(gw['w_dil_o']), w_out=by_rows(gw['w_out']),
        g_cq=rep['cq_norm_g'][l].reshape(1, 256), g_ckv=rep['ckv_norm_g'][l].reshape(1, 128),
        w_uq=by_cols(gw['w_uq']), w_uk=jnp.pad(ukv[:, :, :64], ((0, 0), (0, 0), (0, 64))).reshape(128, 1024),
        w_uv=ukv[:, :, 64:].reshape(128, 512),
        g_mq=g96(rep['mla_q_norm_g']), g_mk=g96(rep['mla_k_norm_g']), g_dq=g64(rep['dil_q_norm_g']), g_dk=g64(rep['dil_k_norm_g']),
        b_merge=rep['b_merge'][l].reshape(1, 3 * D),
    )


def _sharded_grads(g):
    uk = g['w_uk'].reshape(128, 8, 128)[:, :, :64]
    uv = g['w_uv'].reshape(128, 8, 64)
    d = {'w_in': g['w_in'], 'conv_w': g['conv_w'], 'w_lru_o': g['w_lru_o'], 'w_uq': g['w_uq'],
         'w_ukv': jnp.concatenate([uk, uv], axis=-1).reshape(128, 1024), 'w_mla_o': g['w_mla_o'], 'w_dil_o': g['w_dil_o'],
         'w_out': g['w_out']}
    return [d[n] for n in SCATTER]


def _replicated_grads(g):
    return {
        'conv_b': g['conv_b'].reshape(D),
        'w_gate_x': g['w_gx'], 'b_gate_x': g['b_gx'].reshape(8, LANE), 'w_gate_a': g['w_ga'], 'b_gate_a': g['b_ga'].reshape(8, LANE),
        'lru_lambda': g['lam'].reshape(D), 'cq_norm_g': g['g_cq'].reshape(256), 'ckv_norm_g': g['g_ckv'].reshape(128),
        'mla_q_norm_g': g['g_mq'][0, :MLA_QK], 'mla_k_norm_g': g['g_mk'][0, :MLA_QK],
        'dil_q_norm_g': g['g_dq'][0, :DIL_HD] + g['g_dq'][0, DIL_HD:], 'dil_k_norm_g': g['g_dk'][0, :DIL_HD] + g['g_dk'][0, DIL_HD:],
        'b_merge': g['b_merge'].reshape(3 * D),
    }


def kernel(x, positions, norm_g, w_in, conv_w, conv_b, w_gate_x, b_gate_x, w_gate_a, b_gate_a, lru_lambda, w_lru_o, cq_norm_g, ckv_norm_g, w_uq, w_ukv, mla_q_norm_g, mla_k_norm_g, w_mla_o, dil_q_norm_g, dil_k_norm_g, w_dil_o, b_merge, w_out, loss_target, m_norm_g, m_w_in, m_conv_w, m_conv_b, m_w_gate_x, m_b_gate_x, m_w_gate_a, m_b_gate_a, m_lru_lambda, m_w_lru_o, m_cq_norm_g, m_ckv_norm_g, m_w_uq, m_w_ukv, m_mla_q_norm_g, m_mla_k_norm_g, m_w_mla_o, m_dil_q_norm_g, m_dil_k_norm_g, m_w_dil_o, m_b_merge, m_w_out, v_norm_g, v_w_in, v_conv_w, v_conv_b, v_w_gate_x, v_b_gate_x, v_w_gate_a, v_b_gate_a, v_lru_lambda, v_w_lru_o, v_cq_norm_g, v_ckv_norm_g, v_w_uq, v_w_ukv, v_mla_q_norm_g, v_mla_k_norm_g, v_w_mla_o, v_dil_q_norm_g, v_dil_k_norm_g, v_w_dil_o, v_b_merge, v_w_out):
    args = (x, positions, norm_g, w_in, conv_w, conv_b, w_gate_x, b_gate_x, w_gate_a, b_gate_a, lru_lambda, w_lru_o, cq_norm_g, ckv_norm_g, w_uq, w_ukv, mla_q_norm_g, mla_k_norm_g, w_mla_o, dil_q_norm_g, dil_k_norm_g, w_dil_o, b_merge, w_out)
    moments_m = (m_norm_g, m_w_in, m_conv_w, m_conv_b, m_w_gate_x, m_b_gate_x, m_w_gate_a, m_b_gate_a, m_lru_lambda, m_w_lru_o, m_cq_norm_g, m_ckv_norm_g, m_w_uq, m_w_ukv, m_mla_q_norm_g, m_mla_k_norm_g, m_w_mla_o, m_dil_q_norm_g, m_dil_k_norm_g, m_w_dil_o, m_b_merge, m_w_out)
    moments_v = (v_norm_g, v_w_in, v_conv_w, v_conv_b, v_w_gate_x, v_b_gate_x, v_w_gate_a, v_b_gate_a, v_lru_lambda, v_w_lru_o, v_cq_norm_g, v_ckv_norm_g, v_w_uq, v_w_ukv, v_mla_q_norm_g, v_mla_k_norm_g, v_w_mla_o, v_dil_q_norm_g, v_dil_k_norm_g, v_w_dil_o, v_b_merge, v_w_out)
    a = dict(zip(IN_NAMES, args))
    wd = {n: a[n] for n in WEIGHTS}
    md = dict(zip(WEIGHTS, moments_m))
    vd = dict(zip(WEIGHTS, moments_v))

    me = 4 * lax.axis_index("x") + 2 * lax.axis_index("y") + lax.axis_index("c")

    assert DEPTH == 2
    xs, tabs = x[0], _rope_tables(positions[0])
    whole = [_whole] * len(GATHERED)
    slicers = [SCATTER[n][0] for n in SCATTER]
    grad_slices = [SCATTER[n][1:3] for n in SCATTER]

    local = _local_weights(wd, me)
    w_slices = [(a.shape, a.dtype) for a in local[0]]
    landed0 = _gather_two_level(local[0], "gather_w0")
    flying = _exchange_start(local[1], whole, w_slices, landed0[0], "gather_w1_start")
    rep0 = dict(wd, norm_g=wd['norm_g'] + flying[-1][0, 0])
    w0 = _layer_weights(landed0, rep0, 0)
    x1, saved0 = _layer_fwd(xs, w0, tabs)
    w1 = _layer_weights(_exchange_wait(flying, whole, x1, "gather_w1_wait"), wd, 1)
    x2, saved1 = _layer_fwd(x1, w1, tabs)
    loss, dx2 = _loss_fwd_bwd(x2, loss_target[0])
    loss = loss[0, 0]

    sharded = list(SCATTER)
    nsh = len(sharded)
    small = [n for n in REPLICATED if n not in GATE_WEIGHTS and n != 'norm_g']

    def outgoing(g):
        r = _replicated_grads(g)
        return (_sharded_grads(g) + [_pack([r[n] for n in small], F32, 0)]
                + [r[n].astype(BF16).reshape(8 * LANE, LANE) for n in GATE_WEIGHTS])

    out_slicers = slicers + [_whole] * 3
    out_slices = grad_slices + [((_packed_rows([wd[n].shape[1:] for n in small]), LANE), F32)] + [((8 * LANE, LANE), BF16)] * 2
    dx1, g1 = _layer_bwd(dx2, w1, tabs, saved1)
    flying1 = _exchange_start(outgoing(g1), out_slicers, out_slices, dx1, "scatter_g1_start")
    later = {}

    names = sharded + ['small'] + list(GATE_WEIGHTS)
    sliced = [True] * nsh + [False] * 3
    by_chip = [(lambda ref, q: ref.at[q])] * nsh + [_whole] * 3

    def send_layer0(g):
        later['got1'] = _exchange_wait(flying1, out_slicers, g['w_in'], "scatter_g1_wait")
        mine = outgoing(g)
        came = _pair_exchange(mine, out_slicers, out_slices, sliced, "pair_g0")
        my_side = 2 * jnp.arange(4, dtype=jnp.int32) + lax.axis_index("c")
        halves = []
        for n, a, c in zip(names, mine, came):
            if n in SCATTER:
                first = _win_base(my_side) if n == 'w_in' else my_side
                halves.append(_pair_add(a, c, first.astype(jnp.int32), SLICED_AXIS[n], f"pair_sum_{n}"))
            else:
                halves.append(_add2(a, c[0], f"pair_sum_{n}"))
        later['flying0'] = _exchange_start(halves, by_chip, out_slices, later['got1'][0], "scatter_g0_start", plan=_plan_chips, nslots=4)
        return later['flying0'][-1]

    grad_x, g0 = _layer_bwd(dx1, w0, tabs, saved0, hook=send_layer0, after=flying1[-1])
    sum1 = [_sum8(b, f"sum_{n}_1") for n, b in zip(names, later['got1'])]
    behind = grad_x[:1, :1] + sum1[0][:1, :1]
    got0 = _exchange_wait(later['flying0'], by_chip, behind, "scatter_g0_wait", plan=_plan_chips)
    sum0 = [_sum8(b, f"sum_{n}_0") for n, b in zip(names, got0)]
    norm_part = _pack([jnp.stack([g['norm_g'].reshape(D) for g in (g0, g1)])], F32, 0)
    norm_sum = _sum8(_exchange([norm_part], [_whole], [(norm_part.shape, F32)], "gather_norm_g")[0], "sum_norm_g")

    gsh = {n: jnp.stack([sum0[i], sum1[i]]) for i, n in enumerate(sharded)}
    gsh['w_in'] = _from_window(gsh['w_in'], me)
    gsh['w_uq'] = gsh['w_uq'][:, :, :MLA_QK]
    grep = {'norm_g': _unpack(norm_sum, [wd['norm_g'].shape], 0)[0]}
    per_layer = [_unpack(s[nsh], [wd[n].shape[1:] for n in small], 0) for s in (sum0, sum1)]
    grep.update({n: jnp.stack([per_layer[l][i] for l in range(DEPTH)]) for i, n in enumerate(small)})
    for i, n in enumerate(GATE_WEIGHTS):
        grep[n] = jnp.stack([sum0[nsh + 1 + i], sum1[nsh + 1 + i]]).reshape(wd[n].shape)

    out_g, out_d, out_m, out_v = {}, {}, {}, {}
    vecs = ['norm_g'] + small
    vshapes = [wd[n].shape for n in vecs]
    packed_g = _pack([grep[n] for n in vecs], F32, 0)
    d_, m_, v_ = _adamw(_pack([wd[n] for n in vecs], F32, 0)[None], packed_g[None], _pack([md[n] for n in vecs], F32, 0)[None],
                        _pack([vd[n] for n in vecs], F32, 0)[None], "adamw_vectors")
    for dst, buf in ((out_d, d_), (out_m, m_), (out_v, v_)):
        dst.update(zip(vecs, _unpack(buf[0], vshapes, 0)))
    out_g.update({n: grep[n] for n in vecs})
    gsh.update({n: grep[n] for n in GATE_WEIGHTS})
    for n in sharded + list(GATE_WEIGHTS):
        shp = wd[n].shape
        three = (1, -1, shp[-1])
        d_, m_, v_ = _adamw(wd[n].reshape(three), gsh[n].reshape(three), md[n].reshape(three), vd[n].reshape(three), "adamw_" + n)
        out_g[n], out_d[n], out_m[n], out_v[n] = gsh[n], d_.reshape(shp), m_.reshape(shp), v_.reshape(shp)

    loss = lax.psum(loss, ("x", "y", "c"))
    return (loss, grad_x[None], *[out_g[n] for n in WEIGHTS], *[out_d[n] for n in WEIGHTS], *[out_m[n] for n in WEIGHTS],
            *[out_v[n] for n in WEIGHTS])
```

```python
import functools

import numpy as np
import jax
import jax.numpy as jnp
from jax import lax
from jax.experimental import pallas as pl
from jax.experimental.pallas import tpu as pltpu

F32 = jnp.float32
BF16 = jnp.bfloat16

N_DEV = 8
D = 1024
DEPTH = 2
EPS = 1e-6
ROPE_THETA = 10000.0
LRU_C = 8.0
LANE = 128
SUB = 8
IN_WIDTH = 11168
SHARD_IN = IN_WIDTH // N_DEV

C_LRUX, C_LRUG, C_CQ, C_CKV, C_KR, C_MLAG, C_DQ, C_DK, C_DV, C_DILG, C_MERGE = 0, 8, 16, 18, 19, 20, 24, 36, 48, 60, 64
ZW = 88 * LANE
KR_LANE = 64

MLA_QK = 96
MLA_SCALE = MLA_QK ** -0.5
DIL_HD = 64
DIL_SCALE = DIL_HD ** -0.5
DIL_DILATIONS = (1, 4, 16)
NK = 128

ADAM_LR, ADAM_B1, ADAM_B2, ADAM_EPS, ADAM_WD, ADAM_STEP = 0.001, 0.9, 0.999, 1e-08, 0.01, 10

NEG = -1e30
LOG2E = 1.4426950408889634
VMEM_LIMIT = 48 * 1024 * 1024


def _cp(**kw):
    return pltpu.CompilerParams(vmem_limit_bytes=VMEM_LIMIT, **kw)


def _sig(x):
    return 1.0 / (1.0 + jnp.exp(-x))


def _silu(x):
    return x * _sig(x)


def _dsilu(x):
    s = _sig(x)
    return s * (1.0 + x * (1.0 - s))


def _dot(a, b, dims):
    return lax.dot_general(a, b, (dims, ((), ())), preferred_element_type=F32)


def _nn(a, b):
    return _dot(a, b, ((1,), (0,)))


def _nt(a, b):
    return _dot(a, b, ((1,), (1,)))


def _tn(a, b):
    return _dot(a, b, ((0,), (0,)))


def _rsum(x):
    return jnp.sum(x, axis=-1, keepdims=True)


def _rsum_mxu(x):
    ones = jnp.ones((x.shape[-1], LANE), F32)
    return lax.dot_general(x, ones, (((1,), (0,)), ((), ())), precision=lax.Precision.HIGHEST, preferred_element_type=F32)


def _csum(x):
    return jnp.sum(x, axis=0, keepdims=True)


def _mm(a, b, *, mode, name, out_dtype=F32, add=None, after=None, tm=1024, tn=1024, tk=1024):
    if mode == "nn":
        (M, K), (K2, N) = a.shape, b.shape
    elif mode == "nt":
        (M, K), (N, K2) = a.shape, b.shape
    else:
        (K, M), (K2, N) = a.shape, b.shape
    assert K == K2
    tm, tn, tk = min(tm, M), min(tn, N), min(tk, K)
    assert M % tm == 0 and N % tn == 0 and K % tk == 0
    nk = K // tk
    fn = {"nn": _nn, "nt": _nt, "tn": _tn}[mode]
    has_add = add is not None

    def body(*refs):
        a_ref, b_ref = refs[0], refs[1]
        add_ref = refs[2] if has_add else None
        o_ref = refs[2 + has_add + (after is not None)]
        part = fn(a_ref[...].astype(BF16), b_ref[...].astype(BF16))

        def fin(acc):
            if has_add:
                acc = acc + add_ref[...]
            o_ref[...] = acc.astype(out_dtype)

        if nk == 1:
            fin(part)
        else:
            acc_ref = refs[-1]
            k = pl.program_id(2)

            @pl.when(k == 0)
            def _():
                acc_ref[...] = part

            @pl.when(k > 0)
            def _():
                acc_ref[...] += part

            @pl.when(k == nk - 1)
            def _():
                fin(acc_ref[...])

    a_spec = pl.BlockSpec((tk, tm), lambda i, j, k: (k, i)) if mode == "tn" else pl.BlockSpec((tm, tk), lambda i, j, k: (i, k))
    b_spec = pl.BlockSpec((tn, tk), lambda i, j, k: (j, k)) if mode == "nt" else pl.BlockSpec((tk, tn), lambda i, j, k: (k, j))
    o_spec = pl.BlockSpec((tm, tn), lambda i, j, k: (i, j))
    in_specs, args = [a_spec, b_spec], [a, b]
    if has_add:
        in_specs.append(o_spec)
        args.append(add)
    if after is not None:
        in_specs.append(pl.BlockSpec(memory_space=pl.ANY))
        args.append(after)
    return pl.pallas_call(
        body, name=name, grid=(M // tm, N // tn, nk), in_specs=in_specs, out_specs=o_spec,
        out_shape=jax.ShapeDtypeStruct((M, N), out_dtype),
        scratch_shapes=[pltpu.VMEM((tm, tn), F32)] if nk > 1 else [],
        compiler_params=_cp(dimension_semantics=("parallel", "parallel", "arbitrary")),
    )(*args)


T_ROW = 512


def _rms_in_fwd(x, g):
    S = x.shape[0]
    T = T_ROW

    def body(x_ref, g_ref, h_ref):
        xv = x_ref[...]
        r = lax.rsqrt(jnp.mean(xv * xv, axis=-1, keepdims=True) + EPS)
        h_ref[...] = (xv * r * g_ref[...]).astype(BF16)

    return pl.pallas_call(
        body, name="rms_in_fwd", grid=(S // T,),
        in_specs=[pl.BlockSpec((T, D), lambda i: (i, 0)), pl.BlockSpec((1, D), lambda i: (0, 0))],
        out_specs=pl.BlockSpec((T, D), lambda i: (i, 0)),
        out_shape=jax.ShapeDtypeStruct((S, D), BF16), compiler_params=_cp(),
    )(x, g)


def _rms_in_bwd(x, g, dh, dres):
    S = x.shape[0]
    T = T_ROW

    def body(x_ref, g_ref, dh_ref, dr_ref, dx_ref, dg_ref):
        i = pl.program_id(0)
        xv = x_ref[...]
        r = lax.rsqrt(jnp.mean(xv * xv, axis=-1, keepdims=True) + EPS)
        xn = xv * r
        dy = dh_ref[...]
        part = _csum(dy * xn)

        @pl.when(i == 0)
        def _():
            dg_ref[...] = part

        @pl.when(i > 0)
        def _():
            dg_ref[...] += part

        dxh = dy * g_ref[...]
        dx_ref[...] = dr_ref[...] + r * (dxh - xn * jnp.mean(dxh * xn, axis=-1, keepdims=True))

    row = pl.BlockSpec((T, D), lambda i: (i, 0))
    vec = pl.BlockSpec((1, D), lambda i: (0, 0))
    return pl.pallas_call(
        body, name="rms_in_bwd", grid=(S // T,), in_specs=[row, vec, row, row], out_specs=[row, vec],
        out_shape=[jax.ShapeDtypeStruct((S, D), F32), jax.ShapeDtypeStruct((1, D), F32)], compiler_params=_cp(),
    )(x, g, dh, dres)


T_LRU = 1024


def _neg_expm1(y):
    ser = -y * (1.0 + y * 0.5 * (1.0 + y * (1.0 / 3.0) * (1.0 + y * 0.25 * (1.0 + y * 0.2))))
    return jnp.where(y > -0.03, ser, 1.0 - jnp.exp(y))


def _softplus_neg(lam):
    e = jnp.exp(-jnp.abs(lam))
    l1p = jnp.where(e < 0.01, e * (1.0 - e * (0.5 - e * (1.0 / 3.0 - e * 0.25))), jnp.log(1.0 + e))
    return jnp.maximum(-lam, 0.0) + l1p


def _scan_fwd(a, b, T):
    row = lax.broadcasted_iota(jnp.int32, a.shape, 0)
    d = 1
    while d < T:
        m = row >= d
        b = jnp.where(m, a * pltpu.roll(b, d, 0) + b, b)
        a = jnp.where(m, a * pltpu.roll(a, d, 0), a)
        d *= 2
    return a, b


def _scan_bwd(a, b, T):
    row = lax.broadcasted_iota(jnp.int32, a.shape, 0)
    d = 1
    while d < T:
        m = row < T - d
        b = jnp.where(m, a * pltpu.roll(b, T - d, 0) + b, b)
        a = jnp.where(m, a * pltpu.roll(a, T - d, 0), a)
        d *= 2
    return b


def _lru_common(x, prev, first, cw_ref, cb_ref, wgx_ref, bgx_ref, wga_ref, bga_ref, lam_ref, T):
    row = lax.broadcasted_iota(jnp.int32, x.shape, 0)
    prev = jnp.where(first, 0.0, prev)
    xs = []
    for j in (3, 2, 1):
        pv = jnp.tile(pltpu.roll(prev, j, 0), (T // SUB, 1))
        xs.append(jnp.where(row < j, pv, pltpu.roll(x, j, 0)))
    xs.append(x)
    xc = cb_ref[...] + cw_ref[0:1, :] * xs[0] + cw_ref[1:2, :] * xs[1] + cw_ref[2:3, :] * xs[2] + cw_ref[3:4, :] * xs[3]
    xcb = xc.astype(BF16)
    gx = _sig(_nn(xcb, wgx_ref[0]) + bgx_ref[0])
    ga = _sig(_nn(xcb, wga_ref[0]) + bga_ref[0])
    sp = _softplus_neg(lam_ref[...])
    log_a = -LRU_C * ga * sp
    a = jnp.exp(log_a)
    mult = jnp.sqrt(_neg_expm1(2.0 * log_a))
    return xs, xc, xcb, gx, ga, sp, a, mult


def _lru_specs(T, tmap):
    def at(col0):
        return pl.BlockSpec((T, LANE), lambda n, i: (tmap(i), col0 + n))

    def prev(col0):
        return pl.BlockSpec((SUB, LANE), lambda n, i: (jnp.maximum(tmap(i) * (T // SUB) - 1, 0), col0 + n))

    small = [
        pl.BlockSpec((4, LANE), lambda n, i: (0, n)),
        pl.BlockSpec((1, LANE), lambda n, i: (0, n)),
        pl.BlockSpec((1, LANE, LANE), lambda n, i: (n, 0, 0)),
        pl.BlockSpec((1, 1, LANE), lambda n, i: (n, 0, 0)),
        pl.BlockSpec((1, LANE, LANE), lambda n, i: (n, 0, 0)),
        pl.BlockSpec((1, 1, LANE), lambda n, i: (n, 0, 0)),
        pl.BlockSpec((1, LANE), lambda n, i: (0, n)),
    ]
    return at, prev, small


def _lru_fwd(zp, w):
    S = zp.shape[0]
    T = T_LRU
    at, prev, small = _lru_specs(T, lambda i: i)

    def body(x_ref, xp_ref, g_ref, cw_ref, cb_ref, wgx_ref, bgx_ref, wga_ref, bga_ref, lam_ref, hs_ref, y_ref, carry_ref):
        i = pl.program_id(1)

        @pl.when(i == 0)
        def _():
            carry_ref[...] = jnp.zeros_like(carry_ref)

        x = x_ref[...]
        _, xc, _, gx, _, _, a, mult = _lru_common(x, xp_ref[...], i == 0, cw_ref, cb_ref, wgx_ref, bgx_ref, wga_ref, bga_ref, lam_ref, T)
        A, B = _scan_fwd(a, mult * gx * xc, T)
        h = B + A * carry_ref[SUB - 1:SUB, :]
        hs_ref[...] = h
        carry_ref[...] = hs_ref[T - SUB:T, :]
        y_ref[...] = (h * _silu(g_ref[...])).astype(BF16)

    out = pl.BlockSpec((T, LANE), lambda n, i: (i, n))
    return pl.pallas_call(
        body, name="lru_fwd", grid=(8, S // T),
        in_specs=[at(C_LRUX), prev(C_LRUX), at(C_LRUG)] + small, out_specs=[out, out],
        out_shape=[jax.ShapeDtypeStruct((S, D), F32), jax.ShapeDtypeStruct((S, D), BF16)],
        scratch_shapes=[pltpu.VMEM((SUB, LANE), F32)],
        compiler_params=_cp(dimension_semantics=("parallel", "arbitrary")),
    )(zp, zp, zp, w["conv_w"], w["conv_b"], w["w_gx"], w["b_gx"], w["w_ga"], w["b_ga"], w["lam"])


def _lru_bwd(zp, hs, dy, w, dz):
    S = zp.shape[0]
    T = T_LRU
    nT = S // T
    at, prev, small = _lru_specs(T, lambda i: nT - 1 - i)

    def body(x_ref, xp_ref, g_ref, h_ref, hp_ref, dy_ref, cw_ref, cb_ref, wgx_ref, bgx_ref, wga_ref, bga_ref, lam_ref, dz_in,
             dzx_ref, dcw_ref, dcb_ref, dwgx_ref, dbgx_ref, dwga_ref, dbga_ref, dlam_ref, carry_ref, head_ref):
        del dz_in
        j = pl.program_id(1)
        it = nT - 1 - j

        @pl.when(j == 0)
        def _():
            for r in (carry_ref, head_ref, dcw_ref, dcb_ref, dwgx_ref, dbgx_ref, dwga_ref, dbga_ref, dlam_ref):
                r[...] = jnp.zeros_like(r)

        first = it == 0
        x = x_ref[...]
        xs, xc, xcb, gx, ga, sp, a, mult = _lru_common(x, xp_ref[...], first, cw_ref, cb_ref, wgx_ref, bgx_ref, wga_ref, bga_ref, lam_ref, T)
        row = lax.broadcasted_iota(jnp.int32, x.shape, 0)
        u = gx * xc
        h = h_ref[...]
        hp = jnp.where(first, 0.0, hp_ref[...])
        hm1 = jnp.where(row < 1, jnp.tile(pltpu.roll(hp, 1, 0), (T // SUB, 1)), pltpu.roll(h, 1, 0))
        dho = dy_ref[...] * _silu(g_ref[...])
        gin = jnp.where(row == T - 1, dho + carry_ref[0:1, :], dho)
        abar = jnp.where(row == T - 1, 0.0, pltpu.roll(a, T - 1, 0))
        dh = _scan_bwd(abar, gin, T)
        carry_ref[...] = (a * dh)[0:SUB, :]
        da = dh * hm1
        dmult = dh * u
        du = dh * mult
        dgx = du * xc
        dxc = du * gx
        dlog_a = da * a - dmult * a * a / mult
        dga = dlog_a * (-LRU_C * sp)
        lam = lam_ref[...]
        dlam_ref[...] += _csum(dlog_a * (-LRU_C * ga)) * (-1.0 / (1.0 + jnp.exp(lam)))
        dpa = dga * ga * (1.0 - ga)
        dpx = dgx * gx * (1.0 - gx)
        dpab, dpxb = dpa.astype(BF16), dpx.astype(BF16)
        dxc = dxc + _nt(dpxb, wgx_ref[0]) + _nt(dpab, wga_ref[0])
        dwgx_ref[0] += _tn(xcb, dpxb)
        dwga_ref[0] += _tn(xcb, dpab)
        dbgx_ref[0] += _csum(dpx)
        dbga_ref[0] += _csum(dpa)
        dcb_ref[...] += _csum(dxc)
        for k in range(4):
            dcw_ref[k:k + 1, :] += _csum(dxc * xs[k])
        head = head_ref[...]
        dx = cw_ref[3:4, :] * dxc
        for jj in (1, 2, 3):
            hv = jnp.tile(pltpu.roll(head, SUB - jj, 0), (T // SUB, 1))
            dx = dx + cw_ref[3 - jj:4 - jj, :] * jnp.where(row >= T - jj, hv, pltpu.roll(dxc, T - jj, 0))
        head_ref[...] = dxc[0:SUB, :]
        dzx_ref[...] = dx.astype(BF16)

    def acc(shape, imap):
        return pl.BlockSpec(shape, imap)

    out_specs = [
        pl.BlockSpec((T, LANE), lambda n, i: (nT - 1 - i, C_LRUX + n)),
        acc((4, LANE), lambda n, i: (0, n)), acc((1, LANE), lambda n, i: (0, n)),
        acc((1, LANE, LANE), lambda n, i: (n, 0, 0)), acc((1, 1, LANE), lambda n, i: (n, 0, 0)),
        acc((1, LANE, LANE), lambda n, i: (n, 0, 0)), acc((1, 1, LANE), lambda n, i: (n, 0, 0)),
        acc((1, LANE), lambda n, i: (0, n)),
    ]
    out_shape = [
        jax.ShapeDtypeStruct(dz.shape, BF16),
        jax.ShapeDtypeStruct((4, D), F32), jax.ShapeDtypeStruct((1, D), F32),
        jax.ShapeDtypeStruct((8, LANE, LANE), F32), jax.ShapeDtypeStruct((8, 1, LANE), F32),
        jax.ShapeDtypeStruct((8, LANE, LANE), F32), jax.ShapeDtypeStruct((8, 1, LANE), F32),
        jax.ShapeDtypeStruct((1, D), F32),
    ]
    dyspec = pl.BlockSpec((T, LANE), lambda n, i: (nT - 1 - i, n))
    hprev = pl.BlockSpec((SUB, LANE), lambda n, i: (jnp.maximum((nT - 1 - i) * (T // SUB) - 1, 0), n))
    return pl.pallas_call(
        body, name="lru_bwd", grid=(8, nT),
        in_specs=[at(C_LRUX), prev(C_LRUX), at(C_LRUG), dyspec, hprev, dyspec] + small + [pl.BlockSpec(memory_space=pl.ANY)],
        out_specs=out_specs, out_shape=out_shape,
        scratch_shapes=[pltpu.VMEM((SUB, LANE), F32), pltpu.VMEM((SUB, LANE), F32)],
        input_output_aliases={13: 0},
        compiler_params=_cp(dimension_semantics=("parallel", "arbitrary")),
    )(zp, zp, zp, hs, hs, dy, w["conv_w"], w["conv_b"], w["w_gx"], w["b_gx"], w["w_ga"], w["b_ga"], w["lam"], dz)


def _lru_gate_bwd(zp, hs, dy, dz):
    S = zp.shape[0]
    T = T_ROW

    def body(g_ref, h_ref, dy_ref, dz_in, o_ref):
        del dz_in
        o_ref[...] = (dy_ref[...] * h_ref[...] * _dsilu(g_ref[...])).astype(BF16)

    row = pl.BlockSpec((T, D), lambda i: (i, 0))
    zc = pl.BlockSpec((T, D), lambda i: (i, C_LRUG // 8))
    return pl.pallas_call(
        body, name="lru_gate_bwd", grid=(S // T,), in_specs=[zc, row, row, pl.BlockSpec(memory_space=pl.ANY)], out_specs=zc,
        out_shape=jax.ShapeDtypeStruct(dz.shape, BF16), input_output_aliases={3: 0}, compiler_params=_cp(),
    )(zp, hs, dy, dz)


def _rope_tables(pos):
    pf = pos.astype(F32)[:, None]

    def cs(d):
        inv = ROPE_THETA ** (-jnp.arange(0, d, 2, dtype=F32) / d)
        ang = pf * inv
        return jnp.cos(ang), jnp.sin(ang)

    S = pos.shape[0]
    c, s = cs(32)
    one, zero = jnp.ones((S, 64), F32), jnp.zeros((S, 16), F32)
    z32, z64 = jnp.zeros((S, 32), F32), jnp.zeros((S, 64), F32)
    mla = (jnp.concatenate([one, c, c, jnp.ones((S, 32), F32)], 1),
           jnp.concatenate([z64, zero, s, z32], 1),
           jnp.concatenate([z64, -s, zero, z32], 1))
    c, s = cs(64)
    dil = (jnp.concatenate([c, c, c, c], 1),
           jnp.concatenate([z32, s, z32, s], 1),
           jnp.concatenate([-s, z32, -s, z32], 1))
    return mla, dil


def _rope(x, C, S1, S2, sh):
    return x * C + pltpu.roll(x, sh, 1) * S1 + pltpu.roll(x, LANE - sh, 1) * S2


def _rope_t(dy, C, S1, S2, sh):
    return dy * C + pltpu.roll(dy * S1, LANE - sh, 1) + pltpu.roll(dy * S2, sh, 1)


def _lane(shape):
    return lax.broadcasted_iota(jnp.int32, shape, 1)


T_MLA = 256
TA = 512


def _zcol(T, width, col_lanes):
    assert (col_lanes * LANE) % width == 0
    return pl.BlockSpec((T, width), lambda i: (i, col_lanes * LANE // width))


def _full(shape):
    return pl.BlockSpec(shape, lambda *_: (0,) * len(shape))


def _mla_pre_fwd(zp, w, tab):
    S = zp.shape[0]
    T = T_MLA

    def body(cq_ref, ckv_ref, kr_ref, gcq_ref, gckv_ref, wuq_ref, wuk_ref, wuv_ref, gq_ref, gk_ref, C_ref, S1_ref, S2_ref,
             q_ref, k_ref, v_ref):
        cq = cq_ref[...]
        cqn = (cq * lax.rsqrt(jnp.mean(cq * cq, axis=-1, keepdims=True) + EPS) * gcq_ref[...]).astype(BF16)
        ckv = ckv_ref[...]
        ckvn = (ckv * lax.rsqrt(jnp.mean(ckv * ckv, axis=-1, keepdims=True) + EPS) * gckv_ref[...]).astype(BF16)
        q0 = _nn(cqn, wuq_ref[...])
        k0 = _nn(ckvn, wuk_ref[...])
        krb = kr_ref[...]
        C, S1, S2 = C_ref[...], S1_ref[...], S2_ref[...]
        for h in range(8):
            sl = slice(h * LANE, (h + 1) * LANE)
            xq = q0[:, sl]
            xq = xq * lax.rsqrt(_rsum_mxu(xq * xq) * (1.0 / MLA_QK) + EPS) * gq_ref[...]
            q_ref[:, sl] = _rope(xq, C, S1, S2, 16).astype(BF16)
            xk = k0[:, sl] + krb
            xk = xk * lax.rsqrt(_rsum_mxu(xk * xk) * (1.0 / MLA_QK) + EPS) * gk_ref[...]
            k_ref[:, sl] = _rope(xk, C, S1, S2, 16).astype(BF16)
        v_ref[...] = _nn(ckvn, wuv_ref[...]).astype(BF16)

    tabspec = pl.BlockSpec((T, LANE), lambda i: (i, 0))
    in_specs = [_zcol(T, 256, C_CQ), _zcol(T, LANE, C_CKV), _zcol(T, LANE, C_KR), _full((1, 256)), _full((1, LANE)),
                _full((256, 1024)), _full((LANE, 1024)), _full((LANE, 512)), _full((1, LANE)), _full((1, LANE)),
                tabspec, tabspec, tabspec]
    return pl.pallas_call(
        body, name="mla_pre_fwd", grid=(S // T,), in_specs=in_specs,
        out_specs=[pl.BlockSpec((T, 1024), lambda i: (i, 0)), pl.BlockSpec((T, 1024), lambda i: (i, 0)), pl.BlockSpec((T, 512), lambda i: (i, 0))],
        out_shape=[jax.ShapeDtypeStruct((S, 1024), BF16), jax.ShapeDtypeStruct((S, 1024), BF16), jax.ShapeDtypeStruct((S, 512), BF16)],
        compiler_params=_cp(),
    )(zp, zp, zp, w["g_cq"], w["g_ckv"], w["w_uq"], w["w_uk"], w["w_uv"], w["g_mq"], w["g_mk"], *tab)


def _mla_attn_fwd(q, k, v, zp):
    S = q.shape[0]
    nq = S // TA

    def body(q_ref, k_ref, v_ref, g_ref, o_ref, lse_ref, y_ref):
        qi = pl.program_id(1)
        lane = _lane((TA, LANE))
        rowi = lax.broadcasted_iota(jnp.int32, (TA, TA), 0)
        coli = lax.broadcasted_iota(jnp.int32, (TA, TA), 1)
        o_tot = jnp.zeros((TA, LANE), F32)
        for hh in range(2):
            cs = slice(hh * LANE, (hh + 1) * LANE)
            hm = (lane < 64) if hh == 0 else (lane >= 64)
            qh = q_ref[:, cs]
            ones_lane = 64 if hh == 0 else 0

            def step(kb, carry, masked, cs=cs, hm=hm, qh=qh, ones_lane=ones_lane):
                m, acc = carry
                off = pl.multiple_of(kb * TA, TA)
                kh = k_ref[pl.ds(off, TA), cs]
                vv = v_ref[pl.ds(off, TA), :]
                vh = jnp.where(hm, vv, jnp.where(lane == ones_lane, jnp.ones_like(vv), jnp.zeros_like(vv)))
                s = _nt(qh, kh) * (MLA_SCALE * LOG2E)
                if masked:
                    s = jnp.where(rowi >= coli, s, NEG)
                m_new = jnp.maximum(m, jnp.max(s, axis=-1, keepdims=True))
                acc = jnp.exp2(m - m_new) * acc + _nn(jnp.exp2(s - m_new).astype(BF16), vh)
                return m_new, acc

            init = (jnp.full((TA, 1), NEG, F32), jnp.zeros((TA, LANE), F32))
            carry = lax.fori_loop(0, qi, lambda kb, c: step(kb, c, False), init)
            m, acc = step(qi, carry, True)
            l = _rsum(jnp.where(lane == ones_lane, acc, 0.0))
            o_tot = o_tot + jnp.where(hm, acc, 0.0) / l
            lse_ref[:, cs] = jnp.broadcast_to(m * (1.0 / LOG2E) + jnp.log(l), (TA, LANE))
        o_ref[...] = o_tot
        y_ref[...] = (o_tot * _silu(g_ref[...])).astype(BF16)

    blk = pl.BlockSpec((TA, LANE), lambda p, i: (i, p))
    return pl.pallas_call(
        body, name="mla_attn_fwd", grid=(4, nq),
        in_specs=[pl.BlockSpec((TA, 256), lambda p, i: (i, p)), pl.BlockSpec((S, 256), lambda p, i: (0, p)),
                  pl.BlockSpec((S, LANE), lambda p, i: (0, p)), pl.BlockSpec((TA, LANE), lambda p, i: (i, C_MLAG + p))],
        out_specs=[blk, pl.BlockSpec((TA, 256), lambda p, i: (i, p)), blk],
        out_shape=[jax.ShapeDtypeStruct((S, 512), F32), jax.ShapeDtypeStruct((S, 1024), F32), jax.ShapeDtypeStruct((S, 512), BF16)],
        compiler_params=_cp(dimension_semantics=("parallel", "arbitrary")),
    )(q, k, v, zp)


def _mla_post_bwd(zp, o, dy, dz):
    S = zp.shape[0]
    T = T_ROW

    def body(g_ref, o_ref, dy_ref, dz_in, dz_ref, do_ref, D_ref):
        del dz_in
        g, o_, dy_ = g_ref[...], o_ref[...], dy_ref[...]
        do = dy_ * _silu(g)
        do_ref[...] = do.astype(BF16)
        dz_ref[...] = (dy_ * o_ * _dsilu(g)).astype(BF16)
        prod = do * o_
        lane = _lane((T, LANE))
        for p in range(4):
            pr = prod[:, p * LANE:(p + 1) * LANE]
            da = _rsum(jnp.where(lane < 64, pr, 0.0))
            db = _rsum(jnp.where(lane >= 64, pr, 0.0))
            D_ref[:, 2 * p * LANE:(2 * p + 1) * LANE] = jnp.broadcast_to(da, (T, LANE))
            D_ref[:, (2 * p + 1) * LANE:(2 * p + 2) * LANE] = jnp.broadcast_to(db, (T, LANE))

    row = pl.BlockSpec((T, 512), lambda i: (i, 0))
    zc = _zcol(T, 512, C_MLAG)
    return pl.pallas_call(
        body, name="mla_post_bwd", grid=(S // T,), in_specs=[zc, row, row, pl.BlockSpec(memory_space=pl.ANY)],
        out_specs=[zc, row, pl.BlockSpec((T, 1024), lambda i: (i, 0))],
        out_shape=[jax.ShapeDtypeStruct(dz.shape, BF16), jax.ShapeDtypeStruct((S, 512), BF16), jax.ShapeDtypeStruct((S, 1024), F32)],
        input_output_aliases={3: 0}, compiler_params=_cp(),
    )(zp, o, dy, dz)


def _mla_attn_bwd(q, k, v, do, lse, Dr):
    S = q.shape[0]
    nq = S // TA

    def body(q_ref, do_ref, lse_ref, D_ref, k_ref, v_ref, dq_ref, dk_ref, dv_ref):
        ki = pl.program_id(1)

        @pl.when(ki == 0)
        def _():
            dq_ref[...] = jnp.zeros_like(dq_ref)

        lane = _lane((TA, LANE))
        rowi = lax.broadcasted_iota(jnp.int32, (TA, TA), 0)
        coli = lax.broadcasted_iota(jnp.int32, (TA, TA), 1)
        dv_tot = jnp.zeros((TA, LANE), F32)
        for hh in range(2):
            cs = slice(hh * LANE, (hh + 1) * LANE)
            hm = (lane < 64) if hh == 0 else (lane >= 64)
            kh = k_ref[:, cs]
            vv = v_ref[...]
            vm = jnp.where(hm, vv, jnp.zeros_like(vv))

            def step(qb, carry, masked, cs=cs, kh=kh, vm=vm):
                dk_acc, dv_acc = carry
                off = pl.multiple_of(qb * TA, TA)
                qh = q_ref[pl.ds(off, TA), cs]
                doh = do_ref[pl.ds(off, TA), :]
                ls = jnp.tile(lse_ref[pl.ds(off, TA), cs], (1, TA // LANE))
                dd = jnp.tile(D_ref[pl.ds(off, TA), cs], (1, TA // LANE))
                s = _nt(qh, kh) * MLA_SCALE
                if masked:
                    s = jnp.where(rowi >= coli, s, NEG)
                p = jnp.exp(s - ls)
                dp = _nt(doh, vm)
                ds = (p * (dp - dd) * MLA_SCALE).astype(BF16)
                dv_acc = dv_acc + _tn(p.astype(BF16), doh)
                dk_acc = dk_acc + _tn(ds, qh)
                dq_ref[pl.ds(off, TA), cs] += _nn(ds, kh)
                return dk_acc, dv_acc

            z = jnp.zeros((TA, LANE), F32)
            carry = step(ki, (z, z), True)
            dk_acc, dv_acc = lax.fori_loop(ki + 1, nq, lambda qb, c: step(qb, c, False), carry)
            dk_ref[:, cs] = dk_acc
            dv_tot = dv_tot + jnp.where(hm, dv_acc, 0.0)
        dv_ref[...] = dv_tot

    pair = pl.BlockSpec((S, 256), lambda p, i: (0, p))
    return pl.pallas_call(
        body, name="mla_attn_bwd", grid=(4, nq),
        in_specs=[pair, pl.BlockSpec((S, LANE), lambda p, i: (0, p)), pair, pair,
                  pl.BlockSpec((TA, 256), lambda p, i: (i, p)), pl.BlockSpec((TA, LANE), lambda p, i: (i, p))],
        out_specs=[pair, pl.BlockSpec((TA, 256), lambda p, i: (i, p)), pl.BlockSpec((TA, LANE), lambda p, i: (i, p))],
        out_shape=[jax.ShapeDtypeStruct((S, 1024), F32), jax.ShapeDtypeStruct((S, 1024), F32), jax.ShapeDtypeStruct((S, 512), F32)],
        compiler_params=_cp(dimension_semantics=("parallel", "arbitrary")),
    )(q, do, lse, Dr, k, v)


def _mla_pre_bwd(zp, dq, dk, dv, w, tab, dz):
    S = zp.shape[0]
    T = T_MLA

    def body(cq_ref, ckv_ref, kr_ref, dq_ref, dk_ref, dv_ref, gcq_ref, gckv_ref, wuq_ref, wuk_ref, wuv_ref, gq_ref, gk_ref,
             C_ref, S1_ref, S2_ref, dz_in, dz_ref, dwuq_ref, dwuk_ref, dwuv_ref, dgcq_ref, dgckv_ref, dgq_ref, dgk_ref):
        del dz_in
        i = pl.program_id(0)

        @pl.when(i == 0)
        def _():
            for r in (dwuq_ref, dwuk_ref, dwuv_ref, dgcq_ref, dgckv_ref, dgq_ref, dgk_ref):
                r[...] = jnp.zeros_like(r)

        cq = cq_ref[...]
        rq = lax.rsqrt(jnp.mean(cq * cq, axis=-1, keepdims=True) + EPS)
        cqh = cq * rq
        cqn = (cqh * gcq_ref[...]).astype(BF16)
        ckv = ckv_ref[...]
        rkv = lax.rsqrt(jnp.mean(ckv * ckv, axis=-1, keepdims=True) + EPS)
        ckvh = ckv * rkv
        ckvn = (ckvh * gckv_ref[...]).astype(BF16)
        q0 = _nn(cqn, wuq_ref[...])
        k0 = _nn(ckvn, wuk_ref[...])
        krb = kr_ref[...]
        C, S1, S2 = C_ref[...], S1_ref[...], S2_ref[...]
        gq, gk = gq_ref[...], gk_ref[...]

        def head_bwd(x, dy, g):
            r = lax.rsqrt(_rsum_mxu(x * x) * (1.0 / MLA_QK) + EPS)
            xn = x * r
            dyn = _rope_t(dy, C, S1, S2, 16)
            dxh = dyn * g
            return r * (dxh - xn * _rsum_mxu(dxh * xn) * (1.0 / MLA_QK)), _csum(dyn * xn)

        dq0, dk0 = [], []
        dgq_acc = jnp.zeros((1, LANE), F32)
        dgk_acc = jnp.zeros((1, LANE), F32)
        dkr = jnp.zeros((T, LANE), F32)
        for h in range(8):
            sl = slice(h * LANE, (h + 1) * LANE)
            dxq, gq_p = head_bwd(q0[:, sl], dq_ref[:, sl], gq)
            dxk, gk_p = head_bwd(k0[:, sl] + krb, dk_ref[:, sl], gk)
            dq0.append(dxq.astype(BF16))
            dk0.append(dxk.astype(BF16))
            dkr = dkr + dxk
            dgq_acc = dgq_acc + gq_p
            dgk_acc = dgk_acc + gk_p
        dgq_ref[...] += dgq_acc
        dgk_ref[...] += dgk_acc
        dq0 = jnp.concatenate(dq0, axis=1)
        dk0 = jnp.concatenate(dk0, axis=1)
        dvb = dv_ref[...].astype(BF16)
        dwuq_ref[...] += _tn(cqn, dq0)
        dwuk_ref[...] += _tn(ckvn, dk0)
        dwuv_ref[...] += _tn(ckvn, dvb)
        dcqn = _nt(dq0, wuq_ref[...])
        dckvn = _nt(dk0, wuk_ref[...]) + _nt(dvb, wuv_ref[...])
        dgcq_ref[...] += _csum(dcqn * cqh)
        dgckv_ref[...] += _csum(dckvn * ckvh)
        dxh = dcqn * gcq_ref[...]
        dz_ref[:, 0:256] = (rq * (dxh - cqh * jnp.mean(dxh * cqh, axis=-1, keepdims=True))).astype(BF16)
        dxh = dckvn * gckv_ref[...]
        dz_ref[:, 256:384] = (rkv * (dxh - ckvh * jnp.mean(dxh * ckvh, axis=-1, keepdims=True))).astype(BF16)
        lane = _lane((T, LANE))
        dz_ref[:, 384:512] = jnp.where((lane >= KR_LANE) & (lane < KR_LANE + 32), dkr, 0.0).astype(BF16)

    tabspec = pl.BlockSpec((T, LANE), lambda i: (i, 0))
    in_specs = [_zcol(T, 256, C_CQ), _zcol(T, LANE, C_CKV), _zcol(T, LANE, C_KR),
                pl.BlockSpec((T, 1024), lambda i: (i, 0)), pl.BlockSpec((T, 1024), lambda i: (i, 0)), pl.BlockSpec((T, 512), lambda i: (i, 0)),
                _full((1, 256)), _full((1, LANE)), _full((256, 1024)), _full((LANE, 1024)), _full((LANE, 512)), _full((1, LANE)), _full((1, LANE)),
                tabspec, tabspec, tabspec, pl.BlockSpec(memory_space=pl.ANY)]
    out_specs = [_zcol(T, 512, C_CQ), _full((256, 1024)), _full((LANE, 1024)), _full((LANE, 512)), _full((1, 256)), _full((1, LANE)),
                 _full((1, LANE)), _full((1, LANE))]
    out_shape = [jax.ShapeDtypeStruct(dz.shape, BF16), jax.ShapeDtypeStruct((256, 1024), F32), jax.ShapeDtypeStruct((LANE, 1024), F32),
                 jax.ShapeDtypeStruct((LANE, 512), F32), jax.ShapeDtypeStruct((1, 256), F32), jax.ShapeDtypeStruct((1, LANE), F32),
                 jax.ShapeDtypeStruct((1, LANE), F32), jax.ShapeDtypeStruct((1, LANE), F32)]
    return pl.pallas_call(
        body, name="mla_pre_bwd", grid=(S // T,), in_specs=in_specs, out_specs=out_specs, out_shape=out_shape,
        input_output_aliases={16: 0}, compiler_params=_cp(),
    )(zp, zp, zp, dq, dk, dv, w["g_cq"], w["g_ckv"], w["w_uq"], w["w_uk"], w["w_uv"], w["g_mq"], w["g_mk"], *tab, dz)


T_DIL = 256


def _head_stats(x, lane):
    sq = x * x
    sa = _rsum(jnp.where(lane < 64, sq, 0.0))
    sb = _rsum(jnp.where(lane >= 64, sq, 0.0))
    return lax.rsqrt(jnp.where(lane < 64, sa, sb) * (1.0 / DIL_HD) + EPS)


def _head_sum(x, lane):
    sa = _rsum(jnp.where(lane < 64, x, 0.0))
    sb = _rsum(jnp.where(lane >= 64, x, 0.0))
    return jnp.where(lane < 64, sa, sb)


def _head_stats_mxu(x):
    r = lax.broadcasted_iota(jnp.int32, (LANE, LANE), 0)
    c = lax.broadcasted_iota(jnp.int32, (LANE, LANE), 1)
    ones = jnp.where((r < 64) == (c < 64), 1.0, 0.0).astype(F32)
    ss = lax.dot_general(x * x, ones, (((1,), (0,)), ((), ())), precision=lax.Precision.HIGHEST, preferred_element_type=F32)
    return lax.rsqrt(ss * (1.0 / DIL_HD) + EPS)


def _dil_pre_fwd(zp, w, tab):
    S = zp.shape[0]
    T = T_DIL

    def body(q_ref, k_ref, gq_ref, gk_ref, C_ref, S1_ref, S2_ref, qo_ref, ko_ref):
        C, S1, S2 = C_ref[...], S1_ref[...], S2_ref[...]
        for b in range(12):
            sl = slice(b * LANE, (b + 1) * LANE)
            x = q_ref[:, sl]
            qo_ref[:, sl] = _rope(x * _head_stats_mxu(x) * gq_ref[...], C, S1, S2, 32)
            x = k_ref[:, sl]
            ko_ref[:, sl] = _rope(x * _head_stats_mxu(x) * gk_ref[...], C, S1, S2, 32)

    tabspec = pl.BlockSpec((T, LANE), lambda i: (i, 0))
    out = pl.BlockSpec((T, 1536), lambda i: (i, 0))
    return pl.pallas_call(
        body, name="dil_pre_fwd", grid=(S // T,),
        in_specs=[_zcol(T, 1536, C_DQ), _zcol(T, 1536, C_DK), _full((1, LANE)), _full((1, LANE)), tabspec, tabspec, tabspec],
        out_specs=[out, out], out_shape=[jax.ShapeDtypeStruct((S, 1536), F32)] * 2, compiler_params=_cp(),
    )(zp, zp, w["g_dq"], w["g_dk"], *tab)


DIL_ROWS = 2048


def _dil_geometry(g, S):
    d = DIL_DILATIONS[g]
    P = NK * d
    return d, P, DIL_ROWS // P, S // P


def _dil_rows(start, d, blocks=1):
    return pl.ds(pl.multiple_of(start, NK), blocks * NK) if d == 1 else pl.ds(start, blocks * NK, stride=d)


def _dil_specs(g, S, col0):
    _, P, m, nb = _dil_geometry(g, S)
    cur = pl.BlockSpec((DIL_ROWS, LANE), lambda sb, c: (sb, col0 + c))
    prv = pl.BlockSpec((P, LANE), lambda sb, c: (jnp.maximum(sb * m - 1, 0), col0 + c))
    nxt = pl.BlockSpec((P, LANE), lambda sb, c: (jnp.minimum((sb + 1) * m, nb - 1), col0 + c))
    return cur, prv, nxt


def _dil_attn_fwd(q, k, zp, g):
    S = q.shape[0]
    d, P, m, nb = _dil_geometry(g, S)
    R = DIL_ROWS

    def body(q_ref, kc_ref, kp_ref, vc_ref, vp_ref, o_ref, lse_ref, *scr):
        sb = pl.program_id(0)
        if m > 1:
            ks_ref, vs_ref = scr
            ks_ref[0:P, :] = kp_ref[...]
            ks_ref[P:P + R, :] = kc_ref[...]
            vs_ref[0:P, :] = vp_ref[...]
            vs_ref[P:P + R, :] = vc_ref[...]
        lane = _lane((NK, LANE))

        def unit(u, carry):
            j = u // d
            start = j * P + (u - j * d)
            rows = _dil_rows(start, d)
            if m > 1:
                k2, v2 = ks_ref[_dil_rows(start, d, 2), :], vs_ref[_dil_rows(start, d, 2), :]
            else:
                k2 = jnp.concatenate([kp_ref[rows, :], kc_ref[rows, :]], axis=0)
                v2 = jnp.concatenate([vp_ref[rows, :], vc_ref[rows, :]], axis=0)
            k2, v2 = k2.astype(BF16), v2.astype(BF16)
            q_ = q_ref[rows, :].astype(BF16)
            row = lax.broadcasted_iota(jnp.int32, (NK, 2 * NK), 0)
            col = lax.broadcasted_iota(jnp.int32, (NK, 2 * NK), 1)
            band = (col >= row) & (col <= row + NK) & ((col >= NK) | (sb * m + j > 0))
            lane2 = _lane((2 * NK, LANE))
            zb, zv = jnp.zeros_like(q_), jnp.zeros_like(v2)
            o_tot = jnp.zeros((NK, LANE), F32)
            lse_tot = jnp.zeros((NK, LANE), F32)
            for hh in range(2):
                hm = (lane < 64) if hh == 0 else (lane >= 64)
                hm2 = (lane2 < 64) if hh == 0 else (lane2 >= 64)
                s_ = jnp.where(band, _nt(jnp.where(hm, q_, zb), k2) * DIL_SCALE, NEG)
                mx = jnp.max(s_, axis=-1, keepdims=True)
                e = jnp.exp(s_ - mx)
                den = _rsum(e)
                o_tot = o_tot + _nn(e.astype(BF16), jnp.where(hm2, v2, zv)) / den
                lse_tot = jnp.where(hm, mx + jnp.log(den), lse_tot)
            o_ref[rows, :] = o_tot
            lse_ref[rows, :] = lse_tot
            return carry

        lax.fori_loop(0, R // NK, unit, 0, unroll=8)

    qcur, qprv, _ = _dil_specs(g, S, 4 * g)
    vcur, vprv, _ = _dil_specs(g, S, C_DV + 4 * g)
    out = pl.BlockSpec((R, LANE), lambda sb, c: (sb, c))
    return pl.pallas_call(
        body, name=f"dil_attn_fwd{g}", grid=(S // R, 4), in_specs=[qcur, qcur, qprv, vcur, vprv], out_specs=[out, out],
        out_shape=[jax.ShapeDtypeStruct((S, 512), F32)] * 2,
        scratch_shapes=[pltpu.VMEM((P + R, LANE), F32)] * 2 if m > 1 else [], compiler_params=_cp(),
    )(q, k, k, zp, zp)


def _dil_combine(os_, ls_, zp):
    S = zp.shape[0]
    T = T_ROW

    def body(o0, o1, o2, l0, l1, l2, g_ref, oc_ref, L_ref, y_ref):
        a, b, c = l0[...], l1[...], l2[...]
        mx = jnp.maximum(jnp.maximum(a, b), c)
        ea, eb, ec = jnp.exp(a - mx), jnp.exp(b - mx), jnp.exp(c - mx)
        den = ea + eb + ec
        oc = (ea * o0[...] + eb * o1[...] + ec * o2[...]) / den
        oc_ref[...] = oc
        L_ref[...] = mx + jnp.log(den)
        y_ref[...] = (oc * _silu(g_ref[...])).astype(BF16)

    row = pl.BlockSpec((T, 512), lambda i: (i, 0))
    return pl.pallas_call(
        body, name="dil_combine", grid=(S // T,), in_specs=[row] * 6 + [_zcol(T, 512, C_DILG)], out_specs=[row, row, row],
        out_shape=[jax.ShapeDtypeStruct((S, 512), F32), jax.ShapeDtypeStruct((S, 512), F32), jax.ShapeDtypeStruct((S, 512), BF16)],
        compiler_params=_cp(),
    )(*os_, *ls_, zp)


def _dil_comb_bwd(zp, oc, dy, dz):
    S = zp.shape[0]
    T = T_ROW

    def body(g_ref, o_ref, dy_ref, dz_in, dz_ref, do_ref, D_ref):
        del dz_in
        g, o_, dy_ = g_ref[...], o_ref[...], dy_ref[...]
        do = dy_ * _silu(g)
        do_ref[...] = do
        dz_ref[...] = (dy_ * o_ * _dsilu(g)).astype(BF16)
        lane = _lane((T, LANE))
        for p in range(4):
            sl = slice(p * LANE, (p + 1) * LANE)
            D_ref[:, sl] = _head_sum(do[:, sl] * o_[:, sl], lane)

    row = pl.BlockSpec((T, 512), lambda i: (i, 0))
    zc = _zcol(T, 512, C_DILG)
    return pl.pallas_call(
        body, name="dil_comb_bwd", grid=(S // T,), in_specs=[zc, row, row, pl.BlockSpec(memory_space=pl.ANY)], out_specs=[zc, row, row],
        out_shape=[jax.ShapeDtypeStruct(dz.shape, BF16), jax.ShapeDtypeStruct((S, 512), F32), jax.ShapeDtypeStruct((S, 512), F32)],
        input_output_aliases={3: 0}, compiler_params=_cp(),
    )(zp, oc, dy, dz)


def _dil_attn_bwd(q, k, zp, do, L, Dr, g):
    S = q.shape[0]
    d, P, m, nb = _dil_geometry(g, S)
    R = DIL_ROWS
    n_q, n_k = 4, 2

    def body(*refs):
        q_side = refs[0:2 * n_q]
        k_side = refs[2 * n_q:2 * n_q + 2 * n_k]
        dq_ref, dk_ref, dv_ref = refs[2 * n_q + 2 * n_k:2 * n_q + 2 * n_k + 3]
        scr = refs[2 * n_q + 2 * n_k + 3:]
        sb = pl.program_id(0)
        if m > 1:
            for a in range(n_q):
                scr[a][0:R, :] = q_side[2 * a][...]
                scr[a][R:R + P, :] = q_side[2 * a + 1][...]
            for a in range(n_k):
                scr[n_q + a][0:P, :] = k_side[2 * a + 1][...]
                scr[n_q + a][P:P + R, :] = k_side[2 * a][...]
        lane = _lane((NK, LANE))

        def unit(u, carry):
            j = u // d
            start = j * P + (u - j * d)
            rows = _dil_rows(start, d)
            if m > 1:
                rows_b = _dil_rows(start + P, d)
                q2, do2, L2, D2 = [scr[a][_dil_rows(start, d, 2), :] for a in range(n_q)]
                kp, vp = [scr[n_q + a][rows, :] for a in range(n_k)]
                kc, vc = [scr[n_q + a][rows_b, :] for a in range(n_k)]
            else:
                q2, do2, L2, D2 = [jnp.concatenate([q_side[2 * a][rows, :], q_side[2 * a + 1][rows, :]], axis=0) for a in range(n_q)]
                kc, vc = [k_side[2 * a][rows, :] for a in range(n_k)]
                kp, vp = [k_side[2 * a + 1][rows, :] for a in range(n_k)]
            q2, do2 = q2.astype(BF16), do2.astype(BF16)
            kc, kp, vc, vp = kc.astype(BF16), kp.astype(BF16), vc.astype(BF16), vp.astype(BF16)
            n = sb * m + j
            hA = _lane((2 * NK, LANE)) < 64
            zq = jnp.zeros_like(q2)
            L2r, D2r = pltpu.roll(L2, 64, 1), pltpu.roll(D2, 64, 1)
            Q4 = jnp.concatenate([jnp.where(hA, q2, zq), jnp.where(hA, zq, q2)], axis=0)
            O4 = jnp.concatenate([jnp.where(hA, do2, zq), jnp.where(hA, zq, do2)], axis=0)
            L4 = jnp.concatenate([jnp.where(hA, L2, L2r), jnp.where(hA, L2r, L2)], axis=0)
            D4 = jnp.concatenate([jnp.where(hA, D2, D2r), jnp.where(hA, D2r, D2)], axis=0)
            row4 = lax.broadcasted_iota(jnp.int32, (4 * NK, NK), 0) & (2 * NK - 1)
            col4 = lax.broadcasted_iota(jnp.int32, (4 * NK, NK), 1)
            m4 = ((row4 < NK) & (col4 <= row4)) | ((row4 >= NK) & (col4 >= row4 - NK) & (n < nb - 1))
            p4 = jnp.exp(jnp.where(m4, _nt(Q4, kc) * DIL_SCALE, NEG) - L4)
            ds4 = (p4 * (_nt(O4, vc) - D4) * DIL_SCALE).astype(BF16)
            dk_tot = _tn(ds4, Q4)
            dv_tot = _tn(p4.astype(BF16), O4)
            pick = lambda x: jnp.concatenate([x[0:NK], x[2 * NK:3 * NK]], axis=0)
            Qn, On, Ln, Dn = pick(Q4), pick(O4), pick(L4), pick(D4)
            rowp = lax.broadcasted_iota(jnp.int32, (2 * NK, NK), 0) & (NK - 1)
            colp = lax.broadcasted_iota(jnp.int32, (2 * NK, NK), 1)
            pp = jnp.exp(jnp.where((colp >= rowp) & (n > 0), _nt(Qn, kp) * DIL_SCALE, NEG) - Ln)
            dsp = (pp * (_nt(On, vp) - Dn) * DIL_SCALE).astype(BF16)
            dq2 = _nn(pick(ds4), kc) + _nn(dsp, kp)
            dq_tot = jnp.where(lane < 64, dq2[0:NK], dq2[NK:2 * NK])
            dq_ref[rows, :] = dq_tot
            dk_ref[rows, :] = dk_tot
            dv_ref[rows, :] = dv_tot
            return carry

        lax.fori_loop(0, R // NK, unit, 0, unroll=8)

    qcur, qprv, qnxt = _dil_specs(g, S, 4 * g)
    vcur, vprv, _ = _dil_specs(g, S, C_DV + 4 * g)
    ocur, _, onxt = _dil_specs(g, S, 0)
    out = pl.BlockSpec((R, LANE), lambda sb, c: (sb, c))
    scratch = [pltpu.VMEM((P + R, LANE), F32)] * (n_q + n_k) if m > 1 else []
    return pl.pallas_call(
        body, name=f"dil_attn_bwd{g}", grid=(S // R, 4),
        in_specs=[qcur, qnxt, ocur, onxt, ocur, onxt, ocur, onxt, qcur, qprv, vcur, vprv],
        out_specs=[out, out, out], out_shape=[jax.ShapeDtypeStruct((S, 512), F32)] * 3, scratch_shapes=scratch, compiler_params=_cp(),
    )(q, q, do, do, L, L, Dr, Dr, k, k, zp, zp)


def _dil_pre_bwd(zp, dys, g, tab, dz, col, name):
    S = zp.shape[0]
    T = T_DIL

    def body(x_ref, dy0_ref, dy1_ref, dy2_ref, g_ref, C_ref, S1_ref, S2_ref, dz_in, dz_ref, dg_ref):
        del dz_in
        i = pl.program_id(0)
        C, S1, S2 = C_ref[...], S1_ref[...], S2_ref[...]
        lane = _lane((T, LANE))
        gv = g_ref[...]
        acc = jnp.zeros((1, LANE), F32)
        for b in range(12):
            sl = slice(b * LANE, (b + 1) * LANE)
            x = x_ref[:, sl]
            r = _head_stats(x, lane)
            xn = x * r
            dy_ref = (dy0_ref, dy1_ref, dy2_ref)[b // 4]
            dyn = _rope_t(dy_ref[:, (b % 4) * LANE:(b % 4 + 1) * LANE], C, S1, S2, 32)
            acc = acc + _csum(dyn * xn)
            dxh = dyn * gv
            dz_ref[:, sl] = (r * (dxh - xn * _head_sum(dxh * xn, lane) * (1.0 / DIL_HD))).astype(BF16)

        @pl.when(i == 0)
        def _():
            dg_ref[...] = acc

        @pl.when(i > 0)
        def _():
            dg_ref[...] += acc

    tabspec = pl.BlockSpec((T, LANE), lambda i: (i, 0))
    zc = _zcol(T, 1536, col)
    grp = pl.BlockSpec((T, 512), lambda i: (i, 0))
    return pl.pallas_call(
        body, name=name, grid=(S // T,),
        in_specs=[zc, grp, grp, grp, _full((1, LANE)), tabspec, tabspec, tabspec, pl.BlockSpec(memory_space=pl.ANY)],
        out_specs=[zc, _full((1, LANE))], out_shape=[jax.ShapeDtypeStruct(dz.shape, BF16), jax.ShapeDtypeStruct((1, LANE), F32)],
        input_output_aliases={8: 0}, compiler_params=_cp(),
    )(zp, *dys, g, *tab, dz)


def _dil_dv_into(dvs, dz):
    S = dz.shape[0]
    T = T_ROW

    def body(s0, s1, s2, dz_in, o_ref):
        del dz_in
        for gi, s in enumerate((s0, s1, s2)):
            o_ref[:, gi * 512:(gi + 1) * 512] = s[...].astype(BF16)

    grp = pl.BlockSpec((T, 512), lambda i: (i, 0))
    return pl.pallas_call(
        body, name="dil_dv", grid=(S // T,), in_specs=[grp, grp, grp, pl.BlockSpec(memory_space=pl.ANY)],
        out_specs=_zcol(T, 1536, C_DV), out_shape=jax.ShapeDtypeStruct(dz.shape, BF16), input_output_aliases={3: 0}, compiler_params=_cp(),
    )(*dvs, dz)


T_MRG = 256


def _merge_fwd(P, zp, b_merge):
    S = zp.shape[0]
    T = T_MRG

    def body(p0, p1, p2, m0, m1, m2, b_ref, o_ref):
        acc = jnp.zeros((T, D), F32)
        for j, (p, m) in enumerate(((p0, m0), (p1, m1), (p2, m2))):
            acc = acc + _sig(m[...] + b_ref[:, j * D:(j + 1) * D]) * p[...]
        o_ref[...] = acc.astype(BF16)

    row = pl.BlockSpec((T, D), lambda i: (i, 0))
    return pl.pallas_call(
        body, name="merge_fwd", grid=(S // T,),
        in_specs=[row, row, row] + [_zcol(T, D, C_MERGE + 8 * j) for j in range(3)] + [_full((1, 3 * D))], out_specs=row,
        out_shape=jax.ShapeDtypeStruct((S, D), BF16), compiler_params=_cp(),
    )(*P, zp, zp, zp, b_merge)


def _merge_bwd(dm, Pj, zp, bj, dz, j):
    S = zp.shape[0]
    T = T_MRG

    def body(dm_ref, p_ref, m_ref, b_ref, dz_in, dz_ref, dp_ref, db_ref):
        del dz_in
        i = pl.program_id(0)
        g = _sig(m_ref[...] + b_ref[...])
        dmv = dm_ref[...]
        dp_ref[...] = (dmv * g).astype(BF16)
        dg = dmv * p_ref[...] * g * (1.0 - g)
        dz_ref[...] = dg.astype(BF16)
        part = _csum(dg)

        @pl.when(i == 0)
        def _():
            db_ref[...] = part

        @pl.when(i > 0)
        def _():
            db_ref[...] += part

    row = pl.BlockSpec((T, D), lambda i: (i, 0))
    zc = _zcol(T, D, C_MERGE + 8 * j)
    return pl.pallas_call(
        body, name=f"merge_bwd{j}", grid=(S // T,), in_specs=[row, row, zc, _full((1, D)), pl.BlockSpec(memory_space=pl.ANY)],
        out_specs=[zc, row, _full((1, D))],
        out_shape=[jax.ShapeDtypeStruct(dz.shape, BF16), jax.ShapeDtypeStruct((S, D), BF16), jax.ShapeDtypeStruct((1, D), F32)],
        input_output_aliases={4: 0}, compiler_params=_cp(),
    )(dm, Pj, zp, bj, dz)


def _loss_fwd_bwd(y, target):
    S = y.shape[0]
    T = T_ROW

    def body(y_ref, t_ref, loss_ref, dy_ref):
        i = pl.program_id(0)
        err = y_ref[...] - t_ref[...]
        dy_ref[...] = err * (1.0 / D)
        part = jnp.sum(err * err, keepdims=True).reshape(1, 1) * (0.5 / D)

        @pl.when(i == 0)
        def _():
            loss_ref[...] = part

        @pl.when(i > 0)
        def _():
            loss_ref[...] += part

    row = pl.BlockSpec((T, D), lambda i: (i, 0))
    return pl.pallas_call(
        body, name="loss", grid=(S // T,), in_specs=[row, row], out_specs=[_full((1, 1)), row],
        out_shape=[jax.ShapeDtypeStruct((1, 1), F32), jax.ShapeDtypeStruct((S, D), F32)], compiler_params=_cp(),
    )(y, target)


def _layer_fwd(x, w, tabs):
    mla_tab, dil_tab = tabs
    S = x.shape[0]
    h = _rms_in_fwd(x, w["norm_g"])
    zp = _mm(h, w["w_in"], mode="nn", name="in_proj")
    hs, y_lru = _lru_fwd(zp, w)
    q, k, v = _mla_pre_fwd(zp, w, mla_tab)
    o_mla, lse, y_mla = _mla_attn_fwd(q, k, v, zp)
    qd, kd = _dil_pre_fwd(zp, w, dil_tab)
    og, lg = zip(*[_dil_attn_fwd(qd, kd, zp, g) for g in range(len(DIL_DILATIONS))])
    oc, L, y_dil = _dil_combine(og, lg, zp)
    P = [_mm(y_lru, w["w_lru_o"], mode="nn", name="lru_out"), _mm(y_mla, w["w_mla_o"], mode="nn", name="mla_out"),
         _mm(y_dil, w["w_dil_o"], mode="nn", name="dil_out")]
    merged = _merge_fwd(P, zp, w["b_merge"])
    x_out = _mm(merged, w["w_out"], mode="nn", name="out_proj", add=x)
    saved = dict(x=x, h=h, zp=zp, hs=hs, y=(y_lru, y_mla, y_dil), q=q, k=k, v=v, o_mla=o_mla, lse=lse, qd=qd, kd=kd, oc=oc, L=L, P=P,
                 merged=merged)
    return x_out, saved


def _layer_bwd(dout, w, tabs, sv, hook=None, after=None):
    mla_tab, dil_tab = tabs
    zp = sv["zp"]
    S = zp.shape[0]
    g = {}
    dm = _mm(dout, w["w_out"], mode="nt", name="d_merged", after=after)
    g["w_out"] = _mm(sv["merged"], dout, mode="tn", name="dw_out", out_dtype=BF16)
    dz = lax.empty((S, ZW), BF16)
    dP, db = [], []
    for j in range(3):
        dz, dpj, dbj = _merge_bwd(dm, sv["P"][j], zp, w["b_merge"][:, j * D:(j + 1) * D], dz, j)
        dP.append(dpj)
        db.append(dbj)
    g["b_merge"] = jnp.concatenate(db, axis=1)
    names = ("w_lru_o", "w_mla_o", "w_dil_o")
    dy = []
    for j in range(3):
        dy.append(_mm(dP[j], w[names[j]], mode="nt", name="dy_" + names[j]))
        g[names[j]] = _mm(sv["y"][j], dP[j], mode="tn", name="d" + names[j], out_dtype=BF16, tk=S)
    dz = _lru_gate_bwd(zp, sv["hs"], dy[0], dz)
    dz, g["conv_w"], g["conv_b"], g["w_gx"], g["b_gx"], g["w_ga"], g["b_ga"], g["lam"] = _lru_bwd(zp, sv["hs"], dy[0], w, dz)
    dz, do, Dr = _mla_post_bwd(zp, sv["o_mla"], dy[1], dz)
    dq, dk, dv = _mla_attn_bwd(sv["q"], sv["k"], sv["v"], do, sv["lse"], Dr)
    dz, g["w_uq"], g["w_uk"], g["w_uv"], g["g_cq"], g["g_ckv"], g["g_mq"], g["g_mk"] = _mla_pre_bwd(zp, dq, dk, dv, w, mla_tab, dz)
    dz, dod, Dd = _dil_comb_bwd(zp, sv["oc"], dy[2], dz)
    dqs, dks, dvs = zip(*[_dil_attn_bwd(sv["qd"], sv["kd"], zp, dod, sv["L"], Dd, gi) for gi in range(len(DIL_DILATIONS))])
    dz, g["g_dq"] = _dil_pre_bwd(zp, dqs, w["g_dq"], dil_tab, dz, C_DQ, "dil_pre_bwd_q")
    dz, g["g_dk"] = _dil_pre_bwd(zp, dks, w["g_dk"], dil_tab, dz, C_DK, "dil_pre_bwd_k")
    dz = _dil_dv_into(dvs, dz)
    g["w_in"] = _mm(sv["h"], dz, mode="tn", name="dw_in", out_dtype=BF16, tk=S)
    token = hook(g) if hook is not None else None
    dh = _mm(dz, w["w_in"], mode="nt", name="d_h", after=token, tk=ZW // 4)
    dx, g["norm_g"] = _rms_in_bwd(sv["x"], w["norm_g"], dh, dout)
    return dx, g


def _peers():
    mx, my, mc = lax.axis_index("x"), lax.axis_index("y"), lax.axis_index("c")
    me = 4 * mx + 2 * my + mc
    out = []
    for k in range(1, N_DEV):
        px = 1 - mx if k & 4 else mx
        py = 1 - my if k & 2 else my
        pc = 1 - mc if k & 1 else mc
        out.append(((px, py, pc), 4 * px + 2 * py + pc))
    return me, out


def _whole(ref, p):
    del p
    return ref


def _exchange(srcs, slicers, slices, name):
    n = len(srcs)

    def body(*refs):
        ins, outs = refs[:n], refs[n:2 * n]
        send_sems, recv_sems, local_sems = refs[2 * n:]
        me, peers = _peers()
        mine = [pltpu.make_async_copy(slicers[a](ins[a], me), outs[a].at[me], local_sems.at[a]) for a in range(n)]
        for cp in mine:
            cp.start()
        copies = []
        for k, (peer, pidx) in enumerate(peers):
            for a in range(n):
                cp = pltpu.make_async_remote_copy(
                    src_ref=slicers[a](ins[a], pidx), dst_ref=outs[a].at[me], send_sem=send_sems.at[k * n + a],
                    recv_sem=recv_sems.at[k * n + a], device_id=peer, device_id_type=pl.DeviceIdType.MESH)
                cp.start()
                copies.append(cp)
        for cp in copies + mine:
            cp.wait()

    nsem = (N_DEV - 1) * n
    return pl.pallas_call(
        body, name=name, out_shape=[jax.ShapeDtypeStruct((N_DEV,) + shp, dt) for shp, dt in slices],
        in_specs=[pl.BlockSpec(memory_space=pl.ANY)] * n, out_specs=[pl.BlockSpec(memory_space=pl.ANY)] * n,
        scratch_shapes=[pltpu.SemaphoreType.DMA((nsem,)), pltpu.SemaphoreType.DMA((nsem,)), pltpu.SemaphoreType.DMA((n,))],
        compiler_params=pltpu.CompilerParams(has_side_effects=True),
    )(*srcs)


def _gather_two_level(srcs, name):
    n = len(srcs)

    def body(*refs):
        ins, outs = refs[:n], refs[n:2 * n]
        send_sems, recv_sems, local_sems = refs[2 * n:]
        mx, my, mc = lax.axis_index("x"), lax.axis_index("y"), lax.axis_index("c")
        me, sibling = (mx, my, mc), (mx, my, 1 - mc)
        chips = [(1 - mx, my), (mx, 1 - my), (1 - mx, 1 - my)]
        slot = lambda d: 4 * d[0] + 2 * d[1] + d[2]

        def copy(j, a, block, to, own=False):
            return pltpu.make_async_remote_copy(
                src_ref=ins[a] if own else outs[a].at[slot(block)], dst_ref=outs[a].at[slot(block)],
                send_sem=send_sems.at[j * n + a], recv_sem=recv_sems.at[j * n + a], device_id=to, device_id_type=pl.DeviceIdType.MESH)

        mine = [pltpu.make_async_copy(ins[a], outs[a].at[slot(me)], local_sems.at[a]) for a in range(n)]
        first = [copy(1 + j, a, me, (*chip, mc), own=True) for j, chip in enumerate(chips) for a in range(n)]
        first += [copy(0, a, me, sibling, own=True) for a in range(n)]
        for cp in mine + first:
            cp.start()
        passed = []
        for j, chip in enumerate(chips):
            for a in range(n):
                copy(1 + j, a, (*chip, mc), me).wait_recv()
                cp = copy(4 + j, a, (*chip, mc), sibling)
                cp.start()
                passed.append(cp)
        for a in range(n):
            copy(0, a, sibling, me).wait_recv()
        for j, chip in enumerate(chips):
            for a in range(n):
                copy(4 + j, a, (*chip, 1 - mc), me).wait_recv()
        for cp in first + passed:
            cp.wait_send()
        for cp in mine:
            cp.wait()

    nsem = (N_DEV - 1) * n
    return pl.pallas_call(
        body, name=name, out_shape=[jax.ShapeDtypeStruct((N_DEV,) + a.shape, a.dtype) for a in srcs],
        in_specs=[pl.BlockSpec(memory_space=pl.ANY)] * n, out_specs=[pl.BlockSpec(memory_space=pl.ANY)] * n,
        scratch_shapes=[pltpu.SemaphoreType.DMA((nsem,)), pltpu.SemaphoreType.DMA((nsem,)), pltpu.SemaphoreType.DMA((n,))],
        compiler_params=pltpu.CompilerParams(has_side_effects=True),
    )(*srcs)


_HBM = pl.BlockSpec(memory_space=pltpu.HBM)
_SEM = pl.BlockSpec(memory_space=pltpu.SEMAPHORE)
_DATAFLOW = pltpu.SideEffectType.DATAFLOW_SIDE_EFFECTING


def _plan_chips():
    mx, my, mc = lax.axis_index("x"), lax.axis_index("y"), lax.axis_index("c")
    return 2 * mx + my, [((cx, cy, mc), 2 * cx + cy) for cx, cy in ((1 - mx, my), (mx, 1 - my), (1 - mx, 1 - my))]


def _pair_exchange(srcs, slicers, slices, sliced, name):
    n = len(srcs)
    pieces = [4 if s else 1 for s in sliced]

    def body(*refs):
        ins, outs = refs[:n], refs[n:2 * n]
        send_sems, recv_sems = refs[2 * n:]
        mx, my, mc = lax.axis_index("x"), lax.axis_index("y"), lax.axis_index("c")
        copies = []
        for a in range(n):
            for q in range(pieces[a]):
                i = len(copies)
                copies.append(pltpu.make_async_remote_copy(
                    src_ref=slicers[a](ins[a], 2 * q + 1 - mc) if sliced[a] else ins[a], dst_ref=outs[a].at[q],
                    send_sem=send_sems.at[i], recv_sem=recv_sems.at[i], device_id=(mx, my, 1 - mc), device_id_type=pl.DeviceIdType.MESH))
        for cp in copies:
            cp.start()
        for cp in copies:
            cp.wait()

    return pl.pallas_call(
        body, name=name, out_shape=[jax.ShapeDtypeStruct((p,) + shp, dt) for (shp, dt), p in zip(slices, pieces)],
        in_specs=[pl.BlockSpec(memory_space=pl.ANY)] * n, out_specs=[pl.BlockSpec(memory_space=pl.ANY)] * n,
        scratch_shapes=[pltpu.SemaphoreType.DMA((sum(pieces),)), pltpu.SemaphoreType.DMA((sum(pieces),))],
        compiler_params=pltpu.CompilerParams(has_side_effects=True),
    )(*srcs)


def _pair_add(src, came, first_blk, axis, name):
    _, r, c = came.shape
    nblk = (c if axis == 1 else r) // LANE
    if axis == 1:
        s_spec = pl.BlockSpec((r, LANE), lambda q, j, fb: (0, fb[q] + j))
        o_spec = pl.BlockSpec((1, r, LANE), lambda q, j, fb: (q, 0, j))
    else:
        s_spec = pl.BlockSpec((LANE, c), lambda q, j, fb: (fb[q] + j, 0))
        o_spec = pl.BlockSpec((1, LANE, c), lambda q, j, fb: (q, j, 0))

    def body(fb_ref, x_ref, y_ref, o_ref):
        del fb_ref
        o_ref[0] = (x_ref[...].astype(F32) + y_ref[0].astype(F32)).astype(o_ref.dtype)

    return pl.pallas_call(
        body, name=name, out_shape=jax.ShapeDtypeStruct(came.shape, came.dtype),
        grid_spec=pltpu.PrefetchScalarGridSpec(num_scalar_prefetch=1, grid=(4, nblk), in_specs=[s_spec, o_spec], out_specs=o_spec),
        compiler_params=_cp(),
    )(first_blk, src, came)


def _add2(x, y, name):
    shp = x.shape
    x, y = x.reshape(-1, shp[-1]), y.reshape(-1, shp[-1])
    R, C = x.shape
    tr = R
    while tr * C * 4 > (1 << 21) and tr % 32 == 0:
        tr //= 2

    def body(x_ref, y_ref, o_ref):
        o_ref[...] = (x_ref[...].astype(F32) + y_ref[...].astype(F32)).astype(o_ref.dtype)

    spec = pl.BlockSpec((tr, C), lambda i: (i, 0))
    return pl.pallas_call(body, name=name, grid=(R // tr,), in_specs=[spec, spec], out_specs=spec,
                          out_shape=jax.ShapeDtypeStruct((R, C), x.dtype), compiler_params=_cp())(x, y).reshape(shp)


def _exchange_start(srcs, slicers, slices, after, name, plan=_peers, nslots=N_DEV):
    n = len(srcs)
    nsem = (nslots - 1) * n
    lands = [lax.empty((nslots,) + shp, dt) for shp, dt in slices]

    def body(*refs):
        ins, lands_in = refs[:n], refs[n:2 * n]
        send_sems, recv_sems, local_sems = refs[2 * n + 1], refs[2 * n + 2], refs[2 * n + 3]
        token = refs[-1]
        me, peers = plan()
        for a in range(n):
            pltpu.make_async_copy(slicers[a](ins[a], me), lands_in[a].at[me], local_sems.at[a]).start()
        for k, (peer, pidx) in enumerate(peers):
            for a in range(n):
                pltpu.make_async_remote_copy(
                    src_ref=slicers[a](ins[a], pidx), dst_ref=lands_in[a].at[me], send_sem=send_sems.at[k * n + a],
                    recv_sem=recv_sems.at[k * n + a], device_id=peer, device_id_type=pl.DeviceIdType.MESH).start()
        token[...] = jnp.zeros_like(token)

    hbm = lambda a: pltpu.with_memory_space_constraint(a, pltpu.HBM)
    return pl.pallas_call(
        body, name=name,
        out_shape=(pltpu.SemaphoreType.DMA((nsem,)), pltpu.SemaphoreType.DMA((nsem,)), pltpu.SemaphoreType.DMA((n,)),
                   *[pltpu.HBM(a.shape, a.dtype) for a in srcs], *[pltpu.HBM(a.shape, a.dtype) for a in lands],
                   jax.ShapeDtypeStruct((SUB, LANE), F32)),
        in_specs=[_HBM] * (2 * n) + [pl.BlockSpec(memory_space=pl.ANY)],
        out_specs=(_SEM, _SEM, _SEM, *[_HBM] * (2 * n), pl.BlockSpec(memory_space=pltpu.VMEM)),
        input_output_aliases={i: 3 + i for i in range(2 * n)},
        compiler_params=pltpu.CompilerParams(has_side_effects=_DATAFLOW),
    )(*[hbm(a) for a in srcs], *[hbm(a) for a in lands], after)


def _exchange_wait(started, slicers, after, name, plan=_peers):
    n = (len(started) - 4) // 2
    sems, thru = started[0:3], started[3:3 + 2 * n]

    def body(*refs):
        srcs, lands = refs[:n], refs[n:2 * n]
        send_sems, recv_sems, local_sems = refs[2 * n], refs[2 * n + 1], refs[2 * n + 2]
        me, peers = plan()
        for k, (peer, pidx) in enumerate(peers):
            for a in range(n):
                cp = pltpu.make_async_remote_copy(
                    src_ref=slicers[a](srcs[a], pidx), dst_ref=lands[a].at[me], send_sem=send_sems.at[k * n + a],
                    recv_sem=recv_sems.at[k * n + a], device_id=peer, device_id_type=pl.DeviceIdType.MESH)
                cp.wait_send()
                cp.wait_recv()
        for a in range(n):
            pltpu.make_async_copy(slicers[a](srcs[a], me), lands[a].at[me], local_sems.at[a]).wait()

    outs = pl.pallas_call(
        body, name=name, out_shape=[pltpu.HBM(a.shape, a.dtype) for a in thru],
        in_specs=[_HBM] * (2 * n) + [_SEM, _SEM, _SEM, pl.BlockSpec(memory_space=pl.ANY)], out_specs=[_HBM] * (2 * n),
        input_output_aliases={i: i for i in range(2 * n)}, compiler_params=pltpu.CompilerParams(has_side_effects=_DATAFLOW),
    )(*thru, *sems, after)
    return outs[n:]


WIN = 13 * LANE


def _win_base(s):
    n = s * SHARD_IN
    a0 = n + jnp.where(n >= _KR0, KR_LANE, 0) + jnp.where(n >= _KR0 + 32, 32, 0)
    return jnp.minimum(a0 // LANE, (ZW - WIN) // LANE)


def _win_offsets(s):
    n = s * SHARD_IN + jnp.arange(SHARD_IN)
    o = s * SHARD_IN - _win_base(s) * LANE
    return n, (o, o + KR_LANE, o + LANE - 32)


def _to_window(shard, s):
    _, offs = _win_offsets(s)
    padded = jnp.pad(shard, ((0, 0), (0, 0), (WIN, WIN)))
    a, b, c = [lax.dynamic_slice(padded, (0, 0, WIN - o), shard.shape[:2] + (WIN,)) for o in offs]
    col = (_win_base(s) * LANE + jnp.arange(WIN))[None, None, :]
    zero = jnp.zeros_like(a)
    return jnp.where(col < _KR0, a, jnp.where((col >= _KR0 + KR_LANE) & (col < _KR0 + KR_LANE + 32), b, jnp.where(col >= _KR0 + LANE, c, zero)))


def _from_window(win, s):
    n, offs = _win_offsets(s)
    a, b, c = [lax.dynamic_slice(win, (0, 0, o), win.shape[:2] + (SHARD_IN,)) for o in offs]
    return jnp.where((n < _KR0)[None, None, :], a, jnp.where((n < _KR0 + 32)[None, None, :], b, c))


def _win_base_static(s):
    n = s * SHARD_IN
    a0 = n + (KR_LANE if n >= _KR0 else 0) + (32 if n >= _KR0 + 32 else 0)
    return min(a0 // LANE, (ZW - WIN) // LANE)


def _assemble_w_in(gw):
    tr = 128
    bases = [_win_base_static(s) for s in range(N_DEV)]

    def body(g_ref, o_ref):
        for j in range(ZW // LANE):
            acc = None
            for s in range(N_DEV):
                if bases[s] <= j < bases[s] + WIN // LANE:
                    piece = g_ref[s, :, (j - bases[s]) * LANE:(j - bases[s] + 1) * LANE]
                    acc = piece if acc is None else acc + piece
            o_ref[:, j * LANE:(j + 1) * LANE] = acc

    return pl.pallas_call(
        body, name="assemble_w_in", grid=(D // tr,), in_specs=[pl.BlockSpec((N_DEV, tr, WIN), lambda i: (0, i, 0))],
        out_specs=pl.BlockSpec((tr, ZW), lambda i: (i, 0)), out_shape=jax.ShapeDtypeStruct((D, ZW), gw.dtype), compiler_params=_cp(),
    )(gw)


def _cols(width):
    return lambda ref, p: ref.at[:, pl.ds(pl.multiple_of(p * width, width), width)]


def _rows(height):
    return lambda ref, p: ref.at[pl.ds(pl.multiple_of(p * height, height), height), :]


SCATTER = {
    'w_in': (lambda ref, p: ref.at[:, pl.ds(pl.multiple_of(_win_base(p) * LANE, LANE), WIN)], (D, WIN), BF16),
    'conv_w': (_cols(LANE), (4, LANE), F32),
    'w_lru_o': (_rows(LANE), (LANE, D), BF16),
    'w_uq': (_cols(LANE), (256, LANE), F32),
    'w_ukv': (_cols(LANE), (128, LANE), F32),
    'w_mla_o': (_cols(LANE), (512, LANE), BF16),
    'w_dil_o': (_cols(LANE), (512, LANE), BF16),
    'w_out': (_rows(LANE), (LANE, D), BF16),
}
SLICED_AXIS = {'w_in': 1, 'conv_w': 1, 'w_lru_o': 0, 'w_uq': 1, 'w_ukv': 1, 'w_mla_o': 1, 'w_dil_o': 1, 'w_out': 0}


PACK_ROWS = 64


def _packed_rows(shapes):
    n = sum(int(np.prod(s)) for s in shapes)
    return -(-n // (PACK_ROWS * LANE)) * PACK_ROWS


def _sum8(buf, name):
    ns, R, C = buf.shape
    tr = R
    while tr * C * 4 * ns > (1 << 22) and tr % 32 == 0:
        tr //= 2

    def body(b_ref, o_ref):
        acc = b_ref[0].astype(F32)
        for s in range(1, ns):
            acc = acc + b_ref[s].astype(F32)
        o_ref[...] = acc

    return pl.pallas_call(
        body, name=name, grid=(R // tr,), in_specs=[pl.BlockSpec((ns, tr, C), lambda i: (0, i, 0))],
        out_specs=pl.BlockSpec((tr, C), lambda i: (i, 0)), out_shape=jax.ShapeDtypeStruct((R, C), F32), compiler_params=_cp(),
    )(buf)


def _pack(arrs, dtype, lead):
    flat = [a.astype(dtype).reshape(a.shape[:lead] + (-1,)) for a in arrs]
    cat = jnp.concatenate(flat, axis=-1)
    n = cat.shape[-1]
    unit = PACK_ROWS * LANE
    pad = (-n) % unit
    if pad:
        cat = jnp.pad(cat, [(0, 0)] * lead + [(0, pad)])
    return cat.reshape(cat.shape[:lead] + ((n + pad) // LANE, LANE))


def _unpack(buf, shapes, lead):
    flat = buf.reshape(buf.shape[:lead] + (-1,))
    out, off = [], 0
    for shp in shapes:
        n = int(np.prod(shp))
        out.append(flat[..., off:off + n].reshape(buf.shape[:lead] + tuple(shp)))
        off += n
    return out


def _adamw(w, g, m, v, name):
    layers, rows, cols = w.shape
    tr = rows
    while tr * cols * 4 > (3 << 19) and tr % 16 == 0:
        tr //= 2
    c1 = 1.0 - ADAM_B1 ** ADAM_STEP
    c2 = 1.0 - ADAM_B2 ** ADAM_STEP

    def body(w_ref, g_ref, m_ref, v_ref, d_ref, mo_ref, vo_ref):
        gv = g_ref[...]
        mn = ADAM_B1 * m_ref[...] + (1.0 - ADAM_B1) * gv
        vn = ADAM_B2 * v_ref[...] + (1.0 - ADAM_B2) * (gv * gv)
        mo_ref[...] = mn
        vo_ref[...] = vn
        d_ref[...] = -ADAM_LR * ((mn / c1) / (jnp.sqrt(vn / c2) + ADAM_EPS) + ADAM_WD * w_ref[...])

    spec = pl.BlockSpec((1, tr, cols), lambda l, i: (l, i, 0))
    return pl.pallas_call(
        body, name=name, grid=(layers, rows // tr), in_specs=[spec] * 4, out_specs=[spec] * 3,
        out_shape=[jax.ShapeDtypeStruct((layers, rows, cols), F32)] * 3, compiler_params=_cp(),
    )(w, g, m, v)


IN_NAMES = ['x', 'positions', 'norm_g', 'w_in', 'conv_w', 'conv_b', 'w_gate_x', 'b_gate_x', 'w_gate_a', 'b_gate_a', 'lru_lambda', 'w_lru_o',
            'cq_norm_g', 'ckv_norm_g', 'w_uq', 'w_ukv', 'mla_q_norm_g', 'mla_k_norm_g', 'w_mla_o', 'dil_q_norm_g', 'dil_k_norm_g', 'w_dil_o',
            'b_merge', 'w_out']
WEIGHTS = IN_NAMES[2:]
REPLICATED = [n for n in WEIGHTS if n not in SCATTER]
GATE_WEIGHTS = ('w_gate_x', 'w_gate_a')

_KR0 = C_KR * LANE


GATHERED = ['w_in', 'w_lru_o', 'w_uq', 'w_ukv', 'w_mla_o', 'w_dil_o', 'w_out', 'conv_w']


def _local_weights(wd, me):
    loc = {n: wd[n].astype(BF16) for n in GATHERED[:-1]}
    loc['w_in'] = _to_window(loc['w_in'], me)
    loc['w_uq'] = jnp.pad(loc['w_uq'], ((0, 0), (0, 0), (0, LANE - MLA_QK)))
    loc['conv_w'] = wd['conv_w']
    return [[loc[n][l] for n in GATHERED] for l in range(DEPTH)]


def _layer_weights(gathered, rep, l):
    gw = dict(zip(GATHERED, gathered))
    by_rows = lambda a: a.reshape(-1, a.shape[-1])
    by_cols = lambda a: jnp.swapaxes(a, 0, 1).reshape(a.shape[1], -1)
    ukv = jnp.swapaxes(gw['w_ukv'], 0, 1)
    g96 = lambda a: jnp.pad(a[l].reshape(1, MLA_QK), ((0, 0), (0, LANE - MLA_QK)))
    g64 = lambda a: jnp.tile(a[l].reshape(1, DIL_HD), (1, 2))
    return dict(
        norm_g=rep['norm_g'][l].reshape(1, D), w_in=_assemble_w_in(gw['w_in']),
        conv_w=by_cols(gw['conv_w']), conv_b=rep['conv_b'][l].reshape(1, D),
        w_gx=rep['w_gate_x'][l].astype(BF16), b_gx=rep['b_gate_x'][l].reshape(8, 1, LANE),
        w_ga=rep['w_gate_a'][l].astype(BF16), b_ga=rep['b_gate_a'][l].reshape(8, 1, LANE),
        lam=rep['lru_lambda'][l].reshape(1, D),
        w_lru_o=by_rows(gw['w_lru_o']), w_mla_o=by_cols(gw['w_mla_o']), w_dil_o=by_cols(gw['w_dil_o']), w_out=by_rows(gw['w_out']),
        g_cq=rep['cq_norm_g'][l].reshape(1, 256), g_ckv=rep['ckv_norm_g'][l].reshape(1, 128),
        w_uq=by_cols(gw['w_uq']), w_uk=jnp.pad(ukv[:, :, :64], ((0, 0), (0, 0), (0, 64))).reshape(128, 1024),
        w_uv=ukv[:, :, 64:].reshape(128, 512),
        g_mq=g96(rep['mla_q_norm_g']), g_mk=g96(rep['mla_k_norm_g']), g_dq=g64(rep['dil_q_norm_g']), g_dk=g64(rep['dil_k_norm_g']),
        b_merge=rep['b_merge'][l].reshape(1, 3 * D),
    )


def _sharded_grads(g):
    uk = g['w_uk'].reshape(128, 8, 128)[:, :, :64]
    uv = g['w_uv'].reshape(128, 8, 64)
    d = {'w_in': g['w_in'], 'conv_w': g['conv_w'], 'w_lru_o': g['w_lru_o'], 'w_uq': g['w_uq'],
         'w_ukv': jnp.concatenate([uk, uv], axis=-1).reshape(128, 1024), 'w_mla_o': g['w_mla_o'], 'w_dil_o': g['w_dil_o'],
         'w_out': g['w_out']}
    return [d[n] for n in SCATTER]


def _replicated_grads(g):
    return {
        'conv_b': g['conv_b'].reshape(D),
        'w_gate_x': g['w_gx'], 'b_gate_x': g['b_gx'].reshape(8, LANE), 'w_gate_a': g['w_ga'], 'b_gate_a': g['b_ga'].reshape(8, LANE),
        'lru_lambda': g['lam'].reshape(D), 'cq_norm_g': g['g_cq'].reshape(256), 'ckv_norm_g': g['g_ckv'].reshape(128),
        'mla_q_norm_g': g['g_mq'][0, :MLA_QK], 'mla_k_norm_g': g['g_mk'][0, :MLA_QK],
        'dil_q_norm_g': g['g_dq'][0, :DIL_HD] + g['g_dq'][0, DIL_HD:], 'dil_k_norm_g': g['g_dk'][0, :DIL_HD] + g['g_dk'][0, DIL_HD:],
        'b_merge': g['b_merge'].reshape(3 * D),
    }


def kernel(x, positions, norm_g, w_in, conv_w, conv_b, w_gate_x, b_gate_x, w_gate_a, b_gate_a, lru_lambda, w_lru_o, cq_norm_g, ckv_norm_g, w_uq, w_ukv, mla_q_norm_g, mla_k_norm_g, w_mla_o, dil_q_norm_g, dil_k_norm_g, w_dil_o, b_merge, w_out, loss_target, m_norm_g, m_w_in, m_conv_w, m_conv_b, m_w_gate_x, m_b_gate_x, m_w_gate_a, m_b_gate_a, m_lru_lambda, m_w_lru_o, m_cq_norm_g, m_ckv_norm_g, m_w_uq, m_w_ukv, m_mla_q_norm_g, m_mla_k_norm_g, m_w_mla_o, m_dil_q_norm_g, m_dil_k_norm_g, m_w_dil_o, m_b_merge, m_w_out, v_norm_g, v_w_in, v_conv_w, v_conv_b, v_w_gate_x, v_b_gate_x, v_w_gate_a, v_b_gate_a, v_lru_lambda, v_w_lru_o, v_cq_norm_g, v_ckv_norm_g, v_w_uq, v_w_ukv, v_mla_q_norm_g, v_mla_k_norm_g, v_w_mla_o, v_dil_q_norm_g, v_dil_k_norm_g, v_w_dil_o, v_b_merge, v_w_out):
    args = (x, positions, norm_g, w_in, conv_w, conv_b, w_gate_x, b_gate_x, w_gate_a, b_gate_a, lru_lambda, w_lru_o, cq_norm_g, ckv_norm_g, w_uq, w_ukv, mla_q_norm_g, mla_k_norm_g, w_mla_o, dil_q_norm_g, dil_k_norm_g, w_dil_o, b_merge, w_out)
    moments_m = (m_norm_g, m_w_in, m_conv_w, m_conv_b, m_w_gate_x, m_b_gate_x, m_w_gate_a, m_b_gate_a, m_lru_lambda, m_w_lru_o, m_cq_norm_g, m_ckv_norm_g, m_w_uq, m_w_ukv, m_mla_q_norm_g, m_mla_k_norm_g, m_w_mla_o, m_dil_q_norm_g, m_dil_k_norm_g, m_w_dil_o, m_b_merge, m_w_out)
    moments_v = (v_norm_g, v_w_in, v_conv_w, v_conv_b, v_w_gate_x, v_b_gate_x, v_w_gate_a, v_b_gate_a, v_lru_lambda, v_w_lru_o, v_cq_norm_g, v_ckv_norm_g, v_w_uq, v_w_ukv, v_mla_q_norm_g, v_mla_k_norm_g, v_w_mla_o, v_dil_q_norm_g, v_dil_k_norm_g, v_w_dil_o, v_b_merge, v_w_out)
    a = dict(zip(IN_NAMES, args))
    wd = {n: a[n] for n in WEIGHTS}
    md = dict(zip(WEIGHTS, moments_m))
    vd = dict(zip(WEIGHTS, moments_v))

    me = 4 * lax.axis_index("x") + 2 * lax.axis_index("y") + lax.axis_index("c")

    assert DEPTH == 2
    xs, tabs = x[0], _rope_tables(positions[0])
    whole = [_whole] * len(GATHERED)
    slicers = [SCATTER[n][0] for n in SCATTER]
    grad_slices = [SCATTER[n][1:3] for n in SCATTER]

    local = _local_weights(wd, me)
    w_slices = [(a.shape, a.dtype) for a in local[0]]
    landed0 = _gather_two_level(local[0], "gather_w0")
    flying = _exchange_start(local[1], whole, w_slices, landed0[0], "gather_w1_start")
    rep0 = dict(wd, norm_g=wd['norm_g'] + flying[-1][0, 0])
    w0 = _layer_weights(landed0, rep0, 0)
    x1, saved0 = _layer_fwd(xs, w0, tabs)
    w1 = _layer_weights(_exchange_wait(flying, whole, x1, "gather_w1_wait"), wd, 1)
    x2, saved1 = _layer_fwd(x1, w1, tabs)
    loss, dx2 = _loss_fwd_bwd(x2, loss_target[0])
    loss = loss[0, 0]

    sharded = list(SCATTER)
    nsh = len(sharded)
    small = [n for n in REPLICATED if n not in GATE_WEIGHTS and n != 'norm_g']

    def outgoing(g):
        r = _replicated_grads(g)
        return (_sharded_grads(g) + [_pack([r[n] for n in small], F32, 0)]
                + [r[n].astype(BF16).reshape(8 * LANE, LANE) for n in GATE_WEIGHTS])

    out_slicers = slicers + [_whole] * 3
    out_slices = grad_slices + [((_packed_rows([wd[n].shape[1:] for n in small]), LANE), F32)] + [((8 * LANE, LANE), BF16)] * 2
    dx1, g1 = _layer_bwd(dx2, w1, tabs, saved1)
    flying1 = _exchange_start(outgoing(g1), out_slicers, out_slices, dx1, "scatter_g1_start")
    later = {}

    names = sharded + ['small'] + list(GATE_WEIGHTS)
    sliced = [True] * nsh + [False] * 3
    by_chip = [(lambda ref, q: ref.at[q])] * nsh + [_whole] * 3

    def send_layer0(g):
        later['got1'] = _exchange_wait(flying1, out_slicers, g['w_in'], "scatter_g1_wait")
        mine = outgoing(g)
        came = _pair_exchange(mine, out_slicers, out_slices, sliced, "pair_g0")
        my_side = 2 * jnp.arange(4, dtype=jnp.int32) + lax.axis_index("c")
        halves = []
        for n, a, c in zip(names, mine, came):
            if n in SCATTER:
                first = _win_base(my_side) if n == 'w_in' else my_side
                halves.append(_pair_add(a, c, first.astype(jnp.int32), SLICED_AXIS[n], f"pair_sum_{n}"))
            else:
                halves.append(_add2(a, c[0], f"pair_sum_{n}"))
        later['flying0'] = _exchange_start(halves, by_chip, out_slices, later['got1'][0], "scatter_g0_start", plan=_plan_chips, nslots=4)
        return later['flying0'][-1]

    grad_x, g0 = _layer_bwd(dx1, w0, tabs, saved0, hook=send_layer0, after=flying1[-1])
    sum1 = [_sum8(b, f"sum_{n}_1") for n, b in zip(names, later['got1'])]
    behind = grad_x[:1, :1] + sum1[0][:1, :1]
    got0 = _exchange_wait(later['flying0'], by_chip, behind, "scatter_g0_wait", plan=_plan_chips)
    sum0 = [_sum8(b, f"sum_{n}_0") for n, b in zip(names, got0)]
    norm_part = _pack([jnp.stack([g['norm_g'].reshape(D) for g in (g0, g1)])], F32, 0)
    norm_sum = _sum8(_exchange([norm_part], [_whole], [(norm_part.shape, F32)], "gather_norm_g")[0], "sum_norm_g")

    gsh = {n: jnp.stack([sum0[i], sum1[i]]) for i, n in enumerate(sharded)}
    gsh['w_in'] = _from_window(gsh['w_in'], me)
    gsh['w_uq'] = gsh['w_uq'][:, :, :MLA_QK]
    grep = {'norm_g': _unpack(norm_sum, [wd['norm_g'].shape], 0)[0]}
    per_layer = [_unpack(s[nsh], [wd[n].shape[1:] for n in small], 0) for s in (sum0, sum1)]
    grep.update({n: jnp.stack([per_layer[l][i] for l in range(DEPTH)]) for i, n in enumerate(small)})
    for i, n in enumerate(GATE_WEIGHTS):
        grep[n] = jnp.stack([sum0[nsh + 1 + i], sum1[nsh + 1 + i]]).reshape(wd[n].shape)

    out_g, out_d, out_m, out_v = {}, {}, {}, {}
    vecs = ['norm_g'] + small
    vshapes = [wd[n].shape for n in vecs]
    packed_g = _pack([grep[n] for n in vecs], F32, 0)
    d_, m_, v_ = _adamw(_pack([wd[n] for n in vecs], F32, 0)[None], packed_g[None], _pack([md[n] for n in vecs], F32, 0)[None],
                        _pack([vd[n] for n in vecs], F32, 0)[None], "adamw_vectors")
    for dst, buf in ((out_d, d_), (out_m, m_), (out_v, v_)):
        dst.update(zip(vecs, _unpack(buf[0], vshapes, 0)))
    out_g.update({n: grep[n] for n in vecs})
    gsh.update({n: grep[n] for n in GATE_WEIGHTS})
    for n in sharded + list(GATE_WEIGHTS):
        shp = wd[n].shape
        three = (1, -1, shp[-1])
        d_, m_, v_ = _adamw(wd[n].reshape(three), gsh[n].reshape(three), md[n].reshape(three), vd[n].reshape(three), "adamw_" + n)
        out_g[n], out_d[n], out_m[n], out_v[n] = gsh[n], d_.reshape(shp), m_.reshape(shp), v_.reshape(shp)

    loss = lax.psum(loss, ("x", "y", "c"))
    return (loss, grad_x[None], *[out_g[n] for n in WEIGHTS], *[out_d[n] for n in WEIGHTS], *[out_m[n] for n in WEIGHTS],
            *[out_v[n] for n in WEIGHTS])
```

```python
import numpy as np
import jax
import jax.numpy as jnp
from jax import lax
from jax.experimental import pallas as pl
from jax.experimental.pallas import tpu as pltpu

F32 = jnp.float32
BF16 = jnp.bfloat16

N_DEV = 8
D = 1024
DEPTH = 2
EPS = 1e-6
ROPE_THETA = 10000.0
LRU_C = 8.0
LANE = 128
SUB = 8
IN_WIDTH = 11168
SHARD_IN = IN_WIDTH // N_DEV

C_LRUX, C_LRUG, C_CQ, C_CKV, C_KR, C_MLAG, C_DQ, C_DK, C_DV, C_DILG, C_MERGE = 0, 8, 16, 18, 19, 20, 24, 36, 48, 60, 64
ZW = 88 * LANE
KR_LANE = 64

MLA_QK = 96
MLA_SCALE = MLA_QK ** -0.5
DIL_HD = 64
DIL_SCALE = DIL_HD ** -0.5
DIL_DILATIONS = (1, 4, 16)
NK = 128

ADAM_LR, ADAM_B1, ADAM_B2, ADAM_EPS, ADAM_WD, ADAM_STEP = 0.001, 0.9, 0.999, 1e-08, 0.01, 10

NEG = -1e30
LOG2E = 1.4426950408889634
VMEM_LIMIT = 48 * 1024 * 1024


def _cp(**kw):
    return pltpu.CompilerParams(vmem_limit_bytes=VMEM_LIMIT, **kw)


def _sig(x):
    return 1.0 / (1.0 + jnp.exp(-x))


def _silu(x):
    return x * _sig(x)


def _dsilu(x):
    s = _sig(x)
    return s * (1.0 + x * (1.0 - s))


def _dot(a, b, dims):
    return lax.dot_general(a, b, (dims, ((), ())), preferred_element_type=F32)


def _nn(a, b):
    return _dot(a, b, ((1,), (0,)))


def _nt(a, b):
    return _dot(a, b, ((1,), (1,)))


def _tn(a, b):
    return _dot(a, b, ((0,), (0,)))


def _rsum(x):
    return jnp.sum(x, axis=-1, keepdims=True)


def _rsum_mxu(x):
    ones = jnp.ones((x.shape[-1], LANE), F32)
    return lax.dot_general(x, ones, (((1,), (0,)), ((), ())), precision=lax.Precision.HIGHEST, preferred_element_type=F32)


def _csum(x):
    return jnp.sum(x, axis=0, keepdims=True)


def _mm(a, b, *, mode, name, out_dtype=F32, add=None, after=None, tm=1024, tn=1024, tk=1024):
    if mode == "nn":
        (M, K), (K2, N) = a.shape, b.shape
    elif mode == "nt":
        (M, K), (N, K2) = a.shape, b.shape
    else:
        (K, M), (K2, N) = a.shape, b.shape
    assert K == K2
    tm, tn, tk = min(tm, M), min(tn, N), min(tk, K)
    assert M % tm == 0 and N % tn == 0 and K % tk == 0
    nk = K // tk
    fn = {"nn": _nn, "nt": _nt, "tn": _tn}[mode]
    has_add = add is not None

    def body(*refs):
        a_ref, b_ref = refs[0], refs[1]
        add_ref = refs[2] if has_add else None
        o_ref = refs[2 + has_add + (after is not None)]
        part = fn(a_ref[...].astype(BF16), b_ref[...].astype(BF16))

        def fin(acc):
            if has_add:
                acc = acc + add_ref[...]
            o_ref[...] = acc.astype(out_dtype)

        if nk == 1:
            fin(part)
        else:
            acc_ref = refs[-1]
            k = pl.program_id(2)

            @pl.when(k == 0)
            def _():
                acc_ref[...] = part

            @pl.when(k > 0)
            def _():
                acc_ref[...] += part

            @pl.when(k == nk - 1)
            def _():
                fin(acc_ref[...])

    a_spec = pl.BlockSpec((tk, tm), lambda i, j, k: (k, i)) if mode == "tn" else pl.BlockSpec((tm, tk), lambda i, j, k: (i, k))
    b_spec = pl.BlockSpec((tn, tk), lambda i, j, k: (j, k)) if mode == "nt" else pl.BlockSpec((tk, tn), lambda i, j, k: (k, j))
    o_spec = pl.BlockSpec((tm, tn), lambda i, j, k: (i, j))
    in_specs, args = [a_spec, b_spec], [a, b]
    if has_add:
        in_specs.append(o_spec)
        args.append(add)
    if after is not None:
        in_specs.append(pl.BlockSpec(memory_space=pl.ANY))
        args.append(after)
    return pl.pallas_call(
        body, name=name, grid=(M // tm, N // tn, nk), in_specs=in_specs, out_specs=o_spec,
        out_shape=jax.ShapeDtypeStruct((M, N), out_dtype),
        scratch_shapes=[pltpu.VMEM((tm, tn), F32)] if nk > 1 else [],
        compiler_params=_cp(dimension_semantics=("parallel", "parallel", "arbitrary")),
    )(*args)


T_ROW = 512


def _rms_in_fwd(x, g):
    S = x.shape[0]
    T = T_ROW

    def body(x_ref, g_ref, h_ref):
        xv = x_ref[...]
        r = lax.rsqrt(jnp.mean(xv * xv, axis=-1, keepdims=True) + EPS)
        h_ref[...] = (xv * r * g_ref[...]).astype(BF16)

    return pl.pallas_call(
        body, name="rms_in_fwd", grid=(S // T,),
        in_specs=[pl.BlockSpec((T, D), lambda i: (i, 0)), pl.BlockSpec((1, D), lambda i: (0, 0))],
        out_specs=pl.BlockSpec((T, D), lambda i: (i, 0)),
        out_shape=jax.ShapeDtypeStruct((S, D), BF16), compiler_params=_cp(),
    )(x, g)


def _rms_in_bwd(x, g, dh, dres):
    S = x.shape[0]
    T = T_ROW

    def body(x_ref, g_ref, dh_ref, dr_ref, dx_ref, dg_ref):
        i = pl.program_id(0)
        xv = x_ref[...]
        r = lax.rsqrt(jnp.mean(xv * xv, axis=-1, keepdims=True) + EPS)
        xn = xv * r
        dy = dh_ref[...]
        part = _csum(dy * xn)

        @pl.when(i == 0)
        def _():
            dg_ref[...] = part

        @pl.when(i > 0)
        def _():
            dg_ref[...] += part

        dxh = dy * g_ref[...]
        dx_ref[...] = dr_ref[...] + r * (dxh - xn * jnp.mean(dxh * xn, axis=-1, keepdims=True))

    row = pl.BlockSpec((T, D), lambda i: (i, 0))
    vec = pl.BlockSpec((1, D), lambda i: (0, 0))
    return pl.pallas_call(
        body, name="rms_in_bwd", grid=(S // T,), in_specs=[row, vec, row, row], out_specs=[row, vec],
        out_shape=[jax.ShapeDtypeStruct((S, D), F32), jax.ShapeDtypeStruct((1, D), F32)], compiler_params=_cp(),
    )(x, g, dh, dres)


T_LRU = 1024
T_LRU_BWD = 512


def _neg_expm1(y):
    ser = -y * (1.0 + y * 0.5 * (1.0 + y * (1.0 / 3.0) * (1.0 + y * 0.25 * (1.0 + y * 0.2))))
    return jnp.where(y > -0.03, ser, 1.0 - jnp.exp(y))


def _softplus_neg(lam):
    e = jnp.exp(-jnp.abs(lam))
    l1p = jnp.where(e < 0.01, e * (1.0 - e * (0.5 - e * (1.0 / 3.0 - e * 0.25))), jnp.log(1.0 + e))
    return jnp.maximum(-lam, 0.0) + l1p


def _scan_fwd(a, b, T):
    row = lax.broadcasted_iota(jnp.int32, a.shape, 0)
    d = 1
    while d < T:
        m = row >= d
        b = jnp.where(m, a * pltpu.roll(b, d, 0) + b, b)
        a = jnp.where(m, a * pltpu.roll(a, d, 0), a)
        d *= 2
    return a, b


def _scan_bwd(a, b, T):
    row = lax.broadcasted_iota(jnp.int32, a.shape, 0)
    d = 1
    while d < T:
        m = row < T - d
        b = jnp.where(m, a * pltpu.roll(b, T - d, 0) + b, b)
        a = jnp.where(m, a * pltpu.roll(a, T - d, 0), a)
        d *= 2
    return b


def _lru_common(x, prev, first, cw_ref, cb_ref, wgx_ref, bgx_ref, wga_ref, bga_ref, lam_ref, T):
    row = lax.broadcasted_iota(jnp.int32, x.shape, 0)
    prev = jnp.where(first, 0.0, prev)
    xs = []
    for j in (3, 2, 1):
        pv = jnp.tile(pltpu.roll(prev, j, 0), (T // SUB, 1))
        xs.append(jnp.where(row < j, pv, pltpu.roll(x, j, 0)))
    xs.append(x)
    xc = cb_ref[...] + cw_ref[0:1, :] * xs[0] + cw_ref[1:2, :] * xs[1] + cw_ref[2:3, :] * xs[2] + cw_ref[3:4, :] * xs[3]
    xcb = xc.astype(BF16)
    gx = _sig(_nn(xcb, wgx_ref[0]) + bgx_ref[0])
    ga = _sig(_nn(xcb, wga_ref[0]) + bga_ref[0])
    sp = _softplus_neg(lam_ref[...])
    log_a = -LRU_C * ga * sp
    a = jnp.exp(log_a)
    mult = jnp.sqrt(_neg_expm1(2.0 * log_a))
    return xs, xc, xcb, gx, ga, sp, a, mult


def _lru_specs(T, tmap):
    def at(col0):
        return pl.BlockSpec((T, LANE), lambda n, i: (tmap(i), col0 + n))

    def prev(col0):
        return pl.BlockSpec((SUB, LANE), lambda n, i: (jnp.maximum(tmap(i) * (T // SUB) - 1, 0), col0 + n))

    small = [
        pl.BlockSpec((4, LANE), lambda n, i: (0, n)),
        pl.BlockSpec((1, LANE), lambda n, i: (0, n)),
        pl.BlockSpec((1, LANE, LANE), lambda n, i: (n, 0, 0)),
        pl.BlockSpec((1, 1, LANE), lambda n, i: (n, 0, 0)),
        pl.BlockSpec((1, LANE, LANE), lambda n, i: (n, 0, 0)),
        pl.BlockSpec((1, 1, LANE), lambda n, i: (n, 0, 0)),
        pl.BlockSpec((1, LANE), lambda n, i: (0, n)),
    ]
    return at, prev, small


def _lru_fwd(zp, w):
    S = zp.shape[0]
    T = T_LRU
    at, prev, small = _lru_specs(T, lambda i: i)

    def body(x_ref, xp_ref, g_ref, cw_ref, cb_ref, wgx_ref, bgx_ref, wga_ref, bga_ref, lam_ref, hs_ref, y_ref, carry_ref):
        i = pl.program_id(1)

        @pl.when(i == 0)
        def _():
            carry_ref[...] = jnp.zeros_like(carry_ref)

        x = x_ref[...]
        _, xc, _, gx, _, _, a, mult = _lru_common(x, xp_ref[...], i == 0, cw_ref, cb_ref, wgx_ref, bgx_ref, wga_ref, bga_ref, lam_ref, T)
        A, B = _scan_fwd(a, mult * gx * xc, T)
        h = B + A * carry_ref[SUB - 1:SUB, :]
        hs_ref[...] = h
        carry_ref[...] = hs_ref[T - SUB:T, :]
        y_ref[...] = (h * _silu(g_ref[...])).astype(BF16)

    out = pl.BlockSpec((T, LANE), lambda n, i: (i, n))
    return pl.pallas_call(
        body, name="lru_fwd", grid=(8, S // T),
        in_specs=[at(C_LRUX), prev(C_LRUX), at(C_LRUG)] + small, out_specs=[out, out],
        out_shape=[jax.ShapeDtypeStruct((S, D), F32), jax.ShapeDtypeStruct((S, D), BF16)],
        scratch_shapes=[pltpu.VMEM((SUB, LANE), F32)],
        compiler_params=_cp(dimension_semantics=("parallel", "arbitrary")),
    )(zp, zp, zp, w["conv_w"], w["conv_b"], w["w_gx"], w["b_gx"], w["w_ga"], w["b_ga"], w["lam"])


def _lru_bwd(zp, hs, dy, w, dz):
    S = zp.shape[0]
    T = T_LRU_BWD
    nT = S // T
    at, prev, small = _lru_specs(T, lambda i: nT - 1 - i)

    def body(x_ref, xp_ref, g_ref, h_ref, hp_ref, dy_ref, cw_ref, cb_ref, wgx_ref, bgx_ref, wga_ref, bga_ref, lam_ref, dz_in,
             dzx_ref, dcw_ref, dcb_ref, dwgx_ref, dbgx_ref, dwga_ref, dbga_ref, dlam_ref, carry_ref, head_ref):
        del dz_in
        j = pl.program_id(1)
        it = nT - 1 - j

        @pl.when(j == 0)
        def _():
            for r in (carry_ref, head_ref, dcw_ref, dcb_ref, dwgx_ref, dbgx_ref, dwga_ref, dbga_ref, dlam_ref):
                r[...] = jnp.zeros_like(r)

        first = it == 0
        x = x_ref[...]
        xs, xc, xcb, gx, ga, sp, a, mult = _lru_common(x, xp_ref[...], first, cw_ref, cb_ref, wgx_ref, bgx_ref, wga_ref, bga_ref, lam_ref, T)
        row = lax.broadcasted_iota(jnp.int32, x.shape, 0)
        u = gx * xc
        h = h_ref[...]
        hp = jnp.where(first, 0.0, hp_ref[...])
        hm1 = jnp.where(row < 1, jnp.tile(pltpu.roll(hp, 1, 0), (T // SUB, 1)), pltpu.roll(h, 1, 0))
        dho = dy_ref[...] * _silu(g_ref[...])
        gin = jnp.where(row == T - 1, dho + carry_ref[0:1, :], dho)
        abar = jnp.where(row == T - 1, 0.0, pltpu.roll(a, T - 1, 0))
        dh = _scan_bwd(abar, gin, T)
        carry_ref[...] = (a * dh)[0:SUB, :]
        da = dh * hm1
        dmult = dh * u
        du = dh * mult
        dgx = du * xc
        dxc = du * gx
        dlog_a = da * a - dmult * a * a / mult
        dga = dlog_a * (-LRU_C * sp)
        lam = lam_ref[...]
        dlam_ref[...] += _csum(dlog_a * (-LRU_C * ga)) * (-1.0 / (1.0 + jnp.exp(lam)))
        dpa = dga * ga * (1.0 - ga)
        dpx = dgx * gx * (1.0 - gx)
        dpab, dpxb = dpa.astype(BF16), dpx.astype(BF16)
        dxc = dxc + _nt(dpxb, wgx_ref[0]) + _nt(dpab, wga_ref[0])
        dwgx_ref[0] += _tn(xcb, dpxb)
        dwga_ref[0] += _tn(xcb, dpab)
        dbgx_ref[0] += _csum(dpx)
        dbga_ref[0] += _csum(dpa)
        dcb_ref[...] += _csum(dxc)
        for k in range(4):
            dcw_ref[k:k + 1, :] += _csum(dxc * xs[k])
        head = head_ref[...]
        dx = cw_ref[3:4, :] * dxc
        for jj in (1, 2, 3):
            hv = jnp.tile(pltpu.roll(head, SUB - jj, 0), (T // SUB, 1))
            dx = dx + cw_ref[3 - jj:4 - jj, :] * jnp.where(row >= T - jj, hv, pltpu.roll(dxc, T - jj, 0))
        head_ref[...] = dxc[0:SUB, :]
        dzx_ref[...] = dx.astype(BF16)

    def acc(shape, imap):
        return pl.BlockSpec(shape, imap)

    out_specs = [
        pl.BlockSpec((T, LANE), lambda n, i: (nT - 1 - i, C_LRUX + n)),
        acc((4, LANE), lambda n, i: (0, n)), acc((1, LANE), lambda n, i: (0, n)),
        acc((1, LANE, LANE), lambda n, i: (n, 0, 0)), acc((1, 1, LANE), lambda n, i: (n, 0, 0)),
        acc((1, LANE, LANE), lambda n, i: (n, 0, 0)), acc((1, 1, LANE), lambda n, i: (n, 0, 0)),
        acc((1, LANE), lambda n, i: (0, n)),
    ]
    out_shape = [
        jax.ShapeDtypeStruct(dz.shape, BF16),
        jax.ShapeDtypeStruct((4, D), F32), jax.ShapeDtypeStruct((1, D), F32),
        jax.ShapeDtypeStruct((8, LANE, LANE), F32), jax.ShapeDtypeStruct((8, 1, LANE), F32),
        jax.ShapeDtypeStruct((8, LANE, LANE), F32), jax.ShapeDtypeStruct((8, 1, LANE), F32),
        jax.ShapeDtypeStruct((1, D), F32),
    ]
    dyspec = pl.BlockSpec((T, LANE), lambda n, i: (nT - 1 - i, n))
    hprev = pl.BlockSpec((SUB, LANE), lambda n, i: (jnp.maximum((nT - 1 - i) * (T // SUB) - 1, 0), n))
    return pl.pallas_call(
        body, name="lru_bwd", grid=(8, nT),
        in_specs=[at(C_LRUX), prev(C_LRUX), at(C_LRUG), dyspec, hprev, dyspec] + small + [pl.BlockSpec(memory_space=pl.ANY)],
        out_specs=out_specs, out_shape=out_shape,
        scratch_shapes=[pltpu.VMEM((SUB, LANE), F32), pltpu.VMEM((SUB, LANE), F32)],
        input_output_aliases={13: 0},
        compiler_params=_cp(dimension_semantics=("parallel", "arbitrary")),
    )(zp, zp, zp, hs, hs, dy, w["conv_w"], w["conv_b"], w["w_gx"], w["b_gx"], w["w_ga"], w["b_ga"], w["lam"], dz)


def _lru_gate_bwd(zp, hs, dy, dz):
    S = zp.shape[0]
    T = T_ROW

    def body(g_ref, h_ref, dy_ref, dz_in, o_ref):
        del dz_in
        o_ref[...] = (dy_ref[...] * h_ref[...] * _dsilu(g_ref[...])).astype(BF16)

    row = pl.BlockSpec((T, D), lambda i: (i, 0))
    zc = pl.BlockSpec((T, D), lambda i: (i, C_LRUG // 8))
    return pl.pallas_call(
        body, name="lru_gate_bwd", grid=(S // T,), in_specs=[zc, row, row, pl.BlockSpec(memory_space=pl.ANY)], out_specs=zc,
        out_shape=jax.ShapeDtypeStruct(dz.shape, BF16), input_output_aliases={3: 0}, compiler_params=_cp(),
    )(zp, hs, dy, dz)


def _rope_tables(pos):
    pf = pos.astype(F32)[:, None]

    def cs(d):
        inv = ROPE_THETA ** (-jnp.arange(0, d, 2, dtype=F32) / d)
        ang = pf * inv
        return jnp.cos(ang), jnp.sin(ang)

    S = pos.shape[0]
    c, s = cs(32)
    one, zero = jnp.ones((S, 64), F32), jnp.zeros((S, 16), F32)
    z32, z64 = jnp.zeros((S, 32), F32), jnp.zeros((S, 64), F32)
    mla = (jnp.concatenate([one, c, c, jnp.ones((S, 32), F32)], 1),
           jnp.concatenate([z64, zero, s, z32], 1),
           jnp.concatenate([z64, -s, zero, z32], 1))
    c, s = cs(64)
    dil = (jnp.concatenate([c, c, c, c], 1),
           jnp.concatenate([z32, s, z32, s], 1),
           jnp.concatenate([-s, z32, -s, z32], 1))
    return mla, dil


def _rope(x, C, S1, S2, sh):
    return x * C + pltpu.roll(x, sh, 1) * S1 + pltpu.roll(x, LANE - sh, 1) * S2


def _rope_t(dy, C, S1, S2, sh):
    return dy * C + pltpu.roll(dy * S1, LANE - sh, 1) + pltpu.roll(dy * S2, sh, 1)


def _lane(shape):
    return lax.broadcasted_iota(jnp.int32, shape, 1)


T_MLA = 256
TA = 512


def _zcol(T, width, col_lanes):
    assert (col_lanes * LANE) % width == 0
    return pl.BlockSpec((T, width), lambda i: (i, col_lanes * LANE // width))


def _full(shape):
    return pl.BlockSpec(shape, lambda *_: (0,) * len(shape))


def _mla_pre_fwd(zp, w, tab):
    S = zp.shape[0]
    T = T_MLA

    def body(cq_ref, ckv_ref, kr_ref, gcq_ref, gckv_ref, wuq_ref, wuk_ref, wuv_ref, gq_ref, gk_ref, C_ref, S1_ref, S2_ref,
             q_ref, k_ref, v_ref):
        cq = cq_ref[...]
        cqn = (cq * lax.rsqrt(jnp.mean(cq * cq, axis=-1, keepdims=True) + EPS) * gcq_ref[...]).astype(BF16)
        ckv = ckv_ref[...]
        ckvn = (ckv * lax.rsqrt(jnp.mean(ckv * ckv, axis=-1, keepdims=True) + EPS) * gckv_ref[...]).astype(BF16)
        q0 = _nn(cqn, wuq_ref[...])
        k0 = _nn(ckvn, wuk_ref[...])
        krb = kr_ref[...]
        C, S1, S2 = C_ref[...], S1_ref[...], S2_ref[...]
        for h in range(8):
            sl = slice(h * LANE, (h + 1) * LANE)
            xq = q0[:, sl]
            xq = xq * lax.rsqrt(_rsum_mxu(xq * xq) * (1.0 / MLA_QK) + EPS) * gq_ref[...]
            q_ref[:, sl] = _rope(xq, C, S1, S2, 16).astype(BF16)
            xk = k0[:, sl] + krb
            xk = xk * lax.rsqrt(_rsum_mxu(xk * xk) * (1.0 / MLA_QK) + EPS) * gk_ref[...]
            k_ref[:, sl] = _rope(xk, C, S1, S2, 16).astype(BF16)
        v_ref[...] = _nn(ckvn, wuv_ref[...]).astype(BF16)

    tabspec = pl.BlockSpec((T, LANE), lambda i: (i, 0))
    in_specs = [_zcol(T, 256, C_CQ), _zcol(T, LANE, C_CKV), _zcol(T, LANE, C_KR), _full((1, 256)), _full((1, LANE)),
                _full((256, 1024)), _full((LANE, 1024)), _full((LANE, 512)), _full((1, LANE)), _full((1, LANE)),
                tabspec, tabspec, tabspec]
    return pl.pallas_call(
        body, name="mla_pre_fwd", grid=(S // T,), in_specs=in_specs,
        out_specs=[pl.BlockSpec((T, 1024), lambda i: (i, 0)), pl.BlockSpec((T, 1024), lambda i: (i, 0)), pl.BlockSpec((T, 512), lambda i: (i, 0))],
        out_shape=[jax.ShapeDtypeStruct((S, 1024), BF16), jax.ShapeDtypeStruct((S, 1024), BF16), jax.ShapeDtypeStruct((S, 512), BF16)],
        compiler_params=_cp(),
    )(zp, zp, zp, w["g_cq"], w["g_ckv"], w["w_uq"], w["w_uk"], w["w_uv"], w["g_mq"], w["g_mk"], *tab)


def _mla_attn_fwd(q, k, v, zp):
    S = q.shape[0]
    nq = S // TA

    def body(q_ref, k_ref, v_ref, g_ref, o_ref, lse_ref, y_ref):
        qi = pl.program_id(1)
        lane = _lane((TA, LANE))
        rowi = lax.broadcasted_iota(jnp.int32, (TA, TA), 0)
        coli = lax.broadcasted_iota(jnp.int32, (TA, TA), 1)
        o_tot = jnp.zeros((TA, LANE), F32)
        for hh in range(2):
            cs = slice(hh * LANE, (hh + 1) * LANE)
            hm = (lane < 64) if hh == 0 else (lane >= 64)
            qh = q_ref[:, cs]
            ones_lane = 64 if hh == 0 else 0

            def step(kb, carry, masked, cs=cs, hm=hm, qh=qh, ones_lane=ones_lane):
                m, acc = carry
                off = pl.multiple_of(kb * TA, TA)
                kh = k_ref[pl.ds(off, TA), cs]
                vv = v_ref[pl.ds(off, TA), :]
                vh = jnp.where(hm, vv, jnp.where(lane == ones_lane, jnp.ones_like(vv), jnp.zeros_like(vv)))
                s = _nt(qh, kh) * (MLA_SCALE * LOG2E)
                if masked:
                    s = jnp.where(rowi >= coli, s, NEG)
                m_new = jnp.maximum(m, jnp.max(s, axis=-1, keepdims=True))
                acc = jnp.exp2(m - m_new) * acc + _nn(jnp.exp2(s - m_new).astype(BF16), vh)
                return m_new, acc

            init = (jnp.full((TA, 1), NEG, F32), jnp.zeros((TA, LANE), F32))
            carry = lax.fori_loop(0, qi, lambda kb, c: step(kb, c, False), init)
            m, acc = step(qi, carry, True)
            l = _rsum(jnp.where(lane == ones_lane, acc, 0.0))
            o_tot = o_tot + jnp.where(hm, acc, 0.0) / l
            lse_ref[:, cs] = jnp.broadcast_to(m * (1.0 / LOG2E) + jnp.log(l), (TA, LANE))
        o_ref[...] = o_tot
        y_ref[...] = (o_tot * _silu(g_ref[...])).astype(BF16)

    blk = pl.BlockSpec((TA, LANE), lambda p, i: (i, p))
    return pl.pallas_call(
        body, name="mla_attn_fwd", grid=(4, nq),
        in_specs=[pl.BlockSpec((TA, 256), lambda p, i: (i, p)), pl.BlockSpec((S, 256), lambda p, i: (0, p)),
                  pl.BlockSpec((S, LANE), lambda p, i: (0, p)), pl.BlockSpec((TA, LANE), lambda p, i: (i, C_MLAG + p))],
        out_specs=[blk, pl.BlockSpec((TA, 256), lambda p, i: (i, p)), blk],
        out_shape=[jax.ShapeDtypeStruct((S, 512), F32), jax.ShapeDtypeStruct((S, 1024), F32), jax.ShapeDtypeStruct((S, 512), BF16)],
        compiler_params=_cp(dimension_semantics=("parallel", "arbitrary")),
    )(q, k, v, zp)


def _mla_post_bwd(zp, o, dy, dz):
    S = zp.shape[0]
    T = T_ROW

    def body(g_ref, o_ref, dy_ref, dz_in, dz_ref, do_ref, D_ref):
        del dz_in
        g, o_, dy_ = g_ref[...], o_ref[...], dy_ref[...]
        do = dy_ * _silu(g)
        do_ref[...] = do.astype(BF16)
        dz_ref[...] = (dy_ * o_ * _dsilu(g)).astype(BF16)
        prod = do * o_
        lane = _lane((T, LANE))
        for p in range(4):
            pr = prod[:, p * LANE:(p + 1) * LANE]
            da = _rsum(jnp.where(lane < 64, pr, 0.0))
            db = _rsum(jnp.where(lane >= 64, pr, 0.0))
            D_ref[:, 2 * p * LANE:(2 * p + 1) * LANE] = jnp.broadcast_to(da, (T, LANE))
            D_ref[:, (2 * p + 1) * LANE:(2 * p + 2) * LANE] = jnp.broadcast_to(db, (T, LANE))

    row = pl.BlockSpec((T, 512), lambda i: (i, 0))
    zc = _zcol(T, 512, C_MLAG)
    return pl.pallas_call(
        body, name="mla_post_bwd", grid=(S // T,), in_specs=[zc, row, row, pl.BlockSpec(memory_space=pl.ANY)],
        out_specs=[zc, row, pl.BlockSpec((T, 1024), lambda i: (i, 0))],
        out_shape=[jax.ShapeDtypeStruct(dz.shape, BF16), jax.ShapeDtypeStruct((S, 512), BF16), jax.ShapeDtypeStruct((S, 1024), F32)],
        input_output_aliases={3: 0}, compiler_params=_cp(),
    )(zp, o, dy, dz)


def _mla_attn_bwd(q, k, v, do, lse, Dr):
    S = q.shape[0]
    nq = S // TA

    def body(q_ref, do_ref, lse_ref, D_ref, k_ref, v_ref, dq_ref, dk_ref, dv_ref):
        ki = pl.program_id(1)

        @pl.when(ki == 0)
        def _():
            dq_ref[...] = jnp.zeros_like(dq_ref)

        lane = _lane((TA, LANE))
        rowi = lax.broadcasted_iota(jnp.int32, (TA, TA), 0)
        coli = lax.broadcasted_iota(jnp.int32, (TA, TA), 1)
        dv_tot = jnp.zeros((TA, LANE), F32)
        for hh in range(2):
            cs = slice(hh * LANE, (hh + 1) * LANE)
            hm = (lane < 64) if hh == 0 else (lane >= 64)
            kh = k_ref[:, cs]
            vv = v_ref[...]
            vm = jnp.where(hm, vv, jnp.zeros_like(vv))

            def step(qb, carry, masked, cs=cs, kh=kh, vm=vm):
                dk_acc, dv_acc = carry
                off = pl.multiple_of(qb * TA, TA)
                qh = q_ref[pl.ds(off, TA), cs]
                doh = do_ref[pl.ds(off, TA), :]
                ls = jnp.tile(lse_ref[pl.ds(off, TA), cs], (1, TA // LANE))
                dd = jnp.tile(D_ref[pl.ds(off, TA), cs], (1, TA // LANE))
                s = _nt(qh, kh) * MLA_SCALE
                if masked:
                    s = jnp.where(rowi >= coli, s, NEG)
                p = jnp.exp(s - ls)
                dp = _nt(doh, vm)
                ds = (p * (dp - dd) * MLA_SCALE).astype(BF16)
                dv_acc = dv_acc + _tn(p.astype(BF16), doh)
                dk_acc = dk_acc + _tn(ds, qh)
                dq_ref[pl.ds(off, TA), cs] += _nn(ds, kh)
                return dk_acc, dv_acc

            z = jnp.zeros((TA, LANE), F32)
            carry = step(ki, (z, z), True)
            dk_acc, dv_acc = lax.fori_loop(ki + 1, nq, lambda qb, c: step(qb, c, False), carry)
            dk_ref[:, cs] = dk_acc
            dv_tot = dv_tot + jnp.where(hm, dv_acc, 0.0)
        dv_ref[...] = dv_tot

    pair = pl.BlockSpec((S, 256), lambda p, i: (0, p))
    return pl.pallas_call(
        body, name="mla_attn_bwd", grid=(4, nq),
        in_specs=[pair, pl.BlockSpec((S, LANE), lambda p, i: (0, p)), pair, pair,
                  pl.BlockSpec((TA, 256), lambda p, i: (i, p)), pl.BlockSpec((TA, LANE), lambda p, i: (i, p))],
        out_specs=[pair, pl.BlockSpec((TA, 256), lambda p, i: (i, p)), pl.BlockSpec((TA, LANE), lambda p, i: (i, p))],
        out_shape=[jax.ShapeDtypeStruct((S, 1024), F32), jax.ShapeDtypeStruct((S, 1024), F32), jax.ShapeDtypeStruct((S, 512), F32)],
        compiler_params=_cp(dimension_semantics=("parallel", "arbitrary")),
    )(q, do, lse, Dr, k, v)


def _mla_pre_bwd(zp, dq, dk, dv, w, tab, dz):
    S = zp.shape[0]
    T = T_MLA

    def body(cq_ref, ckv_ref, kr_ref, dq_ref, dk_ref, dv_ref, gcq_ref, gckv_ref, wuq_ref, wuk_ref, wuv_ref, gq_ref, gk_ref,
             C_ref, S1_ref, S2_ref, dz_in, dz_ref, dwuq_ref, dwuk_ref, dwuv_ref, dgcq_ref, dgckv_ref, dgq_ref, dgk_ref):
        del dz_in
        i = pl.program_id(0)

        @pl.when(i == 0)
        def _():
            for r in (dwuq_ref, dwuk_ref, dwuv_ref, dgcq_ref, dgckv_ref, dgq_ref, dgk_ref):
                r[...] = jnp.zeros_like(r)

        cq = cq_ref[...]
        rq = lax.rsqrt(jnp.mean(cq * cq, axis=-1, keepdims=True) + EPS)
        cqh = cq * rq
        cqn = (cqh * gcq_ref[...]).astype(BF16)
        ckv = ckv_ref[...]
        rkv = lax.rsqrt(jnp.mean(ckv * ckv, axis=-1, keepdims=True) + EPS)
        ckvh = ckv * rkv
        ckvn = (ckvh * gckv_ref[...]).astype(BF16)
        q0 = _nn(cqn, wuq_ref[...])
        k0 = _nn(ckvn, wuk_ref[...])
        krb = kr_ref[...]
        C, S1, S2 = C_ref[...], S1_ref[...], S2_ref[...]
        gq, gk = gq_ref[...], gk_ref[...]

        def head_bwd(x, dy, g):
            r = lax.rsqrt(_rsum_mxu(x * x) * (1.0 / MLA_QK) + EPS)
            xn = x * r
            dyn = _rope_t(dy, C, S1, S2, 16)
            dxh = dyn * g
            return r * (dxh - xn * _rsum_mxu(dxh * xn) * (1.0 / MLA_QK)), _csum(dyn * xn)

        dq0, dk0 = [], []
        dgq_acc = jnp.zeros((1, LANE), F32)
        dgk_acc = jnp.zeros((1, LANE), F32)
        dkr = jnp.zeros((T, LANE), F32)
        for h in range(8):
            sl = slice(h * LANE, (h + 1) * LANE)
            dxq, gq_p = head_bwd(q0[:, sl], dq_ref[:, sl], gq)
            dxk, gk_p = head_bwd(k0[:, sl] + krb, dk_ref[:, sl], gk)
            dq0.append(dxq.astype(BF16))
            dk0.append(dxk.astype(BF16))
            dkr = dkr + dxk
            dgq_acc = dgq_acc + gq_p
            dgk_acc = dgk_acc + gk_p
        dgq_ref[...] += dgq_acc
        dgk_ref[...] += dgk_acc
        dq0 = jnp.concatenate(dq0, axis=1)
        dk0 = jnp.concatenate(dk0, axis=1)
        dvb = dv_ref[...].astype(BF16)
        dwuq_ref[...] += _tn(cqn, dq0)
        dwuk_ref[...] += _tn(ckvn, dk0)
        dwuv_ref[...] += _tn(ckvn, dvb)
        dcqn = _nt(dq0, wuq_ref[...])
        dckvn = _nt(dk0, wuk_ref[...]) + _nt(dvb, wuv_ref[...])
        dgcq_ref[...] += _csum(dcqn * cqh)
        dgckv_ref[...] += _csum(dckvn * ckvh)
        dxh = dcqn * gcq_ref[...]
        dz_ref[:, 0:256] = (rq * (dxh - cqh * jnp.mean(dxh * cqh, axis=-1, keepdims=True))).astype(BF16)
        dxh = dckvn * gckv_ref[...]
        dz_ref[:, 256:384] = (rkv * (dxh - ckvh * jnp.mean(dxh * ckvh, axis=-1, keepdims=True))).astype(BF16)
        lane = _lane((T, LANE))
        dz_ref[:, 384:512] = jnp.where((lane >= KR_LANE) & (lane < KR_LANE + 32), dkr, 0.0).astype(BF16)

    tabspec = pl.BlockSpec((T, LANE), lambda i: (i, 0))
    in_specs = [_zcol(T, 256, C_CQ), _zcol(T, LANE, C_CKV), _zcol(T, LANE, C_KR),
                pl.BlockSpec((T, 1024), lambda i: (i, 0)), pl.BlockSpec((T, 1024), lambda i: (i, 0)), pl.BlockSpec((T, 512), lambda i: (i, 0)),
                _full((1, 256)), _full((1, LANE)), _full((256, 1024)), _full((LANE, 1024)), _full((LANE, 512)), _full((1, LANE)), _full((1, LANE)),
                tabspec, tabspec, tabspec, pl.BlockSpec(memory_space=pl.ANY)]
    out_specs = [_zcol(T, 512, C_CQ), _full((256, 1024)), _full((LANE, 1024)), _full((LANE, 512)), _full((1, 256)), _full((1, LANE)),
                 _full((1, LANE)), _full((1, LANE))]
    out_shape = [jax.ShapeDtypeStruct(dz.shape, BF16), jax.ShapeDtypeStruct((256, 1024), F32), jax.ShapeDtypeStruct((LANE, 1024), F32),
                 jax.ShapeDtypeStruct((LANE, 512), F32), jax.ShapeDtypeStruct((1, 256), F32), jax.ShapeDtypeStruct((1, LANE), F32),
                 jax.ShapeDtypeStruct((1, LANE), F32), jax.ShapeDtypeStruct((1, LANE), F32)]
    return pl.pallas_call(
        body, name="mla_pre_bwd", grid=(S // T,), in_specs=in_specs, out_specs=out_specs, out_shape=out_shape,
        input_output_aliases={16: 0}, compiler_params=_cp(),
    )(zp, zp, zp, dq, dk, dv, w["g_cq"], w["g_ckv"], w["w_uq"], w["w_uk"], w["w_uv"], w["g_mq"], w["g_mk"], *tab, dz)


T_DIL = 256


def _head_stats(x, lane):
    sq = x * x
    sa = _rsum(jnp.where(lane < 64, sq, 0.0))
    sb = _rsum(jnp.where(lane >= 64, sq, 0.0))
    return lax.rsqrt(jnp.where(lane < 64, sa, sb) * (1.0 / DIL_HD) + EPS)


def _head_sum(x, lane):
    sa = _rsum(jnp.where(lane < 64, x, 0.0))
    sb = _rsum(jnp.where(lane >= 64, x, 0.0))
    return jnp.where(lane < 64, sa, sb)


def _head_stats_mxu(x):
    r = lax.broadcasted_iota(jnp.int32, (LANE, LANE), 0)
    c = lax.broadcasted_iota(jnp.int32, (LANE, LANE), 1)
    ones = jnp.where((r < 64) == (c < 64), 1.0, 0.0).astype(F32)
    ss = lax.dot_general(x * x, ones, (((1,), (0,)), ((), ())), precision=lax.Precision.HIGHEST, preferred_element_type=F32)
    return lax.rsqrt(ss * (1.0 / DIL_HD) + EPS)


def _dil_pre_fwd(zp, w, tab):
    S = zp.shape[0]
    T = T_DIL

    def body(q_ref, k_ref, gq_ref, gk_ref, C_ref, S1_ref, S2_ref, qo_ref, ko_ref):
        C, S1, S2 = C_ref[...], S1_ref[...], S2_ref[...]
        for b in range(12):
            sl = slice(b * LANE, (b + 1) * LANE)
            x = q_ref[:, sl]
            qo_ref[:, sl] = _rope(x * _head_stats_mxu(x) * gq_ref[...], C, S1, S2, 32)
            x = k_ref[:, sl]
            ko_ref[:, sl] = _rope(x * _head_stats_mxu(x) * gk_ref[...], C, S1, S2, 32)

    tabspec = pl.BlockSpec((T, LANE), lambda i: (i, 0))
    out = pl.BlockSpec((T, 1536), lambda i: (i, 0))
    return pl.pallas_call(
        body, name="dil_pre_fwd", grid=(S // T,),
        in_specs=[_zcol(T, 1536, C_DQ), _zcol(T, 1536, C_DK), _full((1, LANE)), _full((1, LANE)), tabspec, tabspec, tabspec],
        out_specs=[out, out], out_shape=[jax.ShapeDtypeStruct((S, 1536), F32)] * 2, compiler_params=_cp(),
    )(zp, zp, w["g_dq"], w["g_dk"], *tab)


DIL_ROWS = 2048


def _dil_geometry(g, S):
    d = DIL_DILATIONS[g]
    P = NK * d
    return d, P, DIL_ROWS // P, S // P


def _dil_rows(start, d, blocks=1):
    return pl.ds(pl.multiple_of(start, NK), blocks * NK) if d == 1 else pl.ds(start, blocks * NK, stride=d)


def _dil_specs(g, S, col0):
    _, P, m, nb = _dil_geometry(g, S)
    cur = pl.BlockSpec((DIL_ROWS, LANE), lambda sb, c: (sb, col0 + c))
    prv = pl.BlockSpec((P, LANE), lambda sb, c: (jnp.maximum(sb * m - 1, 0), col0 + c))
    nxt = pl.BlockSpec((P, LANE), lambda sb, c: (jnp.minimum((sb + 1) * m, nb - 1), col0 + c))
    return cur, prv, nxt


def _dil_attn_fwd(q, k, zp, g):
    S = q.shape[0]
    d, P, m, nb = _dil_geometry(g, S)
    R = DIL_ROWS

    def body(q_ref, kc_ref, kp_ref, vc_ref, vp_ref, o_ref, lse_ref, *scr):
        sb = pl.program_id(0)
        if m > 1:
            ks_ref, vs_ref = scr
            ks_ref[0:P, :] = kp_ref[...]
            ks_ref[P:P + R, :] = kc_ref[...]
            vs_ref[0:P, :] = vp_ref[...]
            vs_ref[P:P + R, :] = vc_ref[...]
        lane = _lane((NK, LANE))

        def unit(u, carry):
            j = u // d
            start = j * P + (u - j * d)
            rows = _dil_rows(start, d)
            if m > 1:
                k2, v2 = ks_ref[_dil_rows(start, d, 2), :], vs_ref[_dil_rows(start, d, 2), :]
            else:
                k2 = jnp.concatenate([kp_ref[rows, :], kc_ref[rows, :]], axis=0)
                v2 = jnp.concatenate([vp_ref[rows, :], vc_ref[rows, :]], axis=0)
            k2, v2 = k2.astype(BF16), v2.astype(BF16)
            q_ = q_ref[rows, :].astype(BF16)
            row = lax.broadcasted_iota(jnp.int32, (NK, 2 * NK), 0)
            col = lax.broadcasted_iota(jnp.int32, (NK, 2 * NK), 1)
            band = (col >= row) & (col <= row + NK) & ((col >= NK) | (sb * m + j > 0))
            lane2 = _lane((2 * NK, LANE))
            zb, zv = jnp.zeros_like(q_), jnp.zeros_like(v2)
            o_tot = jnp.zeros((NK, LANE), F32)
            lse_tot = jnp.zeros((NK, LANE), F32)
            for hh in range(2):
                hm = (lane < 64) if hh == 0 else (lane >= 64)
                hm2 = (lane2 < 64) if hh == 0 else (lane2 >= 64)
                s_ = jnp.where(band, _nt(jnp.where(hm, q_, zb), k2) * DIL_SCALE, NEG)
                mx = jnp.max(s_, axis=-1, keepdims=True)
                e = jnp.exp(s_ - mx)
                den = _rsum(e)
                o_tot = o_tot + _nn(e.astype(BF16), jnp.where(hm2, v2, zv)) / den
                lse_tot = jnp.where(hm, mx + jnp.log(den), lse_tot)
            o_ref[rows, :] = o_tot
            lse_ref[rows, :] = lse_tot
            return carry

        lax.fori_loop(0, R // NK, unit, 0, unroll=8)

    qcur, qprv, _ = _dil_specs(g, S, 4 * g)
    vcur, vprv, _ = _dil_specs(g, S, C_DV + 4 * g)
    out = pl.BlockSpec((R, LANE), lambda sb, c: (sb, c))
    return pl.pallas_call(
        body, name=f"dil_attn_fwd{g}", grid=(S // R, 4), in_specs=[qcur, qcur, qprv, vcur, vprv], out_specs=[out, out],
        out_shape=[jax.ShapeDtypeStruct((S, 512), F32)] * 2,
        scratch_shapes=[pltpu.VMEM((P + R, LANE), F32)] * 2 if m > 1 else [], compiler_params=_cp(),
    )(q, k, k, zp, zp)


def _dil_combine(os_, ls_, zp):
    S = zp.shape[0]
    T = T_ROW

    def body(o0, o1, o2, l0, l1, l2, g_ref, oc_ref, L_ref, y_ref):
        a, b, c = l0[...], l1[...], l2[...]
        mx = jnp.maximum(jnp.maximum(a, b), c)
        ea, eb, ec = jnp.exp(a - mx), jnp.exp(b - mx), jnp.exp(c - mx)
        den = ea + eb + ec
        oc = (ea * o0[...] + eb * o1[...] + ec * o2[...]) / den
        oc_ref[...] = oc
        L_ref[...] = mx + jnp.log(den)
        y_ref[...] = (oc * _silu(g_ref[...])).astype(BF16)

    row = pl.BlockSpec((T, 512), lambda i: (i, 0))
    return pl.pallas_call(
        body, name="dil_combine", grid=(S // T,), in_specs=[row] * 6 + [_zcol(T, 512, C_DILG)], out_specs=[row, row, row],
        out_shape=[jax.ShapeDtypeStruct((S, 512), F32), jax.ShapeDtypeStruct((S, 512), F32), jax.ShapeDtypeStruct((S, 512), BF16)],
        compiler_params=_cp(),
    )(*os_, *ls_, zp)


def _dil_comb_bwd(zp, oc, dy, dz):
    S = zp.shape[0]
    T = T_ROW

    def body(g_ref, o_ref, dy_ref, dz_in, dz_ref, do_ref, D_ref):
        del dz_in
        g, o_, dy_ = g_ref[...], o_ref[...], dy_ref[...]
        do = dy_ * _silu(g)
        do_ref[...] = do
        dz_ref[...] = (dy_ * o_ * _dsilu(g)).astype(BF16)
        lane = _lane((T, LANE))
        for p in range(4):
            sl = slice(p * LANE, (p + 1) * LANE)
            D_ref[:, sl] = _head_sum(do[:, sl] * o_[:, sl], lane)

    row = pl.BlockSpec((T, 512), lambda i: (i, 0))
    zc = _zcol(T, 512, C_DILG)
    return pl.pallas_call(
        body, name="dil_comb_bwd", grid=(S // T,), in_specs=[zc, row, row, pl.BlockSpec(memory_space=pl.ANY)], out_specs=[zc, row, row],
        out_shape=[jax.ShapeDtypeStruct(dz.shape, BF16), jax.ShapeDtypeStruct((S, 512), F32), jax.ShapeDtypeStruct((S, 512), F32)],
        input_output_aliases={3: 0}, compiler_params=_cp(),
    )(zp, oc, dy, dz)


def _dil_attn_bwd(q, k, zp, do, L, Dr, g):
    S = q.shape[0]
    d, P, m, nb = _dil_geometry(g, S)
    R = DIL_ROWS
    n_q, n_k = 4, 2

    def body(*refs):
        q_side = refs[0:2 * n_q]
        k_side = refs[2 * n_q:2 * n_q + 2 * n_k]
        dq_ref, dk_ref, dv_ref = refs[2 * n_q + 2 * n_k:2 * n_q + 2 * n_k + 3]
        scr = refs[2 * n_q + 2 * n_k + 3:]
        sb = pl.program_id(0)
        if m > 1:
            for a in range(n_q):
                scr[a][0:R, :] = q_side[2 * a][...]
                scr[a][R:R + P, :] = q_side[2 * a + 1][...]
            for a in range(n_k):
                scr[n_q + a][0:P, :] = k_side[2 * a + 1][...]
                scr[n_q + a][P:P + R, :] = k_side[2 * a][...]
        lane = _lane((NK, LANE))

        def unit(u, carry):
            j = u // d
            start = j * P + (u - j * d)
            rows = _dil_rows(start, d)
            if m > 1:
                rows_b = _dil_rows(start + P, d)
                q2, do2, L2, D2 = [scr[a][_dil_rows(start, d, 2), :] for a in range(n_q)]
                kp, vp = [scr[n_q + a][rows, :] for a in range(n_k)]
                kc, vc = [scr[n_q + a][rows_b, :] for a in range(n_k)]
            else:
                q2, do2, L2, D2 = [jnp.concatenate([q_side[2 * a][rows, :], q_side[2 * a + 1][rows, :]], axis=0) for a in range(n_q)]
                kc, vc = [k_side[2 * a][rows, :] for a in range(n_k)]
                kp, vp = [k_side[2 * a + 1][rows, :] for a in range(n_k)]
            q2, do2 = q2.astype(BF16), do2.astype(BF16)
            kc, kp, vc, vp = kc.astype(BF16), kp.astype(BF16), vc.astype(BF16), vp.astype(BF16)
            n = sb * m + j
            hA = _lane((2 * NK, LANE)) < 64
            zq = jnp.zeros_like(q2)
            L2r, D2r = pltpu.roll(L2, 64, 1), pltpu.roll(D2, 64, 1)
            Q4 = jnp.concatenate([jnp.where(hA, q2, zq), jnp.where(hA, zq, q2)], axis=0)
            O4 = jnp.concatenate([jnp.where(hA, do2, zq), jnp.where(hA, zq, do2)], axis=0)
            L4 = jnp.concatenate([jnp.where(hA, L2, L2r), jnp.where(hA, L2r, L2)], axis=0)
            D4 = jnp.concatenate([jnp.where(hA, D2, D2r), jnp.where(hA, D2r, D2)], axis=0)
            row4 = lax.broadcasted_iota(jnp.int32, (4 * NK, NK), 0) & (2 * NK - 1)
            col4 = lax.broadcasted_iota(jnp.int32, (4 * NK, NK), 1)
            m4 = ((row4 < NK) & (col4 <= row4)) | ((row4 >= NK) & (col4 >= row4 - NK) & (n < nb - 1))
            p4 = jnp.exp(jnp.where(m4, _nt(Q4, kc) * DIL_SCALE, NEG) - L4)
            ds4 = (p4 * (_nt(O4, vc) - D4) * DIL_SCALE).astype(BF16)
            dk_tot = _tn(ds4, Q4)
            dv_tot = _tn(p4.astype(BF16), O4)
            pick = lambda x: jnp.concatenate([x[0:NK], x[2 * NK:3 * NK]], axis=0)
            Qn, On, Ln, Dn = pick(Q4), pick(O4), pick(L4), pick(D4)
            rowp = lax.broadcasted_iota(jnp.int32, (2 * NK, NK), 0) & (NK - 1)
            colp = lax.broadcasted_iota(jnp.int32, (2 * NK, NK), 1)
            pp = jnp.exp(jnp.where((colp >= rowp) & (n > 0), _nt(Qn, kp) * DIL_SCALE, NEG) - Ln)
            dsp = (pp * (_nt(On, vp) - Dn) * DIL_SCALE).astype(BF16)
            dq2 = _nn(pick(ds4), kc) + _nn(dsp, kp)
            dq_tot = jnp.where(lane < 64, dq2[0:NK], dq2[NK:2 * NK])
            dq_ref[rows, :] = dq_tot
            dk_ref[rows, :] = dk_tot
            dv_ref[rows, :] = dv_tot
            return carry

        lax.fori_loop(0, R // NK, unit, 0, unroll=8)

    qcur, qprv, qnxt = _dil_specs(g, S, 4 * g)
    vcur, vprv, _ = _dil_specs(g, S, C_DV + 4 * g)
    ocur, _, onxt = _dil_specs(g, S, 0)
    out = pl.BlockSpec((R, LANE), lambda sb, c: (sb, c))
    scratch = [pltpu.VMEM((P + R, LANE), F32)] * (n_q + n_k) if m > 1 else []
    return pl.pallas_call(
        body, name=f"dil_attn_bwd{g}", grid=(S // R, 4),
        in_specs=[qcur, qnxt, ocur, onxt, ocur, onxt, ocur, onxt, qcur, qprv, vcur, vprv],
        out_specs=[out, out, out], out_shape=[jax.ShapeDtypeStruct((S, 512), F32)] * 3, scratch_shapes=scratch, compiler_params=_cp(),
    )(q, q, do, do, L, L, Dr, Dr, k, k, zp, zp)


def _dil_pre_bwd(zp, dys, g, tab, dz, col, name):
    S = zp.shape[0]
    T = T_DIL

    def body(x_ref, dy0_ref, dy1_ref, dy2_ref, g_ref, C_ref, S1_ref, S2_ref, dz_in, dz_ref, dg_ref):
        del dz_in
        i = pl.program_id(0)
        C, S1, S2 = C_ref[...], S1_ref[...], S2_ref[...]
        lane = _lane((T, LANE))
        gv = g_ref[...]
        acc = jnp.zeros((1, LANE), F32)
        for b in range(12):
            sl = slice(b * LANE, (b + 1) * LANE)
            x = x_ref[:, sl]
            r = _head_stats(x, lane)
            xn = x * r
            dy_ref = (dy0_ref, dy1_ref, dy2_ref)[b // 4]
            dyn = _rope_t(dy_ref[:, (b % 4) * LANE:(b % 4 + 1) * LANE], C, S1, S2, 32)
            acc = acc + _csum(dyn * xn)
            dxh = dyn * gv
            dz_ref[:, sl] = (r * (dxh - xn * _head_sum(dxh * xn, lane) * (1.0 / DIL_HD))).astype(BF16)

        @pl.when(i == 0)
        def _():
            dg_ref[...] = acc

        @pl.when(i > 0)
        def _():
            dg_ref[...] += acc

    tabspec = pl.BlockSpec((T, LANE), lambda i: (i, 0))
    zc = _zcol(T, 1536, col)
    grp = pl.BlockSpec((T, 512), lambda i: (i, 0))
    return pl.pallas_call(
        body, name=name, grid=(S // T,),
        in_specs=[zc, grp, grp, grp, _full((1, LANE)), tabspec, tabspec, tabspec, pl.BlockSpec(memory_space=pl.ANY)],
        out_specs=[zc, _full((1, LANE))], out_shape=[jax.ShapeDtypeStruct(dz.shape, BF16), jax.ShapeDtypeStruct((1, LANE), F32)],
        input_output_aliases={8: 0}, compiler_params=_cp(),
    )(zp, *dys, g, *tab, dz)


def _dil_dv_into(dvs, dz):
    S = dz.shape[0]
    T = T_ROW

    def body(s0, s1, s2, dz_in, o_ref):
        del dz_in
        for gi, s in enumerate((s0, s1, s2)):
            o_ref[:, gi * 512:(gi + 1) * 512] = s[...].astype(BF16)

    grp = pl.BlockSpec((T, 512), lambda i: (i, 0))
    return pl.pallas_call(
        body, name="dil_dv", grid=(S // T,), in_specs=[grp, grp, grp, pl.BlockSpec(memory_space=pl.ANY)],
        out_specs=_zcol(T, 1536, C_DV), out_shape=jax.ShapeDtypeStruct(dz.shape, BF16), input_output_aliases={3: 0}, compiler_params=_cp(),
    )(*dvs, dz)


T_MRG = 256


def _merge_fwd(P, zp, b_merge):
    S = zp.shape[0]
    T = T_MRG

    def body(p0, p1, p2, m0, m1, m2, b_ref, o_ref):
        acc = jnp.zeros((T, D), F32)
        for j, (p, m) in enumerate(((p0, m0), (p1, m1), (p2, m2))):
            acc = acc + _sig(m[...] + b_ref[:, j * D:(j + 1) * D]) * p[...]
        o_ref[...] = acc.astype(BF16)

    row = pl.BlockSpec((T, D), lambda i: (i, 0))
    return pl.pallas_call(
        body, name="merge_fwd", grid=(S // T,),
        in_specs=[row, row, row] + [_zcol(T, D, C_MERGE + 8 * j) for j in range(3)] + [_full((1, 3 * D))], out_specs=row,
        out_shape=jax.ShapeDtypeStruct((S, D), BF16), compiler_params=_cp(),
    )(*P, zp, zp, zp, b_merge)


def _merge_bwd(dm, Pj, zp, bj, dz, j):
    S = zp.shape[0]
    T = T_MRG

    def body(dm_ref, p_ref, m_ref, b_ref, dz_in, dz_ref, dp_ref, db_ref):
        del dz_in
        i = pl.program_id(0)
        g = _sig(m_ref[...] + b_ref[...])
        dmv = dm_ref[...]
        dp_ref[...] = (dmv * g).astype(BF16)
        dg = dmv * p_ref[...] * g * (1.0 - g)
        dz_ref[...] = dg.astype(BF16)
        part = _csum(dg)

        @pl.when(i == 0)
        def _():
            db_ref[...] = part

        @pl.when(i > 0)
        def _():
            db_ref[...] += part

    row = pl.BlockSpec((T, D), lambda i: (i, 0))
    zc = _zcol(T, D, C_MERGE + 8 * j)
    return pl.pallas_call(
        body, name=f"merge_bwd{j}", grid=(S // T,), in_specs=[row, row, zc, _full((1, D)), pl.BlockSpec(memory_space=pl.ANY)],
        out_specs=[zc, row, _full((1, D))],
        out_shape=[jax.ShapeDtypeStruct(dz.shape, BF16), jax.ShapeDtypeStruct((S, D), BF16), jax.ShapeDtypeStruct((1, D), F32)],
        input_output_aliases={4: 0}, compiler_params=_cp(),
    )(dm, Pj, zp, bj, dz)


def _loss_fwd_bwd(y, target):
    S = y.shape[0]
    T = T_ROW

    def body(y_ref, t_ref, loss_ref, dy_ref):
        i = pl.program_id(0)
        err = y_ref[...] - t_ref[...]
        dy_ref[...] = err * (1.0 / D)
        part = jnp.sum(err * err, keepdims=True).reshape(1, 1) * (0.5 / D)

        @pl.when(i == 0)
        def _():
            loss_ref[...] = part

        @pl.when(i > 0)
        def _():
            loss_ref[...] += part

    row = pl.BlockSpec((T, D), lambda i: (i, 0))
    return pl.pallas_call(
        body, name="loss", grid=(S // T,), in_specs=[row, row], out_specs=[_full((1, 1)), row],
        out_shape=[jax.ShapeDtypeStruct((1, 1), F32), jax.ShapeDtypeStruct((S, D), F32)], compiler_params=_cp(),
    )(y, target)


def _layer_fwd(x, w, tabs):
    mla_tab, dil_tab = tabs
    S = x.shape[0]
    h = _rms_in_fwd(x, w["norm_g"])
    zp = _mm(h, w["w_in"], mode="nn", name="in_proj")
    hs, y_lru = _lru_fwd(zp, w)
    q, k, v = _mla_pre_fwd(zp, w, mla_tab)
    o_mla, lse, y_mla = _mla_attn_fwd(q, k, v, zp)
    qd, kd = _dil_pre_fwd(zp, w, dil_tab)
    og, lg = zip(*[_dil_attn_fwd(qd, kd, zp, g) for g in range(len(DIL_DILATIONS))])
    oc, L, y_dil = _dil_combine(og, lg, zp)
    P = [_mm(y_lru, w["w_lru_o"], mode="nn", name="lru_out"), _mm(y_mla, w["w_mla_o"], mode="nn", name="mla_out"),
         _mm(y_dil, w["w_dil_o"], mode="nn", name="dil_out")]
    merged = _merge_fwd(P, zp, w["b_merge"])
    x_out = _mm(merged, w["w_out"], mode="nn", name="out_proj", add=x)
    saved = dict(x=x, h=h, zp=zp, hs=hs, y=(y_lru, y_mla, y_dil), q=q, k=k, v=v, o_mla=o_mla, lse=lse, qd=qd, kd=kd, oc=oc, L=L, P=P,
                 merged=merged)
    return x_out, saved


def _layer_bwd(dout, w, tabs, sv, hook=None, after=None):
    mla_tab, dil_tab = tabs
    zp = sv["zp"]
    S = zp.shape[0]
    g = {}
    dm = _mm(dout, w["w_out"], mode="nt", name="d_merged", after=after)
    g["w_out"] = _mm(sv["merged"], dout, mode="tn", name="dw_out", out_dtype=BF16)
    dz = lax.empty((S, ZW), BF16)
    dP, db = [], []
    for j in range(3):
        dz, dpj, dbj = _merge_bwd(dm, sv["P"][j], zp, w["b_merge"][:, j * D:(j + 1) * D], dz, j)
        dP.append(dpj)
        db.append(dbj)
    g["b_merge"] = jnp.concatenate(db, axis=1)
    names = ("w_lru_o", "w_mla_o", "w_dil_o")
    dy = []
    for j in range(3):
        dy.append(_mm(dP[j], w[names[j]], mode="nt", name="dy_" + names[j]))
        g[names[j]] = _mm(sv["y"][j], dP[j], mode="tn", name="d" + names[j], out_dtype=BF16)
    dz = _lru_gate_bwd(zp, sv["hs"], dy[0], dz)
    dz, g["conv_w"], g["conv_b"], g["w_gx"], g["b_gx"], g["w_ga"], g["b_ga"], g["lam"] = _lru_bwd(zp, sv["hs"], dy[0], w, dz)
    dz, do, Dr = _mla_post_bwd(zp, sv["o_mla"], dy[1], dz)
    dq, dk, dv = _mla_attn_bwd(sv["q"], sv["k"], sv["v"], do, sv["lse"], Dr)
    dz, g["w_uq"], g["w_uk"], g["w_uv"], g["g_cq"], g["g_ckv"], g["g_mq"], g["g_mk"] = _mla_pre_bwd(zp, dq, dk, dv, w, mla_tab, dz)
    dz, dod, Dd = _dil_comb_bwd(zp, sv["oc"], dy[2], dz)
    dqs, dks, dvs = zip(*[_dil_attn_bwd(sv["qd"], sv["kd"], zp, dod, sv["L"], Dd, gi) for gi in range(len(DIL_DILATIONS))])
    dz, g["g_dq"] = _dil_pre_bwd(zp, dqs, w["g_dq"], dil_tab, dz, C_DQ, "dil_pre_bwd_q")
    dz, g["g_dk"] = _dil_pre_bwd(zp, dks, w["g_dk"], dil_tab, dz, C_DK, "dil_pre_bwd_k")
    dz = _dil_dv_into(dvs, dz)
    g["w_in"] = _mm(sv["h"], dz, mode="tn", name="dw_in", out_dtype=BF16, tk=S)
    token = hook(g) if hook is not None else None
    dh = _mm(dz, w["w_in"], mode="nt", name="d_h", after=token, tk=ZW // 4)
    dx, g["norm_g"] = _rms_in_bwd(sv["x"], w["norm_g"], dh, dout)
    return dx, g


def _peers():
    mx, my, mc = lax.axis_index("x"), lax.axis_index("y"), lax.axis_index("c")
    me = 4 * mx + 2 * my + mc
    out = []
    for k in range(1, N_DEV):
        px = 1 - mx if k & 4 else mx
        py = 1 - my if k & 2 else my
        pc = 1 - mc if k & 1 else mc
        out.append(((px, py, pc), 4 * px + 2 * py + pc))
    return me, out


def _whole(ref, p):
    del p
    return ref


def _exchange(srcs, slicers, slices, name):
    n = len(srcs)

    def body(*refs):
        ins, outs = refs[:n], refs[n:2 * n]
        send_sems, recv_sems, local_sems = refs[2 * n:]
        me, peers = _peers()
        mine = [pltpu.make_async_copy(slicers[a](ins[a], me), outs[a].at[me], local_sems.at[a]) for a in range(n)]
        for cp in mine:
            cp.start()
        copies = []
        for k, (peer, pidx) in enumerate(peers):
            for a in range(n):
                cp = pltpu.make_async_remote_copy(
                    src_ref=slicers[a](ins[a], pidx), dst_ref=outs[a].at[me], send_sem=send_sems.at[k * n + a],
                    recv_sem=recv_sems.at[k * n + a], device_id=peer, device_id_type=pl.DeviceIdType.MESH)
                cp.start()
                copies.append(cp)
        for cp in copies + mine:
            cp.wait()

    nsem = (N_DEV - 1) * n
    return pl.pallas_call(
        body, name=name, out_shape=[jax.ShapeDtypeStruct((N_DEV,) + shp, dt) for shp, dt in slices],
        in_specs=[pl.BlockSpec(memory_space=pl.ANY)] * n, out_specs=[pl.BlockSpec(memory_space=pl.ANY)] * n,
        scratch_shapes=[pltpu.SemaphoreType.DMA((nsem,)), pltpu.SemaphoreType.DMA((nsem,)), pltpu.SemaphoreType.DMA((n,))],
        compiler_params=pltpu.CompilerParams(has_side_effects=True),
    )(*srcs)


def _gather_two_level(srcs, name):
    n = len(srcs)

    def body(*refs):
        ins, outs = refs[:n], refs[n:2 * n]
        send_sems, recv_sems, local_sems = refs[2 * n:]
        mx, my, mc = lax.axis_index("x"), lax.axis_index("y"), lax.axis_index("c")
        me, sibling = (mx, my, mc), (mx, my, 1 - mc)
        chips = [(1 - mx, my), (mx, 1 - my), (1 - mx, 1 - my)]
        slot = lambda d: 4 * d[0] + 2 * d[1] + d[2]

        def copy(j, a, block, to, own=False):
            return pltpu.make_async_remote_copy(
                src_ref=ins[a] if own else outs[a].at[slot(block)], dst_ref=outs[a].at[slot(block)],
                send_sem=send_sems.at[j * n + a], recv_sem=recv_sems.at[j * n + a], device_id=to, device_id_type=pl.DeviceIdType.MESH)

        mine = [pltpu.make_async_copy(ins[a], outs[a].at[slot(me)], local_sems.at[a]) for a in range(n)]
        first = [copy(1 + j, a, me, (*chip, mc), own=True) for j, chip in enumerate(chips) for a in range(n)]
        first += [copy(0, a, me, sibling, own=True) for a in range(n)]
        for cp in mine + first:
            cp.start()
        passed = []
        for j, chip in enumerate(chips):
            for a in range(n):
                copy(1 + j, a, (*chip, mc), me).wait_recv()
                cp = copy(4 + j, a, (*chip, mc), sibling)
                cp.start()
                passed.append(cp)
        for a in range(n):
            copy(0, a, sibling, me).wait_recv()
        for j, chip in enumerate(chips):
            for a in range(n):
                copy(4 + j, a, (*chip, 1 - mc), me).wait_recv()
        for cp in first + passed:
            cp.wait_send()
        for cp in mine:
            cp.wait()

    nsem = (N_DEV - 1) * n
    return pl.pallas_call(
        body, name=name, out_shape=[jax.ShapeDtypeStruct((N_DEV,) + a.shape, a.dtype) for a in srcs],
        in_specs=[pl.BlockSpec(memory_space=pl.ANY)] * n, out_specs=[pl.BlockSpec(memory_space=pl.ANY)] * n,
        scratch_shapes=[pltpu.SemaphoreType.DMA((nsem,)), pltpu.SemaphoreType.DMA((nsem,)), pltpu.SemaphoreType.DMA((n,))],
        compiler_params=pltpu.CompilerParams(has_side_effects=True),
    )(*srcs)


_HBM = pl.BlockSpec(memory_space=pltpu.HBM)
_SEM = pl.BlockSpec(memory_space=pltpu.SEMAPHORE)
_DATAFLOW = pltpu.SideEffectType.DATAFLOW_SIDE_EFFECTING


def _plan_chips():
    mx, my, mc = lax.axis_index("x"), lax.axis_index("y"), lax.axis_index("c")
    return 2 * mx + my, [((cx, cy, mc), 2 * cx + cy) for cx, cy in ((1 - mx, my), (mx, 1 - my), (1 - mx, 1 - my))]


def _pair_exchange(srcs, slicers, slices, sliced, name):
    n = len(srcs)
    pieces = [4 if s else 1 for s in sliced]

    def body(*refs):
        ins, outs = refs[:n], refs[n:2 * n]
        send_sems, recv_sems = refs[2 * n:]
        mx, my, mc = lax.axis_index("x"), lax.axis_index("y"), lax.axis_index("c")
        copies = []
        for a in range(n):
            for q in range(pieces[a]):
                i = len(copies)
                copies.append(pltpu.make_async_remote_copy(
                    src_ref=slicers[a](ins[a], 2 * q + 1 - mc) if sliced[a] else ins[a], dst_ref=outs[a].at[q],
                    send_sem=send_sems.at[i], recv_sem=recv_sems.at[i], device_id=(mx, my, 1 - mc), device_id_type=pl.DeviceIdType.MESH))
        for cp in copies:
            cp.start()
        for cp in copies:
            cp.wait()

    return pl.pallas_call(
        body, name=name, out_shape=[jax.ShapeDtypeStruct((p,) + shp, dt) for (shp, dt), p in zip(slices, pieces)],
        in_specs=[pl.BlockSpec(memory_space=pl.ANY)] * n, out_specs=[pl.BlockSpec(memory_space=pl.ANY)] * n,
        scratch_shapes=[pltpu.SemaphoreType.DMA((sum(pieces),)), pltpu.SemaphoreType.DMA((sum(pieces),))],
        compiler_params=pltpu.CompilerParams(has_side_effects=True),
    )(*srcs)


def _pair_add(src, came, first_blk, axis, name):
    _, r, c = came.shape
    nblk = (c if axis == 1 else r) // LANE
    if axis == 1:
        s_spec = pl.BlockSpec((r, LANE), lambda q, j, fb: (0, fb[q] + j))
        o_spec = pl.BlockSpec((1, r, LANE), lambda q, j, fb: (q, 0, j))
    else:
        s_spec = pl.BlockSpec((LANE, c), lambda q, j, fb: (fb[q] + j, 0))
        o_spec = pl.BlockSpec((1, LANE, c), lambda q, j, fb: (q, j, 0))

    def body(fb_ref, x_ref, y_ref, o_ref):
        del fb_ref
        o_ref[0] = (x_ref[...].astype(F32) + y_ref[0].astype(F32)).astype(o_ref.dtype)

    return pl.pallas_call(
        body, name=name, out_shape=jax.ShapeDtypeStruct(came.shape, came.dtype),
        grid_spec=pltpu.PrefetchScalarGridSpec(num_scalar_prefetch=1, grid=(4, nblk), in_specs=[s_spec, o_spec], out_specs=o_spec),
        compiler_params=_cp(),
    )(first_blk, src, came)


def _add2(x, y, name):
    shp = x.shape
    x, y = x.reshape(-1, shp[-1]), y.reshape(-1, shp[-1])
    R, C = x.shape
    tr = R
    while tr * C * 4 > (1 << 21) and tr % 32 == 0:
        tr //= 2

    def body(x_ref, y_ref, o_ref):
        o_ref[...] = (x_ref[...].astype(F32) + y_ref[...].astype(F32)).astype(o_ref.dtype)

    spec = pl.BlockSpec((tr, C), lambda i: (i, 0))
    return pl.pallas_call(body, name=name, grid=(R // tr,), in_specs=[spec, spec], out_specs=spec,
                          out_shape=jax.ShapeDtypeStruct((R, C), x.dtype), compiler_params=_cp())(x, y).reshape(shp)


def _exchange_start(srcs, slicers, slices, after, name, plan=_peers, nslots=N_DEV):
    n = len(srcs)
    nsem = (nslots - 1) * n
    lands = [lax.empty((nslots,) + shp, dt) for shp, dt in slices]

    def body(*refs):
        ins, lands_in = refs[:n], refs[n:2 * n]
        send_sems, recv_sems, local_sems = refs[2 * n + 1], refs[2 * n + 2], refs[2 * n + 3]
        token = refs[-1]
        me, peers = plan()
        for a in range(n):
            pltpu.make_async_copy(slicers[a](ins[a], me), lands_in[a].at[me], local_sems.at[a]).start()
        for k, (peer, pidx) in enumerate(peers):
            for a in range(n):
                pltpu.make_async_remote_copy(
                    src_ref=slicers[a](ins[a], pidx), dst_ref=lands_in[a].at[me], send_sem=send_sems.at[k * n + a],
                    recv_sem=recv_sems.at[k * n + a], device_id=peer, device_id_type=pl.DeviceIdType.MESH).start()
        token[...] = jnp.zeros_like(token)

    hbm = lambda a: pltpu.with_memory_space_constraint(a, pltpu.HBM)
    return pl.pallas_call(
        body, name=name,
        out_shape=(pltpu.SemaphoreType.DMA((nsem,)), pltpu.SemaphoreType.DMA((nsem,)), pltpu.SemaphoreType.DMA((n,)),
                   *[pltpu.HBM(a.shape, a.dtype) for a in srcs], *[pltpu.HBM(a.shape, a.dtype) for a in lands],
                   jax.ShapeDtypeStruct((SUB, LANE), F32)),
        in_specs=[_HBM] * (2 * n) + [pl.BlockSpec(memory_space=pl.ANY)],
        out_specs=(_SEM, _SEM, _SEM, *[_HBM] * (2 * n), pl.BlockSpec(memory_space=pltpu.VMEM)),
        input_output_aliases={i: 3 + i for i in range(2 * n)},
        compiler_params=pltpu.CompilerParams(has_side_effects=_DATAFLOW),
    )(*[hbm(a) for a in srcs], *[hbm(a) for a in lands], after)


def _exchange_wait(started, slicers, after, name, plan=_peers):
    n = (len(started) - 4) // 2
    sems, thru = started[0:3], started[3:3 + 2 * n]

    def body(*refs):
        srcs, lands = refs[:n], refs[n:2 * n]
        send_sems, recv_sems, local_sems = refs[2 * n], refs[2 * n + 1], refs[2 * n + 2]
        me, peers = plan()
        for k, (peer, pidx) in enumerate(peers):
            for a in range(n):
                cp = pltpu.make_async_remote_copy(
                    src_ref=slicers[a](srcs[a], pidx), dst_ref=lands[a].at[me], send_sem=send_sems.at[k * n + a],
                    recv_sem=recv_sems.at[k * n + a], device_id=peer, device_id_type=pl.DeviceIdType.MESH)
                cp.wait_send()
                cp.wait_recv()
        for a in range(n):
            pltpu.make_async_copy(slicers[a](srcs[a], me), lands[a].at[me], local_sems.at[a]).wait()

    outs = pl.pallas_call(
        body, name=name, out_shape=[pltpu.HBM(a.shape, a.dtype) for a in thru],
        in_specs=[_HBM] * (2 * n) + [_SEM, _SEM, _SEM, pl.BlockSpec(memory_space=pl.ANY)], out_specs=[_HBM] * (2 * n),
        input_output_aliases={i: i for i in range(2 * n)}, compiler_params=pltpu.CompilerParams(has_side_effects=_DATAFLOW),
    )(*thru, *sems, after)
    return outs[n:]


WIN = 13 * LANE


def _win_base(s):
    n = s * SHARD_IN
    a0 = n + jnp.where(n >= _KR0, KR_LANE, 0) + jnp.where(n >= _KR0 + 32, 32, 0)
    return jnp.minimum(a0 // LANE, (ZW - WIN) // LANE)


def _win_offsets(s):
    n = s * SHARD_IN + jnp.arange(SHARD_IN)
    o = s * SHARD_IN - _win_base(s) * LANE
    return n, (o, o + KR_LANE, o + LANE - 32)


def _to_window(shard, s):
    _, offs = _win_offsets(s)
    padded = jnp.pad(shard, ((0, 0), (0, 0), (WIN, WIN)))
    a, b, c = [lax.dynamic_slice(padded, (0, 0, WIN - o), shard.shape[:2] + (WIN,)) for o in offs]
    col = (_win_base(s) * LANE + jnp.arange(WIN))[None, None, :]
    zero = jnp.zeros_like(a)
    return jnp.where(col < _KR0, a, jnp.where((col >= _KR0 + KR_LANE) & (col < _KR0 + KR_LANE + 32), b, jnp.where(col >= _KR0 + LANE, c, zero)))


def _from_window(win, s):
    n, offs = _win_offsets(s)
    a, b, c = [lax.dynamic_slice(win, (0, 0, o), win.shape[:2] + (SHARD_IN,)) for o in offs]
    return jnp.where((n < _KR0)[None, None, :], a, jnp.where((n < _KR0 + 32)[None, None, :], b, c))


def _win_base_static(s):
    n = s * SHARD_IN
    a0 = n + (KR_LANE if n >= _KR0 else 0) + (32 if n >= _KR0 + 32 else 0)
    return min(a0 // LANE, (ZW - WIN) // LANE)


def _assemble_w_in(gw):
    tr = 128
    bases = [_win_base_static(s) for s in range(N_DEV)]

    def body(g_ref, o_ref):
        for j in range(ZW // LANE):
            acc = None
            for s in range(N_DEV):
                if bases[s] <= j < bases[s] + WIN // LANE:
                    piece = g_ref[s, :, (j - bases[s]) * LANE:(j - bases[s] + 1) * LANE]
                    acc = piece if acc is None else acc + piece
            o_ref[:, j * LANE:(j + 1) * LANE] = acc

    return pl.pallas_call(
        body, name="assemble_w_in", grid=(D // tr,), in_specs=[pl.BlockSpec((N_DEV, tr, WIN), lambda i: (0, i, 0))],
        out_specs=pl.BlockSpec((tr, ZW), lambda i: (i, 0)), out_shape=jax.ShapeDtypeStruct((D, ZW), gw.dtype), compiler_params=_cp(),
    )(gw)


def _cols(width):
    return lambda ref, p: ref.at[:, pl.ds(pl.multiple_of(p * width, width), width)]


def _rows(height):
    return lambda ref, p: ref.at[pl.ds(pl.multiple_of(p * height, height), height), :]


SCATTER = {
    'w_in': (lambda ref, p: ref.at[:, pl.ds(pl.multiple_of(_win_base(p) * LANE, LANE), WIN)], (D, WIN), BF16),
    'conv_w': (_cols(LANE), (4, LANE), F32),
    'w_lru_o': (_rows(LANE), (LANE, D), BF16),
    'w_uq': (_cols(LANE), (256, LANE), F32),
    'w_ukv': (_cols(LANE), (128, LANE), F32),
    'w_mla_o': (_cols(LANE), (512, LANE), BF16),
    'w_dil_o': (_cols(LANE), (512, LANE), BF16),
    'w_out': (_rows(LANE), (LANE, D), BF16),
}
SLICED_AXIS = {'w_in': 1, 'conv_w': 1, 'w_lru_o': 0, 'w_uq': 1, 'w_ukv': 1, 'w_mla_o': 1, 'w_dil_o': 1, 'w_out': 0}


PACK_ROWS = 64


def _packed_rows(shapes):
    n = sum(int(np.prod(s)) for s in shapes)
    return -(-n // (PACK_ROWS * LANE)) * PACK_ROWS


def _sum8(buf, name):
    ns, R, C = buf.shape
    tr = R
    while tr * C * 4 * ns > (1 << 22) and tr % 32 == 0:
        tr //= 2

    def body(b_ref, o_ref):
        acc = b_ref[0].astype(F32)
        for s in range(1, ns):
            acc = acc + b_ref[s].astype(F32)
        o_ref[...] = acc

    return pl.pallas_call(
        body, name=name, grid=(R // tr,), in_specs=[pl.BlockSpec((ns, tr, C), lambda i: (0, i, 0))],
        out_specs=pl.BlockSpec((tr, C), lambda i: (i, 0)), out_shape=jax.ShapeDtypeStruct((R, C), F32), compiler_params=_cp(),
    )(buf)


def _pack(arrs, dtype, lead):
    flat = [a.astype(dtype).reshape(a.shape[:lead] + (-1,)) for a in arrs]
    cat = jnp.concatenate(flat, axis=-1)
    n = cat.shape[-1]
    unit = PACK_ROWS * LANE
    pad = (-n) % unit
    if pad:
        cat = jnp.pad(cat, [(0, 0)] * lead + [(0, pad)])
    return cat.reshape(cat.shape[:lead] + ((n + pad) // LANE, LANE))


def _unpack(buf, shapes, lead):
    flat = buf.reshape(buf.shape[:lead] + (-1,))
    out, off = [], 0
    for shp in shapes:
        n = int(np.prod(shp))
        out.append(flat[..., off:off + n].reshape(buf.shape[:lead] + tuple(shp)))
        off += n
    return out


def _adamw(w, g, m, v, name):
    layers, rows, cols = w.shape
    tr = rows
    while tr * cols * 4 > (3 << 19) and tr % 16 == 0:
        tr //= 2
    c1 = 1.0 - ADAM_B1 ** ADAM_STEP
    c2 = 1.0 - ADAM_B2 ** ADAM_STEP

    def body(w_ref, g_ref, m_ref, v_ref, d_ref, mo_ref, vo_ref):
        gv = g_ref[...]
        mn = ADAM_B1 * m_ref[...] + (1.0 - ADAM_B1) * gv
        vn = ADAM_B2 * v_ref[...] + (1.0 - ADAM_B2) * (gv * gv)
        mo_ref[...] = mn
        vo_ref[...] = vn
        d_ref[...] = -ADAM_LR * ((mn / c1) / (jnp.sqrt(vn / c2) + ADAM_EPS) + ADAM_WD * w_ref[...])

    spec = pl.BlockSpec((1, tr, cols), lambda l, i: (l, i, 0))
    return pl.pallas_call(
        body, name=name, grid=(layers, rows // tr), in_specs=[spec] * 4, out_specs=[spec] * 3,
        out_shape=[jax.ShapeDtypeStruct((layers, rows, cols), F32)] * 3, compiler_params=_cp(),
    )(w, g, m, v)


IN_NAMES = ['x', 'positions', 'norm_g', 'w_in', 'conv_w', 'conv_b', 'w_gate_x', 'b_gate_x', 'w_gate_a', 'b_gate_a', 'lru_lambda', 'w_lru_o',
            'cq_norm_g', 'ckv_norm_g', 'w_uq', 'w_ukv', 'mla_q_norm_g', 'mla_k_norm_g', 'w_mla_o', 'dil_q_norm_g', 'dil_k_norm_g', 'w_dil_o',
            'b_merge', 'w_out']
WEIGHTS = IN_NAMES[2:]
REPLICATED = [n for n in WEIGHTS if n not in SCATTER]
GATE_WEIGHTS = ('w_gate_x', 'w_gate_a')

_KR0 = C_KR * LANE


GATHERED = ['w_in', 'w_lru_o', 'w_uq', 'w_ukv', 'w_mla_o', 'w_dil_o', 'w_out', 'conv_w']


def _local_weights(wd, me):
    loc = {n: wd[n].astype(BF16) for n in GATHERED[:-1]}
    loc['w_in'] = _to_window(loc['w_in'], me)
    loc['w_uq'] = jnp.pad(loc['w_uq'], ((0, 0), (0, 0), (0, LANE - MLA_QK)))
    loc['conv_w'] = wd['conv_w']
    return [[loc[n][l] for n in GATHERED] for l in range(DEPTH)]


def _layer_weights(gathered, rep, l):
    gw = dict(zip(GATHERED, gathered))
    by_rows = lambda a: a.reshape(-1, a.shape[-1])
    by_cols = lambda a: jnp.swapaxes(a, 0, 1).reshape(a.shape[1], -1)
    ukv = jnp.swapaxes(gw['w_ukv'], 0, 1)
    g96 = lambda a: jnp.pad(a[l].reshape(1, MLA_QK), ((0, 0), (0, LANE - MLA_QK)))
    g64 = lambda a: jnp.tile(a[l].reshape(1, DIL_HD), (1, 2))
    return dict(
        norm_g=rep['norm_g'][l].reshape(1, D), w_in=_assemble_w_in(gw['w_in']),
        conv_w=by_cols(gw['conv_w']), conv_b=rep['conv_b'][l].reshape(1, D),
        w_gx=rep['w_gate_x'][l].astype(BF16), b_gx=rep['b_gate_x'][l].reshape(8, 1, LANE),
        w_ga=rep['w_gate_a'][l].astype(BF16), b_ga=rep['b_gate_a'][l].reshape(8, 1, LANE),
        lam=rep['lru_lambda'][l].reshape(1, D),
        w_lru_o=by_rows(gw['w_lru_o']), w_mla_o=by_cols(gw['w_mla_o']), w_dil_o=by_cols(gw['w_dil_o']), w_out=by_rows(gw['w_out']),
        g_cq=rep['cq_norm_g'][l].reshape(1, 256), g_ckv=rep['ckv_norm_g'][l].reshape(1, 128),
        w_uq=by_cols(gw['w_uq']), w_uk=jnp.pad(ukv[:, :, :64], ((0, 0), (0, 0), (0, 64))).reshape(128, 1024),
        w_uv=ukv[:, :, 64:].reshape(128, 512),
        g_mq=g96(rep['mla_q_norm_g']), g_mk=g96(rep['mla_k_norm_g']), g_dq=g64(rep['dil_q_norm_g']), g_dk=g64(rep['dil_k_norm_g']),
        b_merge=rep['b_merge'][l].reshape(1, 3 * D),
    )


def _sharded_grads(g):
    uk = g['w_uk'].reshape(128, 8, 128)[:, :, :64]
    uv = g['w_uv'].reshape(128, 8, 64)
    d = {'w_in': g['w_in'], 'conv_w': g['conv_w'], 'w_lru_o': g['w_lru_o'], 'w_uq': g['w_uq'],
         'w_ukv': jnp.concatenate([uk, uv], axis=-1).reshape(128, 1024), 'w_mla_o': g['w_mla_o'], 'w_dil_o': g['w_dil_o'],
         'w_out': g['w_out']}
    return [d[n] for n in SCATTER]


def _replicated_grads(g):
    return {
        'conv_b': g['conv_b'].reshape(D),
        'w_gate_x': g['w_gx'], 'b_gate_x': g['b_gx'].reshape(8, LANE), 'w_gate_a': g['w_ga'], 'b_gate_a': g['b_ga'].reshape(8, LANE),
        'lru_lambda': g['lam'].reshape(D), 'cq_norm_g': g['g_cq'].reshape(256), 'ckv_norm_g': g['g_ckv'].reshape(128),
        'mla_q_norm_g': g['g_mq'][0, :MLA_QK], 'mla_k_norm_g': g['g_mk'][0, :MLA_QK],
        'dil_q_norm_g': g['g_dq'][0, :DIL_HD] + g['g_dq'][0, DIL_HD:], 'dil_k_norm_g': g['g_dk'][0, :DIL_HD] + g['g_dk'][0, DIL_HD:],
        'b_merge': g['b_merge'].reshape(3 * D),
    }


def kernel(x, positions, norm_g, w_in, conv_w, conv_b, w_gate_x, b_gate_x, w_gate_a, b_gate_a, lru_lambda, w_lru_o, cq_norm_g, ckv_norm_g, w_uq, w_ukv, mla_q_norm_g, mla_k_norm_g, w_mla_o, dil_q_norm_g, dil_k_norm_g, w_dil_o, b_merge, w_out, loss_target, m_norm_g, m_w_in, m_conv_w, m_conv_b, m_w_gate_x, m_b_gate_x, m_w_gate_a, m_b_gate_a, m_lru_lambda, m_w_lru_o, m_cq_norm_g, m_ckv_norm_g, m_w_uq, m_w_ukv, m_mla_q_norm_g, m_mla_k_norm_g, m_w_mla_o, m_dil_q_norm_g, m_dil_k_norm_g, m_w_dil_o, m_b_merge, m_w_out, v_norm_g, v_w_in, v_conv_w, v_conv_b, v_w_gate_x, v_b_gate_x, v_w_gate_a, v_b_gate_a, v_lru_lambda, v_w_lru_o, v_cq_norm_g, v_ckv_norm_g, v_w_uq, v_w_ukv, v_mla_q_norm_g, v_mla_k_norm_g, v_w_mla_o, v_dil_q_norm_g, v_dil_k_norm_g, v_w_dil_o, v_b_merge, v_w_out):
    args = (x, positions, norm_g, w_in, conv_w, conv_b, w_gate_x, b_gate_x, w_gate_a, b_gate_a, lru_lambda, w_lru_o, cq_norm_g, ckv_norm_g, w_uq, w_ukv, mla_q_norm_g, mla_k_norm_g, w_mla_o, dil_q_norm_g, dil_k_norm_g, w_dil_o, b_merge, w_out)
    moments_m = (m_norm_g, m_w_in, m_conv_w, m_conv_b, m_w_gate_x, m_b_gate_x, m_w_gate_a, m_b_gate_a, m_lru_lambda, m_w_lru_o, m_cq_norm_g, m_ckv_norm_g, m_w_uq, m_w_ukv, m_mla_q_norm_g, m_mla_k_norm_g, m_w_mla_o, m_dil_q_norm_g, m_dil_k_norm_g, m_w_dil_o, m_b_merge, m_w_out)
    moments_v = (v_norm_g, v_w_in, v_conv_w, v_conv_b, v_w_gate_x, v_b_gate_x, v_w_gate_a, v_b_gate_a, v_lru_lambda, v_w_lru_o, v_cq_norm_g, v_ckv_norm_g, v_w_uq, v_w_ukv, v_mla_q_norm_g, v_mla_k_norm_g, v_w_mla_o, v_dil_q_norm_g, v_dil_k_norm_g, v_w_dil_o, v_b_merge, v_w_out)
    a = dict(zip(IN_NAMES, args))
    wd = {n: a[n] for n in WEIGHTS}
    md = dict(zip(WEIGHTS, moments_m))
    vd = dict(zip(WEIGHTS, moments_v))

    me = 4 * lax.axis_index("x") + 2 * lax.axis_index("y") + lax.axis_index("c")

    assert DEPTH == 2
    xs, tabs = x[0], _rope_tables(positions[0])
    whole = [_whole] * len(GATHERED)
    slicers = [SCATTER[n][0] for n in SCATTER]
    grad_slices = [SCATTER[n][1:3] for n in SCATTER]

    local = _local_weights(wd, me)
    w_slices = [(a.shape, a.dtype) for a in local[0]]
    landed0 = _gather_two_level(local[0], "gather_w0")
    flying = _exchange_start(local[1], whole, w_slices, landed0[0], "gather_w1_start")
    rep0 = dict(wd, norm_g=wd['norm_g'] + flying[-1][0, 0])
    w0 = _layer_weights(landed0, rep0, 0)
    x1, saved0 = _layer_fwd(xs, w0, tabs)
    w1 = _layer_weights(_exchange_wait(flying, whole, x1, "gather_w1_wait"), wd, 1)
    x2, saved1 = _layer_fwd(x1, w1, tabs)
    loss, dx2 = _loss_fwd_bwd(x2, loss_target[0])
    loss = loss[0, 0]

    sharded = list(SCATTER)
    nsh = len(sharded)
    small = [n for n in REPLICATED if n not in GATE_WEIGHTS and n != 'norm_g']

    def outgoing(g):
        r = _replicated_grads(g)
        return (_sharded_grads(g) + [_pack([r[n] for n in small], F32, 0)]
                + [r[n].astype(BF16).reshape(8 * LANE, LANE) for n in GATE_WEIGHTS])

    out_slicers = slicers + [_whole] * 3
    out_slices = grad_slices + [((_packed_rows([wd[n].shape[1:] for n in small]), LANE), F32)] + [((8 * LANE, LANE), BF16)] * 2
    dx1, g1 = _layer_bwd(dx2, w1, tabs, saved1)
    flying1 = _exchange_start(outgoing(g1), out_slicers, out_slices, dx1, "scatter_g1_start")
    later = {}

    names = sharded + ['small'] + list(GATE_WEIGHTS)
    sliced = [True] * nsh + [False] * 3
    by_chip = [(lambda ref, q: ref.at[q])] * nsh + [_whole] * 3

    def send_layer0(g):
        later['got1'] = _exchange_wait(flying1, out_slicers, g['w_in'], "scatter_g1_wait")
        mine = outgoing(g)
        came = _pair_exchange(mine, out_slicers, out_slices, sliced, "pair_g0")
        my_side = 2 * jnp.arange(4, dtype=jnp.int32) + lax.axis_index("c")
        halves = []
        for n, a, c in zip(names, mine, came):
            if n in SCATTER:
                first = _win_base(my_side) if n == 'w_in' else my_side
                halves.append(_pair_add(a, c, first.astype(jnp.int32), SLICED_AXIS[n], f"pair_sum_{n}"))
            else:
                halves.append(_add2(a, c[0], f"pair_sum_{n}"))
        later['flying0'] = _exchange_start(halves, by_chip, out_slices, later['got1'][0], "scatter_g0_start", plan=_plan_chips, nslots=4)
        return later['flying0'][-1]

    grad_x, g0 = _layer_bwd(dx1, w0, tabs, saved0, hook=send_layer0, after=flying1[-1])
    sum1 = [_sum8(b, f"sum_{n}_1") for n, b in zip(names, later['got1'])]
    behind = grad_x[:1, :1] + sum(s_[:1, :1] for s_ in sum1)
    got0 = _exchange_wait(later['flying0'], by_chip, behind, "scatter_g0_wait", plan=_plan_chips)
    sum0 = [_sum8(b, f"sum_{n}_0") for n, b in zip(names, got0)]
    norm_part = _pack([jnp.stack([g['norm_g'].reshape(D) for g in (g0, g1)])], F32, 0)
    norm_sum = _sum8(_exchange([norm_part], [_whole], [(norm_part.shape, F32)], "gather_norm_g")[0], "sum_norm_g")

    gsh = {n: jnp.stack([sum0[i], sum1[i]]) for i, n in enumerate(sharded)}
    gsh['w_in'] = _from_window(gsh['w_in'], me)
    gsh['w_uq'] = gsh['w_uq'][:, :, :MLA_QK]
    grep = {'norm_g': _unpack(norm_sum, [wd['norm_g'].shape], 0)[0]}
    per_layer = [_unpack(s[nsh], [wd[n].shape[1:] for n in small], 0) for s in (sum0, sum1)]
    grep.update({n: jnp.stack([per_layer[l][i] for l in range(DEPTH)]) for i, n in enumerate(small)})
    for i, n in enumerate(GATE_WEIGHTS):
        grep[n] = jnp.stack([sum0[nsh + 1 + i], sum1[nsh + 1 + i]]).reshape(wd[n].shape)

    out_g, out_d, out_m, out_v = {}, {}, {}, {}
    vecs = ['norm_g'] + small
    vshapes = [wd[n].shape for n in vecs]
    packed_g = _pack([grep[n] for n in vecs], F32, 0)
    d_, m_, v_ = _adamw(_pack([wd[n] for n in vecs], F32, 0)[None], packed_g[None], _pack([md[n] for n in vecs], F32, 0)[None],
                        _pack([vd[n] for n in vecs], F32, 0)[None], "adamw_vectors")
    for dst, buf in ((out_d, d_), (out_m, m_), (out_v, v_)):
        dst.update(zip(vecs, _unpack(buf[0], vshapes, 0)))
    out_g.update({n: grep[n] for n in vecs})
    gsh.update({n: grep[n] for n in GATE_WEIGHTS})
    for n in sharded + list(GATE_WEIGHTS):
        shp = wd[n].shape
        three = (1, -1, shp[-1])
        d_, m_, v_ = _adamw(wd[n].reshape(three), gsh[n].reshape(three), md[n].reshape(three), vd[n].reshape(three), "adamw_" + n)
        out_g[n], out_d[n], out_m[n], out_v[n] = gsh[n], d_.reshape(shp), m_.reshape(shp), v_.reshape(shp)

    loss = lax.psum(loss, ("x", "y", "c"))
    return (loss, grad_x[None], *[out_g[n] for n in WEIGHTS], *[out_d[n] for n in WEIGHTS], *[out_m[n] for n in WEIGHTS],
            *[out_v[n] for n in WEIGHTS])
```

```python
import numpy as np
import jax
import jax.numpy as jnp
from jax import lax
from jax.experimental import pallas as pl
from jax.experimental.pallas import tpu as pltpu

F32 = jnp.float32
BF16 = jnp.bfloat16

N_DEV = 8
D = 1024
DEPTH = 2
EPS = 1e-6
ROPE_THETA = 10000.0
LRU_C = 8.0
LANE = 128
SUB = 8
IN_WIDTH = 11168
SHARD_IN = IN_WIDTH // N_DEV

C_LRUX, C_LRUG, C_CQ, C_CKV, C_KR, C_MLAG, C_DQ, C_DK, C_DV, C_DILG, C_MERGE = 0, 8, 16, 18, 19, 20, 24, 36, 48, 60, 64
ZW = 88 * LANE
KR_LANE = 64

MLA_QK = 96
MLA_SCALE = MLA_QK ** -0.5
DIL_HD = 64
DIL_SCALE = DIL_HD ** -0.5
DIL_DILATIONS = (1, 4, 16)
NK = 128

ADAM_LR, ADAM_B1, ADAM_B2, ADAM_EPS, ADAM_WD, ADAM_STEP = 0.001, 0.9, 0.999, 1e-08, 0.01, 10

NEG = -1e30
LOG2E = 1.4426950408889634
VMEM_LIMIT = 48 * 1024 * 1024


def _cp(**kw):
    return pltpu.CompilerParams(vmem_limit_bytes=VMEM_LIMIT, **kw)


def _sig(x):
    return 1.0 / (1.0 + jnp.exp(-x))


def _silu(x):
    return x * _sig(x)


def _dsilu(x):
    s = _sig(x)
    return s * (1.0 + x * (1.0 - s))


def _dot(a, b, dims):
    return lax.dot_general(a, b, (dims, ((), ())), preferred_element_type=F32)


def _nn(a, b):
    return _dot(a, b, ((1,), (0,)))


def _nt(a, b):
    return _dot(a, b, ((1,), (1,)))


def _tn(a, b):
    return _dot(a, b, ((0,), (0,)))


def _rsum(x):
    return jnp.sum(x, axis=-1, keepdims=True)


def _rsum_mxu(x):
    ones = jnp.ones((x.shape[-1], LANE), F32)
    return lax.dot_general(x, ones, (((1,), (0,)), ((), ())), precision=lax.Precision.HIGHEST, preferred_element_type=F32)


def _csum(x):
    return jnp.sum(x, axis=0, keepdims=True)


def _mm(a, b, *, mode, name, out_dtype=F32, add=None, after=None, tm=1024, tn=1024, tk=1024):
    if mode == "nn":
        (M, K), (K2, N) = a.shape, b.shape
    elif mode == "nt":
        (M, K), (N, K2) = a.shape, b.shape
    else:
        (K, M), (K2, N) = a.shape, b.shape
    assert K == K2
    tm, tn, tk = min(tm, M), min(tn, N), min(tk, K)
    assert M % tm == 0 and N % tn == 0 and K % tk == 0
    nk = K // tk
    fn = {"nn": _nn, "nt": _nt, "tn": _tn}[mode]
    has_add = add is not None

    def body(*refs):
        a_ref, b_ref = refs[0], refs[1]
        add_ref = refs[2] if has_add else None
        o_ref = refs[2 + has_add + (after is not None)]
        part = fn(a_ref[...].astype(BF16), b_ref[...].astype(BF16))

        def fin(acc):
            if has_add:
                acc = acc + add_ref[...]
            o_ref[...] = acc.astype(out_dtype)

        if nk == 1:
            fin(part)
        else:
            acc_ref = refs[-1]
            k = pl.program_id(2)

            @pl.when(k == 0)
            def _():
                acc_ref[...] = part

            @pl.when(k > 0)
            def _():
                acc_ref[...] += part

            @pl.when(k == nk - 1)
            def _():
                fin(acc_ref[...])

    a_spec = pl.BlockSpec((tk, tm), lambda i, j, k: (k, i)) if mode == "tn" else pl.BlockSpec((tm, tk), lambda i, j, k: (i, k))
    b_spec = pl.BlockSpec((tn, tk), lambda i, j, k: (j, k)) if mode == "nt" else pl.BlockSpec((tk, tn), lambda i, j, k: (k, j))
    o_spec = pl.BlockSpec((tm, tn), lambda i, j, k: (i, j))
    in_specs, args = [a_spec, b_spec], [a, b]
    if has_add:
        in_specs.append(o_spec)
        args.append(add)
    if after is not None:
        in_specs.append(pl.BlockSpec(memory_space=pl.ANY))
        args.append(after)
    return pl.pallas_call(
        body, name=name, grid=(M // tm, N // tn, nk), in_specs=in_specs, out_specs=o_spec,
        out_shape=jax.ShapeDtypeStruct((M, N), out_dtype),
        scratch_shapes=[pltpu.VMEM((tm, tn), F32)] if nk > 1 else [],
        compiler_params=_cp(dimension_semantics=("parallel", "parallel", "arbitrary")),
    )(*args)


T_ROW = 512


def _rms_in_fwd(x, g):
    S = x.shape[0]
    T = T_ROW

    def body(x_ref, g_ref, h_ref):
        xv = x_ref[...]
        r = lax.rsqrt(jnp.mean(xv * xv, axis=-1, keepdims=True) + EPS)
        h_ref[...] = (xv * r * g_ref[...]).astype(BF16)

    return pl.pallas_call(
        body, name="rms_in_fwd", grid=(S // T,),
        in_specs=[pl.BlockSpec((T, D), lambda i: (i, 0)), pl.BlockSpec((1, D), lambda i: (0, 0))],
        out_specs=pl.BlockSpec((T, D), lambda i: (i, 0)),
        out_shape=jax.ShapeDtypeStruct((S, D), BF16), compiler_params=_cp(),
    )(x, g)


def _rms_in_bwd(x, g, dh, dres):
    S = x.shape[0]
    T = T_ROW

    def body(x_ref, g_ref, dh_ref, dr_ref, dx_ref, dg_ref):
        i = pl.program_id(0)
        xv = x_ref[...]
        r = lax.rsqrt(jnp.mean(xv * xv, axis=-1, keepdims=True) + EPS)
        xn = xv * r
        dy = dh_ref[...]
        part = _csum(dy * xn)

        @pl.when(i == 0)
        def _():
            dg_ref[...] = part

        @pl.when(i > 0)
        def _():
            dg_ref[...] += part

        dxh = dy * g_ref[...]
        dx_ref[...] = dr_ref[...] + r * (dxh - xn * jnp.mean(dxh * xn, axis=-1, keepdims=True))

    row = pl.BlockSpec((T, D), lambda i: (i, 0))
    vec = pl.BlockSpec((1, D), lambda i: (0, 0))
    return pl.pallas_call(
        body, name="rms_in_bwd", grid=(S // T,), in_specs=[row, vec, row, row], out_specs=[row, vec],
        out_shape=[jax.ShapeDtypeStruct((S, D), F32), jax.ShapeDtypeStruct((1, D), F32)], compiler_params=_cp(),
    )(x, g, dh, dres)


T_LRU = 1024
T_LRU_BWD = 512


def _neg_expm1(y):
    ser = -y * (1.0 + y * 0.5 * (1.0 + y * (1.0 / 3.0) * (1.0 + y * 0.25 * (1.0 + y * 0.2))))
    return jnp.where(y > -0.03, ser, 1.0 - jnp.exp(y))


def _softplus_neg(lam):
    e = jnp.exp(-jnp.abs(lam))
    l1p = jnp.where(e < 0.01, e * (1.0 - e * (0.5 - e * (1.0 / 3.0 - e * 0.25))), jnp.log(1.0 + e))
    return jnp.maximum(-lam, 0.0) + l1p


def _scan_fwd(a, b, T):
    row = lax.broadcasted_iota(jnp.int32, a.shape, 0)
    d = 1
    while d < T:
        m = row >= d
        b = jnp.where(m, a * pltpu.roll(b, d, 0) + b, b)
        a = jnp.where(m, a * pltpu.roll(a, d, 0), a)
        d *= 2
    return a, b


def _scan_bwd(a, b, T):
    row = lax.broadcasted_iota(jnp.int32, a.shape, 0)
    d = 1
    while d < T:
        m = row < T - d
        b = jnp.where(m, a * pltpu.roll(b, T - d, 0) + b, b)
        a = jnp.where(m, a * pltpu.roll(a, T - d, 0), a)
        d *= 2
    return b


def _lru_common(x, prev, first, cw_ref, cb_ref, wgx_ref, bgx_ref, wga_ref, bga_ref, lam_ref, T):
    row = lax.broadcasted_iota(jnp.int32, x.shape, 0)
    prev = jnp.where(first, 0.0, prev)
    xs = []
    for j in (3, 2, 1):
        pv = jnp.tile(pltpu.roll(prev, j, 0), (T // SUB, 1))
        xs.append(jnp.where(row < j, pv, pltpu.roll(x, j, 0)))
    xs.append(x)
    xc = cb_ref[...] + cw_ref[0:1, :] * xs[0] + cw_ref[1:2, :] * xs[1] + cw_ref[2:3, :] * xs[2] + cw_ref[3:4, :] * xs[3]
    xcb = xc.astype(BF16)
    gx = _sig(_nn(xcb, wgx_ref[0]) + bgx_ref[0])
    ga = _sig(_nn(xcb, wga_ref[0]) + bga_ref[0])
    sp = _softplus_neg(lam_ref[...])
    log_a = -LRU_C * ga * sp
    a = jnp.exp(log_a)
    mult = jnp.sqrt(_neg_expm1(2.0 * log_a))
    return xs, xc, xcb, gx, ga, sp, a, mult


def _lru_specs(T, tmap):
    def at(col0):
        return pl.BlockSpec((T, LANE), lambda n, i: (tmap(i), col0 + n))

    def prev(col0):
        return pl.BlockSpec((SUB, LANE), lambda n, i: (jnp.maximum(tmap(i) * (T // SUB) - 1, 0), col0 + n))

    small = [
        pl.BlockSpec((4, LANE), lambda n, i: (0, n)),
        pl.BlockSpec((1, LANE), lambda n, i: (0, n)),
        pl.BlockSpec((1, LANE, LANE), lambda n, i: (n, 0, 0)),
        pl.BlockSpec((1, 1, LANE), lambda n, i: (n, 0, 0)),
        pl.BlockSpec((1, LANE, LANE), lambda n, i: (n, 0, 0)),
        pl.BlockSpec((1, 1, LANE), lambda n, i: (n, 0, 0)),
        pl.BlockSpec((1, LANE), lambda n, i: (0, n)),
    ]
    return at, prev, small


def _lru_fwd(zp, w):
    S = zp.shape[0]
    T = T_LRU
    at, prev, small = _lru_specs(T, lambda i: i)

    def body(x_ref, xp_ref, g_ref, cw_ref, cb_ref, wgx_ref, bgx_ref, wga_ref, bga_ref, lam_ref, hs_ref, y_ref, carry_ref):
        i = pl.program_id(1)

        @pl.when(i == 0)
        def _():
            carry_ref[...] = jnp.zeros_like(carry_ref)

        x = x_ref[...]
        _, xc, _, gx, _, _, a, mult = _lru_common(x, xp_ref[...], i == 0, cw_ref, cb_ref, wgx_ref, bgx_ref, wga_ref, bga_ref, lam_ref, T)
        A, B = _scan_fwd(a, mult * gx * xc, T)
        h = B + A * carry_ref[SUB - 1:SUB, :]
        hs_ref[...] = h
        carry_ref[...] = hs_ref[T - SUB:T, :]
        y_ref[...] = (h * _silu(g_ref[...])).astype(BF16)

    out = pl.BlockSpec((T, LANE), lambda n, i: (i, n))
    return pl.pallas_call(
        body, name="lru_fwd", grid=(8, S // T),
        in_specs=[at(C_LRUX), prev(C_LRUX), at(C_LRUG)] + small, out_specs=[out, out],
        out_shape=[jax.ShapeDtypeStruct((S, D), F32), jax.ShapeDtypeStruct((S, D), BF16)],
        scratch_shapes=[pltpu.VMEM((SUB, LANE), F32)],
        compiler_params=_cp(dimension_semantics=("parallel", "arbitrary")),
    )(zp, zp, zp, w["conv_w"], w["conv_b"], w["w_gx"], w["b_gx"], w["w_ga"], w["b_ga"], w["lam"])


def _lru_bwd(zp, hs, dy, w, dz):
    S = zp.shape[0]
    T = T_LRU_BWD
    nT = S // T
    at, prev, small = _lru_specs(T, lambda i: nT - 1 - i)

    def body(x_ref, xp_ref, g_ref, h_ref, hp_ref, dy_ref, cw_ref, cb_ref, wgx_ref, bgx_ref, wga_ref, bga_ref, lam_ref, dz_in,
             dzx_ref, dcw_ref, dcb_ref, dwgx_ref, dbgx_ref, dwga_ref, dbga_ref, dlam_ref, carry_ref, head_ref):
        del dz_in
        j = pl.program_id(1)
        it = nT - 1 - j

        @pl.when(j == 0)
        def _():
            for r in (carry_ref, head_ref, dcw_ref, dcb_ref, dwgx_ref, dbgx_ref, dwga_ref, dbga_ref, dlam_ref):
                r[...] = jnp.zeros_like(r)

        first = it == 0
        x = x_ref[...]
        xs, xc, xcb, gx, ga, sp, a, mult = _lru_common(x, xp_ref[...], first, cw_ref, cb_ref, wgx_ref, bgx_ref, wga_ref, bga_ref, lam_ref, T)
        row = lax.broadcasted_iota(jnp.int32, x.shape, 0)
        u = gx * xc
        h = h_ref[...]
        hp = jnp.where(first, 0.0, hp_ref[...])
        hm1 = jnp.where(row < 1, jnp.tile(pltpu.roll(hp, 1, 0), (T // SUB, 1)), pltpu.roll(h, 1, 0))
        dho = dy_ref[...] * _silu(g_ref[...])
        gin = jnp.where(row == T - 1, dho + carry_ref[0:1, :], dho)
        abar = jnp.where(row == T - 1, 0.0, pltpu.roll(a, T - 1, 0))
        dh = _scan_bwd(abar, gin, T)
        carry_ref[...] = (a * dh)[0:SUB, :]
        da = dh * hm1
        dmult = dh * u
        du = dh * mult
        dgx = du * xc
        dxc = du * gx
        dlog_a = da * a - dmult * a * a / mult
        dga = dlog_a * (-LRU_C * sp)
        lam = lam_ref[...]
        dlam_ref[...] += _csum(dlog_a * (-LRU_C * ga)) * (-1.0 / (1.0 + jnp.exp(lam)))
        dpa = dga * ga * (1.0 - ga)
        dpx = dgx * gx * (1.0 - gx)
        dpab, dpxb = dpa.astype(BF16), dpx.astype(BF16)
        dxc = dxc + _nt(dpxb, wgx_ref[0]) + _nt(dpab, wga_ref[0])
        dwgx_ref[0] += _tn(xcb, dpxb)
        dwga_ref[0] += _tn(xcb, dpab)
        dbgx_ref[0] += _csum(dpx)
        dbga_ref[0] += _csum(dpa)
        dcb_ref[...] += _csum(dxc)
        for k in range(4):
            dcw_ref[k:k + 1, :] += _csum(dxc * xs[k])
        head = head_ref[...]
        dx = cw_ref[3:4, :] * dxc
        for jj in (1, 2, 3):
            hv = jnp.tile(pltpu.roll(head, SUB - jj, 0), (T // SUB, 1))
            dx = dx + cw_ref[3 - jj:4 - jj, :] * jnp.where(row >= T - jj, hv, pltpu.roll(dxc, T - jj, 0))
        head_ref[...] = dxc[0:SUB, :]
        dzx_ref[...] = dx.astype(BF16)

    def acc(shape, imap):
        return pl.BlockSpec(shape, imap)

    out_specs = [
        pl.BlockSpec((T, LANE), lambda n, i: (nT - 1 - i, C_LRUX + n)),
        acc((4, LANE), lambda n, i: (0, n)), acc((1, LANE), lambda n, i: (0, n)),
        acc((1, LANE, LANE), lambda n, i: (n, 0, 0)), acc((1, 1, LANE), lambda n, i: (n, 0, 0)),
        acc((1, LANE, LANE), lambda n, i: (n, 0, 0)), acc((1, 1, LANE), lambda n, i: (n, 0, 0)),
        acc((1, LANE), lambda n, i: (0, n)),
    ]
    out_shape = [
        jax.ShapeDtypeStruct(dz.shape, BF16),
        jax.ShapeDtypeStruct((4, D), F32), jax.ShapeDtypeStruct((1, D), F32),
        jax.ShapeDtypeStruct((8, LANE, LANE), F32), jax.ShapeDtypeStruct((8, 1, LANE), F32),
        jax.ShapeDtypeStruct((8, LANE, LANE), F32), jax.ShapeDtypeStruct((8, 1, LANE), F32),
        jax.ShapeDtypeStruct((1, D), F32),
    ]
    dyspec = pl.BlockSpec((T, LANE), lambda n, i: (nT - 1 - i, n))
    hprev = pl.BlockSpec((SUB, LANE), lambda n, i: (jnp.maximum((nT - 1 - i) * (T // SUB) - 1, 0), n))
    return pl.pallas_call(
        body, name="lru_bwd", grid=(8, nT),
        in_specs=[at(C_LRUX), prev(C_LRUX), at(C_LRUG), dyspec, hprev, dyspec] + small + [pl.BlockSpec(memory_space=pl.ANY)],
        out_specs=out_specs, out_shape=out_shape,
        scratch_shapes=[pltpu.VMEM((SUB, LANE), F32), pltpu.VMEM((SUB, LANE), F32)],
        input_output_aliases={13: 0},
        compiler_params=_cp(dimension_semantics=("parallel", "arbitrary")),
    )(zp, zp, zp, hs, hs, dy, w["conv_w"], w["conv_b"], w["w_gx"], w["b_gx"], w["w_ga"], w["b_ga"], w["lam"], dz)


def _lru_gate_bwd(zp, hs, dy, dz):
    S = zp.shape[0]
    T = T_ROW

    def body(g_ref, h_ref, dy_ref, dz_in, o_ref):
        del dz_in
        o_ref[...] = (dy_ref[...] * h_ref[...] * _dsilu(g_ref[...])).astype(BF16)

    row = pl.BlockSpec((T, D), lambda i: (i, 0))
    zc = pl.BlockSpec((T, D), lambda i: (i, C_LRUG // 8))
    return pl.pallas_call(
        body, name="lru_gate_bwd", grid=(S // T,), in_specs=[zc, row, row, pl.BlockSpec(memory_space=pl.ANY)], out_specs=zc,
        out_shape=jax.ShapeDtypeStruct(dz.shape, BF16), input_output_aliases={3: 0}, compiler_params=_cp(),
    )(zp, hs, dy, dz)


def _rope_tables(pos):
    pf = pos.astype(F32)[:, None]

    def cs(d):
        inv = ROPE_THETA ** (-jnp.arange(0, d, 2, dtype=F32) / d)
        ang = pf * inv
        return jnp.cos(ang), jnp.sin(ang)

    S = pos.shape[0]
    c, s = cs(32)
    one, zero = jnp.ones((S, 64), F32), jnp.zeros((S, 16), F32)
    z32, z64 = jnp.zeros((S, 32), F32), jnp.zeros((S, 64), F32)
    mla = (jnp.concatenate([one, c, c, jnp.ones((S, 32), F32)], 1),
           jnp.concatenate([z64, zero, s, z32], 1),
           jnp.concatenate([z64, -s, zero, z32], 1))
    c, s = cs(64)
    dil = (jnp.concatenate([c, c, c, c], 1),
           jnp.concatenate([z32, s, z32, s], 1),
           jnp.concatenate([-s, z32, -s, z32], 1))
    return mla, dil


def _rope(x, C, S1, S2, sh):
    return x * C + pltpu.roll(x, sh, 1) * S1 + pltpu.roll(x, LANE - sh, 1) * S2


def _rope_t(dy, C, S1, S2, sh):
    return dy * C + pltpu.roll(dy * S1, LANE - sh, 1) + pltpu.roll(dy * S2, sh, 1)


def _lane(shape):
    return lax.broadcasted_iota(jnp.int32, shape, 1)


T_MLA = 256
TA = 512


def _zcol(T, width, col_lanes):
    assert (col_lanes * LANE) % width == 0
    return pl.BlockSpec((T, width), lambda i: (i, col_lanes * LANE // width))


def _full(shape):
    return pl.BlockSpec(shape, lambda *_: (0,) * len(shape))


def _mla_pre_fwd(zp, w, tab):
    S = zp.shape[0]
    T = T_MLA

    def body(cq_ref, ckv_ref, kr_ref, gcq_ref, gckv_ref, wuq_ref, wuk_ref, wuv_ref, gq_ref, gk_ref, C_ref, S1_ref, S2_ref,
             q_ref, k_ref, v_ref):
        cq = cq_ref[...]
        cqn = (cq * lax.rsqrt(jnp.mean(cq * cq, axis=-1, keepdims=True) + EPS) * gcq_ref[...]).astype(BF16)
        ckv = ckv_ref[...]
        ckvn = (ckv * lax.rsqrt(jnp.mean(ckv * ckv, axis=-1, keepdims=True) + EPS) * gckv_ref[...]).astype(BF16)
        q0 = _nn(cqn, wuq_ref[...])
        k0 = _nn(ckvn, wuk_ref[...])
        krb = kr_ref[...]
        C, S1, S2 = C_ref[...], S1_ref[...], S2_ref[...]
        for h in range(8):
            sl = slice(h * LANE, (h + 1) * LANE)
            xq = q0[:, sl]
            xq = xq * lax.rsqrt(_rsum_mxu(xq * xq) * (1.0 / MLA_QK) + EPS) * gq_ref[...]
            q_ref[:, sl] = _rope(xq, C, S1, S2, 16).astype(BF16)
            xk = k0[:, sl] + krb
            xk = xk * lax.rsqrt(_rsum_mxu(xk * xk) * (1.0 / MLA_QK) + EPS) * gk_ref[...]
            k_ref[:, sl] = _rope(xk, C, S1, S2, 16).astype(BF16)
        v_ref[...] = _nn(ckvn, wuv_ref[...]).astype(BF16)

    tabspec = pl.BlockSpec((T, LANE), lambda i: (i, 0))
    in_specs = [_zcol(T, 256, C_CQ), _zcol(T, LANE, C_CKV), _zcol(T, LANE, C_KR), _full((1, 256)), _full((1, LANE)),
                _full((256, 1024)), _full((LANE, 1024)), _full((LANE, 512)), _full((1, LANE)), _full((1, LANE)),
                tabspec, tabspec, tabspec]
    return pl.pallas_call(
        body, name="mla_pre_fwd", grid=(S // T,), in_specs=in_specs,
        out_specs=[pl.BlockSpec((T, 1024), lambda i: (i, 0)), pl.BlockSpec((T, 1024), lambda i: (i, 0)), pl.BlockSpec((T, 512), lambda i: (i, 0))],
        out_shape=[jax.ShapeDtypeStruct((S, 1024), BF16), jax.ShapeDtypeStruct((S, 1024), BF16), jax.ShapeDtypeStruct((S, 512), BF16)],
        compiler_params=_cp(),
    )(zp, zp, zp, w["g_cq"], w["g_ckv"], w["w_uq"], w["w_uk"], w["w_uv"], w["g_mq"], w["g_mk"], *tab)


def _mla_attn_fwd(q, k, v, zp):
    S = q.shape[0]
    nq = S // TA

    def body(q_ref, k_ref, v_ref, g_ref, o_ref, lse_ref, y_ref):
        qi = pl.program_id(1)
        lane = _lane((TA, LANE))
        rowi = lax.broadcasted_iota(jnp.int32, (TA, TA), 0)
        coli = lax.broadcasted_iota(jnp.int32, (TA, TA), 1)
        o_tot = jnp.zeros((TA, LANE), F32)
        for hh in range(2):
            cs = slice(hh * LANE, (hh + 1) * LANE)
            hm = (lane < 64) if hh == 0 else (lane >= 64)
            qh = q_ref[:, cs]
            ones_lane = 64 if hh == 0 else 0

            def step(kb, carry, masked, cs=cs, hm=hm, qh=qh, ones_lane=ones_lane):
                m, acc = carry
                off = pl.multiple_of(kb * TA, TA)
                kh = k_ref[pl.ds(off, TA), cs]
                vv = v_ref[pl.ds(off, TA), :]
                vh = jnp.where(hm, vv, jnp.where(lane == ones_lane, jnp.ones_like(vv), jnp.zeros_like(vv)))
                s = _nt(qh, kh) * (MLA_SCALE * LOG2E)
                if masked:
                    s = jnp.where(rowi >= coli, s, NEG)
                m_new = jnp.maximum(m, jnp.max(s, axis=-1, keepdims=True))
                acc = jnp.exp2(m - m_new) * acc + _nn(jnp.exp2(s - m_new).astype(BF16), vh)
                return m_new, acc

            init = (jnp.full((TA, 1), NEG, F32), jnp.zeros((TA, LANE), F32))
            carry = lax.fori_loop(0, qi, lambda kb, c: step(kb, c, False), init)
            m, acc = step(qi, carry, True)
            l = _rsum(jnp.where(lane == ones_lane, acc, 0.0))
            o_tot = o_tot + jnp.where(hm, acc, 0.0) / l
            lse_ref[:, cs] = jnp.broadcast_to(m * (1.0 / LOG2E) + jnp.log(l), (TA, LANE))
        o_ref[...] = o_tot
        y_ref[...] = (o_tot * _silu(g_ref[...])).astype(BF16)

    blk = pl.BlockSpec((TA, LANE), lambda p, i: (i, p))
    return pl.pallas_call(
        body, name="mla_attn_fwd", grid=(4, nq),
        in_specs=[pl.BlockSpec((TA, 256), lambda p, i: (i, p)), pl.BlockSpec((S, 256), lambda p, i: (0, p)),
                  pl.BlockSpec((S, LANE), lambda p, i: (0, p)), pl.BlockSpec((TA, LANE), lambda p, i: (i, C_MLAG + p))],
        out_specs=[blk, pl.BlockSpec((TA, 256), lambda p, i: (i, p)), blk],
        out_shape=[jax.ShapeDtypeStruct((S, 512), F32), jax.ShapeDtypeStruct((S, 1024), F32), jax.ShapeDtypeStruct((S, 512), BF16)],
        compiler_params=_cp(dimension_semantics=("parallel", "arbitrary")),
    )(q, k, v, zp)


def _mla_post_bwd(zp, o, dy, dz):
    S = zp.shape[0]
    T = T_ROW

    def body(g_ref, o_ref, dy_ref, dz_in, dz_ref, do_ref, D_ref):
        del dz_in
        g, o_, dy_ = g_ref[...], o_ref[...], dy_ref[...]
        do = dy_ * _silu(g)
        do_ref[...] = do.astype(BF16)
        dz_ref[...] = (dy_ * o_ * _dsilu(g)).astype(BF16)
        prod = do * o_
        lane = _lane((T, LANE))
        for p in range(4):
            pr = prod[:, p * LANE:(p + 1) * LANE]
            da = _rsum(jnp.where(lane < 64, pr, 0.0))
            db = _rsum(jnp.where(lane >= 64, pr, 0.0))
            D_ref[:, 2 * p * LANE:(2 * p + 1) * LANE] = jnp.broadcast_to(da, (T, LANE))
            D_ref[:, (2 * p + 1) * LANE:(2 * p + 2) * LANE] = jnp.broadcast_to(db, (T, LANE))

    row = pl.BlockSpec((T, 512), lambda i: (i, 0))
    zc = _zcol(T, 512, C_MLAG)
    return pl.pallas_call(
        body, name="mla_post_bwd", grid=(S // T,), in_specs=[zc, row, row, pl.BlockSpec(memory_space=pl.ANY)],
        out_specs=[zc, row, pl.BlockSpec((T, 1024), lambda i: (i, 0))],
        out_shape=[jax.ShapeDtypeStruct(dz.shape, BF16), jax.ShapeDtypeStruct((S, 512), BF16), jax.ShapeDtypeStruct((S, 1024), F32)],
        input_output_aliases={3: 0}, compiler_params=_cp(),
    )(zp, o, dy, dz)


def _mla_attn_bwd(q, k, v, do, lse, Dr):
    S = q.shape[0]
    nq = S // TA

    def body(q_ref, do_ref, lse_ref, D_ref, k_ref, v_ref, dq_ref, dk_ref, dv_ref):
        ki = pl.program_id(1)

        @pl.when(ki == 0)
        def _():
            dq_ref[...] = jnp.zeros_like(dq_ref)

        lane = _lane((TA, LANE))
        rowi = lax.broadcasted_iota(jnp.int32, (TA, TA), 0)
        coli = lax.broadcasted_iota(jnp.int32, (TA, TA), 1)
        dv_tot = jnp.zeros((TA, LANE), F32)
        for hh in range(2):
            cs = slice(hh * LANE, (hh + 1) * LANE)
            hm = (lane < 64) if hh == 0 else (lane >= 64)
            kh = k_ref[:, cs]
            vv = v_ref[...]
            vm = jnp.where(hm, vv, jnp.zeros_like(vv))

            def step(qb, carry, masked, cs=cs, kh=kh, vm=vm):
                dk_acc, dv_acc = carry
                off = pl.multiple_of(qb * TA, TA)
                qh = q_ref[pl.ds(off, TA), cs]
                doh = do_ref[pl.ds(off, TA), :]
                ls = jnp.tile(lse_ref[pl.ds(off, TA), cs], (1, TA // LANE))
                dd = jnp.tile(D_ref[pl.ds(off, TA), cs], (1, TA // LANE))
                s = _nt(qh, kh) * MLA_SCALE
                if masked:
                    s = jnp.where(rowi >= coli, s, NEG)
                p = jnp.exp(s - ls)
                dp = _nt(doh, vm)
                ds = (p * (dp - dd) * MLA_SCALE).astype(BF16)
                dv_acc = dv_acc + _tn(p.astype(BF16), doh)
                dk_acc = dk_acc + _tn(ds, qh)
                dq_ref[pl.ds(off, TA), cs] += _nn(ds, kh)
                return dk_acc, dv_acc

            z = jnp.zeros((TA, LANE), F32)
            carry = step(ki, (z, z), True)
            dk_acc, dv_acc = lax.fori_loop(ki + 1, nq, lambda qb, c: step(qb, c, False), carry)
            dk_ref[:, cs] = dk_acc
            dv_tot = dv_tot + jnp.where(hm, dv_acc, 0.0)
        dv_ref[...] = dv_tot

    pair = pl.BlockSpec((S, 256), lambda p, i: (0, p))
    return pl.pallas_call(
        body, name="mla_attn_bwd", grid=(4, nq),
        in_specs=[pair, pl.BlockSpec((S, LANE), lambda p, i: (0, p)), pair, pair,
                  pl.BlockSpec((TA, 256), lambda p, i: (i, p)), pl.BlockSpec((TA, LANE), lambda p, i: (i, p))],
        out_specs=[pair, pl.BlockSpec((TA, 256), lambda p, i: (i, p)), pl.BlockSpec((TA, LANE), lambda p, i: (i, p))],
        out_shape=[jax.ShapeDtypeStruct((S, 1024), F32), jax.ShapeDtypeStruct((S, 1024), F32), jax.ShapeDtypeStruct((S, 512), F32)],
        compiler_params=_cp(dimension_semantics=("parallel", "arbitrary")),
    )(q, do, lse, Dr, k, v)


def _mla_pre_bwd(zp, dq, dk, dv, w, tab, dz):
    S = zp.shape[0]
    T = T_MLA

    def body(cq_ref, ckv_ref, kr_ref, dq_ref, dk_ref, dv_ref, gcq_ref, gckv_ref, wuq_ref, wuk_ref, wuv_ref, gq_ref, gk_ref,
             C_ref, S1_ref, S2_ref, dz_in, dz_ref, dwuq_ref, dwuk_ref, dwuv_ref, dgcq_ref, dgckv_ref, dgq_ref, dgk_ref):
        del dz_in
        i = pl.program_id(0)

        @pl.when(i == 0)
        def _():
            for r in (dwuq_ref, dwuk_ref, dwuv_ref, dgcq_ref, dgckv_ref, dgq_ref, dgk_ref):
                r[...] = jnp.zeros_like(r)

        cq = cq_ref[...]
        rq = lax.rsqrt(jnp.mean(cq * cq, axis=-1, keepdims=True) + EPS)
        cqh = cq * rq
        cqn = (cqh * gcq_ref[...]).astype(BF16)
        ckv = ckv_ref[...]
        rkv = lax.rsqrt(jnp.mean(ckv * ckv, axis=-1, keepdims=True) + EPS)
        ckvh = ckv * rkv
        ckvn = (ckvh * gckv_ref[...]).astype(BF16)
        q0 = _nn(cqn, wuq_ref[...])
        k0 = _nn(ckvn, wuk_ref[...])
        krb = kr_ref[...]
        C, S1, S2 = C_ref[...], S1_ref[...], S2_ref[...]
        gq, gk = gq_ref[...], gk_ref[...]

        def head_bwd(x, dy, g):
            r = lax.rsqrt(_rsum_mxu(x * x) * (1.0 / MLA_QK) + EPS)
            xn = x * r
            dyn = _rope_t(dy, C, S1, S2, 16)
            dxh = dyn * g
            return r * (dxh - xn * _rsum_mxu(dxh * xn) * (1.0 / MLA_QK)), _csum(dyn * xn)

        dq0, dk0 = [], []
        dgq_acc = jnp.zeros((1, LANE), F32)
        dgk_acc = jnp.zeros((1, LANE), F32)
        dkr = jnp.zeros((T, LANE), F32)
        for h in range(8):
            sl = slice(h * LANE, (h + 1) * LANE)
            dxq, gq_p = head_bwd(q0[:, sl], dq_ref[:, sl], gq)
            dxk, gk_p = head_bwd(k0[:, sl] + krb, dk_ref[:, sl], gk)
            dq0.append(dxq.astype(BF16))
            dk0.append(dxk.astype(BF16))
            dkr = dkr + dxk
            dgq_acc = dgq_acc + gq_p
            dgk_acc = dgk_acc + gk_p
        dgq_ref[...] += dgq_acc
        dgk_ref[...] += dgk_acc
        dq0 = jnp.concatenate(dq0, axis=1)
        dk0 = jnp.concatenate(dk0, axis=1)
        dvb = dv_ref[...].astype(BF16)
        dwuq_ref[...] += _tn(cqn, dq0)
        dwuk_ref[...] += _tn(ckvn, dk0)
        dwuv_ref[...] += _tn(ckvn, dvb)
        dcqn = _nt(dq0, wuq_ref[...])
        dckvn = _nt(dk0, wuk_ref[...]) + _nt(dvb, wuv_ref[...])
        dgcq_ref[...] += _csum(dcqn * cqh)
        dgckv_ref[...] += _csum(dckvn * ckvh)
        dxh = dcqn * gcq_ref[...]
        dz_ref[:, 0:256] = (rq * (dxh - cqh * jnp.mean(dxh * cqh, axis=-1, keepdims=True))).astype(BF16)
        dxh = dckvn * gckv_ref[...]
        dz_ref[:, 256:384] = (rkv * (dxh - ckvh * jnp.mean(dxh * ckvh, axis=-1, keepdims=True))).astype(BF16)
        lane = _lane((T, LANE))
        dz_ref[:, 384:512] = jnp.where((lane >= KR_LANE) & (lane < KR_LANE + 32), dkr, 0.0).astype(BF16)

    tabspec = pl.BlockSpec((T, LANE), lambda i: (i, 0))
    in_specs = [_zcol(T, 256, C_CQ), _zcol(T, LANE, C_CKV), _zcol(T, LANE, C_KR),
                pl.BlockSpec((T, 1024), lambda i: (i, 0)), pl.BlockSpec((T, 1024), lambda i: (i, 0)), pl.BlockSpec((T, 512), lambda i: (i, 0)),
                _full((1, 256)), _full((1, LANE)), _full((256, 1024)), _full((LANE, 1024)), _full((LANE, 512)), _full((1, LANE)), _full((1, LANE)),
                tabspec, tabspec, tabspec, pl.BlockSpec(memory_space=pl.ANY)]
    out_specs = [_zcol(T, 512, C_CQ), _full((256, 1024)), _full((LANE, 1024)), _full((LANE, 512)), _full((1, 256)), _full((1, LANE)),
                 _full((1, LANE)), _full((1, LANE))]
    out_shape = [jax.ShapeDtypeStruct(dz.shape, BF16), jax.ShapeDtypeStruct((256, 1024), F32), jax.ShapeDtypeStruct((LANE, 1024), F32),
                 jax.ShapeDtypeStruct((LANE, 512), F32), jax.ShapeDtypeStruct((1, 256), F32), jax.ShapeDtypeStruct((1, LANE), F32),
                 jax.ShapeDtypeStruct((1, LANE), F32), jax.ShapeDtypeStruct((1, LANE), F32)]
    return pl.pallas_call(
        body, name="mla_pre_bwd", grid=(S // T,), in_specs=in_specs, out_specs=out_specs, out_shape=out_shape,
        input_output_aliases={16: 0}, compiler_params=_cp(),
    )(zp, zp, zp, dq, dk, dv, w["g_cq"], w["g_ckv"], w["w_uq"], w["w_uk"], w["w_uv"], w["g_mq"], w["g_mk"], *tab, dz)


T_DIL = 256


def _head_stats(x, lane):
    sq = x * x
    sa = _rsum(jnp.where(lane < 64, sq, 0.0))
    sb = _rsum(jnp.where(lane >= 64, sq, 0.0))
    return lax.rsqrt(jnp.where(lane < 64, sa, sb) * (1.0 / DIL_HD) + EPS)


def _head_sum(x, lane):
    sa = _rsum(jnp.where(lane < 64, x, 0.0))
    sb = _rsum(jnp.where(lane >= 64, x, 0.0))
    return jnp.where(lane < 64, sa, sb)


def _head_stats_mxu(x):
    r = lax.broadcasted_iota(jnp.int32, (LANE, LANE), 0)
    c = lax.broadcasted_iota(jnp.int32, (LANE, LANE), 1)
    ones = jnp.where((r < 64) == (c < 64), 1.0, 0.0).astype(F32)
    ss = lax.dot_general(x * x, ones, (((1,), (0,)), ((), ())), precision=lax.Precision.HIGHEST, preferred_element_type=F32)
    return lax.rsqrt(ss * (1.0 / DIL_HD) + EPS)


def _dil_pre_fwd(zp, w, tab):
    S = zp.shape[0]
    T = T_DIL

    def body(q_ref, k_ref, gq_ref, gk_ref, C_ref, S1_ref, S2_ref, qo_ref, ko_ref):
        C, S1, S2 = C_ref[...], S1_ref[...], S2_ref[...]
        for b in range(12):
            sl = slice(b * LANE, (b + 1) * LANE)
            x = q_ref[:, sl]
            qo_ref[:, sl] = _rope(x * _head_stats_mxu(x) * gq_ref[...], C, S1, S2, 32)
            x = k_ref[:, sl]
            ko_ref[:, sl] = _rope(x * _head_stats_mxu(x) * gk_ref[...], C, S1, S2, 32)

    tabspec = pl.BlockSpec((T, LANE), lambda i: (i, 0))
    out = pl.BlockSpec((T, 1536), lambda i: (i, 0))
    return pl.pallas_call(
        body, name="dil_pre_fwd", grid=(S // T,),
        in_specs=[_zcol(T, 1536, C_DQ), _zcol(T, 1536, C_DK), _full((1, LANE)), _full((1, LANE)), tabspec, tabspec, tabspec],
        out_specs=[out, out], out_shape=[jax.ShapeDtypeStruct((S, 1536), F32)] * 2, compiler_params=_cp(),
    )(zp, zp, w["g_dq"], w["g_dk"], *tab)


DIL_ROWS = 2048


def _dil_geometry(g, S):
    d = DIL_DILATIONS[g]
    P = NK * d
    return d, P, DIL_ROWS // P, S // P


def _dil_rows(start, d, blocks=1):
    return pl.ds(pl.multiple_of(start, NK), blocks * NK) if d == 1 else pl.ds(start, blocks * NK, stride=d)


def _dil_specs(g, S, col0):
    _, P, m, nb = _dil_geometry(g, S)
    cur = pl.BlockSpec((DIL_ROWS, LANE), lambda sb, c: (sb, col0 + c))
    prv = pl.BlockSpec((P, LANE), lambda sb, c: (jnp.maximum(sb * m - 1, 0), col0 + c))
    nxt = pl.BlockSpec((P, LANE), lambda sb, c: (jnp.minimum((sb + 1) * m, nb - 1), col0 + c))
    return cur, prv, nxt


def _dil_attn_fwd(q, k, zp, g):
    S = q.shape[0]
    d, P, m, nb = _dil_geometry(g, S)
    R = DIL_ROWS

    def body(q_ref, kc_ref, kp_ref, vc_ref, vp_ref, o_ref, lse_ref, *scr):
        sb = pl.program_id(0)
        if m > 1:
            ks_ref, vs_ref = scr
            ks_ref[0:P, :] = kp_ref[...]
            ks_ref[P:P + R, :] = kc_ref[...]
            vs_ref[0:P, :] = vp_ref[...]
            vs_ref[P:P + R, :] = vc_ref[...]
        lane = _lane((NK, LANE))

        def unit(u, carry):
            j = u // d
            start = j * P + (u - j * d)
            rows = _dil_rows(start, d)
            if m > 1:
                k2, v2 = ks_ref[_dil_rows(start, d, 2), :], vs_ref[_dil_rows(start, d, 2), :]
            else:
                k2 = jnp.concatenate([kp_ref[rows, :], kc_ref[rows, :]], axis=0)
                v2 = jnp.concatenate([vp_ref[rows, :], vc_ref[rows, :]], axis=0)
            k2, v2 = k2.astype(BF16), v2.astype(BF16)
            q_ = q_ref[rows, :].astype(BF16)
            row = lax.broadcasted_iota(jnp.int32, (NK, 2 * NK), 0)
            col = lax.broadcasted_iota(jnp.int32, (NK, 2 * NK), 1)
            band = (col >= row) & (col <= row + NK) & ((col >= NK) | (sb * m + j > 0))
            lane2 = _lane((2 * NK, LANE))
            zb, zv = jnp.zeros_like(q_), jnp.zeros_like(v2)
            o_tot = jnp.zeros((NK, LANE), F32)
            lse_tot = jnp.zeros((NK, LANE), F32)
            for hh in range(2):
                hm = (lane < 64) if hh == 0 else (lane >= 64)
                hm2 = (lane2 < 64) if hh == 0 else (lane2 >= 64)
                s_ = jnp.where(band, _nt(jnp.where(hm, q_, zb), k2) * DIL_SCALE, NEG)
                mx = jnp.max(s_, axis=-1, keepdims=True)
                e = jnp.exp(s_ - mx)
                den = _rsum(e)
                o_tot = o_tot + _nn(e.astype(BF16), jnp.where(hm2, v2, zv)) / den
                lse_tot = jnp.where(hm, mx + jnp.log(den), lse_tot)
            o_ref[rows, :] = o_tot
            lse_ref[rows, :] = lse_tot
            return carry

        lax.fori_loop(0, R // NK, unit, 0, unroll=8)

    qcur, qprv, _ = _dil_specs(g, S, 4 * g)
    vcur, vprv, _ = _dil_specs(g, S, C_DV + 4 * g)
    out = pl.BlockSpec((R, LANE), lambda sb, c: (sb, c))
    return pl.pallas_call(
        body, name=f"dil_attn_fwd{g}", grid=(S // R, 4), in_specs=[qcur, qcur, qprv, vcur, vprv], out_specs=[out, out],
        out_shape=[jax.ShapeDtypeStruct((S, 512), F32)] * 2,
        scratch_shapes=[pltpu.VMEM((P + R, LANE), F32)] * 2 if m > 1 else [], compiler_params=_cp(),
    )(q, k, k, zp, zp)


def _dil_combine(os_, ls_, zp):
    S = zp.shape[0]
    T = T_ROW

    def body(o0, o1, o2, l0, l1, l2, g_ref, oc_ref, L_ref, y_ref):
        a, b, c = l0[...], l1[...], l2[...]
        mx = jnp.maximum(jnp.maximum(a, b), c)
        ea, eb, ec = jnp.exp(a - mx), jnp.exp(b - mx), jnp.exp(c - mx)
        den = ea + eb + ec
        oc = (ea * o0[...] + eb * o1[...] + ec * o2[...]) / den
        oc_ref[...] = oc
        L_ref[...] = mx + jnp.log(den)
        y_ref[...] = (oc * _silu(g_ref[...])).astype(BF16)

    row = pl.BlockSpec((T, 512), lambda i: (i, 0))
    return pl.pallas_call(
        body, name="dil_combine", grid=(S // T,), in_specs=[row] * 6 + [_zcol(T, 512, C_DILG)], out_specs=[row, row, row],
        out_shape=[jax.ShapeDtypeStruct((S, 512), F32), jax.ShapeDtypeStruct((S, 512), F32), jax.ShapeDtypeStruct((S, 512), BF16)],
        compiler_params=_cp(),
    )(*os_, *ls_, zp)


def _dil_comb_bwd(zp, oc, dy, dz):
    S = zp.shape[0]
    T = T_ROW

    def body(g_ref, o_ref, dy_ref, dz_in, dz_ref, do_ref, D_ref):
        del dz_in
        g, o_, dy_ = g_ref[...], o_ref[...], dy_ref[...]
        do = dy_ * _silu(g)
        do_ref[...] = do
        dz_ref[...] = (dy_ * o_ * _dsilu(g)).astype(BF16)
        lane = _lane((T, LANE))
        for p in range(4):
            sl = slice(p * LANE, (p + 1) * LANE)
            D_ref[:, sl] = _head_sum(do[:, sl] * o_[:, sl], lane)

    row = pl.BlockSpec((T, 512), lambda i: (i, 0))
    zc = _zcol(T, 512, C_DILG)
    return pl.pallas_call(
        body, name="dil_comb_bwd", grid=(S // T,), in_specs=[zc, row, row, pl.BlockSpec(memory_space=pl.ANY)], out_specs=[zc, row, row],
        out_shape=[jax.ShapeDtypeStruct(dz.shape, BF16), jax.ShapeDtypeStruct((S, 512), F32), jax.ShapeDtypeStruct((S, 512), F32)],
        input_output_aliases={3: 0}, compiler_params=_cp(),
    )(zp, oc, dy, dz)


def _dil_attn_bwd(q, k, zp, do, L, Dr, g):
    S = q.shape[0]
    d, P, m, nb = _dil_geometry(g, S)
    R = DIL_ROWS
    n_q, n_k = 4, 2

    def body(*refs):
        q_side = refs[0:2 * n_q]
        k_side = refs[2 * n_q:2 * n_q + 2 * n_k]
        dq_ref, dk_ref, dv_ref = refs[2 * n_q + 2 * n_k:2 * n_q + 2 * n_k + 3]
        scr = refs[2 * n_q + 2 * n_k + 3:]
        sb = pl.program_id(0)
        if m > 1:
            for a in range(n_q):
                scr[a][0:R, :] = q_side[2 * a][...]
                scr[a][R:R + P, :] = q_side[2 * a + 1][...]
            for a in range(n_k):
                scr[n_q + a][0:P, :] = k_side[2 * a + 1][...]
                scr[n_q + a][P:P + R, :] = k_side[2 * a][...]
        lane = _lane((NK, LANE))

        def unit(u, carry):
            j = u // d
            start = j * P + (u - j * d)
            rows = _dil_rows(start, d)
            if m > 1:
                rows_b = _dil_rows(start + P, d)
                q2, do2, L2, D2 = [scr[a][_dil_rows(start, d, 2), :] for a in range(n_q)]
                kp, vp = [scr[n_q + a][rows, :] for a in range(n_k)]
                kc, vc = [scr[n_q + a][rows_b, :] for a in range(n_k)]
            else:
                q2, do2, L2, D2 = [jnp.concatenate([q_side[2 * a][rows, :], q_side[2 * a + 1][rows, :]], axis=0) for a in range(n_q)]
                kc, vc = [k_side[2 * a][rows, :] for a in range(n_k)]
                kp, vp = [k_side[2 * a + 1][rows, :] for a in range(n_k)]
            q2, do2 = q2.astype(BF16), do2.astype(BF16)
            kc, kp, vc, vp = kc.astype(BF16), kp.astype(BF16), vc.astype(BF16), vp.astype(BF16)
            n = sb * m + j
            hA = _lane((2 * NK, LANE)) < 64
            zq = jnp.zeros_like(q2)
            L2r, D2r = pltpu.roll(L2, 64, 1), pltpu.roll(D2, 64, 1)
            Q4 = jnp.concatenate([jnp.where(hA, q2, zq), jnp.where(hA, zq, q2)], axis=0)
            O4 = jnp.concatenate([jnp.where(hA, do2, zq), jnp.where(hA, zq, do2)], axis=0)
            L4 = jnp.concatenate([jnp.where(hA, L2, L2r), jnp.where(hA, L2r, L2)], axis=0)
            D4 = jnp.concatenate([jnp.where(hA, D2, D2r), jnp.where(hA, D2r, D2)], axis=0)
            row4 = lax.broadcasted_iota(jnp.int32, (4 * NK, NK), 0) & (2 * NK - 1)
            col4 = lax.broadcasted_iota(jnp.int32, (4 * NK, NK), 1)
            m4 = ((row4 < NK) & (col4 <= row4)) | ((row4 >= NK) & (col4 >= row4 - NK) & (n < nb - 1))
            p4 = jnp.exp(jnp.where(m4, _nt(Q4, kc) * DIL_SCALE, NEG) - L4)
            ds4 = (p4 * (_nt(O4, vc) - D4) * DIL_SCALE).astype(BF16)
            dk_tot = _tn(ds4, Q4)
            dv_tot = _tn(p4.astype(BF16), O4)
            pick = lambda x: jnp.concatenate([x[0:NK], x[2 * NK:3 * NK]], axis=0)
            Qn, On, Ln, Dn = pick(Q4), pick(O4), pick(L4), pick(D4)
            rowp = lax.broadcasted_iota(jnp.int32, (2 * NK, NK), 0) & (NK - 1)
            colp = lax.broadcasted_iota(jnp.int32, (2 * NK, NK), 1)
            pp = jnp.exp(jnp.where((colp >= rowp) & (n > 0), _nt(Qn, kp) * DIL_SCALE, NEG) - Ln)
            dsp = (pp * (_nt(On, vp) - Dn) * DIL_SCALE).astype(BF16)
            dq2 = _nn(pick(ds4), kc) + _nn(dsp, kp)
            dq_tot = jnp.where(lane < 64, dq2[0:NK], dq2[NK:2 * NK])
            dq_ref[rows, :] = dq_tot
            dk_ref[rows, :] = dk_tot
            dv_ref[rows, :] = dv_tot
            return carry

        lax.fori_loop(0, R // NK, unit, 0, unroll=8)

    qcur, qprv, qnxt = _dil_specs(g, S, 4 * g)
    vcur, vprv, _ = _dil_specs(g, S, C_DV + 4 * g)
    ocur, _, onxt = _dil_specs(g, S, 0)
    out = pl.BlockSpec((R, LANE), lambda sb, c: (sb, c))
    scratch = [pltpu.VMEM((P + R, LANE), F32)] * (n_q + n_k) if m > 1 else []
    return pl.pallas_call(
        body, name=f"dil_attn_bwd{g}", grid=(S // R, 4),
        in_specs=[qcur, qnxt, ocur, onxt, ocur, onxt, ocur, onxt, qcur, qprv, vcur, vprv],
        out_specs=[out, out, out], out_shape=[jax.ShapeDtypeStruct((S, 512), F32)] * 3, scratch_shapes=scratch, compiler_params=_cp(),
    )(q, q, do, do, L, L, Dr, Dr, k, k, zp, zp)


def _dil_pre_bwd(zp, dys, g, tab, dz, col, name):
    S = zp.shape[0]
    T = T_DIL

    def body(x_ref, dy0_ref, dy1_ref, dy2_ref, g_ref, C_ref, S1_ref, S2_ref, dz_in, dz_ref, dg_ref):
        del dz_in
        i = pl.program_id(0)
        C, S1, S2 = C_ref[...], S1_ref[...], S2_ref[...]
        lane = _lane((T, LANE))
        gv = g_ref[...]
        acc = jnp.zeros((1, LANE), F32)
        for b in range(12):
            sl = slice(b * LANE, (b + 1) * LANE)
            x = x_ref[:, sl]
            r = _head_stats(x, lane)
            xn = x * r
            dy_ref = (dy0_ref, dy1_ref, dy2_ref)[b // 4]
            dyn = _rope_t(dy_ref[:, (b % 4) * LANE:(b % 4 + 1) * LANE], C, S1, S2, 32)
            acc = acc + _csum(dyn * xn)
            dxh = dyn * gv
            dz_ref[:, sl] = (r * (dxh - xn * _head_sum(dxh * xn, lane) * (1.0 / DIL_HD))).astype(BF16)

        @pl.when(i == 0)
        def _():
            dg_ref[...] = acc

        @pl.when(i > 0)
        def _():
            dg_ref[...] += acc

    tabspec = pl.BlockSpec((T, LANE), lambda i: (i, 0))
    zc = _zcol(T, 1536, col)
    grp = pl.BlockSpec((T, 512), lambda i: (i, 0))
    return pl.pallas_call(
        body, name=name, grid=(S // T,),
        in_specs=[zc, grp, grp, grp, _full((1, LANE)), tabspec, tabspec, tabspec, pl.BlockSpec(memory_space=pl.ANY)],
        out_specs=[zc, _full((1, LANE))], out_shape=[jax.ShapeDtypeStruct(dz.shape, BF16), jax.ShapeDtypeStruct((1, LANE), F32)],
        input_output_aliases={8: 0}, compiler_params=_cp(),
    )(zp, *dys, g, *tab, dz)


def _dil_dv_into(dvs, dz):
    S = dz.shape[0]
    T = T_ROW

    def body(s0, s1, s2, dz_in, o_ref):
        del dz_in
        for gi, s in enumerate((s0, s1, s2)):
            o_ref[:, gi * 512:(gi + 1) * 512] = s[...].astype(BF16)

    grp = pl.BlockSpec((T, 512), lambda i: (i, 0))
    return pl.pallas_call(
        body, name="dil_dv", grid=(S // T,), in_specs=[grp, grp, grp, pl.BlockSpec(memory_space=pl.ANY)],
        out_specs=_zcol(T, 1536, C_DV), out_shape=jax.ShapeDtypeStruct(dz.shape, BF16), input_output_aliases={3: 0}, compiler_params=_cp(),
    )(*dvs, dz)


T_MRG = 256


def _merge_fwd(P, zp, b_merge):
    S = zp.shape[0]
    T = T_MRG

    def body(p0, p1, p2, m0, m1, m2, b_ref, o_ref):
        acc = jnp.zeros((T, D), F32)
        for j, (p, m) in enumerate(((p0, m0), (p1, m1), (p2, m2))):
            acc = acc + _sig(m[...] + b_ref[:, j * D:(j + 1) * D]) * p[...].astype(F32)
        o_ref[...] = acc.astype(BF16)

    row = pl.BlockSpec((T, D), lambda i: (i, 0))
    return pl.pallas_call(
        body, name="merge_fwd", grid=(S // T,),
        in_specs=[row, row, row] + [_zcol(T, D, C_MERGE + 8 * j) for j in range(3)] + [_full((1, 3 * D))], out_specs=row,
        out_shape=jax.ShapeDtypeStruct((S, D), BF16), compiler_params=_cp(),
    )(*P, zp, zp, zp, b_merge)


def _merge_bwd(dm, Pj, zp, bj, dz, j):
    S = zp.shape[0]
    T = T_MRG

    def body(dm_ref, p_ref, m_ref, b_ref, dz_in, dz_ref, dp_ref, db_ref):
        del dz_in
        i = pl.program_id(0)
        g = _sig(m_ref[...] + b_ref[...])
        dmv = dm_ref[...].astype(F32)
        dp_ref[...] = (dmv * g).astype(BF16)
        dg = dmv * p_ref[...].astype(F32) * g * (1.0 - g)
        dz_ref[...] = dg.astype(BF16)
        part = _csum(dg)

        @pl.when(i == 0)
        def _():
            db_ref[...] = part

        @pl.when(i > 0)
        def _():
            db_ref[...] += part

    row = pl.BlockSpec((T, D), lambda i: (i, 0))
    zc = _zcol(T, D, C_MERGE + 8 * j)
    return pl.pallas_call(
        body, name=f"merge_bwd{j}", grid=(S // T,), in_specs=[row, row, zc, _full((1, D)), pl.BlockSpec(memory_space=pl.ANY)],
        out_specs=[zc, row, _full((1, D))],
        out_shape=[jax.ShapeDtypeStruct(dz.shape, BF16), jax.ShapeDtypeStruct((S, D), BF16), jax.ShapeDtypeStruct((1, D), F32)],
        input_output_aliases={4: 0}, compiler_params=_cp(),
    )(dm, Pj, zp, bj, dz)


def _loss_fwd_bwd(y, target):
    S = y.shape[0]
    T = T_ROW

    def body(y_ref, t_ref, loss_ref, dy_ref):
        i = pl.program_id(0)
        err = y_ref[...] - t_ref[...]
        dy_ref[...] = err * (1.0 / D)
        part = jnp.sum(err * err, keepdims=True).reshape(1, 1) * (0.5 / D)

        @pl.when(i == 0)
        def _():
            loss_ref[...] = part

        @pl.when(i > 0)
        def _():
            loss_ref[...] += part

    row = pl.BlockSpec((T, D), lambda i: (i, 0))
    return pl.pallas_call(
        body, name="loss", grid=(S // T,), in_specs=[row, row], out_specs=[_full((1, 1)), row],
        out_shape=[jax.ShapeDtypeStruct((1, 1), F32), jax.ShapeDtypeStruct((S, D), F32)], compiler_params=_cp(),
    )(y, target)


def _layer_fwd(x, w, tabs):
    mla_tab, dil_tab = tabs
    S = x.shape[0]
    h = _rms_in_fwd(x, w["norm_g"])
    zp = _mm(h, w["w_in"], mode="nn", name="in_proj")
    hs, y_lru = _lru_fwd(zp, w)
    q, k, v = _mla_pre_fwd(zp, w, mla_tab)
    o_mla, lse, y_mla = _mla_attn_fwd(q, k, v, zp)
    qd, kd = _dil_pre_fwd(zp, w, dil_tab)
    og, lg = zip(*[_dil_attn_fwd(qd, kd, zp, g) for g in range(len(DIL_DILATIONS))])
    oc, L, y_dil = _dil_combine(og, lg, zp)
    P = [_mm(y_lru, w["w_lru_o"], mode="nn", name="lru_out", out_dtype=BF16),
         _mm(y_mla, w["w_mla_o"], mode="nn", name="mla_out", out_dtype=BF16),
         _mm(y_dil, w["w_dil_o"], mode="nn", name="dil_out", out_dtype=BF16)]
    merged = _merge_fwd(P, zp, w["b_merge"])
    x_out = _mm(merged, w["w_out"], mode="nn", name="out_proj", add=x)
    saved = dict(x=x, h=h, zp=zp, hs=hs, y=(y_lru, y_mla, y_dil), q=q, k=k, v=v, o_mla=o_mla, lse=lse, qd=qd, kd=kd, oc=oc, L=L, P=P,
                 merged=merged)
    return x_out, saved


def _layer_bwd(dout, w, tabs, sv, hook=None, after=None):
    mla_tab, dil_tab = tabs
    zp = sv["zp"]
    S = zp.shape[0]
    g = {}
    dm = _mm(dout, w["w_out"], mode="nt", name="d_merged", after=after, out_dtype=BF16)
    g["w_out"] = _mm(sv["merged"], dout, mode="tn", name="dw_out", out_dtype=BF16)
    dz = lax.empty((S, ZW), BF16)
    dP, db = [], []
    for j in range(3):
        dz, dpj, dbj = _merge_bwd(dm, sv["P"][j], zp, w["b_merge"][:, j * D:(j + 1) * D], dz, j)
        dP.append(dpj)
        db.append(dbj)
    g["b_merge"] = jnp.concatenate(db, axis=1)
    names = ("w_lru_o", "w_mla_o", "w_dil_o")
    dy = []
    for j in range(3):
        dy.append(_mm(dP[j], w[names[j]], mode="nt", name="dy_" + names[j]))
        g[names[j]] = _mm(sv["y"][j], dP[j], mode="tn", name="d" + names[j], out_dtype=BF16)
    dz = _lru_gate_bwd(zp, sv["hs"], dy[0], dz)
    dz, g["conv_w"], g["conv_b"], g["w_gx"], g["b_gx"], g["w_ga"], g["b_ga"], g["lam"] = _lru_bwd(zp, sv["hs"], dy[0], w, dz)
    dz, do, Dr = _mla_post_bwd(zp, sv["o_mla"], dy[1], dz)
    dq, dk, dv = _mla_attn_bwd(sv["q"], sv["k"], sv["v"], do, sv["lse"], Dr)
    dz, g["w_uq"], g["w_uk"], g["w_uv"], g["g_cq"], g["g_ckv"], g["g_mq"], g["g_mk"] = _mla_pre_bwd(zp, dq, dk, dv, w, mla_tab, dz)
    dz, dod, Dd = _dil_comb_bwd(zp, sv["oc"], dy[2], dz)
    dqs, dks, dvs = zip(*[_dil_attn_bwd(sv["qd"], sv["kd"], zp, dod, sv["L"], Dd, gi) for gi in range(len(DIL_DILATIONS))])
    dz, g["g_dq"] = _dil_pre_bwd(zp, dqs, w["g_dq"], dil_tab, dz, C_DQ, "dil_pre_bwd_q")
    dz, g["g_dk"] = _dil_pre_bwd(zp, dks, w["g_dk"], dil_tab, dz, C_DK, "dil_pre_bwd_k")
    dz = _dil_dv_into(dvs, dz)
    g["w_in"] = _mm(sv["h"], dz, mode="tn", name="dw_in", out_dtype=BF16, tk=S)
    token = hook(g) if hook is not None else None
    dh = _mm(dz, w["w_in"], mode="nt", name="d_h", after=token, tk=ZW // 4)
    dx, g["norm_g"] = _rms_in_bwd(sv["x"], w["norm_g"], dh, dout)
    return dx, g


def _peers():
    mx, my, mc = lax.axis_index("x"), lax.axis_index("y"), lax.axis_index("c")
    me = 4 * mx + 2 * my + mc
    out = []
    for k in range(1, N_DEV):
        px = 1 - mx if k & 4 else mx
        py = 1 - my if k & 2 else my
        pc = 1 - mc if k & 1 else mc
        out.append(((px, py, pc), 4 * px + 2 * py + pc))
    return me, out


def _whole(ref, p):
    del p
    return ref


def _exchange(srcs, slicers, slices, name):
    n = len(srcs)

    def body(*refs):
        ins, outs = refs[:n], refs[n:2 * n]
        send_sems, recv_sems, local_sems = refs[2 * n:]
        me, peers = _peers()
        mine = [pltpu.make_async_copy(slicers[a](ins[a], me), outs[a].at[me], local_sems.at[a]) for a in range(n)]
        for cp in mine:
            cp.start()
        copies = []
        for k, (peer, pidx) in enumerate(peers):
            for a in range(n):
                cp = pltpu.make_async_remote_copy(
                    src_ref=slicers[a](ins[a], pidx), dst_ref=outs[a].at[me], send_sem=send_sems.at[k * n + a],
                    recv_sem=recv_sems.at[k * n + a], device_id=peer, device_id_type=pl.DeviceIdType.MESH)
                cp.start()
                copies.append(cp)
        for cp in copies + mine:
            cp.wait()

    nsem = (N_DEV - 1) * n
    return pl.pallas_call(
        body, name=name, out_shape=[jax.ShapeDtypeStruct((N_DEV,) + shp, dt) for shp, dt in slices],
        in_specs=[pl.BlockSpec(memory_space=pl.ANY)] * n, out_specs=[pl.BlockSpec(memory_space=pl.ANY)] * n,
        scratch_shapes=[pltpu.SemaphoreType.DMA((nsem,)), pltpu.SemaphoreType.DMA((nsem,)), pltpu.SemaphoreType.DMA((n,))],
        compiler_params=pltpu.CompilerParams(has_side_effects=True),
    )(*srcs)


def _gather_two_level(srcs, name):
    n = len(srcs)

    def body(*refs):
        ins, outs = refs[:n], refs[n:2 * n]
        send_sems, recv_sems, local_sems = refs[2 * n:]
        mx, my, mc = lax.axis_index("x"), lax.axis_index("y"), lax.axis_index("c")
        me, sibling = (mx, my, mc), (mx, my, 1 - mc)
        chips = [(1 - mx, my), (mx, 1 - my), (1 - mx, 1 - my)]
        slot = lambda d: 4 * d[0] + 2 * d[1] + d[2]

        def copy(j, a, block, to, own=False):
            return pltpu.make_async_remote_copy(
                src_ref=ins[a] if own else outs[a].at[slot(block)], dst_ref=outs[a].at[slot(block)],
                send_sem=send_sems.at[j * n + a], recv_sem=recv_sems.at[j * n + a], device_id=to, device_id_type=pl.DeviceIdType.MESH)

        mine = [pltpu.make_async_copy(ins[a], outs[a].at[slot(me)], local_sems.at[a]) for a in range(n)]
        first = [copy(1 + j, a, me, (*chip, mc), own=True) for j, chip in enumerate(chips) for a in range(n)]
        first += [copy(0, a, me, sibling, own=True) for a in range(n)]
        for cp in mine + first:
            cp.start()
        passed = []
        for j, chip in enumerate(chips):
            for a in range(n):
                copy(1 + j, a, (*chip, mc), me).wait_recv()
                cp = copy(4 + j, a, (*chip, mc), sibling)
                cp.start()
                passed.append(cp)
        for a in range(n):
            copy(0, a, sibling, me).wait_recv()
        for j, chip in enumerate(chips):
            for a in range(n):
                copy(4 + j, a, (*chip, 1 - mc), me).wait_recv()
        for cp in first + passed:
            cp.wait_send()
        for cp in mine:
            cp.wait()

    nsem = (N_DEV - 1) * n
    return pl.pallas_call(
        body, name=name, out_shape=[jax.ShapeDtypeStruct((N_DEV,) + a.shape, a.dtype) for a in srcs],
        in_specs=[pl.BlockSpec(memory_space=pl.ANY)] * n, out_specs=[pl.BlockSpec(memory_space=pl.ANY)] * n,
        scratch_shapes=[pltpu.SemaphoreType.DMA((nsem,)), pltpu.SemaphoreType.DMA((nsem,)), pltpu.SemaphoreType.DMA((n,))],
        compiler_params=pltpu.CompilerParams(has_side_effects=True),
    )(*srcs)


_HBM = pl.BlockSpec(memory_space=pltpu.HBM)
_SEM = pl.BlockSpec(memory_space=pltpu.SEMAPHORE)
_DATAFLOW = pltpu.SideEffectType.DATAFLOW_SIDE_EFFECTING


def _plan_chips():
    mx, my, mc = lax.axis_index("x"), lax.axis_index("y"), lax.axis_index("c")
    return 2 * mx + my, [((cx, cy, mc), 2 * cx + cy) for cx, cy in ((1 - mx, my), (mx, 1 - my), (1 - mx, 1 - my))]


def _pair_exchange(srcs, slicers, slices, sliced, name):
    n = len(srcs)
    pieces = [4 if s else 1 for s in sliced]

    def body(*refs):
        ins, outs = refs[:n], refs[n:2 * n]
        send_sems, recv_sems = refs[2 * n:]
        mx, my, mc = lax.axis_index("x"), lax.axis_index("y"), lax.axis_index("c")
        copies = []
        for a in range(n):
            for q in range(pieces[a]):
                i = len(copies)
                copies.append(pltpu.make_async_remote_copy(
                    src_ref=slicers[a](ins[a], 2 * q + 1 - mc) if sliced[a] else ins[a], dst_ref=outs[a].at[q],
                    send_sem=send_sems.at[i], recv_sem=recv_sems.at[i], device_id=(mx, my, 1 - mc), device_id_type=pl.DeviceIdType.MESH))
        for cp in copies:
            cp.start()
        for cp in copies:
            cp.wait()

    return pl.pallas_call(
        body, name=name, out_shape=[jax.ShapeDtypeStruct((p,) + shp, dt) for (shp, dt), p in zip(slices, pieces)],
        in_specs=[pl.BlockSpec(memory_space=pl.ANY)] * n, out_specs=[pl.BlockSpec(memory_space=pl.ANY)] * n,
        scratch_shapes=[pltpu.SemaphoreType.DMA((sum(pieces),)), pltpu.SemaphoreType.DMA((sum(pieces),))],
        compiler_params=pltpu.CompilerParams(has_side_effects=True),
    )(*srcs)


def _pair_add(src, came, first_blk, axis, name):
    _, r, c = came.shape
    nblk = (c if axis == 1 else r) // LANE
    if axis == 1:
        s_spec = pl.BlockSpec((r, LANE), lambda q, j, fb: (0, fb[q] + j))
        o_spec = pl.BlockSpec((1, r, LANE), lambda q, j, fb: (q, 0, j))
    else:
        s_spec = pl.BlockSpec((LANE, c), lambda q, j, fb: (fb[q] + j, 0))
        o_spec = pl.BlockSpec((1, LANE, c), lambda q, j, fb: (q, j, 0))

    def body(fb_ref, x_ref, y_ref, o_ref):
        del fb_ref
        o_ref[0] = (x_ref[...].astype(F32) + y_ref[0].astype(F32)).astype(o_ref.dtype)

    return pl.pallas_call(
        body, name=name, out_shape=jax.ShapeDtypeStruct(came.shape, came.dtype),
        grid_spec=pltpu.PrefetchScalarGridSpec(num_scalar_prefetch=1, grid=(4, nblk), in_specs=[s_spec, o_spec], out_specs=o_spec),
        compiler_params=_cp(),
    )(first_blk, src, came)


def _add2(x, y, name):
    shp = x.shape
    x, y = x.reshape(-1, shp[-1]), y.reshape(-1, shp[-1])
    R, C = x.shape
    tr = R
    while tr * C * 4 > (1 << 21) and tr % 32 == 0:
        tr //= 2

    def body(x_ref, y_ref, o_ref):
        o_ref[...] = (x_ref[...].astype(F32) + y_ref[...].astype(F32)).astype(o_ref.dtype)

    spec = pl.BlockSpec((tr, C), lambda i: (i, 0))
    return pl.pallas_call(body, name=name, grid=(R // tr,), in_specs=[spec, spec], out_specs=spec,
                          out_shape=jax.ShapeDtypeStruct((R, C), x.dtype), compiler_params=_cp())(x, y).reshape(shp)


def _exchange_start(srcs, slicers, slices, after, name, plan=_peers, nslots=N_DEV):
    n = len(srcs)
    nsem = (nslots - 1) * n
    lands = [lax.empty((nslots,) + shp, dt) for shp, dt in slices]

    def body(*refs):
        ins, lands_in = refs[:n], refs[n:2 * n]
        send_sems, recv_sems, local_sems = refs[2 * n + 1], refs[2 * n + 2], refs[2 * n + 3]
        token = refs[-1]
        me, peers = plan()
        for a in range(n):
            pltpu.make_async_copy(slicers[a](ins[a], me), lands_in[a].at[me], local_sems.at[a]).start()
        for k, (peer, pidx) in enumerate(peers):
            for a in range(n):
                pltpu.make_async_remote_copy(
                    src_ref=slicers[a](ins[a], pidx), dst_ref=lands_in[a].at[me], send_sem=send_sems.at[k * n + a],
                    recv_sem=recv_sems.at[k * n + a], device_id=peer, device_id_type=pl.DeviceIdType.MESH).start()
        token[...] = jnp.zeros_like(token)

    hbm = lambda a: pltpu.with_memory_space_constraint(a, pltpu.HBM)
    return pl.pallas_call(
        body, name=name,
        out_shape=(pltpu.SemaphoreType.DMA((nsem,)), pltpu.SemaphoreType.DMA((nsem,)), pltpu.SemaphoreType.DMA((n,)),
                   *[pltpu.HBM(a.shape, a.dtype) for a in srcs], *[pltpu.HBM(a.shape, a.dtype) for a in lands],
                   jax.ShapeDtypeStruct((SUB, LANE), F32)),
        in_specs=[_HBM] * (2 * n) + [pl.BlockSpec(memory_space=pl.ANY)],
        out_specs=(_SEM, _SEM, _SEM, *[_HBM] * (2 * n), pl.BlockSpec(memory_space=pltpu.VMEM)),
        input_output_aliases={i: 3 + i for i in range(2 * n)},
        compiler_params=pltpu.CompilerParams(has_side_effects=_DATAFLOW),
    )(*[hbm(a) for a in srcs], *[hbm(a) for a in lands], after)


def _exchange_wait(started, slicers, after, name, plan=_peers):
    n = (len(started) - 4) // 2
    sems, thru = started[0:3], started[3:3 + 2 * n]

    def body(*refs):
        srcs, lands = refs[:n], refs[n:2 * n]
        send_sems, recv_sems, local_sems = refs[2 * n], refs[2 * n + 1], refs[2 * n + 2]
        me, peers = plan()
        for k, (peer, pidx) in enumerate(peers):
            for a in range(n):
                cp = pltpu.make_async_remote_copy(
                    src_ref=slicers[a](srcs[a], pidx), dst_ref=lands[a].at[me], send_sem=send_sems.at[k * n + a],
                    recv_sem=recv_sems.at[k * n + a], device_id=peer, device_id_type=pl.DeviceIdType.MESH)
                cp.wait_send()
                cp.wait_recv()
        for a in range(n):
            pltpu.make_async_copy(slicers[a](srcs[a], me), lands[a].at[me], local_sems.at[a]).wait()

    outs = pl.pallas_call(
        body, name=name, out_shape=[pltpu.HBM(a.shape, a.dtype) for a in thru],
        in_specs=[_HBM] * (2 * n) + [_SEM, _SEM, _SEM, pl.BlockSpec(memory_space=pl.ANY)], out_specs=[_HBM] * (2 * n),
        input_output_aliases={i: i for i in range(2 * n)}, compiler_params=pltpu.CompilerParams(has_side_effects=_DATAFLOW),
    )(*thru, *sems, after)
    return outs[n:]


WIN = 13 * LANE


def _win_base(s):
    n = s * SHARD_IN
    a0 = n + jnp.where(n >= _KR0, KR_LANE, 0) + jnp.where(n >= _KR0 + 32, 32, 0)
    return jnp.minimum(a0 // LANE, (ZW - WIN) // LANE)


def _win_offsets(s):
    n = s * SHARD_IN + jnp.arange(SHARD_IN)
    o = s * SHARD_IN - _win_base(s) * LANE
    return n, (o, o + KR_LANE, o + LANE - 32)


def _to_window(shard, s):
    _, offs = _win_offsets(s)
    padded = jnp.pad(shard, ((0, 0), (0, 0), (WIN, WIN)))
    a, b, c = [lax.dynamic_slice(padded, (0, 0, WIN - o), shard.shape[:2] + (WIN,)) for o in offs]
    col = (_win_base(s) * LANE + jnp.arange(WIN))[None, None, :]
    zero = jnp.zeros_like(a)
    return jnp.where(col < _KR0, a, jnp.where((col >= _KR0 + KR_LANE) & (col < _KR0 + KR_LANE + 32), b, jnp.where(col >= _KR0 + LANE, c, zero)))


def _from_window(win, s):
    n, offs = _win_offsets(s)
    a, b, c = [lax.dynamic_slice(win, (0, 0, o), win.shape[:2] + (SHARD_IN,)) for o in offs]
    return jnp.where((n < _KR0)[None, None, :], a, jnp.where((n < _KR0 + 32)[None, None, :], b, c))


def _win_base_static(s):
    n = s * SHARD_IN
    a0 = n + (KR_LANE if n >= _KR0 else 0) + (32 if n >= _KR0 + 32 else 0)
    return min(a0 // LANE, (ZW - WIN) // LANE)


def _assemble_w_in(gw):
    tr = 128
    bases = [_win_base_static(s) for s in range(N_DEV)]

    def body(g_ref, o_ref):
        for j in range(ZW // LANE):
            acc = None
            for s in range(N_DEV):
                if bases[s] <= j < bases[s] + WIN // LANE:
                    piece = g_ref[s, :, (j - bases[s]) * LANE:(j - bases[s] + 1) * LANE]
                    acc = piece if acc is None else acc + piece
            o_ref[:, j * LANE:(j + 1) * LANE] = acc

    return pl.pallas_call(
        body, name="assemble_w_in", grid=(D // tr,), in_specs=[pl.BlockSpec((N_DEV, tr, WIN), lambda i: (0, i, 0))],
        out_specs=pl.BlockSpec((tr, ZW), lambda i: (i, 0)), out_shape=jax.ShapeDtypeStruct((D, ZW), gw.dtype), compiler_params=_cp(),
    )(gw)


def _cols(width):
    return lambda ref, p: ref.at[:, pl.ds(pl.multiple_of(p * width, width), width)]


def _rows(height):
    return lambda ref, p: ref.at[pl.ds(pl.multiple_of(p * height, height), height), :]


SCATTER = {
    'w_in': (lambda ref, p: ref.at[:, pl.ds(pl.multiple_of(_win_base(p) * LANE, LANE), WIN)], (D, WIN), BF16),
    'conv_w': (_cols(LANE), (4, LANE), F32),
    'w_lru_o': (_rows(LANE), (LANE, D), BF16),
    'w_uq': (_cols(LANE), (256, LANE), F32),
    'w_ukv': (_cols(LANE), (128, LANE), F32),
    'w_mla_o': (_cols(LANE), (512, LANE), BF16),
    'w_dil_o': (_cols(LANE), (512, LANE), BF16),
    'w_out': (_rows(LANE), (LANE, D), BF16),
}
SLICED_AXIS = {'w_in': 1, 'conv_w': 1, 'w_lru_o': 0, 'w_uq': 1, 'w_ukv': 1, 'w_mla_o': 1, 'w_dil_o': 1, 'w_out': 0}


PACK_ROWS = 64


def _packed_rows(shapes):
    n = sum(int(np.prod(s)) for s in shapes)
    return -(-n // (PACK_ROWS * LANE)) * PACK_ROWS


def _sum8(buf, name):
    ns, R, C = buf.shape
    tr = R
    while tr * C * 4 * ns > (1 << 22) and tr % 32 == 0:
        tr //= 2

    def body(b_ref, o_ref):
        acc = b_ref[0].astype(F32)
        for s in range(1, ns):
            acc = acc + b_ref[s].astype(F32)
        o_ref[...] = acc

    return pl.pallas_call(
        body, name=name, grid=(R // tr,), in_specs=[pl.BlockSpec((ns, tr, C), lambda i: (0, i, 0))],
        out_specs=pl.BlockSpec((tr, C), lambda i: (i, 0)), out_shape=jax.ShapeDtypeStruct((R, C), F32), compiler_params=_cp(),
    )(buf)


def _pack(arrs, dtype, lead):
    flat = [a.astype(dtype).reshape(a.shape[:lead] + (-1,)) for a in arrs]
    cat = jnp.concatenate(flat, axis=-1)
    n = cat.shape[-1]
    unit = PACK_ROWS * LANE
    pad = (-n) % unit
    if pad:
        cat = jnp.pad(cat, [(0, 0)] * lead + [(0, pad)])
    return cat.reshape(cat.shape[:lead] + ((n + pad) // LANE, LANE))


def _unpack(buf, shapes, lead):
    flat = buf.reshape(buf.shape[:lead] + (-1,))
    out, off = [], 0
    for shp in shapes:
        n = int(np.prod(shp))
        out.append(flat[..., off:off + n].reshape(buf.shape[:lead] + tuple(shp)))
        off += n
    return out


def _adamw(w, g, m, v, name):
    layers, rows, cols = w.shape
    tr = rows
    while tr * cols * 4 > (3 << 19) and tr % 16 == 0:
        tr //= 2
    c1 = 1.0 - ADAM_B1 ** ADAM_STEP
    c2 = 1.0 - ADAM_B2 ** ADAM_STEP

    def body(w_ref, g_ref, m_ref, v_ref, d_ref, mo_ref, vo_ref):
        gv = g_ref[...]
        mn = ADAM_B1 * m_ref[...] + (1.0 - ADAM_B1) * gv
        vn = ADAM_B2 * v_ref[...] + (1.0 - ADAM_B2) * (gv * gv)
        mo_ref[...] = mn
        vo_ref[...] = vn
        d_ref[...] = -ADAM_LR * ((mn / c1) / (jnp.sqrt(vn / c2) + ADAM_EPS) + ADAM_WD * w_ref[...])

    spec = pl.BlockSpec((1, tr, cols), lambda l, i: (l, i, 0))
    return pl.pallas_call(
        body, name=name, grid=(layers, rows // tr), in_specs=[spec] * 4, out_specs=[spec] * 3,
        out_shape=[jax.ShapeDtypeStruct((layers, rows, cols), F32)] * 3, compiler_params=_cp(),
    )(w, g, m, v)


IN_NAMES = ['x', 'positions', 'norm_g', 'w_in', 'conv_w', 'conv_b', 'w_gate_x', 'b_gate_x', 'w_gate_a', 'b_gate_a', 'lru_lambda', 'w_lru_o',
            'cq_norm_g', 'ckv_norm_g', 'w_uq', 'w_ukv', 'mla_q_norm_g', 'mla_k_norm_g', 'w_mla_o', 'dil_q_norm_g', 'dil_k_norm_g', 'w_dil_o',
            'b_merge', 'w_out']
WEIGHTS = IN_NAMES[2:]
REPLICATED = [n for n in WEIGHTS if n not in SCATTER]
GATE_WEIGHTS = ('w_gate_x', 'w_gate_a')

_KR0 = C_KR * LANE


GATHERED = ['w_in', 'w_lru_o', 'w_uq', 'w_ukv', 'w_mla_o', 'w_dil_o', 'w_out', 'conv_w']


def _local_weights(wd, me):
    loc = {n: wd[n].astype(BF16) for n in GATHERED[:-1]}
    loc['w_in'] = _to_window(loc['w_in'], me)
    loc['w_uq'] = jnp.pad(loc['w_uq'], ((0, 0), (0, 0), (0, LANE - MLA_QK)))
    loc['conv_w'] = wd['conv_w']
    return [[loc[n][l] for n in GATHERED] for l in range(DEPTH)]


def _layer_weights(gathered, rep, l):
    gw = dict(zip(GATHERED, gathered))
    by_rows = lambda a: a.reshape(-1, a.shape[-1])
    by_cols = lambda a: jnp.swapaxes(a, 0, 1).reshape(a.shape[1], -1)
    ukv = jnp.swapaxes(gw['w_ukv'], 0, 1)
    g96 = lambda a: jnp.pad(a[l].reshape(1, MLA_QK), ((0, 0), (0, LANE - MLA_QK)))
    g64 = lambda a: jnp.tile(a[l].reshape(1, DIL_HD), (1, 2))
    return dict(
        norm_g=rep['norm_g'][l].reshape(1, D), w_in=_assemble_w_in(gw['w_in']),
        conv_w=by_cols(gw['conv_w']), conv_b=rep['conv_b'][l].reshape(1, D),
        w_gx=rep['w_gate_x'][l].astype(BF16), b_gx=rep['b_gate_x'][l].reshape(8, 1, LANE),
        w_ga=rep['w_gate_a'][l].astype(BF16), b_ga=rep['b_gate_a'][l].reshape(8, 1, LANE),
        lam=rep['lru_lambda'][l].reshape(1, D),
        w_lru_o=by_rows(gw['w_lru_o']), w_mla_o=by_cols(gw['w_mla_o']), w_dil_o=by_cols(gw['w_dil_o']), w_out=by_rows(gw['w_out']),
        g_cq=rep['cq_norm_g'][l].reshape(1, 256), g_ckv=rep['ckv_norm_g'][l].reshape(1, 128),
        w_uq=by_cols(gw['w_uq']), w_uk=jnp.pad(ukv[:, :, :64], ((0, 0), (0, 0), (0, 64))).reshape(128, 1024),
        w_uv=ukv[:, :, 64:].reshape(128, 512),
        g_mq=g96(rep['mla_q_norm_g']), g_mk=g96(rep['mla_k_norm_g']), g_dq=g64(rep['dil_q_norm_g']), g_dk=g64(rep['dil_k_norm_g']),
        b_merge=rep['b_merge'][l].reshape(1, 3 * D),
    )


def _sharded_grads(g):
    uk = g['w_uk'].reshape(128, 8, 128)[:, :, :64]
    uv = g['w_uv'].reshape(128, 8, 64)
    d = {'w_in': g['w_in'], 'conv_w': g['conv_w'], 'w_lru_o': g['w_lru_o'], 'w_uq': g['w_uq'],
         'w_ukv': jnp.concatenate([uk, uv], axis=-1).reshape(128, 1024), 'w_mla_o': g['w_mla_o'], 'w_dil_o': g['w_dil_o'],
         'w_out': g['w_out']}
    return [d[n] for n in SCATTER]


def _replicated_grads(g):
    return {
        'conv_b': g['conv_b'].reshape(D),
        'w_gate_x': g['w_gx'], 'b_gate_x': g['b_gx'].reshape(8, LANE), 'w_gate_a': g['w_ga'], 'b_gate_a': g['b_ga'].reshape(8, LANE),
        'lru_lambda': g['lam'].reshape(D), 'cq_norm_g': g['g_cq'].reshape(256), 'ckv_norm_g': g['g_ckv'].reshape(128),
        'mla_q_norm_g': g['g_mq'][0, :MLA_QK], 'mla_k_norm_g': g['g_mk'][0, :MLA_QK],
        'dil_q_norm_g': g['g_dq'][0, :DIL_HD] + g['g_dq'][0, DIL_HD:], 'dil_k_norm_g': g['g_dk'][0, :DIL_HD] + g['g_dk'][0, DIL_HD:],
        'b_merge': g['b_merge'].reshape(3 * D),
    }


def kernel(x, positions, norm_g, w_in, conv_w, conv_b, w_gate_x, b_gate_x, w_gate_a, b_gate_a, lru_lambda, w_lru_o, cq_norm_g, ckv_norm_g, w_uq, w_ukv, mla_q_norm_g, mla_k_norm_g, w_mla_o, dil_q_norm_g, dil_k_norm_g, w_dil_o, b_merge, w_out, loss_target, m_norm_g, m_w_in, m_conv_w, m_conv_b, m_w_gate_x, m_b_gate_x, m_w_gate_a, m_b_gate_a, m_lru_lambda, m_w_lru_o, m_cq_norm_g, m_ckv_norm_g, m_w_uq, m_w_ukv, m_mla_q_norm_g, m_mla_k_norm_g, m_w_mla_o, m_dil_q_norm_g, m_dil_k_norm_g, m_w_dil_o, m_b_merge, m_w_out, v_norm_g, v_w_in, v_conv_w, v_conv_b, v_w_gate_x, v_b_gate_x, v_w_gate_a, v_b_gate_a, v_lru_lambda, v_w_lru_o, v_cq_norm_g, v_ckv_norm_g, v_w_uq, v_w_ukv, v_mla_q_norm_g, v_mla_k_norm_g, v_w_mla_o, v_dil_q_norm_g, v_dil_k_norm_g, v_w_dil_o, v_b_merge, v_w_out):
    args = (x, positions, norm_g, w_in, conv_w, conv_b, w_gate_x, b_gate_x, w_gate_a, b_gate_a, lru_lambda, w_lru_o, cq_norm_g, ckv_norm_g, w_uq, w_ukv, mla_q_norm_g, mla_k_norm_g, w_mla_o, dil_q_norm_g, dil_k_norm_g, w_dil_o, b_merge, w_out)
    moments_m = (m_norm_g, m_w_in, m_conv_w, m_conv_b, m_w_gate_x, m_b_gate_x, m_w_gate_a, m_b_gate_a, m_lru_lambda, m_w_lru_o, m_cq_norm_g, m_ckv_norm_g, m_w_uq, m_w_ukv, m_mla_q_norm_g, m_mla_k_norm_g, m_w_mla_o, m_dil_q_norm_g, m_dil_k_norm_g, m_w_dil_o, m_b_merge, m_w_out)
    moments_v = (v_norm_g, v_w_in, v_conv_w, v_conv_b, v_w_gate_x, v_b_gate_x, v_w_gate_a, v_b_gate_a, v_lru_lambda, v_w_lru_o, v_cq_norm_g, v_ckv_norm_g, v_w_uq, v_w_ukv, v_mla_q_norm_g, v_mla_k_norm_g, v_w_mla_o, v_dil_q_norm_g, v_dil_k_norm_g, v_w_dil_o, v_b_merge, v_w_out)
    a = dict(zip(IN_NAMES, args))
    wd = {n: a[n] for n in WEIGHTS}
    md = dict(zip(WEIGHTS, moments_m))
    vd = dict(zip(WEIGHTS, moments_v))

    me = 4 * lax.axis_index("x") + 2 * lax.axis_index("y") + lax.axis_index("c")

    assert DEPTH == 2
    xs, tabs = x[0], _rope_tables(positions[0])
    whole = [_whole] * len(GATHERED)
    slicers = [SCATTER[n][0] for n in SCATTER]
    grad_slices = [SCATTER[n][1:3] for n in SCATTER]

    local = _local_weights(wd, me)
    w_slices = [(a.shape, a.dtype) for a in local[0]]
    landed0 = _gather_two_level(local[0], "gather_w0")
    flying = _exchange_start(local[1], whole, w_slices, landed0[0], "gather_w1_start")
    rep0 = dict(wd, norm_g=wd['norm_g'] + flying[-1][0, 0])
    w0 = _layer_weights(landed0, rep0, 0)
    x1, saved0 = _layer_fwd(xs, w0, tabs)
    w1 = _layer_weights(_exchange_wait(flying, whole, x1, "gather_w1_wait"), wd, 1)
    x2, saved1 = _layer_fwd(x1, w1, tabs)
    loss, dx2 = _loss_fwd_bwd(x2, loss_target[0])
    loss = loss[0, 0]

    sharded = list(SCATTER)
    nsh = len(sharded)
    small = [n for n in REPLICATED if n not in GATE_WEIGHTS and n != 'norm_g']

    def outgoing(g):
        r = _replicated_grads(g)
        return (_sharded_grads(g) + [_pack([r[n] for n in small], F32, 0)]
                + [r[n].astype(BF16).reshape(8 * LANE, LANE) for n in GATE_WEIGHTS])

    out_slicers = slicers + [_whole] * 3
    out_slices = grad_slices + [((_packed_rows([wd[n].shape[1:] for n in small]), LANE), F32)] + [((8 * LANE, LANE), BF16)] * 2
    dx1, g1 = _layer_bwd(dx2, w1, tabs, saved1)
    flying1 = _exchange_start(outgoing(g1), out_slicers, out_slices, dx1, "scatter_g1_start")
    later = {}

    names = sharded + ['small'] + list(GATE_WEIGHTS)
    sliced = [True] * nsh + [False] * 3
    by_chip = [(lambda ref, q: ref.at[q])] * nsh + [_whole] * 3

    def send_layer0(g):
        later['got1'] = _exchange_wait(flying1, out_slicers, g['w_in'], "scatter_g1_wait")
        mine = outgoing(g)
        came = _pair_exchange(mine, out_slicers, out_slices, sliced, "pair_g0")
        my_side = 2 * jnp.arange(4, dtype=jnp.int32) + lax.axis_index("c")
        halves = []
        for n, a, c in zip(names, mine, came):
            if n in SCATTER:
                first = _win_base(my_side) if n == 'w_in' else my_side
                halves.append(_pair_add(a, c, first.astype(jnp.int32), SLICED_AXIS[n], f"pair_sum_{n}"))
            else:
                halves.append(_add2(a, c[0], f"pair_sum_{n}"))
        later['flying0'] = _exchange_start(halves, by_chip, out_slices, later['got1'][0], "scatter_g0_start", plan=_plan_chips, nslots=4)
        return later['flying0'][-1]

    grad_x, g0 = _layer_bwd(dx1, w0, tabs, saved0, hook=send_layer0, after=flying1[-1])
    sum1 = [_sum8(b, f"sum_{n}_1") for n, b in zip(names, later['got1'])]
    behind = grad_x[:1, :1] + sum(s_[:1, :1] for s_ in sum1)
    got0 = _exchange_wait(later['flying0'], by_chip, behind, "scatter_g0_wait", plan=_plan_chips)
    sum0 = [_sum8(b, f"sum_{n}_0") for n, b in zip(names, got0)]
    norm_part = _pack([jnp.stack([g['norm_g'].reshape(D) for g in (g0, g1)])], F32, 0)
    norm_sum = _sum8(_exchange([norm_part], [_whole], [(norm_part.shape, F32)], "gather_norm_g")[0], "sum_norm_g")

    gsh = {n: jnp.stack([sum0[i], sum1[i]]) for i, n in enumerate(sharded)}
    gsh['w_in'] = _from_window(gsh['w_in'], me)
    gsh['w_uq'] = gsh['w_uq'][:, :, :MLA_QK]
    grep = {'norm_g': _unpack(norm_sum, [wd['norm_g'].shape], 0)[0]}
    per_layer = [_unpack(s[nsh], [wd[n].shape[1:] for n in small], 0) for s in (sum0, sum1)]
    grep.update({n: jnp.stack([per_layer[l][i] for l in range(DEPTH)]) for i, n in enumerate(small)})
    for i, n in enumerate(GATE_WEIGHTS):
        grep[n] = jnp.stack([sum0[nsh + 1 + i], sum1[nsh + 1 + i]]).reshape(wd[n].shape)

    out_g, out_d, out_m, out_v = {}, {}, {}, {}
    vecs = ['norm_g'] + small
    vshapes = [wd[n].shape for n in vecs]
    packed_g = _pack([grep[n] for n in vecs], F32, 0)
    d_, m_, v_ = _adamw(_pack([wd[n] for n in vecs], F32, 0)[None], packed_g[None], _pack([md[n] for n in vecs], F32, 0)[None],
                        _pack([vd[n] for n in vecs], F32, 0)[None], "adamw_vectors")
    for dst, buf in ((out_d, d_), (out_m, m_), (out_v, v_)):
        dst.update(zip(vecs, _unpack(buf[0], vshapes, 0)))
    out_g.update({n: grep[n] for n in vecs})
    gsh.update({n: grep[n] for n in GATE_WEIGHTS})
    for n in sharded + list(GATE_WEIGHTS):
        shp = wd[n].shape
        three = (1, -1, shp[-1])
        d_, m_, v_ = _adamw(wd[n].reshape(three), gsh[n].reshape(three), md[n].reshape(three), vd[n].reshape(three), "adamw_" + n)
        out_g[n], out_d[n], out_m[n], out_v[n] = gsh[n], d_.reshape(shp), m_.reshape(shp), v_.reshape(shp)

    loss = lax.psum(loss, ("x", "y", "c"))
    return (loss, grad_x[None], *[out_g[n] for n in WEIGHTS], *[out_d[n] for n in WEIGHTS], *[out_m[n] for n in WEIGHTS],
            *[out_v[n] for n in WEIGHTS])
```

```python
import numpy as np
import jax
import jax.numpy as jnp
from jax import lax
from jax.experimental import pallas as pl
from jax.experimental.pallas import tpu as pltpu

F32 = jnp.float32
BF16 = jnp.bfloat16

N_DEV = 8
D = 1024
DEPTH = 2
EPS = 1e-6
ROPE_THETA = 10000.0
LRU_C = 8.0
LANE = 128
SUB = 8
IN_WIDTH = 11168
SHARD_IN = IN_WIDTH // N_DEV

C_LRUX, C_LRUG, C_CQ, C_CKV, C_KR, C_MLAG, C_DQ, C_DK, C_DV, C_DILG, C_MERGE = 0, 8, 16, 18, 19, 20, 24, 36, 48, 60, 64
ZW = 88 * LANE
KR_LANE = 64

MLA_QK = 96
MLA_SCALE = MLA_QK ** -0.5
DIL_HD = 64
DIL_SCALE = DIL_HD ** -0.5
DIL_DILATIONS = (1, 4, 16)
NK = 128

ADAM_LR, ADAM_B1, ADAM_B2, ADAM_EPS, ADAM_WD, ADAM_STEP = 0.001, 0.9, 0.999, 1e-08, 0.01, 10

NEG = -1e30
LOG2E = 1.4426950408889634
VMEM_LIMIT = 48 * 1024 * 1024


def _cp(**kw):
    return pltpu.CompilerParams(vmem_limit_bytes=VMEM_LIMIT, **kw)


def _sig(x):
    return 1.0 / (1.0 + jnp.exp(-x))


def _silu(x):
    return x * _sig(x)


def _dsilu(x):
    s = _sig(x)
    return s * (1.0 + x * (1.0 - s))


def _dot(a, b, dims):
    return lax.dot_general(a, b, (dims, ((), ())), preferred_element_type=F32)


def _nn(a, b):
    return _dot(a, b, ((1,), (0,)))


def _nt(a, b):
    return _dot(a, b, ((1,), (1,)))


def _tn(a, b):
    return _dot(a, b, ((0,), (0,)))


def _rsum(x):
    return jnp.sum(x, axis=-1, keepdims=True)


def _rsum_mxu(x):
    ones = jnp.ones((x.shape[-1], LANE), F32)
    return lax.dot_general(x, ones, (((1,), (0,)), ((), ())), precision=lax.Precision.HIGHEST, preferred_element_type=F32)


def _csum(x):
    return jnp.sum(x, axis=0, keepdims=True)


def _mm(a, b, *, mode, name, out_dtype=F32, add=None, after=None, tm=1024, tn=1024, tk=1024):
    if mode == "nn":
        (M, K), (K2, N) = a.shape, b.shape
    elif mode == "nt":
        (M, K), (N, K2) = a.shape, b.shape
    else:
        (K, M), (K2, N) = a.shape, b.shape
    assert K == K2
    tm, tn, tk = min(tm, M), min(tn, N), min(tk, K)
    assert M % tm == 0 and N % tn == 0 and K % tk == 0
    nk = K // tk
    fn = {"nn": _nn, "nt": _nt, "tn": _tn}[mode]
    has_add = add is not None

    def body(*refs):
        a_ref, b_ref = refs[0], refs[1]
        add_ref = refs[2] if has_add else None
        o_ref = refs[2 + has_add + (after is not None)]
        part = fn(a_ref[...].astype(BF16), b_ref[...].astype(BF16))

        def fin(acc):
            if has_add:
                acc = acc + add_ref[...]
            o_ref[...] = acc.astype(out_dtype)

        if nk == 1:
            fin(part)
        else:
            acc_ref = refs[-1]
            k = pl.program_id(2)

            @pl.when(k == 0)
            def _():
                acc_ref[...] = part

            @pl.when(k > 0)
            def _():
                acc_ref[...] += part

            @pl.when(k == nk - 1)
            def _():
                fin(acc_ref[...])

    a_spec = pl.BlockSpec((tk, tm), lambda i, j, k: (k, i)) if mode == "tn" else pl.BlockSpec((tm, tk), lambda i, j, k: (i, k))
    b_spec = pl.BlockSpec((tn, tk), lambda i, j, k: (j, k)) if mode == "nt" else pl.BlockSpec((tk, tn), lambda i, j, k: (k, j))
    o_spec = pl.BlockSpec((tm, tn), lambda i, j, k: (i, j))
    in_specs, args = [a_spec, b_spec], [a, b]
    if has_add:
        in_specs.append(o_spec)
        args.append(add)
    if after is not None:
        in_specs.append(pl.BlockSpec(memory_space=pl.ANY))
        args.append(after)
    return pl.pallas_call(
        body, name=name, grid=(M // tm, N // tn, nk), in_specs=in_specs, out_specs=o_spec,
        out_shape=jax.ShapeDtypeStruct((M, N), out_dtype),
        scratch_shapes=[pltpu.VMEM((tm, tn), F32)] if nk > 1 else [],
        compiler_params=_cp(dimension_semantics=("parallel", "parallel", "arbitrary")),
    )(*args)


T_ROW = 512


def _rms_in_fwd(x, g):
    S = x.shape[0]
    T = T_ROW

    def body(x_ref, g_ref, h_ref):
        xv = x_ref[...]
        r = lax.rsqrt(jnp.mean(xv * xv, axis=-1, keepdims=True) + EPS)
        h_ref[...] = (xv * r * g_ref[...]).astype(BF16)

    return pl.pallas_call(
        body, name="rms_in_fwd", grid=(S // T,),
        in_specs=[pl.BlockSpec((T, D), lambda i: (i, 0)), pl.BlockSpec((1, D), lambda i: (0, 0))],
        out_specs=pl.BlockSpec((T, D), lambda i: (i, 0)),
        out_shape=jax.ShapeDtypeStruct((S, D), BF16), compiler_params=_cp(),
    )(x, g)


def _rms_in_bwd(x, g, dh, dres):
    S = x.shape[0]
    T = T_ROW

    def body(x_ref, g_ref, dh_ref, dr_ref, dx_ref, dg_ref):
        i = pl.program_id(0)
        xv = x_ref[...]
        r = lax.rsqrt(jnp.mean(xv * xv, axis=-1, keepdims=True) + EPS)
        xn = xv * r
        dy = dh_ref[...]
        part = _csum(dy * xn)

        @pl.when(i == 0)
        def _():
            dg_ref[...] = part

        @pl.when(i > 0)
        def _():
            dg_ref[...] += part

        dxh = dy * g_ref[...]
        dx_ref[...] = dr_ref[...] + r * (dxh - xn * jnp.mean(dxh * xn, axis=-1, keepdims=True))

    row = pl.BlockSpec((T, D), lambda i: (i, 0))
    vec = pl.BlockSpec((1, D), lambda i: (0, 0))
    return pl.pallas_call(
        body, name="rms_in_bwd", grid=(S // T,), in_specs=[row, vec, row, row], out_specs=[row, vec],
        out_shape=[jax.ShapeDtypeStruct((S, D), F32), jax.ShapeDtypeStruct((1, D), F32)], compiler_params=_cp(),
    )(x, g, dh, dres)


T_LRU = 1024
T_LRU_BWD = 512


def _neg_expm1(y):
    ser = -y * (1.0 + y * 0.5 * (1.0 + y * (1.0 / 3.0) * (1.0 + y * 0.25 * (1.0 + y * 0.2))))
    return jnp.where(y > -0.03, ser, 1.0 - jnp.exp(y))


def _softplus_neg(lam):
    e = jnp.exp(-jnp.abs(lam))
    l1p = jnp.where(e < 0.01, e * (1.0 - e * (0.5 - e * (1.0 / 3.0 - e * 0.25))), jnp.log(1.0 + e))
    return jnp.maximum(-lam, 0.0) + l1p


def _scan_fwd(a, b, T):
    row = lax.broadcasted_iota(jnp.int32, a.shape, 0)
    d = 1
    while d < T:
        m = row >= d
        b = jnp.where(m, a * pltpu.roll(b, d, 0) + b, b)
        a = jnp.where(m, a * pltpu.roll(a, d, 0), a)
        d *= 2
    return a, b


def _scan_bwd(a, b, T):
    row = lax.broadcasted_iota(jnp.int32, a.shape, 0)
    d = 1
    while d < T:
        m = row < T - d
        b = jnp.where(m, a * pltpu.roll(b, T - d, 0) + b, b)
        a = jnp.where(m, a * pltpu.roll(a, T - d, 0), a)
        d *= 2
    return b


def _lru_common(x, prev, first, cw_ref, cb_ref, wgx_ref, bgx_ref, wga_ref, bga_ref, lam_ref, T):
    row = lax.broadcasted_iota(jnp.int32, x.shape, 0)
    prev = jnp.where(first, 0.0, prev)
    xs = []
    for j in (3, 2, 1):
        pv = jnp.tile(pltpu.roll(prev, j, 0), (T // SUB, 1))
        xs.append(jnp.where(row < j, pv, pltpu.roll(x, j, 0)))
    xs.append(x)
    xc = cb_ref[...] + cw_ref[0:1, :] * xs[0] + cw_ref[1:2, :] * xs[1] + cw_ref[2:3, :] * xs[2] + cw_ref[3:4, :] * xs[3]
    xcb = xc.astype(BF16)
    gx = _sig(_nn(xcb, wgx_ref[0]) + bgx_ref[0])
    ga = _sig(_nn(xcb, wga_ref[0]) + bga_ref[0])
    sp = _softplus_neg(lam_ref[...])
    log_a = -LRU_C * ga * sp
    a = jnp.exp(log_a)
    mult = jnp.sqrt(_neg_expm1(2.0 * log_a))
    return xs, xc, xcb, gx, ga, sp, a, mult


def _lru_specs(T, tmap):
    def at(col0):
        return pl.BlockSpec((T, LANE), lambda n, i: (tmap(i), col0 + n))

    def prev(col0):
        return pl.BlockSpec((SUB, LANE), lambda n, i: (jnp.maximum(tmap(i) * (T // SUB) - 1, 0), col0 + n))

    small = [
        pl.BlockSpec((4, LANE), lambda n, i: (0, n)),
        pl.BlockSpec((1, LANE), lambda n, i: (0, n)),
        pl.BlockSpec((1, LANE, LANE), lambda n, i: (n, 0, 0)),
        pl.BlockSpec((1, 1, LANE), lambda n, i: (n, 0, 0)),
        pl.BlockSpec((1, LANE, LANE), lambda n, i: (n, 0, 0)),
        pl.BlockSpec((1, 1, LANE), lambda n, i: (n, 0, 0)),
        pl.BlockSpec((1, LANE), lambda n, i: (0, n)),
    ]
    return at, prev, small


def _lru_fwd(zp, w):
    S = zp.shape[0]
    T = T_LRU
    at, prev, small = _lru_specs(T, lambda i: i)

    def body(x_ref, xp_ref, g_ref, cw_ref, cb_ref, wgx_ref, bgx_ref, wga_ref, bga_ref, lam_ref, hs_ref, y_ref, carry_ref):
        i = pl.program_id(1)

        @pl.when(i == 0)
        def _():
            carry_ref[...] = jnp.zeros_like(carry_ref)

        x = x_ref[...]
        _, xc, _, gx, _, _, a, mult = _lru_common(x, xp_ref[...], i == 0, cw_ref, cb_ref, wgx_ref, bgx_ref, wga_ref, bga_ref, lam_ref, T)
        A, B = _scan_fwd(a, mult * gx * xc, T)
        h = B + A * carry_ref[SUB - 1:SUB, :]
        hs_ref[...] = h
        carry_ref[...] = hs_ref[T - SUB:T, :]
        y_ref[...] = (h * _silu(g_ref[...])).astype(BF16)

    out = pl.BlockSpec((T, LANE), lambda n, i: (i, n))
    return pl.pallas_call(
        body, name="lru_fwd", grid=(8, S // T),
        in_specs=[at(C_LRUX), prev(C_LRUX), at(C_LRUG)] + small, out_specs=[out, out],
        out_shape=[jax.ShapeDtypeStruct((S, D), F32), jax.ShapeDtypeStruct((S, D), BF16)],
        scratch_shapes=[pltpu.VMEM((SUB, LANE), F32)],
        compiler_params=_cp(dimension_semantics=("parallel", "arbitrary")),
    )(zp, zp, zp, w["conv_w"], w["conv_b"], w["w_gx"], w["b_gx"], w["w_ga"], w["b_ga"], w["lam"])


def _lru_bwd(zp, hs, dy, w, dz):
    S = zp.shape[0]
    T = T_LRU_BWD
    nT = S // T
    at, prev, small = _lru_specs(T, lambda i: nT - 1 - i)

    def body(x_ref, xp_ref, g_ref, h_ref, hp_ref, dy_ref, cw_ref, cb_ref, wgx_ref, bgx_ref, wga_ref, bga_ref, lam_ref, dz_in,
             dzx_ref, dcw_ref, dcb_ref, dwgx_ref, dbgx_ref, dwga_ref, dbga_ref, dlam_ref, carry_ref, head_ref):
        del dz_in
        j = pl.program_id(1)
        it = nT - 1 - j

        @pl.when(j == 0)
        def _():
            for r in (carry_ref, head_ref, dcw_ref, dcb_ref, dwgx_ref, dbgx_ref, dwga_ref, dbga_ref, dlam_ref):
                r[...] = jnp.zeros_like(r)

        first = it == 0
        x = x_ref[...]
        xs, xc, xcb, gx, ga, sp, a, mult = _lru_common(x, xp_ref[...], first, cw_ref, cb_ref, wgx_ref, bgx_ref, wga_ref, bga_ref, lam_ref, T)
        row = lax.broadcasted_iota(jnp.int32, x.shape, 0)
        u = gx * xc
        h = h_ref[...]
        hp = jnp.where(first, 0.0, hp_ref[...])
        hm1 = jnp.where(row < 1, jnp.tile(pltpu.roll(hp, 1, 0), (T // SUB, 1)), pltpu.roll(h, 1, 0))
        dho = dy_ref[...] * _silu(g_ref[...])
        gin = jnp.where(row == T - 1, dho + carry_ref[0:1, :], dho)
        abar = jnp.where(row == T - 1, 0.0, pltpu.roll(a, T - 1, 0))
        dh = _scan_bwd(abar, gin, T)
        carry_ref[...] = (a * dh)[0:SUB, :]
        da = dh * hm1
        dmult = dh * u
        du = dh * mult
        dgx = du * xc
        dxc = du * gx
        dlog_a = da * a - dmult * a * a / mult
        dga = dlog_a * (-LRU_C * sp)
        lam = lam_ref[...]
        dlam_ref[...] += _csum(dlog_a * (-LRU_C * ga)) * (-1.0 / (1.0 + jnp.exp(lam)))
        dpa = dga * ga * (1.0 - ga)
        dpx = dgx * gx * (1.0 - gx)
        dpab, dpxb = dpa.astype(BF16), dpx.astype(BF16)
        dxc = dxc + _nt(dpxb, wgx_ref[0]) + _nt(dpab, wga_ref[0])
        dwgx_ref[0] += _tn(xcb, dpxb)
        dwga_ref[0] += _tn(xcb, dpab)
        dbgx_ref[0] += _csum(dpx)
        dbga_ref[0] += _csum(dpa)
        dcb_ref[...] += _csum(dxc)
        for k in range(4):
            dcw_ref[k:k + 1, :] += _csum(dxc * xs[k])
        head = head_ref[...]
        dx = cw_ref[3:4, :] * dxc
        for jj in (1, 2, 3):
            hv = jnp.tile(pltpu.roll(head, SUB - jj, 0), (T // SUB, 1))
            dx = dx + cw_ref[3 - jj:4 - jj, :] * jnp.where(row >= T - jj, hv, pltpu.roll(dxc, T - jj, 0))
        head_ref[...] = dxc[0:SUB, :]
        dzx_ref[...] = dx.astype(BF16)

    def acc(shape, imap):
        return pl.BlockSpec(shape, imap)

    out_specs = [
        pl.BlockSpec((T, LANE), lambda n, i: (nT - 1 - i, C_LRUX + n)),
        acc((4, LANE), lambda n, i: (0, n)), acc((1, LANE), lambda n, i: (0, n)),
        acc((1, LANE, LANE), lambda n, i: (n, 0, 0)), acc((1, 1, LANE), lambda n, i: (n, 0, 0)),
        acc((1, LANE, LANE), lambda n, i: (n, 0, 0)), acc((1, 1, LANE), lambda n, i: (n, 0, 0)),
        acc((1, LANE), lambda n, i: (0, n)),
    ]
    out_shape = [
        jax.ShapeDtypeStruct(dz.shape, BF16),
        jax.ShapeDtypeStruct((4, D), F32), jax.ShapeDtypeStruct((1, D), F32),
        jax.ShapeDtypeStruct((8, LANE, LANE), F32), jax.ShapeDtypeStruct((8, 1, LANE), F32),
        jax.ShapeDtypeStruct((8, LANE, LANE), F32), jax.ShapeDtypeStruct((8, 1, LANE), F32),
        jax.ShapeDtypeStruct((1, D), F32),
    ]
    dyspec = pl.BlockSpec((T, LANE), lambda n, i: (nT - 1 - i, n))
    hprev = pl.BlockSpec((SUB, LANE), lambda n, i: (jnp.maximum((nT - 1 - i) * (T // SUB) - 1, 0), n))
    return pl.pallas_call(
        body, name="lru_bwd", grid=(8, nT),
        in_specs=[at(C_LRUX), prev(C_LRUX), at(C_LRUG), dyspec, hprev, dyspec] + small + [pl.BlockSpec(memory_space=pl.ANY)],
        out_specs=out_specs, out_shape=out_shape,
        scratch_shapes=[pltpu.VMEM((SUB, LANE), F32), pltpu.VMEM((SUB, LANE), F32)],
        input_output_aliases={13: 0},
        compiler_params=_cp(dimension_semantics=("parallel", "arbitrary")),
    )(zp, zp, zp, hs, hs, dy, w["conv_w"], w["conv_b"], w["w_gx"], w["b_gx"], w["w_ga"], w["b_ga"], w["lam"], dz)


def _lru_gate_bwd(zp, hs, dy, dz):
    S = zp.shape[0]
    T = T_ROW

    def body(g_ref, h_ref, dy_ref, dz_in, o_ref):
        del dz_in
        o_ref[...] = (dy_ref[...] * h_ref[...] * _dsilu(g_ref[...])).astype(BF16)

    row = pl.BlockSpec((T, D), lambda i: (i, 0))
    zc = pl.BlockSpec((T, D), lambda i: (i, C_LRUG // 8))
    return pl.pallas_call(
        body, name="lru_gate_bwd", grid=(S // T,), in_specs=[zc, row, row, pl.BlockSpec(memory_space=pl.ANY)], out_specs=zc,
        out_shape=jax.ShapeDtypeStruct(dz.shape, BF16), input_output_aliases={3: 0}, compiler_params=_cp(),
    )(zp, hs, dy, dz)


def _rope_tables(pos):
    pf = pos.astype(F32)[:, None]

    def cs(d):
        inv = ROPE_THETA ** (-jnp.arange(0, d, 2, dtype=F32) / d)
        ang = pf * inv
        return jnp.cos(ang), jnp.sin(ang)

    S = pos.shape[0]
    c, s = cs(32)
    one, zero = jnp.ones((S, 64), F32), jnp.zeros((S, 16), F32)
    z32, z64 = jnp.zeros((S, 32), F32), jnp.zeros((S, 64), F32)
    mla = (jnp.concatenate([one, c, c, jnp.ones((S, 32), F32)], 1),
           jnp.concatenate([z64, zero, s, z32], 1),
           jnp.concatenate([z64, -s, zero, z32], 1))
    c, s = cs(64)
    dil = (jnp.concatenate([c, c, c, c], 1),
           jnp.concatenate([z32, s, z32, s], 1),
           jnp.concatenate([-s, z32, -s, z32], 1))
    return mla, dil


def _rope(x, C, S1, S2, sh):
    return x * C + pltpu.roll(x, sh, 1) * S1 + pltpu.roll(x, LANE - sh, 1) * S2


def _rope_t(dy, C, S1, S2, sh):
    return dy * C + pltpu.roll(dy * S1, LANE - sh, 1) + pltpu.roll(dy * S2, sh, 1)


def _lane(shape):
    return lax.broadcasted_iota(jnp.int32, shape, 1)


T_MLA = 512
TA = 512


def _zcol(T, width, col_lanes):
    assert (col_lanes * LANE) % width == 0
    return pl.BlockSpec((T, width), lambda i: (i, col_lanes * LANE // width))


def _full(shape):
    return pl.BlockSpec(shape, lambda *_: (0,) * len(shape))


def _mla_pre_fwd(zp, w, tab):
    S = zp.shape[0]
    T = T_MLA

    def body(cq_ref, ckv_ref, kr_ref, gcq_ref, gckv_ref, wuq_ref, wuk_ref, wuv_ref, gq_ref, gk_ref, C_ref, S1_ref, S2_ref,
             q_ref, k_ref, v_ref):
        cq = cq_ref[...]
        cqn = (cq * lax.rsqrt(jnp.mean(cq * cq, axis=-1, keepdims=True) + EPS) * gcq_ref[...]).astype(BF16)
        ckv = ckv_ref[...]
        ckvn = (ckv * lax.rsqrt(jnp.mean(ckv * ckv, axis=-1, keepdims=True) + EPS) * gckv_ref[...]).astype(BF16)
        q0 = _nn(cqn, wuq_ref[...])
        k0 = _nn(ckvn, wuk_ref[...])
        krb = kr_ref[...]
        C, S1, S2 = C_ref[...], S1_ref[...], S2_ref[...]
        for h in range(8):
            sl = slice(h * LANE, (h + 1) * LANE)
            xq = q0[:, sl]
            xq = xq * lax.rsqrt(_rsum_mxu(xq * xq) * (1.0 / MLA_QK) + EPS) * gq_ref[...]
            q_ref[:, sl] = _rope(xq, C, S1, S2, 16).astype(BF16)
            xk = k0[:, sl] + krb
            xk = xk * lax.rsqrt(_rsum_mxu(xk * xk) * (1.0 / MLA_QK) + EPS) * gk_ref[...]
            k_ref[:, sl] = _rope(xk, C, S1, S2, 16).astype(BF16)
        v_ref[...] = _nn(ckvn, wuv_ref[...]).astype(BF16)

    tabspec = pl.BlockSpec((T, LANE), lambda i: (i, 0))
    in_specs = [_zcol(T, 256, C_CQ), _zcol(T, LANE, C_CKV), _zcol(T, LANE, C_KR), _full((1, 256)), _full((1, LANE)),
                _full((256, 1024)), _full((LANE, 1024)), _full((LANE, 512)), _full((1, LANE)), _full((1, LANE)),
                tabspec, tabspec, tabspec]
    return pl.pallas_call(
        body, name="mla_pre_fwd", grid=(S // T,), in_specs=in_specs,
        out_specs=[pl.BlockSpec((T, 1024), lambda i: (i, 0)), pl.BlockSpec((T, 1024), lambda i: (i, 0)), pl.BlockSpec((T, 512), lambda i: (i, 0))],
        out_shape=[jax.ShapeDtypeStruct((S, 1024), BF16), jax.ShapeDtypeStruct((S, 1024), BF16), jax.ShapeDtypeStruct((S, 512), BF16)],
        compiler_params=_cp(),
    )(zp, zp, zp, w["g_cq"], w["g_ckv"], w["w_uq"], w["w_uk"], w["w_uv"], w["g_mq"], w["g_mk"], *tab)


def _mla_attn_fwd(q, k, v, zp):
    S = q.shape[0]
    nq = S // TA

    def body(q_ref, k_ref, v_ref, g_ref, o_ref, lse_ref, y_ref):
        qi = pl.program_id(1)
        lane = _lane((TA, LANE))
        rowi = lax.broadcasted_iota(jnp.int32, (TA, TA), 0)
        coli = lax.broadcasted_iota(jnp.int32, (TA, TA), 1)
        o_tot = jnp.zeros((TA, LANE), F32)
        for hh in range(2):
            cs = slice(hh * LANE, (hh + 1) * LANE)
            hm = (lane < 64) if hh == 0 else (lane >= 64)
            qh = q_ref[:, cs]
            ones_lane = 64 if hh == 0 else 0

            def step(kb, carry, masked, cs=cs, hm=hm, qh=qh, ones_lane=ones_lane):
                m, acc = carry
                off = pl.multiple_of(kb * TA, TA)
                kh = k_ref[pl.ds(off, TA), cs]
                vv = v_ref[pl.ds(off, TA), :]
                vh = jnp.where(hm, vv, jnp.where(lane == ones_lane, jnp.ones_like(vv), jnp.zeros_like(vv)))
                s = _nt(qh, kh) * (MLA_SCALE * LOG2E)
                if masked:
                    s = jnp.where(rowi >= coli, s, NEG)
                m_new = jnp.maximum(m, jnp.max(s, axis=-1, keepdims=True))
                acc = jnp.exp2(m - m_new) * acc + _nn(jnp.exp2(s - m_new).astype(BF16), vh)
                return m_new, acc

            init = (jnp.full((TA, 1), NEG, F32), jnp.zeros((TA, LANE), F32))
            carry = lax.fori_loop(0, qi, lambda kb, c: step(kb, c, False), init)
            m, acc = step(qi, carry, True)
            l = _rsum(jnp.where(lane == ones_lane, acc, 0.0))
            o_tot = o_tot + jnp.where(hm, acc, 0.0) / l
            lse_ref[:, cs] = jnp.broadcast_to(m * (1.0 / LOG2E) + jnp.log(l), (TA, LANE))
        o_ref[...] = o_tot
        y_ref[...] = (o_tot * _silu(g_ref[...])).astype(BF16)

    blk = pl.BlockSpec((TA, LANE), lambda p, i: (i, p))
    return pl.pallas_call(
        body, name="mla_attn_fwd", grid=(4, nq),
        in_specs=[pl.BlockSpec((TA, 256), lambda p, i: (i, p)), pl.BlockSpec((S, 256), lambda p, i: (0, p)),
                  pl.BlockSpec((S, LANE), lambda p, i: (0, p)), pl.BlockSpec((TA, LANE), lambda p, i: (i, C_MLAG + p))],
        out_specs=[blk, pl.BlockSpec((TA, 256), lambda p, i: (i, p)), blk],
        out_shape=[jax.ShapeDtypeStruct((S, 512), F32), jax.ShapeDtypeStruct((S, 1024), F32), jax.ShapeDtypeStruct((S, 512), BF16)],
        compiler_params=_cp(dimension_semantics=("parallel", "arbitrary")),
    )(q, k, v, zp)


def _mla_post_bwd(zp, o, dy, dz):
    S = zp.shape[0]
    T = T_ROW

    def body(g_ref, o_ref, dy_ref, dz_in, dz_ref, do_ref, D_ref):
        del dz_in
        g, o_, dy_ = g_ref[...], o_ref[...], dy_ref[...]
        do = dy_ * _silu(g)
        do_ref[...] = do.astype(BF16)
        dz_ref[...] = (dy_ * o_ * _dsilu(g)).astype(BF16)
        prod = do * o_
        lane = _lane((T, LANE))
        for p in range(4):
            pr = prod[:, p * LANE:(p + 1) * LANE]
            da = _rsum(jnp.where(lane < 64, pr, 0.0))
            db = _rsum(jnp.where(lane >= 64, pr, 0.0))
            D_ref[:, 2 * p * LANE:(2 * p + 1) * LANE] = jnp.broadcast_to(da, (T, LANE))
            D_ref[:, (2 * p + 1) * LANE:(2 * p + 2) * LANE] = jnp.broadcast_to(db, (T, LANE))

    row = pl.BlockSpec((T, 512), lambda i: (i, 0))
    zc = _zcol(T, 512, C_MLAG)
    return pl.pallas_call(
        body, name="mla_post_bwd", grid=(S // T,), in_specs=[zc, row, row, pl.BlockSpec(memory_space=pl.ANY)],
        out_specs=[zc, row, pl.BlockSpec((T, 1024), lambda i: (i, 0))],
        out_shape=[jax.ShapeDtypeStruct(dz.shape, BF16), jax.ShapeDtypeStruct((S, 512), BF16), jax.ShapeDtypeStruct((S, 1024), F32)],
        input_output_aliases={3: 0}, compiler_params=_cp(),
    )(zp, o, dy, dz)


def _mla_attn_bwd(q, k, v, do, lse, Dr):
    S = q.shape[0]
    nq = S // TA

    def body(q_ref, do_ref, lse_ref, D_ref, k_ref, v_ref, dq_ref, dk_ref, dv_ref):
        ki = pl.program_id(1)

        @pl.when(ki == 0)
        def _():
            dq_ref[...] = jnp.zeros_like(dq_ref)

        lane = _lane((TA, LANE))
        rowi = lax.broadcasted_iota(jnp.int32, (TA, TA), 0)
        coli = lax.broadcasted_iota(jnp.int32, (TA, TA), 1)
        dv_tot = jnp.zeros((TA, LANE), F32)
        for hh in range(2):
            cs = slice(hh * LANE, (hh + 1) * LANE)
            hm = (lane < 64) if hh == 0 else (lane >= 64)
            kh = k_ref[:, cs]
            vv = v_ref[...]
            vm = jnp.where(hm, vv, jnp.zeros_like(vv))

            def step(qb, carry, masked, cs=cs, kh=kh, vm=vm):
                dk_acc, dv_acc = carry
                off = pl.multiple_of(qb * TA, TA)
                qh = q_ref[pl.ds(off, TA), cs]
                doh = do_ref[pl.ds(off, TA), :]
                ls = jnp.tile(lse_ref[pl.ds(off, TA), cs], (1, TA // LANE))
                dd = jnp.tile(D_ref[pl.ds(off, TA), cs], (1, TA // LANE))
                s = _nt(qh, kh) * MLA_SCALE
                if masked:
                    s = jnp.where(rowi >= coli, s, NEG)
                p = jnp.exp(s - ls)
                dp = _nt(doh, vm)
                ds = (p * (dp - dd) * MLA_SCALE).astype(BF16)
                dv_acc = dv_acc + _tn(p.astype(BF16), doh)
                dk_acc = dk_acc + _tn(ds, qh)
                dq_ref[pl.ds(off, TA), cs] += _nn(ds, kh)
                return dk_acc, dv_acc

            z = jnp.zeros((TA, LANE), F32)
            carry = step(ki, (z, z), True)
            dk_acc, dv_acc = lax.fori_loop(ki + 1, nq, lambda qb, c: step(qb, c, False), carry)
            dk_ref[:, cs] = dk_acc
            dv_tot = dv_tot + jnp.where(hm, dv_acc, 0.0)
        dv_ref[...] = dv_tot

    pair = pl.BlockSpec((S, 256), lambda p, i: (0, p))
    return pl.pallas_call(
        body, name="mla_attn_bwd", grid=(4, nq),
        in_specs=[pair, pl.BlockSpec((S, LANE), lambda p, i: (0, p)), pair, pair,
                  pl.BlockSpec((TA, 256), lambda p, i: (i, p)), pl.BlockSpec((TA, LANE), lambda p, i: (i, p))],
        out_specs=[pair, pl.BlockSpec((TA, 256), lambda p, i: (i, p)), pl.BlockSpec((TA, LANE), lambda p, i: (i, p))],
        out_shape=[jax.ShapeDtypeStruct((S, 1024), F32), jax.ShapeDtypeStruct((S, 1024), F32), jax.ShapeDtypeStruct((S, 512), F32)],
        compiler_params=_cp(dimension_semantics=("parallel", "arbitrary")),
    )(q, do, lse, Dr, k, v)


def _mla_pre_bwd(zp, dq, dk, dv, w, tab, dz):
    S = zp.shape[0]
    T = T_MLA

    def body(cq_ref, ckv_ref, kr_ref, dq_ref, dk_ref, dv_ref, gcq_ref, gckv_ref, wuq_ref, wuk_ref, wuv_ref, gq_ref, gk_ref,
             C_ref, S1_ref, S2_ref, dz_in, dz_ref, dwuq_ref, dwuk_ref, dwuv_ref, dgcq_ref, dgckv_ref, dgq_ref, dgk_ref):
        del dz_in
        i = pl.program_id(0)

        @pl.when(i == 0)
        def _():
            for r in (dwuq_ref, dwuk_ref, dwuv_ref, dgcq_ref, dgckv_ref, dgq_ref, dgk_ref):
                r[...] = jnp.zeros_like(r)

        cq = cq_ref[...]
        rq = lax.rsqrt(jnp.mean(cq * cq, axis=-1, keepdims=True) + EPS)
        cqh = cq * rq
        cqn = (cqh * gcq_ref[...]).astype(BF16)
        ckv = ckv_ref[...]
        rkv = lax.rsqrt(jnp.mean(ckv * ckv, axis=-1, keepdims=True) + EPS)
        ckvh = ckv * rkv
        ckvn = (ckvh * gckv_ref[...]).astype(BF16)
        q0 = _nn(cqn, wuq_ref[...])
        k0 = _nn(ckvn, wuk_ref[...])
        krb = kr_ref[...]
        C, S1, S2 = C_ref[...], S1_ref[...], S2_ref[...]
        gq, gk = gq_ref[...], gk_ref[...]

        def head_bwd(x, dy, g):
            r = lax.rsqrt(_rsum_mxu(x * x) * (1.0 / MLA_QK) + EPS)
            xn = x * r
            dyn = _rope_t(dy, C, S1, S2, 16)
            dxh = dyn * g
            return r * (dxh - xn * _rsum_mxu(dxh * xn) * (1.0 / MLA_QK)), _csum(dyn * xn)

        dq0, dk0 = [], []
        dgq_acc = jnp.zeros((1, LANE), F32)
        dgk_acc = jnp.zeros((1, LANE), F32)
        dkr = jnp.zeros((T, LANE), F32)
        for h in range(8):
            sl = slice(h * LANE, (h + 1) * LANE)
            dxq, gq_p = head_bwd(q0[:, sl], dq_ref[:, sl], gq)
            dxk, gk_p = head_bwd(k0[:, sl] + krb, dk_ref[:, sl], gk)
            dq0.append(dxq.astype(BF16))
            dk0.append(dxk.astype(BF16))
            dkr = dkr + dxk
            dgq_acc = dgq_acc + gq_p
            dgk_acc = dgk_acc + gk_p
        dgq_ref[...] += dgq_acc
        dgk_ref[...] += dgk_acc
        dq0 = jnp.concatenate(dq0, axis=1)
        dk0 = jnp.concatenate(dk0, axis=1)
        dvb = dv_ref[...].astype(BF16)
        dwuq_ref[...] += _tn(cqn, dq0)
        dwuk_ref[...] += _tn(ckvn, dk0)
        dwuv_ref[...] += _tn(ckvn, dvb)
        dcqn = _nt(dq0, wuq_ref[...])
        dckvn = _nt(dk0, wuk_ref[...]) + _nt(dvb, wuv_ref[...])
        dgcq_ref[...] += _csum(dcqn * cqh)
        dgckv_ref[...] += _csum(dckvn * ckvh)
        dxh = dcqn * gcq_ref[...]
        dz_ref[:, 0:256] = (rq * (dxh - cqh * jnp.mean(dxh * cqh, axis=-1, keepdims=True))).astype(BF16)
        dxh = dckvn * gckv_ref[...]
        dz_ref[:, 256:384] = (rkv * (dxh - ckvh * jnp.mean(dxh * ckvh, axis=-1, keepdims=True))).astype(BF16)
        lane = _lane((T, LANE))
        dz_ref[:, 384:512] = jnp.where((lane >= KR_LANE) & (lane < KR_LANE + 32), dkr, 0.0).astype(BF16)

    tabspec = pl.BlockSpec((T, LANE), lambda i: (i, 0))
    in_specs = [_zcol(T, 256, C_CQ), _zcol(T, LANE, C_CKV), _zcol(T, LANE, C_KR),
                pl.BlockSpec((T, 1024), lambda i: (i, 0)), pl.BlockSpec((T, 1024), lambda i: (i, 0)), pl.BlockSpec((T, 512), lambda i: (i, 0)),
                _full((1, 256)), _full((1, LANE)), _full((256, 1024)), _full((LANE, 1024)), _full((LANE, 512)), _full((1, LANE)), _full((1, LANE)),
                tabspec, tabspec, tabspec, pl.BlockSpec(memory_space=pl.ANY)]
    out_specs = [_zcol(T, 512, C_CQ), _full((256, 1024)), _full((LANE, 1024)), _full((LANE, 512)), _full((1, 256)), _full((1, LANE)),
                 _full((1, LANE)), _full((1, LANE))]
    out_shape = [jax.ShapeDtypeStruct(dz.shape, BF16), jax.ShapeDtypeStruct((256, 1024), F32), jax.ShapeDtypeStruct((LANE, 1024), F32),
                 jax.ShapeDtypeStruct((LANE, 512), F32), jax.ShapeDtypeStruct((1, 256), F32), jax.ShapeDtypeStruct((1, LANE), F32),
                 jax.ShapeDtypeStruct((1, LANE), F32), jax.ShapeDtypeStruct((1, LANE), F32)]
    return pl.pallas_call(
        body, name="mla_pre_bwd", grid=(S // T,), in_specs=in_specs, out_specs=out_specs, out_shape=out_shape,
        input_output_aliases={16: 0}, compiler_params=_cp(),
    )(zp, zp, zp, dq, dk, dv, w["g_cq"], w["g_ckv"], w["w_uq"], w["w_uk"], w["w_uv"], w["g_mq"], w["g_mk"], *tab, dz)


T_DIL = 512


def _head_stats(x, lane):
    sq = x * x
    sa = _rsum(jnp.where(lane < 64, sq, 0.0))
    sb = _rsum(jnp.where(lane >= 64, sq, 0.0))
    return lax.rsqrt(jnp.where(lane < 64, sa, sb) * (1.0 / DIL_HD) + EPS)


def _head_sum(x, lane):
    sa = _rsum(jnp.where(lane < 64, x, 0.0))
    sb = _rsum(jnp.where(lane >= 64, x, 0.0))
    return jnp.where(lane < 64, sa, sb)


def _head_stats_mxu(x):
    r = lax.broadcasted_iota(jnp.int32, (LANE, LANE), 0)
    c = lax.broadcasted_iota(jnp.int32, (LANE, LANE), 1)
    ones = jnp.where((r < 64) == (c < 64), 1.0, 0.0).astype(F32)
    ss = lax.dot_general(x * x, ones, (((1,), (0,)), ((), ())), precision=lax.Precision.HIGHEST, preferred_element_type=F32)
    return lax.rsqrt(ss * (1.0 / DIL_HD) + EPS)


def _dil_pre_fwd(zp, w, tab):
    S = zp.shape[0]
    T = T_DIL

    def body(q_ref, k_ref, gq_ref, gk_ref, C_ref, S1_ref, S2_ref, qo_ref, ko_ref):
        C, S1, S2 = C_ref[...], S1_ref[...], S2_ref[...]
        for b in range(12):
            sl = slice(b * LANE, (b + 1) * LANE)
            x = q_ref[:, sl]
            qo_ref[:, sl] = _rope(x * _head_stats_mxu(x) * gq_ref[...], C, S1, S2, 32)
            x = k_ref[:, sl]
            ko_ref[:, sl] = _rope(x * _head_stats_mxu(x) * gk_ref[...], C, S1, S2, 32)

    tabspec = pl.BlockSpec((T, LANE), lambda i: (i, 0))
    out = pl.BlockSpec((T, 1536), lambda i: (i, 0))
    return pl.pallas_call(
        body, name="dil_pre_fwd", grid=(S // T,),
        in_specs=[_zcol(T, 1536, C_DQ), _zcol(T, 1536, C_DK), _full((1, LANE)), _full((1, LANE)), tabspec, tabspec, tabspec],
        out_specs=[out, out], out_shape=[jax.ShapeDtypeStruct((S, 1536), F32)] * 2, compiler_params=_cp(),
    )(zp, zp, w["g_dq"], w["g_dk"], *tab)


DIL_ROWS = 2048


def _dil_geometry(g, S):
    d = DIL_DILATIONS[g]
    P = NK * d
    return d, P, DIL_ROWS // P, S // P


def _dil_rows(start, d, blocks=1):
    return pl.ds(pl.multiple_of(start, NK), blocks * NK) if d == 1 else pl.ds(start, blocks * NK, stride=d)


def _dil_specs(g, S, col0):
    _, P, m, nb = _dil_geometry(g, S)
    cur = pl.BlockSpec((DIL_ROWS, LANE), lambda sb, c: (sb, col0 + c))
    prv = pl.BlockSpec((P, LANE), lambda sb, c: (jnp.maximum(sb * m - 1, 0), col0 + c))
    nxt = pl.BlockSpec((P, LANE), lambda sb, c: (jnp.minimum((sb + 1) * m, nb - 1), col0 + c))
    return cur, prv, nxt


def _dil_attn_fwd(q, k, zp, g):
    S = q.shape[0]
    d, P, m, nb = _dil_geometry(g, S)
    R = DIL_ROWS

    def body(q_ref, kc_ref, kp_ref, vc_ref, vp_ref, o_ref, lse_ref, *scr):
        sb = pl.program_id(0)
        if m > 1:
            ks_ref, vs_ref = scr
            ks_ref[0:P, :] = kp_ref[...]
            ks_ref[P:P + R, :] = kc_ref[...]
            vs_ref[0:P, :] = vp_ref[...]
            vs_ref[P:P + R, :] = vc_ref[...]
        lane = _lane((NK, LANE))

        def unit(u, carry):
            j = u // d
            start = j * P + (u - j * d)
            rows = _dil_rows(start, d)
            if m > 1:
                k2, v2 = ks_ref[_dil_rows(start, d, 2), :], vs_ref[_dil_rows(start, d, 2), :]
            else:
                k2 = jnp.concatenate([kp_ref[rows, :], kc_ref[rows, :]], axis=0)
                v2 = jnp.concatenate([vp_ref[rows, :], vc_ref[rows, :]], axis=0)
            k2, v2 = k2.astype(BF16), v2.astype(BF16)
            q_ = q_ref[rows, :].astype(BF16)
            row = lax.broadcasted_iota(jnp.int32, (NK, 2 * NK), 0)
            col = lax.broadcasted_iota(jnp.int32, (NK, 2 * NK), 1)
            band = (col >= row) & (col <= row + NK) & ((col >= NK) | (sb * m + j > 0))
            lane2 = _lane((2 * NK, LANE))
            zb, zv = jnp.zeros_like(q_), jnp.zeros_like(v2)
            o_tot = jnp.zeros((NK, LANE), F32)
            lse_tot = jnp.zeros((NK, LANE), F32)
            for hh in range(2):
                hm = (lane < 64) if hh == 0 else (lane >= 64)
                hm2 = (lane2 < 64) if hh == 0 else (lane2 >= 64)
                s_ = jnp.where(band, _nt(jnp.where(hm, q_, zb), k2) * DIL_SCALE, NEG)
                mx = jnp.max(s_, axis=-1, keepdims=True)
                e = jnp.exp(s_ - mx)
                den = _rsum(e)
                o_tot = o_tot + _nn(e.astype(BF16), jnp.where(hm2, v2, zv)) / den
                lse_tot = jnp.where(hm, mx + jnp.log(den), lse_tot)
            o_ref[rows, :] = o_tot
            lse_ref[rows, :] = lse_tot
            return carry

        lax.fori_loop(0, R // NK, unit, 0, unroll=8)

    qcur, qprv, _ = _dil_specs(g, S, 4 * g)
    vcur, vprv, _ = _dil_specs(g, S, C_DV + 4 * g)
    out = pl.BlockSpec((R, LANE), lambda sb, c: (sb, c))
    return pl.pallas_call(
        body, name=f"dil_attn_fwd{g}", grid=(S // R, 4), in_specs=[qcur, qcur, qprv, vcur, vprv], out_specs=[out, out],
        out_shape=[jax.ShapeDtypeStruct((S, 512), F32)] * 2,
        scratch_shapes=[pltpu.VMEM((P + R, LANE), F32)] * 2 if m > 1 else [], compiler_params=_cp(),
    )(q, k, k, zp, zp)


def _dil_combine(os_, ls_, zp):
    S = zp.shape[0]
    T = T_ROW

    def body(o0, o1, o2, l0, l1, l2, g_ref, oc_ref, L_ref, y_ref):
        a, b, c = l0[...], l1[...], l2[...]
        mx = jnp.maximum(jnp.maximum(a, b), c)
        ea, eb, ec = jnp.exp(a - mx), jnp.exp(b - mx), jnp.exp(c - mx)
        den = ea + eb + ec
        oc = (ea * o0[...] + eb * o1[...] + ec * o2[...]) / den
        oc_ref[...] = oc
        L_ref[...] = mx + jnp.log(den)
        y_ref[...] = (oc * _silu(g_ref[...])).astype(BF16)

    row = pl.BlockSpec((T, 512), lambda i: (i, 0))
    return pl.pallas_call(
        body, name="dil_combine", grid=(S // T,), in_specs=[row] * 6 + [_zcol(T, 512, C_DILG)], out_specs=[row, row, row],
        out_shape=[jax.ShapeDtypeStruct((S, 512), F32), jax.ShapeDtypeStruct((S, 512), F32), jax.ShapeDtypeStruct((S, 512), BF16)],
        compiler_params=_cp(),
    )(*os_, *ls_, zp)


def _dil_comb_bwd(zp, oc, dy, dz):
    S = zp.shape[0]
    T = T_ROW

    def body(g_ref, o_ref, dy_ref, dz_in, dz_ref, do_ref, D_ref):
        del dz_in
        g, o_, dy_ = g_ref[...], o_ref[...], dy_ref[...]
        do = dy_ * _silu(g)
        do_ref[...] = do
        dz_ref[...] = (dy_ * o_ * _dsilu(g)).astype(BF16)
        lane = _lane((T, LANE))
        for p in range(4):
            sl = slice(p * LANE, (p + 1) * LANE)
            D_ref[:, sl] = _head_sum(do[:, sl] * o_[:, sl], lane)

    row = pl.BlockSpec((T, 512), lambda i: (i, 0))
    zc = _zcol(T, 512, C_DILG)
    return pl.pallas_call(
        body, name="dil_comb_bwd", grid=(S // T,), in_specs=[zc, row, row, pl.BlockSpec(memory_space=pl.ANY)], out_specs=[zc, row, row],
        out_shape=[jax.ShapeDtypeStruct(dz.shape, BF16), jax.ShapeDtypeStruct((S, 512), F32), jax.ShapeDtypeStruct((S, 512), F32)],
        input_output_aliases={3: 0}, compiler_params=_cp(),
    )(zp, oc, dy, dz)


def _dil_attn_bwd(q, k, zp, do, L, Dr, g):
    S = q.shape[0]
    d, P, m, nb = _dil_geometry(g, S)
    R = DIL_ROWS
    n_q, n_k = 4, 2

    def body(*refs):
        q_side = refs[0:2 * n_q]
        k_side = refs[2 * n_q:2 * n_q + 2 * n_k]
        dq_ref, dk_ref, dv_ref = refs[2 * n_q + 2 * n_k:2 * n_q + 2 * n_k + 3]
        scr = refs[2 * n_q + 2 * n_k + 3:]
        sb = pl.program_id(0)
        if m > 1:
            for a in range(n_q):
                scr[a][0:R, :] = q_side[2 * a][...]
                scr[a][R:R + P, :] = q_side[2 * a + 1][...]
            for a in range(n_k):
                scr[n_q + a][0:P, :] = k_side[2 * a + 1][...]
                scr[n_q + a][P:P + R, :] = k_side[2 * a][...]
        lane = _lane((NK, LANE))

        def unit(u, carry):
            j = u // d
            start = j * P + (u - j * d)
            rows = _dil_rows(start, d)
            if m > 1:
                rows_b = _dil_rows(start + P, d)
                q2, do2, L2, D2 = [scr[a][_dil_rows(start, d, 2), :] for a in range(n_q)]
                kp, vp = [scr[n_q + a][rows, :] for a in range(n_k)]
                kc, vc = [scr[n_q + a][rows_b, :] for a in range(n_k)]
            else:
                q2, do2, L2, D2 = [jnp.concatenate([q_side[2 * a][rows, :], q_side[2 * a + 1][rows, :]], axis=0) for a in range(n_q)]
                kc, vc = [k_side[2 * a][rows, :] for a in range(n_k)]
                kp, vp = [k_side[2 * a + 1][rows, :] for a in range(n_k)]
            q2, do2 = q2.astype(BF16), do2.astype(BF16)
            kc, kp, vc, vp = kc.astype(BF16), kp.astype(BF16), vc.astype(BF16), vp.astype(BF16)
            n = sb * m + j
            hA = _lane((2 * NK, LANE)) < 64
            zq = jnp.zeros_like(q2)
            L2r, D2r = pltpu.roll(L2, 64, 1), pltpu.roll(D2, 64, 1)
            Q4 = jnp.concatenate([jnp.where(hA, q2, zq), jnp.where(hA, zq, q2)], axis=0)
            O4 = jnp.concatenate([jnp.where(hA, do2, zq), jnp.where(hA, zq, do2)], axis=0)
            L4 = jnp.concatenate([jnp.where(hA, L2, L2r), jnp.where(hA, L2r, L2)], axis=0)
            D4 = jnp.concatenate([jnp.where(hA, D2, D2r), jnp.where(hA, D2r, D2)], axis=0)
            row4 = lax.broadcasted_iota(jnp.int32, (4 * NK, NK), 0) & (2 * NK - 1)
            col4 = lax.broadcasted_iota(jnp.int32, (4 * NK, NK), 1)
            m4 = ((row4 < NK) & (col4 <= row4)) | ((row4 >= NK) & (col4 >= row4 - NK) & (n < nb - 1))
            p4 = jnp.exp(jnp.where(m4, _nt(Q4, kc) * DIL_SCALE, NEG) - L4)
            ds4 = (p4 * (_nt(O4, vc) - D4) * DIL_SCALE).astype(BF16)
            dk_tot = _tn(ds4, Q4)
            dv_tot = _tn(p4.astype(BF16), O4)
            pick = lambda x: jnp.concatenate([x[0:NK], x[2 * NK:3 * NK]], axis=0)
            Qn, On, Ln, Dn = pick(Q4), pick(O4), pick(L4), pick(D4)
            rowp = lax.broadcasted_iota(jnp.int32, (2 * NK, NK), 0) & (NK - 1)
            colp = lax.broadcasted_iota(jnp.int32, (2 * NK, NK), 1)
            pp = jnp.exp(jnp.where((colp >= rowp) & (n > 0), _nt(Qn, kp) * DIL_SCALE, NEG) - Ln)
            dsp = (pp * (_nt(On, vp) - Dn) * DIL_SCALE).astype(BF16)
            dq2 = _nn(pick(ds4), kc) + _nn(dsp, kp)
            dq_tot = jnp.where(lane < 64, dq2[0:NK], dq2[NK:2 * NK])
            dq_ref[rows, :] = dq_tot
            dk_ref[rows, :] = dk_tot
            dv_ref[rows, :] = dv_tot
            return carry

        lax.fori_loop(0, R // NK, unit, 0, unroll=8)

    qcur, qprv, qnxt = _dil_specs(g, S, 4 * g)
    vcur, vprv, _ = _dil_specs(g, S, C_DV + 4 * g)
    ocur, _, onxt = _dil_specs(g, S, 0)
    out = pl.BlockSpec((R, LANE), lambda sb, c: (sb, c))
    scratch = [pltpu.VMEM((P + R, LANE), F32)] * (n_q + n_k) if m > 1 else []
    return pl.pallas_call(
        body, name=f"dil_attn_bwd{g}", grid=(S // R, 4),
        in_specs=[qcur, qnxt, ocur, onxt, ocur, onxt, ocur, onxt, qcur, qprv, vcur, vprv],
        out_specs=[out, out, out], out_shape=[jax.ShapeDtypeStruct((S, 512), F32)] * 3, scratch_shapes=scratch, compiler_params=_cp(),
    )(q, q, do, do, L, L, Dr, Dr, k, k, zp, zp)


def _dil_pre_bwd(zp, dys, g, tab, dz, col, name):
    S = zp.shape[0]
    T = T_DIL

    def body(x_ref, dy0_ref, dy1_ref, dy2_ref, g_ref, C_ref, S1_ref, S2_ref, dz_in, dz_ref, dg_ref):
        del dz_in
        i = pl.program_id(0)
        C, S1, S2 = C_ref[...], S1_ref[...], S2_ref[...]
        lane = _lane((T, LANE))
        gv = g_ref[...]
        acc = jnp.zeros((1, LANE), F32)
        for b in range(12):
            sl = slice(b * LANE, (b + 1) * LANE)
            x = x_ref[:, sl]
            r = _head_stats(x, lane)
            xn = x * r
            dy_ref = (dy0_ref, dy1_ref, dy2_ref)[b // 4]
            dyn = _rope_t(dy_ref[:, (b % 4) * LANE:(b % 4 + 1) * LANE], C, S1, S2, 32)
            acc = acc + _csum(dyn * xn)
            dxh = dyn * gv
            dz_ref[:, sl] = (r * (dxh - xn * _head_sum(dxh * xn, lane) * (1.0 / DIL_HD))).astype(BF16)

        @pl.when(i == 0)
        def _():
            dg_ref[...] = acc

        @pl.when(i > 0)
        def _():
            dg_ref[...] += acc

    tabspec = pl.BlockSpec((T, LANE), lambda i: (i, 0))
    zc = _zcol(T, 1536, col)
    grp = pl.BlockSpec((T, 512), lambda i: (i, 0))
    return pl.pallas_call(
        body, name=name, grid=(S // T,),
        in_specs=[zc, grp, grp, grp, _full((1, LANE)), tabspec, tabspec, tabspec, pl.BlockSpec(memory_space=pl.ANY)],
        out_specs=[zc, _full((1, LANE))], out_shape=[jax.ShapeDtypeStruct(dz.shape, BF16), jax.ShapeDtypeStruct((1, LANE), F32)],
        input_output_aliases={8: 0}, compiler_params=_cp(),
    )(zp, *dys, g, *tab, dz)


def _dil_dv_into(dvs, dz):
    S = dz.shape[0]
    T = T_ROW

    def body(s0, s1, s2, dz_in, o_ref):
        del dz_in
        for gi, s in enumerate((s0, s1, s2)):
            o_ref[:, gi * 512:(gi + 1) * 512] = s[...].astype(BF16)

    grp = pl.BlockSpec((T, 512), lambda i: (i, 0))
    return pl.pallas_call(
        body, name="dil_dv", grid=(S // T,), in_specs=[grp, grp, grp, pl.BlockSpec(memory_space=pl.ANY)],
        out_specs=_zcol(T, 1536, C_DV), out_shape=jax.ShapeDtypeStruct(dz.shape, BF16), input_output_aliases={3: 0}, compiler_params=_cp(),
    )(*dvs, dz)


T_MRG = 256


def _merge_fwd(P, zp, b_merge):
    S = zp.shape[0]
    T = T_MRG

    def body(p0, p1, p2, m0, m1, m2, b_ref, o_ref):
        acc = jnp.zeros((T, D), F32)
        for j, (p, m) in enumerate(((p0, m0), (p1, m1), (p2, m2))):
            acc = acc + _sig(m[...] + b_ref[:, j * D:(j + 1) * D]) * p[...].astype(F32)
        o_ref[...] = acc.astype(BF16)

    row = pl.BlockSpec((T, D), lambda i: (i, 0))
    return pl.pallas_call(
        body, name="merge_fwd", grid=(S // T,),
        in_specs=[row, row, row] + [_zcol(T, D, C_MERGE + 8 * j) for j in range(3)] + [_full((1, 3 * D))], out_specs=row,
        out_shape=jax.ShapeDtypeStruct((S, D), BF16), compiler_params=_cp(),
    )(*P, zp, zp, zp, b_merge)


def _merge_bwd(dm, Pj, zp, bj, dz, j):
    S = zp.shape[0]
    T = T_MRG

    def body(dm_ref, p_ref, m_ref, b_ref, dz_in, dz_ref, dp_ref, db_ref):
        del dz_in
        i = pl.program_id(0)
        g = _sig(m_ref[...] + b_ref[...])
        dmv = dm_ref[...].astype(F32)
        dp_ref[...] = (dmv * g).astype(BF16)
        dg = dmv * p_ref[...].astype(F32) * g * (1.0 - g)
        dz_ref[...] = dg.astype(BF16)
        part = _csum(dg)

        @pl.when(i == 0)
        def _():
            db_ref[...] = part

        @pl.when(i > 0)
        def _():
            db_ref[...] += part

    row = pl.BlockSpec((T, D), lambda i: (i, 0))
    zc = _zcol(T, D, C_MERGE + 8 * j)
    return pl.pallas_call(
        body, name=f"merge_bwd{j}", grid=(S // T,), in_specs=[row, row, zc, _full((1, D)), pl.BlockSpec(memory_space=pl.ANY)],
        out_specs=[zc, row, _full((1, D))],
        out_shape=[jax.ShapeDtypeStruct(dz.shape, BF16), jax.ShapeDtypeStruct((S, D), BF16), jax.ShapeDtypeStruct((1, D), F32)],
        input_output_aliases={4: 0}, compiler_params=_cp(),
    )(dm, Pj, zp, bj, dz)


def _loss_fwd_bwd(y, target):
    S = y.shape[0]
    T = T_ROW

    def body(y_ref, t_ref, loss_ref, dy_ref):
        i = pl.program_id(0)
        err = y_ref[...] - t_ref[...]
        dy_ref[...] = err * (1.0 / D)
        part = jnp.sum(err * err, keepdims=True).reshape(1, 1) * (0.5 / D)

        @pl.when(i == 0)
        def _():
            loss_ref[...] = part

        @pl.when(i > 0)
        def _():
            loss_ref[...] += part

    row = pl.BlockSpec((T, D), lambda i: (i, 0))
    return pl.pallas_call(
        body, name="loss", grid=(S // T,), in_specs=[row, row], out_specs=[_full((1, 1)), row],
        out_shape=[jax.ShapeDtypeStruct((1, 1), F32), jax.ShapeDtypeStruct((S, D), F32)], compiler_params=_cp(),
    )(y, target)


def _layer_fwd(x, w, tabs):
    mla_tab, dil_tab = tabs
    S = x.shape[0]
    h = _rms_in_fwd(x, w["norm_g"])
    zp = _mm(h, w["w_in"], mode="nn", name="in_proj")
    hs, y_lru = _lru_fwd(zp, w)
    q, k, v = _mla_pre_fwd(zp, w, mla_tab)
    o_mla, lse, y_mla = _mla_attn_fwd(q, k, v, zp)
    qd, kd = _dil_pre_fwd(zp, w, dil_tab)
    og, lg = zip(*[_dil_attn_fwd(qd, kd, zp, g) for g in range(len(DIL_DILATIONS))])
    oc, L, y_dil = _dil_combine(og, lg, zp)
    P = [_mm(y_lru, w["w_lru_o"], mode="nn", name="lru_out", out_dtype=BF16),
         _mm(y_mla, w["w_mla_o"], mode="nn", name="mla_out", out_dtype=BF16),
         _mm(y_dil, w["w_dil_o"], mode="nn", name="dil_out", out_dtype=BF16)]
    merged = _merge_fwd(P, zp, w["b_merge"])
    x_out = _mm(merged, w["w_out"], mode="nn", name="out_proj", add=x)
    saved = dict(x=x, h=h, zp=zp, hs=hs, y=(y_lru, y_mla, y_dil), q=q, k=k, v=v, o_mla=o_mla, lse=lse, qd=qd, kd=kd, oc=oc, L=L, P=P,
                 merged=merged)
    return x_out, saved


def _layer_bwd(dout, w, tabs, sv, hook=None, after=None):
    mla_tab, dil_tab = tabs
    zp = sv["zp"]
    S = zp.shape[0]
    g = {}
    dm = _mm(dout, w["w_out"], mode="nt", name="d_merged", after=after, out_dtype=BF16)
    g["w_out"] = _mm(sv["merged"], dout, mode="tn", name="dw_out", out_dtype=BF16)
    dz = lax.empty((S, ZW), BF16)
    dP, db = [], []
    for j in range(3):
        dz, dpj, dbj = _merge_bwd(dm, sv["P"][j], zp, w["b_merge"][:, j * D:(j + 1) * D], dz, j)
        dP.append(dpj)
        db.append(dbj)
    g["b_merge"] = jnp.concatenate(db, axis=1)
    names = ("w_lru_o", "w_mla_o", "w_dil_o")
    dy = []
    for j in range(3):
        dy.append(_mm(dP[j], w[names[j]], mode="nt", name="dy_" + names[j]))
        g[names[j]] = _mm(sv["y"][j], dP[j], mode="tn", name="d" + names[j], out_dtype=BF16)
    dz = _lru_gate_bwd(zp, sv["hs"], dy[0], dz)
    dz, g["conv_w"], g["conv_b"], g["w_gx"], g["b_gx"], g["w_ga"], g["b_ga"], g["lam"] = _lru_bwd(zp, sv["hs"], dy[0], w, dz)
    dz, do, Dr = _mla_post_bwd(zp, sv["o_mla"], dy[1], dz)
    dq, dk, dv = _mla_attn_bwd(sv["q"], sv["k"], sv["v"], do, sv["lse"], Dr)
    dz, g["w_uq"], g["w_uk"], g["w_uv"], g["g_cq"], g["g_ckv"], g["g_mq"], g["g_mk"] = _mla_pre_bwd(zp, dq, dk, dv, w, mla_tab, dz)
    dz, dod, Dd = _dil_comb_bwd(zp, sv["oc"], dy[2], dz)
    dqs, dks, dvs = zip(*[_dil_attn_bwd(sv["qd"], sv["kd"], zp, dod, sv["L"], Dd, gi) for gi in range(len(DIL_DILATIONS))])
    dz, g["g_dq"] = _dil_pre_bwd(zp, dqs, w["g_dq"], dil_tab, dz, C_DQ, "dil_pre_bwd_q")
    dz, g["g_dk"] = _dil_pre_bwd(zp, dks, w["g_dk"], dil_tab, dz, C_DK, "dil_pre_bwd_k")
    dz = _dil_dv_into(dvs, dz)
    g["w_in"] = _mm(sv["h"], dz, mode="tn", name="dw_in", out_dtype=BF16, tk=S)
    token = hook(g) if hook is not None else None
    dh = _mm(dz, w["w_in"], mode="nt", name="d_h", after=token, tk=ZW // 4)
    dx, g["norm_g"] = _rms_in_bwd(sv["x"], w["norm_g"], dh, dout)
    return dx, g


def _peers():
    mx, my, mc = lax.axis_index("x"), lax.axis_index("y"), lax.axis_index("c")
    me = 4 * mx + 2 * my + mc
    out = []
    for k in range(1, N_DEV):
        px = 1 - mx if k & 4 else mx
        py = 1 - my if k & 2 else my
        pc = 1 - mc if k & 1 else mc
        out.append(((px, py, pc), 4 * px + 2 * py + pc))
    return me, out


def _whole(ref, p):
    del p
    return ref


def _exchange(srcs, slicers, slices, name):
    n = len(srcs)

    def body(*refs):
        ins, outs = refs[:n], refs[n:2 * n]
        send_sems, recv_sems, local_sems = refs[2 * n:]
        me, peers = _peers()
        mine = [pltpu.make_async_copy(slicers[a](ins[a], me), outs[a].at[me], local_sems.at[a]) for a in range(n)]
        for cp in mine:
            cp.start()
        copies = []
        for k, (peer, pidx) in enumerate(peers):
            for a in range(n):
                cp = pltpu.make_async_remote_copy(
                    src_ref=slicers[a](ins[a], pidx), dst_ref=outs[a].at[me], send_sem=send_sems.at[k * n + a],
                    recv_sem=recv_sems.at[k * n + a], device_id=peer, device_id_type=pl.DeviceIdType.MESH)
                cp.start()
                copies.append(cp)
        for cp in copies + mine:
            cp.wait()

    nsem = (N_DEV - 1) * n
    return pl.pallas_call(
        body, name=name, out_shape=[jax.ShapeDtypeStruct((N_DEV,) + shp, dt) for shp, dt in slices],
        in_specs=[pl.BlockSpec(memory_space=pl.ANY)] * n, out_specs=[pl.BlockSpec(memory_space=pl.ANY)] * n,
        scratch_shapes=[pltpu.SemaphoreType.DMA((nsem,)), pltpu.SemaphoreType.DMA((nsem,)), pltpu.SemaphoreType.DMA((n,))],
        compiler_params=pltpu.CompilerParams(has_side_effects=True),
    )(*srcs)


def _gather_two_level(srcs, name):
    n = len(srcs)

    def body(*refs):
        ins, outs = refs[:n], refs[n:2 * n]
        send_sems, recv_sems, local_sems = refs[2 * n:]
        mx, my, mc = lax.axis_index("x"), lax.axis_index("y"), lax.axis_index("c")
        me, sibling = (mx, my, mc), (mx, my, 1 - mc)
        chips = [(1 - mx, my), (mx, 1 - my), (1 - mx, 1 - my)]
        slot = lambda d: 4 * d[0] + 2 * d[1] + d[2]

        def copy(j, a, block, to, own=False):
            return pltpu.make_async_remote_copy(
                src_ref=ins[a] if own else outs[a].at[slot(block)], dst_ref=outs[a].at[slot(block)],
                send_sem=send_sems.at[j * n + a], recv_sem=recv_sems.at[j * n + a], device_id=to, device_id_type=pl.DeviceIdType.MESH)

        mine = [pltpu.make_async_copy(ins[a], outs[a].at[slot(me)], local_sems.at[a]) for a in range(n)]
        first = [copy(1 + j, a, me, (*chip, mc), own=True) for j, chip in enumerate(chips) for a in range(n)]
        first += [copy(0, a, me, sibling, own=True) for a in range(n)]
        for cp in mine + first:
            cp.start()
        passed = []
        for j, chip in enumerate(chips):
            for a in range(n):
                copy(1 + j, a, (*chip, mc), me).wait_recv()
                cp = copy(4 + j, a, (*chip, mc), sibling)
                cp.start()
                passed.append(cp)
        for a in range(n):
            copy(0, a, sibling, me).wait_recv()
        for j, chip in enumerate(chips):
            for a in range(n):
                copy(4 + j, a, (*chip, 1 - mc), me).wait_recv()
        for cp in first + passed:
            cp.wait_send()
        for cp in mine:
            cp.wait()

    nsem = (N_DEV - 1) * n
    return pl.pallas_call(
        body, name=name, out_shape=[jax.ShapeDtypeStruct((N_DEV,) + a.shape, a.dtype) for a in srcs],
        in_specs=[pl.BlockSpec(memory_space=pl.ANY)] * n, out_specs=[pl.BlockSpec(memory_space=pl.ANY)] * n,
        scratch_shapes=[pltpu.SemaphoreType.DMA((nsem,)), pltpu.SemaphoreType.DMA((nsem,)), pltpu.SemaphoreType.DMA((n,))],
        compiler_params=pltpu.CompilerParams(has_side_effects=True),
    )(*srcs)


_HBM = pl.BlockSpec(memory_space=pltpu.HBM)
_SEM = pl.BlockSpec(memory_space=pltpu.SEMAPHORE)
_DATAFLOW = pltpu.SideEffectType.DATAFLOW_SIDE_EFFECTING


def _plan_chips():
    mx, my, mc = lax.axis_index("x"), lax.axis_index("y"), lax.axis_index("c")
    return 2 * mx + my, [((cx, cy, mc), 2 * cx + cy) for cx, cy in ((1 - mx, my), (mx, 1 - my), (1 - mx, 1 - my))]


def _pair_exchange(srcs, slicers, slices, sliced, name):
    n = len(srcs)
    pieces = [4 if s else 1 for s in sliced]

    def body(*refs):
        ins, outs = refs[:n], refs[n:2 * n]
        send_sems, recv_sems = refs[2 * n:]
        mx, my, mc = lax.axis_index("x"), lax.axis_index("y"), lax.axis_index("c")
        copies = []
        for a in range(n):
            for q in range(pieces[a]):
                i = len(copies)
                copies.append(pltpu.make_async_remote_copy(
                    src_ref=slicers[a](ins[a], 2 * q + 1 - mc) if sliced[a] else ins[a], dst_ref=outs[a].at[q],
                    send_sem=send_sems.at[i], recv_sem=recv_sems.at[i], device_id=(mx, my, 1 - mc), device_id_type=pl.DeviceIdType.MESH))
        for cp in copies:
            cp.start()
        for cp in copies:
            cp.wait()

    return pl.pallas_call(
        body, name=name, out_shape=[jax.ShapeDtypeStruct((p,) + shp, dt) for (shp, dt), p in zip(slices, pieces)],
        in_specs=[pl.BlockSpec(memory_space=pl.ANY)] * n, out_specs=[pl.BlockSpec(memory_space=pl.ANY)] * n,
        scratch_shapes=[pltpu.SemaphoreType.DMA((sum(pieces),)), pltpu.SemaphoreType.DMA((sum(pieces),))],
        compiler_params=pltpu.CompilerParams(has_side_effects=True),
    )(*srcs)


def _pair_add(src, came, first_blk, axis, name):
    _, r, c = came.shape
    nblk = (c if axis == 1 else r) // LANE
    if axis == 1:
        s_spec = pl.BlockSpec((r, LANE), lambda q, j, fb: (0, fb[q] + j))
        o_spec = pl.BlockSpec((1, r, LANE), lambda q, j, fb: (q, 0, j))
    else:
        s_spec = pl.BlockSpec((LANE, c), lambda q, j, fb: (fb[q] + j, 0))
        o_spec = pl.BlockSpec((1, LANE, c), lambda q, j, fb: (q, j, 0))

    def body(fb_ref, x_ref, y_ref, o_ref):
        del fb_ref
        o_ref[0] = (x_ref[...].astype(F32) + y_ref[0].astype(F32)).astype(o_ref.dtype)

    return pl.pallas_call(
        body, name=name, out_shape=jax.ShapeDtypeStruct(came.shape, came.dtype),
        grid_spec=pltpu.PrefetchScalarGridSpec(num_scalar_prefetch=1, grid=(4, nblk), in_specs=[s_spec, o_spec], out_specs=o_spec),
        compiler_params=_cp(),
    )(first_blk, src, came)


def _add2(x, y, name):
    shp = x.shape
    x, y = x.reshape(-1, shp[-1]), y.reshape(-1, shp[-1])
    R, C = x.shape
    tr = R
    while tr * C * 4 > (1 << 21) and tr % 32 == 0:
        tr //= 2

    def body(x_ref, y_ref, o_ref):
        o_ref[...] = (x_ref[...].astype(F32) + y_ref[...].astype(F32)).astype(o_ref.dtype)

    spec = pl.BlockSpec((tr, C), lambda i: (i, 0))
    return pl.pallas_call(body, name=name, grid=(R // tr,), in_specs=[spec, spec], out_specs=spec,
                          out_shape=jax.ShapeDtypeStruct((R, C), x.dtype), compiler_params=_cp())(x, y).reshape(shp)


def _exchange_start(srcs, slicers, slices, after, name, plan=_peers, nslots=N_DEV):
    n = len(srcs)
    nsem = (nslots - 1) * n
    lands = [lax.empty((nslots,) + shp, dt) for shp, dt in slices]

    def body(*refs):
        ins, lands_in = refs[:n], refs[n:2 * n]
        send_sems, recv_sems, local_sems = refs[2 * n + 1], refs[2 * n + 2], refs[2 * n + 3]
        token = refs[-1]
        me, peers = plan()
        for a in range(n):
            pltpu.make_async_copy(slicers[a](ins[a], me), lands_in[a].at[me], local_sems.at[a]).start()
        for k, (peer, pidx) in enumerate(peers):
            for a in range(n):
                pltpu.make_async_remote_copy(
                    src_ref=slicers[a](ins[a], pidx), dst_ref=lands_in[a].at[me], send_sem=send_sems.at[k * n + a],
                    recv_sem=recv_sems.at[k * n + a], device_id=peer, device_id_type=pl.DeviceIdType.MESH).start()
        token[...] = jnp.zeros_like(token)

    hbm = lambda a: pltpu.with_memory_space_constraint(a, pltpu.HBM)
    return pl.pallas_call(
        body, name=name,
        out_shape=(pltpu.SemaphoreType.DMA((nsem,)), pltpu.SemaphoreType.DMA((nsem,)), pltpu.SemaphoreType.DMA((n,)),
                   *[pltpu.HBM(a.shape, a.dtype) for a in srcs], *[pltpu.HBM(a.shape, a.dtype) for a in lands],
                   jax.ShapeDtypeStruct((SUB, LANE), F32)),
        in_specs=[_HBM] * (2 * n) + [pl.BlockSpec(memory_space=pl.ANY)],
        out_specs=(_SEM, _SEM, _SEM, *[_HBM] * (2 * n), pl.BlockSpec(memory_space=pltpu.VMEM)),
        input_output_aliases={i: 3 + i for i in range(2 * n)},
        compiler_params=pltpu.CompilerParams(has_side_effects=_DATAFLOW),
    )(*[hbm(a) for a in srcs], *[hbm(a) for a in lands], after)


def _exchange_wait(started, slicers, after, name, plan=_peers):
    n = (len(started) - 4) // 2
    sems, thru = started[0:3], started[3:3 + 2 * n]

    def body(*refs):
        srcs, lands = refs[:n], refs[n:2 * n]
        send_sems, recv_sems, local_sems = refs[2 * n], refs[2 * n + 1], refs[2 * n + 2]
        me, peers = plan()
        for k, (peer, pidx) in enumerate(peers):
            for a in range(n):
                cp = pltpu.make_async_remote_copy(
                    src_ref=slicers[a](srcs[a], pidx), dst_ref=lands[a].at[me], send_sem=send_sems.at[k * n + a],
                    recv_sem=recv_sems.at[k * n + a], device_id=peer, device_id_type=pl.DeviceIdType.MESH)
                cp.wait_send()
                cp.wait_recv()
        for a in range(n):
            pltpu.make_async_copy(slicers[a](srcs[a], me), lands[a].at[me], local_sems.at[a]).wait()

    outs = pl.pallas_call(
        body, name=name, out_shape=[pltpu.HBM(a.shape, a.dtype) for a in thru],
        in_specs=[_HBM] * (2 * n) + [_SEM, _SEM, _SEM, pl.BlockSpec(memory_space=pl.ANY)], out_specs=[_HBM] * (2 * n),
        input_output_aliases={i: i for i in range(2 * n)}, compiler_params=pltpu.CompilerParams(has_side_effects=_DATAFLOW),
    )(*thru, *sems, after)
    return outs[n:]


WIN = 13 * LANE


def _win_base(s):
    n = s * SHARD_IN
    a0 = n + jnp.where(n >= _KR0, KR_LANE, 0) + jnp.where(n >= _KR0 + 32, 32, 0)
    return jnp.minimum(a0 // LANE, (ZW - WIN) // LANE)


def _win_offsets(s):
    n = s * SHARD_IN + jnp.arange(SHARD_IN)
    o = s * SHARD_IN - _win_base(s) * LANE
    return n, (o, o + KR_LANE, o + LANE - 32)


def _to_window(shard, s):
    _, offs = _win_offsets(s)
    padded = jnp.pad(shard, ((0, 0), (0, 0), (WIN, WIN)))
    a, b, c = [lax.dynamic_slice(padded, (0, 0, WIN - o), shard.shape[:2] + (WIN,)) for o in offs]
    col = (_win_base(s) * LANE + jnp.arange(WIN))[None, None, :]
    zero = jnp.zeros_like(a)
    return jnp.where(col < _KR0, a, jnp.where((col >= _KR0 + KR_LANE) & (col < _KR0 + KR_LANE + 32), b, jnp.where(col >= _KR0 + LANE, c, zero)))


def _from_window(win, s):
    n, offs = _win_offsets(s)
    a, b, c = [lax.dynamic_slice(win, (0, 0, o), win.shape[:2] + (SHARD_IN,)) for o in offs]
    return jnp.where((n < _KR0)[None, None, :], a, jnp.where((n < _KR0 + 32)[None, None, :], b, c))


def _win_base_static(s):
    n = s * SHARD_IN
    a0 = n + (KR_LANE if n >= _KR0 else 0) + (32 if n >= _KR0 + 32 else 0)
    return min(a0 // LANE, (ZW - WIN) // LANE)


def _assemble_w_in(gw):
    tr = 128
    bases = [_win_base_static(s) for s in range(N_DEV)]

    def body(g_ref, o_ref):
        for j in range(ZW // LANE):
            acc = None
            for s in range(N_DEV):
                if bases[s] <= j < bases[s] + WIN // LANE:
                    piece = g_ref[s, :, (j - bases[s]) * LANE:(j - bases[s] + 1) * LANE]
                    acc = piece if acc is None else acc + piece
            o_ref[:, j * LANE:(j + 1) * LANE] = acc

    return pl.pallas_call(
        body, name="assemble_w_in", grid=(D // tr,), in_specs=[pl.BlockSpec((N_DEV, tr, WIN), lambda i: (0, i, 0))],
        out_specs=pl.BlockSpec((tr, ZW), lambda i: (i, 0)), out_shape=jax.ShapeDtypeStruct((D, ZW), gw.dtype), compiler_params=_cp(),
    )(gw)


def _cols(width):
    return lambda ref, p: ref.at[:, pl.ds(pl.multiple_of(p * width, width), width)]


def _rows(height):
    return lambda ref, p: ref.at[pl.ds(pl.multiple_of(p * height, height), height), :]


SCATTER = {
    'w_in': (lambda ref, p: ref.at[:, pl.ds(pl.multiple_of(_win_base(p) * LANE, LANE), WIN)], (D, WIN), BF16),
    'conv_w': (_cols(LANE), (4, LANE), F32),
    'w_lru_o': (_rows(LANE), (LANE, D), BF16),
    'w_uq': (_cols(LANE), (256, LANE), F32),
    'w_ukv': (_cols(LANE), (128, LANE), F32),
    'w_mla_o': (_cols(LANE), (512, LANE), BF16),
    'w_dil_o': (_cols(LANE), (512, LANE), BF16),
    'w_out': (_rows(LANE), (LANE, D), BF16),
}
SLICED_AXIS = {'w_in': 1, 'conv_w': 1, 'w_lru_o': 0, 'w_uq': 1, 'w_ukv': 1, 'w_mla_o': 1, 'w_dil_o': 1, 'w_out': 0}


PACK_ROWS = 64


def _packed_rows(shapes):
    n = sum(int(np.prod(s)) for s in shapes)
    return -(-n // (PACK_ROWS * LANE)) * PACK_ROWS


def _sum8(buf, name):
    ns, R, C = buf.shape
    tr = R
    while tr * C * 4 * ns > (1 << 22) and tr % 32 == 0:
        tr //= 2

    def body(b_ref, o_ref):
        acc = b_ref[0].astype(F32)
        for s in range(1, ns):
            acc = acc + b_ref[s].astype(F32)
        o_ref[...] = acc

    return pl.pallas_call(
        body, name=name, grid=(R // tr,), in_specs=[pl.BlockSpec((ns, tr, C), lambda i: (0, i, 0))],
        out_specs=pl.BlockSpec((tr, C), lambda i: (i, 0)), out_shape=jax.ShapeDtypeStruct((R, C), F32), compiler_params=_cp(),
    )(buf)


def _pack(arrs, dtype, lead):
    flat = [a.astype(dtype).reshape(a.shape[:lead] + (-1,)) for a in arrs]
    cat = jnp.concatenate(flat, axis=-1)
    n = cat.shape[-1]
    unit = PACK_ROWS * LANE
    pad = (-n) % unit
    if pad:
        cat = jnp.pad(cat, [(0, 0)] * lead + [(0, pad)])
    return cat.reshape(cat.shape[:lead] + ((n + pad) // LANE, LANE))


def _unpack(buf, shapes, lead):
    flat = buf.reshape(buf.shape[:lead] + (-1,))
    out, off = [], 0
    for shp in shapes:
        n = int(np.prod(shp))
        out.append(flat[..., off:off + n].reshape(buf.shape[:lead] + tuple(shp)))
        off += n
    return out


def _adamw(w, g, m, v, name):
    layers, rows, cols = w.shape
    tr = rows
    while tr * cols * 4 > (3 << 19) and tr % 16 == 0:
        tr //= 2
    c1 = 1.0 - ADAM_B1 ** ADAM_STEP
    c2 = 1.0 - ADAM_B2 ** ADAM_STEP

    def body(w_ref, g_ref, m_ref, v_ref, d_ref, mo_ref, vo_ref):
        gv = g_ref[...]
        mn = ADAM_B1 * m_ref[...] + (1.0 - ADAM_B1) * gv
        vn = ADAM_B2 * v_ref[...] + (1.0 - ADAM_B2) * (gv * gv)
        mo_ref[...] = mn
        vo_ref[...] = vn
        d_ref[...] = -ADAM_LR * ((mn / c1) / (jnp.sqrt(vn / c2) + ADAM_EPS) + ADAM_WD * w_ref[...])

    spec = pl.BlockSpec((1, tr, cols), lambda l, i: (l, i, 0))
    return pl.pallas_call(
        body, name=name, grid=(layers, rows // tr), in_specs=[spec] * 4, out_specs=[spec] * 3,
        out_shape=[jax.ShapeDtypeStruct((layers, rows, cols), F32)] * 3, compiler_params=_cp(),
    )(w, g, m, v)


IN_NAMES = ['x', 'positions', 'norm_g', 'w_in', 'conv_w', 'conv_b', 'w_gate_x', 'b_gate_x', 'w_gate_a', 'b_gate_a', 'lru_lambda', 'w_lru_o',
            'cq_norm_g', 'ckv_norm_g', 'w_uq', 'w_ukv', 'mla_q_norm_g', 'mla_k_norm_g', 'w_mla_o', 'dil_q_norm_g', 'dil_k_norm_g', 'w_dil_o',
            'b_merge', 'w_out']
WEIGHTS = IN_NAMES[2:]
REPLICATED = [n for n in WEIGHTS if n not in SCATTER]
GATE_WEIGHTS = ('w_gate_x', 'w_gate_a')

_KR0 = C_KR * LANE


GATHERED = ['w_in', 'w_lru_o', 'w_uq', 'w_ukv', 'w_mla_o', 'w_dil_o', 'w_out', 'conv_w']


def _local_weights(wd, me):
    loc = {n: wd[n].astype(BF16) for n in GATHERED[:-1]}
    loc['w_in'] = _to_window(loc['w_in'], me)
    loc['w_uq'] = jnp.pad(loc['w_uq'], ((0, 0), (0, 0), (0, LANE - MLA_QK)))
    loc['conv_w'] = wd['conv_w']
    return [[loc[n][l] for n in GATHERED] for l in range(DEPTH)]


def _layer_weights(gathered, rep, l):
    gw = dict(zip(GATHERED, gathered))
    by_rows = lambda a: a.reshape(-1, a.shape[-1])
    by_cols = lambda a: jnp.swapaxes(a, 0, 1).reshape(a.shape[1], -1)
    ukv = jnp.swapaxes(gw['w_ukv'], 0, 1)
    g96 = lambda a: jnp.pad(a[l].reshape(1, MLA_QK), ((0, 0), (0, LANE - MLA_QK)))
    g64 = lambda a: jnp.tile(a[l].reshape(1, DIL_HD), (1, 2))
    return dict(
        norm_g=rep['norm_g'][l].reshape(1, D), w_in=_assemble_w_in(gw['w_in']),
        conv_w=by_cols(gw['conv_w']), conv_b=rep['conv_b'][l].reshape(1, D),
        w_gx=rep['w_gate_x'][l].astype(BF16), b_gx=rep['b_gate_x'][l].reshape(8, 1, LANE),
        w_ga=rep['w_gate_a'][l].astype(BF16), b_ga=rep['b_gate_a'][l].reshape(8, 1, LANE),
        lam=rep['lru_lambda'][l].reshape(1, D),
        w_lru_o=by_rows(gw['w_lru_o']), w_mla_o=by_cols(gw['w_mla_o']), w_dil_o=by_cols(gw['w_dil_o']), w_out=by_rows(gw['w_out']),
        g_cq=rep['cq_norm_g'][l].reshape(1, 256), g_ckv=rep['ckv_norm_g'][l].reshape(1, 128),
        w_uq=by_cols(gw['w_uq']), w_uk=jnp.pad(ukv[:, :, :64], ((0, 0), (0, 0), (0, 64))).reshape(128, 1024),
        w_uv=ukv[:, :, 64:].reshape(128, 512),
        g_mq=g96(rep['mla_q_norm_g']), g_mk=g96(rep['mla_k_norm_g']), g_dq=g64(rep['dil_q_norm_g']), g_dk=g64(rep['dil_k_norm_g']),
        b_merge=rep['b_merge'][l].reshape(1, 3 * D),
    )


def _sharded_grads(g):
    uk = g['w_uk'].reshape(128, 8, 128)[:, :, :64]
    uv = g['w_uv'].reshape(128, 8, 64)
    d = {'w_in': g['w_in'], 'conv_w': g['conv_w'], 'w_lru_o': g['w_lru_o'], 'w_uq': g['w_uq'],
         'w_ukv': jnp.concatenate([uk, uv], axis=-1).reshape(128, 1024), 'w_mla_o': g['w_mla_o'], 'w_dil_o': g['w_dil_o'],
         'w_out': g['w_out']}
    return [d[n] for n in SCATTER]


def _replicated_grads(g):
    return {
        'conv_b': g['conv_b'].reshape(D),
        'w_gate_x': g['w_gx'], 'b_gate_x': g['b_gx'].reshape(8, LANE), 'w_gate_a': g['w_ga'], 'b_gate_a': g['b_ga'].reshape(8, LANE),
        'lru_lambda': g['lam'].reshape(D), 'cq_norm_g': g['g_cq'].reshape(256), 'ckv_norm_g': g['g_ckv'].reshape(128),
        'mla_q_norm_g': g['g_mq'][0, :MLA_QK], 'mla_k_norm_g': g['g_mk'][0, :MLA_QK],
        'dil_q_norm_g': g['g_dq'][0, :DIL_HD] + g['g_dq'][0, DIL_HD:], 'dil_k_norm_g': g['g_dk'][0, :DIL_HD] + g['g_dk'][0, DIL_HD:],
        'b_merge': g['b_merge'].reshape(3 * D),
    }


def kernel(x, positions, norm_g, w_in, conv_w, conv_b, w_gate_x, b_gate_x, w_gate_a, b_gate_a, lru_lambda, w_lru_o, cq_norm_g, ckv_norm_g, w_uq, w_ukv, mla_q_norm_g, mla_k_norm_g, w_mla_o, dil_q_norm_g, dil_k_norm_g, w_dil_o, b_merge, w_out, loss_target, m_norm_g, m_w_in, m_conv_w, m_conv_b, m_w_gate_x, m_b_gate_x, m_w_gate_a, m_b_gate_a, m_lru_lambda, m_w_lru_o, m_cq_norm_g, m_ckv_norm_g, m_w_uq, m_w_ukv, m_mla_q_norm_g, m_mla_k_norm_g, m_w_mla_o, m_dil_q_norm_g, m_dil_k_norm_g, m_w_dil_o, m_b_merge, m_w_out, v_norm_g, v_w_in, v_conv_w, v_conv_b, v_w_gate_x, v_b_gate_x, v_w_gate_a, v_b_gate_a, v_lru_lambda, v_w_lru_o, v_cq_norm_g, v_ckv_norm_g, v_w_uq, v_w_ukv, v_mla_q_norm_g, v_mla_k_norm_g, v_w_mla_o, v_dil_q_norm_g, v_dil_k_norm_g, v_w_dil_o, v_b_merge, v_w_out):
    args = (x, positions, norm_g, w_in, conv_w, conv_b, w_gate_x, b_gate_x, w_gate_a, b_gate_a, lru_lambda, w_lru_o, cq_norm_g, ckv_norm_g, w_uq, w_ukv, mla_q_norm_g, mla_k_norm_g, w_mla_o, dil_q_norm_g, dil_k_norm_g, w_dil_o, b_merge, w_out)
    moments_m = (m_norm_g, m_w_in, m_conv_w, m_conv_b, m_w_gate_x, m_b_gate_x, m_w_gate_a, m_b_gate_a, m_lru_lambda, m_w_lru_o, m_cq_norm_g, m_ckv_norm_g, m_w_uq, m_w_ukv, m_mla_q_norm_g, m_mla_k_norm_g, m_w_mla_o, m_dil_q_norm_g, m_dil_k_norm_g, m_w_dil_o, m_b_merge, m_w_out)
    moments_v = (v_norm_g, v_w_in, v_conv_w, v_conv_b, v_w_gate_x, v_b_gate_x, v_w_gate_a, v_b_gate_a, v_lru_lambda, v_w_lru_o, v_cq_norm_g, v_ckv_norm_g, v_w_uq, v_w_ukv, v_mla_q_norm_g, v_mla_k_norm_g, v_w_mla_o, v_dil_q_norm_g, v_dil_k_norm_g, v_w_dil_o, v_b_merge, v_w_out)
    a = dict(zip(IN_NAMES, args))
    wd = {n: a[n] for n in WEIGHTS}
    md = dict(zip(WEIGHTS, moments_m))
    vd = dict(zip(WEIGHTS, moments_v))

    me = 4 * lax.axis_index("x") + 2 * lax.axis_index("y") + lax.axis_index("c")

    assert DEPTH == 2
    xs, tabs = x[0], _rope_tables(positions[0])
    whole = [_whole] * len(GATHERED)
    slicers = [SCATTER[n][0] for n in SCATTER]
    grad_slices = [SCATTER[n][1:3] for n in SCATTER]

    local = _local_weights(wd, me)
    w_slices = [(a.shape, a.dtype) for a in local[0]]
    landed0 = _gather_two_level(local[0], "gather_w0")
    flying = _exchange_start(local[1], whole, w_slices, landed0[0], "gather_w1_start")
    rep0 = dict(wd, norm_g=wd['norm_g'] + flying[-1][0, 0])
    w0 = _layer_weights(landed0, rep0, 0)
    x1, saved0 = _layer_fwd(xs, w0, tabs)
    w1 = _layer_weights(_exchange_wait(flying, whole, x1, "gather_w1_wait"), wd, 1)
    x2, saved1 = _layer_fwd(x1, w1, tabs)
    loss, dx2 = _loss_fwd_bwd(x2, loss_target[0])
    loss = loss[0, 0]

    sharded = list(SCATTER)
    nsh = len(sharded)
    small = [n for n in REPLICATED if n not in GATE_WEIGHTS and n != 'norm_g']

    def outgoing(g):
        r = _replicated_grads(g)
        return (_sharded_grads(g) + [_pack([r[n] for n in small], F32, 0)]
                + [r[n].astype(BF16).reshape(8 * LANE, LANE) for n in GATE_WEIGHTS])

    out_slicers = slicers + [_whole] * 3
    out_slices = grad_slices + [((_packed_rows([wd[n].shape[1:] for n in small]), LANE), F32)] + [((8 * LANE, LANE), BF16)] * 2
    dx1, g1 = _layer_bwd(dx2, w1, tabs, saved1)
    flying1 = _exchange_start(outgoing(g1), out_slicers, out_slices, dx1, "scatter_g1_start")
    later = {}

    names = sharded + ['small'] + list(GATE_WEIGHTS)
    sliced = [True] * nsh + [False] * 3
    by_chip = [(lambda ref, q: ref.at[q])] * nsh + [_whole] * 3

    def send_layer0(g):
        later['got1'] = _exchange_wait(flying1, out_slicers, g['w_in'], "scatter_g1_wait")
        mine = outgoing(g)
        came = _pair_exchange(mine, out_slicers, out_slices, sliced, "pair_g0")
        my_side = 2 * jnp.arange(4, dtype=jnp.int32) + lax.axis_index("c")
        halves = []
        for n, a, c in zip(names, mine, came):
            if n in SCATTER:
                first = _win_base(my_side) if n == 'w_in' else my_side
                halves.append(_pair_add(a, c, first.astype(jnp.int32), SLICED_AXIS[n], f"pair_sum_{n}"))
            else:
                halves.append(_add2(a, c[0], f"pair_sum_{n}"))
        later['flying0'] = _exchange_start(halves, by_chip, out_slices, later['got1'][0], "scatter_g0_start", plan=_plan_chips, nslots=4)
        return later['flying0'][-1]

    grad_x, g0 = _layer_bwd(dx1, w0, tabs, saved0, hook=send_layer0, after=flying1[-1])
    sum1 = [_sum8(b, f"sum_{n}_1") for n, b in zip(names, later['got1'])]
    behind = grad_x[:1, :1] + sum(s_[:1, :1] for s_ in sum1)
    got0 = _exchange_wait(later['flying0'], by_chip, behind, "scatter_g0_wait", plan=_plan_chips)
    sum0 = [_sum8(b, f"sum_{n}_0") for n, b in zip(names, got0)]
    norm_part = _pack([jnp.stack([g['norm_g'].reshape(D) for g in (g0, g1)])], F32, 0)
    norm_sum = _sum8(_exchange([norm_part], [_whole], [(norm_part.shape, F32)], "gather_norm_g")[0], "sum_norm_g")

    gsh = {n: jnp.stack([sum0[i], sum1[i]]) for i, n in enumerate(sharded)}
    gsh['w_in'] = _from_window(gsh['w_in'], me)
    gsh['w_uq'] = gsh['w_uq'][:, :, :MLA_QK]
    grep = {'norm_g': _unpack(norm_sum, [wd['norm_g'].shape], 0)[0]}
    per_layer = [_unpack(s[nsh], [wd[n].shape[1:] for n in small], 0) for s in (sum0, sum1)]
    grep.update({n: jnp.stack([per_layer[l][i] for l in range(DEPTH)]) for i, n in enumerate(small)})
    for i, n in enumerate(GATE_WEIGHTS):
        grep[n] = jnp.stack([sum0[nsh + 1 + i], sum1[nsh + 1 + i]]).reshape(wd[n].shape)

    out_g, out_d, out_m, out_v = {}, {}, {}, {}
    vecs = ['norm_g'] + small
    vshapes = [wd[n].shape for n in vecs]
    packed_g = _pack([grep[n] for n in vecs], F32, 0)
    d_, m_, v_ = _adamw(_pack([wd[n] for n in vecs], F32, 0)[None], packed_g[None], _pack([md[n] for n in vecs], F32, 0)[None],
                        _pack([vd[n] for n in vecs], F32, 0)[None], "adamw_vectors")
    for dst, buf in ((out_d, d_), (out_m, m_), (out_v, v_)):
        dst.update(zip(vecs, _unpack(buf[0], vshapes, 0)))
    out_g.update({n: grep[n] for n in vecs})
    gsh.update({n: grep[n] for n in GATE_WEIGHTS})
    for n in sharded + list(GATE_WEIGHTS):
        shp = wd[n].shape
        three = (1, -1, shp[-1])
        d_, m_, v_ = _adamw(wd[n].reshape(three), gsh[n].reshape(three), md[n].reshape(three), vd[n].reshape(three), "adamw_" + n)
        out_g[n], out_d[n], out_m[n], out_v[n] = gsh[n], d_.reshape(shp), m_.reshape(shp), v_.reshape(shp)

    loss = lax.psum(loss, ("x", "y", "c"))
    return (loss, grad_x[None], *[out_g[n] for n in WEIGHTS], *[out_d[n] for n in WEIGHTS], *[out_m[n] for n in WEIGHTS],
            *[out_v[n] for n in WEIGHTS])
```

```python
import numpy as np
import jax
import jax.numpy as jnp
from jax import lax
from jax.experimental import pallas as pl
from jax.experimental.pallas import tpu as pltpu

F32 = jnp.float32
BF16 = jnp.bfloat16

N_DEV = 8
D = 1024
DEPTH = 2
EPS = 1e-6
ROPE_THETA = 10000.0
LRU_C = 8.0
LANE = 128
SUB = 8
IN_WIDTH = 11168
SHARD_IN = IN_WIDTH // N_DEV

C_LRUX, C_LRUG, C_CQ, C_CKV, C_KR, C_MLAG, C_DQ, C_DK, C_DV, C_DILG, C_MERGE = 0, 8, 16, 18, 19, 20, 24, 36, 48, 60, 64
ZW = 88 * LANE
KR_LANE = 64

MLA_QK = 96
MLA_SCALE = MLA_QK ** -0.5
DIL_HD = 64
DIL_SCALE = DIL_HD ** -0.5
DIL_DILATIONS = (1, 4, 16)
NK = 128

ADAM_LR, ADAM_B1, ADAM_B2, ADAM_EPS, ADAM_WD, ADAM_STEP = 0.001, 0.9, 0.999, 1e-08, 0.01, 10

NEG = -1e30
LOG2E = 1.4426950408889634
VMEM_LIMIT = 48 * 1024 * 1024


def _cp(**kw):
    return pltpu.CompilerParams(vmem_limit_bytes=VMEM_LIMIT, **kw)


def _sig(x):
    return 1.0 / (1.0 + jnp.exp(-x))


def _silu(x):
    return x * _sig(x)


def _dsilu(x):
    s = _sig(x)
    return s * (1.0 + x * (1.0 - s))


def _dot(a, b, dims):
    return lax.dot_general(a, b, (dims, ((), ())), preferred_element_type=F32)


def _nn(a, b):
    return _dot(a, b, ((1,), (0,)))


def _nt(a, b):
    return _dot(a, b, ((1,), (1,)))


def _tn(a, b):
    return _dot(a, b, ((0,), (0,)))


def _rsum(x):
    return jnp.sum(x, axis=-1, keepdims=True)


def _rsum_mxu(x):
    ones = jnp.ones((x.shape[-1], LANE), F32)
    return lax.dot_general(x, ones, (((1,), (0,)), ((), ())), precision=lax.Precision.HIGHEST, preferred_element_type=F32)


def _csum(x):
    return jnp.sum(x, axis=0, keepdims=True)


def _mm(a, b, *, mode, name, out_dtype=F32, add=None, after=None, tm=1024, tn=1024, tk=1024):
    if mode == "nn":
        (M, K), (K2, N) = a.shape, b.shape
    elif mode == "nt":
        (M, K), (N, K2) = a.shape, b.shape
    else:
        (K, M), (K2, N) = a.shape, b.shape
    assert K == K2
    tm, tn, tk = min(tm, M), min(tn, N), min(tk, K)
    assert M % tm == 0 and N % tn == 0 and K % tk == 0
    nk = K // tk
    fn = {"nn": _nn, "nt": _nt, "tn": _tn}[mode]
    has_add = add is not None

    def body(*refs):
        a_ref, b_ref = refs[0], refs[1]
        add_ref = refs[2] if has_add else None
        o_ref = refs[2 + has_add + (after is not None)]
        part = fn(a_ref[...].astype(BF16), b_ref[...].astype(BF16))

        def fin(acc):
            if has_add:
                acc = acc + add_ref[...]
            o_ref[...] = acc.astype(out_dtype)

        if nk == 1:
            fin(part)
        else:
            acc_ref = refs[-1]
            k = pl.program_id(2)

            @pl.when(k == 0)
            def _():
                acc_ref[...] = part

            @pl.when(k > 0)
            def _():
                acc_ref[...] += part

            @pl.when(k == nk - 1)
            def _():
                fin(acc_ref[...])

    a_spec = pl.BlockSpec((tk, tm), lambda i, j, k: (k, i)) if mode == "tn" else pl.BlockSpec((tm, tk), lambda i, j, k: (i, k))
    b_spec = pl.BlockSpec((tn, tk), lambda i, j, k: (j, k)) if mode == "nt" else pl.BlockSpec((tk, tn), lambda i, j, k: (k, j))
    o_spec = pl.BlockSpec((tm, tn), lambda i, j, k: (i, j))
    in_specs, args = [a_spec, b_spec], [a, b]
    if has_add:
        in_specs.append(o_spec)
        args.append(add)
    if after is not None:
        in_specs.append(pl.BlockSpec(memory_space=pl.ANY))
        args.append(after)
    return pl.pallas_call(
        body, name=name, grid=(M // tm, N // tn, nk), in_specs=in_specs, out_specs=o_spec,
        out_shape=jax.ShapeDtypeStruct((M, N), out_dtype),
        scratch_shapes=[pltpu.VMEM((tm, tn), F32)] if nk > 1 else [],
        compiler_params=_cp(dimension_semantics=("parallel", "parallel", "arbitrary")),
    )(*args)


T_ROW = 512


def _rms_in_fwd(x, g):
    S = x.shape[0]
    T = T_ROW

    def body(x_ref, g_ref, h_ref):
        xv = x_ref[...]
        r = lax.rsqrt(jnp.mean(xv * xv, axis=-1, keepdims=True) + EPS)
        h_ref[...] = (xv * r * g_ref[...]).astype(BF16)

    return pl.pallas_call(
        body, name="rms_in_fwd", grid=(S // T,),
        in_specs=[pl.BlockSpec((T, D), lambda i: (i, 0)), pl.BlockSpec((1, D), lambda i: (0, 0))],
        out_specs=pl.BlockSpec((T, D), lambda i: (i, 0)),
        out_shape=jax.ShapeDtypeStruct((S, D), BF16), compiler_params=_cp(),
    )(x, g)


def _rms_in_bwd(x, g, dh, dres):
    S = x.shape[0]
    T = T_ROW

    def body(x_ref, g_ref, dh_ref, dr_ref, dx_ref, dg_ref):
        i = pl.program_id(0)
        xv = x_ref[...]
        r = lax.rsqrt(jnp.mean(xv * xv, axis=-1, keepdims=True) + EPS)
        xn = xv * r
        dy = dh_ref[...]
        part = _csum(dy * xn)

        @pl.when(i == 0)
        def _():
            dg_ref[...] = part

        @pl.when(i > 0)
        def _():
            dg_ref[...] += part

        dxh = dy * g_ref[...]
        dx_ref[...] = dr_ref[...] + r * (dxh - xn * jnp.mean(dxh * xn, axis=-1, keepdims=True))

    row = pl.BlockSpec((T, D), lambda i: (i, 0))
    vec = pl.BlockSpec((1, D), lambda i: (0, 0))
    return pl.pallas_call(
        body, name="rms_in_bwd", grid=(S // T,), in_specs=[row, vec, row, row], out_specs=[row, vec],
        out_shape=[jax.ShapeDtypeStruct((S, D), F32), jax.ShapeDtypeStruct((1, D), F32)], compiler_params=_cp(),
    )(x, g, dh, dres)


T_LRU = 1024
T_LRU_BWD = 512


def _neg_expm1(y):
    ser = -y * (1.0 + y * 0.5 * (1.0 + y * (1.0 / 3.0) * (1.0 + y * 0.25 * (1.0 + y * 0.2))))
    return jnp.where(y > -0.03, ser, 1.0 - jnp.exp(y))


def _softplus_neg(lam):
    e = jnp.exp(-jnp.abs(lam))
    l1p = jnp.where(e < 0.01, e * (1.0 - e * (0.5 - e * (1.0 / 3.0 - e * 0.25))), jnp.log(1.0 + e))
    return jnp.maximum(-lam, 0.0) + l1p


def _scan_fwd(a, b, T):
    row = lax.broadcasted_iota(jnp.int32, a.shape, 0)
    d = 1
    while d < T:
        m = row >= d
        b = jnp.where(m, a * pltpu.roll(b, d, 0) + b, b)
        a = jnp.where(m, a * pltpu.roll(a, d, 0), a)
        d *= 2
    return a, b


def _scan_bwd(a, b, T):
    row = lax.broadcasted_iota(jnp.int32, a.shape, 0)
    d = 1
    while d < T:
        m = row < T - d
        b = jnp.where(m, a * pltpu.roll(b, T - d, 0) + b, b)
        a = jnp.where(m, a * pltpu.roll(a, T - d, 0), a)
        d *= 2
    return b


def _lru_common(x, prev, first, cw_ref, cb_ref, wgx_ref, bgx_ref, wga_ref, bga_ref, lam_ref, T):
    row = lax.broadcasted_iota(jnp.int32, x.shape, 0)
    prev = jnp.where(first, 0.0, prev)
    xs = []
    for j in (3, 2, 1):
        pv = jnp.tile(pltpu.roll(prev, j, 0), (T // SUB, 1))
        xs.append(jnp.where(row < j, pv, pltpu.roll(x, j, 0)))
    xs.append(x)
    xc = cb_ref[...] + cw_ref[0:1, :] * xs[0] + cw_ref[1:2, :] * xs[1] + cw_ref[2:3, :] * xs[2] + cw_ref[3:4, :] * xs[3]
    xcb = xc.astype(BF16)
    gx = _sig(_nn(xcb, wgx_ref[0]) + bgx_ref[0])
    ga = _sig(_nn(xcb, wga_ref[0]) + bga_ref[0])
    sp = _softplus_neg(lam_ref[...])
    log_a = -LRU_C * ga * sp
    a = jnp.exp(log_a)
    mult = jnp.sqrt(_neg_expm1(2.0 * log_a))
    return xs, xc, xcb, gx, ga, sp, a, mult


def _lru_specs(T, tmap):
    def at(col0):
        return pl.BlockSpec((T, LANE), lambda n, i: (tmap(i), col0 + n))

    def prev(col0):
        return pl.BlockSpec((SUB, LANE), lambda n, i: (jnp.maximum(tmap(i) * (T // SUB) - 1, 0), col0 + n))

    small = [
        pl.BlockSpec((4, LANE), lambda n, i: (0, n)),
        pl.BlockSpec((1, LANE), lambda n, i: (0, n)),
        pl.BlockSpec((1, LANE, LANE), lambda n, i: (n, 0, 0)),
        pl.BlockSpec((1, 1, LANE), lambda n, i: (n, 0, 0)),
        pl.BlockSpec((1, LANE, LANE), lambda n, i: (n, 0, 0)),
        pl.BlockSpec((1, 1, LANE), lambda n, i: (n, 0, 0)),
        pl.BlockSpec((1, LANE), lambda n, i: (0, n)),
    ]
    return at, prev, small


def _lru_fwd(zp, w):
    S = zp.shape[0]
    T = T_LRU
    at, prev, small = _lru_specs(T, lambda i: i)

    def body(x_ref, xp_ref, g_ref, cw_ref, cb_ref, wgx_ref, bgx_ref, wga_ref, bga_ref, lam_ref, hs_ref, y_ref, carry_ref):
        i = pl.program_id(1)

        @pl.when(i == 0)
        def _():
            carry_ref[...] = jnp.zeros_like(carry_ref)

        x = x_ref[...]
        _, xc, _, gx, _, _, a, mult = _lru_common(x, xp_ref[...], i == 0, cw_ref, cb_ref, wgx_ref, bgx_ref, wga_ref, bga_ref, lam_ref, T)
        A, B = _scan_fwd(a, mult * gx * xc, T)
        h = B + A * carry_ref[SUB - 1:SUB, :]
        hs_ref[...] = h
        carry_ref[...] = hs_ref[T - SUB:T, :]
        y_ref[...] = (h * _silu(g_ref[...])).astype(BF16)

    out = pl.BlockSpec((T, LANE), lambda n, i: (i, n))
    return pl.pallas_call(
        body, name="lru_fwd", grid=(8, S // T),
        in_specs=[at(C_LRUX), prev(C_LRUX), at(C_LRUG)] + small, out_specs=[out, out],
        out_shape=[jax.ShapeDtypeStruct((S, D), F32), jax.ShapeDtypeStruct((S, D), BF16)],
        scratch_shapes=[pltpu.VMEM((SUB, LANE), F32)],
        compiler_params=_cp(dimension_semantics=("parallel", "arbitrary")),
    )(zp, zp, zp, w["conv_w"], w["conv_b"], w["w_gx"], w["b_gx"], w["w_ga"], w["b_ga"], w["lam"])


def _lru_bwd(zp, hs, dy, w, dz):
    S = zp.shape[0]
    T = T_LRU_BWD
    nT = S // T
    at, prev, small = _lru_specs(T, lambda i: nT - 1 - i)

    def body(x_ref, xp_ref, g_ref, h_ref, hp_ref, dy_ref, cw_ref, cb_ref, wgx_ref, bgx_ref, wga_ref, bga_ref, lam_ref, dz_in,
             dzx_ref, dcw_ref, dcb_ref, dwgx_ref, dbgx_ref, dwga_ref, dbga_ref, dlam_ref, carry_ref, head_ref):
        del dz_in
        j = pl.program_id(1)
        it = nT - 1 - j

        @pl.when(j == 0)
        def _():
            for r in (carry_ref, head_ref, dcw_ref, dcb_ref, dwgx_ref, dbgx_ref, dwga_ref, dbga_ref, dlam_ref):
                r[...] = jnp.zeros_like(r)

        first = it == 0
        x = x_ref[...]
        xs, xc, xcb, gx, ga, sp, a, mult = _lru_common(x, xp_ref[...], first, cw_ref, cb_ref, wgx_ref, bgx_ref, wga_ref, bga_ref, lam_ref, T)
        row = lax.broadcasted_iota(jnp.int32, x.shape, 0)
        u = gx * xc
        h = h_ref[...]
        hp = jnp.where(first, 0.0, hp_ref[...])
        hm1 = jnp.where(row < 1, jnp.tile(pltpu.roll(hp, 1, 0), (T // SUB, 1)), pltpu.roll(h, 1, 0))
        dho = dy_ref[...] * _silu(g_ref[...])
        gin = jnp.where(row == T - 1, dho + carry_ref[0:1, :], dho)
        abar = jnp.where(row == T - 1, 0.0, pltpu.roll(a, T - 1, 0))
        dh = _scan_bwd(abar, gin, T)
        carry_ref[...] = (a * dh)[0:SUB, :]
        da = dh * hm1
        dmult = dh * u
        du = dh * mult
        dgx = du * xc
        dxc = du * gx
        dlog_a = da * a - dmult * a * a / mult
        dga = dlog_a * (-LRU_C * sp)
        lam = lam_ref[...]
        dlam_ref[...] += _csum(dlog_a * (-LRU_C * ga)) * (-1.0 / (1.0 + jnp.exp(lam)))
        dpa = dga * ga * (1.0 - ga)
        dpx = dgx * gx * (1.0 - gx)
        dpab, dpxb = dpa.astype(BF16), dpx.astype(BF16)
        dxc = dxc + _nt(dpxb, wgx_ref[0]) + _nt(dpab, wga_ref[0])
        dwgx_ref[0] += _tn(xcb, dpxb)
        dwga_ref[0] += _tn(xcb, dpab)
        dbgx_ref[0] += _csum(dpx)
        dbga_ref[0] += _csum(dpa)
        dcb_ref[...] += _csum(dxc)
        for k in range(4):
            dcw_ref[k:k + 1, :] += _csum(dxc * xs[k])
        head = head_ref[...]
        dx = cw_ref[3:4, :] * dxc
        for jj in (1, 2, 3):
            hv = jnp.tile(pltpu.roll(head, SUB - jj, 0), (T // SUB, 1))
            dx = dx + cw_ref[3 - jj:4 - jj, :] * jnp.where(row >= T - jj, hv, pltpu.roll(dxc, T - jj, 0))
        head_ref[...] = dxc[0:SUB, :]
        dzx_ref[...] = dx.astype(BF16)

    def acc(shape, imap):
        return pl.BlockSpec(shape, imap)

    out_specs = [
        pl.BlockSpec((T, LANE), lambda n, i: (nT - 1 - i, C_LRUX + n)),
        acc((4, LANE), lambda n, i: (0, n)), acc((1, LANE), lambda n, i: (0, n)),
        acc((1, LANE, LANE), lambda n, i: (n, 0, 0)), acc((1, 1, LANE), lambda n, i: (n, 0, 0)),
        acc((1, LANE, LANE), lambda n, i: (n, 0, 0)), acc((1, 1, LANE), lambda n, i: (n, 0, 0)),
        acc((1, LANE), lambda n, i: (0, n)),
    ]
    out_shape = [
        jax.ShapeDtypeStruct(dz.shape, BF16),
        jax.ShapeDtypeStruct((4, D), F32), jax.ShapeDtypeStruct((1, D), F32),
        jax.ShapeDtypeStruct((8, LANE, LANE), F32), jax.ShapeDtypeStruct((8, 1, LANE), F32),
        jax.ShapeDtypeStruct((8, LANE, LANE), F32), jax.ShapeDtypeStruct((8, 1, LANE), F32),
        jax.ShapeDtypeStruct((1, D), F32),
    ]
    dyspec = pl.BlockSpec((T, LANE), lambda n, i: (nT - 1 - i, n))
    hprev = pl.BlockSpec((SUB, LANE), lambda n, i: (jnp.maximum((nT - 1 - i) * (T // SUB) - 1, 0), n))
    return pl.pallas_call(
        body, name="lru_bwd", grid=(8, nT),
        in_specs=[at(C_LRUX), prev(C_LRUX), at(C_LRUG), dyspec, hprev, dyspec] + small + [pl.BlockSpec(memory_space=pl.ANY)],
        out_specs=out_specs, out_shape=out_shape,
        scratch_shapes=[pltpu.VMEM((SUB, LANE), F32), pltpu.VMEM((SUB, LANE), F32)],
        input_output_aliases={13: 0},
        compiler_params=_cp(dimension_semantics=("parallel", "arbitrary")),
    )(zp, zp, zp, hs, hs, dy, w["conv_w"], w["conv_b"], w["w_gx"], w["b_gx"], w["w_ga"], w["b_ga"], w["lam"], dz)


def _lru_gate_bwd(zp, hs, dy, dz):
    S = zp.shape[0]
    T = T_ROW

    def body(g_ref, h_ref, dy_ref, dz_in, o_ref):
        del dz_in
        o_ref[...] = (dy_ref[...] * h_ref[...] * _dsilu(g_ref[...])).astype(BF16)

    row = pl.BlockSpec((T, D), lambda i: (i, 0))
    zc = pl.BlockSpec((T, D), lambda i: (i, C_LRUG // 8))
    return pl.pallas_call(
        body, name="lru_gate_bwd", grid=(S // T,), in_specs=[zc, row, row, pl.BlockSpec(memory_space=pl.ANY)], out_specs=zc,
        out_shape=jax.ShapeDtypeStruct(dz.shape, BF16), input_output_aliases={3: 0}, compiler_params=_cp(),
    )(zp, hs, dy, dz)


def _rope_tables(pos):
    pf = pos.astype(F32)[:, None]

    def cs(d):
        inv = ROPE_THETA ** (-jnp.arange(0, d, 2, dtype=F32) / d)
        ang = pf * inv
        return jnp.cos(ang), jnp.sin(ang)

    S = pos.shape[0]
    c, s = cs(32)
    one, zero = jnp.ones((S, 64), F32), jnp.zeros((S, 16), F32)
    z32, z64 = jnp.zeros((S, 32), F32), jnp.zeros((S, 64), F32)
    mla = (jnp.concatenate([one, c, c, jnp.ones((S, 32), F32)], 1),
           jnp.concatenate([z64, zero, s, z32], 1),
           jnp.concatenate([z64, -s, zero, z32], 1))
    c, s = cs(64)
    dil = (jnp.concatenate([c, c, c, c], 1),
           jnp.concatenate([z32, s, z32, s], 1),
           jnp.concatenate([-s, z32, -s, z32], 1))
    return mla, dil


def _rope(x, C, S1, S2, sh):
    return x * C + pltpu.roll(x, sh, 1) * S1 + pltpu.roll(x, LANE - sh, 1) * S2


def _rope_t(dy, C, S1, S2, sh):
    return dy * C + pltpu.roll(dy * S1, LANE - sh, 1) + pltpu.roll(dy * S2, sh, 1)


def _lane(shape):
    return lax.broadcasted_iota(jnp.int32, shape, 1)


T_MLA = 512
TA = 512


def _zcol(T, width, col_lanes):
    assert (col_lanes * LANE) % width == 0
    return pl.BlockSpec((T, width), lambda i: (i, col_lanes * LANE // width))


def _full(shape):
    return pl.BlockSpec(shape, lambda *_: (0,) * len(shape))


def _mla_pre_fwd(zp, w, tab):
    S = zp.shape[0]
    T = T_MLA

    def body(cq_ref, ckv_ref, kr_ref, gcq_ref, gckv_ref, wuq_ref, wuk_ref, wuv_ref, gq_ref, gk_ref, C_ref, S1_ref, S2_ref,
             q_ref, k_ref, v_ref):
        cq = cq_ref[...]
        cqn = (cq * lax.rsqrt(jnp.mean(cq * cq, axis=-1, keepdims=True) + EPS) * gcq_ref[...]).astype(BF16)
        ckv = ckv_ref[...]
        ckvn = (ckv * lax.rsqrt(jnp.mean(ckv * ckv, axis=-1, keepdims=True) + EPS) * gckv_ref[...]).astype(BF16)
        q0 = _nn(cqn, wuq_ref[...])
        k0 = _nn(ckvn, wuk_ref[...])
        krb = kr_ref[...]
        C, S1, S2 = C_ref[...], S1_ref[...], S2_ref[...]
        for h in range(8):
            sl = slice(h * LANE, (h + 1) * LANE)
            xq = q0[:, sl]
            xq = xq * lax.rsqrt(_rsum_mxu(xq * xq) * (1.0 / MLA_QK) + EPS) * gq_ref[...]
            q_ref[:, sl] = _rope(xq, C, S1, S2, 16).astype(BF16)
            xk = k0[:, sl] + krb
            xk = xk * lax.rsqrt(_rsum_mxu(xk * xk) * (1.0 / MLA_QK) + EPS) * gk_ref[...]
            k_ref[:, sl] = _rope(xk, C, S1, S2, 16).astype(BF16)
        v_ref[...] = _nn(ckvn, wuv_ref[...]).astype(BF16)

    tabspec = pl.BlockSpec((T, LANE), lambda i: (i, 0))
    in_specs = [_zcol(T, 256, C_CQ), _zcol(T, LANE, C_CKV), _zcol(T, LANE, C_KR), _full((1, 256)), _full((1, LANE)),
                _full((256, 1024)), _full((LANE, 1024)), _full((LANE, 512)), _full((1, LANE)), _full((1, LANE)),
                tabspec, tabspec, tabspec]
    return pl.pallas_call(
        body, name="mla_pre_fwd", grid=(S // T,), in_specs=in_specs,
        out_specs=[pl.BlockSpec((T, 1024), lambda i: (i, 0)), pl.BlockSpec((T, 1024), lambda i: (i, 0)), pl.BlockSpec((T, 512), lambda i: (i, 0))],
        out_shape=[jax.ShapeDtypeStruct((S, 1024), BF16), jax.ShapeDtypeStruct((S, 1024), BF16), jax.ShapeDtypeStruct((S, 512), BF16)],
        compiler_params=_cp(),
    )(zp, zp, zp, w["g_cq"], w["g_ckv"], w["w_uq"], w["w_uk"], w["w_uv"], w["g_mq"], w["g_mk"], *tab)


def _mla_attn_fwd(q, k, v, zp):
    S = q.shape[0]
    nq = S // TA

    def body(q_ref, k_ref, v_ref, g_ref, o_ref, lse_ref, y_ref):
        qi = pl.program_id(1)
        lane = _lane((TA, LANE))
        rowi = lax.broadcasted_iota(jnp.int32, (TA, TA), 0)
        coli = lax.broadcasted_iota(jnp.int32, (TA, TA), 1)
        o_tot = jnp.zeros((TA, LANE), F32)
        for hh in range(2):
            cs = slice(hh * LANE, (hh + 1) * LANE)
            hm = (lane < 64) if hh == 0 else (lane >= 64)
            qh = q_ref[:, cs]
            ones_lane = 64 if hh == 0 else 0

            def step(kb, carry, masked, cs=cs, hm=hm, qh=qh, ones_lane=ones_lane):
                m, acc = carry
                off = pl.multiple_of(kb * TA, TA)
                kh = k_ref[pl.ds(off, TA), cs]
                vv = v_ref[pl.ds(off, TA), :]
                vh = jnp.where(hm, vv, jnp.where(lane == ones_lane, jnp.ones_like(vv), jnp.zeros_like(vv)))
                s = _nt(qh, kh) * (MLA_SCALE * LOG2E)
                if masked:
                    s = jnp.where(rowi >= coli, s, NEG)
                m_new = jnp.maximum(m, jnp.max(s, axis=-1, keepdims=True))
                acc = jnp.exp2(m - m_new) * acc + _nn(jnp.exp2(s - m_new).astype(BF16), vh)
                return m_new, acc

            init = (jnp.full((TA, 1), NEG, F32), jnp.zeros((TA, LANE), F32))
            carry = lax.fori_loop(0, qi, lambda kb, c: step(kb, c, False), init)
            m, acc = step(qi, carry, True)
            l = _rsum(jnp.where(lane == ones_lane, acc, 0.0))
            o_tot = o_tot + jnp.where(hm, acc, 0.0) / l
            lse_ref[:, cs] = jnp.broadcast_to(m * (1.0 / LOG2E) + jnp.log(l), (TA, LANE))
        o_ref[...] = o_tot
        y_ref[...] = (o_tot * _silu(g_ref[...])).astype(BF16)

    blk = pl.BlockSpec((TA, LANE), lambda p, i: (i, p))
    return pl.pallas_call(
        body, name="mla_attn_fwd", grid=(4, nq),
        in_specs=[pl.BlockSpec((TA, 256), lambda p, i: (i, p)), pl.BlockSpec((S, 256), lambda p, i: (0, p)),
                  pl.BlockSpec((S, LANE), lambda p, i: (0, p)), pl.BlockSpec((TA, LANE), lambda p, i: (i, C_MLAG + p))],
        out_specs=[blk, pl.BlockSpec((TA, 256), lambda p, i: (i, p)), blk],
        out_shape=[jax.ShapeDtypeStruct((S, 512), F32), jax.ShapeDtypeStruct((S, 1024), F32), jax.ShapeDtypeStruct((S, 512), BF16)],
        compiler_params=_cp(dimension_semantics=("parallel", "arbitrary")),
    )(q, k, v, zp)


def _mla_post_bwd(zp, o, dy, dz):
    S = zp.shape[0]
    T = T_ROW

    def body(g_ref, o_ref, dy_ref, dz_in, dz_ref, do_ref, D_ref):
        del dz_in
        g, o_, dy_ = g_ref[...], o_ref[...], dy_ref[...]
        do = dy_ * _silu(g)
        do_ref[...] = do.astype(BF16)
        dz_ref[...] = (dy_ * o_ * _dsilu(g)).astype(BF16)
        prod = do * o_
        lane = _lane((T, LANE))
        for p in range(4):
            pr = prod[:, p * LANE:(p + 1) * LANE]
            da = _rsum(jnp.where(lane < 64, pr, 0.0))
            db = _rsum(jnp.where(lane >= 64, pr, 0.0))
            D_ref[:, 2 * p * LANE:(2 * p + 1) * LANE] = jnp.broadcast_to(da, (T, LANE))
            D_ref[:, (2 * p + 1) * LANE:(2 * p + 2) * LANE] = jnp.broadcast_to(db, (T, LANE))

    row = pl.BlockSpec((T, 512), lambda i: (i, 0))
    zc = _zcol(T, 512, C_MLAG)
    return pl.pallas_call(
        body, name="mla_post_bwd", grid=(S // T,), in_specs=[zc, row, row, pl.BlockSpec(memory_space=pl.ANY)],
        out_specs=[zc, row, pl.BlockSpec((T, 1024), lambda i: (i, 0))],
        out_shape=[jax.ShapeDtypeStruct(dz.shape, BF16), jax.ShapeDtypeStruct((S, 512), BF16), jax.ShapeDtypeStruct((S, 1024), F32)],
        input_output_aliases={3: 0}, compiler_params=_cp(),
    )(zp, o, dy, dz)


def _mla_attn_bwd(q, k, v, do, lse, Dr):
    S = q.shape[0]
    nq = S // TA

    def body(q_ref, do_ref, lse_ref, D_ref, k_ref, v_ref, dq_ref, dk_ref, dv_ref):
        ki = pl.program_id(1)

        @pl.when(ki == 0)
        def _():
            dq_ref[...] = jnp.zeros_like(dq_ref)

        lane = _lane((TA, LANE))
        rowi = lax.broadcasted_iota(jnp.int32, (TA, TA), 0)
        coli = lax.broadcasted_iota(jnp.int32, (TA, TA), 1)
        dv_tot = jnp.zeros((TA, LANE), F32)
        for hh in range(2):
            cs = slice(hh * LANE, (hh + 1) * LANE)
            hm = (lane < 64) if hh == 0 else (lane >= 64)
            kh = k_ref[:, cs]
            vv = v_ref[...]
            vm = jnp.where(hm, vv, jnp.zeros_like(vv))

            def step(qb, carry, masked, cs=cs, kh=kh, vm=vm):
                dk_acc, dv_acc = carry
                off = pl.multiple_of(qb * TA, TA)
                qh = q_ref[pl.ds(off, TA), cs]
                doh = do_ref[pl.ds(off, TA), :]
                ls = jnp.tile(lse_ref[pl.ds(off, TA), cs], (1, TA // LANE))
                dd = jnp.tile(D_ref[pl.ds(off, TA), cs], (1, TA // LANE))
                s = _nt(qh, kh) * MLA_SCALE
                if masked:
                    s = jnp.where(rowi >= coli, s, NEG)
                p = jnp.exp(s - ls)
                dp = _nt(doh, vm)
                ds = (p * (dp - dd) * MLA_SCALE).astype(BF16)
                dv_acc = dv_acc + _tn(p.astype(BF16), doh)
                dk_acc = dk_acc + _tn(ds, qh)
                dq_ref[pl.ds(off, TA), cs] += _nn(ds, kh)
                return dk_acc, dv_acc

            z = jnp.zeros((TA, LANE), F32)
            carry = step(ki, (z, z), True)
            dk_acc, dv_acc = lax.fori_loop(ki + 1, nq, lambda qb, c: step(qb, c, False), carry)
            dk_ref[:, cs] = dk_acc
            dv_tot = dv_tot + jnp.where(hm, dv_acc, 0.0)
        dv_ref[...] = dv_tot

    pair = pl.BlockSpec((S, 256), lambda p, i: (0, p))
    return pl.pallas_call(
        body, name="mla_attn_bwd", grid=(4, nq),
        in_specs=[pair, pl.BlockSpec((S, LANE), lambda p, i: (0, p)), pair, pair,
                  pl.BlockSpec((TA, 256), lambda p, i: (i, p)), pl.BlockSpec((TA, LANE), lambda p, i: (i, p))],
        out_specs=[pair, pl.BlockSpec((TA, 256), lambda p, i: (i, p)), pl.BlockSpec((TA, LANE), lambda p, i: (i, p))],
        out_shape=[jax.ShapeDtypeStruct((S, 1024), F32), jax.ShapeDtypeStruct((S, 1024), F32), jax.ShapeDtypeStruct((S, 512), F32)],
        compiler_params=_cp(dimension_semantics=("parallel", "arbitrary")),
    )(q, do, lse, Dr, k, v)


def _mla_pre_bwd(zp, dq, dk, dv, w, tab, dz):
    S = zp.shape[0]
    T = T_MLA

    def body(cq_ref, ckv_ref, kr_ref, dq_ref, dk_ref, dv_ref, gcq_ref, gckv_ref, wuq_ref, wuk_ref, wuv_ref, gq_ref, gk_ref,
             C_ref, S1_ref, S2_ref, dz_in, dz_ref, dwuq_ref, dwuk_ref, dwuv_ref, dgcq_ref, dgckv_ref, dgq_ref, dgk_ref):
        del dz_in
        i = pl.program_id(0)

        @pl.when(i == 0)
        def _():
            for r in (dwuq_ref, dwuk_ref, dwuv_ref, dgcq_ref, dgckv_ref, dgq_ref, dgk_ref):
                r[...] = jnp.zeros_like(r)

        cq = cq_ref[...]
        rq = lax.rsqrt(jnp.mean(cq * cq, axis=-1, keepdims=True) + EPS)
        cqh = cq * rq
        cqn = (cqh * gcq_ref[...]).astype(BF16)
        ckv = ckv_ref[...]
        rkv = lax.rsqrt(jnp.mean(ckv * ckv, axis=-1, keepdims=True) + EPS)
        ckvh = ckv * rkv
        ckvn = (ckvh * gckv_ref[...]).astype(BF16)
        q0 = _nn(cqn, wuq_ref[...])
        k0 = _nn(ckvn, wuk_ref[...])
        krb = kr_ref[...]
        C, S1, S2 = C_ref[...], S1_ref[...], S2_ref[...]
        gq, gk = gq_ref[...], gk_ref[...]

        def head_bwd(x, dy, g):
            r = lax.rsqrt(_rsum_mxu(x * x) * (1.0 / MLA_QK) + EPS)
            xn = x * r
            dyn = _rope_t(dy, C, S1, S2, 16)
            dxh = dyn * g
            return r * (dxh - xn * _rsum_mxu(dxh * xn) * (1.0 / MLA_QK)), _csum(dyn * xn)

        dq0, dk0 = [], []
        dgq_acc = jnp.zeros((1, LANE), F32)
        dgk_acc = jnp.zeros((1, LANE), F32)
        dkr = jnp.zeros((T, LANE), F32)
        for h in range(8):
            sl = slice(h * LANE, (h + 1) * LANE)
            dxq, gq_p = head_bwd(q0[:, sl], dq_ref[:, sl], gq)
            dxk, gk_p = head_bwd(k0[:, sl] + krb, dk_ref[:, sl], gk)
            dq0.append(dxq.astype(BF16))
            dk0.append(dxk.astype(BF16))
            dkr = dkr + dxk
            dgq_acc = dgq_acc + gq_p
            dgk_acc = dgk_acc + gk_p
        dgq_ref[...] += dgq_acc
        dgk_ref[...] += dgk_acc
        dq0 = jnp.concatenate(dq0, axis=1)
        dk0 = jnp.concatenate(dk0, axis=1)
        dvb = dv_ref[...].astype(BF16)
        dwuq_ref[...] += _tn(cqn, dq0)
        dwuk_ref[...] += _tn(ckvn, dk0)
        dwuv_ref[...] += _tn(ckvn, dvb)
        dcqn = _nt(dq0, wuq_ref[...])
        dckvn = _nt(dk0, wuk_ref[...]) + _nt(dvb, wuv_ref[...])
        dgcq_ref[...] += _csum(dcqn * cqh)
        dgckv_ref[...] += _csum(dckvn * ckvh)
        dxh = dcqn * gcq_ref[...]
        dz_ref[:, 0:256] = (rq * (dxh - cqh * jnp.mean(dxh * cqh, axis=-1, keepdims=True))).astype(BF16)
        dxh = dckvn * gckv_ref[...]
        dz_ref[:, 256:384] = (rkv * (dxh - ckvh * jnp.mean(dxh * ckvh, axis=-1, keepdims=True))).astype(BF16)
        lane = _lane((T, LANE))
        dz_ref[:, 384:512] = jnp.where((lane >= KR_LANE) & (lane < KR_LANE + 32), dkr, 0.0).astype(BF16)

    tabspec = pl.BlockSpec((T, LANE), lambda i: (i, 0))
    in_specs = [_zcol(T, 256, C_CQ), _zcol(T, LANE, C_CKV), _zcol(T, LANE, C_KR),
                pl.BlockSpec((T, 1024), lambda i: (i, 0)), pl.BlockSpec((T, 1024), lambda i: (i, 0)), pl.BlockSpec((T, 512), lambda i: (i, 0)),
                _full((1, 256)), _full((1, LANE)), _full((256, 1024)), _full((LANE, 1024)), _full((LANE, 512)), _full((1, LANE)), _full((1, LANE)),
                tabspec, tabspec, tabspec, pl.BlockSpec(memory_space=pl.ANY)]
    out_specs = [_zcol(T, 512, C_CQ), _full((256, 1024)), _full((LANE, 1024)), _full((LANE, 512)), _full((1, 256)), _full((1, LANE)),
                 _full((1, LANE)), _full((1, LANE))]
    out_shape = [jax.ShapeDtypeStruct(dz.shape, BF16), jax.ShapeDtypeStruct((256, 1024), F32), jax.ShapeDtypeStruct((LANE, 1024), F32),
                 jax.ShapeDtypeStruct((LANE, 512), F32), jax.ShapeDtypeStruct((1, 256), F32), jax.ShapeDtypeStruct((1, LANE), F32),
                 jax.ShapeDtypeStruct((1, LANE), F32), jax.ShapeDtypeStruct((1, LANE), F32)]
    return pl.pallas_call(
        body, name="mla_pre_bwd", grid=(S // T,), in_specs=in_specs, out_specs=out_specs, out_shape=out_shape,
        input_output_aliases={16: 0}, compiler_params=_cp(),
    )(zp, zp, zp, dq, dk, dv, w["g_cq"], w["g_ckv"], w["w_uq"], w["w_uk"], w["w_uv"], w["g_mq"], w["g_mk"], *tab, dz)


T_DIL = 512


def _head_stats(x, lane):
    sq = x * x
    sa = _rsum(jnp.where(lane < 64, sq, 0.0))
    sb = _rsum(jnp.where(lane >= 64, sq, 0.0))
    return lax.rsqrt(jnp.where(lane < 64, sa, sb) * (1.0 / DIL_HD) + EPS)


def _head_sum(x, lane):
    sa = _rsum(jnp.where(lane < 64, x, 0.0))
    sb = _rsum(jnp.where(lane >= 64, x, 0.0))
    return jnp.where(lane < 64, sa, sb)


def _head_stats_mxu(x):
    r = lax.broadcasted_iota(jnp.int32, (LANE, LANE), 0)
    c = lax.broadcasted_iota(jnp.int32, (LANE, LANE), 1)
    ones = jnp.where((r < 64) == (c < 64), 1.0, 0.0).astype(F32)
    ss = lax.dot_general(x * x, ones, (((1,), (0,)), ((), ())), precision=lax.Precision.HIGHEST, preferred_element_type=F32)
    return lax.rsqrt(ss * (1.0 / DIL_HD) + EPS)


def _dil_pre_fwd(zp, w, tab):
    S = zp.shape[0]
    T = T_DIL

    def body(q_ref, k_ref, gq_ref, gk_ref, C_ref, S1_ref, S2_ref, qo_ref, ko_ref):
        C, S1, S2 = C_ref[...], S1_ref[...], S2_ref[...]
        for b in range(12):
            sl = slice(b * LANE, (b + 1) * LANE)
            x = q_ref[:, sl]
            qo_ref[:, sl] = _rope(x * _head_stats_mxu(x) * gq_ref[...], C, S1, S2, 32)
            x = k_ref[:, sl]
            ko_ref[:, sl] = _rope(x * _head_stats_mxu(x) * gk_ref[...], C, S1, S2, 32)

    tabspec = pl.BlockSpec((T, LANE), lambda i: (i, 0))
    out = pl.BlockSpec((T, 1536), lambda i: (i, 0))
    return pl.pallas_call(
        body, name="dil_pre_fwd", grid=(S // T,),
        in_specs=[_zcol(T, 1536, C_DQ), _zcol(T, 1536, C_DK), _full((1, LANE)), _full((1, LANE)), tabspec, tabspec, tabspec],
        out_specs=[out, out], out_shape=[jax.ShapeDtypeStruct((S, 1536), F32)] * 2, compiler_params=_cp(),
    )(zp, zp, w["g_dq"], w["g_dk"], *tab)


DIL_ROWS = 2048


def _dil_geometry(g, S):
    d = DIL_DILATIONS[g]
    P = NK * d
    return d, P, DIL_ROWS // P, S // P


def _dil_rows(start, d, blocks=1):
    return pl.ds(pl.multiple_of(start, NK), blocks * NK) if d == 1 else pl.ds(start, blocks * NK, stride=d)


def _dil_specs(g, S, col0):
    _, P, m, nb = _dil_geometry(g, S)
    cur = pl.BlockSpec((DIL_ROWS, LANE), lambda sb, c: (sb, col0 + c))
    prv = pl.BlockSpec((P, LANE), lambda sb, c: (jnp.maximum(sb * m - 1, 0), col0 + c))
    nxt = pl.BlockSpec((P, LANE), lambda sb, c: (jnp.minimum((sb + 1) * m, nb - 1), col0 + c))
    return cur, prv, nxt


def _dil_attn_fwd(q, k, zp, g):
    S = q.shape[0]
    d, P, m, nb = _dil_geometry(g, S)
    R = DIL_ROWS

    def body(q_ref, kc_ref, kp_ref, vc_ref, vp_ref, o_ref, lse_ref, *scr):
        sb = pl.program_id(0)
        if m > 1:
            ks_ref, vs_ref = scr
            ks_ref[0:P, :] = kp_ref[...]
            ks_ref[P:P + R, :] = kc_ref[...]
            vs_ref[0:P, :] = vp_ref[...]
            vs_ref[P:P + R, :] = vc_ref[...]
        lane = _lane((NK, LANE))

        def unit(u, carry):
            j = u // d
            start = j * P + (u - j * d)
            rows = _dil_rows(start, d)
            if m > 1:
                k2, v2 = ks_ref[_dil_rows(start, d, 2), :], vs_ref[_dil_rows(start, d, 2), :]
            else:
                k2 = jnp.concatenate([kp_ref[rows, :], kc_ref[rows, :]], axis=0)
                v2 = jnp.concatenate([vp_ref[rows, :], vc_ref[rows, :]], axis=0)
            k2, v2 = k2.astype(BF16), v2.astype(BF16)
            q_ = q_ref[rows, :].astype(BF16)
            row = lax.broadcasted_iota(jnp.int32, (NK, 2 * NK), 0)
            col = lax.broadcasted_iota(jnp.int32, (NK, 2 * NK), 1)
            band = (col >= row) & (col <= row + NK) & ((col >= NK) | (sb * m + j > 0))
            lane2 = _lane((2 * NK, LANE))
            zb, zv = jnp.zeros_like(q_), jnp.zeros_like(v2)
            o_tot = jnp.zeros((NK, LANE), F32)
            lse_tot = jnp.zeros((NK, LANE), F32)
            for hh in range(2):
                hm = (lane < 64) if hh == 0 else (lane >= 64)
                hm2 = (lane2 < 64) if hh == 0 else (lane2 >= 64)
                s_ = jnp.where(band, _nt(jnp.where(hm, q_, zb), k2) * DIL_SCALE, NEG)
                mx = jnp.max(s_, axis=-1, keepdims=True)
                e = jnp.exp(s_ - mx)
                den = _rsum(e)
                o_tot = o_tot + _nn(e.astype(BF16), jnp.where(hm2, v2, zv)) / den
                lse_tot = jnp.where(hm, mx + jnp.log(den), lse_tot)
            o_ref[rows, :] = o_tot
            lse_ref[rows, :] = lse_tot
            return carry

        lax.fori_loop(0, R // NK, unit, 0, unroll=R // NK)

    qcur, qprv, _ = _dil_specs(g, S, 4 * g)
    vcur, vprv, _ = _dil_specs(g, S, C_DV + 4 * g)
    out = pl.BlockSpec((R, LANE), lambda sb, c: (sb, c))
    return pl.pallas_call(
        body, name=f"dil_attn_fwd{g}", grid=(S // R, 4), in_specs=[qcur, qcur, qprv, vcur, vprv], out_specs=[out, out],
        out_shape=[jax.ShapeDtypeStruct((S, 512), F32)] * 2,
        scratch_shapes=[pltpu.VMEM((P + R, LANE), F32)] * 2 if m > 1 else [], compiler_params=_cp(),
    )(q, k, k, zp, zp)


def _dil_combine(os_, ls_, zp):
    S = zp.shape[0]
    T = T_ROW

    def body(o0, o1, o2, l0, l1, l2, g_ref, oc_ref, L_ref, y_ref):
        a, b, c = l0[...], l1[...], l2[...]
        mx = jnp.maximum(jnp.maximum(a, b), c)
        ea, eb, ec = jnp.exp(a - mx), jnp.exp(b - mx), jnp.exp(c - mx)
        den = ea + eb + ec
        oc = (ea * o0[...] + eb * o1[...] + ec * o2[...]) / den
        oc_ref[...] = oc
        L_ref[...] = mx + jnp.log(den)
        y_ref[...] = (oc * _silu(g_ref[...])).astype(BF16)

    row = pl.BlockSpec((T, 512), lambda i: (i, 0))
    return pl.pallas_call(
        body, name="dil_combine", grid=(S // T,), in_specs=[row] * 6 + [_zcol(T, 512, C_DILG)], out_specs=[row, row, row],
        out_shape=[jax.ShapeDtypeStruct((S, 512), F32), jax.ShapeDtypeStruct((S, 512), F32), jax.ShapeDtypeStruct((S, 512), BF16)],
        compiler_params=_cp(),
    )(*os_, *ls_, zp)


def _dil_comb_bwd(zp, oc, dy, dz):
    S = zp.shape[0]
    T = T_ROW

    def body(g_ref, o_ref, dy_ref, dz_in, dz_ref, do_ref, D_ref):
        del dz_in
        g, o_, dy_ = g_ref[...], o_ref[...], dy_ref[...]
        do = dy_ * _silu(g)
        do_ref[...] = do
        dz_ref[...] = (dy_ * o_ * _dsilu(g)).astype(BF16)
        lane = _lane((T, LANE))
        for p in range(4):
            sl = slice(p * LANE, (p + 1) * LANE)
            D_ref[:, sl] = _head_sum(do[:, sl] * o_[:, sl], lane)

    row = pl.BlockSpec((T, 512), lambda i: (i, 0))
    zc = _zcol(T, 512, C_DILG)
    return pl.pallas_call(
        body, name="dil_comb_bwd", grid=(S // T,), in_specs=[zc, row, row, pl.BlockSpec(memory_space=pl.ANY)], out_specs=[zc, row, row],
        out_shape=[jax.ShapeDtypeStruct(dz.shape, BF16), jax.ShapeDtypeStruct((S, 512), F32), jax.ShapeDtypeStruct((S, 512), F32)],
        input_output_aliases={3: 0}, compiler_params=_cp(),
    )(zp, oc, dy, dz)


def _dil_attn_bwd(q, k, zp, do, L, Dr, g):
    S = q.shape[0]
    d, P, m, nb = _dil_geometry(g, S)
    R = DIL_ROWS
    n_q, n_k = 4, 2

    def body(*refs):
        q_side = refs[0:2 * n_q]
        k_side = refs[2 * n_q:2 * n_q + 2 * n_k]
        dq_ref, dk_ref, dv_ref = refs[2 * n_q + 2 * n_k:2 * n_q + 2 * n_k + 3]
        scr = refs[2 * n_q + 2 * n_k + 3:]
        sb = pl.program_id(0)
        if m > 1:
            for a in range(n_q):
                scr[a][0:R, :] = q_side[2 * a][...]
                scr[a][R:R + P, :] = q_side[2 * a + 1][...]
            for a in range(n_k):
                scr[n_q + a][0:P, :] = k_side[2 * a + 1][...]
                scr[n_q + a][P:P + R, :] = k_side[2 * a][...]
        lane = _lane((NK, LANE))

        def unit(u, carry):
            j = u // d
            start = j * P + (u - j * d)
            rows = _dil_rows(start, d)
            if m > 1:
                rows_b = _dil_rows(start + P, d)
                q2, do2, L2, D2 = [scr[a][_dil_rows(start, d, 2), :] for a in range(n_q)]
                kp, vp = [scr[n_q + a][rows, :] for a in range(n_k)]
                kc, vc = [scr[n_q + a][rows_b, :] for a in range(n_k)]
            else:
                q2, do2, L2, D2 = [jnp.concatenate([q_side[2 * a][rows, :], q_side[2 * a + 1][rows, :]], axis=0) for a in range(n_q)]
                kc, vc = [k_side[2 * a][rows, :] for a in range(n_k)]
                kp, vp = [k_side[2 * a + 1][rows, :] for a in range(n_k)]
            q2, do2 = q2.astype(BF16), do2.astype(BF16)
            kc, kp, vc, vp = kc.astype(BF16), kp.astype(BF16), vc.astype(BF16), vp.astype(BF16)
            n = sb * m + j
            hA = _lane((2 * NK, LANE)) < 64
            zq = jnp.zeros_like(q2)
            L2r, D2r = pltpu.roll(L2, 64, 1), pltpu.roll(D2, 64, 1)
            Q4 = jnp.concatenate([jnp.where(hA, q2, zq), jnp.where(hA, zq, q2)], axis=0)
            O4 = jnp.concatenate([jnp.where(hA, do2, zq), jnp.where(hA, zq, do2)], axis=0)
            L4 = jnp.concatenate([jnp.where(hA, L2, L2r), jnp.where(hA, L2r, L2)], axis=0)
            D4 = jnp.concatenate([jnp.where(hA, D2, D2r), jnp.where(hA, D2r, D2)], axis=0)
            row4 = lax.broadcasted_iota(jnp.int32, (4 * NK, NK), 0) & (2 * NK - 1)
            col4 = lax.broadcasted_iota(jnp.int32, (4 * NK, NK), 1)
            m4 = ((row4 < NK) & (col4 <= row4)) | ((row4 >= NK) & (col4 >= row4 - NK) & (n < nb - 1))
            p4 = jnp.exp(jnp.where(m4, _nt(Q4, kc) * DIL_SCALE, NEG) - L4)
            ds4 = (p4 * (_nt(O4, vc) - D4) * DIL_SCALE).astype(BF16)
            dk_tot = _tn(ds4, Q4)
            dv_tot = _tn(p4.astype(BF16), O4)
            pick = lambda x: jnp.concatenate([x[0:NK], x[2 * NK:3 * NK]], axis=0)
            Qn, On, Ln, Dn = pick(Q4), pick(O4), pick(L4), pick(D4)
            rowp = lax.broadcasted_iota(jnp.int32, (2 * NK, NK), 0) & (NK - 1)
            colp = lax.broadcasted_iota(jnp.int32, (2 * NK, NK), 1)
            pp = jnp.exp(jnp.where((colp >= rowp) & (n > 0), _nt(Qn, kp) * DIL_SCALE, NEG) - Ln)
            dsp = (pp * (_nt(On, vp) - Dn) * DIL_SCALE).astype(BF16)
            dq2 = _nn(pick(ds4), kc) + _nn(dsp, kp)
            dq_tot = jnp.where(lane < 64, dq2[0:NK], dq2[NK:2 * NK])
            dq_ref[rows, :] = dq_tot
            dk_ref[rows, :] = dk_tot
            dv_ref[rows, :] = dv_tot
            return carry

        lax.fori_loop(0, R // NK, unit, 0, unroll=R // NK)

    qcur, qprv, qnxt = _dil_specs(g, S, 4 * g)
    vcur, vprv, _ = _dil_specs(g, S, C_DV + 4 * g)
    ocur, _, onxt = _dil_specs(g, S, 0)
    out = pl.BlockSpec((R, LANE), lambda sb, c: (sb, c))
    scratch = [pltpu.VMEM((P + R, LANE), F32)] * (n_q + n_k) if m > 1 else []
    return pl.pallas_call(
        body, name=f"dil_attn_bwd{g}", grid=(S // R, 4),
        in_specs=[qcur, qnxt, ocur, onxt, ocur, onxt, ocur, onxt, qcur, qprv, vcur, vprv],
        out_specs=[out, out, out], out_shape=[jax.ShapeDtypeStruct((S, 512), F32)] * 3, scratch_shapes=scratch, compiler_params=_cp(),
    )(q, q, do, do, L, L, Dr, Dr, k, k, zp, zp)


def _dil_pre_bwd(zp, dys, g, tab, dz, col, name):
    S = zp.shape[0]
    T = T_DIL

    def body(x_ref, dy0_ref, dy1_ref, dy2_ref, g_ref, C_ref, S1_ref, S2_ref, dz_in, dz_ref, dg_ref):
        del dz_in
        i = pl.program_id(0)
        C, S1, S2 = C_ref[...], S1_ref[...], S2_ref[...]
        lane = _lane((T, LANE))
        gv = g_ref[...]
        acc = jnp.zeros((1, LANE), F32)
        for b in range(12):
            sl = slice(b * LANE, (b + 1) * LANE)
            x = x_ref[:, sl]
            r = _head_stats(x, lane)
            xn = x * r
            dy_ref = (dy0_ref, dy1_ref, dy2_ref)[b // 4]
            dyn = _rope_t(dy_ref[:, (b % 4) * LANE:(b % 4 + 1) * LANE], C, S1, S2, 32)
            acc = acc + _csum(dyn * xn)
            dxh = dyn * gv
            dz_ref[:, sl] = (r * (dxh - xn * _head_sum(dxh * xn, lane) * (1.0 / DIL_HD))).astype(BF16)

        @pl.when(i == 0)
        def _():
            dg_ref[...] = acc

        @pl.when(i > 0)
        def _():
            dg_ref[...] += acc

    tabspec = pl.BlockSpec((T, LANE), lambda i: (i, 0))
    zc = _zcol(T, 1536, col)
    grp = pl.BlockSpec((T, 512), lambda i: (i, 0))
    return pl.pallas_call(
        body, name=name, grid=(S // T,),
        in_specs=[zc, grp, grp, grp, _full((1, LANE)), tabspec, tabspec, tabspec, pl.BlockSpec(memory_space=pl.ANY)],
        out_specs=[zc, _full((1, LANE))], out_shape=[jax.ShapeDtypeStruct(dz.shape, BF16), jax.ShapeDtypeStruct((1, LANE), F32)],
        input_output_aliases={8: 0}, compiler_params=_cp(),
    )(zp, *dys, g, *tab, dz)


def _dil_dv_into(dvs, dz):
    S = dz.shape[0]
    T = T_ROW

    def body(s0, s1, s2, dz_in, o_ref):
        del dz_in
        for gi, s in enumerate((s0, s1, s2)):
            o_ref[:, gi * 512:(gi + 1) * 512] = s[...].astype(BF16)

    grp = pl.BlockSpec((T, 512), lambda i: (i, 0))
    return pl.pallas_call(
        body, name="dil_dv", grid=(S // T,), in_specs=[grp, grp, grp, pl.BlockSpec(memory_space=pl.ANY)],
        out_specs=_zcol(T, 1536, C_DV), out_shape=jax.ShapeDtypeStruct(dz.shape, BF16), input_output_aliases={3: 0}, compiler_params=_cp(),
    )(*dvs, dz)


T_MRG = 256


def _merge_fwd(P, zp, b_merge):
    S = zp.shape[0]
    T = T_MRG

    def body(p0, p1, p2, m0, m1, m2, b_ref, o_ref):
        acc = jnp.zeros((T, D), F32)
        for j, (p, m) in enumerate(((p0, m0), (p1, m1), (p2, m2))):
            acc = acc + _sig(m[...] + b_ref[:, j * D:(j + 1) * D]) * p[...].astype(F32)
        o_ref[...] = acc.astype(BF16)

    row = pl.BlockSpec((T, D), lambda i: (i, 0))
    return pl.pallas_call(
        body, name="merge_fwd", grid=(S // T,),
        in_specs=[row, row, row] + [_zcol(T, D, C_MERGE + 8 * j) for j in range(3)] + [_full((1, 3 * D))], out_specs=row,
        out_shape=jax.ShapeDtypeStruct((S, D), BF16), compiler_params=_cp(),
    )(*P, zp, zp, zp, b_merge)


def _merge_bwd(dm, Pj, zp, bj, dz, j):
    S = zp.shape[0]
    T = T_MRG

    def body(dm_ref, p_ref, m_ref, b_ref, dz_in, dz_ref, dp_ref, db_ref):
        del dz_in
        i = pl.program_id(0)
        g = _sig(m_ref[...] + b_ref[...])
        dmv = dm_ref[...].astype(F32)
        dp_ref[...] = (dmv * g).astype(BF16)
        dg = dmv * p_ref[...].astype(F32) * g * (1.0 - g)
        dz_ref[...] = dg.astype(BF16)
        part = _csum(dg)

        @pl.when(i == 0)
        def _():
            db_ref[...] = part

        @pl.when(i > 0)
        def _():
            db_ref[...] += part

    row = pl.BlockSpec((T, D), lambda i: (i, 0))
    zc = _zcol(T, D, C_MERGE + 8 * j)
    return pl.pallas_call(
        body, name=f"merge_bwd{j}", grid=(S // T,), in_specs=[row, row, zc, _full((1, D)), pl.BlockSpec(memory_space=pl.ANY)],
        out_specs=[zc, row, _full((1, D))],
        out_shape=[jax.ShapeDtypeStruct(dz.shape, BF16), jax.ShapeDtypeStruct((S, D), BF16), jax.ShapeDtypeStruct((1, D), F32)],
        input_output_aliases={4: 0}, compiler_params=_cp(),
    )(dm, Pj, zp, bj, dz)


def _loss_fwd_bwd(y, target):
    S = y.shape[0]
    T = T_ROW

    def body(y_ref, t_ref, loss_ref, dy_ref):
        i = pl.program_id(0)
        err = y_ref[...] - t_ref[...]
        dy_ref[...] = err * (1.0 / D)
        part = jnp.sum(err * err, keepdims=True).reshape(1, 1) * (0.5 / D)

        @pl.when(i == 0)
        def _():
            loss_ref[...] = part

        @pl.when(i > 0)
        def _():
            loss_ref[...] += part

    row = pl.BlockSpec((T, D), lambda i: (i, 0))
    return pl.pallas_call(
        body, name="loss", grid=(S // T,), in_specs=[row, row], out_specs=[_full((1, 1)), row],
        out_shape=[jax.ShapeDtypeStruct((1, 1), F32), jax.ShapeDtypeStruct((S, D), F32)], compiler_params=_cp(),
    )(y, target)


def _layer_fwd(x, w, tabs):
    mla_tab, dil_tab = tabs
    S = x.shape[0]
    h = _rms_in_fwd(x, w["norm_g"])
    zp = _mm(h, w["w_in"], mode="nn", name="in_proj")
    hs, y_lru = _lru_fwd(zp, w)
    q, k, v = _mla_pre_fwd(zp, w, mla_tab)
    o_mla, lse, y_mla = _mla_attn_fwd(q, k, v, zp)
    qd, kd = _dil_pre_fwd(zp, w, dil_tab)
    og, lg = zip(*[_dil_attn_fwd(qd, kd, zp, g) for g in range(len(DIL_DILATIONS))])
    oc, L, y_dil = _dil_combine(og, lg, zp)
    P = [_mm(y_lru, w["w_lru_o"], mode="nn", name="lru_out", out_dtype=BF16),
         _mm(y_mla, w["w_mla_o"], mode="nn", name="mla_out", out_dtype=BF16),
         _mm(y_dil, w["w_dil_o"], mode="nn", name="dil_out", out_dtype=BF16)]
    merged = _merge_fwd(P, zp, w["b_merge"])
    x_out = _mm(merged, w["w_out"], mode="nn", name="out_proj", add=x)
    saved = dict(x=x, h=h, zp=zp, hs=hs, y=(y_lru, y_mla, y_dil), q=q, k=k, v=v, o_mla=o_mla, lse=lse, qd=qd, kd=kd, oc=oc, L=L, P=P,
                 merged=merged)
    return x_out, saved


def _layer_bwd(dout, w, tabs, sv, hook=None, after=None):
    mla_tab, dil_tab = tabs
    zp = sv["zp"]
    S = zp.shape[0]
    g = {}
    dm = _mm(dout, w["w_out"], mode="nt", name="d_merged", after=after, out_dtype=BF16)
    g["w_out"] = _mm(sv["merged"], dout, mode="tn", name="dw_out", out_dtype=BF16)
    dz = lax.empty((S, ZW), BF16)
    dP, db = [], []
    for j in range(3):
        dz, dpj, dbj = _merge_bwd(dm, sv["P"][j], zp, w["b_merge"][:, j * D:(j + 1) * D], dz, j)
        dP.append(dpj)
        db.append(dbj)
    g["b_merge"] = jnp.concatenate(db, axis=1)
    names = ("w_lru_o", "w_mla_o", "w_dil_o")
    dy = []
    for j in range(3):
        dy.append(_mm(dP[j], w[names[j]], mode="nt", name="dy_" + names[j]))
        g[names[j]] = _mm(sv["y"][j], dP[j], mode="tn", name="d" + names[j], out_dtype=BF16)
    dz = _lru_gate_bwd(zp, sv["hs"], dy[0], dz)
    dz, g["conv_w"], g["conv_b"], g["w_gx"], g["b_gx"], g["w_ga"], g["b_ga"], g["lam"] = _lru_bwd(zp, sv["hs"], dy[0], w, dz)
    dz, do, Dr = _mla_post_bwd(zp, sv["o_mla"], dy[1], dz)
    dq, dk, dv = _mla_attn_bwd(sv["q"], sv["k"], sv["v"], do, sv["lse"], Dr)
    dz, g["w_uq"], g["w_uk"], g["w_uv"], g["g_cq"], g["g_ckv"], g["g_mq"], g["g_mk"] = _mla_pre_bwd(zp, dq, dk, dv, w, mla_tab, dz)
    dz, dod, Dd = _dil_comb_bwd(zp, sv["oc"], dy[2], dz)
    dqs, dks, dvs = zip(*[_dil_attn_bwd(sv["qd"], sv["kd"], zp, dod, sv["L"], Dd, gi) for gi in range(len(DIL_DILATIONS))])
    dz, g["g_dq"] = _dil_pre_bwd(zp, dqs, w["g_dq"], dil_tab, dz, C_DQ, "dil_pre_bwd_q")
    dz, g["g_dk"] = _dil_pre_bwd(zp, dks, w["g_dk"], dil_tab, dz, C_DK, "dil_pre_bwd_k")
    dz = _dil_dv_into(dvs, dz)
    g["w_in"] = _mm(sv["h"], dz, mode="tn", name="dw_in", out_dtype=BF16, tk=S)
    token = hook(g) if hook is not None else None
    dh = _mm(dz, w["w_in"], mode="nt", name="d_h", after=token, tk=ZW // 4)
    dx, g["norm_g"] = _rms_in_bwd(sv["x"], w["norm_g"], dh, dout)
    return dx, g


def _peers():
    mx, my, mc = lax.axis_index("x"), lax.axis_index("y"), lax.axis_index("c")
    me = 4 * mx + 2 * my + mc
    out = []
    for k in range(1, N_DEV):
        px = 1 - mx if k & 4 else mx
        py = 1 - my if k & 2 else my
        pc = 1 - mc if k & 1 else mc
        out.append(((px, py, pc), 4 * px + 2 * py + pc))
    return me, out


def _whole(ref, p):
    del p
    return ref


def _exchange(srcs, slicers, slices, name):
    n = len(srcs)

    def body(*refs):
        ins, outs = refs[:n], refs[n:2 * n]
        send_sems, recv_sems, local_sems = refs[2 * n:]
        me, peers = _peers()
        mine = [pltpu.make_async_copy(slicers[a](ins[a], me), outs[a].at[me], local_sems.at[a]) for a in range(n)]
        for cp in mine:
            cp.start()
        copies = []
        for k, (peer, pidx) in enumerate(peers):
            for a in range(n):
                cp = pltpu.make_async_remote_copy(
                    src_ref=slicers[a](ins[a], pidx), dst_ref=outs[a].at[me], send_sem=send_sems.at[k * n + a],
                    recv_sem=recv_sems.at[k * n + a], device_id=peer, device_id_type=pl.DeviceIdType.MESH)
                cp.start()
                copies.append(cp)
        for cp in copies + mine:
            cp.wait()

    nsem = (N_DEV - 1) * n
    return pl.pallas_call(
        body, name=name, out_shape=[jax.ShapeDtypeStruct((N_DEV,) + shp, dt) for shp, dt in slices],
        in_specs=[pl.BlockSpec(memory_space=pl.ANY)] * n, out_specs=[pl.BlockSpec(memory_space=pl.ANY)] * n,
        scratch_shapes=[pltpu.SemaphoreType.DMA((nsem,)), pltpu.SemaphoreType.DMA((nsem,)), pltpu.SemaphoreType.DMA((n,))],
        compiler_params=pltpu.CompilerParams(has_side_effects=True),
    )(*srcs)


def _gather_two_level(srcs, name):
    n = len(srcs)

    def body(*refs):
        ins, outs = refs[:n], refs[n:2 * n]
        send_sems, recv_sems, local_sems = refs[2 * n:]
        mx, my, mc = lax.axis_index("x"), lax.axis_index("y"), lax.axis_index("c")
        me, sibling = (mx, my, mc), (mx, my, 1 - mc)
        chips = [(1 - mx, my), (mx, 1 - my), (1 - mx, 1 - my)]
        slot = lambda d: 4 * d[0] + 2 * d[1] + d[2]

        def copy(j, a, block, to, own=False):
            return pltpu.make_async_remote_copy(
                src_ref=ins[a] if own else outs[a].at[slot(block)], dst_ref=outs[a].at[slot(block)],
                send_sem=send_sems.at[j * n + a], recv_sem=recv_sems.at[j * n + a], device_id=to, device_id_type=pl.DeviceIdType.MESH)

        mine = [pltpu.make_async_copy(ins[a], outs[a].at[slot(me)], local_sems.at[a]) for a in range(n)]
        first = [copy(1 + j, a, me, (*chip, mc), own=True) for j, chip in enumerate(chips) for a in range(n)]
        first += [copy(0, a, me, sibling, own=True) for a in range(n)]
        for cp in mine + first:
            cp.start()
        passed = []
        for j, chip in enumerate(chips):
            for a in range(n):
                copy(1 + j, a, (*chip, mc), me).wait_recv()
                cp = copy(4 + j, a, (*chip, mc), sibling)
                cp.start()
                passed.append(cp)
        for a in range(n):
            copy(0, a, sibling, me).wait_recv()
        for j, chip in enumerate(chips):
            for a in range(n):
                copy(4 + j, a, (*chip, 1 - mc), me).wait_recv()
        for cp in first + passed:
            cp.wait_send()
        for cp in mine:
            cp.wait()

    nsem = (N_DEV - 1) * n
    return pl.pallas_call(
        body, name=name, out_shape=[jax.ShapeDtypeStruct((N_DEV,) + a.shape, a.dtype) for a in srcs],
        in_specs=[pl.BlockSpec(memory_space=pl.ANY)] * n, out_specs=[pl.BlockSpec(memory_space=pl.ANY)] * n,
        scratch_shapes=[pltpu.SemaphoreType.DMA((nsem,)), pltpu.SemaphoreType.DMA((nsem,)), pltpu.SemaphoreType.DMA((n,))],
        compiler_params=pltpu.CompilerParams(has_side_effects=True),
    )(*srcs)


_HBM = pl.BlockSpec(memory_space=pltpu.HBM)
_SEM = pl.BlockSpec(memory_space=pltpu.SEMAPHORE)
_DATAFLOW = pltpu.SideEffectType.DATAFLOW_SIDE_EFFECTING


def _plan_chips():
    mx, my, mc = lax.axis_index("x"), lax.axis_index("y"), lax.axis_index("c")
    return 2 * mx + my, [((cx, cy, mc), 2 * cx + cy) for cx, cy in ((1 - mx, my), (mx, 1 - my), (1 - mx, 1 - my))]


def _pair_exchange(srcs, slicers, slices, sliced, name):
    n = len(srcs)
    pieces = [4 if s else 1 for s in sliced]

    def body(*refs):
        ins, outs = refs[:n], refs[n:2 * n]
        send_sems, recv_sems = refs[2 * n:]
        mx, my, mc = lax.axis_index("x"), lax.axis_index("y"), lax.axis_index("c")
        copies = []
        for a in range(n):
            for q in range(pieces[a]):
                i = len(copies)
                copies.append(pltpu.make_async_remote_copy(
                    src_ref=slicers[a](ins[a], 2 * q + 1 - mc) if sliced[a] else ins[a], dst_ref=outs[a].at[q],
                    send_sem=send_sems.at[i], recv_sem=recv_sems.at[i], device_id=(mx, my, 1 - mc), device_id_type=pl.DeviceIdType.MESH))
        for cp in copies:
            cp.start()
        for cp in copies:
            cp.wait()

    return pl.pallas_call(
        body, name=name, out_shape=[jax.ShapeDtypeStruct((p,) + shp, dt) for (shp, dt), p in zip(slices, pieces)],
        in_specs=[pl.BlockSpec(memory_space=pl.ANY)] * n, out_specs=[pl.BlockSpec(memory_space=pl.ANY)] * n,
        scratch_shapes=[pltpu.SemaphoreType.DMA((sum(pieces),)), pltpu.SemaphoreType.DMA((sum(pieces),))],
        compiler_params=pltpu.CompilerParams(has_side_effects=True),
    )(*srcs)


def _pair_add(src, came, first_blk, axis, name):
    _, r, c = came.shape
    nblk = (c if axis == 1 else r) // LANE
    if axis == 1:
        s_spec = pl.BlockSpec((r, LANE), lambda q, j, fb: (0, fb[q] + j))
        o_spec = pl.BlockSpec((1, r, LANE), lambda q, j, fb: (q, 0, j))
    else:
        s_spec = pl.BlockSpec((LANE, c), lambda q, j, fb: (fb[q] + j, 0))
        o_spec = pl.BlockSpec((1, LANE, c), lambda q, j, fb: (q, j, 0))

    def body(fb_ref, x_ref, y_ref, o_ref):
        del fb_ref
        o_ref[0] = (x_ref[...].astype(F32) + y_ref[0].astype(F32)).astype(o_ref.dtype)

    return pl.pallas_call(
        body, name=name, out_shape=jax.ShapeDtypeStruct(came.shape, came.dtype),
        grid_spec=pltpu.PrefetchScalarGridSpec(num_scalar_prefetch=1, grid=(4, nblk), in_specs=[s_spec, o_spec], out_specs=o_spec),
        compiler_params=_cp(),
    )(first_blk, src, came)


def _add2(x, y, name):
    shp = x.shape
    x, y = x.reshape(-1, shp[-1]), y.reshape(-1, shp[-1])
    R, C = x.shape
    tr = R
    while tr * C * 4 > (1 << 21) and tr % 32 == 0:
        tr //= 2

    def body(x_ref, y_ref, o_ref):
        o_ref[...] = (x_ref[...].astype(F32) + y_ref[...].astype(F32)).astype(o_ref.dtype)

    spec = pl.BlockSpec((tr, C), lambda i: (i, 0))
    return pl.pallas_call(body, name=name, grid=(R // tr,), in_specs=[spec, spec], out_specs=spec,
                          out_shape=jax.ShapeDtypeStruct((R, C), x.dtype), compiler_params=_cp())(x, y).reshape(shp)


def _exchange_start(srcs, slicers, slices, after, name, plan=_peers, nslots=N_DEV):
    n = len(srcs)
    nsem = (nslots - 1) * n
    lands = [lax.empty((nslots,) + shp, dt) for shp, dt in slices]

    def body(*refs):
        ins, lands_in = refs[:n], refs[n:2 * n]
        send_sems, recv_sems, local_sems = refs[2 * n + 1], refs[2 * n + 2], refs[2 * n + 3]
        token = refs[-1]
        me, peers = plan()
        for a in range(n):
            pltpu.make_async_copy(slicers[a](ins[a], me), lands_in[a].at[me], local_sems.at[a]).start()
        for k, (peer, pidx) in enumerate(peers):
            for a in range(n):
                pltpu.make_async_remote_copy(
                    src_ref=slicers[a](ins[a], pidx), dst_ref=lands_in[a].at[me], send_sem=send_sems.at[k * n + a],
                    recv_sem=recv_sems.at[k * n + a], device_id=peer, device_id_type=pl.DeviceIdType.MESH).start()
        token[...] = jnp.zeros_like(token)

    hbm = lambda a: pltpu.with_memory_space_constraint(a, pltpu.HBM)
    return pl.pallas_call(
        body, name=name,
        out_shape=(pltpu.SemaphoreType.DMA((nsem,)), pltpu.SemaphoreType.DMA((nsem,)), pltpu.SemaphoreType.DMA((n,)),
                   *[pltpu.HBM(a.shape, a.dtype) for a in srcs], *[pltpu.HBM(a.shape, a.dtype) for a in lands],
                   jax.ShapeDtypeStruct((SUB, LANE), F32)),
        in_specs=[_HBM] * (2 * n) + [pl.BlockSpec(memory_space=pl.ANY)],
        out_specs=(_SEM, _SEM, _SEM, *[_HBM] * (2 * n), pl.BlockSpec(memory_space=pltpu.VMEM)),
        input_output_aliases={i: 3 + i for i in range(2 * n)},
        compiler_params=pltpu.CompilerParams(has_side_effects=_DATAFLOW),
    )(*[hbm(a) for a in srcs], *[hbm(a) for a in lands], after)


def _exchange_wait(started, slicers, after, name, plan=_peers):
    n = (len(started) - 4) // 2
    sems, thru = started[0:3], started[3:3 + 2 * n]

    def body(*refs):
        srcs, lands = refs[:n], refs[n:2 * n]
        send_sems, recv_sems, local_sems = refs[2 * n], refs[2 * n + 1], refs[2 * n + 2]
        me, peers = plan()
        for k, (peer, pidx) in enumerate(peers):
            for a in range(n):
                cp = pltpu.make_async_remote_copy(
                    src_ref=slicers[a](srcs[a], pidx), dst_ref=lands[a].at[me], send_sem=send_sems.at[k * n + a],
                    recv_sem=recv_sems.at[k * n + a], device_id=peer, device_id_type=pl.DeviceIdType.MESH)
                cp.wait_send()
                cp.wait_recv()
        for a in range(n):
            pltpu.make_async_copy(slicers[a](srcs[a], me), lands[a].at[me], local_sems.at[a]).wait()

    outs = pl.pallas_call(
        body, name=name, out_shape=[pltpu.HBM(a.shape, a.dtype) for a in thru],
        in_specs=[_HBM] * (2 * n) + [_SEM, _SEM, _SEM, pl.BlockSpec(memory_space=pl.ANY)], out_specs=[_HBM] * (2 * n),
        input_output_aliases={i: i for i in range(2 * n)}, compiler_params=pltpu.CompilerParams(has_side_effects=_DATAFLOW),
    )(*thru, *sems, after)
    return outs[n:]


WIN = 13 * LANE


def _win_base(s):
    n = s * SHARD_IN
    a0 = n + jnp.where(n >= _KR0, KR_LANE, 0) + jnp.where(n >= _KR0 + 32, 32, 0)
    return jnp.minimum(a0 // LANE, (ZW - WIN) // LANE)


def _win_offsets(s):
    n = s * SHARD_IN + jnp.arange(SHARD_IN)
    o = s * SHARD_IN - _win_base(s) * LANE
    return n, (o, o + KR_LANE, o + LANE - 32)


def _to_window(shard, s):
    _, offs = _win_offsets(s)
    padded = jnp.pad(shard, ((0, 0), (0, 0), (WIN, WIN)))
    a, b, c = [lax.dynamic_slice(padded, (0, 0, WIN - o), shard.shape[:2] + (WIN,)) for o in offs]
    col = (_win_base(s) * LANE + jnp.arange(WIN))[None, None, :]
    zero = jnp.zeros_like(a)
    return jnp.where(col < _KR0, a, jnp.where((col >= _KR0 + KR_LANE) & (col < _KR0 + KR_LANE + 32), b, jnp.where(col >= _KR0 + LANE, c, zero)))


def _from_window(win, s):
    n, offs = _win_offsets(s)
    a, b, c = [lax.dynamic_slice(win, (0, 0, o), win.shape[:2] + (SHARD_IN,)) for o in offs]
    return jnp.where((n < _KR0)[None, None, :], a, jnp.where((n < _KR0 + 32)[None, None, :], b, c))


def _win_base_static(s):
    n = s * SHARD_IN
    a0 = n + (KR_LANE if n >= _KR0 else 0) + (32 if n >= _KR0 + 32 else 0)
    return min(a0 // LANE, (ZW - WIN) // LANE)


def _assemble_w_in(gw):
    tr = 128
    bases = [_win_base_static(s) for s in range(N_DEV)]

    def body(g_ref, o_ref):
        for j in range(ZW // LANE):
            acc = None
            for s in range(N_DEV):
                if bases[s] <= j < bases[s] + WIN // LANE:
                    piece = g_ref[s, :, (j - bases[s]) * LANE:(j - bases[s] + 1) * LANE]
                    acc = piece if acc is None else acc + piece
            o_ref[:, j * LANE:(j + 1) * LANE] = acc

    return pl.pallas_call(
        body, name="assemble_w_in", grid=(D // tr,), in_specs=[pl.BlockSpec((N_DEV, tr, WIN), lambda i: (0, i, 0))],
        out_specs=pl.BlockSpec((tr, ZW), lambda i: (i, 0)), out_shape=jax.ShapeDtypeStruct((D, ZW), gw.dtype), compiler_params=_cp(),
    )(gw)


def _cols(width):
    return lambda ref, p: ref.at[:, pl.ds(pl.multiple_of(p * width, width), width)]


def _rows(height):
    return lambda ref, p: ref.at[pl.ds(pl.multiple_of(p * height, height), height), :]


SCATTER = {
    'w_in': (lambda ref, p: ref.at[:, pl.ds(pl.multiple_of(_win_base(p) * LANE, LANE), WIN)], (D, WIN), BF16),
    'conv_w': (_cols(LANE), (4, LANE), F32),
    'w_lru_o': (_rows(LANE), (LANE, D), BF16),
    'w_uq': (_cols(LANE), (256, LANE), F32),
    'w_ukv': (_cols(LANE), (128, LANE), F32),
    'w_mla_o': (_cols(LANE), (512, LANE), BF16),
    'w_dil_o': (_cols(LANE), (512, LANE), BF16),
    'w_out': (_rows(LANE), (LANE, D), BF16),
}
SLICED_AXIS = {'w_in': 1, 'conv_w': 1, 'w_lru_o': 0, 'w_uq': 1, 'w_ukv': 1, 'w_mla_o': 1, 'w_dil_o': 1, 'w_out': 0}


PACK_ROWS = 64


def _packed_rows(shapes):
    n = sum(int(np.prod(s)) for s in shapes)
    return -(-n // (PACK_ROWS * LANE)) * PACK_ROWS


def _sum8(buf, name):
    ns, R, C = buf.shape
    tr = R
    while tr * C * 4 * ns > (1 << 22) and tr % 32 == 0:
        tr //= 2

    def body(b_ref, o_ref):
        acc = b_ref[0].astype(F32)
        for s in range(1, ns):
            acc = acc + b_ref[s].astype(F32)
        o_ref[...] = acc

    return pl.pallas_call(
        body, name=name, grid=(R // tr,), in_specs=[pl.BlockSpec((ns, tr, C), lambda i: (0, i, 0))],
        out_specs=pl.BlockSpec((tr, C), lambda i: (i, 0)), out_shape=jax.ShapeDtypeStruct((R, C), F32), compiler_params=_cp(),
    )(buf)


def _pack(arrs, dtype, lead):
    flat = [a.astype(dtype).reshape(a.shape[:lead] + (-1,)) for a in arrs]
    cat = jnp.concatenate(flat, axis=-1)
    n = cat.shape[-1]
    unit = PACK_ROWS * LANE
    pad = (-n) % unit
    if pad:
        cat = jnp.pad(cat, [(0, 0)] * lead + [(0, pad)])
    return cat.reshape(cat.shape[:lead] + ((n + pad) // LANE, LANE))


def _unpack(buf, shapes, lead):
    flat = buf.reshape(buf.shape[:lead] + (-1,))
    out, off = [], 0
    for shp in shapes:
        n = int(np.prod(shp))
        out.append(flat[..., off:off + n].reshape(buf.shape[:lead] + tuple(shp)))
        off += n
    return out


def _adamw(w, g, m, v, name):
    layers, rows, cols = w.shape
    tr = rows
    while tr * cols * 4 > (3 << 19) and tr % 16 == 0:
        tr //= 2
    c1 = 1.0 - ADAM_B1 ** ADAM_STEP
    c2 = 1.0 - ADAM_B2 ** ADAM_STEP

    def body(w_ref, g_ref, m_ref, v_ref, d_ref, mo_ref, vo_ref):
        gv = g_ref[...]
        mn = ADAM_B1 * m_ref[...] + (1.0 - ADAM_B1) * gv
        vn = ADAM_B2 * v_ref[...] + (1.0 - ADAM_B2) * (gv * gv)
        mo_ref[...] = mn
        vo_ref[...] = vn
        d_ref[...] = -ADAM_LR * ((mn / c1) / (jnp.sqrt(vn / c2) + ADAM_EPS) + ADAM_WD * w_ref[...])

    spec = pl.BlockSpec((1, tr, cols), lambda l, i: (l, i, 0))
    return pl.pallas_call(
        body, name=name, grid=(layers, rows // tr), in_specs=[spec] * 4, out_specs=[spec] * 3,
        out_shape=[jax.ShapeDtypeStruct((layers, rows, cols), F32)] * 3, compiler_params=_cp(),
    )(w, g, m, v)


IN_NAMES = ['x', 'positions', 'norm_g', 'w_in', 'conv_w', 'conv_b', 'w_gate_x', 'b_gate_x', 'w_gate_a', 'b_gate_a', 'lru_lambda', 'w_lru_o',
            'cq_norm_g', 'ckv_norm_g', 'w_uq', 'w_ukv', 'mla_q_norm_g', 'mla_k_norm_g', 'w_mla_o', 'dil_q_norm_g', 'dil_k_norm_g', 'w_dil_o',
            'b_merge', 'w_out']
WEIGHTS = IN_NAMES[2:]
REPLICATED = [n for n in WEIGHTS if n not in SCATTER]
GATE_WEIGHTS = ('w_gate_x', 'w_gate_a')

_KR0 = C_KR * LANE


GATHERED = ['w_in', 'w_lru_o', 'w_uq', 'w_ukv', 'w_mla_o', 'w_dil_o', 'w_out', 'conv_w']


def _local_weights(wd, me):
    loc = {n: wd[n].astype(BF16) for n in GATHERED[:-1]}
    loc['w_in'] = _to_window(loc['w_in'], me)
    loc['w_uq'] = jnp.pad(loc['w_uq'], ((0, 0), (0, 0), (0, LANE - MLA_QK)))
    loc['conv_w'] = wd['conv_w']
    return [[loc[n][l] for n in GATHERED] for l in range(DEPTH)]


def _layer_weights(gathered, rep, l):
    gw = dict(zip(GATHERED, gathered))
    by_rows = lambda a: a.reshape(-1, a.shape[-1])
    by_cols = lambda a: jnp.swapaxes(a, 0, 1).reshape(a.shape[1], -1)
    ukv = jnp.swapaxes(gw['w_ukv'], 0, 1)
    g96 = lambda a: jnp.pad(a[l].reshape(1, MLA_QK), ((0, 0), (0, LANE - MLA_QK)))
    g64 = lambda a: jnp.tile(a[l].reshape(1, DIL_HD), (1, 2))
    return dict(
        norm_g=rep['norm_g'][l].reshape(1, D), w_in=_assemble_w_in(gw['w_in']),
        conv_w=by_cols(gw['conv_w']), conv_b=rep['conv_b'][l].reshape(1, D),
        w_gx=rep['w_gate_x'][l].astype(BF16), b_gx=rep['b_gate_x'][l].reshape(8, 1, LANE),
        w_ga=rep['w_gate_a'][l].astype(BF16), b_ga=rep['b_gate_a'][l].reshape(8, 1, LANE),
        lam=rep['lru_lambda'][l].reshape(1, D),
        w_lru_o=by_rows(gw['w_lru_o']), w_mla_o=by_cols(gw['w_mla_o']), w_dil_o=by_cols(gw['w_dil_o']), w_out=by_rows(gw['w_out']),
        g_cq=rep['cq_norm_g'][l].reshape(1, 256), g_ckv=rep['ckv_norm_g'][l].reshape(1, 128),
        w_uq=by_cols(gw['w_uq']), w_uk=jnp.pad(ukv[:, :, :64], ((0, 0), (0, 0), (0, 64))).reshape(128, 1024),
        w_uv=ukv[:, :, 64:].reshape(128, 512),
        g_mq=g96(rep['mla_q_norm_g']), g_mk=g96(rep['mla_k_norm_g']), g_dq=g64(rep['dil_q_norm_g']), g_dk=g64(rep['dil_k_norm_g']),
        b_merge=rep['b_merge'][l].reshape(1, 3 * D),
    )


def _sharded_grads(g):
    uk = g['w_uk'].reshape(128, 8, 128)[:, :, :64]
    uv = g['w_uv'].reshape(128, 8, 64)
    d = {'w_in': g['w_in'], 'conv_w': g['conv_w'], 'w_lru_o': g['w_lru_o'], 'w_uq': g['w_uq'],
         'w_ukv': jnp.concatenate([uk, uv], axis=-1).reshape(128, 1024), 'w_mla_o': g['w_mla_o'], 'w_dil_o': g['w_dil_o'],
         'w_out': g['w_out']}
    return [d[n] for n in SCATTER]


def _replicated_grads(g):
    return {
        'conv_b': g['conv_b'].reshape(D),
        'w_gate_x': g['w_gx'], 'b_gate_x': g['b_gx'].reshape(8, LANE), 'w_gate_a': g['w_ga'], 'b_gate_a': g['b_ga'].reshape(8, LANE),
        'lru_lambda': g['lam'].reshape(D), 'cq_norm_g': g['g_cq'].reshape(256), 'ckv_norm_g': g['g_ckv'].reshape(128),
        'mla_q_norm_g': g['g_mq'][0, :MLA_QK], 'mla_k_norm_g': g['g_mk'][0, :MLA_QK],
        'dil_q_norm_g': g['g_dq'][0, :DIL_HD] + g['g_dq'][0, DIL_HD:], 'dil_k_norm_g': g['g_dk'][0, :DIL_HD] + g['g_dk'][0, DIL_HD:],
        'b_merge': g['b_merge'].reshape(3 * D),
    }


def kernel(x, positions, norm_g, w_in, conv_w, conv_b, w_gate_x, b_gate_x, w_gate_a, b_gate_a, lru_lambda, w_lru_o, cq_norm_g, ckv_norm_g, w_uq, w_ukv, mla_q_norm_g, mla_k_norm_g, w_mla_o, dil_q_norm_g, dil_k_norm_g, w_dil_o, b_merge, w_out, loss_target, m_norm_g, m_w_in, m_conv_w, m_conv_b, m_w_gate_x, m_b_gate_x, m_w_gate_a, m_b_gate_a, m_lru_lambda, m_w_lru_o, m_cq_norm_g, m_ckv_norm_g, m_w_uq, m_w_ukv, m_mla_q_norm_g, m_mla_k_norm_g, m_w_mla_o, m_dil_q_norm_g, m_dil_k_norm_g, m_w_dil_o, m_b_merge, m_w_out, v_norm_g, v_w_in, v_conv_w, v_conv_b, v_w_gate_x, v_b_gate_x, v_w_gate_a, v_b_gate_a, v_lru_lambda, v_w_lru_o, v_cq_norm_g, v_ckv_norm_g, v_w_uq, v_w_ukv, v_mla_q_norm_g, v_mla_k_norm_g, v_w_mla_o, v_dil_q_norm_g, v_dil_k_norm_g, v_w_dil_o, v_b_merge, v_w_out):
    args = (x, positions, norm_g, w_in, conv_w, conv_b, w_gate_x, b_gate_x, w_gate_a, b_gate_a, lru_lambda, w_lru_o, cq_norm_g, ckv_norm_g, w_uq, w_ukv, mla_q_norm_g, mla_k_norm_g, w_mla_o, dil_q_norm_g, dil_k_norm_g, w_dil_o, b_merge, w_out)
    moments_m = (m_norm_g, m_w_in, m_conv_w, m_conv_b, m_w_gate_x, m_b_gate_x, m_w_gate_a, m_b_gate_a, m_lru_lambda, m_w_lru_o, m_cq_norm_g, m_ckv_norm_g, m_w_uq, m_w_ukv, m_mla_q_norm_g, m_mla_k_norm_g, m_w_mla_o, m_dil_q_norm_g, m_dil_k_norm_g, m_w_dil_o, m_b_merge, m_w_out)
    moments_v = (v_norm_g, v_w_in, v_conv_w, v_conv_b, v_w_gate_x, v_b_gate_x, v_w_gate_a, v_b_gate_a, v_lru_lambda, v_w_lru_o, v_cq_norm_g, v_ckv_norm_g, v_w_uq, v_w_ukv, v_mla_q_norm_g, v_mla_k_norm_g, v_w_mla_o, v_dil_q_norm_g, v_dil_k_norm_g, v_w_dil_o, v_b_merge, v_w_out)
    a = dict(zip(IN_NAMES, args))
    wd = {n: a[n] for n in WEIGHTS}
    md = dict(zip(WEIGHTS, moments_m))
    vd = dict(zip(WEIGHTS, moments_v))

    me = 4 * lax.axis_index("x") + 2 * lax.axis_index("y") + lax.axis_index("c")

    assert DEPTH == 2
    xs, tabs = x[0], _rope_tables(positions[0])
    whole = [_whole] * len(GATHERED)
    slicers = [SCATTER[n][0] for n in SCATTER]
    grad_slices = [SCATTER[n][1:3] for n in SCATTER]

    local = _local_weights(wd, me)
    w_slices = [(a.shape, a.dtype) for a in local[0]]
    landed0 = _gather_two_level(local[0], "gather_w0")
    flying = _exchange_start(local[1], whole, w_slices, landed0[0], "gather_w1_start")
    rep0 = dict(wd, norm_g=wd['norm_g'] + flying[-1][0, 0])
    w0 = _layer_weights(landed0, rep0, 0)
    x1, saved0 = _layer_fwd(xs, w0, tabs)
    w1 = _layer_weights(_exchange_wait(flying, whole, x1, "gather_w1_wait"), wd, 1)
    x2, saved1 = _layer_fwd(x1, w1, tabs)
    loss, dx2 = _loss_fwd_bwd(x2, loss_target[0])
    loss = loss[0, 0]

    sharded = list(SCATTER)
    nsh = len(sharded)
    small = [n for n in REPLICATED if n not in GATE_WEIGHTS and n != 'norm_g']

    def outgoing(g):
        r = _replicated_grads(g)
        return (_sharded_grads(g) + [_pack([r[n] for n in small], F32, 0)]
                + [r[n].astype(BF16).reshape(8 * LANE, LANE) for n in GATE_WEIGHTS])

    out_slicers = slicers + [_whole] * 3
    out_slices = grad_slices + [((_packed_rows([wd[n].shape[1:] for n in small]), LANE), F32)] + [((8 * LANE, LANE), BF16)] * 2
    dx1, g1 = _layer_bwd(dx2, w1, tabs, saved1)
    flying1 = _exchange_start(outgoing(g1), out_slicers, out_slices, dx1, "scatter_g1_start")
    later = {}

    names = sharded + ['small'] + list(GATE_WEIGHTS)
    sliced = [True] * nsh + [False] * 3
    by_chip = [(lambda ref, q: ref.at[q])] * nsh + [_whole] * 3

    def send_layer0(g):
        later['got1'] = _exchange_wait(flying1, out_slicers, g['w_in'], "scatter_g1_wait")
        mine = outgoing(g)
        came = _pair_exchange(mine, out_slicers, out_slices, sliced, "pair_g0")
        my_side = 2 * jnp.arange(4, dtype=jnp.int32) + lax.axis_index("c")
        halves = []
        for n, a, c in zip(names, mine, came):
            if n in SCATTER:
                first = _win_base(my_side) if n == 'w_in' else my_side
                halves.append(_pair_add(a, c, first.astype(jnp.int32), SLICED_AXIS[n], f"pair_sum_{n}"))
            else:
                halves.append(_add2(a, c[0], f"pair_sum_{n}"))
        later['flying0'] = _exchange_start(halves, by_chip, out_slices, later['got1'][0], "scatter_g0_start", plan=_plan_chips, nslots=4)
        return later['flying0'][-1]

    grad_x, g0 = _layer_bwd(dx1, w0, tabs, saved0, hook=send_layer0, after=flying1[-1])
    sum1 = [_sum8(b, f"sum_{n}_1") for n, b in zip(names, later['got1'])]
    behind = grad_x[:1, :1] + sum(s_[:1, :1] for s_ in sum1)
    got0 = _exchange_wait(later['flying0'], by_chip, behind, "scatter_g0_wait", plan=_plan_chips)
    sum0 = [_sum8(b, f"sum_{n}_0") for n, b in zip(names, got0)]
    norm_part = _pack([jnp.stack([g['norm_g'].reshape(D) for g in (g0, g1)])], F32, 0)
    norm_sum = _sum8(_exchange([norm_part], [_whole], [(norm_part.shape, F32)], "gather_norm_g")[0], "sum_norm_g")

    gsh = {n: jnp.stack([sum0[i], sum1[i]]) for i, n in enumerate(sharded)}
    gsh['w_in'] = _from_window(gsh['w_in'], me)
    gsh['w_uq'] = gsh['w_uq'][:, :, :MLA_QK]
    grep = {'norm_g': _unpack(norm_sum, [wd['norm_g'].shape], 0)[0]}
    per_layer = [_unpack(s[nsh], [wd[n].shape[1:] for n in small], 0) for s in (sum0, sum1)]
    grep.update({n: jnp.stack([per_layer[l][i] for l in range(DEPTH)]) for i, n in enumerate(small)})
    for i, n in enumerate(GATE_WEIGHTS):
        grep[n] = jnp.stack([sum0[nsh + 1 + i], sum1[nsh + 1 + i]]).reshape(wd[n].shape)

    out_g, out_d, out_m, out_v = {}, {}, {}, {}
    vecs = ['norm_g'] + small
    vshapes = [wd[n].shape for n in vecs]
    packed_g = _pack([grep[n] for n in vecs], F32, 0)
    d_, m_, v_ = _adamw(_pack([wd[n] for n in vecs], F32, 0)[None], packed_g[None], _pack([md[n] for n in vecs], F32, 0)[None],
                        _pack([vd[n] for n in vecs], F32, 0)[None], "adamw_vectors")
    for dst, buf in ((out_d, d_), (out_m, m_), (out_v, v_)):
        dst.update(zip(vecs, _unpack(buf[0], vshapes, 0)))
    out_g.update({n: grep[n] for n in vecs})
    gsh.update({n: grep[n] for n in GATE_WEIGHTS})
    for n in sharded + list(GATE_WEIGHTS):
        shp = wd[n].shape
        three = (1, -1, shp[-1])
        d_, m_, v_ = _adamw(wd[n].reshape(three), gsh[n].reshape(three), md[n].reshape(three), vd[n].reshape(three), "adamw_" + n)
        out_g[n], out_d[n], out_m[n], out_v[n] = gsh[n], d_.reshape(shp), m_.reshape(shp), v_.reshape(shp)

    loss = lax.psum(loss, ("x", "y", "c"))
    return (loss, grad_x[None], *[out_g[n] for n in WEIGHTS], *[out_d[n] for n in WEIGHTS], *[out_m[n] for n in WEIGHTS],
            *[out_v[n] for n in WEIGHTS])
```

```python
import numpy as np
import jax
import jax.numpy as jnp
from jax import lax
from jax.experimental import pallas as pl
from jax.experimental.pallas import tpu as pltpu

F32 = jnp.float32
BF16 = jnp.bfloat16

N_DEV = 8
D = 1024
DEPTH = 2
EPS = 1e-6
ROPE_THETA = 10000.0
LRU_C = 8.0
LANE = 128
SUB = 8
IN_WIDTH = 11168
SHARD_IN = IN_WIDTH // N_DEV

C_LRUX, C_LRUG, C_CQ, C_CKV, C_KR, C_MLAG, C_DQ, C_DK, C_DV, C_DILG, C_MERGE = 0, 8, 16, 18, 19, 20, 24, 36, 48, 60, 64
ZW = 88 * LANE
KR_LANE = 64

MLA_QK = 96
MLA_SCALE = MLA_QK ** -0.5
DIL_HD = 64
DIL_SCALE = DIL_HD ** -0.5
DIL_DILATIONS = (1, 4, 16)
NK = 128

ADAM_LR, ADAM_B1, ADAM_B2, ADAM_EPS, ADAM_WD, ADAM_STEP = 0.001, 0.9, 0.999, 1e-08, 0.01, 10

NEG = -1e30
LOG2E = 1.4426950408889634
VMEM_LIMIT = 48 * 1024 * 1024


def _cp(**kw):
    return pltpu.CompilerParams(vmem_limit_bytes=VMEM_LIMIT, **kw)


def _sig(x):
    return 1.0 / (1.0 + jnp.exp(-x))


def _silu(x):
    return x * _sig(x)


def _dsilu(x):
    s = _sig(x)
    return s * (1.0 + x * (1.0 - s))


def _dot(a, b, dims):
    return lax.dot_general(a, b, (dims, ((), ())), preferred_element_type=F32)


def _nn(a, b):
    return _dot(a, b, ((1,), (0,)))


def _nt(a, b):
    return _dot(a, b, ((1,), (1,)))


def _tn(a, b):
    return _dot(a, b, ((0,), (0,)))


def _rsum(x):
    return jnp.sum(x, axis=-1, keepdims=True)


def _rsum_mxu(x):
    ones = jnp.ones((x.shape[-1], LANE), F32)
    return lax.dot_general(x, ones, (((1,), (0,)), ((), ())), precision=lax.Precision.HIGHEST, preferred_element_type=F32)


def _csum(x):
    return jnp.sum(x, axis=0, keepdims=True)


def _mm(a, b, *, mode, name, out_dtype=F32, add=None, after=None, tm=1024, tn=1024, tk=1024):
    if mode == "nn":
        (M, K), (K2, N) = a.shape, b.shape
    elif mode == "nt":
        (M, K), (N, K2) = a.shape, b.shape
    else:
        (K, M), (K2, N) = a.shape, b.shape
    assert K == K2
    tm, tn, tk = min(tm, M), min(tn, N), min(tk, K)
    assert M % tm == 0 and N % tn == 0 and K % tk == 0
    nk = K // tk
    fn = {"nn": _nn, "nt": _nt, "tn": _tn}[mode]
    has_add = add is not None

    def body(*refs):
        a_ref, b_ref = refs[0], refs[1]
        add_ref = refs[2] if has_add else None
        o_ref = refs[2 + has_add + (after is not None)]
        part = fn(a_ref[...].astype(BF16), b_ref[...].astype(BF16))

        def fin(acc):
            if has_add:
                acc = acc + add_ref[...]
            o_ref[...] = acc.astype(out_dtype)

        if nk == 1:
            fin(part)
        else:
            acc_ref = refs[-1]
            k = pl.program_id(2)

            @pl.when(k == 0)
            def _():
                acc_ref[...] = part

            @pl.when(k > 0)
            def _():
                acc_ref[...] += part

            @pl.when(k == nk - 1)
            def _():
                fin(acc_ref[...])

    a_spec = pl.BlockSpec((tk, tm), lambda i, j, k: (k, i)) if mode == "tn" else pl.BlockSpec((tm, tk), lambda i, j, k: (i, k))
    b_spec = pl.BlockSpec((tn, tk), lambda i, j, k: (j, k)) if mode == "nt" else pl.BlockSpec((tk, tn), lambda i, j, k: (k, j))
    o_spec = pl.BlockSpec((tm, tn), lambda i, j, k: (i, j))
    in_specs, args = [a_spec, b_spec], [a, b]
    if has_add:
        in_specs.append(o_spec)
        args.append(add)
    if after is not None:
        in_specs.append(pl.BlockSpec(memory_space=pl.ANY))
        args.append(after)
    return pl.pallas_call(
        body, name=name, grid=(M // tm, N // tn, nk), in_specs=in_specs, out_specs=o_spec,
        out_shape=jax.ShapeDtypeStruct((M, N), out_dtype),
        scratch_shapes=[pltpu.VMEM((tm, tn), F32)] if nk > 1 else [],
        compiler_params=_cp(dimension_semantics=("parallel", "parallel", "arbitrary")),
    )(*args)


T_ROW = 512


def _rms_in_fwd(x, g):
    S = x.shape[0]
    T = T_ROW

    def body(x_ref, g_ref, h_ref):
        xv = x_ref[...]
        r = lax.rsqrt(jnp.mean(xv * xv, axis=-1, keepdims=True) + EPS)
        h_ref[...] = (xv * r * g_ref[...]).astype(BF16)

    return pl.pallas_call(
        body, name="rms_in_fwd", grid=(S // T,),
        in_specs=[pl.BlockSpec((T, D), lambda i: (i, 0)), pl.BlockSpec((1, D), lambda i: (0, 0))],
        out_specs=pl.BlockSpec((T, D), lambda i: (i, 0)),
        out_shape=jax.ShapeDtypeStruct((S, D), BF16), compiler_params=_cp(),
    )(x, g)


def _rms_in_bwd(x, g, dh, dres):
    S = x.shape[0]
    T = T_ROW

    def body(x_ref, g_ref, dh_ref, dr_ref, dx_ref, dg_ref):
        i = pl.program_id(0)
        xv = x_ref[...]
        r = lax.rsqrt(jnp.mean(xv * xv, axis=-1, keepdims=True) + EPS)
        xn = xv * r
        dy = dh_ref[...]
        part = _csum(dy * xn)

        @pl.when(i == 0)
        def _():
            dg_ref[...] = part

        @pl.when(i > 0)
        def _():
            dg_ref[...] += part

        dxh = dy * g_ref[...]
        dx_ref[...] = dr_ref[...] + r * (dxh - xn * jnp.mean(dxh * xn, axis=-1, keepdims=True))

    row = pl.BlockSpec((T, D), lambda i: (i, 0))
    vec = pl.BlockSpec((1, D), lambda i: (0, 0))
    return pl.pallas_call(
        body, name="rms_in_bwd", grid=(S // T,), in_specs=[row, vec, row, row], out_specs=[row, vec],
        out_shape=[jax.ShapeDtypeStruct((S, D), F32), jax.ShapeDtypeStruct((1, D), F32)], compiler_params=_cp(),
    )(x, g, dh, dres)


T_LRU = 1024
T_LRU_BWD = 512


def _neg_expm1(y):
    ser = -y * (1.0 + y * 0.5 * (1.0 + y * (1.0 / 3.0) * (1.0 + y * 0.25 * (1.0 + y * 0.2))))
    return jnp.where(y > -0.03, ser, 1.0 - jnp.exp(y))


def _softplus_neg(lam):
    e = jnp.exp(-jnp.abs(lam))
    l1p = jnp.where(e < 0.01, e * (1.0 - e * (0.5 - e * (1.0 / 3.0 - e * 0.25))), jnp.log(1.0 + e))
    return jnp.maximum(-lam, 0.0) + l1p


def _scan_fwd(a, b, T):
    row = lax.broadcasted_iota(jnp.int32, a.shape, 0)
    d = 1
    while d < T:
        m = row >= d
        b = jnp.where(m, a * pltpu.roll(b, d, 0) + b, b)
        a = jnp.where(m, a * pltpu.roll(a, d, 0), a)
        d *= 2
    return a, b


def _scan_bwd(a, b, T):
    row = lax.broadcasted_iota(jnp.int32, a.shape, 0)
    d = 1
    while d < T:
        m = row < T - d
        b = jnp.where(m, a * pltpu.roll(b, T - d, 0) + b, b)
        a = jnp.where(m, a * pltpu.roll(a, T - d, 0), a)
        d *= 2
    return b


def _lru_common(x, prev, first, cw_ref, cb_ref, wgx_ref, bgx_ref, wga_ref, bga_ref, lam_ref, T):
    row = lax.broadcasted_iota(jnp.int32, x.shape, 0)
    prev = jnp.where(first, 0.0, prev)
    xs = []
    for j in (3, 2, 1):
        pv = jnp.tile(pltpu.roll(prev, j, 0), (T // SUB, 1))
        xs.append(jnp.where(row < j, pv, pltpu.roll(x, j, 0)))
    xs.append(x)
    xc = cb_ref[...] + cw_ref[0:1, :] * xs[0] + cw_ref[1:2, :] * xs[1] + cw_ref[2:3, :] * xs[2] + cw_ref[3:4, :] * xs[3]
    xcb = xc.astype(BF16)
    gx = _sig(_nn(xcb, wgx_ref[0]) + bgx_ref[0])
    ga = _sig(_nn(xcb, wga_ref[0]) + bga_ref[0])
    sp = _softplus_neg(lam_ref[...])
    log_a = -LRU_C * ga * sp
    a = jnp.exp(log_a)
    mult = jnp.sqrt(_neg_expm1(2.0 * log_a))
    return xs, xc, xcb, gx, ga, sp, a, mult


def _lru_specs(T, tmap):
    def at(col0):
        return pl.BlockSpec((T, LANE), lambda n, i: (tmap(i), col0 + n))

    def prev(col0):
        return pl.BlockSpec((SUB, LANE), lambda n, i: (jnp.maximum(tmap(i) * (T // SUB) - 1, 0), col0 + n))

    small = [
        pl.BlockSpec((4, LANE), lambda n, i: (0, n)),
        pl.BlockSpec((1, LANE), lambda n, i: (0, n)),
        pl.BlockSpec((1, LANE, LANE), lambda n, i: (n, 0, 0)),
        pl.BlockSpec((1, 1, LANE), lambda n, i: (n, 0, 0)),
        pl.BlockSpec((1, LANE, LANE), lambda n, i: (n, 0, 0)),
        pl.BlockSpec((1, 1, LANE), lambda n, i: (n, 0, 0)),
        pl.BlockSpec((1, LANE), lambda n, i: (0, n)),
    ]
    return at, prev, small


def _lru_fwd(zp, w):
    S = zp.shape[0]
    T = T_LRU
    at, prev, small = _lru_specs(T, lambda i: i)

    def body(x_ref, xp_ref, g_ref, cw_ref, cb_ref, wgx_ref, bgx_ref, wga_ref, bga_ref, lam_ref, hs_ref, y_ref, carry_ref):
        i = pl.program_id(1)

        @pl.when(i == 0)
        def _():
            carry_ref[...] = jnp.zeros_like(carry_ref)

        x = x_ref[...]
        _, xc, _, gx, _, _, a, mult = _lru_common(x, xp_ref[...], i == 0, cw_ref, cb_ref, wgx_ref, bgx_ref, wga_ref, bga_ref, lam_ref, T)
        A, B = _scan_fwd(a, mult * gx * xc, T)
        h = B + A * carry_ref[SUB - 1:SUB, :]
        hs_ref[...] = h
        carry_ref[...] = hs_ref[T - SUB:T, :]
        y_ref[...] = (h * _silu(g_ref[...])).astype(BF16)

    out = pl.BlockSpec((T, LANE), lambda n, i: (i, n))
    return pl.pallas_call(
        body, name="lru_fwd", grid=(8, S // T),
        in_specs=[at(C_LRUX), prev(C_LRUX), at(C_LRUG)] + small, out_specs=[out, out],
        out_shape=[jax.ShapeDtypeStruct((S, D), F32), jax.ShapeDtypeStruct((S, D), BF16)],
        scratch_shapes=[pltpu.VMEM((SUB, LANE), F32)],
        compiler_params=_cp(dimension_semantics=("parallel", "arbitrary")),
    )(zp, zp, zp, w["conv_w"], w["conv_b"], w["w_gx"], w["b_gx"], w["w_ga"], w["b_ga"], w["lam"])


def _lru_bwd(zp, hs, dy, w, dz):
    S = zp.shape[0]
    T = T_LRU_BWD
    nT = S // T
    at, prev, small = _lru_specs(T, lambda i: nT - 1 - i)

    def body(x_ref, xp_ref, g_ref, h_ref, hp_ref, dy_ref, cw_ref, cb_ref, wgx_ref, bgx_ref, wga_ref, bga_ref, lam_ref, dz_in,
             dzx_ref, dcw_ref, dcb_ref, dwgx_ref, dbgx_ref, dwga_ref, dbga_ref, dlam_ref, carry_ref, head_ref):
        del dz_in
        j = pl.program_id(1)
        it = nT - 1 - j

        @pl.when(j == 0)
        def _():
            for r in (carry_ref, head_ref, dcw_ref, dcb_ref, dwgx_ref, dbgx_ref, dwga_ref, dbga_ref, dlam_ref):
                r[...] = jnp.zeros_like(r)

        first = it == 0
        x = x_ref[...]
        xs, xc, xcb, gx, ga, sp, a, mult = _lru_common(x, xp_ref[...], first, cw_ref, cb_ref, wgx_ref, bgx_ref, wga_ref, bga_ref, lam_ref, T)
        row = lax.broadcasted_iota(jnp.int32, x.shape, 0)
        u = gx * xc
        h = h_ref[...]
        hp = jnp.where(first, 0.0, hp_ref[...])
        hm1 = jnp.where(row < 1, jnp.tile(pltpu.roll(hp, 1, 0), (T // SUB, 1)), pltpu.roll(h, 1, 0))
        dho = dy_ref[...] * _silu(g_ref[...])
        gin = jnp.where(row == T - 1, dho + carry_ref[0:1, :], dho)
        abar = jnp.where(row == T - 1, 0.0, pltpu.roll(a, T - 1, 0))
        dh = _scan_bwd(abar, gin, T)
        carry_ref[...] = (a * dh)[0:SUB, :]
        da = dh * hm1
        dmult = dh * u
        du = dh * mult
        dgx = du * xc
        dxc = du * gx
        dlog_a = da * a - dmult * a * a / mult
        dga = dlog_a * (-LRU_C * sp)
        lam = lam_ref[...]
        dlam_ref[...] += _csum(dlog_a * (-LRU_C * ga)) * (-1.0 / (1.0 + jnp.exp(lam)))
        dpa = dga * ga * (1.0 - ga)
        dpx = dgx * gx * (1.0 - gx)
        dpab, dpxb = dpa.astype(BF16), dpx.astype(BF16)
        dxc = dxc + _nt(dpxb, wgx_ref[0]) + _nt(dpab, wga_ref[0])
        dwgx_ref[0] += _tn(xcb, dpxb)
        dwga_ref[0] += _tn(xcb, dpab)
        dbgx_ref[0] += _csum(dpx)
        dbga_ref[0] += _csum(dpa)
        dcb_ref[...] += _csum(dxc)
        for k in range(4):
            dcw_ref[k:k + 1, :] += _csum(dxc * xs[k])
        head = head_ref[...]
        dx = cw_ref[3:4, :] * dxc
        for jj in (1, 2, 3):
            hv = jnp.tile(pltpu.roll(head, SUB - jj, 0), (T // SUB, 1))
            dx = dx + cw_ref[3 - jj:4 - jj, :] * jnp.where(row >= T - jj, hv, pltpu.roll(dxc, T - jj, 0))
        head_ref[...] = dxc[0:SUB, :]
        dzx_ref[...] = dx.astype(BF16)

    def acc(shape, imap):
        return pl.BlockSpec(shape, imap)

    out_specs = [
        pl.BlockSpec((T, LANE), lambda n, i: (nT - 1 - i, C_LRUX + n)),
        acc((4, LANE), lambda n, i: (0, n)), acc((1, LANE), lambda n, i: (0, n)),
        acc((1, LANE, LANE), lambda n, i: (n, 0, 0)), acc((1, 1, LANE), lambda n, i: (n, 0, 0)),
        acc((1, LANE, LANE), lambda n, i: (n, 0, 0)), acc((1, 1, LANE), lambda n, i: (n, 0, 0)),
        acc((1, LANE), lambda n, i: (0, n)),
    ]
    out_shape = [
        jax.ShapeDtypeStruct(dz.shape, BF16),
        jax.ShapeDtypeStruct((4, D), F32), jax.ShapeDtypeStruct((1, D), F32),
        jax.ShapeDtypeStruct((8, LANE, LANE), F32), jax.ShapeDtypeStruct((8, 1, LANE), F32),
        jax.ShapeDtypeStruct((8, LANE, LANE), F32), jax.ShapeDtypeStruct((8, 1, LANE), F32),
        jax.ShapeDtypeStruct((1, D), F32),
    ]
    dyspec = pl.BlockSpec((T, LANE), lambda n, i: (nT - 1 - i, n))
    hprev = pl.BlockSpec((SUB, LANE), lambda n, i: (jnp.maximum((nT - 1 - i) * (T // SUB) - 1, 0), n))
    return pl.pallas_call(
        body, name="lru_bwd", grid=(8, nT),
        in_specs=[at(C_LRUX), prev(C_LRUX), at(C_LRUG), dyspec, hprev, dyspec] + small + [pl.BlockSpec(memory_space=pl.ANY)],
        out_specs=out_specs, out_shape=out_shape,
        scratch_shapes=[pltpu.VMEM((SUB, LANE), F32), pltpu.VMEM((SUB, LANE), F32)],
        input_output_aliases={13: 0},
        compiler_params=_cp(dimension_semantics=("parallel", "arbitrary")),
    )(zp, zp, zp, hs, hs, dy, w["conv_w"], w["conv_b"], w["w_gx"], w["b_gx"], w["w_ga"], w["b_ga"], w["lam"], dz)


def _lru_gate_bwd(zp, hs, dy, dz):
    S = zp.shape[0]
    T = T_ROW

    def body(g_ref, h_ref, dy_ref, dz_in, o_ref):
        del dz_in
        o_ref[...] = (dy_ref[...] * h_ref[...] * _dsilu(g_ref[...])).astype(BF16)

    row = pl.BlockSpec((T, D), lambda i: (i, 0))
    zc = pl.BlockSpec((T, D), lambda i: (i, C_LRUG // 8))
    return pl.pallas_call(
        body, name="lru_gate_bwd", grid=(S // T,), in_specs=[zc, row, row, pl.BlockSpec(memory_space=pl.ANY)], out_specs=zc,
        out_shape=jax.ShapeDtypeStruct(dz.shape, BF16), input_output_aliases={3: 0}, compiler_params=_cp(),
    )(zp, hs, dy, dz)


def _rope_tables(pos):
    pf = pos.astype(F32)[:, None]

    def cs(d):
        inv = ROPE_THETA ** (-jnp.arange(0, d, 2, dtype=F32) / d)
        ang = pf * inv
        return jnp.cos(ang), jnp.sin(ang)

    S = pos.shape[0]
    c, s = cs(32)
    one, zero = jnp.ones((S, 64), F32), jnp.zeros((S, 16), F32)
    z32, z64 = jnp.zeros((S, 32), F32), jnp.zeros((S, 64), F32)
    mla = (jnp.concatenate([one, c, c, jnp.ones((S, 32), F32)], 1),
           jnp.concatenate([z64, zero, s, z32], 1),
           jnp.concatenate([z64, -s, zero, z32], 1))
    c, s = cs(64)
    dil = (jnp.concatenate([c, c, c, c], 1),
           jnp.concatenate([z32, s, z32, s], 1),
           jnp.concatenate([-s, z32, -s, z32], 1))
    return mla, dil


def _rope(x, C, S1, S2, sh):
    return x * C + pltpu.roll(x, sh, 1) * S1 + pltpu.roll(x, LANE - sh, 1) * S2


def _rope_t(dy, C, S1, S2, sh):
    return dy * C + pltpu.roll(dy * S1, LANE - sh, 1) + pltpu.roll(dy * S2, sh, 1)


def _lane(shape):
    return lax.broadcasted_iota(jnp.int32, shape, 1)


T_MLA = 512
TA = 512


def _zcol(T, width, col_lanes):
    assert (col_lanes * LANE) % width == 0
    return pl.BlockSpec((T, width), lambda i: (i, col_lanes * LANE // width))


def _full(shape):
    return pl.BlockSpec(shape, lambda *_: (0,) * len(shape))


def _mla_pre_fwd(zp, w, tab):
    S = zp.shape[0]
    T = T_MLA

    def body(cq_ref, ckv_ref, kr_ref, gcq_ref, gckv_ref, wuq_ref, wuk_ref, wuv_ref, gq_ref, gk_ref, C_ref, S1_ref, S2_ref,
             q_ref, k_ref, v_ref):
        cq = cq_ref[...]
        cqn = (cq * lax.rsqrt(jnp.mean(cq * cq, axis=-1, keepdims=True) + EPS) * gcq_ref[...]).astype(BF16)
        ckv = ckv_ref[...]
        ckvn = (ckv * lax.rsqrt(jnp.mean(ckv * ckv, axis=-1, keepdims=True) + EPS) * gckv_ref[...]).astype(BF16)
        q0 = _nn(cqn, wuq_ref[...])
        k0 = _nn(ckvn, wuk_ref[...])
        krb = kr_ref[...]
        C, S1, S2 = C_ref[...], S1_ref[...], S2_ref[...]
        for h in range(8):
            sl = slice(h * LANE, (h + 1) * LANE)
            xq = q0[:, sl]
            xq = xq * lax.rsqrt(_rsum_mxu(xq * xq) * (1.0 / MLA_QK) + EPS) * gq_ref[...]
            q_ref[:, sl] = _rope(xq, C, S1, S2, 16).astype(BF16)
            xk = k0[:, sl] + krb
            xk = xk * lax.rsqrt(_rsum_mxu(xk * xk) * (1.0 / MLA_QK) + EPS) * gk_ref[...]
            k_ref[:, sl] = _rope(xk, C, S1, S2, 16).astype(BF16)
        v_ref[...] = _nn(ckvn, wuv_ref[...]).astype(BF16)

    tabspec = pl.BlockSpec((T, LANE), lambda i: (i, 0))
    in_specs = [_zcol(T, 256, C_CQ), _zcol(T, LANE, C_CKV), _zcol(T, LANE, C_KR), _full((1, 256)), _full((1, LANE)),
                _full((256, 1024)), _full((LANE, 1024)), _full((LANE, 512)), _full((1, LANE)), _full((1, LANE)),
                tabspec, tabspec, tabspec]
    return pl.pallas_call(
        body, name="mla_pre_fwd", grid=(S // T,), in_specs=in_specs,
        out_specs=[pl.BlockSpec((T, 1024), lambda i: (i, 0)), pl.BlockSpec((T, 1024), lambda i: (i, 0)), pl.BlockSpec((T, 512), lambda i: (i, 0))],
        out_shape=[jax.ShapeDtypeStruct((S, 1024), BF16), jax.ShapeDtypeStruct((S, 1024), BF16), jax.ShapeDtypeStruct((S, 512), BF16)],
        compiler_params=_cp(),
    )(zp, zp, zp, w["g_cq"], w["g_ckv"], w["w_uq"], w["w_uk"], w["w_uv"], w["g_mq"], w["g_mk"], *tab)


def _mla_attn_fwd(q, k, v, zp):
    S = q.shape[0]
    nq = S // TA

    def body(q_ref, k_ref, v_ref, g_ref, o_ref, lse_ref, y_ref):
        qi = pl.program_id(1)
        lane = _lane((TA, LANE))
        rowi = lax.broadcasted_iota(jnp.int32, (TA, TA), 0)
        coli = lax.broadcasted_iota(jnp.int32, (TA, TA), 1)
        o_tot = jnp.zeros((TA, LANE), F32)
        for hh in range(2):
            cs = slice(hh * LANE, (hh + 1) * LANE)
            hm = (lane < 64) if hh == 0 else (lane >= 64)
            qh = q_ref[:, cs]
            ones_lane = 64 if hh == 0 else 0

            def step(kb, carry, masked, cs=cs, hm=hm, qh=qh, ones_lane=ones_lane):
                m, acc = carry
                off = pl.multiple_of(kb * TA, TA)
                kh = k_ref[pl.ds(off, TA), cs]
                vv = v_ref[pl.ds(off, TA), :]
                vh = jnp.where(hm, vv, jnp.where(lane == ones_lane, jnp.ones_like(vv), jnp.zeros_like(vv)))
                s = _nt(qh, kh) * (MLA_SCALE * LOG2E)
                if masked:
                    s = jnp.where(rowi >= coli, s, NEG)
                m_new = jnp.maximum(m, jnp.max(s, axis=-1, keepdims=True))
                acc = jnp.exp2(m - m_new) * acc + _nn(jnp.exp2(s - m_new).astype(BF16), vh)
                return m_new, acc

            init = (jnp.full((TA, 1), NEG, F32), jnp.zeros((TA, LANE), F32))
            carry = lax.fori_loop(0, qi, lambda kb, c: step(kb, c, False), init)
            m, acc = step(qi, carry, True)
            l = _rsum(jnp.where(lane == ones_lane, acc, 0.0))
            o_tot = o_tot + jnp.where(hm, acc, 0.0) / l
            lse_ref[:, cs] = jnp.broadcast_to(m * (1.0 / LOG2E) + jnp.log(l), (TA, LANE))
        o_ref[...] = o_tot
        y_ref[...] = (o_tot * _silu(g_ref[...])).astype(BF16)

    blk = pl.BlockSpec((TA, LANE), lambda p, i: (i, p))
    return pl.pallas_call(
        body, name="mla_attn_fwd", grid=(4, nq),
        in_specs=[pl.BlockSpec((TA, 256), lambda p, i: (i, p)), pl.BlockSpec((S, 256), lambda p, i: (0, p)),
                  pl.BlockSpec((S, LANE), lambda p, i: (0, p)), pl.BlockSpec((TA, LANE), lambda p, i: (i, C_MLAG + p))],
        out_specs=[blk, pl.BlockSpec((TA, 256), lambda p, i: (i, p)), blk],
        out_shape=[jax.ShapeDtypeStruct((S, 512), F32), jax.ShapeDtypeStruct((S, 1024), F32), jax.ShapeDtypeStruct((S, 512), BF16)],
        compiler_params=_cp(dimension_semantics=("parallel", "arbitrary")),
    )(q, k, v, zp)


def _mla_post_bwd(zp, o, dy, dz):
    S = zp.shape[0]
    T = T_ROW

    def body(g_ref, o_ref, dy_ref, dz_in, dz_ref, do_ref, D_ref):
        del dz_in
        g, o_, dy_ = g_ref[...], o_ref[...], dy_ref[...]
        do = dy_ * _silu(g)
        do_ref[...] = do.astype(BF16)
        dz_ref[...] = (dy_ * o_ * _dsilu(g)).astype(BF16)
        prod = do * o_
        lane = _lane((T, LANE))
        for p in range(4):
            pr = prod[:, p * LANE:(p + 1) * LANE]
            da = _rsum(jnp.where(lane < 64, pr, 0.0))
            db = _rsum(jnp.where(lane >= 64, pr, 0.0))
            D_ref[:, 2 * p * LANE:(2 * p + 1) * LANE] = jnp.broadcast_to(da, (T, LANE))
            D_ref[:, (2 * p + 1) * LANE:(2 * p + 2) * LANE] = jnp.broadcast_to(db, (T, LANE))

    row = pl.BlockSpec((T, 512), lambda i: (i, 0))
    zc = _zcol(T, 512, C_MLAG)
    return pl.pallas_call(
        body, name="mla_post_bwd", grid=(S // T,), in_specs=[zc, row, row, pl.BlockSpec(memory_space=pl.ANY)],
        out_specs=[zc, row, pl.BlockSpec((T, 1024), lambda i: (i, 0))],
        out_shape=[jax.ShapeDtypeStruct(dz.shape, BF16), jax.ShapeDtypeStruct((S, 512), BF16), jax.ShapeDtypeStruct((S, 1024), F32)],
        input_output_aliases={3: 0}, compiler_params=_cp(),
    )(zp, o, dy, dz)


def _mla_attn_bwd(q, k, v, do, lse, Dr):
    S = q.shape[0]
    nq = S // TA

    def body(q_ref, do_ref, lse_ref, D_ref, k_ref, v_ref, dq_ref, dk_ref, dv_ref):
        ki = pl.program_id(1)

        @pl.when(ki == 0)
        def _():
            dq_ref[...] = jnp.zeros_like(dq_ref)

        lane = _lane((TA, LANE))
        rowi = lax.broadcasted_iota(jnp.int32, (TA, TA), 0)
        coli = lax.broadcasted_iota(jnp.int32, (TA, TA), 1)
        dv_tot = jnp.zeros((TA, LANE), F32)
        for hh in range(2):
            cs = slice(hh * LANE, (hh + 1) * LANE)
            hm = (lane < 64) if hh == 0 else (lane >= 64)
            kh = k_ref[:, cs]
            vv = v_ref[...]
            vm = jnp.where(hm, vv, jnp.zeros_like(vv))

            def step(qb, carry, masked, cs=cs, kh=kh, vm=vm):
                dk_acc, dv_acc = carry
                off = pl.multiple_of(qb * TA, TA)
                qh = q_ref[pl.ds(off, TA), cs]
                doh = do_ref[pl.ds(off, TA), :]
                ls = jnp.tile(lse_ref[pl.ds(off, TA), cs], (1, TA // LANE))
                dd = jnp.tile(D_ref[pl.ds(off, TA), cs], (1, TA // LANE))
                s = _nt(qh, kh) * MLA_SCALE
                if masked:
                    s = jnp.where(rowi >= coli, s, NEG)
                p = jnp.exp(s - ls)
                dp = _nt(doh, vm)
                ds = (p * (dp - dd) * MLA_SCALE).astype(BF16)
                dv_acc = dv_acc + _tn(p.astype(BF16), doh)
                dk_acc = dk_acc + _tn(ds, qh)
                dq_ref[pl.ds(off, TA), cs] += _nn(ds, kh)
                return dk_acc, dv_acc

            z = jnp.zeros((TA, LANE), F32)
            carry = step(ki, (z, z), True)
            dk_acc, dv_acc = lax.fori_loop(ki + 1, nq, lambda qb, c: step(qb, c, False), carry)
            dk_ref[:, cs] = dk_acc
            dv_tot = dv_tot + jnp.where(hm, dv_acc, 0.0)
        dv_ref[...] = dv_tot

    pair = pl.BlockSpec((S, 256), lambda p, i: (0, p))
    return pl.pallas_call(
        body, name="mla_attn_bwd", grid=(4, nq),
        in_specs=[pair, pl.BlockSpec((S, LANE), lambda p, i: (0, p)), pair, pair,
                  pl.BlockSpec((TA, 256), lambda p, i: (i, p)), pl.BlockSpec((TA, LANE), lambda p, i: (i, p))],
        out_specs=[pair, pl.BlockSpec((TA, 256), lambda p, i: (i, p)), pl.BlockSpec((TA, LANE), lambda p, i: (i, p))],
        out_shape=[jax.ShapeDtypeStruct((S, 1024), F32), jax.ShapeDtypeStruct((S, 1024), F32), jax.ShapeDtypeStruct((S, 512), F32)],
        compiler_params=_cp(dimension_semantics=("parallel", "arbitrary")),
    )(q, do, lse, Dr, k, v)


def _mla_pre_bwd(zp, dq, dk, dv, w, tab, dz):
    S = zp.shape[0]
    T = T_MLA

    def body(cq_ref, ckv_ref, kr_ref, dq_ref, dk_ref, dv_ref, gcq_ref, gckv_ref, wuq_ref, wuk_ref, wuv_ref, gq_ref, gk_ref,
             C_ref, S1_ref, S2_ref, dz_in, dz_ref, dwuq_ref, dwuk_ref, dwuv_ref, dgcq_ref, dgckv_ref, dgq_ref, dgk_ref):
        del dz_in
        i = pl.program_id(0)

        @pl.when(i == 0)
        def _():
            for r in (dwuq_ref, dwuk_ref, dwuv_ref, dgcq_ref, dgckv_ref, dgq_ref, dgk_ref):
                r[...] = jnp.zeros_like(r)

        cq = cq_ref[...]
        rq = lax.rsqrt(jnp.mean(cq * cq, axis=-1, keepdims=True) + EPS)
        cqh = cq * rq
        cqn = (cqh * gcq_ref[...]).astype(BF16)
        ckv = ckv_ref[...]
        rkv = lax.rsqrt(jnp.mean(ckv * ckv, axis=-1, keepdims=True) + EPS)
        ckvh = ckv * rkv
        ckvn = (ckvh * gckv_ref[...]).astype(BF16)
        q0 = _nn(cqn, wuq_ref[...])
        k0 = _nn(ckvn, wuk_ref[...])
        krb = kr_ref[...]
        C, S1, S2 = C_ref[...], S1_ref[...], S2_ref[...]
        gq, gk = gq_ref[...], gk_ref[...]

        def head_bwd(x, dy, g):
            r = lax.rsqrt(_rsum_mxu(x * x) * (1.0 / MLA_QK) + EPS)
            xn = x * r
            dyn = _rope_t(dy, C, S1, S2, 16)
            dxh = dyn * g
            return r * (dxh - xn * _rsum_mxu(dxh * xn) * (1.0 / MLA_QK)), _csum(dyn * xn)

        dq0, dk0 = [], []
        dgq_acc = jnp.zeros((1, LANE), F32)
        dgk_acc = jnp.zeros((1, LANE), F32)
        dkr = jnp.zeros((T, LANE), F32)
        for h in range(8):
            sl = slice(h * LANE, (h + 1) * LANE)
            dxq, gq_p = head_bwd(q0[:, sl], dq_ref[:, sl], gq)
            dxk, gk_p = head_bwd(k0[:, sl] + krb, dk_ref[:, sl], gk)
            dq0.append(dxq.astype(BF16))
            dk0.append(dxk.astype(BF16))
            dkr = dkr + dxk
            dgq_acc = dgq_acc + gq_p
            dgk_acc = dgk_acc + gk_p
        dgq_ref[...] += dgq_acc
        dgk_ref[...] += dgk_acc
        dq0 = jnp.concatenate(dq0, axis=1)
        dk0 = jnp.concatenate(dk0, axis=1)
        dvb = dv_ref[...].astype(BF16)
        dwuq_ref[...] += _tn(cqn, dq0)
        dwuk_ref[...] += _tn(ckvn, dk0)
        dwuv_ref[...] += _tn(ckvn, dvb)
        dcqn = _nt(dq0, wuq_ref[...])
        dckvn = _nt(dk0, wuk_ref[...]) + _nt(dvb, wuv_ref[...])
        dgcq_ref[...] += _csum(dcqn * cqh)
        dgckv_ref[...] += _csum(dckvn * ckvh)
        dxh = dcqn * gcq_ref[...]
        dz_ref[:, 0:256] = (rq * (dxh - cqh * jnp.mean(dxh * cqh, axis=-1, keepdims=True))).astype(BF16)
        dxh = dckvn * gckv_ref[...]
        dz_ref[:, 256:384] = (rkv * (dxh - ckvh * jnp.mean(dxh * ckvh, axis=-1, keepdims=True))).astype(BF16)
        lane = _lane((T, LANE))
        dz_ref[:, 384:512] = jnp.where((lane >= KR_LANE) & (lane < KR_LANE + 32), dkr, 0.0).astype(BF16)

    tabspec = pl.BlockSpec((T, LANE), lambda i: (i, 0))
    in_specs = [_zcol(T, 256, C_CQ), _zcol(T, LANE, C_CKV), _zcol(T, LANE, C_KR),
                pl.BlockSpec((T, 1024), lambda i: (i, 0)), pl.BlockSpec((T, 1024), lambda i: (i, 0)), pl.BlockSpec((T, 512), lambda i: (i, 0)),
                _full((1, 256)), _full((1, LANE)), _full((256, 1024)), _full((LANE, 1024)), _full((LANE, 512)), _full((1, LANE)), _full((1, LANE)),
                tabspec, tabspec, tabspec, pl.BlockSpec(memory_space=pl.ANY)]
    out_specs = [_zcol(T, 512, C_CQ), _full((256, 1024)), _full((LANE, 1024)), _full((LANE, 512)), _full((1, 256)), _full((1, LANE)),
                 _full((1, LANE)), _full((1, LANE))]
    out_shape = [jax.ShapeDtypeStruct(dz.shape, BF16), jax.ShapeDtypeStruct((256, 1024), F32), jax.ShapeDtypeStruct((LANE, 1024), F32),
                 jax.ShapeDtypeStruct((LANE, 512), F32), jax.ShapeDtypeStruct((1, 256), F32), jax.ShapeDtypeStruct((1, LANE), F32),
                 jax.ShapeDtypeStruct((1, LANE), F32), jax.ShapeDtypeStruct((1, LANE), F32)]
    return pl.pallas_call(
        body, name="mla_pre_bwd", grid=(S // T,), in_specs=in_specs, out_specs=out_specs, out_shape=out_shape,
        input_output_aliases={16: 0}, compiler_params=_cp(),
    )(zp, zp, zp, dq, dk, dv, w["g_cq"], w["g_ckv"], w["w_uq"], w["w_uk"], w["w_uv"], w["g_mq"], w["g_mk"], *tab, dz)


T_DIL = 512


def _head_stats(x, lane):
    sq = x * x
    sa = _rsum(jnp.where(lane < 64, sq, 0.0))
    sb = _rsum(jnp.where(lane >= 64, sq, 0.0))
    return lax.rsqrt(jnp.where(lane < 64, sa, sb) * (1.0 / DIL_HD) + EPS)


def _head_sum(x, lane):
    sa = _rsum(jnp.where(lane < 64, x, 0.0))
    sb = _rsum(jnp.where(lane >= 64, x, 0.0))
    return jnp.where(lane < 64, sa, sb)


def _head_stats_mxu(x):
    r = lax.broadcasted_iota(jnp.int32, (LANE, LANE), 0)
    c = lax.broadcasted_iota(jnp.int32, (LANE, LANE), 1)
    ones = jnp.where((r < 64) == (c < 64), 1.0, 0.0).astype(F32)
    ss = lax.dot_general(x * x, ones, (((1,), (0,)), ((), ())), precision=lax.Precision.HIGHEST, preferred_element_type=F32)
    return lax.rsqrt(ss * (1.0 / DIL_HD) + EPS)


def _dil_pre_fwd(zp, w, tab):
    S = zp.shape[0]
    T = T_DIL

    def body(q_ref, k_ref, gq_ref, gk_ref, C_ref, S1_ref, S2_ref, qo_ref, ko_ref):
        C, S1, S2 = C_ref[...], S1_ref[...], S2_ref[...]
        for b in range(12):
            sl = slice(b * LANE, (b + 1) * LANE)
            x = q_ref[:, sl]
            qo_ref[:, sl] = _rope(x * _head_stats_mxu(x) * gq_ref[...], C, S1, S2, 32)
            x = k_ref[:, sl]
            ko_ref[:, sl] = _rope(x * _head_stats_mxu(x) * gk_ref[...], C, S1, S2, 32)

    tabspec = pl.BlockSpec((T, LANE), lambda i: (i, 0))
    out = pl.BlockSpec((T, 1536), lambda i: (i, 0))
    return pl.pallas_call(
        body, name="dil_pre_fwd", grid=(S // T,),
        in_specs=[_zcol(T, 1536, C_DQ), _zcol(T, 1536, C_DK), _full((1, LANE)), _full((1, LANE)), tabspec, tabspec, tabspec],
        out_specs=[out, out], out_shape=[jax.ShapeDtypeStruct((S, 1536), F32)] * 2, compiler_params=_cp(),
    )(zp, zp, w["g_dq"], w["g_dk"], *tab)


DIL_ROWS = 2048


def _dil_geometry(g, S):
    d = DIL_DILATIONS[g]
    P = NK * d
    return d, P, DIL_ROWS // P, S // P


def _dil_rows(start, d, blocks=1):
    return pl.ds(pl.multiple_of(start, NK), blocks * NK) if d == 1 else pl.ds(start, blocks * NK, stride=d)


def _dil_specs(g, S, col0):
    _, P, m, nb = _dil_geometry(g, S)
    cur = pl.BlockSpec((DIL_ROWS, LANE), lambda sb, c: (sb, col0 + c))
    prv = pl.BlockSpec((P, LANE), lambda sb, c: (jnp.maximum(sb * m - 1, 0), col0 + c))
    nxt = pl.BlockSpec((P, LANE), lambda sb, c: (jnp.minimum((sb + 1) * m, nb - 1), col0 + c))
    return cur, prv, nxt


def _dil_attn_fwd(q, k, zp, g):
    S = q.shape[0]
    d, P, m, nb = _dil_geometry(g, S)
    R = DIL_ROWS

    def body(q_ref, kc_ref, kp_ref, vc_ref, vp_ref, o_ref, lse_ref, *scr):
        sb = pl.program_id(0)
        if m > 1:
            ks_ref, vs_ref = scr
            ks_ref[0:P, :] = kp_ref[...]
            ks_ref[P:P + R, :] = kc_ref[...]
            vs_ref[0:P, :] = vp_ref[...]
            vs_ref[P:P + R, :] = vc_ref[...]
        lane = _lane((NK, LANE))

        def unit(u, carry):
            j = u // d
            start = j * P + (u - j * d)
            rows = _dil_rows(start, d)
            if m > 1:
                k2, v2 = ks_ref[_dil_rows(start, d, 2), :], vs_ref[_dil_rows(start, d, 2), :]
            else:
                k2 = jnp.concatenate([kp_ref[rows, :], kc_ref[rows, :]], axis=0)
                v2 = jnp.concatenate([vp_ref[rows, :], vc_ref[rows, :]], axis=0)
            k2, v2 = k2.astype(BF16), v2.astype(BF16)
            q_ = q_ref[rows, :].astype(BF16)
            row = lax.broadcasted_iota(jnp.int32, (NK, 2 * NK), 0)
            col = lax.broadcasted_iota(jnp.int32, (NK, 2 * NK), 1)
            band = (col >= row) & (col <= row + NK) & ((col >= NK) | (sb * m + j > 0))
            lane2 = _lane((2 * NK, LANE))
            zb, zv = jnp.zeros_like(q_), jnp.zeros_like(v2)
            o_tot = jnp.zeros((NK, LANE), F32)
            lse_tot = jnp.zeros((NK, LANE), F32)
            for hh in range(2):
                hm = (lane < 64) if hh == 0 else (lane >= 64)
                hm2 = (lane2 < 64) if hh == 0 else (lane2 >= 64)
                s_ = jnp.where(band, _nt(jnp.where(hm, q_, zb), k2) * DIL_SCALE, NEG)
                mx = jnp.max(s_, axis=-1, keepdims=True)
                e = jnp.exp(s_ - mx)
                den = _rsum(e)
                o_tot = o_tot + _nn(e.astype(BF16), jnp.where(hm2, v2, zv)) / den
                lse_tot = jnp.where(hm, mx + jnp.log(den), lse_tot)
            o_ref[rows, :] = o_tot
            lse_ref[rows, :] = lse_tot
            return carry

        lax.fori_loop(0, R // NK, unit, 0, unroll=R // NK)

    qcur, qprv, _ = _dil_specs(g, S, 4 * g)
    vcur, vprv, _ = _dil_specs(g, S, C_DV + 4 * g)
    out = pl.BlockSpec((R, LANE), lambda sb, c: (sb, c))
    return pl.pallas_call(
        body, name=f"dil_attn_fwd{g}", grid=(S // R, 4), in_specs=[qcur, qcur, qprv, vcur, vprv], out_specs=[out, out],
        out_shape=[jax.ShapeDtypeStruct((S, 512), F32)] * 2,
        scratch_shapes=[pltpu.VMEM((P + R, LANE), F32)] * 2 if m > 1 else [], compiler_params=_cp(),
    )(q, k, k, zp, zp)


def _dil_combine(os_, ls_, zp):
    S = zp.shape[0]
    T = T_ROW

    def body(o0, o1, o2, l0, l1, l2, g_ref, oc_ref, L_ref, y_ref):
        a, b, c = l0[...], l1[...], l2[...]
        mx = jnp.maximum(jnp.maximum(a, b), c)
        ea, eb, ec = jnp.exp(a - mx), jnp.exp(b - mx), jnp.exp(c - mx)
        den = ea + eb + ec
        oc = (ea * o0[...] + eb * o1[...] + ec * o2[...]) / den
        oc_ref[...] = oc
        L_ref[...] = mx + jnp.log(den)
        y_ref[...] = (oc * _silu(g_ref[...])).astype(BF16)

    row = pl.BlockSpec((T, 512), lambda i: (i, 0))
    return pl.pallas_call(
        body, name="dil_combine", grid=(S // T,), in_specs=[row] * 6 + [_zcol(T, 512, C_DILG)], out_specs=[row, row, row],
        out_shape=[jax.ShapeDtypeStruct((S, 512), F32), jax.ShapeDtypeStruct((S, 512), F32), jax.ShapeDtypeStruct((S, 512), BF16)],
        compiler_params=_cp(),
    )(*os_, *ls_, zp)


def _dil_comb_bwd(zp, oc, dy, dz):
    S = zp.shape[0]
    T = T_ROW

    def body(g_ref, o_ref, dy_ref, dz_in, dz_ref, do_ref, D_ref):
        del dz_in
        g, o_, dy_ = g_ref[...], o_ref[...], dy_ref[...]
        do = dy_ * _silu(g)
        do_ref[...] = do
        dz_ref[...] = (dy_ * o_ * _dsilu(g)).astype(BF16)
        lane = _lane((T, LANE))
        for p in range(4):
            sl = slice(p * LANE, (p + 1) * LANE)
            D_ref[:, sl] = _head_sum(do[:, sl] * o_[:, sl], lane)

    row = pl.BlockSpec((T, 512), lambda i: (i, 0))
    zc = _zcol(T, 512, C_DILG)
    return pl.pallas_call(
        body, name="dil_comb_bwd", grid=(S // T,), in_specs=[zc, row, row, pl.BlockSpec(memory_space=pl.ANY)], out_specs=[zc, row, row],
        out_shape=[jax.ShapeDtypeStruct(dz.shape, BF16), jax.ShapeDtypeStruct((S, 512), F32), jax.ShapeDtypeStruct((S, 512), F32)],
        input_output_aliases={3: 0}, compiler_params=_cp(),
    )(zp, oc, dy, dz)


def _dil_attn_bwd(q, k, zp, do, L, Dr, g):
    S = q.shape[0]
    d, P, m, nb = _dil_geometry(g, S)
    R = DIL_ROWS
    n_q, n_k = 4, 2

    def body(*refs):
        q_side = refs[0:2 * n_q]
        k_side = refs[2 * n_q:2 * n_q + 2 * n_k]
        dq_ref, dk_ref, dv_ref = refs[2 * n_q + 2 * n_k:2 * n_q + 2 * n_k + 3]
        scr = refs[2 * n_q + 2 * n_k + 3:]
        sb = pl.program_id(0)
        if m > 1:
            for a in range(n_q):
                scr[a][0:R, :] = q_side[2 * a][...]
                scr[a][R:R + P, :] = q_side[2 * a + 1][...]
            for a in range(n_k):
                scr[n_q + a][0:P, :] = k_side[2 * a + 1][...]
                scr[n_q + a][P:P + R, :] = k_side[2 * a][...]
        lane = _lane((NK, LANE))

        def unit(u, carry):
            j = u // d
            start = j * P + (u - j * d)
            rows = _dil_rows(start, d)
            if m > 1:
                rows_b = _dil_rows(start + P, d)
                q2, do2, L2, D2 = [scr[a][_dil_rows(start, d, 2), :] for a in range(n_q)]
                kp, vp = [scr[n_q + a][rows, :] for a in range(n_k)]
                kc, vc = [scr[n_q + a][rows_b, :] for a in range(n_k)]
            else:
                q2, do2, L2, D2 = [jnp.concatenate([q_side[2 * a][rows, :], q_side[2 * a + 1][rows, :]], axis=0) for a in range(n_q)]
                kc, vc = [k_side[2 * a][rows, :] for a in range(n_k)]
                kp, vp = [k_side[2 * a + 1][rows, :] for a in range(n_k)]
            q2, do2 = q2.astype(BF16), do2.astype(BF16)
            kc, kp, vc, vp = kc.astype(BF16), kp.astype(BF16), vc.astype(BF16), vp.astype(BF16)
            n = sb * m + j
            hA = _lane((2 * NK, LANE)) < 64
            zq = jnp.zeros_like(q2)
            L2r, D2r = pltpu.roll(L2, 64, 1), pltpu.roll(D2, 64, 1)
            Q4 = jnp.concatenate([jnp.where(hA, q2, zq), jnp.where(hA, zq, q2)], axis=0)
            O4 = jnp.concatenate([jnp.where(hA, do2, zq), jnp.where(hA, zq, do2)], axis=0)
            L4 = jnp.concatenate([jnp.where(hA, L2, L2r), jnp.where(hA, L2r, L2)], axis=0)
            D4 = jnp.concatenate([jnp.where(hA, D2, D2r), jnp.where(hA, D2r, D2)], axis=0)
            row4 = lax.broadcasted_iota(jnp.int32, (4 * NK, NK), 0) & (2 * NK - 1)
            col4 = lax.broadcasted_iota(jnp.int32, (4 * NK, NK), 1)
            m4 = ((row4 < NK) & (col4 <= row4)) | ((row4 >= NK) & (col4 >= row4 - NK) & (n < nb - 1))
            p4 = jnp.exp(jnp.where(m4, _nt(Q4, kc) * DIL_SCALE, NEG) - L4)
            ds4 = (p4 * (_nt(O4, vc) - D4) * DIL_SCALE).astype(BF16)
            dk_tot = _tn(ds4, Q4)
            dv_tot = _tn(p4.astype(BF16), O4)
            pick = lambda x: jnp.concatenate([x[0:NK], x[2 * NK:3 * NK]], axis=0)
            Qn, On, Ln, Dn = pick(Q4), pick(O4), pick(L4), pick(D4)
            rowp = lax.broadcasted_iota(jnp.int32, (2 * NK, NK), 0) & (NK - 1)
            colp = lax.broadcasted_iota(jnp.int32, (2 * NK, NK), 1)
            pp = jnp.exp(jnp.where((colp >= rowp) & (n > 0), _nt(Qn, kp) * DIL_SCALE, NEG) - Ln)
            dsp = (pp * (_nt(On, vp) - Dn) * DIL_SCALE).astype(BF16)
            dq2 = _nn(pick(ds4), kc) + _nn(dsp, kp)
            dq_tot = jnp.where(lane < 64, dq2[0:NK], dq2[NK:2 * NK])
            dq_ref[rows, :] = dq_tot
            dk_ref[rows, :] = dk_tot
            dv_ref[rows, :] = dv_tot
            return carry

        lax.fori_loop(0, R // NK, unit, 0, unroll=R // NK)

    qcur, qprv, qnxt = _dil_specs(g, S, 4 * g)
    vcur, vprv, _ = _dil_specs(g, S, C_DV + 4 * g)
    ocur, _, onxt = _dil_specs(g, S, 0)
    out = pl.BlockSpec((R, LANE), lambda sb, c: (sb, c))
    scratch = [pltpu.VMEM((P + R, LANE), F32)] * (n_q + n_k) if m > 1 else []
    return pl.pallas_call(
        body, name=f"dil_attn_bwd{g}", grid=(S // R, 4),
        in_specs=[qcur, qnxt, ocur, onxt, ocur, onxt, ocur, onxt, qcur, qprv, vcur, vprv],
        out_specs=[out, out, out], out_shape=[jax.ShapeDtypeStruct((S, 512), F32)] * 3, scratch_shapes=scratch, compiler_params=_cp(),
    )(q, q, do, do, L, L, Dr, Dr, k, k, zp, zp)


def _dil_pre_bwd(zp, dys, g, tab, dz, col, name):
    S = zp.shape[0]
    T = T_DIL

    def body(x_ref, dy0_ref, dy1_ref, dy2_ref, g_ref, C_ref, S1_ref, S2_ref, dz_in, dz_ref, dg_ref):
        del dz_in
        i = pl.program_id(0)
        C, S1, S2 = C_ref[...], S1_ref[...], S2_ref[...]
        lane = _lane((T, LANE))
        gv = g_ref[...]
        acc = jnp.zeros((1, LANE), F32)
        for b in range(12):
            sl = slice(b * LANE, (b + 1) * LANE)
            x = x_ref[:, sl]
            r = _head_stats(x, lane)
            xn = x * r
            dy_ref = (dy0_ref, dy1_ref, dy2_ref)[b // 4]
            dyn = _rope_t(dy_ref[:, (b % 4) * LANE:(b % 4 + 1) * LANE], C, S1, S2, 32)
            acc = acc + _csum(dyn * xn)
            dxh = dyn * gv
            dz_ref[:, sl] = (r * (dxh - xn * _head_sum(dxh * xn, lane) * (1.0 / DIL_HD))).astype(BF16)

        @pl.when(i == 0)
        def _():
            dg_ref[...] = acc

        @pl.when(i > 0)
        def _():
            dg_ref[...] += acc

    tabspec = pl.BlockSpec((T, LANE), lambda i: (i, 0))
    zc = _zcol(T, 1536, col)
    grp = pl.BlockSpec((T, 512), lambda i: (i, 0))
    return pl.pallas_call(
        body, name=name, grid=(S // T,),
        in_specs=[zc, grp, grp, grp, _full((1, LANE)), tabspec, tabspec, tabspec, pl.BlockSpec(memory_space=pl.ANY)],
        out_specs=[zc, _full((1, LANE))], out_shape=[jax.ShapeDtypeStruct(dz.shape, BF16), jax.ShapeDtypeStruct((1, LANE), F32)],
        input_output_aliases={8: 0}, compiler_params=_cp(),
    )(zp, *dys, g, *tab, dz)


def _dil_dv_into(dvs, dz):
    S = dz.shape[0]
    T = T_ROW

    def body(s0, s1, s2, dz_in, o_ref):
        del dz_in
        for gi, s in enumerate((s0, s1, s2)):
            o_ref[:, gi * 512:(gi + 1) * 512] = s[...].astype(BF16)

    grp = pl.BlockSpec((T, 512), lambda i: (i, 0))
    return pl.pallas_call(
        body, name="dil_dv", grid=(S // T,), in_specs=[grp, grp, grp, pl.BlockSpec(memory_space=pl.ANY)],
        out_specs=_zcol(T, 1536, C_DV), out_shape=jax.ShapeDtypeStruct(dz.shape, BF16), input_output_aliases={3: 0}, compiler_params=_cp(),
    )(*dvs, dz)


T_MRG = 512


def _merge_fwd(P, zp, b_merge):
    S = zp.shape[0]
    T = T_MRG

    def body(p0, p1, p2, m0, m1, m2, b_ref, o_ref):
        acc = jnp.zeros((T, D), F32)
        for j, (p, m) in enumerate(((p0, m0), (p1, m1), (p2, m2))):
            acc = acc + _sig(m[...] + b_ref[:, j * D:(j + 1) * D]) * p[...].astype(F32)
        o_ref[...] = acc.astype(BF16)

    row = pl.BlockSpec((T, D), lambda i: (i, 0))
    return pl.pallas_call(
        body, name="merge_fwd", grid=(S // T,),
        in_specs=[row, row, row] + [_zcol(T, D, C_MERGE + 8 * j) for j in range(3)] + [_full((1, 3 * D))], out_specs=row,
        out_shape=jax.ShapeDtypeStruct((S, D), BF16), compiler_params=_cp(),
    )(*P, zp, zp, zp, b_merge)


def _merge_bwd(dm, Pj, zp, bj, dz, j):
    S = zp.shape[0]
    T = T_MRG

    def body(dm_ref, p_ref, m_ref, b_ref, dz_in, dz_ref, dp_ref, db_ref):
        del dz_in
        i = pl.program_id(0)
        g = _sig(m_ref[...] + b_ref[...])
        dmv = dm_ref[...].astype(F32)
        dp_ref[...] = (dmv * g).astype(BF16)
        dg = dmv * p_ref[...].astype(F32) * g * (1.0 - g)
        dz_ref[...] = dg.astype(BF16)
        part = _csum(dg)

        @pl.when(i == 0)
        def _():
            db_ref[...] = part

        @pl.when(i > 0)
        def _():
            db_ref[...] += part

    row = pl.BlockSpec((T, D), lambda i: (i, 0))
    zc = _zcol(T, D, C_MERGE + 8 * j)
    return pl.pallas_call(
        body, name=f"merge_bwd{j}", grid=(S // T,), in_specs=[row, row, zc, _full((1, D)), pl.BlockSpec(memory_space=pl.ANY)],
        out_specs=[zc, row, _full((1, D))],
        out_shape=[jax.ShapeDtypeStruct(dz.shape, BF16), jax.ShapeDtypeStruct((S, D), BF16), jax.ShapeDtypeStruct((1, D), F32)],
        input_output_aliases={4: 0}, compiler_params=_cp(),
    )(dm, Pj, zp, bj, dz)


def _loss_fwd_bwd(y, target):
    S = y.shape[0]
    T = T_ROW

    def body(y_ref, t_ref, loss_ref, dy_ref):
        i = pl.program_id(0)
        err = y_ref[...] - t_ref[...]
        dy_ref[...] = err * (1.0 / D)
        part = jnp.sum(err * err, keepdims=True).reshape(1, 1) * (0.5 / D)

        @pl.when(i == 0)
        def _():
            loss_ref[...] = part

        @pl.when(i > 0)
        def _():
            loss_ref[...] += part

    row = pl.BlockSpec((T, D), lambda i: (i, 0))
    return pl.pallas_call(
        body, name="loss", grid=(S // T,), in_specs=[row, row], out_specs=[_full((1, 1)), row],
        out_shape=[jax.ShapeDtypeStruct((1, 1), F32), jax.ShapeDtypeStruct((S, D), F32)], compiler_params=_cp(),
    )(y, target)


def _layer_fwd(x, w, tabs):
    mla_tab, dil_tab = tabs
    S = x.shape[0]
    h = _rms_in_fwd(x, w["norm_g"])
    zp = _mm(h, w["w_in"], mode="nn", name="in_proj")
    hs, y_lru = _lru_fwd(zp, w)
    q, k, v = _mla_pre_fwd(zp, w, mla_tab)
    o_mla, lse, y_mla = _mla_attn_fwd(q, k, v, zp)
    qd, kd = _dil_pre_fwd(zp, w, dil_tab)
    og, lg = zip(*[_dil_attn_fwd(qd, kd, zp, g) for g in range(len(DIL_DILATIONS))])
    oc, L, y_dil = _dil_combine(og, lg, zp)
    P = [_mm(y_lru, w["w_lru_o"], mode="nn", name="lru_out", out_dtype=BF16),
         _mm(y_mla, w["w_mla_o"], mode="nn", name="mla_out", out_dtype=BF16),
         _mm(y_dil, w["w_dil_o"], mode="nn", name="dil_out", out_dtype=BF16)]
    merged = _merge_fwd(P, zp, w["b_merge"])
    x_out = _mm(merged, w["w_out"], mode="nn", name="out_proj", add=x)
    saved = dict(x=x, h=h, zp=zp, hs=hs, y=(y_lru, y_mla, y_dil), q=q, k=k, v=v, o_mla=o_mla, lse=lse, qd=qd, kd=kd, oc=oc, L=L, P=P,
                 merged=merged)
    return x_out, saved


def _layer_bwd(dout, w, tabs, sv, hook=None, after=None):
    mla_tab, dil_tab = tabs
    zp = sv["zp"]
    S = zp.shape[0]
    g = {}
    dm = _mm(dout, w["w_out"], mode="nt", name="d_merged", after=after, out_dtype=BF16)
    g["w_out"] = _mm(sv["merged"], dout, mode="tn", name="dw_out", out_dtype=BF16)
    dz = lax.empty((S, ZW), BF16)
    dP, db = [], []
    for j in range(3):
        dz, dpj, dbj = _merge_bwd(dm, sv["P"][j], zp, w["b_merge"][:, j * D:(j + 1) * D], dz, j)
        dP.append(dpj)
        db.append(dbj)
    g["b_merge"] = jnp.concatenate(db, axis=1)
    names = ("w_lru_o", "w_mla_o", "w_dil_o")
    dy = []
    for j in range(3):
        dy.append(_mm(dP[j], w[names[j]], mode="nt", name="dy_" + names[j]))
        g[names[j]] = _mm(sv["y"][j], dP[j], mode="tn", name="d" + names[j], out_dtype=BF16)
    dz = _lru_gate_bwd(zp, sv["hs"], dy[0], dz)
    dz, g["conv_w"], g["conv_b"], g["w_gx"], g["b_gx"], g["w_ga"], g["b_ga"], g["lam"] = _lru_bwd(zp, sv["hs"], dy[0], w, dz)
    dz, do, Dr = _mla_post_bwd(zp, sv["o_mla"], dy[1], dz)
    dq, dk, dv = _mla_attn_bwd(sv["q"], sv["k"], sv["v"], do, sv["lse"], Dr)
    dz, g["w_uq"], g["w_uk"], g["w_uv"], g["g_cq"], g["g_ckv"], g["g_mq"], g["g_mk"] = _mla_pre_bwd(zp, dq, dk, dv, w, mla_tab, dz)
    dz, dod, Dd = _dil_comb_bwd(zp, sv["oc"], dy[2], dz)
    dqs, dks, dvs = zip(*[_dil_attn_bwd(sv["qd"], sv["kd"], zp, dod, sv["L"], Dd, gi) for gi in range(len(DIL_DILATIONS))])
    dz, g["g_dq"] = _dil_pre_bwd(zp, dqs, w["g_dq"], dil_tab, dz, C_DQ, "dil_pre_bwd_q")
    dz, g["g_dk"] = _dil_pre_bwd(zp, dks, w["g_dk"], dil_tab, dz, C_DK, "dil_pre_bwd_k")
    dz = _dil_dv_into(dvs, dz)
    g["w_in"] = _mm(sv["h"], dz, mode="tn", name="dw_in", out_dtype=BF16, tk=S)
    token = hook(g) if hook is not None else None
    dh = _mm(dz, w["w_in"], mode="nt", name="d_h", after=token, tk=ZW // 4)
    dx, g["norm_g"] = _rms_in_bwd(sv["x"], w["norm_g"], dh, dout)
    return dx, g


def _peers():
    mx, my, mc = lax.axis_index("x"), lax.axis_index("y"), lax.axis_index("c")
    me = 4 * mx + 2 * my + mc
    out = []
    for k in range(1, N_DEV):
        px = 1 - mx if k & 4 else mx
        py = 1 - my if k & 2 else my
        pc = 1 - mc if k & 1 else mc
        out.append(((px, py, pc), 4 * px + 2 * py + pc))
    return me, out


def _whole(ref, p):
    del p
    return ref


def _exchange(srcs, slicers, slices, name):
    n = len(srcs)

    def body(*refs):
        ins, outs = refs[:n], refs[n:2 * n]
        send_sems, recv_sems, local_sems = refs[2 * n:]
        me, peers = _peers()
        mine = [pltpu.make_async_copy(slicers[a](ins[a], me), outs[a].at[me], local_sems.at[a]) for a in range(n)]
        for cp in mine:
            cp.start()
        copies = []
        for k, (peer, pidx) in enumerate(peers):
            for a in range(n):
                cp = pltpu.make_async_remote_copy(
                    src_ref=slicers[a](ins[a], pidx), dst_ref=outs[a].at[me], send_sem=send_sems.at[k * n + a],
                    recv_sem=recv_sems.at[k * n + a], device_id=peer, device_id_type=pl.DeviceIdType.MESH)
                cp.start()
                copies.append(cp)
        for cp in copies + mine:
            cp.wait()

    nsem = (N_DEV - 1) * n
    return pl.pallas_call(
        body, name=name, out_shape=[jax.ShapeDtypeStruct((N_DEV,) + shp, dt) for shp, dt in slices],
        in_specs=[pl.BlockSpec(memory_space=pl.ANY)] * n, out_specs=[pl.BlockSpec(memory_space=pl.ANY)] * n,
        scratch_shapes=[pltpu.SemaphoreType.DMA((nsem,)), pltpu.SemaphoreType.DMA((nsem,)), pltpu.SemaphoreType.DMA((n,))],
        compiler_params=pltpu.CompilerParams(has_side_effects=True),
    )(*srcs)


def _gather_two_level(srcs, name):
    n = len(srcs)

    def body(*refs):
        ins, outs = refs[:n], refs[n:2 * n]
        send_sems, recv_sems, local_sems = refs[2 * n:]
        mx, my, mc = lax.axis_index("x"), lax.axis_index("y"), lax.axis_index("c")
        me, sibling = (mx, my, mc), (mx, my, 1 - mc)
        chips = [(1 - mx, my), (mx, 1 - my), (1 - mx, 1 - my)]
        slot = lambda d: 4 * d[0] + 2 * d[1] + d[2]

        def copy(j, a, block, to, own=False):
            return pltpu.make_async_remote_copy(
                src_ref=ins[a] if own else outs[a].at[slot(block)], dst_ref=outs[a].at[slot(block)],
                send_sem=send_sems.at[j * n + a], recv_sem=recv_sems.at[j * n + a], device_id=to, device_id_type=pl.DeviceIdType.MESH)

        mine = [pltpu.make_async_copy(ins[a], outs[a].at[slot(me)], local_sems.at[a]) for a in range(n)]
        first = [copy(1 + j, a, me, (*chip, mc), own=True) for j, chip in enumerate(chips) for a in range(n)]
        first += [copy(0, a, me, sibling, own=True) for a in range(n)]
        for cp in mine + first:
            cp.start()
        passed = []
        for j, chip in enumerate(chips):
            for a in range(n):
                copy(1 + j, a, (*chip, mc), me).wait_recv()
                cp = copy(4 + j, a, (*chip, mc), sibling)
                cp.start()
                passed.append(cp)
        for a in range(n):
            copy(0, a, sibling, me).wait_recv()
        for j, chip in enumerate(chips):
            for a in range(n):
                copy(4 + j, a, (*chip, 1 - mc), me).wait_recv()
        for cp in first + passed:
            cp.wait_send()
        for cp in mine:
            cp.wait()

    nsem = (N_DEV - 1) * n
    return pl.pallas_call(
        body, name=name, out_shape=[jax.ShapeDtypeStruct((N_DEV,) + a.shape, a.dtype) for a in srcs],
        in_specs=[pl.BlockSpec(memory_space=pl.ANY)] * n, out_specs=[pl.BlockSpec(memory_space=pl.ANY)] * n,
        scratch_shapes=[pltpu.SemaphoreType.DMA((nsem,)), pltpu.SemaphoreType.DMA((nsem,)), pltpu.SemaphoreType.DMA((n,))],
        compiler_params=pltpu.CompilerParams(has_side_effects=True),
    )(*srcs)


_HBM = pl.BlockSpec(memory_space=pltpu.HBM)
_SEM = pl.BlockSpec(memory_space=pltpu.SEMAPHORE)
_DATAFLOW = pltpu.SideEffectType.DATAFLOW_SIDE_EFFECTING


def _plan_chips():
    mx, my, mc = lax.axis_index("x"), lax.axis_index("y"), lax.axis_index("c")
    return 2 * mx + my, [((cx, cy, mc), 2 * cx + cy) for cx, cy in ((1 - mx, my), (mx, 1 - my), (1 - mx, 1 - my))]


def _pair_exchange(srcs, slicers, slices, sliced, name):
    n = len(srcs)
    pieces = [4 if s else 1 for s in sliced]

    def body(*refs):
        ins, outs = refs[:n], refs[n:2 * n]
        send_sems, recv_sems = refs[2 * n:]
        mx, my, mc = lax.axis_index("x"), lax.axis_index("y"), lax.axis_index("c")
        copies = []
        for a in range(n):
            for q in range(pieces[a]):
                i = len(copies)
                copies.append(pltpu.make_async_remote_copy(
                    src_ref=slicers[a](ins[a], 2 * q + 1 - mc) if sliced[a] else ins[a], dst_ref=outs[a].at[q],
                    send_sem=send_sems.at[i], recv_sem=recv_sems.at[i], device_id=(mx, my, 1 - mc), device_id_type=pl.DeviceIdType.MESH))
        for cp in copies:
            cp.start()
        for cp in copies:
            cp.wait()

    return pl.pallas_call(
        body, name=name, out_shape=[jax.ShapeDtypeStruct((p,) + shp, dt) for (shp, dt), p in zip(slices, pieces)],
        in_specs=[pl.BlockSpec(memory_space=pl.ANY)] * n, out_specs=[pl.BlockSpec(memory_space=pl.ANY)] * n,
        scratch_shapes=[pltpu.SemaphoreType.DMA((sum(pieces),)), pltpu.SemaphoreType.DMA((sum(pieces),))],
        compiler_params=pltpu.CompilerParams(has_side_effects=True),
    )(*srcs)


def _pair_add(src, came, first_blk, axis, name):
    _, r, c = came.shape
    nblk = (c if axis == 1 else r) // LANE
    if axis == 1:
        s_spec = pl.BlockSpec((r, LANE), lambda q, j, fb: (0, fb[q] + j))
        o_spec = pl.BlockSpec((1, r, LANE), lambda q, j, fb: (q, 0, j))
    else:
        s_spec = pl.BlockSpec((LANE, c), lambda q, j, fb: (fb[q] + j, 0))
        o_spec = pl.BlockSpec((1, LANE, c), lambda q, j, fb: (q, j, 0))

    def body(fb_ref, x_ref, y_ref, o_ref):
        del fb_ref
        o_ref[0] = (x_ref[...].astype(F32) + y_ref[0].astype(F32)).astype(o_ref.dtype)

    return pl.pallas_call(
        body, name=name, out_shape=jax.ShapeDtypeStruct(came.shape, came.dtype),
        grid_spec=pltpu.PrefetchScalarGridSpec(num_scalar_prefetch=1, grid=(4, nblk), in_specs=[s_spec, o_spec], out_specs=o_spec),
        compiler_params=_cp(),
    )(first_blk, src, came)


def _add2(x, y, name):
    shp = x.shape
    x, y = x.reshape(-1, shp[-1]), y.reshape(-1, shp[-1])
    R, C = x.shape
    tr = R
    while tr * C * 4 > (1 << 21) and tr % 32 == 0:
        tr //= 2

    def body(x_ref, y_ref, o_ref):
        o_ref[...] = (x_ref[...].astype(F32) + y_ref[...].astype(F32)).astype(o_ref.dtype)

    spec = pl.BlockSpec((tr, C), lambda i: (i, 0))
    return pl.pallas_call(body, name=name, grid=(R // tr,), in_specs=[spec, spec], out_specs=spec,
                          out_shape=jax.ShapeDtypeStruct((R, C), x.dtype), compiler_params=_cp())(x, y).reshape(shp)


def _exchange_start(srcs, slicers, slices, after, name, plan=_peers, nslots=N_DEV):
    n = len(srcs)
    nsem = (nslots - 1) * n
    lands = [lax.empty((nslots,) + shp, dt) for shp, dt in slices]

    def body(*refs):
        ins, lands_in = refs[:n], refs[n:2 * n]
        send_sems, recv_sems, local_sems = refs[2 * n + 1], refs[2 * n + 2], refs[2 * n + 3]
        token = refs[-1]
        me, peers = plan()
        for a in range(n):
            pltpu.make_async_copy(slicers[a](ins[a], me), lands_in[a].at[me], local_sems.at[a]).start()
        for k, (peer, pidx) in enumerate(peers):
            for a in range(n):
                pltpu.make_async_remote_copy(
                    src_ref=slicers[a](ins[a], pidx), dst_ref=lands_in[a].at[me], send_sem=send_sems.at[k * n + a],
                    recv_sem=recv_sems.at[k * n + a], device_id=peer, device_id_type=pl.DeviceIdType.MESH).start()
        token[...] = jnp.zeros_like(token)

    hbm = lambda a: pltpu.with_memory_space_constraint(a, pltpu.HBM)
    return pl.pallas_call(
        body, name=name,
        out_shape=(pltpu.SemaphoreType.DMA((nsem,)), pltpu.SemaphoreType.DMA((nsem,)), pltpu.SemaphoreType.DMA((n,)),
                   *[pltpu.HBM(a.shape, a.dtype) for a in srcs], *[pltpu.HBM(a.shape, a.dtype) for a in lands],
                   jax.ShapeDtypeStruct((SUB, LANE), F32)),
        in_specs=[_HBM] * (2 * n) + [pl.BlockSpec(memory_space=pl.ANY)],
        out_specs=(_SEM, _SEM, _SEM, *[_HBM] * (2 * n), pl.BlockSpec(memory_space=pltpu.VMEM)),
        input_output_aliases={i: 3 + i for i in range(2 * n)},
        compiler_params=pltpu.CompilerParams(has_side_effects=_DATAFLOW),
    )(*[hbm(a) for a in srcs], *[hbm(a) for a in lands], after)


def _exchange_wait(started, slicers, after, name, plan=_peers):
    n = (len(started) - 4) // 2
    sems, thru = started[0:3], started[3:3 + 2 * n]

    def body(*refs):
        srcs, lands = refs[:n], refs[n:2 * n]
        send_sems, recv_sems, local_sems = refs[2 * n], refs[2 * n + 1], refs[2 * n + 2]
        me, peers = plan()
        for k, (peer, pidx) in enumerate(peers):
            for a in range(n):
                cp = pltpu.make_async_remote_copy(
                    src_ref=slicers[a](srcs[a], pidx), dst_ref=lands[a].at[me], send_sem=send_sems.at[k * n + a],
                    recv_sem=recv_sems.at[k * n + a], device_id=peer, device_id_type=pl.DeviceIdType.MESH)
                cp.wait_send()
                cp.wait_recv()
        for a in range(n):
            pltpu.make_async_copy(slicers[a](srcs[a], me), lands[a].at[me], local_sems.at[a]).wait()

    outs = pl.pallas_call(
        body, name=name, out_shape=[pltpu.HBM(a.shape, a.dtype) for a in thru],
        in_specs=[_HBM] * (2 * n) + [_SEM, _SEM, _SEM, pl.BlockSpec(memory_space=pl.ANY)], out_specs=[_HBM] * (2 * n),
        input_output_aliases={i: i for i in range(2 * n)}, compiler_params=pltpu.CompilerParams(has_side_effects=_DATAFLOW),
    )(*thru, *sems, after)
    return outs[n:]


WIN = 13 * LANE


def _win_base(s):
    n = s * SHARD_IN
    a0 = n + jnp.where(n >= _KR0, KR_LANE, 0) + jnp.where(n >= _KR0 + 32, 32, 0)
    return jnp.minimum(a0 // LANE, (ZW - WIN) // LANE)


def _win_offsets(s):
    n = s * SHARD_IN + jnp.arange(SHARD_IN)
    o = s * SHARD_IN - _win_base(s) * LANE
    return n, (o, o + KR_LANE, o + LANE - 32)


def _to_window(shard, s):
    _, offs = _win_offsets(s)
    padded = jnp.pad(shard, ((0, 0), (0, 0), (WIN, WIN)))
    a, b, c = [lax.dynamic_slice(padded, (0, 0, WIN - o), shard.shape[:2] + (WIN,)) for o in offs]
    col = (_win_base(s) * LANE + jnp.arange(WIN))[None, None, :]
    zero = jnp.zeros_like(a)
    return jnp.where(col < _KR0, a, jnp.where((col >= _KR0 + KR_LANE) & (col < _KR0 + KR_LANE + 32), b, jnp.where(col >= _KR0 + LANE, c, zero)))


def _from_window(win, s):
    n, offs = _win_offsets(s)
    a, b, c = [lax.dynamic_slice(win, (0, 0, o), win.shape[:2] + (SHARD_IN,)) for o in offs]
    return jnp.where((n < _KR0)[None, None, :], a, jnp.where((n < _KR0 + 32)[None, None, :], b, c))


def _win_base_static(s):
    n = s * SHARD_IN
    a0 = n + (KR_LANE if n >= _KR0 else 0) + (32 if n >= _KR0 + 32 else 0)
    return min(a0 // LANE, (ZW - WIN) // LANE)


def _assemble_w_in(gw):
    tr = 128
    bases = [_win_base_static(s) for s in range(N_DEV)]

    def body(g_ref, o_ref):
        for j in range(ZW // LANE):
            acc = None
            for s in range(N_DEV):
                if bases[s] <= j < bases[s] + WIN // LANE:
                    piece = g_ref[s, :, (j - bases[s]) * LANE:(j - bases[s] + 1) * LANE]
                    acc = piece if acc is None else acc + piece
            o_ref[:, j * LANE:(j + 1) * LANE] = acc

    return pl.pallas_call(
        body, name="assemble_w_in", grid=(D // tr,), in_specs=[pl.BlockSpec((N_DEV, tr, WIN), lambda i: (0, i, 0))],
        out_specs=pl.BlockSpec((tr, ZW), lambda i: (i, 0)), out_shape=jax.ShapeDtypeStruct((D, ZW), gw.dtype), compiler_params=_cp(),
    )(gw)


def _cols(width):
    return lambda ref, p: ref.at[:, pl.ds(pl.multiple_of(p * width, width), width)]


def _rows(height):
    return lambda ref, p: ref.at[pl.ds(pl.multiple_of(p * height, height), height), :]


SCATTER = {
    'w_in': (lambda ref, p: ref.at[:, pl.ds(pl.multiple_of(_win_base(p) * LANE, LANE), WIN)], (D, WIN), BF16),
    'conv_w': (_cols(LANE), (4, LANE), F32),
    'w_lru_o': (_rows(LANE), (LANE, D), BF16),
    'w_uq': (_cols(LANE), (256, LANE), F32),
    'w_ukv': (_cols(LANE), (128, LANE), F32),
    'w_mla_o': (_cols(LANE), (512, LANE), BF16),
    'w_dil_o': (_cols(LANE), (512, LANE), BF16),
    'w_out': (_rows(LANE), (LANE, D), BF16),
}
SLICED_AXIS = {'w_in': 1, 'conv_w': 1, 'w_lru_o': 0, 'w_uq': 1, 'w_ukv': 1, 'w_mla_o': 1, 'w_dil_o': 1, 'w_out': 0}


PACK_ROWS = 64


def _packed_rows(shapes):
    n = sum(int(np.prod(s)) for s in shapes)
    return -(-n // (PACK_ROWS * LANE)) * PACK_ROWS


def _sum8(buf, name):
    ns, R, C = buf.shape
    tr = R
    while tr * C * 4 * ns > (1 << 22) and tr % 32 == 0:
        tr //= 2

    def body(b_ref, o_ref):
        acc = b_ref[0].astype(F32)
        for s in range(1, ns):
            acc = acc + b_ref[s].astype(F32)
        o_ref[...] = acc

    return pl.pallas_call(
        body, name=name, grid=(R // tr,), in_specs=[pl.BlockSpec((ns, tr, C), lambda i: (0, i, 0))],
        out_specs=pl.BlockSpec((tr, C), lambda i: (i, 0)), out_shape=jax.ShapeDtypeStruct((R, C), F32), compiler_params=_cp(),
    )(buf)


def _pack(arrs, dtype, lead):
    flat = [a.astype(dtype).reshape(a.shape[:lead] + (-1,)) for a in arrs]
    cat = jnp.concatenate(flat, axis=-1)
    n = cat.shape[-1]
    unit = PACK_ROWS * LANE
    pad = (-n) % unit
    if pad:
        cat = jnp.pad(cat, [(0, 0)] * lead + [(0, pad)])
    return cat.reshape(cat.shape[:lead] + ((n + pad) // LANE, LANE))


def _unpack(buf, shapes, lead):
    flat = buf.reshape(buf.shape[:lead] + (-1,))
    out, off = [], 0
    for shp in shapes:
        n = int(np.prod(shp))
        out.append(flat[..., off:off + n].reshape(buf.shape[:lead] + tuple(shp)))
        off += n
    return out


def _adamw(w, g, m, v, name):
    layers, rows, cols = w.shape
    tr = rows
    while tr * cols * 4 > (3 << 19) and tr % 16 == 0:
        tr //= 2
    c1 = 1.0 - ADAM_B1 ** ADAM_STEP
    c2 = 1.0 - ADAM_B2 ** ADAM_STEP

    def body(w_ref, g_ref, m_ref, v_ref, d_ref, mo_ref, vo_ref):
        gv = g_ref[...]
        mn = ADAM_B1 * m_ref[...] + (1.0 - ADAM_B1) * gv
        vn = ADAM_B2 * v_ref[...] + (1.0 - ADAM_B2) * (gv * gv)
        mo_ref[...] = mn
        vo_ref[...] = vn
        d_ref[...] = -ADAM_LR * ((mn / c1) / (jnp.sqrt(vn / c2) + ADAM_EPS) + ADAM_WD * w_ref[...])

    spec = pl.BlockSpec((1, tr, cols), lambda l, i: (l, i, 0))
    return pl.pallas_call(
        body, name=name, grid=(layers, rows // tr), in_specs=[spec] * 4, out_specs=[spec] * 3,
        out_shape=[jax.ShapeDtypeStruct((layers, rows, cols), F32)] * 3, compiler_params=_cp(),
    )(w, g, m, v)


IN_NAMES = ['x', 'positions', 'norm_g', 'w_in', 'conv_w', 'conv_b', 'w_gate_x', 'b_gate_x', 'w_gate_a', 'b_gate_a', 'lru_lambda', 'w_lru_o',
            'cq_norm_g', 'ckv_norm_g', 'w_uq', 'w_ukv', 'mla_q_norm_g', 'mla_k_norm_g', 'w_mla_o', 'dil_q_norm_g', 'dil_k_norm_g', 'w_dil_o',
            'b_merge', 'w_out']
WEIGHTS = IN_NAMES[2:]
REPLICATED = [n for n in WEIGHTS if n not in SCATTER]
GATE_WEIGHTS = ('w_gate_x', 'w_gate_a')

_KR0 = C_KR * LANE


GATHERED = ['w_in', 'w_lru_o', 'w_uq', 'w_ukv', 'w_mla_o', 'w_dil_o', 'w_out', 'conv_w']


def _local_weights(wd, me):
    loc = {n: wd[n].astype(BF16) for n in GATHERED[:-1]}
    loc['w_in'] = _to_window(loc['w_in'], me)
    loc['w_uq'] = jnp.pad(loc['w_uq'], ((0, 0), (0, 0), (0, LANE - MLA_QK)))
    loc['conv_w'] = wd['conv_w']
    return [[loc[n][l] for n in GATHERED] for l in range(DEPTH)]


def _layer_weights(gathered, rep, l):
    gw = dict(zip(GATHERED, gathered))
    by_rows = lambda a: a.reshape(-1, a.shape[-1])
    by_cols = lambda a: jnp.swapaxes(a, 0, 1).reshape(a.shape[1], -1)
    ukv = jnp.swapaxes(gw['w_ukv'], 0, 1)
    g96 = lambda a: jnp.pad(a[l].reshape(1, MLA_QK), ((0, 0), (0, LANE - MLA_QK)))
    g64 = lambda a: jnp.tile(a[l].reshape(1, DIL_HD), (1, 2))
    return dict(
        norm_g=rep['norm_g'][l].reshape(1, D), w_in=_assemble_w_in(gw['w_in']),
        conv_w=by_cols(gw['conv_w']), conv_b=rep['conv_b'][l].reshape(1, D),
        w_gx=rep['w_gate_x'][l].astype(BF16), b_gx=rep['b_gate_x'][l].reshape(8, 1, LANE),
        w_ga=rep['w_gate_a'][l].astype(BF16), b_ga=rep['b_gate_a'][l].reshape(8, 1, LANE),
        lam=rep['lru_lambda'][l].reshape(1, D),
        w_lru_o=by_rows(gw['w_lru_o']), w_mla_o=by_cols(gw['w_mla_o']), w_dil_o=by_cols(gw['w_dil_o']), w_out=by_rows(gw['w_out']),
        g_cq=rep['cq_norm_g'][l].reshape(1, 256), g_ckv=rep['ckv_norm_g'][l].reshape(1, 128),
        w_uq=by_cols(gw['w_uq']), w_uk=jnp.pad(ukv[:, :, :64], ((0, 0), (0, 0), (0, 64))).reshape(128, 1024),
        w_uv=ukv[:, :, 64:].reshape(128, 512),
        g_mq=g96(rep['mla_q_norm_g']), g_mk=g96(rep['mla_k_norm_g']), g_dq=g64(rep['dil_q_norm_g']), g_dk=g64(rep['dil_k_norm_g']),
        b_merge=rep['b_merge'][l].reshape(1, 3 * D),
    )


def _sharded_grads(g):
    uk = g['w_uk'].reshape(128, 8, 128)[:, :, :64]
    uv = g['w_uv'].reshape(128, 8, 64)
    d = {'w_in': g['w_in'], 'conv_w': g['conv_w'], 'w_lru_o': g['w_lru_o'], 'w_uq': g['w_uq'],
         'w_ukv': jnp.concatenate([uk, uv], axis=-1).reshape(128, 1024), 'w_mla_o': g['w_mla_o'], 'w_dil_o': g['w_dil_o'],
         'w_out': g['w_out']}
    return [d[n] for n in SCATTER]


def _replicated_grads(g):
    return {
        'conv_b': g['conv_b'].reshape(D),
        'w_gate_x': g['w_gx'], 'b_gate_x': g['b_gx'].reshape(8, LANE), 'w_gate_a': g['w_ga'], 'b_gate_a': g['b_ga'].reshape(8, LANE),
        'lru_lambda': g['lam'].reshape(D), 'cq_norm_g': g['g_cq'].reshape(256), 'ckv_norm_g': g['g_ckv'].reshape(128),
        'mla_q_norm_g': g['g_mq'][0, :MLA_QK], 'mla_k_norm_g': g['g_mk'][0, :MLA_QK],
        'dil_q_norm_g': g['g_dq'][0, :DIL_HD] + g['g_dq'][0, DIL_HD:], 'dil_k_norm_g': g['g_dk'][0, :DIL_HD] + g['g_dk'][0, DIL_HD:],
        'b_merge': g['b_merge'].reshape(3 * D),
    }


def kernel(x, positions, norm_g, w_in, conv_w, conv_b, w_gate_x, b_gate_x, w_gate_a, b_gate_a, lru_lambda, w_lru_o, cq_norm_g, ckv_norm_g, w_uq, w_ukv, mla_q_norm_g, mla_k_norm_g, w_mla_o, dil_q_norm_g, dil_k_norm_g, w_dil_o, b_merge, w_out, loss_target, m_norm_g, m_w_in, m_conv_w, m_conv_b, m_w_gate_x, m_b_gate_x, m_w_gate_a, m_b_gate_a, m_lru_lambda, m_w_lru_o, m_cq_norm_g, m_ckv_norm_g, m_w_uq, m_w_ukv, m_mla_q_norm_g, m_mla_k_norm_g, m_w_mla_o, m_dil_q_norm_g, m_dil_k_norm_g, m_w_dil_o, m_b_merge, m_w_out, v_norm_g, v_w_in, v_conv_w, v_conv_b, v_w_gate_x, v_b_gate_x, v_w_gate_a, v_b_gate_a, v_lru_lambda, v_w_lru_o, v_cq_norm_g, v_ckv_norm_g, v_w_uq, v_w_ukv, v_mla_q_norm_g, v_mla_k_norm_g, v_w_mla_o, v_dil_q_norm_g, v_dil_k_norm_g, v_w_dil_o, v_b_merge, v_w_out):
    args = (x, positions, norm_g, w_in, conv_w, conv_b, w_gate_x, b_gate_x, w_gate_a, b_gate_a, lru_lambda, w_lru_o, cq_norm_g, ckv_norm_g, w_uq, w_ukv, mla_q_norm_g, mla_k_norm_g, w_mla_o, dil_q_norm_g, dil_k_norm_g, w_dil_o, b_merge, w_out)
    moments_m = (m_norm_g, m_w_in, m_conv_w, m_conv_b, m_w_gate_x, m_b_gate_x, m_w_gate_a, m_b_gate_a, m_lru_lambda, m_w_lru_o, m_cq_norm_g, m_ckv_norm_g, m_w_uq, m_w_ukv, m_mla_q_norm_g, m_mla_k_norm_g, m_w_mla_o, m_dil_q_norm_g, m_dil_k_norm_g, m_w_dil_o, m_b_merge, m_w_out)
    moments_v = (v_norm_g, v_w_in, v_conv_w, v_conv_b, v_w_gate_x, v_b_gate_x, v_w_gate_a, v_b_gate_a, v_lru_lambda, v_w_lru_o, v_cq_norm_g, v_ckv_norm_g, v_w_uq, v_w_ukv, v_mla_q_norm_g, v_mla_k_norm_g, v_w_mla_o, v_dil_q_norm_g, v_dil_k_norm_g, v_w_dil_o, v_b_merge, v_w_out)
    a = dict(zip(IN_NAMES, args))
    wd = {n: a[n] for n in WEIGHTS}
    md = dict(zip(WEIGHTS, moments_m))
    vd = dict(zip(WEIGHTS, moments_v))

    me = 4 * lax.axis_index("x") + 2 * lax.axis_index("y") + lax.axis_index("c")

    assert DEPTH == 2
    xs, tabs = x[0], _rope_tables(positions[0])
    whole = [_whole] * len(GATHERED)
    slicers = [SCATTER[n][0] for n in SCATTER]
    grad_slices = [SCATTER[n][1:3] for n in SCATTER]

    local = _local_weights(wd, me)
    w_slices = [(a.shape, a.dtype) for a in local[0]]
    landed0 = _gather_two_level(local[0], "gather_w0")
    flying = _exchange_start(local[1], whole, w_slices, landed0[0], "gather_w1_start")
    rep0 = dict(wd, norm_g=wd['norm_g'] + flying[-1][0, 0])
    w0 = _layer_weights(landed0, rep0, 0)
    x1, saved0 = _layer_fwd(xs, w0, tabs)
    w1 = _layer_weights(_exchange_wait(flying, whole, x1, "gather_w1_wait"), wd, 1)
    x2, saved1 = _layer_fwd(x1, w1, tabs)
    loss, dx2 = _loss_fwd_bwd(x2, loss_target[0])
    loss = loss[0, 0]

    sharded = list(SCATTER)
    nsh = len(sharded)
    small = [n for n in REPLICATED if n not in GATE_WEIGHTS and n != 'norm_g']

    def outgoing(g):
        r = _replicated_grads(g)
        return (_sharded_grads(g) + [_pack([r[n] for n in small], F32, 0)]
                + [r[n].astype(BF16).reshape(8 * LANE, LANE) for n in GATE_WEIGHTS])

    out_slicers = slicers + [_whole] * 3
    out_slices = grad_slices + [((_packed_rows([wd[n].shape[1:] for n in small]), LANE), F32)] + [((8 * LANE, LANE), BF16)] * 2
    dx1, g1 = _layer_bwd(dx2, w1, tabs, saved1)
    flying1 = _exchange_start(outgoing(g1), out_slicers, out_slices, dx1, "scatter_g1_start")
    later = {}

    names = sharded + ['small'] + list(GATE_WEIGHTS)
    sliced = [True] * nsh + [False] * 3
    by_chip = [(lambda ref, q: ref.at[q])] * nsh + [_whole] * 3

    def send_layer0(g):
        later['got1'] = _exchange_wait(flying1, out_slicers, g['w_in'], "scatter_g1_wait")
        mine = outgoing(g)
        came = _pair_exchange(mine, out_slicers, out_slices, sliced, "pair_g0")
        my_side = 2 * jnp.arange(4, dtype=jnp.int32) + lax.axis_index("c")
        halves = []
        for n, a, c in zip(names, mine, came):
            if n in SCATTER:
                first = _win_base(my_side) if n == 'w_in' else my_side
                halves.append(_pair_add(a, c, first.astype(jnp.int32), SLICED_AXIS[n], f"pair_sum_{n}"))
            else:
                halves.append(_add2(a, c[0], f"pair_sum_{n}"))
        later['flying0'] = _exchange_start(halves, by_chip, out_slices, later['got1'][0], "scatter_g0_start", plan=_plan_chips, nslots=4)
        return later['flying0'][-1]

    grad_x, g0 = _layer_bwd(dx1, w0, tabs, saved0, hook=send_layer0, after=flying1[-1])
    sum1 = [_sum8(b, f"sum_{n}_1") for n, b in zip(names, later['got1'])]
    behind = grad_x[:1, :1] + sum(s_[:1, :1] for s_ in sum1)
    got0 = _exchange_wait(later['flying0'], by_chip, behind, "scatter_g0_wait", plan=_plan_chips)
    sum0 = [_sum8(b, f"sum_{n}_0") for n, b in zip(names, got0)]
    norm_part = _pack([jnp.stack([g['norm_g'].reshape(D) for g in (g0, g1)])], F32, 0)
    norm_sum = _sum8(_exchange([norm_part], [_whole], [(norm_part.shape, F32)], "gather_norm_g")[0], "sum_norm_g")

    gsh = {n: jnp.stack([sum0[i], sum1[i]]) for i, n in enumerate(sharded)}
    gsh['w_in'] = _from_window(gsh['w_in'], me)
    gsh['w_uq'] = gsh['w_uq'][:, :, :MLA_QK]
    grep = {'norm_g': _unpack(norm_sum, [wd['norm_g'].shape], 0)[0]}
    per_layer = [_unpack(s[nsh], [wd[n].shape[1:] for n in small], 0) for s in (sum0, sum1)]
    grep.update({n: jnp.stack([per_layer[l][i] for l in range(DEPTH)]) for i, n in enumerate(small)})
    for i, n in enumerate(GATE_WEIGHTS):
        grep[n] = jnp.stack([sum0[nsh + 1 + i], sum1[nsh + 1 + i]]).reshape(wd[n].shape)

    out_g, out_d, out_m, out_v = {}, {}, {}, {}
    vecs = ['norm_g'] + small
    vshapes = [wd[n].shape for n in vecs]
    packed_g = _pack([grep[n] for n in vecs], F32, 0)
    d_, m_, v_ = _adamw(_pack([wd[n] for n in vecs], F32, 0)[None], packed_g[None], _pack([md[n] for n in vecs], F32, 0)[None],
                        _pack([vd[n] for n in vecs], F32, 0)[None], "adamw_vectors")
    for dst, buf in ((out_d, d_), (out_m, m_), (out_v, v_)):
        dst.update(zip(vecs, _unpack(buf[0], vshapes, 0)))
    out_g.update({n: grep[n] for n in vecs})
    gsh.update({n: grep[n] for n in GATE_WEIGHTS})
    for n in sharded + list(GATE_WEIGHTS):
        shp = wd[n].shape
        three = (1, -1, shp[-1])
        d_, m_, v_ = _adamw(wd[n].reshape(three), gsh[n].reshape(three), md[n].reshape(three), vd[n].reshape(three), "adamw_" + n)
        out_g[n], out_d[n], out_m[n], out_v[n] = gsh[n], d_.reshape(shp), m_.reshape(shp), v_.reshape(shp)

    loss = lax.psum(loss, ("x", "y", "c"))
    return (loss, grad_x[None], *[out_g[n] for n in WEIGHTS], *[out_d[n] for n in WEIGHTS], *[out_m[n] for n in WEIGHTS],
            *[out_v[n] for n in WEIGHTS])
```

```python
import numpy as np
import jax
import jax.numpy as jnp
from jax import lax
from jax.experimental import pallas as pl
from jax.experimental.pallas import tpu as pltpu

F32 = jnp.float32
BF16 = jnp.bfloat16

N_DEV = 8
D = 1024
DEPTH = 2
EPS = 1e-6
ROPE_THETA = 10000.0
LRU_C = 8.0
LANE = 128
SUB = 8
IN_WIDTH = 11168
SHARD_IN = IN_WIDTH // N_DEV

C_LRUX, C_LRUG, C_CQ, C_CKV, C_KR, C_MLAG, C_DQ, C_DK, C_DV, C_DILG, C_MERGE = 0, 8, 16, 18, 19, 20, 24, 36, 48, 60, 64
ZW = 88 * LANE
KR_LANE = 64

MLA_QK = 96
MLA_SCALE = MLA_QK ** -0.5
DIL_HD = 64
DIL_SCALE = DIL_HD ** -0.5
DIL_DILATIONS = (1, 4, 16)
NK = 128

ADAM_LR, ADAM_B1, ADAM_B2, ADAM_EPS, ADAM_WD, ADAM_STEP = 0.001, 0.9, 0.999, 1e-08, 0.01, 10

NEG = -1e30
LOG2E = 1.4426950408889634
VMEM_LIMIT = 48 * 1024 * 1024


def _cp(**kw):
    return pltpu.CompilerParams(vmem_limit_bytes=VMEM_LIMIT, **kw)


def _sig(x):
    return 1.0 / (1.0 + jnp.exp(-x))


def _silu(x):
    return x * _sig(x)


def _dsilu(x):
    s = _sig(x)
    return s * (1.0 + x * (1.0 - s))


def _dot(a, b, dims):
    return lax.dot_general(a, b, (dims, ((), ())), preferred_element_type=F32)


def _nn(a, b):
    return _dot(a, b, ((1,), (0,)))


def _nt(a, b):
    return _dot(a, b, ((1,), (1,)))


def _tn(a, b):
    return _dot(a, b, ((0,), (0,)))


def _rsum(x):
    return jnp.sum(x, axis=-1, keepdims=True)


def _rsum_mxu(x):
    ones = jnp.ones((x.shape[-1], LANE), F32)
    return lax.dot_general(x, ones, (((1,), (0,)), ((), ())), precision=lax.Precision.HIGHEST, preferred_element_type=F32)


def _csum(x):
    return jnp.sum(x, axis=0, keepdims=True)


def _mm(a, b, *, mode, name, out_dtype=F32, add=None, after=None, tm=1024, tn=1024, tk=1024):
    if mode == "nn":
        (M, K), (K2, N) = a.shape, b.shape
    elif mode == "nt":
        (M, K), (N, K2) = a.shape, b.shape
    else:
        (K, M), (K2, N) = a.shape, b.shape
    assert K == K2
    tm, tn, tk = min(tm, M), min(tn, N), min(tk, K)
    assert M % tm == 0 and N % tn == 0 and K % tk == 0
    nk = K // tk
    fn = {"nn": _nn, "nt": _nt, "tn": _tn}[mode]
    has_add = add is not None

    def body(*refs):
        a_ref, b_ref = refs[0], refs[1]
        add_ref = refs[2] if has_add else None
        o_ref = refs[2 + has_add + (after is not None)]
        part = fn(a_ref[...].astype(BF16), b_ref[...].astype(BF16))

        def fin(acc):
            if has_add:
                acc = acc + add_ref[...]
            o_ref[...] = acc.astype(out_dtype)

        if nk == 1:
            fin(part)
        else:
            acc_ref = refs[-1]
            k = pl.program_id(2)

            @pl.when(k == 0)
            def _():
                acc_ref[...] = part

            @pl.when(k > 0)
            def _():
                acc_ref[...] += part

            @pl.when(k == nk - 1)
            def _():
                fin(acc_ref[...])

    a_spec = pl.BlockSpec((tk, tm), lambda i, j, k: (k, i)) if mode == "tn" else pl.BlockSpec((tm, tk), lambda i, j, k: (i, k))
    b_spec = pl.BlockSpec((tn, tk), lambda i, j, k: (j, k)) if mode == "nt" else pl.BlockSpec((tk, tn), lambda i, j, k: (k, j))
    o_spec = pl.BlockSpec((tm, tn), lambda i, j, k: (i, j))
    in_specs, args = [a_spec, b_spec], [a, b]
    if has_add:
        in_specs.append(o_spec)
        args.append(add)
    if after is not None:
        in_specs.append(pl.BlockSpec(memory_space=pl.ANY))
        args.append(after)
    return pl.pallas_call(
        body, name=name, grid=(M // tm, N // tn, nk), in_specs=in_specs, out_specs=o_spec,
        out_shape=jax.ShapeDtypeStruct((M, N), out_dtype),
        scratch_shapes=[pltpu.VMEM((tm, tn), F32)] if nk > 1 else [],
        compiler_params=_cp(dimension_semantics=("parallel", "parallel", "arbitrary")),
    )(*args)


T_ROW = 512


def _rms_in_fwd(x, g):
    S = x.shape[0]
    T = T_ROW

    def body(x_ref, g_ref, h_ref):
        xv = x_ref[...]
        r = lax.rsqrt(jnp.mean(xv * xv, axis=-1, keepdims=True) + EPS)
        h_ref[...] = (xv * r * g_ref[...]).astype(BF16)

    return pl.pallas_call(
        body, name="rms_in_fwd", grid=(S // T,),
        in_specs=[pl.BlockSpec((T, D), lambda i: (i, 0)), pl.BlockSpec((1, D), lambda i: (0, 0))],
        out_specs=pl.BlockSpec((T, D), lambda i: (i, 0)),
        out_shape=jax.ShapeDtypeStruct((S, D), BF16), compiler_params=_cp(),
    )(x, g)


def _rms_in_bwd(x, g, dh, dres):
    S = x.shape[0]
    T = T_ROW

    def body(x_ref, g_ref, dh_ref, dr_ref, dx_ref, dg_ref):
        i = pl.program_id(0)
        xv = x_ref[...]
        r = lax.rsqrt(jnp.mean(xv * xv, axis=-1, keepdims=True) + EPS)
        xn = xv * r
        dy = dh_ref[...]
        part = _csum(dy * xn)

        @pl.when(i == 0)
        def _():
            dg_ref[...] = part

        @pl.when(i > 0)
        def _():
            dg_ref[...] += part

        dxh = dy * g_ref[...]
        dx_ref[...] = dr_ref[...] + r * (dxh - xn * jnp.mean(dxh * xn, axis=-1, keepdims=True))

    row = pl.BlockSpec((T, D), lambda i: (i, 0))
    vec = pl.BlockSpec((1, D), lambda i: (0, 0))
    return pl.pallas_call(
        body, name="rms_in_bwd", grid=(S // T,), in_specs=[row, vec, row, row], out_specs=[row, vec],
        out_shape=[jax.ShapeDtypeStruct((S, D), F32), jax.ShapeDtypeStruct((1, D), F32)], compiler_params=_cp(),
    )(x, g, dh, dres)


T_LRU = 1024
T_LRU_BWD = 512


def _neg_expm1(y):
    ser = -y * (1.0 + y * 0.5 * (1.0 + y * (1.0 / 3.0) * (1.0 + y * 0.25 * (1.0 + y * 0.2))))
    return jnp.where(y > -0.03, ser, 1.0 - jnp.exp(y))


def _softplus_neg(lam):
    e = jnp.exp(-jnp.abs(lam))
    l1p = jnp.where(e < 0.01, e * (1.0 - e * (0.5 - e * (1.0 / 3.0 - e * 0.25))), jnp.log(1.0 + e))
    return jnp.maximum(-lam, 0.0) + l1p


def _scan_fwd(a, b, T):
    row = lax.broadcasted_iota(jnp.int32, a.shape, 0)
    d = 1
    while d < T:
        m = row >= d
        b = jnp.where(m, a * pltpu.roll(b, d, 0) + b, b)
        a = jnp.where(m, a * pltpu.roll(a, d, 0), a)
        d *= 2
    return a, b


def _scan_bwd(a, b, T):
    row = lax.broadcasted_iota(jnp.int32, a.shape, 0)
    d = 1
    while d < T:
        m = row < T - d
        b = jnp.where(m, a * pltpu.roll(b, T - d, 0) + b, b)
        a = jnp.where(m, a * pltpu.roll(a, T - d, 0), a)
        d *= 2
    return b


def _lru_common(x, prev, first, cw_ref, cb_ref, wgx_ref, bgx_ref, wga_ref, bga_ref, lam_ref, T):
    row = lax.broadcasted_iota(jnp.int32, x.shape, 0)
    prev = jnp.where(first, 0.0, prev)
    xs = []
    for j in (3, 2, 1):
        pv = jnp.tile(pltpu.roll(prev, j, 0), (T // SUB, 1))
        xs.append(jnp.where(row < j, pv, pltpu.roll(x, j, 0)))
    xs.append(x)
    xc = cb_ref[...] + cw_ref[0:1, :] * xs[0] + cw_ref[1:2, :] * xs[1] + cw_ref[2:3, :] * xs[2] + cw_ref[3:4, :] * xs[3]
    xcb = xc.astype(BF16)
    gx = _sig(_nn(xcb, wgx_ref[0]) + bgx_ref[0])
    ga = _sig(_nn(xcb, wga_ref[0]) + bga_ref[0])
    sp = _softplus_neg(lam_ref[...])
    log_a = -LRU_C * ga * sp
    a = jnp.exp(log_a)
    mult = jnp.sqrt(_neg_expm1(2.0 * log_a))
    return xs, xc, xcb, gx, ga, sp, a, mult


def _lru_specs(T, tmap):
    def at(col0):
        return pl.BlockSpec((T, LANE), lambda n, i: (tmap(i), col0 + n))

    def prev(col0):
        return pl.BlockSpec((SUB, LANE), lambda n, i: (jnp.maximum(tmap(i) * (T // SUB) - 1, 0), col0 + n))

    small = [
        pl.BlockSpec((4, LANE), lambda n, i: (0, n)),
        pl.BlockSpec((1, LANE), lambda n, i: (0, n)),
        pl.BlockSpec((1, LANE, LANE), lambda n, i: (n, 0, 0)),
        pl.BlockSpec((1, 1, LANE), lambda n, i: (n, 0, 0)),
        pl.BlockSpec((1, LANE, LANE), lambda n, i: (n, 0, 0)),
        pl.BlockSpec((1, 1, LANE), lambda n, i: (n, 0, 0)),
        pl.BlockSpec((1, LANE), lambda n, i: (0, n)),
    ]
    return at, prev, small


def _lru_fwd(zp, w):
    S = zp.shape[0]
    T = T_LRU
    at, prev, small = _lru_specs(T, lambda i: i)

    def body(x_ref, xp_ref, g_ref, cw_ref, cb_ref, wgx_ref, bgx_ref, wga_ref, bga_ref, lam_ref, hs_ref, y_ref, carry_ref):
        i = pl.program_id(1)

        @pl.when(i == 0)
        def _():
            carry_ref[...] = jnp.zeros_like(carry_ref)

        x = x_ref[...]
        _, xc, _, gx, _, _, a, mult = _lru_common(x, xp_ref[...], i == 0, cw_ref, cb_ref, wgx_ref, bgx_ref, wga_ref, bga_ref, lam_ref, T)
        A, B = _scan_fwd(a, mult * gx * xc, T)
        h = B + A * carry_ref[SUB - 1:SUB, :]
        hs_ref[...] = h
        carry_ref[...] = hs_ref[T - SUB:T, :]
        y_ref[...] = (h * _silu(g_ref[...])).astype(BF16)

    out = pl.BlockSpec((T, LANE), lambda n, i: (i, n))
    return pl.pallas_call(
        body, name="lru_fwd", grid=(8, S // T),
        in_specs=[at(C_LRUX), prev(C_LRUX), at(C_LRUG)] + small, out_specs=[out, out],
        out_shape=[jax.ShapeDtypeStruct((S, D), F32), jax.ShapeDtypeStruct((S, D), BF16)],
        scratch_shapes=[pltpu.VMEM((SUB, LANE), F32)],
        compiler_params=_cp(dimension_semantics=("parallel", "arbitrary")),
    )(zp, zp, zp, w["conv_w"], w["conv_b"], w["w_gx"], w["b_gx"], w["w_ga"], w["b_ga"], w["lam"])


def _lru_bwd(zp, hs, dy, w, dz):
    S = zp.shape[0]
    T = T_LRU_BWD
    nT = S // T
    at, prev, small = _lru_specs(T, lambda i: nT - 1 - i)

    def body(x_ref, xp_ref, g_ref, h_ref, hp_ref, dy_ref, cw_ref, cb_ref, wgx_ref, bgx_ref, wga_ref, bga_ref, lam_ref, dz_in,
             dzx_ref, dcw_ref, dcb_ref, dwgx_ref, dbgx_ref, dwga_ref, dbga_ref, dlam_ref, carry_ref, head_ref):
        del dz_in
        j = pl.program_id(1)
        it = nT - 1 - j

        @pl.when(j == 0)
        def _():
            for r in (carry_ref, head_ref, dcw_ref, dcb_ref, dwgx_ref, dbgx_ref, dwga_ref, dbga_ref, dlam_ref):
                r[...] = jnp.zeros_like(r)

        first = it == 0
        x = x_ref[...]
        xs, xc, xcb, gx, ga, sp, a, mult = _lru_common(x, xp_ref[...], first, cw_ref, cb_ref, wgx_ref, bgx_ref, wga_ref, bga_ref, lam_ref, T)
        row = lax.broadcasted_iota(jnp.int32, x.shape, 0)
        u = gx * xc
        h = h_ref[...]
        hp = jnp.where(first, 0.0, hp_ref[...])
        hm1 = jnp.where(row < 1, jnp.tile(pltpu.roll(hp, 1, 0), (T // SUB, 1)), pltpu.roll(h, 1, 0))
        dho = dy_ref[...] * _silu(g_ref[...])
        gin = jnp.where(row == T - 1, dho + carry_ref[0:1, :], dho)
        abar = jnp.where(row == T - 1, 0.0, pltpu.roll(a, T - 1, 0))
        dh = _scan_bwd(abar, gin, T)
        carry_ref[...] = (a * dh)[0:SUB, :]
        da = dh * hm1
        dmult = dh * u
        du = dh * mult
        dgx = du * xc
        dxc = du * gx
        dlog_a = da * a - dmult * a * a / mult
        dga = dlog_a * (-LRU_C * sp)
        lam = lam_ref[...]
        dlam_ref[...] += _csum(dlog_a * (-LRU_C * ga)) * (-1.0 / (1.0 + jnp.exp(lam)))
        dpa = dga * ga * (1.0 - ga)
        dpx = dgx * gx * (1.0 - gx)
        dpab, dpxb = dpa.astype(BF16), dpx.astype(BF16)
        dxc = dxc + _nt(dpxb, wgx_ref[0]) + _nt(dpab, wga_ref[0])
        dwgx_ref[0] += _tn(xcb, dpxb)
        dwga_ref[0] += _tn(xcb, dpab)
        dbgx_ref[0] += _csum(dpx)
        dbga_ref[0] += _csum(dpa)
        dcb_ref[...] += _csum(dxc)
        for k in range(4):
            dcw_ref[k:k + 1, :] += _csum(dxc * xs[k])
        head = head_ref[...]
        dx = cw_ref[3:4, :] * dxc
        for jj in (1, 2, 3):
            hv = jnp.tile(pltpu.roll(head, SUB - jj, 0), (T // SUB, 1))
            dx = dx + cw_ref[3 - jj:4 - jj, :] * jnp.where(row >= T - jj, hv, pltpu.roll(dxc, T - jj, 0))
        head_ref[...] = dxc[0:SUB, :]
        dzx_ref[...] = dx.astype(BF16)

    def acc(shape, imap):
        return pl.BlockSpec(shape, imap)

    out_specs = [
        pl.BlockSpec((T, LANE), lambda n, i: (nT - 1 - i, C_LRUX + n)),
        acc((4, LANE), lambda n, i: (0, n)), acc((1, LANE), lambda n, i: (0, n)),
        acc((1, LANE, LANE), lambda n, i: (n, 0, 0)), acc((1, 1, LANE), lambda n, i: (n, 0, 0)),
        acc((1, LANE, LANE), lambda n, i: (n, 0, 0)), acc((1, 1, LANE), lambda n, i: (n, 0, 0)),
        acc((1, LANE), lambda n, i: (0, n)),
    ]
    out_shape = [
        jax.ShapeDtypeStruct(dz.shape, BF16),
        jax.ShapeDtypeStruct((4, D), F32), jax.ShapeDtypeStruct((1, D), F32),
        jax.ShapeDtypeStruct((8, LANE, LANE), F32), jax.ShapeDtypeStruct((8, 1, LANE), F32),
        jax.ShapeDtypeStruct((8, LANE, LANE), F32), jax.ShapeDtypeStruct((8, 1, LANE), F32),
        jax.ShapeDtypeStruct((1, D), F32),
    ]
    dyspec = pl.BlockSpec((T, LANE), lambda n, i: (nT - 1 - i, n))
    hprev = pl.BlockSpec((SUB, LANE), lambda n, i: (jnp.maximum((nT - 1 - i) * (T // SUB) - 1, 0), n))
    return pl.pallas_call(
        body, name="lru_bwd", grid=(8, nT),
        in_specs=[at(C_LRUX), prev(C_LRUX), at(C_LRUG), dyspec, hprev, dyspec] + small + [pl.BlockSpec(memory_space=pl.ANY)],
        out_specs=out_specs, out_shape=out_shape,
        scratch_shapes=[pltpu.VMEM((SUB, LANE), F32), pltpu.VMEM((SUB, LANE), F32)],
        input_output_aliases={13: 0},
        compiler_params=_cp(dimension_semantics=("parallel", "arbitrary")),
    )(zp, zp, zp, hs, hs, dy, w["conv_w"], w["conv_b"], w["w_gx"], w["b_gx"], w["w_ga"], w["b_ga"], w["lam"], dz)


def _lru_gate_bwd(zp, hs, dy, dz):
    S = zp.shape[0]
    T = T_ROW

    def body(g_ref, h_ref, dy_ref, dz_in, o_ref):
        del dz_in
        o_ref[...] = (dy_ref[...] * h_ref[...] * _dsilu(g_ref[...])).astype(BF16)

    row = pl.BlockSpec((T, D), lambda i: (i, 0))
    zc = pl.BlockSpec((T, D), lambda i: (i, C_LRUG // 8))
    return pl.pallas_call(
        body, name="lru_gate_bwd", grid=(S // T,), in_specs=[zc, row, row, pl.BlockSpec(memory_space=pl.ANY)], out_specs=zc,
        out_shape=jax.ShapeDtypeStruct(dz.shape, BF16), input_output_aliases={3: 0}, compiler_params=_cp(),
    )(zp, hs, dy, dz)


def _rope_tables(pos):
    pf = pos.astype(F32)[:, None]

    def cs(d):
        inv = ROPE_THETA ** (-jnp.arange(0, d, 2, dtype=F32) / d)
        ang = pf * inv
        return jnp.cos(ang), jnp.sin(ang)

    S = pos.shape[0]
    c, s = cs(32)
    one, zero = jnp.ones((S, 64), F32), jnp.zeros((S, 16), F32)
    z32, z64 = jnp.zeros((S, 32), F32), jnp.zeros((S, 64), F32)
    mla = (jnp.concatenate([one, c, c, jnp.ones((S, 32), F32)], 1),
           jnp.concatenate([z64, zero, s, z32], 1),
           jnp.concatenate([z64, -s, zero, z32], 1))
    c, s = cs(64)
    dil = (jnp.concatenate([c, c, c, c], 1),
           jnp.concatenate([z32, s, z32, s], 1),
           jnp.concatenate([-s, z32, -s, z32], 1))
    return mla, dil


def _rope(x, C, S1, S2, sh):
    return x * C + pltpu.roll(x, sh, 1) * S1 + pltpu.roll(x, LANE - sh, 1) * S2


def _rope_t(dy, C, S1, S2, sh):
    return dy * C + pltpu.roll(dy * S1, LANE - sh, 1) + pltpu.roll(dy * S2, sh, 1)


def _lane(shape):
    return lax.broadcasted_iota(jnp.int32, shape, 1)


T_MLA = 512
TA = 512


def _zcol(T, width, col_lanes):
    assert (col_lanes * LANE) % width == 0
    return pl.BlockSpec((T, width), lambda i: (i, col_lanes * LANE // width))


def _full(shape):
    return pl.BlockSpec(shape, lambda *_: (0,) * len(shape))


def _mla_pre_fwd(zp, w, tab):
    S = zp.shape[0]
    T = T_MLA

    def body(cq_ref, ckv_ref, kr_ref, gcq_ref, gckv_ref, wuq_ref, wuk_ref, wuv_ref, gq_ref, gk_ref, C_ref, S1_ref, S2_ref,
             q_ref, k_ref, v_ref):
        cq = cq_ref[...]
        cqn = (cq * lax.rsqrt(jnp.mean(cq * cq, axis=-1, keepdims=True) + EPS) * gcq_ref[...]).astype(BF16)
        ckv = ckv_ref[...]
        ckvn = (ckv * lax.rsqrt(jnp.mean(ckv * ckv, axis=-1, keepdims=True) + EPS) * gckv_ref[...]).astype(BF16)
        q0 = _nn(cqn, wuq_ref[...])
        k0 = _nn(ckvn, wuk_ref[...])
        krb = kr_ref[...]
        C, S1, S2 = C_ref[...], S1_ref[...], S2_ref[...]
        for h in range(8):
            sl = slice(h * LANE, (h + 1) * LANE)
            xq = q0[:, sl]
            xq = xq * lax.rsqrt(_rsum_mxu(xq * xq) * (1.0 / MLA_QK) + EPS) * gq_ref[...]
            q_ref[:, sl] = _rope(xq, C, S1, S2, 16).astype(BF16)
            xk = k0[:, sl] + krb
            xk = xk * lax.rsqrt(_rsum_mxu(xk * xk) * (1.0 / MLA_QK) + EPS) * gk_ref[...]
            k_ref[:, sl] = _rope(xk, C, S1, S2, 16).astype(BF16)
        v_ref[...] = _nn(ckvn, wuv_ref[...]).astype(BF16)

    tabspec = pl.BlockSpec((T, LANE), lambda i: (i, 0))
    in_specs = [_zcol(T, 256, C_CQ), _zcol(T, LANE, C_CKV), _zcol(T, LANE, C_KR), _full((1, 256)), _full((1, LANE)),
                _full((256, 1024)), _full((LANE, 1024)), _full((LANE, 512)), _full((1, LANE)), _full((1, LANE)),
                tabspec, tabspec, tabspec]
    return pl.pallas_call(
        body, name="mla_pre_fwd", grid=(S // T,), in_specs=in_specs,
        out_specs=[pl.BlockSpec((T, 1024), lambda i: (i, 0)), pl.BlockSpec((T, 1024), lambda i: (i, 0)), pl.BlockSpec((T, 512), lambda i: (i, 0))],
        out_shape=[jax.ShapeDtypeStruct((S, 1024), BF16), jax.ShapeDtypeStruct((S, 1024), BF16), jax.ShapeDtypeStruct((S, 512), BF16)],
        compiler_params=_cp(),
    )(zp, zp, zp, w["g_cq"], w["g_ckv"], w["w_uq"], w["w_uk"], w["w_uv"], w["g_mq"], w["g_mk"], *tab)


def _mla_attn_fwd(q, k, v, zp):
    S = q.shape[0]
    nq = S // TA

    def body(q_ref, k_ref, v_ref, g_ref, o_ref, lse_ref, y_ref):
        qi = pl.program_id(1)
        lane = _lane((TA, LANE))
        rowi = lax.broadcasted_iota(jnp.int32, (TA, TA), 0)
        coli = lax.broadcasted_iota(jnp.int32, (TA, TA), 1)
        o_tot = jnp.zeros((TA, LANE), F32)
        for hh in range(2):
            cs = slice(hh * LANE, (hh + 1) * LANE)
            hm = (lane < 64) if hh == 0 else (lane >= 64)
            qh = q_ref[:, cs]
            ones_lane = 64 if hh == 0 else 0

            def step(kb, carry, masked, cs=cs, hm=hm, qh=qh, ones_lane=ones_lane):
                m, acc = carry
                off = pl.multiple_of(kb * TA, TA)
                kh = k_ref[pl.ds(off, TA), cs]
                vv = v_ref[pl.ds(off, TA), :]
                vh = jnp.where(hm, vv, jnp.where(lane == ones_lane, jnp.ones_like(vv), jnp.zeros_like(vv)))
                s = _nt(qh, kh) * (MLA_SCALE * LOG2E)
                if masked:
                    s = jnp.where(rowi >= coli, s, NEG)
                m_new = jnp.maximum(m, jnp.max(s, axis=-1, keepdims=True))
                acc = jnp.exp2(m - m_new) * acc + _nn(jnp.exp2(s - m_new).astype(BF16), vh)
                return m_new, acc

            init = (jnp.full((TA, 1), NEG, F32), jnp.zeros((TA, LANE), F32))
            carry = lax.fori_loop(0, qi, lambda kb, c: step(kb, c, False), init)
            m, acc = step(qi, carry, True)
            l = _rsum(jnp.where(lane == ones_lane, acc, 0.0))
            o_tot = o_tot + jnp.where(hm, acc, 0.0) / l
            lse_ref[:, cs] = jnp.broadcast_to(m * (1.0 / LOG2E) + jnp.log(l), (TA, LANE))
        o_ref[...] = o_tot
        y_ref[...] = (o_tot * _silu(g_ref[...])).astype(BF16)

    blk = pl.BlockSpec((TA, LANE), lambda p, i: (i, p))
    return pl.pallas_call(
        body, name="mla_attn_fwd", grid=(4, nq),
        in_specs=[pl.BlockSpec((TA, 256), lambda p, i: (i, p)), pl.BlockSpec((S, 256), lambda p, i: (0, p)),
                  pl.BlockSpec((S, LANE), lambda p, i: (0, p)), pl.BlockSpec((TA, LANE), lambda p, i: (i, C_MLAG + p))],
        out_specs=[blk, pl.BlockSpec((TA, 256), lambda p, i: (i, p)), blk],
        out_shape=[jax.ShapeDtypeStruct((S, 512), F32), jax.ShapeDtypeStruct((S, 1024), F32), jax.ShapeDtypeStruct((S, 512), BF16)],
        compiler_params=_cp(dimension_semantics=("parallel", "arbitrary")),
    )(q, k, v, zp)


def _mla_post_bwd(zp, o, dy, dz):
    S = zp.shape[0]
    T = T_ROW

    def body(g_ref, o_ref, dy_ref, dz_in, dz_ref, do_ref, D_ref):
        del dz_in
        g, o_, dy_ = g_ref[...], o_ref[...], dy_ref[...]
        do = dy_ * _silu(g)
        do_ref[...] = do.astype(BF16)
        dz_ref[...] = (dy_ * o_ * _dsilu(g)).astype(BF16)
        prod = do * o_
        lane = _lane((T, LANE))
        for p in range(4):
            pr = prod[:, p * LANE:(p + 1) * LANE]
            da = _rsum(jnp.where(lane < 64, pr, 0.0))
            db = _rsum(jnp.where(lane >= 64, pr, 0.0))
            D_ref[:, 2 * p * LANE:(2 * p + 1) * LANE] = jnp.broadcast_to(da, (T, LANE))
            D_ref[:, (2 * p + 1) * LANE:(2 * p + 2) * LANE] = jnp.broadcast_to(db, (T, LANE))

    row = pl.BlockSpec((T, 512), lambda i: (i, 0))
    zc = _zcol(T, 512, C_MLAG)
    return pl.pallas_call(
        body, name="mla_post_bwd", grid=(S // T,), in_specs=[zc, row, row, pl.BlockSpec(memory_space=pl.ANY)],
        out_specs=[zc, row, pl.BlockSpec((T, 1024), lambda i: (i, 0))],
        out_shape=[jax.ShapeDtypeStruct(dz.shape, BF16), jax.ShapeDtypeStruct((S, 512), BF16), jax.ShapeDtypeStruct((S, 1024), F32)],
        input_output_aliases={3: 0}, compiler_params=_cp(),
    )(zp, o, dy, dz)


def _mla_attn_bwd(q, k, v, do, lse, Dr):
    S = q.shape[0]
    nq = S // TA

    def body(q_ref, do_ref, lse_ref, D_ref, k_ref, v_ref, dq_ref, dk_ref, dv_ref):
        ki = pl.program_id(1)

        @pl.when(ki == 0)
        def _():
            dq_ref[...] = jnp.zeros_like(dq_ref)

        lane = _lane((TA, LANE))
        rowi = lax.broadcasted_iota(jnp.int32, (TA, TA), 0)
        coli = lax.broadcasted_iota(jnp.int32, (TA, TA), 1)
        dv_tot = jnp.zeros((TA, LANE), F32)
        for hh in range(2):
            cs = slice(hh * LANE, (hh + 1) * LANE)
            hm = (lane < 64) if hh == 0 else (lane >= 64)
            kh = k_ref[:, cs]
            vv = v_ref[...]
            vm = jnp.where(hm, vv, jnp.zeros_like(vv))

            def step(qb, carry, masked, cs=cs, kh=kh, vm=vm):
                dk_acc, dv_acc = carry
                off = pl.multiple_of(qb * TA, TA)
                qh = q_ref[pl.ds(off, TA), cs]
                doh = do_ref[pl.ds(off, TA), :]
                ls = jnp.tile(lse_ref[pl.ds(off, TA), cs], (1, TA // LANE))
                dd = jnp.tile(D_ref[pl.ds(off, TA), cs], (1, TA // LANE))
                s = _nt(qh, kh) * MLA_SCALE
                if masked:
                    s = jnp.where(rowi >= coli, s, NEG)
                p = jnp.exp(s - ls)
                dp = _nt(doh, vm)
                ds = (p * (dp - dd) * MLA_SCALE).astype(BF16)
                dv_acc = dv_acc + _tn(p.astype(BF16), doh)
                dk_acc = dk_acc + _tn(ds, qh)
                dq_ref[pl.ds(off, TA), cs] += _nn(ds, kh)
                return dk_acc, dv_acc

            z = jnp.zeros((TA, LANE), F32)
            carry = step(ki, (z, z), True)
            dk_acc, dv_acc = lax.fori_loop(ki + 1, nq, lambda qb, c: step(qb, c, False), carry)
            dk_ref[:, cs] = dk_acc
            dv_tot = dv_tot + jnp.where(hm, dv_acc, 0.0)
        dv_ref[...] = dv_tot

    pair = pl.BlockSpec((S, 256), lambda p, i: (0, p))
    return pl.pallas_call(
        body, name="mla_attn_bwd", grid=(4, nq),
        in_specs=[pair, pl.BlockSpec((S, LANE), lambda p, i: (0, p)), pair, pair,
                  pl.BlockSpec((TA, 256), lambda p, i: (i, p)), pl.BlockSpec((TA, LANE), lambda p, i: (i, p))],
        out_specs=[pair, pl.BlockSpec((TA, 256), lambda p, i: (i, p)), pl.BlockSpec((TA, LANE), lambda p, i: (i, p))],
        out_shape=[jax.ShapeDtypeStruct((S, 1024), F32), jax.ShapeDtypeStruct((S, 1024), F32), jax.ShapeDtypeStruct((S, 512), F32)],
        compiler_params=_cp(dimension_semantics=("parallel", "arbitrary")),
    )(q, do, lse, Dr, k, v)


def _mla_pre_bwd(zp, dq, dk, dv, w, tab, dz):
    S = zp.shape[0]
    T = T_MLA

    def body(cq_ref, ckv_ref, kr_ref, dq_ref, dk_ref, dv_ref, gcq_ref, gckv_ref, wuq_ref, wuk_ref, wuv_ref, gq_ref, gk_ref,
             C_ref, S1_ref, S2_ref, dz_in, dz_ref, dwuq_ref, dwuk_ref, dwuv_ref, dgcq_ref, dgckv_ref, dgq_ref, dgk_ref):
        del dz_in
        i = pl.program_id(0)

        @pl.when(i == 0)
        def _():
            for r in (dwuq_ref, dwuk_ref, dwuv_ref, dgcq_ref, dgckv_ref, dgq_ref, dgk_ref):
                r[...] = jnp.zeros_like(r)

        cq = cq_ref[...]
        rq = lax.rsqrt(jnp.mean(cq * cq, axis=-1, keepdims=True) + EPS)
        cqh = cq * rq
        cqn = (cqh * gcq_ref[...]).astype(BF16)
        ckv = ckv_ref[...]
        rkv = lax.rsqrt(jnp.mean(ckv * ckv, axis=-1, keepdims=True) + EPS)
        ckvh = ckv * rkv
        ckvn = (ckvh * gckv_ref[...]).astype(BF16)
        q0 = _nn(cqn, wuq_ref[...])
        k0 = _nn(ckvn, wuk_ref[...])
        krb = kr_ref[...]
        C, S1, S2 = C_ref[...], S1_ref[...], S2_ref[...]
        gq, gk = gq_ref[...], gk_ref[...]

        def head_bwd(x, dy, g):
            r = lax.rsqrt(_rsum_mxu(x * x) * (1.0 / MLA_QK) + EPS)
            xn = x * r
            dyn = _rope_t(dy, C, S1, S2, 16)
            dxh = dyn * g
            return r * (dxh - xn * _rsum_mxu(dxh * xn) * (1.0 / MLA_QK)), _csum(dyn * xn)

        dq0, dk0 = [], []
        dgq_acc = jnp.zeros((1, LANE), F32)
        dgk_acc = jnp.zeros((1, LANE), F32)
        dkr = jnp.zeros((T, LANE), F32)
        for h in range(8):
            sl = slice(h * LANE, (h + 1) * LANE)
            dxq, gq_p = head_bwd(q0[:, sl], dq_ref[:, sl], gq)
            dxk, gk_p = head_bwd(k0[:, sl] + krb, dk_ref[:, sl], gk)
            dq0.append(dxq.astype(BF16))
            dk0.append(dxk.astype(BF16))
            dkr = dkr + dxk
            dgq_acc = dgq_acc + gq_p
            dgk_acc = dgk_acc + gk_p
        dgq_ref[...] += dgq_acc
        dgk_ref[...] += dgk_acc
        dq0 = jnp.concatenate(dq0, axis=1)
        dk0 = jnp.concatenate(dk0, axis=1)
        dvb = dv_ref[...].astype(BF16)
        dwuq_ref[...] += _tn(cqn, dq0)
        dwuk_ref[...] += _tn(ckvn, dk0)
        dwuv_ref[...] += _tn(ckvn, dvb)
        dcqn = _nt(dq0, wuq_ref[...])
        dckvn = _nt(dk0, wuk_ref[...]) + _nt(dvb, wuv_ref[...])
        dgcq_ref[...] += _csum(dcqn * cqh)
        dgckv_ref[...] += _csum(dckvn * ckvh)
        dxh = dcqn * gcq_ref[...]
        dz_ref[:, 0:256] = (rq * (dxh - cqh * jnp.mean(dxh * cqh, axis=-1, keepdims=True))).astype(BF16)
        dxh = dckvn * gckv_ref[...]
        dz_ref[:, 256:384] = (rkv * (dxh - ckvh * jnp.mean(dxh * ckvh, axis=-1, keepdims=True))).astype(BF16)
        lane = _lane((T, LANE))
        dz_ref[:, 384:512] = jnp.where((lane >= KR_LANE) & (lane < KR_LANE + 32), dkr, 0.0).astype(BF16)

    tabspec = pl.BlockSpec((T, LANE), lambda i: (i, 0))
    in_specs = [_zcol(T, 256, C_CQ), _zcol(T, LANE, C_CKV), _zcol(T, LANE, C_KR),
                pl.BlockSpec((T, 1024), lambda i: (i, 0)), pl.BlockSpec((T, 1024), lambda i: (i, 0)), pl.BlockSpec((T, 512), lambda i: (i, 0)),
                _full((1, 256)), _full((1, LANE)), _full((256, 1024)), _full((LANE, 1024)), _full((LANE, 512)), _full((1, LANE)), _full((1, LANE)),
                tabspec, tabspec, tabspec, pl.BlockSpec(memory_space=pl.ANY)]
    out_specs = [_zcol(T, 512, C_CQ), _full((256, 1024)), _full((LANE, 1024)), _full((LANE, 512)), _full((1, 256)), _full((1, LANE)),
                 _full((1, LANE)), _full((1, LANE))]
    out_shape = [jax.ShapeDtypeStruct(dz.shape, BF16), jax.ShapeDtypeStruct((256, 1024), F32), jax.ShapeDtypeStruct((LANE, 1024), F32),
                 jax.ShapeDtypeStruct((LANE, 512), F32), jax.ShapeDtypeStruct((1, 256), F32), jax.ShapeDtypeStruct((1, LANE), F32),
                 jax.ShapeDtypeStruct((1, LANE), F32), jax.ShapeDtypeStruct((1, LANE), F32)]
    return pl.pallas_call(
        body, name="mla_pre_bwd", grid=(S // T,), in_specs=in_specs, out_specs=out_specs, out_shape=out_shape,
        input_output_aliases={16: 0}, compiler_params=_cp(),
    )(zp, zp, zp, dq, dk, dv, w["g_cq"], w["g_ckv"], w["w_uq"], w["w_uk"], w["w_uv"], w["g_mq"], w["g_mk"], *tab, dz)


T_DIL = 512


def _head_stats(x, lane):
    sq = x * x
    sa = _rsum(jnp.where(lane < 64, sq, 0.0))
    sb = _rsum(jnp.where(lane >= 64, sq, 0.0))
    return lax.rsqrt(jnp.where(lane < 64, sa, sb) * (1.0 / DIL_HD) + EPS)


def _head_sum(x, lane):
    sa = _rsum(jnp.where(lane < 64, x, 0.0))
    sb = _rsum(jnp.where(lane >= 64, x, 0.0))
    return jnp.where(lane < 64, sa, sb)


def _head_stats_mxu(x):
    r = lax.broadcasted_iota(jnp.int32, (LANE, LANE), 0)
    c = lax.broadcasted_iota(jnp.int32, (LANE, LANE), 1)
    ones = jnp.where((r < 64) == (c < 64), 1.0, 0.0).astype(F32)
    ss = lax.dot_general(x * x, ones, (((1,), (0,)), ((), ())), precision=lax.Precision.HIGHEST, preferred_element_type=F32)
    return lax.rsqrt(ss * (1.0 / DIL_HD) + EPS)


def _dil_pre_fwd(zp, w, tab):
    S = zp.shape[0]
    T = T_DIL

    def body(q_ref, k_ref, gq_ref, gk_ref, C_ref, S1_ref, S2_ref, qo_ref, ko_ref):
        C, S1, S2 = C_ref[...], S1_ref[...], S2_ref[...]
        for b in range(12):
            sl = slice(b * LANE, (b + 1) * LANE)
            x = q_ref[:, sl]
            qo_ref[:, sl] = _rope(x * _head_stats_mxu(x) * gq_ref[...], C, S1, S2, 32)
            x = k_ref[:, sl]
            ko_ref[:, sl] = _rope(x * _head_stats_mxu(x) * gk_ref[...], C, S1, S2, 32)

    tabspec = pl.BlockSpec((T, LANE), lambda i: (i, 0))
    out = pl.BlockSpec((T, 1536), lambda i: (i, 0))
    return pl.pallas_call(
        body, name="dil_pre_fwd", grid=(S // T,),
        in_specs=[_zcol(T, 1536, C_DQ), _zcol(T, 1536, C_DK), _full((1, LANE)), _full((1, LANE)), tabspec, tabspec, tabspec],
        out_specs=[out, out], out_shape=[jax.ShapeDtypeStruct((S, 1536), F32)] * 2, compiler_params=_cp(),
    )(zp, zp, w["g_dq"], w["g_dk"], *tab)


DIL_ROWS = 2048


def _dil_geometry(g, S):
    d = DIL_DILATIONS[g]
    P = NK * d
    return d, P, DIL_ROWS // P, S // P


def _dil_rows(start, d, blocks=1):
    return pl.ds(pl.multiple_of(start, NK), blocks * NK) if d == 1 else pl.ds(start, blocks * NK, stride=d)


def _dil_specs(g, S, col0):
    _, P, m, nb = _dil_geometry(g, S)
    cur = pl.BlockSpec((DIL_ROWS, LANE), lambda sb, c: (sb, col0 + c))
    prv = pl.BlockSpec((P, LANE), lambda sb, c: (jnp.maximum(sb * m - 1, 0), col0 + c))
    nxt = pl.BlockSpec((P, LANE), lambda sb, c: (jnp.minimum((sb + 1) * m, nb - 1), col0 + c))
    return cur, prv, nxt


def _dil_attn_fwd(q, k, zp, g):
    S = q.shape[0]
    d, P, m, nb = _dil_geometry(g, S)
    R = DIL_ROWS

    def body(q_ref, kc_ref, kp_ref, vc_ref, vp_ref, o_ref, lse_ref, *scr):
        sb = pl.program_id(0)
        if m > 1:
            ks_ref, vs_ref = scr
            ks_ref[0:P, :] = kp_ref[...]
            ks_ref[P:P + R, :] = kc_ref[...]
            vs_ref[0:P, :] = vp_ref[...]
            vs_ref[P:P + R, :] = vc_ref[...]
        lane = _lane((NK, LANE))

        def unit(u, carry):
            j = u // d
            start = j * P + (u - j * d)
            rows = _dil_rows(start, d)
            if m > 1:
                k2, v2 = ks_ref[_dil_rows(start, d, 2), :], vs_ref[_dil_rows(start, d, 2), :]
            else:
                k2 = jnp.concatenate([kp_ref[rows, :], kc_ref[rows, :]], axis=0)
                v2 = jnp.concatenate([vp_ref[rows, :], vc_ref[rows, :]], axis=0)
            k2, v2 = k2.astype(BF16), v2.astype(BF16)
            q_ = q_ref[rows, :].astype(BF16)
            row = lax.broadcasted_iota(jnp.int32, (NK, 2 * NK), 0)
            col = lax.broadcasted_iota(jnp.int32, (NK, 2 * NK), 1)
            band = (col >= row) & (col <= row + NK) & ((col >= NK) | (sb * m + j > 0))
            lane2 = _lane((2 * NK, LANE))
            zb, zv = jnp.zeros_like(q_), jnp.zeros_like(v2)
            o_tot = jnp.zeros((NK, LANE), F32)
            lse_tot = jnp.zeros((NK, LANE), F32)
            for hh in range(2):
                hm = (lane < 64) if hh == 0 else (lane >= 64)
                hm2 = (lane2 < 64) if hh == 0 else (lane2 >= 64)
                s_ = jnp.where(band, _nt(jnp.where(hm, q_, zb), k2) * DIL_SCALE, NEG)
                mx = jnp.max(s_, axis=-1, keepdims=True)
                e = jnp.exp(s_ - mx)
                den = _rsum(e)
                o_tot = o_tot + _nn(e.astype(BF16), jnp.where(hm2, v2, zv)) / den
                lse_tot = jnp.where(hm, mx + jnp.log(den), lse_tot)
            o_ref[rows, :] = o_tot
            lse_ref[rows, :] = lse_tot
            return carry

        lax.fori_loop(0, R // NK, unit, 0, unroll=R // NK)

    qcur, qprv, _ = _dil_specs(g, S, 4 * g)
    vcur, vprv, _ = _dil_specs(g, S, C_DV + 4 * g)
    out = pl.BlockSpec((R, LANE), lambda sb, c: (sb, c))
    return pl.pallas_call(
        body, name=f"dil_attn_fwd{g}", grid=(S // R, 4), in_specs=[qcur, qcur, qprv, vcur, vprv], out_specs=[out, out],
        out_shape=[jax.ShapeDtypeStruct((S, 512), F32)] * 2,
        scratch_shapes=[pltpu.VMEM((P + R, LANE), F32)] * 2 if m > 1 else [], compiler_params=_cp(),
    )(q, k, k, zp, zp)


def _dil_combine(os_, ls_, zp):
    S = zp.shape[0]
    T = T_ROW

    def body(o0, o1, o2, l0, l1, l2, g_ref, oc_ref, L_ref, y_ref):
        a, b, c = l0[...], l1[...], l2[...]
        mx = jnp.maximum(jnp.maximum(a, b), c)
        ea, eb, ec = jnp.exp(a - mx), jnp.exp(b - mx), jnp.exp(c - mx)
        den = ea + eb + ec
        oc = (ea * o0[...] + eb * o1[...] + ec * o2[...]) / den
        oc_ref[...] = oc
        L_ref[...] = mx + jnp.log(den)
        y_ref[...] = (oc * _silu(g_ref[...])).astype(BF16)

    row = pl.BlockSpec((T, 512), lambda i: (i, 0))
    return pl.pallas_call(
        body, name="dil_combine", grid=(S // T,), in_specs=[row] * 6 + [_zcol(T, 512, C_DILG)], out_specs=[row, row, row],
        out_shape=[jax.ShapeDtypeStruct((S, 512), F32), jax.ShapeDtypeStruct((S, 512), F32), jax.ShapeDtypeStruct((S, 512), BF16)],
        compiler_params=_cp(),
    )(*os_, *ls_, zp)


def _dil_comb_bwd(zp, oc, dy, dz):
    S = zp.shape[0]
    T = T_ROW

    def body(g_ref, o_ref, dy_ref, dz_in, dz_ref, do_ref, D_ref):
        del dz_in
        g, o_, dy_ = g_ref[...], o_ref[...], dy_ref[...]
        do = dy_ * _silu(g)
        do_ref[...] = do
        dz_ref[...] = (dy_ * o_ * _dsilu(g)).astype(BF16)
        lane = _lane((T, LANE))
        for p in range(4):
            sl = slice(p * LANE, (p + 1) * LANE)
            D_ref[:, sl] = _head_sum(do[:, sl] * o_[:, sl], lane)

    row = pl.BlockSpec((T, 512), lambda i: (i, 0))
    zc = _zcol(T, 512, C_DILG)
    return pl.pallas_call(
        body, name="dil_comb_bwd", grid=(S // T,), in_specs=[zc, row, row, pl.BlockSpec(memory_space=pl.ANY)], out_specs=[zc, row, row],
        out_shape=[jax.ShapeDtypeStruct(dz.shape, BF16), jax.ShapeDtypeStruct((S, 512), F32), jax.ShapeDtypeStruct((S, 512), F32)],
        input_output_aliases={3: 0}, compiler_params=_cp(),
    )(zp, oc, dy, dz)


def _dil_attn_bwd(q, k, zp, do, L, Dr, g):
    S = q.shape[0]
    d, P, m, nb = _dil_geometry(g, S)
    R = DIL_ROWS
    n_q, n_k = 4, 2

    def body(*refs):
        q_side = refs[0:2 * n_q]
        k_side = refs[2 * n_q:2 * n_q + 2 * n_k]
        dq_ref, dk_ref, dv_ref = refs[2 * n_q + 2 * n_k:2 * n_q + 2 * n_k + 3]
        scr = refs[2 * n_q + 2 * n_k + 3:]
        sb = pl.program_id(0)
        if m > 1:
            for a in range(n_q):
                scr[a][0:R, :] = q_side[2 * a][...]
                scr[a][R:R + P, :] = q_side[2 * a + 1][...]
            for a in range(n_k):
                scr[n_q + a][0:P, :] = k_side[2 * a + 1][...]
                scr[n_q + a][P:P + R, :] = k_side[2 * a][...]
        lane = _lane((NK, LANE))

        def unit(u, carry):
            j = u // d
            start = j * P + (u - j * d)
            rows = _dil_rows(start, d)
            if m > 1:
                rows_b = _dil_rows(start + P, d)
                q2, do2, L2, D2 = [scr[a][_dil_rows(start, d, 2), :] for a in range(n_q)]
                kp, vp = [scr[n_q + a][rows, :] for a in range(n_k)]
                kc, vc = [scr[n_q + a][rows_b, :] for a in range(n_k)]
            else:
                q2, do2, L2, D2 = [jnp.concatenate([q_side[2 * a][rows, :], q_side[2 * a + 1][rows, :]], axis=0) for a in range(n_q)]
                kc, vc = [k_side[2 * a][rows, :] for a in range(n_k)]
                kp, vp = [k_side[2 * a + 1][rows, :] for a in range(n_k)]
            q2, do2 = q2.astype(BF16), do2.astype(BF16)
            kc, kp, vc, vp = kc.astype(BF16), kp.astype(BF16), vc.astype(BF16), vp.astype(BF16)
            n = sb * m + j
            hA = _lane((2 * NK, LANE)) < 64
            zq = jnp.zeros_like(q2)
            L2r, D2r = pltpu.roll(L2, 64, 1), pltpu.roll(D2, 64, 1)
            Q4 = jnp.concatenate([jnp.where(hA, q2, zq), jnp.where(hA, zq, q2)], axis=0)
            O4 = jnp.concatenate([jnp.where(hA, do2, zq), jnp.where(hA, zq, do2)], axis=0)
            L4 = jnp.concatenate([jnp.where(hA, L2, L2r), jnp.where(hA, L2r, L2)], axis=0)
            D4 = jnp.concatenate([jnp.where(hA, D2, D2r), jnp.where(hA, D2r, D2)], axis=0)
            row4 = lax.broadcasted_iota(jnp.int32, (4 * NK, NK), 0) & (2 * NK - 1)
            col4 = lax.broadcasted_iota(jnp.int32, (4 * NK, NK), 1)
            m4 = ((row4 < NK) & (col4 <= row4)) | ((row4 >= NK) & (col4 >= row4 - NK) & (n < nb - 1))
            p4 = jnp.exp(jnp.where(m4, _nt(Q4, kc) * DIL_SCALE, NEG) - L4)
            ds4 = (p4 * (_nt(O4, vc) - D4) * DIL_SCALE).astype(BF16)
            dk_tot = _tn(ds4, Q4)
            dv_tot = _tn(p4.astype(BF16), O4)
            pick = lambda x: jnp.concatenate([x[0:NK], x[2 * NK:3 * NK]], axis=0)
            Qn, On, Ln, Dn = pick(Q4), pick(O4), pick(L4), pick(D4)
            rowp = lax.broadcasted_iota(jnp.int32, (2 * NK, NK), 0) & (NK - 1)
            colp = lax.broadcasted_iota(jnp.int32, (2 * NK, NK), 1)
            pp = jnp.exp(jnp.where((colp >= rowp) & (n > 0), _nt(Qn, kp) * DIL_SCALE, NEG) - Ln)
            dsp = (pp * (_nt(On, vp) - Dn) * DIL_SCALE).astype(BF16)
            dq2 = _nn(pick(ds4), kc) + _nn(dsp, kp)
            dq_tot = jnp.where(lane < 64, dq2[0:NK], dq2[NK:2 * NK])
            dq_ref[rows, :] = dq_tot
            dk_ref[rows, :] = dk_tot
            dv_ref[rows, :] = dv_tot
            return carry

        lax.fori_loop(0, R // NK, unit, 0, unroll=R // NK)

    qcur, qprv, qnxt = _dil_specs(g, S, 4 * g)
    vcur, vprv, _ = _dil_specs(g, S, C_DV + 4 * g)
    ocur, _, onxt = _dil_specs(g, S, 0)
    out = pl.BlockSpec((R, LANE), lambda sb, c: (sb, c))
    scratch = [pltpu.VMEM((P + R, LANE), F32)] * (n_q + n_k) if m > 1 else []
    return pl.pallas_call(
        body, name=f"dil_attn_bwd{g}", grid=(S // R, 4),
        in_specs=[qcur, qnxt, ocur, onxt, ocur, onxt, ocur, onxt, qcur, qprv, vcur, vprv],
        out_specs=[out, out, out], out_shape=[jax.ShapeDtypeStruct((S, 512), F32)] * 3, scratch_shapes=scratch, compiler_params=_cp(),
    )(q, q, do, do, L, L, Dr, Dr, k, k, zp, zp)


def _dil_pre_bwd(zp, dys, g, tab, dz, col, name):
    S = zp.shape[0]
    T = T_DIL

    def body(x_ref, dy0_ref, dy1_ref, dy2_ref, g_ref, C_ref, S1_ref, S2_ref, dz_in, dz_ref, dg_ref):
        del dz_in
        i = pl.program_id(0)
        C, S1, S2 = C_ref[...], S1_ref[...], S2_ref[...]
        lane = _lane((T, LANE))
        gv = g_ref[...]
        acc = jnp.zeros((1, LANE), F32)
        for b in range(12):
            sl = slice(b * LANE, (b + 1) * LANE)
            x = x_ref[:, sl]
            r = _head_stats(x, lane)
            xn = x * r
            dy_ref = (dy0_ref, dy1_ref, dy2_ref)[b // 4]
            dyn = _rope_t(dy_ref[:, (b % 4) * LANE:(b % 4 + 1) * LANE], C, S1, S2, 32)
            acc = acc + _csum(dyn * xn)
            dxh = dyn * gv
            dz_ref[:, sl] = (r * (dxh - xn * _head_sum(dxh * xn, lane) * (1.0 / DIL_HD))).astype(BF16)

        @pl.when(i == 0)
        def _():
            dg_ref[...] = acc

        @pl.when(i > 0)
        def _():
            dg_ref[...] += acc

    tabspec = pl.BlockSpec((T, LANE), lambda i: (i, 0))
    zc = _zcol(T, 1536, col)
    grp = pl.BlockSpec((T, 512), lambda i: (i, 0))
    return pl.pallas_call(
        body, name=name, grid=(S // T,),
        in_specs=[zc, grp, grp, grp, _full((1, LANE)), tabspec, tabspec, tabspec, pl.BlockSpec(memory_space=pl.ANY)],
        out_specs=[zc, _full((1, LANE))], out_shape=[jax.ShapeDtypeStruct(dz.shape, BF16), jax.ShapeDtypeStruct((1, LANE), F32)],
        input_output_aliases={8: 0}, compiler_params=_cp(),
    )(zp, *dys, g, *tab, dz)


def _dil_dv_into(dvs, dz):
    S = dz.shape[0]
    T = T_ROW

    def body(s0, s1, s2, dz_in, o_ref):
        del dz_in
        for gi, s in enumerate((s0, s1, s2)):
            o_ref[:, gi * 512:(gi + 1) * 512] = s[...].astype(BF16)

    grp = pl.BlockSpec((T, 512), lambda i: (i, 0))
    return pl.pallas_call(
        body, name="dil_dv", grid=(S // T,), in_specs=[grp, grp, grp, pl.BlockSpec(memory_space=pl.ANY)],
        out_specs=_zcol(T, 1536, C_DV), out_shape=jax.ShapeDtypeStruct(dz.shape, BF16), input_output_aliases={3: 0}, compiler_params=_cp(),
    )(*dvs, dz)


T_MRG = 512


def _merge_fwd(P, zp, b_merge):
    S = zp.shape[0]
    T = T_MRG

    def body(p0, p1, p2, m0, m1, m2, b_ref, o_ref):
        acc = jnp.zeros((T, D), F32)
        for j, (p, m) in enumerate(((p0, m0), (p1, m1), (p2, m2))):
            acc = acc + _sig(m[...] + b_ref[:, j * D:(j + 1) * D]) * p[...].astype(F32)
        o_ref[...] = acc.astype(BF16)

    row = pl.BlockSpec((T, D), lambda i: (i, 0))
    return pl.pallas_call(
        body, name="merge_fwd", grid=(S // T,),
        in_specs=[row, row, row] + [_zcol(T, D, C_MERGE + 8 * j) for j in range(3)] + [_full((1, 3 * D))], out_specs=row,
        out_shape=jax.ShapeDtypeStruct((S, D), BF16), compiler_params=_cp(),
    )(*P, zp, zp, zp, b_merge)


def _merge_bwd(dm, Pj, zp, bj, dz, j):
    S = zp.shape[0]
    T = T_MRG

    def body(dm_ref, p_ref, m_ref, b_ref, dz_in, dz_ref, dp_ref, db_ref):
        del dz_in
        i = pl.program_id(0)
        g = _sig(m_ref[...] + b_ref[...])
        dmv = dm_ref[...].astype(F32)
        dp_ref[...] = (dmv * g).astype(BF16)
        dg = dmv * p_ref[...].astype(F32) * g * (1.0 - g)
        dz_ref[...] = dg.astype(BF16)
        part = _csum(dg)

        @pl.when(i == 0)
        def _():
            db_ref[...] = part

        @pl.when(i > 0)
        def _():
            db_ref[...] += part

    row = pl.BlockSpec((T, D), lambda i: (i, 0))
    zc = _zcol(T, D, C_MERGE + 8 * j)
    return pl.pallas_call(
        body, name=f"merge_bwd{j}", grid=(S // T,), in_specs=[row, row, zc, _full((1, D)), pl.BlockSpec(memory_space=pl.ANY)],
        out_specs=[zc, row, _full((1, D))],
        out_shape=[jax.ShapeDtypeStruct(dz.shape, BF16), jax.ShapeDtypeStruct((S, D), BF16), jax.ShapeDtypeStruct((1, D), F32)],
        input_output_aliases={4: 0}, compiler_params=_cp(),
    )(dm, Pj, zp, bj, dz)


def _loss_fwd_bwd(y, target):
    S = y.shape[0]
    T = T_ROW

    def body(y_ref, t_ref, loss_ref, dy_ref):
        i = pl.program_id(0)
        err = y_ref[...] - t_ref[...]
        dy_ref[...] = err * (1.0 / D)
        part = jnp.sum(err * err, keepdims=True).reshape(1, 1) * (0.5 / D)

        @pl.when(i == 0)
        def _():
            loss_ref[...] = part

        @pl.when(i > 0)
        def _():
            loss_ref[...] += part

    row = pl.BlockSpec((T, D), lambda i: (i, 0))
    return pl.pallas_call(
        body, name="loss", grid=(S // T,), in_specs=[row, row], out_specs=[_full((1, 1)), row],
        out_shape=[jax.ShapeDtypeStruct((1, 1), F32), jax.ShapeDtypeStruct((S, D), F32)], compiler_params=_cp(),
    )(y, target)


def _layer_fwd(x, w, tabs, late=None):
    mla_tab, dil_tab = tabs
    S = x.shape[0]
    h = _rms_in_fwd(x, w["norm_g"])
    zp = _mm(h, w["w_in"], mode="nn", name="in_proj")
    if late is not None:
        w = late(zp)
    hs, y_lru = _lru_fwd(zp, w)
    q, k, v = _mla_pre_fwd(zp, w, mla_tab)
    o_mla, lse, y_mla = _mla_attn_fwd(q, k, v, zp)
    qd, kd = _dil_pre_fwd(zp, w, dil_tab)
    og, lg = zip(*[_dil_attn_fwd(qd, kd, zp, g) for g in range(len(DIL_DILATIONS))])
    oc, L, y_dil = _dil_combine(og, lg, zp)
    P = [_mm(y_lru, w["w_lru_o"], mode="nn", name="lru_out", out_dtype=BF16),
         _mm(y_mla, w["w_mla_o"], mode="nn", name="mla_out", out_dtype=BF16),
         _mm(y_dil, w["w_dil_o"], mode="nn", name="dil_out", out_dtype=BF16)]
    merged = _merge_fwd(P, zp, w["b_merge"])
    x_out = _mm(merged, w["w_out"], mode="nn", name="out_proj", add=x)
    saved = dict(x=x, h=h, zp=zp, hs=hs, y=(y_lru, y_mla, y_dil), q=q, k=k, v=v, o_mla=o_mla, lse=lse, qd=qd, kd=kd, oc=oc, L=L, P=P,
                 merged=merged)
    return x_out, saved


def _layer_bwd(dout, w, tabs, sv, hook=None, after=None):
    mla_tab, dil_tab = tabs
    zp = sv["zp"]
    S = zp.shape[0]
    g = {}
    dm = _mm(dout, w["w_out"], mode="nt", name="d_merged", after=after, out_dtype=BF16)
    g["w_out"] = _mm(sv["merged"], dout, mode="tn", name="dw_out", out_dtype=BF16)
    dz = lax.empty((S, ZW), BF16)
    dP, db = [], []
    for j in range(3):
        dz, dpj, dbj = _merge_bwd(dm, sv["P"][j], zp, w["b_merge"][:, j * D:(j + 1) * D], dz, j)
        dP.append(dpj)
        db.append(dbj)
    g["b_merge"] = jnp.concatenate(db, axis=1)
    names = ("w_lru_o", "w_mla_o", "w_dil_o")
    dy = []
    for j in range(3):
        dy.append(_mm(dP[j], w[names[j]], mode="nt", name="dy_" + names[j]))
        g[names[j]] = _mm(sv["y"][j], dP[j], mode="tn", name="d" + names[j], out_dtype=BF16)
    dz = _lru_gate_bwd(zp, sv["hs"], dy[0], dz)
    dz, g["conv_w"], g["conv_b"], g["w_gx"], g["b_gx"], g["w_ga"], g["b_ga"], g["lam"] = _lru_bwd(zp, sv["hs"], dy[0], w, dz)
    dz, do, Dr = _mla_post_bwd(zp, sv["o_mla"], dy[1], dz)
    dq, dk, dv = _mla_attn_bwd(sv["q"], sv["k"], sv["v"], do, sv["lse"], Dr)
    dz, g["w_uq"], g["w_uk"], g["w_uv"], g["g_cq"], g["g_ckv"], g["g_mq"], g["g_mk"] = _mla_pre_bwd(zp, dq, dk, dv, w, mla_tab, dz)
    dz, dod, Dd = _dil_comb_bwd(zp, sv["oc"], dy[2], dz)
    dqs, dks, dvs = zip(*[_dil_attn_bwd(sv["qd"], sv["kd"], zp, dod, sv["L"], Dd, gi) for gi in range(len(DIL_DILATIONS))])
    dz, g["g_dq"] = _dil_pre_bwd(zp, dqs, w["g_dq"], dil_tab, dz, C_DQ, "dil_pre_bwd_q")
    dz, g["g_dk"] = _dil_pre_bwd(zp, dks, w["g_dk"], dil_tab, dz, C_DK, "dil_pre_bwd_k")
    dz = _dil_dv_into(dvs, dz)
    g["w_in"] = _mm(sv["h"], dz, mode="tn", name="dw_in", out_dtype=BF16, tk=S)
    token = hook(g) if hook is not None else None
    dh = _mm(dz, w["w_in"], mode="nt", name="d_h", after=token, tk=ZW // 4)
    dx, g["norm_g"] = _rms_in_bwd(sv["x"], w["norm_g"], dh, dout)
    return dx, g


def _peers():
    mx, my, mc = lax.axis_index("x"), lax.axis_index("y"), lax.axis_index("c")
    me = 4 * mx + 2 * my + mc
    out = []
    for k in range(1, N_DEV):
        px = 1 - mx if k & 4 else mx
        py = 1 - my if k & 2 else my
        pc = 1 - mc if k & 1 else mc
        out.append(((px, py, pc), 4 * px + 2 * py + pc))
    return me, out


def _whole(ref, p):
    del p
    return ref


def _exchange(srcs, slicers, slices, name):
    n = len(srcs)

    def body(*refs):
        ins, outs = refs[:n], refs[n:2 * n]
        send_sems, recv_sems, local_sems = refs[2 * n:]
        me, peers = _peers()
        mine = [pltpu.make_async_copy(slicers[a](ins[a], me), outs[a].at[me], local_sems.at[a]) for a in range(n)]
        for cp in mine:
            cp.start()
        copies = []
        for k, (peer, pidx) in enumerate(peers):
            for a in range(n):
                cp = pltpu.make_async_remote_copy(
                    src_ref=slicers[a](ins[a], pidx), dst_ref=outs[a].at[me], send_sem=send_sems.at[k * n + a],
                    recv_sem=recv_sems.at[k * n + a], device_id=peer, device_id_type=pl.DeviceIdType.MESH)
                cp.start()
                copies.append(cp)
        for cp in copies + mine:
            cp.wait()

    nsem = (N_DEV - 1) * n
    return pl.pallas_call(
        body, name=name, out_shape=[jax.ShapeDtypeStruct((N_DEV,) + shp, dt) for shp, dt in slices],
        in_specs=[pl.BlockSpec(memory_space=pl.ANY)] * n, out_specs=[pl.BlockSpec(memory_space=pl.ANY)] * n,
        scratch_shapes=[pltpu.SemaphoreType.DMA((nsem,)), pltpu.SemaphoreType.DMA((nsem,)), pltpu.SemaphoreType.DMA((n,))],
        compiler_params=pltpu.CompilerParams(has_side_effects=True),
    )(*srcs)


def _gather_two_level(srcs, name):
    n = len(srcs)

    def body(*refs):
        ins, outs = refs[:n], refs[n:2 * n]
        send_sems, recv_sems, local_sems = refs[2 * n:]
        mx, my, mc = lax.axis_index("x"), lax.axis_index("y"), lax.axis_index("c")
        me, sibling = (mx, my, mc), (mx, my, 1 - mc)
        chips = [(1 - mx, my), (mx, 1 - my), (1 - mx, 1 - my)]
        slot = lambda d: 4 * d[0] + 2 * d[1] + d[2]

        def copy(j, a, block, to, own=False):
            return pltpu.make_async_remote_copy(
                src_ref=ins[a] if own else outs[a].at[slot(block)], dst_ref=outs[a].at[slot(block)],
                send_sem=send_sems.at[j * n + a], recv_sem=recv_sems.at[j * n + a], device_id=to, device_id_type=pl.DeviceIdType.MESH)

        mine = [pltpu.make_async_copy(ins[a], outs[a].at[slot(me)], local_sems.at[a]) for a in range(n)]
        first = [copy(1 + j, a, me, (*chip, mc), own=True) for j, chip in enumerate(chips) for a in range(n)]
        first += [copy(0, a, me, sibling, own=True) for a in range(n)]
        for cp in mine + first:
            cp.start()
        passed = []
        for j, chip in enumerate(chips):
            for a in range(n):
                copy(1 + j, a, (*chip, mc), me).wait_recv()
                cp = copy(4 + j, a, (*chip, mc), sibling)
                cp.start()
                passed.append(cp)
        for a in range(n):
            copy(0, a, sibling, me).wait_recv()
        for j, chip in enumerate(chips):
            for a in range(n):
                copy(4 + j, a, (*chip, 1 - mc), me).wait_recv()
        for cp in first + passed:
            cp.wait_send()
        for cp in mine:
            cp.wait()

    nsem = (N_DEV - 1) * n
    return pl.pallas_call(
        body, name=name, out_shape=[jax.ShapeDtypeStruct((N_DEV,) + a.shape, a.dtype) for a in srcs],
        in_specs=[pl.BlockSpec(memory_space=pl.ANY)] * n, out_specs=[pl.BlockSpec(memory_space=pl.ANY)] * n,
        scratch_shapes=[pltpu.SemaphoreType.DMA((nsem,)), pltpu.SemaphoreType.DMA((nsem,)), pltpu.SemaphoreType.DMA((n,))],
        compiler_params=pltpu.CompilerParams(has_side_effects=True),
    )(*srcs)


_HBM = pl.BlockSpec(memory_space=pltpu.HBM)
_SEM = pl.BlockSpec(memory_space=pltpu.SEMAPHORE)
_DATAFLOW = pltpu.SideEffectType.DATAFLOW_SIDE_EFFECTING


def _plan_chips():
    mx, my, mc = lax.axis_index("x"), lax.axis_index("y"), lax.axis_index("c")
    return 2 * mx + my, [((cx, cy, mc), 2 * cx + cy) for cx, cy in ((1 - mx, my), (mx, 1 - my), (1 - mx, 1 - my))]


def _pair_exchange(srcs, slicers, slices, sliced, name):
    n = len(srcs)
    pieces = [4 if s else 1 for s in sliced]

    def body(*refs):
        ins, outs = refs[:n], refs[n:2 * n]
        send_sems, recv_sems = refs[2 * n:]
        mx, my, mc = lax.axis_index("x"), lax.axis_index("y"), lax.axis_index("c")
        copies = []
        for a in range(n):
            for q in range(pieces[a]):
                i = len(copies)
                copies.append(pltpu.make_async_remote_copy(
                    src_ref=slicers[a](ins[a], 2 * q + 1 - mc) if sliced[a] else ins[a], dst_ref=outs[a].at[q],
                    send_sem=send_sems.at[i], recv_sem=recv_sems.at[i], device_id=(mx, my, 1 - mc), device_id_type=pl.DeviceIdType.MESH))
        for cp in copies:
            cp.start()
        for cp in copies:
            cp.wait()

    return pl.pallas_call(
        body, name=name, out_shape=[jax.ShapeDtypeStruct((p,) + shp, dt) for (shp, dt), p in zip(slices, pieces)],
        in_specs=[pl.BlockSpec(memory_space=pl.ANY)] * n, out_specs=[pl.BlockSpec(memory_space=pl.ANY)] * n,
        scratch_shapes=[pltpu.SemaphoreType.DMA((sum(pieces),)), pltpu.SemaphoreType.DMA((sum(pieces),))],
        compiler_params=pltpu.CompilerParams(has_side_effects=True),
    )(*srcs)


def _pair_add(src, came, first_blk, axis, name):
    _, r, c = came.shape
    nblk = (c if axis == 1 else r) // LANE
    if axis == 1:
        s_spec = pl.BlockSpec((r, LANE), lambda q, j, fb: (0, fb[q] + j))
        o_spec = pl.BlockSpec((1, r, LANE), lambda q, j, fb: (q, 0, j))
    else:
        s_spec = pl.BlockSpec((LANE, c), lambda q, j, fb: (fb[q] + j, 0))
        o_spec = pl.BlockSpec((1, LANE, c), lambda q, j, fb: (q, j, 0))

    def body(fb_ref, x_ref, y_ref, o_ref):
        del fb_ref
        o_ref[0] = (x_ref[...].astype(F32) + y_ref[0].astype(F32)).astype(o_ref.dtype)

    return pl.pallas_call(
        body, name=name, out_shape=jax.ShapeDtypeStruct(came.shape, came.dtype),
        grid_spec=pltpu.PrefetchScalarGridSpec(num_scalar_prefetch=1, grid=(4, nblk), in_specs=[s_spec, o_spec], out_specs=o_spec),
        compiler_params=_cp(),
    )(first_blk, src, came)


def _add2(x, y, name):
    shp = x.shape
    x, y = x.reshape(-1, shp[-1]), y.reshape(-1, shp[-1])
    R, C = x.shape
    tr = R
    while tr * C * 4 > (1 << 21) and tr % 32 == 0:
        tr //= 2

    def body(x_ref, y_ref, o_ref):
        o_ref[...] = (x_ref[...].astype(F32) + y_ref[...].astype(F32)).astype(o_ref.dtype)

    spec = pl.BlockSpec((tr, C), lambda i: (i, 0))
    return pl.pallas_call(body, name=name, grid=(R // tr,), in_specs=[spec, spec], out_specs=spec,
                          out_shape=jax.ShapeDtypeStruct((R, C), x.dtype), compiler_params=_cp())(x, y).reshape(shp)


def _exchange_start(srcs, slicers, slices, after, name, plan=_peers, nslots=N_DEV):
    n = len(srcs)
    nsem = (nslots - 1) * n
    lands = [lax.empty((nslots,) + shp, dt) for shp, dt in slices]

    def body(*refs):
        ins, lands_in = refs[:n], refs[n:2 * n]
        send_sems, recv_sems, local_sems = refs[2 * n + 1], refs[2 * n + 2], refs[2 * n + 3]
        token = refs[-1]
        me, peers = plan()
        for a in range(n):
            pltpu.make_async_copy(slicers[a](ins[a], me), lands_in[a].at[me], local_sems.at[a]).start()
        for k, (peer, pidx) in enumerate(peers):
            for a in range(n):
                pltpu.make_async_remote_copy(
                    src_ref=slicers[a](ins[a], pidx), dst_ref=lands_in[a].at[me], send_sem=send_sems.at[k * n + a],
                    recv_sem=recv_sems.at[k * n + a], device_id=peer, device_id_type=pl.DeviceIdType.MESH).start()
        token[...] = jnp.zeros_like(token)

    hbm = lambda a: pltpu.with_memory_space_constraint(a, pltpu.HBM)
    return pl.pallas_call(
        body, name=name,
        out_shape=(pltpu.SemaphoreType.DMA((nsem,)), pltpu.SemaphoreType.DMA((nsem,)), pltpu.SemaphoreType.DMA((n,)),
                   *[pltpu.HBM(a.shape, a.dtype) for a in srcs], *[pltpu.HBM(a.shape, a.dtype) for a in lands],
                   jax.ShapeDtypeStruct((SUB, LANE), F32)),
        in_specs=[_HBM] * (2 * n) + [pl.BlockSpec(memory_space=pl.ANY)],
        out_specs=(_SEM, _SEM, _SEM, *[_HBM] * (2 * n), pl.BlockSpec(memory_space=pltpu.VMEM)),
        input_output_aliases={i: 3 + i for i in range(2 * n)},
        compiler_params=pltpu.CompilerParams(has_side_effects=_DATAFLOW),
    )(*[hbm(a) for a in srcs], *[hbm(a) for a in lands], after)


def _exchange_wait(started, slicers, after, name, plan=_peers):
    n = (len(started) - 4) // 2
    sems, thru = started[0:3], started[3:3 + 2 * n]

    def body(*refs):
        srcs, lands = refs[:n], refs[n:2 * n]
        send_sems, recv_sems, local_sems = refs[2 * n], refs[2 * n + 1], refs[2 * n + 2]
        me, peers = plan()
        for k, (peer, pidx) in enumerate(peers):
            for a in range(n):
                cp = pltpu.make_async_remote_copy(
                    src_ref=slicers[a](srcs[a], pidx), dst_ref=lands[a].at[me], send_sem=send_sems.at[k * n + a],
                    recv_sem=recv_sems.at[k * n + a], device_id=peer, device_id_type=pl.DeviceIdType.MESH)
                cp.wait_send()
                cp.wait_recv()
        for a in range(n):
            pltpu.make_async_copy(slicers[a](srcs[a], me), lands[a].at[me], local_sems.at[a]).wait()

    outs = pl.pallas_call(
        body, name=name, out_shape=[pltpu.HBM(a.shape, a.dtype) for a in thru],
        in_specs=[_HBM] * (2 * n) + [_SEM, _SEM, _SEM, pl.BlockSpec(memory_space=pl.ANY)], out_specs=[_HBM] * (2 * n),
        input_output_aliases={i: i for i in range(2 * n)}, compiler_params=pltpu.CompilerParams(has_side_effects=_DATAFLOW),
    )(*thru, *sems, after)
    return outs[n:]


WIN = 13 * LANE


def _win_base(s):
    n = s * SHARD_IN
    a0 = n + jnp.where(n >= _KR0, KR_LANE, 0) + jnp.where(n >= _KR0 + 32, 32, 0)
    return jnp.minimum(a0 // LANE, (ZW - WIN) // LANE)


def _win_offsets(s):
    n = s * SHARD_IN + jnp.arange(SHARD_IN)
    o = s * SHARD_IN - _win_base(s) * LANE
    return n, (o, o + KR_LANE, o + LANE - 32)


def _to_window(shard, s):
    _, offs = _win_offsets(s)
    padded = jnp.pad(shard, ((0, 0), (0, 0), (WIN, WIN)))
    a, b, c = [lax.dynamic_slice(padded, (0, 0, WIN - o), shard.shape[:2] + (WIN,)) for o in offs]
    col = (_win_base(s) * LANE + jnp.arange(WIN))[None, None, :]
    zero = jnp.zeros_like(a)
    return jnp.where(col < _KR0, a, jnp.where((col >= _KR0 + KR_LANE) & (col < _KR0 + KR_LANE + 32), b, jnp.where(col >= _KR0 + LANE, c, zero)))


def _from_window(win, s):
    n, offs = _win_offsets(s)
    a, b, c = [lax.dynamic_slice(win, (0, 0, o), win.shape[:2] + (SHARD_IN,)) for o in offs]
    return jnp.where((n < _KR0)[None, None, :], a, jnp.where((n < _KR0 + 32)[None, None, :], b, c))


def _win_base_static(s):
    n = s * SHARD_IN
    a0 = n + (KR_LANE if n >= _KR0 else 0) + (32 if n >= _KR0 + 32 else 0)
    return min(a0 // LANE, (ZW - WIN) // LANE)


def _assemble_w_in(gw):
    tr = 128
    bases = [_win_base_static(s) for s in range(N_DEV)]

    def body(g_ref, o_ref):
        for j in range(ZW // LANE):
            acc = None
            for s in range(N_DEV):
                if bases[s] <= j < bases[s] + WIN // LANE:
                    piece = g_ref[s, :, (j - bases[s]) * LANE:(j - bases[s] + 1) * LANE]
                    acc = piece if acc is None else acc + piece
            o_ref[:, j * LANE:(j + 1) * LANE] = acc

    return pl.pallas_call(
        body, name="assemble_w_in", grid=(D // tr,), in_specs=[pl.BlockSpec((N_DEV, tr, WIN), lambda i: (0, i, 0))],
        out_specs=pl.BlockSpec((tr, ZW), lambda i: (i, 0)), out_shape=jax.ShapeDtypeStruct((D, ZW), gw.dtype), compiler_params=_cp(),
    )(gw)


def _cols(width):
    return lambda ref, p: ref.at[:, pl.ds(pl.multiple_of(p * width, width), width)]


def _rows(height):
    return lambda ref, p: ref.at[pl.ds(pl.multiple_of(p * height, height), height), :]


SCATTER = {
    'w_in': (lambda ref, p: ref.at[:, pl.ds(pl.multiple_of(_win_base(p) * LANE, LANE), WIN)], (D, WIN), BF16),
    'conv_w': (_cols(LANE), (4, LANE), F32),
    'w_lru_o': (_rows(LANE), (LANE, D), BF16),
    'w_uq': (_cols(LANE), (256, LANE), F32),
    'w_ukv': (_cols(LANE), (128, LANE), F32),
    'w_mla_o': (_cols(LANE), (512, LANE), BF16),
    'w_dil_o': (_cols(LANE), (512, LANE), BF16),
    'w_out': (_rows(LANE), (LANE, D), BF16),
}
SLICED_AXIS = {'w_in': 1, 'conv_w': 1, 'w_lru_o': 0, 'w_uq': 1, 'w_ukv': 1, 'w_mla_o': 1, 'w_dil_o': 1, 'w_out': 0}


PACK_ROWS = 64


def _packed_rows(shapes):
    n = sum(int(np.prod(s)) for s in shapes)
    return -(-n // (PACK_ROWS * LANE)) * PACK_ROWS


def _sum8(buf, name):
    ns, R, C = buf.shape
    tr = R
    while tr * C * 4 * ns > (1 << 22) and tr % 32 == 0:
        tr //= 2

    def body(b_ref, o_ref):
        acc = b_ref[0].astype(F32)
        for s in range(1, ns):
            acc = acc + b_ref[s].astype(F32)
        o_ref[...] = acc

    return pl.pallas_call(
        body, name=name, grid=(R // tr,), in_specs=[pl.BlockSpec((ns, tr, C), lambda i: (0, i, 0))],
        out_specs=pl.BlockSpec((tr, C), lambda i: (i, 0)), out_shape=jax.ShapeDtypeStruct((R, C), F32), compiler_params=_cp(),
    )(buf)


def _pack(arrs, dtype, lead):
    flat = [a.astype(dtype).reshape(a.shape[:lead] + (-1,)) for a in arrs]
    cat = jnp.concatenate(flat, axis=-1)
    n = cat.shape[-1]
    unit = PACK_ROWS * LANE
    pad = (-n) % unit
    if pad:
        cat = jnp.pad(cat, [(0, 0)] * lead + [(0, pad)])
    return cat.reshape(cat.shape[:lead] + ((n + pad) // LANE, LANE))


def _unpack(buf, shapes, lead):
    flat = buf.reshape(buf.shape[:lead] + (-1,))
    out, off = [], 0
    for shp in shapes:
        n = int(np.prod(shp))
        out.append(flat[..., off:off + n].reshape(buf.shape[:lead] + tuple(shp)))
        off += n
    return out


def _adamw(w, g, m, v, name):
    layers, rows, cols = w.shape
    tr = rows
    while tr * cols * 4 > (3 << 19) and tr % 16 == 0:
        tr //= 2
    c1 = 1.0 - ADAM_B1 ** ADAM_STEP
    c2 = 1.0 - ADAM_B2 ** ADAM_STEP

    def body(w_ref, g_ref, m_ref, v_ref, d_ref, mo_ref, vo_ref):
        gv = g_ref[...]
        mn = ADAM_B1 * m_ref[...] + (1.0 - ADAM_B1) * gv
        vn = ADAM_B2 * v_ref[...] + (1.0 - ADAM_B2) * (gv * gv)
        mo_ref[...] = mn
        vo_ref[...] = vn
        d_ref[...] = -ADAM_LR * ((mn / c1) / (jnp.sqrt(vn / c2) + ADAM_EPS) + ADAM_WD * w_ref[...])

    spec = pl.BlockSpec((1, tr, cols), lambda l, i: (l, i, 0))
    return pl.pallas_call(
        body, name=name, grid=(layers, rows // tr), in_specs=[spec] * 4, out_specs=[spec] * 3,
        out_shape=[jax.ShapeDtypeStruct((layers, rows, cols), F32)] * 3, compiler_params=_cp(),
    )(w, g, m, v)


IN_NAMES = ['x', 'positions', 'norm_g', 'w_in', 'conv_w', 'conv_b', 'w_gate_x', 'b_gate_x', 'w_gate_a', 'b_gate_a', 'lru_lambda', 'w_lru_o',
            'cq_norm_g', 'ckv_norm_g', 'w_uq', 'w_ukv', 'mla_q_norm_g', 'mla_k_norm_g', 'w_mla_o', 'dil_q_norm_g', 'dil_k_norm_g', 'w_dil_o',
            'b_merge', 'w_out']
WEIGHTS = IN_NAMES[2:]
REPLICATED = [n for n in WEIGHTS if n not in SCATTER]
GATE_WEIGHTS = ('w_gate_x', 'w_gate_a')

_KR0 = C_KR * LANE


GATHERED = ['w_in', 'w_lru_o', 'w_uq', 'w_ukv', 'w_mla_o', 'w_dil_o', 'w_out', 'conv_w']


def _local_weights(wd, me):
    loc = {n: wd[n].astype(BF16) for n in GATHERED[:-1]}
    loc['w_in'] = _to_window(loc['w_in'], me)
    loc['w_uq'] = jnp.pad(loc['w_uq'], ((0, 0), (0, 0), (0, LANE - MLA_QK)))
    loc['conv_w'] = wd['conv_w']
    return [[loc[n][l] for n in GATHERED] for l in range(DEPTH)]


def _layer_weights(gathered, rep, l, w_in=None):
    gw = dict(zip(GATHERED, gathered))
    by_rows = lambda a: a.reshape(-1, a.shape[-1])
    by_cols = lambda a: jnp.swapaxes(a, 0, 1).reshape(a.shape[1], -1)
    ukv = jnp.swapaxes(gw['w_ukv'], 0, 1)
    g96 = lambda a: jnp.pad(a[l].reshape(1, MLA_QK), ((0, 0), (0, LANE - MLA_QK)))
    g64 = lambda a: jnp.tile(a[l].reshape(1, DIL_HD), (1, 2))
    return dict(
        norm_g=rep['norm_g'][l].reshape(1, D), w_in=_assemble_w_in(gw['w_in']) if w_in is None else w_in,
        conv_w=by_cols(gw['conv_w']), conv_b=rep['conv_b'][l].reshape(1, D),
        w_gx=rep['w_gate_x'][l].astype(BF16), b_gx=rep['b_gate_x'][l].reshape(8, 1, LANE),
        w_ga=rep['w_gate_a'][l].astype(BF16), b_ga=rep['b_gate_a'][l].reshape(8, 1, LANE),
        lam=rep['lru_lambda'][l].reshape(1, D),
        w_lru_o=by_rows(gw['w_lru_o']), w_mla_o=by_cols(gw['w_mla_o']), w_dil_o=by_cols(gw['w_dil_o']), w_out=by_rows(gw['w_out']),
        g_cq=rep['cq_norm_g'][l].reshape(1, 256), g_ckv=rep['ckv_norm_g'][l].reshape(1, 128),
        w_uq=by_cols(gw['w_uq']), w_uk=jnp.pad(ukv[:, :, :64], ((0, 0), (0, 0), (0, 64))).reshape(128, 1024),
        w_uv=ukv[:, :, 64:].reshape(128, 512),
        g_mq=g96(rep['mla_q_norm_g']), g_mk=g96(rep['mla_k_norm_g']), g_dq=g64(rep['dil_q_norm_g']), g_dk=g64(rep['dil_k_norm_g']),
        b_merge=rep['b_merge'][l].reshape(1, 3 * D),
    )


def _sharded_grads(g):
    uk = g['w_uk'].reshape(128, 8, 128)[:, :, :64]
    uv = g['w_uv'].reshape(128, 8, 64)
    d = {'w_in': g['w_in'], 'conv_w': g['conv_w'], 'w_lru_o': g['w_lru_o'], 'w_uq': g['w_uq'],
         'w_ukv': jnp.concatenate([uk, uv], axis=-1).reshape(128, 1024), 'w_mla_o': g['w_mla_o'], 'w_dil_o': g['w_dil_o'],
         'w_out': g['w_out']}
    return [d[n] for n in SCATTER]


def _replicated_grads(g):
    return {
        'conv_b': g['conv_b'].reshape(D),
        'w_gate_x': g['w_gx'], 'b_gate_x': g['b_gx'].reshape(8, LANE), 'w_gate_a': g['w_ga'], 'b_gate_a': g['b_ga'].reshape(8, LANE),
        'lru_lambda': g['lam'].reshape(D), 'cq_norm_g': g['g_cq'].reshape(256), 'ckv_norm_g': g['g_ckv'].reshape(128),
        'mla_q_norm_g': g['g_mq'][0, :MLA_QK], 'mla_k_norm_g': g['g_mk'][0, :MLA_QK],
        'dil_q_norm_g': g['g_dq'][0, :DIL_HD] + g['g_dq'][0, DIL_HD:], 'dil_k_norm_g': g['g_dk'][0, :DIL_HD] + g['g_dk'][0, DIL_HD:],
        'b_merge': g['b_merge'].reshape(3 * D),
    }


def kernel(x, positions, norm_g, w_in, conv_w, conv_b, w_gate_x, b_gate_x, w_gate_a, b_gate_a, lru_lambda, w_lru_o, cq_norm_g, ckv_norm_g, w_uq, w_ukv, mla_q_norm_g, mla_k_norm_g, w_mla_o, dil_q_norm_g, dil_k_norm_g, w_dil_o, b_merge, w_out, loss_target, m_norm_g, m_w_in, m_conv_w, m_conv_b, m_w_gate_x, m_b_gate_x, m_w_gate_a, m_b_gate_a, m_lru_lambda, m_w_lru_o, m_cq_norm_g, m_ckv_norm_g, m_w_uq, m_w_ukv, m_mla_q_norm_g, m_mla_k_norm_g, m_w_mla_o, m_dil_q_norm_g, m_dil_k_norm_g, m_w_dil_o, m_b_merge, m_w_out, v_norm_g, v_w_in, v_conv_w, v_conv_b, v_w_gate_x, v_b_gate_x, v_w_gate_a, v_b_gate_a, v_lru_lambda, v_w_lru_o, v_cq_norm_g, v_ckv_norm_g, v_w_uq, v_w_ukv, v_mla_q_norm_g, v_mla_k_norm_g, v_w_mla_o, v_dil_q_norm_g, v_dil_k_norm_g, v_w_dil_o, v_b_merge, v_w_out):
    args = (x, positions, norm_g, w_in, conv_w, conv_b, w_gate_x, b_gate_x, w_gate_a, b_gate_a, lru_lambda, w_lru_o, cq_norm_g, ckv_norm_g, w_uq, w_ukv, mla_q_norm_g, mla_k_norm_g, w_mla_o, dil_q_norm_g, dil_k_norm_g, w_dil_o, b_merge, w_out)
    moments_m = (m_norm_g, m_w_in, m_conv_w, m_conv_b, m_w_gate_x, m_b_gate_x, m_w_gate_a, m_b_gate_a, m_lru_lambda, m_w_lru_o, m_cq_norm_g, m_ckv_norm_g, m_w_uq, m_w_ukv, m_mla_q_norm_g, m_mla_k_norm_g, m_w_mla_o, m_dil_q_norm_g, m_dil_k_norm_g, m_w_dil_o, m_b_merge, m_w_out)
    moments_v = (v_norm_g, v_w_in, v_conv_w, v_conv_b, v_w_gate_x, v_b_gate_x, v_w_gate_a, v_b_gate_a, v_lru_lambda, v_w_lru_o, v_cq_norm_g, v_ckv_norm_g, v_w_uq, v_w_ukv, v_mla_q_norm_g, v_mla_k_norm_g, v_w_mla_o, v_dil_q_norm_g, v_dil_k_norm_g, v_w_dil_o, v_b_merge, v_w_out)
    a = dict(zip(IN_NAMES, args))
    wd = {n: a[n] for n in WEIGHTS}
    md = dict(zip(WEIGHTS, moments_m))
    vd = dict(zip(WEIGHTS, moments_v))

    me = 4 * lax.axis_index("x") + 2 * lax.axis_index("y") + lax.axis_index("c")

    assert DEPTH == 2
    xs, tabs = x[0], _rope_tables(positions[0])
    whole = [_whole] * len(GATHERED)
    slicers = [SCATTER[n][0] for n in SCATTER]
    grad_slices = [SCATTER[n][1:3] for n in SCATTER]

    local = _local_weights(wd, me)
    w_slices = [(a.shape, a.dtype) for a in local[0]]
    win0 = _gather_two_level(local[0][:1], "gather_w0")[0]
    flying_r = _exchange_start(local[0][1:], whole[1:], w_slices[1:], win0, "gather_r0_start")
    w_in0 = _assemble_w_in(win0)
    first = {}

    def rest_of_layer0(zp):
        rest = _exchange_wait(flying_r, whole[1:], zp, "gather_r0_wait")
        first['flying'] = _exchange_start(local[1], whole, w_slices, rest[0], "gather_w1_start")
        rep0 = dict(wd, conv_b=wd['conv_b'] + first['flying'][-1][0, 0])
        first['w0'] = _layer_weights([win0] + list(rest), rep0, 0, w_in=w_in0)
        return first['w0']

    start0 = dict(norm_g=(wd['norm_g'][0] + flying_r[-1][0, 0]).reshape(1, D), w_in=w_in0)
    x1, saved0 = _layer_fwd(xs, start0, tabs, late=rest_of_layer0)
    w0, flying = first['w0'], first['flying']
    w1 = _layer_weights(_exchange_wait(flying, whole, x1, "gather_w1_wait"), wd, 1)
    x2, saved1 = _layer_fwd(x1, w1, tabs)
    loss, dx2 = _loss_fwd_bwd(x2, loss_target[0])
    loss = loss[0, 0]

    sharded = list(SCATTER)
    nsh = len(sharded)
    small = [n for n in REPLICATED if n not in GATE_WEIGHTS and n != 'norm_g']

    def outgoing(g):
        r = _replicated_grads(g)
        return (_sharded_grads(g) + [_pack([r[n] for n in small], F32, 0)]
                + [r[n].astype(BF16).reshape(8 * LANE, LANE) for n in GATE_WEIGHTS])

    out_slicers = slicers + [_whole] * 3
    out_slices = grad_slices + [((_packed_rows([wd[n].shape[1:] for n in small]), LANE), F32)] + [((8 * LANE, LANE), BF16)] * 2
    dx1, g1 = _layer_bwd(dx2, w1, tabs, saved1)
    flying1 = _exchange_start(outgoing(g1), out_slicers, out_slices, dx1, "scatter_g1_start")
    later = {}

    names = sharded + ['small'] + list(GATE_WEIGHTS)
    sliced = [True] * nsh + [False] * 3
    by_chip = [(lambda ref, q: ref.at[q])] * nsh + [_whole] * 3

    def send_layer0(g):
        later['got1'] = _exchange_wait(flying1, out_slicers, g['w_in'], "scatter_g1_wait")
        mine = outgoing(g)
        came = _pair_exchange(mine, out_slicers, out_slices, sliced, "pair_g0")
        my_side = 2 * jnp.arange(4, dtype=jnp.int32) + lax.axis_index("c")
        halves = []
        for n, a, c in zip(names, mine, came):
            if n in SCATTER:
                first = _win_base(my_side) if n == 'w_in' else my_side
                halves.append(_pair_add(a, c, first.astype(jnp.int32), SLICED_AXIS[n], f"pair_sum_{n}"))
            else:
                halves.append(_add2(a, c[0], f"pair_sum_{n}"))
        later['flying0'] = _exchange_start(halves, by_chip, out_slices, later['got1'][0], "scatter_g0_start", plan=_plan_chips, nslots=4)
        return later['flying0'][-1]

    grad_x, g0 = _layer_bwd(dx1, w0, tabs, saved0, hook=send_layer0, after=flying1[-1])
    sum1 = [_sum8(b, f"sum_{n}_1") for n, b in zip(names, later['got1'])]
    behind = grad_x[:1, :1] + sum(s_[:1, :1] for s_ in sum1)
    got0 = _exchange_wait(later['flying0'], by_chip, behind, "scatter_g0_wait", plan=_plan_chips)
    sum0 = [_sum8(b, f"sum_{n}_0") for n, b in zip(names, got0)]
    norm_part = _pack([jnp.stack([g['norm_g'].reshape(D) for g in (g0, g1)])], F32, 0)
    norm_sum = _sum8(_exchange([norm_part], [_whole], [(norm_part.shape, F32)], "gather_norm_g")[0], "sum_norm_g")

    gsh = {n: jnp.stack([sum0[i], sum1[i]]) for i, n in enumerate(sharded)}
    gsh['w_in'] = _from_window(gsh['w_in'], me)
    gsh['w_uq'] = gsh['w_uq'][:, :, :MLA_QK]
    grep = {'norm_g': _unpack(norm_sum, [wd['norm_g'].shape], 0)[0]}
    per_layer = [_unpack(s[nsh], [wd[n].shape[1:] for n in small], 0) for s in (sum0, sum1)]
    grep.update({n: jnp.stack([per_layer[l][i] for l in range(DEPTH)]) for i, n in enumerate(small)})
    for i, n in enumerate(GATE_WEIGHTS):
        grep[n] = jnp.stack([sum0[nsh + 1 + i], sum1[nsh + 1 + i]]).reshape(wd[n].shape)

    out_g, out_d, out_m, out_v = {}, {}, {}, {}
    vecs = ['norm_g'] + small
    vshapes = [wd[n].shape for n in vecs]
    packed_g = _pack([grep[n] for n in vecs], F32, 0)
    d_, m_, v_ = _adamw(_pack([wd[n] for n in vecs], F32, 0)[None], packed_g[None], _pack([md[n] for n in vecs], F32, 0)[None],
                        _pack([vd[n] for n in vecs], F32, 0)[None], "adamw_vectors")
    for dst, buf in ((out_d, d_), (out_m, m_), (out_v, v_)):
        dst.update(zip(vecs, _unpack(buf[0], vshapes, 0)))
    out_g.update({n: grep[n] for n in vecs})
    gsh.update({n: grep[n] for n in GATE_WEIGHTS})
    for n in sharded + list(GATE_WEIGHTS):
        shp = wd[n].shape
        three = (1, -1, shp[-1])
        d_, m_, v_ = _adamw(wd[n].reshape(three), gsh[n].reshape(three), md[n].reshape(three), vd[n].reshape(three), "adamw_" + n)
        out_g[n], out_d[n], out_m[n], out_v[n] = gsh[n], d_.reshape(shp), m_.reshape(shp), v_.reshape(shp)

    loss = lax.psum(loss, ("x", "y", "c"))
    return (loss, grad_x[None], *[out_g[n] for n in WEIGHTS], *[out_d[n] for n in WEIGHTS], *[out_m[n] for n in WEIGHTS],
            *[out_v[n] for n in WEIGHTS])
```
